```python
import jax, jax.numpy as jnp
from jax import lax
import numpy as np

D_MODEL = 1024
BATCH = 8
SEQ = 2048
DEPTH = 2

CTX_LEN = 256
GRID_W = 64
N_MIXERS = 2
HGRN_HEADS = 8
HGRN_HEAD_DIM = D_MODEL // HGRN_HEADS
HGRN_WIDTH = HGRN_HEADS * HGRN_HEAD_DIM
CHUNK = 64
POOL_WINDOWS = (2, 4, 8, 16)
POOL_WIDTH = D_MODEL
POOL_GROUPS = len(POOL_WINDOWS)
POOL_GROUP_DIM = POOL_WIDTH // POOL_GROUPS
N_HGRN_LAYERS = (DEPTH + 1) // 2
N_POOL_LAYERS = DEPTH // 2
EPS = 1e-6

kernel_name = 'hybrid_hgrn2_pool_prefix_dit'


def rmsnorm(x, w):
    xf = x.astype(jnp.float32)
    y = xf * lax.rsqrt(jnp.mean(xf * xf, axis=-1, keepdims=True) + EPS)
    return (y * w.astype(jnp.float32)).astype(x.dtype)


def _adaln(cond, w, b):
    mod = jax.nn.silu(cond) @ w + b
    return jnp.split(mod[:, None, :], 3, axis=-1)


def lower_bounds(logits):
    lg = jnp.concatenate([logits, jnp.zeros_like(logits[:1])], axis=0).astype(jnp.float32)
    p = jax.nn.softmax(lg, axis=0)
    return jnp.cumsum(p, axis=0)[: logits.shape[0]]


def _heads(a):
    b, t, _ = a.shape
    return a.reshape(b, t, HGRN_HEADS, HGRN_HEAD_DIM).transpose(0, 2, 1, 3)


def _chunk(a):
    b, h, t, d = a.shape
    return a.reshape(b, h, t // CHUNK, CHUNK, d)


def _inter_chunk_states(k_c, v_c, b_c, s0):
    b_last = b_c[:, :, :, -1]
    k_end = k_c * jnp.exp(b_last[:, :, :, None, :] - b_c)
    u = jnp.einsum('bhnck,bhncv->bhnkv', k_end, v_c)
    decay = jnp.exp(b_last)

    def step(s, inp):
        d_n, u_n = inp
        return d_n[..., None] * s + u_n, s

    s_final, s_prev = lax.scan(step, s0, (jnp.moveaxis(decay, 2, 0), jnp.moveaxis(u, 2, 0)))
    return jnp.moveaxis(s_prev, 0, 2), s_final


def gla_forward(q, k, v, log_f, s0):
    b, h, t, _ = q.shape
    q_c, k_c, v_c = _chunk(q), _chunk(k), _chunk(v)
    b_c = jnp.cumsum(_chunk(log_f), axis=3)
    s_prev, _ = _inter_chunk_states(k_c, v_c, b_c, s0)
    o_inter = jnp.einsum('bhnck,bhnkv->bhncv', q_c * jnp.exp(b_c), s_prev)
    b_mid = b_c[:, :, :, CHUNK // 2 - 1:CHUNK // 2]
    a = jnp.einsum('bhnck,bhnsk->bhncs', q_c * jnp.exp(b_c - b_mid), k_c * jnp.exp(b_mid - b_c))
    a = jnp.where(jnp.tril(jnp.ones((CHUNK, CHUNK), dtype=bool)), a, 0.0)
    o_intra = jnp.einsum('bhncs,bhnsv->bhncv', a, v_c)
    return (o_inter + o_intra).reshape(b, h, t, -1)


def gla_final_state(k, v, log_f, s0):
    k_c, v_c = _chunk(k), _chunk(v)
    b_c = jnp.cumsum(_chunk(log_f), axis=3)
    _, s_final = _inter_chunk_states(k_c, v_c, b_c, s0)
    return s_final


def _rev(a):
    return jnp.flip(a, axis=2)


def _hgrn2_gates(h, w_in, lb):
    f_f, f_b, i = jnp.split((h @ w_in[:, : 3 * HGRN_WIDTH]).astype(jnp.float32), 3, axis=-1)
    out = []
    for d, f_pre in enumerate((f_f, f_b)):
        f = lb[d] + (1.0 - lb[d]) * jax.nn.sigmoid(f_pre)
        out.append((_heads(1.0 - f), _heads(jnp.log(f))))
    return out[0], out[1], _heads(i)


def hgrn2_final_states(h, w_in, lb):
    (k_f, lf_f), (k_b, lf_b), v = _hgrn2_gates(h, w_in, lb)
    zeros = jnp.zeros(v.shape[:2] + (HGRN_HEAD_DIM, HGRN_HEAD_DIM), jnp.float32)
    s_f = gla_final_state(k_f, v, lf_f, zeros)
    s_b = gla_final_state(_rev(k_b), _rev(v), _rev(lf_b), zeros)
    return s_f, s_b


def hgrn2_mixer(h, w_in, lb, gnorm_w, w_out, s0_f, s0_b):
    bsz, t, _ = h.shape
    (k_f, lf_f), (k_b, lf_b), v = _hgrn2_gates(h, w_in, lb)
    q, z = jnp.split(h @ w_in[:, 3 * HGRN_WIDTH:], 2, axis=-1)
    q = _heads(jax.nn.silu(q).astype(jnp.float32)) * (HGRN_HEAD_DIM ** -0.5)
    o_f = gla_forward(q, k_f, v, lf_f, s0_f)
    o_b = _rev(gla_forward(_rev(q), _rev(k_b), _rev(v), _rev(lf_b), s0_b))
    o = (o_f + o_b).transpose(0, 2, 1, 3)
    o = rmsnorm(o, gnorm_w.reshape(HGRN_HEADS, HGRN_HEAD_DIM))
    o = o.reshape(bsz, t, HGRN_WIDTH).astype(h.dtype) * jax.nn.silu(z)
    return o @ w_out


def _centred_mean(u, w):
    length = u.shape[-2]
    csum = jnp.concatenate([jnp.zeros_like(u[..., :1, :]), jnp.cumsum(u, axis=-2)], axis=-2)
    t = jnp.arange(length)
    lo = jnp.clip(t - w // 2, 0, length)
    hi = jnp.clip(t - w // 2 + w, 0, length)
    s = jnp.take(csum, hi, axis=-2) - jnp.take(csum, lo, axis=-2)
    return s / (hi - lo).astype(u.dtype)[:, None]


def pool_mixer(h, rows, row_len, w_in, w_grp, scale, w_out):
    bsz, t, _ = h.shape
    u, z = jnp.split(h @ w_in, 2, axis=-1)
    ug = u.astype(jnp.float32).reshape(bsz, rows, row_len, POOL_GROUPS, POOL_GROUP_DIM)
    pooled = jnp.stack([_centred_mean(ug[..., g, :], w) for g, w in enumerate(POOL_WINDOWS)], axis=-2) - ug
    y = jnp.einsum('brwgi,gio->brwgo', pooled, w_grp).reshape(bsz, t, POOL_WIDTH) * scale
    return (y.astype(h.dtype) * jax.nn.silu(z)) @ w_out


def _fwd_setup_inputs(seed: int = 0) -> dict:
    key = jax.random.key(seed)
    ks = jax.random.split(key, 18)
    d, e, g = D_MODEL, HGRN_WIDTH, POOL_GROUP_DIM
    n = jax.random.normal
    return {
        'x': n(ks[0], (BATCH, SEQ, d), jnp.float32),
        'c': n(ks[1], (BATCH, d), jnp.float32),
        'ctx': n(ks[2], (BATCH, CTX_LEN, d), jnp.float32),
        'c_ctx': n(ks[3], (d,), jnp.float32),
        'ada_w': n(ks[4], (DEPTH, d, 3 * d), jnp.float32) * (0.5 * d ** -0.5),
        'ada_b': n(ks[5], (DEPTH, 3 * d), jnp.float32) * 0.02,
        'norm_w': 1.0 + 0.05 * n(ks[6], (DEPTH, d), jnp.float32),
        'hgrn_w_in': n(ks[7], (N_HGRN_LAYERS, d, 5 * e), jnp.float32) * d ** -0.5,
        'hgrn_lb_logits': 0.1 * n(ks[8], (N_HGRN_LAYERS, 2, e), jnp.float32),
        'hgrn_gnorm_w': 1.0 + 0.05 * n(ks[9], (N_HGRN_LAYERS, e), jnp.float32),
        'hgrn_w_out': n(ks[10], (N_HGRN_LAYERS, e, d), jnp.float32) * e ** -0.5,
        'pool_w_in': n(ks[11], (N_POOL_LAYERS, d, 2 * POOL_WIDTH), jnp.float32) * d ** -0.5,
        'pool_w_grp': n(ks[12], (N_POOL_LAYERS, POOL_GROUPS, g, g), jnp.float32) * g ** -0.5,
        'pool_scale': 1.0 + 0.05 * n(ks[13], (N_POOL_LAYERS, POOL_WIDTH), jnp.float32),
        'pool_w_out': n(ks[14], (N_POOL_LAYERS, POOL_WIDTH, d), jnp.float32) * POOL_WIDTH ** -0.5,
        'final_norm_w': 1.0 + 0.05 * n(ks[15], (d,), jnp.float32),
    }


def _fwd_reference(x, c, ctx, c_ctx, ada_w, ada_b, norm_w, hgrn_w_in, hgrn_lb_logits, hgrn_gnorm_w,
              hgrn_w_out, pool_w_in, pool_w_grp, pool_scale, pool_w_out, final_norm_w):
    rows = x.shape[1] // GRID_W
    lbs = lower_bounds(hgrn_lb_logits)
    h_ctx = ctx
    for i in range(DEPTH):
        is_hgrn = (i % N_MIXERS == 0)
        j = i // N_MIXERS
        ctx_update = any(l % N_MIXERS == 0 for l in range(i + 1, DEPTH))
        shift, scale, gate = _adaln(c, ada_w[i], ada_b[i])
        hx = rmsnorm(x, norm_w[i]) * (1.0 + scale) + shift
        if is_hgrn or ctx_update:
            c_shift, c_scale, c_gate = _adaln(c_ctx[None, :], ada_w[i], ada_b[i])
            hc = rmsnorm(h_ctx, norm_w[i]) * (1.0 + c_scale) + c_shift
        if is_hgrn:
            s_f, s_b = hgrn2_final_states(hc, hgrn_w_in[j], lbs[j])
            x = x + gate * hgrn2_mixer(hx, hgrn_w_in[j], lbs[j], hgrn_gnorm_w[j], hgrn_w_out[j], s_f, s_b)
            if ctx_update:
                zeros = jnp.zeros_like(s_f)
                h_ctx = h_ctx + c_gate * hgrn2_mixer(hc, hgrn_w_in[j], lbs[j], hgrn_gnorm_w[j], hgrn_w_out[j], zeros, zeros)
        else:
            x = x + gate * pool_mixer(hx, rows, GRID_W, pool_w_in[j], pool_w_grp[j], pool_scale[j], pool_w_out[j])
            if ctx_update:
                h_ctx = h_ctx + c_gate * pool_mixer(hc, 1, hc.shape[1], pool_w_in[j], pool_w_grp[j], pool_scale[j], pool_w_out[j])
    return rmsnorm(x, final_norm_w)


import jax as _jax
import jax.numpy as _jnp

TWIN_FORMAT = 'train_step'
FWD_PARAMS = ['x', 'c', 'ctx', 'c_ctx', 'ada_w', 'ada_b', 'norm_w', 'hgrn_w_in', 'hgrn_lb_logits', 'hgrn_gnorm_w', 'hgrn_w_out', 'pool_w_in', 'pool_w_grp', 'pool_scale', 'pool_w_out', 'final_norm_w']
TWIN_WEIGHTS = ['c_ctx', 'ada_w', 'ada_b', 'norm_w', 'hgrn_w_in', 'hgrn_lb_logits', 'hgrn_gnorm_w', 'hgrn_w_out', 'pool_w_in', 'pool_w_grp', 'pool_scale', 'pool_w_out', 'final_norm_w']
TWIN_DIFF_INPUT = 'x'
TWIN_INPUTS = ['x', 'c', 'ctx', 'c_ctx', 'ada_w', 'ada_b', 'norm_w', 'hgrn_w_in', 'hgrn_lb_logits', 'hgrn_gnorm_w', 'hgrn_w_out', 'pool_w_in', 'pool_w_grp', 'pool_scale', 'pool_w_out', 'final_norm_w', 'loss_target', 'm_c_ctx', 'm_ada_w', 'm_ada_b', 'm_norm_w', 'm_hgrn_w_in', 'm_hgrn_lb_logits', 'm_hgrn_gnorm_w', 'm_hgrn_w_out', 'm_pool_w_in', 'm_pool_w_grp', 'm_pool_scale', 'm_pool_w_out', 'm_final_norm_w', 'v_c_ctx', 'v_ada_w', 'v_ada_b', 'v_norm_w', 'v_hgrn_w_in', 'v_hgrn_lb_logits', 'v_hgrn_gnorm_w', 'v_hgrn_w_out', 'v_pool_w_in', 'v_pool_w_grp', 'v_pool_scale', 'v_pool_w_out', 'v_final_norm_w']
TWIN_OUTPUTS = ['loss', 'grad_x', 'grad_c_ctx', 'grad_ada_w', 'grad_ada_b', 'grad_norm_w', 'grad_hgrn_w_in', 'grad_hgrn_lb_logits', 'grad_hgrn_gnorm_w', 'grad_hgrn_w_out', 'grad_pool_w_in', 'grad_pool_w_grp', 'grad_pool_scale', 'grad_pool_w_out', 'grad_final_norm_w', 'delta_c_ctx', 'delta_ada_w', 'delta_ada_b', 'delta_norm_w', 'delta_hgrn_w_in', 'delta_hgrn_lb_logits', 'delta_hgrn_gnorm_w', 'delta_hgrn_w_out', 'delta_pool_w_in', 'delta_pool_w_grp', 'delta_pool_scale', 'delta_pool_w_out', 'delta_final_norm_w', 'new_m_c_ctx', 'new_m_ada_w', 'new_m_ada_b', 'new_m_norm_w', 'new_m_hgrn_w_in', 'new_m_hgrn_lb_logits', 'new_m_hgrn_gnorm_w', 'new_m_hgrn_w_out', 'new_m_pool_w_in', 'new_m_pool_w_grp', 'new_m_pool_scale', 'new_m_pool_w_out', 'new_m_final_norm_w', 'new_v_c_ctx', 'new_v_ada_w', 'new_v_ada_b', 'new_v_norm_w', 'new_v_hgrn_w_in', 'new_v_hgrn_lb_logits', 'new_v_hgrn_gnorm_w', 'new_v_hgrn_w_out', 'new_v_pool_w_in', 'new_v_pool_w_grp', 'new_v_pool_scale', 'new_v_pool_w_out', 'new_v_final_norm_w']
TWIN_LEAF_KINDS = {'loss': 'loss', 'grad_x': 'grad_x', 'grad_c_ctx': 'grad_w', 'grad_ada_w': 'grad_w', 'grad_ada_b': 'grad_w', 'grad_norm_w': 'grad_w', 'grad_hgrn_w_in': 'grad_w', 'grad_hgrn_lb_logits': 'grad_w', 'grad_hgrn_gnorm_w': 'grad_w', 'grad_hgrn_w_out': 'grad_w', 'grad_pool_w_in': 'grad_w', 'grad_pool_w_grp': 'grad_w', 'grad_pool_scale': 'grad_w', 'grad_pool_w_out': 'grad_w', 'grad_final_norm_w': 'grad_w', 'delta_c_ctx': 'delta_w', 'delta_ada_w': 'delta_w', 'delta_ada_b': 'delta_w', 'delta_norm_w': 'delta_w', 'delta_hgrn_w_in': 'delta_w', 'delta_hgrn_lb_logits': 'delta_w', 'delta_hgrn_gnorm_w': 'delta_w', 'delta_hgrn_w_out': 'delta_w', 'delta_pool_w_in': 'delta_w', 'delta_pool_w_grp': 'delta_w', 'delta_pool_scale': 'delta_w', 'delta_pool_w_out': 'delta_w', 'delta_final_norm_w': 'delta_w', 'new_m_c_ctx': 'new_m', 'new_m_ada_w': 'new_m', 'new_m_ada_b': 'new_m', 'new_m_norm_w': 'new_m', 'new_m_hgrn_w_in': 'new_m', 'new_m_hgrn_lb_logits': 'new_m', 'new_m_hgrn_gnorm_w': 'new_m', 'new_m_hgrn_w_out': 'new_m', 'new_m_pool_w_in': 'new_m', 'new_m_pool_w_grp': 'new_m', 'new_m_pool_scale': 'new_m', 'new_m_pool_w_out': 'new_m', 'new_m_final_norm_w': 'new_m', 'new_v_c_ctx': 'new_v', 'new_v_ada_w': 'new_v', 'new_v_ada_b': 'new_v', 'new_v_norm_w': 'new_v', 'new_v_hgrn_w_in': 'new_v', 'new_v_hgrn_lb_logits': 'new_v', 'new_v_hgrn_gnorm_w': 'new_v', 'new_v_hgrn_w_out': 'new_v', 'new_v_pool_w_in': 'new_v', 'new_v_pool_w_grp': 'new_v', 'new_v_pool_scale': 'new_v', 'new_v_pool_w_out': 'new_v', 'new_v_final_norm_w': 'new_v'}


def _forward(args):
    return _fwd_reference(*[args[k] for k in FWD_PARAMS])


def _output_shape():
    out = _jax.eval_shape(lambda: _forward(_fwd_setup_inputs(0)))
    return out.shape, out.dtype

N_MICROBATCH = 1
ADAM_LR = 0.001
ADAM_B1 = 0.9
ADAM_B2 = 0.999
ADAM_EPS = 1e-08
ADAM_WD = 0.01
ADAM_STEP = 10
PER_EXAMPLE_BATCH_AXIS = {'x': 0, 'c': 0, 'ctx': 0, 'loss_target': 0}
SHARED_INPUTS = []
_WEIGHT_DTYPES = {'c_ctx': _jnp.float32, 'ada_w': _jnp.float32, 'ada_b': _jnp.float32, 'norm_w': _jnp.float32, 'hgrn_w_in': _jnp.float32, 'hgrn_lb_logits': _jnp.float32, 'hgrn_gnorm_w': _jnp.float32, 'hgrn_w_out': _jnp.float32, 'pool_w_in': _jnp.float32, 'pool_w_grp': _jnp.float32, 'pool_scale': _jnp.float32, 'pool_w_out': _jnp.float32, 'final_norm_w': _jnp.float32}
MOMENT_SCALE = {'c_ctx': 9.910923e-04, 'ada_w': 4.181841e-02, 'ada_b': 7.009258e-02, 'norm_w': 3.223261e-02, 'hgrn_w_in': 1.681159e-02, 'hgrn_lb_logits': 1.136930e-03, 'hgrn_gnorm_w': 2.683819e-02, 'hgrn_w_out': 2.600322e-02, 'pool_w_in': 2.267997e-02, 'pool_w_grp': 2.239292e-02, 'pool_scale': 2.242062e-02, 'pool_w_out': 2.231490e-02, 'final_norm_w': 1.599932e+01}


def _to_microbatches(a, axis):
    t = _jnp.moveaxis(a, axis, 0)
    t = t.reshape((N_MICROBATCH, t.shape[0] // N_MICROBATCH) + t.shape[1:])
    return _jnp.moveaxis(t, 1, axis + 1)


def setup_inputs(seed: int = 0) -> dict:
    inp = _fwd_setup_inputs(seed)
    key = _jax.random.fold_in(_jax.random.key(seed), 7919)
    shape, _ = _output_shape()
    out = dict(inp)
    out["loss_target"] = _jax.random.normal(_jax.random.fold_in(key, 0), shape, _jnp.float32)
    for i, name in enumerate(TWIN_WEIGHTS):
        w = inp[name].astype(_jnp.float32)
        if MOMENT_SCALE is None:
            s = _jnp.sqrt(_jnp.mean(_jnp.square(w)) + 1e-30)
        else:
            s = MOMENT_SCALE[name]
        km, kv = _jax.random.split(_jax.random.fold_in(key, i + 1))
        out[name] = w
        out["m_" + name] = s * _jax.random.normal(km, w.shape, _jnp.float32)
        out["v_" + name] = (s * s) * _jax.random.uniform(kv, w.shape, _jnp.float32, 0.5, 1.5)
    if N_MICROBATCH > 1:
        for name, axis in PER_EXAMPLE_BATCH_AXIS.items():
            out[name] = _to_microbatches(out[name], axis)
    return {'x': out['x'], 'c': out['c'], 'ctx': out['ctx'], 'c_ctx': out['c_ctx'], 'ada_w': out['ada_w'], 'ada_b': out['ada_b'], 'norm_w': out['norm_w'], 'hgrn_w_in': out['hgrn_w_in'], 'hgrn_lb_logits': out['hgrn_lb_logits'], 'hgrn_gnorm_w': out['hgrn_gnorm_w'], 'hgrn_w_out': out['hgrn_w_out'], 'pool_w_in': out['pool_w_in'], 'pool_w_grp': out['pool_w_grp'], 'pool_scale': out['pool_scale'], 'pool_w_out': out['pool_w_out'], 'final_norm_w': out['final_norm_w'], 'loss_target': out['loss_target'], 'm_c_ctx': out['m_c_ctx'], 'm_ada_w': out['m_ada_w'], 'm_ada_b': out['m_ada_b'], 'm_norm_w': out['m_norm_w'], 'm_hgrn_w_in': out['m_hgrn_w_in'], 'm_hgrn_lb_logits': out['m_hgrn_lb_logits'], 'm_hgrn_gnorm_w': out['m_hgrn_gnorm_w'], 'm_hgrn_w_out': out['m_hgrn_w_out'], 'm_pool_w_in': out['m_pool_w_in'], 'm_pool_w_grp': out['m_pool_w_grp'], 'm_pool_scale': out['m_pool_scale'], 'm_pool_w_out': out['m_pool_w_out'], 'm_final_norm_w': out['m_final_norm_w'], 'v_c_ctx': out['v_c_ctx'], 'v_ada_w': out['v_ada_w'], 'v_ada_b': out['v_ada_b'], 'v_norm_w': out['v_norm_w'], 'v_hgrn_w_in': out['v_hgrn_w_in'], 'v_hgrn_lb_logits': out['v_hgrn_lb_logits'], 'v_hgrn_gnorm_w': out['v_hgrn_gnorm_w'], 'v_hgrn_w_out': out['v_hgrn_w_out'], 'v_pool_w_in': out['v_pool_w_in'], 'v_pool_w_grp': out['v_pool_w_grp'], 'v_pool_scale': out['v_pool_scale'], 'v_pool_w_out': out['v_pool_w_out'], 'v_final_norm_w': out['v_final_norm_w']}


def _loss(weights, diff, rest, loss_target):
    with _jax.named_scope("forward"):
        args = {**rest, TWIN_DIFF_INPUT: diff, **{k: w.astype(_WEIGHT_DTYPES[k]) for k, w in weights.items()}}
        y = _forward(args)
    with _jax.named_scope("loss_head"):
        err = _jnp.square(y.astype(_jnp.float32) - loss_target)
        return 0.5 * _jnp.sum(_jnp.mean(err, axis=-1)) if err.ndim else 0.5 * err


def _adamw(w, g, m, v):
    m = ADAM_B1 * m + (1.0 - ADAM_B1) * g
    v = ADAM_B2 * v + (1.0 - ADAM_B2) * _jnp.square(g)
    m_hat = m / (1.0 - ADAM_B1 ** ADAM_STEP)
    v_hat = v / (1.0 - ADAM_B2 ** ADAM_STEP)
    delta = -ADAM_LR * (m_hat / (_jnp.sqrt(v_hat) + ADAM_EPS) + ADAM_WD * w)
    return delta, m, v


def reference(x, c, ctx, c_ctx, ada_w, ada_b, norm_w, hgrn_w_in, hgrn_lb_logits, hgrn_gnorm_w, hgrn_w_out, pool_w_in, pool_w_grp, pool_scale, pool_w_out, final_norm_w, loss_target, m_c_ctx, m_ada_w, m_ada_b, m_norm_w, m_hgrn_w_in, m_hgrn_lb_logits, m_hgrn_gnorm_w, m_hgrn_w_out, m_pool_w_in, m_pool_w_grp, m_pool_scale, m_pool_w_out, m_final_norm_w, v_c_ctx, v_ada_w, v_ada_b, v_norm_w, v_hgrn_w_in, v_hgrn_lb_logits, v_hgrn_gnorm_w, v_hgrn_w_out, v_pool_w_in, v_pool_w_grp, v_pool_scale, v_pool_w_out, v_final_norm_w):
    given = dict(x=x, c=c, ctx=ctx, c_ctx=c_ctx, ada_w=ada_w, ada_b=ada_b, norm_w=norm_w, hgrn_w_in=hgrn_w_in, hgrn_lb_logits=hgrn_lb_logits, hgrn_gnorm_w=hgrn_gnorm_w, hgrn_w_out=hgrn_w_out, pool_w_in=pool_w_in, pool_w_grp=pool_w_grp, pool_scale=pool_scale, pool_w_out=pool_w_out, final_norm_w=final_norm_w, loss_target=loss_target, m_c_ctx=m_c_ctx, m_ada_w=m_ada_w, m_ada_b=m_ada_b, m_norm_w=m_norm_w, m_hgrn_w_in=m_hgrn_w_in, m_hgrn_lb_logits=m_hgrn_lb_logits, m_hgrn_gnorm_w=m_hgrn_gnorm_w, m_hgrn_w_out=m_hgrn_w_out, m_pool_w_in=m_pool_w_in, m_pool_w_grp=m_pool_w_grp, m_pool_scale=m_pool_scale, m_pool_w_out=m_pool_w_out, m_final_norm_w=m_final_norm_w, v_c_ctx=v_c_ctx, v_ada_w=v_ada_w, v_ada_b=v_ada_b, v_norm_w=v_norm_w, v_hgrn_w_in=v_hgrn_w_in, v_hgrn_lb_logits=v_hgrn_lb_logits, v_hgrn_gnorm_w=v_hgrn_gnorm_w, v_hgrn_w_out=v_hgrn_w_out, v_pool_w_in=v_pool_w_in, v_pool_w_grp=v_pool_w_grp, v_pool_scale=v_pool_scale, v_pool_w_out=v_pool_w_out, v_final_norm_w=v_final_norm_w)
    weights = {n: given[n] for n in TWIN_WEIGHTS}
    shared = {n: given[n] for n in SHARED_INPUTS}
    per_example = {n: given[n] for n in ['x', 'c', 'ctx']}
    grad_fn = _jax.value_and_grad(_loss, argnums=(0, 1))

    def one_microbatch(ex, loss_target):
        ex = dict(ex)
        diff = ex.pop(TWIN_DIFF_INPUT)
        return grad_fn(weights, diff, {**shared, **ex}, loss_target)

    if N_MICROBATCH == 1:
        loss, (grad_w, grad_x) = one_microbatch(per_example, given["loss_target"])
    else:
        def body(carry, xs):
            loss_sum, grad_sum = carry
            l_k, (gw_k, gx_k) = one_microbatch(xs[0], xs[1])
            with _jax.named_scope("update"):
                return (loss_sum + l_k, _jax.tree.map(_jnp.add, grad_sum, gw_k)), gx_k

        init = (_jnp.zeros((), _jnp.float32), _jax.tree.map(_jnp.zeros_like, weights))
        (loss, grad_w), grad_x = _jax.lax.scan(body, init, (per_example, given["loss_target"]))
    with _jax.named_scope("update"):
        delta_w, new_m, new_v = {}, {}, {}
        for n in TWIN_WEIGHTS:
            delta_w[n], new_m[n], new_v[n] = _adamw(weights[n], grad_w[n], given["m_" + n], given["v_" + n])
    return (loss, grad_x, *[grad_w[n] for n in TWIN_WEIGHTS], *[delta_w[n] for n in TWIN_WEIGHTS],
            *[new_m[n] for n in TWIN_WEIGHTS], *[new_v[n] for n in TWIN_WEIGHTS])
```

```python
import functools

import numpy as np
import jax
import jax.numpy as jnp
from jax import lax
from jax.experimental import pallas as pl
from jax.experimental.pallas import tpu as pltpu

F32 = jnp.float32
BF16 = jnp.bfloat16

D_MODEL = 1024
SEQ = 2048
CTX_LEN = 256
ROWS_ALL = CTX_LEN + SEQ
HEADS = 8
HEAD_DIM = 128
CHUNK = 64
N_CTX_CHUNKS = CTX_LEN // CHUNK
N_LAT_CHUNKS = SEQ // CHUNK
N_CHUNKS = N_CTX_CHUNKS + N_LAT_CHUNKS
GRID_W = 64
POOL_WINDOWS = (2, 4, 8, 16)
POOL_GROUPS = 4
POOL_GROUP_DIM = 256
HGRN_SECTIONS = 5
POOL_SECTIONS = 2
EPS = 1e-6
N_DEV = 8
N_CHIPS = 4
ROW_TILE = 256
SMALL_ROWS = 24

ADAM_LR = 0.001
ADAM_B1 = 0.9
ADAM_B2 = 0.999
ADAM_EPS = 1e-08
ADAM_WD = 0.01
ADAM_STEP = 10

MESH = pl.DeviceIdType.MESH
MIB = 1 << 20
ANY = pl.BlockSpec(memory_space=pl.ANY)
VMEM = pl.BlockSpec(memory_space=pltpu.VMEM)


def _params(vmem_mib=None):
    if vmem_mib is None:
        return pltpu.CompilerParams()
    return pltpu.CompilerParams(vmem_limit_bytes=vmem_mib * MIB)


def _sig(a):
    return 1.0 / (1.0 + jnp.exp(-a))


def _silu(a):
    return a * _sig(a)


def _dsilu(a):
    s = _sig(a)
    return s * (1.0 + a * (1.0 - s))


def _mm(a, b):
    return jnp.dot(a.astype(BF16), b.astype(BF16), preferred_element_type=F32)


def _mm_nt(a, b):
    return lax.dot_general(a.astype(BF16), b.astype(BF16), (((1,), (1,)), ((), ())), preferred_element_type=F32)


def _mm_tn(a, b):
    return lax.dot_general(a.astype(BF16), b.astype(BF16), (((0,), (0,)), ((), ())), preferred_element_type=F32)


def _split2(a):
    hi = a.astype(BF16)
    lo = (a - hi.astype(F32)).astype(BF16)
    return hi, lo


def _mm_exact_lhs(m_bf, a):
    hi, lo = _split2(a)
    return jnp.dot(m_bf, hi, preferred_element_type=F32) + jnp.dot(m_bf, lo, preferred_element_type=F32)


def _mm_f32(a, b):
    ah, al = _split2(a)
    bh, bl = _split2(b)
    return (jnp.dot(ah, bh, preferred_element_type=F32) + jnp.dot(al, bh, preferred_element_type=F32)
            + jnp.dot(ah, bl, preferred_element_type=F32))


def _my_place():
    return lax.axis_index("x"), lax.axis_index("y"), lax.axis_index("c")


def _all_gather_small(blk, name):
    m_per, n = blk.shape

    def body(x_ref, out_ref, send_sems, recv_sems, local_sem):
        x, y, c = _my_place()
        me, sibling = (x, y, c), (x, y, 1 - c)
        chips = [(1 - x, y), (x, 1 - y), (1 - x, 1 - y)]

        def rows(px, py, pc):
            return out_ref.at[pl.ds((4 * px + 2 * py + pc) * m_per, m_per), :]

        def copy(k, block, to, src=None):
            return pltpu.make_async_remote_copy(
                src_ref=rows(*block) if src is None else src, dst_ref=rows(*block),
                send_sem=send_sems.at[k], recv_sem=recv_sems.at[k], device_id=to, device_id_type=MESH)

        mine = pltpu.make_async_copy(x_ref, rows(*me), local_sem)
        mine.start()
        first = [copy(0, me, sibling, src=x_ref)]
        first += [copy(1 + j, me, (*chip, c), src=x_ref) for j, chip in enumerate(chips)]
        for cp in first:
            cp.start()
        passed = [copy(4 + j, (*chip, c), sibling) for j, chip in enumerate(chips)]
        for j, chip in enumerate(chips):
            copy(1 + j, (*chip, c), me).wait_recv()
            passed[j].start()
        copy(0, sibling, me).wait_recv()
        for j, chip in enumerate(chips):
            copy(4 + j, (*chip, 1 - c), me).wait_recv()
        for cp in first + passed:
            cp.wait_send()
        mine.wait()

    return pl.pallas_call(
        body, name=name,
        out_shape=jax.ShapeDtypeStruct((N_DEV * m_per, n), blk.dtype),
        in_specs=[VMEM], out_specs=VMEM,
        scratch_shapes=[pltpu.SemaphoreType.DMA((7,)), pltpu.SemaphoreType.DMA((7,)), pltpu.SemaphoreType.DMA],
    )(blk)


W_SPECS = {
    "hgrn_w_in": ((D_MODEL, 5 * D_MODEL), (1, 1280, 0, 512)),
    "hgrn_w_out": ((D_MODEL, D_MODEL), (0, 256, 0, 128)),
    "pool_w_in": ((D_MODEL, 2 * D_MODEL), (1, 512, 0, 512)),
    "pool_w_grp": ((POOL_GROUPS, POOL_GROUP_DIM, POOL_GROUP_DIM), (1, 64, 1, 32)),
    "pool_w_out": ((D_MODEL, D_MODEL), (0, 256, 0, 128)),
}
W_NAMES = tuple(W_SPECS)


def _al(v, m):
    return pl.multiple_of(v, m)


def _region(ref, spec, chip, half):
    ca, cn, ha, hn = spec
    idx = [slice(None)] * len(ref.shape)
    if ca == ha:
        if half is None:
            idx[ca] = pl.ds(_al(chip * cn, cn), cn)
        else:
            idx[ca] = pl.ds(_al(chip * cn + half * hn, hn), hn)
    else:
        idx[ca] = pl.ds(_al(chip * cn, cn), cn)
        if half is not None:
            idx[ha] = pl.ds(_al(half * hn, hn), hn)
    return ref.at[tuple(idx)]


def _half_of(ref, spec, half):
    _, _, ha, hn = spec
    idx = [slice(None)] * len(ref.shape)
    idx[ha] = pl.ds(_al(half * hn, hn), hn)
    return ref.at[tuple(idx)]


def _half_shape(name):
    full, (ca, cn, ha, hn) = W_SPECS[name]
    shp = list(full)
    shp[ca] = cn
    shp[ha] = hn
    return tuple(shp)


def _all_gather_weights(shards):
    n = len(W_NAMES)
    specs = [W_SPECS[k][1] for k in W_NAMES]

    def body(*refs):
        sh, full = refs[:n], refs[n:2 * n]
        send_sems, recv_sems, local_sems = refs[2 * n:]
        x, y, c = _my_place()
        chip_me = 2 * x + y
        sibling = (x, y, 1 - c)
        chips = [(1 - x, y), (x, 1 - y), (1 - x, 1 - y)]

        def remote(a, k, src, dst, to):
            return pltpu.make_async_remote_copy(src_ref=src, dst_ref=dst, send_sem=send_sems.at[6 * a + k],
                                                recv_sem=recv_sems.at[6 * a + k], device_id=to, device_id_type=MESH)

        local = [pltpu.make_async_copy(sh[a], _region(full[a], specs[a], chip_me, None), local_sems.at[a]) for a in range(n)]
        for cp in local:
            cp.start()
        sends = []
        for a in range(n):
            for j, (px, py) in enumerate(chips):
                cp = remote(a, j, _half_of(sh[a], specs[a], c), _region(full[a], specs[a], chip_me, c), (px, py, c))
                cp.start()
                sends.append(cp)
        for a in range(n):
            for j, (px, py) in enumerate(chips):
                landed = _region(full[a], specs[a], 2 * px + py, c)
                remote(a, j, landed, landed, (px, py, c)).wait_recv()
                cp = remote(a, 3 + j, landed, landed, sibling)
                cp.start()
                sends.append(cp)
        for a in range(n):
            for j, (px, py) in enumerate(chips):
                other = _region(full[a], specs[a], 2 * px + py, 1 - c)
                remote(a, 3 + j, other, other, sibling).wait_recv()
        for cp in sends:
            cp.wait_send()
        for cp in local:
            cp.wait()

    outs = pl.pallas_call(
        body, name="all_gather_weights",
        out_shape=[jax.ShapeDtypeStruct(W_SPECS[k][0], BF16) for k in W_NAMES],
        in_specs=[ANY] * n, out_specs=[ANY] * n,
        scratch_shapes=[pltpu.SemaphoreType.DMA((6 * n,)), pltpu.SemaphoreType.DMA((6 * n,)), pltpu.SemaphoreType.DMA((n,))],
    )(*shards)
    return dict(zip(W_NAMES, outs))


def _scatter_grads(parts):
    n = len(W_NAMES)
    specs = [W_SPECS[k][1] for k in W_NAMES]

    def body(*refs):
        part, slots = refs[:n], refs[n:2 * n]
        send_sems, recv_sems, local_sems = refs[2 * n:]
        x, y, c = _my_place()
        dev_me = 4 * x + 2 * y + c
        local = [pltpu.make_async_copy(_region(part[a], specs[a], 2 * x + y, c), slots[a].at[dev_me], local_sems.at[a])
                 for a in range(n)]
        for cp in local:
            cp.start()
        for a in range(n):
            for flip in range(1, N_DEV):
                fx, fy, fc = flip >> 2, (flip >> 1) & 1, flip & 1
                tx = 1 - x if fx else x
                ty = 1 - y if fy else y
                tc = 1 - c if fc else c
                pltpu.make_async_remote_copy(
                    src_ref=_region(part[a], specs[a], 2 * tx + ty, tc), dst_ref=slots[a].at[dev_me],
                    send_sem=send_sems.at[a], recv_sem=recv_sems.at[a], device_id=(tx, ty, tc), device_id_type=MESH).start()
        for a in range(n):
            seven = slots[a].at[pl.ds(0, N_DEV - 1)]
            pltpu.make_async_remote_copy(src_ref=seven, dst_ref=seven, send_sem=send_sems.at[a], recv_sem=recv_sems.at[a],
                                         device_id=(x, y, c), device_id_type=MESH).wait()
        for cp in local:
            cp.wait()

    return pl.pallas_call(
        body, name="scatter_grads",
        out_shape=[jax.ShapeDtypeStruct((N_DEV,) + _half_shape(k), BF16) for k in W_NAMES],
        in_specs=[ANY] * n, out_specs=[ANY] * n,
        scratch_shapes=[pltpu.SemaphoreType.DMA((n,)), pltpu.SemaphoreType.DMA((n,)), pltpu.SemaphoreType.DMA((n,))],
    )(*parts)


def _exchange_halves(halves):
    n = len(W_NAMES)
    specs = [W_SPECS[k][1] for k in W_NAMES]

    def shard_shape(k):
        shp = list(_half_shape(k))
        shp[W_SPECS[k][1][2]] *= 2
        return tuple(shp)

    def body(*refs):
        half, out = refs[:n], refs[n:2 * n]
        send_sems, recv_sems, local_sems = refs[2 * n:]
        x, y, c = _my_place()
        local = [pltpu.make_async_copy(half[a], _half_of(out[a], specs[a], c), local_sems.at[a]) for a in range(n)]
        sends = [pltpu.make_async_remote_copy(src_ref=half[a], dst_ref=_half_of(out[a], specs[a], c), send_sem=send_sems.at[a],
                                              recv_sem=recv_sems.at[a], device_id=(x, y, 1 - c), device_id_type=MESH)
                 for a in range(n)]
        for cp in local + sends:
            cp.start()
        for a in range(n):
            theirs = _half_of(out[a], specs[a], 1 - c)
            pltpu.make_async_remote_copy(src_ref=theirs, dst_ref=theirs, send_sem=send_sems.at[a], recv_sem=recv_sems.at[a],
                                         device_id=(x, y, 1 - c), device_id_type=MESH).wait_recv()
        for cp in sends:
            cp.wait_send()
        for cp in local:
            cp.wait()

    return pl.pallas_call(
        body, name="exchange_halves",
        out_shape=[jax.ShapeDtypeStruct(shard_shape(k), F32) for k in W_NAMES],
        in_specs=[ANY] * n, out_specs=[ANY] * n,
        scratch_shapes=[pltpu.SemaphoreType.DMA((n,)), pltpu.SemaphoreType.DMA((n,)), pltpu.SemaphoreType.DMA((n,))],
    )(*halves)


def _mod_parts(c_rows, ada_w, ada_b_cols):
    n_layers, _, n_cols = ada_w.shape

    def body(c_ref, w_ref, b_ref, o_ref):
        o_ref[...] = _mm_f32(_silu(c_ref[...]), w_ref[...]) + b_ref[...]

    return pl.pallas_call(
        body, name="mod_parts", grid=(n_layers,),
        out_shape=jax.ShapeDtypeStruct((n_layers, 16, n_cols), F32),
        in_specs=[pl.BlockSpec((16, D_MODEL), lambda i: (0, 0)),
                  pl.BlockSpec((None, D_MODEL, n_cols), lambda i: (i, 0, 0)),
                  pl.BlockSpec((None, 1, n_cols), lambda i: (i, 0, 0))],
        out_specs=pl.BlockSpec((None, 16, n_cols), lambda i: (i, 0, 0)),
        compiler_params=_params(40),
    )(c_rows, ada_w, ada_b_cols.reshape(n_layers, 1, n_cols))


def _ln_mod_matmul(xin, nw, shift, scale, w, name):
    rows = xin.shape[0]
    n_sec = w.shape[1] // D_MODEL
    n_mod = shift.shape[0]
    tm = ROW_TILE

    def body(x_ref, nw_ref, sh_ref, sc_ref, w_ref, g_ref, h_ref, h_scr):
        @pl.when(pl.program_id(1) == 0)
        def _():
            xv = x_ref[...]
            r = lax.rsqrt(jnp.mean(xv * xv, axis=-1, keepdims=True) + EPS)
            h = ((xv * r * nw_ref[...]) * (1.0 + sc_ref[...]) + sh_ref[...]).astype(BF16)
            h_scr[...] = h
            h_ref[...] = h

        g_ref[...] = jnp.dot(h_scr[...], w_ref[...], preferred_element_type=F32)

    mod_spec = pl.BlockSpec((None, 1, D_MODEL), lambda i, j: (jnp.minimum(i, n_mod - 1), 0, 0))
    return pl.pallas_call(
        body, name=name, grid=(rows // tm, n_sec),
        out_shape=[jax.ShapeDtypeStruct((n_sec, rows, D_MODEL), F32), jax.ShapeDtypeStruct((rows, D_MODEL), BF16)],
        in_specs=[pl.BlockSpec((tm, D_MODEL), lambda i, j: (i, 0)),
                  pl.BlockSpec((1, D_MODEL), lambda i, j: (0, 0)),
                  mod_spec, mod_spec,
                  pl.BlockSpec((D_MODEL, D_MODEL), lambda i, j: (0, j))],
        out_specs=[pl.BlockSpec((None, tm, D_MODEL), lambda i, j: (j, i, 0)),
                   pl.BlockSpec((tm, D_MODEL), lambda i, j: (i, 0))],
        scratch_shapes=[pltpu.VMEM((tm, D_MODEL), BF16)],
        compiler_params=_params(32),
    )(xin, nw, shift.reshape(n_mod, 1, D_MODEL), scale.reshape(n_mod, 1, D_MODEL), w)


def _chunk_masks(rev):
    rid = lax.broadcasted_iota(jnp.int32, (CHUNK, CHUNK), 0)
    cid = lax.broadcasted_iota(jnp.int32, (CHUNK, CHUNK), 1)
    keep = (cid >= rid) if rev else (cid <= rid)
    keep_t = (cid <= rid) if rev else (cid >= rid)
    one, zero = jnp.ones((CHUNK, CHUNK), F32), jnp.zeros((CHUNK, CHUNK), F32)
    return keep, jnp.where(keep, one, zero).astype(BF16), jnp.where(keep_t, one, zero).astype(BF16)


def _chunk_rows(t, rev, latent):
    n = N_LAT_CHUNKS if latent else N_CTX_CHUNKS
    base = CTX_LEN if latent else 0
    idx = (n - 1 - t) if rev else t
    return pl.multiple_of(base + idx * CHUNK, CHUNK)


def _gates(fpre, lb):
    sg = _sig(fpre)
    f = lb + (1.0 - lb) * sg
    return sg, f, 1.0 - f, jnp.log(f)


G_SPEC = lambda sec: pl.BlockSpec((None, ROWS_ALL, HEAD_DIM), lambda h, sec=sec: (sec, 0, h))


def _gla_forward(g5, lb_logits):
    q_scale = HEAD_DIM ** -0.5

    def body(ff_ref, fb_ref, v_ref, q_ref, lg_ref, o_ref, st_ref):
        for rev in (False, True):
            f_ref = fb_ref if rev else ff_ref
            lb = _sig(lg_ref[1:2, :] if rev else lg_ref[0:1, :])
            keep, tri, _ = _chunk_masks(rev)
            last = 0 if rev else CHUNK - 1
            mid = CHUNK // 2 if rev else CHUNK // 2 - 1
            st_ref[...] = jnp.zeros((HEAD_DIM, HEAD_DIM), F32)

            def step(t, carry, latent):
                r0 = _chunk_rows(t, rev, latent)
                rows = pl.ds(r0, CHUNK)
                _, _, k, lf = _gates(f_ref[rows, :], lb)
                v = v_ref[rows, :]
                b = _mm_exact_lhs(tri, lf)
                bl = b[last:last + 1, :]
                st = st_ref[...]
                if latent:
                    q = _silu(q_ref[rows, :]) * q_scale
                    bm = b[mid:mid + 1, :]
                    a = _mm_nt(q * jnp.exp(b - bm), k * jnp.exp(bm - b))
                    a = jnp.where(keep, a, 0.0)
                    o = _mm_nt(q * jnp.exp(b), st) + _mm(a, v)
                    orow = pl.ds(pl.multiple_of(r0 - CTX_LEN, CHUNK), CHUNK)
                    if rev:
                        o_ref[orow, :] += o
                    else:
                        o_ref[orow, :] = o
                st_ref[...] = st * jnp.exp(bl) + _mm_tn(v, k * jnp.exp(bl - b))
                return carry

            lax.fori_loop(0, N_CTX_CHUNKS, functools.partial(step, latent=False), 0)
            lax.fori_loop(0, N_LAT_CHUNKS, functools.partial(step, latent=True), 0)

    return pl.pallas_call(
        body, name="gla_forward", grid=(HEADS,),
        out_shape=jax.ShapeDtypeStruct((SEQ, D_MODEL), F32),
        in_specs=[G_SPEC(0), G_SPEC(1), G_SPEC(2), G_SPEC(3), pl.BlockSpec((2, HEAD_DIM), lambda h: (0, h))],
        out_specs=pl.BlockSpec((SEQ, HEAD_DIM), lambda h: (0, h)),
        scratch_shapes=[pltpu.VMEM((HEAD_DIM, HEAD_DIM), F32)],
        compiler_params=_params(32),
    )(g5, g5, g5, g5, lb_logits)


def _gla_backward(g5, lb_logits, d_o, d_z):
    q_scale = HEAD_DIM ** -0.5

    def body(ff_ref, fb_ref, v_ref, q_ref, lg_ref, do_ref, dz_ref, dg_ref, dlg_ref, st_ref, dst_ref, states_ref):
        dg_ref[3, 0:CTX_LEN, :] = jnp.zeros((CTX_LEN, HEAD_DIM), F32)
        dg_ref[4, 0:CTX_LEN, :] = jnp.zeros((CTX_LEN, HEAD_DIM), F32)
        dg_ref[4, CTX_LEN:ROWS_ALL, :] = dz_ref[...]
        is_row = lax.broadcasted_iota(jnp.int32, (CHUNK, HEAD_DIM), 0)
        for rev in (False, True):
            d = 1 if rev else 0
            f_ref = fb_ref if rev else ff_ref
            lb = _sig(lg_ref[d:d + 1, :])
            keep, tri, tri_t = _chunk_masks(rev)
            last = 0 if rev else CHUNK - 1
            mid = CHUNK // 2 if rev else CHUNK // 2 - 1
            st_ref[...] = jnp.zeros((HEAD_DIM, HEAD_DIM), F32)
            dst_ref[...] = jnp.zeros((HEAD_DIM, HEAD_DIM), F32)

            def state_step(t, carry, latent):
                rows = pl.ds(_chunk_rows(t, rev, latent), CHUNK)
                _, _, k, lf = _gates(f_ref[rows, :], lb)
                b = _mm_exact_lhs(tri, lf)
                bl = b[last:last + 1, :]
                st = st_ref[...]
                states_ref[t + (N_CTX_CHUNKS if latent else 0)] = st
                st_ref[...] = st * jnp.exp(bl) + _mm_tn(v_ref[rows, :], k * jnp.exp(bl - b))
                return carry

            lax.fori_loop(0, N_CTX_CHUNKS, functools.partial(state_step, latent=False), 0)
            lax.fori_loop(0, N_LAT_CHUNKS, functools.partial(state_step, latent=True), 0)

            def grad_step(i, dlb, latent):
                n = N_LAT_CHUNKS if latent else N_CTX_CHUNKS
                t = n - 1 - i
                r0 = _chunk_rows(t, rev, latent)
                rows = pl.ds(r0, CHUNK)
                sg, f, k, lf = _gates(f_ref[rows, :], lb)
                v = v_ref[rows, :]
                b = _mm_exact_lhs(tri, lf)
                bl = b[last:last + 1, :]
                e_end = jnp.exp(bl - b)
                k_end = k * e_end
                decay = jnp.exp(bl)
                d_st = dst_ref[...]
                st_prev = states_ref[t + (N_CTX_CHUNKS if latent else 0)]
                d_kend = _mm(v, d_st)
                d_decay = jnp.sum(d_st * st_prev, axis=0, keepdims=True)
                t_kend = d_kend * k_end
                d_v = _mm_nt(k_end, d_st)
                d_k = d_kend * e_end
                d_b = -t_kend
                if latent:
                    qpre = q_ref[rows, :]
                    q = _silu(qpre) * q_scale
                    bm = b[mid:mid + 1, :]
                    e_b, e_qm, e_km = jnp.exp(b), jnp.exp(b - bm), jnp.exp(bm - b)
                    q_t, q_m, k_m = q * e_b, q * e_qm, k * e_km
                    a = jnp.where(keep, _mm_nt(q_m, k_m), 0.0)
                    d_out = do_ref[pl.ds(pl.multiple_of(r0 - CTX_LEN, CHUNK), CHUNK), :]
                    d_a = jnp.where(keep, _mm_nt(d_out, v), 0.0)
                    d_qm = _mm(d_a, k_m)
                    d_km = _mm_tn(d_a, q_m)
                    d_qt = _mm(d_out, st_prev)
                    d_v = d_v + _mm_tn(a, d_out)
                    d_k = d_k + d_km * e_km
                    d_b = d_b + d_qt * q_t + d_qm * q_m - d_km * k_m
                    d_q = d_qt * e_b + d_qm * e_qm
                    dst_ref[...] = d_st * decay + _mm_tn(d_out, q_t)
                else:
                    dst_ref[...] = d_st * decay
                at_last = jnp.sum(t_kend, axis=0, keepdims=True) + d_decay * decay
                d_b = d_b + jnp.where(is_row == last, at_last, 0.0)
                d_lf = _mm_exact_lhs(tri_t, d_b)
                d_f = d_lf / f - d_k
                dg_ref[d, rows, :] = d_f * (1.0 - lb) * sg * (1.0 - sg)
                if rev:
                    dg_ref[2, rows, :] += d_v
                else:
                    dg_ref[2, rows, :] = d_v
                if latent:
                    d_qpre = d_q * q_scale * _dsilu(qpre)
                    if rev:
                        dg_ref[3, rows, :] += d_qpre
                    else:
                        dg_ref[3, rows, :] = d_qpre
                return dlb + jnp.sum(d_f * (1.0 - sg), axis=0, keepdims=True)

            dlb = lax.fori_loop(0, N_LAT_CHUNKS, functools.partial(grad_step, latent=True), jnp.zeros((1, HEAD_DIM), F32))
            dlb = lax.fori_loop(0, N_CTX_CHUNKS, functools.partial(grad_step, latent=False), dlb)
            dlg_ref[d:d + 1, :] = dlb * lb * (1.0 - lb)

    col = pl.BlockSpec((SEQ, HEAD_DIM), lambda h: (0, h))
    return pl.pallas_call(
        body, name="gla_backward", grid=(HEADS,),
        out_shape=[jax.ShapeDtypeStruct((HGRN_SECTIONS, ROWS_ALL, D_MODEL), F32), jax.ShapeDtypeStruct((2, D_MODEL), F32)],
        in_specs=[G_SPEC(0), G_SPEC(1), G_SPEC(2), G_SPEC(3), pl.BlockSpec((2, HEAD_DIM), lambda h: (0, h)), col, col],
        out_specs=[pl.BlockSpec((HGRN_SECTIONS, ROWS_ALL, HEAD_DIM), lambda h: (0, 0, h)),
                   pl.BlockSpec((2, HEAD_DIM), lambda h: (0, h))],
        scratch_shapes=[pltpu.VMEM((HEAD_DIM, HEAD_DIM), F32), pltpu.VMEM((HEAD_DIM, HEAD_DIM), F32),
                        pltpu.VMEM((N_CHUNKS, HEAD_DIM, HEAD_DIM), F32)],
        compiler_params=_params(48),
    )(g5, g5, g5, g5, lb_logits, d_o, d_z)


def _head_norm(o, gw, scr):
    rs = []
    for h in range(HEADS):
        cols = slice(h * HEAD_DIM, (h + 1) * HEAD_DIM)
        oh = o[:, cols]
        r = lax.rsqrt(jnp.mean(oh * oh, axis=-1, keepdims=True) + EPS)
        scr[:, cols] = oh * r
        rs.append(r)
    return rs


def _hgrn_out_forward(o_raw, g5, xin, gnorm_w, gate, w_out):
    tm = ROW_TILE

    def body(o_ref, z_ref, x_ref, gw_ref, gate_ref, w_ref, x1_ref, res_ref, scr):
        _head_norm(o_ref[...], None, scr)
        a = scr[...] * gw_ref[...] * _silu(z_ref[...])
        res = _mm(a, w_ref[...])
        res_ref[...] = res
        x1_ref[...] = x_ref[...] + gate_ref[...] * res

    tile = pl.BlockSpec((tm, D_MODEL), lambda i: (i, 0))
    vec = pl.BlockSpec((1, D_MODEL), lambda i: (0, 0))
    return pl.pallas_call(
        body, name="hgrn_out_forward", grid=(SEQ // tm,),
        out_shape=[jax.ShapeDtypeStruct((SEQ, D_MODEL), F32)] * 2,
        in_specs=[tile, pl.BlockSpec((None, tm, D_MODEL), lambda i: (4, i + CTX_LEN // tm, 0)), tile, vec, vec,
                  pl.BlockSpec((D_MODEL, D_MODEL), lambda i: (0, 0))],
        out_specs=[tile, tile],
        scratch_shapes=[pltpu.VMEM((tm, D_MODEL), F32)],
        compiler_params=_params(32),
    )(o_raw, g5, xin, gnorm_w, gate, w_out)


def _hgrn_out_backward(d_x1, o_raw, g5, res, gnorm_w, gate, w_out):
    tm = ROW_TILE

    def body(dx_ref, o_ref, z_ref, res_ref, gw_ref, gate_ref, w_ref, do_ref, dz_ref, dw_ref, dgate_ref, dgw_ref, scr, scr2):
        @pl.when(pl.program_id(0) == 0)
        def _():
            dw_ref[...] = jnp.zeros_like(dw_ref)
            dgate_ref[...] = jnp.zeros_like(dgate_ref)
            dgw_ref[...] = jnp.zeros_like(dgw_ref)

        dx = dx_ref[...]
        dgate_ref[...] += jnp.sum(dx * res_ref[...], axis=0, keepdims=True)
        d_res = (dx * gate_ref[...]).astype(BF16)
        d_a = _mm_nt(d_res, w_ref[...])
        rs = _head_norm(o_ref[...], None, scr)
        z = z_ref[...]
        sz = _silu(z)
        o_hat = scr[...]
        o_n = o_hat * gw_ref[...]
        dw_ref[...] += _mm_tn(o_n * sz, d_res)
        d_on = d_a * sz
        dz_ref[...] = d_a * o_n * _dsilu(z)
        dgw_ref[...] += jnp.sum(d_on * o_hat, axis=0, keepdims=True)
        scr2[...] = d_on * gw_ref[...]
        for h in range(HEADS):
            cols = slice(h * HEAD_DIM, (h + 1) * HEAD_DIM)
            dh, oh = scr2[:, cols], scr[:, cols]
            do_ref[:, cols] = rs[h] * (dh - oh * jnp.mean(dh * oh, axis=-1, keepdims=True))

    tile = pl.BlockSpec((tm, D_MODEL), lambda i: (i, 0))
    vec = pl.BlockSpec((1, D_MODEL), lambda i: (0, 0))
    mat = pl.BlockSpec((D_MODEL, D_MODEL), lambda i: (0, 0))
    return pl.pallas_call(
        body, name="hgrn_out_backward", grid=(SEQ // tm,),
        out_shape=[jax.ShapeDtypeStruct((SEQ, D_MODEL), F32)] * 2 + [jax.ShapeDtypeStruct((D_MODEL, D_MODEL), F32)]
        + [jax.ShapeDtypeStruct((1, D_MODEL), F32)] * 2,
        in_specs=[tile, tile, pl.BlockSpec((None, tm, D_MODEL), lambda i: (4, i + CTX_LEN // tm, 0)), tile, vec, vec, mat],
        out_specs=[tile, tile, mat, vec, vec],
        scratch_shapes=[pltpu.VMEM((tm, D_MODEL), F32)] * 2,
        compiler_params=_params(40),
    )(d_x1, o_raw, g5, res, gnorm_w, gate, w_out)


def _pool_constants():
    win = np.zeros((POOL_GROUPS, ROW_TILE, ROW_TILE), np.float32)
    inv = np.zeros((POOL_GROUPS, ROW_TILE, 1), np.float32)
    for g, w in enumerate(POOL_WINDOWS):
        for t in range(ROW_TILE):
            base, p = (t // GRID_W) * GRID_W, t % GRID_W
            lo = min(max(p - w // 2, 0), GRID_W)
            hi = min(max(p - w // 2 + w, 0), GRID_W)
            win[g, t, base + lo:base + hi] = 1.0
            inv[g, t, 0] = 1.0 / np.float32(hi - lo)
    return jnp.asarray(win, BF16), jnp.asarray(win.transpose(0, 2, 1), BF16), jnp.asarray(inv, F32)


def _pool_mix(u_ref, wg_ref, ps_ref, win_ref, inv_ref, pooled_scr, yg_scr):
    for g in range(POOL_GROUPS):
        cols = slice(g * POOL_GROUP_DIM, (g + 1) * POOL_GROUP_DIM)
        ug = u_ref[:, cols]
        pooled = _mm_exact_lhs(win_ref[g], ug) * inv_ref[g] - ug
        if pooled_scr is not None:
            pooled_scr[:, cols] = pooled
        yg_scr[:, cols] = _mm(pooled, wg_ref[g])


def _pool_forward_loss(uz, x1, target, gate, w_grp, pool_scale, w_out, final_w):
    tm = ROW_TILE
    win, _, inv = _pool_constants()

    def body(u_ref, z_ref, x_ref, t_ref, gate_ref, wg_ref, ps_ref, w_ref, fw_ref, win_ref, inv_ref,
             dx_ref, loss_ref, dfw_ref, dgate_ref, yg_scr):
        @pl.when(pl.program_id(0) == 0)
        def _():
            loss_ref[...] = jnp.zeros_like(loss_ref)
            dfw_ref[...] = jnp.zeros_like(dfw_ref)
            dgate_ref[...] = jnp.zeros_like(dgate_ref)

        _pool_mix(u_ref, wg_ref, ps_ref, win_ref, inv_ref, None, yg_scr)
        a = yg_scr[...] * ps_ref[...] * _silu(z_ref[...])
        res = _mm(a, w_ref[...])
        x2 = x_ref[...] + gate_ref[...] * res
        r = lax.rsqrt(jnp.mean(x2 * x2, axis=-1, keepdims=True) + EPS)
        xh = x2 * r
        fw = fw_ref[...]
        err = xh * fw - t_ref[...]
        loss_ref[...] += 0.5 * jnp.sum(jnp.mean(err * err, axis=-1, keepdims=True))
        d_y = err * (1.0 / D_MODEL)
        dfw_ref[...] += jnp.sum(d_y * xh, axis=0, keepdims=True)
        d_xh = d_y * fw
        d_x2 = r * (d_xh - xh * jnp.mean(d_xh * xh, axis=-1, keepdims=True))
        dx_ref[...] = d_x2
        dgate_ref[...] += jnp.sum(d_x2 * res, axis=0, keepdims=True)

    tile = pl.BlockSpec((tm, D_MODEL), lambda i: (i, 0))
    vec = pl.BlockSpec((1, D_MODEL), lambda i: (0, 0))
    grp = pl.BlockSpec((POOL_GROUPS, POOL_GROUP_DIM, POOL_GROUP_DIM), lambda i: (0, 0, 0))
    return pl.pallas_call(
        body, name="pool_forward_loss", grid=(SEQ // tm,),
        out_shape=[jax.ShapeDtypeStruct((SEQ, D_MODEL), F32), jax.ShapeDtypeStruct((8, 128), F32),
                   jax.ShapeDtypeStruct((1, D_MODEL), F32), jax.ShapeDtypeStruct((1, D_MODEL), F32)],
        in_specs=[pl.BlockSpec((None, tm, D_MODEL), lambda i: (0, i, 0)), pl.BlockSpec((None, tm, D_MODEL), lambda i: (1, i, 0)),
                  tile, tile, vec, grp, vec, pl.BlockSpec((D_MODEL, D_MODEL), lambda i: (0, 0)), vec, grp,
                  pl.BlockSpec((POOL_GROUPS, ROW_TILE, 1), lambda i: (0, 0, 0))],
        out_specs=[tile, pl.BlockSpec((8, 128), lambda i: (0, 0)), vec, vec],
        scratch_shapes=[pltpu.VMEM((tm, D_MODEL), F32)],
        compiler_params=_params(32),
    )(uz, uz, x1, target, gate, w_grp, pool_scale, w_out, final_w, win, inv)


def _pool_backward(d_x2, uz, gate, w_grp, pool_scale, w_out):
    tm = ROW_TILE
    win, win_t, inv = _pool_constants()

    def body(dx_ref, u_ref, z_ref, gate_ref, wg_ref, ps_ref, w_ref, win_ref, wint_ref, inv_ref,
             duz_ref, dw_ref, dwg_ref, dps_ref, pooled_scr, yg_scr, dyg_scr):
        @pl.when(pl.program_id(0) == 0)
        def _():
            dw_ref[...] = jnp.zeros_like(dw_ref)
            dwg_ref[...] = jnp.zeros_like(dwg_ref)
            dps_ref[...] = jnp.zeros_like(dps_ref)

        _pool_mix(u_ref, wg_ref, ps_ref, win_ref, inv_ref, pooled_scr, yg_scr)
        z = z_ref[...]
        sz = _silu(z)
        yg = yg_scr[...]
        y = yg * ps_ref[...]
        d_res = (dx_ref[...] * gate_ref[...]).astype(BF16)
        d_a = _mm_nt(d_res, w_ref[...])
        dw_ref[...] += _mm_tn(y * sz, d_res)
        d_y = d_a * sz
        duz_ref[1] = d_a * y * _dsilu(z)
        dps_ref[...] += jnp.sum(d_y * yg, axis=0, keepdims=True)
        dyg_scr[...] = d_y * ps_ref[...]
        for g in range(POOL_GROUPS):
            cols = slice(g * POOL_GROUP_DIM, (g + 1) * POOL_GROUP_DIM)
            d_yg = dyg_scr[:, cols].astype(BF16)
            d_pool = _mm_nt(d_yg, wg_ref[g])
            dwg_ref[g] += _mm_tn(pooled_scr[:, cols], d_yg)
            duz_ref[0, :, cols] = _mm_exact_lhs(wint_ref[g], d_pool * inv_ref[g]) - d_pool

    tile = pl.BlockSpec((tm, D_MODEL), lambda i: (i, 0))
    vec = pl.BlockSpec((1, D_MODEL), lambda i: (0, 0))
    mat = pl.BlockSpec((D_MODEL, D_MODEL), lambda i: (0, 0))
    grp = pl.BlockSpec((POOL_GROUPS, POOL_GROUP_DIM, POOL_GROUP_DIM), lambda i: (0, 0, 0))
    return pl.pallas_call(
        body, name="pool_backward", grid=(SEQ // tm,),
        out_shape=[jax.ShapeDtypeStruct((POOL_SECTIONS, SEQ, D_MODEL), F32), jax.ShapeDtypeStruct((D_MODEL, D_MODEL), F32),
                   jax.ShapeDtypeStruct((POOL_GROUPS, POOL_GROUP_DIM, POOL_GROUP_DIM), F32), jax.ShapeDtypeStruct((1, D_MODEL), F32)],
        in_specs=[tile, pl.BlockSpec((None, tm, D_MODEL), lambda i: (0, i, 0)), pl.BlockSpec((None, tm, D_MODEL), lambda i: (1, i, 0)),
                  vec, grp, vec, mat, grp, grp, pl.BlockSpec((POOL_GROUPS, ROW_TILE, 1), lambda i: (0, 0, 0))],
        out_specs=[pl.BlockSpec((POOL_SECTIONS, tm, D_MODEL), lambda i: (0, i, 0)), mat, grp, vec],
        scratch_shapes=[pltpu.VMEM((tm, D_MODEL), F32)] * 3,
        compiler_params=_params(40),
    )(d_x2, uz, uz, gate, w_grp, pool_scale, w_out, win, win_t, inv)


def _ln_mod_backward(d_g, w, xin, nw, scale, d_up, name):
    n_sec, rows, _ = d_g.shape
    n_mod = scale.shape[0]
    tm = ROW_TILE
    skip = n_mod - 1

    def body(dg_ref, w_ref, x_ref, nw_ref, sc_ref, up_ref, dx_ref, dnw_ref, dmod_ref, acc):
        i, k = pl.program_id(0), pl.program_id(1)

        @pl.when(k == 0)
        def _():
            acc[...] = jnp.zeros_like(acc)

        acc[...] += _mm_nt(dg_ref[...], w_ref[...])

        @pl.when(k == n_sec - 1)
        def _():
            @pl.when(i == 0)
            def _():
                dnw_ref[...] = jnp.zeros_like(dnw_ref)

            @pl.when((i == 0) | (i == skip))
            def _():
                dmod_ref[...] = jnp.zeros_like(dmod_ref)

            d_h = acc[...]
            xv = x_ref[...]
            r = lax.rsqrt(jnp.mean(xv * xv, axis=-1, keepdims=True) + EPS)
            xh = xv * r
            nw_row = nw_ref[...]
            dmod_ref[0:1, :] += jnp.sum(d_h, axis=0, keepdims=True)
            dmod_ref[1:2, :] += jnp.sum(d_h * (xh * nw_row), axis=0, keepdims=True)
            d_xn = d_h * (1.0 + sc_ref[...])
            dnw_ref[...] += jnp.sum(d_xn * xh, axis=0, keepdims=True)
            d_xh = d_xn * nw_row

            @pl.when(i >= skip)
            def _():
                dx_ref[...] = up_ref[...] + r * (d_xh - xh * jnp.mean(d_xh * xh, axis=-1, keepdims=True))

    lat = lambda i, k: (jnp.maximum(i - skip, 0), 0)
    mod_idx = lambda i, k: (jnp.minimum(i, n_mod - 1), 0, 0)
    return pl.pallas_call(
        body, name=name, grid=(rows // tm, n_sec),
        out_shape=[jax.ShapeDtypeStruct((rows - skip * tm, D_MODEL), F32), jax.ShapeDtypeStruct((1, D_MODEL), F32),
                   jax.ShapeDtypeStruct((n_mod, 8, D_MODEL), F32)],
        in_specs=[pl.BlockSpec((None, tm, D_MODEL), lambda i, k: (k, i, 0)),
                  pl.BlockSpec((D_MODEL, D_MODEL), lambda i, k: (0, k)),
                  pl.BlockSpec((tm, D_MODEL), lambda i, k: (i, 0)),
                  pl.BlockSpec((1, D_MODEL), lambda i, k: (0, 0)),
                  pl.BlockSpec((None, 1, D_MODEL), mod_idx),
                  pl.BlockSpec((tm, D_MODEL), lat)],
        out_specs=[pl.BlockSpec((tm, D_MODEL), lat), pl.BlockSpec((1, D_MODEL), lambda i, k: (0, 0)),
                   pl.BlockSpec((None, 8, D_MODEL), mod_idx)],
        scratch_shapes=[pltpu.VMEM((tm, D_MODEL), F32)],
        compiler_params=_params(32),
    )(d_g, w, xin, nw, scale.reshape(n_mod, 1, D_MODEL), d_up)


def _weight_grad(h, d_g, name):
    n_sec, rows, _ = d_g.shape
    tm = 768 if rows % 768 == 0 else 512

    def body(h_ref, dg_ref, dw_ref):
        @pl.when(pl.program_id(1) == 0)
        def _():
            dw_ref[...] = jnp.zeros_like(dw_ref)

        dw_ref[...] += _mm_tn(h_ref[...], dg_ref[...])

    return pl.pallas_call(
        body, name=name, grid=(n_sec, rows // tm),
        out_shape=jax.ShapeDtypeStruct((D_MODEL, n_sec * D_MODEL), F32),
        in_specs=[pl.BlockSpec((tm, D_MODEL), lambda j, i: (i, 0)), pl.BlockSpec((None, tm, D_MODEL), lambda j, i: (j, i, 0))],
        out_specs=pl.BlockSpec((D_MODEL, D_MODEL), lambda j, i: (0, j)),
        compiler_params=_params(32),
    )(h, d_g)


def _sum_slots(slots, name):
    _, rows, cols = slots.shape
    tr = 128

    def body(s_ref, o_ref):
        acc = s_ref[0].astype(F32)
        for d in range(1, N_DEV):
            acc = acc + s_ref[d].astype(F32)
        o_ref[...] = acc

    return pl.pallas_call(
        body, name=name, grid=(rows // tr,),
        out_shape=jax.ShapeDtypeStruct((rows, cols), F32),
        in_specs=[pl.BlockSpec((N_DEV, tr, cols), lambda i: (0, i, 0))],
        out_specs=pl.BlockSpec((tr, cols), lambda i: (i, 0)),
    )(slots)


def _adamw_math(w, g, m, v):
    m = ADAM_B1 * m + (1.0 - ADAM_B1) * g
    v = ADAM_B2 * v + (1.0 - ADAM_B2) * (g * g)
    m_hat = m / (1.0 - ADAM_B1 ** ADAM_STEP)
    v_hat = v / (1.0 - ADAM_B2 ** ADAM_STEP)
    return -ADAM_LR * (m_hat / (jnp.sqrt(v_hat) + ADAM_EPS) + ADAM_WD * w), m, v


def _adamw(w, g, m, v, name):
    rows, cols = w.shape
    tr = rows if rows <= 128 else 128

    def body(w_ref, g_ref, m_ref, v_ref, d_ref, mo_ref, vo_ref):
        d_ref[...], mo_ref[...], vo_ref[...] = _adamw_math(w_ref[...], g_ref[...], m_ref[...], v_ref[...])

    tile = pl.BlockSpec((tr, cols), lambda i: (i, 0))
    return pl.pallas_call(
        body, name=name, grid=(rows // tr,),
        out_shape=[jax.ShapeDtypeStruct((rows, cols), F32)] * 3,
        in_specs=[tile] * 4, out_specs=[tile] * 3,
    )(w, g, m, v)


def _sum_devices(gathered):
    def body(p_ref, o_ref):
        acc = p_ref[0:SMALL_ROWS, :]
        for d in range(1, N_DEV):
            acc = acc + p_ref[d * SMALL_ROWS:(d + 1) * SMALL_ROWS, :]
        o_ref[...] = acc

    return pl.pallas_call(body, name="sum_devices", out_shape=jax.ShapeDtypeStruct((SMALL_ROWS, D_MODEL), F32),
                          in_specs=[VMEM], out_specs=VMEM)(gathered)


def _ada_update(cond_t, d_mod, ada_w, m, v):
    n_layers, _, n_cols = ada_w.shape
    tr = ROW_TILE

    def body(c_ref, dm_ref, w_ref, m_ref, v_ref, g_ref, d_ref, mo_ref, vo_ref):
        g = _mm_f32(_silu(c_ref[...]), dm_ref[...])
        g_ref[...] = g
        d_ref[...], mo_ref[...], vo_ref[...] = _adamw_math(w_ref[...], g, m_ref[...], v_ref[...])

    tile = pl.BlockSpec((None, tr, n_cols), lambda l, i: (l, i, 0))
    return pl.pallas_call(
        body, name="ada_update", grid=(n_layers, D_MODEL // tr),
        out_shape=[jax.ShapeDtypeStruct(ada_w.shape, F32)] * 4,
        in_specs=[pl.BlockSpec((tr, 16), lambda l, i: (i, 0)), pl.BlockSpec((None, 16, n_cols), lambda l, i: (l, 0, 0)),
                  tile, tile, tile],
        out_specs=[tile] * 4,
    )(cond_t, d_mod, ada_w, m, v)


def _cond_ctx_partial(d_modc, ada_w0):
    n_cols = ada_w0.shape[1]
    tr = ROW_TILE

    def body(dm_ref, w_ref, o_ref):
        o_ref[...] = jnp.sum(w_ref[...] * dm_ref[...], axis=-1, keepdims=True)

    return pl.pallas_call(
        body, name="cond_ctx_partial", grid=(D_MODEL // tr,),
        out_shape=jax.ShapeDtypeStruct((D_MODEL, 1), F32),
        in_specs=[pl.BlockSpec((1, n_cols), lambda i: (0, 0)), pl.BlockSpec((tr, n_cols), lambda i: (i, 0))],
        out_specs=pl.BlockSpec((tr, 1), lambda i: (i, 0)),
    )(d_modc, ada_w0)


def _cond_ctx_update(gathered, c_ctx, m, v):
    def body(p_ref, w_ref, m_ref, v_ref, g_ref, d_ref, mo_ref, vo_ref):
        acc = p_ref[0:1, :]
        for s in range(1, N_CHIPS):
            acc = acc + p_ref[16 * s:16 * s + 1, :]
        w = w_ref[...]
        g = acc * _dsilu(w)
        g_ref[...] = g
        d_ref[...], mo_ref[...], vo_ref[...] = _adamw_math(w, g, m_ref[...], v_ref[...])

    return pl.pallas_call(body, name="cond_ctx_update", out_shape=[jax.ShapeDtypeStruct((1, D_MODEL), F32)] * 4,
                          in_specs=[VMEM] * 4, out_specs=[VMEM] * 4)(gathered, c_ctx, m, v)


def _local_step(x2, ctx2, target, mod_mine, mod_ctx, lb_logits, scale_full, full, norm_w, gnorm, final_w):
    row = lambda a: a.reshape(1, -1)
    shift0, scale0, gate0 = (row(a) for a in jnp.split(mod_mine[0], 3))
    shift1, scale1, gate1 = (row(a) for a in jnp.split(mod_mine[1], 3))
    shift_c, scale_c, _ = (row(a) for a in jnp.split(mod_ctx, 3))
    xa = jnp.concatenate([ctx2, x2], axis=0)
    nw0, nw1 = norm_w[0:1], norm_w[1:2]
    scales0 = jnp.concatenate([scale_c, scale0])

    g5, h0 = _ln_mod_matmul(xa, nw0, jnp.concatenate([shift_c, shift0]), scales0, full["hgrn_w_in"], "hgrn_in_forward")
    o_raw = _gla_forward(g5, lb_logits)
    x1, res0 = _hgrn_out_forward(o_raw, g5, x2, gnorm, gate0, full["hgrn_w_out"])
    uz, h1 = _ln_mod_matmul(x1, nw1, shift1, scale1, full["pool_w_in"], "pool_in_forward")
    d_x2, loss_part, d_final, d_gate1 = _pool_forward_loss(uz, x1, target, gate1, full["pool_w_grp"], scale_full,
                                                           full["pool_w_out"], final_w)

    d_uz, dw_pool_out, dw_pool_grp, d_pscale = _pool_backward(d_x2, uz, gate1, full["pool_w_grp"], scale_full, full["pool_w_out"])
    d_x1, d_nw1, d_mod1 = _ln_mod_backward(d_uz, full["pool_w_in"], x1, nw1, scale1, d_x2, "pool_in_backward")
    dw_pool_in = _weight_grad(h1, d_uz, "pool_in_weight_grad")
    d_o, d_z, dw_hgrn_out, d_gate0, d_gnorm = _hgrn_out_backward(d_x1, o_raw, g5, res0, gnorm, gate0, full["hgrn_w_out"])
    d_g5, d_lb = _gla_backward(g5, lb_logits, d_o, d_z)
    d_x, d_nw0, d_mod0 = _ln_mod_backward(d_g5, full["hgrn_w_in"], xa, nw0, scales0, d_x1, "hgrn_in_backward")
    dw_hgrn_in = _weight_grad(h0, d_g5, "hgrn_in_weight_grad")

    zero = jnp.zeros((1, D_MODEL), F32)
    small = jnp.concatenate([d_mod0[1, 0:2], d_gate0, d_mod1[0, 0:2], d_gate1, d_mod0[0, 0:2], zero, d_nw0, d_nw1, d_gnorm,
                             d_final, d_pscale, d_lb, jnp.broadcast_to(loss_part[0:1, 0:1], (1, D_MODEL)),
                             jnp.zeros((SMALL_ROWS - 17, D_MODEL), F32)], axis=0)
    grads = {"hgrn_w_in": dw_hgrn_in, "hgrn_w_out": dw_hgrn_out, "pool_w_in": dw_pool_in, "pool_w_grp": dw_pool_grp,
             "pool_w_out": dw_pool_out}
    return {"d_x": d_x, "grads": grads, "small": small}


def kernel(x, c, ctx, c_ctx, ada_w, ada_b, norm_w, hgrn_w_in, hgrn_lb_logits, hgrn_gnorm_w, hgrn_w_out, pool_w_in, pool_w_grp, pool_scale, pool_w_out, final_norm_w, loss_target, m_c_ctx, m_ada_w, m_ada_b, m_norm_w, m_hgrn_w_in, m_hgrn_lb_logits, m_hgrn_gnorm_w, m_hgrn_w_out, m_pool_w_in, m_pool_w_grp, m_pool_scale, m_pool_w_out, m_final_norm_w, v_c_ctx, v_ada_w, v_ada_b, v_norm_w, v_hgrn_w_in, v_hgrn_lb_logits, v_hgrn_gnorm_w, v_hgrn_w_out, v_pool_w_in, v_pool_w_grp, v_pool_scale, v_pool_w_out, v_final_norm_w):
    xi, yi, ci = _my_place()
    chip = 2 * xi + yi
    dev = 4 * xi + 2 * yi + ci
    ada_cols = ada_w.shape[2]
    lb_cols = hgrn_lb_logits.shape[2]
    ps_cols = pool_scale.shape[1]
    row = lambda a: a.reshape(1, -1)

    def chip_cols(a, n):
        return lax.dynamic_slice_in_dim(a, chip * n, n, axis=a.ndim - 1)

    def from_chips(g, rows_per_dev, take):
        return jnp.concatenate([g[2 * s * rows_per_dev:2 * s * rows_per_dev + take] for s in range(N_CHIPS)], axis=1)

    first = jnp.concatenate([jnp.broadcast_to(c, (8, D_MODEL)), jnp.pad(hgrn_lb_logits[0], ((0, 6), (0, 0))),
                             jnp.pad(pool_scale, ((0, 7), (0, 0)))], axis=1)
    first_all = _all_gather_small(first, "gather_cond")
    cond_all = first_all[::8, :D_MODEL]
    lb_logits = from_chips(first_all[:, D_MODEL:D_MODEL + lb_cols], 8, 2)
    scale_full = from_chips(first_all[:, D_MODEL + lb_cols:], 8, 1)
    cond_rows = jnp.concatenate([cond_all, row(c_ctx), jnp.zeros((7, D_MODEL), F32)], axis=0)

    parts = _mod_parts(cond_rows, ada_w, chip_cols(ada_b, ada_cols))
    parts_all = _all_gather_small(parts.reshape(32, ada_cols), "gather_mod")
    mod_all = from_chips(parts_all, 32, 32).reshape(2, 16, 3 * D_MODEL)
    mod_mine = lax.dynamic_index_in_dim(mod_all, dev, axis=1, keepdims=False)

    shards = {"hgrn_w_in": hgrn_w_in[0], "hgrn_w_out": hgrn_w_out[0], "pool_w_in": pool_w_in[0],
              "pool_w_grp": pool_w_grp[0], "pool_w_out": pool_w_out[0]}
    full = _all_gather_weights([shards[k].astype(BF16) for k in W_NAMES])

    loc = _local_step(x[0], ctx[0], loss_target[0], mod_mine, mod_all[0, 8], lb_logits, scale_full, full, norm_w,
                      hgrn_gnorm_w, row(final_norm_w))

    small_all = _all_gather_small(loc["small"], "gather_small")
    sums = _sum_devices(small_all)
    loss = sums[16, 0]

    slots = _scatter_grads([loc["grads"][k].astype(BF16) for k in W_NAMES])
    halves = []
    for k, s in zip(W_NAMES, slots):
        hs = _half_shape(k)
        flat = _sum_slots(s.reshape(N_DEV, -1, hs[-1]), "sum_" + k)
        halves.append(flat.reshape(hs))
    big_grads = dict(zip(W_NAMES, _exchange_halves(halves)))

    out = {}

    def update(name, w, g, m, v):
        shape = w.shape
        w2, g2, m2, v2 = (a.reshape(-1, shape[-1]) for a in (w, g, m, v))
        d, mn, vn = _adamw(w2, g2, m2, v2, "adamw_" + name)
        out[name] = tuple(a.reshape(shape) for a in (g2, d, mn, vn))

    moments = {"hgrn_w_in": (m_hgrn_w_in, v_hgrn_w_in), "hgrn_w_out": (m_hgrn_w_out, v_hgrn_w_out),
               "pool_w_in": (m_pool_w_in, v_pool_w_in), "pool_w_grp": (m_pool_w_grp, v_pool_w_grp),
               "pool_w_out": (m_pool_w_out, v_pool_w_out)}
    weights = {"hgrn_w_in": hgrn_w_in, "hgrn_w_out": hgrn_w_out, "pool_w_in": pool_w_in, "pool_w_grp": pool_w_grp,
               "pool_w_out": pool_w_out}
    for k in W_NAMES:
        update(k, weights[k], big_grads[k].reshape(weights[k].shape), *moments[k])

    g_ada_b = jnp.stack([(sums[0:3] + sums[6:9]).reshape(-1), sums[3:6].reshape(-1)])
    update("ada_b", ada_b, g_ada_b, m_ada_b, v_ada_b)
    update("norm_w", norm_w, sums[9:11], m_norm_w, v_norm_w)
    update("hgrn_gnorm_w", hgrn_gnorm_w, sums[11:12], m_hgrn_gnorm_w, v_hgrn_gnorm_w)
    update("final_norm_w", row(final_norm_w), sums[12:13], row(m_final_norm_w), row(v_final_norm_w))
    update("pool_scale", pool_scale, chip_cols(sums[13:14], ps_cols), m_pool_scale, v_pool_scale)
    update("hgrn_lb_logits", hgrn_lb_logits, chip_cols(sums[14:16], lb_cols)[None], m_hgrn_lb_logits, v_hgrn_lb_logits)

    per_dev = small_all.reshape(N_DEV, SMALL_ROWS, D_MODEL)
    pad7 = jnp.zeros((7, 3 * D_MODEL), F32)
    dm0 = jnp.concatenate([per_dev[:, 0:3].reshape(N_DEV, -1), sums[6:9].reshape(1, -1), pad7], axis=0)
    dm1 = jnp.concatenate([per_dev[:, 3:6].reshape(N_DEV, -1), jnp.zeros((8, 3 * D_MODEL), F32)], axis=0)
    d_mod = chip_cols(jnp.stack([dm0, dm1]), ada_cols)
    out["ada_w"] = _ada_update(cond_rows.T, d_mod, ada_w, m_ada_w, v_ada_w)

    part_c = _cond_ctx_partial(chip_cols(sums[6:9].reshape(1, -1), ada_cols), ada_w[0])
    part_all = _all_gather_small(jnp.pad(part_c.reshape(1, D_MODEL), ((0, 7), (0, 0))), "gather_cond_ctx")
    g_c, d_c, m_c, v_c = _cond_ctx_update(part_all, row(c_ctx), row(m_c_ctx), row(v_c_ctx))
    out["c_ctx"] = tuple(a.reshape(-1) for a in (g_c, d_c, m_c, v_c))
    out["final_norm_w"] = tuple(a.reshape(-1) for a in out["final_norm_w"])

    names = ["c_ctx", "ada_w", "ada_b", "norm_w", "hgrn_w_in", "hgrn_lb_logits", "hgrn_gnorm_w", "hgrn_w_out", "pool_w_in",
             "pool_w_grp", "pool_scale", "pool_w_out", "final_norm_w"]
    return (loss, loc["d_x"][None], *[out[k][0] for k in names], *[out[k][1] for k in names], *[out[k][2] for k in names],
            *[out[k][3] for k in names])
```

```python
import functools

import numpy as np
import jax
import jax.numpy as jnp
from jax import lax
from jax.experimental import pallas as pl
from jax.experimental.pallas import tpu as pltpu

F32 = jnp.float32
BF16 = jnp.bfloat16

D_MODEL = 1024
SEQ = 2048
CTX_LEN = 256
ROWS_ALL = CTX_LEN + SEQ
HEADS = 8
HEAD_DIM = 128
CHUNK = 64
N_CTX_CHUNKS = CTX_LEN // CHUNK
N_LAT_CHUNKS = SEQ // CHUNK
N_CHUNKS = N_CTX_CHUNKS + N_LAT_CHUNKS
GRID_W = 64
POOL_WINDOWS = (2, 4, 8, 16)
POOL_GROUPS = 4
POOL_GROUP_DIM = 256
HGRN_SECTIONS = 5
POOL_SECTIONS = 2
EPS = 1e-6
N_DEV = 8
N_CHIPS = 4
ROW_TILE = 256
SMALL_ROWS = 24

ADAM_LR = 0.001
ADAM_B1 = 0.9
ADAM_B2 = 0.999
ADAM_EPS = 1e-08
ADAM_WD = 0.01
ADAM_STEP = 10

MESH = pl.DeviceIdType.MESH
MIB = 1 << 20
ANY = pl.BlockSpec(memory_space=pl.ANY)
VMEM = pl.BlockSpec(memory_space=pltpu.VMEM)


def _params(vmem_mib=None):
    if vmem_mib is None:
        return pltpu.CompilerParams()
    return pltpu.CompilerParams(vmem_limit_bytes=vmem_mib * MIB)


def _sig(a):
    return 1.0 / (1.0 + jnp.exp(-a))


def _silu(a):
    return a * _sig(a)


def _dsilu(a):
    s = _sig(a)
    return s * (1.0 + a * (1.0 - s))


def _mm(a, b):
    return jnp.dot(a.astype(BF16), b.astype(BF16), preferred_element_type=F32)


def _mm_nt(a, b):
    return lax.dot_general(a.astype(BF16), b.astype(BF16), (((1,), (1,)), ((), ())), preferred_element_type=F32)


def _mm_tn(a, b):
    return lax.dot_general(a.astype(BF16), b.astype(BF16), (((0,), (0,)), ((), ())), preferred_element_type=F32)


def _split2(a):
    hi = a.astype(BF16)
    lo = (a - hi.astype(F32)).astype(BF16)
    return hi, lo


def _mm_exact_lhs(m_bf, a):
    hi, lo = _split2(a)
    return jnp.dot(m_bf, hi, preferred_element_type=F32) + jnp.dot(m_bf, lo, preferred_element_type=F32)


def _mm_f32(a, b):
    ah, al = _split2(a)
    bh, bl = _split2(b)
    return (jnp.dot(ah, bh, preferred_element_type=F32) + jnp.dot(al, bh, preferred_element_type=F32)
            + jnp.dot(ah, bl, preferred_element_type=F32))


def _my_place():
    return lax.axis_index("x"), lax.axis_index("y"), lax.axis_index("c")


def _all_gather_small(blk, name):
    m_per, n = blk.shape

    def body(x_ref, out_ref, send_sems, recv_sems, local_sem):
        x, y, c = _my_place()
        me, sibling = (x, y, c), (x, y, 1 - c)
        chips = [(1 - x, y), (x, 1 - y), (1 - x, 1 - y)]

        def rows(px, py, pc):
            return out_ref.at[pl.ds((4 * px + 2 * py + pc) * m_per, m_per), :]

        def copy(k, block, to, src=None):
            return pltpu.make_async_remote_copy(
                src_ref=rows(*block) if src is None else src, dst_ref=rows(*block),
                send_sem=send_sems.at[k], recv_sem=recv_sems.at[k], device_id=to, device_id_type=MESH)

        mine = pltpu.make_async_copy(x_ref, rows(*me), local_sem)
        mine.start()
        first = [copy(0, me, sibling, src=x_ref)]
        first += [copy(1 + j, me, (*chip, c), src=x_ref) for j, chip in enumerate(chips)]
        for cp in first:
            cp.start()
        passed = [copy(4 + j, (*chip, c), sibling) for j, chip in enumerate(chips)]
        for j, chip in enumerate(chips):
            copy(1 + j, (*chip, c), me).wait_recv()
            passed[j].start()
        copy(0, sibling, me).wait_recv()
        for j, chip in enumerate(chips):
            copy(4 + j, (*chip, 1 - c), me).wait_recv()
        for cp in first + passed:
            cp.wait_send()
        mine.wait()

    return pl.pallas_call(
        body, name=name,
        out_shape=jax.ShapeDtypeStruct((N_DEV * m_per, n), blk.dtype),
        in_specs=[VMEM], out_specs=VMEM,
        scratch_shapes=[pltpu.SemaphoreType.DMA((7,)), pltpu.SemaphoreType.DMA((7,)), pltpu.SemaphoreType.DMA],
    )(blk)


W_SPECS = {
    "hgrn_w_in": ((D_MODEL, 5 * D_MODEL), (1, 1280, 0, 512)),
    "hgrn_w_out": ((D_MODEL, D_MODEL), (0, 256, 0, 128)),
    "pool_w_in": ((D_MODEL, 2 * D_MODEL), (1, 512, 0, 512)),
    "pool_w_grp": ((POOL_GROUPS, POOL_GROUP_DIM, POOL_GROUP_DIM), (1, 64, 1, 32)),
    "pool_w_out": ((D_MODEL, D_MODEL), (0, 256, 0, 128)),
}
W_NAMES = tuple(W_SPECS)


def _al(v, m):
    return pl.multiple_of(v, m)


def _region(ref, spec, chip, half):
    ca, cn, ha, hn = spec
    idx = [slice(None)] * len(ref.shape)
    if ca == ha:
        if half is None:
            idx[ca] = pl.ds(_al(chip * cn, cn), cn)
        else:
            idx[ca] = pl.ds(_al(chip * cn + half * hn, hn), hn)
    else:
        idx[ca] = pl.ds(_al(chip * cn, cn), cn)
        if half is not None:
            idx[ha] = pl.ds(_al(half * hn, hn), hn)
    return ref.at[tuple(idx)]


def _half_of(ref, spec, half):
    _, _, ha, hn = spec
    idx = [slice(None)] * len(ref.shape)
    idx[ha] = pl.ds(_al(half * hn, hn), hn)
    return ref.at[tuple(idx)]


PIECE_BYTES = 256 * 1024


def _pieces(ref):
    lead = ref.shape[0]
    want = (int(np.prod(ref.shape)) * ref.dtype.itemsize) // PIECE_BYTES
    n = max([1] + [k for k in range(1, want + 1) if lead % k == 0 and (lead // k) % 16 == 0])
    rows = lead // n
    return [ref.at[pl.ds(i * rows, rows)] for i in range(n)]


def _half_shape(name):
    full, (ca, cn, ha, hn) = W_SPECS[name]
    shp = list(full)
    shp[ca] = cn
    shp[ha] = hn
    return tuple(shp)


def _all_gather_weights(shards):
    n = len(W_NAMES)
    specs = [W_SPECS[k][1] for k in W_NAMES]

    def body(*refs):
        sh, full = refs[:n], refs[n:2 * n]
        send_sems, recv_sems, local_sems = refs[2 * n:]
        x, y, c = _my_place()
        chip_me = 2 * x + y
        sibling = (x, y, 1 - c)
        chips = [(1 - x, y), (x, 1 - y), (1 - x, 1 - y)]

        def remote(a, k, src, dst, to):
            return pltpu.make_async_remote_copy(src_ref=src, dst_ref=dst, send_sem=send_sems.at[6 * a + k],
                                                recv_sem=recv_sems.at[6 * a + k], device_id=to, device_id_type=MESH)

        local = [pltpu.make_async_copy(sh[a], _region(full[a], specs[a], chip_me, None), local_sems.at[a]) for a in range(n)]
        for cp in local:
            cp.start()
        sends = []
        for a in range(n):
            for j, (px, py) in enumerate(chips):
                src, dst = _half_of(sh[a], specs[a], c), _region(full[a], specs[a], chip_me, c)
                for s_piece, d_piece in zip(_pieces(src), _pieces(dst)):
                    remote(a, j, s_piece, d_piece, (px, py, c)).start()
                sends.append(remote(a, j, src, dst, (px, py, c)))
        for a in range(n):
            for j, (px, py) in enumerate(chips):
                landed = _region(full[a], specs[a], 2 * px + py, c)
                remote(a, j, landed, landed, (px, py, c)).wait_recv()
                for piece in _pieces(landed):
                    remote(a, 3 + j, piece, piece, sibling).start()
                sends.append(remote(a, 3 + j, landed, landed, sibling))
        for a in range(n):
            for j, (px, py) in enumerate(chips):
                other = _region(full[a], specs[a], 2 * px + py, 1 - c)
                remote(a, 3 + j, other, other, sibling).wait_recv()
        for cp in sends:
            cp.wait_send()
        for cp in local:
            cp.wait()

    outs = pl.pallas_call(
        body, name="all_gather_weights",
        out_shape=[jax.ShapeDtypeStruct(W_SPECS[k][0], BF16) for k in W_NAMES],
        in_specs=[ANY] * n, out_specs=[ANY] * n,
        scratch_shapes=[pltpu.SemaphoreType.DMA((6 * n,)), pltpu.SemaphoreType.DMA((6 * n,)), pltpu.SemaphoreType.DMA((n,))],
    )(*shards)
    return dict(zip(W_NAMES, outs))


def _scatter_grads(parts):
    n = len(W_NAMES)
    specs = [W_SPECS[k][1] for k in W_NAMES]

    def body(*refs):
        part, slots = refs[:n], refs[n:2 * n]
        send_sems, recv_sems, local_sems = refs[2 * n:]
        x, y, c = _my_place()
        dev_me = 4 * x + 2 * y + c
        local = [pltpu.make_async_copy(_region(part[a], specs[a], 2 * x + y, c), slots[a].at[dev_me], local_sems.at[a])
                 for a in range(n)]
        for cp in local:
            cp.start()
        for a in range(n):
            for flip in range(1, N_DEV):
                fx, fy, fc = flip >> 2, (flip >> 1) & 1, flip & 1
                tx = 1 - x if fx else x
                ty = 1 - y if fy else y
                tc = 1 - c if fc else c
                for src, dst in zip(_pieces(_region(part[a], specs[a], 2 * tx + ty, tc)), _pieces(slots[a].at[dev_me])):
                    pltpu.make_async_remote_copy(src_ref=src, dst_ref=dst, send_sem=send_sems.at[a], recv_sem=recv_sems.at[a],
                                                 device_id=(tx, ty, tc), device_id_type=MESH).start()
        for a in range(n):
            seven = slots[a].at[pl.ds(0, N_DEV - 1)]
            pltpu.make_async_remote_copy(src_ref=seven, dst_ref=seven, send_sem=send_sems.at[a], recv_sem=recv_sems.at[a],
                                         device_id=(x, y, c), device_id_type=MESH).wait()
        for cp in local:
            cp.wait()

    return pl.pallas_call(
        body, name="scatter_grads",
        out_shape=[jax.ShapeDtypeStruct((N_DEV,) + _half_shape(k), BF16) for k in W_NAMES],
        in_specs=[ANY] * n, out_specs=[ANY] * n,
        scratch_shapes=[pltpu.SemaphoreType.DMA((n,)), pltpu.SemaphoreType.DMA((n,)), pltpu.SemaphoreType.DMA((n,))],
    )(*parts)


def _exchange_halves(halves):
    n = len(W_NAMES)
    specs = [W_SPECS[k][1] for k in W_NAMES]

    def shard_shape(k):
        shp = list(_half_shape(k))
        shp[W_SPECS[k][1][2]] *= 2
        return tuple(shp)

    def body(*refs):
        half, out = refs[:n], refs[n:2 * n]
        send_sems, recv_sems, local_sems = refs[2 * n:]
        x, y, c = _my_place()
        sibling = (x, y, 1 - c)

        def remote(a, src, dst):
            return pltpu.make_async_remote_copy(src_ref=src, dst_ref=dst, send_sem=send_sems.at[a], recv_sem=recv_sems.at[a],
                                                device_id=sibling, device_id_type=MESH)

        local = [pltpu.make_async_copy(half[a], _half_of(out[a], specs[a], c), local_sems.at[a]) for a in range(n)]
        for cp in local:
            cp.start()
        for a in range(n):
            mine = _half_of(out[a], specs[a], c)
            for src, dst in zip(_pieces(half[a]), _pieces(mine)):
                remote(a, src, dst).start()
        for a in range(n):
            theirs = _half_of(out[a], specs[a], 1 - c)
            remote(a, theirs, theirs).wait_recv()
        for a in range(n):
            remote(a, half[a], half[a]).wait_send()
        for cp in local:
            cp.wait()

    return pl.pallas_call(
        body, name="exchange_halves",
        out_shape=[jax.ShapeDtypeStruct(shard_shape(k), F32) for k in W_NAMES],
        in_specs=[ANY] * n, out_specs=[ANY] * n,
        scratch_shapes=[pltpu.SemaphoreType.DMA((n,)), pltpu.SemaphoreType.DMA((n,)), pltpu.SemaphoreType.DMA((n,))],
    )(*halves)


def _mod_parts(c_rows, ada_w, ada_b_cols):
    n_layers, _, n_cols = ada_w.shape

    def body(c_ref, w_ref, b_ref, o_ref):
        o_ref[...] = _mm_f32(_silu(c_ref[...]), w_ref[...]) + b_ref[...]

    return pl.pallas_call(
        body, name="mod_parts", grid=(n_layers,),
        out_shape=jax.ShapeDtypeStruct((n_layers, 16, n_cols), F32),
        in_specs=[pl.BlockSpec((16, D_MODEL), lambda i: (0, 0)),
                  pl.BlockSpec((None, D_MODEL, n_cols), lambda i: (i, 0, 0)),
                  pl.BlockSpec((None, 1, n_cols), lambda i: (i, 0, 0))],
        out_specs=pl.BlockSpec((None, 16, n_cols), lambda i: (i, 0, 0)),
        compiler_params=_params(40),
    )(c_rows, ada_w, ada_b_cols.reshape(n_layers, 1, n_cols))


def _ln_mod_matmul(xin, nw, shift, scale, w, name):
    rows = xin.shape[0]
    n_sec = w.shape[1] // D_MODEL
    n_mod = shift.shape[0]
    tm = ROW_TILE

    def body(x_ref, nw_ref, sh_ref, sc_ref, w_ref, g_ref, h_ref, h_scr):
        @pl.when(pl.program_id(1) == 0)
        def _():
            xv = x_ref[...]
            r = lax.rsqrt(jnp.mean(xv * xv, axis=-1, keepdims=True) + EPS)
            h = ((xv * r * nw_ref[...]) * (1.0 + sc_ref[...]) + sh_ref[...]).astype(BF16)
            h_scr[...] = h
            h_ref[...] = h

        g_ref[...] = jnp.dot(h_scr[...], w_ref[...], preferred_element_type=F32)

    mod_spec = pl.BlockSpec((None, 1, D_MODEL), lambda i, j: (jnp.minimum(i, n_mod - 1), 0, 0))
    return pl.pallas_call(
        body, name=name, grid=(rows // tm, n_sec),
        out_shape=[jax.ShapeDtypeStruct((n_sec, rows, D_MODEL), F32), jax.ShapeDtypeStruct((rows, D_MODEL), BF16)],
        in_specs=[pl.BlockSpec((tm, D_MODEL), lambda i, j: (i, 0)),
                  pl.BlockSpec((1, D_MODEL), lambda i, j: (0, 0)),
                  mod_spec, mod_spec,
                  pl.BlockSpec((D_MODEL, D_MODEL), lambda i, j: (0, j))],
        out_specs=[pl.BlockSpec((None, tm, D_MODEL), lambda i, j: (j, i, 0)),
                   pl.BlockSpec((tm, D_MODEL), lambda i, j: (i, 0))],
        scratch_shapes=[pltpu.VMEM((tm, D_MODEL), BF16)],
        compiler_params=_params(32),
    )(xin, nw, shift.reshape(n_mod, 1, D_MODEL), scale.reshape(n_mod, 1, D_MODEL), w)


def _chunk_masks(rev):
    rid = lax.broadcasted_iota(jnp.int32, (CHUNK, CHUNK), 0)
    cid = lax.broadcasted_iota(jnp.int32, (CHUNK, CHUNK), 1)
    keep = (cid >= rid) if rev else (cid <= rid)
    keep_t = (cid <= rid) if rev else (cid >= rid)
    one, zero = jnp.ones((CHUNK, CHUNK), F32), jnp.zeros((CHUNK, CHUNK), F32)
    return keep, jnp.where(keep, one, zero).astype(BF16), jnp.where(keep_t, one, zero).astype(BF16)


def _chunk_rows(t, rev, latent):
    n = N_LAT_CHUNKS if latent else N_CTX_CHUNKS
    base = CTX_LEN if latent else 0
    idx = (n - 1 - t) if rev else t
    return pl.multiple_of(base + idx * CHUNK, CHUNK)


def _gates(fpre, lb):
    sg = _sig(fpre)
    f = lb + (1.0 - lb) * sg
    return sg, f, 1.0 - f, jnp.log(f)


G_SPEC = lambda sec: pl.BlockSpec((None, ROWS_ALL, HEAD_DIM), lambda h, sec=sec: (sec, 0, h))


def _gla_forward(g5, lb_logits):
    q_scale = HEAD_DIM ** -0.5

    def body(ff_ref, fb_ref, v_ref, q_ref, lg_ref, o_ref, st_ref, decay_ref, qt_ref):
        for rev in (False, True):
            f_ref = fb_ref if rev else ff_ref
            lb = _sig(lg_ref[1:2, :] if rev else lg_ref[0:1, :])
            keep, tri, _ = _chunk_masks(rev)
            last = 0 if rev else CHUNK - 1
            mid = CHUNK // 2 if rev else CHUNK // 2 - 1

            def local_step(t, carry, latent):
                r0 = _chunk_rows(t, rev, latent)
                rows = pl.ds(r0, CHUNK)
                step = t + (N_CTX_CHUNKS if latent else 0)
                _, _, k, lf = _gates(f_ref[rows, :], lb)
                v = v_ref[rows, :]
                b = _mm_exact_lhs(tri, lf)
                bl = b[last:last + 1, :]
                if latent:
                    q = _silu(q_ref[rows, :]) * q_scale
                    bm = b[mid:mid + 1, :]
                    a = _mm_nt(q * jnp.exp(b - bm), k * jnp.exp(bm - b))
                    o = _mm(jnp.where(keep, a, 0.0), v)
                    orow = pl.ds(pl.multiple_of(r0 - CTX_LEN, CHUNK), CHUNK)
                    qt_ref[orow, :] = (q * jnp.exp(b)).astype(BF16)
                    if rev:
                        o_ref[orow, :] += o
                    else:
                        o_ref[orow, :] = o
                decay_ref[step] = jnp.exp(bl)
                st_ref[step] = _mm_tn(v, k * jnp.exp(bl - b))
                return carry

            lax.fori_loop(0, N_CTX_CHUNKS, functools.partial(local_step, latent=False), 0, unroll=2)
            lax.fori_loop(0, N_LAT_CHUNKS, functools.partial(local_step, latent=True), 0, unroll=4)

            def scan_step(t, st):
                update = st_ref[t]
                st_ref[t] = st
                return st * decay_ref[t] + update

            lax.fori_loop(0, N_CHUNKS, scan_step, jnp.zeros((HEAD_DIM, HEAD_DIM), F32), unroll=2)

            def inter_step(t, carry):
                r0 = _chunk_rows(t, rev, True)
                orow = pl.ds(pl.multiple_of(r0 - CTX_LEN, CHUNK), CHUNK)
                o_ref[orow, :] += lax.dot_general(qt_ref[orow, :], st_ref[t + N_CTX_CHUNKS].astype(BF16),
                                                  (((1,), (1,)), ((), ())), preferred_element_type=F32)
                return carry

            lax.fori_loop(0, N_LAT_CHUNKS, inter_step, 0, unroll=4)

    return pl.pallas_call(
        body, name="gla_forward", grid=(HEADS,),
        out_shape=jax.ShapeDtypeStruct((SEQ, D_MODEL), F32),
        in_specs=[G_SPEC(0), G_SPEC(1), G_SPEC(2), G_SPEC(3), pl.BlockSpec((2, HEAD_DIM), lambda h: (0, h))],
        out_specs=pl.BlockSpec((SEQ, HEAD_DIM), lambda h: (0, h)),
        scratch_shapes=[pltpu.VMEM((N_CHUNKS, HEAD_DIM, HEAD_DIM), F32), pltpu.VMEM((N_CHUNKS, 1, HEAD_DIM), F32),
                        pltpu.VMEM((SEQ, HEAD_DIM), BF16)],
        compiler_params=_params(32),
    )(g5, g5, g5, g5, lb_logits)


def _gla_backward(g5, lb_logits, d_o, d_z):
    q_scale = HEAD_DIM ** -0.5

    def body(ff_ref, fb_ref, v_ref, q_ref, lg_ref, do_ref, dz_ref, dg_ref, dlg_ref, st_ref, dst_ref, decay_ref):
        dg_ref[3, 0:CTX_LEN, :] = jnp.zeros((CTX_LEN, HEAD_DIM), F32)
        dg_ref[4, 0:CTX_LEN, :] = jnp.zeros((CTX_LEN, HEAD_DIM), F32)
        dg_ref[4, CTX_LEN:ROWS_ALL, :] = dz_ref[...]
        is_row = lax.broadcasted_iota(jnp.int32, (CHUNK, HEAD_DIM), 0)
        for rev in (False, True):
            d = 1 if rev else 0
            f_ref = fb_ref if rev else ff_ref
            lb = _sig(lg_ref[d:d + 1, :])
            keep, tri, tri_t = _chunk_masks(rev)
            last = 0 if rev else CHUNK - 1
            mid = CHUNK // 2 if rev else CHUNK // 2 - 1

            def local_step(t, carry, latent):
                r0 = _chunk_rows(t, rev, latent)
                rows = pl.ds(r0, CHUNK)
                step = t + (N_CTX_CHUNKS if latent else 0)
                _, _, k, lf = _gates(f_ref[rows, :], lb)
                b = _mm_exact_lhs(tri, lf)
                bl = b[last:last + 1, :]
                decay_ref[step] = jnp.exp(bl)
                st_ref[step] = _mm_tn(v_ref[rows, :], k * jnp.exp(bl - b))
                if latent:
                    q_t = _silu(q_ref[rows, :]) * q_scale * jnp.exp(b)
                    dst_ref[step] = _mm_tn(do_ref[pl.ds(pl.multiple_of(r0 - CTX_LEN, CHUNK), CHUNK), :], q_t)
                else:
                    dst_ref[step] = jnp.zeros((HEAD_DIM, HEAD_DIM), F32)
                return carry

            lax.fori_loop(0, N_CTX_CHUNKS, functools.partial(local_step, latent=False), 0, unroll=2)
            lax.fori_loop(0, N_LAT_CHUNKS, functools.partial(local_step, latent=True), 0, unroll=4)

            def scan_step(i, carry):
                st, d_st = carry
                j = N_CHUNKS - 1 - i
                update, d_update = st_ref[i], dst_ref[j]
                st_ref[i] = st
                dst_ref[j] = d_st
                return st * decay_ref[i] + update, d_st * decay_ref[j] + d_update

            zero_state = jnp.zeros((HEAD_DIM, HEAD_DIM), F32)
            lax.fori_loop(0, N_CHUNKS, scan_step, (zero_state, zero_state))

            def grad_step(t, dlb, latent):
                r0 = _chunk_rows(t, rev, latent)
                rows = pl.ds(r0, CHUNK)
                step = t + (N_CTX_CHUNKS if latent else 0)
                sg, f, k, lf = _gates(f_ref[rows, :], lb)
                v = v_ref[rows, :]
                b = _mm_exact_lhs(tri, lf)
                bl = b[last:last + 1, :]
                e_end = jnp.exp(bl - b)
                k_end = k * e_end
                decay = jnp.exp(bl)
                d_st = dst_ref[step]
                st_prev = st_ref[step]
                d_kend = _mm(v, d_st)
                d_decay = jnp.sum(d_st * st_prev, axis=0, keepdims=True)
                t_kend = d_kend * k_end
                d_v = _mm_nt(k_end, d_st)
                d_k = d_kend * e_end
                d_b = -t_kend
                if latent:
                    qpre = q_ref[rows, :]
                    q = _silu(qpre) * q_scale
                    bm = b[mid:mid + 1, :]
                    e_b, e_qm, e_km = jnp.exp(b), jnp.exp(b - bm), jnp.exp(bm - b)
                    q_t, q_m, k_m = q * e_b, q * e_qm, k * e_km
                    a = jnp.where(keep, _mm_nt(q_m, k_m), 0.0)
                    d_out = do_ref[pl.ds(pl.multiple_of(r0 - CTX_LEN, CHUNK), CHUNK), :]
                    d_a = jnp.where(keep, _mm_nt(d_out, v), 0.0)
                    d_qm = _mm(d_a, k_m)
                    d_km = _mm_tn(d_a, q_m)
                    d_qt = _mm(d_out, st_prev)
                    d_v = d_v + _mm_tn(a, d_out)
                    d_k = d_k + d_km * e_km
                    d_b = d_b + d_qt * q_t + d_qm * q_m - d_km * k_m
                    d_q = d_qt * e_b + d_qm * e_qm
                at_last = jnp.sum(t_kend, axis=0, keepdims=True) + d_decay * decay
                d_b = d_b + jnp.where(is_row == last, at_last, 0.0)
                d_lf = _mm_exact_lhs(tri_t, d_b)
                d_f = d_lf / f - d_k
                dg_ref[d, rows, :] = d_f * (1.0 - lb) * sg * (1.0 - sg)
                if rev:
                    dg_ref[2, rows, :] += d_v
                else:
                    dg_ref[2, rows, :] = d_v
                if latent:
                    d_qpre = d_q * q_scale * _dsilu(qpre)
                    if rev:
                        dg_ref[3, rows, :] += d_qpre
                    else:
                        dg_ref[3, rows, :] = d_qpre
                return dlb + jnp.sum(d_f * (1.0 - sg), axis=0, keepdims=True)

            dlb = lax.fori_loop(0, N_LAT_CHUNKS, functools.partial(grad_step, latent=True), jnp.zeros((1, HEAD_DIM), F32),
                                unroll=2)
            dlb = lax.fori_loop(0, N_CTX_CHUNKS, functools.partial(grad_step, latent=False), dlb, unroll=2)
            dlg_ref[d:d + 1, :] = dlb * lb * (1.0 - lb)

    col = pl.BlockSpec((SEQ, HEAD_DIM), lambda h: (0, h))
    return pl.pallas_call(
        body, name="gla_backward", grid=(HEADS,),
        out_shape=[jax.ShapeDtypeStruct((HGRN_SECTIONS, ROWS_ALL, D_MODEL), F32), jax.ShapeDtypeStruct((2, D_MODEL), F32)],
        in_specs=[G_SPEC(0), G_SPEC(1), G_SPEC(2), G_SPEC(3), pl.BlockSpec((2, HEAD_DIM), lambda h: (0, h)), col, col],
        out_specs=[pl.BlockSpec((HGRN_SECTIONS, ROWS_ALL, HEAD_DIM), lambda h: (0, 0, h)),
                   pl.BlockSpec((2, HEAD_DIM), lambda h: (0, h))],
        scratch_shapes=[pltpu.VMEM((N_CHUNKS, HEAD_DIM, HEAD_DIM), F32), pltpu.VMEM((N_CHUNKS, HEAD_DIM, HEAD_DIM), F32),
                        pltpu.VMEM((N_CHUNKS, 1, HEAD_DIM), F32)],
        compiler_params=_params(48),
    )(g5, g5, g5, g5, lb_logits, d_o, d_z)


def _head_norm(o, gw, scr):
    rs = []
    for h in range(HEADS):
        cols = slice(h * HEAD_DIM, (h + 1) * HEAD_DIM)
        oh = o[:, cols]
        r = lax.rsqrt(jnp.mean(oh * oh, axis=-1, keepdims=True) + EPS)
        scr[:, cols] = oh * r
        rs.append(r)
    return rs


def _hgrn_out_forward(o_raw, g5, xin, gnorm_w, gate, w_out):
    tm = ROW_TILE

    def body(o_ref, z_ref, x_ref, gw_ref, gate_ref, w_ref, x1_ref, res_ref, scr):
        _head_norm(o_ref[...], None, scr)
        a = scr[...] * gw_ref[...] * _silu(z_ref[...])
        res = _mm(a, w_ref[...])
        res_ref[...] = res
        x1_ref[...] = x_ref[...] + gate_ref[...] * res

    tile = pl.BlockSpec((tm, D_MODEL), lambda i: (i, 0))
    vec = pl.BlockSpec((1, D_MODEL), lambda i: (0, 0))
    return pl.pallas_call(
        body, name="hgrn_out_forward", grid=(SEQ // tm,),
        out_shape=[jax.ShapeDtypeStruct((SEQ, D_MODEL), F32)] * 2,
        in_specs=[tile, pl.BlockSpec((None, tm, D_MODEL), lambda i: (4, i + CTX_LEN // tm, 0)), tile, vec, vec,
                  pl.BlockSpec((D_MODEL, D_MODEL), lambda i: (0, 0))],
        out_specs=[tile, tile],
        scratch_shapes=[pltpu.VMEM((tm, D_MODEL), F32)],
        compiler_params=_params(32),
    )(o_raw, g5, xin, gnorm_w, gate, w_out)


def _hgrn_out_backward(d_x1, o_raw, g5, res, gnorm_w, gate, w_out):
    tm = ROW_TILE

    def body(dx_ref, o_ref, z_ref, res_ref, gw_ref, gate_ref, w_ref, do_ref, dz_ref, dw_ref, dgate_ref, dgw_ref, scr, scr2):
        @pl.when(pl.program_id(0) == 0)
        def _():
            dw_ref[...] = jnp.zeros_like(dw_ref)
            dgate_ref[...] = jnp.zeros_like(dgate_ref)
            dgw_ref[...] = jnp.zeros_like(dgw_ref)

        dx = dx_ref[...]
        dgate_ref[...] += jnp.sum(dx * res_ref[...], axis=0, keepdims=True)
        d_res = (dx * gate_ref[...]).astype(BF16)
        d_a = _mm_nt(d_res, w_ref[...])
        rs = _head_norm(o_ref[...], None, scr)
        z = z_ref[...]
        sz = _silu(z)
        o_hat = scr[...]
        o_n = o_hat * gw_ref[...]
        dw_ref[...] += _mm_tn(o_n * sz, d_res)
        d_on = d_a * sz
        dz_ref[...] = d_a * o_n * _dsilu(z)
        dgw_ref[...] += jnp.sum(d_on * o_hat, axis=0, keepdims=True)
        scr2[...] = d_on * gw_ref[...]
        for h in range(HEADS):
            cols = slice(h * HEAD_DIM, (h + 1) * HEAD_DIM)
            dh, oh = scr2[:, cols], scr[:, cols]
            do_ref[:, cols] = rs[h] * (dh - oh * jnp.mean(dh * oh, axis=-1, keepdims=True))

    tile = pl.BlockSpec((tm, D_MODEL), lambda i: (i, 0))
    vec = pl.BlockSpec((1, D_MODEL), lambda i: (0, 0))
    mat = pl.BlockSpec((D_MODEL, D_MODEL), lambda i: (0, 0))
    return pl.pallas_call(
        body, name="hgrn_out_backward", grid=(SEQ // tm,),
        out_shape=[jax.ShapeDtypeStruct((SEQ, D_MODEL), F32)] * 2 + [jax.ShapeDtypeStruct((D_MODEL, D_MODEL), F32)]
        + [jax.ShapeDtypeStruct((1, D_MODEL), F32)] * 2,
        in_specs=[tile, tile, pl.BlockSpec((None, tm, D_MODEL), lambda i: (4, i + CTX_LEN // tm, 0)), tile, vec, vec, mat],
        out_specs=[tile, tile, mat, vec, vec],
        scratch_shapes=[pltpu.VMEM((tm, D_MODEL), F32)] * 2,
        compiler_params=_params(40),
    )(d_x1, o_raw, g5, res, gnorm_w, gate, w_out)


def _pool_constants():
    win = np.zeros((POOL_GROUPS, ROW_TILE, ROW_TILE), np.float32)
    inv = np.zeros((POOL_GROUPS, ROW_TILE, 1), np.float32)
    for g, w in enumerate(POOL_WINDOWS):
        for t in range(ROW_TILE):
            base, p = (t // GRID_W) * GRID_W, t % GRID_W
            lo = min(max(p - w // 2, 0), GRID_W)
            hi = min(max(p - w // 2 + w, 0), GRID_W)
            win[g, t, base + lo:base + hi] = 1.0
            inv[g, t, 0] = 1.0 / np.float32(hi - lo)
    return jnp.asarray(win, BF16), jnp.asarray(win.transpose(0, 2, 1), BF16), jnp.asarray(inv, F32)


def _pool_mix(u_ref, wg_ref, ps_ref, win_ref, inv_ref, pooled_scr, yg_scr):
    for g in range(POOL_GROUPS):
        cols = slice(g * POOL_GROUP_DIM, (g + 1) * POOL_GROUP_DIM)
        ug = u_ref[:, cols]
        pooled = _mm_exact_lhs(win_ref[g], ug) * inv_ref[g] - ug
        if pooled_scr is not None:
            pooled_scr[:, cols] = pooled
        yg_scr[:, cols] = _mm(pooled, wg_ref[g])


def _pool_forward_loss(uz, x1, target, gate, w_grp, pool_scale, w_out, final_w):
    tm = ROW_TILE
    win, _, inv = _pool_constants()

    def body(u_ref, z_ref, x_ref, t_ref, gate_ref, wg_ref, ps_ref, w_ref, fw_ref, win_ref, inv_ref,
             dx_ref, loss_ref, dfw_ref, dgate_ref, yg_scr):
        @pl.when(pl.program_id(0) == 0)
        def _():
            loss_ref[...] = jnp.zeros_like(loss_ref)
            dfw_ref[...] = jnp.zeros_like(dfw_ref)
            dgate_ref[...] = jnp.zeros_like(dgate_ref)

        _pool_mix(u_ref, wg_ref, ps_ref, win_ref, inv_ref, None, yg_scr)
        a = yg_scr[...] * ps_ref[...] * _silu(z_ref[...])
        res = _mm(a, w_ref[...])
        x2 = x_ref[...] + gate_ref[...] * res
        r = lax.rsqrt(jnp.mean(x2 * x2, axis=-1, keepdims=True) + EPS)
        xh = x2 * r
        fw = fw_ref[...]
        err = xh * fw - t_ref[...]
        loss_ref[...] += 0.5 * jnp.sum(jnp.mean(err * err, axis=-1, keepdims=True))
        d_y = err * (1.0 / D_MODEL)
        dfw_ref[...] += jnp.sum(d_y * xh, axis=0, keepdims=True)
        d_xh = d_y * fw
        d_x2 = r * (d_xh - xh * jnp.mean(d_xh * xh, axis=-1, keepdims=True))
        dx_ref[...] = d_x2
        dgate_ref[...] += jnp.sum(d_x2 * res, axis=0, keepdims=True)

    tile = pl.BlockSpec((tm, D_MODEL), lambda i: (i, 0))
    vec = pl.BlockSpec((1, D_MODEL), lambda i: (0, 0))
    grp = pl.BlockSpec((POOL_GROUPS, POOL_GROUP_DIM, POOL_GROUP_DIM), lambda i: (0, 0, 0))
    return pl.pallas_call(
        body, name="pool_forward_loss", grid=(SEQ // tm,),
        out_shape=[jax.ShapeDtypeStruct((SEQ, D_MODEL), F32), jax.ShapeDtypeStruct((8, 128), F32),
                   jax.ShapeDtypeStruct((1, D_MODEL), F32), jax.ShapeDtypeStruct((1, D_MODEL), F32)],
        in_specs=[pl.BlockSpec((None, tm, D_MODEL), lambda i: (0, i, 0)), pl.BlockSpec((None, tm, D_MODEL), lambda i: (1, i, 0)),
                  tile, tile, vec, grp, vec, pl.BlockSpec((D_MODEL, D_MODEL), lambda i: (0, 0)), vec, grp,
                  pl.BlockSpec((POOL_GROUPS, ROW_TILE, 1), lambda i: (0, 0, 0))],
        out_specs=[tile, pl.BlockSpec((8, 128), lambda i: (0, 0)), vec, vec],
        scratch_shapes=[pltpu.VMEM((tm, D_MODEL), F32)],
        compiler_params=_params(32),
    )(uz, uz, x1, target, gate, w_grp, pool_scale, w_out, final_w, win, inv)


def _pool_backward(d_x2, uz, gate, w_grp, pool_scale, w_out):
    tm = ROW_TILE
    win, win_t, inv = _pool_constants()

    def body(dx_ref, u_ref, z_ref, gate_ref, wg_ref, ps_ref, w_ref, win_ref, wint_ref, inv_ref,
             duz_ref, dw_ref, dwg_ref, dps_ref, pooled_scr, yg_scr, dyg_scr):
        @pl.when(pl.program_id(0) == 0)
        def _():
            dw_ref[...] = jnp.zeros_like(dw_ref)
            dwg_ref[...] = jnp.zeros_like(dwg_ref)
            dps_ref[...] = jnp.zeros_like(dps_ref)

        _pool_mix(u_ref, wg_ref, ps_ref, win_ref, inv_ref, pooled_scr, yg_scr)
        z = z_ref[...]
        sz = _silu(z)
        yg = yg_scr[...]
        y = yg * ps_ref[...]
        d_res = (dx_ref[...] * gate_ref[...]).astype(BF16)
        d_a = _mm_nt(d_res, w_ref[...])
        dw_ref[...] += _mm_tn(y * sz, d_res)
        d_y = d_a * sz
        duz_ref[1] = d_a * y * _dsilu(z)
        dps_ref[...] += jnp.sum(d_y * yg, axis=0, keepdims=True)
        dyg_scr[...] = d_y * ps_ref[...]
        for g in range(POOL_GROUPS):
            cols = slice(g * POOL_GROUP_DIM, (g + 1) * POOL_GROUP_DIM)
            d_yg = dyg_scr[:, cols].astype(BF16)
            d_pool = _mm_nt(d_yg, wg_ref[g])
            dwg_ref[g] += _mm_tn(pooled_scr[:, cols], d_yg)
            duz_ref[0, :, cols] = _mm_exact_lhs(wint_ref[g], d_pool * inv_ref[g]) - d_pool

    tile = pl.BlockSpec((tm, D_MODEL), lambda i: (i, 0))
    vec = pl.BlockSpec((1, D_MODEL), lambda i: (0, 0))
    mat = pl.BlockSpec((D_MODEL, D_MODEL), lambda i: (0, 0))
    grp = pl.BlockSpec((POOL_GROUPS, POOL_GROUP_DIM, POOL_GROUP_DIM), lambda i: (0, 0, 0))
    return pl.pallas_call(
        body, name="pool_backward", grid=(SEQ // tm,),
        out_shape=[jax.ShapeDtypeStruct((POOL_SECTIONS, SEQ, D_MODEL), F32), jax.ShapeDtypeStruct((D_MODEL, D_MODEL), F32),
                   jax.ShapeDtypeStruct((POOL_GROUPS, POOL_GROUP_DIM, POOL_GROUP_DIM), F32), jax.ShapeDtypeStruct((1, D_MODEL), F32)],
        in_specs=[tile, pl.BlockSpec((None, tm, D_MODEL), lambda i: (0, i, 0)), pl.BlockSpec((None, tm, D_MODEL), lambda i: (1, i, 0)),
                  vec, grp, vec, mat, grp, grp, pl.BlockSpec((POOL_GROUPS, ROW_TILE, 1), lambda i: (0, 0, 0))],
        out_specs=[pl.BlockSpec((POOL_SECTIONS, tm, D_MODEL), lambda i: (0, i, 0)), mat, grp, vec],
        scratch_shapes=[pltpu.VMEM((tm, D_MODEL), F32)] * 3,
        compiler_params=_params(40),
    )(d_x2, uz, uz, gate, w_grp, pool_scale, w_out, win, win_t, inv)


def _ln_mod_backward(d_g, w, xin, nw, scale, d_up, name):
    n_sec, rows, _ = d_g.shape
    n_mod = scale.shape[0]
    tm = ROW_TILE
    skip = n_mod - 1

    def body(dg_ref, w_ref, x_ref, nw_ref, sc_ref, up_ref, dx_ref, dnw_ref, dmod_ref, acc):
        i, k = pl.program_id(0), pl.program_id(1)

        @pl.when(k == 0)
        def _():
            acc[...] = jnp.zeros_like(acc)

        acc[...] += _mm_nt(dg_ref[...], w_ref[...])

        @pl.when(k == n_sec - 1)
        def _():
            @pl.when(i == 0)
            def _():
                dnw_ref[...] = jnp.zeros_like(dnw_ref)

            @pl.when((i == 0) | (i == skip))
            def _():
                dmod_ref[...] = jnp.zeros_like(dmod_ref)

            d_h = acc[...]
            xv = x_ref[...]
            r = lax.rsqrt(jnp.mean(xv * xv, axis=-1, keepdims=True) + EPS)
            xh = xv * r
            nw_row = nw_ref[...]
            dmod_ref[0:1, :] += jnp.sum(d_h, axis=0, keepdims=True)
            dmod_ref[1:2, :] += jnp.sum(d_h * (xh * nw_row), axis=0, keepdims=True)
            d_xn = d_h * (1.0 + sc_ref[...])
            dnw_ref[...] += jnp.sum(d_xn * xh, axis=0, keepdims=True)
            d_xh = d_xn * nw_row

            @pl.when(i >= skip)
            def _():
                dx_ref[...] = up_ref[...] + r * (d_xh - xh * jnp.mean(d_xh * xh, axis=-1, keepdims=True))

    lat = lambda i, k: (jnp.maximum(i - skip, 0), 0)
    mod_idx = lambda i, k: (jnp.minimum(i, n_mod - 1), 0, 0)
    return pl.pallas_call(
        body, name=name, grid=(rows // tm, n_sec),
        out_shape=[jax.ShapeDtypeStruct((rows - skip * tm, D_MODEL), F32), jax.ShapeDtypeStruct((1, D_MODEL), F32),
                   jax.ShapeDtypeStruct((n_mod, 8, D_MODEL), F32)],
        in_specs=[pl.BlockSpec((None, tm, D_MODEL), lambda i, k: (k, i, 0)),
                  pl.BlockSpec((D_MODEL, D_MODEL), lambda i, k: (0, k)),
                  pl.BlockSpec((tm, D_MODEL), lambda i, k: (i, 0)),
                  pl.BlockSpec((1, D_MODEL), lambda i, k: (0, 0)),
                  pl.BlockSpec((None, 1, D_MODEL), mod_idx),
                  pl.BlockSpec((tm, D_MODEL), lat)],
        out_specs=[pl.BlockSpec((tm, D_MODEL), lat), pl.BlockSpec((1, D_MODEL), lambda i, k: (0, 0)),
                   pl.BlockSpec((None, 8, D_MODEL), mod_idx)],
        scratch_shapes=[pltpu.VMEM((tm, D_MODEL), F32)],
        compiler_params=_params(32),
    )(d_g, w, xin, nw, scale.reshape(n_mod, 1, D_MODEL), d_up)


def _weight_grad(h, d_g, name):
    n_sec, rows, _ = d_g.shape
    tm = 768 if rows % 768 == 0 else 512

    def body(h_ref, dg_ref, dw_ref):
        @pl.when(pl.program_id(1) == 0)
        def _():
            dw_ref[...] = jnp.zeros_like(dw_ref)

        dw_ref[...] += _mm_tn(h_ref[...], dg_ref[...])

    return pl.pallas_call(
        body, name=name, grid=(n_sec, rows // tm),
        out_shape=jax.ShapeDtypeStruct((D_MODEL, n_sec * D_MODEL), F32),
        in_specs=[pl.BlockSpec((tm, D_MODEL), lambda j, i: (i, 0)), pl.BlockSpec((None, tm, D_MODEL), lambda j, i: (j, i, 0))],
        out_specs=pl.BlockSpec((D_MODEL, D_MODEL), lambda j, i: (0, j)),
        compiler_params=_params(32),
    )(h, d_g)


def _sum_slots(slots, name):
    _, rows, cols = slots.shape
    tr = 128

    def body(s_ref, o_ref):
        acc = s_ref[0].astype(F32)
        for d in range(1, N_DEV):
            acc = acc + s_ref[d].astype(F32)
        o_ref[...] = acc

    return pl.pallas_call(
        body, name=name, grid=(rows // tr,),
        out_shape=jax.ShapeDtypeStruct((rows, cols), F32),
        in_specs=[pl.BlockSpec((N_DEV, tr, cols), lambda i: (0, i, 0))],
        out_specs=pl.BlockSpec((tr, cols), lambda i: (i, 0)),
    )(slots)


def _adamw_math(w, g, m, v):
    m = ADAM_B1 * m + (1.0 - ADAM_B1) * g
    v = ADAM_B2 * v + (1.0 - ADAM_B2) * (g * g)
    m_hat = m / (1.0 - ADAM_B1 ** ADAM_STEP)
    v_hat = v / (1.0 - ADAM_B2 ** ADAM_STEP)
    return -ADAM_LR * (m_hat / (jnp.sqrt(v_hat) + ADAM_EPS) + ADAM_WD * w), m, v


def _adamw(w, g, m, v, name):
    rows, cols = w.shape
    tr = rows if rows <= 128 else 128

    def body(w_ref, g_ref, m_ref, v_ref, d_ref, mo_ref, vo_ref):
        d_ref[...], mo_ref[...], vo_ref[...] = _adamw_math(w_ref[...], g_ref[...], m_ref[...], v_ref[...])

    tile = pl.BlockSpec((tr, cols), lambda i: (i, 0))
    return pl.pallas_call(
        body, name=name, grid=(rows // tr,),
        out_shape=[jax.ShapeDtypeStruct((rows, cols), F32)] * 3,
        in_specs=[tile] * 4, out_specs=[tile] * 3,
    )(w, g, m, v)


def _sum_devices(gathered):
    def body(p_ref, o_ref):
        acc = p_ref[0:SMALL_ROWS, :]
        for d in range(1, N_DEV):
            acc = acc + p_ref[d * SMALL_ROWS:(d + 1) * SMALL_ROWS, :]
        o_ref[...] = acc

    return pl.pallas_call(body, name="sum_devices", out_shape=jax.ShapeDtypeStruct((SMALL_ROWS, D_MODEL), F32),
                          in_specs=[VMEM], out_specs=VMEM)(gathered)


def _ada_update(cond_t, d_mod, ada_w, m, v):
    n_layers, _, n_cols = ada_w.shape
    tr = ROW_TILE

    def body(c_ref, dm_ref, w_ref, m_ref, v_ref, g_ref, d_ref, mo_ref, vo_ref):
        g = _mm_f32(_silu(c_ref[...]), dm_ref[...])
        g_ref[...] = g
        d_ref[...], mo_ref[...], vo_ref[...] = _adamw_math(w_ref[...], g, m_ref[...], v_ref[...])

    tile = pl.BlockSpec((None, tr, n_cols), lambda l, i: (l, i, 0))
    return pl.pallas_call(
        body, name="ada_update", grid=(n_layers, D_MODEL // tr),
        out_shape=[jax.ShapeDtypeStruct(ada_w.shape, F32)] * 4,
        in_specs=[pl.BlockSpec((tr, 16), lambda l, i: (i, 0)), pl.BlockSpec((None, 16, n_cols), lambda l, i: (l, 0, 0)),
                  tile, tile, tile],
        out_specs=[tile] * 4,
    )(cond_t, d_mod, ada_w, m, v)


def _cond_ctx_partial(d_modc, ada_w0):
    n_cols = ada_w0.shape[1]
    tr = ROW_TILE

    def body(dm_ref, w_ref, o_ref):
        o_ref[...] = jnp.sum(w_ref[...] * dm_ref[...], axis=-1, keepdims=True)

    return pl.pallas_call(
        body, name="cond_ctx_partial", grid=(D_MODEL // tr,),
        out_shape=jax.ShapeDtypeStruct((D_MODEL, 1), F32),
        in_specs=[pl.BlockSpec((1, n_cols), lambda i: (0, 0)), pl.BlockSpec((tr, n_cols), lambda i: (i, 0))],
        out_specs=pl.BlockSpec((tr, 1), lambda i: (i, 0)),
    )(d_modc, ada_w0)


def _cond_ctx_update(gathered, c_ctx, m, v):
    def body(p_ref, w_ref, m_ref, v_ref, g_ref, d_ref, mo_ref, vo_ref):
        acc = p_ref[0:1, :]
        for s in range(1, N_CHIPS):
            acc = acc + p_ref[16 * s:16 * s + 1, :]
        w = w_ref[...]
        g = acc * _dsilu(w)
        g_ref[...] = g
        d_ref[...], mo_ref[...], vo_ref[...] = _adamw_math(w, g, m_ref[...], v_ref[...])

    return pl.pallas_call(body, name="cond_ctx_update", out_shape=[jax.ShapeDtypeStruct((1, D_MODEL), F32)] * 4,
                          in_specs=[VMEM] * 4, out_specs=[VMEM] * 4)(gathered, c_ctx, m, v)


def _local_step(x2, ctx2, target, mod_mine, mod_ctx, lb_logits, scale_full, full, norm_w, gnorm, final_w):
    row = lambda a: a.reshape(1, -1)
    shift0, scale0, gate0 = (row(a) for a in jnp.split(mod_mine[0], 3))
    shift1, scale1, gate1 = (row(a) for a in jnp.split(mod_mine[1], 3))
    shift_c, scale_c, _ = (row(a) for a in jnp.split(mod_ctx, 3))
    xa = jnp.concatenate([ctx2, x2], axis=0)
    nw0, nw1 = norm_w[0:1], norm_w[1:2]
    scales0 = jnp.concatenate([scale_c, scale0])

    g5, h0 = _ln_mod_matmul(xa, nw0, jnp.concatenate([shift_c, shift0]), scales0, full["hgrn_w_in"], "hgrn_in_forward")
    o_raw = _gla_forward(g5, lb_logits)
    x1, res0 = _hgrn_out_forward(o_raw, g5, x2, gnorm, gate0, full["hgrn_w_out"])
    uz, h1 = _ln_mod_matmul(x1, nw1, shift1, scale1, full["pool_w_in"], "pool_in_forward")
    d_x2, loss_part, d_final, d_gate1 = _pool_forward_loss(uz, x1, target, gate1, full["pool_w_grp"], scale_full,
                                                           full["pool_w_out"], final_w)

    d_uz, dw_pool_out, dw_pool_grp, d_pscale = _pool_backward(d_x2, uz, gate1, full["pool_w_grp"], scale_full, full["pool_w_out"])
    d_x1, d_nw1, d_mod1 = _ln_mod_backward(d_uz, full["pool_w_in"], x1, nw1, scale1, d_x2, "pool_in_backward")
    dw_pool_in = _weight_grad(h1, d_uz, "pool_in_weight_grad")
    d_o, d_z, dw_hgrn_out, d_gate0, d_gnorm = _hgrn_out_backward(d_x1, o_raw, g5, res0, gnorm, gate0, full["hgrn_w_out"])
    d_g5, d_lb = _gla_backward(g5, lb_logits, d_o, d_z)
    d_x, d_nw0, d_mod0 = _ln_mod_backward(d_g5, full["hgrn_w_in"], xa, nw0, scales0, d_x1, "hgrn_in_backward")
    dw_hgrn_in = _weight_grad(h0, d_g5, "hgrn_in_weight_grad")

    zero = jnp.zeros((1, D_MODEL), F32)
    small = jnp.concatenate([d_mod0[1, 0:2], d_gate0, d_mod1[0, 0:2], d_gate1, d_mod0[0, 0:2], zero, d_nw0, d_nw1, d_gnorm,
                             d_final, d_pscale, d_lb, jnp.broadcast_to(loss_part[0:1, 0:1], (1, D_MODEL)),
                             jnp.zeros((SMALL_ROWS - 17, D_MODEL), F32)], axis=0)
    grads = {"hgrn_w_in": dw_hgrn_in, "hgrn_w_out": dw_hgrn_out, "pool_w_in": dw_pool_in, "pool_w_grp": dw_pool_grp,
             "pool_w_out": dw_pool_out}
    return {"d_x": d_x, "grads": grads, "small": small}


def kernel(x, c, ctx, c_ctx, ada_w, ada_b, norm_w, hgrn_w_in, hgrn_lb_logits, hgrn_gnorm_w, hgrn_w_out, pool_w_in, pool_w_grp, pool_scale, pool_w_out, final_norm_w, loss_target, m_c_ctx, m_ada_w, m_ada_b, m_norm_w, m_hgrn_w_in, m_hgrn_lb_logits, m_hgrn_gnorm_w, m_hgrn_w_out, m_pool_w_in, m_pool_w_grp, m_pool_scale, m_pool_w_out, m_final_norm_w, v_c_ctx, v_ada_w, v_ada_b, v_norm_w, v_hgrn_w_in, v_hgrn_lb_logits, v_hgrn_gnorm_w, v_hgrn_w_out, v_pool_w_in, v_pool_w_grp, v_pool_scale, v_pool_w_out, v_final_norm_w):
    xi, yi, ci = _my_place()
    chip = 2 * xi + yi
    dev = 4 * xi + 2 * yi + ci
    ada_cols = ada_w.shape[2]
    lb_cols = hgrn_lb_logits.shape[2]
    ps_cols = pool_scale.shape[1]
    row = lambda a: a.reshape(1, -1)

    def chip_cols(a, n):
        return lax.dynamic_slice_in_dim(a, chip * n, n, axis=a.ndim - 1)

    def from_chips(g, rows_per_dev, take):
        return jnp.concatenate([g[2 * s * rows_per_dev:2 * s * rows_per_dev + take] for s in range(N_CHIPS)], axis=1)

    first = jnp.concatenate([jnp.broadcast_to(c, (8, D_MODEL)), jnp.pad(hgrn_lb_logits[0], ((0, 6), (0, 0))),
                             jnp.pad(pool_scale, ((0, 7), (0, 0)))], axis=1)
    first_all = _all_gather_small(first, "gather_cond")
    cond_all = first_all[::8, :D_MODEL]
    lb_logits = from_chips(first_all[:, D_MODEL:D_MODEL + lb_cols], 8, 2)
    scale_full = from_chips(first_all[:, D_MODEL + lb_cols:], 8, 1)
    cond_rows = jnp.concatenate([cond_all, row(c_ctx), jnp.zeros((7, D_MODEL), F32)], axis=0)

    parts = _mod_parts(cond_rows, ada_w, chip_cols(ada_b, ada_cols))
    parts_all = _all_gather_small(parts.reshape(32, ada_cols), "gather_mod")
    mod_all = from_chips(parts_all, 32, 32).reshape(2, 16, 3 * D_MODEL)
    mod_mine = lax.dynamic_index_in_dim(mod_all, dev, axis=1, keepdims=False)

    shards = {"hgrn_w_in": hgrn_w_in[0], "hgrn_w_out": hgrn_w_out[0], "pool_w_in": pool_w_in[0],
              "pool_w_grp": pool_w_grp[0], "pool_w_out": pool_w_out[0]}
    full = _all_gather_weights([shards[k].astype(BF16) for k in W_NAMES])

    loc = _local_step(x[0], ctx[0], loss_target[0], mod_mine, mod_all[0, 8], lb_logits, scale_full, full, norm_w,
                      hgrn_gnorm_w, row(final_norm_w))

    small_all = _all_gather_small(loc["small"], "gather_small")
    sums = _sum_devices(small_all)
    loss = sums[16, 0]

    slots = _scatter_grads([loc["grads"][k].astype(BF16) for k in W_NAMES])
    halves = []
    for k, s in zip(W_NAMES, slots):
        hs = _half_shape(k)
        flat = _sum_slots(s.reshape(N_DEV, -1, hs[-1]), "sum_" + k)
        halves.append(flat.reshape(hs))
    big_grads = dict(zip(W_NAMES, _exchange_halves(halves)))

    out = {}

    def update(name, w, g, m, v):
        shape = w.shape
        w2, g2, m2, v2 = (a.reshape(-1, shape[-1]) for a in (w, g, m, v))
        d, mn, vn = _adamw(w2, g2, m2, v2, "adamw_" + name)
        out[name] = tuple(a.reshape(shape) for a in (g2, d, mn, vn))

    moments = {"hgrn_w_in": (m_hgrn_w_in, v_hgrn_w_in), "hgrn_w_out": (m_hgrn_w_out, v_hgrn_w_out),
               "pool_w_in": (m_pool_w_in, v_pool_w_in), "pool_w_grp": (m_pool_w_grp, v_pool_w_grp),
               "pool_w_out": (m_pool_w_out, v_pool_w_out)}
    weights = {"hgrn_w_in": hgrn_w_in, "hgrn_w_out": hgrn_w_out, "pool_w_in": pool_w_in, "pool_w_grp": pool_w_grp,
               "pool_w_out": pool_w_out}
    for k in W_NAMES:
        update(k, weights[k], big_grads[k].reshape(weights[k].shape), *moments[k])

    g_ada_b = jnp.stack([(sums[0:3] + sums[6:9]).reshape(-1), sums[3:6].reshape(-1)])
    update("ada_b", ada_b, g_ada_b, m_ada_b, v_ada_b)
    update("norm_w", norm_w, sums[9:11], m_norm_w, v_norm_w)
    update("hgrn_gnorm_w", hgrn_gnorm_w, sums[11:12], m_hgrn_gnorm_w, v_hgrn_gnorm_w)
    update("final_norm_w", row(final_norm_w), sums[12:13], row(m_final_norm_w), row(v_final_norm_w))
    update("pool_scale", pool_scale, chip_cols(sums[13:14], ps_cols), m_pool_scale, v_pool_scale)
    update("hgrn_lb_logits", hgrn_lb_logits, chip_cols(sums[14:16], lb_cols)[None], m_hgrn_lb_logits, v_hgrn_lb_logits)

    per_dev = small_all.reshape(N_DEV, SMALL_ROWS, D_MODEL)
    pad7 = jnp.zeros((7, 3 * D_MODEL), F32)
    dm0 = jnp.concatenate([per_dev[:, 0:3].reshape(N_DEV, -1), sums[6:9].reshape(1, -1), pad7], axis=0)
    dm1 = jnp.concatenate([per_dev[:, 3:6].reshape(N_DEV, -1), jnp.zeros((8, 3 * D_MODEL), F32)], axis=0)
    d_mod = chip_cols(jnp.stack([dm0, dm1]), ada_cols)
    out["ada_w"] = _ada_update(cond_rows.T, d_mod, ada_w, m_ada_w, v_ada_w)

    part_c = _cond_ctx_partial(chip_cols(sums[6:9].reshape(1, -1), ada_cols), ada_w[0])
    part_all = _all_gather_small(jnp.pad(part_c.reshape(1, D_MODEL), ((0, 7), (0, 0))), "gather_cond_ctx")
    g_c, d_c, m_c, v_c = _cond_ctx_update(part_all, row(c_ctx), row(m_c_ctx), row(v_c_ctx))
    out["c_ctx"] = tuple(a.reshape(-1) for a in (g_c, d_c, m_c, v_c))
    out["final_norm_w"] = tuple(a.reshape(-1) for a in out["final_norm_w"])

    names = ["c_ctx", "ada_w", "ada_b", "norm_w", "hgrn_w_in", "hgrn_lb_logits", "hgrn_gnorm_w", "hgrn_w_out", "pool_w_in",
             "pool_w_grp", "pool_scale", "pool_w_out", "final_norm_w"]
    return (loss, loc["d_x"][None], *[out[k][0] for k in names], *[out[k][1] for k in names], *[out[k][2] for k in names],
            *[out[k][3] for k in names])
```

```python
import functools

import numpy as np
import jax
import jax.numpy as jnp
from jax import lax
from jax.experimental import pallas as pl
from jax.experimental.pallas import tpu as pltpu

F32 = jnp.float32
BF16 = jnp.bfloat16

D_MODEL = 1024
SEQ = 2048
CTX_LEN = 256
ROWS_ALL = CTX_LEN + SEQ
HEADS = 8
HEAD_DIM = 128
CHUNK = 64
N_CTX_CHUNKS = CTX_LEN // CHUNK
N_LAT_CHUNKS = SEQ // CHUNK
N_CHUNKS = N_CTX_CHUNKS + N_LAT_CHUNKS
GRID_W = 64
POOL_WINDOWS = (2, 4, 8, 16)
POOL_GROUPS = 4
POOL_GROUP_DIM = 256
HGRN_SECTIONS = 5
POOL_SECTIONS = 2
EPS = 1e-6
N_DEV = 8
N_CHIPS = 4
ROW_TILE = 256
SMALL_ROWS = 24

ADAM_LR = 0.001
ADAM_B1 = 0.9
ADAM_B2 = 0.999
ADAM_EPS = 1e-08
ADAM_WD = 0.01
ADAM_STEP = 10

MESH = pl.DeviceIdType.MESH
MIB = 1 << 20
ANY = pl.BlockSpec(memory_space=pl.ANY)
VMEM = pl.BlockSpec(memory_space=pltpu.VMEM)


def _params(vmem_mib=None):
    if vmem_mib is None:
        return pltpu.CompilerParams()
    return pltpu.CompilerParams(vmem_limit_bytes=vmem_mib * MIB)


def _sig(a):
    return 1.0 / (1.0 + jnp.exp(-a))


def _silu(a):
    return a * _sig(a)


def _dsilu(a):
    s = _sig(a)
    return s * (1.0 + a * (1.0 - s))


def _mm(a, b):
    return jnp.dot(a.astype(BF16), b.astype(BF16), preferred_element_type=F32)


def _mm_nt(a, b):
    return lax.dot_general(a.astype(BF16), b.astype(BF16), (((1,), (1,)), ((), ())), preferred_element_type=F32)


def _mm_tn(a, b):
    return lax.dot_general(a.astype(BF16), b.astype(BF16), (((0,), (0,)), ((), ())), preferred_element_type=F32)


def _split2(a):
    hi = a.astype(BF16)
    lo = (a - hi.astype(F32)).astype(BF16)
    return hi, lo


def _mm_exact_lhs(m_bf, a):
    hi, lo = _split2(a)
    return jnp.dot(m_bf, hi, preferred_element_type=F32) + jnp.dot(m_bf, lo, preferred_element_type=F32)


def _mm_f32(a, b):
    ah, al = _split2(a)
    bh, bl = _split2(b)
    return (jnp.dot(ah, bh, preferred_element_type=F32) + jnp.dot(al, bh, preferred_element_type=F32)
            + jnp.dot(ah, bl, preferred_element_type=F32))


def _my_place():
    return lax.axis_index("x"), lax.axis_index("y"), lax.axis_index("c")


def _all_gather_small(blk, name):
    m_per, n = blk.shape

    def body(x_ref, out_ref, send_sems, recv_sems, local_sem):
        x, y, c = _my_place()
        me, sibling = (x, y, c), (x, y, 1 - c)
        chips = [(1 - x, y), (x, 1 - y), (1 - x, 1 - y)]

        def rows(px, py, pc):
            return out_ref.at[pl.ds((4 * px + 2 * py + pc) * m_per, m_per), :]

        def copy(k, block, to, src=None):
            return pltpu.make_async_remote_copy(
                src_ref=rows(*block) if src is None else src, dst_ref=rows(*block),
                send_sem=send_sems.at[k], recv_sem=recv_sems.at[k], device_id=to, device_id_type=MESH)

        mine = pltpu.make_async_copy(x_ref, rows(*me), local_sem)
        mine.start()
        first = [copy(0, me, sibling, src=x_ref)]
        first += [copy(1 + j, me, (*chip, c), src=x_ref) for j, chip in enumerate(chips)]
        for cp in first:
            cp.start()
        passed = [copy(4 + j, (*chip, c), sibling) for j, chip in enumerate(chips)]
        for j, chip in enumerate(chips):
            copy(1 + j, (*chip, c), me).wait_recv()
            passed[j].start()
        copy(0, sibling, me).wait_recv()
        for j, chip in enumerate(chips):
            copy(4 + j, (*chip, 1 - c), me).wait_recv()
        for cp in first + passed:
            cp.wait_send()
        mine.wait()

    return pl.pallas_call(
        body, name=name,
        out_shape=jax.ShapeDtypeStruct((N_DEV * m_per, n), blk.dtype),
        in_specs=[VMEM], out_specs=VMEM,
        scratch_shapes=[pltpu.SemaphoreType.DMA((7,)), pltpu.SemaphoreType.DMA((7,)), pltpu.SemaphoreType.DMA],
    )(blk)


W_SPECS = {
    "hgrn_w_in": ((D_MODEL, 5 * D_MODEL), (1, 1280, 0, 512)),
    "hgrn_w_out": ((D_MODEL, D_MODEL), (0, 256, 0, 128)),
    "pool_w_in": ((D_MODEL, 2 * D_MODEL), (1, 512, 0, 512)),
    "pool_w_grp": ((POOL_GROUPS, POOL_GROUP_DIM, POOL_GROUP_DIM), (1, 64, 1, 32)),
    "pool_w_out": ((D_MODEL, D_MODEL), (0, 256, 0, 128)),
}
W_NAMES = tuple(W_SPECS)


def _al(v, m):
    return pl.multiple_of(v, m)


def _region(ref, spec, chip, half):
    ca, cn, ha, hn = spec
    idx = [slice(None)] * len(ref.shape)
    if ca == ha:
        if half is None:
            idx[ca] = pl.ds(_al(chip * cn, cn), cn)
        else:
            idx[ca] = pl.ds(_al(chip * cn + half * hn, hn), hn)
    else:
        idx[ca] = pl.ds(_al(chip * cn, cn), cn)
        if half is not None:
            idx[ha] = pl.ds(_al(half * hn, hn), hn)
    return ref.at[tuple(idx)]


def _half_of(ref, spec, half):
    _, _, ha, hn = spec
    idx = [slice(None)] * len(ref.shape)
    idx[ha] = pl.ds(_al(half * hn, hn), hn)
    return ref.at[tuple(idx)]


PIECE_BYTES = 256 * 1024


def _pieces(ref):
    lead = ref.shape[0]
    want = (int(np.prod(ref.shape)) * ref.dtype.itemsize) // PIECE_BYTES
    n = max([1] + [k for k in range(1, want + 1) if lead % k == 0 and (lead // k) % 16 == 0])
    rows = lead // n
    return [ref.at[pl.ds(i * rows, rows)] for i in range(n)]


def _half_shape(name):
    full, (ca, cn, ha, hn) = W_SPECS[name]
    shp = list(full)
    shp[ca] = cn
    shp[ha] = hn
    return tuple(shp)


def _all_gather_weights(shards):
    n = len(W_NAMES)
    specs = [W_SPECS[k][1] for k in W_NAMES]

    def body(*refs):
        sh, full = refs[:n], refs[n:2 * n]
        send_sems, recv_sems, local_sems = refs[2 * n:]
        x, y, c = _my_place()
        chip_me = 2 * x + y
        sibling = (x, y, 1 - c)
        chips = [(1 - x, y), (x, 1 - y), (1 - x, 1 - y)]

        def remote(a, k, src, dst, to):
            return pltpu.make_async_remote_copy(src_ref=src, dst_ref=dst, send_sem=send_sems.at[6 * a + k],
                                                recv_sem=recv_sems.at[6 * a + k], device_id=to, device_id_type=MESH)

        local = [pltpu.make_async_copy(sh[a], _region(full[a], specs[a], chip_me, None), local_sems.at[a]) for a in range(n)]
        for cp in local:
            cp.start()
        sends = []
        for a in range(n):
            for j, (px, py) in enumerate(chips):
                src, dst = _half_of(sh[a], specs[a], c), _region(full[a], specs[a], chip_me, c)
                for s_piece, d_piece in zip(_pieces(src), _pieces(dst)):
                    remote(a, j, s_piece, d_piece, (px, py, c)).start()
                sends.append(remote(a, j, src, dst, (px, py, c)))
        for a in range(n):
            for j, (px, py) in enumerate(chips):
                landed = _region(full[a], specs[a], 2 * px + py, c)
                remote(a, j, landed, landed, (px, py, c)).wait_recv()
                for piece in _pieces(landed):
                    remote(a, 3 + j, piece, piece, sibling).start()
                sends.append(remote(a, 3 + j, landed, landed, sibling))
        for a in range(n):
            for j, (px, py) in enumerate(chips):
                other = _region(full[a], specs[a], 2 * px + py, 1 - c)
                remote(a, 3 + j, other, other, sibling).wait_recv()
        for cp in sends:
            cp.wait_send()
        for cp in local:
            cp.wait()

    outs = pl.pallas_call(
        body, name="all_gather_weights",
        out_shape=[jax.ShapeDtypeStruct(W_SPECS[k][0], BF16) for k in W_NAMES],
        in_specs=[ANY] * n, out_specs=[ANY] * n,
        scratch_shapes=[pltpu.SemaphoreType.DMA((6 * n,)), pltpu.SemaphoreType.DMA((6 * n,)), pltpu.SemaphoreType.DMA((n,))],
    )(*shards)
    return dict(zip(W_NAMES, outs))


def _scatter_grads(parts):
    n = len(W_NAMES)
    specs = [W_SPECS[k][1] for k in W_NAMES]

    def body(*refs):
        part, slots = refs[:n], refs[n:2 * n]
        send_sems, recv_sems, local_sems = refs[2 * n:]
        x, y, c = _my_place()
        dev_me = 4 * x + 2 * y + c
        local = [pltpu.make_async_copy(_region(part[a], specs[a], 2 * x + y, c), slots[a].at[dev_me], local_sems.at[a])
                 for a in range(n)]
        for cp in local:
            cp.start()
        for a in range(n):
            for flip in range(1, N_DEV):
                fx, fy, fc = flip >> 2, (flip >> 1) & 1, flip & 1
                tx = 1 - x if fx else x
                ty = 1 - y if fy else y
                tc = 1 - c if fc else c
                for src, dst in zip(_pieces(_region(part[a], specs[a], 2 * tx + ty, tc)), _pieces(slots[a].at[dev_me])):
                    pltpu.make_async_remote_copy(src_ref=src, dst_ref=dst, send_sem=send_sems.at[a], recv_sem=recv_sems.at[a],
                                                 device_id=(tx, ty, tc), device_id_type=MESH).start()
        for a in range(n):
            seven = slots[a].at[pl.ds(0, N_DEV - 1)]
            pltpu.make_async_remote_copy(src_ref=seven, dst_ref=seven, send_sem=send_sems.at[a], recv_sem=recv_sems.at[a],
                                         device_id=(x, y, c), device_id_type=MESH).wait()
        for cp in local:
            cp.wait()

    return pl.pallas_call(
        body, name="scatter_grads",
        out_shape=[jax.ShapeDtypeStruct((N_DEV,) + _half_shape(k), BF16) for k in W_NAMES],
        in_specs=[ANY] * n, out_specs=[ANY] * n,
        scratch_shapes=[pltpu.SemaphoreType.DMA((n,)), pltpu.SemaphoreType.DMA((n,)), pltpu.SemaphoreType.DMA((n,))],
    )(*parts)


def _exchange_halves(halves):
    n = len(W_NAMES)
    specs = [W_SPECS[k][1] for k in W_NAMES]

    def shard_shape(k):
        shp = list(_half_shape(k))
        shp[W_SPECS[k][1][2]] *= 2
        return tuple(shp)

    def body(*refs):
        half, out = refs[:n], refs[n:2 * n]
        send_sems, recv_sems, local_sems = refs[2 * n:]
        x, y, c = _my_place()
        sibling = (x, y, 1 - c)

        def remote(a, src, dst):
            return pltpu.make_async_remote_copy(src_ref=src, dst_ref=dst, send_sem=send_sems.at[a], recv_sem=recv_sems.at[a],
                                                device_id=sibling, device_id_type=MESH)

        local = [pltpu.make_async_copy(half[a], _half_of(out[a], specs[a], c), local_sems.at[a]) for a in range(n)]
        for cp in local:
            cp.start()
        for a in range(n):
            mine = _half_of(out[a], specs[a], c)
            for src, dst in zip(_pieces(half[a]), _pieces(mine)):
                remote(a, src, dst).start()
        for a in range(n):
            theirs = _half_of(out[a], specs[a], 1 - c)
            remote(a, theirs, theirs).wait_recv()
        for a in range(n):
            remote(a, half[a], half[a]).wait_send()
        for cp in local:
            cp.wait()

    return pl.pallas_call(
        body, name="exchange_halves",
        out_shape=[jax.ShapeDtypeStruct(shard_shape(k), F32) for k in W_NAMES],
        in_specs=[VMEM] * n, out_specs=[VMEM] * n,
        scratch_shapes=[pltpu.SemaphoreType.DMA((n,)), pltpu.SemaphoreType.DMA((n,)), pltpu.SemaphoreType.DMA((n,))],
        compiler_params=_params(32),
    )(*halves)


def _mod_parts(c_rows, ada_w, ada_b_cols):
    n_layers, _, n_cols = ada_w.shape

    def body(c_ref, w_ref, b_ref, o_ref):
        o_ref[...] = _mm_f32(_silu(c_ref[...]), w_ref[...]) + b_ref[...]

    return pl.pallas_call(
        body, name="mod_parts", grid=(n_layers,),
        out_shape=jax.ShapeDtypeStruct((n_layers, 16, n_cols), F32),
        in_specs=[pl.BlockSpec((16, D_MODEL), lambda i: (0, 0)),
                  pl.BlockSpec((None, D_MODEL, n_cols), lambda i: (i, 0, 0)),
                  pl.BlockSpec((None, 1, n_cols), lambda i: (i, 0, 0))],
        out_specs=pl.BlockSpec((None, 16, n_cols), lambda i: (i, 0, 0)),
        compiler_params=_params(40),
    )(c_rows, ada_w, ada_b_cols.reshape(n_layers, 1, n_cols))


def _ln_mod_matmul(xin, nw, shift, scale, w, name):
    rows = xin.shape[0]
    n_sec = w.shape[1] // D_MODEL
    n_mod = shift.shape[0]
    tm = ROW_TILE

    def body(x_ref, nw_ref, sh_ref, sc_ref, w_ref, g_ref, h_ref, h_scr):
        @pl.when(pl.program_id(1) == 0)
        def _():
            xv = x_ref[...]
            r = lax.rsqrt(jnp.mean(xv * xv, axis=-1, keepdims=True) + EPS)
            h = ((xv * r * nw_ref[...]) * (1.0 + sc_ref[...]) + sh_ref[...]).astype(BF16)
            h_scr[...] = h
            h_ref[...] = h

        g_ref[...] = jnp.dot(h_scr[...], w_ref[...], preferred_element_type=F32)

    mod_spec = pl.BlockSpec((None, 1, D_MODEL), lambda i, j: (jnp.minimum(i, n_mod - 1), 0, 0))
    return pl.pallas_call(
        body, name=name, grid=(rows // tm, n_sec),
        out_shape=[jax.ShapeDtypeStruct((n_sec, rows, D_MODEL), F32), jax.ShapeDtypeStruct((rows, D_MODEL), BF16)],
        in_specs=[pl.BlockSpec((tm, D_MODEL), lambda i, j: (i, 0)),
                  pl.BlockSpec((1, D_MODEL), lambda i, j: (0, 0)),
                  mod_spec, mod_spec,
                  pl.BlockSpec((D_MODEL, D_MODEL), lambda i, j: (0, j))],
        out_specs=[pl.BlockSpec((None, tm, D_MODEL), lambda i, j: (j, i, 0)),
                   pl.BlockSpec((tm, D_MODEL), lambda i, j: (i, 0))],
        scratch_shapes=[pltpu.VMEM((tm, D_MODEL), BF16)],
        compiler_params=_params(32),
    )(xin, nw, shift.reshape(n_mod, 1, D_MODEL), scale.reshape(n_mod, 1, D_MODEL), w)


def _chunk_masks(rev):
    rid = lax.broadcasted_iota(jnp.int32, (CHUNK, CHUNK), 0)
    cid = lax.broadcasted_iota(jnp.int32, (CHUNK, CHUNK), 1)
    keep = (cid >= rid) if rev else (cid <= rid)
    keep_t = (cid <= rid) if rev else (cid >= rid)
    one, zero = jnp.ones((CHUNK, CHUNK), F32), jnp.zeros((CHUNK, CHUNK), F32)
    return keep, jnp.where(keep, one, zero).astype(BF16), jnp.where(keep_t, one, zero).astype(BF16)


def _chunk_rows(t, rev, latent):
    n = N_LAT_CHUNKS if latent else N_CTX_CHUNKS
    base = CTX_LEN if latent else 0
    idx = (n - 1 - t) if rev else t
    return pl.multiple_of(base + idx * CHUNK, CHUNK)


def _gates(fpre, lb):
    sg = _sig(fpre)
    f = lb + (1.0 - lb) * sg
    return sg, f, 1.0 - f, jnp.log(f)


G_SPEC = lambda sec: pl.BlockSpec((None, ROWS_ALL, HEAD_DIM), lambda h, sec=sec: (sec, 0, h))


def _gla_forward(g5, lb_logits):
    q_scale = HEAD_DIM ** -0.5

    def body(ff_ref, fb_ref, v_ref, q_ref, lg_ref, o_ref, st_ref, decay_ref, qt_ref):
        for rev in (False, True):
            f_ref = fb_ref if rev else ff_ref
            lb = _sig(lg_ref[1:2, :] if rev else lg_ref[0:1, :])
            keep, tri, _ = _chunk_masks(rev)
            last = 0 if rev else CHUNK - 1
            mid = CHUNK // 2 if rev else CHUNK // 2 - 1

            def local_step(t, carry, latent):
                r0 = _chunk_rows(t, rev, latent)
                rows = pl.ds(r0, CHUNK)
                step = t + (N_CTX_CHUNKS if latent else 0)
                _, _, k, lf = _gates(f_ref[rows, :], lb)
                v = v_ref[rows, :]
                b = _mm_exact_lhs(tri, lf)
                bl = b[last:last + 1, :]
                if latent:
                    q = _silu(q_ref[rows, :]) * q_scale
                    bm = b[mid:mid + 1, :]
                    a = _mm_nt(q * jnp.exp(b - bm), k * jnp.exp(bm - b))
                    o = _mm(jnp.where(keep, a, 0.0), v)
                    orow = pl.ds(pl.multiple_of(r0 - CTX_LEN, CHUNK), CHUNK)
                    qt_ref[orow, :] = (q * jnp.exp(b)).astype(BF16)
                    if rev:
                        o_ref[orow, :] += o
                    else:
                        o_ref[orow, :] = o
                decay_ref[step] = jnp.exp(bl)
                st_ref[step] = _mm_tn(v, k * jnp.exp(bl - b))
                return carry

            lax.fori_loop(0, N_CTX_CHUNKS, functools.partial(local_step, latent=False), 0, unroll=2)
            lax.fori_loop(0, N_LAT_CHUNKS, functools.partial(local_step, latent=True), 0, unroll=4)

            def scan_step(t, st):
                update = st_ref[t]
                st_ref[t] = st
                return st * decay_ref[t] + update

            lax.fori_loop(0, N_CHUNKS, scan_step, jnp.zeros((HEAD_DIM, HEAD_DIM), F32), unroll=2)

            def inter_step(t, carry):
                r0 = _chunk_rows(t, rev, True)
                orow = pl.ds(pl.multiple_of(r0 - CTX_LEN, CHUNK), CHUNK)
                o_ref[orow, :] += lax.dot_general(qt_ref[orow, :], st_ref[t + N_CTX_CHUNKS].astype(BF16),
                                                  (((1,), (1,)), ((), ())), preferred_element_type=F32)
                return carry

            lax.fori_loop(0, N_LAT_CHUNKS, inter_step, 0, unroll=4)

    return pl.pallas_call(
        body, name="gla_forward", grid=(HEADS,),
        out_shape=jax.ShapeDtypeStruct((SEQ, D_MODEL), F32),
        in_specs=[G_SPEC(0), G_SPEC(1), G_SPEC(2), G_SPEC(3), pl.BlockSpec((2, HEAD_DIM), lambda h: (0, h))],
        out_specs=pl.BlockSpec((SEQ, HEAD_DIM), lambda h: (0, h)),
        scratch_shapes=[pltpu.VMEM((N_CHUNKS, HEAD_DIM, HEAD_DIM), F32), pltpu.VMEM((N_CHUNKS, 1, HEAD_DIM), F32),
                        pltpu.VMEM((SEQ, HEAD_DIM), BF16)],
        compiler_params=_params(32),
    )(g5, g5, g5, g5, lb_logits)


def _gla_backward(g5, lb_logits, d_o, d_z):
    q_scale = HEAD_DIM ** -0.5

    def body(ff_ref, fb_ref, v_ref, q_ref, lg_ref, do_ref, dz_ref, dg_ref, dlg_ref, st_ref, dst_ref, decay_ref):
        dg_ref[3, 0:CTX_LEN, :] = jnp.zeros((CTX_LEN, HEAD_DIM), F32)
        dg_ref[4, 0:CTX_LEN, :] = jnp.zeros((CTX_LEN, HEAD_DIM), F32)
        dg_ref[4, CTX_LEN:ROWS_ALL, :] = dz_ref[...]
        is_row = lax.broadcasted_iota(jnp.int32, (CHUNK, HEAD_DIM), 0)
        for rev in (False, True):
            d = 1 if rev else 0
            f_ref = fb_ref if rev else ff_ref
            lb = _sig(lg_ref[d:d + 1, :])
            keep, tri, tri_t = _chunk_masks(rev)
            last = 0 if rev else CHUNK - 1
            mid = CHUNK // 2 if rev else CHUNK // 2 - 1

            def local_step(t, carry, latent):
                r0 = _chunk_rows(t, rev, latent)
                rows = pl.ds(r0, CHUNK)
                step = t + (N_CTX_CHUNKS if latent else 0)
                _, _, k, lf = _gates(f_ref[rows, :], lb)
                b = _mm_exact_lhs(tri, lf)
                bl = b[last:last + 1, :]
                decay_ref[step] = jnp.exp(bl)
                st_ref[step] = _mm_tn(v_ref[rows, :], k * jnp.exp(bl - b))
                if latent:
                    q_t = _silu(q_ref[rows, :]) * q_scale * jnp.exp(b)
                    dst_ref[step] = _mm_tn(do_ref[pl.ds(pl.multiple_of(r0 - CTX_LEN, CHUNK), CHUNK), :], q_t)
                else:
                    dst_ref[step] = jnp.zeros((HEAD_DIM, HEAD_DIM), F32)
                return carry

            lax.fori_loop(0, N_CTX_CHUNKS, functools.partial(local_step, latent=False), 0, unroll=2)
            lax.fori_loop(0, N_LAT_CHUNKS, functools.partial(local_step, latent=True), 0, unroll=4)

            def scan_step(i, carry):
                st, d_st = carry
                j = N_CHUNKS - 1 - i
                update, d_update = st_ref[i], dst_ref[j]
                st_ref[i] = st
                dst_ref[j] = d_st
                return st * decay_ref[i] + update, d_st * decay_ref[j] + d_update

            zero_state = jnp.zeros((HEAD_DIM, HEAD_DIM), F32)
            lax.fori_loop(0, N_CHUNKS, scan_step, (zero_state, zero_state))

            def grad_step(t, dlb, latent):
                r0 = _chunk_rows(t, rev, latent)
                rows = pl.ds(r0, CHUNK)
                step = t + (N_CTX_CHUNKS if latent else 0)
                sg, f, k, lf = _gates(f_ref[rows, :], lb)
                v = v_ref[rows, :]
                b = _mm_exact_lhs(tri, lf)
                bl = b[last:last + 1, :]
                e_end = jnp.exp(bl - b)
                k_end = k * e_end
                decay = jnp.exp(bl)
                d_st = dst_ref[step]
                st_prev = st_ref[step]
                d_kend = _mm(v, d_st)
                d_decay = jnp.sum(d_st * st_prev, axis=0, keepdims=True)
                t_kend = d_kend * k_end
                d_v = _mm_nt(k_end, d_st)
                d_k = d_kend * e_end
                d_b = -t_kend
                if latent:
                    qpre = q_ref[rows, :]
                    q = _silu(qpre) * q_scale
                    bm = b[mid:mid + 1, :]
                    e_b, e_qm, e_km = jnp.exp(b), jnp.exp(b - bm), jnp.exp(bm - b)
                    q_t, q_m, k_m = q * e_b, q * e_qm, k * e_km
                    a = jnp.where(keep, _mm_nt(q_m, k_m), 0.0)
                    d_out = do_ref[pl.ds(pl.multiple_of(r0 - CTX_LEN, CHUNK), CHUNK), :]
                    d_a = jnp.where(keep, _mm_nt(d_out, v), 0.0)
                    d_qm = _mm(d_a, k_m)
                    d_km = _mm_tn(d_a, q_m)
                    d_qt = _mm(d_out, st_prev)
                    d_v = d_v + _mm_tn(a, d_out)
                    d_k = d_k + d_km * e_km
                    d_b = d_b + d_qt * q_t + d_qm * q_m - d_km * k_m
                    d_q = d_qt * e_b + d_qm * e_qm
                at_last = jnp.sum(t_kend, axis=0, keepdims=True) + d_decay * decay
                d_b = d_b + jnp.where(is_row == last, at_last, 0.0)
                d_lf = _mm_exact_lhs(tri_t, d_b)
                d_f = d_lf / f - d_k
                dg_ref[d, rows, :] = d_f * (1.0 - lb) * sg * (1.0 - sg)
                if rev:
                    dg_ref[2, rows, :] += d_v
                else:
                    dg_ref[2, rows, :] = d_v
                if latent:
                    d_qpre = d_q * q_scale * _dsilu(qpre)
                    if rev:
                        dg_ref[3, rows, :] += d_qpre
                    else:
                        dg_ref[3, rows, :] = d_qpre
                return dlb + jnp.sum(d_f * (1.0 - sg), axis=0, keepdims=True)

            dlb = lax.fori_loop(0, N_LAT_CHUNKS, functools.partial(grad_step, latent=True), jnp.zeros((1, HEAD_DIM), F32),
                                unroll=2)
            dlb = lax.fori_loop(0, N_CTX_CHUNKS, functools.partial(grad_step, latent=False), dlb, unroll=2)
            dlg_ref[d:d + 1, :] = dlb * lb * (1.0 - lb)

    col = pl.BlockSpec((SEQ, HEAD_DIM), lambda h: (0, h))
    return pl.pallas_call(
        body, name="gla_backward", grid=(HEADS,),
        out_shape=[jax.ShapeDtypeStruct((HGRN_SECTIONS, ROWS_ALL, D_MODEL), F32), jax.ShapeDtypeStruct((2, D_MODEL), F32)],
        in_specs=[G_SPEC(0), G_SPEC(1), G_SPEC(2), G_SPEC(3), pl.BlockSpec((2, HEAD_DIM), lambda h: (0, h)), col, col],
        out_specs=[pl.BlockSpec((HGRN_SECTIONS, ROWS_ALL, HEAD_DIM), lambda h: (0, 0, h)),
                   pl.BlockSpec((2, HEAD_DIM), lambda h: (0, h))],
        scratch_shapes=[pltpu.VMEM((N_CHUNKS, HEAD_DIM, HEAD_DIM), F32), pltpu.VMEM((N_CHUNKS, HEAD_DIM, HEAD_DIM), F32),
                        pltpu.VMEM((N_CHUNKS, 1, HEAD_DIM), F32)],
        compiler_params=_params(48),
    )(g5, g5, g5, g5, lb_logits, d_o, d_z)


GROUP = 2 * CHUNK


def _group_masks(rev):
    rid = lax.broadcasted_iota(jnp.int32, (GROUP, GROUP), 0)
    cid = lax.broadcasted_iota(jnp.int32, (GROUP, GROUP), 1)
    same = (rid >= CHUNK) == (cid >= CHUNK)
    causal = (cid >= rid) if rev else (cid <= rid)
    anti = (cid <= rid) if rev else (cid >= rid)
    bf = lambda m: jnp.where(m, jnp.ones((GROUP, GROUP), F32), jnp.zeros((GROUP, GROUP), F32)).astype(BF16)
    keep = same & causal
    return keep, bf(keep), bf(same & anti), bf(same & jnp.logical_not(anti))


def _group_sum(m_bf, a):
    hi, lo = _split2(a)
    r = jnp.dot(m_bf, jnp.concatenate([hi, lo], axis=1), preferred_element_type=F32)
    return r[:, :HEAD_DIM] + r[:, HEAD_DIM:]


def _chunk_row(a, pos):
    return jnp.concatenate([jnp.broadcast_to(a[c * CHUNK + pos:c * CHUNK + pos + 1, :], (CHUNK, HEAD_DIM)) for c in range(2)], axis=0)


def _by_chunk(a, second):
    return jnp.concatenate([jnp.where(second, 0.0, a), jnp.where(second, a, 0.0)], axis=1)


def _own_block(r):
    return jnp.concatenate([r[0:CHUNK, 0:HEAD_DIM], r[CHUNK:GROUP, HEAD_DIM:2 * HEAD_DIM]], axis=0)


def _scan_step_of(row_chunk, rev, latent):
    if not rev:
        return row_chunk
    return (N_CHUNKS + N_CTX_CHUNKS - 1 - row_chunk) if latent else (N_CTX_CHUNKS - 1 - row_chunk)


def _group_rows(i, j, per_step, latent):
    base = CTX_LEN if latent else 0
    return pl.multiple_of(base + (i * per_step + j) * GROUP, GROUP)


GROUPS_PER_STEP = 2


def _gla_forward(g5, lb_logits):
    q_scale = HEAD_DIM ** -0.5
    per = GROUPS_PER_STEP

    def body(ff_ref, fb_ref, v_ref, q_ref, lg_ref, o_ref, st_ref, decay_ref, qt_ref):
        second = lax.broadcasted_iota(jnp.int32, (GROUP, HEAD_DIM), 0) >= CHUNK
        for rev in (False, True):
            f_ref = fb_ref if rev else ff_ref
            lb = _sig(lg_ref[1:2, :] if rev else lg_ref[0:1, :])
            keep, tri, _, _ = _group_masks(rev)
            last = 0 if rev else CHUNK - 1
            mid = CHUNK // 2 if rev else CHUNK // 2 - 1

            def local_step(i, carry, latent):
                r0s = [_group_rows(i, j, per, latent) for j in range(per)]
                rows = [pl.ds(r0, GROUP) for r0 in r0s]
                gates = [_gates(f_ref[r, :], lb) for r in rows]
                vs = [v_ref[r, :] for r in rows]
                bs = [_group_sum(tri, g[3]) for g in gates]
                bls = [_chunk_row(b, last) for b in bs]
                ups = [_mm_tn(v, _by_chunk(g[2] * jnp.exp(bl - b), second)) for v, g, b, bl in zip(vs, gates, bs, bls)]
                if latent:
                    qs = [_silu(q_ref[r, :]) * q_scale for r in rows]
                    bms = [_chunk_row(b, mid) for b in bs]
                    a_s = [_mm_nt(q * jnp.exp(b - bm), g[2] * jnp.exp(bm - b)) for q, g, b, bm in zip(qs, gates, bs, bms)]
                    outs = [_mm(jnp.where(keep, a, 0.0), v) for a, v in zip(a_s, vs)]
                for j in range(per):
                    for c in range(2):
                        step = _scan_step_of(r0s[j] // CHUNK + c, rev, latent)
                        decay_ref[step] = jnp.exp(bls[j][c * CHUNK:c * CHUNK + 1, :])
                        st_ref[step] = ups[j][:, c * HEAD_DIM:(c + 1) * HEAD_DIM]
                    if latent:
                        orow = pl.ds(pl.multiple_of(r0s[j] - CTX_LEN, GROUP), GROUP)
                        qt_ref[orow, :] = (qs[j] * jnp.exp(bs[j])).astype(BF16)
                        if rev:
                            o_ref[orow, :] += outs[j]
                        else:
                            o_ref[orow, :] = outs[j]
                return carry

            lax.fori_loop(0, CTX_LEN // (per * GROUP), functools.partial(local_step, latent=False), 0)
            lax.fori_loop(0, SEQ // (per * GROUP), functools.partial(local_step, latent=True), 0)

            def scan_step(t, st):
                update = st_ref[t]
                st_ref[t] = st
                return st * decay_ref[t] + update

            lax.fori_loop(0, N_CHUNKS, scan_step, jnp.zeros((HEAD_DIM, HEAD_DIM), F32), unroll=2)

            def inter_step(i, carry):
                r0s = [_group_rows(i, j, per, True) for j in range(per)]
                orows = [pl.ds(pl.multiple_of(r0 - CTX_LEN, GROUP), GROUP) for r0 in r0s]
                states = [jnp.concatenate([st_ref[_scan_step_of(r0 // CHUNK + c, rev, True)] for c in range(2)], axis=0)
                          for r0 in r0s]
                prods = [lax.dot_general(qt_ref[orow, :], s.astype(BF16), (((1,), (1,)), ((), ())), preferred_element_type=F32)
                         for orow, s in zip(orows, states)]
                for orow, r in zip(orows, prods):
                    o_ref[orow, :] += _own_block(r)
                return carry

            lax.fori_loop(0, SEQ // (per * GROUP), inter_step, 0)

    return pl.pallas_call(
        body, name="gla_forward", grid=(HEADS,),
        out_shape=jax.ShapeDtypeStruct((SEQ, D_MODEL), F32),
        in_specs=[G_SPEC(0), G_SPEC(1), G_SPEC(2), G_SPEC(3), pl.BlockSpec((2, HEAD_DIM), lambda h: (0, h))],
        out_specs=pl.BlockSpec((SEQ, HEAD_DIM), lambda h: (0, h)),
        scratch_shapes=[pltpu.VMEM((N_CHUNKS, HEAD_DIM, HEAD_DIM), F32), pltpu.VMEM((N_CHUNKS, 1, HEAD_DIM), F32),
                        pltpu.VMEM((SEQ, HEAD_DIM), BF16)],
        compiler_params=_params(32),
    )(g5, g5, g5, g5, lb_logits)


def _gla_backward(g5, lb_logits, d_o, d_z):
    q_scale = HEAD_DIM ** -0.5
    per = GROUPS_PER_STEP

    def body(ff_ref, fb_ref, v_ref, q_ref, lg_ref, do_ref, dz_ref, dg_ref, dlg_ref, st_ref, dst_ref, decay_ref):
        dg_ref[3, 0:CTX_LEN, :] = jnp.zeros((CTX_LEN, HEAD_DIM), F32)
        dg_ref[4, 0:CTX_LEN, :] = jnp.zeros((CTX_LEN, HEAD_DIM), F32)
        dg_ref[4, CTX_LEN:ROWS_ALL, :] = dz_ref[...]
        second = lax.broadcasted_iota(jnp.int32, (GROUP, HEAD_DIM), 0) >= CHUNK
        for rev in (False, True):
            d = 1 if rev else 0
            f_ref = fb_ref if rev else ff_ref
            lb = _sig(lg_ref[d:d + 1, :])
            keep, tri, tri_t, strict = _group_masks(rev)
            last = 0 if rev else CHUNK - 1
            mid = CHUNK // 2 if rev else CHUNK // 2 - 1

            def local_step(i, carry, latent):
                r0s = [_group_rows(i, j, per, latent) for j in range(per)]
                rows = [pl.ds(r0, GROUP) for r0 in r0s]
                gates = [_gates(f_ref[r, :], lb) for r in rows]
                bs = [_group_sum(tri, g[3]) for g in gates]
                bls = [_chunk_row(b, last) for b in bs]
                ups = [_mm_tn(v_ref[r, :], _by_chunk(g[2] * jnp.exp(bl - b), second)) for r, g, b, bl in zip(rows, gates, bs, bls)]
                if latent:
                    orows = [pl.ds(pl.multiple_of(r0 - CTX_LEN, GROUP), GROUP) for r0 in r0s]
                    d_ups = [_mm_tn(do_ref[orow, :], _by_chunk(_silu(q_ref[r, :]) * q_scale * jnp.exp(b), second))
                             for orow, r, b in zip(orows, rows, bs)]
                for j in range(per):
                    for c in range(2):
                        step = _scan_step_of(r0s[j] // CHUNK + c, rev, latent)
                        decay_ref[step] = jnp.exp(bls[j][c * CHUNK:c * CHUNK + 1, :])
                        st_ref[step] = ups[j][:, c * HEAD_DIM:(c + 1) * HEAD_DIM]
                        if latent:
                            dst_ref[step] = d_ups[j][:, c * HEAD_DIM:(c + 1) * HEAD_DIM]
                        else:
                            dst_ref[step] = jnp.zeros((HEAD_DIM, HEAD_DIM), F32)
                return carry

            lax.fori_loop(0, CTX_LEN // (per * GROUP), functools.partial(local_step, latent=False), 0)
            lax.fori_loop(0, SEQ // (per * GROUP), functools.partial(local_step, latent=True), 0)

            def scan_step(i, carry):
                st, d_st = carry
                j = N_CHUNKS - 1 - i
                update, d_update = st_ref[i], dst_ref[j]
                st_ref[i] = st
                dst_ref[j] = d_st
                return st * decay_ref[i] + update, d_st * decay_ref[j] + d_update

            zero_state = jnp.zeros((HEAD_DIM, HEAD_DIM), F32)
            lax.fori_loop(0, N_CHUNKS, scan_step, (zero_state, zero_state))

            def grad_step(i, dlb, latent):
                r0s = [_group_rows(i, j, per, latent) for j in range(per)]
                rows = [pl.ds(r0, GROUP) for r0 in r0s]
                gates = [_gates(f_ref[r, :], lb) for r in rows]
                vs = [v_ref[r, :] for r in rows]
                bs = [_group_sum(tri, g[3]) for g in gates]
                bls = [_chunk_row(b, last) for b in bs]
                e_ends = [jnp.exp(bl - b) for b, bl in zip(bs, bls)]
                k_ends = [g[2] * e for g, e in zip(gates, e_ends)]
                sts = [[st_ref[_scan_step_of(r0 // CHUNK + c, rev, latent)] for c in range(2)] for r0 in r0s]
                d_sts = [[dst_ref[_scan_step_of(r0 // CHUNK + c, rev, latent)] for c in range(2)] for r0 in r0s]
                d_kends = [_own_block(_mm(v, jnp.concatenate(ds, axis=1))) for v, ds in zip(vs, d_sts)]
                d_vs = [_own_block(_mm_nt(ke, jnp.concatenate(ds, axis=0))) for ke, ds in zip(k_ends, d_sts)]
                at_last = [jnp.concatenate([jnp.broadcast_to(jnp.sum(ds[c] * s[c], axis=0, keepdims=True), (CHUNK, HEAD_DIM))
                                            for c in range(2)], axis=0) * jnp.exp(bl) for ds, s, bl in zip(d_sts, sts, bls)]
                t_kends = [dk * ke for dk, ke in zip(d_kends, k_ends)]
                d_ks = [dk * e for dk, e in zip(d_kends, e_ends)]
                d_lfs = [_group_sum(strict, t) + al for t, al in zip(t_kends, at_last)]
                if latent:
                    orows = [pl.ds(pl.multiple_of(r0 - CTX_LEN, GROUP), GROUP) for r0 in r0s]
                    qpres = [q_ref[r, :] for r in rows]
                    qs = [_silu(qp) * q_scale for qp in qpres]
                    bms = [_chunk_row(b, mid) for b in bs]
                    e_bs = [jnp.exp(b) for b in bs]
                    e_qms = [jnp.exp(b - bm) for b, bm in zip(bs, bms)]
                    e_kms = [jnp.exp(bm - b) for b, bm in zip(bs, bms)]
                    q_ts = [q * e for q, e in zip(qs, e_bs)]
                    q_ms = [q * e for q, e in zip(qs, e_qms)]
                    k_ms = [g[2] * e for g, e in zip(gates, e_kms)]
                    d_outs = [do_ref[orow, :] for orow in orows]
                    a_s = [jnp.where(keep, _mm_nt(qm, km), 0.0) for qm, km in zip(q_ms, k_ms)]
                    d_as = [jnp.where(keep, _mm_nt(do, v), 0.0) for do, v in zip(d_outs, vs)]
                    d_qts = [_own_block(_mm(do, jnp.concatenate(s, axis=1))) for do, s in zip(d_outs, sts)]
                    d_qms = [_mm(da, km) for da, km in zip(d_as, k_ms)]
                    d_kms = [_mm_tn(da, qm) for da, qm in zip(d_as, q_ms)]
                    d_vs = [dv + _mm_tn(a, do) for dv, a, do in zip(d_vs, a_s, d_outs)]
                    d_ks = [dk + dkm * e for dk, dkm, e in zip(d_ks, d_kms, e_kms)]
                    d_lfs = [dl + _group_sum(tri_t, dqt * qt + dqm * qm - dkm * km)
                             for dl, dqt, qt, dqm, qm, dkm, km in zip(d_lfs, d_qts, q_ts, d_qms, q_ms, d_kms, k_ms)]
                    d_qs = [dqt * eb + dqm * eq for dqt, eb, dqm, eq in zip(d_qts, e_bs, d_qms, e_qms)]
                for j in range(per):
                    sg, f = gates[j][0], gates[j][1]
                    d_f = d_lfs[j] / f - d_ks[j]
                    dg_ref[d, rows[j], :] = d_f * (1.0 - lb) * sg * (1.0 - sg)
                    dlb = dlb + jnp.sum(d_f * (1.0 - sg), axis=0, keepdims=True)
                    if rev:
                        dg_ref[2, rows[j], :] += d_vs[j]
                    else:
                        dg_ref[2, rows[j], :] = d_vs[j]
                    if latent:
                        d_qpre = d_qs[j] * q_scale * _dsilu(qpres[j])
                        if rev:
                            dg_ref[3, rows[j], :] += d_qpre
                        else:
                            dg_ref[3, rows[j], :] = d_qpre
                return dlb

            dlb = lax.fori_loop(0, SEQ // (per * GROUP), functools.partial(grad_step, latent=True), jnp.zeros((1, HEAD_DIM), F32))
            dlb = lax.fori_loop(0, CTX_LEN // (per * GROUP), functools.partial(grad_step, latent=False), dlb)
            dlg_ref[d:d + 1, :] = dlb * lb * (1.0 - lb)

    col = pl.BlockSpec((SEQ, HEAD_DIM), lambda h: (0, h))
    return pl.pallas_call(
        body, name="gla_backward", grid=(HEADS,),
        out_shape=[jax.ShapeDtypeStruct((HGRN_SECTIONS, ROWS_ALL, D_MODEL), F32), jax.ShapeDtypeStruct((2, D_MODEL), F32)],
        in_specs=[G_SPEC(0), G_SPEC(1), G_SPEC(2), G_SPEC(3), pl.BlockSpec((2, HEAD_DIM), lambda h: (0, h)), col, col],
        out_specs=[pl.BlockSpec((HGRN_SECTIONS, ROWS_ALL, HEAD_DIM), lambda h: (0, 0, h)),
                   pl.BlockSpec((2, HEAD_DIM), lambda h: (0, h))],
        scratch_shapes=[pltpu.VMEM((N_CHUNKS, HEAD_DIM, HEAD_DIM), F32), pltpu.VMEM((N_CHUNKS, HEAD_DIM, HEAD_DIM), F32),
                        pltpu.VMEM((N_CHUNKS, 1, HEAD_DIM), F32)],
        compiler_params=_params(48),
    )(g5, g5, g5, g5, lb_logits, d_o, d_z)


def _head_norm(o, gw, scr):
    rs = []
    for h in range(HEADS):
        cols = slice(h * HEAD_DIM, (h + 1) * HEAD_DIM)
        oh = o[:, cols]
        r = lax.rsqrt(jnp.mean(oh * oh, axis=-1, keepdims=True) + EPS)
        scr[:, cols] = oh * r
        rs.append(r)
    return rs


def _hgrn_out_forward(o_raw, g5, xin, gnorm_w, gate, w_out):
    tm = ROW_TILE

    def body(o_ref, z_ref, x_ref, gw_ref, gate_ref, w_ref, x1_ref, res_ref, scr):
        _head_norm(o_ref[...], None, scr)
        a = scr[...] * gw_ref[...] * _silu(z_ref[...])
        res = _mm(a, w_ref[...])
        res_ref[...] = res
        x1_ref[...] = x_ref[...] + gate_ref[...] * res

    tile = pl.BlockSpec((tm, D_MODEL), lambda i: (i, 0))
    vec = pl.BlockSpec((1, D_MODEL), lambda i: (0, 0))
    return pl.pallas_call(
        body, name="hgrn_out_forward", grid=(SEQ // tm,),
        out_shape=[jax.ShapeDtypeStruct((SEQ, D_MODEL), F32)] * 2,
        in_specs=[tile, pl.BlockSpec((None, tm, D_MODEL), lambda i: (4, i + CTX_LEN // tm, 0)), tile, vec, vec,
                  pl.BlockSpec((D_MODEL, D_MODEL), lambda i: (0, 0))],
        out_specs=[tile, tile],
        scratch_shapes=[pltpu.VMEM((tm, D_MODEL), F32)],
        compiler_params=_params(32),
    )(o_raw, g5, xin, gnorm_w, gate, w_out)


def _hgrn_out_backward(d_x1, o_raw, g5, res, gnorm_w, gate, w_out):
    tm = ROW_TILE

    def body(dx_ref, o_ref, z_ref, res_ref, gw_ref, gate_ref, w_ref, do_ref, dz_ref, dw_ref, dgate_ref, dgw_ref, scr, scr2):
        @pl.when(pl.program_id(0) == 0)
        def _():
            dw_ref[...] = jnp.zeros_like(dw_ref)
            dgate_ref[...] = jnp.zeros_like(dgate_ref)
            dgw_ref[...] = jnp.zeros_like(dgw_ref)

        dx = dx_ref[...]
        dgate_ref[...] += jnp.sum(dx * res_ref[...], axis=0, keepdims=True)
        d_res = (dx * gate_ref[...]).astype(BF16)
        d_a = _mm_nt(d_res, w_ref[...])
        rs = _head_norm(o_ref[...], None, scr)
        z = z_ref[...]
        sz = _silu(z)
        o_hat = scr[...]
        o_n = o_hat * gw_ref[...]
        dw_ref[...] += _mm_tn(o_n * sz, d_res)
        d_on = d_a * sz
        dz_ref[...] = d_a * o_n * _dsilu(z)
        dgw_ref[...] += jnp.sum(d_on * o_hat, axis=0, keepdims=True)
        scr2[...] = d_on * gw_ref[...]
        for h in range(HEADS):
            cols = slice(h * HEAD_DIM, (h + 1) * HEAD_DIM)
            dh, oh = scr2[:, cols], scr[:, cols]
            do_ref[:, cols] = rs[h] * (dh - oh * jnp.mean(dh * oh, axis=-1, keepdims=True))

    tile = pl.BlockSpec((tm, D_MODEL), lambda i: (i, 0))
    vec = pl.BlockSpec((1, D_MODEL), lambda i: (0, 0))
    mat = pl.BlockSpec((D_MODEL, D_MODEL), lambda i: (0, 0))
    return pl.pallas_call(
        body, name="hgrn_out_backward", grid=(SEQ // tm,),
        out_shape=[jax.ShapeDtypeStruct((SEQ, D_MODEL), F32)] * 2 + [jax.ShapeDtypeStruct((D_MODEL, D_MODEL), F32)]
        + [jax.ShapeDtypeStruct((1, D_MODEL), F32)] * 2,
        in_specs=[tile, tile, pl.BlockSpec((None, tm, D_MODEL), lambda i: (4, i + CTX_LEN // tm, 0)), tile, vec, vec, mat],
        out_specs=[tile, tile, mat, vec, vec],
        scratch_shapes=[pltpu.VMEM((tm, D_MODEL), F32)] * 2,
        compiler_params=_params(40),
    )(d_x1, o_raw, g5, res, gnorm_w, gate, w_out)


def _pool_constants():
    win = np.zeros((POOL_GROUPS, ROW_TILE, ROW_TILE), np.float32)
    inv = np.zeros((POOL_GROUPS, ROW_TILE, 1), np.float32)
    for g, w in enumerate(POOL_WINDOWS):
        for t in range(ROW_TILE):
            base, p = (t // GRID_W) * GRID_W, t % GRID_W
            lo = min(max(p - w // 2, 0), GRID_W)
            hi = min(max(p - w // 2 + w, 0), GRID_W)
            win[g, t, base + lo:base + hi] = 1.0
            inv[g, t, 0] = 1.0 / np.float32(hi - lo)
    return jnp.asarray(win, BF16), jnp.asarray(win.transpose(0, 2, 1), BF16), jnp.asarray(inv, F32)


def _pool_mix(u_ref, wg_ref, ps_ref, win_ref, inv_ref, pooled_scr, yg_scr):
    for g in range(POOL_GROUPS):
        cols = slice(g * POOL_GROUP_DIM, (g + 1) * POOL_GROUP_DIM)
        ug = u_ref[:, cols]
        pooled = _mm_exact_lhs(win_ref[g], ug) * inv_ref[g] - ug
        if pooled_scr is not None:
            pooled_scr[:, cols] = pooled
        yg_scr[:, cols] = _mm(pooled, wg_ref[g])


def _pool_forward_loss(uz, x1, target, gate, w_grp, pool_scale, w_out, final_w):
    tm = ROW_TILE
    win, _, inv = _pool_constants()

    def body(u_ref, z_ref, x_ref, t_ref, gate_ref, wg_ref, ps_ref, w_ref, fw_ref, win_ref, inv_ref,
             dx_ref, loss_ref, dfw_ref, dgate_ref, yg_scr):
        @pl.when(pl.program_id(0) == 0)
        def _():
            loss_ref[...] = jnp.zeros_like(loss_ref)
            dfw_ref[...] = jnp.zeros_like(dfw_ref)
            dgate_ref[...] = jnp.zeros_like(dgate_ref)

        _pool_mix(u_ref, wg_ref, ps_ref, win_ref, inv_ref, None, yg_scr)
        a = yg_scr[...] * ps_ref[...] * _silu(z_ref[...])
        res = _mm(a, w_ref[...])
        x2 = x_ref[...] + gate_ref[...] * res
        r = lax.rsqrt(jnp.mean(x2 * x2, axis=-1, keepdims=True) + EPS)
        xh = x2 * r
        fw = fw_ref[...]
        err = xh * fw - t_ref[...]
        loss_ref[...] += 0.5 * jnp.sum(jnp.mean(err * err, axis=-1, keepdims=True))
        d_y = err * (1.0 / D_MODEL)
        dfw_ref[...] += jnp.sum(d_y * xh, axis=0, keepdims=True)
        d_xh = d_y * fw
        d_x2 = r * (d_xh - xh * jnp.mean(d_xh * xh, axis=-1, keepdims=True))
        dx_ref[...] = d_x2
        dgate_ref[...] += jnp.sum(d_x2 * res, axis=0, keepdims=True)

    tile = pl.BlockSpec((tm, D_MODEL), lambda i: (i, 0))
    vec = pl.BlockSpec((1, D_MODEL), lambda i: (0, 0))
    grp = pl.BlockSpec((POOL_GROUPS, POOL_GROUP_DIM, POOL_GROUP_DIM), lambda i: (0, 0, 0))
    return pl.pallas_call(
        body, name="pool_forward_loss", grid=(SEQ // tm,),
        out_shape=[jax.ShapeDtypeStruct((SEQ, D_MODEL), F32), jax.ShapeDtypeStruct((8, 128), F32),
                   jax.ShapeDtypeStruct((1, D_MODEL), F32), jax.ShapeDtypeStruct((1, D_MODEL), F32)],
        in_specs=[pl.BlockSpec((None, tm, D_MODEL), lambda i: (0, i, 0)), pl.BlockSpec((None, tm, D_MODEL), lambda i: (1, i, 0)),
                  tile, tile, vec, grp, vec, pl.BlockSpec((D_MODEL, D_MODEL), lambda i: (0, 0)), vec, grp,
                  pl.BlockSpec((POOL_GROUPS, ROW_TILE, 1), lambda i: (0, 0, 0))],
        out_specs=[tile, pl.BlockSpec((8, 128), lambda i: (0, 0)), vec, vec],
        scratch_shapes=[pltpu.VMEM((tm, D_MODEL), F32)],
        compiler_params=_params(32),
    )(uz, uz, x1, target, gate, w_grp, pool_scale, w_out, final_w, win, inv)


def _pool_backward(d_x2, uz, gate, w_grp, pool_scale, w_out):
    tm = ROW_TILE
    win, win_t, inv = _pool_constants()

    def body(dx_ref, u_ref, z_ref, gate_ref, wg_ref, ps_ref, w_ref, win_ref, wint_ref, inv_ref,
             duz_ref, dw_ref, dwg_ref, dps_ref, pooled_scr, yg_scr, dyg_scr):
        @pl.when(pl.program_id(0) == 0)
        def _():
            dw_ref[...] = jnp.zeros_like(dw_ref)
            dwg_ref[...] = jnp.zeros_like(dwg_ref)
            dps_ref[...] = jnp.zeros_like(dps_ref)

        _pool_mix(u_ref, wg_ref, ps_ref, win_ref, inv_ref, pooled_scr, yg_scr)
        z = z_ref[...]
        sz = _silu(z)
        yg = yg_scr[...]
        y = yg * ps_ref[...]
        d_res = (dx_ref[...] * gate_ref[...]).astype(BF16)
        d_a = _mm_nt(d_res, w_ref[...])
        dw_ref[...] += _mm_tn(y * sz, d_res)
        d_y = d_a * sz
        duz_ref[1] = d_a * y * _dsilu(z)
        dps_ref[...] += jnp.sum(d_y * yg, axis=0, keepdims=True)
        dyg_scr[...] = d_y * ps_ref[...]
        for g in range(POOL_GROUPS):
            cols = slice(g * POOL_GROUP_DIM, (g + 1) * POOL_GROUP_DIM)
            d_yg = dyg_scr[:, cols].astype(BF16)
            d_pool = _mm_nt(d_yg, wg_ref[g])
            dwg_ref[g] += _mm_tn(pooled_scr[:, cols], d_yg)
            duz_ref[0, :, cols] = _mm_exact_lhs(wint_ref[g], d_pool * inv_ref[g]) - d_pool

    tile = pl.BlockSpec((tm, D_MODEL), lambda i: (i, 0))
    vec = pl.BlockSpec((1, D_MODEL), lambda i: (0, 0))
    mat = pl.BlockSpec((D_MODEL, D_MODEL), lambda i: (0, 0))
    grp = pl.BlockSpec((POOL_GROUPS, POOL_GROUP_DIM, POOL_GROUP_DIM), lambda i: (0, 0, 0))
    return pl.pallas_call(
        body, name="pool_backward", grid=(SEQ // tm,),
        out_shape=[jax.ShapeDtypeStruct((POOL_SECTIONS, SEQ, D_MODEL), F32), jax.ShapeDtypeStruct((D_MODEL, D_MODEL), F32),
                   jax.ShapeDtypeStruct((POOL_GROUPS, POOL_GROUP_DIM, POOL_GROUP_DIM), F32), jax.ShapeDtypeStruct((1, D_MODEL), F32)],
        in_specs=[tile, pl.BlockSpec((None, tm, D_MODEL), lambda i: (0, i, 0)), pl.BlockSpec((None, tm, D_MODEL), lambda i: (1, i, 0)),
                  vec, grp, vec, mat, grp, grp, pl.BlockSpec((POOL_GROUPS, ROW_TILE, 1), lambda i: (0, 0, 0))],
        out_specs=[pl.BlockSpec((POOL_SECTIONS, tm, D_MODEL), lambda i: (0, i, 0)), mat, grp, vec],
        scratch_shapes=[pltpu.VMEM((tm, D_MODEL), F32)] * 3,
        compiler_params=_params(40),
    )(d_x2, uz, uz, gate, w_grp, pool_scale, w_out, win, win_t, inv)


def _ln_mod_backward(d_g, w, xin, nw, scale, d_up, name):
    n_sec, rows, _ = d_g.shape
    n_mod = scale.shape[0]
    tm = ROW_TILE
    skip = n_mod - 1

    def body(dg_ref, w_ref, x_ref, nw_ref, sc_ref, up_ref, dx_ref, dnw_ref, dmod_ref, acc):
        i, k = pl.program_id(0), pl.program_id(1)

        @pl.when(k == 0)
        def _():
            acc[...] = jnp.zeros_like(acc)

        acc[...] += _mm_nt(dg_ref[...], w_ref[...])

        @pl.when(k == n_sec - 1)
        def _():
            @pl.when(i == 0)
            def _():
                dnw_ref[...] = jnp.zeros_like(dnw_ref)

            @pl.when((i == 0) | (i == skip))
            def _():
                dmod_ref[...] = jnp.zeros_like(dmod_ref)

            d_h = acc[...]
            xv = x_ref[...]
            r = lax.rsqrt(jnp.mean(xv * xv, axis=-1, keepdims=True) + EPS)
            xh = xv * r
            nw_row = nw_ref[...]
            dmod_ref[0:1, :] += jnp.sum(d_h, axis=0, keepdims=True)
            dmod_ref[1:2, :] += jnp.sum(d_h * (xh * nw_row), axis=0, keepdims=True)
            d_xn = d_h * (1.0 + sc_ref[...])
            dnw_ref[...] += jnp.sum(d_xn * xh, axis=0, keepdims=True)
            d_xh = d_xn * nw_row

            @pl.when(i >= skip)
            def _():
                dx_ref[...] = up_ref[...] + r * (d_xh - xh * jnp.mean(d_xh * xh, axis=-1, keepdims=True))

    lat = lambda i, k: (jnp.maximum(i - skip, 0), 0)
    mod_idx = lambda i, k: (jnp.minimum(i, n_mod - 1), 0, 0)
    return pl.pallas_call(
        body, name=name, grid=(rows // tm, n_sec),
        out_shape=[jax.ShapeDtypeStruct((rows - skip * tm, D_MODEL), F32), jax.ShapeDtypeStruct((1, D_MODEL), F32),
                   jax.ShapeDtypeStruct((n_mod, 8, D_MODEL), F32)],
        in_specs=[pl.BlockSpec((None, tm, D_MODEL), lambda i, k: (k, i, 0)),
                  pl.BlockSpec((D_MODEL, D_MODEL), lambda i, k: (0, k)),
                  pl.BlockSpec((tm, D_MODEL), lambda i, k: (i, 0)),
                  pl.BlockSpec((1, D_MODEL), lambda i, k: (0, 0)),
                  pl.BlockSpec((None, 1, D_MODEL), mod_idx),
                  pl.BlockSpec((tm, D_MODEL), lat)],
        out_specs=[pl.BlockSpec((tm, D_MODEL), lat), pl.BlockSpec((1, D_MODEL), lambda i, k: (0, 0)),
                   pl.BlockSpec((None, 8, D_MODEL), mod_idx)],
        scratch_shapes=[pltpu.VMEM((tm, D_MODEL), F32)],
        compiler_params=_params(32),
    )(d_g, w, xin, nw, scale.reshape(n_mod, 1, D_MODEL), d_up)


def _weight_grad(h, d_g, name):
    n_sec, rows, _ = d_g.shape
    tm = 768 if rows % 768 == 0 else 512

    def body(h_ref, dg_ref, dw_ref):
        @pl.when(pl.program_id(1) == 0)
        def _():
            dw_ref[...] = jnp.zeros_like(dw_ref)

        dw_ref[...] += _mm_tn(h_ref[...], dg_ref[...])

    return pl.pallas_call(
        body, name=name, grid=(n_sec, rows // tm),
        out_shape=jax.ShapeDtypeStruct((D_MODEL, n_sec * D_MODEL), F32),
        in_specs=[pl.BlockSpec((tm, D_MODEL), lambda j, i: (i, 0)), pl.BlockSpec((None, tm, D_MODEL), lambda j, i: (j, i, 0))],
        out_specs=pl.BlockSpec((D_MODEL, D_MODEL), lambda j, i: (0, j)),
        compiler_params=_params(32),
    )(h, d_g)


def _sum_slots(slots, name):
    _, rows, cols = slots.shape
    tr = 128

    def body(s_ref, o_ref):
        acc = s_ref[0].astype(F32)
        for d in range(1, N_DEV):
            acc = acc + s_ref[d].astype(F32)
        o_ref[...] = acc

    return pl.pallas_call(
        body, name=name, grid=(rows // tr,),
        out_shape=jax.ShapeDtypeStruct((rows, cols), F32),
        in_specs=[pl.BlockSpec((N_DEV, tr, cols), lambda i: (0, i, 0))],
        out_specs=pl.BlockSpec((tr, cols), lambda i: (i, 0)),
    )(slots)


def _adamw_math(w, g, m, v):
    m = ADAM_B1 * m + (1.0 - ADAM_B1) * g
    v = ADAM_B2 * v + (1.0 - ADAM_B2) * (g * g)
    m_hat = m / (1.0 - ADAM_B1 ** ADAM_STEP)
    v_hat = v / (1.0 - ADAM_B2 ** ADAM_STEP)
    return -ADAM_LR * (m_hat / (jnp.sqrt(v_hat) + ADAM_EPS) + ADAM_WD * w), m, v


def _adamw(w, g, m, v, name):
    rows, cols = w.shape
    tr = rows if rows <= 128 else 128

    def body(w_ref, g_ref, m_ref, v_ref, d_ref, mo_ref, vo_ref):
        d_ref[...], mo_ref[...], vo_ref[...] = _adamw_math(w_ref[...], g_ref[...], m_ref[...], v_ref[...])

    tile = pl.BlockSpec((tr, cols), lambda i: (i, 0))
    return pl.pallas_call(
        body, name=name, grid=(rows // tr,),
        out_shape=[jax.ShapeDtypeStruct((rows, cols), F32)] * 3,
        in_specs=[tile] * 4, out_specs=[tile] * 3,
    )(w, g, m, v)


def _sum_devices(gathered):
    def body(p_ref, o_ref):
        acc = p_ref[0:SMALL_ROWS, :]
        for d in range(1, N_DEV):
            acc = acc + p_ref[d * SMALL_ROWS:(d + 1) * SMALL_ROWS, :]
        o_ref[...] = acc

    return pl.pallas_call(body, name="sum_devices", out_shape=jax.ShapeDtypeStruct((SMALL_ROWS, D_MODEL), F32),
                          in_specs=[VMEM], out_specs=VMEM)(gathered)


def _ada_update(cond_t, d_mod, ada_w, m, v):
    n_layers, _, n_cols = ada_w.shape
    tr = ROW_TILE

    def body(c_ref, dm_ref, w_ref, m_ref, v_ref, g_ref, d_ref, mo_ref, vo_ref):
        g = _mm_f32(_silu(c_ref[...]), dm_ref[...])
        g_ref[...] = g
        d_ref[...], mo_ref[...], vo_ref[...] = _adamw_math(w_ref[...], g, m_ref[...], v_ref[...])

    tile = pl.BlockSpec((None, tr, n_cols), lambda l, i: (l, i, 0))
    return pl.pallas_call(
        body, name="ada_update", grid=(n_layers, D_MODEL // tr),
        out_shape=[jax.ShapeDtypeStruct(ada_w.shape, F32)] * 4,
        in_specs=[pl.BlockSpec((tr, 16), lambda l, i: (i, 0)), pl.BlockSpec((None, 16, n_cols), lambda l, i: (l, 0, 0)),
                  tile, tile, tile],
        out_specs=[tile] * 4,
    )(cond_t, d_mod, ada_w, m, v)


def _cond_ctx_partial(d_modc, ada_w0):
    n_cols = ada_w0.shape[1]
    tr = ROW_TILE

    def body(dm_ref, w_ref, o_ref):
        o_ref[...] = jnp.sum(w_ref[...] * dm_ref[...], axis=-1, keepdims=True)

    return pl.pallas_call(
        body, name="cond_ctx_partial", grid=(D_MODEL // tr,),
        out_shape=jax.ShapeDtypeStruct((D_MODEL, 1), F32),
        in_specs=[pl.BlockSpec((1, n_cols), lambda i: (0, 0)), pl.BlockSpec((tr, n_cols), lambda i: (i, 0))],
        out_specs=pl.BlockSpec((tr, 1), lambda i: (i, 0)),
    )(d_modc, ada_w0)


def _cond_ctx_update(gathered, c_ctx, m, v):
    def body(p_ref, w_ref, m_ref, v_ref, g_ref, d_ref, mo_ref, vo_ref):
        acc = p_ref[0:1, :]
        for s in range(1, N_CHIPS):
            acc = acc + p_ref[16 * s:16 * s + 1, :]
        w = w_ref[...]
        g = acc * _dsilu(w)
        g_ref[...] = g
        d_ref[...], mo_ref[...], vo_ref[...] = _adamw_math(w, g, m_ref[...], v_ref[...])

    return pl.pallas_call(body, name="cond_ctx_update", out_shape=[jax.ShapeDtypeStruct((1, D_MODEL), F32)] * 4,
                          in_specs=[VMEM] * 4, out_specs=[VMEM] * 4)(gathered, c_ctx, m, v)


def _local_step(x2, ctx2, target, mod_mine, mod_ctx, lb_logits, scale_full, full, norm_w, gnorm, final_w):
    row = lambda a: a.reshape(1, -1)
    shift0, scale0, gate0 = (row(a) for a in jnp.split(mod_mine[0], 3))
    shift1, scale1, gate1 = (row(a) for a in jnp.split(mod_mine[1], 3))
    shift_c, scale_c, _ = (row(a) for a in jnp.split(mod_ctx, 3))
    xa = jnp.concatenate([ctx2, x2], axis=0)
    nw0, nw1 = norm_w[0:1], norm_w[1:2]
    scales0 = jnp.concatenate([scale_c, scale0])

    g5, h0 = _ln_mod_matmul(xa, nw0, jnp.concatenate([shift_c, shift0]), scales0, full["hgrn_w_in"], "hgrn_in_forward")
    o_raw = _gla_forward(g5, lb_logits)
    x1, res0 = _hgrn_out_forward(o_raw, g5, x2, gnorm, gate0, full["hgrn_w_out"])
    uz, h1 = _ln_mod_matmul(x1, nw1, shift1, scale1, full["pool_w_in"], "pool_in_forward")
    d_x2, loss_part, d_final, d_gate1 = _pool_forward_loss(uz, x1, target, gate1, full["pool_w_grp"], scale_full,
                                                           full["pool_w_out"], final_w)

    d_uz, dw_pool_out, dw_pool_grp, d_pscale = _pool_backward(d_x2, uz, gate1, full["pool_w_grp"], scale_full, full["pool_w_out"])
    d_x1, d_nw1, d_mod1 = _ln_mod_backward(d_uz, full["pool_w_in"], x1, nw1, scale1, d_x2, "pool_in_backward")
    dw_pool_in = _weight_grad(h1, d_uz, "pool_in_weight_grad")
    d_o, d_z, dw_hgrn_out, d_gate0, d_gnorm = _hgrn_out_backward(d_x1, o_raw, g5, res0, gnorm, gate0, full["hgrn_w_out"])
    d_g5, d_lb = _gla_backward(g5, lb_logits, d_o, d_z)
    d_x, d_nw0, d_mod0 = _ln_mod_backward(d_g5, full["hgrn_w_in"], xa, nw0, scales0, d_x1, "hgrn_in_backward")
    dw_hgrn_in = _weight_grad(h0, d_g5, "hgrn_in_weight_grad")

    zero = jnp.zeros((1, D_MODEL), F32)
    small = jnp.concatenate([d_mod0[1, 0:2], d_gate0, d_mod1[0, 0:2], d_gate1, d_mod0[0, 0:2], zero, d_nw0, d_nw1, d_gnorm,
                             d_final, d_pscale, d_lb, jnp.broadcast_to(loss_part[0:1, 0:1], (1, D_MODEL)),
                             jnp.zeros((SMALL_ROWS - 17, D_MODEL), F32)], axis=0)
    grads = {"hgrn_w_in": dw_hgrn_in, "hgrn_w_out": dw_hgrn_out, "pool_w_in": dw_pool_in, "pool_w_grp": dw_pool_grp,
             "pool_w_out": dw_pool_out}
    return {"d_x": d_x, "grads": grads, "small": small}


def kernel(x, c, ctx, c_ctx, ada_w, ada_b, norm_w, hgrn_w_in, hgrn_lb_logits, hgrn_gnorm_w, hgrn_w_out, pool_w_in, pool_w_grp, pool_scale, pool_w_out, final_norm_w, loss_target, m_c_ctx, m_ada_w, m_ada_b, m_norm_w, m_hgrn_w_in, m_hgrn_lb_logits, m_hgrn_gnorm_w, m_hgrn_w_out, m_pool_w_in, m_pool_w_grp, m_pool_scale, m_pool_w_out, m_final_norm_w, v_c_ctx, v_ada_w, v_ada_b, v_norm_w, v_hgrn_w_in, v_hgrn_lb_logits, v_hgrn_gnorm_w, v_hgrn_w_out, v_pool_w_in, v_pool_w_grp, v_pool_scale, v_pool_w_out, v_final_norm_w):
    xi, yi, ci = _my_place()
    chip = 2 * xi + yi
    dev = 4 * xi + 2 * yi + ci
    ada_cols = ada_w.shape[2]
    lb_cols = hgrn_lb_logits.shape[2]
    ps_cols = pool_scale.shape[1]
    row = lambda a: a.reshape(1, -1)

    def chip_cols(a, n):
        return lax.dynamic_slice_in_dim(a, chip * n, n, axis=a.ndim - 1)

    def from_chips(g, rows_per_dev, take):
        return jnp.concatenate([g[2 * s * rows_per_dev:2 * s * rows_per_dev + take] for s in range(N_CHIPS)], axis=1)

    first = jnp.concatenate([jnp.broadcast_to(c, (8, D_MODEL)), jnp.pad(hgrn_lb_logits[0], ((0, 6), (0, 0))),
                             jnp.pad(pool_scale, ((0, 7), (0, 0)))], axis=1)
    first_all = _all_gather_small(first, "gather_cond")
    cond_all = first_all[::8, :D_MODEL]
    lb_logits = from_chips(first_all[:, D_MODEL:D_MODEL + lb_cols], 8, 2)
    scale_full = from_chips(first_all[:, D_MODEL + lb_cols:], 8, 1)
    cond_rows = jnp.concatenate([cond_all, row(c_ctx), jnp.zeros((7, D_MODEL), F32)], axis=0)

    parts = _mod_parts(cond_rows, ada_w, chip_cols(ada_b, ada_cols))
    parts_all = _all_gather_small(parts.reshape(32, ada_cols), "gather_mod")
    mod_all = from_chips(parts_all, 32, 32).reshape(2, 16, 3 * D_MODEL)
    mod_mine = lax.dynamic_index_in_dim(mod_all, dev, axis=1, keepdims=False)

    shards = {"hgrn_w_in": hgrn_w_in[0], "hgrn_w_out": hgrn_w_out[0], "pool_w_in": pool_w_in[0],
              "pool_w_grp": pool_w_grp[0], "pool_w_out": pool_w_out[0]}
    full = _all_gather_weights([shards[k].astype(BF16) for k in W_NAMES])

    loc = _local_step(x[0], ctx[0], loss_target[0], mod_mine, mod_all[0, 8], lb_logits, scale_full, full, norm_w,
                      hgrn_gnorm_w, row(final_norm_w))

    small_all = _all_gather_small(loc["small"], "gather_small")
    sums = _sum_devices(small_all)
    loss = sums[16, 0]

    slots = _scatter_grads([loc["grads"][k].astype(BF16) for k in W_NAMES])
    halves = []
    for k, s in zip(W_NAMES, slots):
        hs = _half_shape(k)
        flat = _sum_slots(s.reshape(N_DEV, -1, hs[-1]), "sum_" + k)
        halves.append(flat.reshape(hs))
    big_grads = dict(zip(W_NAMES, _exchange_halves(halves)))

    out = {}

    def update(name, w, g, m, v):
        shape = w.shape
        w2, g2, m2, v2 = (a.reshape(-1, shape[-1]) for a in (w, g, m, v))
        d, mn, vn = _adamw(w2, g2, m2, v2, "adamw_" + name)
        out[name] = tuple(a.reshape(shape) for a in (g2, d, mn, vn))

    moments = {"hgrn_w_in": (m_hgrn_w_in, v_hgrn_w_in), "hgrn_w_out": (m_hgrn_w_out, v_hgrn_w_out),
               "pool_w_in": (m_pool_w_in, v_pool_w_in), "pool_w_grp": (m_pool_w_grp, v_pool_w_grp),
               "pool_w_out": (m_pool_w_out, v_pool_w_out)}
    weights = {"hgrn_w_in": hgrn_w_in, "hgrn_w_out": hgrn_w_out, "pool_w_in": pool_w_in, "pool_w_grp": pool_w_grp,
               "pool_w_out": pool_w_out}
    for k in W_NAMES:
        update(k, weights[k], big_grads[k].reshape(weights[k].shape), *moments[k])

    g_ada_b = jnp.stack([(sums[0:3] + sums[6:9]).reshape(-1), sums[3:6].reshape(-1)])
    update("ada_b", ada_b, g_ada_b, m_ada_b, v_ada_b)
    update("norm_w", norm_w, sums[9:11], m_norm_w, v_norm_w)
    update("hgrn_gnorm_w", hgrn_gnorm_w, sums[11:12], m_hgrn_gnorm_w, v_hgrn_gnorm_w)
    update("final_norm_w", row(final_norm_w), sums[12:13], row(m_final_norm_w), row(v_final_norm_w))
    update("pool_scale", pool_scale, chip_cols(sums[13:14], ps_cols), m_pool_scale, v_pool_scale)
    update("hgrn_lb_logits", hgrn_lb_logits, chip_cols(sums[14:16], lb_cols)[None], m_hgrn_lb_logits, v_hgrn_lb_logits)

    per_dev = small_all.reshape(N_DEV, SMALL_ROWS, D_MODEL)
    pad7 = jnp.zeros((7, 3 * D_MODEL), F32)
    dm0 = jnp.concatenate([per_dev[:, 0:3].reshape(N_DEV, -1), sums[6:9].reshape(1, -1), pad7], axis=0)
    dm1 = jnp.concatenate([per_dev[:, 3:6].reshape(N_DEV, -1), jnp.zeros((8, 3 * D_MODEL), F32)], axis=0)
    d_mod = chip_cols(jnp.stack([dm0, dm1]), ada_cols)
    out["ada_w"] = _ada_update(cond_rows.T, d_mod, ada_w, m_ada_w, v_ada_w)

    part_c = _cond_ctx_partial(chip_cols(sums[6:9].reshape(1, -1), ada_cols), ada_w[0])
    part_all = _all_gather_small(jnp.pad(part_c.reshape(1, D_MODEL), ((0, 7), (0, 0))), "gather_cond_ctx")
    g_c, d_c, m_c, v_c = _cond_ctx_update(part_all, row(c_ctx), row(m_c_ctx), row(v_c_ctx))
    out["c_ctx"] = tuple(a.reshape(-1) for a in (g_c, d_c, m_c, v_c))
    out["final_norm_w"] = tuple(a.reshape(-1) for a in out["final_norm_w"])

    names = ["c_ctx", "ada_w", "ada_b", "norm_w", "hgrn_w_in", "hgrn_lb_logits", "hgrn_gnorm_w", "hgrn_w_out", "pool_w_in",
             "pool_w_grp", "pool_scale", "pool_w_out", "final_norm_w"]
    return (loss, loc["d_x"][None], *[out[k][0] for k in names], *[out[k][1] for k in names], *[out[k][2] for k in names],
            *[out[k][3] for k in names])
```

```python
import functools

import numpy as np
import jax
import jax.numpy as jnp
from jax import lax
from jax.experimental import pallas as pl
from jax.experimental.pallas import tpu as pltpu

F32 = jnp.float32
BF16 = jnp.bfloat16

D_MODEL = 1024
SEQ = 2048
CTX_LEN = 256
ROWS_ALL = CTX_LEN + SEQ
HEADS = 8
HEAD_DIM = 128
CHUNK = 64
N_CTX_CHUNKS = CTX_LEN // CHUNK
N_LAT_CHUNKS = SEQ // CHUNK
N_CHUNKS = N_CTX_CHUNKS + N_LAT_CHUNKS
GRID_W = 64
POOL_WINDOWS = (2, 4, 8, 16)
POOL_GROUPS = 4
POOL_GROUP_DIM = 256
HGRN_SECTIONS = 5
POOL_SECTIONS = 2
EPS = 1e-6
N_DEV = 8
N_CHIPS = 4
ROW_TILE = 256
SMALL_ROWS = 24

ADAM_LR = 0.001
ADAM_B1 = 0.9
ADAM_B2 = 0.999
ADAM_EPS = 1e-08
ADAM_WD = 0.01
ADAM_STEP = 10

MESH = pl.DeviceIdType.MESH
MIB = 1 << 20
ANY = pl.BlockSpec(memory_space=pl.ANY)
VMEM = pl.BlockSpec(memory_space=pltpu.VMEM)


def _params(vmem_mib=None):
    if vmem_mib is None:
        return pltpu.CompilerParams()
    return pltpu.CompilerParams(vmem_limit_bytes=vmem_mib * MIB)


def _sig(a):
    return 1.0 / (1.0 + jnp.exp(-a))


def _silu(a):
    return a * _sig(a)


def _dsilu(a):
    s = _sig(a)
    return s * (1.0 + a * (1.0 - s))


def _mm(a, b):
    return jnp.dot(a.astype(BF16), b.astype(BF16), preferred_element_type=F32)


def _mm_nt(a, b):
    return lax.dot_general(a.astype(BF16), b.astype(BF16), (((1,), (1,)), ((), ())), preferred_element_type=F32)


def _mm_tn(a, b):
    return lax.dot_general(a.astype(BF16), b.astype(BF16), (((0,), (0,)), ((), ())), preferred_element_type=F32)


def _split2(a):
    hi = a.astype(BF16)
    lo = (a - hi.astype(F32)).astype(BF16)
    return hi, lo


def _mm_exact_lhs(m_bf, a):
    hi, lo = _split2(a)
    return jnp.dot(m_bf, hi, preferred_element_type=F32) + jnp.dot(m_bf, lo, preferred_element_type=F32)


def _mm_f32(a, b):
    ah, al = _split2(a)
    bh, bl = _split2(b)
    return (jnp.dot(ah, bh, preferred_element_type=F32) + jnp.dot(al, bh, preferred_element_type=F32)
            + jnp.dot(ah, bl, preferred_element_type=F32))


def _my_place():
    return lax.axis_index("x"), lax.axis_index("y"), lax.axis_index("c")


def _all_gather_small(blk, name):
    m_per, n = blk.shape

    def body(x_ref, out_ref, send_sems, recv_sems, local_sem):
        x, y, c = _my_place()
        me, sibling = (x, y, c), (x, y, 1 - c)
        chips = [(1 - x, y), (x, 1 - y), (1 - x, 1 - y)]

        def rows(px, py, pc):
            return out_ref.at[pl.ds((4 * px + 2 * py + pc) * m_per, m_per), :]

        def copy(k, block, to, src=None):
            return pltpu.make_async_remote_copy(
                src_ref=rows(*block) if src is None else src, dst_ref=rows(*block),
                send_sem=send_sems.at[k], recv_sem=recv_sems.at[k], device_id=to, device_id_type=MESH)

        mine = pltpu.make_async_copy(x_ref, rows(*me), local_sem)
        mine.start()
        first = [copy(0, me, sibling, src=x_ref)]
        first += [copy(1 + j, me, (*chip, c), src=x_ref) for j, chip in enumerate(chips)]
        for cp in first:
            cp.start()
        passed = [copy(4 + j, (*chip, c), sibling) for j, chip in enumerate(chips)]
        for j, chip in enumerate(chips):
            copy(1 + j, (*chip, c), me).wait_recv()
            passed[j].start()
        copy(0, sibling, me).wait_recv()
        for j, chip in enumerate(chips):
            copy(4 + j, (*chip, 1 - c), me).wait_recv()
        for cp in first + passed:
            cp.wait_send()
        mine.wait()

    return pl.pallas_call(
        body, name=name,
        out_shape=jax.ShapeDtypeStruct((N_DEV * m_per, n), blk.dtype),
        in_specs=[VMEM], out_specs=VMEM,
        scratch_shapes=[pltpu.SemaphoreType.DMA((7,)), pltpu.SemaphoreType.DMA((7,)), pltpu.SemaphoreType.DMA],
    )(blk)


W_SPECS = {
    "hgrn_w_in": ((D_MODEL, 5 * D_MODEL), (1, 1280, 0, 512)),
    "hgrn_w_out": ((D_MODEL, D_MODEL), (0, 256, 0, 128)),
    "pool_w_in": ((D_MODEL, 2 * D_MODEL), (1, 512, 0, 512)),
    "pool_w_grp": ((POOL_GROUPS, POOL_GROUP_DIM, POOL_GROUP_DIM), (1, 64, 1, 32)),
    "pool_w_out": ((D_MODEL, D_MODEL), (0, 256, 0, 128)),
}
W_NAMES = tuple(W_SPECS)


def _al(v, m):
    return pl.multiple_of(v, m)


def _region(ref, spec, chip, half):
    ca, cn, ha, hn = spec
    idx = [slice(None)] * len(ref.shape)
    if ca == ha:
        if half is None:
            idx[ca] = pl.ds(_al(chip * cn, cn), cn)
        else:
            idx[ca] = pl.ds(_al(chip * cn + half * hn, hn), hn)
    else:
        idx[ca] = pl.ds(_al(chip * cn, cn), cn)
        if half is not None:
            idx[ha] = pl.ds(_al(half * hn, hn), hn)
    return ref.at[tuple(idx)]


def _half_of(ref, spec, half):
    _, _, ha, hn = spec
    idx = [slice(None)] * len(ref.shape)
    idx[ha] = pl.ds(_al(half * hn, hn), hn)
    return ref.at[tuple(idx)]


PIECE_BYTES = 256 * 1024


def _pieces(ref):
    lead = ref.shape[0]
    want = (int(np.prod(ref.shape)) * ref.dtype.itemsize) // PIECE_BYTES
    n = max([1] + [k for k in range(1, want + 1) if lead % k == 0 and (lead // k) % 16 == 0])
    rows = lead // n
    return [ref.at[pl.ds(i * rows, rows)] for i in range(n)]


def _half_shape(name):
    full, (ca, cn, ha, hn) = W_SPECS[name]
    shp = list(full)
    shp[ca] = cn
    shp[ha] = hn
    return tuple(shp)


def _gather_direct(names, sh, full, send_sems, recv_sems, local_sems):
    specs = [W_SPECS[k][1] for k in names]
    x, y, c = _my_place()
    chip_me = 2 * x + y
    chips = [(1 - x, y), (x, 1 - y), (1 - x, 1 - y)]

    def local(a):
        return pltpu.make_async_copy(sh[a], _region(full[a], specs[a], chip_me, None), local_sems.at[a])

    def remote(a, src, dst, to):
        return pltpu.make_async_remote_copy(src_ref=src, dst_ref=dst, send_sem=send_sems.at[a], recv_sem=recv_sems.at[a],
                                            device_id=to, device_id_type=MESH)

    def start():
        for a in range(len(names)):
            local(a).start()
            for px, py in chips:
                remote(a, sh[a], _region(full[a], specs[a], chip_me, None), (px, py, c)).start()

    def wait():
        for a in range(len(names)):
            ca, cn, _, _ = specs[a]
            idx = [slice(None)] * len(full[a].shape)
            idx[ca] = pl.ds(0, 3 * cn)
            three = full[a].at[tuple(idx)]
            remote(a, three, three, (x, y, c)).wait()
            local(a).wait()

    return start, wait


def _scatter_direct(names, part, slots, send_sems, recv_sems, local_sems):
    specs = [W_SPECS[k][1] for k in names]
    x, y, c = _my_place()
    dev_me = 4 * x + 2 * y + c

    def local(a):
        return pltpu.make_async_copy(_region(part[a], specs[a], 2 * x + y, c), slots[a].at[dev_me], local_sems.at[a])

    def start():
        for a in range(len(names)):
            local(a).start()
            for flip in range(1, N_DEV):
                tx = 1 - x if flip >> 2 else x
                ty = 1 - y if (flip >> 1) & 1 else y
                tc = 1 - c if flip & 1 else c
                pltpu.make_async_remote_copy(src_ref=_region(part[a], specs[a], 2 * tx + ty, tc), dst_ref=slots[a].at[dev_me],
                                             send_sem=send_sems.at[a], recv_sem=recv_sems.at[a], device_id=(tx, ty, tc),
                                             device_id_type=MESH).start()

    def wait():
        for a in range(len(names)):
            seven = slots[a].at[pl.ds(0, N_DEV - 1)]
            pltpu.make_async_remote_copy(src_ref=seven, dst_ref=seven, send_sem=send_sems.at[a], recv_sem=recv_sems.at[a],
                                         device_id=(x, y, c), device_id_type=MESH).wait()
            local(a).wait()

    return start, wait


def _comm_sems(n):
    return [pltpu.SemaphoreType.DMA((n,)), pltpu.SemaphoreType.DMA((n,)), pltpu.SemaphoreType.DMA((n,))]


GATHER_EARLY = ("hgrn_w_in",)
GATHER_LATE = ("hgrn_w_out", "pool_w_in", "pool_w_grp", "pool_w_out")


def _all_gather_weights(shards, names):
    n = len(names)
    specs = [W_SPECS[k][1] for k in names]

    def body(*refs):
        sh, full = refs[:n], refs[n:2 * n]
        send_sems, recv_sems, local_sems = refs[2 * n:]
        x, y, c = _my_place()
        chip_me = 2 * x + y
        sibling = (x, y, 1 - c)
        chips = [(1 - x, y), (x, 1 - y), (1 - x, 1 - y)]

        def remote(a, k, src, dst, to):
            return pltpu.make_async_remote_copy(src_ref=src, dst_ref=dst, send_sem=send_sems.at[6 * a + k],
                                                recv_sem=recv_sems.at[6 * a + k], device_id=to, device_id_type=MESH)

        local = [pltpu.make_async_copy(sh[a], _region(full[a], specs[a], chip_me, None), local_sems.at[a]) for a in range(n)]
        for cp in local:
            cp.start()
        sends = []
        for a in range(n):
            for j, (px, py) in enumerate(chips):
                src, dst = _half_of(sh[a], specs[a], c), _region(full[a], specs[a], chip_me, c)
                for s_piece, d_piece in zip(_pieces(src), _pieces(dst)):
                    remote(a, j, s_piece, d_piece, (px, py, c)).start()
                sends.append(remote(a, j, src, dst, (px, py, c)))
        for a in range(n):
            for j, (px, py) in enumerate(chips):
                landed = _region(full[a], specs[a], 2 * px + py, c)
                remote(a, j, landed, landed, (px, py, c)).wait_recv()
                for piece in _pieces(landed):
                    remote(a, 3 + j, piece, piece, sibling).start()
                sends.append(remote(a, 3 + j, landed, landed, sibling))
        for a in range(n):
            for j, (px, py) in enumerate(chips):
                other = _region(full[a], specs[a], 2 * px + py, 1 - c)
                remote(a, 3 + j, other, other, sibling).wait_recv()
        for cp in sends:
            cp.wait_send()
        for cp in local:
            cp.wait()

    return pl.pallas_call(
        body, name="all_gather_weights",
        out_shape=[jax.ShapeDtypeStruct(W_SPECS[k][0], BF16) for k in names],
        in_specs=[VMEM] * n, out_specs=[VMEM] * n,
        scratch_shapes=[pltpu.SemaphoreType.DMA((6 * n,)), pltpu.SemaphoreType.DMA((6 * n,)), pltpu.SemaphoreType.DMA((n,))],
        compiler_params=_params(32),
    )(*shards)


def _slot_shapes(names):
    return [jax.ShapeDtypeStruct((N_DEV,) + _half_shape(k), BF16) for k in names]


def _scatter_grads(parts, names):
    n = len(names)

    def body(*refs):
        start, wait = _scatter_direct(names, refs[:n], refs[n:2 * n], *refs[2 * n:])
        start()
        wait()

    return pl.pallas_call(body, name="scatter_grads", out_shape=_slot_shapes(names), in_specs=[ANY] * n, out_specs=[ANY] * n,
                          scratch_shapes=_comm_sems(n))(*parts)


def _exchange_halves(halves):
    n = len(W_NAMES)
    specs = [W_SPECS[k][1] for k in W_NAMES]

    def shard_shape(k):
        shp = list(_half_shape(k))
        shp[W_SPECS[k][1][2]] *= 2
        return tuple(shp)

    def body(*refs):
        half, out = refs[:n], refs[n:2 * n]
        send_sems, recv_sems, local_sems = refs[2 * n:]
        x, y, c = _my_place()
        sibling = (x, y, 1 - c)

        def remote(a, src, dst):
            return pltpu.make_async_remote_copy(src_ref=src, dst_ref=dst, send_sem=send_sems.at[a], recv_sem=recv_sems.at[a],
                                                device_id=sibling, device_id_type=MESH)

        local = [pltpu.make_async_copy(half[a], _half_of(out[a], specs[a], c), local_sems.at[a]) for a in range(n)]
        for cp in local:
            cp.start()
        for a in range(n):
            mine = _half_of(out[a], specs[a], c)
            for src, dst in zip(_pieces(half[a]), _pieces(mine)):
                remote(a, src, dst).start()
        for a in range(n):
            theirs = _half_of(out[a], specs[a], 1 - c)
            remote(a, theirs, theirs).wait_recv()
        for a in range(n):
            remote(a, half[a], half[a]).wait_send()
        for cp in local:
            cp.wait()

    return pl.pallas_call(
        body, name="exchange_halves",
        out_shape=[jax.ShapeDtypeStruct(shard_shape(k), F32) for k in W_NAMES],
        in_specs=[VMEM] * n, out_specs=[VMEM] * n,
        scratch_shapes=[pltpu.SemaphoreType.DMA((n,)), pltpu.SemaphoreType.DMA((n,)), pltpu.SemaphoreType.DMA((n,))],
        compiler_params=_params(32),
    )(*halves)


def _mod_parts(c_rows, ada_w, ada_b_cols):
    n_layers, _, n_cols = ada_w.shape

    def body(c_ref, w_ref, b_ref, o_ref):
        o_ref[...] = _mm_f32(_silu(c_ref[...]), w_ref[...]) + b_ref[...]

    return pl.pallas_call(
        body, name="mod_parts", grid=(n_layers,),
        out_shape=jax.ShapeDtypeStruct((n_layers, 16, n_cols), F32),
        in_specs=[pl.BlockSpec((16, D_MODEL), lambda i: (0, 0)),
                  pl.BlockSpec((None, D_MODEL, n_cols), lambda i: (i, 0, 0)),
                  pl.BlockSpec((None, 1, n_cols), lambda i: (i, 0, 0))],
        out_specs=pl.BlockSpec((None, 16, n_cols), lambda i: (i, 0, 0)),
        compiler_params=_params(40),
    )(c_rows, ada_w, ada_b_cols.reshape(n_layers, 1, n_cols))


def _ln_mod_matmul(xin, nw, shift, scale, w, name):
    rows = xin.shape[0]
    n_sec = w.shape[1] // D_MODEL
    n_mod = shift.shape[0]
    tm = ROW_TILE

    def body(x_ref, nw_ref, sh_ref, sc_ref, w_ref, g_ref, h_ref, h_scr):
        @pl.when(pl.program_id(1) == 0)
        def _():
            xv = x_ref[...]
            r = lax.rsqrt(jnp.mean(xv * xv, axis=-1, keepdims=True) + EPS)
            h = ((xv * r * nw_ref[...]) * (1.0 + sc_ref[...]) + sh_ref[...]).astype(BF16)
            h_scr[...] = h
            h_ref[...] = h

        g_ref[...] = jnp.dot(h_scr[...], w_ref[...], preferred_element_type=F32)

    mod_spec = pl.BlockSpec((None, 1, D_MODEL), lambda i, j: (jnp.minimum(i, n_mod - 1), 0, 0))
    return pl.pallas_call(
        body, name=name, grid=(rows // tm, n_sec),
        out_shape=[jax.ShapeDtypeStruct((n_sec, rows, D_MODEL), F32), jax.ShapeDtypeStruct((rows, D_MODEL), BF16)],
        in_specs=[pl.BlockSpec((tm, D_MODEL), lambda i, j: (i, 0)),
                  pl.BlockSpec((1, D_MODEL), lambda i, j: (0, 0)),
                  mod_spec, mod_spec,
                  pl.BlockSpec((D_MODEL, D_MODEL), lambda i, j: (0, j))],
        out_specs=[pl.BlockSpec((None, tm, D_MODEL), lambda i, j: (j, i, 0)),
                   pl.BlockSpec((tm, D_MODEL), lambda i, j: (i, 0))],
        scratch_shapes=[pltpu.VMEM((tm, D_MODEL), BF16)],
        compiler_params=_params(32),
    )(xin, nw, shift.reshape(n_mod, 1, D_MODEL), scale.reshape(n_mod, 1, D_MODEL), w)


def _chunk_masks(rev):
    rid = lax.broadcasted_iota(jnp.int32, (CHUNK, CHUNK), 0)
    cid = lax.broadcasted_iota(jnp.int32, (CHUNK, CHUNK), 1)
    keep = (cid >= rid) if rev else (cid <= rid)
    keep_t = (cid <= rid) if rev else (cid >= rid)
    one, zero = jnp.ones((CHUNK, CHUNK), F32), jnp.zeros((CHUNK, CHUNK), F32)
    return keep, jnp.where(keep, one, zero).astype(BF16), jnp.where(keep_t, one, zero).astype(BF16)


def _chunk_rows(t, rev, latent):
    n = N_LAT_CHUNKS if latent else N_CTX_CHUNKS
    base = CTX_LEN if latent else 0
    idx = (n - 1 - t) if rev else t
    return pl.multiple_of(base + idx * CHUNK, CHUNK)


def _gates(fpre, lb):
    sg = _sig(fpre)
    f = lb + (1.0 - lb) * sg
    return sg, f, 1.0 - f, jnp.log(f)


G_SPEC = lambda sec: pl.BlockSpec((None, ROWS_ALL, HEAD_DIM), lambda h, sec=sec: (sec, 0, h))


def _gla_forward(g5, lb_logits):
    q_scale = HEAD_DIM ** -0.5

    def body(ff_ref, fb_ref, v_ref, q_ref, lg_ref, o_ref, st_ref, decay_ref, qt_ref):
        for rev in (False, True):
            f_ref = fb_ref if rev else ff_ref
            lb = _sig(lg_ref[1:2, :] if rev else lg_ref[0:1, :])
            keep, tri, _ = _chunk_masks(rev)
            last = 0 if rev else CHUNK - 1
            mid = CHUNK // 2 if rev else CHUNK // 2 - 1

            def local_step(t, carry, latent):
                r0 = _chunk_rows(t, rev, latent)
                rows = pl.ds(r0, CHUNK)
                step = t + (N_CTX_CHUNKS if latent else 0)
                _, _, k, lf = _gates(f_ref[rows, :], lb)
                v = v_ref[rows, :]
                b = _mm_exact_lhs(tri, lf)
                bl = b[last:last + 1, :]
                if latent:
                    q = _silu(q_ref[rows, :]) * q_scale
                    bm = b[mid:mid + 1, :]
                    a = _mm_nt(q * jnp.exp(b - bm), k * jnp.exp(bm - b))
                    o = _mm(jnp.where(keep, a, 0.0), v)
                    orow = pl.ds(pl.multiple_of(r0 - CTX_LEN, CHUNK), CHUNK)
                    qt_ref[orow, :] = (q * jnp.exp(b)).astype(BF16)
                    if rev:
                        o_ref[orow, :] += o
                    else:
                        o_ref[orow, :] = o
                decay_ref[step] = jnp.exp(bl)
                st_ref[step] = _mm_tn(v, k * jnp.exp(bl - b))
                return carry

            lax.fori_loop(0, N_CTX_CHUNKS, functools.partial(local_step, latent=False), 0, unroll=2)
            lax.fori_loop(0, N_LAT_CHUNKS, functools.partial(local_step, latent=True), 0, unroll=4)

            def scan_step(t, st):
                update = st_ref[t]
                st_ref[t] = st
                return st * decay_ref[t] + update

            lax.fori_loop(0, N_CHUNKS, scan_step, jnp.zeros((HEAD_DIM, HEAD_DIM), F32), unroll=2)

            def inter_step(t, carry):
                r0 = _chunk_rows(t, rev, True)
                orow = pl.ds(pl.multiple_of(r0 - CTX_LEN, CHUNK), CHUNK)
                o_ref[orow, :] += lax.dot_general(qt_ref[orow, :], st_ref[t + N_CTX_CHUNKS].astype(BF16),
                                                  (((1,), (1,)), ((), ())), preferred_element_type=F32)
                return carry

            lax.fori_loop(0, N_LAT_CHUNKS, inter_step, 0, unroll=4)

    return pl.pallas_call(
        body, name="gla_forward", grid=(HEADS,),
        out_shape=jax.ShapeDtypeStruct((SEQ, D_MODEL), F32),
        in_specs=[G_SPEC(0), G_SPEC(1), G_SPEC(2), G_SPEC(3), pl.BlockSpec((2, HEAD_DIM), lambda h: (0, h))],
        out_specs=pl.BlockSpec((SEQ, HEAD_DIM), lambda h: (0, h)),
        scratch_shapes=[pltpu.VMEM((N_CHUNKS, HEAD_DIM, HEAD_DIM), F32), pltpu.VMEM((N_CHUNKS, 1, HEAD_DIM), F32),
                        pltpu.VMEM((SEQ, HEAD_DIM), BF16)],
        compiler_params=_params(32),
    )(g5, g5, g5, g5, lb_logits)


def _gla_backward(g5, lb_logits, d_o, d_z):
    q_scale = HEAD_DIM ** -0.5

    def body(ff_ref, fb_ref, v_ref, q_ref, lg_ref, do_ref, dz_ref, dg_ref, dlg_ref, st_ref, dst_ref, decay_ref):
        dg_ref[3, 0:CTX_LEN, :] = jnp.zeros((CTX_LEN, HEAD_DIM), F32)
        dg_ref[4, 0:CTX_LEN, :] = jnp.zeros((CTX_LEN, HEAD_DIM), F32)
        dg_ref[4, CTX_LEN:ROWS_ALL, :] = dz_ref[...]
        is_row = lax.broadcasted_iota(jnp.int32, (CHUNK, HEAD_DIM), 0)
        for rev in (False, True):
            d = 1 if rev else 0
            f_ref = fb_ref if rev else ff_ref
            lb = _sig(lg_ref[d:d + 1, :])
            keep, tri, tri_t = _chunk_masks(rev)
            last = 0 if rev else CHUNK - 1
            mid = CHUNK // 2 if rev else CHUNK // 2 - 1

            def local_step(t, carry, latent):
                r0 = _chunk_rows(t, rev, latent)
                rows = pl.ds(r0, CHUNK)
                step = t + (N_CTX_CHUNKS if latent else 0)
                _, _, k, lf = _gates(f_ref[rows, :], lb)
                b = _mm_exact_lhs(tri, lf)
                bl = b[last:last + 1, :]
                decay_ref[step] = jnp.exp(bl)
                st_ref[step] = _mm_tn(v_ref[rows, :], k * jnp.exp(bl - b))
                if latent:
                    q_t = _silu(q_ref[rows, :]) * q_scale * jnp.exp(b)
                    dst_ref[step] = _mm_tn(do_ref[pl.ds(pl.multiple_of(r0 - CTX_LEN, CHUNK), CHUNK), :], q_t)
                else:
                    dst_ref[step] = jnp.zeros((HEAD_DIM, HEAD_DIM), F32)
                return carry

            lax.fori_loop(0, N_CTX_CHUNKS, functools.partial(local_step, latent=False), 0, unroll=2)
            lax.fori_loop(0, N_LAT_CHUNKS, functools.partial(local_step, latent=True), 0, unroll=4)

            def scan_step(i, carry):
                st, d_st = carry
                j = N_CHUNKS - 1 - i
                update, d_update = st_ref[i], dst_ref[j]
                st_ref[i] = st
                dst_ref[j] = d_st
                return st * decay_ref[i] + update, d_st * decay_ref[j] + d_update

            zero_state = jnp.zeros((HEAD_DIM, HEAD_DIM), F32)
            lax.fori_loop(0, N_CHUNKS, scan_step, (zero_state, zero_state))

            def grad_step(t, dlb, latent):
                r0 = _chunk_rows(t, rev, latent)
                rows = pl.ds(r0, CHUNK)
                step = t + (N_CTX_CHUNKS if latent else 0)
                sg, f, k, lf = _gates(f_ref[rows, :], lb)
                v = v_ref[rows, :]
                b = _mm_exact_lhs(tri, lf)
                bl = b[last:last + 1, :]
                e_end = jnp.exp(bl - b)
                k_end = k * e_end
                decay = jnp.exp(bl)
                d_st = dst_ref[step]
                st_prev = st_ref[step]
                d_kend = _mm(v, d_st)
                d_decay = jnp.sum(d_st * st_prev, axis=0, keepdims=True)
                t_kend = d_kend * k_end
                d_v = _mm_nt(k_end, d_st)
                d_k = d_kend * e_end
                d_b = -t_kend
                if latent:
                    qpre = q_ref[rows, :]
                    q = _silu(qpre) * q_scale
                    bm = b[mid:mid + 1, :]
                    e_b, e_qm, e_km = jnp.exp(b), jnp.exp(b - bm), jnp.exp(bm - b)
                    q_t, q_m, k_m = q * e_b, q * e_qm, k * e_km
                    a = jnp.where(keep, _mm_nt(q_m, k_m), 0.0)
                    d_out = do_ref[pl.ds(pl.multiple_of(r0 - CTX_LEN, CHUNK), CHUNK), :]
                    d_a = jnp.where(keep, _mm_nt(d_out, v), 0.0)
                    d_qm = _mm(d_a, k_m)
                    d_km = _mm_tn(d_a, q_m)
                    d_qt = _mm(d_out, st_prev)
                    d_v = d_v + _mm_tn(a, d_out)
                    d_k = d_k + d_km * e_km
                    d_b = d_b + d_qt * q_t + d_qm * q_m - d_km * k_m
                    d_q = d_qt * e_b + d_qm * e_qm
                at_last = jnp.sum(t_kend, axis=0, keepdims=True) + d_decay * decay
                d_b = d_b + jnp.where(is_row == last, at_last, 0.0)
                d_lf = _mm_exact_lhs(tri_t, d_b)
                d_f = d_lf / f - d_k
                dg_ref[d, rows, :] = d_f * (1.0 - lb) * sg * (1.0 - sg)
                if rev:
                    dg_ref[2, rows, :] += d_v
                else:
                    dg_ref[2, rows, :] = d_v
                if latent:
                    d_qpre = d_q * q_scale * _dsilu(qpre)
                    if rev:
                        dg_ref[3, rows, :] += d_qpre
                    else:
                        dg_ref[3, rows, :] = d_qpre
                return dlb + jnp.sum(d_f * (1.0 - sg), axis=0, keepdims=True)

            dlb = lax.fori_loop(0, N_LAT_CHUNKS, functools.partial(grad_step, latent=True), jnp.zeros((1, HEAD_DIM), F32),
                                unroll=2)
            dlb = lax.fori_loop(0, N_CTX_CHUNKS, functools.partial(grad_step, latent=False), dlb, unroll=2)
            dlg_ref[d:d + 1, :] = dlb * lb * (1.0 - lb)

    col = pl.BlockSpec((SEQ, HEAD_DIM), lambda h: (0, h))
    return pl.pallas_call(
        body, name="gla_backward", grid=(HEADS,),
        out_shape=[jax.ShapeDtypeStruct((HGRN_SECTIONS, ROWS_ALL, D_MODEL), F32), jax.ShapeDtypeStruct((2, D_MODEL), F32)],
        in_specs=[G_SPEC(0), G_SPEC(1), G_SPEC(2), G_SPEC(3), pl.BlockSpec((2, HEAD_DIM), lambda h: (0, h)), col, col],
        out_specs=[pl.BlockSpec((HGRN_SECTIONS, ROWS_ALL, HEAD_DIM), lambda h: (0, 0, h)),
                   pl.BlockSpec((2, HEAD_DIM), lambda h: (0, h))],
        scratch_shapes=[pltpu.VMEM((N_CHUNKS, HEAD_DIM, HEAD_DIM), F32), pltpu.VMEM((N_CHUNKS, HEAD_DIM, HEAD_DIM), F32),
                        pltpu.VMEM((N_CHUNKS, 1, HEAD_DIM), F32)],
        compiler_params=_params(48),
    )(g5, g5, g5, g5, lb_logits, d_o, d_z)


GROUP = 2 * CHUNK


def _group_masks(rev):
    rid = lax.broadcasted_iota(jnp.int32, (GROUP, GROUP), 0)
    cid = lax.broadcasted_iota(jnp.int32, (GROUP, GROUP), 1)
    same = (rid >= CHUNK) == (cid >= CHUNK)
    causal = (cid >= rid) if rev else (cid <= rid)
    anti = (cid <= rid) if rev else (cid >= rid)
    bf = lambda m: jnp.where(m, jnp.ones((GROUP, GROUP), F32), jnp.zeros((GROUP, GROUP), F32)).astype(BF16)
    keep = same & causal
    return keep, bf(keep), bf(same & anti), bf(same & jnp.logical_not(anti))


def _group_sum(m_bf, a):
    hi, lo = _split2(a)
    r = jnp.dot(m_bf, jnp.concatenate([hi, lo], axis=1), preferred_element_type=F32)
    return r[:, :HEAD_DIM] + r[:, HEAD_DIM:]


def _chunk_row(a, pos):
    return jnp.concatenate([jnp.broadcast_to(a[c * CHUNK + pos:c * CHUNK + pos + 1, :], (CHUNK, HEAD_DIM)) for c in range(2)], axis=0)


def _by_chunk(a, second):
    return jnp.concatenate([jnp.where(second, 0.0, a), jnp.where(second, a, 0.0)], axis=1)


def _own_block(r):
    return jnp.concatenate([r[0:CHUNK, 0:HEAD_DIM], r[CHUNK:GROUP, HEAD_DIM:2 * HEAD_DIM]], axis=0)


def _scan_step_of(row_chunk, rev, latent):
    if not rev:
        return row_chunk
    return (N_CHUNKS + N_CTX_CHUNKS - 1 - row_chunk) if latent else (N_CTX_CHUNKS - 1 - row_chunk)


def _group_rows(i, j, per_step, latent):
    base = CTX_LEN if latent else 0
    return pl.multiple_of(base + (i * per_step + j) * GROUP, GROUP)


GROUPS_PER_STEP = 2


def _gla_forward(g5, lb_logits, late_shards):
    q_scale = HEAD_DIM ** -0.5
    per = GROUPS_PER_STEP
    n_late = len(GATHER_LATE)

    def body(ff_ref, fb_ref, v_ref, q_ref, lg_ref, *rest):
        shard_refs, o_ref, full_refs = rest[:n_late], rest[n_late], rest[n_late + 1:2 * n_late + 1]
        st_ref, decay_ref, qt_ref = rest[2 * n_late + 1:2 * n_late + 4]
        start_gather, wait_gather = _gather_direct(GATHER_LATE, shard_refs, full_refs, *rest[2 * n_late + 4:])

        @pl.when(pl.program_id(0) == 0)
        def _():
            start_gather()

        second = lax.broadcasted_iota(jnp.int32, (GROUP, HEAD_DIM), 0) >= CHUNK
        for rev in (False, True):
            f_ref = fb_ref if rev else ff_ref
            lb = _sig(lg_ref[1:2, :] if rev else lg_ref[0:1, :])
            keep, tri, _, _ = _group_masks(rev)
            last = 0 if rev else CHUNK - 1
            mid = CHUNK // 2 if rev else CHUNK // 2 - 1

            def local_step(i, carry, latent):
                r0s = [_group_rows(i, j, per, latent) for j in range(per)]
                rows = [pl.ds(r0, GROUP) for r0 in r0s]
                gates = [_gates(f_ref[r, :], lb) for r in rows]
                vs = [v_ref[r, :] for r in rows]
                bs = [_group_sum(tri, g[3]) for g in gates]
                bls = [_chunk_row(b, last) for b in bs]
                ups = [_mm_tn(v, _by_chunk(g[2] * jnp.exp(bl - b), second)) for v, g, b, bl in zip(vs, gates, bs, bls)]
                if latent:
                    qs = [_silu(q_ref[r, :]) * q_scale for r in rows]
                    bms = [_chunk_row(b, mid) for b in bs]
                    a_s = [_mm_nt(q * jnp.exp(b - bm), g[2] * jnp.exp(bm - b)) for q, g, b, bm in zip(qs, gates, bs, bms)]
                    outs = [_mm(jnp.where(keep, a, 0.0), v) for a, v in zip(a_s, vs)]
                for j in range(per):
                    for c in range(2):
                        step = _scan_step_of(r0s[j] // CHUNK + c, rev, latent)
                        decay_ref[step] = jnp.exp(bls[j][c * CHUNK:c * CHUNK + 1, :])
                        st_ref[step] = ups[j][:, c * HEAD_DIM:(c + 1) * HEAD_DIM]
                    if latent:
                        orow = pl.ds(pl.multiple_of(r0s[j] - CTX_LEN, GROUP), GROUP)
                        qt_ref[orow, :] = (qs[j] * jnp.exp(bs[j])).astype(BF16)
                        if rev:
                            o_ref[orow, :] += outs[j]
                        else:
                            o_ref[orow, :] = outs[j]
                return carry

            lax.fori_loop(0, CTX_LEN // (per * GROUP), functools.partial(local_step, latent=False), 0)
            lax.fori_loop(0, SEQ // (per * GROUP), functools.partial(local_step, latent=True), 0)

            def scan_step(t, st):
                update = st_ref[t]
                st_ref[t] = st
                return st * decay_ref[t] + update

            lax.fori_loop(0, N_CHUNKS, scan_step, jnp.zeros((HEAD_DIM, HEAD_DIM), F32), unroll=2)

            def inter_step(i, carry):
                r0s = [_group_rows(i, j, per, True) for j in range(per)]
                orows = [pl.ds(pl.multiple_of(r0 - CTX_LEN, GROUP), GROUP) for r0 in r0s]
                states = [jnp.concatenate([st_ref[_scan_step_of(r0 // CHUNK + c, rev, True)] for c in range(2)], axis=0)
                          for r0 in r0s]
                prods = [lax.dot_general(qt_ref[orow, :], s.astype(BF16), (((1,), (1,)), ((), ())), preferred_element_type=F32)
                         for orow, s in zip(orows, states)]
                for orow, r in zip(orows, prods):
                    o_ref[orow, :] += _own_block(r)
                return carry

            lax.fori_loop(0, SEQ // (per * GROUP), inter_step, 0)

        @pl.when(pl.program_id(0) == HEADS - 1)
        def _():
            wait_gather()

    outs = pl.pallas_call(
        body, name="gla_forward", grid=(HEADS,),
        out_shape=[jax.ShapeDtypeStruct((SEQ, D_MODEL), F32)] + [jax.ShapeDtypeStruct(W_SPECS[k][0], BF16) for k in GATHER_LATE],
        in_specs=[G_SPEC(0), G_SPEC(1), G_SPEC(2), G_SPEC(3), pl.BlockSpec((2, HEAD_DIM), lambda h: (0, h))] + [ANY] * n_late,
        out_specs=[pl.BlockSpec((SEQ, HEAD_DIM), lambda h: (0, h))] + [ANY] * n_late,
        scratch_shapes=[pltpu.VMEM((N_CHUNKS, HEAD_DIM, HEAD_DIM), F32), pltpu.VMEM((N_CHUNKS, 1, HEAD_DIM), F32),
                        pltpu.VMEM((SEQ, HEAD_DIM), BF16)] + _comm_sems(n_late),
        compiler_params=_params(32),
    )(g5, g5, g5, g5, lb_logits, *late_shards)
    return outs[0], dict(zip(GATHER_LATE, outs[1:]))


def _gla_backward(g5, lb_logits, d_o, d_z, late_parts):
    q_scale = HEAD_DIM ** -0.5
    per = GROUPS_PER_STEP
    n_late = len(GATHER_LATE)

    def body(ff_ref, fb_ref, v_ref, q_ref, lg_ref, do_ref, dz_ref, *rest):
        part_refs, (dg_ref, dlg_ref), slot_refs = rest[:n_late], rest[n_late:n_late + 2], rest[n_late + 2:2 * n_late + 2]
        st_ref, dst_ref, decay_ref = rest[2 * n_late + 2:2 * n_late + 5]
        start_scatter, wait_scatter = _scatter_direct(GATHER_LATE, part_refs, slot_refs, *rest[2 * n_late + 5:])

        @pl.when(pl.program_id(0) == 0)
        def _():
            start_scatter()

        dg_ref[3, 0:CTX_LEN, :] = jnp.zeros((CTX_LEN, HEAD_DIM), F32)
        dg_ref[4, 0:CTX_LEN, :] = jnp.zeros((CTX_LEN, HEAD_DIM), F32)
        dg_ref[4, CTX_LEN:ROWS_ALL, :] = dz_ref[...]
        second = lax.broadcasted_iota(jnp.int32, (GROUP, HEAD_DIM), 0) >= CHUNK
        for rev in (False, True):
            d = 1 if rev else 0
            f_ref = fb_ref if rev else ff_ref
            lb = _sig(lg_ref[d:d + 1, :])
            keep, tri, tri_t, strict = _group_masks(rev)
            last = 0 if rev else CHUNK - 1
            mid = CHUNK // 2 if rev else CHUNK // 2 - 1

            def local_step(i, carry, latent):
                r0s = [_group_rows(i, j, per, latent) for j in range(per)]
                rows = [pl.ds(r0, GROUP) for r0 in r0s]
                gates = [_gates(f_ref[r, :], lb) for r in rows]
                bs = [_group_sum(tri, g[3]) for g in gates]
                bls = [_chunk_row(b, last) for b in bs]
                ups = [_mm_tn(v_ref[r, :], _by_chunk(g[2] * jnp.exp(bl - b), second)) for r, g, b, bl in zip(rows, gates, bs, bls)]
                if latent:
                    orows = [pl.ds(pl.multiple_of(r0 - CTX_LEN, GROUP), GROUP) for r0 in r0s]
                    d_ups = [_mm_tn(do_ref[orow, :], _by_chunk(_silu(q_ref[r, :]) * q_scale * jnp.exp(b), second))
                             for orow, r, b in zip(orows, rows, bs)]
                for j in range(per):
                    for c in range(2):
                        step = _scan_step_of(r0s[j] // CHUNK + c, rev, latent)
                        decay_ref[step] = jnp.exp(bls[j][c * CHUNK:c * CHUNK + 1, :])
                        st_ref[step] = ups[j][:, c * HEAD_DIM:(c + 1) * HEAD_DIM]
                        if latent:
                            dst_ref[step] = d_ups[j][:, c * HEAD_DIM:(c + 1) * HEAD_DIM]
                        else:
                            dst_ref[step] = jnp.zeros((HEAD_DIM, HEAD_DIM), F32)
                return carry

            lax.fori_loop(0, CTX_LEN // (per * GROUP), functools.partial(local_step, latent=False), 0)
            lax.fori_loop(0, SEQ // (per * GROUP), functools.partial(local_step, latent=True), 0)

            def scan_step(i, carry):
                st, d_st = carry
                j = N_CHUNKS - 1 - i
                update, d_update = st_ref[i], dst_ref[j]
                st_ref[i] = st
                dst_ref[j] = d_st
                return st * decay_ref[i] + update, d_st * decay_ref[j] + d_update

            zero_state = jnp.zeros((HEAD_DIM, HEAD_DIM), F32)
            lax.fori_loop(0, N_CHUNKS, scan_step, (zero_state, zero_state))

            def grad_step(i, dlb, latent):
                r0s = [_group_rows(i, j, per, latent) for j in range(per)]
                rows = [pl.ds(r0, GROUP) for r0 in r0s]
                gates = [_gates(f_ref[r, :], lb) for r in rows]
                vs = [v_ref[r, :] for r in rows]
                bs = [_group_sum(tri, g[3]) for g in gates]
                bls = [_chunk_row(b, last) for b in bs]
                e_ends = [jnp.exp(bl - b) for b, bl in zip(bs, bls)]
                k_ends = [g[2] * e for g, e in zip(gates, e_ends)]
                sts = [[st_ref[_scan_step_of(r0 // CHUNK + c, rev, latent)] for c in range(2)] for r0 in r0s]
                d_sts = [[dst_ref[_scan_step_of(r0 // CHUNK + c, rev, latent)] for c in range(2)] for r0 in r0s]
                d_kends = [_own_block(_mm(v, jnp.concatenate(ds, axis=1))) for v, ds in zip(vs, d_sts)]
                d_vs = [_own_block(_mm_nt(ke, jnp.concatenate(ds, axis=0))) for ke, ds in zip(k_ends, d_sts)]
                at_last = [jnp.concatenate([jnp.broadcast_to(jnp.sum(ds[c] * s[c], axis=0, keepdims=True), (CHUNK, HEAD_DIM))
                                            for c in range(2)], axis=0) * jnp.exp(bl) for ds, s, bl in zip(d_sts, sts, bls)]
                t_kends = [dk * ke for dk, ke in zip(d_kends, k_ends)]
                d_ks = [dk * e for dk, e in zip(d_kends, e_ends)]
                d_lfs = [_group_sum(strict, t) + al for t, al in zip(t_kends, at_last)]
                if latent:
                    orows = [pl.ds(pl.multiple_of(r0 - CTX_LEN, GROUP), GROUP) for r0 in r0s]
                    qpres = [q_ref[r, :] for r in rows]
                    qs = [_silu(qp) * q_scale for qp in qpres]
                    bms = [_chunk_row(b, mid) for b in bs]
                    e_bs = [jnp.exp(b) for b in bs]
                    e_qms = [jnp.exp(b - bm) for b, bm in zip(bs, bms)]
                    e_kms = [jnp.exp(bm - b) for b, bm in zip(bs, bms)]
                    q_ts = [q * e for q, e in zip(qs, e_bs)]
                    q_ms = [q * e for q, e in zip(qs, e_qms)]
                    k_ms = [g[2] * e for g, e in zip(gates, e_kms)]
                    d_outs = [do_ref[orow, :] for orow in orows]
                    a_s = [jnp.where(keep, _mm_nt(qm, km), 0.0) for qm, km in zip(q_ms, k_ms)]
                    d_as = [jnp.where(keep, _mm_nt(do, v), 0.0) for do, v in zip(d_outs, vs)]
                    d_qts = [_own_block(_mm(do, jnp.concatenate(s, axis=1))) for do, s in zip(d_outs, sts)]
                    d_qms = [_mm(da, km) for da, km in zip(d_as, k_ms)]
                    d_kms = [_mm_tn(da, qm) for da, qm in zip(d_as, q_ms)]
                    d_vs = [dv + _mm_tn(a, do) for dv, a, do in zip(d_vs, a_s, d_outs)]
                    d_ks = [dk + dkm * e for dk, dkm, e in zip(d_ks, d_kms, e_kms)]
                    d_lfs = [dl + _group_sum(tri_t, dqt * qt + dqm * qm - dkm * km)
                             for dl, dqt, qt, dqm, qm, dkm, km in zip(d_lfs, d_qts, q_ts, d_qms, q_ms, d_kms, k_ms)]
                    d_qs = [dqt * eb + dqm * eq for dqt, eb, dqm, eq in zip(d_qts, e_bs, d_qms, e_qms)]
                for j in range(per):
                    sg, f = gates[j][0], gates[j][1]
                    d_f = d_lfs[j] / f - d_ks[j]
                    dg_ref[d, rows[j], :] = d_f * (1.0 - lb) * sg * (1.0 - sg)
                    dlb = dlb + jnp.sum(d_f * (1.0 - sg), axis=0, keepdims=True)
                    if rev:
                        dg_ref[2, rows[j], :] += d_vs[j]
                    else:
                        dg_ref[2, rows[j], :] = d_vs[j]
                    if latent:
                        d_qpre = d_qs[j] * q_scale * _dsilu(qpres[j])
                        if rev:
                            dg_ref[3, rows[j], :] += d_qpre
                        else:
                            dg_ref[3, rows[j], :] = d_qpre
                return dlb

            dlb = lax.fori_loop(0, SEQ // (per * GROUP), functools.partial(grad_step, latent=True), jnp.zeros((1, HEAD_DIM), F32))
            dlb = lax.fori_loop(0, CTX_LEN // (per * GROUP), functools.partial(grad_step, latent=False), dlb)
            dlg_ref[d:d + 1, :] = dlb * lb * (1.0 - lb)

        @pl.when(pl.program_id(0) == HEADS - 1)
        def _():
            wait_scatter()

    col = pl.BlockSpec((SEQ, HEAD_DIM), lambda h: (0, h))
    outs = pl.pallas_call(
        body, name="gla_backward", grid=(HEADS,),
        out_shape=[jax.ShapeDtypeStruct((HGRN_SECTIONS, ROWS_ALL, D_MODEL), F32), jax.ShapeDtypeStruct((2, D_MODEL), F32)]
        + _slot_shapes(GATHER_LATE),
        in_specs=[G_SPEC(0), G_SPEC(1), G_SPEC(2), G_SPEC(3), pl.BlockSpec((2, HEAD_DIM), lambda h: (0, h)), col, col]
        + [ANY] * n_late,
        out_specs=[pl.BlockSpec((HGRN_SECTIONS, ROWS_ALL, HEAD_DIM), lambda h: (0, 0, h)),
                   pl.BlockSpec((2, HEAD_DIM), lambda h: (0, h))] + [ANY] * n_late,
        scratch_shapes=[pltpu.VMEM((N_CHUNKS, HEAD_DIM, HEAD_DIM), F32), pltpu.VMEM((N_CHUNKS, HEAD_DIM, HEAD_DIM), F32),
                        pltpu.VMEM((N_CHUNKS, 1, HEAD_DIM), F32)] + _comm_sems(n_late),
        compiler_params=_params(48),
    )(g5, g5, g5, g5, lb_logits, d_o, d_z, *late_parts)
    return outs[0], outs[1], dict(zip(GATHER_LATE, outs[2:]))


def _head_norm(o, gw, scr):
    rs = []
    for h in range(HEADS):
        cols = slice(h * HEAD_DIM, (h + 1) * HEAD_DIM)
        oh = o[:, cols]
        r = lax.rsqrt(jnp.mean(oh * oh, axis=-1, keepdims=True) + EPS)
        scr[:, cols] = oh * r
        rs.append(r)
    return rs


def _hgrn_out_forward(o_raw, g5, xin, gnorm_w, gate, w_out):
    tm = ROW_TILE

    def body(o_ref, z_ref, x_ref, gw_ref, gate_ref, w_ref, x1_ref, res_ref, scr):
        _head_norm(o_ref[...], None, scr)
        a = scr[...] * gw_ref[...] * _silu(z_ref[...])
        res = _mm(a, w_ref[...])
        res_ref[...] = res
        x1_ref[...] = x_ref[...] + gate_ref[...] * res

    tile = pl.BlockSpec((tm, D_MODEL), lambda i: (i, 0))
    vec = pl.BlockSpec((1, D_MODEL), lambda i: (0, 0))
    return pl.pallas_call(
        body, name="hgrn_out_forward", grid=(SEQ // tm,),
        out_shape=[jax.ShapeDtypeStruct((SEQ, D_MODEL), F32)] * 2,
        in_specs=[tile, pl.BlockSpec((None, tm, D_MODEL), lambda i: (4, i + CTX_LEN // tm, 0)), tile, vec, vec,
                  pl.BlockSpec((D_MODEL, D_MODEL), lambda i: (0, 0))],
        out_specs=[tile, tile],
        scratch_shapes=[pltpu.VMEM((tm, D_MODEL), F32)],
        compiler_params=_params(32),
    )(o_raw, g5, xin, gnorm_w, gate, w_out)


def _hgrn_out_backward(d_x1, o_raw, g5, res, gnorm_w, gate, w_out):
    tm = ROW_TILE

    def body(dx_ref, o_ref, z_ref, res_ref, gw_ref, gate_ref, w_ref, do_ref, dz_ref, dw_ref, dgate_ref, dgw_ref, scr, scr2):
        @pl.when(pl.program_id(0) == 0)
        def _():
            dw_ref[...] = jnp.zeros_like(dw_ref)
            dgate_ref[...] = jnp.zeros_like(dgate_ref)
            dgw_ref[...] = jnp.zeros_like(dgw_ref)

        dx = dx_ref[...]
        dgate_ref[...] += jnp.sum(dx * res_ref[...], axis=0, keepdims=True)
        d_res = (dx * gate_ref[...]).astype(BF16)
        d_a = _mm_nt(d_res, w_ref[...])
        rs = _head_norm(o_ref[...], None, scr)
        z = z_ref[...]
        sz = _silu(z)
        o_hat = scr[...]
        o_n = o_hat * gw_ref[...]
        dw_ref[...] += _mm_tn(o_n * sz, d_res)
        d_on = d_a * sz
        dz_ref[...] = d_a * o_n * _dsilu(z)
        dgw_ref[...] += jnp.sum(d_on * o_hat, axis=0, keepdims=True)
        scr2[...] = d_on * gw_ref[...]
        for h in range(HEADS):
            cols = slice(h * HEAD_DIM, (h + 1) * HEAD_DIM)
            dh, oh = scr2[:, cols], scr[:, cols]
            do_ref[:, cols] = rs[h] * (dh - oh * jnp.mean(dh * oh, axis=-1, keepdims=True))

    tile = pl.BlockSpec((tm, D_MODEL), lambda i: (i, 0))
    vec = pl.BlockSpec((1, D_MODEL), lambda i: (0, 0))
    mat = pl.BlockSpec((D_MODEL, D_MODEL), lambda i: (0, 0))
    return pl.pallas_call(
        body, name="hgrn_out_backward", grid=(SEQ // tm,),
        out_shape=[jax.ShapeDtypeStruct((SEQ, D_MODEL), F32)] * 2 + [jax.ShapeDtypeStruct((D_MODEL, D_MODEL), F32)]
        + [jax.ShapeDtypeStruct((1, D_MODEL), F32)] * 2,
        in_specs=[tile, tile, pl.BlockSpec((None, tm, D_MODEL), lambda i: (4, i + CTX_LEN // tm, 0)), tile, vec, vec, mat],
        out_specs=[tile, tile, mat, vec, vec],
        scratch_shapes=[pltpu.VMEM((tm, D_MODEL), F32)] * 2,
        compiler_params=_params(40),
    )(d_x1, o_raw, g5, res, gnorm_w, gate, w_out)


def _pool_constants():
    win = np.zeros((POOL_GROUPS, ROW_TILE, ROW_TILE), np.float32)
    inv = np.zeros((POOL_GROUPS, ROW_TILE, 1), np.float32)
    for g, w in enumerate(POOL_WINDOWS):
        for t in range(ROW_TILE):
            base, p = (t // GRID_W) * GRID_W, t % GRID_W
            lo = min(max(p - w // 2, 0), GRID_W)
            hi = min(max(p - w // 2 + w, 0), GRID_W)
            win[g, t, base + lo:base + hi] = 1.0
            inv[g, t, 0] = 1.0 / np.float32(hi - lo)
    return jnp.asarray(win, BF16), jnp.asarray(win.transpose(0, 2, 1), BF16), jnp.asarray(inv, F32)


def _pool_mix(u_ref, wg_ref, ps_ref, win_ref, inv_ref, pooled_scr, yg_scr):
    for g in range(POOL_GROUPS):
        cols = slice(g * POOL_GROUP_DIM, (g + 1) * POOL_GROUP_DIM)
        ug = u_ref[:, cols]
        pooled = _mm_exact_lhs(win_ref[g], ug) * inv_ref[g] - ug
        if pooled_scr is not None:
            pooled_scr[:, cols] = pooled
        yg_scr[:, cols] = _mm(pooled, wg_ref[g])


def _pool_forward_loss(uz, x1, target, gate, w_grp, pool_scale, w_out, final_w):
    tm = ROW_TILE
    win, _, inv = _pool_constants()

    def body(u_ref, z_ref, x_ref, t_ref, gate_ref, wg_ref, ps_ref, w_ref, fw_ref, win_ref, inv_ref,
             dx_ref, loss_ref, dfw_ref, dgate_ref, yg_scr):
        @pl.when(pl.program_id(0) == 0)
        def _():
            loss_ref[...] = jnp.zeros_like(loss_ref)
            dfw_ref[...] = jnp.zeros_like(dfw_ref)
            dgate_ref[...] = jnp.zeros_like(dgate_ref)

        _pool_mix(u_ref, wg_ref, ps_ref, win_ref, inv_ref, None, yg_scr)
        a = yg_scr[...] * ps_ref[...] * _silu(z_ref[...])
        res = _mm(a, w_ref[...])
        x2 = x_ref[...] + gate_ref[...] * res
        r = lax.rsqrt(jnp.mean(x2 * x2, axis=-1, keepdims=True) + EPS)
        xh = x2 * r
        fw = fw_ref[...]
        err = xh * fw - t_ref[...]
        loss_ref[...] += 0.5 * jnp.sum(jnp.mean(err * err, axis=-1, keepdims=True))
        d_y = err * (1.0 / D_MODEL)
        dfw_ref[...] += jnp.sum(d_y * xh, axis=0, keepdims=True)
        d_xh = d_y * fw
        d_x2 = r * (d_xh - xh * jnp.mean(d_xh * xh, axis=-1, keepdims=True))
        dx_ref[...] = d_x2
        dgate_ref[...] += jnp.sum(d_x2 * res, axis=0, keepdims=True)

    tile = pl.BlockSpec((tm, D_MODEL), lambda i: (i, 0))
    vec = pl.BlockSpec((1, D_MODEL), lambda i: (0, 0))
    grp = pl.BlockSpec((POOL_GROUPS, POOL_GROUP_DIM, POOL_GROUP_DIM), lambda i: (0, 0, 0))
    return pl.pallas_call(
        body, name="pool_forward_loss", grid=(SEQ // tm,),
        out_shape=[jax.ShapeDtypeStruct((SEQ, D_MODEL), F32), jax.ShapeDtypeStruct((8, 128), F32),
                   jax.ShapeDtypeStruct((1, D_MODEL), F32), jax.ShapeDtypeStruct((1, D_MODEL), F32)],
        in_specs=[pl.BlockSpec((None, tm, D_MODEL), lambda i: (0, i, 0)), pl.BlockSpec((None, tm, D_MODEL), lambda i: (1, i, 0)),
                  tile, tile, vec, grp, vec, pl.BlockSpec((D_MODEL, D_MODEL), lambda i: (0, 0)), vec, grp,
                  pl.BlockSpec((POOL_GROUPS, ROW_TILE, 1), lambda i: (0, 0, 0))],
        out_specs=[tile, pl.BlockSpec((8, 128), lambda i: (0, 0)), vec, vec],
        scratch_shapes=[pltpu.VMEM((tm, D_MODEL), F32)],
        compiler_params=_params(32),
    )(uz, uz, x1, target, gate, w_grp, pool_scale, w_out, final_w, win, inv)


def _pool_backward(d_x2, uz, gate, w_grp, pool_scale, w_out):
    tm = ROW_TILE
    win, win_t, inv = _pool_constants()

    def body(dx_ref, u_ref, z_ref, gate_ref, wg_ref, ps_ref, w_ref, win_ref, wint_ref, inv_ref,
             duz_ref, dw_ref, dwg_ref, dps_ref, pooled_scr, yg_scr, dyg_scr):
        @pl.when(pl.program_id(0) == 0)
        def _():
            dw_ref[...] = jnp.zeros_like(dw_ref)
            dwg_ref[...] = jnp.zeros_like(dwg_ref)
            dps_ref[...] = jnp.zeros_like(dps_ref)

        _pool_mix(u_ref, wg_ref, ps_ref, win_ref, inv_ref, pooled_scr, yg_scr)
        z = z_ref[...]
        sz = _silu(z)
        yg = yg_scr[...]
        y = yg * ps_ref[...]
        d_res = (dx_ref[...] * gate_ref[...]).astype(BF16)
        d_a = _mm_nt(d_res, w_ref[...])
        dw_ref[...] += _mm_tn(y * sz, d_res)
        d_y = d_a * sz
        duz_ref[1] = d_a * y * _dsilu(z)
        dps_ref[...] += jnp.sum(d_y * yg, axis=0, keepdims=True)
        dyg_scr[...] = d_y * ps_ref[...]
        for g in range(POOL_GROUPS):
            cols = slice(g * POOL_GROUP_DIM, (g + 1) * POOL_GROUP_DIM)
            d_yg = dyg_scr[:, cols].astype(BF16)
            d_pool = _mm_nt(d_yg, wg_ref[g])
            dwg_ref[g] += _mm_tn(pooled_scr[:, cols], d_yg)
            duz_ref[0, :, cols] = _mm_exact_lhs(wint_ref[g], d_pool * inv_ref[g]) - d_pool

    tile = pl.BlockSpec((tm, D_MODEL), lambda i: (i, 0))
    vec = pl.BlockSpec((1, D_MODEL), lambda i: (0, 0))
    mat = pl.BlockSpec((D_MODEL, D_MODEL), lambda i: (0, 0))
    grp = pl.BlockSpec((POOL_GROUPS, POOL_GROUP_DIM, POOL_GROUP_DIM), lambda i: (0, 0, 0))
    return pl.pallas_call(
        body, name="pool_backward", grid=(SEQ // tm,),
        out_shape=[jax.ShapeDtypeStruct((POOL_SECTIONS, SEQ, D_MODEL), F32), jax.ShapeDtypeStruct((D_MODEL, D_MODEL), F32),
                   jax.ShapeDtypeStruct((POOL_GROUPS, POOL_GROUP_DIM, POOL_GROUP_DIM), F32), jax.ShapeDtypeStruct((1, D_MODEL), F32)],
        in_specs=[tile, pl.BlockSpec((None, tm, D_MODEL), lambda i: (0, i, 0)), pl.BlockSpec((None, tm, D_MODEL), lambda i: (1, i, 0)),
                  vec, grp, vec, mat, grp, grp, pl.BlockSpec((POOL_GROUPS, ROW_TILE, 1), lambda i: (0, 0, 0))],
        out_specs=[pl.BlockSpec((POOL_SECTIONS, tm, D_MODEL), lambda i: (0, i, 0)), mat, grp, vec],
        scratch_shapes=[pltpu.VMEM((tm, D_MODEL), F32)] * 3,
        compiler_params=_params(40),
    )(d_x2, uz, uz, gate, w_grp, pool_scale, w_out, win, win_t, inv)


def _ln_mod_backward(d_g, w, xin, nw, scale, d_up, name):
    n_sec, rows, _ = d_g.shape
    n_mod = scale.shape[0]
    tm = ROW_TILE
    skip = n_mod - 1

    def body(dg_ref, w_ref, x_ref, nw_ref, sc_ref, up_ref, dx_ref, dnw_ref, dmod_ref, acc):
        i, k = pl.program_id(0), pl.program_id(1)

        @pl.when(k == 0)
        def _():
            acc[...] = jnp.zeros_like(acc)

        acc[...] += _mm_nt(dg_ref[...], w_ref[...])

        @pl.when(k == n_sec - 1)
        def _():
            @pl.when(i == 0)
            def _():
                dnw_ref[...] = jnp.zeros_like(dnw_ref)

            @pl.when((i == 0) | (i == skip))
            def _():
                dmod_ref[...] = jnp.zeros_like(dmod_ref)

            d_h = acc[...]
            xv = x_ref[...]
            r = lax.rsqrt(jnp.mean(xv * xv, axis=-1, keepdims=True) + EPS)
            xh = xv * r
            nw_row = nw_ref[...]
            dmod_ref[0:1, :] += jnp.sum(d_h, axis=0, keepdims=True)
            dmod_ref[1:2, :] += jnp.sum(d_h * (xh * nw_row), axis=0, keepdims=True)
            d_xn = d_h * (1.0 + sc_ref[...])
            dnw_ref[...] += jnp.sum(d_xn * xh, axis=0, keepdims=True)
            d_xh = d_xn * nw_row

            @pl.when(i >= skip)
            def _():
                dx_ref[...] = up_ref[...] + r * (d_xh - xh * jnp.mean(d_xh * xh, axis=-1, keepdims=True))

    lat = lambda i, k: (jnp.maximum(i - skip, 0), 0)
    mod_idx = lambda i, k: (jnp.minimum(i, n_mod - 1), 0, 0)
    return pl.pallas_call(
        body, name=name, grid=(rows // tm, n_sec),
        out_shape=[jax.ShapeDtypeStruct((rows - skip * tm, D_MODEL), F32), jax.ShapeDtypeStruct((1, D_MODEL), F32),
                   jax.ShapeDtypeStruct((n_mod, 8, D_MODEL), F32)],
        in_specs=[pl.BlockSpec((None, tm, D_MODEL), lambda i, k: (k, i, 0)),
                  pl.BlockSpec((D_MODEL, D_MODEL), lambda i, k: (0, k)),
                  pl.BlockSpec((tm, D_MODEL), lambda i, k: (i, 0)),
                  pl.BlockSpec((1, D_MODEL), lambda i, k: (0, 0)),
                  pl.BlockSpec((None, 1, D_MODEL), mod_idx),
                  pl.BlockSpec((tm, D_MODEL), lat)],
        out_specs=[pl.BlockSpec((tm, D_MODEL), lat), pl.BlockSpec((1, D_MODEL), lambda i, k: (0, 0)),
                   pl.BlockSpec((None, 8, D_MODEL), mod_idx)],
        scratch_shapes=[pltpu.VMEM((tm, D_MODEL), F32)],
        compiler_params=_params(32),
    )(d_g, w, xin, nw, scale.reshape(n_mod, 1, D_MODEL), d_up)


def _weight_grad(h, d_g, name):
    n_sec, rows, _ = d_g.shape
    tm = 768 if rows % 768 == 0 else 512

    def body(h_ref, dg_ref, dw_ref):
        @pl.when(pl.program_id(1) == 0)
        def _():
            dw_ref[...] = jnp.zeros_like(dw_ref)

        dw_ref[...] += _mm_tn(h_ref[...], dg_ref[...])

    return pl.pallas_call(
        body, name=name, grid=(n_sec, rows // tm),
        out_shape=jax.ShapeDtypeStruct((D_MODEL, n_sec * D_MODEL), F32),
        in_specs=[pl.BlockSpec((tm, D_MODEL), lambda j, i: (i, 0)), pl.BlockSpec((None, tm, D_MODEL), lambda j, i: (j, i, 0))],
        out_specs=pl.BlockSpec((D_MODEL, D_MODEL), lambda j, i: (0, j)),
        compiler_params=_params(32),
    )(h, d_g)


def _sum_slots(slots, name):
    _, rows, cols = slots.shape
    tr = 128

    def body(s_ref, o_ref):
        acc = s_ref[0].astype(F32)
        for d in range(1, N_DEV):
            acc = acc + s_ref[d].astype(F32)
        o_ref[...] = acc

    return pl.pallas_call(
        body, name=name, grid=(rows // tr,),
        out_shape=jax.ShapeDtypeStruct((rows, cols), F32),
        in_specs=[pl.BlockSpec((N_DEV, tr, cols), lambda i: (0, i, 0))],
        out_specs=pl.BlockSpec((tr, cols), lambda i: (i, 0)),
    )(slots)


def _adamw_math(w, g, m, v):
    m = ADAM_B1 * m + (1.0 - ADAM_B1) * g
    v = ADAM_B2 * v + (1.0 - ADAM_B2) * (g * g)
    m_hat = m / (1.0 - ADAM_B1 ** ADAM_STEP)
    v_hat = v / (1.0 - ADAM_B2 ** ADAM_STEP)
    return -ADAM_LR * (m_hat / (jnp.sqrt(v_hat) + ADAM_EPS) + ADAM_WD * w), m, v


def _adamw(w, g, m, v, name):
    rows, cols = w.shape
    tr = rows if rows <= 128 else 128

    def body(w_ref, g_ref, m_ref, v_ref, d_ref, mo_ref, vo_ref):
        d_ref[...], mo_ref[...], vo_ref[...] = _adamw_math(w_ref[...], g_ref[...], m_ref[...], v_ref[...])

    tile = pl.BlockSpec((tr, cols), lambda i: (i, 0))
    return pl.pallas_call(
        body, name=name, grid=(rows // tr,),
        out_shape=[jax.ShapeDtypeStruct((rows, cols), F32)] * 3,
        in_specs=[tile] * 4, out_specs=[tile] * 3,
    )(w, g, m, v)


def _sum_devices(gathered):
    def body(p_ref, o_ref):
        acc = p_ref[0:SMALL_ROWS, :]
        for d in range(1, N_DEV):
            acc = acc + p_ref[d * SMALL_ROWS:(d + 1) * SMALL_ROWS, :]
        o_ref[...] = acc

    return pl.pallas_call(body, name="sum_devices", out_shape=jax.ShapeDtypeStruct((SMALL_ROWS, D_MODEL), F32),
                          in_specs=[VMEM], out_specs=VMEM)(gathered)


def _ada_update(cond_t, d_mod, ada_w, m, v):
    n_layers, _, n_cols = ada_w.shape
    tr = ROW_TILE

    def body(c_ref, dm_ref, w_ref, m_ref, v_ref, g_ref, d_ref, mo_ref, vo_ref):
        g = _mm_f32(_silu(c_ref[...]), dm_ref[...])
        g_ref[...] = g
        d_ref[...], mo_ref[...], vo_ref[...] = _adamw_math(w_ref[...], g, m_ref[...], v_ref[...])

    tile = pl.BlockSpec((None, tr, n_cols), lambda l, i: (l, i, 0))
    return pl.pallas_call(
        body, name="ada_update", grid=(n_layers, D_MODEL // tr),
        out_shape=[jax.ShapeDtypeStruct(ada_w.shape, F32)] * 4,
        in_specs=[pl.BlockSpec((tr, 16), lambda l, i: (i, 0)), pl.BlockSpec((None, 16, n_cols), lambda l, i: (l, 0, 0)),
                  tile, tile, tile],
        out_specs=[tile] * 4,
    )(cond_t, d_mod, ada_w, m, v)


def _cond_ctx_partial(d_modc, ada_w0):
    n_cols = ada_w0.shape[1]
    tr = ROW_TILE

    def body(dm_ref, w_ref, o_ref):
        o_ref[...] = jnp.sum(w_ref[...] * dm_ref[...], axis=-1, keepdims=True)

    return pl.pallas_call(
        body, name="cond_ctx_partial", grid=(D_MODEL // tr,),
        out_shape=jax.ShapeDtypeStruct((D_MODEL, 1), F32),
        in_specs=[pl.BlockSpec((1, n_cols), lambda i: (0, 0)), pl.BlockSpec((tr, n_cols), lambda i: (i, 0))],
        out_specs=pl.BlockSpec((tr, 1), lambda i: (i, 0)),
    )(d_modc, ada_w0)


def _cond_ctx_update(gathered, c_ctx, m, v):
    def body(p_ref, w_ref, m_ref, v_ref, g_ref, d_ref, mo_ref, vo_ref):
        acc = p_ref[0:1, :]
        for s in range(1, N_CHIPS):
            acc = acc + p_ref[16 * s:16 * s + 1, :]
        w = w_ref[...]
        g = acc * _dsilu(w)
        g_ref[...] = g
        d_ref[...], mo_ref[...], vo_ref[...] = _adamw_math(w, g, m_ref[...], v_ref[...])

    return pl.pallas_call(body, name="cond_ctx_update", out_shape=[jax.ShapeDtypeStruct((1, D_MODEL), F32)] * 4,
                          in_specs=[VMEM] * 4, out_specs=[VMEM] * 4)(gathered, c_ctx, m, v)


def _local_step(x2, ctx2, target, mod_mine, mod_ctx, lb_logits, scale_full, w_in_full, late_shards, norm_w, gnorm, final_w):
    row = lambda a: a.reshape(1, -1)
    shift0, scale0, gate0 = (row(a) for a in jnp.split(mod_mine[0], 3))
    shift1, scale1, gate1 = (row(a) for a in jnp.split(mod_mine[1], 3))
    shift_c, scale_c, _ = (row(a) for a in jnp.split(mod_ctx, 3))
    xa = jnp.concatenate([ctx2, x2], axis=0)
    nw0, nw1 = norm_w[0:1], norm_w[1:2]
    scales0 = jnp.concatenate([scale_c, scale0])

    g5, h0 = _ln_mod_matmul(xa, nw0, jnp.concatenate([shift_c, shift0]), scales0, w_in_full, "hgrn_in_forward")
    o_raw, full = _gla_forward(g5, lb_logits, late_shards)
    x1, res0 = _hgrn_out_forward(o_raw, g5, x2, gnorm, gate0, full["hgrn_w_out"])
    uz, h1 = _ln_mod_matmul(x1, nw1, shift1, scale1, full["pool_w_in"], "pool_in_forward")
    d_x2, loss_part, d_final, d_gate1 = _pool_forward_loss(uz, x1, target, gate1, full["pool_w_grp"], scale_full,
                                                           full["pool_w_out"], final_w)

    d_uz, dw_pool_out, dw_pool_grp, d_pscale = _pool_backward(d_x2, uz, gate1, full["pool_w_grp"], scale_full, full["pool_w_out"])
    d_x1, d_nw1, d_mod1 = _ln_mod_backward(d_uz, full["pool_w_in"], x1, nw1, scale1, d_x2, "pool_in_backward")
    dw_pool_in = _weight_grad(h1, d_uz, "pool_in_weight_grad")
    d_o, d_z, dw_hgrn_out, d_gate0, d_gnorm = _hgrn_out_backward(d_x1, o_raw, g5, res0, gnorm, gate0, full["hgrn_w_out"])
    late_grads = {"hgrn_w_out": dw_hgrn_out, "pool_w_in": dw_pool_in, "pool_w_grp": dw_pool_grp, "pool_w_out": dw_pool_out}
    d_g5, d_lb, late_slots = _gla_backward(g5, lb_logits, d_o, d_z, [late_grads[k].astype(BF16) for k in GATHER_LATE])
    d_x, d_nw0, d_mod0 = _ln_mod_backward(d_g5, w_in_full, xa, nw0, scales0, d_x1, "hgrn_in_backward")
    dw_hgrn_in = _weight_grad(h0, d_g5, "hgrn_in_weight_grad")

    zero = jnp.zeros((1, D_MODEL), F32)
    small = jnp.concatenate([d_mod0[1, 0:2], d_gate0, d_mod1[0, 0:2], d_gate1, d_mod0[0, 0:2], zero, d_nw0, d_nw1, d_gnorm,
                             d_final, d_pscale, d_lb, jnp.broadcast_to(loss_part[0:1, 0:1], (1, D_MODEL)),
                             jnp.zeros((SMALL_ROWS - 17, D_MODEL), F32)], axis=0)
    return {"d_x": d_x, "dw_hgrn_in": dw_hgrn_in, "late_slots": late_slots, "small": small}


def kernel(x, c, ctx, c_ctx, ada_w, ada_b, norm_w, hgrn_w_in, hgrn_lb_logits, hgrn_gnorm_w, hgrn_w_out, pool_w_in, pool_w_grp, pool_scale, pool_w_out, final_norm_w, loss_target, m_c_ctx, m_ada_w, m_ada_b, m_norm_w, m_hgrn_w_in, m_hgrn_lb_logits, m_hgrn_gnorm_w, m_hgrn_w_out, m_pool_w_in, m_pool_w_grp, m_pool_scale, m_pool_w_out, m_final_norm_w, v_c_ctx, v_ada_w, v_ada_b, v_norm_w, v_hgrn_w_in, v_hgrn_lb_logits, v_hgrn_gnorm_w, v_hgrn_w_out, v_pool_w_in, v_pool_w_grp, v_pool_scale, v_pool_w_out, v_final_norm_w):
    xi, yi, ci = _my_place()
    chip = 2 * xi + yi
    dev = 4 * xi + 2 * yi + ci
    ada_cols = ada_w.shape[2]
    lb_cols = hgrn_lb_logits.shape[2]
    ps_cols = pool_scale.shape[1]
    row = lambda a: a.reshape(1, -1)

    def chip_cols(a, n):
        return lax.dynamic_slice_in_dim(a, chip * n, n, axis=a.ndim - 1)

    def from_chips(g, rows_per_dev, take):
        return jnp.concatenate([g[2 * s * rows_per_dev:2 * s * rows_per_dev + take] for s in range(N_CHIPS)], axis=1)

    first = jnp.concatenate([jnp.broadcast_to(c, (8, D_MODEL)), jnp.pad(hgrn_lb_logits[0], ((0, 6), (0, 0))),
                             jnp.pad(pool_scale, ((0, 7), (0, 0)))], axis=1)
    first_all = _all_gather_small(first, "gather_cond")
    cond_all = first_all[::8, :D_MODEL]
    lb_logits = from_chips(first_all[:, D_MODEL:D_MODEL + lb_cols], 8, 2)
    scale_full = from_chips(first_all[:, D_MODEL + lb_cols:], 8, 1)
    cond_rows = jnp.concatenate([cond_all, row(c_ctx), jnp.zeros((7, D_MODEL), F32)], axis=0)

    parts = _mod_parts(cond_rows, ada_w, chip_cols(ada_b, ada_cols))
    parts_all = _all_gather_small(parts.reshape(32, ada_cols), "gather_mod")
    mod_all = from_chips(parts_all, 32, 32).reshape(2, 16, 3 * D_MODEL)
    mod_mine = lax.dynamic_index_in_dim(mod_all, dev, axis=1, keepdims=False)

    shards = {"hgrn_w_in": hgrn_w_in[0], "hgrn_w_out": hgrn_w_out[0], "pool_w_in": pool_w_in[0],
              "pool_w_grp": pool_w_grp[0], "pool_w_out": pool_w_out[0]}
    w_in_full = _all_gather_weights([shards[k].astype(BF16) for k in GATHER_EARLY], GATHER_EARLY)[0]

    loc = _local_step(x[0], ctx[0], loss_target[0], mod_mine, mod_all[0, 8], lb_logits, scale_full, w_in_full,
                      [shards[k].astype(BF16) for k in GATHER_LATE], norm_w, hgrn_gnorm_w, row(final_norm_w))

    small_all = _all_gather_small(loc["small"], "gather_small")
    sums = _sum_devices(small_all)
    loss = sums[16, 0]

    slots = dict(loc["late_slots"])
    slots["hgrn_w_in"] = _scatter_grads([loc["dw_hgrn_in"].astype(BF16)], GATHER_EARLY)[0]
    halves = []
    for k in W_NAMES:
        hs = _half_shape(k)
        flat = _sum_slots(slots[k].reshape(N_DEV, -1, hs[-1]), "sum_" + k)
        halves.append(flat.reshape(hs))
    big_grads = dict(zip(W_NAMES, _exchange_halves(halves)))

    out = {}

    def update(name, w, g, m, v):
        shape = w.shape
        w2, g2, m2, v2 = (a.reshape(-1, shape[-1]) for a in (w, g, m, v))
        d, mn, vn = _adamw(w2, g2, m2, v2, "adamw_" + name)
        out[name] = tuple(a.reshape(shape) for a in (g2, d, mn, vn))

    moments = {"hgrn_w_in": (m_hgrn_w_in, v_hgrn_w_in), "hgrn_w_out": (m_hgrn_w_out, v_hgrn_w_out),
               "pool_w_in": (m_pool_w_in, v_pool_w_in), "pool_w_grp": (m_pool_w_grp, v_pool_w_grp),
               "pool_w_out": (m_pool_w_out, v_pool_w_out)}
    weights = {"hgrn_w_in": hgrn_w_in, "hgrn_w_out": hgrn_w_out, "pool_w_in": pool_w_in, "pool_w_grp": pool_w_grp,
               "pool_w_out": pool_w_out}
    for k in W_NAMES:
        update(k, weights[k], big_grads[k].reshape(weights[k].shape), *moments[k])

    g_ada_b = jnp.stack([(sums[0:3] + sums[6:9]).reshape(-1), sums[3:6].reshape(-1)])
    update("ada_b", ada_b, g_ada_b, m_ada_b, v_ada_b)
    update("norm_w", norm_w, sums[9:11], m_norm_w, v_norm_w)
    update("hgrn_gnorm_w", hgrn_gnorm_w, sums[11:12], m_hgrn_gnorm_w, v_hgrn_gnorm_w)
    update("final_norm_w", row(final_norm_w), sums[12:13], row(m_final_norm_w), row(v_final_norm_w))
    update("pool_scale", pool_scale, chip_cols(sums[13:14], ps_cols), m_pool_scale, v_pool_scale)
    update("hgrn_lb_logits", hgrn_lb_logits, chip_cols(sums[14:16], lb_cols)[None], m_hgrn_lb_logits, v_hgrn_lb_logits)

    per_dev = small_all.reshape(N_DEV, SMALL_ROWS, D_MODEL)
    pad7 = jnp.zeros((7, 3 * D_MODEL), F32)
    dm0 = jnp.concatenate([per_dev[:, 0:3].reshape(N_DEV, -1), sums[6:9].reshape(1, -1), pad7], axis=0)
    dm1 = jnp.concatenate([per_dev[:, 3:6].reshape(N_DEV, -1), jnp.zeros((8, 3 * D_MODEL), F32)], axis=0)
    d_mod = chip_cols(jnp.stack([dm0, dm1]), ada_cols)
    out["ada_w"] = _ada_update(cond_rows.T, d_mod, ada_w, m_ada_w, v_ada_w)

    part_c = _cond_ctx_partial(chip_cols(sums[6:9].reshape(1, -1), ada_cols), ada_w[0])
    part_all = _all_gather_small(jnp.pad(part_c.reshape(1, D_MODEL), ((0, 7), (0, 0))), "gather_cond_ctx")
    g_c, d_c, m_c, v_c = _cond_ctx_update(part_all, row(c_ctx), row(m_c_ctx), row(v_c_ctx))
    out["c_ctx"] = tuple(a.reshape(-1) for a in (g_c, d_c, m_c, v_c))
    out["final_norm_w"] = tuple(a.reshape(-1) for a in out["final_norm_w"])

    names = ["c_ctx", "ada_w", "ada_b", "norm_w", "hgrn_w_in", "hgrn_lb_logits", "hgrn_gnorm_w", "hgrn_w_out", "pool_w_in",
             "pool_w_grp", "pool_scale", "pool_w_out", "final_norm_w"]
    return (loss, loc["d_x"][None], *[out[k][0] for k in names], *[out[k][1] for k in names], *[out[k][2] for k in names],
            *[out[k][3] for k in names])
```

```python
import functools

import numpy as np
import jax
import jax.numpy as jnp
from jax import lax
from jax.experimental import pallas as pl
from jax.experimental.pallas import tpu as pltpu

F32 = jnp.float32
BF16 = jnp.bfloat16

D_MODEL = 1024
SEQ = 2048
CTX_LEN = 256
ROWS_ALL = CTX_LEN + SEQ
HEADS = 8
HEAD_DIM = 128
CHUNK = 64
N_CTX_CHUNKS = CTX_LEN // CHUNK
N_LAT_CHUNKS = SEQ // CHUNK
N_CHUNKS = N_CTX_CHUNKS + N_LAT_CHUNKS
GRID_W = 64
POOL_WINDOWS = (2, 4, 8, 16)
POOL_GROUPS = 4
POOL_GROUP_DIM = 256
HGRN_SECTIONS = 5
POOL_SECTIONS = 2
EPS = 1e-6
N_DEV = 8
N_CHIPS = 4
ROW_TILE = 256
SMALL_ROWS = 24

ADAM_LR = 0.001
ADAM_B1 = 0.9
ADAM_B2 = 0.999
ADAM_EPS = 1e-08
ADAM_WD = 0.01
ADAM_STEP = 10

MESH = pl.DeviceIdType.MESH
MIB = 1 << 20
ANY = pl.BlockSpec(memory_space=pl.ANY)
VMEM = pl.BlockSpec(memory_space=pltpu.VMEM)


def _params(vmem_mib=None):
    if vmem_mib is None:
        return pltpu.CompilerParams()
    return pltpu.CompilerParams(vmem_limit_bytes=vmem_mib * MIB)


def _sig(a):
    return 1.0 / (1.0 + jnp.exp(-a))


def _silu(a):
    return a * _sig(a)


def _dsilu(a):
    s = _sig(a)
    return s * (1.0 + a * (1.0 - s))


def _mm(a, b):
    return jnp.dot(a.astype(BF16), b.astype(BF16), preferred_element_type=F32)


def _mm_nt(a, b):
    return lax.dot_general(a.astype(BF16), b.astype(BF16), (((1,), (1,)), ((), ())), preferred_element_type=F32)


def _mm_tn(a, b):
    return lax.dot_general(a.astype(BF16), b.astype(BF16), (((0,), (0,)), ((), ())), preferred_element_type=F32)


def _split2(a):
    hi = a.astype(BF16)
    lo = (a - hi.astype(F32)).astype(BF16)
    return hi, lo


def _mm_exact_lhs(m_bf, a):
    hi, lo = _split2(a)
    return jnp.dot(m_bf, hi, preferred_element_type=F32) + jnp.dot(m_bf, lo, preferred_element_type=F32)


def _mm_f32(a, b):
    ah, al = _split2(a)
    bh, bl = _split2(b)
    return (jnp.dot(ah, bh, preferred_element_type=F32) + jnp.dot(al, bh, preferred_element_type=F32)
            + jnp.dot(ah, bl, preferred_element_type=F32))


def _my_place():
    return lax.axis_index("x"), lax.axis_index("y"), lax.axis_index("c")


def _all_gather_small(blk, name):
    m_per, n = blk.shape

    def body(x_ref, out_ref, send_sems, recv_sems, local_sem):
        x, y, c = _my_place()
        me, sibling = (x, y, c), (x, y, 1 - c)
        chips = [(1 - x, y), (x, 1 - y), (1 - x, 1 - y)]

        def rows(px, py, pc):
            return out_ref.at[pl.ds((4 * px + 2 * py + pc) * m_per, m_per), :]

        def copy(k, block, to, src=None):
            return pltpu.make_async_remote_copy(
                src_ref=rows(*block) if src is None else src, dst_ref=rows(*block),
                send_sem=send_sems.at[k], recv_sem=recv_sems.at[k], device_id=to, device_id_type=MESH)

        mine = pltpu.make_async_copy(x_ref, rows(*me), local_sem)
        mine.start()
        first = [copy(0, me, sibling, src=x_ref)]
        first += [copy(1 + j, me, (*chip, c), src=x_ref) for j, chip in enumerate(chips)]
        for cp in first:
            cp.start()
        passed = [copy(4 + j, (*chip, c), sibling) for j, chip in enumerate(chips)]
        for j, chip in enumerate(chips):
            copy(1 + j, (*chip, c), me).wait_recv()
            passed[j].start()
        copy(0, sibling, me).wait_recv()
        for j, chip in enumerate(chips):
            copy(4 + j, (*chip, 1 - c), me).wait_recv()
        for cp in first + passed:
            cp.wait_send()
        mine.wait()

    return pl.pallas_call(
        body, name=name,
        out_shape=jax.ShapeDtypeStruct((N_DEV * m_per, n), blk.dtype),
        in_specs=[VMEM], out_specs=VMEM,
        scratch_shapes=[pltpu.SemaphoreType.DMA((7,)), pltpu.SemaphoreType.DMA((7,)), pltpu.SemaphoreType.DMA],
    )(blk)


W_SPECS = {
    "hgrn_w_in": ((D_MODEL, 5 * D_MODEL), (1, 1280, 0, 512)),
    "hgrn_w_out": ((D_MODEL, D_MODEL), (0, 256, 0, 128)),
    "pool_w_in": ((D_MODEL, 2 * D_MODEL), (1, 512, 0, 512)),
    "pool_w_grp": ((POOL_GROUPS, POOL_GROUP_DIM, POOL_GROUP_DIM), (1, 64, 1, 32)),
    "pool_w_out": ((D_MODEL, D_MODEL), (0, 256, 0, 128)),
}
W_NAMES = tuple(W_SPECS)


def _al(v, m):
    return pl.multiple_of(v, m)


def _region(ref, spec, chip, half):
    ca, cn, ha, hn = spec
    idx = [slice(None)] * len(ref.shape)
    if ca == ha:
        if half is None:
            idx[ca] = pl.ds(_al(chip * cn, cn), cn)
        else:
            idx[ca] = pl.ds(_al(chip * cn + half * hn, hn), hn)
    else:
        idx[ca] = pl.ds(_al(chip * cn, cn), cn)
        if half is not None:
            idx[ha] = pl.ds(_al(half * hn, hn), hn)
    return ref.at[tuple(idx)]


def _half_of(ref, spec, half):
    _, _, ha, hn = spec
    idx = [slice(None)] * len(ref.shape)
    idx[ha] = pl.ds(_al(half * hn, hn), hn)
    return ref.at[tuple(idx)]


PIECE_BYTES = 256 * 1024


def _pieces(ref):
    lead = ref.shape[0]
    want = (int(np.prod(ref.shape)) * ref.dtype.itemsize) // PIECE_BYTES
    n = max([1] + [k for k in range(1, want + 1) if lead % k == 0 and (lead // k) % 16 == 0])
    rows = lead // n
    return [ref.at[pl.ds(i * rows, rows)] for i in range(n)]


def _half_shape(name):
    full, (ca, cn, ha, hn) = W_SPECS[name]
    shp = list(full)
    shp[ca] = cn
    shp[ha] = hn
    return tuple(shp)


def _gather_direct(names, sh, full, send_sems, recv_sems, local_sems):
    specs = [W_SPECS[k][1] for k in names]
    x, y, c = _my_place()
    chip_me = 2 * x + y
    chips = [(1 - x, y), (x, 1 - y), (1 - x, 1 - y)]

    def local(a):
        return pltpu.make_async_copy(sh[a], _region(full[a], specs[a], chip_me, None), local_sems.at[a])

    def remote(a, src, dst, to):
        return pltpu.make_async_remote_copy(src_ref=src, dst_ref=dst, send_sem=send_sems.at[a], recv_sem=recv_sems.at[a],
                                            device_id=to, device_id_type=MESH)

    def start():
        for a in range(len(names)):
            local(a).start()
            for px, py in chips:
                remote(a, sh[a], _region(full[a], specs[a], chip_me, None), (px, py, c)).start()

    def wait():
        for a in range(len(names)):
            ca, cn, _, _ = specs[a]
            idx = [slice(None)] * len(full[a].shape)
            idx[ca] = pl.ds(0, 3 * cn)
            three = full[a].at[tuple(idx)]
            remote(a, three, three, (x, y, c)).wait()
            local(a).wait()

    return start, wait


def _scatter_direct(names, part, slots, send_sems, recv_sems, local_sems):
    specs = [W_SPECS[k][1] for k in names]
    x, y, c = _my_place()
    dev_me = 4 * x + 2 * y + c

    def local(a):
        return pltpu.make_async_copy(_region(part[a], specs[a], 2 * x + y, c), slots[a].at[dev_me], local_sems.at[a])

    def start():
        for a in range(len(names)):
            local(a).start()
            for flip in range(1, N_DEV):
                tx = 1 - x if flip >> 2 else x
                ty = 1 - y if (flip >> 1) & 1 else y
                tc = 1 - c if flip & 1 else c
                pltpu.make_async_remote_copy(src_ref=_region(part[a], specs[a], 2 * tx + ty, tc), dst_ref=slots[a].at[dev_me],
                                             send_sem=send_sems.at[a], recv_sem=recv_sems.at[a], device_id=(tx, ty, tc),
                                             device_id_type=MESH).start()

    def wait():
        for a in range(len(names)):
            seven = slots[a].at[pl.ds(0, N_DEV - 1)]
            pltpu.make_async_remote_copy(src_ref=seven, dst_ref=seven, send_sem=send_sems.at[a], recv_sem=recv_sems.at[a],
                                         device_id=(x, y, c), device_id_type=MESH).wait()
            local(a).wait()

    return start, wait


def _comm_sems(n):
    return [pltpu.SemaphoreType.DMA((n,)), pltpu.SemaphoreType.DMA((n,)), pltpu.SemaphoreType.DMA((n,))]


GATHER_EARLY = ("hgrn_w_in",)
GATHER_LATE = ("hgrn_w_out", "pool_w_in", "pool_w_grp", "pool_w_out")


def _all_gather_weights(shards, names):
    n = len(names)
    specs = [W_SPECS[k][1] for k in names]

    def body(*refs):
        sh, full = refs[:n], refs[n:2 * n]
        send_sems, recv_sems, local_sems = refs[2 * n:]
        x, y, c = _my_place()
        chip_me = 2 * x + y
        sibling = (x, y, 1 - c)
        chips = [(1 - x, y), (x, 1 - y), (1 - x, 1 - y)]

        def remote(a, k, src, dst, to):
            return pltpu.make_async_remote_copy(src_ref=src, dst_ref=dst, send_sem=send_sems.at[6 * a + k],
                                                recv_sem=recv_sems.at[6 * a + k], device_id=to, device_id_type=MESH)

        local = [pltpu.make_async_copy(sh[a], _region(full[a], specs[a], chip_me, None), local_sems.at[a]) for a in range(n)]
        for cp in local:
            cp.start()
        sends = []
        for a in range(n):
            for j, (px, py) in enumerate(chips):
                src, dst = _half_of(sh[a], specs[a], c), _region(full[a], specs[a], chip_me, c)
                for s_piece, d_piece in zip(_pieces(src), _pieces(dst)):
                    remote(a, j, s_piece, d_piece, (px, py, c)).start()
                sends.append(remote(a, j, src, dst, (px, py, c)))
        for a in range(n):
            for j, (px, py) in enumerate(chips):
                landed = _region(full[a], specs[a], 2 * px + py, c)
                remote(a, j, landed, landed, (px, py, c)).wait_recv()
                for piece in _pieces(landed):
                    remote(a, 3 + j, piece, piece, sibling).start()
                sends.append(remote(a, 3 + j, landed, landed, sibling))
        for a in range(n):
            for j, (px, py) in enumerate(chips):
                other = _region(full[a], specs[a], 2 * px + py, 1 - c)
                remote(a, 3 + j, other, other, sibling).wait_recv()
        for cp in sends:
            cp.wait_send()
        for cp in local:
            cp.wait()

    return pl.pallas_call(
        body, name="all_gather_weights",
        out_shape=[jax.ShapeDtypeStruct(W_SPECS[k][0], BF16) for k in names],
        in_specs=[VMEM] * n, out_specs=[VMEM] * n,
        scratch_shapes=[pltpu.SemaphoreType.DMA((6 * n,)), pltpu.SemaphoreType.DMA((6 * n,)), pltpu.SemaphoreType.DMA((n,))],
        compiler_params=_params(32),
    )(*shards)


def _slot_shapes(names):
    return [jax.ShapeDtypeStruct((N_DEV,) + _half_shape(k), BF16) for k in names]


def _scatter_grads(parts, names):
    n = len(names)

    def body(*refs):
        start, wait = _scatter_direct(names, refs[:n], refs[n:2 * n], *refs[2 * n:])
        start()
        wait()

    return pl.pallas_call(body, name="scatter_grads", out_shape=_slot_shapes(names), in_specs=[ANY] * n, out_specs=[ANY] * n,
                          scratch_shapes=_comm_sems(n))(*parts)


def _exchange_halves(halves):
    n = len(W_NAMES)
    specs = [W_SPECS[k][1] for k in W_NAMES]

    def shard_shape(k):
        shp = list(_half_shape(k))
        shp[W_SPECS[k][1][2]] *= 2
        return tuple(shp)

    def body(*refs):
        half, out = refs[:n], refs[n:2 * n]
        send_sems, recv_sems, local_sems = refs[2 * n:]
        x, y, c = _my_place()
        sibling = (x, y, 1 - c)

        def remote(a, src, dst):
            return pltpu.make_async_remote_copy(src_ref=src, dst_ref=dst, send_sem=send_sems.at[a], recv_sem=recv_sems.at[a],
                                                device_id=sibling, device_id_type=MESH)

        local = [pltpu.make_async_copy(half[a], _half_of(out[a], specs[a], c), local_sems.at[a]) for a in range(n)]
        for cp in local:
            cp.start()
        for a in range(n):
            mine = _half_of(out[a], specs[a], c)
            for src, dst in zip(_pieces(half[a]), _pieces(mine)):
                remote(a, src, dst).start()
        for a in range(n):
            theirs = _half_of(out[a], specs[a], 1 - c)
            remote(a, theirs, theirs).wait_recv()
        for a in range(n):
            remote(a, half[a], half[a]).wait_send()
        for cp in local:
            cp.wait()

    return pl.pallas_call(
        body, name="exchange_halves",
        out_shape=[jax.ShapeDtypeStruct(shard_shape(k), F32) for k in W_NAMES],
        in_specs=[VMEM] * n, out_specs=[VMEM] * n,
        scratch_shapes=[pltpu.SemaphoreType.DMA((n,)), pltpu.SemaphoreType.DMA((n,)), pltpu.SemaphoreType.DMA((n,))],
        compiler_params=_params(32),
    )(*halves)


def _mod_parts(c_rows, ada_w, ada_b_cols):
    n_layers, _, n_cols = ada_w.shape

    def body(c_ref, w_ref, b_ref, o_ref):
        o_ref[...] = _mm_f32(_silu(c_ref[...]), w_ref[...]) + b_ref[...]

    return pl.pallas_call(
        body, name="mod_parts", grid=(n_layers,),
        out_shape=jax.ShapeDtypeStruct((n_layers, 16, n_cols), F32),
        in_specs=[pl.BlockSpec((16, D_MODEL), lambda i: (0, 0)),
                  pl.BlockSpec((None, D_MODEL, n_cols), lambda i: (i, 0, 0)),
                  pl.BlockSpec((None, 1, n_cols), lambda i: (i, 0, 0))],
        out_specs=pl.BlockSpec((None, 16, n_cols), lambda i: (i, 0, 0)),
        compiler_params=_params(40),
    )(c_rows, ada_w, ada_b_cols.reshape(n_layers, 1, n_cols))


def _ln_mod_matmul(xin, nw, shift, scale, w, name):
    rows = xin.shape[0]
    n_sec = w.shape[1] // D_MODEL
    n_mod = shift.shape[0]
    tm = ROW_TILE

    def body(x_ref, nw_ref, sh_ref, sc_ref, w_ref, g_ref, h_ref):
        xv = x_ref[...]
        r = lax.rsqrt(jnp.mean(xv * xv, axis=-1, keepdims=True) + EPS)
        h = ((xv * r * nw_ref[...]) * (1.0 + sc_ref[...]) + sh_ref[...]).astype(BF16)
        h_ref[...] = h
        for k in range(n_sec):
            g_ref[k] = jnp.dot(h, w_ref[:, k * D_MODEL:(k + 1) * D_MODEL], preferred_element_type=F32)

    mod_spec = pl.BlockSpec((None, 1, D_MODEL), lambda i: (jnp.minimum(i, n_mod - 1), 0, 0))
    return pl.pallas_call(
        body, name=name, grid=(rows // tm,),
        out_shape=[jax.ShapeDtypeStruct((n_sec, rows, D_MODEL), F32), jax.ShapeDtypeStruct((rows, D_MODEL), BF16)],
        in_specs=[pl.BlockSpec((tm, D_MODEL), lambda i: (i, 0)),
                  pl.BlockSpec((1, D_MODEL), lambda i: (0, 0)),
                  mod_spec, mod_spec,
                  pl.BlockSpec((D_MODEL, n_sec * D_MODEL), lambda i: (0, 0))],
        out_specs=[pl.BlockSpec((n_sec, tm, D_MODEL), lambda i: (0, i, 0)),
                   pl.BlockSpec((tm, D_MODEL), lambda i: (i, 0))],
        compiler_params=_params(48),
    )(xin, nw, shift.reshape(n_mod, 1, D_MODEL), scale.reshape(n_mod, 1, D_MODEL), w)


def _chunk_masks(rev):
    rid = lax.broadcasted_iota(jnp.int32, (CHUNK, CHUNK), 0)
    cid = lax.broadcasted_iota(jnp.int32, (CHUNK, CHUNK), 1)
    keep = (cid >= rid) if rev else (cid <= rid)
    keep_t = (cid <= rid) if rev else (cid >= rid)
    one, zero = jnp.ones((CHUNK, CHUNK), F32), jnp.zeros((CHUNK, CHUNK), F32)
    return keep, jnp.where(keep, one, zero).astype(BF16), jnp.where(keep_t, one, zero).astype(BF16)


def _chunk_rows(t, rev, latent):
    n = N_LAT_CHUNKS if latent else N_CTX_CHUNKS
    base = CTX_LEN if latent else 0
    idx = (n - 1 - t) if rev else t
    return pl.multiple_of(base + idx * CHUNK, CHUNK)


def _gates(fpre, lb):
    sg = _sig(fpre)
    f = lb + (1.0 - lb) * sg
    return sg, f, 1.0 - f, jnp.log(f)


G_SPEC = lambda sec: pl.BlockSpec((None, ROWS_ALL, HEAD_DIM), lambda h, sec=sec: (sec, 0, h))


def _gla_forward(g5, lb_logits):
    q_scale = HEAD_DIM ** -0.5

    def body(ff_ref, fb_ref, v_ref, q_ref, lg_ref, o_ref, st_ref, decay_ref, qt_ref):
        for rev in (False, True):
            f_ref = fb_ref if rev else ff_ref
            lb = _sig(lg_ref[1:2, :] if rev else lg_ref[0:1, :])
            keep, tri, _ = _chunk_masks(rev)
            last = 0 if rev else CHUNK - 1
            mid = CHUNK // 2 if rev else CHUNK // 2 - 1

            def local_step(t, carry, latent):
                r0 = _chunk_rows(t, rev, latent)
                rows = pl.ds(r0, CHUNK)
                step = t + (N_CTX_CHUNKS if latent else 0)
                _, _, k, lf = _gates(f_ref[rows, :], lb)
                v = v_ref[rows, :]
                b = _mm_exact_lhs(tri, lf)
                bl = b[last:last + 1, :]
                if latent:
                    q = _silu(q_ref[rows, :]) * q_scale
                    bm = b[mid:mid + 1, :]
                    a = _mm_nt(q * jnp.exp(b - bm), k * jnp.exp(bm - b))
                    o = _mm(jnp.where(keep, a, 0.0), v)
                    orow = pl.ds(pl.multiple_of(r0 - CTX_LEN, CHUNK), CHUNK)
                    qt_ref[orow, :] = (q * jnp.exp(b)).astype(BF16)
                    if rev:
                        o_ref[orow, :] += o
                    else:
                        o_ref[orow, :] = o
                decay_ref[step] = jnp.exp(bl)
                st_ref[step] = _mm_tn(v, k * jnp.exp(bl - b))
                return carry

            lax.fori_loop(0, N_CTX_CHUNKS, functools.partial(local_step, latent=False), 0, unroll=2)
            lax.fori_loop(0, N_LAT_CHUNKS, functools.partial(local_step, latent=True), 0, unroll=4)

            def scan_step(t, st):
                update = st_ref[t]
                st_ref[t] = st
                return st * decay_ref[t] + update

            lax.fori_loop(0, N_CHUNKS, scan_step, jnp.zeros((HEAD_DIM, HEAD_DIM), F32), unroll=2)

            def inter_step(t, carry):
                r0 = _chunk_rows(t, rev, True)
                orow = pl.ds(pl.multiple_of(r0 - CTX_LEN, CHUNK), CHUNK)
                o_ref[orow, :] += lax.dot_general(qt_ref[orow, :], st_ref[t + N_CTX_CHUNKS].astype(BF16),
                                                  (((1,), (1,)), ((), ())), preferred_element_type=F32)
                return carry

            lax.fori_loop(0, N_LAT_CHUNKS, inter_step, 0, unroll=4)

    return pl.pallas_call(
        body, name="gla_forward", grid=(HEADS,),
        out_shape=jax.ShapeDtypeStruct((SEQ, D_MODEL), F32),
        in_specs=[G_SPEC(0), G_SPEC(1), G_SPEC(2), G_SPEC(3), pl.BlockSpec((2, HEAD_DIM), lambda h: (0, h))],
        out_specs=pl.BlockSpec((SEQ, HEAD_DIM), lambda h: (0, h)),
        scratch_shapes=[pltpu.VMEM((N_CHUNKS, HEAD_DIM, HEAD_DIM), F32), pltpu.VMEM((N_CHUNKS, 1, HEAD_DIM), F32),
                        pltpu.VMEM((SEQ, HEAD_DIM), BF16)],
        compiler_params=_params(32),
    )(g5, g5, g5, g5, lb_logits)


def _gla_backward(g5, lb_logits, d_o, d_z):
    q_scale = HEAD_DIM ** -0.5

    def body(ff_ref, fb_ref, v_ref, q_ref, lg_ref, do_ref, dz_ref, dg_ref, dlg_ref, st_ref, dst_ref, decay_ref):
        dg_ref[3, 0:CTX_LEN, :] = jnp.zeros((CTX_LEN, HEAD_DIM), F32)
        dg_ref[4, 0:CTX_LEN, :] = jnp.zeros((CTX_LEN, HEAD_DIM), F32)
        dg_ref[4, CTX_LEN:ROWS_ALL, :] = dz_ref[...]
        is_row = lax.broadcasted_iota(jnp.int32, (CHUNK, HEAD_DIM), 0)
        for rev in (False, True):
            d = 1 if rev else 0
            f_ref = fb_ref if rev else ff_ref
            lb = _sig(lg_ref[d:d + 1, :])
            keep, tri, tri_t = _chunk_masks(rev)
            last = 0 if rev else CHUNK - 1
            mid = CHUNK // 2 if rev else CHUNK // 2 - 1

            def local_step(t, carry, latent):
                r0 = _chunk_rows(t, rev, latent)
                rows = pl.ds(r0, CHUNK)
                step = t + (N_CTX_CHUNKS if latent else 0)
                _, _, k, lf = _gates(f_ref[rows, :], lb)
                b = _mm_exact_lhs(tri, lf)
                bl = b[last:last + 1, :]
                decay_ref[step] = jnp.exp(bl)
                st_ref[step] = _mm_tn(v_ref[rows, :], k * jnp.exp(bl - b))
                if latent:
                    q_t = _silu(q_ref[rows, :]) * q_scale * jnp.exp(b)
                    dst_ref[step] = _mm_tn(do_ref[pl.ds(pl.multiple_of(r0 - CTX_LEN, CHUNK), CHUNK), :], q_t)
                else:
                    dst_ref[step] = jnp.zeros((HEAD_DIM, HEAD_DIM), F32)
                return carry

            lax.fori_loop(0, N_CTX_CHUNKS, functools.partial(local_step, latent=False), 0, unroll=2)
            lax.fori_loop(0, N_LAT_CHUNKS, functools.partial(local_step, latent=True), 0, unroll=4)

            def scan_step(i, carry):
                st, d_st = carry
                j = N_CHUNKS - 1 - i
                update, d_update = st_ref[i], dst_ref[j]
                st_ref[i] = st
                dst_ref[j] = d_st
                return st * decay_ref[i] + update, d_st * decay_ref[j] + d_update

            zero_state = jnp.zeros((HEAD_DIM, HEAD_DIM), F32)
            lax.fori_loop(0, N_CHUNKS, scan_step, (zero_state, zero_state))

            def grad_step(t, dlb, latent):
                r0 = _chunk_rows(t, rev, latent)
                rows = pl.ds(r0, CHUNK)
                step = t + (N_CTX_CHUNKS if latent else 0)
                sg, f, k, lf = _gates(f_ref[rows, :], lb)
                v = v_ref[rows, :]
                b = _mm_exact_lhs(tri, lf)
                bl = b[last:last + 1, :]
                e_end = jnp.exp(bl - b)
                k_end = k * e_end
                decay = jnp.exp(bl)
                d_st = dst_ref[step]
                st_prev = st_ref[step]
                d_kend = _mm(v, d_st)
                d_decay = jnp.sum(d_st * st_prev, axis=0, keepdims=True)
                t_kend = d_kend * k_end
                d_v = _mm_nt(k_end, d_st)
                d_k = d_kend * e_end
                d_b = -t_kend
                if latent:
                    qpre = q_ref[rows, :]
                    q = _silu(qpre) * q_scale
                    bm = b[mid:mid + 1, :]
                    e_b, e_qm, e_km = jnp.exp(b), jnp.exp(b - bm), jnp.exp(bm - b)
                    q_t, q_m, k_m = q * e_b, q * e_qm, k * e_km
                    a = jnp.where(keep, _mm_nt(q_m, k_m), 0.0)
                    d_out = do_ref[pl.ds(pl.multiple_of(r0 - CTX_LEN, CHUNK), CHUNK), :]
                    d_a = jnp.where(keep, _mm_nt(d_out, v), 0.0)
                    d_qm = _mm(d_a, k_m)
                    d_km = _mm_tn(d_a, q_m)
                    d_qt = _mm(d_out, st_prev)
                    d_v = d_v + _mm_tn(a, d_out)
                    d_k = d_k + d_km * e_km
                    d_b = d_b + d_qt * q_t + d_qm * q_m - d_km * k_m
                    d_q = d_qt * e_b + d_qm * e_qm
                at_last = jnp.sum(t_kend, axis=0, keepdims=True) + d_decay * decay
                d_b = d_b + jnp.where(is_row == last, at_last, 0.0)
                d_lf = _mm_exact_lhs(tri_t, d_b)
                d_f = d_lf / f - d_k
                dg_ref[d, rows, :] = d_f * (1.0 - lb) * sg * (1.0 - sg)
                if rev:
                    dg_ref[2, rows, :] += d_v
                else:
                    dg_ref[2, rows, :] = d_v
                if latent:
                    d_qpre = d_q * q_scale * _dsilu(qpre)
                    if rev:
                        dg_ref[3, rows, :] += d_qpre
                    else:
                        dg_ref[3, rows, :] = d_qpre
                return dlb + jnp.sum(d_f * (1.0 - sg), axis=0, keepdims=True)

            dlb = lax.fori_loop(0, N_LAT_CHUNKS, functools.partial(grad_step, latent=True), jnp.zeros((1, HEAD_DIM), F32),
                                unroll=2)
            dlb = lax.fori_loop(0, N_CTX_CHUNKS, functools.partial(grad_step, latent=False), dlb, unroll=2)
            dlg_ref[d:d + 1, :] = dlb * lb * (1.0 - lb)

    col = pl.BlockSpec((SEQ, HEAD_DIM), lambda h: (0, h))
    return pl.pallas_call(
        body, name="gla_backward", grid=(HEADS,),
        out_shape=[jax.ShapeDtypeStruct((HGRN_SECTIONS, ROWS_ALL, D_MODEL), F32), jax.ShapeDtypeStruct((2, D_MODEL), F32)],
        in_specs=[G_SPEC(0), G_SPEC(1), G_SPEC(2), G_SPEC(3), pl.BlockSpec((2, HEAD_DIM), lambda h: (0, h)), col, col],
        out_specs=[pl.BlockSpec((HGRN_SECTIONS, ROWS_ALL, HEAD_DIM), lambda h: (0, 0, h)),
                   pl.BlockSpec((2, HEAD_DIM), lambda h: (0, h))],
        scratch_shapes=[pltpu.VMEM((N_CHUNKS, HEAD_DIM, HEAD_DIM), F32), pltpu.VMEM((N_CHUNKS, HEAD_DIM, HEAD_DIM), F32),
                        pltpu.VMEM((N_CHUNKS, 1, HEAD_DIM), F32)],
        compiler_params=_params(48),
    )(g5, g5, g5, g5, lb_logits, d_o, d_z)


GROUP = 2 * CHUNK


def _group_masks(rev):
    rid = lax.broadcasted_iota(jnp.int32, (GROUP, GROUP), 0)
    cid = lax.broadcasted_iota(jnp.int32, (GROUP, GROUP), 1)
    same = (rid >= CHUNK) == (cid >= CHUNK)
    causal = (cid >= rid) if rev else (cid <= rid)
    anti = (cid <= rid) if rev else (cid >= rid)
    bf = lambda m: jnp.where(m, jnp.ones((GROUP, GROUP), F32), jnp.zeros((GROUP, GROUP), F32)).astype(BF16)
    keep = same & causal
    return keep, bf(keep), bf(same & anti), bf(same & jnp.logical_not(anti))


def _group_sum(m_bf, a):
    hi, lo = _split2(a)
    r = jnp.dot(m_bf, jnp.concatenate([hi, lo], axis=1), preferred_element_type=F32)
    return r[:, :HEAD_DIM] + r[:, HEAD_DIM:]


def _chunk_row(a, pos):
    return jnp.concatenate([jnp.broadcast_to(a[c * CHUNK + pos:c * CHUNK + pos + 1, :], (CHUNK, HEAD_DIM)) for c in range(2)], axis=0)


def _by_chunk(a, second):
    return jnp.concatenate([jnp.where(second, 0.0, a), jnp.where(second, a, 0.0)], axis=1)


def _own_block(r):
    return jnp.concatenate([r[0:CHUNK, 0:HEAD_DIM], r[CHUNK:GROUP, HEAD_DIM:2 * HEAD_DIM]], axis=0)


def _scan_step_of(row_chunk, rev, latent):
    if not rev:
        return row_chunk
    return (N_CHUNKS + N_CTX_CHUNKS - 1 - row_chunk) if latent else (N_CTX_CHUNKS - 1 - row_chunk)


def _group_rows(i, j, per_step, latent):
    base = CTX_LEN if latent else 0
    return pl.multiple_of(base + (i * per_step + j) * GROUP, GROUP)


GROUPS_PER_STEP = 4


def _gla_forward(g5, lb_logits, late_shards):
    q_scale = HEAD_DIM ** -0.5
    per_lat, per_ctx = GROUPS_PER_STEP, min(GROUPS_PER_STEP, CTX_LEN // GROUP)
    n_late = len(GATHER_LATE)

    def body(ff_ref, fb_ref, v_ref, q_ref, lg_ref, *rest):
        shard_refs, o_ref, full_refs = rest[:n_late], rest[n_late], rest[n_late + 1:2 * n_late + 1]
        st_ref, decay_ref, qt_ref = rest[2 * n_late + 1:2 * n_late + 4]
        start_gather, wait_gather = _gather_direct(GATHER_LATE, shard_refs, full_refs, *rest[2 * n_late + 4:])

        @pl.when(pl.program_id(0) == 0)
        def _():
            start_gather()

        second = lax.broadcasted_iota(jnp.int32, (GROUP, HEAD_DIM), 0) >= CHUNK
        for rev in (False, True):
            f_ref = fb_ref if rev else ff_ref
            lb = _sig(lg_ref[1:2, :] if rev else lg_ref[0:1, :])
            keep, tri, _, _ = _group_masks(rev)
            last = 0 if rev else CHUNK - 1
            mid = CHUNK // 2 if rev else CHUNK // 2 - 1

            def local_step(i, carry, latent, per):
                r0s = [_group_rows(i, j, per, latent) for j in range(per)]
                rows = [pl.ds(r0, GROUP) for r0 in r0s]
                gates = [_gates(f_ref[r, :], lb) for r in rows]
                vs = [v_ref[r, :] for r in rows]
                bs = [_group_sum(tri, g[3]) for g in gates]
                bls = [_chunk_row(b, last) for b in bs]
                ups = [_mm_tn(v, _by_chunk(g[2] * jnp.exp(bl - b), second)) for v, g, b, bl in zip(vs, gates, bs, bls)]
                if latent:
                    qs = [_silu(q_ref[r, :]) * q_scale for r in rows]
                    bms = [_chunk_row(b, mid) for b in bs]
                    a_s = [_mm_nt(q * jnp.exp(b - bm), g[2] * jnp.exp(bm - b)) for q, g, b, bm in zip(qs, gates, bs, bms)]
                    outs = [_mm(jnp.where(keep, a, 0.0), v) for a, v in zip(a_s, vs)]
                for j in range(per):
                    for c in range(2):
                        step = _scan_step_of(r0s[j] // CHUNK + c, rev, latent)
                        decay_ref[step] = jnp.exp(bls[j][c * CHUNK:c * CHUNK + 1, :])
                        st_ref[step] = ups[j][:, c * HEAD_DIM:(c + 1) * HEAD_DIM]
                    if latent:
                        orow = pl.ds(pl.multiple_of(r0s[j] - CTX_LEN, GROUP), GROUP)
                        qt_ref[orow, :] = (qs[j] * jnp.exp(bs[j])).astype(BF16)
                        if rev:
                            o_ref[orow, :] += outs[j]
                        else:
                            o_ref[orow, :] = outs[j]
                return carry

            lax.fori_loop(0, CTX_LEN // (per_ctx * GROUP), functools.partial(local_step, latent=False, per=per_ctx), 0)
            lax.fori_loop(0, SEQ // (per_lat * GROUP), functools.partial(local_step, latent=True, per=per_lat), 0)

            def scan_step(t, st):
                update = st_ref[t]
                st_ref[t] = st
                return st * decay_ref[t] + update

            lax.fori_loop(0, N_CHUNKS, scan_step, jnp.zeros((HEAD_DIM, HEAD_DIM), F32), unroll=2)

            def inter_step(i, carry):
                r0s = [_group_rows(i, j, per_lat, True) for j in range(per_lat)]
                orows = [pl.ds(pl.multiple_of(r0 - CTX_LEN, GROUP), GROUP) for r0 in r0s]
                states = [jnp.concatenate([st_ref[_scan_step_of(r0 // CHUNK + c, rev, True)] for c in range(2)], axis=0)
                          for r0 in r0s]
                prods = [lax.dot_general(qt_ref[orow, :], s.astype(BF16), (((1,), (1,)), ((), ())), preferred_element_type=F32)
                         for orow, s in zip(orows, states)]
                for orow, r in zip(orows, prods):
                    o_ref[orow, :] += _own_block(r)
                return carry

            lax.fori_loop(0, SEQ // (per_lat * GROUP), inter_step, 0)

        @pl.when(pl.program_id(0) == HEADS - 1)
        def _():
            wait_gather()

    outs = pl.pallas_call(
        body, name="gla_forward", grid=(HEADS,),
        out_shape=[jax.ShapeDtypeStruct((SEQ, D_MODEL), F32)] + [jax.ShapeDtypeStruct(W_SPECS[k][0], BF16) for k in GATHER_LATE],
        in_specs=[G_SPEC(0), G_SPEC(1), G_SPEC(2), G_SPEC(3), pl.BlockSpec((2, HEAD_DIM), lambda h: (0, h))] + [ANY] * n_late,
        out_specs=[pl.BlockSpec((SEQ, HEAD_DIM), lambda h: (0, h))] + [ANY] * n_late,
        scratch_shapes=[pltpu.VMEM((N_CHUNKS, HEAD_DIM, HEAD_DIM), F32), pltpu.VMEM((N_CHUNKS, 1, HEAD_DIM), F32),
                        pltpu.VMEM((SEQ, HEAD_DIM), BF16)] + _comm_sems(n_late),
        compiler_params=_params(32),
    )(g5, g5, g5, g5, lb_logits, *late_shards)
    return outs[0], dict(zip(GATHER_LATE, outs[1:]))


def _gla_backward(g5, lb_logits, d_o, d_z, late_parts):
    q_scale = HEAD_DIM ** -0.5
    per_lat, per_ctx = GROUPS_PER_STEP, min(GROUPS_PER_STEP, CTX_LEN // GROUP)
    n_late = len(GATHER_LATE)

    def body(ff_ref, fb_ref, v_ref, q_ref, lg_ref, do_ref, dz_ref, *rest):
        part_refs, (dg_ref, dlg_ref), slot_refs = rest[:n_late], rest[n_late:n_late + 2], rest[n_late + 2:2 * n_late + 2]
        st_ref, dst_ref, decay_ref = rest[2 * n_late + 2:2 * n_late + 5]
        start_scatter, wait_scatter = _scatter_direct(GATHER_LATE, part_refs, slot_refs, *rest[2 * n_late + 5:])

        @pl.when(pl.program_id(0) == 0)
        def _():
            start_scatter()

        dg_ref[3, 0:CTX_LEN, :] = jnp.zeros((CTX_LEN, HEAD_DIM), F32)
        dg_ref[4, 0:CTX_LEN, :] = jnp.zeros((CTX_LEN, HEAD_DIM), F32)
        dg_ref[4, CTX_LEN:ROWS_ALL, :] = dz_ref[...]
        second = lax.broadcasted_iota(jnp.int32, (GROUP, HEAD_DIM), 0) >= CHUNK
        for rev in (False, True):
            d = 1 if rev else 0
            f_ref = fb_ref if rev else ff_ref
            lb = _sig(lg_ref[d:d + 1, :])
            keep, tri, tri_t, strict = _group_masks(rev)
            last = 0 if rev else CHUNK - 1
            mid = CHUNK // 2 if rev else CHUNK // 2 - 1

            def local_step(i, carry, latent, per):
                r0s = [_group_rows(i, j, per, latent) for j in range(per)]
                rows = [pl.ds(r0, GROUP) for r0 in r0s]
                gates = [_gates(f_ref[r, :], lb) for r in rows]
                bs = [_group_sum(tri, g[3]) for g in gates]
                bls = [_chunk_row(b, last) for b in bs]
                ups = [_mm_tn(v_ref[r, :], _by_chunk(g[2] * jnp.exp(bl - b), second)) for r, g, b, bl in zip(rows, gates, bs, bls)]
                if latent:
                    orows = [pl.ds(pl.multiple_of(r0 - CTX_LEN, GROUP), GROUP) for r0 in r0s]
                    d_ups = [_mm_tn(do_ref[orow, :], _by_chunk(_silu(q_ref[r, :]) * q_scale * jnp.exp(b), second))
                             for orow, r, b in zip(orows, rows, bs)]
                for j in range(per):
                    for c in range(2):
                        step = _scan_step_of(r0s[j] // CHUNK + c, rev, latent)
                        decay_ref[step] = jnp.exp(bls[j][c * CHUNK:c * CHUNK + 1, :])
                        st_ref[step] = ups[j][:, c * HEAD_DIM:(c + 1) * HEAD_DIM]
                        if latent:
                            dst_ref[step] = d_ups[j][:, c * HEAD_DIM:(c + 1) * HEAD_DIM]
                        else:
                            dst_ref[step] = jnp.zeros((HEAD_DIM, HEAD_DIM), F32)
                return carry

            lax.fori_loop(0, CTX_LEN // (per_ctx * GROUP), functools.partial(local_step, latent=False, per=per_ctx), 0)
            lax.fori_loop(0, SEQ // (per_lat * GROUP), functools.partial(local_step, latent=True, per=per_lat), 0)

            def scan_step(i, carry):
                st, d_st = carry
                j = N_CHUNKS - 1 - i
                update, d_update = st_ref[i], dst_ref[j]
                st_ref[i] = st
                dst_ref[j] = d_st
                return st * decay_ref[i] + update, d_st * decay_ref[j] + d_update

            zero_state = jnp.zeros((HEAD_DIM, HEAD_DIM), F32)
            lax.fori_loop(0, N_CHUNKS, scan_step, (zero_state, zero_state))

            def grad_step(i, dlb, latent, per):
                r0s = [_group_rows(i, j, per, latent) for j in range(per)]
                rows = [pl.ds(r0, GROUP) for r0 in r0s]
                gates = [_gates(f_ref[r, :], lb) for r in rows]
                vs = [v_ref[r, :] for r in rows]
                bs = [_group_sum(tri, g[3]) for g in gates]
                bls = [_chunk_row(b, last) for b in bs]
                e_ends = [jnp.exp(bl - b) for b, bl in zip(bs, bls)]
                k_ends = [g[2] * e for g, e in zip(gates, e_ends)]
                sts = [[st_ref[_scan_step_of(r0 // CHUNK + c, rev, latent)] for c in range(2)] for r0 in r0s]
                d_sts = [[dst_ref[_scan_step_of(r0 // CHUNK + c, rev, latent)] for c in range(2)] for r0 in r0s]
                d_kends = [_own_block(_mm(v, jnp.concatenate(ds, axis=1))) for v, ds in zip(vs, d_sts)]
                d_vs = [_own_block(_mm_nt(ke, jnp.concatenate(ds, axis=0))) for ke, ds in zip(k_ends, d_sts)]
                at_last = [jnp.concatenate([jnp.broadcast_to(jnp.sum(ds[c] * s[c], axis=0, keepdims=True), (CHUNK, HEAD_DIM))
                                            for c in range(2)], axis=0) * jnp.exp(bl) for ds, s, bl in zip(d_sts, sts, bls)]
                t_kends = [dk * ke for dk, ke in zip(d_kends, k_ends)]
                d_ks = [dk * e for dk, e in zip(d_kends, e_ends)]
                d_lfs = [_group_sum(strict, t) + al for t, al in zip(t_kends, at_last)]
                if latent:
                    orows = [pl.ds(pl.multiple_of(r0 - CTX_LEN, GROUP), GROUP) for r0 in r0s]
                    qpres = [q_ref[r, :] for r in rows]
                    qs = [_silu(qp) * q_scale for qp in qpres]
                    bms = [_chunk_row(b, mid) for b in bs]
                    e_bs = [jnp.exp(b) for b in bs]
                    e_qms = [jnp.exp(b - bm) for b, bm in zip(bs, bms)]
                    e_kms = [jnp.exp(bm - b) for b, bm in zip(bs, bms)]
                    q_ts = [q * e for q, e in zip(qs, e_bs)]
                    q_ms = [q * e for q, e in zip(qs, e_qms)]
                    k_ms = [g[2] * e for g, e in zip(gates, e_kms)]
                    d_outs = [do_ref[orow, :] for orow in orows]
                    a_s = [jnp.where(keep, _mm_nt(qm, km), 0.0) for qm, km in zip(q_ms, k_ms)]
                    d_as = [jnp.where(keep, _mm_nt(do, v), 0.0) for do, v in zip(d_outs, vs)]
                    d_qts = [_own_block(_mm(do, jnp.concatenate(s, axis=1))) for do, s in zip(d_outs, sts)]
                    d_qms = [_mm(da, km) for da, km in zip(d_as, k_ms)]
                    d_kms = [_mm_tn(da, qm) for da, qm in zip(d_as, q_ms)]
                    d_vs = [dv + _mm_tn(a, do) for dv, a, do in zip(d_vs, a_s, d_outs)]
                    d_ks = [dk + dkm * e for dk, dkm, e in zip(d_ks, d_kms, e_kms)]
                    d_lfs = [dl + _group_sum(tri_t, dqt * qt + dqm * qm - dkm * km)
                             for dl, dqt, qt, dqm, qm, dkm, km in zip(d_lfs, d_qts, q_ts, d_qms, q_ms, d_kms, k_ms)]
                    d_qs = [dqt * eb + dqm * eq for dqt, eb, dqm, eq in zip(d_qts, e_bs, d_qms, e_qms)]
                for j in range(per):
                    sg, f = gates[j][0], gates[j][1]
                    d_f = d_lfs[j] / f - d_ks[j]
                    dg_ref[d, rows[j], :] = d_f * (1.0 - lb) * sg * (1.0 - sg)
                    dlb = dlb + jnp.sum(d_f * (1.0 - sg), axis=0, keepdims=True)
                    if rev:
                        dg_ref[2, rows[j], :] += d_vs[j]
                    else:
                        dg_ref[2, rows[j], :] = d_vs[j]
                    if latent:
                        d_qpre = d_qs[j] * q_scale * _dsilu(qpres[j])
                        if rev:
                            dg_ref[3, rows[j], :] += d_qpre
                        else:
                            dg_ref[3, rows[j], :] = d_qpre
                return dlb

            dlb = lax.fori_loop(0, SEQ // (per_lat * GROUP), functools.partial(grad_step, latent=True, per=per_lat),
                                jnp.zeros((1, HEAD_DIM), F32))
            dlb = lax.fori_loop(0, CTX_LEN // (per_ctx * GROUP), functools.partial(grad_step, latent=False, per=per_ctx), dlb)
            dlg_ref[d:d + 1, :] = dlb * lb * (1.0 - lb)

        @pl.when(pl.program_id(0) == HEADS - 1)
        def _():
            wait_scatter()

    col = pl.BlockSpec((SEQ, HEAD_DIM), lambda h: (0, h))
    outs = pl.pallas_call(
        body, name="gla_backward", grid=(HEADS,),
        out_shape=[jax.ShapeDtypeStruct((HGRN_SECTIONS, ROWS_ALL, D_MODEL), F32), jax.ShapeDtypeStruct((2, D_MODEL), F32)]
        + _slot_shapes(GATHER_LATE),
        in_specs=[G_SPEC(0), G_SPEC(1), G_SPEC(2), G_SPEC(3), pl.BlockSpec((2, HEAD_DIM), lambda h: (0, h)), col, col]
        + [ANY] * n_late,
        out_specs=[pl.BlockSpec((HGRN_SECTIONS, ROWS_ALL, HEAD_DIM), lambda h: (0, 0, h)),
                   pl.BlockSpec((2, HEAD_DIM), lambda h: (0, h))] + [ANY] * n_late,
        scratch_shapes=[pltpu.VMEM((N_CHUNKS, HEAD_DIM, HEAD_DIM), F32), pltpu.VMEM((N_CHUNKS, HEAD_DIM, HEAD_DIM), F32),
                        pltpu.VMEM((N_CHUNKS, 1, HEAD_DIM), F32)] + _comm_sems(n_late),
        compiler_params=_params(48),
    )(g5, g5, g5, g5, lb_logits, d_o, d_z, *late_parts)
    return outs[0], outs[1], dict(zip(GATHER_LATE, outs[2:]))


def _head_norm(o, gw, scr):
    rs = []
    for h in range(HEADS):
        cols = slice(h * HEAD_DIM, (h + 1) * HEAD_DIM)
        oh = o[:, cols]
        r = lax.rsqrt(jnp.mean(oh * oh, axis=-1, keepdims=True) + EPS)
        scr[:, cols] = oh * r
        rs.append(r)
    return rs


def _hgrn_out_forward(o_raw, g5, xin, gnorm_w, gate, w_out):
    tm = ROW_TILE

    def body(o_ref, z_ref, x_ref, gw_ref, gate_ref, w_ref, x1_ref, res_ref, scr):
        _head_norm(o_ref[...], None, scr)
        a = scr[...] * gw_ref[...] * _silu(z_ref[...])
        res = _mm(a, w_ref[...])
        res_ref[...] = res
        x1_ref[...] = x_ref[...] + gate_ref[...] * res

    tile = pl.BlockSpec((tm, D_MODEL), lambda i: (i, 0))
    vec = pl.BlockSpec((1, D_MODEL), lambda i: (0, 0))
    return pl.pallas_call(
        body, name="hgrn_out_forward", grid=(SEQ // tm,),
        out_shape=[jax.ShapeDtypeStruct((SEQ, D_MODEL), F32)] * 2,
        in_specs=[tile, pl.BlockSpec((None, tm, D_MODEL), lambda i: (4, i + CTX_LEN // tm, 0)), tile, vec, vec,
                  pl.BlockSpec((D_MODEL, D_MODEL), lambda i: (0, 0))],
        out_specs=[tile, tile],
        scratch_shapes=[pltpu.VMEM((tm, D_MODEL), F32)],
        compiler_params=_params(32),
    )(o_raw, g5, xin, gnorm_w, gate, w_out)


def _hgrn_out_backward(d_x1, o_raw, g5, res, gnorm_w, gate, w_out):
    tm = ROW_TILE

    def body(dx_ref, o_ref, z_ref, res_ref, gw_ref, gate_ref, w_ref, do_ref, dz_ref, dw_ref, dgate_ref, dgw_ref, scr, scr2):
        @pl.when(pl.program_id(0) == 0)
        def _():
            dw_ref[...] = jnp.zeros_like(dw_ref)
            dgate_ref[...] = jnp.zeros_like(dgate_ref)
            dgw_ref[...] = jnp.zeros_like(dgw_ref)

        dx = dx_ref[...]
        dgate_ref[...] += jnp.sum(dx * res_ref[...], axis=0, keepdims=True)
        d_res = (dx * gate_ref[...]).astype(BF16)
        d_a = _mm_nt(d_res, w_ref[...])
        rs = _head_norm(o_ref[...], None, scr)
        z = z_ref[...]
        sz = _silu(z)
        o_hat = scr[...]
        o_n = o_hat * gw_ref[...]
        dw_ref[...] += _mm_tn(o_n * sz, d_res)
        d_on = d_a * sz
        dz_ref[...] = d_a * o_n * _dsilu(z)
        dgw_ref[...] += jnp.sum(d_on * o_hat, axis=0, keepdims=True)
        scr2[...] = d_on * gw_ref[...]
        for h in range(HEADS):
            cols = slice(h * HEAD_DIM, (h + 1) * HEAD_DIM)
            dh, oh = scr2[:, cols], scr[:, cols]
            do_ref[:, cols] = rs[h] * (dh - oh * jnp.mean(dh * oh, axis=-1, keepdims=True))

    tile = pl.BlockSpec((tm, D_MODEL), lambda i: (i, 0))
    vec = pl.BlockSpec((1, D_MODEL), lambda i: (0, 0))
    mat = pl.BlockSpec((D_MODEL, D_MODEL), lambda i: (0, 0))
    return pl.pallas_call(
        body, name="hgrn_out_backward", grid=(SEQ // tm,),
        out_shape=[jax.ShapeDtypeStruct((SEQ, D_MODEL), F32)] * 2 + [jax.ShapeDtypeStruct((D_MODEL, D_MODEL), F32)]
        + [jax.ShapeDtypeStruct((1, D_MODEL), F32)] * 2,
        in_specs=[tile, tile, pl.BlockSpec((None, tm, D_MODEL), lambda i: (4, i + CTX_LEN // tm, 0)), tile, vec, vec, mat],
        out_specs=[tile, tile, mat, vec, vec],
        scratch_shapes=[pltpu.VMEM((tm, D_MODEL), F32)] * 2,
        compiler_params=_params(40),
    )(d_x1, o_raw, g5, res, gnorm_w, gate, w_out)


def _pool_constants():
    win = np.zeros((POOL_GROUPS, ROW_TILE, ROW_TILE), np.float32)
    inv = np.zeros((POOL_GROUPS, ROW_TILE, 1), np.float32)
    for g, w in enumerate(POOL_WINDOWS):
        for t in range(ROW_TILE):
            base, p = (t // GRID_W) * GRID_W, t % GRID_W
            lo = min(max(p - w // 2, 0), GRID_W)
            hi = min(max(p - w // 2 + w, 0), GRID_W)
            win[g, t, base + lo:base + hi] = 1.0
            inv[g, t, 0] = 1.0 / np.float32(hi - lo)
    return jnp.asarray(win, BF16), jnp.asarray(win.transpose(0, 2, 1), BF16), jnp.asarray(inv, F32)


def _pool_mix(u_ref, wg_ref, ps_ref, win_ref, inv_ref, pooled_scr, yg_scr):
    for g in range(POOL_GROUPS):
        cols = slice(g * POOL_GROUP_DIM, (g + 1) * POOL_GROUP_DIM)
        ug = u_ref[:, cols]
        pooled = _mm_exact_lhs(win_ref[g], ug) * inv_ref[g] - ug
        if pooled_scr is not None:
            pooled_scr[:, cols] = pooled
        yg_scr[:, cols] = _mm(pooled, wg_ref[g])


def _pool_forward_loss(uz, x1, target, gate, w_grp, pool_scale, w_out, final_w):
    tm = ROW_TILE
    win, _, inv = _pool_constants()

    def body(u_ref, z_ref, x_ref, t_ref, gate_ref, wg_ref, ps_ref, w_ref, fw_ref, win_ref, inv_ref,
             dx_ref, loss_ref, dfw_ref, dgate_ref, yg_scr):
        @pl.when(pl.program_id(0) == 0)
        def _():
            loss_ref[...] = jnp.zeros_like(loss_ref)
            dfw_ref[...] = jnp.zeros_like(dfw_ref)
            dgate_ref[...] = jnp.zeros_like(dgate_ref)

        _pool_mix(u_ref, wg_ref, ps_ref, win_ref, inv_ref, None, yg_scr)
        a = yg_scr[...] * ps_ref[...] * _silu(z_ref[...])
        res = _mm(a, w_ref[...])
        x2 = x_ref[...] + gate_ref[...] * res
        r = lax.rsqrt(jnp.mean(x2 * x2, axis=-1, keepdims=True) + EPS)
        xh = x2 * r
        fw = fw_ref[...]
        err = xh * fw - t_ref[...]
        loss_ref[...] += 0.5 * jnp.sum(jnp.mean(err * err, axis=-1, keepdims=True))
        d_y = err * (1.0 / D_MODEL)
        dfw_ref[...] += jnp.sum(d_y * xh, axis=0, keepdims=True)
        d_xh = d_y * fw
        d_x2 = r * (d_xh - xh * jnp.mean(d_xh * xh, axis=-1, keepdims=True))
        dx_ref[...] = d_x2
        dgate_ref[...] += jnp.sum(d_x2 * res, axis=0, keepdims=True)

    tile = pl.BlockSpec((tm, D_MODEL), lambda i: (i, 0))
    vec = pl.BlockSpec((1, D_MODEL), lambda i: (0, 0))
    grp = pl.BlockSpec((POOL_GROUPS, POOL_GROUP_DIM, POOL_GROUP_DIM), lambda i: (0, 0, 0))
    return pl.pallas_call(
        body, name="pool_forward_loss", grid=(SEQ // tm,),
        out_shape=[jax.ShapeDtypeStruct((SEQ, D_MODEL), F32), jax.ShapeDtypeStruct((8, 128), F32),
                   jax.ShapeDtypeStruct((1, D_MODEL), F32), jax.ShapeDtypeStruct((1, D_MODEL), F32)],
        in_specs=[pl.BlockSpec((None, tm, D_MODEL), lambda i: (0, i, 0)), pl.BlockSpec((None, tm, D_MODEL), lambda i: (1, i, 0)),
                  tile, tile, vec, grp, vec, pl.BlockSpec((D_MODEL, D_MODEL), lambda i: (0, 0)), vec, grp,
                  pl.BlockSpec((POOL_GROUPS, ROW_TILE, 1), lambda i: (0, 0, 0))],
        out_specs=[tile, pl.BlockSpec((8, 128), lambda i: (0, 0)), vec, vec],
        scratch_shapes=[pltpu.VMEM((tm, D_MODEL), F32)],
        compiler_params=_params(32),
    )(uz, uz, x1, target, gate, w_grp, pool_scale, w_out, final_w, win, inv)


def _pool_backward(d_x2, uz, gate, w_grp, pool_scale, w_out):
    tm = ROW_TILE
    win, win_t, inv = _pool_constants()

    def body(dx_ref, u_ref, z_ref, gate_ref, wg_ref, ps_ref, w_ref, win_ref, wint_ref, inv_ref,
             duz_ref, dw_ref, dwg_ref, dps_ref, pooled_scr, yg_scr, dyg_scr):
        @pl.when(pl.program_id(0) == 0)
        def _():
            dw_ref[...] = jnp.zeros_like(dw_ref)
            dwg_ref[...] = jnp.zeros_like(dwg_ref)
            dps_ref[...] = jnp.zeros_like(dps_ref)

        _pool_mix(u_ref, wg_ref, ps_ref, win_ref, inv_ref, pooled_scr, yg_scr)
        z = z_ref[...]
        sz = _silu(z)
        yg = yg_scr[...]
        y = yg * ps_ref[...]
        d_res = (dx_ref[...] * gate_ref[...]).astype(BF16)
        d_a = _mm_nt(d_res, w_ref[...])
        dw_ref[...] += _mm_tn(y * sz, d_res)
        d_y = d_a * sz
        duz_ref[1] = d_a * y * _dsilu(z)
        dps_ref[...] += jnp.sum(d_y * yg, axis=0, keepdims=True)
        dyg_scr[...] = d_y * ps_ref[...]
        for g in range(POOL_GROUPS):
            cols = slice(g * POOL_GROUP_DIM, (g + 1) * POOL_GROUP_DIM)
            d_yg = dyg_scr[:, cols].astype(BF16)
            d_pool = _mm_nt(d_yg, wg_ref[g])
            dwg_ref[g] += _mm_tn(pooled_scr[:, cols], d_yg)
            duz_ref[0, :, cols] = _mm_exact_lhs(wint_ref[g], d_pool * inv_ref[g]) - d_pool

    tile = pl.BlockSpec((tm, D_MODEL), lambda i: (i, 0))
    vec = pl.BlockSpec((1, D_MODEL), lambda i: (0, 0))
    mat = pl.BlockSpec((D_MODEL, D_MODEL), lambda i: (0, 0))
    grp = pl.BlockSpec((POOL_GROUPS, POOL_GROUP_DIM, POOL_GROUP_DIM), lambda i: (0, 0, 0))
    return pl.pallas_call(
        body, name="pool_backward", grid=(SEQ // tm,),
        out_shape=[jax.ShapeDtypeStruct((POOL_SECTIONS, SEQ, D_MODEL), F32), jax.ShapeDtypeStruct((D_MODEL, D_MODEL), F32),
                   jax.ShapeDtypeStruct((POOL_GROUPS, POOL_GROUP_DIM, POOL_GROUP_DIM), F32), jax.ShapeDtypeStruct((1, D_MODEL), F32)],
        in_specs=[tile, pl.BlockSpec((None, tm, D_MODEL), lambda i: (0, i, 0)), pl.BlockSpec((None, tm, D_MODEL), lambda i: (1, i, 0)),
                  vec, grp, vec, mat, grp, grp, pl.BlockSpec((POOL_GROUPS, ROW_TILE, 1), lambda i: (0, 0, 0))],
        out_specs=[pl.BlockSpec((POOL_SECTIONS, tm, D_MODEL), lambda i: (0, i, 0)), mat, grp, vec],
        scratch_shapes=[pltpu.VMEM((tm, D_MODEL), F32)] * 3,
        compiler_params=_params(40),
    )(d_x2, uz, uz, gate, w_grp, pool_scale, w_out, win, win_t, inv)


def _ln_mod_backward(d_g, w, xin, nw, scale, d_up, name, scatter_names=(), scatter_parts=()):
    n_sec, rows, _ = d_g.shape
    n_mod = scale.shape[0]
    tm = ROW_TILE
    n_tiles = rows // tm
    skip = n_mod - 1
    n_sc = len(scatter_names)

    def body(dg_ref, w_ref, x_ref, nw_ref, sc_ref, up_ref, *rest):
        part_refs, (dx_ref, dnw_ref, dmod_ref), slot_refs = rest[:n_sc], rest[n_sc:n_sc + 3], rest[n_sc + 3:2 * n_sc + 3]
        i = pl.program_id(0)
        if n_sc:
            start_scatter, wait_scatter = _scatter_direct(scatter_names, part_refs, slot_refs, *rest[2 * n_sc + 3:])

            @pl.when(i == 0)
            def _():
                start_scatter()

        @pl.when(i == 0)
        def _():
            dnw_ref[...] = jnp.zeros_like(dnw_ref)

        @pl.when((i == 0) | (i == skip))
        def _():
            dmod_ref[...] = jnp.zeros_like(dmod_ref)

        d_h = _mm_nt(dg_ref[0], w_ref[:, 0:D_MODEL])
        for k in range(1, n_sec):
            d_h = d_h + _mm_nt(dg_ref[k], w_ref[:, k * D_MODEL:(k + 1) * D_MODEL])
        xv = x_ref[...]
        r = lax.rsqrt(jnp.mean(xv * xv, axis=-1, keepdims=True) + EPS)
        xh = xv * r
        nw_row = nw_ref[...]
        dmod_ref[0:1, :] += jnp.sum(d_h, axis=0, keepdims=True)
        dmod_ref[1:2, :] += jnp.sum(d_h * (xh * nw_row), axis=0, keepdims=True)
        d_xn = d_h * (1.0 + sc_ref[...])
        dnw_ref[...] += jnp.sum(d_xn * xh, axis=0, keepdims=True)
        d_xh = d_xn * nw_row

        @pl.when(i >= skip)
        def _():
            dx_ref[...] = up_ref[...] + r * (d_xh - xh * jnp.mean(d_xh * xh, axis=-1, keepdims=True))

        if n_sc:
            @pl.when(i == n_tiles - 1)
            def _():
                wait_scatter()

    lat = lambda i: (jnp.maximum(i - skip, 0), 0)
    mod_idx = lambda i: (jnp.minimum(i, n_mod - 1), 0, 0)
    outs = pl.pallas_call(
        body, name=name, grid=(n_tiles,),
        out_shape=[jax.ShapeDtypeStruct((rows - skip * tm, D_MODEL), F32), jax.ShapeDtypeStruct((1, D_MODEL), F32),
                   jax.ShapeDtypeStruct((n_mod, 8, D_MODEL), F32)] + _slot_shapes(scatter_names),
        in_specs=[pl.BlockSpec((n_sec, tm, D_MODEL), lambda i: (0, i, 0)),
                  pl.BlockSpec((D_MODEL, n_sec * D_MODEL), lambda i: (0, 0)),
                  pl.BlockSpec((tm, D_MODEL), lambda i: (i, 0)),
                  pl.BlockSpec((1, D_MODEL), lambda i: (0, 0)),
                  pl.BlockSpec((None, 1, D_MODEL), mod_idx),
                  pl.BlockSpec((tm, D_MODEL), lat)] + [ANY] * n_sc,
        out_specs=[pl.BlockSpec((tm, D_MODEL), lat), pl.BlockSpec((1, D_MODEL), lambda i: (0, 0)),
                   pl.BlockSpec((None, 8, D_MODEL), mod_idx)] + [ANY] * n_sc,
        scratch_shapes=_comm_sems(n_sc) if n_sc else [],
        compiler_params=_params(48),
    )(d_g, w, xin, nw, scale.reshape(n_mod, 1, D_MODEL), d_up, *scatter_parts)
    return outs[0], outs[1], outs[2], list(outs[3:])


def _weight_grad(h, d_g, name):
    n_sec, rows, _ = d_g.shape
    tm = 768 if rows % 768 == 0 else 512

    def body(h_ref, dg_ref, dw_ref):
        @pl.when(pl.program_id(1) == 0)
        def _():
            dw_ref[...] = jnp.zeros_like(dw_ref)

        dw_ref[...] += _mm_tn(h_ref[...], dg_ref[...])

    return pl.pallas_call(
        body, name=name, grid=(n_sec, rows // tm),
        out_shape=jax.ShapeDtypeStruct((D_MODEL, n_sec * D_MODEL), F32),
        in_specs=[pl.BlockSpec((tm, D_MODEL), lambda j, i: (i, 0)), pl.BlockSpec((None, tm, D_MODEL), lambda j, i: (j, i, 0))],
        out_specs=pl.BlockSpec((D_MODEL, D_MODEL), lambda j, i: (0, j)),
        compiler_params=_params(32),
    )(h, d_g)


def _sum_slots(slots, name):
    _, rows, cols = slots.shape
    tr = 128

    def body(s_ref, o_ref):
        acc = s_ref[0].astype(F32)
        for d in range(1, N_DEV):
            acc = acc + s_ref[d].astype(F32)
        o_ref[...] = acc

    return pl.pallas_call(
        body, name=name, grid=(rows // tr,),
        out_shape=jax.ShapeDtypeStruct((rows, cols), F32),
        in_specs=[pl.BlockSpec((N_DEV, tr, cols), lambda i: (0, i, 0))],
        out_specs=pl.BlockSpec((tr, cols), lambda i: (i, 0)),
    )(slots)


def _adamw_math(w, g, m, v):
    m = ADAM_B1 * m + (1.0 - ADAM_B1) * g
    v = ADAM_B2 * v + (1.0 - ADAM_B2) * (g * g)
    m_hat = m / (1.0 - ADAM_B1 ** ADAM_STEP)
    v_hat = v / (1.0 - ADAM_B2 ** ADAM_STEP)
    return -ADAM_LR * (m_hat / (jnp.sqrt(v_hat) + ADAM_EPS) + ADAM_WD * w), m, v


def _adamw(w, g, m, v, name):
    rows, cols = w.shape
    tr = rows if rows <= 128 else 128

    def body(w_ref, g_ref, m_ref, v_ref, d_ref, mo_ref, vo_ref):
        d_ref[...], mo_ref[...], vo_ref[...] = _adamw_math(w_ref[...], g_ref[...], m_ref[...], v_ref[...])

    tile = pl.BlockSpec((tr, cols), lambda i: (i, 0))
    return pl.pallas_call(
        body, name=name, grid=(rows // tr,),
        out_shape=[jax.ShapeDtypeStruct((rows, cols), F32)] * 3,
        in_specs=[tile] * 4, out_specs=[tile] * 3,
    )(w, g, m, v)


def _sum_devices(gathered):
    def body(p_ref, o_ref):
        acc = p_ref[0:SMALL_ROWS, :]
        for d in range(1, N_DEV):
            acc = acc + p_ref[d * SMALL_ROWS:(d + 1) * SMALL_ROWS, :]
        o_ref[...] = acc

    return pl.pallas_call(body, name="sum_devices", out_shape=jax.ShapeDtypeStruct((SMALL_ROWS, D_MODEL), F32),
                          in_specs=[VMEM], out_specs=VMEM)(gathered)


def _ada_update(cond_t, d_mod, ada_w, m, v):
    n_layers, _, n_cols = ada_w.shape
    tr = ROW_TILE

    def body(c_ref, dm_ref, w_ref, m_ref, v_ref, g_ref, d_ref, mo_ref, vo_ref):
        g = _mm_f32(_silu(c_ref[...]), dm_ref[...])
        g_ref[...] = g
        d_ref[...], mo_ref[...], vo_ref[...] = _adamw_math(w_ref[...], g, m_ref[...], v_ref[...])

    tile = pl.BlockSpec((None, tr, n_cols), lambda l, i: (l, i, 0))
    return pl.pallas_call(
        body, name="ada_update", grid=(n_layers, D_MODEL // tr),
        out_shape=[jax.ShapeDtypeStruct(ada_w.shape, F32)] * 4,
        in_specs=[pl.BlockSpec((tr, 16), lambda l, i: (i, 0)), pl.BlockSpec((None, 16, n_cols), lambda l, i: (l, 0, 0)),
                  tile, tile, tile],
        out_specs=[tile] * 4,
    )(cond_t, d_mod, ada_w, m, v)


def _cond_ctx_partial(d_modc, ada_w0):
    n_cols = ada_w0.shape[1]
    tr = ROW_TILE

    def body(dm_ref, w_ref, o_ref):
        o_ref[...] = jnp.sum(w_ref[...] * dm_ref[...], axis=-1, keepdims=True)

    return pl.pallas_call(
        body, name="cond_ctx_partial", grid=(D_MODEL // tr,),
        out_shape=jax.ShapeDtypeStruct((D_MODEL, 1), F32),
        in_specs=[pl.BlockSpec((1, n_cols), lambda i: (0, 0)), pl.BlockSpec((tr, n_cols), lambda i: (i, 0))],
        out_specs=pl.BlockSpec((tr, 1), lambda i: (i, 0)),
    )(d_modc, ada_w0)


def _cond_ctx_update(gathered, c_ctx, m, v):
    def body(p_ref, w_ref, m_ref, v_ref, g_ref, d_ref, mo_ref, vo_ref):
        acc = p_ref[0:1, :]
        for s in range(1, N_CHIPS):
            acc = acc + p_ref[16 * s:16 * s + 1, :]
        w = w_ref[...]
        g = acc * _dsilu(w)
        g_ref[...] = g
        d_ref[...], mo_ref[...], vo_ref[...] = _adamw_math(w, g, m_ref[...], v_ref[...])

    return pl.pallas_call(body, name="cond_ctx_update", out_shape=[jax.ShapeDtypeStruct((1, D_MODEL), F32)] * 4,
                          in_specs=[VMEM] * 4, out_specs=[VMEM] * 4)(gathered, c_ctx, m, v)


def _local_step(x2, ctx2, target, mod_mine, mod_ctx, lb_logits, scale_full, w_in_full, late_shards, norm_w, gnorm, final_w):
    row = lambda a: a.reshape(1, -1)
    shift0, scale0, gate0 = (row(a) for a in jnp.split(mod_mine[0], 3))
    shift1, scale1, gate1 = (row(a) for a in jnp.split(mod_mine[1], 3))
    shift_c, scale_c, _ = (row(a) for a in jnp.split(mod_ctx, 3))
    xa = jnp.concatenate([ctx2, x2], axis=0)
    nw0, nw1 = norm_w[0:1], norm_w[1:2]
    scales0 = jnp.concatenate([scale_c, scale0])

    g5, h0 = _ln_mod_matmul(xa, nw0, jnp.concatenate([shift_c, shift0]), scales0, w_in_full, "hgrn_in_forward")
    o_raw, full = _gla_forward(g5, lb_logits, late_shards)
    x1, res0 = _hgrn_out_forward(o_raw, g5, x2, gnorm, gate0, full["hgrn_w_out"])
    uz, h1 = _ln_mod_matmul(x1, nw1, shift1, scale1, full["pool_w_in"], "pool_in_forward")
    d_x2, loss_part, d_final, d_gate1 = _pool_forward_loss(uz, x1, target, gate1, full["pool_w_grp"], scale_full,
                                                           full["pool_w_out"], final_w)

    d_uz, dw_pool_out, dw_pool_grp, d_pscale = _pool_backward(d_x2, uz, gate1, full["pool_w_grp"], scale_full, full["pool_w_out"])
    d_x1, d_nw1, d_mod1, _ = _ln_mod_backward(d_uz, full["pool_w_in"], x1, nw1, scale1, d_x2, "pool_in_backward")
    dw_pool_in = _weight_grad(h1, d_uz, "pool_in_weight_grad")
    d_o, d_z, dw_hgrn_out, d_gate0, d_gnorm = _hgrn_out_backward(d_x1, o_raw, g5, res0, gnorm, gate0, full["hgrn_w_out"])
    late_grads = {"hgrn_w_out": dw_hgrn_out, "pool_w_in": dw_pool_in, "pool_w_grp": dw_pool_grp, "pool_w_out": dw_pool_out}
    d_g5, d_lb, late_slots = _gla_backward(g5, lb_logits, d_o, d_z, [late_grads[k].astype(BF16) for k in GATHER_LATE])
    dw_hgrn_in = _weight_grad(h0, d_g5, "hgrn_in_weight_grad")
    d_x, d_nw0, d_mod0, early_slots = _ln_mod_backward(d_g5, w_in_full, xa, nw0, scales0, d_x1, "hgrn_in_backward",
                                                       GATHER_EARLY, [dw_hgrn_in.astype(BF16)])
    slots = dict(late_slots)
    slots.update(zip(GATHER_EARLY, early_slots))

    zero = jnp.zeros((1, D_MODEL), F32)
    small = jnp.concatenate([d_mod0[1, 0:2], d_gate0, d_mod1[0, 0:2], d_gate1, d_mod0[0, 0:2], zero, d_nw0, d_nw1, d_gnorm,
                             d_final, d_pscale, d_lb, jnp.broadcast_to(loss_part[0:1, 0:1], (1, D_MODEL)),
                             jnp.zeros((SMALL_ROWS - 17, D_MODEL), F32)], axis=0)
    return {"d_x": d_x, "slots": slots, "small": small}


def kernel(x, c, ctx, c_ctx, ada_w, ada_b, norm_w, hgrn_w_in, hgrn_lb_logits, hgrn_gnorm_w, hgrn_w_out, pool_w_in, pool_w_grp, pool_scale, pool_w_out, final_norm_w, loss_target, m_c_ctx, m_ada_w, m_ada_b, m_norm_w, m_hgrn_w_in, m_hgrn_lb_logits, m_hgrn_gnorm_w, m_hgrn_w_out, m_pool_w_in, m_pool_w_grp, m_pool_scale, m_pool_w_out, m_final_norm_w, v_c_ctx, v_ada_w, v_ada_b, v_norm_w, v_hgrn_w_in, v_hgrn_lb_logits, v_hgrn_gnorm_w, v_hgrn_w_out, v_pool_w_in, v_pool_w_grp, v_pool_scale, v_pool_w_out, v_final_norm_w):
    xi, yi, ci = _my_place()
    chip = 2 * xi + yi
    dev = 4 * xi + 2 * yi + ci
    ada_cols = ada_w.shape[2]
    lb_cols = hgrn_lb_logits.shape[2]
    ps_cols = pool_scale.shape[1]
    row = lambda a: a.reshape(1, -1)

    def chip_cols(a, n):
        return lax.dynamic_slice_in_dim(a, chip * n, n, axis=a.ndim - 1)

    def from_chips(g, rows_per_dev, take):
        return jnp.concatenate([g[2 * s * rows_per_dev:2 * s * rows_per_dev + take] for s in range(N_CHIPS)], axis=1)

    first = jnp.concatenate([jnp.broadcast_to(c, (8, D_MODEL)), jnp.pad(hgrn_lb_logits[0], ((0, 6), (0, 0))),
                             jnp.pad(pool_scale, ((0, 7), (0, 0)))], axis=1)
    first_all = _all_gather_small(first, "gather_cond")
    cond_all = first_all[::8, :D_MODEL]
    lb_logits = from_chips(first_all[:, D_MODEL:D_MODEL + lb_cols], 8, 2)
    scale_full = from_chips(first_all[:, D_MODEL + lb_cols:], 8, 1)
    cond_rows = jnp.concatenate([cond_all, row(c_ctx), jnp.zeros((7, D_MODEL), F32)], axis=0)

    parts = _mod_parts(cond_rows, ada_w, chip_cols(ada_b, ada_cols))
    parts_all = _all_gather_small(parts.reshape(32, ada_cols), "gather_mod")
    mod_all = from_chips(parts_all, 32, 32).reshape(2, 16, 3 * D_MODEL)
    mod_mine = lax.dynamic_index_in_dim(mod_all, dev, axis=1, keepdims=False)

    shards = {"hgrn_w_in": hgrn_w_in[0], "hgrn_w_out": hgrn_w_out[0], "pool_w_in": pool_w_in[0],
              "pool_w_grp": pool_w_grp[0], "pool_w_out": pool_w_out[0]}
    w_in_full = _all_gather_weights([shards[k].astype(BF16) for k in GATHER_EARLY], GATHER_EARLY)[0]

    loc = _local_step(x[0], ctx[0], loss_target[0], mod_mine, mod_all[0, 8], lb_logits, scale_full, w_in_full,
                      [shards[k].astype(BF16) for k in GATHER_LATE], norm_w, hgrn_gnorm_w, row(final_norm_w))

    small_all = _all_gather_small(loc["small"], "gather_small")
    sums = _sum_devices(small_all)
    loss = sums[16, 0]

    slots = loc["slots"]
    halves = []
    for k in W_NAMES:
        hs = _half_shape(k)
        flat = _sum_slots(slots[k].reshape(N_DEV, -1, hs[-1]), "sum_" + k)
        halves.append(flat.reshape(hs))
    big_grads = dict(zip(W_NAMES, _exchange_halves(halves)))

    out = {}

    def update(name, w, g, m, v):
        shape = w.shape
        w2, g2, m2, v2 = (a.reshape(-1, shape[-1]) for a in (w, g, m, v))
        d, mn, vn = _adamw(w2, g2, m2, v2, "adamw_" + name)
        out[name] = tuple(a.reshape(shape) for a in (g2, d, mn, vn))

    moments = {"hgrn_w_in": (m_hgrn_w_in, v_hgrn_w_in), "hgrn_w_out": (m_hgrn_w_out, v_hgrn_w_out),
               "pool_w_in": (m_pool_w_in, v_pool_w_in), "pool_w_grp": (m_pool_w_grp, v_pool_w_grp),
               "pool_w_out": (m_pool_w_out, v_pool_w_out)}
    weights = {"hgrn_w_in": hgrn_w_in, "hgrn_w_out": hgrn_w_out, "pool_w_in": pool_w_in, "pool_w_grp": pool_w_grp,
               "pool_w_out": pool_w_out}
    for k in W_NAMES:
        update(k, weights[k], big_grads[k].reshape(weights[k].shape), *moments[k])

    g_ada_b = jnp.stack([(sums[0:3] + sums[6:9]).reshape(-1), sums[3:6].reshape(-1)])
    update("ada_b", ada_b, g_ada_b, m_ada_b, v_ada_b)
    update("norm_w", norm_w, sums[9:11], m_norm_w, v_norm_w)
    update("hgrn_gnorm_w", hgrn_gnorm_w, sums[11:12], m_hgrn_gnorm_w, v_hgrn_gnorm_w)
    update("final_norm_w", row(final_norm_w), sums[12:13], row(m_final_norm_w), row(v_final_norm_w))
    update("pool_scale", pool_scale, chip_cols(sums[13:14], ps_cols), m_pool_scale, v_pool_scale)
    update("hgrn_lb_logits", hgrn_lb_logits, chip_cols(sums[14:16], lb_cols)[None], m_hgrn_lb_logits, v_hgrn_lb_logits)

    per_dev = small_all.reshape(N_DEV, SMALL_ROWS, D_MODEL)
    pad7 = jnp.zeros((7, 3 * D_MODEL), F32)
    dm0 = jnp.concatenate([per_dev[:, 0:3].reshape(N_DEV, -1), sums[6:9].reshape(1, -1), pad7], axis=0)
    dm1 = jnp.concatenate([per_dev[:, 3:6].reshape(N_DEV, -1), jnp.zeros((8, 3 * D_MODEL), F32)], axis=0)
    d_mod = chip_cols(jnp.stack([dm0, dm1]), ada_cols)
    out["ada_w"] = _ada_update(cond_rows.T, d_mod, ada_w, m_ada_w, v_ada_w)

    part_c = _cond_ctx_partial(chip_cols(sums[6:9].reshape(1, -1), ada_cols), ada_w[0])
    part_all = _all_gather_small(jnp.pad(part_c.reshape(1, D_MODEL), ((0, 7), (0, 0))), "gather_cond_ctx")
    g_c, d_c, m_c, v_c = _cond_ctx_update(part_all, row(c_ctx), row(m_c_ctx), row(v_c_ctx))
    out["c_ctx"] = tuple(a.reshape(-1) for a in (g_c, d_c, m_c, v_c))
    out["final_norm_w"] = tuple(a.reshape(-1) for a in out["final_norm_w"])

    names = ["c_ctx", "ada_w", "ada_b", "norm_w", "hgrn_w_in", "hgrn_lb_logits", "hgrn_gnorm_w", "hgrn_w_out", "pool_w_in",
             "pool_w_grp", "pool_scale", "pool_w_out", "final_norm_w"]
    return (loss, loc["d_x"][None], *[out[k][0] for k in names], *[out[k][1] for k in names], *[out[k][2] for k in names],
            *[out[k][3] for k in names])
```

```python
import functools

import numpy as np
import jax
import jax.numpy as jnp
from jax import lax
from jax.experimental import pallas as pl
from jax.experimental.pallas import tpu as pltpu

F32 = jnp.float32
BF16 = jnp.bfloat16

D_MODEL = 1024
SEQ = 2048
CTX_LEN = 256
ROWS_ALL = CTX_LEN + SEQ
HEADS = 8
HEAD_DIM = 128
CHUNK = 64
N_CTX_CHUNKS = CTX_LEN // CHUNK
N_LAT_CHUNKS = SEQ // CHUNK
N_CHUNKS = N_CTX_CHUNKS + N_LAT_CHUNKS
GRID_W = 64
POOL_WINDOWS = (2, 4, 8, 16)
POOL_GROUPS = 4
POOL_GROUP_DIM = 256
HGRN_SECTIONS = 5
POOL_SECTIONS = 2
EPS = 1e-6
N_DEV = 8
N_CHIPS = 4
ROW_TILE = 256
SMALL_ROWS = 24

ADAM_LR = 0.001
ADAM_B1 = 0.9
ADAM_B2 = 0.999
ADAM_EPS = 1e-08
ADAM_WD = 0.01
ADAM_STEP = 10

MESH = pl.DeviceIdType.MESH
MIB = 1 << 20
ANY = pl.BlockSpec(memory_space=pl.ANY)
VMEM = pl.BlockSpec(memory_space=pltpu.VMEM)


def _params(vmem_mib=None):
    if vmem_mib is None:
        return pltpu.CompilerParams()
    return pltpu.CompilerParams(vmem_limit_bytes=vmem_mib * MIB)


def _sig(a):
    return 1.0 / (1.0 + jnp.exp(-a))


def _silu(a):
    return a * _sig(a)


def _dsilu(a):
    s = _sig(a)
    return s * (1.0 + a * (1.0 - s))


def _mm(a, b):
    return jnp.dot(a.astype(BF16), b.astype(BF16), preferred_element_type=F32)


def _mm_nt(a, b):
    return lax.dot_general(a.astype(BF16), b.astype(BF16), (((1,), (1,)), ((), ())), preferred_element_type=F32)


def _mm_tn(a, b):
    return lax.dot_general(a.astype(BF16), b.astype(BF16), (((0,), (0,)), ((), ())), preferred_element_type=F32)


def _split2(a):
    hi = a.astype(BF16)
    lo = (a - hi.astype(F32)).astype(BF16)
    return hi, lo


def _mm_exact_lhs(m_bf, a):
    hi, lo = _split2(a)
    return jnp.dot(m_bf, hi, preferred_element_type=F32) + jnp.dot(m_bf, lo, preferred_element_type=F32)


def _mm_f32(a, b):
    ah, al = _split2(a)
    bh, bl = _split2(b)
    return (jnp.dot(ah, bh, preferred_element_type=F32) + jnp.dot(al, bh, preferred_element_type=F32)
            + jnp.dot(ah, bl, preferred_element_type=F32))


def _my_place():
    return lax.axis_index("x"), lax.axis_index("y"), lax.axis_index("c")


def _all_gather_small(blk, name):
    m_per, n = blk.shape

    def body(x_ref, out_ref, send_sems, recv_sems, local_sem):
        x, y, c = _my_place()
        me, sibling = (x, y, c), (x, y, 1 - c)
        chips = [(1 - x, y), (x, 1 - y), (1 - x, 1 - y)]

        def rows(px, py, pc):
            return out_ref.at[pl.ds((4 * px + 2 * py + pc) * m_per, m_per), :]

        def copy(k, block, to, src=None):
            return pltpu.make_async_remote_copy(
                src_ref=rows(*block) if src is None else src, dst_ref=rows(*block),
                send_sem=send_sems.at[k], recv_sem=recv_sems.at[k], device_id=to, device_id_type=MESH)

        mine = pltpu.make_async_copy(x_ref, rows(*me), local_sem)
        mine.start()
        first = [copy(0, me, sibling, src=x_ref)]
        first += [copy(1 + j, me, (*chip, c), src=x_ref) for j, chip in enumerate(chips)]
        for cp in first:
            cp.start()
        passed = [copy(4 + j, (*chip, c), sibling) for j, chip in enumerate(chips)]
        for j, chip in enumerate(chips):
            copy(1 + j, (*chip, c), me).wait_recv()
            passed[j].start()
        copy(0, sibling, me).wait_recv()
        for j, chip in enumerate(chips):
            copy(4 + j, (*chip, 1 - c), me).wait_recv()
        for cp in first + passed:
            cp.wait_send()
        mine.wait()

    return pl.pallas_call(
        body, name=name,
        out_shape=jax.ShapeDtypeStruct((N_DEV * m_per, n), blk.dtype),
        in_specs=[VMEM], out_specs=VMEM,
        scratch_shapes=[pltpu.SemaphoreType.DMA((7,)), pltpu.SemaphoreType.DMA((7,)), pltpu.SemaphoreType.DMA],
    )(blk)


W_SPECS = {
    "hgrn_w_in": ((D_MODEL, 5 * D_MODEL), (1, 1280, 0, 512)),
    "hgrn_w_out": ((D_MODEL, D_MODEL), (0, 256, 0, 128)),
    "pool_w_in": ((D_MODEL, 2 * D_MODEL), (1, 512, 0, 512)),
    "pool_w_grp": ((POOL_GROUPS, POOL_GROUP_DIM, POOL_GROUP_DIM), (1, 64, 1, 32)),
    "pool_w_out": ((D_MODEL, D_MODEL), (0, 256, 0, 128)),
}
W_NAMES = tuple(W_SPECS)


def _al(v, m):
    return pl.multiple_of(v, m)


def _region(ref, spec, chip, half):
    ca, cn, ha, hn = spec
    idx = [slice(None)] * len(ref.shape)
    if ca == ha:
        if half is None:
            idx[ca] = pl.ds(_al(chip * cn, cn), cn)
        else:
            idx[ca] = pl.ds(_al(chip * cn + half * hn, hn), hn)
    else:
        idx[ca] = pl.ds(_al(chip * cn, cn), cn)
        if half is not None:
            idx[ha] = pl.ds(_al(half * hn, hn), hn)
    return ref.at[tuple(idx)]


def _half_of(ref, spec, half):
    _, _, ha, hn = spec
    idx = [slice(None)] * len(ref.shape)
    idx[ha] = pl.ds(_al(half * hn, hn), hn)
    return ref.at[tuple(idx)]


PIECE_BYTES = 256 * 1024


def _pieces(ref):
    lead = ref.shape[0]
    want = (int(np.prod(ref.shape)) * ref.dtype.itemsize) // PIECE_BYTES
    n = max([1] + [k for k in range(1, want + 1) if lead % k == 0 and (lead // k) % 16 == 0])
    rows = lead // n
    return [ref.at[pl.ds(i * rows, rows)] for i in range(n)]


def _half_shape(name):
    full, (ca, cn, ha, hn) = W_SPECS[name]
    shp = list(full)
    shp[ca] = cn
    shp[ha] = hn
    return tuple(shp)


def _gather_direct(names, sh, full, send_sems, recv_sems, local_sems):
    specs = [W_SPECS[k][1] for k in names]
    x, y, c = _my_place()
    chip_me = 2 * x + y
    chips = [(1 - x, y), (x, 1 - y), (1 - x, 1 - y)]

    def local(a):
        return pltpu.make_async_copy(sh[a], _region(full[a], specs[a], chip_me, None), local_sems.at[a])

    def remote(a, src, dst, to):
        return pltpu.make_async_remote_copy(src_ref=src, dst_ref=dst, send_sem=send_sems.at[a], recv_sem=recv_sems.at[a],
                                            device_id=to, device_id_type=MESH)

    def start():
        for a in range(len(names)):
            local(a).start()
            for px, py in chips:
                remote(a, sh[a], _region(full[a], specs[a], chip_me, None), (px, py, c)).start()

    def wait():
        for a in range(len(names)):
            ca, cn, _, _ = specs[a]
            idx = [slice(None)] * len(full[a].shape)
            idx[ca] = pl.ds(0, 3 * cn)
            three = full[a].at[tuple(idx)]
            remote(a, three, three, (x, y, c)).wait()
            local(a).wait()

    return start, wait


def _scatter_direct(names, part, slots, send_sems, recv_sems, local_sems):
    specs = [W_SPECS[k][1] for k in names]
    x, y, c = _my_place()
    dev_me = 4 * x + 2 * y + c

    def local(a):
        return pltpu.make_async_copy(_region(part[a], specs[a], 2 * x + y, c), slots[a].at[dev_me], local_sems.at[a])

    def start():
        for a in range(len(names)):
            local(a).start()
            for flip in range(1, N_DEV):
                tx = 1 - x if flip >> 2 else x
                ty = 1 - y if (flip >> 1) & 1 else y
                tc = 1 - c if flip & 1 else c
                pltpu.make_async_remote_copy(src_ref=_region(part[a], specs[a], 2 * tx + ty, tc), dst_ref=slots[a].at[dev_me],
                                             send_sem=send_sems.at[a], recv_sem=recv_sems.at[a], device_id=(tx, ty, tc),
                                             device_id_type=MESH).start()

    def wait():
        for a in range(len(names)):
            seven = slots[a].at[pl.ds(0, N_DEV - 1)]
            pltpu.make_async_remote_copy(src_ref=seven, dst_ref=seven, send_sem=send_sems.at[a], recv_sem=recv_sems.at[a],
                                         device_id=(x, y, c), device_id_type=MESH).wait()
            local(a).wait()

    return start, wait


HBM_SPEC = pl.BlockSpec(memory_space=pltpu.HBM)
SEM_SPEC = pl.BlockSpec(memory_space=pltpu.SEMAPHORE)
SPLIT_EFFECT = pltpu.SideEffectType.DATAFLOW_SIDE_EFFECTING


def _scatter_start(part, slots, name_key):
    spec = W_SPECS[name_key][1]

    def body(part_ref, slots_ref, send_sem, recv_sem, part_thru, slots_thru, token):
        x, y, c = _my_place()
        dev_me = 4 * x + 2 * y + c
        for flip in range(1, N_DEV):
            tx = 1 - x if flip >> 2 else x
            ty = 1 - y if (flip >> 1) & 1 else y
            tc = 1 - c if flip & 1 else c
            pltpu.make_async_remote_copy(src_ref=_region(part_ref, spec, 2 * tx + ty, tc), dst_ref=slots_ref.at[dev_me],
                                         send_sem=send_sem, recv_sem=recv_sem, device_id=(tx, ty, tc), device_id_type=MESH).start()
        token[...] = jnp.zeros_like(token)

    return pl.pallas_call(
        body, name="scatter_start_" + name_key,
        out_shape=(pltpu.SemaphoreType.DMA(()), pltpu.SemaphoreType.DMA(()), pltpu.HBM(part.shape, part.dtype),
                   pltpu.HBM(slots.shape, slots.dtype), jax.ShapeDtypeStruct((8, 128), F32)),
        in_specs=(HBM_SPEC, HBM_SPEC), out_specs=(SEM_SPEC, SEM_SPEC, HBM_SPEC, HBM_SPEC, VMEM),
        input_output_aliases={0: 2, 1: 3},
        compiler_params=pltpu.CompilerParams(has_side_effects=SPLIT_EFFECT),
    )(pltpu.with_memory_space_constraint(part, pltpu.HBM), pltpu.with_memory_space_constraint(slots, pltpu.HBM))


def _scatter_wait(send_sem, recv_sem, part_thru, slots_thru, after, name_key):
    def body(part_ref, slots_ref, send_sem, recv_sem, after_ref, part_dead, slots_out):
        x, y, c = _my_place()
        seven = slots_ref.at[pl.ds(0, N_DEV - 1)]
        copy = pltpu.make_async_remote_copy(src_ref=seven, dst_ref=seven, send_sem=send_sem, recv_sem=recv_sem,
                                            device_id=(x, y, c), device_id_type=MESH)
        copy.wait_send()
        copy.wait_recv()

    return pl.pallas_call(
        body, name="scatter_wait_" + name_key,
        out_shape=(pltpu.HBM(part_thru.shape, part_thru.dtype), pltpu.HBM(slots_thru.shape, slots_thru.dtype)),
        in_specs=(HBM_SPEC, HBM_SPEC, SEM_SPEC, SEM_SPEC, pl.BlockSpec(memory_space=pl.ANY)), out_specs=(HBM_SPEC, HBM_SPEC),
        input_output_aliases={0: 0, 1: 1},
        compiler_params=pltpu.CompilerParams(has_side_effects=SPLIT_EFFECT),
    )(part_thru, slots_thru, send_sem, recv_sem, after)[1]


def _comm_sems(n):
    return [pltpu.SemaphoreType.DMA((n,)), pltpu.SemaphoreType.DMA((n,)), pltpu.SemaphoreType.DMA((n,))]


GATHER_EARLY = ("hgrn_w_in",)
GATHER_LATE = ("hgrn_w_out", "pool_w_in", "pool_w_grp", "pool_w_out")


def _all_gather_weights(shards, names):
    n = len(names)
    specs = [W_SPECS[k][1] for k in names]

    def body(*refs):
        sh, full = refs[:n], refs[n:2 * n]
        send_sems, recv_sems, local_sems = refs[2 * n:]
        x, y, c = _my_place()
        chip_me = 2 * x + y
        sibling = (x, y, 1 - c)
        chips = [(1 - x, y), (x, 1 - y), (1 - x, 1 - y)]

        def remote(a, k, src, dst, to):
            return pltpu.make_async_remote_copy(src_ref=src, dst_ref=dst, send_sem=send_sems.at[6 * a + k],
                                                recv_sem=recv_sems.at[6 * a + k], device_id=to, device_id_type=MESH)

        local = [pltpu.make_async_copy(sh[a], _region(full[a], specs[a], chip_me, None), local_sems.at[a]) for a in range(n)]
        for cp in local:
            cp.start()
        sends = []
        for a in range(n):
            for j, (px, py) in enumerate(chips):
                src, dst = _half_of(sh[a], specs[a], c), _region(full[a], specs[a], chip_me, c)
                for s_piece, d_piece in zip(_pieces(src), _pieces(dst)):
                    remote(a, j, s_piece, d_piece, (px, py, c)).start()
                sends.append(remote(a, j, src, dst, (px, py, c)))
        for a in range(n):
            for j, (px, py) in enumerate(chips):
                landed = _region(full[a], specs[a], 2 * px + py, c)
                remote(a, j, landed, landed, (px, py, c)).wait_recv()
                for piece in _pieces(landed):
                    remote(a, 3 + j, piece, piece, sibling).start()
                sends.append(remote(a, 3 + j, landed, landed, sibling))
        for a in range(n):
            for j, (px, py) in enumerate(chips):
                other = _region(full[a], specs[a], 2 * px + py, 1 - c)
                remote(a, 3 + j, other, other, sibling).wait_recv()
        for cp in sends:
            cp.wait_send()
        for cp in local:
            cp.wait()

    return pl.pallas_call(
        body, name="all_gather_weights",
        out_shape=[jax.ShapeDtypeStruct(W_SPECS[k][0], BF16) for k in names],
        in_specs=[VMEM] * n, out_specs=[VMEM] * n,
        scratch_shapes=[pltpu.SemaphoreType.DMA((6 * n,)), pltpu.SemaphoreType.DMA((6 * n,)), pltpu.SemaphoreType.DMA((n,))],
        compiler_params=_params(32),
    )(*shards)


def _slot_shapes(names):
    return [jax.ShapeDtypeStruct((N_DEV,) + _half_shape(k), BF16) for k in names]


def _scatter_grads(parts, names):
    n = len(names)

    def body(*refs):
        start, wait = _scatter_direct(names, refs[:n], refs[n:2 * n], *refs[2 * n:])
        start()
        wait()

    return pl.pallas_call(body, name="scatter_grads", out_shape=_slot_shapes(names), in_specs=[ANY] * n, out_specs=[ANY] * n,
                          scratch_shapes=_comm_sems(n))(*parts)


def _exchange_halves(halves, names, name):
    n = len(names)
    specs = [W_SPECS[k][1] for k in names]

    def shard_shape(k):
        shp = list(_half_shape(k))
        shp[W_SPECS[k][1][2]] *= 2
        return tuple(shp)

    def body(*refs):
        half, out = refs[:n], refs[n:2 * n]
        send_sems, recv_sems, local_sems = refs[2 * n:]
        x, y, c = _my_place()
        sibling = (x, y, 1 - c)

        def remote(a, src, dst):
            return pltpu.make_async_remote_copy(src_ref=src, dst_ref=dst, send_sem=send_sems.at[a], recv_sem=recv_sems.at[a],
                                                device_id=sibling, device_id_type=MESH)

        local = [pltpu.make_async_copy(half[a], _half_of(out[a], specs[a], c), local_sems.at[a]) for a in range(n)]
        for cp in local:
            cp.start()
        for a in range(n):
            mine = _half_of(out[a], specs[a], c)
            for src, dst in zip(_pieces(half[a]), _pieces(mine)):
                remote(a, src, dst).start()
        for a in range(n):
            theirs = _half_of(out[a], specs[a], 1 - c)
            remote(a, theirs, theirs).wait_recv()
        for a in range(n):
            remote(a, half[a], half[a]).wait_send()
        for cp in local:
            cp.wait()

    return pl.pallas_call(
        body, name=name,
        out_shape=[jax.ShapeDtypeStruct(shard_shape(k), F32) for k in names],
        in_specs=[VMEM] * n, out_specs=[VMEM] * n,
        scratch_shapes=[pltpu.SemaphoreType.DMA((n,)), pltpu.SemaphoreType.DMA((n,)), pltpu.SemaphoreType.DMA((n,))],
        compiler_params=_params(32),
    )(*halves)


def _mod_parts(c_rows, ada_w, ada_b_cols):
    n_layers, _, n_cols = ada_w.shape

    def body(c_ref, w_ref, b_ref, o_ref):
        o_ref[...] = _mm_f32(_silu(c_ref[...]), w_ref[...]) + b_ref[...]

    return pl.pallas_call(
        body, name="mod_parts", grid=(n_layers,),
        out_shape=jax.ShapeDtypeStruct((n_layers, 16, n_cols), F32),
        in_specs=[pl.BlockSpec((16, D_MODEL), lambda i: (0, 0)),
                  pl.BlockSpec((None, D_MODEL, n_cols), lambda i: (i, 0, 0)),
                  pl.BlockSpec((None, 1, n_cols), lambda i: (i, 0, 0))],
        out_specs=pl.BlockSpec((None, 16, n_cols), lambda i: (i, 0, 0)),
        compiler_params=_params(40),
    )(c_rows, ada_w, ada_b_cols.reshape(n_layers, 1, n_cols))


def _ln_mod_matmul(xin, nw, shift, scale, w, name):
    rows = xin.shape[0]
    n_sec = w.shape[1] // D_MODEL
    n_mod = shift.shape[0]
    tm = ROW_TILE

    def body(x_ref, nw_ref, sh_ref, sc_ref, w_ref, g_ref, h_ref):
        xv = x_ref[...]
        r = lax.rsqrt(jnp.mean(xv * xv, axis=-1, keepdims=True) + EPS)
        h = ((xv * r * nw_ref[...]) * (1.0 + sc_ref[...]) + sh_ref[...]).astype(BF16)
        h_ref[...] = h
        for k in range(n_sec):
            g_ref[k] = jnp.dot(h, w_ref[:, k * D_MODEL:(k + 1) * D_MODEL], preferred_element_type=F32)

    mod_spec = pl.BlockSpec((None, 1, D_MODEL), lambda i: (jnp.minimum(i, n_mod - 1), 0, 0))
    return pl.pallas_call(
        body, name=name, grid=(rows // tm,),
        out_shape=[jax.ShapeDtypeStruct((n_sec, rows, D_MODEL), F32), jax.ShapeDtypeStruct((rows, D_MODEL), BF16)],
        in_specs=[pl.BlockSpec((tm, D_MODEL), lambda i: (i, 0)),
                  pl.BlockSpec((1, D_MODEL), lambda i: (0, 0)),
                  mod_spec, mod_spec,
                  pl.BlockSpec((D_MODEL, n_sec * D_MODEL), lambda i: (0, 0))],
        out_specs=[pl.BlockSpec((n_sec, tm, D_MODEL), lambda i: (0, i, 0)),
                   pl.BlockSpec((tm, D_MODEL), lambda i: (i, 0))],
        compiler_params=_params(48),
    )(xin, nw, shift.reshape(n_mod, 1, D_MODEL), scale.reshape(n_mod, 1, D_MODEL), w)


def _chunk_masks(rev):
    rid = lax.broadcasted_iota(jnp.int32, (CHUNK, CHUNK), 0)
    cid = lax.broadcasted_iota(jnp.int32, (CHUNK, CHUNK), 1)
    keep = (cid >= rid) if rev else (cid <= rid)
    keep_t = (cid <= rid) if rev else (cid >= rid)
    one, zero = jnp.ones((CHUNK, CHUNK), F32), jnp.zeros((CHUNK, CHUNK), F32)
    return keep, jnp.where(keep, one, zero).astype(BF16), jnp.where(keep_t, one, zero).astype(BF16)


def _chunk_rows(t, rev, latent):
    n = N_LAT_CHUNKS if latent else N_CTX_CHUNKS
    base = CTX_LEN if latent else 0
    idx = (n - 1 - t) if rev else t
    return pl.multiple_of(base + idx * CHUNK, CHUNK)


def _gates(fpre, lb):
    sg = _sig(fpre)
    f = lb + (1.0 - lb) * sg
    return sg, f, 1.0 - f, jnp.log(f)


G_SPEC = lambda sec: pl.BlockSpec((None, ROWS_ALL, HEAD_DIM), lambda h, sec=sec: (sec, 0, h))


def _gla_forward(g5, lb_logits):
    q_scale = HEAD_DIM ** -0.5

    def body(ff_ref, fb_ref, v_ref, q_ref, lg_ref, o_ref, st_ref, decay_ref, qt_ref):
        for rev in (False, True):
            f_ref = fb_ref if rev else ff_ref
            lb = _sig(lg_ref[1:2, :] if rev else lg_ref[0:1, :])
            keep, tri, _ = _chunk_masks(rev)
            last = 0 if rev else CHUNK - 1
            mid = CHUNK // 2 if rev else CHUNK // 2 - 1

            def local_step(t, carry, latent):
                r0 = _chunk_rows(t, rev, latent)
                rows = pl.ds(r0, CHUNK)
                step = t + (N_CTX_CHUNKS if latent else 0)
                _, _, k, lf = _gates(f_ref[rows, :], lb)
                v = v_ref[rows, :]
                b = _mm_exact_lhs(tri, lf)
                bl = b[last:last + 1, :]
                if latent:
                    q = _silu(q_ref[rows, :]) * q_scale
                    bm = b[mid:mid + 1, :]
                    a = _mm_nt(q * jnp.exp(b - bm), k * jnp.exp(bm - b))
                    o = _mm(jnp.where(keep, a, 0.0), v)
                    orow = pl.ds(pl.multiple_of(r0 - CTX_LEN, CHUNK), CHUNK)
                    qt_ref[orow, :] = (q * jnp.exp(b)).astype(BF16)
                    if rev:
                        o_ref[orow, :] += o
                    else:
                        o_ref[orow, :] = o
                decay_ref[step] = jnp.exp(bl)
                st_ref[step] = _mm_tn(v, k * jnp.exp(bl - b))
                return carry

            lax.fori_loop(0, N_CTX_CHUNKS, functools.partial(local_step, latent=False), 0, unroll=2)
            lax.fori_loop(0, N_LAT_CHUNKS, functools.partial(local_step, latent=True), 0, unroll=4)

            def scan_step(t, st):
                update = st_ref[t]
                st_ref[t] = st
                return st * decay_ref[t] + update

            lax.fori_loop(0, N_CHUNKS, scan_step, jnp.zeros((HEAD_DIM, HEAD_DIM), F32), unroll=2)

            def inter_step(t, carry):
                r0 = _chunk_rows(t, rev, True)
                orow = pl.ds(pl.multiple_of(r0 - CTX_LEN, CHUNK), CHUNK)
                o_ref[orow, :] += lax.dot_general(qt_ref[orow, :], st_ref[t + N_CTX_CHUNKS].astype(BF16),
                                                  (((1,), (1,)), ((), ())), preferred_element_type=F32)
                return carry

            lax.fori_loop(0, N_LAT_CHUNKS, inter_step, 0, unroll=4)

    return pl.pallas_call(
        body, name="gla_forward", grid=(HEADS,),
        out_shape=jax.ShapeDtypeStruct((SEQ, D_MODEL), F32),
        in_specs=[G_SPEC(0), G_SPEC(1), G_SPEC(2), G_SPEC(3), pl.BlockSpec((2, HEAD_DIM), lambda h: (0, h))],
        out_specs=pl.BlockSpec((SEQ, HEAD_DIM), lambda h: (0, h)),
        scratch_shapes=[pltpu.VMEM((N_CHUNKS, HEAD_DIM, HEAD_DIM), F32), pltpu.VMEM((N_CHUNKS, 1, HEAD_DIM), F32),
                        pltpu.VMEM((SEQ, HEAD_DIM), BF16)],
        compiler_params=_params(32),
    )(g5, g5, g5, g5, lb_logits)


def _gla_backward(g5, lb_logits, d_o, d_z):
    q_scale = HEAD_DIM ** -0.5

    def body(ff_ref, fb_ref, v_ref, q_ref, lg_ref, do_ref, dz_ref, dg_ref, dlg_ref, st_ref, dst_ref, decay_ref):
        dg_ref[3, 0:CTX_LEN, :] = jnp.zeros((CTX_LEN, HEAD_DIM), F32)
        dg_ref[4, 0:CTX_LEN, :] = jnp.zeros((CTX_LEN, HEAD_DIM), F32)
        dg_ref[4, CTX_LEN:ROWS_ALL, :] = dz_ref[...]
        is_row = lax.broadcasted_iota(jnp.int32, (CHUNK, HEAD_DIM), 0)
        for rev in (False, True):
            d = 1 if rev else 0
            f_ref = fb_ref if rev else ff_ref
            lb = _sig(lg_ref[d:d + 1, :])
            keep, tri, tri_t = _chunk_masks(rev)
            last = 0 if rev else CHUNK - 1
            mid = CHUNK // 2 if rev else CHUNK // 2 - 1

            def local_step(t, carry, latent):
                r0 = _chunk_rows(t, rev, latent)
                rows = pl.ds(r0, CHUNK)
                step = t + (N_CTX_CHUNKS if latent else 0)
                _, _, k, lf = _gates(f_ref[rows, :], lb)
                b = _mm_exact_lhs(tri, lf)
                bl = b[last:last + 1, :]
                decay_ref[step] = jnp.exp(bl)
                st_ref[step] = _mm_tn(v_ref[rows, :], k * jnp.exp(bl - b))
                if latent:
                    q_t = _silu(q_ref[rows, :]) * q_scale * jnp.exp(b)
                    dst_ref[step] = _mm_tn(do_ref[pl.ds(pl.multiple_of(r0 - CTX_LEN, CHUNK), CHUNK), :], q_t)
                else:
                    dst_ref[step] = jnp.zeros((HEAD_DIM, HEAD_DIM), F32)
                return carry

            lax.fori_loop(0, N_CTX_CHUNKS, functools.partial(local_step, latent=False), 0, unroll=2)
            lax.fori_loop(0, N_LAT_CHUNKS, functools.partial(local_step, latent=True), 0, unroll=4)

            def scan_step(i, carry):
                st, d_st = carry
                j = N_CHUNKS - 1 - i
                update, d_update = st_ref[i], dst_ref[j]
                st_ref[i] = st
                dst_ref[j] = d_st
                return st * decay_ref[i] + update, d_st * decay_ref[j] + d_update

            zero_state = jnp.zeros((HEAD_DIM, HEAD_DIM), F32)
            lax.fori_loop(0, N_CHUNKS, scan_step, (zero_state, zero_state))

            def grad_step(t, dlb, latent):
                r0 = _chunk_rows(t, rev, latent)
                rows = pl.ds(r0, CHUNK)
                step = t + (N_CTX_CHUNKS if latent else 0)
                sg, f, k, lf = _gates(f_ref[rows, :], lb)
                v = v_ref[rows, :]
                b = _mm_exact_lhs(tri, lf)
                bl = b[last:last + 1, :]
                e_end = jnp.exp(bl - b)
                k_end = k * e_end
                decay = jnp.exp(bl)
                d_st = dst_ref[step]
                st_prev = st_ref[step]
                d_kend = _mm(v, d_st)
                d_decay = jnp.sum(d_st * st_prev, axis=0, keepdims=True)
                t_kend = d_kend * k_end
                d_v = _mm_nt(k_end, d_st)
                d_k = d_kend * e_end
                d_b = -t_kend
                if latent:
                    qpre = q_ref[rows, :]
                    q = _silu(qpre) * q_scale
                    bm = b[mid:mid + 1, :]
                    e_b, e_qm, e_km = jnp.exp(b), jnp.exp(b - bm), jnp.exp(bm - b)
                    q_t, q_m, k_m = q * e_b, q * e_qm, k * e_km
                    a = jnp.where(keep, _mm_nt(q_m, k_m), 0.0)
                    d_out = do_ref[pl.ds(pl.multiple_of(r0 - CTX_LEN, CHUNK), CHUNK), :]
                    d_a = jnp.where(keep, _mm_nt(d_out, v), 0.0)
                    d_qm = _mm(d_a, k_m)
                    d_km = _mm_tn(d_a, q_m)
                    d_qt = _mm(d_out, st_prev)
                    d_v = d_v + _mm_tn(a, d_out)
                    d_k = d_k + d_km * e_km
                    d_b = d_b + d_qt * q_t + d_qm * q_m - d_km * k_m
                    d_q = d_qt * e_b + d_qm * e_qm
                at_last = jnp.sum(t_kend, axis=0, keepdims=True) + d_decay * decay
                d_b = d_b + jnp.where(is_row == last, at_last, 0.0)
                d_lf = _mm_exact_lhs(tri_t, d_b)
                d_f = d_lf / f - d_k
                dg_ref[d, rows, :] = d_f * (1.0 - lb) * sg * (1.0 - sg)
                if rev:
                    dg_ref[2, rows, :] += d_v
                else:
                    dg_ref[2, rows, :] = d_v
                if latent:
                    d_qpre = d_q * q_scale * _dsilu(qpre)
                    if rev:
                        dg_ref[3, rows, :] += d_qpre
                    else:
                        dg_ref[3, rows, :] = d_qpre
                return dlb + jnp.sum(d_f * (1.0 - sg), axis=0, keepdims=True)

            dlb = lax.fori_loop(0, N_LAT_CHUNKS, functools.partial(grad_step, latent=True), jnp.zeros((1, HEAD_DIM), F32),
                                unroll=2)
            dlb = lax.fori_loop(0, N_CTX_CHUNKS, functools.partial(grad_step, latent=False), dlb, unroll=2)
            dlg_ref[d:d + 1, :] = dlb * lb * (1.0 - lb)

    col = pl.BlockSpec((SEQ, HEAD_DIM), lambda h: (0, h))
    return pl.pallas_call(
        body, name="gla_backward", grid=(HEADS,),
        out_shape=[jax.ShapeDtypeStruct((HGRN_SECTIONS, ROWS_ALL, D_MODEL), F32), jax.ShapeDtypeStruct((2, D_MODEL), F32)],
        in_specs=[G_SPEC(0), G_SPEC(1), G_SPEC(2), G_SPEC(3), pl.BlockSpec((2, HEAD_DIM), lambda h: (0, h)), col, col],
        out_specs=[pl.BlockSpec((HGRN_SECTIONS, ROWS_ALL, HEAD_DIM), lambda h: (0, 0, h)),
                   pl.BlockSpec((2, HEAD_DIM), lambda h: (0, h))],
        scratch_shapes=[pltpu.VMEM((N_CHUNKS, HEAD_DIM, HEAD_DIM), F32), pltpu.VMEM((N_CHUNKS, HEAD_DIM, HEAD_DIM), F32),
                        pltpu.VMEM((N_CHUNKS, 1, HEAD_DIM), F32)],
        compiler_params=_params(48),
    )(g5, g5, g5, g5, lb_logits, d_o, d_z)


GROUP = 2 * CHUNK


def _group_masks(rev):
    rid = lax.broadcasted_iota(jnp.int32, (GROUP, GROUP), 0)
    cid = lax.broadcasted_iota(jnp.int32, (GROUP, GROUP), 1)
    same = (rid >= CHUNK) == (cid >= CHUNK)
    causal = (cid >= rid) if rev else (cid <= rid)
    anti = (cid <= rid) if rev else (cid >= rid)
    bf = lambda m: jnp.where(m, jnp.ones((GROUP, GROUP), F32), jnp.zeros((GROUP, GROUP), F32)).astype(BF16)
    keep = same & causal
    return keep, bf(keep), bf(same & anti), bf(same & jnp.logical_not(anti))


def _group_sum(m_bf, a):
    hi, lo = _split2(a)
    r = jnp.dot(m_bf, jnp.concatenate([hi, lo], axis=1), preferred_element_type=F32)
    return r[:, :HEAD_DIM] + r[:, HEAD_DIM:]


def _chunk_row(a, pos):
    return jnp.concatenate([jnp.broadcast_to(a[c * CHUNK + pos:c * CHUNK + pos + 1, :], (CHUNK, HEAD_DIM)) for c in range(2)], axis=0)


def _by_chunk(a, second):
    return jnp.concatenate([jnp.where(second, 0.0, a), jnp.where(second, a, 0.0)], axis=1)


def _own_block(r):
    return jnp.concatenate([r[0:CHUNK, 0:HEAD_DIM], r[CHUNK:GROUP, HEAD_DIM:2 * HEAD_DIM]], axis=0)


def _scan_step_of(row_chunk, rev, latent):
    if not rev:
        return row_chunk
    return (N_CHUNKS + N_CTX_CHUNKS - 1 - row_chunk) if latent else (N_CTX_CHUNKS - 1 - row_chunk)


def _group_rows(i, j, per_step, latent):
    base = CTX_LEN if latent else 0
    return pl.multiple_of(base + (i * per_step + j) * GROUP, GROUP)


GROUPS_PER_STEP = 4


def _gla_forward(g5, lb_logits, late_shards):
    q_scale = HEAD_DIM ** -0.5
    per_lat, per_ctx = GROUPS_PER_STEP, min(GROUPS_PER_STEP, CTX_LEN // GROUP)
    n_late = len(GATHER_LATE)

    def body(ff_ref, fb_ref, v_ref, q_ref, lg_ref, *rest):
        shard_refs, o_ref, full_refs = rest[:n_late], rest[n_late], rest[n_late + 1:2 * n_late + 1]
        st_ref, decay_ref, qt_ref = rest[2 * n_late + 1:2 * n_late + 4]
        start_gather, wait_gather = _gather_direct(GATHER_LATE, shard_refs, full_refs, *rest[2 * n_late + 4:])

        @pl.when(pl.program_id(0) == 0)
        def _():
            start_gather()

        second = lax.broadcasted_iota(jnp.int32, (GROUP, HEAD_DIM), 0) >= CHUNK
        for rev in (False, True):
            f_ref = fb_ref if rev else ff_ref
            lb = _sig(lg_ref[1:2, :] if rev else lg_ref[0:1, :])
            keep, tri, _, _ = _group_masks(rev)
            last = 0 if rev else CHUNK - 1
            mid = CHUNK // 2 if rev else CHUNK // 2 - 1

            def local_step(i, carry, latent, per):
                r0s = [_group_rows(i, j, per, latent) for j in range(per)]
                rows = [pl.ds(r0, GROUP) for r0 in r0s]
                gates = [_gates(f_ref[r, :], lb) for r in rows]
                vs = [v_ref[r, :] for r in rows]
                bs = [_group_sum(tri, g[3]) for g in gates]
                bls = [_chunk_row(b, last) for b in bs]
                ups = [_mm_tn(v, _by_chunk(g[2] * jnp.exp(bl - b), second)) for v, g, b, bl in zip(vs, gates, bs, bls)]
                if latent:
                    qs = [_silu(q_ref[r, :]) * q_scale for r in rows]
                    bms = [_chunk_row(b, mid) for b in bs]
                    a_s = [_mm_nt(q * jnp.exp(b - bm), g[2] * jnp.exp(bm - b)) for q, g, b, bm in zip(qs, gates, bs, bms)]
                    outs = [_mm(jnp.where(keep, a, 0.0), v) for a, v in zip(a_s, vs)]
                for j in range(per):
                    for c in range(2):
                        step = _scan_step_of(r0s[j] // CHUNK + c, rev, latent)
                        decay_ref[step] = jnp.exp(bls[j][c * CHUNK:c * CHUNK + 1, :])
                        st_ref[step] = ups[j][:, c * HEAD_DIM:(c + 1) * HEAD_DIM]
                    if latent:
                        orow = pl.ds(pl.multiple_of(r0s[j] - CTX_LEN, GROUP), GROUP)
                        qt_ref[orow, :] = (qs[j] * jnp.exp(bs[j])).astype(BF16)
                        if rev:
                            o_ref[orow, :] += outs[j]
                        else:
                            o_ref[orow, :] = outs[j]
                return carry

            lax.fori_loop(0, CTX_LEN // (per_ctx * GROUP), functools.partial(local_step, latent=False, per=per_ctx), 0)
            lax.fori_loop(0, SEQ // (per_lat * GROUP), functools.partial(local_step, latent=True, per=per_lat), 0)

            def scan_step(t, st):
                update = st_ref[t]
                st_ref[t] = st
                return st * decay_ref[t] + update

            lax.fori_loop(0, N_CHUNKS, scan_step, jnp.zeros((HEAD_DIM, HEAD_DIM), F32), unroll=2)

            def inter_step(i, carry):
                r0s = [_group_rows(i, j, per_lat, True) for j in range(per_lat)]
                orows = [pl.ds(pl.multiple_of(r0 - CTX_LEN, GROUP), GROUP) for r0 in r0s]
                states = [jnp.concatenate([st_ref[_scan_step_of(r0 // CHUNK + c, rev, True)] for c in range(2)], axis=0)
                          for r0 in r0s]
                prods = [lax.dot_general(qt_ref[orow, :], s.astype(BF16), (((1,), (1,)), ((), ())), preferred_element_type=F32)
                         for orow, s in zip(orows, states)]
                for orow, r in zip(orows, prods):
                    o_ref[orow, :] += _own_block(r)
                return carry

            lax.fori_loop(0, SEQ // (per_lat * GROUP), inter_step, 0)

        @pl.when(pl.program_id(0) == HEADS - 1)
        def _():
            wait_gather()

    outs = pl.pallas_call(
        body, name="gla_forward", grid=(HEADS,),
        out_shape=[jax.ShapeDtypeStruct((SEQ, D_MODEL), F32)] + [jax.ShapeDtypeStruct(W_SPECS[k][0], BF16) for k in GATHER_LATE],
        in_specs=[G_SPEC(0), G_SPEC(1), G_SPEC(2), G_SPEC(3), pl.BlockSpec((2, HEAD_DIM), lambda h: (0, h))] + [ANY] * n_late,
        out_specs=[pl.BlockSpec((SEQ, HEAD_DIM), lambda h: (0, h))] + [ANY] * n_late,
        scratch_shapes=[pltpu.VMEM((N_CHUNKS, HEAD_DIM, HEAD_DIM), F32), pltpu.VMEM((N_CHUNKS, 1, HEAD_DIM), F32),
                        pltpu.VMEM((SEQ, HEAD_DIM), BF16)] + _comm_sems(n_late),
        compiler_params=_params(32),
    )(g5, g5, g5, g5, lb_logits, *late_shards)
    return outs[0], dict(zip(GATHER_LATE, outs[1:]))


def _gla_backward(g5, lb_logits, d_o, d_z, late_parts):
    q_scale = HEAD_DIM ** -0.5
    per_lat, per_ctx = GROUPS_PER_STEP, min(GROUPS_PER_STEP, CTX_LEN // GROUP)
    n_late = len(GATHER_LATE)

    def body(ff_ref, fb_ref, v_ref, q_ref, lg_ref, do_ref, dz_ref, *rest):
        part_refs, (dg_ref, dlg_ref), slot_refs = rest[:n_late], rest[n_late:n_late + 2], rest[n_late + 2:2 * n_late + 2]
        st_ref, dst_ref, decay_ref = rest[2 * n_late + 2:2 * n_late + 5]
        start_scatter, wait_scatter = _scatter_direct(GATHER_LATE, part_refs, slot_refs, *rest[2 * n_late + 5:])

        @pl.when(pl.program_id(0) == 0)
        def _():
            start_scatter()

        dg_ref[3, 0:CTX_LEN, :] = jnp.zeros((CTX_LEN, HEAD_DIM), F32)
        dg_ref[4, 0:CTX_LEN, :] = jnp.zeros((CTX_LEN, HEAD_DIM), F32)
        dg_ref[4, CTX_LEN:ROWS_ALL, :] = dz_ref[...]
        second = lax.broadcasted_iota(jnp.int32, (GROUP, HEAD_DIM), 0) >= CHUNK
        for rev in (False, True):
            d = 1 if rev else 0
            f_ref = fb_ref if rev else ff_ref
            lb = _sig(lg_ref[d:d + 1, :])
            keep, tri, tri_t, strict = _group_masks(rev)
            last = 0 if rev else CHUNK - 1
            mid = CHUNK // 2 if rev else CHUNK // 2 - 1

            def local_step(i, carry, latent, per):
                r0s = [_group_rows(i, j, per, latent) for j in range(per)]
                rows = [pl.ds(r0, GROUP) for r0 in r0s]
                gates = [_gates(f_ref[r, :], lb) for r in rows]
                bs = [_group_sum(tri, g[3]) for g in gates]
                bls = [_chunk_row(b, last) for b in bs]
                ups = [_mm_tn(v_ref[r, :], _by_chunk(g[2] * jnp.exp(bl - b), second)) for r, g, b, bl in zip(rows, gates, bs, bls)]
                if latent:
                    orows = [pl.ds(pl.multiple_of(r0 - CTX_LEN, GROUP), GROUP) for r0 in r0s]
                    d_ups = [_mm_tn(do_ref[orow, :], _by_chunk(_silu(q_ref[r, :]) * q_scale * jnp.exp(b), second))
                             for orow, r, b in zip(orows, rows, bs)]
                for j in range(per):
                    for c in range(2):
                        step = _scan_step_of(r0s[j] // CHUNK + c, rev, latent)
                        decay_ref[step] = jnp.exp(bls[j][c * CHUNK:c * CHUNK + 1, :])
                        st_ref[step] = ups[j][:, c * HEAD_DIM:(c + 1) * HEAD_DIM]
                        if latent:
                            dst_ref[step] = d_ups[j][:, c * HEAD_DIM:(c + 1) * HEAD_DIM]
                        else:
                            dst_ref[step] = jnp.zeros((HEAD_DIM, HEAD_DIM), F32)
                return carry

            lax.fori_loop(0, CTX_LEN // (per_ctx * GROUP), functools.partial(local_step, latent=False, per=per_ctx), 0)
            lax.fori_loop(0, SEQ // (per_lat * GROUP), functools.partial(local_step, latent=True, per=per_lat), 0)

            def scan_step(i, carry):
                st, d_st = carry
                j = N_CHUNKS - 1 - i
                update, d_update = st_ref[i], dst_ref[j]
                st_ref[i] = st
                dst_ref[j] = d_st
                return st * decay_ref[i] + update, d_st * decay_ref[j] + d_update

            zero_state = jnp.zeros((HEAD_DIM, HEAD_DIM), F32)
            lax.fori_loop(0, N_CHUNKS, scan_step, (zero_state, zero_state))

            def grad_step(i, dlb, latent, per):
                r0s = [_group_rows(i, j, per, latent) for j in range(per)]
                rows = [pl.ds(r0, GROUP) for r0 in r0s]
                gates = [_gates(f_ref[r, :], lb) for r in rows]
                vs = [v_ref[r, :] for r in rows]
                bs = [_group_sum(tri, g[3]) for g in gates]
                bls = [_chunk_row(b, last) for b in bs]
                e_ends = [jnp.exp(bl - b) for b, bl in zip(bs, bls)]
                k_ends = [g[2] * e for g, e in zip(gates, e_ends)]
                sts = [[st_ref[_scan_step_of(r0 // CHUNK + c, rev, latent)] for c in range(2)] for r0 in r0s]
                d_sts = [[dst_ref[_scan_step_of(r0 // CHUNK + c, rev, latent)] for c in range(2)] for r0 in r0s]
                d_kends = [_own_block(_mm(v, jnp.concatenate(ds, axis=1))) for v, ds in zip(vs, d_sts)]
                d_vs = [_own_block(_mm_nt(ke, jnp.concatenate(ds, axis=0))) for ke, ds in zip(k_ends, d_sts)]
                at_last = [jnp.concatenate([jnp.broadcast_to(jnp.sum(ds[c] * s[c], axis=0, keepdims=True), (CHUNK, HEAD_DIM))
                                            for c in range(2)], axis=0) * jnp.exp(bl) for ds, s, bl in zip(d_sts, sts, bls)]
                t_kends = [dk * ke for dk, ke in zip(d_kends, k_ends)]
                d_ks = [dk * e for dk, e in zip(d_kends, e_ends)]
                d_lfs = [_group_sum(strict, t) + al for t, al in zip(t_kends, at_last)]
                if latent:
                    orows = [pl.ds(pl.multiple_of(r0 - CTX_LEN, GROUP), GROUP) for r0 in r0s]
                    qpres = [q_ref[r, :] for r in rows]
                    qs = [_silu(qp) * q_scale for qp in qpres]
                    bms = [_chunk_row(b, mid) for b in bs]
                    e_bs = [jnp.exp(b) for b in bs]
                    e_qms = [jnp.exp(b - bm) for b, bm in zip(bs, bms)]
                    e_kms = [jnp.exp(bm - b) for b, bm in zip(bs, bms)]
                    q_ts = [q * e for q, e in zip(qs, e_bs)]
                    q_ms = [q * e for q, e in zip(qs, e_qms)]
                    k_ms = [g[2] * e for g, e in zip(gates, e_kms)]
                    d_outs = [do_ref[orow, :] for orow in orows]
                    a_s = [jnp.where(keep, _mm_nt(qm, km), 0.0) for qm, km in zip(q_ms, k_ms)]
                    d_as = [jnp.where(keep, _mm_nt(do, v), 0.0) for do, v in zip(d_outs, vs)]
                    d_qts = [_own_block(_mm(do, jnp.concatenate(s, axis=1))) for do, s in zip(d_outs, sts)]
                    d_qms = [_mm(da, km) for da, km in zip(d_as, k_ms)]
                    d_kms = [_mm_tn(da, qm) for da, qm in zip(d_as, q_ms)]
                    d_vs = [dv + _mm_tn(a, do) for dv, a, do in zip(d_vs, a_s, d_outs)]
                    d_ks = [dk + dkm * e for dk, dkm, e in zip(d_ks, d_kms, e_kms)]
                    d_lfs = [dl + _group_sum(tri_t, dqt * qt + dqm * qm - dkm * km)
                             for dl, dqt, qt, dqm, qm, dkm, km in zip(d_lfs, d_qts, q_ts, d_qms, q_ms, d_kms, k_ms)]
                    d_qs = [dqt * eb + dqm * eq for dqt, eb, dqm, eq in zip(d_qts, e_bs, d_qms, e_qms)]
                for j in range(per):
                    sg, f = gates[j][0], gates[j][1]
                    d_f = d_lfs[j] / f - d_ks[j]
                    dg_ref[d, rows[j], :] = d_f * (1.0 - lb) * sg * (1.0 - sg)
                    dlb = dlb + jnp.sum(d_f * (1.0 - sg), axis=0, keepdims=True)
                    if rev:
                        dg_ref[2, rows[j], :] += d_vs[j]
                    else:
                        dg_ref[2, rows[j], :] = d_vs[j]
                    if latent:
                        d_qpre = d_qs[j] * q_scale * _dsilu(qpres[j])
                        if rev:
                            dg_ref[3, rows[j], :] += d_qpre
                        else:
                            dg_ref[3, rows[j], :] = d_qpre
                return dlb

            dlb = lax.fori_loop(0, SEQ // (per_lat * GROUP), functools.partial(grad_step, latent=True, per=per_lat),
                                jnp.zeros((1, HEAD_DIM), F32))
            dlb = lax.fori_loop(0, CTX_LEN // (per_ctx * GROUP), functools.partial(grad_step, latent=False, per=per_ctx), dlb)
            dlg_ref[d:d + 1, :] = dlb * lb * (1.0 - lb)

        @pl.when(pl.program_id(0) == HEADS - 1)
        def _():
            wait_scatter()

    col = pl.BlockSpec((SEQ, HEAD_DIM), lambda h: (0, h))
    outs = pl.pallas_call(
        body, name="gla_backward", grid=(HEADS,),
        out_shape=[jax.ShapeDtypeStruct((HGRN_SECTIONS, ROWS_ALL, D_MODEL), F32), jax.ShapeDtypeStruct((2, D_MODEL), F32)]
        + _slot_shapes(GATHER_LATE),
        in_specs=[G_SPEC(0), G_SPEC(1), G_SPEC(2), G_SPEC(3), pl.BlockSpec((2, HEAD_DIM), lambda h: (0, h)), col, col]
        + [ANY] * n_late,
        out_specs=[pl.BlockSpec((HGRN_SECTIONS, ROWS_ALL, HEAD_DIM), lambda h: (0, 0, h)),
                   pl.BlockSpec((2, HEAD_DIM), lambda h: (0, h))] + [ANY] * n_late,
        scratch_shapes=[pltpu.VMEM((N_CHUNKS, HEAD_DIM, HEAD_DIM), F32), pltpu.VMEM((N_CHUNKS, HEAD_DIM, HEAD_DIM), F32),
                        pltpu.VMEM((N_CHUNKS, 1, HEAD_DIM), F32)] + _comm_sems(n_late),
        compiler_params=_params(48),
    )(g5, g5, g5, g5, lb_logits, d_o, d_z, *late_parts)
    return outs[0], outs[1], dict(zip(GATHER_LATE, outs[2:]))


def _head_norm(o, gw, scr):
    rs = []
    for h in range(HEADS):
        cols = slice(h * HEAD_DIM, (h + 1) * HEAD_DIM)
        oh = o[:, cols]
        r = lax.rsqrt(jnp.mean(oh * oh, axis=-1, keepdims=True) + EPS)
        scr[:, cols] = oh * r
        rs.append(r)
    return rs


def _hgrn_out_forward(o_raw, g5, xin, gnorm_w, gate, w_out):
    tm = ROW_TILE

    def body(o_ref, z_ref, x_ref, gw_ref, gate_ref, w_ref, x1_ref, res_ref, scr):
        _head_norm(o_ref[...], None, scr)
        a = scr[...] * gw_ref[...] * _silu(z_ref[...])
        res = _mm(a, w_ref[...])
        res_ref[...] = res
        x1_ref[...] = x_ref[...] + gate_ref[...] * res

    tile = pl.BlockSpec((tm, D_MODEL), lambda i: (i, 0))
    vec = pl.BlockSpec((1, D_MODEL), lambda i: (0, 0))
    return pl.pallas_call(
        body, name="hgrn_out_forward", grid=(SEQ // tm,),
        out_shape=[jax.ShapeDtypeStruct((SEQ, D_MODEL), F32)] * 2,
        in_specs=[tile, pl.BlockSpec((None, tm, D_MODEL), lambda i: (4, i + CTX_LEN // tm, 0)), tile, vec, vec,
                  pl.BlockSpec((D_MODEL, D_MODEL), lambda i: (0, 0))],
        out_specs=[tile, tile],
        scratch_shapes=[pltpu.VMEM((tm, D_MODEL), F32)],
        compiler_params=_params(32),
    )(o_raw, g5, xin, gnorm_w, gate, w_out)


def _hgrn_out_backward(d_x1, o_raw, g5, res, gnorm_w, gate, w_out):
    tm = ROW_TILE

    def body(dx_ref, o_ref, z_ref, res_ref, gw_ref, gate_ref, w_ref, do_ref, dz_ref, dw_out, dgate_ref, dgw_ref, scr, scr2,
             dw_ref):
        @pl.when(pl.program_id(0) == 0)
        def _():
            dw_ref[...] = jnp.zeros_like(dw_ref)
            dgate_ref[...] = jnp.zeros_like(dgate_ref)
            dgw_ref[...] = jnp.zeros_like(dgw_ref)

        dx = dx_ref[...]
        dgate_ref[...] += jnp.sum(dx * res_ref[...], axis=0, keepdims=True)
        d_res = (dx * gate_ref[...]).astype(BF16)
        d_a = _mm_nt(d_res, w_ref[...])
        rs = _head_norm(o_ref[...], None, scr)
        z = z_ref[...]
        sz = _silu(z)
        o_hat = scr[...]
        o_n = o_hat * gw_ref[...]
        dw_ref[...] += _mm_tn(o_n * sz, d_res)
        d_on = d_a * sz
        dz_ref[...] = d_a * o_n * _dsilu(z)
        dgw_ref[...] += jnp.sum(d_on * o_hat, axis=0, keepdims=True)
        scr2[...] = d_on * gw_ref[...]
        for h in range(HEADS):
            cols = slice(h * HEAD_DIM, (h + 1) * HEAD_DIM)
            dh, oh = scr2[:, cols], scr[:, cols]
            do_ref[:, cols] = rs[h] * (dh - oh * jnp.mean(dh * oh, axis=-1, keepdims=True))

        @pl.when(pl.program_id(0) == SEQ // tm - 1)
        def _():
            dw_out[...] = dw_ref[...].astype(BF16)

    tile = pl.BlockSpec((tm, D_MODEL), lambda i: (i, 0))
    vec = pl.BlockSpec((1, D_MODEL), lambda i: (0, 0))
    mat = pl.BlockSpec((D_MODEL, D_MODEL), lambda i: (0, 0))
    return pl.pallas_call(
        body, name="hgrn_out_backward", grid=(SEQ // tm,),
        out_shape=[jax.ShapeDtypeStruct((SEQ, D_MODEL), F32)] * 2 + [jax.ShapeDtypeStruct((D_MODEL, D_MODEL), BF16)]
        + [jax.ShapeDtypeStruct((1, D_MODEL), F32)] * 2,
        in_specs=[tile, tile, pl.BlockSpec((None, tm, D_MODEL), lambda i: (4, i + CTX_LEN // tm, 0)), tile, vec, vec, mat],
        out_specs=[tile, tile, mat, vec, vec],
        scratch_shapes=[pltpu.VMEM((tm, D_MODEL), F32)] * 2 + [pltpu.VMEM((D_MODEL, D_MODEL), F32)],
        compiler_params=_params(40),
    )(d_x1, o_raw, g5, res, gnorm_w, gate, w_out)


def _pool_constants():
    win = np.zeros((POOL_GROUPS, ROW_TILE, ROW_TILE), np.float32)
    inv = np.zeros((POOL_GROUPS, ROW_TILE, 1), np.float32)
    for g, w in enumerate(POOL_WINDOWS):
        for t in range(ROW_TILE):
            base, p = (t // GRID_W) * GRID_W, t % GRID_W
            lo = min(max(p - w // 2, 0), GRID_W)
            hi = min(max(p - w // 2 + w, 0), GRID_W)
            win[g, t, base + lo:base + hi] = 1.0
            inv[g, t, 0] = 1.0 / np.float32(hi - lo)
    return jnp.asarray(win, BF16), jnp.asarray(win.transpose(0, 2, 1), BF16), jnp.asarray(inv, F32)


def _pool_mix(u_ref, wg_ref, ps_ref, win_ref, inv_ref, pooled_scr, yg_scr):
    for g in range(POOL_GROUPS):
        cols = slice(g * POOL_GROUP_DIM, (g + 1) * POOL_GROUP_DIM)
        ug = u_ref[:, cols]
        pooled = _mm_exact_lhs(win_ref[g], ug) * inv_ref[g] - ug
        if pooled_scr is not None:
            pooled_scr[:, cols] = pooled
        yg_scr[:, cols] = _mm(pooled, wg_ref[g])


def _pool_forward_loss(uz, x1, target, gate, w_grp, pool_scale, w_out, final_w):
    tm = ROW_TILE
    win, _, inv = _pool_constants()

    def body(u_ref, z_ref, x_ref, t_ref, gate_ref, wg_ref, ps_ref, w_ref, fw_ref, win_ref, inv_ref,
             dx_ref, loss_ref, dfw_ref, dgate_ref, yg_scr):
        @pl.when(pl.program_id(0) == 0)
        def _():
            loss_ref[...] = jnp.zeros_like(loss_ref)
            dfw_ref[...] = jnp.zeros_like(dfw_ref)
            dgate_ref[...] = jnp.zeros_like(dgate_ref)

        _pool_mix(u_ref, wg_ref, ps_ref, win_ref, inv_ref, None, yg_scr)
        a = yg_scr[...] * ps_ref[...] * _silu(z_ref[...])
        res = _mm(a, w_ref[...])
        x2 = x_ref[...] + gate_ref[...] * res
        r = lax.rsqrt(jnp.mean(x2 * x2, axis=-1, keepdims=True) + EPS)
        xh = x2 * r
        fw = fw_ref[...]
        err = xh * fw - t_ref[...]
        loss_ref[...] += 0.5 * jnp.sum(jnp.mean(err * err, axis=-1, keepdims=True))
        d_y = err * (1.0 / D_MODEL)
        dfw_ref[...] += jnp.sum(d_y * xh, axis=0, keepdims=True)
        d_xh = d_y * fw
        d_x2 = r * (d_xh - xh * jnp.mean(d_xh * xh, axis=-1, keepdims=True))
        dx_ref[...] = d_x2
        dgate_ref[...] += jnp.sum(d_x2 * res, axis=0, keepdims=True)

    tile = pl.BlockSpec((tm, D_MODEL), lambda i: (i, 0))
    vec = pl.BlockSpec((1, D_MODEL), lambda i: (0, 0))
    grp = pl.BlockSpec((POOL_GROUPS, POOL_GROUP_DIM, POOL_GROUP_DIM), lambda i: (0, 0, 0))
    return pl.pallas_call(
        body, name="pool_forward_loss", grid=(SEQ // tm,),
        out_shape=[jax.ShapeDtypeStruct((SEQ, D_MODEL), F32), jax.ShapeDtypeStruct((8, 128), F32),
                   jax.ShapeDtypeStruct((1, D_MODEL), F32), jax.ShapeDtypeStruct((1, D_MODEL), F32)],
        in_specs=[pl.BlockSpec((None, tm, D_MODEL), lambda i: (0, i, 0)), pl.BlockSpec((None, tm, D_MODEL), lambda i: (1, i, 0)),
                  tile, tile, vec, grp, vec, pl.BlockSpec((D_MODEL, D_MODEL), lambda i: (0, 0)), vec, grp,
                  pl.BlockSpec((POOL_GROUPS, ROW_TILE, 1), lambda i: (0, 0, 0))],
        out_specs=[tile, pl.BlockSpec((8, 128), lambda i: (0, 0)), vec, vec],
        scratch_shapes=[pltpu.VMEM((tm, D_MODEL), F32)],
        compiler_params=_params(32),
    )(uz, uz, x1, target, gate, w_grp, pool_scale, w_out, final_w, win, inv)


def _pool_backward(d_x2, uz, gate, w_grp, pool_scale, w_out):
    tm = ROW_TILE
    win, win_t, inv = _pool_constants()

    def body(dx_ref, u_ref, z_ref, gate_ref, wg_ref, ps_ref, w_ref, win_ref, wint_ref, inv_ref,
             duz_ref, dw_out, dwg_out, dps_ref, pooled_scr, yg_scr, dyg_scr, dw_ref, dwg_ref):
        @pl.when(pl.program_id(0) == 0)
        def _():
            dw_ref[...] = jnp.zeros_like(dw_ref)
            dwg_ref[...] = jnp.zeros_like(dwg_ref)
            dps_ref[...] = jnp.zeros_like(dps_ref)

        _pool_mix(u_ref, wg_ref, ps_ref, win_ref, inv_ref, pooled_scr, yg_scr)
        z = z_ref[...]
        sz = _silu(z)
        yg = yg_scr[...]
        y = yg * ps_ref[...]
        d_res = (dx_ref[...] * gate_ref[...]).astype(BF16)
        d_a = _mm_nt(d_res, w_ref[...])
        dw_ref[...] += _mm_tn(y * sz, d_res)
        d_y = d_a * sz
        duz_ref[1] = d_a * y * _dsilu(z)
        dps_ref[...] += jnp.sum(d_y * yg, axis=0, keepdims=True)
        dyg_scr[...] = d_y * ps_ref[...]
        for g in range(POOL_GROUPS):
            cols = slice(g * POOL_GROUP_DIM, (g + 1) * POOL_GROUP_DIM)
            d_yg = dyg_scr[:, cols].astype(BF16)
            d_pool = _mm_nt(d_yg, wg_ref[g])
            dwg_ref[g] += _mm_tn(pooled_scr[:, cols], d_yg)
            duz_ref[0, :, cols] = _mm_exact_lhs(wint_ref[g], d_pool * inv_ref[g]) - d_pool

        @pl.when(pl.program_id(0) == SEQ // tm - 1)
        def _():
            dw_out[...] = dw_ref[...].astype(BF16)
            dwg_out[...] = dwg_ref[...].astype(BF16)

    tile = pl.BlockSpec((tm, D_MODEL), lambda i: (i, 0))
    vec = pl.BlockSpec((1, D_MODEL), lambda i: (0, 0))
    mat = pl.BlockSpec((D_MODEL, D_MODEL), lambda i: (0, 0))
    grp = pl.BlockSpec((POOL_GROUPS, POOL_GROUP_DIM, POOL_GROUP_DIM), lambda i: (0, 0, 0))
    return pl.pallas_call(
        body, name="pool_backward", grid=(SEQ // tm,),
        out_shape=[jax.ShapeDtypeStruct((POOL_SECTIONS, SEQ, D_MODEL), F32), jax.ShapeDtypeStruct((D_MODEL, D_MODEL), BF16),
                   jax.ShapeDtypeStruct((POOL_GROUPS, POOL_GROUP_DIM, POOL_GROUP_DIM), BF16), jax.ShapeDtypeStruct((1, D_MODEL), F32)],
        in_specs=[tile, pl.BlockSpec((None, tm, D_MODEL), lambda i: (0, i, 0)), pl.BlockSpec((None, tm, D_MODEL), lambda i: (1, i, 0)),
                  vec, grp, vec, mat, grp, grp, pl.BlockSpec((POOL_GROUPS, ROW_TILE, 1), lambda i: (0, 0, 0))],
        out_specs=[pl.BlockSpec((POOL_SECTIONS, tm, D_MODEL), lambda i: (0, i, 0)), mat, grp, vec],
        scratch_shapes=[pltpu.VMEM((tm, D_MODEL), F32)] * 3 + [pltpu.VMEM((D_MODEL, D_MODEL), F32),
                                                               pltpu.VMEM((POOL_GROUPS, POOL_GROUP_DIM, POOL_GROUP_DIM), F32)],
        compiler_params=_params(40),
    )(d_x2, uz, uz, gate, w_grp, pool_scale, w_out, win, win_t, inv)


def _ln_mod_backward(d_g, w, xin, nw, scale, d_up, name, scatter_names=(), scatter_parts=()):
    n_sec, rows, _ = d_g.shape
    n_mod = scale.shape[0]
    tm = ROW_TILE
    n_tiles = rows // tm
    skip = n_mod - 1
    n_sc = len(scatter_names)

    def body(dg_ref, w_ref, x_ref, nw_ref, sc_ref, up_ref, *rest):
        part_refs, (dx_ref, dnw_ref, dmod_ref), slot_refs = rest[:n_sc], rest[n_sc:n_sc + 3], rest[n_sc + 3:2 * n_sc + 3]
        i = pl.program_id(0)
        if n_sc:
            start_scatter, wait_scatter = _scatter_direct(scatter_names, part_refs, slot_refs, *rest[2 * n_sc + 3:])

            @pl.when(i == 0)
            def _():
                start_scatter()

        @pl.when(i == 0)
        def _():
            dnw_ref[...] = jnp.zeros_like(dnw_ref)

        @pl.when((i == 0) | (i == skip))
        def _():
            dmod_ref[...] = jnp.zeros_like(dmod_ref)

        d_h = _mm_nt(dg_ref[0], w_ref[:, 0:D_MODEL])
        for k in range(1, n_sec):
            d_h = d_h + _mm_nt(dg_ref[k], w_ref[:, k * D_MODEL:(k + 1) * D_MODEL])
        xv = x_ref[...]
        r = lax.rsqrt(jnp.mean(xv * xv, axis=-1, keepdims=True) + EPS)
        xh = xv * r
        nw_row = nw_ref[...]
        dmod_ref[0:1, :] += jnp.sum(d_h, axis=0, keepdims=True)
        dmod_ref[1:2, :] += jnp.sum(d_h * (xh * nw_row), axis=0, keepdims=True)
        d_xn = d_h * (1.0 + sc_ref[...])
        dnw_ref[...] += jnp.sum(d_xn * xh, axis=0, keepdims=True)
        d_xh = d_xn * nw_row

        @pl.when(i >= skip)
        def _():
            dx_ref[...] = up_ref[...] + r * (d_xh - xh * jnp.mean(d_xh * xh, axis=-1, keepdims=True))

        if n_sc:
            @pl.when(i == n_tiles - 1)
            def _():
                wait_scatter()

    lat = lambda i: (jnp.maximum(i - skip, 0), 0)
    mod_idx = lambda i: (jnp.minimum(i, n_mod - 1), 0, 0)
    outs = pl.pallas_call(
        body, name=name, grid=(n_tiles,),
        out_shape=[jax.ShapeDtypeStruct((rows - skip * tm, D_MODEL), F32), jax.ShapeDtypeStruct((1, D_MODEL), F32),
                   jax.ShapeDtypeStruct((n_mod, 8, D_MODEL), F32)] + _slot_shapes(scatter_names),
        in_specs=[pl.BlockSpec((n_sec, tm, D_MODEL), lambda i: (0, i, 0)),
                  pl.BlockSpec((D_MODEL, n_sec * D_MODEL), lambda i: (0, 0)),
                  pl.BlockSpec((tm, D_MODEL), lambda i: (i, 0)),
                  pl.BlockSpec((1, D_MODEL), lambda i: (0, 0)),
                  pl.BlockSpec((None, 1, D_MODEL), mod_idx),
                  pl.BlockSpec((tm, D_MODEL), lat)] + [ANY] * n_sc,
        out_specs=[pl.BlockSpec((tm, D_MODEL), lat), pl.BlockSpec((1, D_MODEL), lambda i: (0, 0)),
                   pl.BlockSpec((None, 8, D_MODEL), mod_idx)] + [ANY] * n_sc,
        scratch_shapes=_comm_sems(n_sc) if n_sc else [],
        compiler_params=_params(48),
    )(d_g, w, xin, nw, scale.reshape(n_mod, 1, D_MODEL), d_up, *scatter_parts)
    return outs[0], outs[1], outs[2], list(outs[3:])


def _weight_grad(h, d_g, name):
    n_sec, rows, _ = d_g.shape
    tm = 768 if rows % 768 == 0 else 512
    n_tiles = rows // tm

    def body(h_ref, dg_ref, dw_ref, acc):
        i = pl.program_id(1)
        prod = _mm_tn(h_ref[...], dg_ref[...])

        @pl.when(i == 0)
        def _():
            acc[...] = prod

        @pl.when((i > 0) & (i < n_tiles - 1))
        def _():
            acc[...] += prod

        @pl.when(i == n_tiles - 1)
        def _():
            dw_ref[...] = (acc[...] + prod).astype(BF16)

    return pl.pallas_call(
        body, name=name, grid=(n_sec, n_tiles),
        out_shape=jax.ShapeDtypeStruct((D_MODEL, n_sec * D_MODEL), BF16),
        in_specs=[pl.BlockSpec((tm, D_MODEL), lambda j, i: (i, 0)), pl.BlockSpec((None, tm, D_MODEL), lambda j, i: (j, i, 0))],
        out_specs=pl.BlockSpec((D_MODEL, D_MODEL), lambda j, i: (0, j)),
        scratch_shapes=[pltpu.VMEM((D_MODEL, D_MODEL), F32)],
        compiler_params=_params(32),
    )(h, d_g)


def _sum_slots(slots, name):
    _, rows, cols = slots.shape
    tr = 128

    def body(s_ref, o_ref):
        acc = s_ref[0].astype(F32)
        for d in range(1, N_DEV):
            acc = acc + s_ref[d].astype(F32)
        o_ref[...] = acc

    return pl.pallas_call(
        body, name=name, grid=(rows // tr,),
        out_shape=jax.ShapeDtypeStruct((rows, cols), F32),
        in_specs=[pl.BlockSpec((N_DEV, tr, cols), lambda i: (0, i, 0))],
        out_specs=pl.BlockSpec((tr, cols), lambda i: (i, 0)),
    )(slots)


def _adamw_math(w, g, m, v):
    m = ADAM_B1 * m + (1.0 - ADAM_B1) * g
    v = ADAM_B2 * v + (1.0 - ADAM_B2) * (g * g)
    m_hat = m / (1.0 - ADAM_B1 ** ADAM_STEP)
    v_hat = v / (1.0 - ADAM_B2 ** ADAM_STEP)
    return -ADAM_LR * (m_hat / (jnp.sqrt(v_hat) + ADAM_EPS) + ADAM_WD * w), m, v


def _adamw(w, g, m, v, name):
    rows, cols = w.shape
    tr = rows if rows <= 128 else 128

    def body(w_ref, g_ref, m_ref, v_ref, d_ref, mo_ref, vo_ref):
        d_ref[...], mo_ref[...], vo_ref[...] = _adamw_math(w_ref[...], g_ref[...], m_ref[...], v_ref[...])

    tile = pl.BlockSpec((tr, cols), lambda i: (i, 0))
    return pl.pallas_call(
        body, name=name, grid=(rows // tr,),
        out_shape=[jax.ShapeDtypeStruct((rows, cols), F32)] * 3,
        in_specs=[tile] * 4, out_specs=[tile] * 3,
    )(w, g, m, v)


def _sum_devices(gathered):
    def body(p_ref, o_ref):
        acc = p_ref[0:SMALL_ROWS, :]
        for d in range(1, N_DEV):
            acc = acc + p_ref[d * SMALL_ROWS:(d + 1) * SMALL_ROWS, :]
        o_ref[...] = acc

    return pl.pallas_call(body, name="sum_devices", out_shape=jax.ShapeDtypeStruct((SMALL_ROWS, D_MODEL), F32),
                          in_specs=[VMEM], out_specs=VMEM)(gathered)


def _ada_update(cond_t, d_mod, ada_w, m, v):
    n_layers, _, n_cols = ada_w.shape
    tr = ROW_TILE

    def body(c_ref, dm_ref, w_ref, m_ref, v_ref, g_ref, d_ref, mo_ref, vo_ref):
        g = _mm_f32(_silu(c_ref[...]), dm_ref[...])
        g_ref[...] = g
        d_ref[...], mo_ref[...], vo_ref[...] = _adamw_math(w_ref[...], g, m_ref[...], v_ref[...])

    tile = pl.BlockSpec((None, tr, n_cols), lambda l, i: (l, i, 0))
    return pl.pallas_call(
        body, name="ada_update", grid=(n_layers, D_MODEL // tr),
        out_shape=[jax.ShapeDtypeStruct(ada_w.shape, F32)] * 4,
        in_specs=[pl.BlockSpec((tr, 16), lambda l, i: (i, 0)), pl.BlockSpec((None, 16, n_cols), lambda l, i: (l, 0, 0)),
                  tile, tile, tile],
        out_specs=[tile] * 4,
    )(cond_t, d_mod, ada_w, m, v)


def _cond_ctx_partial(d_modc, ada_w0):
    n_cols = ada_w0.shape[1]
    tr = ROW_TILE

    def body(dm_ref, w_ref, o_ref):
        o_ref[...] = jnp.sum(w_ref[...] * dm_ref[...], axis=-1, keepdims=True)

    return pl.pallas_call(
        body, name="cond_ctx_partial", grid=(D_MODEL // tr,),
        out_shape=jax.ShapeDtypeStruct((D_MODEL, 1), F32),
        in_specs=[pl.BlockSpec((1, n_cols), lambda i: (0, 0)), pl.BlockSpec((tr, n_cols), lambda i: (i, 0))],
        out_specs=pl.BlockSpec((tr, 1), lambda i: (i, 0)),
    )(d_modc, ada_w0)


def _cond_ctx_update(gathered, c_ctx, m, v):
    def body(p_ref, w_ref, m_ref, v_ref, g_ref, d_ref, mo_ref, vo_ref):
        acc = p_ref[0:1, :]
        for s in range(1, N_CHIPS):
            acc = acc + p_ref[16 * s:16 * s + 1, :]
        w = w_ref[...]
        g = acc * _dsilu(w)
        g_ref[...] = g
        d_ref[...], mo_ref[...], vo_ref[...] = _adamw_math(w, g, m_ref[...], v_ref[...])

    return pl.pallas_call(body, name="cond_ctx_update", out_shape=[jax.ShapeDtypeStruct((1, D_MODEL), F32)] * 4,
                          in_specs=[VMEM] * 4, out_specs=[VMEM] * 4)(gathered, c_ctx, m, v)


def _local_step(x2, ctx2, target, mod_mine, mod_ctx, lb_logits, scale_full, w_in_full, late_shards, norm_w, gnorm, final_w):
    row = lambda a: a.reshape(1, -1)
    shift0, scale0, gate0 = (row(a) for a in jnp.split(mod_mine[0], 3))
    shift1, scale1, gate1 = (row(a) for a in jnp.split(mod_mine[1], 3))
    shift_c, scale_c, _ = (row(a) for a in jnp.split(mod_ctx, 3))
    xa = jnp.concatenate([ctx2, x2], axis=0)
    nw0, nw1 = norm_w[0:1], norm_w[1:2]
    scales0 = jnp.concatenate([scale_c, scale0])

    g5, h0 = _ln_mod_matmul(xa, nw0, jnp.concatenate([shift_c, shift0]), scales0, w_in_full, "hgrn_in_forward")
    o_raw, full = _gla_forward(g5, lb_logits, late_shards)
    x1, res0 = _hgrn_out_forward(o_raw, g5, x2, gnorm, gate0, full["hgrn_w_out"])
    uz, h1 = _ln_mod_matmul(x1, nw1, shift1, scale1, full["pool_w_in"], "pool_in_forward")
    d_x2, loss_part, d_final, d_gate1 = _pool_forward_loss(uz, x1, target, gate1, full["pool_w_grp"], scale_full,
                                                           full["pool_w_out"], final_w)

    d_uz, dw_pool_out, dw_pool_grp, d_pscale = _pool_backward(d_x2, uz, gate1, full["pool_w_grp"], scale_full, full["pool_w_out"])
    d_x1, d_nw1, d_mod1, _ = _ln_mod_backward(d_uz, full["pool_w_in"], x1, nw1, scale1, d_x2, "pool_in_backward")
    dw_pool_in = _weight_grad(h1, d_uz, "pool_in_weight_grad")
    d_o, d_z, dw_hgrn_out, d_gate0, d_gnorm = _hgrn_out_backward(d_x1, o_raw, g5, res0, gnorm, gate0, full["hgrn_w_out"])
    late_grads = {"hgrn_w_out": dw_hgrn_out, "pool_w_in": dw_pool_in, "pool_w_grp": dw_pool_grp, "pool_w_out": dw_pool_out}
    d_g5, d_lb, late_slots = _gla_backward(g5, lb_logits, d_o, d_z, [late_grads[k].astype(BF16) for k in GATHER_LATE])
    dw_hgrn_in = _weight_grad(h0, d_g5, "hgrn_in_weight_grad")
    key = GATHER_EARLY[0]
    half = _half_shape(key)
    xi, yi, ci = _my_place()
    own = lax.dynamic_slice(dw_hgrn_in, (ci * half[0], (2 * xi + yi) * half[1]), half)
    slots0 = lax.dynamic_update_slice(jnp.zeros((N_DEV,) + half, BF16), own[None], (4 * xi + 2 * yi + ci, 0, 0))
    send_sem, recv_sem, part_thru, slots_thru, token = _scatter_start(dw_hgrn_in, slots0, key)
    d_x, d_nw0, d_mod0, _ = _ln_mod_backward(d_g5, w_in_full, xa, nw0 + token[0:1, 0:1], scales0, d_x1, "hgrn_in_backward")
    slots = dict(late_slots)
    pending = (send_sem, recv_sem, part_thru, slots_thru)

    zero = jnp.zeros((1, D_MODEL), F32)
    small = jnp.concatenate([d_mod0[1, 0:2], d_gate0, d_mod1[0, 0:2], d_gate1, d_mod0[0, 0:2], zero, d_nw0, d_nw1, d_gnorm,
                             d_final, d_pscale, d_lb, jnp.broadcast_to(loss_part[0:1, 0:1], (1, D_MODEL)),
                             jnp.zeros((SMALL_ROWS - 17, D_MODEL), F32)], axis=0)
    return {"d_x": d_x, "slots": slots, "pending": pending, "small": small}


def kernel(x, c, ctx, c_ctx, ada_w, ada_b, norm_w, hgrn_w_in, hgrn_lb_logits, hgrn_gnorm_w, hgrn_w_out, pool_w_in, pool_w_grp, pool_scale, pool_w_out, final_norm_w, loss_target, m_c_ctx, m_ada_w, m_ada_b, m_norm_w, m_hgrn_w_in, m_hgrn_lb_logits, m_hgrn_gnorm_w, m_hgrn_w_out, m_pool_w_in, m_pool_w_grp, m_pool_scale, m_pool_w_out, m_final_norm_w, v_c_ctx, v_ada_w, v_ada_b, v_norm_w, v_hgrn_w_in, v_hgrn_lb_logits, v_hgrn_gnorm_w, v_hgrn_w_out, v_pool_w_in, v_pool_w_grp, v_pool_scale, v_pool_w_out, v_final_norm_w):
    xi, yi, ci = _my_place()
    chip = 2 * xi + yi
    dev = 4 * xi + 2 * yi + ci
    ada_cols = ada_w.shape[2]
    lb_cols = hgrn_lb_logits.shape[2]
    ps_cols = pool_scale.shape[1]
    row = lambda a: a.reshape(1, -1)

    def chip_cols(a, n):
        return lax.dynamic_slice_in_dim(a, chip * n, n, axis=a.ndim - 1)

    def from_chips(g, rows_per_dev, take):
        return jnp.concatenate([g[2 * s * rows_per_dev:2 * s * rows_per_dev + take] for s in range(N_CHIPS)], axis=1)

    first = jnp.concatenate([jnp.broadcast_to(c, (8, D_MODEL)), jnp.pad(hgrn_lb_logits[0], ((0, 6), (0, 0))),
                             jnp.pad(pool_scale, ((0, 7), (0, 0)))], axis=1)
    first_all = _all_gather_small(first, "gather_cond")
    cond_all = first_all[::8, :D_MODEL]
    lb_logits = from_chips(first_all[:, D_MODEL:D_MODEL + lb_cols], 8, 2)
    scale_full = from_chips(first_all[:, D_MODEL + lb_cols:], 8, 1)
    cond_rows = jnp.concatenate([cond_all, row(c_ctx), jnp.zeros((7, D_MODEL), F32)], axis=0)

    parts = _mod_parts(cond_rows, ada_w, chip_cols(ada_b, ada_cols))
    parts_all = _all_gather_small(parts.reshape(32, ada_cols), "gather_mod")
    mod_all = from_chips(parts_all, 32, 32).reshape(2, 16, 3 * D_MODEL)
    mod_mine = lax.dynamic_index_in_dim(mod_all, dev, axis=1, keepdims=False)

    shards = {"hgrn_w_in": hgrn_w_in[0], "hgrn_w_out": hgrn_w_out[0], "pool_w_in": pool_w_in[0],
              "pool_w_grp": pool_w_grp[0], "pool_w_out": pool_w_out[0]}
    w_in_full = _all_gather_weights([shards[k].astype(BF16) for k in GATHER_EARLY], GATHER_EARLY)[0]

    loc = _local_step(x[0], ctx[0], loss_target[0], mod_mine, mod_all[0, 8], lb_logits, scale_full, w_in_full,
                      [shards[k].astype(BF16) for k in GATHER_LATE], norm_w, hgrn_gnorm_w, row(final_norm_w))

    small_all = _all_gather_small(loc["small"], "gather_small")
    sums = _sum_devices(small_all)
    loss = sums[16, 0]

    def reduce_scattered(slots, names, name):
        halves = []
        for k in names:
            hs = _half_shape(k)
            halves.append(_sum_slots(slots[k].reshape(N_DEV, -1, hs[-1]), "sum_" + k).reshape(hs))
        return dict(zip(names, _exchange_halves(halves, names, name)))

    big_grads = reduce_scattered(loc["slots"], GATHER_LATE, "exchange_halves_late")

    out = {}

    def update(name, w, g, m, v):
        shape = w.shape
        w2, g2, m2, v2 = (a.reshape(-1, shape[-1]) for a in (w, g, m, v))
        d, mn, vn = _adamw(w2, g2, m2, v2, "adamw_" + name)
        out[name] = tuple(a.reshape(shape) for a in (g2, d, mn, vn))

    moments = {"hgrn_w_in": (m_hgrn_w_in, v_hgrn_w_in), "hgrn_w_out": (m_hgrn_w_out, v_hgrn_w_out),
               "pool_w_in": (m_pool_w_in, v_pool_w_in), "pool_w_grp": (m_pool_w_grp, v_pool_w_grp),
               "pool_w_out": (m_pool_w_out, v_pool_w_out)}
    weights = {"hgrn_w_in": hgrn_w_in, "hgrn_w_out": hgrn_w_out, "pool_w_in": pool_w_in, "pool_w_grp": pool_w_grp,
               "pool_w_out": pool_w_out}
    for k in GATHER_LATE:
        update(k, weights[k], big_grads[k], *moments[k])

    g_ada_b = jnp.stack([(sums[0:3] + sums[6:9]).reshape(-1), sums[3:6].reshape(-1)])
    update("ada_b", ada_b, g_ada_b, m_ada_b, v_ada_b)
    update("norm_w", norm_w, sums[9:11], m_norm_w, v_norm_w)
    update("hgrn_gnorm_w", hgrn_gnorm_w, sums[11:12], m_hgrn_gnorm_w, v_hgrn_gnorm_w)
    update("final_norm_w", row(final_norm_w), sums[12:13], row(m_final_norm_w), row(v_final_norm_w))
    update("pool_scale", pool_scale, chip_cols(sums[13:14], ps_cols), m_pool_scale, v_pool_scale)
    update("hgrn_lb_logits", hgrn_lb_logits, chip_cols(sums[14:16], lb_cols)[None], m_hgrn_lb_logits, v_hgrn_lb_logits)

    per_dev = small_all.reshape(N_DEV, SMALL_ROWS, D_MODEL)
    pad7 = jnp.zeros((7, 3 * D_MODEL), F32)
    dm0 = jnp.concatenate([per_dev[:, 0:3].reshape(N_DEV, -1), sums[6:9].reshape(1, -1), pad7], axis=0)
    dm1 = jnp.concatenate([per_dev[:, 3:6].reshape(N_DEV, -1), jnp.zeros((8, 3 * D_MODEL), F32)], axis=0)
    d_mod = chip_cols(jnp.stack([dm0, dm1]), ada_cols)
    out["ada_w"] = _ada_update(cond_rows.T, d_mod, ada_w, m_ada_w, v_ada_w)

    part_c = _cond_ctx_partial(chip_cols(sums[6:9].reshape(1, -1), ada_cols), ada_w[0])
    part_all = _all_gather_small(jnp.pad(part_c.reshape(1, D_MODEL), ((0, 7), (0, 0))), "gather_cond_ctx")
    g_c, d_c, m_c, v_c = _cond_ctx_update(part_all, row(c_ctx), row(m_c_ctx), row(v_c_ctx))
    out["c_ctx"] = tuple(a.reshape(-1) for a in (g_c, d_c, m_c, v_c))
    out["final_norm_w"] = tuple(a.reshape(-1) for a in out["final_norm_w"])

    early_slots = {GATHER_EARLY[0]: _scatter_wait(*loc["pending"], g_c, GATHER_EARLY[0])}
    big_grads = reduce_scattered(early_slots, GATHER_EARLY, "exchange_halves_early")
    for k in GATHER_EARLY:
        update(k, weights[k], big_grads[k], *moments[k])

    names = ["c_ctx", "ada_w", "ada_b", "norm_w", "hgrn_w_in", "hgrn_lb_logits", "hgrn_gnorm_w", "hgrn_w_out", "pool_w_in",
             "pool_w_grp", "pool_scale", "pool_w_out", "final_norm_w"]
    return (loss, loc["d_x"][None], *[out[k][0] for k in names], *[out[k][1] for k in names], *[out[k][2] for k in names],
            *[out[k][3] for k in names])
```

```python
import functools

import numpy as np
import jax
import jax.numpy as jnp
from jax import lax
from jax.experimental import pallas as pl
from jax.experimental.pallas import tpu as pltpu

F32 = jnp.float32
BF16 = jnp.bfloat16

D_MODEL = 1024
SEQ = 2048
CTX_LEN = 256
ROWS_ALL = CTX_LEN + SEQ
HEADS = 8
HEAD_DIM = 128
CHUNK = 64
N_CTX_CHUNKS = CTX_LEN // CHUNK
N_LAT_CHUNKS = SEQ // CHUNK
N_CHUNKS = N_CTX_CHUNKS + N_LAT_CHUNKS
GRID_W = 64
POOL_WINDOWS = (2, 4, 8, 16)
POOL_GROUPS = 4
POOL_GROUP_DIM = 256
HGRN_SECTIONS = 5
POOL_SECTIONS = 2
EPS = 1e-6
N_DEV = 8
N_CHIPS = 4
ROW_TILE = 256
SMALL_ROWS = 24

ADAM_LR = 0.001
ADAM_B1 = 0.9
ADAM_B2 = 0.999
ADAM_EPS = 1e-08
ADAM_WD = 0.01
ADAM_STEP = 10

MESH = pl.DeviceIdType.MESH
MIB = 1 << 20
ANY = pl.BlockSpec(memory_space=pl.ANY)
VMEM = pl.BlockSpec(memory_space=pltpu.VMEM)


def _params(vmem_mib=None):
    if vmem_mib is None:
        return pltpu.CompilerParams()
    return pltpu.CompilerParams(vmem_limit_bytes=vmem_mib * MIB)


def _sig(a):
    return 1.0 / (1.0 + jnp.exp(-a))


def _silu(a):
    return a * _sig(a)


def _dsilu(a):
    s = _sig(a)
    return s * (1.0 + a * (1.0 - s))


def _mm(a, b):
    return jnp.dot(a.astype(BF16), b.astype(BF16), preferred_element_type=F32)


def _mm_nt(a, b):
    return lax.dot_general(a.astype(BF16), b.astype(BF16), (((1,), (1,)), ((), ())), preferred_element_type=F32)


def _mm_tn(a, b):
    return lax.dot_general(a.astype(BF16), b.astype(BF16), (((0,), (0,)), ((), ())), preferred_element_type=F32)


def _split2(a):
    hi = a.astype(BF16)
    lo = (a - hi.astype(F32)).astype(BF16)
    return hi, lo


def _mm_exact_lhs(m_bf, a):
    hi, lo = _split2(a)
    return jnp.dot(m_bf, hi, preferred_element_type=F32) + jnp.dot(m_bf, lo, preferred_element_type=F32)


def _mm_f32(a, b):
    ah, al = _split2(a)
    bh, bl = _split2(b)
    return (jnp.dot(ah, bh, preferred_element_type=F32) + jnp.dot(al, bh, preferred_element_type=F32)
            + jnp.dot(ah, bl, preferred_element_type=F32))


def _my_place():
    return lax.axis_index("x"), lax.axis_index("y"), lax.axis_index("c")


def _all_gather_small(blk, name):
    m_per, n = blk.shape

    def body(x_ref, out_ref, send_sems, recv_sems, local_sem):
        x, y, c = _my_place()
        me, sibling = (x, y, c), (x, y, 1 - c)
        chips = [(1 - x, y), (x, 1 - y), (1 - x, 1 - y)]

        def rows(px, py, pc):
            return out_ref.at[pl.ds((4 * px + 2 * py + pc) * m_per, m_per), :]

        def copy(k, block, to, src=None):
            return pltpu.make_async_remote_copy(
                src_ref=rows(*block) if src is None else src, dst_ref=rows(*block),
                send_sem=send_sems.at[k], recv_sem=recv_sems.at[k], device_id=to, device_id_type=MESH)

        mine = pltpu.make_async_copy(x_ref, rows(*me), local_sem)
        mine.start()
        first = [copy(0, me, sibling, src=x_ref)]
        first += [copy(1 + j, me, (*chip, c), src=x_ref) for j, chip in enumerate(chips)]
        for cp in first:
            cp.start()
        passed = [copy(4 + j, (*chip, c), sibling) for j, chip in enumerate(chips)]
        for j, chip in enumerate(chips):
            copy(1 + j, (*chip, c), me).wait_recv()
            passed[j].start()
        copy(0, sibling, me).wait_recv()
        for j, chip in enumerate(chips):
            copy(4 + j, (*chip, 1 - c), me).wait_recv()
        for cp in first + passed:
            cp.wait_send()
        mine.wait()

    return pl.pallas_call(
        body, name=name,
        out_shape=jax.ShapeDtypeStruct((N_DEV * m_per, n), blk.dtype),
        in_specs=[VMEM], out_specs=VMEM,
        scratch_shapes=[pltpu.SemaphoreType.DMA((7,)), pltpu.SemaphoreType.DMA((7,)), pltpu.SemaphoreType.DMA],
    )(blk)


W_SPECS = {
    "hgrn_w_in": ((D_MODEL, 5 * D_MODEL), (1, 1280, 0, 512)),
    "hgrn_w_out": ((D_MODEL, D_MODEL), (0, 256, 0, 128)),
    "pool_w_in": ((D_MODEL, 2 * D_MODEL), (1, 512, 0, 512)),
    "pool_w_grp": ((POOL_GROUPS, POOL_GROUP_DIM, POOL_GROUP_DIM), (1, 64, 1, 32)),
    "pool_w_out": ((D_MODEL, D_MODEL), (0, 256, 0, 128)),
}
W_NAMES = tuple(W_SPECS)


def _al(v, m):
    return pl.multiple_of(v, m)


def _region(ref, spec, chip, half):
    ca, cn, ha, hn = spec
    idx = [slice(None)] * len(ref.shape)
    if ca == ha:
        if half is None:
            idx[ca] = pl.ds(_al(chip * cn, cn), cn)
        else:
            idx[ca] = pl.ds(_al(chip * cn + half * hn, hn), hn)
    else:
        idx[ca] = pl.ds(_al(chip * cn, cn), cn)
        if half is not None:
            idx[ha] = pl.ds(_al(half * hn, hn), hn)
    return ref.at[tuple(idx)]


def _half_of(ref, spec, half):
    _, _, ha, hn = spec
    idx = [slice(None)] * len(ref.shape)
    idx[ha] = pl.ds(_al(half * hn, hn), hn)
    return ref.at[tuple(idx)]


PIECE_BYTES = 256 * 1024


def _pieces(ref):
    lead = ref.shape[0]
    want = (int(np.prod(ref.shape)) * ref.dtype.itemsize) // PIECE_BYTES
    n = max([1] + [k for k in range(1, want + 1) if lead % k == 0 and (lead // k) % 16 == 0])
    rows = lead // n
    return [ref.at[pl.ds(i * rows, rows)] for i in range(n)]


def _half_shape(name):
    full, (ca, cn, ha, hn) = W_SPECS[name]
    shp = list(full)
    shp[ca] = cn
    shp[ha] = hn
    return tuple(shp)


def _gather_direct(names, sh, full, send_sems, recv_sems, local_sems):
    specs = [W_SPECS[k][1] for k in names]
    x, y, c = _my_place()
    chip_me = 2 * x + y
    chips = [(1 - x, y), (x, 1 - y), (1 - x, 1 - y)]

    def local(a):
        return pltpu.make_async_copy(sh[a], _region(full[a], specs[a], chip_me, None), local_sems.at[a])

    def remote(a, src, dst, to):
        return pltpu.make_async_remote_copy(src_ref=src, dst_ref=dst, send_sem=send_sems.at[a], recv_sem=recv_sems.at[a],
                                            device_id=to, device_id_type=MESH)

    def start():
        for a in range(len(names)):
            local(a).start()
            for px, py in chips:
                remote(a, sh[a], _region(full[a], specs[a], chip_me, None), (px, py, c)).start()

    def wait():
        for a in range(len(names)):
            ca, cn, _, _ = specs[a]
            idx = [slice(None)] * len(full[a].shape)
            idx[ca] = pl.ds(0, 3 * cn)
            three = full[a].at[tuple(idx)]
            remote(a, three, three, (x, y, c)).wait()
            local(a).wait()

    return start, wait


def _scatter_direct(names, part, slots, send_sems, recv_sems, local_sems):
    specs = [W_SPECS[k][1] for k in names]
    x, y, c = _my_place()
    dev_me = 4 * x + 2 * y + c

    def local(a):
        return pltpu.make_async_copy(_region(part[a], specs[a], 2 * x + y, c), slots[a].at[dev_me], local_sems.at[a])

    def start():
        for a in range(len(names)):
            local(a).start()
            for flip in range(1, N_DEV):
                tx = 1 - x if flip >> 2 else x
                ty = 1 - y if (flip >> 1) & 1 else y
                tc = 1 - c if flip & 1 else c
                pltpu.make_async_remote_copy(src_ref=_region(part[a], specs[a], 2 * tx + ty, tc), dst_ref=slots[a].at[dev_me],
                                             send_sem=send_sems.at[a], recv_sem=recv_sems.at[a], device_id=(tx, ty, tc),
                                             device_id_type=MESH).start()

    def wait():
        for a in range(len(names)):
            seven = slots[a].at[pl.ds(0, N_DEV - 1)]
            pltpu.make_async_remote_copy(src_ref=seven, dst_ref=seven, send_sem=send_sems.at[a], recv_sem=recv_sems.at[a],
                                         device_id=(x, y, c), device_id_type=MESH).wait()
            local(a).wait()

    return start, wait


HBM_SPEC = pl.BlockSpec(memory_space=pltpu.HBM)
SEM_SPEC = pl.BlockSpec(memory_space=pltpu.SEMAPHORE)
SPLIT_EFFECT = pltpu.SideEffectType.DATAFLOW_SIDE_EFFECTING


SCATTER_STAGES = ((1, 2, 4, 6), (3, 5, 7))


def _scatter_start(part, slots, after, name_key, stage):
    spec = W_SPECS[name_key][1]

    def body(part_ref, slots_ref, after_ref, send_sem, recv_sem, part_thru, slots_thru, token):
        x, y, c = _my_place()
        dev_me = 4 * x + 2 * y + c
        for flip in SCATTER_STAGES[stage]:
            tx = 1 - x if flip >> 2 else x
            ty = 1 - y if (flip >> 1) & 1 else y
            tc = 1 - c if flip & 1 else c
            pltpu.make_async_remote_copy(src_ref=_region(part_ref, spec, 2 * tx + ty, tc), dst_ref=slots_ref.at[dev_me],
                                         send_sem=send_sem, recv_sem=recv_sem, device_id=(tx, ty, tc), device_id_type=MESH).start()
        token[...] = jnp.zeros_like(token)

    return pl.pallas_call(
        body, name="scatter_start_%s_%d" % (name_key, stage),
        out_shape=(pltpu.SemaphoreType.DMA(()), pltpu.SemaphoreType.DMA(()), pltpu.HBM(part.shape, part.dtype),
                   pltpu.HBM(slots.shape, slots.dtype), jax.ShapeDtypeStruct((8, 128), F32)),
        in_specs=(HBM_SPEC, HBM_SPEC, ANY), out_specs=(SEM_SPEC, SEM_SPEC, HBM_SPEC, HBM_SPEC, VMEM),
        input_output_aliases={0: 2, 1: 3},
        compiler_params=pltpu.CompilerParams(has_side_effects=SPLIT_EFFECT),
    )(pltpu.with_memory_space_constraint(part, pltpu.HBM), pltpu.with_memory_space_constraint(slots, pltpu.HBM), after)


def _scatter_wait(send_sem, recv_sem, part_thru, slots_thru, after, name_key, stage):
    n_copies = len(SCATTER_STAGES[stage])

    def body(part_ref, slots_ref, send_sem, recv_sem, *rest):
        x, y, c = _my_place()
        landed = slots_ref.at[pl.ds(0, n_copies)]
        copy = pltpu.make_async_remote_copy(src_ref=landed, dst_ref=landed, send_sem=send_sem, recv_sem=recv_sem,
                                            device_id=(x, y, c), device_id_type=MESH)
        copy.wait_send()
        copy.wait_recv()

    return pl.pallas_call(
        body, name="scatter_wait_%s_%d" % (name_key, stage),
        out_shape=(pltpu.HBM(part_thru.shape, part_thru.dtype), pltpu.HBM(slots_thru.shape, slots_thru.dtype)),
        in_specs=(HBM_SPEC, HBM_SPEC, SEM_SPEC, SEM_SPEC) + (ANY,) * len(after), out_specs=(HBM_SPEC, HBM_SPEC),
        input_output_aliases={0: 0, 1: 1},
        compiler_params=pltpu.CompilerParams(has_side_effects=SPLIT_EFFECT),
    )(part_thru, slots_thru, send_sem, recv_sem, *after)


def _comm_sems(n):
    return [pltpu.SemaphoreType.DMA((n,)), pltpu.SemaphoreType.DMA((n,)), pltpu.SemaphoreType.DMA((n,))]


GATHER_EARLY = ("hgrn_w_in",)
GATHER_LATE = ("hgrn_w_out", "pool_w_in", "pool_w_grp", "pool_w_out")


def _all_gather_weights(shards, names):
    n = len(names)
    specs = [W_SPECS[k][1] for k in names]

    def body(*refs):
        sh, full = refs[:n], refs[n:2 * n]
        send_sems, recv_sems, local_sems = refs[2 * n:]
        x, y, c = _my_place()
        chip_me = 2 * x + y
        sibling = (x, y, 1 - c)
        chips = [(1 - x, y), (x, 1 - y), (1 - x, 1 - y)]

        def remote(a, k, src, dst, to):
            return pltpu.make_async_remote_copy(src_ref=src, dst_ref=dst, send_sem=send_sems.at[6 * a + k],
                                                recv_sem=recv_sems.at[6 * a + k], device_id=to, device_id_type=MESH)

        local = [pltpu.make_async_copy(sh[a], _region(full[a], specs[a], chip_me, None), local_sems.at[a]) for a in range(n)]
        for cp in local:
            cp.start()
        sends = []
        for a in range(n):
            for j, (px, py) in enumerate(chips):
                src, dst = _half_of(sh[a], specs[a], c), _region(full[a], specs[a], chip_me, c)
                for s_piece, d_piece in zip(_pieces(src), _pieces(dst)):
                    remote(a, j, s_piece, d_piece, (px, py, c)).start()
                sends.append(remote(a, j, src, dst, (px, py, c)))
        for a in range(n):
            for j, (px, py) in enumerate(chips):
                landed = _region(full[a], specs[a], 2 * px + py, c)
                remote(a, j, landed, landed, (px, py, c)).wait_recv()
                for piece in _pieces(landed):
                    remote(a, 3 + j, piece, piece, sibling).start()
                sends.append(remote(a, 3 + j, landed, landed, sibling))
        for a in range(n):
            for j, (px, py) in enumerate(chips):
                other = _region(full[a], specs[a], 2 * px + py, 1 - c)
                remote(a, 3 + j, other, other, sibling).wait_recv()
        for cp in sends:
            cp.wait_send()
        for cp in local:
            cp.wait()

    return pl.pallas_call(
        body, name="all_gather_weights",
        out_shape=[jax.ShapeDtypeStruct(W_SPECS[k][0], BF16) for k in names],
        in_specs=[VMEM] * n, out_specs=[VMEM] * n,
        scratch_shapes=[pltpu.SemaphoreType.DMA((6 * n,)), pltpu.SemaphoreType.DMA((6 * n,)), pltpu.SemaphoreType.DMA((n,))],
        compiler_params=_params(32),
    )(*shards)


def _slot_shapes(names):
    return [jax.ShapeDtypeStruct((N_DEV,) + _half_shape(k), BF16) for k in names]


def _scatter_grads(parts, names):
    n = len(names)

    def body(*refs):
        start, wait = _scatter_direct(names, refs[:n], refs[n:2 * n], *refs[2 * n:])
        start()
        wait()

    return pl.pallas_call(body, name="scatter_grads", out_shape=_slot_shapes(names), in_specs=[ANY] * n, out_specs=[ANY] * n,
                          scratch_shapes=_comm_sems(n))(*parts)


def _exchange_halves(halves, names, name):
    n = len(names)
    specs = [W_SPECS[k][1] for k in names]

    def shard_shape(k):
        shp = list(_half_shape(k))
        shp[W_SPECS[k][1][2]] *= 2
        return tuple(shp)

    def body(*refs):
        half, out = refs[:n], refs[n:2 * n]
        send_sems, recv_sems, local_sems = refs[2 * n:]
        x, y, c = _my_place()
        sibling = (x, y, 1 - c)

        def remote(a, src, dst):
            return pltpu.make_async_remote_copy(src_ref=src, dst_ref=dst, send_sem=send_sems.at[a], recv_sem=recv_sems.at[a],
                                                device_id=sibling, device_id_type=MESH)

        local = [pltpu.make_async_copy(half[a], _half_of(out[a], specs[a], c), local_sems.at[a]) for a in range(n)]
        for cp in local:
            cp.start()
        for a in range(n):
            mine = _half_of(out[a], specs[a], c)
            for src, dst in zip(_pieces(half[a]), _pieces(mine)):
                remote(a, src, dst).start()
        for a in range(n):
            theirs = _half_of(out[a], specs[a], 1 - c)
            remote(a, theirs, theirs).wait_recv()
        for a in range(n):
            remote(a, half[a], half[a]).wait_send()
        for cp in local:
            cp.wait()

    return pl.pallas_call(
        body, name=name,
        out_shape=[jax.ShapeDtypeStruct(shard_shape(k), F32) for k in names],
        in_specs=[VMEM] * n, out_specs=[VMEM] * n,
        scratch_shapes=[pltpu.SemaphoreType.DMA((n,)), pltpu.SemaphoreType.DMA((n,)), pltpu.SemaphoreType.DMA((n,))],
        compiler_params=_params(32),
    )(*halves)


def _mod_parts(c_rows, ada_w, ada_b_cols):
    n_layers, _, n_cols = ada_w.shape

    def body(c_ref, w_ref, b_ref, o_ref):
        o_ref[...] = _mm_f32(_silu(c_ref[...]), w_ref[...]) + b_ref[...]

    return pl.pallas_call(
        body, name="mod_parts", grid=(n_layers,),
        out_shape=jax.ShapeDtypeStruct((n_layers, 16, n_cols), F32),
        in_specs=[pl.BlockSpec((16, D_MODEL), lambda i: (0, 0)),
                  pl.BlockSpec((None, D_MODEL, n_cols), lambda i: (i, 0, 0)),
                  pl.BlockSpec((None, 1, n_cols), lambda i: (i, 0, 0))],
        out_specs=pl.BlockSpec((None, 16, n_cols), lambda i: (i, 0, 0)),
        compiler_params=_params(40),
    )(c_rows, ada_w, ada_b_cols.reshape(n_layers, 1, n_cols))


def _ln_mod_matmul(xin, nw, shift, scale, w, name):
    rows = xin.shape[0]
    n_sec = w.shape[1] // D_MODEL
    n_mod = shift.shape[0]
    tm = ROW_TILE

    def body(x_ref, nw_ref, sh_ref, sc_ref, w_ref, g_ref, h_ref):
        xv = x_ref[...]
        r = lax.rsqrt(jnp.mean(xv * xv, axis=-1, keepdims=True) + EPS)
        h = ((xv * r * nw_ref[...]) * (1.0 + sc_ref[...]) + sh_ref[...]).astype(BF16)
        h_ref[...] = h
        for k in range(n_sec):
            g_ref[k] = jnp.dot(h, w_ref[:, k * D_MODEL:(k + 1) * D_MODEL], preferred_element_type=F32)

    mod_spec = pl.BlockSpec((None, 1, D_MODEL), lambda i: (jnp.minimum(i, n_mod - 1), 0, 0))
    return pl.pallas_call(
        body, name=name, grid=(rows // tm,),
        out_shape=[jax.ShapeDtypeStruct((n_sec, rows, D_MODEL), F32), jax.ShapeDtypeStruct((rows, D_MODEL), BF16)],
        in_specs=[pl.BlockSpec((tm, D_MODEL), lambda i: (i, 0)),
                  pl.BlockSpec((1, D_MODEL), lambda i: (0, 0)),
                  mod_spec, mod_spec,
                  pl.BlockSpec((D_MODEL, n_sec * D_MODEL), lambda i: (0, 0))],
        out_specs=[pl.BlockSpec((n_sec, tm, D_MODEL), lambda i: (0, i, 0)),
                   pl.BlockSpec((tm, D_MODEL), lambda i: (i, 0))],
        compiler_params=_params(48),
    )(xin, nw, shift.reshape(n_mod, 1, D_MODEL), scale.reshape(n_mod, 1, D_MODEL), w)


def _chunk_masks(rev):
    rid = lax.broadcasted_iota(jnp.int32, (CHUNK, CHUNK), 0)
    cid = lax.broadcasted_iota(jnp.int32, (CHUNK, CHUNK), 1)
    keep = (cid >= rid) if rev else (cid <= rid)
    keep_t = (cid <= rid) if rev else (cid >= rid)
    one, zero = jnp.ones((CHUNK, CHUNK), F32), jnp.zeros((CHUNK, CHUNK), F32)
    return keep, jnp.where(keep, one, zero).astype(BF16), jnp.where(keep_t, one, zero).astype(BF16)


def _chunk_rows(t, rev, latent):
    n = N_LAT_CHUNKS if latent else N_CTX_CHUNKS
    base = CTX_LEN if latent else 0
    idx = (n - 1 - t) if rev else t
    return pl.multiple_of(base + idx * CHUNK, CHUNK)


def _gates(fpre, lb):
    sg = _sig(fpre)
    f = lb + (1.0 - lb) * sg
    return sg, f, 1.0 - f, jnp.log(f)


G_SPEC = lambda sec: pl.BlockSpec((None, ROWS_ALL, HEAD_DIM), lambda h, sec=sec: (sec, 0, h))


def _gla_forward(g5, lb_logits):
    q_scale = HEAD_DIM ** -0.5

    def body(ff_ref, fb_ref, v_ref, q_ref, lg_ref, o_ref, st_ref, decay_ref, qt_ref):
        for rev in (False, True):
            f_ref = fb_ref if rev else ff_ref
            lb = _sig(lg_ref[1:2, :] if rev else lg_ref[0:1, :])
            keep, tri, _ = _chunk_masks(rev)
            last = 0 if rev else CHUNK - 1
            mid = CHUNK // 2 if rev else CHUNK // 2 - 1

            def local_step(t, carry, latent):
                r0 = _chunk_rows(t, rev, latent)
                rows = pl.ds(r0, CHUNK)
                step = t + (N_CTX_CHUNKS if latent else 0)
                _, _, k, lf = _gates(f_ref[rows, :], lb)
                v = v_ref[rows, :]
                b = _mm_exact_lhs(tri, lf)
                bl = b[last:last + 1, :]
                if latent:
                    q = _silu(q_ref[rows, :]) * q_scale
                    bm = b[mid:mid + 1, :]
                    a = _mm_nt(q * jnp.exp(b - bm), k * jnp.exp(bm - b))
                    o = _mm(jnp.where(keep, a, 0.0), v)
                    orow = pl.ds(pl.multiple_of(r0 - CTX_LEN, CHUNK), CHUNK)
                    qt_ref[orow, :] = (q * jnp.exp(b)).astype(BF16)
                    if rev:
                        o_ref[orow, :] += o
                    else:
                        o_ref[orow, :] = o
                decay_ref[step] = jnp.exp(bl)
                st_ref[step] = _mm_tn(v, k * jnp.exp(bl - b))
                return carry

            lax.fori_loop(0, N_CTX_CHUNKS, functools.partial(local_step, latent=False), 0, unroll=2)
            lax.fori_loop(0, N_LAT_CHUNKS, functools.partial(local_step, latent=True), 0, unroll=4)

            def scan_step(t, st):
                update = st_ref[t]
                st_ref[t] = st
                return st * decay_ref[t] + update

            lax.fori_loop(0, N_CHUNKS, scan_step, jnp.zeros((HEAD_DIM, HEAD_DIM), F32), unroll=2)

            def inter_step(t, carry):
                r0 = _chunk_rows(t, rev, True)
                orow = pl.ds(pl.multiple_of(r0 - CTX_LEN, CHUNK), CHUNK)
                o_ref[orow, :] += lax.dot_general(qt_ref[orow, :], st_ref[t + N_CTX_CHUNKS].astype(BF16),
                                                  (((1,), (1,)), ((), ())), preferred_element_type=F32)
                return carry

            lax.fori_loop(0, N_LAT_CHUNKS, inter_step, 0, unroll=4)

    return pl.pallas_call(
        body, name="gla_forward", grid=(HEADS,),
        out_shape=jax.ShapeDtypeStruct((SEQ, D_MODEL), F32),
        in_specs=[G_SPEC(0), G_SPEC(1), G_SPEC(2), G_SPEC(3), pl.BlockSpec((2, HEAD_DIM), lambda h: (0, h))],
        out_specs=pl.BlockSpec((SEQ, HEAD_DIM), lambda h: (0, h)),
        scratch_shapes=[pltpu.VMEM((N_CHUNKS, HEAD_DIM, HEAD_DIM), F32), pltpu.VMEM((N_CHUNKS, 1, HEAD_DIM), F32),
                        pltpu.VMEM((SEQ, HEAD_DIM), BF16)],
        compiler_params=_params(32),
    )(g5, g5, g5, g5, lb_logits)


def _gla_backward(g5, lb_logits, d_o, d_z):
    q_scale = HEAD_DIM ** -0.5

    def body(ff_ref, fb_ref, v_ref, q_ref, lg_ref, do_ref, dz_ref, dg_ref, dlg_ref, st_ref, dst_ref, decay_ref):
        dg_ref[3, 0:CTX_LEN, :] = jnp.zeros((CTX_LEN, HEAD_DIM), F32)
        dg_ref[4, 0:CTX_LEN, :] = jnp.zeros((CTX_LEN, HEAD_DIM), F32)
        dg_ref[4, CTX_LEN:ROWS_ALL, :] = dz_ref[...]
        is_row = lax.broadcasted_iota(jnp.int32, (CHUNK, HEAD_DIM), 0)
        for rev in (False, True):
            d = 1 if rev else 0
            f_ref = fb_ref if rev else ff_ref
            lb = _sig(lg_ref[d:d + 1, :])
            keep, tri, tri_t = _chunk_masks(rev)
            last = 0 if rev else CHUNK - 1
            mid = CHUNK // 2 if rev else CHUNK // 2 - 1

            def local_step(t, carry, latent):
                r0 = _chunk_rows(t, rev, latent)
                rows = pl.ds(r0, CHUNK)
                step = t + (N_CTX_CHUNKS if latent else 0)
                _, _, k, lf = _gates(f_ref[rows, :], lb)
                b = _mm_exact_lhs(tri, lf)
                bl = b[last:last + 1, :]
                decay_ref[step] = jnp.exp(bl)
                st_ref[step] = _mm_tn(v_ref[rows, :], k * jnp.exp(bl - b))
                if latent:
                    q_t = _silu(q_ref[rows, :]) * q_scale * jnp.exp(b)
                    dst_ref[step] = _mm_tn(do_ref[pl.ds(pl.multiple_of(r0 - CTX_LEN, CHUNK), CHUNK), :], q_t)
                else:
                    dst_ref[step] = jnp.zeros((HEAD_DIM, HEAD_DIM), F32)
                return carry

            lax.fori_loop(0, N_CTX_CHUNKS, functools.partial(local_step, latent=False), 0, unroll=2)
            lax.fori_loop(0, N_LAT_CHUNKS, functools.partial(local_step, latent=True), 0, unroll=4)

            def scan_step(i, carry):
                st, d_st = carry
                j = N_CHUNKS - 1 - i
                update, d_update = st_ref[i], dst_ref[j]
                st_ref[i] = st
                dst_ref[j] = d_st
                return st * decay_ref[i] + update, d_st * decay_ref[j] + d_update

            zero_state = jnp.zeros((HEAD_DIM, HEAD_DIM), F32)
            lax.fori_loop(0, N_CHUNKS, scan_step, (zero_state, zero_state))

            def grad_step(t, dlb, latent):
                r0 = _chunk_rows(t, rev, latent)
                rows = pl.ds(r0, CHUNK)
                step = t + (N_CTX_CHUNKS if latent else 0)
                sg, f, k, lf = _gates(f_ref[rows, :], lb)
                v = v_ref[rows, :]
                b = _mm_exact_lhs(tri, lf)
                bl = b[last:last + 1, :]
                e_end = jnp.exp(bl - b)
                k_end = k * e_end
                decay = jnp.exp(bl)
                d_st = dst_ref[step]
                st_prev = st_ref[step]
                d_kend = _mm(v, d_st)
                d_decay = jnp.sum(d_st * st_prev, axis=0, keepdims=True)
                t_kend = d_kend * k_end
                d_v = _mm_nt(k_end, d_st)
                d_k = d_kend * e_end
                d_b = -t_kend
                if latent:
                    qpre = q_ref[rows, :]
                    q = _silu(qpre) * q_scale
                    bm = b[mid:mid + 1, :]
                    e_b, e_qm, e_km = jnp.exp(b), jnp.exp(b - bm), jnp.exp(bm - b)
                    q_t, q_m, k_m = q * e_b, q * e_qm, k * e_km
                    a = jnp.where(keep, _mm_nt(q_m, k_m), 0.0)
                    d_out = do_ref[pl.ds(pl.multiple_of(r0 - CTX_LEN, CHUNK), CHUNK), :]
                    d_a = jnp.where(keep, _mm_nt(d_out, v), 0.0)
                    d_qm = _mm(d_a, k_m)
                    d_km = _mm_tn(d_a, q_m)
                    d_qt = _mm(d_out, st_prev)
                    d_v = d_v + _mm_tn(a, d_out)
                    d_k = d_k + d_km * e_km
                    d_b = d_b + d_qt * q_t + d_qm * q_m - d_km * k_m
                    d_q = d_qt * e_b + d_qm * e_qm
                at_last = jnp.sum(t_kend, axis=0, keepdims=True) + d_decay * decay
                d_b = d_b + jnp.where(is_row == last, at_last, 0.0)
                d_lf = _mm_exact_lhs(tri_t, d_b)
                d_f = d_lf / f - d_k
                dg_ref[d, rows, :] = d_f * (1.0 - lb) * sg * (1.0 - sg)
                if rev:
                    dg_ref[2, rows, :] += d_v
                else:
                    dg_ref[2, rows, :] = d_v
                if latent:
                    d_qpre = d_q * q_scale * _dsilu(qpre)
                    if rev:
                        dg_ref[3, rows, :] += d_qpre
                    else:
                        dg_ref[3, rows, :] = d_qpre
                return dlb + jnp.sum(d_f * (1.0 - sg), axis=0, keepdims=True)

            dlb = lax.fori_loop(0, N_LAT_CHUNKS, functools.partial(grad_step, latent=True), jnp.zeros((1, HEAD_DIM), F32),
                                unroll=2)
            dlb = lax.fori_loop(0, N_CTX_CHUNKS, functools.partial(grad_step, latent=False), dlb, unroll=2)
            dlg_ref[d:d + 1, :] = dlb * lb * (1.0 - lb)

    col = pl.BlockSpec((SEQ, HEAD_DIM), lambda h: (0, h))
    return pl.pallas_call(
        body, name="gla_backward", grid=(HEADS,),
        out_shape=[jax.ShapeDtypeStruct((HGRN_SECTIONS, ROWS_ALL, D_MODEL), F32), jax.ShapeDtypeStruct((2, D_MODEL), F32)],
        in_specs=[G_SPEC(0), G_SPEC(1), G_SPEC(2), G_SPEC(3), pl.BlockSpec((2, HEAD_DIM), lambda h: (0, h)), col, col],
        out_specs=[pl.BlockSpec((HGRN_SECTIONS, ROWS_ALL, HEAD_DIM), lambda h: (0, 0, h)),
                   pl.BlockSpec((2, HEAD_DIM), lambda h: (0, h))],
        scratch_shapes=[pltpu.VMEM((N_CHUNKS, HEAD_DIM, HEAD_DIM), F32), pltpu.VMEM((N_CHUNKS, HEAD_DIM, HEAD_DIM), F32),
                        pltpu.VMEM((N_CHUNKS, 1, HEAD_DIM), F32)],
        compiler_params=_params(48),
    )(g5, g5, g5, g5, lb_logits, d_o, d_z)


GROUP = 2 * CHUNK


def _group_masks(rev):
    rid = lax.broadcasted_iota(jnp.int32, (GROUP, GROUP), 0)
    cid = lax.broadcasted_iota(jnp.int32, (GROUP, GROUP), 1)
    same = (rid >= CHUNK) == (cid >= CHUNK)
    causal = (cid >= rid) if rev else (cid <= rid)
    anti = (cid <= rid) if rev else (cid >= rid)
    bf = lambda m: jnp.where(m, jnp.ones((GROUP, GROUP), F32), jnp.zeros((GROUP, GROUP), F32)).astype(BF16)
    keep = same & causal
    return keep, bf(keep), bf(same & anti), bf(same & jnp.logical_not(anti))


def _group_sum(m_bf, a):
    hi, lo = _split2(a)
    r = jnp.dot(m_bf, jnp.concatenate([hi, lo], axis=1), preferred_element_type=F32)
    return r[:, :HEAD_DIM] + r[:, HEAD_DIM:]


def _chunk_row(a, pos):
    return jnp.concatenate([jnp.broadcast_to(a[c * CHUNK + pos:c * CHUNK + pos + 1, :], (CHUNK, HEAD_DIM)) for c in range(2)], axis=0)


def _by_chunk(a, second):
    return jnp.concatenate([jnp.where(second, 0.0, a), jnp.where(second, a, 0.0)], axis=1)


def _own_block(r):
    return jnp.concatenate([r[0:CHUNK, 0:HEAD_DIM], r[CHUNK:GROUP, HEAD_DIM:2 * HEAD_DIM]], axis=0)


def _scan_step_of(row_chunk, rev, latent):
    if not rev:
        return row_chunk
    return (N_CHUNKS + N_CTX_CHUNKS - 1 - row_chunk) if latent else (N_CTX_CHUNKS - 1 - row_chunk)


def _group_rows(i, j, per_step, latent):
    base = CTX_LEN if latent else 0
    return pl.multiple_of(base + (i * per_step + j) * GROUP, GROUP)


GROUPS_PER_STEP = 4


def _gla_forward(g5, lb_logits, late_shards):
    q_scale = HEAD_DIM ** -0.5
    per_lat, per_ctx = GROUPS_PER_STEP, min(GROUPS_PER_STEP, CTX_LEN // GROUP)
    n_late = len(GATHER_LATE)

    def body(ff_ref, fb_ref, v_ref, q_ref, lg_ref, *rest):
        shard_refs, o_ref, full_refs = rest[:n_late], rest[n_late], rest[n_late + 1:2 * n_late + 1]
        st_ref, decay_ref, qt_ref = rest[2 * n_late + 1:2 * n_late + 4]
        start_gather, wait_gather = _gather_direct(GATHER_LATE, shard_refs, full_refs, *rest[2 * n_late + 4:])

        @pl.when(pl.program_id(0) == 0)
        def _():
            start_gather()

        second = lax.broadcasted_iota(jnp.int32, (GROUP, HEAD_DIM), 0) >= CHUNK
        for rev in (False, True):
            f_ref = fb_ref if rev else ff_ref
            lb = _sig(lg_ref[1:2, :] if rev else lg_ref[0:1, :])
            keep, tri, _, _ = _group_masks(rev)
            last = 0 if rev else CHUNK - 1
            mid = CHUNK // 2 if rev else CHUNK // 2 - 1

            def local_step(i, carry, latent, per):
                r0s = [_group_rows(i, j, per, latent) for j in range(per)]
                rows = [pl.ds(r0, GROUP) for r0 in r0s]
                gates = [_gates(f_ref[r, :], lb) for r in rows]
                vs = [v_ref[r, :] for r in rows]
                bs = [_group_sum(tri, g[3]) for g in gates]
                bls = [_chunk_row(b, last) for b in bs]
                ups = [_mm_tn(v, _by_chunk(g[2] * jnp.exp(bl - b), second)) for v, g, b, bl in zip(vs, gates, bs, bls)]
                if latent:
                    qs = [_silu(q_ref[r, :]) * q_scale for r in rows]
                    bms = [_chunk_row(b, mid) for b in bs]
                    a_s = [_mm_nt(q * jnp.exp(b - bm), g[2] * jnp.exp(bm - b)) for q, g, b, bm in zip(qs, gates, bs, bms)]
                    outs = [_mm(jnp.where(keep, a, 0.0), v) for a, v in zip(a_s, vs)]
                for j in range(per):
                    for c in range(2):
                        step = _scan_step_of(r0s[j] // CHUNK + c, rev, latent)
                        decay_ref[step] = jnp.exp(bls[j][c * CHUNK:c * CHUNK + 1, :])
                        st_ref[step] = ups[j][:, c * HEAD_DIM:(c + 1) * HEAD_DIM]
                    if latent:
                        orow = pl.ds(pl.multiple_of(r0s[j] - CTX_LEN, GROUP), GROUP)
                        qt_ref[orow, :] = (qs[j] * jnp.exp(bs[j])).astype(BF16)
                        if rev:
                            o_ref[orow, :] += outs[j]
                        else:
                            o_ref[orow, :] = outs[j]
                return carry

            lax.fori_loop(0, CTX_LEN // (per_ctx * GROUP), functools.partial(local_step, latent=False, per=per_ctx), 0)
            lax.fori_loop(0, SEQ // (per_lat * GROUP), functools.partial(local_step, latent=True, per=per_lat), 0)

            def scan_step(t, st):
                update = st_ref[t]
                st_ref[t] = st
                return st * decay_ref[t] + update

            lax.fori_loop(0, N_CHUNKS, scan_step, jnp.zeros((HEAD_DIM, HEAD_DIM), F32), unroll=2)

            def inter_step(i, carry):
                r0s = [_group_rows(i, j, per_lat, True) for j in range(per_lat)]
                orows = [pl.ds(pl.multiple_of(r0 - CTX_LEN, GROUP), GROUP) for r0 in r0s]
                states = [jnp.concatenate([st_ref[_scan_step_of(r0 // CHUNK + c, rev, True)] for c in range(2)], axis=0)
                          for r0 in r0s]
                prods = [lax.dot_general(qt_ref[orow, :], s.astype(BF16), (((1,), (1,)), ((), ())), preferred_element_type=F32)
                         for orow, s in zip(orows, states)]
                for orow, r in zip(orows, prods):
                    o_ref[orow, :] += _own_block(r)
                return carry

            lax.fori_loop(0, SEQ // (per_lat * GROUP), inter_step, 0)

        @pl.when(pl.program_id(0) == HEADS - 1)
        def _():
            wait_gather()

    outs = pl.pallas_call(
        body, name="gla_forward", grid=(HEADS,),
        out_shape=[jax.ShapeDtypeStruct((SEQ, D_MODEL), F32)] + [jax.ShapeDtypeStruct(W_SPECS[k][0], BF16) for k in GATHER_LATE],
        in_specs=[G_SPEC(0), G_SPEC(1), G_SPEC(2), G_SPEC(3), pl.BlockSpec((2, HEAD_DIM), lambda h: (0, h))] + [ANY] * n_late,
        out_specs=[pl.BlockSpec((SEQ, HEAD_DIM), lambda h: (0, h))] + [ANY] * n_late,
        scratch_shapes=[pltpu.VMEM((N_CHUNKS, HEAD_DIM, HEAD_DIM), F32), pltpu.VMEM((N_CHUNKS, 1, HEAD_DIM), F32),
                        pltpu.VMEM((SEQ, HEAD_DIM), BF16)] + _comm_sems(n_late),
        compiler_params=_params(32),
    )(g5, g5, g5, g5, lb_logits, *late_shards)
    return outs[0], dict(zip(GATHER_LATE, outs[1:]))


def _gla_backward(g5, lb_logits, d_o, d_z, late_parts):
    q_scale = HEAD_DIM ** -0.5
    per_lat, per_ctx = GROUPS_PER_STEP, min(GROUPS_PER_STEP, CTX_LEN // GROUP)
    n_late = len(GATHER_LATE)

    def body(ff_ref, fb_ref, v_ref, q_ref, lg_ref, do_ref, dz_ref, *rest):
        part_refs, (dg_ref, dlg_ref), slot_refs = rest[:n_late], rest[n_late:n_late + 2], rest[n_late + 2:2 * n_late + 2]
        st_ref, dst_ref, decay_ref = rest[2 * n_late + 2:2 * n_late + 5]
        start_scatter, wait_scatter = _scatter_direct(GATHER_LATE, part_refs, slot_refs, *rest[2 * n_late + 5:])

        @pl.when(pl.program_id(0) == 0)
        def _():
            start_scatter()

        dg_ref[3, 0:CTX_LEN, :] = jnp.zeros((CTX_LEN, HEAD_DIM), F32)
        dg_ref[4, 0:CTX_LEN, :] = jnp.zeros((CTX_LEN, HEAD_DIM), F32)
        dg_ref[4, CTX_LEN:ROWS_ALL, :] = dz_ref[...]
        second = lax.broadcasted_iota(jnp.int32, (GROUP, HEAD_DIM), 0) >= CHUNK
        for rev in (False, True):
            d = 1 if rev else 0
            f_ref = fb_ref if rev else ff_ref
            lb = _sig(lg_ref[d:d + 1, :])
            keep, tri, tri_t, strict = _group_masks(rev)
            last = 0 if rev else CHUNK - 1
            mid = CHUNK // 2 if rev else CHUNK // 2 - 1

            def local_step(i, carry, latent, per):
                r0s = [_group_rows(i, j, per, latent) for j in range(per)]
                rows = [pl.ds(r0, GROUP) for r0 in r0s]
                gates = [_gates(f_ref[r, :], lb) for r in rows]
                bs = [_group_sum(tri, g[3]) for g in gates]
                bls = [_chunk_row(b, last) for b in bs]
                ups = [_mm_tn(v_ref[r, :], _by_chunk(g[2] * jnp.exp(bl - b), second)) for r, g, b, bl in zip(rows, gates, bs, bls)]
                if latent:
                    orows = [pl.ds(pl.multiple_of(r0 - CTX_LEN, GROUP), GROUP) for r0 in r0s]
                    d_ups = [_mm_tn(do_ref[orow, :], _by_chunk(_silu(q_ref[r, :]) * q_scale * jnp.exp(b), second))
                             for orow, r, b in zip(orows, rows, bs)]
                for j in range(per):
                    for c in range(2):
                        step = _scan_step_of(r0s[j] // CHUNK + c, rev, latent)
                        decay_ref[step] = jnp.exp(bls[j][c * CHUNK:c * CHUNK + 1, :])
                        st_ref[step] = ups[j][:, c * HEAD_DIM:(c + 1) * HEAD_DIM]
                        if latent:
                            dst_ref[step] = d_ups[j][:, c * HEAD_DIM:(c + 1) * HEAD_DIM]
                        else:
                            dst_ref[step] = jnp.zeros((HEAD_DIM, HEAD_DIM), F32)
                return carry

            lax.fori_loop(0, CTX_LEN // (per_ctx * GROUP), functools.partial(local_step, latent=False, per=per_ctx), 0)
            lax.fori_loop(0, SEQ // (per_lat * GROUP), functools.partial(local_step, latent=True, per=per_lat), 0)

            def scan_step(i, carry):
                st, d_st = carry
                j = N_CHUNKS - 1 - i
                update, d_update = st_ref[i], dst_ref[j]
                st_ref[i] = st
                dst_ref[j] = d_st
                return st * decay_ref[i] + update, d_st * decay_ref[j] + d_update

            zero_state = jnp.zeros((HEAD_DIM, HEAD_DIM), F32)
            lax.fori_loop(0, N_CHUNKS, scan_step, (zero_state, zero_state))

            def grad_step(i, dlb, latent, per):
                r0s = [_group_rows(i, j, per, latent) for j in range(per)]
                rows = [pl.ds(r0, GROUP) for r0 in r0s]
                gates = [_gates(f_ref[r, :], lb) for r in rows]
                vs = [v_ref[r, :] for r in rows]
                bs = [_group_sum(tri, g[3]) for g in gates]
                bls = [_chunk_row(b, last) for b in bs]
                e_ends = [jnp.exp(bl - b) for b, bl in zip(bs, bls)]
                k_ends = [g[2] * e for g, e in zip(gates, e_ends)]
                sts = [[st_ref[_scan_step_of(r0 // CHUNK + c, rev, latent)] for c in range(2)] for r0 in r0s]
                d_sts = [[dst_ref[_scan_step_of(r0 // CHUNK + c, rev, latent)] for c in range(2)] for r0 in r0s]
                d_kends = [_own_block(_mm(v, jnp.concatenate(ds, axis=1))) for v, ds in zip(vs, d_sts)]
                d_vs = [_own_block(_mm_nt(ke, jnp.concatenate(ds, axis=0))) for ke, ds in zip(k_ends, d_sts)]
                at_last = [jnp.concatenate([jnp.broadcast_to(jnp.sum(ds[c] * s[c], axis=0, keepdims=True), (CHUNK, HEAD_DIM))
                                            for c in range(2)], axis=0) * jnp.exp(bl) for ds, s, bl in zip(d_sts, sts, bls)]
                t_kends = [dk * ke for dk, ke in zip(d_kends, k_ends)]
                d_ks = [dk * e for dk, e in zip(d_kends, e_ends)]
                d_lfs = [_group_sum(strict, t) + al for t, al in zip(t_kends, at_last)]
                if latent:
                    orows = [pl.ds(pl.multiple_of(r0 - CTX_LEN, GROUP), GROUP) for r0 in r0s]
                    qpres = [q_ref[r, :] for r in rows]
                    qs = [_silu(qp) * q_scale for qp in qpres]
                    bms = [_chunk_row(b, mid) for b in bs]
                    e_bs = [jnp.exp(b) for b in bs]
                    e_qms = [jnp.exp(b - bm) for b, bm in zip(bs, bms)]
                    e_kms = [jnp.exp(bm - b) for b, bm in zip(bs, bms)]
                    q_ts = [q * e for q, e in zip(qs, e_bs)]
                    q_ms = [q * e for q, e in zip(qs, e_qms)]
                    k_ms = [g[2] * e for g, e in zip(gates, e_kms)]
                    d_outs = [do_ref[orow, :] for orow in orows]
                    a_s = [jnp.where(keep, _mm_nt(qm, km), 0.0) for qm, km in zip(q_ms, k_ms)]
                    d_as = [jnp.where(keep, _mm_nt(do, v), 0.0) for do, v in zip(d_outs, vs)]
                    d_qts = [_own_block(_mm(do, jnp.concatenate(s, axis=1))) for do, s in zip(d_outs, sts)]
                    d_qms = [_mm(da, km) for da, km in zip(d_as, k_ms)]
                    d_kms = [_mm_tn(da, qm) for da, qm in zip(d_as, q_ms)]
                    d_vs = [dv + _mm_tn(a, do) for dv, a, do in zip(d_vs, a_s, d_outs)]
                    d_ks = [dk + dkm * e for dk, dkm, e in zip(d_ks, d_kms, e_kms)]
                    d_lfs = [dl + _group_sum(tri_t, dqt * qt + dqm * qm - dkm * km)
                             for dl, dqt, qt, dqm, qm, dkm, km in zip(d_lfs, d_qts, q_ts, d_qms, q_ms, d_kms, k_ms)]
                    d_qs = [dqt * eb + dqm * eq for dqt, eb, dqm, eq in zip(d_qts, e_bs, d_qms, e_qms)]
                for j in range(per):
                    sg, f = gates[j][0], gates[j][1]
                    d_f = d_lfs[j] / f - d_ks[j]
                    dg_ref[d, rows[j], :] = d_f * (1.0 - lb) * sg * (1.0 - sg)
                    dlb = dlb + jnp.sum(d_f * (1.0 - sg), axis=0, keepdims=True)
                    if rev:
                        dg_ref[2, rows[j], :] += d_vs[j]
                    else:
                        dg_ref[2, rows[j], :] = d_vs[j]
                    if latent:
                        d_qpre = d_qs[j] * q_scale * _dsilu(qpres[j])
                        if rev:
                            dg_ref[3, rows[j], :] += d_qpre
                        else:
                            dg_ref[3, rows[j], :] = d_qpre
                return dlb

            dlb = lax.fori_loop(0, SEQ // (per_lat * GROUP), functools.partial(grad_step, latent=True, per=per_lat),
                                jnp.zeros((1, HEAD_DIM), F32))
            dlb = lax.fori_loop(0, CTX_LEN // (per_ctx * GROUP), functools.partial(grad_step, latent=False, per=per_ctx), dlb)
            dlg_ref[d:d + 1, :] = dlb * lb * (1.0 - lb)

        @pl.when(pl.program_id(0) == HEADS - 1)
        def _():
            wait_scatter()

    col = pl.BlockSpec((SEQ, HEAD_DIM), lambda h: (0, h))
    outs = pl.pallas_call(
        body, name="gla_backward", grid=(HEADS,),
        out_shape=[jax.ShapeDtypeStruct((HGRN_SECTIONS, ROWS_ALL, D_MODEL), F32), jax.ShapeDtypeStruct((2, D_MODEL), F32)]
        + _slot_shapes(GATHER_LATE),
        in_specs=[G_SPEC(0), G_SPEC(1), G_SPEC(2), G_SPEC(3), pl.BlockSpec((2, HEAD_DIM), lambda h: (0, h)), col, col]
        + [ANY] * n_late,
        out_specs=[pl.BlockSpec((HGRN_SECTIONS, ROWS_ALL, HEAD_DIM), lambda h: (0, 0, h)),
                   pl.BlockSpec((2, HEAD_DIM), lambda h: (0, h))] + [ANY] * n_late,
        scratch_shapes=[pltpu.VMEM((N_CHUNKS, HEAD_DIM, HEAD_DIM), F32), pltpu.VMEM((N_CHUNKS, HEAD_DIM, HEAD_DIM), F32),
                        pltpu.VMEM((N_CHUNKS, 1, HEAD_DIM), F32)] + _comm_sems(n_late),
        compiler_params=_params(48),
    )(g5, g5, g5, g5, lb_logits, d_o, d_z, *late_parts)
    return outs[0], outs[1], dict(zip(GATHER_LATE, outs[2:]))


def _head_norm(o, gw, scr):
    rs = []
    for h in range(HEADS):
        cols = slice(h * HEAD_DIM, (h + 1) * HEAD_DIM)
        oh = o[:, cols]
        r = lax.rsqrt(jnp.mean(oh * oh, axis=-1, keepdims=True) + EPS)
        scr[:, cols] = oh * r
        rs.append(r)
    return rs


def _hgrn_out_forward(o_raw, g5, xin, gnorm_w, gate, w_out):
    tm = ROW_TILE

    def body(o_ref, z_ref, x_ref, gw_ref, gate_ref, w_ref, x1_ref, res_ref, scr):
        _head_norm(o_ref[...], None, scr)
        a = scr[...] * gw_ref[...] * _silu(z_ref[...])
        res = _mm(a, w_ref[...])
        res_ref[...] = res
        x1_ref[...] = x_ref[...] + gate_ref[...] * res

    tile = pl.BlockSpec((tm, D_MODEL), lambda i: (i, 0))
    vec = pl.BlockSpec((1, D_MODEL), lambda i: (0, 0))
    return pl.pallas_call(
        body, name="hgrn_out_forward", grid=(SEQ // tm,),
        out_shape=[jax.ShapeDtypeStruct((SEQ, D_MODEL), F32)] * 2,
        in_specs=[tile, pl.BlockSpec((None, tm, D_MODEL), lambda i: (4, i + CTX_LEN // tm, 0)), tile, vec, vec,
                  pl.BlockSpec((D_MODEL, D_MODEL), lambda i: (0, 0))],
        out_specs=[tile, tile],
        scratch_shapes=[pltpu.VMEM((tm, D_MODEL), F32)],
        compiler_params=_params(32),
    )(o_raw, g5, xin, gnorm_w, gate, w_out)


def _hgrn_out_backward(d_x1, o_raw, g5, res, gnorm_w, gate, w_out):
    tm = ROW_TILE

    def body(dx_ref, o_ref, z_ref, res_ref, gw_ref, gate_ref, w_ref, do_ref, dz_ref, dw_out, dgate_ref, dgw_ref, scr, scr2,
             dw_ref):
        @pl.when(pl.program_id(0) == 0)
        def _():
            dw_ref[...] = jnp.zeros_like(dw_ref)
            dgate_ref[...] = jnp.zeros_like(dgate_ref)
            dgw_ref[...] = jnp.zeros_like(dgw_ref)

        dx = dx_ref[...]
        dgate_ref[...] += jnp.sum(dx * res_ref[...], axis=0, keepdims=True)
        d_res = (dx * gate_ref[...]).astype(BF16)
        d_a = _mm_nt(d_res, w_ref[...])
        rs = _head_norm(o_ref[...], None, scr)
        z = z_ref[...]
        sz = _silu(z)
        o_hat = scr[...]
        o_n = o_hat * gw_ref[...]
        dw_ref[...] += _mm_tn(o_n * sz, d_res)
        d_on = d_a * sz
        dz_ref[...] = d_a * o_n * _dsilu(z)
        dgw_ref[...] += jnp.sum(d_on * o_hat, axis=0, keepdims=True)
        scr2[...] = d_on * gw_ref[...]
        for h in range(HEADS):
            cols = slice(h * HEAD_DIM, (h + 1) * HEAD_DIM)
            dh, oh = scr2[:, cols], scr[:, cols]
            do_ref[:, cols] = rs[h] * (dh - oh * jnp.mean(dh * oh, axis=-1, keepdims=True))

        @pl.when(pl.program_id(0) == SEQ // tm - 1)
        def _():
            dw_out[...] = dw_ref[...].astype(BF16)

    tile = pl.BlockSpec((tm, D_MODEL), lambda i: (i, 0))
    vec = pl.BlockSpec((1, D_MODEL), lambda i: (0, 0))
    mat = pl.BlockSpec((D_MODEL, D_MODEL), lambda i: (0, 0))
    return pl.pallas_call(
        body, name="hgrn_out_backward", grid=(SEQ // tm,),
        out_shape=[jax.ShapeDtypeStruct((SEQ, D_MODEL), F32)] * 2 + [jax.ShapeDtypeStruct((D_MODEL, D_MODEL), BF16)]
        + [jax.ShapeDtypeStruct((1, D_MODEL), F32)] * 2,
        in_specs=[tile, tile, pl.BlockSpec((None, tm, D_MODEL), lambda i: (4, i + CTX_LEN // tm, 0)), tile, vec, vec, mat],
        out_specs=[tile, tile, mat, vec, vec],
        scratch_shapes=[pltpu.VMEM((tm, D_MODEL), F32)] * 2 + [pltpu.VMEM((D_MODEL, D_MODEL), F32)],
        compiler_params=_params(40),
    )(d_x1, o_raw, g5, res, gnorm_w, gate, w_out)


def _pool_constants():
    win = np.zeros((POOL_GROUPS, ROW_TILE, ROW_TILE), np.float32)
    inv = np.zeros((POOL_GROUPS, ROW_TILE, 1), np.float32)
    for g, w in enumerate(POOL_WINDOWS):
        for t in range(ROW_TILE):
            base, p = (t // GRID_W) * GRID_W, t % GRID_W
            lo = min(max(p - w // 2, 0), GRID_W)
            hi = min(max(p - w // 2 + w, 0), GRID_W)
            win[g, t, base + lo:base + hi] = 1.0
            inv[g, t, 0] = 1.0 / np.float32(hi - lo)
    return jnp.asarray(win, BF16), jnp.asarray(win.transpose(0, 2, 1), BF16), jnp.asarray(inv, F32)


def _pool_mix(u_ref, wg_ref, ps_ref, win_ref, inv_ref, pooled_scr, yg_scr):
    for g in range(POOL_GROUPS):
        cols = slice(g * POOL_GROUP_DIM, (g + 1) * POOL_GROUP_DIM)
        ug = u_ref[:, cols]
        pooled = _mm_exact_lhs(win_ref[g], ug) * inv_ref[g] - ug
        if pooled_scr is not None:
            pooled_scr[:, cols] = pooled
        yg_scr[:, cols] = _mm(pooled, wg_ref[g])


def _pool_forward_loss(uz, x1, target, gate, w_grp, pool_scale, w_out, final_w):
    tm = ROW_TILE
    win, _, inv = _pool_constants()

    def body(u_ref, z_ref, x_ref, t_ref, gate_ref, wg_ref, ps_ref, w_ref, fw_ref, win_ref, inv_ref,
             dx_ref, loss_ref, dfw_ref, dgate_ref, yg_scr):
        @pl.when(pl.program_id(0) == 0)
        def _():
            loss_ref[...] = jnp.zeros_like(loss_ref)
            dfw_ref[...] = jnp.zeros_like(dfw_ref)
            dgate_ref[...] = jnp.zeros_like(dgate_ref)

        _pool_mix(u_ref, wg_ref, ps_ref, win_ref, inv_ref, None, yg_scr)
        a = yg_scr[...] * ps_ref[...] * _silu(z_ref[...])
        res = _mm(a, w_ref[...])
        x2 = x_ref[...] + gate_ref[...] * res
        r = lax.rsqrt(jnp.mean(x2 * x2, axis=-1, keepdims=True) + EPS)
        xh = x2 * r
        fw = fw_ref[...]
        err = xh * fw - t_ref[...]
        loss_ref[...] += 0.5 * jnp.sum(jnp.mean(err * err, axis=-1, keepdims=True))
        d_y = err * (1.0 / D_MODEL)
        dfw_ref[...] += jnp.sum(d_y * xh, axis=0, keepdims=True)
        d_xh = d_y * fw
        d_x2 = r * (d_xh - xh * jnp.mean(d_xh * xh, axis=-1, keepdims=True))
        dx_ref[...] = d_x2
        dgate_ref[...] += jnp.sum(d_x2 * res, axis=0, keepdims=True)

    tile = pl.BlockSpec((tm, D_MODEL), lambda i: (i, 0))
    vec = pl.BlockSpec((1, D_MODEL), lambda i: (0, 0))
    grp = pl.BlockSpec((POOL_GROUPS, POOL_GROUP_DIM, POOL_GROUP_DIM), lambda i: (0, 0, 0))
    return pl.pallas_call(
        body, name="pool_forward_loss", grid=(SEQ // tm,),
        out_shape=[jax.ShapeDtypeStruct((SEQ, D_MODEL), F32), jax.ShapeDtypeStruct((8, 128), F32),
                   jax.ShapeDtypeStruct((1, D_MODEL), F32), jax.ShapeDtypeStruct((1, D_MODEL), F32)],
        in_specs=[pl.BlockSpec((None, tm, D_MODEL), lambda i: (0, i, 0)), pl.BlockSpec((None, tm, D_MODEL), lambda i: (1, i, 0)),
                  tile, tile, vec, grp, vec, pl.BlockSpec((D_MODEL, D_MODEL), lambda i: (0, 0)), vec, grp,
                  pl.BlockSpec((POOL_GROUPS, ROW_TILE, 1), lambda i: (0, 0, 0))],
        out_specs=[tile, pl.BlockSpec((8, 128), lambda i: (0, 0)), vec, vec],
        scratch_shapes=[pltpu.VMEM((tm, D_MODEL), F32)],
        compiler_params=_params(32),
    )(uz, uz, x1, target, gate, w_grp, pool_scale, w_out, final_w, win, inv)


def _pool_backward(d_x2, uz, gate, w_grp, pool_scale, w_out):
    tm = ROW_TILE
    win, win_t, inv = _pool_constants()

    def body(dx_ref, u_ref, z_ref, gate_ref, wg_ref, ps_ref, w_ref, win_ref, wint_ref, inv_ref,
             duz_ref, dw_out, dwg_out, dps_ref, pooled_scr, yg_scr, dyg_scr, dw_ref, dwg_ref):
        @pl.when(pl.program_id(0) == 0)
        def _():
            dw_ref[...] = jnp.zeros_like(dw_ref)
            dwg_ref[...] = jnp.zeros_like(dwg_ref)
            dps_ref[...] = jnp.zeros_like(dps_ref)

        _pool_mix(u_ref, wg_ref, ps_ref, win_ref, inv_ref, pooled_scr, yg_scr)
        z = z_ref[...]
        sz = _silu(z)
        yg = yg_scr[...]
        y = yg * ps_ref[...]
        d_res = (dx_ref[...] * gate_ref[...]).astype(BF16)
        d_a = _mm_nt(d_res, w_ref[...])
        dw_ref[...] += _mm_tn(y * sz, d_res)
        d_y = d_a * sz
        duz_ref[1] = d_a * y * _dsilu(z)
        dps_ref[...] += jnp.sum(d_y * yg, axis=0, keepdims=True)
        dyg_scr[...] = d_y * ps_ref[...]
        for g in range(POOL_GROUPS):
            cols = slice(g * POOL_GROUP_DIM, (g + 1) * POOL_GROUP_DIM)
            d_yg = dyg_scr[:, cols].astype(BF16)
            d_pool = _mm_nt(d_yg, wg_ref[g])
            dwg_ref[g] += _mm_tn(pooled_scr[:, cols], d_yg)
            duz_ref[0, :, cols] = _mm_exact_lhs(wint_ref[g], d_pool * inv_ref[g]) - d_pool

        @pl.when(pl.program_id(0) == SEQ // tm - 1)
        def _():
            dw_out[...] = dw_ref[...].astype(BF16)
            dwg_out[...] = dwg_ref[...].astype(BF16)

    tile = pl.BlockSpec((tm, D_MODEL), lambda i: (i, 0))
    vec = pl.BlockSpec((1, D_MODEL), lambda i: (0, 0))
    mat = pl.BlockSpec((D_MODEL, D_MODEL), lambda i: (0, 0))
    grp = pl.BlockSpec((POOL_GROUPS, POOL_GROUP_DIM, POOL_GROUP_DIM), lambda i: (0, 0, 0))
    return pl.pallas_call(
        body, name="pool_backward", grid=(SEQ // tm,),
        out_shape=[jax.ShapeDtypeStruct((POOL_SECTIONS, SEQ, D_MODEL), F32), jax.ShapeDtypeStruct((D_MODEL, D_MODEL), BF16),
                   jax.ShapeDtypeStruct((POOL_GROUPS, POOL_GROUP_DIM, POOL_GROUP_DIM), BF16), jax.ShapeDtypeStruct((1, D_MODEL), F32)],
        in_specs=[tile, pl.BlockSpec((None, tm, D_MODEL), lambda i: (0, i, 0)), pl.BlockSpec((None, tm, D_MODEL), lambda i: (1, i, 0)),
                  vec, grp, vec, mat, grp, grp, pl.BlockSpec((POOL_GROUPS, ROW_TILE, 1), lambda i: (0, 0, 0))],
        out_specs=[pl.BlockSpec((POOL_SECTIONS, tm, D_MODEL), lambda i: (0, i, 0)), mat, grp, vec],
        scratch_shapes=[pltpu.VMEM((tm, D_MODEL), F32)] * 3 + [pltpu.VMEM((D_MODEL, D_MODEL), F32),
                                                               pltpu.VMEM((POOL_GROUPS, POOL_GROUP_DIM, POOL_GROUP_DIM), F32)],
        compiler_params=_params(40),
    )(d_x2, uz, uz, gate, w_grp, pool_scale, w_out, win, win_t, inv)


def _ln_mod_backward(d_g, w, xin, nw, scale, d_up, name, scatter_names=(), scatter_parts=()):
    n_sec, rows, _ = d_g.shape
    n_mod = scale.shape[0]
    tm = ROW_TILE
    n_tiles = rows // tm
    skip = n_mod - 1
    n_sc = len(scatter_names)

    def body(dg_ref, w_ref, x_ref, nw_ref, sc_ref, up_ref, *rest):
        part_refs, (dx_ref, dnw_ref, dmod_ref), slot_refs = rest[:n_sc], rest[n_sc:n_sc + 3], rest[n_sc + 3:2 * n_sc + 3]
        i = pl.program_id(0)
        if n_sc:
            start_scatter, wait_scatter = _scatter_direct(scatter_names, part_refs, slot_refs, *rest[2 * n_sc + 3:])

            @pl.when(i == 0)
            def _():
                start_scatter()

        @pl.when(i == 0)
        def _():
            dnw_ref[...] = jnp.zeros_like(dnw_ref)

        @pl.when((i == 0) | (i == skip))
        def _():
            dmod_ref[...] = jnp.zeros_like(dmod_ref)

        d_h = _mm_nt(dg_ref[0], w_ref[:, 0:D_MODEL])
        for k in range(1, n_sec):
            d_h = d_h + _mm_nt(dg_ref[k], w_ref[:, k * D_MODEL:(k + 1) * D_MODEL])
        xv = x_ref[...]
        r = lax.rsqrt(jnp.mean(xv * xv, axis=-1, keepdims=True) + EPS)
        xh = xv * r
        nw_row = nw_ref[...]
        dmod_ref[0:1, :] += jnp.sum(d_h, axis=0, keepdims=True)
        dmod_ref[1:2, :] += jnp.sum(d_h * (xh * nw_row), axis=0, keepdims=True)
        d_xn = d_h * (1.0 + sc_ref[...])
        dnw_ref[...] += jnp.sum(d_xn * xh, axis=0, keepdims=True)
        d_xh = d_xn * nw_row

        @pl.when(i >= skip)
        def _():
            dx_ref[...] = up_ref[...] + r * (d_xh - xh * jnp.mean(d_xh * xh, axis=-1, keepdims=True))

        if n_sc:
            @pl.when(i == n_tiles - 1)
            def _():
                wait_scatter()

    lat = lambda i: (jnp.maximum(i - skip, 0), 0)
    mod_idx = lambda i: (jnp.minimum(i, n_mod - 1), 0, 0)
    outs = pl.pallas_call(
        body, name=name, grid=(n_tiles,),
        out_shape=[jax.ShapeDtypeStruct((rows - skip * tm, D_MODEL), F32), jax.ShapeDtypeStruct((1, D_MODEL), F32),
                   jax.ShapeDtypeStruct((n_mod, 8, D_MODEL), F32)] + _slot_shapes(scatter_names),
        in_specs=[pl.BlockSpec((n_sec, tm, D_MODEL), lambda i: (0, i, 0)),
                  pl.BlockSpec((D_MODEL, n_sec * D_MODEL), lambda i: (0, 0)),
                  pl.BlockSpec((tm, D_MODEL), lambda i: (i, 0)),
                  pl.BlockSpec((1, D_MODEL), lambda i: (0, 0)),
                  pl.BlockSpec((None, 1, D_MODEL), mod_idx),
                  pl.BlockSpec((tm, D_MODEL), lat)] + [ANY] * n_sc,
        out_specs=[pl.BlockSpec((tm, D_MODEL), lat), pl.BlockSpec((1, D_MODEL), lambda i: (0, 0)),
                   pl.BlockSpec((None, 8, D_MODEL), mod_idx)] + [ANY] * n_sc,
        scratch_shapes=_comm_sems(n_sc) if n_sc else [],
        compiler_params=_params(48),
    )(d_g, w, xin, nw, scale.reshape(n_mod, 1, D_MODEL), d_up, *scatter_parts)
    return outs[0], outs[1], outs[2], list(outs[3:])


def _weight_grad(h, d_g, name):
    n_sec, rows, _ = d_g.shape
    tm = 768 if rows % 768 == 0 else 512
    n_tiles = rows // tm

    def body(h_ref, dg_ref, dw_ref, acc):
        i = pl.program_id(1)
        prod = _mm_tn(h_ref[...], dg_ref[...])

        @pl.when(i == 0)
        def _():
            acc[...] = prod

        @pl.when((i > 0) & (i < n_tiles - 1))
        def _():
            acc[...] += prod

        @pl.when(i == n_tiles - 1)
        def _():
            dw_ref[...] = (acc[...] + prod).astype(BF16)

    return pl.pallas_call(
        body, name=name, grid=(n_sec, n_tiles),
        out_shape=jax.ShapeDtypeStruct((D_MODEL, n_sec * D_MODEL), BF16),
        in_specs=[pl.BlockSpec((tm, D_MODEL), lambda j, i: (i, 0)), pl.BlockSpec((None, tm, D_MODEL), lambda j, i: (j, i, 0))],
        out_specs=pl.BlockSpec((D_MODEL, D_MODEL), lambda j, i: (0, j)),
        scratch_shapes=[pltpu.VMEM((D_MODEL, D_MODEL), F32)],
        compiler_params=_params(32),
    )(h, d_g)


def _sum_slots(slots, name):
    _, rows, cols = slots.shape
    tr = 128

    def body(s_ref, o_ref):
        acc = s_ref[0].astype(F32)
        for d in range(1, N_DEV):
            acc = acc + s_ref[d].astype(F32)
        o_ref[...] = acc

    return pl.pallas_call(
        body, name=name, grid=(rows // tr,),
        out_shape=jax.ShapeDtypeStruct((rows, cols), F32),
        in_specs=[pl.BlockSpec((N_DEV, tr, cols), lambda i: (0, i, 0))],
        out_specs=pl.BlockSpec((tr, cols), lambda i: (i, 0)),
    )(slots)


def _adamw_math(w, g, m, v):
    m = ADAM_B1 * m + (1.0 - ADAM_B1) * g
    v = ADAM_B2 * v + (1.0 - ADAM_B2) * (g * g)
    m_hat = m / (1.0 - ADAM_B1 ** ADAM_STEP)
    v_hat = v / (1.0 - ADAM_B2 ** ADAM_STEP)
    return -ADAM_LR * (m_hat / (jnp.sqrt(v_hat) + ADAM_EPS) + ADAM_WD * w), m, v


def _adamw(w, g, m, v, name):
    rows, cols = w.shape
    tr = rows if rows <= 128 else 128

    def body(w_ref, g_ref, m_ref, v_ref, d_ref, mo_ref, vo_ref):
        d_ref[...], mo_ref[...], vo_ref[...] = _adamw_math(w_ref[...], g_ref[...], m_ref[...], v_ref[...])

    tile = pl.BlockSpec((tr, cols), lambda i: (i, 0))
    return pl.pallas_call(
        body, name=name, grid=(rows // tr,),
        out_shape=[jax.ShapeDtypeStruct((rows, cols), F32)] * 3,
        in_specs=[tile] * 4, out_specs=[tile] * 3,
    )(w, g, m, v)


def _sum_devices(gathered):
    def body(p_ref, o_ref):
        acc = p_ref[0:SMALL_ROWS, :]
        for d in range(1, N_DEV):
            acc = acc + p_ref[d * SMALL_ROWS:(d + 1) * SMALL_ROWS, :]
        o_ref[...] = acc

    return pl.pallas_call(body, name="sum_devices", out_shape=jax.ShapeDtypeStruct((SMALL_ROWS, D_MODEL), F32),
                          in_specs=[VMEM], out_specs=VMEM)(gathered)


def _ada_update(cond_t, d_mod, ada_w, m, v):
    n_layers, _, n_cols = ada_w.shape
    tr = ROW_TILE

    def body(c_ref, dm_ref, w_ref, m_ref, v_ref, g_ref, d_ref, mo_ref, vo_ref):
        g = _mm_f32(_silu(c_ref[...]), dm_ref[...])
        g_ref[...] = g
        d_ref[...], mo_ref[...], vo_ref[...] = _adamw_math(w_ref[...], g, m_ref[...], v_ref[...])

    tile = pl.BlockSpec((None, tr, n_cols), lambda l, i: (l, i, 0))
    return pl.pallas_call(
        body, name="ada_update", grid=(n_layers, D_MODEL // tr),
        out_shape=[jax.ShapeDtypeStruct(ada_w.shape, F32)] * 4,
        in_specs=[pl.BlockSpec((tr, 16), lambda l, i: (i, 0)), pl.BlockSpec((None, 16, n_cols), lambda l, i: (l, 0, 0)),
                  tile, tile, tile],
        out_specs=[tile] * 4,
    )(cond_t, d_mod, ada_w, m, v)


def _cond_ctx_partial(d_modc, ada_w0):
    n_cols = ada_w0.shape[1]
    tr = ROW_TILE

    def body(dm_ref, w_ref, o_ref):
        o_ref[...] = jnp.sum(w_ref[...] * dm_ref[...], axis=-1, keepdims=True)

    return pl.pallas_call(
        body, name="cond_ctx_partial", grid=(D_MODEL // tr,),
        out_shape=jax.ShapeDtypeStruct((D_MODEL, 1), F32),
        in_specs=[pl.BlockSpec((1, n_cols), lambda i: (0, 0)), pl.BlockSpec((tr, n_cols), lambda i: (i, 0))],
        out_specs=pl.BlockSpec((tr, 1), lambda i: (i, 0)),
    )(d_modc, ada_w0)


def _cond_ctx_update(gathered, c_ctx, m, v):
    def body(p_ref, w_ref, m_ref, v_ref, g_ref, d_ref, mo_ref, vo_ref):
        acc = p_ref[0:1, :]
        for s in range(1, N_CHIPS):
            acc = acc + p_ref[16 * s:16 * s + 1, :]
        w = w_ref[...]
        g = acc * _dsilu(w)
        g_ref[...] = g
        d_ref[...], mo_ref[...], vo_ref[...] = _adamw_math(w, g, m_ref[...], v_ref[...])

    return pl.pallas_call(body, name="cond_ctx_update", out_shape=[jax.ShapeDtypeStruct((1, D_MODEL), F32)] * 4,
                          in_specs=[VMEM] * 4, out_specs=[VMEM] * 4)(gathered, c_ctx, m, v)


def _local_step(x2, ctx2, target, mod_mine, mod_ctx, lb_logits, scale_full, w_in_full, late_shards, norm_w, gnorm, final_w):
    row = lambda a: a.reshape(1, -1)
    shift0, scale0, gate0 = (row(a) for a in jnp.split(mod_mine[0], 3))
    shift1, scale1, gate1 = (row(a) for a in jnp.split(mod_mine[1], 3))
    shift_c, scale_c, _ = (row(a) for a in jnp.split(mod_ctx, 3))
    xa = jnp.concatenate([ctx2, x2], axis=0)
    nw0, nw1 = norm_w[0:1], norm_w[1:2]
    scales0 = jnp.concatenate([scale_c, scale0])

    g5, h0 = _ln_mod_matmul(xa, nw0, jnp.concatenate([shift_c, shift0]), scales0, w_in_full, "hgrn_in_forward")
    o_raw, full = _gla_forward(g5, lb_logits, late_shards)
    x1, res0 = _hgrn_out_forward(o_raw, g5, x2, gnorm, gate0, full["hgrn_w_out"])
    uz, h1 = _ln_mod_matmul(x1, nw1, shift1, scale1, full["pool_w_in"], "pool_in_forward")
    d_x2, loss_part, d_final, d_gate1 = _pool_forward_loss(uz, x1, target, gate1, full["pool_w_grp"], scale_full,
                                                           full["pool_w_out"], final_w)

    d_uz, dw_pool_out, dw_pool_grp, d_pscale = _pool_backward(d_x2, uz, gate1, full["pool_w_grp"], scale_full, full["pool_w_out"])
    d_x1, d_nw1, d_mod1, _ = _ln_mod_backward(d_uz, full["pool_w_in"], x1, nw1, scale1, d_x2, "pool_in_backward")
    dw_pool_in = _weight_grad(h1, d_uz, "pool_in_weight_grad")
    d_o, d_z, dw_hgrn_out, d_gate0, d_gnorm = _hgrn_out_backward(d_x1, o_raw, g5, res0, gnorm, gate0, full["hgrn_w_out"])
    late_grads = {"hgrn_w_out": dw_hgrn_out, "pool_w_in": dw_pool_in, "pool_w_grp": dw_pool_grp, "pool_w_out": dw_pool_out}
    d_g5, d_lb, late_slots = _gla_backward(g5, lb_logits, d_o, d_z, [late_grads[k].astype(BF16) for k in GATHER_LATE])
    dw_hgrn_in = _weight_grad(h0, d_g5, "hgrn_in_weight_grad")
    key = GATHER_EARLY[0]
    half = _half_shape(key)
    xi, yi, ci = _my_place()
    own = lax.dynamic_slice(dw_hgrn_in, (ci * half[0], (2 * xi + yi) * half[1]), half)
    slots0 = lax.dynamic_update_slice(jnp.zeros((N_DEV,) + half, BF16), own[None], (4 * xi + 2 * yi + ci, 0, 0))
    send_sem, recv_sem, part_thru, slots_thru, token = _scatter_start(dw_hgrn_in, slots0, own, key, 0)
    d_x, d_nw0, d_mod0, _ = _ln_mod_backward(d_g5, w_in_full, xa, nw0 + token[0:1, 0:1], scales0, d_x1, "hgrn_in_backward")
    slots = dict(late_slots)
    pending = (send_sem, recv_sem, part_thru, slots_thru)

    zero = jnp.zeros((1, D_MODEL), F32)
    small = jnp.concatenate([d_mod0[1, 0:2], d_gate0, d_mod1[0, 0:2], d_gate1, d_mod0[0, 0:2], zero, d_nw0, d_nw1, d_gnorm,
                             d_final, d_pscale, d_lb, jnp.broadcast_to(loss_part[0:1, 0:1], (1, D_MODEL)),
                             jnp.zeros((SMALL_ROWS - 17, D_MODEL), F32)], axis=0)
    return {"d_x": d_x, "slots": slots, "pending": pending, "small": small}


def kernel(x, c, ctx, c_ctx, ada_w, ada_b, norm_w, hgrn_w_in, hgrn_lb_logits, hgrn_gnorm_w, hgrn_w_out, pool_w_in, pool_w_grp, pool_scale, pool_w_out, final_norm_w, loss_target, m_c_ctx, m_ada_w, m_ada_b, m_norm_w, m_hgrn_w_in, m_hgrn_lb_logits, m_hgrn_gnorm_w, m_hgrn_w_out, m_pool_w_in, m_pool_w_grp, m_pool_scale, m_pool_w_out, m_final_norm_w, v_c_ctx, v_ada_w, v_ada_b, v_norm_w, v_hgrn_w_in, v_hgrn_lb_logits, v_hgrn_gnorm_w, v_hgrn_w_out, v_pool_w_in, v_pool_w_grp, v_pool_scale, v_pool_w_out, v_final_norm_w):
    xi, yi, ci = _my_place()
    chip = 2 * xi + yi
    dev = 4 * xi + 2 * yi + ci
    ada_cols = ada_w.shape[2]
    lb_cols = hgrn_lb_logits.shape[2]
    ps_cols = pool_scale.shape[1]
    row = lambda a: a.reshape(1, -1)

    def chip_cols(a, n):
        return lax.dynamic_slice_in_dim(a, chip * n, n, axis=a.ndim - 1)

    def from_chips(g, rows_per_dev, take):
        return jnp.concatenate([g[2 * s * rows_per_dev:2 * s * rows_per_dev + take] for s in range(N_CHIPS)], axis=1)

    first = jnp.concatenate([jnp.broadcast_to(c, (8, D_MODEL)), jnp.pad(hgrn_lb_logits[0], ((0, 6), (0, 0))),
                             jnp.pad(pool_scale, ((0, 7), (0, 0)))], axis=1)
    first_all = _all_gather_small(first, "gather_cond")
    cond_all = first_all[::8, :D_MODEL]
    lb_logits = from_chips(first_all[:, D_MODEL:D_MODEL + lb_cols], 8, 2)
    scale_full = from_chips(first_all[:, D_MODEL + lb_cols:], 8, 1)
    cond_rows = jnp.concatenate([cond_all, row(c_ctx), jnp.zeros((7, D_MODEL), F32)], axis=0)

    parts = _mod_parts(cond_rows, ada_w, chip_cols(ada_b, ada_cols))
    parts_all = _all_gather_small(parts.reshape(32, ada_cols), "gather_mod")
    mod_all = from_chips(parts_all, 32, 32).reshape(2, 16, 3 * D_MODEL)
    mod_mine = lax.dynamic_index_in_dim(mod_all, dev, axis=1, keepdims=False)

    shards = {"hgrn_w_in": hgrn_w_in[0], "hgrn_w_out": hgrn_w_out[0], "pool_w_in": pool_w_in[0],
              "pool_w_grp": pool_w_grp[0], "pool_w_out": pool_w_out[0]}
    w_in_full = _all_gather_weights([shards[k].astype(BF16) for k in GATHER_EARLY], GATHER_EARLY)[0]

    loc = _local_step(x[0], ctx[0], loss_target[0], mod_mine, mod_all[0, 8], lb_logits, scale_full, w_in_full,
                      [shards[k].astype(BF16) for k in GATHER_LATE], norm_w, hgrn_gnorm_w, row(final_norm_w))

    small_all = _all_gather_small(loc["small"], "gather_small")
    sums = _sum_devices(small_all)
    loss = sums[16, 0]

    part_c = _cond_ctx_partial(chip_cols(sums[6:9].reshape(1, -1), ada_cols), ada_w[0])
    part_all = _all_gather_small(jnp.pad(part_c.reshape(1, D_MODEL), ((0, 7), (0, 0))), "gather_cond_ctx")
    key = GATHER_EARLY[0]
    send0, recv0, part_thru, slots_thru = loc["pending"]
    send1, recv1, part_thru, slots_thru, _ = _scatter_start(part_thru, slots_thru, part_all, key, 1)

    def reduce_scattered(slots, names, name):
        halves = []
        for k in names:
            hs = _half_shape(k)
            halves.append(_sum_slots(slots[k].reshape(N_DEV, -1, hs[-1]), "sum_" + k).reshape(hs))
        return dict(zip(names, _exchange_halves(halves, names, name)))

    big_grads = reduce_scattered(loc["slots"], GATHER_LATE, "exchange_halves_late")

    out = {}

    def update(name, w, g, m, v):
        shape = w.shape
        w2, g2, m2, v2 = (a.reshape(-1, shape[-1]) for a in (w, g, m, v))
        d, mn, vn = _adamw(w2, g2, m2, v2, "adamw_" + name)
        out[name] = tuple(a.reshape(shape) for a in (g2, d, mn, vn))

    moments = {"hgrn_w_in": (m_hgrn_w_in, v_hgrn_w_in), "hgrn_w_out": (m_hgrn_w_out, v_hgrn_w_out),
               "pool_w_in": (m_pool_w_in, v_pool_w_in), "pool_w_grp": (m_pool_w_grp, v_pool_w_grp),
               "pool_w_out": (m_pool_w_out, v_pool_w_out)}
    weights = {"hgrn_w_in": hgrn_w_in, "hgrn_w_out": hgrn_w_out, "pool_w_in": pool_w_in, "pool_w_grp": pool_w_grp,
               "pool_w_out": pool_w_out}
    for k in GATHER_LATE:
        update(k, weights[k], big_grads[k], *moments[k])

    g_ada_b = jnp.stack([(sums[0:3] + sums[6:9]).reshape(-1), sums[3:6].reshape(-1)])
    update("ada_b", ada_b, g_ada_b, m_ada_b, v_ada_b)
    update("norm_w", norm_w, sums[9:11], m_norm_w, v_norm_w)
    update("hgrn_gnorm_w", hgrn_gnorm_w, sums[11:12], m_hgrn_gnorm_w, v_hgrn_gnorm_w)
    update("final_norm_w", row(final_norm_w), sums[12:13], row(m_final_norm_w), row(v_final_norm_w))
    update("pool_scale", pool_scale, chip_cols(sums[13:14], ps_cols), m_pool_scale, v_pool_scale)
    update("hgrn_lb_logits", hgrn_lb_logits, chip_cols(sums[14:16], lb_cols)[None], m_hgrn_lb_logits, v_hgrn_lb_logits)

    per_dev = small_all.reshape(N_DEV, SMALL_ROWS, D_MODEL)
    pad7 = jnp.zeros((7, 3 * D_MODEL), F32)
    dm0 = jnp.concatenate([per_dev[:, 0:3].reshape(N_DEV, -1), sums[6:9].reshape(1, -1), pad7], axis=0)
    dm1 = jnp.concatenate([per_dev[:, 3:6].reshape(N_DEV, -1), jnp.zeros((8, 3 * D_MODEL), F32)], axis=0)
    d_mod = chip_cols(jnp.stack([dm0, dm1]), ada_cols)
    out["ada_w"] = _ada_update(cond_rows.T, d_mod, ada_w, m_ada_w, v_ada_w)

    g_c, d_c, m_c, v_c = _cond_ctx_update(part_all, row(c_ctx), row(m_c_ctx), row(v_c_ctx))
    out["c_ctx"] = tuple(a.reshape(-1) for a in (g_c, d_c, m_c, v_c))
    out["final_norm_w"] = tuple(a.reshape(-1) for a in out["final_norm_w"])

    done = [g_c, out["ada_w"][1]] + [out[k][1] for k in GATHER_LATE]
    part_thru, slots_thru = _scatter_wait(send0, recv0, part_thru, slots_thru, done, key, 0)
    _, early = _scatter_wait(send1, recv1, part_thru, slots_thru, done, key, 1)
    big_grads = reduce_scattered({key: early}, GATHER_EARLY, "exchange_halves_early")
    for k in GATHER_EARLY:
        update(k, weights[k], big_grads[k], *moments[k])

    names = ["c_ctx", "ada_w", "ada_b", "norm_w", "hgrn_w_in", "hgrn_lb_logits", "hgrn_gnorm_w", "hgrn_w_out", "pool_w_in",
             "pool_w_grp", "pool_scale", "pool_w_out", "final_norm_w"]
    return (loss, loc["d_x"][None], *[out[k][0] for k in names], *[out[k][1] for k in names], *[out[k][2] for k in names],
            *[out[k][3] for k in names])
```

```python
import functools

import numpy as np
import jax
import jax.numpy as jnp
from jax import lax
from jax.experimental import pallas as pl
from jax.experimental.pallas import tpu as pltpu

F32 = jnp.float32
BF16 = jnp.bfloat16

D_MODEL = 1024
SEQ = 2048
CTX_LEN = 256
ROWS_ALL = CTX_LEN + SEQ
HEADS = 8
HEAD_DIM = 128
CHUNK = 64
N_CTX_CHUNKS = CTX_LEN // CHUNK
N_LAT_CHUNKS = SEQ // CHUNK
N_CHUNKS = N_CTX_CHUNKS + N_LAT_CHUNKS
GRID_W = 64
POOL_WINDOWS = (2, 4, 8, 16)
POOL_GROUPS = 4
POOL_GROUP_DIM = 256
HGRN_SECTIONS = 5
POOL_SECTIONS = 2
EPS = 1e-6
N_DEV = 8
N_CHIPS = 4
ROW_TILE = 256
SMALL_ROWS = 24

ADAM_LR = 0.001
ADAM_B1 = 0.9
ADAM_B2 = 0.999
ADAM_EPS = 1e-08
ADAM_WD = 0.01
ADAM_STEP = 10

MESH = pl.DeviceIdType.MESH
MIB = 1 << 20
ANY = pl.BlockSpec(memory_space=pl.ANY)
VMEM = pl.BlockSpec(memory_space=pltpu.VMEM)


def _params(vmem_mib=None):
    if vmem_mib is None:
        return pltpu.CompilerParams()
    return pltpu.CompilerParams(vmem_limit_bytes=vmem_mib * MIB)


def _sig(a):
    return 1.0 / (1.0 + jnp.exp(-a))


def _silu(a):
    return a * _sig(a)


def _dsilu(a):
    s = _sig(a)
    return s * (1.0 + a * (1.0 - s))


def _mm(a, b):
    return jnp.dot(a.astype(BF16), b.astype(BF16), preferred_element_type=F32)


def _mm_nt(a, b):
    return lax.dot_general(a.astype(BF16), b.astype(BF16), (((1,), (1,)), ((), ())), preferred_element_type=F32)


def _mm_tn(a, b):
    return lax.dot_general(a.astype(BF16), b.astype(BF16), (((0,), (0,)), ((), ())), preferred_element_type=F32)


def _split2(a):
    hi = a.astype(BF16)
    lo = (a - hi.astype(F32)).astype(BF16)
    return hi, lo


def _mm_exact_lhs(m_bf, a):
    hi, lo = _split2(a)
    return jnp.dot(m_bf, hi, preferred_element_type=F32) + jnp.dot(m_bf, lo, preferred_element_type=F32)


def _mm_f32(a, b):
    ah, al = _split2(a)
    bh, bl = _split2(b)
    return (jnp.dot(ah, bh, preferred_element_type=F32) + jnp.dot(al, bh, preferred_element_type=F32)
            + jnp.dot(ah, bl, preferred_element_type=F32))


def _my_place():
    return lax.axis_index("x"), lax.axis_index("y"), lax.axis_index("c")


def _all_gather_small(blk, name):
    m_per, n = blk.shape

    def body(x_ref, out_ref, send_sems, recv_sems, local_sem):
        x, y, c = _my_place()
        me, sibling = (x, y, c), (x, y, 1 - c)
        chips = [(1 - x, y), (x, 1 - y), (1 - x, 1 - y)]

        def rows(px, py, pc):
            return out_ref.at[pl.ds((4 * px + 2 * py + pc) * m_per, m_per), :]

        def copy(k, block, to, src=None):
            return pltpu.make_async_remote_copy(
                src_ref=rows(*block) if src is None else src, dst_ref=rows(*block),
                send_sem=send_sems.at[k], recv_sem=recv_sems.at[k], device_id=to, device_id_type=MESH)

        mine = pltpu.make_async_copy(x_ref, rows(*me), local_sem)
        mine.start()
        first = [copy(0, me, sibling, src=x_ref)]
        first += [copy(1 + j, me, (*chip, c), src=x_ref) for j, chip in enumerate(chips)]
        for cp in first:
            cp.start()
        passed = [copy(4 + j, (*chip, c), sibling) for j, chip in enumerate(chips)]
        for j, chip in enumerate(chips):
            copy(1 + j, (*chip, c), me).wait_recv()
            passed[j].start()
        copy(0, sibling, me).wait_recv()
        for j, chip in enumerate(chips):
            copy(4 + j, (*chip, 1 - c), me).wait_recv()
        for cp in first + passed:
            cp.wait_send()
        mine.wait()

    return pl.pallas_call(
        body, name=name,
        out_shape=jax.ShapeDtypeStruct((N_DEV * m_per, n), blk.dtype),
        in_specs=[VMEM], out_specs=VMEM,
        scratch_shapes=[pltpu.SemaphoreType.DMA((7,)), pltpu.SemaphoreType.DMA((7,)), pltpu.SemaphoreType.DMA],
    )(blk)


W_SPECS = {
    "hgrn_w_in": ((D_MODEL, 5 * D_MODEL), (1, 1280, 0, 512)),
    "hgrn_w_out": ((D_MODEL, D_MODEL), (0, 256, 0, 128)),
    "pool_w_in": ((D_MODEL, 2 * D_MODEL), (1, 512, 0, 512)),
    "pool_w_grp": ((POOL_GROUPS, POOL_GROUP_DIM, POOL_GROUP_DIM), (1, 64, 1, 32)),
    "pool_w_out": ((D_MODEL, D_MODEL), (0, 256, 0, 128)),
}
W_NAMES = tuple(W_SPECS)


def _al(v, m):
    return pl.multiple_of(v, m)


def _region(ref, spec, chip, half):
    ca, cn, ha, hn = spec
    idx = [slice(None)] * len(ref.shape)
    if ca == ha:
        if half is None:
            idx[ca] = pl.ds(_al(chip * cn, cn), cn)
        else:
            idx[ca] = pl.ds(_al(chip * cn + half * hn, hn), hn)
    else:
        idx[ca] = pl.ds(_al(chip * cn, cn), cn)
        if half is not None:
            idx[ha] = pl.ds(_al(half * hn, hn), hn)
    return ref.at[tuple(idx)]


def _half_of(ref, spec, half):
    _, _, ha, hn = spec
    idx = [slice(None)] * len(ref.shape)
    idx[ha] = pl.ds(_al(half * hn, hn), hn)
    return ref.at[tuple(idx)]


PIECE_BYTES = 256 * 1024


def _pieces(ref):
    lead = ref.shape[0]
    want = (int(np.prod(ref.shape)) * ref.dtype.itemsize) // PIECE_BYTES
    n = max([1] + [k for k in range(1, want + 1) if lead % k == 0 and (lead // k) % 16 == 0])
    rows = lead // n
    return [ref.at[pl.ds(i * rows, rows)] for i in range(n)]


def _half_shape(name):
    full, (ca, cn, ha, hn) = W_SPECS[name]
    shp = list(full)
    shp[ca] = cn
    shp[ha] = hn
    return tuple(shp)


def _gather_direct(names, sh, full, send_sems, recv_sems, local_sems):
    specs = [W_SPECS[k][1] for k in names]
    x, y, c = _my_place()
    chip_me = 2 * x + y
    chips = [(1 - x, y), (x, 1 - y), (1 - x, 1 - y)]

    def local(a):
        return pltpu.make_async_copy(sh[a], _region(full[a], specs[a], chip_me, None), local_sems.at[a])

    def remote(a, src, dst, to):
        return pltpu.make_async_remote_copy(src_ref=src, dst_ref=dst, send_sem=send_sems.at[a], recv_sem=recv_sems.at[a],
                                            device_id=to, device_id_type=MESH)

    def start():
        for a in range(len(names)):
            local(a).start()
            for px, py in chips:
                remote(a, sh[a], _region(full[a], specs[a], chip_me, None), (px, py, c)).start()

    def wait():
        for a in range(len(names)):
            ca, cn, _, _ = specs[a]
            idx = [slice(None)] * len(full[a].shape)
            idx[ca] = pl.ds(0, 3 * cn)
            three = full[a].at[tuple(idx)]
            remote(a, three, three, (x, y, c)).wait()
            local(a).wait()

    return start, wait


def _scatter_direct(names, part, slots, send_sems, recv_sems, local_sems):
    specs = [W_SPECS[k][1] for k in names]
    x, y, c = _my_place()
    dev_me = 4 * x + 2 * y + c

    def local(a):
        return pltpu.make_async_copy(_region(part[a], specs[a], 2 * x + y, c), slots[a].at[dev_me], local_sems.at[a])

    def start():
        for a in range(len(names)):
            local(a).start()
            for flip in range(1, N_DEV):
                tx = 1 - x if flip >> 2 else x
                ty = 1 - y if (flip >> 1) & 1 else y
                tc = 1 - c if flip & 1 else c
                pltpu.make_async_remote_copy(src_ref=_region(part[a], specs[a], 2 * tx + ty, tc), dst_ref=slots[a].at[dev_me],
                                             send_sem=send_sems.at[a], recv_sem=recv_sems.at[a], device_id=(tx, ty, tc),
                                             device_id_type=MESH).start()

    def wait():
        for a in range(len(names)):
            seven = slots[a].at[pl.ds(0, N_DEV - 1)]
            pltpu.make_async_remote_copy(src_ref=seven, dst_ref=seven, send_sem=send_sems.at[a], recv_sem=recv_sems.at[a],
                                         device_id=(x, y, c), device_id_type=MESH).wait()
            local(a).wait()

    return start, wait


HBM_SPEC = pl.BlockSpec(memory_space=pltpu.HBM)
SEM_SPEC = pl.BlockSpec(memory_space=pltpu.SEMAPHORE)
SPLIT_EFFECT = pltpu.SideEffectType.DATAFLOW_SIDE_EFFECTING


SCATTER_STAGES = ((1, 2, 4, 6), (3, 5, 7))


def _scatter_start(part, slots, after, name_key, stage):
    spec = W_SPECS[name_key][1]

    def body(part_ref, slots_ref, after_ref, send_sem, recv_sem, part_thru, slots_thru, token):
        x, y, c = _my_place()
        dev_me = 4 * x + 2 * y + c
        for flip in SCATTER_STAGES[stage]:
            tx = 1 - x if flip >> 2 else x
            ty = 1 - y if (flip >> 1) & 1 else y
            tc = 1 - c if flip & 1 else c
            pltpu.make_async_remote_copy(src_ref=_region(part_ref, spec, 2 * tx + ty, tc), dst_ref=slots_ref.at[dev_me],
                                         send_sem=send_sem, recv_sem=recv_sem, device_id=(tx, ty, tc), device_id_type=MESH).start()
        token[...] = jnp.zeros_like(token)

    return pl.pallas_call(
        body, name="scatter_start_%s_%d" % (name_key, stage),
        out_shape=(pltpu.SemaphoreType.DMA(()), pltpu.SemaphoreType.DMA(()), pltpu.HBM(part.shape, part.dtype),
                   pltpu.HBM(slots.shape, slots.dtype), jax.ShapeDtypeStruct((8, 128), F32)),
        in_specs=(HBM_SPEC, HBM_SPEC, ANY), out_specs=(SEM_SPEC, SEM_SPEC, HBM_SPEC, HBM_SPEC, VMEM),
        input_output_aliases={0: 2, 1: 3},
        compiler_params=pltpu.CompilerParams(has_side_effects=SPLIT_EFFECT),
    )(pltpu.with_memory_space_constraint(part, pltpu.HBM), pltpu.with_memory_space_constraint(slots, pltpu.HBM), after)


def _scatter_wait(send_sem, recv_sem, part_thru, slots_thru, after, name_key, stage):
    n_copies = len(SCATTER_STAGES[stage])

    def body(part_ref, slots_ref, send_sem, recv_sem, *rest):
        x, y, c = _my_place()
        landed = slots_ref.at[pl.ds(0, n_copies)]
        copy = pltpu.make_async_remote_copy(src_ref=landed, dst_ref=landed, send_sem=send_sem, recv_sem=recv_sem,
                                            device_id=(x, y, c), device_id_type=MESH)
        copy.wait_send()
        copy.wait_recv()

    return pl.pallas_call(
        body, name="scatter_wait_%s_%d" % (name_key, stage),
        out_shape=(pltpu.HBM(part_thru.shape, part_thru.dtype), pltpu.HBM(slots_thru.shape, slots_thru.dtype)),
        in_specs=(HBM_SPEC, HBM_SPEC, SEM_SPEC, SEM_SPEC) + (ANY,) * len(after), out_specs=(HBM_SPEC, HBM_SPEC),
        input_output_aliases={0: 0, 1: 1},
        compiler_params=pltpu.CompilerParams(has_side_effects=SPLIT_EFFECT),
    )(part_thru, slots_thru, send_sem, recv_sem, *after)


def _comm_sems(n):
    return [pltpu.SemaphoreType.DMA((n,)), pltpu.SemaphoreType.DMA((n,)), pltpu.SemaphoreType.DMA((n,))]


GATHER_EARLY = ("hgrn_w_in",)
GATHER_LATE = ("hgrn_w_out", "pool_w_in", "pool_w_grp", "pool_w_out")


def _all_gather_weights(shards, names):
    n = len(names)
    specs = [W_SPECS[k][1] for k in names]

    def body(*refs):
        sh, full = refs[:n], refs[n:2 * n]
        send_sems, recv_sems, local_sems = refs[2 * n:]
        x, y, c = _my_place()
        chip_me = 2 * x + y
        sibling = (x, y, 1 - c)
        chips = [(1 - x, y), (x, 1 - y), (1 - x, 1 - y)]

        def remote(a, k, src, dst, to):
            return pltpu.make_async_remote_copy(src_ref=src, dst_ref=dst, send_sem=send_sems.at[6 * a + k],
                                                recv_sem=recv_sems.at[6 * a + k], device_id=to, device_id_type=MESH)

        local = [pltpu.make_async_copy(sh[a], _region(full[a], specs[a], chip_me, None), local_sems.at[a]) for a in range(n)]
        for cp in local:
            cp.start()
        sends = []
        for a in range(n):
            for j, (px, py) in enumerate(chips):
                src, dst = _half_of(sh[a], specs[a], c), _region(full[a], specs[a], chip_me, c)
                for s_piece, d_piece in zip(_pieces(src), _pieces(dst)):
                    remote(a, j, s_piece, d_piece, (px, py, c)).start()
                sends.append(remote(a, j, src, dst, (px, py, c)))
        for a in range(n):
            for j, (px, py) in enumerate(chips):
                landed = _region(full[a], specs[a], 2 * px + py, c)
                remote(a, j, landed, landed, (px, py, c)).wait_recv()
                for piece in _pieces(landed):
                    remote(a, 3 + j, piece, piece, sibling).start()
                sends.append(remote(a, 3 + j, landed, landed, sibling))
        for a in range(n):
            for j, (px, py) in enumerate(chips):
                other = _region(full[a], specs[a], 2 * px + py, 1 - c)
                remote(a, 3 + j, other, other, sibling).wait_recv()
        for cp in sends:
            cp.wait_send()
        for cp in local:
            cp.wait()

    return pl.pallas_call(
        body, name="all_gather_weights",
        out_shape=[jax.ShapeDtypeStruct(W_SPECS[k][0], BF16) for k in names],
        in_specs=[VMEM] * n, out_specs=[VMEM] * n,
        scratch_shapes=[pltpu.SemaphoreType.DMA((6 * n,)), pltpu.SemaphoreType.DMA((6 * n,)), pltpu.SemaphoreType.DMA((n,))],
        compiler_params=_params(32),
    )(*shards)


def _slot_shapes(names):
    return [jax.ShapeDtypeStruct((N_DEV,) + _half_shape(k), BF16) for k in names]


def _scatter_grads(parts, names):
    n = len(names)

    def body(*refs):
        start, wait = _scatter_direct(names, refs[:n], refs[n:2 * n], *refs[2 * n:])
        start()
        wait()

    return pl.pallas_call(body, name="scatter_grads", out_shape=_slot_shapes(names), in_specs=[ANY] * n, out_specs=[ANY] * n,
                          scratch_shapes=_comm_sems(n))(*parts)


def _exchange_halves(halves, names, name):
    n = len(names)
    specs = [W_SPECS[k][1] for k in names]

    def shard_shape(k):
        shp = list(_half_shape(k))
        shp[W_SPECS[k][1][2]] *= 2
        return tuple(shp)

    def body(*refs):
        half, out = refs[:n], refs[n:2 * n]
        send_sems, recv_sems, local_sems = refs[2 * n:]
        x, y, c = _my_place()
        sibling = (x, y, 1 - c)

        def remote(a, src, dst):
            return pltpu.make_async_remote_copy(src_ref=src, dst_ref=dst, send_sem=send_sems.at[a], recv_sem=recv_sems.at[a],
                                                device_id=sibling, device_id_type=MESH)

        local = [pltpu.make_async_copy(half[a], _half_of(out[a], specs[a], c), local_sems.at[a]) for a in range(n)]
        for cp in local:
            cp.start()
        for a in range(n):
            mine = _half_of(out[a], specs[a], c)
            for src, dst in zip(_pieces(half[a]), _pieces(mine)):
                remote(a, src, dst).start()
        for a in range(n):
            theirs = _half_of(out[a], specs[a], 1 - c)
            remote(a, theirs, theirs).wait_recv()
        for a in range(n):
            remote(a, half[a], half[a]).wait_send()
        for cp in local:
            cp.wait()

    return pl.pallas_call(
        body, name=name,
        out_shape=[jax.ShapeDtypeStruct(shard_shape(k), F32) for k in names],
        in_specs=[VMEM] * n, out_specs=[VMEM] * n,
        scratch_shapes=[pltpu.SemaphoreType.DMA((n,)), pltpu.SemaphoreType.DMA((n,)), pltpu.SemaphoreType.DMA((n,))],
        compiler_params=_params(32),
    )(*halves)


def _mod_parts(c_rows, ada_w, ada_b_cols):
    n_layers, _, n_cols = ada_w.shape

    def body(c_ref, w_ref, b_ref, o_ref):
        o_ref[...] = _mm_f32(_silu(c_ref[...]), w_ref[...]) + b_ref[...]

    return pl.pallas_call(
        body, name="mod_parts", grid=(n_layers,),
        out_shape=jax.ShapeDtypeStruct((n_layers, 16, n_cols), F32),
        in_specs=[pl.BlockSpec((16, D_MODEL), lambda i: (0, 0)),
                  pl.BlockSpec((None, D_MODEL, n_cols), lambda i: (i, 0, 0)),
                  pl.BlockSpec((None, 1, n_cols), lambda i: (i, 0, 0))],
        out_specs=pl.BlockSpec((None, 16, n_cols), lambda i: (i, 0, 0)),
        compiler_params=_params(40),
    )(c_rows, ada_w, ada_b_cols.reshape(n_layers, 1, n_cols))


def _ln_mod_matmul(ctx_tile, xin, nw, shift, scale, w, name):
    n_mod = shift.shape[0]
    tm = ROW_TILE
    skip = n_mod - 1
    rows = xin.shape[0] + skip * tm
    n_sec = w.shape[1] // D_MODEL

    def body(*refs):
        c_ref = refs[0] if skip else None
        x_ref, nw_ref, sh_ref, sc_ref, w_ref, g_ref, h_ref = refs[skip:]
        xv = jnp.where(pl.program_id(0) == 0, c_ref[...], x_ref[...]) if skip else x_ref[...]
        r = lax.rsqrt(jnp.mean(xv * xv, axis=-1, keepdims=True) + EPS)
        h = ((xv * r * nw_ref[...]) * (1.0 + sc_ref[...]) + sh_ref[...]).astype(BF16)
        h_ref[...] = h
        for k in range(n_sec):
            g_ref[k] = jnp.dot(h, w_ref[:, k * D_MODEL:(k + 1) * D_MODEL], preferred_element_type=F32)

    mod_spec = pl.BlockSpec((None, 1, D_MODEL), lambda i: (jnp.minimum(i, n_mod - 1), 0, 0))
    return pl.pallas_call(
        body, name=name, grid=(rows // tm,),
        out_shape=[pltpu.HBM((n_sec, rows, D_MODEL), F32), pltpu.HBM((rows, D_MODEL), BF16)],
        in_specs=[pl.BlockSpec((tm, D_MODEL), lambda i: (0, 0))] * skip
        + [pl.BlockSpec((tm, D_MODEL), lambda i: (jnp.maximum(i - skip, 0), 0)),
           pl.BlockSpec((1, D_MODEL), lambda i: (0, 0)),
           mod_spec, mod_spec,
           pl.BlockSpec((D_MODEL, n_sec * D_MODEL), lambda i: (0, 0))],
        out_specs=[pl.BlockSpec((n_sec, tm, D_MODEL), lambda i: (0, i, 0)),
                   pl.BlockSpec((tm, D_MODEL), lambda i: (i, 0))],
        compiler_params=_params(48),
    )(*([ctx_tile] * skip), xin, nw, shift.reshape(n_mod, 1, D_MODEL), scale.reshape(n_mod, 1, D_MODEL), w)


def _chunk_masks(rev):
    rid = lax.broadcasted_iota(jnp.int32, (CHUNK, CHUNK), 0)
    cid = lax.broadcasted_iota(jnp.int32, (CHUNK, CHUNK), 1)
    keep = (cid >= rid) if rev else (cid <= rid)
    keep_t = (cid <= rid) if rev else (cid >= rid)
    one, zero = jnp.ones((CHUNK, CHUNK), F32), jnp.zeros((CHUNK, CHUNK), F32)
    return keep, jnp.where(keep, one, zero).astype(BF16), jnp.where(keep_t, one, zero).astype(BF16)


def _chunk_rows(t, rev, latent):
    n = N_LAT_CHUNKS if latent else N_CTX_CHUNKS
    base = CTX_LEN if latent else 0
    idx = (n - 1 - t) if rev else t
    return pl.multiple_of(base + idx * CHUNK, CHUNK)


def _gates(fpre, lb):
    sg = _sig(fpre)
    f = lb + (1.0 - lb) * sg
    return sg, f, 1.0 - f, jnp.log(f)


G_SPEC = lambda sec: pl.BlockSpec((None, ROWS_ALL, HEAD_DIM), lambda h, sec=sec: (sec, 0, h))


def _gla_forward(g5, lb_logits):
    q_scale = HEAD_DIM ** -0.5

    def body(ff_ref, fb_ref, v_ref, q_ref, lg_ref, o_ref, st_ref, decay_ref, qt_ref):
        for rev in (False, True):
            f_ref = fb_ref if rev else ff_ref
            lb = _sig(lg_ref[1:2, :] if rev else lg_ref[0:1, :])
            keep, tri, _ = _chunk_masks(rev)
            last = 0 if rev else CHUNK - 1
            mid = CHUNK // 2 if rev else CHUNK // 2 - 1

            def local_step(t, carry, latent):
                r0 = _chunk_rows(t, rev, latent)
                rows = pl.ds(r0, CHUNK)
                step = t + (N_CTX_CHUNKS if latent else 0)
                _, _, k, lf = _gates(f_ref[rows, :], lb)
                v = v_ref[rows, :]
                b = _mm_exact_lhs(tri, lf)
                bl = b[last:last + 1, :]
                if latent:
                    q = _silu(q_ref[rows, :]) * q_scale
                    bm = b[mid:mid + 1, :]
                    a = _mm_nt(q * jnp.exp(b - bm), k * jnp.exp(bm - b))
                    o = _mm(jnp.where(keep, a, 0.0), v)
                    orow = pl.ds(pl.multiple_of(r0 - CTX_LEN, CHUNK), CHUNK)
                    qt_ref[orow, :] = (q * jnp.exp(b)).astype(BF16)
                    if rev:
                        o_ref[orow, :] += o
                    else:
                        o_ref[orow, :] = o
                decay_ref[step] = jnp.exp(bl)
                st_ref[step] = _mm_tn(v, k * jnp.exp(bl - b))
                return carry

            lax.fori_loop(0, N_CTX_CHUNKS, functools.partial(local_step, latent=False), 0, unroll=2)
            lax.fori_loop(0, N_LAT_CHUNKS, functools.partial(local_step, latent=True), 0, unroll=4)

            def scan_step(t, st):
                update = st_ref[t]
                st_ref[t] = st
                return st * decay_ref[t] + update

            lax.fori_loop(0, N_CHUNKS, scan_step, jnp.zeros((HEAD_DIM, HEAD_DIM), F32), unroll=2)

            def inter_step(t, carry):
                r0 = _chunk_rows(t, rev, True)
                orow = pl.ds(pl.multiple_of(r0 - CTX_LEN, CHUNK), CHUNK)
                o_ref[orow, :] += lax.dot_general(qt_ref[orow, :], st_ref[t + N_CTX_CHUNKS].astype(BF16),
                                                  (((1,), (1,)), ((), ())), preferred_element_type=F32)
                return carry

            lax.fori_loop(0, N_LAT_CHUNKS, inter_step, 0, unroll=4)

    return pl.pallas_call(
        body, name="gla_forward", grid=(HEADS,),
        out_shape=jax.ShapeDtypeStruct((SEQ, D_MODEL), F32),
        in_specs=[G_SPEC(0), G_SPEC(1), G_SPEC(2), G_SPEC(3), pl.BlockSpec((2, HEAD_DIM), lambda h: (0, h))],
        out_specs=pl.BlockSpec((SEQ, HEAD_DIM), lambda h: (0, h)),
        scratch_shapes=[pltpu.VMEM((N_CHUNKS, HEAD_DIM, HEAD_DIM), F32), pltpu.VMEM((N_CHUNKS, 1, HEAD_DIM), F32),
                        pltpu.VMEM((SEQ, HEAD_DIM), BF16)],
        compiler_params=_params(32),
    )(g5, g5, g5, g5, lb_logits)


def _gla_backward(g5, lb_logits, d_o, d_z):
    q_scale = HEAD_DIM ** -0.5

    def body(ff_ref, fb_ref, v_ref, q_ref, lg_ref, do_ref, dz_ref, dg_ref, dlg_ref, st_ref, dst_ref, decay_ref):
        dg_ref[3, 0:CTX_LEN, :] = jnp.zeros((CTX_LEN, HEAD_DIM), F32)
        dg_ref[4, 0:CTX_LEN, :] = jnp.zeros((CTX_LEN, HEAD_DIM), F32)
        dg_ref[4, CTX_LEN:ROWS_ALL, :] = dz_ref[...]
        is_row = lax.broadcasted_iota(jnp.int32, (CHUNK, HEAD_DIM), 0)
        for rev in (False, True):
            d = 1 if rev else 0
            f_ref = fb_ref if rev else ff_ref
            lb = _sig(lg_ref[d:d + 1, :])
            keep, tri, tri_t = _chunk_masks(rev)
            last = 0 if rev else CHUNK - 1
            mid = CHUNK // 2 if rev else CHUNK // 2 - 1

            def local_step(t, carry, latent):
                r0 = _chunk_rows(t, rev, latent)
                rows = pl.ds(r0, CHUNK)
                step = t + (N_CTX_CHUNKS if latent else 0)
                _, _, k, lf = _gates(f_ref[rows, :], lb)
                b = _mm_exact_lhs(tri, lf)
                bl = b[last:last + 1, :]
                decay_ref[step] = jnp.exp(bl)
                st_ref[step] = _mm_tn(v_ref[rows, :], k * jnp.exp(bl - b))
                if latent:
                    q_t = _silu(q_ref[rows, :]) * q_scale * jnp.exp(b)
                    dst_ref[step] = _mm_tn(do_ref[pl.ds(pl.multiple_of(r0 - CTX_LEN, CHUNK), CHUNK), :], q_t)
                else:
                    dst_ref[step] = jnp.zeros((HEAD_DIM, HEAD_DIM), F32)
                return carry

            lax.fori_loop(0, N_CTX_CHUNKS, functools.partial(local_step, latent=False), 0, unroll=2)
            lax.fori_loop(0, N_LAT_CHUNKS, functools.partial(local_step, latent=True), 0, unroll=4)

            def scan_step(i, carry):
                st, d_st = carry
                j = N_CHUNKS - 1 - i
                update, d_update = st_ref[i], dst_ref[j]
                st_ref[i] = st
                dst_ref[j] = d_st
                return st * decay_ref[i] + update, d_st * decay_ref[j] + d_update

            zero_state = jnp.zeros((HEAD_DIM, HEAD_DIM), F32)
            lax.fori_loop(0, N_CHUNKS, scan_step, (zero_state, zero_state))

            def grad_step(t, dlb, latent):
                r0 = _chunk_rows(t, rev, latent)
                rows = pl.ds(r0, CHUNK)
                step = t + (N_CTX_CHUNKS if latent else 0)
                sg, f, k, lf = _gates(f_ref[rows, :], lb)
                v = v_ref[rows, :]
                b = _mm_exact_lhs(tri, lf)
                bl = b[last:last + 1, :]
                e_end = jnp.exp(bl - b)
                k_end = k * e_end
                decay = jnp.exp(bl)
                d_st = dst_ref[step]
                st_prev = st_ref[step]
                d_kend = _mm(v, d_st)
                d_decay = jnp.sum(d_st * st_prev, axis=0, keepdims=True)
                t_kend = d_kend * k_end
                d_v = _mm_nt(k_end, d_st)
                d_k = d_kend * e_end
                d_b = -t_kend
                if latent:
                    qpre = q_ref[rows, :]
                    q = _silu(qpre) * q_scale
                    bm = b[mid:mid + 1, :]
                    e_b, e_qm, e_km = jnp.exp(b), jnp.exp(b - bm), jnp.exp(bm - b)
                    q_t, q_m, k_m = q * e_b, q * e_qm, k * e_km
                    a = jnp.where(keep, _mm_nt(q_m, k_m), 0.0)
                    d_out = do_ref[pl.ds(pl.multiple_of(r0 - CTX_LEN, CHUNK), CHUNK), :]
                    d_a = jnp.where(keep, _mm_nt(d_out, v), 0.0)
                    d_qm = _mm(d_a, k_m)
                    d_km = _mm_tn(d_a, q_m)
                    d_qt = _mm(d_out, st_prev)
                    d_v = d_v + _mm_tn(a, d_out)
                    d_k = d_k + d_km * e_km
                    d_b = d_b + d_qt * q_t + d_qm * q_m - d_km * k_m
                    d_q = d_qt * e_b + d_qm * e_qm
                at_last = jnp.sum(t_kend, axis=0, keepdims=True) + d_decay * decay
                d_b = d_b + jnp.where(is_row == last, at_last, 0.0)
                d_lf = _mm_exact_lhs(tri_t, d_b)
                d_f = d_lf / f - d_k
                dg_ref[d, rows, :] = d_f * (1.0 - lb) * sg * (1.0 - sg)
                if rev:
                    dg_ref[2, rows, :] += d_v
                else:
                    dg_ref[2, rows, :] = d_v
                if latent:
                    d_qpre = d_q * q_scale * _dsilu(qpre)
                    if rev:
                        dg_ref[3, rows, :] += d_qpre
                    else:
                        dg_ref[3, rows, :] = d_qpre
                return dlb + jnp.sum(d_f * (1.0 - sg), axis=0, keepdims=True)

            dlb = lax.fori_loop(0, N_LAT_CHUNKS, functools.partial(grad_step, latent=True), jnp.zeros((1, HEAD_DIM), F32),
                                unroll=2)
            dlb = lax.fori_loop(0, N_CTX_CHUNKS, functools.partial(grad_step, latent=False), dlb, unroll=2)
            dlg_ref[d:d + 1, :] = dlb * lb * (1.0 - lb)

    col = pl.BlockSpec((SEQ, HEAD_DIM), lambda h: (0, h))
    return pl.pallas_call(
        body, name="gla_backward", grid=(HEADS,),
        out_shape=[jax.ShapeDtypeStruct((HGRN_SECTIONS, ROWS_ALL, D_MODEL), F32), jax.ShapeDtypeStruct((2, D_MODEL), F32)],
        in_specs=[G_SPEC(0), G_SPEC(1), G_SPEC(2), G_SPEC(3), pl.BlockSpec((2, HEAD_DIM), lambda h: (0, h)), col, col],
        out_specs=[pl.BlockSpec((HGRN_SECTIONS, ROWS_ALL, HEAD_DIM), lambda h: (0, 0, h)),
                   pl.BlockSpec((2, HEAD_DIM), lambda h: (0, h))],
        scratch_shapes=[pltpu.VMEM((N_CHUNKS, HEAD_DIM, HEAD_DIM), F32), pltpu.VMEM((N_CHUNKS, HEAD_DIM, HEAD_DIM), F32),
                        pltpu.VMEM((N_CHUNKS, 1, HEAD_DIM), F32)],
        compiler_params=_params(48),
    )(g5, g5, g5, g5, lb_logits, d_o, d_z)


GROUP = 2 * CHUNK


def _group_masks(rev):
    rid = lax.broadcasted_iota(jnp.int32, (GROUP, GROUP), 0)
    cid = lax.broadcasted_iota(jnp.int32, (GROUP, GROUP), 1)
    same = (rid >= CHUNK) == (cid >= CHUNK)
    causal = (cid >= rid) if rev else (cid <= rid)
    anti = (cid <= rid) if rev else (cid >= rid)
    bf = lambda m: jnp.where(m, jnp.ones((GROUP, GROUP), F32), jnp.zeros((GROUP, GROUP), F32)).astype(BF16)
    keep = same & causal
    return keep, bf(keep), bf(same & anti), bf(same & jnp.logical_not(anti))


def _group_sum(m_bf, a):
    hi, lo = _split2(a)
    r = jnp.dot(m_bf, jnp.concatenate([hi, lo], axis=1), preferred_element_type=F32)
    return r[:, :HEAD_DIM] + r[:, HEAD_DIM:]


def _chunk_row(a, pos):
    return jnp.concatenate([jnp.broadcast_to(a[c * CHUNK + pos:c * CHUNK + pos + 1, :], (CHUNK, HEAD_DIM)) for c in range(2)], axis=0)


def _by_chunk(a, second):
    return jnp.concatenate([jnp.where(second, 0.0, a), jnp.where(second, a, 0.0)], axis=1)


def _own_block(r):
    return jnp.concatenate([r[0:CHUNK, 0:HEAD_DIM], r[CHUNK:GROUP, HEAD_DIM:2 * HEAD_DIM]], axis=0)


def _scan_step_of(row_chunk, rev, latent):
    if not rev:
        return row_chunk
    return (N_CHUNKS + N_CTX_CHUNKS - 1 - row_chunk) if latent else (N_CTX_CHUNKS - 1 - row_chunk)


def _group_rows(i, j, per_step, latent):
    base = CTX_LEN if latent else 0
    return pl.multiple_of(base + (i * per_step + j) * GROUP, GROUP)


GROUPS_PER_STEP = 4


def _gla_forward(g5, lb_logits, late_shards):
    q_scale = HEAD_DIM ** -0.5
    per_lat, per_ctx = GROUPS_PER_STEP, min(GROUPS_PER_STEP, CTX_LEN // GROUP)
    n_late = len(GATHER_LATE)

    def body(ff_ref, fb_ref, v_ref, q_ref, lg_ref, *rest):
        shard_refs, o_ref, full_refs = rest[:n_late], rest[n_late], rest[n_late + 1:2 * n_late + 1]
        st_ref, decay_ref, qt_ref = rest[2 * n_late + 1:2 * n_late + 4]
        start_gather, wait_gather = _gather_direct(GATHER_LATE, shard_refs, full_refs, *rest[2 * n_late + 4:])

        @pl.when(pl.program_id(0) == 0)
        def _():
            start_gather()

        second = lax.broadcasted_iota(jnp.int32, (GROUP, HEAD_DIM), 0) >= CHUNK
        for rev in (False, True):
            f_ref = fb_ref if rev else ff_ref
            lb = _sig(lg_ref[1:2, :] if rev else lg_ref[0:1, :])
            keep, tri, _, _ = _group_masks(rev)
            last = 0 if rev else CHUNK - 1
            mid = CHUNK // 2 if rev else CHUNK // 2 - 1

            def local_step(i, carry, latent, per):
                r0s = [_group_rows(i, j, per, latent) for j in range(per)]
                rows = [pl.ds(r0, GROUP) for r0 in r0s]
                gates = [_gates(f_ref[r, :], lb) for r in rows]
                vs = [v_ref[r, :] for r in rows]
                bs = [_group_sum(tri, g[3]) for g in gates]
                bls = [_chunk_row(b, last) for b in bs]
                ups = [_mm_tn(v, _by_chunk(g[2] * jnp.exp(bl - b), second)) for v, g, b, bl in zip(vs, gates, bs, bls)]
                if latent:
                    qs = [_silu(q_ref[r, :]) * q_scale for r in rows]
                    bms = [_chunk_row(b, mid) for b in bs]
                    a_s = [_mm_nt(q * jnp.exp(b - bm), g[2] * jnp.exp(bm - b)) for q, g, b, bm in zip(qs, gates, bs, bms)]
                    outs = [_mm(jnp.where(keep, a, 0.0), v) for a, v in zip(a_s, vs)]
                for j in range(per):
                    for c in range(2):
                        step = _scan_step_of(r0s[j] // CHUNK + c, rev, latent)
                        decay_ref[step] = jnp.exp(bls[j][c * CHUNK:c * CHUNK + 1, :])
                        st_ref[step] = ups[j][:, c * HEAD_DIM:(c + 1) * HEAD_DIM]
                    if latent:
                        orow = pl.ds(pl.multiple_of(r0s[j] - CTX_LEN, GROUP), GROUP)
                        qt_ref[orow, :] = (qs[j] * jnp.exp(bs[j])).astype(BF16)
                        if rev:
                            o_ref[orow, :] += outs[j]
                        else:
                            o_ref[orow, :] = outs[j]
                return carry

            lax.fori_loop(0, CTX_LEN // (per_ctx * GROUP), functools.partial(local_step, latent=False, per=per_ctx), 0)
            lax.fori_loop(0, SEQ // (per_lat * GROUP), functools.partial(local_step, latent=True, per=per_lat), 0)

            def scan_step(t, st):
                update = st_ref[t]
                st_ref[t] = st
                return st * decay_ref[t] + update

            lax.fori_loop(0, N_CHUNKS, scan_step, jnp.zeros((HEAD_DIM, HEAD_DIM), F32), unroll=2)

            def inter_step(i, carry):
                r0s = [_group_rows(i, j, per_lat, True) for j in range(per_lat)]
                orows = [pl.ds(pl.multiple_of(r0 - CTX_LEN, GROUP), GROUP) for r0 in r0s]
                states = [jnp.concatenate([st_ref[_scan_step_of(r0 // CHUNK + c, rev, True)] for c in range(2)], axis=0)
                          for r0 in r0s]
                prods = [lax.dot_general(qt_ref[orow, :], s.astype(BF16), (((1,), (1,)), ((), ())), preferred_element_type=F32)
                         for orow, s in zip(orows, states)]
                for orow, r in zip(orows, prods):
                    o_ref[orow, :] += _own_block(r)
                return carry

            lax.fori_loop(0, SEQ // (per_lat * GROUP), inter_step, 0)

        @pl.when(pl.program_id(0) == HEADS - 1)
        def _():
            wait_gather()

    outs = pl.pallas_call(
        body, name="gla_forward", grid=(HEADS,),
        out_shape=[pltpu.HBM((SEQ, D_MODEL), F32)] + [jax.ShapeDtypeStruct(W_SPECS[k][0], BF16) for k in GATHER_LATE],
        in_specs=[G_SPEC(0), G_SPEC(1), G_SPEC(2), G_SPEC(3), pl.BlockSpec((2, HEAD_DIM), lambda h: (0, h))] + [ANY] * n_late,
        out_specs=[pl.BlockSpec((SEQ, HEAD_DIM), lambda h: (0, h))] + [ANY] * n_late,
        scratch_shapes=[pltpu.VMEM((N_CHUNKS, HEAD_DIM, HEAD_DIM), F32), pltpu.VMEM((N_CHUNKS, 1, HEAD_DIM), F32),
                        pltpu.VMEM((SEQ, HEAD_DIM), BF16)] + _comm_sems(n_late),
        compiler_params=_params(32),
    )(g5, g5, g5, g5, lb_logits, *late_shards)
    return outs[0], dict(zip(GATHER_LATE, outs[1:]))


def _gla_backward(g5, lb_logits, d_o, d_z, late_parts):
    q_scale = HEAD_DIM ** -0.5
    per_lat, per_ctx = GROUPS_PER_STEP, min(GROUPS_PER_STEP, CTX_LEN // GROUP)
    n_late = len(GATHER_LATE)

    def body(ff_ref, fb_ref, v_ref, q_ref, lg_ref, do_ref, dz_ref, *rest):
        part_refs, (dg_ref, dlg_ref), slot_refs = rest[:n_late], rest[n_late:n_late + 2], rest[n_late + 2:2 * n_late + 2]
        st_ref, dst_ref, decay_ref = rest[2 * n_late + 2:2 * n_late + 5]
        start_scatter, wait_scatter = _scatter_direct(GATHER_LATE, part_refs, slot_refs, *rest[2 * n_late + 5:])

        @pl.when(pl.program_id(0) == 0)
        def _():
            start_scatter()

        dg_ref[3, 0:CTX_LEN, :] = jnp.zeros((CTX_LEN, HEAD_DIM), F32)
        dg_ref[4, 0:CTX_LEN, :] = jnp.zeros((CTX_LEN, HEAD_DIM), F32)
        dg_ref[4, CTX_LEN:ROWS_ALL, :] = dz_ref[...]
        second = lax.broadcasted_iota(jnp.int32, (GROUP, HEAD_DIM), 0) >= CHUNK
        for rev in (False, True):
            d = 1 if rev else 0
            f_ref = fb_ref if rev else ff_ref
            lb = _sig(lg_ref[d:d + 1, :])
            keep, tri, tri_t, strict = _group_masks(rev)
            last = 0 if rev else CHUNK - 1
            mid = CHUNK // 2 if rev else CHUNK // 2 - 1

            def local_step(i, carry, latent, per):
                r0s = [_group_rows(i, j, per, latent) for j in range(per)]
                rows = [pl.ds(r0, GROUP) for r0 in r0s]
                gates = [_gates(f_ref[r, :], lb) for r in rows]
                bs = [_group_sum(tri, g[3]) for g in gates]
                bls = [_chunk_row(b, last) for b in bs]
                ups = [_mm_tn(v_ref[r, :], _by_chunk(g[2] * jnp.exp(bl - b), second)) for r, g, b, bl in zip(rows, gates, bs, bls)]
                if latent:
                    orows = [pl.ds(pl.multiple_of(r0 - CTX_LEN, GROUP), GROUP) for r0 in r0s]
                    d_ups = [_mm_tn(do_ref[orow, :], _by_chunk(_silu(q_ref[r, :]) * q_scale * jnp.exp(b), second))
                             for orow, r, b in zip(orows, rows, bs)]
                for j in range(per):
                    for c in range(2):
                        step = _scan_step_of(r0s[j] // CHUNK + c, rev, latent)
                        decay_ref[step] = jnp.exp(bls[j][c * CHUNK:c * CHUNK + 1, :])
                        st_ref[step] = ups[j][:, c * HEAD_DIM:(c + 1) * HEAD_DIM]
                        if latent:
                            dst_ref[step] = d_ups[j][:, c * HEAD_DIM:(c + 1) * HEAD_DIM]
                        else:
                            dst_ref[step] = jnp.zeros((HEAD_DIM, HEAD_DIM), F32)
                return carry

            lax.fori_loop(0, CTX_LEN // (per_ctx * GROUP), functools.partial(local_step, latent=False, per=per_ctx), 0)
            lax.fori_loop(0, SEQ // (per_lat * GROUP), functools.partial(local_step, latent=True, per=per_lat), 0)

            def scan_step(i, carry):
                st, d_st = carry
                j = N_CHUNKS - 1 - i
                update, d_update = st_ref[i], dst_ref[j]
                st_ref[i] = st
                dst_ref[j] = d_st
                return st * decay_ref[i] + update, d_st * decay_ref[j] + d_update

            zero_state = jnp.zeros((HEAD_DIM, HEAD_DIM), F32)
            lax.fori_loop(0, N_CHUNKS, scan_step, (zero_state, zero_state))

            def grad_step(i, dlb, latent, per):
                r0s = [_group_rows(i, j, per, latent) for j in range(per)]
                rows = [pl.ds(r0, GROUP) for r0 in r0s]
                gates = [_gates(f_ref[r, :], lb) for r in rows]
                vs = [v_ref[r, :] for r in rows]
                bs = [_group_sum(tri, g[3]) for g in gates]
                bls = [_chunk_row(b, last) for b in bs]
                e_ends = [jnp.exp(bl - b) for b, bl in zip(bs, bls)]
                k_ends = [g[2] * e for g, e in zip(gates, e_ends)]
                sts = [[st_ref[_scan_step_of(r0 // CHUNK + c, rev, latent)] for c in range(2)] for r0 in r0s]
                d_sts = [[dst_ref[_scan_step_of(r0 // CHUNK + c, rev, latent)] for c in range(2)] for r0 in r0s]
                d_kends = [_own_block(_mm(v, jnp.concatenate(ds, axis=1))) for v, ds in zip(vs, d_sts)]
                d_vs = [_own_block(_mm_nt(ke, jnp.concatenate(ds, axis=0))) for ke, ds in zip(k_ends, d_sts)]
                at_last = [jnp.concatenate([jnp.broadcast_to(jnp.sum(ds[c] * s[c], axis=0, keepdims=True), (CHUNK, HEAD_DIM))
                                            for c in range(2)], axis=0) * jnp.exp(bl) for ds, s, bl in zip(d_sts, sts, bls)]
                t_kends = [dk * ke for dk, ke in zip(d_kends, k_ends)]
                d_ks = [dk * e for dk, e in zip(d_kends, e_ends)]
                d_lfs = [_group_sum(strict, t) + al for t, al in zip(t_kends, at_last)]
                if latent:
                    orows = [pl.ds(pl.multiple_of(r0 - CTX_LEN, GROUP), GROUP) for r0 in r0s]
                    qpres = [q_ref[r, :] for r in rows]
                    qs = [_silu(qp) * q_scale for qp in qpres]
                    bms = [_chunk_row(b, mid) for b in bs]
                    e_bs = [jnp.exp(b) for b in bs]
                    e_qms = [jnp.exp(b - bm) for b, bm in zip(bs, bms)]
                    e_kms = [jnp.exp(bm - b) for b, bm in zip(bs, bms)]
                    q_ts = [q * e for q, e in zip(qs, e_bs)]
                    q_ms = [q * e for q, e in zip(qs, e_qms)]
                    k_ms = [g[2] * e for g, e in zip(gates, e_kms)]
                    d_outs = [do_ref[orow, :] for orow in orows]
                    a_s = [jnp.where(keep, _mm_nt(qm, km), 0.0) for qm, km in zip(q_ms, k_ms)]
                    d_as = [jnp.where(keep, _mm_nt(do, v), 0.0) for do, v in zip(d_outs, vs)]
                    d_qts = [_own_block(_mm(do, jnp.concatenate(s, axis=1))) for do, s in zip(d_outs, sts)]
                    d_qms = [_mm(da, km) for da, km in zip(d_as, k_ms)]
                    d_kms = [_mm_tn(da, qm) for da, qm in zip(d_as, q_ms)]
                    d_vs = [dv + _mm_tn(a, do) for dv, a, do in zip(d_vs, a_s, d_outs)]
                    d_ks = [dk + dkm * e for dk, dkm, e in zip(d_ks, d_kms, e_kms)]
                    d_lfs = [dl + _group_sum(tri_t, dqt * qt + dqm * qm - dkm * km)
                             for dl, dqt, qt, dqm, qm, dkm, km in zip(d_lfs, d_qts, q_ts, d_qms, q_ms, d_kms, k_ms)]
                    d_qs = [dqt * eb + dqm * eq for dqt, eb, dqm, eq in zip(d_qts, e_bs, d_qms, e_qms)]
                for j in range(per):
                    sg, f = gates[j][0], gates[j][1]
                    d_f = d_lfs[j] / f - d_ks[j]
                    dg_ref[d, rows[j], :] = d_f * (1.0 - lb) * sg * (1.0 - sg)
                    dlb = dlb + jnp.sum(d_f * (1.0 - sg), axis=0, keepdims=True)
                    if rev:
                        dg_ref[2, rows[j], :] += d_vs[j]
                    else:
                        dg_ref[2, rows[j], :] = d_vs[j]
                    if latent:
                        d_qpre = d_qs[j] * q_scale * _dsilu(qpres[j])
                        if rev:
                            dg_ref[3, rows[j], :] += d_qpre
                        else:
                            dg_ref[3, rows[j], :] = d_qpre
                return dlb

            dlb = lax.fori_loop(0, SEQ // (per_lat * GROUP), functools.partial(grad_step, latent=True, per=per_lat),
                                jnp.zeros((1, HEAD_DIM), F32))
            dlb = lax.fori_loop(0, CTX_LEN // (per_ctx * GROUP), functools.partial(grad_step, latent=False, per=per_ctx), dlb)
            dlg_ref[d:d + 1, :] = dlb * lb * (1.0 - lb)

        @pl.when(pl.program_id(0) == HEADS - 1)
        def _():
            wait_scatter()

    col = pl.BlockSpec((SEQ, HEAD_DIM), lambda h: (0, h))
    outs = pl.pallas_call(
        body, name="gla_backward", grid=(HEADS,),
        out_shape=[pltpu.HBM((HGRN_SECTIONS, ROWS_ALL, D_MODEL), F32), jax.ShapeDtypeStruct((2, D_MODEL), F32)]
        + _slot_shapes(GATHER_LATE),
        in_specs=[G_SPEC(0), G_SPEC(1), G_SPEC(2), G_SPEC(3), pl.BlockSpec((2, HEAD_DIM), lambda h: (0, h)), col, col]
        + [ANY] * n_late,
        out_specs=[pl.BlockSpec((HGRN_SECTIONS, ROWS_ALL, HEAD_DIM), lambda h: (0, 0, h)),
                   pl.BlockSpec((2, HEAD_DIM), lambda h: (0, h))] + [ANY] * n_late,
        scratch_shapes=[pltpu.VMEM((N_CHUNKS, HEAD_DIM, HEAD_DIM), F32), pltpu.VMEM((N_CHUNKS, HEAD_DIM, HEAD_DIM), F32),
                        pltpu.VMEM((N_CHUNKS, 1, HEAD_DIM), F32)] + _comm_sems(n_late),
        compiler_params=_params(48),
    )(g5, g5, g5, g5, lb_logits, d_o, d_z, *late_parts)
    return outs[0], outs[1], dict(zip(GATHER_LATE, outs[2:]))


def _head_norm(o, gw, scr):
    rs = []
    for h in range(HEADS):
        cols = slice(h * HEAD_DIM, (h + 1) * HEAD_DIM)
        oh = o[:, cols]
        r = lax.rsqrt(jnp.mean(oh * oh, axis=-1, keepdims=True) + EPS)
        scr[:, cols] = oh * r
        rs.append(r)
    return rs


def _hgrn_out_forward(o_raw, g5, xin, gnorm_w, gate, w_out):
    tm = ROW_TILE

    def body(o_ref, z_ref, x_ref, gw_ref, gate_ref, w_ref, x1_ref, res_ref, scr):
        _head_norm(o_ref[...], None, scr)
        a = scr[...] * gw_ref[...] * _silu(z_ref[...])
        res = _mm(a, w_ref[...])
        res_ref[...] = res
        x1_ref[...] = x_ref[...] + gate_ref[...] * res

    tile = pl.BlockSpec((tm, D_MODEL), lambda i: (i, 0))
    vec = pl.BlockSpec((1, D_MODEL), lambda i: (0, 0))
    return pl.pallas_call(
        body, name="hgrn_out_forward", grid=(SEQ // tm,),
        out_shape=[pltpu.HBM((SEQ, D_MODEL), F32)] * 2,
        in_specs=[tile, pl.BlockSpec((None, tm, D_MODEL), lambda i: (4, i + CTX_LEN // tm, 0)), tile, vec, vec,
                  pl.BlockSpec((D_MODEL, D_MODEL), lambda i: (0, 0))],
        out_specs=[tile, tile],
        scratch_shapes=[pltpu.VMEM((tm, D_MODEL), F32)],
        compiler_params=_params(32),
    )(o_raw, g5, xin, gnorm_w, gate, w_out)


def _hgrn_out_backward(d_x1, o_raw, g5, res, gnorm_w, gate, w_out):
    tm = ROW_TILE

    def body(dx_ref, o_ref, z_ref, res_ref, gw_ref, gate_ref, w_ref, do_ref, dz_ref, dw_out, dgate_ref, dgw_ref, scr, scr2,
             dw_ref):
        @pl.when(pl.program_id(0) == 0)
        def _():
            dw_ref[...] = jnp.zeros_like(dw_ref)
            dgate_ref[...] = jnp.zeros_like(dgate_ref)
            dgw_ref[...] = jnp.zeros_like(dgw_ref)

        dx = dx_ref[...]
        dgate_ref[...] += jnp.sum(dx * res_ref[...], axis=0, keepdims=True)
        d_res = (dx * gate_ref[...]).astype(BF16)
        d_a = _mm_nt(d_res, w_ref[...])
        rs = _head_norm(o_ref[...], None, scr)
        z = z_ref[...]
        sz = _silu(z)
        o_hat = scr[...]
        o_n = o_hat * gw_ref[...]
        dw_ref[...] += _mm_tn(o_n * sz, d_res)
        d_on = d_a * sz
        dz_ref[...] = d_a * o_n * _dsilu(z)
        dgw_ref[...] += jnp.sum(d_on * o_hat, axis=0, keepdims=True)
        scr2[...] = d_on * gw_ref[...]
        for h in range(HEADS):
            cols = slice(h * HEAD_DIM, (h + 1) * HEAD_DIM)
            dh, oh = scr2[:, cols], scr[:, cols]
            do_ref[:, cols] = rs[h] * (dh - oh * jnp.mean(dh * oh, axis=-1, keepdims=True))

        @pl.when(pl.program_id(0) == SEQ // tm - 1)
        def _():
            dw_out[...] = dw_ref[...].astype(BF16)

    tile = pl.BlockSpec((tm, D_MODEL), lambda i: (i, 0))
    vec = pl.BlockSpec((1, D_MODEL), lambda i: (0, 0))
    mat = pl.BlockSpec((D_MODEL, D_MODEL), lambda i: (0, 0))
    return pl.pallas_call(
        body, name="hgrn_out_backward", grid=(SEQ // tm,),
        out_shape=[pltpu.HBM((SEQ, D_MODEL), F32)] * 2 + [pltpu.HBM((D_MODEL, D_MODEL), BF16)]
        + [jax.ShapeDtypeStruct((1, D_MODEL), F32)] * 2,
        in_specs=[tile, tile, pl.BlockSpec((None, tm, D_MODEL), lambda i: (4, i + CTX_LEN // tm, 0)), tile, vec, vec, mat],
        out_specs=[tile, tile, mat, vec, vec],
        scratch_shapes=[pltpu.VMEM((tm, D_MODEL), F32)] * 2 + [pltpu.VMEM((D_MODEL, D_MODEL), F32)],
        compiler_params=_params(40),
    )(d_x1, o_raw, g5, res, gnorm_w, gate, w_out)


def _pool_constants():
    win = np.zeros((POOL_GROUPS, ROW_TILE, ROW_TILE), np.float32)
    inv = np.zeros((POOL_GROUPS, ROW_TILE, 1), np.float32)
    for g, w in enumerate(POOL_WINDOWS):
        for t in range(ROW_TILE):
            base, p = (t // GRID_W) * GRID_W, t % GRID_W
            lo = min(max(p - w // 2, 0), GRID_W)
            hi = min(max(p - w // 2 + w, 0), GRID_W)
            win[g, t, base + lo:base + hi] = 1.0
            inv[g, t, 0] = 1.0 / np.float32(hi - lo)
    return jnp.asarray(win, BF16), jnp.asarray(win.transpose(0, 2, 1), BF16), jnp.asarray(inv, F32)


def _pool_mix(u_ref, wg_ref, ps_ref, win_ref, inv_ref, pooled_scr, yg_scr):
    for g in range(POOL_GROUPS):
        cols = slice(g * POOL_GROUP_DIM, (g + 1) * POOL_GROUP_DIM)
        ug = u_ref[:, cols]
        pooled = _mm_exact_lhs(win_ref[g], ug) * inv_ref[g] - ug
        if pooled_scr is not None:
            pooled_scr[:, cols] = pooled
        yg_scr[:, cols] = _mm(pooled, wg_ref[g])


def _pool_forward_loss(uz, x1, target, gate, w_grp, pool_scale, w_out, final_w):
    tm = ROW_TILE
    win, _, inv = _pool_constants()

    def body(u_ref, z_ref, x_ref, t_ref, gate_ref, wg_ref, ps_ref, w_ref, fw_ref, win_ref, inv_ref,
             dx_ref, loss_ref, dfw_ref, dgate_ref, yg_scr):
        @pl.when(pl.program_id(0) == 0)
        def _():
            loss_ref[...] = jnp.zeros_like(loss_ref)
            dfw_ref[...] = jnp.zeros_like(dfw_ref)
            dgate_ref[...] = jnp.zeros_like(dgate_ref)

        _pool_mix(u_ref, wg_ref, ps_ref, win_ref, inv_ref, None, yg_scr)
        a = yg_scr[...] * ps_ref[...] * _silu(z_ref[...])
        res = _mm(a, w_ref[...])
        x2 = x_ref[...] + gate_ref[...] * res
        r = lax.rsqrt(jnp.mean(x2 * x2, axis=-1, keepdims=True) + EPS)
        xh = x2 * r
        fw = fw_ref[...]
        err = xh * fw - t_ref[...]
        loss_ref[...] += 0.5 * jnp.sum(jnp.mean(err * err, axis=-1, keepdims=True))
        d_y = err * (1.0 / D_MODEL)
        dfw_ref[...] += jnp.sum(d_y * xh, axis=0, keepdims=True)
        d_xh = d_y * fw
        d_x2 = r * (d_xh - xh * jnp.mean(d_xh * xh, axis=-1, keepdims=True))
        dx_ref[...] = d_x2
        dgate_ref[...] += jnp.sum(d_x2 * res, axis=0, keepdims=True)

    tile = pl.BlockSpec((tm, D_MODEL), lambda i: (i, 0))
    vec = pl.BlockSpec((1, D_MODEL), lambda i: (0, 0))
    grp = pl.BlockSpec((POOL_GROUPS, POOL_GROUP_DIM, POOL_GROUP_DIM), lambda i: (0, 0, 0))
    return pl.pallas_call(
        body, name="pool_forward_loss", grid=(SEQ // tm,),
        out_shape=[pltpu.HBM((SEQ, D_MODEL), F32), jax.ShapeDtypeStruct((8, 128), F32),
                   jax.ShapeDtypeStruct((1, D_MODEL), F32), jax.ShapeDtypeStruct((1, D_MODEL), F32)],
        in_specs=[pl.BlockSpec((None, tm, D_MODEL), lambda i: (0, i, 0)), pl.BlockSpec((None, tm, D_MODEL), lambda i: (1, i, 0)),
                  tile, tile, vec, grp, vec, pl.BlockSpec((D_MODEL, D_MODEL), lambda i: (0, 0)), vec, grp,
                  pl.BlockSpec((POOL_GROUPS, ROW_TILE, 1), lambda i: (0, 0, 0))],
        out_specs=[tile, pl.BlockSpec((8, 128), lambda i: (0, 0)), vec, vec],
        scratch_shapes=[pltpu.VMEM((tm, D_MODEL), F32)],
        compiler_params=_params(32),
    )(uz, uz, x1, target, gate, w_grp, pool_scale, w_out, final_w, win, inv)


def _pool_backward(d_x2, uz, gate, w_grp, pool_scale, w_out):
    tm = ROW_TILE
    win, win_t, inv = _pool_constants()

    def body(dx_ref, u_ref, z_ref, gate_ref, wg_ref, ps_ref, w_ref, win_ref, wint_ref, inv_ref,
             duz_ref, dw_out, dwg_out, dps_ref, pooled_scr, yg_scr, dyg_scr, dw_ref, dwg_ref):
        @pl.when(pl.program_id(0) == 0)
        def _():
            dw_ref[...] = jnp.zeros_like(dw_ref)
            dwg_ref[...] = jnp.zeros_like(dwg_ref)
            dps_ref[...] = jnp.zeros_like(dps_ref)

        _pool_mix(u_ref, wg_ref, ps_ref, win_ref, inv_ref, pooled_scr, yg_scr)
        z = z_ref[...]
        sz = _silu(z)
        yg = yg_scr[...]
        y = yg * ps_ref[...]
        d_res = (dx_ref[...] * gate_ref[...]).astype(BF16)
        d_a = _mm_nt(d_res, w_ref[...])
        dw_ref[...] += _mm_tn(y * sz, d_res)
        d_y = d_a * sz
        duz_ref[1] = d_a * y * _dsilu(z)
        dps_ref[...] += jnp.sum(d_y * yg, axis=0, keepdims=True)
        dyg_scr[...] = d_y * ps_ref[...]
        for g in range(POOL_GROUPS):
            cols = slice(g * POOL_GROUP_DIM, (g + 1) * POOL_GROUP_DIM)
            d_yg = dyg_scr[:, cols].astype(BF16)
            d_pool = _mm_nt(d_yg, wg_ref[g])
            dwg_ref[g] += _mm_tn(pooled_scr[:, cols], d_yg)
            duz_ref[0, :, cols] = _mm_exact_lhs(wint_ref[g], d_pool * inv_ref[g]) - d_pool

        @pl.when(pl.program_id(0) == SEQ // tm - 1)
        def _():
            dw_out[...] = dw_ref[...].astype(BF16)
            dwg_out[...] = dwg_ref[...].astype(BF16)

    tile = pl.BlockSpec((tm, D_MODEL), lambda i: (i, 0))
    vec = pl.BlockSpec((1, D_MODEL), lambda i: (0, 0))
    mat = pl.BlockSpec((D_MODEL, D_MODEL), lambda i: (0, 0))
    grp = pl.BlockSpec((POOL_GROUPS, POOL_GROUP_DIM, POOL_GROUP_DIM), lambda i: (0, 0, 0))
    return pl.pallas_call(
        body, name="pool_backward", grid=(SEQ // tm,),
        out_shape=[pltpu.HBM((POOL_SECTIONS, SEQ, D_MODEL), F32), pltpu.HBM((D_MODEL, D_MODEL), BF16),
                   pltpu.HBM((POOL_GROUPS, POOL_GROUP_DIM, POOL_GROUP_DIM), BF16), jax.ShapeDtypeStruct((1, D_MODEL), F32)],
        in_specs=[tile, pl.BlockSpec((None, tm, D_MODEL), lambda i: (0, i, 0)), pl.BlockSpec((None, tm, D_MODEL), lambda i: (1, i, 0)),
                  vec, grp, vec, mat, grp, grp, pl.BlockSpec((POOL_GROUPS, ROW_TILE, 1), lambda i: (0, 0, 0))],
        out_specs=[pl.BlockSpec((POOL_SECTIONS, tm, D_MODEL), lambda i: (0, i, 0)), mat, grp, vec],
        scratch_shapes=[pltpu.VMEM((tm, D_MODEL), F32)] * 3 + [pltpu.VMEM((D_MODEL, D_MODEL), F32),
                                                               pltpu.VMEM((POOL_GROUPS, POOL_GROUP_DIM, POOL_GROUP_DIM), F32)],
        compiler_params=_params(40),
    )(d_x2, uz, uz, gate, w_grp, pool_scale, w_out, win, win_t, inv)


def _ln_mod_backward(d_g, w, ctx_tile, xin, nw, scale, d_up, name):
    n_sec, rows, _ = d_g.shape
    n_mod = scale.shape[0]
    tm = ROW_TILE
    n_tiles = rows // tm
    skip = n_mod - 1

    def body(dg_ref, w_ref, *refs):
        c_ref = refs[0] if skip else None
        x_ref, nw_ref, sc_ref, up_ref, dx_ref, dnw_ref, dmod_ref = refs[skip:]
        i = pl.program_id(0)

        @pl.when(i == 0)
        def _():
            dnw_ref[...] = jnp.zeros_like(dnw_ref)

        @pl.when((i == 0) | (i == skip))
        def _():
            dmod_ref[...] = jnp.zeros_like(dmod_ref)

        d_h = _mm_nt(dg_ref[0], w_ref[:, 0:D_MODEL])
        for k in range(1, n_sec):
            d_h = d_h + _mm_nt(dg_ref[k], w_ref[:, k * D_MODEL:(k + 1) * D_MODEL])
        xv = jnp.where(i == 0, c_ref[...], x_ref[...]) if skip else x_ref[...]
        r = lax.rsqrt(jnp.mean(xv * xv, axis=-1, keepdims=True) + EPS)
        xh = xv * r
        nw_row = nw_ref[...]
        dmod_ref[0:1, :] += jnp.sum(d_h, axis=0, keepdims=True)
        dmod_ref[1:2, :] += jnp.sum(d_h * (xh * nw_row), axis=0, keepdims=True)
        d_xn = d_h * (1.0 + sc_ref[...])
        dnw_ref[...] += jnp.sum(d_xn * xh, axis=0, keepdims=True)
        d_xh = d_xn * nw_row

        @pl.when(i >= skip)
        def _():
            dx_ref[...] = up_ref[...] + r * (d_xh - xh * jnp.mean(d_xh * xh, axis=-1, keepdims=True))

    lat = lambda i: (jnp.maximum(i - skip, 0), 0)
    mod_idx = lambda i: (jnp.minimum(i, n_mod - 1), 0, 0)
    return pl.pallas_call(
        body, name=name, grid=(n_tiles,),
        out_shape=[pltpu.HBM((rows - skip * tm, D_MODEL), F32), jax.ShapeDtypeStruct((1, D_MODEL), F32),
                   jax.ShapeDtypeStruct((n_mod, 8, D_MODEL), F32)],
        in_specs=[pl.BlockSpec((n_sec, tm, D_MODEL), lambda i: (0, i, 0)),
                  pl.BlockSpec((D_MODEL, n_sec * D_MODEL), lambda i: (0, 0))]
        + [pl.BlockSpec((tm, D_MODEL), lambda i: (0, 0))] * skip
        + [pl.BlockSpec((tm, D_MODEL), lat),
           pl.BlockSpec((1, D_MODEL), lambda i: (0, 0)),
           pl.BlockSpec((None, 1, D_MODEL), mod_idx),
           pl.BlockSpec((tm, D_MODEL), lat)],
        out_specs=[pl.BlockSpec((tm, D_MODEL), lat), pl.BlockSpec((1, D_MODEL), lambda i: (0, 0)),
                   pl.BlockSpec((None, 8, D_MODEL), mod_idx)],
        compiler_params=_params(48),
    )(d_g, w, *([ctx_tile] * skip), xin, nw, scale.reshape(n_mod, 1, D_MODEL), d_up)


def _weight_grad(h, d_g, name):
    n_sec, rows, _ = d_g.shape
    tm = 768 if rows % 768 == 0 else 512
    n_tiles = rows // tm

    def body(h_ref, dg_ref, dw_ref, acc):
        i = pl.program_id(1)
        prod = _mm_tn(h_ref[...], dg_ref[...])

        @pl.when(i == 0)
        def _():
            acc[...] = prod

        @pl.when((i > 0) & (i < n_tiles - 1))
        def _():
            acc[...] += prod

        @pl.when(i == n_tiles - 1)
        def _():
            dw_ref[...] = (acc[...] + prod).astype(BF16)

    return pl.pallas_call(
        body, name=name, grid=(n_sec, n_tiles),
        out_shape=pltpu.HBM((D_MODEL, n_sec * D_MODEL), BF16),
        in_specs=[pl.BlockSpec((tm, D_MODEL), lambda j, i: (i, 0)), pl.BlockSpec((None, tm, D_MODEL), lambda j, i: (j, i, 0))],
        out_specs=pl.BlockSpec((D_MODEL, D_MODEL), lambda j, i: (0, j)),
        scratch_shapes=[pltpu.VMEM((D_MODEL, D_MODEL), F32)],
        compiler_params=_params(32),
    )(h, d_g)


def _sum_slots(slots, name):
    _, rows, cols = slots.shape
    tr = 128

    def body(s_ref, o_ref):
        acc = s_ref[0].astype(F32)
        for d in range(1, N_DEV):
            acc = acc + s_ref[d].astype(F32)
        o_ref[...] = acc

    return pl.pallas_call(
        body, name=name, grid=(rows // tr,),
        out_shape=jax.ShapeDtypeStruct((rows, cols), F32),
        in_specs=[pl.BlockSpec((N_DEV, tr, cols), lambda i: (0, i, 0))],
        out_specs=pl.BlockSpec((tr, cols), lambda i: (i, 0)),
    )(slots)


def _adamw_math(w, g, m, v):
    m = ADAM_B1 * m + (1.0 - ADAM_B1) * g
    v = ADAM_B2 * v + (1.0 - ADAM_B2) * (g * g)
    m_hat = m / (1.0 - ADAM_B1 ** ADAM_STEP)
    v_hat = v / (1.0 - ADAM_B2 ** ADAM_STEP)
    return -ADAM_LR * (m_hat / (jnp.sqrt(v_hat) + ADAM_EPS) + ADAM_WD * w), m, v


def _adamw(w, g, m, v, name):
    rows, cols = w.shape
    tr = rows if rows <= 128 else 128

    def body(w_ref, g_ref, m_ref, v_ref, d_ref, mo_ref, vo_ref):
        d_ref[...], mo_ref[...], vo_ref[...] = _adamw_math(w_ref[...], g_ref[...], m_ref[...], v_ref[...])

    tile = pl.BlockSpec((tr, cols), lambda i: (i, 0))
    return pl.pallas_call(
        body, name=name, grid=(rows // tr,),
        out_shape=[pltpu.HBM((rows, cols), F32)] * 3,
        in_specs=[tile] * 4, out_specs=[tile] * 3,
    )(w, g, m, v)


def _sum_devices(gathered):
    def body(p_ref, o_ref):
        acc = p_ref[0:SMALL_ROWS, :]
        for d in range(1, N_DEV):
            acc = acc + p_ref[d * SMALL_ROWS:(d + 1) * SMALL_ROWS, :]
        o_ref[...] = acc

    return pl.pallas_call(body, name="sum_devices", out_shape=jax.ShapeDtypeStruct((SMALL_ROWS, D_MODEL), F32),
                          in_specs=[VMEM], out_specs=VMEM)(gathered)


def _ada_update(cond_t, d_mod, ada_w, m, v):
    n_layers, _, n_cols = ada_w.shape
    tr = ROW_TILE

    def body(c_ref, dm_ref, w_ref, m_ref, v_ref, g_ref, d_ref, mo_ref, vo_ref):
        g = _mm_f32(_silu(c_ref[...]), dm_ref[...])
        g_ref[...] = g
        d_ref[...], mo_ref[...], vo_ref[...] = _adamw_math(w_ref[...], g, m_ref[...], v_ref[...])

    tile = pl.BlockSpec((None, tr, n_cols), lambda l, i: (l, i, 0))
    return pl.pallas_call(
        body, name="ada_update", grid=(n_layers, D_MODEL // tr),
        out_shape=[pltpu.HBM(ada_w.shape, F32)] * 4,
        in_specs=[pl.BlockSpec((tr, 16), lambda l, i: (i, 0)), pl.BlockSpec((None, 16, n_cols), lambda l, i: (l, 0, 0)),
                  tile, tile, tile],
        out_specs=[tile] * 4,
    )(cond_t, d_mod, ada_w, m, v)


def _cond_ctx_partial(d_modc, ada_w0):
    n_cols = ada_w0.shape[1]
    tr = ROW_TILE

    def body(dm_ref, w_ref, o_ref):
        o_ref[...] = jnp.sum(w_ref[...] * dm_ref[...], axis=-1, keepdims=True)

    return pl.pallas_call(
        body, name="cond_ctx_partial", grid=(D_MODEL // tr,),
        out_shape=jax.ShapeDtypeStruct((D_MODEL, 1), F32),
        in_specs=[pl.BlockSpec((1, n_cols), lambda i: (0, 0)), pl.BlockSpec((tr, n_cols), lambda i: (i, 0))],
        out_specs=pl.BlockSpec((tr, 1), lambda i: (i, 0)),
    )(d_modc, ada_w0)


def _cond_ctx_update(gathered, c_ctx, m, v):
    def body(p_ref, w_ref, m_ref, v_ref, g_ref, d_ref, mo_ref, vo_ref):
        acc = p_ref[0:1, :]
        for s in range(1, N_CHIPS):
            acc = acc + p_ref[16 * s:16 * s + 1, :]
        w = w_ref[...]
        g = acc * _dsilu(w)
        g_ref[...] = g
        d_ref[...], mo_ref[...], vo_ref[...] = _adamw_math(w, g, m_ref[...], v_ref[...])

    return pl.pallas_call(body, name="cond_ctx_update", out_shape=[jax.ShapeDtypeStruct((1, D_MODEL), F32)] * 4,
                          in_specs=[VMEM] * 4, out_specs=[VMEM] * 4)(gathered, c_ctx, m, v)


def _local_step(x2, ctx2, target, mod_mine, mod_ctx, lb_logits, scale_full, w_in_full, late_shards, norm_w, gnorm, final_w):
    row = lambda a: a.reshape(1, -1)
    shift0, scale0, gate0 = (row(a) for a in jnp.split(mod_mine[0], 3))
    shift1, scale1, gate1 = (row(a) for a in jnp.split(mod_mine[1], 3))
    shift_c, scale_c, _ = (row(a) for a in jnp.split(mod_ctx, 3))
    nw0, nw1 = norm_w[0:1], norm_w[1:2]
    scales0 = jnp.concatenate([scale_c, scale0])

    g5, h0 = _ln_mod_matmul(ctx2, x2, nw0, jnp.concatenate([shift_c, shift0]), scales0, w_in_full, "hgrn_in_forward")
    o_raw, full = _gla_forward(g5, lb_logits, late_shards)
    x1, res0 = _hgrn_out_forward(o_raw, g5, x2, gnorm, gate0, full["hgrn_w_out"])
    uz, h1 = _ln_mod_matmul(None, x1, nw1, shift1, scale1, full["pool_w_in"], "pool_in_forward")
    d_x2, loss_part, d_final, d_gate1 = _pool_forward_loss(uz, x1, target, gate1, full["pool_w_grp"], scale_full,
                                                           full["pool_w_out"], final_w)

    d_uz, dw_pool_out, dw_pool_grp, d_pscale = _pool_backward(d_x2, uz, gate1, full["pool_w_grp"], scale_full, full["pool_w_out"])
    d_x1, d_nw1, d_mod1 = _ln_mod_backward(d_uz, full["pool_w_in"], None, x1, nw1, scale1, d_x2, "pool_in_backward")
    dw_pool_in = _weight_grad(h1, d_uz, "pool_in_weight_grad")
    d_o, d_z, dw_hgrn_out, d_gate0, d_gnorm = _hgrn_out_backward(d_x1, o_raw, g5, res0, gnorm, gate0, full["hgrn_w_out"])
    late_grads = {"hgrn_w_out": dw_hgrn_out, "pool_w_in": dw_pool_in, "pool_w_grp": dw_pool_grp, "pool_w_out": dw_pool_out}
    d_g5, d_lb, late_slots = _gla_backward(g5, lb_logits, d_o, d_z, [late_grads[k].astype(BF16) for k in GATHER_LATE])
    dw_hgrn_in = _weight_grad(h0, d_g5, "hgrn_in_weight_grad")
    key = GATHER_EARLY[0]
    half = _half_shape(key)
    xi, yi, ci = _my_place()
    own = lax.dynamic_slice(dw_hgrn_in, (ci * half[0], (2 * xi + yi) * half[1]), half)
    slots0 = lax.dynamic_update_slice(jnp.zeros((N_DEV,) + half, BF16), own[None], (4 * xi + 2 * yi + ci, 0, 0))
    send_sem, recv_sem, part_thru, slots_thru, token = _scatter_start(dw_hgrn_in, slots0, own, key, 0)
    d_x, d_nw0, d_mod0 = _ln_mod_backward(d_g5, w_in_full, ctx2, x2, nw0 + token[0:1, 0:1], scales0, d_x1, "hgrn_in_backward")
    slots = dict(late_slots)
    pending = (send_sem, recv_sem, part_thru, slots_thru)

    zero = jnp.zeros((1, D_MODEL), F32)
    small = jnp.concatenate([d_mod0[1, 0:2], d_gate0, d_mod1[0, 0:2], d_gate1, d_mod0[0, 0:2], zero, d_nw0, d_nw1, d_gnorm,
                             d_final, d_pscale, d_lb, jnp.broadcast_to(loss_part[0:1, 0:1], (1, D_MODEL)),
                             jnp.zeros((SMALL_ROWS - 17, D_MODEL), F32)], axis=0)
    return {"d_x": d_x, "slots": slots, "pending": pending, "small": small}


def kernel(x, c, ctx, c_ctx, ada_w, ada_b, norm_w, hgrn_w_in, hgrn_lb_logits, hgrn_gnorm_w, hgrn_w_out, pool_w_in, pool_w_grp, pool_scale, pool_w_out, final_norm_w, loss_target, m_c_ctx, m_ada_w, m_ada_b, m_norm_w, m_hgrn_w_in, m_hgrn_lb_logits, m_hgrn_gnorm_w, m_hgrn_w_out, m_pool_w_in, m_pool_w_grp, m_pool_scale, m_pool_w_out, m_final_norm_w, v_c_ctx, v_ada_w, v_ada_b, v_norm_w, v_hgrn_w_in, v_hgrn_lb_logits, v_hgrn_gnorm_w, v_hgrn_w_out, v_pool_w_in, v_pool_w_grp, v_pool_scale, v_pool_w_out, v_final_norm_w):
    xi, yi, ci = _my_place()
    chip = 2 * xi + yi
    dev = 4 * xi + 2 * yi + ci
    ada_cols = ada_w.shape[2]
    lb_cols = hgrn_lb_logits.shape[2]
    ps_cols = pool_scale.shape[1]
    row = lambda a: a.reshape(1, -1)

    def chip_cols(a, n):
        return lax.dynamic_slice_in_dim(a, chip * n, n, axis=a.ndim - 1)

    def from_chips(g, rows_per_dev, take):
        return jnp.concatenate([g[2 * s * rows_per_dev:2 * s * rows_per_dev + take] for s in range(N_CHIPS)], axis=1)

    first = jnp.concatenate([jnp.broadcast_to(c, (8, D_MODEL)), jnp.pad(hgrn_lb_logits[0], ((0, 6), (0, 0))),
                             jnp.pad(pool_scale, ((0, 7), (0, 0)))], axis=1)
    first_all = _all_gather_small(first, "gather_cond")
    cond_all = first_all[::8, :D_MODEL]
    lb_logits = from_chips(first_all[:, D_MODEL:D_MODEL + lb_cols], 8, 2)
    scale_full = from_chips(first_all[:, D_MODEL + lb_cols:], 8, 1)
    cond_rows = jnp.concatenate([cond_all, row(c_ctx), jnp.zeros((7, D_MODEL), F32)], axis=0)

    parts = _mod_parts(cond_rows, ada_w, chip_cols(ada_b, ada_cols))
    parts_all = _all_gather_small(parts.reshape(32, ada_cols), "gather_mod")
    mod_all = from_chips(parts_all, 32, 32).reshape(2, 16, 3 * D_MODEL)
    mod_mine = lax.dynamic_index_in_dim(mod_all, dev, axis=1, keepdims=False)

    shards = {"hgrn_w_in": hgrn_w_in[0], "hgrn_w_out": hgrn_w_out[0], "pool_w_in": pool_w_in[0],
              "pool_w_grp": pool_w_grp[0], "pool_w_out": pool_w_out[0]}
    w_in_full = _all_gather_weights([shards[k].astype(BF16) for k in GATHER_EARLY], GATHER_EARLY)[0]

    loc = _local_step(x[0], ctx[0], loss_target[0], mod_mine, mod_all[0, 8], lb_logits, scale_full, w_in_full,
                      [shards[k].astype(BF16) for k in GATHER_LATE], norm_w, hgrn_gnorm_w, row(final_norm_w))

    small_all = _all_gather_small(loc["small"], "gather_small")
    sums = _sum_devices(small_all)
    loss = sums[16, 0]

    part_c = _cond_ctx_partial(chip_cols(sums[6:9].reshape(1, -1), ada_cols), ada_w[0])
    part_all = _all_gather_small(jnp.pad(part_c.reshape(1, D_MODEL), ((0, 7), (0, 0))), "gather_cond_ctx")
    key = GATHER_EARLY[0]
    send0, recv0, part_thru, slots_thru = loc["pending"]
    send1, recv1, part_thru, slots_thru, _ = _scatter_start(part_thru, slots_thru, part_all, key, 1)

    def reduce_scattered(slots, names, name):
        halves = []
        for k in names:
            hs = _half_shape(k)
            halves.append(_sum_slots(slots[k].reshape(N_DEV, -1, hs[-1]), "sum_" + k).reshape(hs))
        return dict(zip(names, _exchange_halves(halves, names, name)))

    big_grads = reduce_scattered(loc["slots"], GATHER_LATE, "exchange_halves_late")

    out = {}

    def update(name, w, g, m, v):
        shape = w.shape
        w2, g2, m2, v2 = (a.reshape(-1, shape[-1]) for a in (w, g, m, v))
        d, mn, vn = _adamw(w2, g2, m2, v2, "adamw_" + name)
        out[name] = tuple(a.reshape(shape) for a in (g2, d, mn, vn))

    moments = {"hgrn_w_in": (m_hgrn_w_in, v_hgrn_w_in), "hgrn_w_out": (m_hgrn_w_out, v_hgrn_w_out),
               "pool_w_in": (m_pool_w_in, v_pool_w_in), "pool_w_grp": (m_pool_w_grp, v_pool_w_grp),
               "pool_w_out": (m_pool_w_out, v_pool_w_out)}
    weights = {"hgrn_w_in": hgrn_w_in, "hgrn_w_out": hgrn_w_out, "pool_w_in": pool_w_in, "pool_w_grp": pool_w_grp,
               "pool_w_out": pool_w_out}
    for k in GATHER_LATE:
        update(k, weights[k], big_grads[k], *moments[k])

    g_ada_b = jnp.stack([(sums[0:3] + sums[6:9]).reshape(-1), sums[3:6].reshape(-1)])
    update("ada_b", ada_b, g_ada_b, m_ada_b, v_ada_b)
    update("norm_w", norm_w, sums[9:11], m_norm_w, v_norm_w)
    update("hgrn_gnorm_w", hgrn_gnorm_w, sums[11:12], m_hgrn_gnorm_w, v_hgrn_gnorm_w)
    update("final_norm_w", row(final_norm_w), sums[12:13], row(m_final_norm_w), row(v_final_norm_w))
    update("pool_scale", pool_scale, chip_cols(sums[13:14], ps_cols), m_pool_scale, v_pool_scale)
    update("hgrn_lb_logits", hgrn_lb_logits, chip_cols(sums[14:16], lb_cols)[None], m_hgrn_lb_logits, v_hgrn_lb_logits)

    per_dev = small_all.reshape(N_DEV, SMALL_ROWS, D_MODEL)
    pad7 = jnp.zeros((7, 3 * D_MODEL), F32)
    dm0 = jnp.concatenate([per_dev[:, 0:3].reshape(N_DEV, -1), sums[6:9].reshape(1, -1), pad7], axis=0)
    dm1 = jnp.concatenate([per_dev[:, 3:6].reshape(N_DEV, -1), jnp.zeros((8, 3 * D_MODEL), F32)], axis=0)
    d_mod = chip_cols(jnp.stack([dm0, dm1]), ada_cols)
    out["ada_w"] = _ada_update(cond_rows.T, d_mod, ada_w, m_ada_w, v_ada_w)

    g_c, d_c, m_c, v_c = _cond_ctx_update(part_all, row(c_ctx), row(m_c_ctx), row(v_c_ctx))
    out["c_ctx"] = tuple(a.reshape(-1) for a in (g_c, d_c, m_c, v_c))
    out["final_norm_w"] = tuple(a.reshape(-1) for a in out["final_norm_w"])

    done = [g_c, out["ada_w"][1]] + [out[k][1] for k in GATHER_LATE]
    part_thru, slots_thru = _scatter_wait(send0, recv0, part_thru, slots_thru, done, key, 0)
    _, early = _scatter_wait(send1, recv1, part_thru, slots_thru, done, key, 1)
    big_grads = reduce_scattered({key: early}, GATHER_EARLY, "exchange_halves_early")
    for k in GATHER_EARLY:
        update(k, weights[k], big_grads[k], *moments[k])

    names = ["c_ctx", "ada_w", "ada_b", "norm_w", "hgrn_w_in", "hgrn_lb_logits", "hgrn_gnorm_w", "hgrn_w_out", "pool_w_in",
             "pool_w_grp", "pool_scale", "pool_w_out", "final_norm_w"]
    return (loss, loc["d_x"][None], *[out[k][0] for k in names], *[out[k][1] for k in names], *[out[k][2] for k in names],
            *[out[k][3] for k in names])
```

```python
import functools

import numpy as np
import jax
import jax.numpy as jnp
from jax import lax
from jax.experimental import pallas as pl
from jax.experimental.pallas import tpu as pltpu

F32 = jnp.float32
BF16 = jnp.bfloat16

D_MODEL = 1024
SEQ = 2048
CTX_LEN = 256
ROWS_ALL = CTX_LEN + SEQ
HEADS = 8
HEAD_DIM = 128
CHUNK = 64
N_CTX_CHUNKS = CTX_LEN // CHUNK
N_LAT_CHUNKS = SEQ // CHUNK
N_CHUNKS = N_CTX_CHUNKS + N_LAT_CHUNKS
GRID_W = 64
POOL_WINDOWS = (2, 4, 8, 16)
POOL_GROUPS = 4
POOL_GROUP_DIM = 256
HGRN_SECTIONS = 5
POOL_SECTIONS = 2
EPS = 1e-6
N_DEV = 8
N_CHIPS = 4
ROW_TILE = 256
SMALL_ROWS = 24

ADAM_LR = 0.001
ADAM_B1 = 0.9
ADAM_B2 = 0.999
ADAM_EPS = 1e-08
ADAM_WD = 0.01
ADAM_STEP = 10

MESH = pl.DeviceIdType.MESH
MIB = 1 << 20
ANY = pl.BlockSpec(memory_space=pl.ANY)
VMEM = pl.BlockSpec(memory_space=pltpu.VMEM)


def _params(vmem_mib=None):
    if vmem_mib is None:
        return pltpu.CompilerParams()
    return pltpu.CompilerParams(vmem_limit_bytes=vmem_mib * MIB)


def _pin(*operands):
    return [pltpu.with_memory_space_constraint(a, pltpu.HBM) if a.size * a.dtype.itemsize >= MIB else a for a in operands]


def _sig(a):
    return 1.0 / (1.0 + jnp.exp(-a))


def _silu(a):
    return a * _sig(a)


def _dsilu(a):
    s = _sig(a)
    return s * (1.0 + a * (1.0 - s))


def _mm(a, b):
    return jnp.dot(a.astype(BF16), b.astype(BF16), preferred_element_type=F32)


def _mm_nt(a, b):
    return lax.dot_general(a.astype(BF16), b.astype(BF16), (((1,), (1,)), ((), ())), preferred_element_type=F32)


def _mm_tn(a, b):
    return lax.dot_general(a.astype(BF16), b.astype(BF16), (((0,), (0,)), ((), ())), preferred_element_type=F32)


def _split2(a):
    hi = a.astype(BF16)
    lo = (a - hi.astype(F32)).astype(BF16)
    return hi, lo


def _mm_exact_lhs(m_bf, a):
    hi, lo = _split2(a)
    return jnp.dot(m_bf, hi, preferred_element_type=F32) + jnp.dot(m_bf, lo, preferred_element_type=F32)


def _mm_f32(a, b):
    ah, al = _split2(a)
    bh, bl = _split2(b)
    return (jnp.dot(ah, bh, preferred_element_type=F32) + jnp.dot(al, bh, preferred_element_type=F32)
            + jnp.dot(ah, bl, preferred_element_type=F32))


def _my_place():
    return lax.axis_index("x"), lax.axis_index("y"), lax.axis_index("c")


def _all_gather_small(blk, name):
    m_per, n = blk.shape

    def body(x_ref, out_ref, send_sems, recv_sems, local_sem):
        x, y, c = _my_place()
        me, sibling = (x, y, c), (x, y, 1 - c)
        chips = [(1 - x, y), (x, 1 - y), (1 - x, 1 - y)]

        def rows(px, py, pc):
            return out_ref.at[pl.ds((4 * px + 2 * py + pc) * m_per, m_per), :]

        def copy(k, block, to, src=None):
            return pltpu.make_async_remote_copy(
                src_ref=rows(*block) if src is None else src, dst_ref=rows(*block),
                send_sem=send_sems.at[k], recv_sem=recv_sems.at[k], device_id=to, device_id_type=MESH)

        mine = pltpu.make_async_copy(x_ref, rows(*me), local_sem)
        mine.start()
        first = [copy(0, me, sibling, src=x_ref)]
        first += [copy(1 + j, me, (*chip, c), src=x_ref) for j, chip in enumerate(chips)]
        for cp in first:
            cp.start()
        passed = [copy(4 + j, (*chip, c), sibling) for j, chip in enumerate(chips)]
        for j, chip in enumerate(chips):
            copy(1 + j, (*chip, c), me).wait_recv()
            passed[j].start()
        copy(0, sibling, me).wait_recv()
        for j, chip in enumerate(chips):
            copy(4 + j, (*chip, 1 - c), me).wait_recv()
        for cp in first + passed:
            cp.wait_send()
        mine.wait()

    return pl.pallas_call(
        body, name=name,
        out_shape=jax.ShapeDtypeStruct((N_DEV * m_per, n), blk.dtype),
        in_specs=[VMEM], out_specs=VMEM,
        scratch_shapes=[pltpu.SemaphoreType.DMA((7,)), pltpu.SemaphoreType.DMA((7,)), pltpu.SemaphoreType.DMA],
    )(blk)


W_SPECS = {
    "hgrn_w_in": ((D_MODEL, 5 * D_MODEL), (1, 1280, 0, 512)),
    "hgrn_w_out": ((D_MODEL, D_MODEL), (0, 256, 0, 128)),
    "pool_w_in": ((D_MODEL, 2 * D_MODEL), (1, 512, 0, 512)),
    "pool_w_grp": ((POOL_GROUPS, POOL_GROUP_DIM, POOL_GROUP_DIM), (1, 64, 1, 32)),
    "pool_w_out": ((D_MODEL, D_MODEL), (0, 256, 0, 128)),
}
W_NAMES = tuple(W_SPECS)


def _al(v, m):
    return pl.multiple_of(v, m)


def _region(ref, spec, chip, half):
    ca, cn, ha, hn = spec
    idx = [slice(None)] * len(ref.shape)
    if ca == ha:
        if half is None:
            idx[ca] = pl.ds(_al(chip * cn, cn), cn)
        else:
            idx[ca] = pl.ds(_al(chip * cn + half * hn, hn), hn)
    else:
        idx[ca] = pl.ds(_al(chip * cn, cn), cn)
        if half is not None:
            idx[ha] = pl.ds(_al(half * hn, hn), hn)
    return ref.at[tuple(idx)]


def _half_of(ref, spec, half):
    _, _, ha, hn = spec
    idx = [slice(None)] * len(ref.shape)
    idx[ha] = pl.ds(_al(half * hn, hn), hn)
    return ref.at[tuple(idx)]


PIECE_BYTES = 256 * 1024


def _pieces(ref):
    lead = ref.shape[0]
    want = (int(np.prod(ref.shape)) * ref.dtype.itemsize) // PIECE_BYTES
    n = max([1] + [k for k in range(1, want + 1) if lead % k == 0 and (lead // k) % 16 == 0])
    rows = lead // n
    return [ref.at[pl.ds(i * rows, rows)] for i in range(n)]


def _half_shape(name):
    full, (ca, cn, ha, hn) = W_SPECS[name]
    shp = list(full)
    shp[ca] = cn
    shp[ha] = hn
    return tuple(shp)


def _gather_direct(names, sh, full, send_sems, recv_sems, local_sems):
    specs = [W_SPECS[k][1] for k in names]
    x, y, c = _my_place()
    chip_me = 2 * x + y
    chips = [(1 - x, y), (x, 1 - y), (1 - x, 1 - y)]

    def local(a):
        return pltpu.make_async_copy(sh[a], _region(full[a], specs[a], chip_me, None), local_sems.at[a])

    def remote(a, src, dst, to):
        return pltpu.make_async_remote_copy(src_ref=src, dst_ref=dst, send_sem=send_sems.at[a], recv_sem=recv_sems.at[a],
                                            device_id=to, device_id_type=MESH)

    def start():
        for a in range(len(names)):
            local(a).start()
            for px, py in chips:
                remote(a, sh[a], _region(full[a], specs[a], chip_me, None), (px, py, c)).start()

    def wait():
        for a in range(len(names)):
            ca, cn, _, _ = specs[a]
            idx = [slice(None)] * len(full[a].shape)
            idx[ca] = pl.ds(0, 3 * cn)
            three = full[a].at[tuple(idx)]
            remote(a, three, three, (x, y, c)).wait()
            local(a).wait()

    return start, wait


def _scatter_direct(names, part, slots, send_sems, recv_sems, local_sems):
    specs = [W_SPECS[k][1] for k in names]
    x, y, c = _my_place()
    dev_me = 4 * x + 2 * y + c

    def local(a):
        return pltpu.make_async_copy(_region(part[a], specs[a], 2 * x + y, c), slots[a].at[dev_me], local_sems.at[a])

    def start():
        for a in range(len(names)):
            local(a).start()
            for flip in range(1, N_DEV):
                tx = 1 - x if flip >> 2 else x
                ty = 1 - y if (flip >> 1) & 1 else y
                tc = 1 - c if flip & 1 else c
                pltpu.make_async_remote_copy(src_ref=_region(part[a], specs[a], 2 * tx + ty, tc), dst_ref=slots[a].at[dev_me],
                                             send_sem=send_sems.at[a], recv_sem=recv_sems.at[a], device_id=(tx, ty, tc),
                                             device_id_type=MESH).start()

    def wait():
        for a in range(len(names)):
            seven = slots[a].at[pl.ds(0, N_DEV - 1)]
            pltpu.make_async_remote_copy(src_ref=seven, dst_ref=seven, send_sem=send_sems.at[a], recv_sem=recv_sems.at[a],
                                         device_id=(x, y, c), device_id_type=MESH).wait()
            local(a).wait()

    return start, wait


HBM_SPEC = pl.BlockSpec(memory_space=pltpu.HBM)
SEM_SPEC = pl.BlockSpec(memory_space=pltpu.SEMAPHORE)
SPLIT_EFFECT = pltpu.SideEffectType.DATAFLOW_SIDE_EFFECTING


SCATTER_STAGES = ((1, 2, 4, 6), (3, 5, 7))


def _scatter_start(part, slots, after, name_key, stage):
    spec = W_SPECS[name_key][1]

    def body(part_ref, slots_ref, after_ref, send_sem, recv_sem, part_thru, slots_thru, token):
        x, y, c = _my_place()
        dev_me = 4 * x + 2 * y + c
        for flip in SCATTER_STAGES[stage]:
            tx = 1 - x if flip >> 2 else x
            ty = 1 - y if (flip >> 1) & 1 else y
            tc = 1 - c if flip & 1 else c
            pltpu.make_async_remote_copy(src_ref=_region(part_ref, spec, 2 * tx + ty, tc), dst_ref=slots_ref.at[dev_me],
                                         send_sem=send_sem, recv_sem=recv_sem, device_id=(tx, ty, tc), device_id_type=MESH).start()
        token[...] = jnp.zeros_like(token)

    return pl.pallas_call(
        body, name="scatter_start_%s_%d" % (name_key, stage),
        out_shape=(pltpu.SemaphoreType.DMA(()), pltpu.SemaphoreType.DMA(()), pltpu.HBM(part.shape, part.dtype),
                   pltpu.HBM(slots.shape, slots.dtype), jax.ShapeDtypeStruct((8, 128), F32)),
        in_specs=(HBM_SPEC, HBM_SPEC, ANY), out_specs=(SEM_SPEC, SEM_SPEC, HBM_SPEC, HBM_SPEC, VMEM),
        input_output_aliases={0: 2, 1: 3},
        compiler_params=pltpu.CompilerParams(has_side_effects=SPLIT_EFFECT),
    )(pltpu.with_memory_space_constraint(part, pltpu.HBM), pltpu.with_memory_space_constraint(slots, pltpu.HBM), after)


def _scatter_wait(send_sem, recv_sem, part_thru, slots_thru, after, name_key, stage):
    n_copies = len(SCATTER_STAGES[stage])

    def body(part_ref, slots_ref, send_sem, recv_sem, *rest):
        x, y, c = _my_place()
        landed = slots_ref.at[pl.ds(0, n_copies)]
        copy = pltpu.make_async_remote_copy(src_ref=landed, dst_ref=landed, send_sem=send_sem, recv_sem=recv_sem,
                                            device_id=(x, y, c), device_id_type=MESH)
        copy.wait_send()
        copy.wait_recv()

    return pl.pallas_call(
        body, name="scatter_wait_%s_%d" % (name_key, stage),
        out_shape=(pltpu.HBM(part_thru.shape, part_thru.dtype), pltpu.HBM(slots_thru.shape, slots_thru.dtype)),
        in_specs=(HBM_SPEC, HBM_SPEC, SEM_SPEC, SEM_SPEC) + (ANY,) * len(after), out_specs=(HBM_SPEC, HBM_SPEC),
        input_output_aliases={0: 0, 1: 1},
        compiler_params=pltpu.CompilerParams(has_side_effects=SPLIT_EFFECT),
    )(part_thru, slots_thru, send_sem, recv_sem, *after)


def _comm_sems(n):
    return [pltpu.SemaphoreType.DMA((n,)), pltpu.SemaphoreType.DMA((n,)), pltpu.SemaphoreType.DMA((n,))]


GATHER_EARLY = ("hgrn_w_in",)
GATHER_LATE = ("hgrn_w_out", "pool_w_in", "pool_w_grp", "pool_w_out")


def _all_gather_weights(shards, names):
    n = len(names)
    specs = [W_SPECS[k][1] for k in names]

    def body(*refs):
        sh, full = refs[:n], refs[n:2 * n]
        send_sems, recv_sems, local_sems = refs[2 * n:]
        x, y, c = _my_place()
        chip_me = 2 * x + y
        sibling = (x, y, 1 - c)
        chips = [(1 - x, y), (x, 1 - y), (1 - x, 1 - y)]

        def remote(a, k, src, dst, to):
            return pltpu.make_async_remote_copy(src_ref=src, dst_ref=dst, send_sem=send_sems.at[6 * a + k],
                                                recv_sem=recv_sems.at[6 * a + k], device_id=to, device_id_type=MESH)

        local = [pltpu.make_async_copy(sh[a], _region(full[a], specs[a], chip_me, None), local_sems.at[a]) for a in range(n)]
        for cp in local:
            cp.start()
        sends = []
        for a in range(n):
            for j, (px, py) in enumerate(chips):
                src, dst = _half_of(sh[a], specs[a], c), _region(full[a], specs[a], chip_me, c)
                for s_piece, d_piece in zip(_pieces(src), _pieces(dst)):
                    remote(a, j, s_piece, d_piece, (px, py, c)).start()
                sends.append(remote(a, j, src, dst, (px, py, c)))
        for a in range(n):
            for j, (px, py) in enumerate(chips):
                landed = _region(full[a], specs[a], 2 * px + py, c)
                remote(a, j, landed, landed, (px, py, c)).wait_recv()
                for piece in _pieces(landed):
                    remote(a, 3 + j, piece, piece, sibling).start()
                sends.append(remote(a, 3 + j, landed, landed, sibling))
        for a in range(n):
            for j, (px, py) in enumerate(chips):
                other = _region(full[a], specs[a], 2 * px + py, 1 - c)
                remote(a, 3 + j, other, other, sibling).wait_recv()
        for cp in sends:
            cp.wait_send()
        for cp in local:
            cp.wait()

    return pl.pallas_call(
        body, name="all_gather_weights",
        out_shape=[jax.ShapeDtypeStruct(W_SPECS[k][0], BF16) for k in names],
        in_specs=[VMEM] * n, out_specs=[VMEM] * n,
        scratch_shapes=[pltpu.SemaphoreType.DMA((6 * n,)), pltpu.SemaphoreType.DMA((6 * n,)), pltpu.SemaphoreType.DMA((n,))],
        compiler_params=_params(32),
    )(*shards)


def _slot_shapes(names):
    return [jax.ShapeDtypeStruct((N_DEV,) + _half_shape(k), BF16) for k in names]


def _scatter_grads(parts, names):
    n = len(names)

    def body(*refs):
        start, wait = _scatter_direct(names, refs[:n], refs[n:2 * n], *refs[2 * n:])
        start()
        wait()

    return pl.pallas_call(body, name="scatter_grads", out_shape=_slot_shapes(names), in_specs=[ANY] * n, out_specs=[ANY] * n,
                          scratch_shapes=_comm_sems(n))(*parts)


def _exchange_halves(halves, names, name):
    n = len(names)
    specs = [W_SPECS[k][1] for k in names]

    def shard_shape(k):
        shp = list(_half_shape(k))
        shp[W_SPECS[k][1][2]] *= 2
        return tuple(shp)

    def body(*refs):
        half, out = refs[:n], refs[n:2 * n]
        send_sems, recv_sems, local_sems = refs[2 * n:]
        x, y, c = _my_place()
        sibling = (x, y, 1 - c)

        def remote(a, src, dst):
            return pltpu.make_async_remote_copy(src_ref=src, dst_ref=dst, send_sem=send_sems.at[a], recv_sem=recv_sems.at[a],
                                                device_id=sibling, device_id_type=MESH)

        local = [pltpu.make_async_copy(half[a], _half_of(out[a], specs[a], c), local_sems.at[a]) for a in range(n)]
        for cp in local:
            cp.start()
        for a in range(n):
            mine = _half_of(out[a], specs[a], c)
            for src, dst in zip(_pieces(half[a]), _pieces(mine)):
                remote(a, src, dst).start()
        for a in range(n):
            theirs = _half_of(out[a], specs[a], 1 - c)
            remote(a, theirs, theirs).wait_recv()
        for a in range(n):
            remote(a, half[a], half[a]).wait_send()
        for cp in local:
            cp.wait()

    return pl.pallas_call(
        body, name=name,
        out_shape=[jax.ShapeDtypeStruct(shard_shape(k), F32) for k in names],
        in_specs=[VMEM] * n, out_specs=[VMEM] * n,
        scratch_shapes=[pltpu.SemaphoreType.DMA((n,)), pltpu.SemaphoreType.DMA((n,)), pltpu.SemaphoreType.DMA((n,))],
        compiler_params=_params(32),
    )(*halves)


def _mod_parts(c_rows, ada_w, ada_b_cols):
    n_layers, _, n_cols = ada_w.shape

    def body(c_ref, w_ref, b_ref, o_ref):
        o_ref[...] = _mm_f32(_silu(c_ref[...]), w_ref[...]) + b_ref[...]

    return pl.pallas_call(
        body, name="mod_parts", grid=(n_layers,),
        out_shape=jax.ShapeDtypeStruct((n_layers, 16, n_cols), F32),
        in_specs=[pl.BlockSpec((16, D_MODEL), lambda i: (0, 0)),
                  pl.BlockSpec((None, D_MODEL, n_cols), lambda i: (i, 0, 0)),
                  pl.BlockSpec((None, 1, n_cols), lambda i: (i, 0, 0))],
        out_specs=pl.BlockSpec((None, 16, n_cols), lambda i: (i, 0, 0)),
        compiler_params=_params(40),
    )(c_rows, ada_w, ada_b_cols.reshape(n_layers, 1, n_cols))


def _ln_mod_matmul(ctx_tile, xin, nw, shift, scale, w, name):
    n_mod = shift.shape[0]
    tm = ROW_TILE
    skip = n_mod - 1
    rows = xin.shape[0] + skip * tm
    n_sec = w.shape[1] // D_MODEL

    def body(*refs):
        c_ref = refs[0] if skip else None
        x_ref, nw_ref, sh_ref, sc_ref, w_ref, g_ref, h_ref = refs[skip:]
        xv = jnp.where(pl.program_id(0) == 0, c_ref[...], x_ref[...]) if skip else x_ref[...]
        r = lax.rsqrt(jnp.mean(xv * xv, axis=-1, keepdims=True) + EPS)
        h = ((xv * r * nw_ref[...]) * (1.0 + sc_ref[...]) + sh_ref[...]).astype(BF16)
        h_ref[...] = h
        for k in range(n_sec):
            g_ref[k] = jnp.dot(h, w_ref[:, k * D_MODEL:(k + 1) * D_MODEL], preferred_element_type=F32)

    mod_spec = pl.BlockSpec((None, 1, D_MODEL), lambda i: (jnp.minimum(i, n_mod - 1), 0, 0))
    return pl.pallas_call(
        body, name=name, grid=(rows // tm,),
        out_shape=[pltpu.HBM((n_sec, rows, D_MODEL), F32), pltpu.HBM((rows, D_MODEL), BF16)],
        in_specs=[pl.BlockSpec((tm, D_MODEL), lambda i: (0, 0))] * skip
        + [pl.BlockSpec((tm, D_MODEL), lambda i: (jnp.maximum(i - skip, 0), 0)),
           pl.BlockSpec((1, D_MODEL), lambda i: (0, 0)),
           mod_spec, mod_spec,
           pl.BlockSpec((D_MODEL, n_sec * D_MODEL), lambda i: (0, 0))],
        out_specs=[pl.BlockSpec((n_sec, tm, D_MODEL), lambda i: (0, i, 0)),
                   pl.BlockSpec((tm, D_MODEL), lambda i: (i, 0))],
        compiler_params=_params(48),
    )(*_pin(*([ctx_tile] * skip), xin, nw, shift.reshape(n_mod, 1, D_MODEL), scale.reshape(n_mod, 1, D_MODEL), w))


def _chunk_masks(rev):
    rid = lax.broadcasted_iota(jnp.int32, (CHUNK, CHUNK), 0)
    cid = lax.broadcasted_iota(jnp.int32, (CHUNK, CHUNK), 1)
    keep = (cid >= rid) if rev else (cid <= rid)
    keep_t = (cid <= rid) if rev else (cid >= rid)
    one, zero = jnp.ones((CHUNK, CHUNK), F32), jnp.zeros((CHUNK, CHUNK), F32)
    return keep, jnp.where(keep, one, zero).astype(BF16), jnp.where(keep_t, one, zero).astype(BF16)


def _chunk_rows(t, rev, latent):
    n = N_LAT_CHUNKS if latent else N_CTX_CHUNKS
    base = CTX_LEN if latent else 0
    idx = (n - 1 - t) if rev else t
    return pl.multiple_of(base + idx * CHUNK, CHUNK)


def _gates(fpre, lb):
    sg = _sig(fpre)
    f = lb + (1.0 - lb) * sg
    return sg, f, 1.0 - f, jnp.log(f)


G_SPEC = lambda sec: pl.BlockSpec((None, ROWS_ALL, HEAD_DIM), lambda h, sec=sec: (sec, 0, h))


def _gla_forward(g5, lb_logits):
    q_scale = HEAD_DIM ** -0.5

    def body(ff_ref, fb_ref, v_ref, q_ref, lg_ref, o_ref, st_ref, decay_ref, qt_ref):
        for rev in (False, True):
            f_ref = fb_ref if rev else ff_ref
            lb = _sig(lg_ref[1:2, :] if rev else lg_ref[0:1, :])
            keep, tri, _ = _chunk_masks(rev)
            last = 0 if rev else CHUNK - 1
            mid = CHUNK // 2 if rev else CHUNK // 2 - 1

            def local_step(t, carry, latent):
                r0 = _chunk_rows(t, rev, latent)
                rows = pl.ds(r0, CHUNK)
                step = t + (N_CTX_CHUNKS if latent else 0)
                _, _, k, lf = _gates(f_ref[rows, :], lb)
                v = v_ref[rows, :]
                b = _mm_exact_lhs(tri, lf)
                bl = b[last:last + 1, :]
                if latent:
                    q = _silu(q_ref[rows, :]) * q_scale
                    bm = b[mid:mid + 1, :]
                    a = _mm_nt(q * jnp.exp(b - bm), k * jnp.exp(bm - b))
                    o = _mm(jnp.where(keep, a, 0.0), v)
                    orow = pl.ds(pl.multiple_of(r0 - CTX_LEN, CHUNK), CHUNK)
                    qt_ref[orow, :] = (q * jnp.exp(b)).astype(BF16)
                    if rev:
                        o_ref[orow, :] += o
                    else:
                        o_ref[orow, :] = o
                decay_ref[step] = jnp.exp(bl)
                st_ref[step] = _mm_tn(v, k * jnp.exp(bl - b))
                return carry

            lax.fori_loop(0, N_CTX_CHUNKS, functools.partial(local_step, latent=False), 0, unroll=2)
            lax.fori_loop(0, N_LAT_CHUNKS, functools.partial(local_step, latent=True), 0, unroll=4)

            def scan_step(t, st):
                update = st_ref[t]
                st_ref[t] = st
                return st * decay_ref[t] + update

            lax.fori_loop(0, N_CHUNKS, scan_step, jnp.zeros((HEAD_DIM, HEAD_DIM), F32), unroll=2)

            def inter_step(t, carry):
                r0 = _chunk_rows(t, rev, True)
                orow = pl.ds(pl.multiple_of(r0 - CTX_LEN, CHUNK), CHUNK)
                o_ref[orow, :] += lax.dot_general(qt_ref[orow, :], st_ref[t + N_CTX_CHUNKS].astype(BF16),
                                                  (((1,), (1,)), ((), ())), preferred_element_type=F32)
                return carry

            lax.fori_loop(0, N_LAT_CHUNKS, inter_step, 0, unroll=4)

    return pl.pallas_call(
        body, name="gla_forward", grid=(HEADS,),
        out_shape=jax.ShapeDtypeStruct((SEQ, D_MODEL), F32),
        in_specs=[G_SPEC(0), G_SPEC(1), G_SPEC(2), G_SPEC(3), pl.BlockSpec((2, HEAD_DIM), lambda h: (0, h))],
        out_specs=pl.BlockSpec((SEQ, HEAD_DIM), lambda h: (0, h)),
        scratch_shapes=[pltpu.VMEM((N_CHUNKS, HEAD_DIM, HEAD_DIM), F32), pltpu.VMEM((N_CHUNKS, 1, HEAD_DIM), F32),
                        pltpu.VMEM((SEQ, HEAD_DIM), BF16)],
        compiler_params=_params(32),
    )(g5, g5, g5, g5, lb_logits)


def _gla_backward(g5, lb_logits, d_o, d_z):
    q_scale = HEAD_DIM ** -0.5

    def body(ff_ref, fb_ref, v_ref, q_ref, lg_ref, do_ref, dz_ref, dg_ref, dlg_ref, st_ref, dst_ref, decay_ref):
        dg_ref[3, 0:CTX_LEN, :] = jnp.zeros((CTX_LEN, HEAD_DIM), F32)
        dg_ref[4, 0:CTX_LEN, :] = jnp.zeros((CTX_LEN, HEAD_DIM), F32)
        dg_ref[4, CTX_LEN:ROWS_ALL, :] = dz_ref[...]
        is_row = lax.broadcasted_iota(jnp.int32, (CHUNK, HEAD_DIM), 0)
        for rev in (False, True):
            d = 1 if rev else 0
            f_ref = fb_ref if rev else ff_ref
            lb = _sig(lg_ref[d:d + 1, :])
            keep, tri, tri_t = _chunk_masks(rev)
            last = 0 if rev else CHUNK - 1
            mid = CHUNK // 2 if rev else CHUNK // 2 - 1

            def local_step(t, carry, latent):
                r0 = _chunk_rows(t, rev, latent)
                rows = pl.ds(r0, CHUNK)
                step = t + (N_CTX_CHUNKS if latent else 0)
                _, _, k, lf = _gates(f_ref[rows, :], lb)
                b = _mm_exact_lhs(tri, lf)
                bl = b[last:last + 1, :]
                decay_ref[step] = jnp.exp(bl)
                st_ref[step] = _mm_tn(v_ref[rows, :], k * jnp.exp(bl - b))
                if latent:
                    q_t = _silu(q_ref[rows, :]) * q_scale * jnp.exp(b)
                    dst_ref[step] = _mm_tn(do_ref[pl.ds(pl.multiple_of(r0 - CTX_LEN, CHUNK), CHUNK), :], q_t)
                else:
                    dst_ref[step] = jnp.zeros((HEAD_DIM, HEAD_DIM), F32)
                return carry

            lax.fori_loop(0, N_CTX_CHUNKS, functools.partial(local_step, latent=False), 0, unroll=2)
            lax.fori_loop(0, N_LAT_CHUNKS, functools.partial(local_step, latent=True), 0, unroll=4)

            def scan_step(i, carry):
                st, d_st = carry
                j = N_CHUNKS - 1 - i
                update, d_update = st_ref[i], dst_ref[j]
                st_ref[i] = st
                dst_ref[j] = d_st
                return st * decay_ref[i] + update, d_st * decay_ref[j] + d_update

            zero_state = jnp.zeros((HEAD_DIM, HEAD_DIM), F32)
            lax.fori_loop(0, N_CHUNKS, scan_step, (zero_state, zero_state))

            def grad_step(t, dlb, latent):
                r0 = _chunk_rows(t, rev, latent)
                rows = pl.ds(r0, CHUNK)
                step = t + (N_CTX_CHUNKS if latent else 0)
                sg, f, k, lf = _gates(f_ref[rows, :], lb)
                v = v_ref[rows, :]
                b = _mm_exact_lhs(tri, lf)
                bl = b[last:last + 1, :]
                e_end = jnp.exp(bl - b)
                k_end = k * e_end
                decay = jnp.exp(bl)
                d_st = dst_ref[step]
                st_prev = st_ref[step]
                d_kend = _mm(v, d_st)
                d_decay = jnp.sum(d_st * st_prev, axis=0, keepdims=True)
                t_kend = d_kend * k_end
                d_v = _mm_nt(k_end, d_st)
                d_k = d_kend * e_end
                d_b = -t_kend
                if latent:
                    qpre = q_ref[rows, :]
                    q = _silu(qpre) * q_scale
                    bm = b[mid:mid + 1, :]
                    e_b, e_qm, e_km = jnp.exp(b), jnp.exp(b - bm), jnp.exp(bm - b)
                    q_t, q_m, k_m = q * e_b, q * e_qm, k * e_km
                    a = jnp.where(keep, _mm_nt(q_m, k_m), 0.0)
                    d_out = do_ref[pl.ds(pl.multiple_of(r0 - CTX_LEN, CHUNK), CHUNK), :]
                    d_a = jnp.where(keep, _mm_nt(d_out, v), 0.0)
                    d_qm = _mm(d_a, k_m)
                    d_km = _mm_tn(d_a, q_m)
                    d_qt = _mm(d_out, st_prev)
                    d_v = d_v + _mm_tn(a, d_out)
                    d_k = d_k + d_km * e_km
                    d_b = d_b + d_qt * q_t + d_qm * q_m - d_km * k_m
                    d_q = d_qt * e_b + d_qm * e_qm
                at_last = jnp.sum(t_kend, axis=0, keepdims=True) + d_decay * decay
                d_b = d_b + jnp.where(is_row == last, at_last, 0.0)
                d_lf = _mm_exact_lhs(tri_t, d_b)
                d_f = d_lf / f - d_k
                dg_ref[d, rows, :] = d_f * (1.0 - lb) * sg * (1.0 - sg)
                if rev:
                    dg_ref[2, rows, :] += d_v
                else:
                    dg_ref[2, rows, :] = d_v
                if latent:
                    d_qpre = d_q * q_scale * _dsilu(qpre)
                    if rev:
                        dg_ref[3, rows, :] += d_qpre
                    else:
                        dg_ref[3, rows, :] = d_qpre
                return dlb + jnp.sum(d_f * (1.0 - sg), axis=0, keepdims=True)

            dlb = lax.fori_loop(0, N_LAT_CHUNKS, functools.partial(grad_step, latent=True), jnp.zeros((1, HEAD_DIM), F32),
                                unroll=2)
            dlb = lax.fori_loop(0, N_CTX_CHUNKS, functools.partial(grad_step, latent=False), dlb, unroll=2)
            dlg_ref[d:d + 1, :] = dlb * lb * (1.0 - lb)

    col = pl.BlockSpec((SEQ, HEAD_DIM), lambda h: (0, h))
    return pl.pallas_call(
        body, name="gla_backward", grid=(HEADS,),
        out_shape=[jax.ShapeDtypeStruct((HGRN_SECTIONS, ROWS_ALL, D_MODEL), F32), jax.ShapeDtypeStruct((2, D_MODEL), F32)],
        in_specs=[G_SPEC(0), G_SPEC(1), G_SPEC(2), G_SPEC(3), pl.BlockSpec((2, HEAD_DIM), lambda h: (0, h)), col, col],
        out_specs=[pl.BlockSpec((HGRN_SECTIONS, ROWS_ALL, HEAD_DIM), lambda h: (0, 0, h)),
                   pl.BlockSpec((2, HEAD_DIM), lambda h: (0, h))],
        scratch_shapes=[pltpu.VMEM((N_CHUNKS, HEAD_DIM, HEAD_DIM), F32), pltpu.VMEM((N_CHUNKS, HEAD_DIM, HEAD_DIM), F32),
                        pltpu.VMEM((N_CHUNKS, 1, HEAD_DIM), F32)],
        compiler_params=_params(48),
    )(g5, g5, g5, g5, lb_logits, d_o, d_z)


GROUP = 2 * CHUNK


def _group_masks(rev):
    rid = lax.broadcasted_iota(jnp.int32, (GROUP, GROUP), 0)
    cid = lax.broadcasted_iota(jnp.int32, (GROUP, GROUP), 1)
    same = (rid >= CHUNK) == (cid >= CHUNK)
    causal = (cid >= rid) if rev else (cid <= rid)
    anti = (cid <= rid) if rev else (cid >= rid)
    bf = lambda m: jnp.where(m, jnp.ones((GROUP, GROUP), F32), jnp.zeros((GROUP, GROUP), F32)).astype(BF16)
    keep = same & causal
    return keep, bf(keep), bf(same & anti), bf(same & jnp.logical_not(anti))


def _group_sum(m_bf, a):
    hi, lo = _split2(a)
    r = jnp.dot(m_bf, jnp.concatenate([hi, lo], axis=1), preferred_element_type=F32)
    return r[:, :HEAD_DIM] + r[:, HEAD_DIM:]


def _chunk_row(a, pos):
    return jnp.concatenate([jnp.broadcast_to(a[c * CHUNK + pos:c * CHUNK + pos + 1, :], (CHUNK, HEAD_DIM)) for c in range(2)], axis=0)


def _by_chunk(a, second):
    return jnp.concatenate([jnp.where(second, 0.0, a), jnp.where(second, a, 0.0)], axis=1)


def _own_block(r):
    return jnp.concatenate([r[0:CHUNK, 0:HEAD_DIM], r[CHUNK:GROUP, HEAD_DIM:2 * HEAD_DIM]], axis=0)


def _scan_step_of(row_chunk, rev, latent):
    if not rev:
        return row_chunk
    return (N_CHUNKS + N_CTX_CHUNKS - 1 - row_chunk) if latent else (N_CTX_CHUNKS - 1 - row_chunk)


def _group_rows(i, j, per_step, latent):
    base = CTX_LEN if latent else 0
    return pl.multiple_of(base + (i * per_step + j) * GROUP, GROUP)


GROUPS_PER_STEP = 8
GROUPS_PER_GRAD_STEP = 4


def _gla_forward(g5, lb_logits, late_shards):
    q_scale = HEAD_DIM ** -0.5
    per_lat, per_ctx = GROUPS_PER_STEP, min(GROUPS_PER_STEP, CTX_LEN // GROUP)
    n_late = len(GATHER_LATE)

    def body(ff_ref, fb_ref, v_ref, q_ref, lg_ref, *rest):
        shard_refs, o_ref, full_refs = rest[:n_late], rest[n_late], rest[n_late + 1:2 * n_late + 1]
        st_ref, decay_ref, qt_ref = rest[2 * n_late + 1:2 * n_late + 4]
        start_gather, wait_gather = _gather_direct(GATHER_LATE, shard_refs, full_refs, *rest[2 * n_late + 4:])

        @pl.when(pl.program_id(0) == 0)
        def _():
            start_gather()

        second = lax.broadcasted_iota(jnp.int32, (GROUP, HEAD_DIM), 0) >= CHUNK
        for rev in (False, True):
            f_ref = fb_ref if rev else ff_ref
            lb = _sig(lg_ref[1:2, :] if rev else lg_ref[0:1, :])
            keep, tri, _, _ = _group_masks(rev)
            last = 0 if rev else CHUNK - 1
            mid = CHUNK // 2 if rev else CHUNK // 2 - 1

            def local_step(i, carry, latent, per):
                r0s = [_group_rows(i, j, per, latent) for j in range(per)]
                rows = [pl.ds(r0, GROUP) for r0 in r0s]
                gates = [_gates(f_ref[r, :], lb) for r in rows]
                vs = [v_ref[r, :] for r in rows]
                bs = [_group_sum(tri, g[3]) for g in gates]
                bls = [_chunk_row(b, last) for b in bs]
                ups = [_mm_tn(v, _by_chunk(g[2] * jnp.exp(bl - b), second)) for v, g, b, bl in zip(vs, gates, bs, bls)]
                if latent:
                    qs = [_silu(q_ref[r, :]) * q_scale for r in rows]
                    bms = [_chunk_row(b, mid) for b in bs]
                    a_s = [_mm_nt(q * jnp.exp(b - bm), g[2] * jnp.exp(bm - b)) for q, g, b, bm in zip(qs, gates, bs, bms)]
                    outs = [_mm(jnp.where(keep, a, 0.0), v) for a, v in zip(a_s, vs)]
                for j in range(per):
                    for c in range(2):
                        step = _scan_step_of(r0s[j] // CHUNK + c, rev, latent)
                        decay_ref[step] = jnp.exp(bls[j][c * CHUNK:c * CHUNK + 1, :])
                        st_ref[step] = ups[j][:, c * HEAD_DIM:(c + 1) * HEAD_DIM]
                    if latent:
                        orow = pl.ds(pl.multiple_of(r0s[j] - CTX_LEN, GROUP), GROUP)
                        qt_ref[orow, :] = (qs[j] * jnp.exp(bs[j])).astype(BF16)
                        if rev:
                            o_ref[orow, :] += outs[j]
                        else:
                            o_ref[orow, :] = outs[j]
                return carry

            lax.fori_loop(0, CTX_LEN // (per_ctx * GROUP), functools.partial(local_step, latent=False, per=per_ctx), 0)
            lax.fori_loop(0, SEQ // (per_lat * GROUP), functools.partial(local_step, latent=True, per=per_lat), 0)

            def scan_step(t, st):
                update = st_ref[t]
                st_ref[t] = st
                return st * decay_ref[t] + update

            lax.fori_loop(0, N_CHUNKS, scan_step, jnp.zeros((HEAD_DIM, HEAD_DIM), F32), unroll=2)

            def inter_step(i, carry):
                r0s = [_group_rows(i, j, per_lat, True) for j in range(per_lat)]
                orows = [pl.ds(pl.multiple_of(r0 - CTX_LEN, GROUP), GROUP) for r0 in r0s]
                states = [jnp.concatenate([st_ref[_scan_step_of(r0 // CHUNK + c, rev, True)] for c in range(2)], axis=0)
                          for r0 in r0s]
                prods = [lax.dot_general(qt_ref[orow, :], s.astype(BF16), (((1,), (1,)), ((), ())), preferred_element_type=F32)
                         for orow, s in zip(orows, states)]
                for orow, r in zip(orows, prods):
                    o_ref[orow, :] += _own_block(r)
                return carry

            lax.fori_loop(0, SEQ // (per_lat * GROUP), inter_step, 0)

        @pl.when(pl.program_id(0) == HEADS - 1)
        def _():
            wait_gather()

    outs = pl.pallas_call(
        body, name="gla_forward", grid=(HEADS,),
        out_shape=[pltpu.HBM((SEQ, D_MODEL), F32)] + [jax.ShapeDtypeStruct(W_SPECS[k][0], BF16) for k in GATHER_LATE],
        in_specs=[G_SPEC(0), G_SPEC(1), G_SPEC(2), G_SPEC(3), pl.BlockSpec((2, HEAD_DIM), lambda h: (0, h))] + [ANY] * n_late,
        out_specs=[pl.BlockSpec((SEQ, HEAD_DIM), lambda h: (0, h))] + [ANY] * n_late,
        scratch_shapes=[pltpu.VMEM((N_CHUNKS, HEAD_DIM, HEAD_DIM), F32), pltpu.VMEM((N_CHUNKS, 1, HEAD_DIM), F32),
                        pltpu.VMEM((SEQ, HEAD_DIM), BF16)] + _comm_sems(n_late),
        compiler_params=_params(32),
    )(*_pin(g5, g5, g5, g5, lb_logits), *late_shards)
    return outs[0], dict(zip(GATHER_LATE, outs[1:]))


def _gla_backward(g5, lb_logits, d_o, d_z, late_parts):
    q_scale = HEAD_DIM ** -0.5
    per_lat, per_ctx = GROUPS_PER_STEP, min(GROUPS_PER_STEP, CTX_LEN // GROUP)
    n_late = len(GATHER_LATE)

    def body(ff_ref, fb_ref, v_ref, q_ref, lg_ref, do_ref, dz_ref, *rest):
        part_refs, (dg_ref, dlg_ref), slot_refs = rest[:n_late], rest[n_late:n_late + 2], rest[n_late + 2:2 * n_late + 2]
        st_ref, dst_ref, decay_ref = rest[2 * n_late + 2:2 * n_late + 5]
        start_scatter, wait_scatter = _scatter_direct(GATHER_LATE, part_refs, slot_refs, *rest[2 * n_late + 5:])

        @pl.when(pl.program_id(0) == 0)
        def _():
            start_scatter()

        dg_ref[3, 0:CTX_LEN, :] = jnp.zeros((CTX_LEN, HEAD_DIM), F32)
        dg_ref[4, 0:CTX_LEN, :] = jnp.zeros((CTX_LEN, HEAD_DIM), F32)
        dg_ref[4, CTX_LEN:ROWS_ALL, :] = dz_ref[...]
        second = lax.broadcasted_iota(jnp.int32, (GROUP, HEAD_DIM), 0) >= CHUNK
        for rev in (False, True):
            d = 1 if rev else 0
            f_ref = fb_ref if rev else ff_ref
            lb = _sig(lg_ref[d:d + 1, :])
            keep, tri, tri_t, strict = _group_masks(rev)
            last = 0 if rev else CHUNK - 1
            mid = CHUNK // 2 if rev else CHUNK // 2 - 1

            def local_step(i, carry, latent, per):
                r0s = [_group_rows(i, j, per, latent) for j in range(per)]
                rows = [pl.ds(r0, GROUP) for r0 in r0s]
                gates = [_gates(f_ref[r, :], lb) for r in rows]
                bs = [_group_sum(tri, g[3]) for g in gates]
                bls = [_chunk_row(b, last) for b in bs]
                ups = [_mm_tn(v_ref[r, :], _by_chunk(g[2] * jnp.exp(bl - b), second)) for r, g, b, bl in zip(rows, gates, bs, bls)]
                if latent:
                    orows = [pl.ds(pl.multiple_of(r0 - CTX_LEN, GROUP), GROUP) for r0 in r0s]
                    d_ups = [_mm_tn(do_ref[orow, :], _by_chunk(_silu(q_ref[r, :]) * q_scale * jnp.exp(b), second))
                             for orow, r, b in zip(orows, rows, bs)]
                for j in range(per):
                    for c in range(2):
                        step = _scan_step_of(r0s[j] // CHUNK + c, rev, latent)
                        decay_ref[step] = jnp.exp(bls[j][c * CHUNK:c * CHUNK + 1, :])
                        st_ref[step] = ups[j][:, c * HEAD_DIM:(c + 1) * HEAD_DIM]
                        if latent:
                            dst_ref[step] = d_ups[j][:, c * HEAD_DIM:(c + 1) * HEAD_DIM]
                        else:
                            dst_ref[step] = jnp.zeros((HEAD_DIM, HEAD_DIM), F32)
                return carry

            lax.fori_loop(0, CTX_LEN // (per_ctx * GROUP), functools.partial(local_step, latent=False, per=per_ctx), 0)
            lax.fori_loop(0, SEQ // (per_lat * GROUP), functools.partial(local_step, latent=True, per=per_lat), 0)

            def scan_step(i, carry):
                st, d_st = carry
                j = N_CHUNKS - 1 - i
                update, d_update = st_ref[i], dst_ref[j]
                st_ref[i] = st
                dst_ref[j] = d_st
                return st * decay_ref[i] + update, d_st * decay_ref[j] + d_update

            zero_state = jnp.zeros((HEAD_DIM, HEAD_DIM), F32)
            lax.fori_loop(0, N_CHUNKS, scan_step, (zero_state, zero_state))

            def grad_step(i, dlb, latent, per):
                r0s = [_group_rows(i, j, per, latent) for j in range(per)]
                rows = [pl.ds(r0, GROUP) for r0 in r0s]
                gates = [_gates(f_ref[r, :], lb) for r in rows]
                vs = [v_ref[r, :] for r in rows]
                bs = [_group_sum(tri, g[3]) for g in gates]
                bls = [_chunk_row(b, last) for b in bs]
                e_ends = [jnp.exp(bl - b) for b, bl in zip(bs, bls)]
                k_ends = [g[2] * e for g, e in zip(gates, e_ends)]
                sts = [[st_ref[_scan_step_of(r0 // CHUNK + c, rev, latent)] for c in range(2)] for r0 in r0s]
                d_sts = [[dst_ref[_scan_step_of(r0 // CHUNK + c, rev, latent)] for c in range(2)] for r0 in r0s]
                d_kends = [_own_block(_mm(v, jnp.concatenate(ds, axis=1))) for v, ds in zip(vs, d_sts)]
                d_vs = [_own_block(_mm_nt(ke, jnp.concatenate(ds, axis=0))) for ke, ds in zip(k_ends, d_sts)]
                at_last = [jnp.concatenate([jnp.broadcast_to(jnp.sum(ds[c] * s[c], axis=0, keepdims=True), (CHUNK, HEAD_DIM))
                                            for c in range(2)], axis=0) * jnp.exp(bl) for ds, s, bl in zip(d_sts, sts, bls)]
                t_kends = [dk * ke for dk, ke in zip(d_kends, k_ends)]
                d_ks = [dk * e for dk, e in zip(d_kends, e_ends)]
                d_lfs = [_group_sum(strict, t) + al for t, al in zip(t_kends, at_last)]
                if latent:
                    orows = [pl.ds(pl.multiple_of(r0 - CTX_LEN, GROUP), GROUP) for r0 in r0s]
                    qpres = [q_ref[r, :] for r in rows]
                    qs = [_silu(qp) * q_scale for qp in qpres]
                    bms = [_chunk_row(b, mid) for b in bs]
                    e_bs = [jnp.exp(b) for b in bs]
                    e_qms = [jnp.exp(b - bm) for b, bm in zip(bs, bms)]
                    e_kms = [jnp.exp(bm - b) for b, bm in zip(bs, bms)]
                    q_ts = [q * e for q, e in zip(qs, e_bs)]
                    q_ms = [q * e for q, e in zip(qs, e_qms)]
                    k_ms = [g[2] * e for g, e in zip(gates, e_kms)]
                    d_outs = [do_ref[orow, :] for orow in orows]
                    a_s = [jnp.where(keep, _mm_nt(qm, km), 0.0) for qm, km in zip(q_ms, k_ms)]
                    d_as = [jnp.where(keep, _mm_nt(do, v), 0.0) for do, v in zip(d_outs, vs)]
                    d_qts = [_own_block(_mm(do, jnp.concatenate(s, axis=1))) for do, s in zip(d_outs, sts)]
                    d_qms = [_mm(da, km) for da, km in zip(d_as, k_ms)]
                    d_kms = [_mm_tn(da, qm) for da, qm in zip(d_as, q_ms)]
                    d_vs = [dv + _mm_tn(a, do) for dv, a, do in zip(d_vs, a_s, d_outs)]
                    d_ks = [dk + dkm * e for dk, dkm, e in zip(d_ks, d_kms, e_kms)]
                    d_lfs = [dl + _group_sum(tri_t, dqt * qt + dqm * qm - dkm * km)
                             for dl, dqt, qt, dqm, qm, dkm, km in zip(d_lfs, d_qts, q_ts, d_qms, q_ms, d_kms, k_ms)]
                    d_qs = [dqt * eb + dqm * eq for dqt, eb, dqm, eq in zip(d_qts, e_bs, d_qms, e_qms)]
                for j in range(per):
                    sg, f = gates[j][0], gates[j][1]
                    d_f = d_lfs[j] / f - d_ks[j]
                    dg_ref[d, rows[j], :] = d_f * (1.0 - lb) * sg * (1.0 - sg)
                    dlb = dlb + jnp.sum(d_f * (1.0 - sg), axis=0, keepdims=True)
                    if rev:
                        dg_ref[2, rows[j], :] += d_vs[j]
                    else:
                        dg_ref[2, rows[j], :] = d_vs[j]
                    if latent:
                        d_qpre = d_qs[j] * q_scale * _dsilu(qpres[j])
                        if rev:
                            dg_ref[3, rows[j], :] += d_qpre
                        else:
                            dg_ref[3, rows[j], :] = d_qpre
                return dlb

            dlb = lax.fori_loop(0, SEQ // (GROUPS_PER_GRAD_STEP * GROUP),
                                functools.partial(grad_step, latent=True, per=GROUPS_PER_GRAD_STEP), jnp.zeros((1, HEAD_DIM), F32))
            dlb = lax.fori_loop(0, CTX_LEN // (per_ctx * GROUP), functools.partial(grad_step, latent=False, per=per_ctx), dlb)
            dlg_ref[d:d + 1, :] = dlb * lb * (1.0 - lb)

        @pl.when(pl.program_id(0) == HEADS - 1)
        def _():
            wait_scatter()

    col = pl.BlockSpec((SEQ, HEAD_DIM), lambda h: (0, h))
    outs = pl.pallas_call(
        body, name="gla_backward", grid=(HEADS,),
        out_shape=[pltpu.HBM((HGRN_SECTIONS, ROWS_ALL, D_MODEL), F32), jax.ShapeDtypeStruct((2, D_MODEL), F32)]
        + _slot_shapes(GATHER_LATE),
        in_specs=[G_SPEC(0), G_SPEC(1), G_SPEC(2), G_SPEC(3), pl.BlockSpec((2, HEAD_DIM), lambda h: (0, h)), col, col]
        + [ANY] * n_late,
        out_specs=[pl.BlockSpec((HGRN_SECTIONS, ROWS_ALL, HEAD_DIM), lambda h: (0, 0, h)),
                   pl.BlockSpec((2, HEAD_DIM), lambda h: (0, h))] + [ANY] * n_late,
        scratch_shapes=[pltpu.VMEM((N_CHUNKS, HEAD_DIM, HEAD_DIM), F32), pltpu.VMEM((N_CHUNKS, HEAD_DIM, HEAD_DIM), F32),
                        pltpu.VMEM((N_CHUNKS, 1, HEAD_DIM), F32)] + _comm_sems(n_late),
        compiler_params=_params(48),
    )(*_pin(g5, g5, g5, g5, lb_logits, d_o, d_z), *late_parts)
    return outs[0], outs[1], dict(zip(GATHER_LATE, outs[2:]))


def _head_norm(o, gw, scr):
    rs = []
    for h in range(HEADS):
        cols = slice(h * HEAD_DIM, (h + 1) * HEAD_DIM)
        oh = o[:, cols]
        r = lax.rsqrt(jnp.mean(oh * oh, axis=-1, keepdims=True) + EPS)
        scr[:, cols] = oh * r
        rs.append(r)
    return rs


def _hgrn_out_forward(o_raw, g5, xin, gnorm_w, gate, w_out):
    tm = ROW_TILE

    def body(o_ref, z_ref, x_ref, gw_ref, gate_ref, w_ref, x1_ref, res_ref, scr):
        _head_norm(o_ref[...], None, scr)
        a = scr[...] * gw_ref[...] * _silu(z_ref[...])
        res = _mm(a, w_ref[...])
        res_ref[...] = res
        x1_ref[...] = x_ref[...] + gate_ref[...] * res

    tile = pl.BlockSpec((tm, D_MODEL), lambda i: (i, 0))
    vec = pl.BlockSpec((1, D_MODEL), lambda i: (0, 0))
    return pl.pallas_call(
        body, name="hgrn_out_forward", grid=(SEQ // tm,),
        out_shape=[pltpu.HBM((SEQ, D_MODEL), F32)] * 2,
        in_specs=[tile, pl.BlockSpec((None, tm, D_MODEL), lambda i: (4, i + CTX_LEN // tm, 0)), tile, vec, vec,
                  pl.BlockSpec((D_MODEL, D_MODEL), lambda i: (0, 0))],
        out_specs=[tile, tile],
        scratch_shapes=[pltpu.VMEM((tm, D_MODEL), F32)],
        compiler_params=_params(32),
    )(*_pin(o_raw, g5, xin, gnorm_w, gate, w_out))


def _hgrn_out_backward(d_x1, o_raw, g5, res, gnorm_w, gate, w_out):
    tm = ROW_TILE

    def body(dx_ref, o_ref, z_ref, res_ref, gw_ref, gate_ref, w_ref, do_ref, dz_ref, dw_out, dgate_ref, dgw_ref, scr, scr2,
             dw_ref):
        @pl.when(pl.program_id(0) == 0)
        def _():
            dw_ref[...] = jnp.zeros_like(dw_ref)
            dgate_ref[...] = jnp.zeros_like(dgate_ref)
            dgw_ref[...] = jnp.zeros_like(dgw_ref)

        dx = dx_ref[...]
        dgate_ref[...] += jnp.sum(dx * res_ref[...], axis=0, keepdims=True)
        d_res = (dx * gate_ref[...]).astype(BF16)
        d_a = _mm_nt(d_res, w_ref[...])
        rs = _head_norm(o_ref[...], None, scr)
        z = z_ref[...]
        sz = _silu(z)
        o_hat = scr[...]
        o_n = o_hat * gw_ref[...]
        dw_ref[...] += _mm_tn(o_n * sz, d_res)
        d_on = d_a * sz
        dz_ref[...] = d_a * o_n * _dsilu(z)
        dgw_ref[...] += jnp.sum(d_on * o_hat, axis=0, keepdims=True)
        scr2[...] = d_on * gw_ref[...]
        for h in range(HEADS):
            cols = slice(h * HEAD_DIM, (h + 1) * HEAD_DIM)
            dh, oh = scr2[:, cols], scr[:, cols]
            do_ref[:, cols] = rs[h] * (dh - oh * jnp.mean(dh * oh, axis=-1, keepdims=True))

        @pl.when(pl.program_id(0) == SEQ // tm - 1)
        def _():
            dw_out[...] = dw_ref[...].astype(BF16)

    tile = pl.BlockSpec((tm, D_MODEL), lambda i: (i, 0))
    vec = pl.BlockSpec((1, D_MODEL), lambda i: (0, 0))
    mat = pl.BlockSpec((D_MODEL, D_MODEL), lambda i: (0, 0))
    return pl.pallas_call(
        body, name="hgrn_out_backward", grid=(SEQ // tm,),
        out_shape=[pltpu.HBM((SEQ, D_MODEL), F32)] * 2 + [pltpu.HBM((D_MODEL, D_MODEL), BF16)]
        + [jax.ShapeDtypeStruct((1, D_MODEL), F32)] * 2,
        in_specs=[tile, tile, pl.BlockSpec((None, tm, D_MODEL), lambda i: (4, i + CTX_LEN // tm, 0)), tile, vec, vec, mat],
        out_specs=[tile, tile, mat, vec, vec],
        scratch_shapes=[pltpu.VMEM((tm, D_MODEL), F32)] * 2 + [pltpu.VMEM((D_MODEL, D_MODEL), F32)],
        compiler_params=_params(40),
    )(*_pin(d_x1, o_raw, g5, res, gnorm_w, gate, w_out))


def _pool_constants():
    win = np.zeros((POOL_GROUPS, ROW_TILE, ROW_TILE), np.float32)
    inv = np.zeros((POOL_GROUPS, ROW_TILE, 1), np.float32)
    for g, w in enumerate(POOL_WINDOWS):
        for t in range(ROW_TILE):
            base, p = (t // GRID_W) * GRID_W, t % GRID_W
            lo = min(max(p - w // 2, 0), GRID_W)
            hi = min(max(p - w // 2 + w, 0), GRID_W)
            win[g, t, base + lo:base + hi] = 1.0
            inv[g, t, 0] = 1.0 / np.float32(hi - lo)
    return jnp.asarray(win, BF16), jnp.asarray(win.transpose(0, 2, 1), BF16), jnp.asarray(inv, F32)


def _pool_mix(u_ref, wg_ref, ps_ref, win_ref, inv_ref, pooled_scr, yg_scr):
    for g in range(POOL_GROUPS):
        cols = slice(g * POOL_GROUP_DIM, (g + 1) * POOL_GROUP_DIM)
        ug = u_ref[:, cols]
        pooled = _mm_exact_lhs(win_ref[g], ug) * inv_ref[g] - ug
        if pooled_scr is not None:
            pooled_scr[:, cols] = pooled
        yg_scr[:, cols] = _mm(pooled, wg_ref[g])


def _pool_forward_loss(uz, x1, target, gate, w_grp, pool_scale, w_out, final_w):
    tm = ROW_TILE
    win, _, inv = _pool_constants()

    def body(u_ref, z_ref, x_ref, t_ref, gate_ref, wg_ref, ps_ref, w_ref, fw_ref, win_ref, inv_ref,
             dx_ref, loss_ref, dfw_ref, dgate_ref, yg_scr):
        @pl.when(pl.program_id(0) == 0)
        def _():
            loss_ref[...] = jnp.zeros_like(loss_ref)
            dfw_ref[...] = jnp.zeros_like(dfw_ref)
            dgate_ref[...] = jnp.zeros_like(dgate_ref)

        _pool_mix(u_ref, wg_ref, ps_ref, win_ref, inv_ref, None, yg_scr)
        a = yg_scr[...] * ps_ref[...] * _silu(z_ref[...])
        res = _mm(a, w_ref[...])
        x2 = x_ref[...] + gate_ref[...] * res
        r = lax.rsqrt(jnp.mean(x2 * x2, axis=-1, keepdims=True) + EPS)
        xh = x2 * r
        fw = fw_ref[...]
        err = xh * fw - t_ref[...]
        loss_ref[...] += 0.5 * jnp.sum(jnp.mean(err * err, axis=-1, keepdims=True))
        d_y = err * (1.0 / D_MODEL)
        dfw_ref[...] += jnp.sum(d_y * xh, axis=0, keepdims=True)
        d_xh = d_y * fw
        d_x2 = r * (d_xh - xh * jnp.mean(d_xh * xh, axis=-1, keepdims=True))
        dx_ref[...] = d_x2
        dgate_ref[...] += jnp.sum(d_x2 * res, axis=0, keepdims=True)

    tile = pl.BlockSpec((tm, D_MODEL), lambda i: (i, 0))
    vec = pl.BlockSpec((1, D_MODEL), lambda i: (0, 0))
    grp = pl.BlockSpec((POOL_GROUPS, POOL_GROUP_DIM, POOL_GROUP_DIM), lambda i: (0, 0, 0))
    return pl.pallas_call(
        body, name="pool_forward_loss", grid=(SEQ // tm,),
        out_shape=[pltpu.HBM((SEQ, D_MODEL), F32), jax.ShapeDtypeStruct((8, 128), F32),
                   jax.ShapeDtypeStruct((1, D_MODEL), F32), jax.ShapeDtypeStruct((1, D_MODEL), F32)],
        in_specs=[pl.BlockSpec((None, tm, D_MODEL), lambda i: (0, i, 0)), pl.BlockSpec((None, tm, D_MODEL), lambda i: (1, i, 0)),
                  tile, tile, vec, grp, vec, pl.BlockSpec((D_MODEL, D_MODEL), lambda i: (0, 0)), vec, grp,
                  pl.BlockSpec((POOL_GROUPS, ROW_TILE, 1), lambda i: (0, 0, 0))],
        out_specs=[tile, pl.BlockSpec((8, 128), lambda i: (0, 0)), vec, vec],
        scratch_shapes=[pltpu.VMEM((tm, D_MODEL), F32)],
        compiler_params=_params(32),
    )(*_pin(uz, uz, x1, target, gate, w_grp, pool_scale, w_out, final_w, win, inv))


def _pool_backward(d_x2, uz, gate, w_grp, pool_scale, w_out):
    tm = ROW_TILE
    win, win_t, inv = _pool_constants()

    def body(dx_ref, u_ref, z_ref, gate_ref, wg_ref, ps_ref, w_ref, win_ref, wint_ref, inv_ref,
             duz_ref, dw_out, dwg_out, dps_ref, pooled_scr, yg_scr, dyg_scr, dw_ref, dwg_ref):
        @pl.when(pl.program_id(0) == 0)
        def _():
            dw_ref[...] = jnp.zeros_like(dw_ref)
            dwg_ref[...] = jnp.zeros_like(dwg_ref)
            dps_ref[...] = jnp.zeros_like(dps_ref)

        _pool_mix(u_ref, wg_ref, ps_ref, win_ref, inv_ref, pooled_scr, yg_scr)
        z = z_ref[...]
        sz = _silu(z)
        yg = yg_scr[...]
        y = yg * ps_ref[...]
        d_res = (dx_ref[...] * gate_ref[...]).astype(BF16)
        d_a = _mm_nt(d_res, w_ref[...])
        dw_ref[...] += _mm_tn(y * sz, d_res)
        d_y = d_a * sz
        duz_ref[1] = d_a * y * _dsilu(z)
        dps_ref[...] += jnp.sum(d_y * yg, axis=0, keepdims=True)
        dyg_scr[...] = d_y * ps_ref[...]
        for g in range(POOL_GROUPS):
            cols = slice(g * POOL_GROUP_DIM, (g + 1) * POOL_GROUP_DIM)
            d_yg = dyg_scr[:, cols].astype(BF16)
            d_pool = _mm_nt(d_yg, wg_ref[g])
            dwg_ref[g] += _mm_tn(pooled_scr[:, cols], d_yg)
            duz_ref[0, :, cols] = _mm_exact_lhs(wint_ref[g], d_pool * inv_ref[g]) - d_pool

        @pl.when(pl.program_id(0) == SEQ // tm - 1)
        def _():
            dw_out[...] = dw_ref[...].astype(BF16)
            dwg_out[...] = dwg_ref[...].astype(BF16)

    tile = pl.BlockSpec((tm, D_MODEL), lambda i: (i, 0))
    vec = pl.BlockSpec((1, D_MODEL), lambda i: (0, 0))
    mat = pl.BlockSpec((D_MODEL, D_MODEL), lambda i: (0, 0))
    grp = pl.BlockSpec((POOL_GROUPS, POOL_GROUP_DIM, POOL_GROUP_DIM), lambda i: (0, 0, 0))
    return pl.pallas_call(
        body, name="pool_backward", grid=(SEQ // tm,),
        out_shape=[pltpu.HBM((POOL_SECTIONS, SEQ, D_MODEL), F32), pltpu.HBM((D_MODEL, D_MODEL), BF16),
                   pltpu.HBM((POOL_GROUPS, POOL_GROUP_DIM, POOL_GROUP_DIM), BF16), jax.ShapeDtypeStruct((1, D_MODEL), F32)],
        in_specs=[tile, pl.BlockSpec((None, tm, D_MODEL), lambda i: (0, i, 0)), pl.BlockSpec((None, tm, D_MODEL), lambda i: (1, i, 0)),
                  vec, grp, vec, mat, grp, grp, pl.BlockSpec((POOL_GROUPS, ROW_TILE, 1), lambda i: (0, 0, 0))],
        out_specs=[pl.BlockSpec((POOL_SECTIONS, tm, D_MODEL), lambda i: (0, i, 0)), mat, grp, vec],
        scratch_shapes=[pltpu.VMEM((tm, D_MODEL), F32)] * 3 + [pltpu.VMEM((D_MODEL, D_MODEL), F32),
                                                               pltpu.VMEM((POOL_GROUPS, POOL_GROUP_DIM, POOL_GROUP_DIM), F32)],
        compiler_params=_params(40),
    )(*_pin(d_x2, uz, uz, gate, w_grp, pool_scale, w_out, win, win_t, inv))


def _ln_mod_backward(d_g, w, ctx_tile, xin, nw, scale, d_up, name):
    n_sec, rows, _ = d_g.shape
    n_mod = scale.shape[0]
    tm = ROW_TILE
    n_tiles = rows // tm
    skip = n_mod - 1

    def body(dg_ref, w_ref, *refs):
        c_ref = refs[0] if skip else None
        x_ref, nw_ref, sc_ref, up_ref, dx_ref, dnw_ref, dmod_ref = refs[skip:]
        i = pl.program_id(0)

        @pl.when(i == 0)
        def _():
            dnw_ref[...] = jnp.zeros_like(dnw_ref)

        @pl.when((i == 0) | (i == skip))
        def _():
            dmod_ref[...] = jnp.zeros_like(dmod_ref)

        d_h = _mm_nt(dg_ref[0], w_ref[:, 0:D_MODEL])
        for k in range(1, n_sec):
            d_h = d_h + _mm_nt(dg_ref[k], w_ref[:, k * D_MODEL:(k + 1) * D_MODEL])
        xv = jnp.where(i == 0, c_ref[...], x_ref[...]) if skip else x_ref[...]
        r = lax.rsqrt(jnp.mean(xv * xv, axis=-1, keepdims=True) + EPS)
        xh = xv * r
        nw_row = nw_ref[...]
        dmod_ref[0:1, :] += jnp.sum(d_h, axis=0, keepdims=True)
        dmod_ref[1:2, :] += jnp.sum(d_h * (xh * nw_row), axis=0, keepdims=True)
        d_xn = d_h * (1.0 + sc_ref[...])
        dnw_ref[...] += jnp.sum(d_xn * xh, axis=0, keepdims=True)
        d_xh = d_xn * nw_row

        @pl.when(i >= skip)
        def _():
            dx_ref[...] = up_ref[...] + r * (d_xh - xh * jnp.mean(d_xh * xh, axis=-1, keepdims=True))

    lat = lambda i: (jnp.maximum(i - skip, 0), 0)
    mod_idx = lambda i: (jnp.minimum(i, n_mod - 1), 0, 0)
    return pl.pallas_call(
        body, name=name, grid=(n_tiles,),
        out_shape=[pltpu.HBM((rows - skip * tm, D_MODEL), F32), jax.ShapeDtypeStruct((1, D_MODEL), F32),
                   jax.ShapeDtypeStruct((n_mod, 8, D_MODEL), F32)],
        in_specs=[pl.BlockSpec((n_sec, tm, D_MODEL), lambda i: (0, i, 0)),
                  pl.BlockSpec((D_MODEL, n_sec * D_MODEL), lambda i: (0, 0))]
        + [pl.BlockSpec((tm, D_MODEL), lambda i: (0, 0))] * skip
        + [pl.BlockSpec((tm, D_MODEL), lat),
           pl.BlockSpec((1, D_MODEL), lambda i: (0, 0)),
           pl.BlockSpec((None, 1, D_MODEL), mod_idx),
           pl.BlockSpec((tm, D_MODEL), lat)],
        out_specs=[pl.BlockSpec((tm, D_MODEL), lat), pl.BlockSpec((1, D_MODEL), lambda i: (0, 0)),
                   pl.BlockSpec((None, 8, D_MODEL), mod_idx)],
        compiler_params=_params(48),
    )(*_pin(d_g, w, *([ctx_tile] * skip), xin, nw, scale.reshape(n_mod, 1, D_MODEL), d_up))


def _weight_grad(h, d_g, name):
    n_sec, rows, _ = d_g.shape
    tm = 768 if rows % 768 == 0 else 512
    n_tiles = rows // tm

    def body(h_ref, dg_ref, dw_ref, acc):
        i = pl.program_id(1)
        prod = _mm_tn(h_ref[...], dg_ref[...])

        @pl.when(i == 0)
        def _():
            acc[...] = prod

        @pl.when((i > 0) & (i < n_tiles - 1))
        def _():
            acc[...] += prod

        @pl.when(i == n_tiles - 1)
        def _():
            dw_ref[...] = (acc[...] + prod).astype(BF16)

    return pl.pallas_call(
        body, name=name, grid=(n_sec, n_tiles),
        out_shape=pltpu.HBM((D_MODEL, n_sec * D_MODEL), BF16),
        in_specs=[pl.BlockSpec((tm, D_MODEL), lambda j, i: (i, 0)), pl.BlockSpec((None, tm, D_MODEL), lambda j, i: (j, i, 0))],
        out_specs=pl.BlockSpec((D_MODEL, D_MODEL), lambda j, i: (0, j)),
        scratch_shapes=[pltpu.VMEM((D_MODEL, D_MODEL), F32)],
        compiler_params=_params(32),
    )(*_pin(h, d_g))


def _sum_slots(slots, name):
    _, rows, cols = slots.shape
    tr = 128

    def body(s_ref, o_ref):
        acc = s_ref[0].astype(F32)
        for d in range(1, N_DEV):
            acc = acc + s_ref[d].astype(F32)
        o_ref[...] = acc

    return pl.pallas_call(
        body, name=name, grid=(rows // tr,),
        out_shape=jax.ShapeDtypeStruct((rows, cols), F32),
        in_specs=[pl.BlockSpec((N_DEV, tr, cols), lambda i: (0, i, 0))],
        out_specs=pl.BlockSpec((tr, cols), lambda i: (i, 0)),
    )(*_pin(slots))


def _adamw_math(w, g, m, v):
    m = ADAM_B1 * m + (1.0 - ADAM_B1) * g
    v = ADAM_B2 * v + (1.0 - ADAM_B2) * (g * g)
    m_hat = m / (1.0 - ADAM_B1 ** ADAM_STEP)
    v_hat = v / (1.0 - ADAM_B2 ** ADAM_STEP)
    return -ADAM_LR * (m_hat / (jnp.sqrt(v_hat) + ADAM_EPS) + ADAM_WD * w), m, v


def _adamw(w, g, m, v, name):
    rows, cols = w.shape
    tr = rows if rows <= 128 else 128

    def body(w_ref, g_ref, m_ref, v_ref, d_ref, mo_ref, vo_ref):
        d_ref[...], mo_ref[...], vo_ref[...] = _adamw_math(w_ref[...], g_ref[...], m_ref[...], v_ref[...])

    tile = pl.BlockSpec((tr, cols), lambda i: (i, 0))
    return pl.pallas_call(
        body, name=name, grid=(rows // tr,),
        out_shape=[pltpu.HBM((rows, cols), F32)] * 3,
        in_specs=[tile] * 4, out_specs=[tile] * 3,
    )(*_pin(w, g, m, v))


def _sum_devices(gathered):
    def body(p_ref, o_ref):
        acc = p_ref[0:SMALL_ROWS, :]
        for d in range(1, N_DEV):
            acc = acc + p_ref[d * SMALL_ROWS:(d + 1) * SMALL_ROWS, :]
        o_ref[...] = acc

    return pl.pallas_call(body, name="sum_devices", out_shape=jax.ShapeDtypeStruct((SMALL_ROWS, D_MODEL), F32),
                          in_specs=[VMEM], out_specs=VMEM)(gathered)


def _ada_update(cond_t, d_mod, ada_w, m, v):
    n_layers, _, n_cols = ada_w.shape
    tr = ROW_TILE

    def body(c_ref, dm_ref, w_ref, m_ref, v_ref, g_ref, d_ref, mo_ref, vo_ref):
        g = _mm_f32(_silu(c_ref[...]), dm_ref[...])
        g_ref[...] = g
        d_ref[...], mo_ref[...], vo_ref[...] = _adamw_math(w_ref[...], g, m_ref[...], v_ref[...])

    tile = pl.BlockSpec((None, tr, n_cols), lambda l, i: (l, i, 0))
    return pl.pallas_call(
        body, name="ada_update", grid=(n_layers, D_MODEL // tr),
        out_shape=[pltpu.HBM(ada_w.shape, F32)] * 4,
        in_specs=[pl.BlockSpec((tr, 16), lambda l, i: (i, 0)), pl.BlockSpec((None, 16, n_cols), lambda l, i: (l, 0, 0)),
                  tile, tile, tile],
        out_specs=[tile] * 4,
    )(*_pin(cond_t, d_mod, ada_w, m, v))


def _cond_ctx_partial(d_modc, ada_w0):
    n_cols = ada_w0.shape[1]
    tr = ROW_TILE

    def body(dm_ref, w_ref, o_ref):
        o_ref[...] = jnp.sum(w_ref[...] * dm_ref[...], axis=-1, keepdims=True)

    return pl.pallas_call(
        body, name="cond_ctx_partial", grid=(D_MODEL // tr,),
        out_shape=jax.ShapeDtypeStruct((D_MODEL, 1), F32),
        in_specs=[pl.BlockSpec((1, n_cols), lambda i: (0, 0)), pl.BlockSpec((tr, n_cols), lambda i: (i, 0))],
        out_specs=pl.BlockSpec((tr, 1), lambda i: (i, 0)),
    )(d_modc, ada_w0)


def _cond_ctx_update(gathered, c_ctx, m, v):
    def body(p_ref, w_ref, m_ref, v_ref, g_ref, d_ref, mo_ref, vo_ref):
        acc = p_ref[0:1, :]
        for s in range(1, N_CHIPS):
            acc = acc + p_ref[16 * s:16 * s + 1, :]
        w = w_ref[...]
        g = acc * _dsilu(w)
        g_ref[...] = g
        d_ref[...], mo_ref[...], vo_ref[...] = _adamw_math(w, g, m_ref[...], v_ref[...])

    return pl.pallas_call(body, name="cond_ctx_update", out_shape=[jax.ShapeDtypeStruct((1, D_MODEL), F32)] * 4,
                          in_specs=[VMEM] * 4, out_specs=[VMEM] * 4)(gathered, c_ctx, m, v)


def _local_step(x2, ctx2, target, mod_mine, mod_ctx, lb_logits, scale_full, w_in_full, late_shards, norm_w, gnorm, final_w):
    row = lambda a: a.reshape(1, -1)
    shift0, scale0, gate0 = (row(a) for a in jnp.split(mod_mine[0], 3))
    shift1, scale1, gate1 = (row(a) for a in jnp.split(mod_mine[1], 3))
    shift_c, scale_c, _ = (row(a) for a in jnp.split(mod_ctx, 3))
    nw0, nw1 = norm_w[0:1], norm_w[1:2]
    scales0 = jnp.concatenate([scale_c, scale0])

    g5, h0 = _ln_mod_matmul(ctx2, x2, nw0, jnp.concatenate([shift_c, shift0]), scales0, w_in_full, "hgrn_in_forward")
    o_raw, full = _gla_forward(g5, lb_logits, late_shards)
    x1, res0 = _hgrn_out_forward(o_raw, g5, x2, gnorm, gate0, full["hgrn_w_out"])
    uz, h1 = _ln_mod_matmul(None, x1, nw1, shift1, scale1, full["pool_w_in"], "pool_in_forward")
    d_x2, loss_part, d_final, d_gate1 = _pool_forward_loss(uz, x1, target, gate1, full["pool_w_grp"], scale_full,
                                                           full["pool_w_out"], final_w)

    d_uz, dw_pool_out, dw_pool_grp, d_pscale = _pool_backward(d_x2, uz, gate1, full["pool_w_grp"], scale_full, full["pool_w_out"])
    d_x1, d_nw1, d_mod1 = _ln_mod_backward(d_uz, full["pool_w_in"], None, x1, nw1, scale1, d_x2, "pool_in_backward")
    dw_pool_in = _weight_grad(h1, d_uz, "pool_in_weight_grad")
    d_o, d_z, dw_hgrn_out, d_gate0, d_gnorm = _hgrn_out_backward(d_x1, o_raw, g5, res0, gnorm, gate0, full["hgrn_w_out"])
    late_grads = {"hgrn_w_out": dw_hgrn_out, "pool_w_in": dw_pool_in, "pool_w_grp": dw_pool_grp, "pool_w_out": dw_pool_out}
    d_g5, d_lb, late_slots = _gla_backward(g5, lb_logits, d_o, d_z, [late_grads[k].astype(BF16) for k in GATHER_LATE])
    dw_hgrn_in = _weight_grad(h0, d_g5, "hgrn_in_weight_grad")
    key = GATHER_EARLY[0]
    half = _half_shape(key)
    xi, yi, ci = _my_place()
    own = lax.dynamic_slice(dw_hgrn_in, (ci * half[0], (2 * xi + yi) * half[1]), half)
    slots0 = lax.dynamic_update_slice(jnp.zeros((N_DEV,) + half, BF16), own[None], (4 * xi + 2 * yi + ci, 0, 0))
    send_sem, recv_sem, part_thru, slots_thru, token = _scatter_start(dw_hgrn_in, slots0, own, key, 0)
    d_x, d_nw0, d_mod0 = _ln_mod_backward(d_g5, w_in_full, ctx2, x2, nw0 + token[0:1, 0:1], scales0, d_x1, "hgrn_in_backward")
    slots = dict(late_slots)
    pending = (send_sem, recv_sem, part_thru, slots_thru)

    zero = jnp.zeros((1, D_MODEL), F32)
    small = jnp.concatenate([d_mod0[1, 0:2], d_gate0, d_mod1[0, 0:2], d_gate1, d_mod0[0, 0:2], zero, d_nw0, d_nw1, d_gnorm,
                             d_final, d_pscale, d_lb, jnp.broadcast_to(loss_part[0:1, 0:1], (1, D_MODEL)),
                             jnp.zeros((SMALL_ROWS - 17, D_MODEL), F32)], axis=0)
    return {"d_x": d_x, "slots": slots, "pending": pending, "small": small}


def kernel(x, c, ctx, c_ctx, ada_w, ada_b, norm_w, hgrn_w_in, hgrn_lb_logits, hgrn_gnorm_w, hgrn_w_out, pool_w_in, pool_w_grp, pool_scale, pool_w_out, final_norm_w, loss_target, m_c_ctx, m_ada_w, m_ada_b, m_norm_w, m_hgrn_w_in, m_hgrn_lb_logits, m_hgrn_gnorm_w, m_hgrn_w_out, m_pool_w_in, m_pool_w_grp, m_pool_scale, m_pool_w_out, m_final_norm_w, v_c_ctx, v_ada_w, v_ada_b, v_norm_w, v_hgrn_w_in, v_hgrn_lb_logits, v_hgrn_gnorm_w, v_hgrn_w_out, v_pool_w_in, v_pool_w_grp, v_pool_scale, v_pool_w_out, v_final_norm_w):
    xi, yi, ci = _my_place()
    chip = 2 * xi + yi
    dev = 4 * xi + 2 * yi + ci
    ada_cols = ada_w.shape[2]
    lb_cols = hgrn_lb_logits.shape[2]
    ps_cols = pool_scale.shape[1]
    row = lambda a: a.reshape(1, -1)

    def chip_cols(a, n):
        return lax.dynamic_slice_in_dim(a, chip * n, n, axis=a.ndim - 1)

    def from_chips(g, rows_per_dev, take):
        return jnp.concatenate([g[2 * s * rows_per_dev:2 * s * rows_per_dev + take] for s in range(N_CHIPS)], axis=1)

    first = jnp.concatenate([jnp.broadcast_to(c, (8, D_MODEL)), jnp.pad(hgrn_lb_logits[0], ((0, 6), (0, 0))),
                             jnp.pad(pool_scale, ((0, 7), (0, 0)))], axis=1)
    first_all = _all_gather_small(first, "gather_cond")
    cond_all = first_all[::8, :D_MODEL]
    lb_logits = from_chips(first_all[:, D_MODEL:D_MODEL + lb_cols], 8, 2)
    scale_full = from_chips(first_all[:, D_MODEL + lb_cols:], 8, 1)
    cond_rows = jnp.concatenate([cond_all, row(c_ctx), jnp.zeros((7, D_MODEL), F32)], axis=0)

    parts = _mod_parts(cond_rows, ada_w, chip_cols(ada_b, ada_cols))
    parts_all = _all_gather_small(parts.reshape(32, ada_cols), "gather_mod")
    mod_all = from_chips(parts_all, 32, 32).reshape(2, 16, 3 * D_MODEL)
    mod_mine = lax.dynamic_index_in_dim(mod_all, dev, axis=1, keepdims=False)

    shards = {"hgrn_w_in": hgrn_w_in[0], "hgrn_w_out": hgrn_w_out[0], "pool_w_in": pool_w_in[0],
              "pool_w_grp": pool_w_grp[0], "pool_w_out": pool_w_out[0]}
    w_in_full = _all_gather_weights([shards[k].astype(BF16) for k in GATHER_EARLY], GATHER_EARLY)[0]

    loc = _local_step(x[0], ctx[0], loss_target[0], mod_mine, mod_all[0, 8], lb_logits, scale_full, w_in_full,
                      [shards[k].astype(BF16) for k in GATHER_LATE], norm_w, hgrn_gnorm_w, row(final_norm_w))

    small_all = _all_gather_small(loc["small"], "gather_small")
    sums = _sum_devices(small_all)
    loss = sums[16, 0]

    part_c = _cond_ctx_partial(chip_cols(sums[6:9].reshape(1, -1), ada_cols), ada_w[0])
    part_all = _all_gather_small(jnp.pad(part_c.reshape(1, D_MODEL), ((0, 7), (0, 0))), "gather_cond_ctx")
    key = GATHER_EARLY[0]
    send0, recv0, part_thru, slots_thru = loc["pending"]
    send1, recv1, part_thru, slots_thru, _ = _scatter_start(part_thru, slots_thru, part_all, key, 1)

    def reduce_scattered(slots, names, name):
        halves = []
        for k in names:
            hs = _half_shape(k)
            halves.append(_sum_slots(slots[k].reshape(N_DEV, -1, hs[-1]), "sum_" + k).reshape(hs))
        return dict(zip(names, _exchange_halves(halves, names, name)))

    big_grads = reduce_scattered(loc["slots"], GATHER_LATE, "exchange_halves_late")

    out = {}

    def update(name, w, g, m, v):
        shape = w.shape
        w2, g2, m2, v2 = (a.reshape(-1, shape[-1]) for a in (w, g, m, v))
        d, mn, vn = _adamw(w2, g2, m2, v2, "adamw_" + name)
        out[name] = tuple(a.reshape(shape) for a in (g2, d, mn, vn))

    moments = {"hgrn_w_in": (m_hgrn_w_in, v_hgrn_w_in), "hgrn_w_out": (m_hgrn_w_out, v_hgrn_w_out),
               "pool_w_in": (m_pool_w_in, v_pool_w_in), "pool_w_grp": (m_pool_w_grp, v_pool_w_grp),
               "pool_w_out": (m_pool_w_out, v_pool_w_out)}
    weights = {"hgrn_w_in": hgrn_w_in, "hgrn_w_out": hgrn_w_out, "pool_w_in": pool_w_in, "pool_w_grp": pool_w_grp,
               "pool_w_out": pool_w_out}
    for k in GATHER_LATE:
        update(k, weights[k], big_grads[k], *moments[k])

    g_ada_b = jnp.stack([(sums[0:3] + sums[6:9]).reshape(-1), sums[3:6].reshape(-1)])
    update("ada_b", ada_b, g_ada_b, m_ada_b, v_ada_b)
    update("norm_w", norm_w, sums[9:11], m_norm_w, v_norm_w)
    update("hgrn_gnorm_w", hgrn_gnorm_w, sums[11:12], m_hgrn_gnorm_w, v_hgrn_gnorm_w)
    update("final_norm_w", row(final_norm_w), sums[12:13], row(m_final_norm_w), row(v_final_norm_w))
    update("pool_scale", pool_scale, chip_cols(sums[13:14], ps_cols), m_pool_scale, v_pool_scale)
    update("hgrn_lb_logits", hgrn_lb_logits, chip_cols(sums[14:16], lb_cols)[None], m_hgrn_lb_logits, v_hgrn_lb_logits)

    per_dev = small_all.reshape(N_DEV, SMALL_ROWS, D_MODEL)
    pad7 = jnp.zeros((7, 3 * D_MODEL), F32)
    dm0 = jnp.concatenate([per_dev[:, 0:3].reshape(N_DEV, -1), sums[6:9].reshape(1, -1), pad7], axis=0)
    dm1 = jnp.concatenate([per_dev[:, 3:6].reshape(N_DEV, -1), jnp.zeros((8, 3 * D_MODEL), F32)], axis=0)
    d_mod = chip_cols(jnp.stack([dm0, dm1]), ada_cols)
    out["ada_w"] = _ada_update(cond_rows.T, d_mod, ada_w, m_ada_w, v_ada_w)

    g_c, d_c, m_c, v_c = _cond_ctx_update(part_all, row(c_ctx), row(m_c_ctx), row(v_c_ctx))
    out["c_ctx"] = tuple(a.reshape(-1) for a in (g_c, d_c, m_c, v_c))
    out["final_norm_w"] = tuple(a.reshape(-1) for a in out["final_norm_w"])

    done = [g_c, out["ada_w"][1]] + [out[k][1] for k in GATHER_LATE]
    part_thru, slots_thru = _scatter_wait(send0, recv0, part_thru, slots_thru, done, key, 0)
    _, early = _scatter_wait(send1, recv1, part_thru, slots_thru, done, key, 1)
    big_grads = reduce_scattered({key: early}, GATHER_EARLY, "exchange_halves_early")
    for k in GATHER_EARLY:
        update(k, weights[k], big_grads[k], *moments[k])

    names = ["c_ctx", "ada_w", "ada_b", "norm_w", "hgrn_w_in", "hgrn_lb_logits", "hgrn_gnorm_w", "hgrn_w_out", "pool_w_in",
             "pool_w_grp", "pool_scale", "pool_w_out", "final_norm_w"]
    return (loss, loc["d_x"][None], *[out[k][0] for k in names], *[out[k][1] for k in names], *[out[k][2] for k in names],
            *[out[k][3] for k in names])
```

```python
import functools

import numpy as np
import jax
import jax.numpy as jnp
from jax import lax
from jax.experimental import pallas as pl
from jax.experimental.pallas import tpu as pltpu

F32 = jnp.float32
BF16 = jnp.bfloat16

D_MODEL = 1024
SEQ = 2048
CTX_LEN = 256
ROWS_ALL = CTX_LEN + SEQ
HEADS = 8
HEAD_DIM = 128
CHUNK = 64
N_CTX_CHUNKS = CTX_LEN // CHUNK
N_LAT_CHUNKS = SEQ // CHUNK
N_CHUNKS = N_CTX_CHUNKS + N_LAT_CHUNKS
GRID_W = 64
POOL_WINDOWS = (2, 4, 8, 16)
POOL_GROUPS = 4
POOL_GROUP_DIM = 256
HGRN_SECTIONS = 5
POOL_SECTIONS = 2
EPS = 1e-6
N_DEV = 8
N_CHIPS = 4
ROW_TILE = 256
SMALL_ROWS = 24

ADAM_LR = 0.001
ADAM_B1 = 0.9
ADAM_B2 = 0.999
ADAM_EPS = 1e-08
ADAM_WD = 0.01
ADAM_STEP = 10

MESH = pl.DeviceIdType.MESH
MIB = 1 << 20
ANY = pl.BlockSpec(memory_space=pl.ANY)
VMEM = pl.BlockSpec(memory_space=pltpu.VMEM)


def _params(vmem_mib=None):
    if vmem_mib is None:
        return pltpu.CompilerParams()
    return pltpu.CompilerParams(vmem_limit_bytes=vmem_mib * MIB)


def _pin(*operands):
    return [pltpu.with_memory_space_constraint(a, pltpu.HBM) if a.size * a.dtype.itemsize >= MIB else a for a in operands]


def _sig(a):
    return 1.0 / (1.0 + jnp.exp(-a))


def _silu(a):
    return a * _sig(a)


def _dsilu(a):
    s = _sig(a)
    return s * (1.0 + a * (1.0 - s))


def _mm(a, b):
    return jnp.dot(a.astype(BF16), b.astype(BF16), preferred_element_type=F32)


def _mm_nt(a, b):
    return lax.dot_general(a.astype(BF16), b.astype(BF16), (((1,), (1,)), ((), ())), preferred_element_type=F32)


def _mm_tn(a, b):
    return lax.dot_general(a.astype(BF16), b.astype(BF16), (((0,), (0,)), ((), ())), preferred_element_type=F32)


def _split2(a):
    hi = a.astype(BF16)
    lo = (a - hi.astype(F32)).astype(BF16)
    return hi, lo


def _mm_exact_lhs(m_bf, a):
    hi, lo = _split2(a)
    return jnp.dot(m_bf, hi, preferred_element_type=F32) + jnp.dot(m_bf, lo, preferred_element_type=F32)


def _mm_f32(a, b):
    ah, al = _split2(a)
    bh, bl = _split2(b)
    return (jnp.dot(ah, bh, preferred_element_type=F32) + jnp.dot(al, bh, preferred_element_type=F32)
            + jnp.dot(ah, bl, preferred_element_type=F32))


def _my_place():
    return lax.axis_index("x"), lax.axis_index("y"), lax.axis_index("c")


def _all_gather_small(blk, name):
    m_per, n = blk.shape

    def body(x_ref, out_ref, send_sems, recv_sems, local_sem):
        x, y, c = _my_place()
        me, sibling = (x, y, c), (x, y, 1 - c)
        chips = [(1 - x, y), (x, 1 - y), (1 - x, 1 - y)]

        def rows(px, py, pc):
            return out_ref.at[pl.ds((4 * px + 2 * py + pc) * m_per, m_per), :]

        def copy(k, block, to, src=None):
            return pltpu.make_async_remote_copy(
                src_ref=rows(*block) if src is None else src, dst_ref=rows(*block),
                send_sem=send_sems.at[k], recv_sem=recv_sems.at[k], device_id=to, device_id_type=MESH)

        mine = pltpu.make_async_copy(x_ref, rows(*me), local_sem)
        mine.start()
        first = [copy(0, me, sibling, src=x_ref)]
        first += [copy(1 + j, me, (*chip, c), src=x_ref) for j, chip in enumerate(chips)]
        for cp in first:
            cp.start()
        passed = [copy(4 + j, (*chip, c), sibling) for j, chip in enumerate(chips)]
        for j, chip in enumerate(chips):
            copy(1 + j, (*chip, c), me).wait_recv()
            passed[j].start()
        copy(0, sibling, me).wait_recv()
        for j, chip in enumerate(chips):
            copy(4 + j, (*chip, 1 - c), me).wait_recv()
        for cp in first + passed:
            cp.wait_send()
        mine.wait()

    return pl.pallas_call(
        body, name=name,
        out_shape=jax.ShapeDtypeStruct((N_DEV * m_per, n), blk.dtype),
        in_specs=[VMEM], out_specs=VMEM,
        scratch_shapes=[pltpu.SemaphoreType.DMA((7,)), pltpu.SemaphoreType.DMA((7,)), pltpu.SemaphoreType.DMA],
    )(blk)


W_SPECS = {
    "hgrn_w_in": ((D_MODEL, 5 * D_MODEL), (1, 1280, 0, 512)),
    "hgrn_w_out": ((D_MODEL, D_MODEL), (0, 256, 0, 128)),
    "pool_w_in": ((D_MODEL, 2 * D_MODEL), (1, 512, 0, 512)),
    "pool_w_grp": ((POOL_GROUPS, POOL_GROUP_DIM, POOL_GROUP_DIM), (1, 64, 1, 32)),
    "pool_w_out": ((D_MODEL, D_MODEL), (0, 256, 0, 128)),
}
W_NAMES = tuple(W_SPECS)


def _al(v, m):
    return pl.multiple_of(v, m)


def _region(ref, spec, chip, half):
    ca, cn, ha, hn = spec
    idx = [slice(None)] * len(ref.shape)
    if ca == ha:
        if half is None:
            idx[ca] = pl.ds(_al(chip * cn, cn), cn)
        else:
            idx[ca] = pl.ds(_al(chip * cn + half * hn, hn), hn)
    else:
        idx[ca] = pl.ds(_al(chip * cn, cn), cn)
        if half is not None:
            idx[ha] = pl.ds(_al(half * hn, hn), hn)
    return ref.at[tuple(idx)]


def _half_of(ref, spec, half):
    _, _, ha, hn = spec
    idx = [slice(None)] * len(ref.shape)
    idx[ha] = pl.ds(_al(half * hn, hn), hn)
    return ref.at[tuple(idx)]


PIECE_BYTES = 256 * 1024


def _pieces(ref):
    lead = ref.shape[0]
    want = (int(np.prod(ref.shape)) * ref.dtype.itemsize) // PIECE_BYTES
    n = max([1] + [k for k in range(1, want + 1) if lead % k == 0 and (lead // k) % 16 == 0])
    rows = lead // n
    return [ref.at[pl.ds(i * rows, rows)] for i in range(n)]


def _half_shape(name):
    full, (ca, cn, ha, hn) = W_SPECS[name]
    shp = list(full)
    shp[ca] = cn
    shp[ha] = hn
    return tuple(shp)


def _gather_direct(names, sh, full, send_sems, recv_sems, local_sems):
    specs = [W_SPECS[k][1] for k in names]
    x, y, c = _my_place()
    chip_me = 2 * x + y
    chips = [(1 - x, y), (x, 1 - y), (1 - x, 1 - y)]

    def local(a):
        return pltpu.make_async_copy(sh[a], _region(full[a], specs[a], chip_me, None), local_sems.at[a])

    def remote(a, src, dst, to):
        return pltpu.make_async_remote_copy(src_ref=src, dst_ref=dst, send_sem=send_sems.at[a], recv_sem=recv_sems.at[a],
                                            device_id=to, device_id_type=MESH)

    def start():
        for a in range(len(names)):
            local(a).start()
            for px, py in chips:
                remote(a, sh[a], _region(full[a], specs[a], chip_me, None), (px, py, c)).start()

    def wait():
        for a in range(len(names)):
            ca, cn, _, _ = specs[a]
            idx = [slice(None)] * len(full[a].shape)
            idx[ca] = pl.ds(0, 3 * cn)
            three = full[a].at[tuple(idx)]
            remote(a, three, three, (x, y, c)).wait()
            local(a).wait()

    return start, wait


def _gather_two_level(names, sh, land, full, send_sems, recv_sems, local_sems):
    n = len(names)
    specs = [W_SPECS[k][1] for k in names]
    x, y, c = _my_place()
    chip_me = 2 * x + y
    sibling = (x, y, 1 - c)
    chips = [(1 - x, y), (x, 1 - y), (1 - x, 1 - y)]

    def remote(k, src, dst, to):
        return pltpu.make_async_remote_copy(src_ref=src, dst_ref=dst, send_sem=send_sems.at[k], recv_sem=recv_sems.at[k],
                                            device_id=to, device_id_type=MESH)

    def three_halves(a):
        ca, cn, ha, hn = specs[a]
        idx = [slice(None)] * len(land[a].shape)
        if ca == ha:
            idx[ca] = pl.ds(0, 3 * hn)
        else:
            idx[ca], idx[ha] = pl.ds(0, 3 * cn), pl.ds(0, hn)
        return land[a].at[tuple(idx)]

    def own(a):
        return pltpu.make_async_copy(sh[a], _region(land[a], specs[a], chip_me, None), local_sems.at[a])

    def out(a):
        return pltpu.make_async_copy(land[a], full[a], local_sems.at[a])

    def start():
        for a in range(n):
            own(a).start()
            for px, py in chips:
                remote(a, _half_of(sh[a], specs[a], c), _region(land[a], specs[a], chip_me, c), (px, py, c)).start()

    def forward():
        for a in range(n):
            remote(a, three_halves(a), three_halves(a), sibling).wait_recv()
            for px, py in chips:
                landed = _region(land[a], specs[a], 2 * px + py, c)
                remote(n + a, landed, landed, sibling).start()

    def finish():
        for a in range(n):
            remote(n + a, three_halves(a), three_halves(a), sibling).wait_recv()
            remote(a, three_halves(a), three_halves(a), sibling).wait_send()
            remote(n + a, three_halves(a), three_halves(a), sibling).wait_send()
            own(a).wait()
        for a in range(n):
            out(a).start()
        for a in range(n):
            out(a).wait()

    return start, forward, finish


def _scatter_direct(names, part, slots, send_sems, recv_sems, local_sems):
    specs = [W_SPECS[k][1] for k in names]
    x, y, c = _my_place()
    dev_me = 4 * x + 2 * y + c

    def local(a):
        return pltpu.make_async_copy(_region(part[a], specs[a], 2 * x + y, c), slots[a].at[dev_me], local_sems.at[a])

    def start():
        for a in range(len(names)):
            local(a).start()
            for flip in range(1, N_DEV):
                tx = 1 - x if flip >> 2 else x
                ty = 1 - y if (flip >> 1) & 1 else y
                tc = 1 - c if flip & 1 else c
                pltpu.make_async_remote_copy(src_ref=_region(part[a], specs[a], 2 * tx + ty, tc), dst_ref=slots[a].at[dev_me],
                                             send_sem=send_sems.at[a], recv_sem=recv_sems.at[a], device_id=(tx, ty, tc),
                                             device_id_type=MESH).start()

    def wait():
        for a in range(len(names)):
            seven = slots[a].at[pl.ds(0, N_DEV - 1)]
            pltpu.make_async_remote_copy(src_ref=seven, dst_ref=seven, send_sem=send_sems.at[a], recv_sem=recv_sems.at[a],
                                         device_id=(x, y, c), device_id_type=MESH).wait()
            local(a).wait()

    return start, wait


HBM_SPEC = pl.BlockSpec(memory_space=pltpu.HBM)
SEM_SPEC = pl.BlockSpec(memory_space=pltpu.SEMAPHORE)
SPLIT_EFFECT = pltpu.SideEffectType.DATAFLOW_SIDE_EFFECTING


SCATTER_STAGES = ((1, 2, 4, 6), (3, 5, 7))


def _scatter_start(part, slots, after, name_key, stage):
    spec = W_SPECS[name_key][1]

    def body(part_ref, slots_ref, after_ref, send_sem, recv_sem, part_thru, slots_thru, token):
        x, y, c = _my_place()
        dev_me = 4 * x + 2 * y + c
        for flip in SCATTER_STAGES[stage]:
            tx = 1 - x if flip >> 2 else x
            ty = 1 - y if (flip >> 1) & 1 else y
            tc = 1 - c if flip & 1 else c
            pltpu.make_async_remote_copy(src_ref=_region(part_ref, spec, 2 * tx + ty, tc), dst_ref=slots_ref.at[dev_me],
                                         send_sem=send_sem, recv_sem=recv_sem, device_id=(tx, ty, tc), device_id_type=MESH).start()
        token[...] = jnp.zeros_like(token)

    return pl.pallas_call(
        body, name="scatter_start_%s_%d" % (name_key, stage),
        out_shape=(pltpu.SemaphoreType.DMA(()), pltpu.SemaphoreType.DMA(()), pltpu.HBM(part.shape, part.dtype),
                   pltpu.HBM(slots.shape, slots.dtype), jax.ShapeDtypeStruct((8, 128), F32)),
        in_specs=(HBM_SPEC, HBM_SPEC, ANY), out_specs=(SEM_SPEC, SEM_SPEC, HBM_SPEC, HBM_SPEC, VMEM),
        input_output_aliases={0: 2, 1: 3},
        compiler_params=pltpu.CompilerParams(has_side_effects=SPLIT_EFFECT),
    )(pltpu.with_memory_space_constraint(part, pltpu.HBM), pltpu.with_memory_space_constraint(slots, pltpu.HBM), after)


def _scatter_wait(send_sem, recv_sem, part_thru, slots_thru, after, name_key, stage):
    n_copies = len(SCATTER_STAGES[stage])

    def body(part_ref, slots_ref, send_sem, recv_sem, *rest):
        x, y, c = _my_place()
        landed = slots_ref.at[pl.ds(0, n_copies)]
        copy = pltpu.make_async_remote_copy(src_ref=landed, dst_ref=landed, send_sem=send_sem, recv_sem=recv_sem,
                                            device_id=(x, y, c), device_id_type=MESH)
        copy.wait_send()
        copy.wait_recv()

    return pl.pallas_call(
        body, name="scatter_wait_%s_%d" % (name_key, stage),
        out_shape=(pltpu.HBM(part_thru.shape, part_thru.dtype), pltpu.HBM(slots_thru.shape, slots_thru.dtype)),
        in_specs=(HBM_SPEC, HBM_SPEC, SEM_SPEC, SEM_SPEC) + (ANY,) * len(after), out_specs=(HBM_SPEC, HBM_SPEC),
        input_output_aliases={0: 0, 1: 1},
        compiler_params=pltpu.CompilerParams(has_side_effects=SPLIT_EFFECT),
    )(part_thru, slots_thru, send_sem, recv_sem, *after)


def _comm_sems(n):
    return [pltpu.SemaphoreType.DMA((n,)), pltpu.SemaphoreType.DMA((n,)), pltpu.SemaphoreType.DMA((n,))]


GATHER_EARLY = ("hgrn_w_in",)
GATHER_LATE = ("hgrn_w_out", "pool_w_in", "pool_w_grp", "pool_w_out")


def _all_gather_weights(shards, names):
    n = len(names)
    specs = [W_SPECS[k][1] for k in names]

    def body(*refs):
        sh, full = refs[:n], refs[n:2 * n]
        send_sems, recv_sems, local_sems = refs[2 * n:]
        x, y, c = _my_place()
        chip_me = 2 * x + y
        sibling = (x, y, 1 - c)
        chips = [(1 - x, y), (x, 1 - y), (1 - x, 1 - y)]

        def remote(a, k, src, dst, to):
            return pltpu.make_async_remote_copy(src_ref=src, dst_ref=dst, send_sem=send_sems.at[6 * a + k],
                                                recv_sem=recv_sems.at[6 * a + k], device_id=to, device_id_type=MESH)

        local = [pltpu.make_async_copy(sh[a], _region(full[a], specs[a], chip_me, None), local_sems.at[a]) for a in range(n)]
        for cp in local:
            cp.start()
        sends = []
        for a in range(n):
            for j, (px, py) in enumerate(chips):
                src, dst = _half_of(sh[a], specs[a], c), _region(full[a], specs[a], chip_me, c)
                for s_piece, d_piece in zip(_pieces(src), _pieces(dst)):
                    remote(a, j, s_piece, d_piece, (px, py, c)).start()
                sends.append(remote(a, j, src, dst, (px, py, c)))
        for a in range(n):
            for j, (px, py) in enumerate(chips):
                landed = _region(full[a], specs[a], 2 * px + py, c)
                remote(a, j, landed, landed, (px, py, c)).wait_recv()
                for piece in _pieces(landed):
                    remote(a, 3 + j, piece, piece, sibling).start()
                sends.append(remote(a, 3 + j, landed, landed, sibling))
        for a in range(n):
            for j, (px, py) in enumerate(chips):
                other = _region(full[a], specs[a], 2 * px + py, 1 - c)
                remote(a, 3 + j, other, other, sibling).wait_recv()
        for cp in sends:
            cp.wait_send()
        for cp in local:
            cp.wait()

    return pl.pallas_call(
        body, name="all_gather_weights",
        out_shape=[jax.ShapeDtypeStruct(W_SPECS[k][0], BF16) for k in names],
        in_specs=[VMEM] * n, out_specs=[VMEM] * n,
        scratch_shapes=[pltpu.SemaphoreType.DMA((6 * n,)), pltpu.SemaphoreType.DMA((6 * n,)), pltpu.SemaphoreType.DMA((n,))],
        compiler_params=_params(32),
    )(*shards)


def _slot_shapes(names):
    return [jax.ShapeDtypeStruct((N_DEV,) + _half_shape(k), BF16) for k in names]


def _scatter_grads(parts, names):
    n = len(names)

    def body(*refs):
        start, wait = _scatter_direct(names, refs[:n], refs[n:2 * n], *refs[2 * n:])
        start()
        wait()

    return pl.pallas_call(body, name="scatter_grads", out_shape=_slot_shapes(names), in_specs=[ANY] * n, out_specs=[ANY] * n,
                          scratch_shapes=_comm_sems(n))(*parts)


def _exchange_halves(halves, names, name):
    n = len(names)
    specs = [W_SPECS[k][1] for k in names]

    def shard_shape(k):
        shp = list(_half_shape(k))
        shp[W_SPECS[k][1][2]] *= 2
        return tuple(shp)

    def body(*refs):
        half, out = refs[:n], refs[n:2 * n]
        send_sems, recv_sems, local_sems = refs[2 * n:]
        x, y, c = _my_place()
        sibling = (x, y, 1 - c)

        def remote(a, src, dst):
            return pltpu.make_async_remote_copy(src_ref=src, dst_ref=dst, send_sem=send_sems.at[a], recv_sem=recv_sems.at[a],
                                                device_id=sibling, device_id_type=MESH)

        local = [pltpu.make_async_copy(half[a], _half_of(out[a], specs[a], c), local_sems.at[a]) for a in range(n)]
        for cp in local:
            cp.start()
        for a in range(n):
            mine = _half_of(out[a], specs[a], c)
            for src, dst in zip(_pieces(half[a]), _pieces(mine)):
                remote(a, src, dst).start()
        for a in range(n):
            theirs = _half_of(out[a], specs[a], 1 - c)
            remote(a, theirs, theirs).wait_recv()
        for a in range(n):
            remote(a, half[a], half[a]).wait_send()
        for cp in local:
            cp.wait()

    return pl.pallas_call(
        body, name=name,
        out_shape=[jax.ShapeDtypeStruct(shard_shape(k), F32) for k in names],
        in_specs=[VMEM] * n, out_specs=[VMEM] * n,
        scratch_shapes=[pltpu.SemaphoreType.DMA((n,)), pltpu.SemaphoreType.DMA((n,)), pltpu.SemaphoreType.DMA((n,))],
        compiler_params=_params(32),
    )(*halves)


def _mod_parts(c_rows, ada_w, ada_b_cols):
    n_layers, _, n_cols = ada_w.shape

    def body(c_ref, w_ref, b_ref, o_ref):
        o_ref[...] = _mm_f32(_silu(c_ref[...]), w_ref[...]) + b_ref[...]

    return pl.pallas_call(
        body, name="mod_parts", grid=(n_layers,),
        out_shape=jax.ShapeDtypeStruct((n_layers, 16, n_cols), F32),
        in_specs=[pl.BlockSpec((16, D_MODEL), lambda i: (0, 0)),
                  pl.BlockSpec((None, D_MODEL, n_cols), lambda i: (i, 0, 0)),
                  pl.BlockSpec((None, 1, n_cols), lambda i: (i, 0, 0))],
        out_specs=pl.BlockSpec((None, 16, n_cols), lambda i: (i, 0, 0)),
        compiler_params=_params(40),
    )(c_rows, ada_w, ada_b_cols.reshape(n_layers, 1, n_cols))


def _ln_mod_matmul(ctx_tile, xin, nw, shift, scale, w, name):
    n_mod = shift.shape[0]
    tm = ROW_TILE
    skip = n_mod - 1
    rows = xin.shape[0] + skip * tm
    n_sec = w.shape[1] // D_MODEL

    def body(*refs):
        c_ref = refs[0] if skip else None
        x_ref, nw_ref, sh_ref, sc_ref, w_ref, g_ref, h_ref = refs[skip:]
        xv = jnp.where(pl.program_id(0) == 0, c_ref[...], x_ref[...]) if skip else x_ref[...]
        r = lax.rsqrt(jnp.mean(xv * xv, axis=-1, keepdims=True) + EPS)
        h = ((xv * r * nw_ref[...]) * (1.0 + sc_ref[...]) + sh_ref[...]).astype(BF16)
        h_ref[...] = h
        for k in range(n_sec):
            g_ref[k] = jnp.dot(h, w_ref[:, k * D_MODEL:(k + 1) * D_MODEL], preferred_element_type=F32)

    mod_spec = pl.BlockSpec((None, 1, D_MODEL), lambda i: (jnp.minimum(i, n_mod - 1), 0, 0))
    return pl.pallas_call(
        body, name=name, grid=(rows // tm,),
        out_shape=[pltpu.HBM((n_sec, rows, D_MODEL), F32), pltpu.HBM((rows, D_MODEL), BF16)],
        in_specs=[pl.BlockSpec((tm, D_MODEL), lambda i: (0, 0))] * skip
        + [pl.BlockSpec((tm, D_MODEL), lambda i: (jnp.maximum(i - skip, 0), 0)),
           pl.BlockSpec((1, D_MODEL), lambda i: (0, 0)),
           mod_spec, mod_spec,
           pl.BlockSpec((D_MODEL, n_sec * D_MODEL), lambda i: (0, 0))],
        out_specs=[pl.BlockSpec((n_sec, tm, D_MODEL), lambda i: (0, i, 0)),
                   pl.BlockSpec((tm, D_MODEL), lambda i: (i, 0))],
        compiler_params=_params(48),
    )(*_pin(*([ctx_tile] * skip), xin, nw, shift.reshape(n_mod, 1, D_MODEL), scale.reshape(n_mod, 1, D_MODEL), w))


def _chunk_masks(rev):
    rid = lax.broadcasted_iota(jnp.int32, (CHUNK, CHUNK), 0)
    cid = lax.broadcasted_iota(jnp.int32, (CHUNK, CHUNK), 1)
    keep = (cid >= rid) if rev else (cid <= rid)
    keep_t = (cid <= rid) if rev else (cid >= rid)
    one, zero = jnp.ones((CHUNK, CHUNK), F32), jnp.zeros((CHUNK, CHUNK), F32)
    return keep, jnp.where(keep, one, zero).astype(BF16), jnp.where(keep_t, one, zero).astype(BF16)


def _chunk_rows(t, rev, latent):
    n = N_LAT_CHUNKS if latent else N_CTX_CHUNKS
    base = CTX_LEN if latent else 0
    idx = (n - 1 - t) if rev else t
    return pl.multiple_of(base + idx * CHUNK, CHUNK)


def _gates(fpre, lb):
    sg = _sig(fpre)
    f = lb + (1.0 - lb) * sg
    return sg, f, 1.0 - f, jnp.log(f)


G_SPEC = lambda sec: pl.BlockSpec((None, ROWS_ALL, HEAD_DIM), lambda h, sec=sec: (sec, 0, h))


def _gla_forward(g5, lb_logits):
    q_scale = HEAD_DIM ** -0.5

    def body(ff_ref, fb_ref, v_ref, q_ref, lg_ref, o_ref, st_ref, decay_ref, qt_ref):
        for rev in (False, True):
            f_ref = fb_ref if rev else ff_ref
            lb = _sig(lg_ref[1:2, :] if rev else lg_ref[0:1, :])
            keep, tri, _ = _chunk_masks(rev)
            last = 0 if rev else CHUNK - 1
            mid = CHUNK // 2 if rev else CHUNK // 2 - 1

            def local_step(t, carry, latent):
                r0 = _chunk_rows(t, rev, latent)
                rows = pl.ds(r0, CHUNK)
                step = t + (N_CTX_CHUNKS if latent else 0)
                _, _, k, lf = _gates(f_ref[rows, :], lb)
                v = v_ref[rows, :]
                b = _mm_exact_lhs(tri, lf)
                bl = b[last:last + 1, :]
                if latent:
                    q = _silu(q_ref[rows, :]) * q_scale
                    bm = b[mid:mid + 1, :]
                    a = _mm_nt(q * jnp.exp(b - bm), k * jnp.exp(bm - b))
                    o = _mm(jnp.where(keep, a, 0.0), v)
                    orow = pl.ds(pl.multiple_of(r0 - CTX_LEN, CHUNK), CHUNK)
                    qt_ref[orow, :] = (q * jnp.exp(b)).astype(BF16)
                    if rev:
                        o_ref[orow, :] += o
                    else:
                        o_ref[orow, :] = o
                decay_ref[step] = jnp.exp(bl)
                st_ref[step] = _mm_tn(v, k * jnp.exp(bl - b))
                return carry

            lax.fori_loop(0, N_CTX_CHUNKS, functools.partial(local_step, latent=False), 0, unroll=2)
            lax.fori_loop(0, N_LAT_CHUNKS, functools.partial(local_step, latent=True), 0, unroll=4)

            def scan_step(t, st):
                update = st_ref[t]
                st_ref[t] = st
                return st * decay_ref[t] + update

            lax.fori_loop(0, N_CHUNKS, scan_step, jnp.zeros((HEAD_DIM, HEAD_DIM), F32), unroll=2)

            def inter_step(t, carry):
                r0 = _chunk_rows(t, rev, True)
                orow = pl.ds(pl.multiple_of(r0 - CTX_LEN, CHUNK), CHUNK)
                o_ref[orow, :] += lax.dot_general(qt_ref[orow, :], st_ref[t + N_CTX_CHUNKS].astype(BF16),
                                                  (((1,), (1,)), ((), ())), preferred_element_type=F32)
                return carry

            lax.fori_loop(0, N_LAT_CHUNKS, inter_step, 0, unroll=4)

    return pl.pallas_call(
        body, name="gla_forward", grid=(HEADS,),
        out_shape=jax.ShapeDtypeStruct((SEQ, D_MODEL), F32),
        in_specs=[G_SPEC(0), G_SPEC(1), G_SPEC(2), G_SPEC(3), pl.BlockSpec((2, HEAD_DIM), lambda h: (0, h))],
        out_specs=pl.BlockSpec((SEQ, HEAD_DIM), lambda h: (0, h)),
        scratch_shapes=[pltpu.VMEM((N_CHUNKS, HEAD_DIM, HEAD_DIM), F32), pltpu.VMEM((N_CHUNKS, 1, HEAD_DIM), F32),
                        pltpu.VMEM((SEQ, HEAD_DIM), BF16)],
        compiler_params=_params(32),
    )(g5, g5, g5, g5, lb_logits)


def _gla_backward(g5, lb_logits, d_o, d_z):
    q_scale = HEAD_DIM ** -0.5

    def body(ff_ref, fb_ref, v_ref, q_ref, lg_ref, do_ref, dz_ref, dg_ref, dlg_ref, st_ref, dst_ref, decay_ref):
        dg_ref[3, 0:CTX_LEN, :] = jnp.zeros((CTX_LEN, HEAD_DIM), F32)
        dg_ref[4, 0:CTX_LEN, :] = jnp.zeros((CTX_LEN, HEAD_DIM), F32)
        dg_ref[4, CTX_LEN:ROWS_ALL, :] = dz_ref[...]
        is_row = lax.broadcasted_iota(jnp.int32, (CHUNK, HEAD_DIM), 0)
        for rev in (False, True):
            d = 1 if rev else 0
            f_ref = fb_ref if rev else ff_ref
            lb = _sig(lg_ref[d:d + 1, :])
            keep, tri, tri_t = _chunk_masks(rev)
            last = 0 if rev else CHUNK - 1
            mid = CHUNK // 2 if rev else CHUNK // 2 - 1

            def local_step(t, carry, latent):
                r0 = _chunk_rows(t, rev, latent)
                rows = pl.ds(r0, CHUNK)
                step = t + (N_CTX_CHUNKS if latent else 0)
                _, _, k, lf = _gates(f_ref[rows, :], lb)
                b = _mm_exact_lhs(tri, lf)
                bl = b[last:last + 1, :]
                decay_ref[step] = jnp.exp(bl)
                st_ref[step] = _mm_tn(v_ref[rows, :], k * jnp.exp(bl - b))
                if latent:
                    q_t = _silu(q_ref[rows, :]) * q_scale * jnp.exp(b)
                    dst_ref[step] = _mm_tn(do_ref[pl.ds(pl.multiple_of(r0 - CTX_LEN, CHUNK), CHUNK), :], q_t)
                else:
                    dst_ref[step] = jnp.zeros((HEAD_DIM, HEAD_DIM), F32)
                return carry

            lax.fori_loop(0, N_CTX_CHUNKS, functools.partial(local_step, latent=False), 0, unroll=2)
            lax.fori_loop(0, N_LAT_CHUNKS, functools.partial(local_step, latent=True), 0, unroll=4)

            def scan_step(i, carry):
                st, d_st = carry
                j = N_CHUNKS - 1 - i
                update, d_update = st_ref[i], dst_ref[j]
                st_ref[i] = st
                dst_ref[j] = d_st
                return st * decay_ref[i] + update, d_st * decay_ref[j] + d_update

            zero_state = jnp.zeros((HEAD_DIM, HEAD_DIM), F32)
            lax.fori_loop(0, N_CHUNKS, scan_step, (zero_state, zero_state))

            def grad_step(t, dlb, latent):
                r0 = _chunk_rows(t, rev, latent)
                rows = pl.ds(r0, CHUNK)
                step = t + (N_CTX_CHUNKS if latent else 0)
                sg, f, k, lf = _gates(f_ref[rows, :], lb)
                v = v_ref[rows, :]
                b = _mm_exact_lhs(tri, lf)
                bl = b[last:last + 1, :]
                e_end = jnp.exp(bl - b)
                k_end = k * e_end
                decay = jnp.exp(bl)
                d_st = dst_ref[step]
                st_prev = st_ref[step]
                d_kend = _mm(v, d_st)
                d_decay = jnp.sum(d_st * st_prev, axis=0, keepdims=True)
                t_kend = d_kend * k_end
                d_v = _mm_nt(k_end, d_st)
                d_k = d_kend * e_end
                d_b = -t_kend
                if latent:
                    qpre = q_ref[rows, :]
                    q = _silu(qpre) * q_scale
                    bm = b[mid:mid + 1, :]
                    e_b, e_qm, e_km = jnp.exp(b), jnp.exp(b - bm), jnp.exp(bm - b)
                    q_t, q_m, k_m = q * e_b, q * e_qm, k * e_km
                    a = jnp.where(keep, _mm_nt(q_m, k_m), 0.0)
                    d_out = do_ref[pl.ds(pl.multiple_of(r0 - CTX_LEN, CHUNK), CHUNK), :]
                    d_a = jnp.where(keep, _mm_nt(d_out, v), 0.0)
                    d_qm = _mm(d_a, k_m)
                    d_km = _mm_tn(d_a, q_m)
                    d_qt = _mm(d_out, st_prev)
                    d_v = d_v + _mm_tn(a, d_out)
                    d_k = d_k + d_km * e_km
                    d_b = d_b + d_qt * q_t + d_qm * q_m - d_km * k_m
                    d_q = d_qt * e_b + d_qm * e_qm
                at_last = jnp.sum(t_kend, axis=0, keepdims=True) + d_decay * decay
                d_b = d_b + jnp.where(is_row == last, at_last, 0.0)
                d_lf = _mm_exact_lhs(tri_t, d_b)
                d_f = d_lf / f - d_k
                dg_ref[d, rows, :] = d_f * (1.0 - lb) * sg * (1.0 - sg)
                if rev:
                    dg_ref[2, rows, :] += d_v
                else:
                    dg_ref[2, rows, :] = d_v
                if latent:
                    d_qpre = d_q * q_scale * _dsilu(qpre)
                    if rev:
                        dg_ref[3, rows, :] += d_qpre
                    else:
                        dg_ref[3, rows, :] = d_qpre
                return dlb + jnp.sum(d_f * (1.0 - sg), axis=0, keepdims=True)

            dlb = lax.fori_loop(0, N_LAT_CHUNKS, functools.partial(grad_step, latent=True), jnp.zeros((1, HEAD_DIM), F32),
                                unroll=2)
            dlb = lax.fori_loop(0, N_CTX_CHUNKS, functools.partial(grad_step, latent=False), dlb, unroll=2)
            dlg_ref[d:d + 1, :] = dlb * lb * (1.0 - lb)

    col = pl.BlockSpec((SEQ, HEAD_DIM), lambda h: (0, h))
    return pl.pallas_call(
        body, name="gla_backward", grid=(HEADS,),
        out_shape=[jax.ShapeDtypeStruct((HGRN_SECTIONS, ROWS_ALL, D_MODEL), F32), jax.ShapeDtypeStruct((2, D_MODEL), F32)],
        in_specs=[G_SPEC(0), G_SPEC(1), G_SPEC(2), G_SPEC(3), pl.BlockSpec((2, HEAD_DIM), lambda h: (0, h)), col, col],
        out_specs=[pl.BlockSpec((HGRN_SECTIONS, ROWS_ALL, HEAD_DIM), lambda h: (0, 0, h)),
                   pl.BlockSpec((2, HEAD_DIM), lambda h: (0, h))],
        scratch_shapes=[pltpu.VMEM((N_CHUNKS, HEAD_DIM, HEAD_DIM), F32), pltpu.VMEM((N_CHUNKS, HEAD_DIM, HEAD_DIM), F32),
                        pltpu.VMEM((N_CHUNKS, 1, HEAD_DIM), F32)],
        compiler_params=_params(48),
    )(g5, g5, g5, g5, lb_logits, d_o, d_z)


GROUP = 2 * CHUNK


def _group_masks(rev):
    rid = lax.broadcasted_iota(jnp.int32, (GROUP, GROUP), 0)
    cid = lax.broadcasted_iota(jnp.int32, (GROUP, GROUP), 1)
    same = (rid >= CHUNK) == (cid >= CHUNK)
    causal = (cid >= rid) if rev else (cid <= rid)
    anti = (cid <= rid) if rev else (cid >= rid)
    bf = lambda m: jnp.where(m, jnp.ones((GROUP, GROUP), F32), jnp.zeros((GROUP, GROUP), F32)).astype(BF16)
    keep = same & causal
    return keep, bf(keep), bf(same & anti), bf(same & jnp.logical_not(anti))


def _group_sum(m_bf, a):
    hi, lo = _split2(a)
    r = jnp.dot(m_bf, jnp.concatenate([hi, lo], axis=1), preferred_element_type=F32)
    return r[:, :HEAD_DIM] + r[:, HEAD_DIM:]


def _chunk_row(a, pos):
    return jnp.concatenate([jnp.broadcast_to(a[c * CHUNK + pos:c * CHUNK + pos + 1, :], (CHUNK, HEAD_DIM)) for c in range(2)], axis=0)


def _by_chunk(a, second):
    return jnp.concatenate([jnp.where(second, 0.0, a), jnp.where(second, a, 0.0)], axis=1)


def _own_block(r):
    return jnp.concatenate([r[0:CHUNK, 0:HEAD_DIM], r[CHUNK:GROUP, HEAD_DIM:2 * HEAD_DIM]], axis=0)


def _scan_step_of(row_chunk, rev, latent):
    if not rev:
        return row_chunk
    return (N_CHUNKS + N_CTX_CHUNKS - 1 - row_chunk) if latent else (N_CTX_CHUNKS - 1 - row_chunk)


def _group_rows(i, j, per_step, latent):
    base = CTX_LEN if latent else 0
    return pl.multiple_of(base + (i * per_step + j) * GROUP, GROUP)


GROUPS_PER_STEP = 8
GROUPS_PER_GRAD_STEP = 4


def _gla_forward(g5, lb_logits, late_shards):
    q_scale = HEAD_DIM ** -0.5
    per_lat, per_ctx = GROUPS_PER_STEP, min(GROUPS_PER_STEP, CTX_LEN // GROUP)
    n_late = len(GATHER_LATE)

    def body(ff_ref, fb_ref, v_ref, q_ref, lg_ref, *rest):
        shard_refs, o_ref, full_refs = rest[:n_late], rest[n_late], rest[n_late + 1:2 * n_late + 1]
        st_ref, decay_ref, qt_ref = rest[2 * n_late + 1:2 * n_late + 4]
        land_refs = rest[2 * n_late + 4:3 * n_late + 4]
        start_gather, forward_gather, finish_gather = _gather_two_level(GATHER_LATE, shard_refs, land_refs, full_refs,
                                                                        *rest[3 * n_late + 4:])

        @pl.when(pl.program_id(0) == 0)
        def _():
            start_gather()

        @pl.when(pl.program_id(0) == HEADS // 2)
        def _():
            forward_gather()

        second = lax.broadcasted_iota(jnp.int32, (GROUP, HEAD_DIM), 0) >= CHUNK
        for rev in (False, True):
            f_ref = fb_ref if rev else ff_ref
            lb = _sig(lg_ref[1:2, :] if rev else lg_ref[0:1, :])
            keep, tri, _, _ = _group_masks(rev)
            last = 0 if rev else CHUNK - 1
            mid = CHUNK // 2 if rev else CHUNK // 2 - 1

            def local_step(i, carry, latent, per):
                r0s = [_group_rows(i, j, per, latent) for j in range(per)]
                rows = [pl.ds(r0, GROUP) for r0 in r0s]
                gates = [_gates(f_ref[r, :], lb) for r in rows]
                vs = [v_ref[r, :] for r in rows]
                bs = [_group_sum(tri, g[3]) for g in gates]
                bls = [_chunk_row(b, last) for b in bs]
                ups = [_mm_tn(v, _by_chunk(g[2] * jnp.exp(bl - b), second)) for v, g, b, bl in zip(vs, gates, bs, bls)]
                if latent:
                    qs = [_silu(q_ref[r, :]) * q_scale for r in rows]
                    bms = [_chunk_row(b, mid) for b in bs]
                    a_s = [_mm_nt(q * jnp.exp(b - bm), g[2] * jnp.exp(bm - b)) for q, g, b, bm in zip(qs, gates, bs, bms)]
                    outs = [_mm(jnp.where(keep, a, 0.0), v) for a, v in zip(a_s, vs)]
                for j in range(per):
                    for c in range(2):
                        step = _scan_step_of(r0s[j] // CHUNK + c, rev, latent)
                        decay_ref[step] = jnp.exp(bls[j][c * CHUNK:c * CHUNK + 1, :])
                        st_ref[step] = ups[j][:, c * HEAD_DIM:(c + 1) * HEAD_DIM]
                    if latent:
                        orow = pl.ds(pl.multiple_of(r0s[j] - CTX_LEN, GROUP), GROUP)
                        qt_ref[orow, :] = (qs[j] * jnp.exp(bs[j])).astype(BF16)
                        if rev:
                            o_ref[orow, :] += outs[j]
                        else:
                            o_ref[orow, :] = outs[j]
                return carry

            lax.fori_loop(0, CTX_LEN // (per_ctx * GROUP), functools.partial(local_step, latent=False, per=per_ctx), 0)
            lax.fori_loop(0, SEQ // (per_lat * GROUP), functools.partial(local_step, latent=True, per=per_lat), 0)

            def scan_step(t, st):
                update = st_ref[t]
                st_ref[t] = st
                return st * decay_ref[t] + update

            lax.fori_loop(0, N_CHUNKS, scan_step, jnp.zeros((HEAD_DIM, HEAD_DIM), F32), unroll=2)

            def inter_step(i, carry):
                r0s = [_group_rows(i, j, per_lat, True) for j in range(per_lat)]
                orows = [pl.ds(pl.multiple_of(r0 - CTX_LEN, GROUP), GROUP) for r0 in r0s]
                states = [jnp.concatenate([st_ref[_scan_step_of(r0 // CHUNK + c, rev, True)] for c in range(2)], axis=0)
                          for r0 in r0s]
                prods = [lax.dot_general(qt_ref[orow, :], s.astype(BF16), (((1,), (1,)), ((), ())), preferred_element_type=F32)
                         for orow, s in zip(orows, states)]
                for orow, r in zip(orows, prods):
                    o_ref[orow, :] += _own_block(r)
                return carry

            lax.fori_loop(0, SEQ // (per_lat * GROUP), inter_step, 0)

        @pl.when(pl.program_id(0) == HEADS - 1)
        def _():
            finish_gather()

    outs = pl.pallas_call(
        body, name="gla_forward", grid=(HEADS,),
        out_shape=[pltpu.HBM((SEQ, D_MODEL), F32)] + [jax.ShapeDtypeStruct(W_SPECS[k][0], BF16) for k in GATHER_LATE],
        in_specs=[G_SPEC(0), G_SPEC(1), G_SPEC(2), G_SPEC(3), pl.BlockSpec((2, HEAD_DIM), lambda h: (0, h))] + [ANY] * n_late,
        out_specs=[pl.BlockSpec((SEQ, HEAD_DIM), lambda h: (0, h))] + [ANY] * n_late,
        scratch_shapes=[pltpu.VMEM((N_CHUNKS, HEAD_DIM, HEAD_DIM), F32), pltpu.VMEM((N_CHUNKS, 1, HEAD_DIM), F32),
                        pltpu.VMEM((SEQ, HEAD_DIM), BF16)] + [pltpu.VMEM(W_SPECS[k][0], BF16) for k in GATHER_LATE]
        + [pltpu.SemaphoreType.DMA((2 * n_late,)), pltpu.SemaphoreType.DMA((2 * n_late,)), pltpu.SemaphoreType.DMA((n_late,))],
        compiler_params=_params(48),
    )(*_pin(g5, g5, g5, g5, lb_logits), *late_shards)
    return outs[0], dict(zip(GATHER_LATE, outs[1:]))


def _gla_backward(g5, lb_logits, d_o, d_z, late_parts):
    q_scale = HEAD_DIM ** -0.5
    per_lat, per_ctx = GROUPS_PER_STEP, min(GROUPS_PER_STEP, CTX_LEN // GROUP)
    n_late = len(GATHER_LATE)

    def body(ff_ref, fb_ref, v_ref, q_ref, lg_ref, do_ref, dz_ref, *rest):
        part_refs, (dg_ref, dlg_ref), slot_refs = rest[:n_late], rest[n_late:n_late + 2], rest[n_late + 2:2 * n_late + 2]
        st_ref, dst_ref, decay_ref = rest[2 * n_late + 2:2 * n_late + 5]
        start_scatter, wait_scatter = _scatter_direct(GATHER_LATE, part_refs, slot_refs, *rest[2 * n_late + 5:])

        @pl.when(pl.program_id(0) == 0)
        def _():
            start_scatter()

        dg_ref[3, 0:CTX_LEN, :] = jnp.zeros((CTX_LEN, HEAD_DIM), F32)
        dg_ref[4, 0:CTX_LEN, :] = jnp.zeros((CTX_LEN, HEAD_DIM), F32)
        dg_ref[4, CTX_LEN:ROWS_ALL, :] = dz_ref[...]
        second = lax.broadcasted_iota(jnp.int32, (GROUP, HEAD_DIM), 0) >= CHUNK
        for rev in (False, True):
            d = 1 if rev else 0
            f_ref = fb_ref if rev else ff_ref
            lb = _sig(lg_ref[d:d + 1, :])
            keep, tri, tri_t, strict = _group_masks(rev)
            last = 0 if rev else CHUNK - 1
            mid = CHUNK // 2 if rev else CHUNK // 2 - 1

            def local_step(i, carry, latent, per):
                r0s = [_group_rows(i, j, per, latent) for j in range(per)]
                rows = [pl.ds(r0, GROUP) for r0 in r0s]
                gates = [_gates(f_ref[r, :], lb) for r in rows]
                bs = [_group_sum(tri, g[3]) for g in gates]
                bls = [_chunk_row(b, last) for b in bs]
                ups = [_mm_tn(v_ref[r, :], _by_chunk(g[2] * jnp.exp(bl - b), second)) for r, g, b, bl in zip(rows, gates, bs, bls)]
                if latent:
                    orows = [pl.ds(pl.multiple_of(r0 - CTX_LEN, GROUP), GROUP) for r0 in r0s]
                    d_ups = [_mm_tn(do_ref[orow, :], _by_chunk(_silu(q_ref[r, :]) * q_scale * jnp.exp(b), second))
                             for orow, r, b in zip(orows, rows, bs)]
                for j in range(per):
                    for c in range(2):
                        step = _scan_step_of(r0s[j] // CHUNK + c, rev, latent)
                        decay_ref[step] = jnp.exp(bls[j][c * CHUNK:c * CHUNK + 1, :])
                        st_ref[step] = ups[j][:, c * HEAD_DIM:(c + 1) * HEAD_DIM]
                        if latent:
                            dst_ref[step] = d_ups[j][:, c * HEAD_DIM:(c + 1) * HEAD_DIM]
                        else:
                            dst_ref[step] = jnp.zeros((HEAD_DIM, HEAD_DIM), F32)
                return carry

            lax.fori_loop(0, CTX_LEN // (per_ctx * GROUP), functools.partial(local_step, latent=False, per=per_ctx), 0)
            lax.fori_loop(0, SEQ // (per_lat * GROUP), functools.partial(local_step, latent=True, per=per_lat), 0)

            def scan_step(i, carry):
                st, d_st = carry
                j = N_CHUNKS - 1 - i
                update, d_update = st_ref[i], dst_ref[j]
                st_ref[i] = st
                dst_ref[j] = d_st
                return st * decay_ref[i] + update, d_st * decay_ref[j] + d_update

            zero_state = jnp.zeros((HEAD_DIM, HEAD_DIM), F32)
            lax.fori_loop(0, N_CHUNKS, scan_step, (zero_state, zero_state))

            def grad_step(i, dlb, latent, per):
                r0s = [_group_rows(i, j, per, latent) for j in range(per)]
                rows = [pl.ds(r0, GROUP) for r0 in r0s]
                gates = [_gates(f_ref[r, :], lb) for r in rows]
                vs = [v_ref[r, :] for r in rows]
                bs = [_group_sum(tri, g[3]) for g in gates]
                bls = [_chunk_row(b, last) for b in bs]
                e_ends = [jnp.exp(bl - b) for b, bl in zip(bs, bls)]
                k_ends = [g[2] * e for g, e in zip(gates, e_ends)]
                sts = [[st_ref[_scan_step_of(r0 // CHUNK + c, rev, latent)] for c in range(2)] for r0 in r0s]
                d_sts = [[dst_ref[_scan_step_of(r0 // CHUNK + c, rev, latent)] for c in range(2)] for r0 in r0s]
                d_kends = [_own_block(_mm(v, jnp.concatenate(ds, axis=1))) for v, ds in zip(vs, d_sts)]
                d_vs = [_own_block(_mm_nt(ke, jnp.concatenate(ds, axis=0))) for ke, ds in zip(k_ends, d_sts)]
                at_last = [jnp.concatenate([jnp.broadcast_to(jnp.sum(ds[c] * s[c], axis=0, keepdims=True), (CHUNK, HEAD_DIM))
                                            for c in range(2)], axis=0) * jnp.exp(bl) for ds, s, bl in zip(d_sts, sts, bls)]
                t_kends = [dk * ke for dk, ke in zip(d_kends, k_ends)]
                d_ks = [dk * e for dk, e in zip(d_kends, e_ends)]
                d_lfs = [_group_sum(strict, t) + al for t, al in zip(t_kends, at_last)]
                if latent:
                    orows = [pl.ds(pl.multiple_of(r0 - CTX_LEN, GROUP), GROUP) for r0 in r0s]
                    qpres = [q_ref[r, :] for r in rows]
                    qs = [_silu(qp) * q_scale for qp in qpres]
                    bms = [_chunk_row(b, mid) for b in bs]
                    e_bs = [jnp.exp(b) for b in bs]
                    e_qms = [jnp.exp(b - bm) for b, bm in zip(bs, bms)]
                    e_kms = [jnp.exp(bm - b) for b, bm in zip(bs, bms)]
                    q_ts = [q * e for q, e in zip(qs, e_bs)]
                    q_ms = [q * e for q, e in zip(qs, e_qms)]
                    k_ms = [g[2] * e for g, e in zip(gates, e_kms)]
                    d_outs = [do_ref[orow, :] for orow in orows]
                    a_s = [jnp.where(keep, _mm_nt(qm, km), 0.0) for qm, km in zip(q_ms, k_ms)]
                    d_as = [jnp.where(keep, _mm_nt(do, v), 0.0) for do, v in zip(d_outs, vs)]
                    d_qts = [_own_block(_mm(do, jnp.concatenate(s, axis=1))) for do, s in zip(d_outs, sts)]
                    d_qms = [_mm(da, km) for da, km in zip(d_as, k_ms)]
                    d_kms = [_mm_tn(da, qm) for da, qm in zip(d_as, q_ms)]
                    d_vs = [dv + _mm_tn(a, do) for dv, a, do in zip(d_vs, a_s, d_outs)]
                    d_ks = [dk + dkm * e for dk, dkm, e in zip(d_ks, d_kms, e_kms)]
                    d_lfs = [dl + _group_sum(tri_t, dqt * qt + dqm * qm - dkm * km)
                             for dl, dqt, qt, dqm, qm, dkm, km in zip(d_lfs, d_qts, q_ts, d_qms, q_ms, d_kms, k_ms)]
                    d_qs = [dqt * eb + dqm * eq for dqt, eb, dqm, eq in zip(d_qts, e_bs, d_qms, e_qms)]
                for j in range(per):
                    sg, f = gates[j][0], gates[j][1]
                    d_f = d_lfs[j] / f - d_ks[j]
                    dg_ref[d, rows[j], :] = d_f * (1.0 - lb) * sg * (1.0 - sg)
                    dlb = dlb + jnp.sum(d_f * (1.0 - sg), axis=0, keepdims=True)
                    if rev:
                        dg_ref[2, rows[j], :] += d_vs[j]
                    else:
                        dg_ref[2, rows[j], :] = d_vs[j]
                    if latent:
                        d_qpre = d_qs[j] * q_scale * _dsilu(qpres[j])
                        if rev:
                            dg_ref[3, rows[j], :] += d_qpre
                        else:
                            dg_ref[3, rows[j], :] = d_qpre
                return dlb

            dlb = lax.fori_loop(0, SEQ // (GROUPS_PER_GRAD_STEP * GROUP),
                                functools.partial(grad_step, latent=True, per=GROUPS_PER_GRAD_STEP), jnp.zeros((1, HEAD_DIM), F32))
            dlb = lax.fori_loop(0, CTX_LEN // (per_ctx * GROUP), functools.partial(grad_step, latent=False, per=per_ctx), dlb)
            dlg_ref[d:d + 1, :] = dlb * lb * (1.0 - lb)

        @pl.when(pl.program_id(0) == HEADS - 1)
        def _():
            wait_scatter()

    col = pl.BlockSpec((SEQ, HEAD_DIM), lambda h: (0, h))
    outs = pl.pallas_call(
        body, name="gla_backward", grid=(HEADS,),
        out_shape=[pltpu.HBM((HGRN_SECTIONS, ROWS_ALL, D_MODEL), F32), jax.ShapeDtypeStruct((2, D_MODEL), F32)]
        + _slot_shapes(GATHER_LATE),
        in_specs=[G_SPEC(0), G_SPEC(1), G_SPEC(2), G_SPEC(3), pl.BlockSpec((2, HEAD_DIM), lambda h: (0, h)), col, col]
        + [ANY] * n_late,
        out_specs=[pl.BlockSpec((HGRN_SECTIONS, ROWS_ALL, HEAD_DIM), lambda h: (0, 0, h)),
                   pl.BlockSpec((2, HEAD_DIM), lambda h: (0, h))] + [ANY] * n_late,
        scratch_shapes=[pltpu.VMEM((N_CHUNKS, HEAD_DIM, HEAD_DIM), F32), pltpu.VMEM((N_CHUNKS, HEAD_DIM, HEAD_DIM), F32),
                        pltpu.VMEM((N_CHUNKS, 1, HEAD_DIM), F32)] + _comm_sems(n_late),
        compiler_params=_params(48),
    )(*_pin(g5, g5, g5, g5, lb_logits, d_o, d_z), *late_parts)
    return outs[0], outs[1], dict(zip(GATHER_LATE, outs[2:]))


def _head_norm(o, gw, scr):
    rs = []
    for h in range(HEADS):
        cols = slice(h * HEAD_DIM, (h + 1) * HEAD_DIM)
        oh = o[:, cols]
        r = lax.rsqrt(jnp.mean(oh * oh, axis=-1, keepdims=True) + EPS)
        scr[:, cols] = oh * r
        rs.append(r)
    return rs


def _hgrn_out_forward(o_raw, g5, xin, gnorm_w, gate, w_out):
    tm = ROW_TILE

    def body(o_ref, z_ref, x_ref, gw_ref, gate_ref, w_ref, x1_ref, res_ref, scr):
        _head_norm(o_ref[...], None, scr)
        a = scr[...] * gw_ref[...] * _silu(z_ref[...])
        res = _mm(a, w_ref[...])
        res_ref[...] = res
        x1_ref[...] = x_ref[...] + gate_ref[...] * res

    tile = pl.BlockSpec((tm, D_MODEL), lambda i: (i, 0))
    vec = pl.BlockSpec((1, D_MODEL), lambda i: (0, 0))
    return pl.pallas_call(
        body, name="hgrn_out_forward", grid=(SEQ // tm,),
        out_shape=[pltpu.HBM((SEQ, D_MODEL), F32)] * 2,
        in_specs=[tile, pl.BlockSpec((None, tm, D_MODEL), lambda i: (4, i + CTX_LEN // tm, 0)), tile, vec, vec,
                  pl.BlockSpec((D_MODEL, D_MODEL), lambda i: (0, 0))],
        out_specs=[tile, tile],
        scratch_shapes=[pltpu.VMEM((tm, D_MODEL), F32)],
        compiler_params=_params(32),
    )(*_pin(o_raw, g5, xin, gnorm_w, gate, w_out))


def _hgrn_out_backward(d_x1, o_raw, g5, res, gnorm_w, gate, w_out):
    tm = ROW_TILE

    def body(dx_ref, o_ref, z_ref, res_ref, gw_ref, gate_ref, w_ref, do_ref, dz_ref, dw_out, dgate_ref, dgw_ref, scr, scr2,
             dw_ref):
        @pl.when(pl.program_id(0) == 0)
        def _():
            dw_ref[...] = jnp.zeros_like(dw_ref)
            dgate_ref[...] = jnp.zeros_like(dgate_ref)
            dgw_ref[...] = jnp.zeros_like(dgw_ref)

        dx = dx_ref[...]
        dgate_ref[...] += jnp.sum(dx * res_ref[...], axis=0, keepdims=True)
        d_res = (dx * gate_ref[...]).astype(BF16)
        d_a = _mm_nt(d_res, w_ref[...])
        rs = _head_norm(o_ref[...], None, scr)
        z = z_ref[...]
        sz = _silu(z)
        o_hat = scr[...]
        o_n = o_hat * gw_ref[...]
        dw_ref[...] += _mm_tn(o_n * sz, d_res)
        d_on = d_a * sz
        dz_ref[...] = d_a * o_n * _dsilu(z)
        dgw_ref[...] += jnp.sum(d_on * o_hat, axis=0, keepdims=True)
        scr2[...] = d_on * gw_ref[...]
        for h in range(HEADS):
            cols = slice(h * HEAD_DIM, (h + 1) * HEAD_DIM)
            dh, oh = scr2[:, cols], scr[:, cols]
            do_ref[:, cols] = rs[h] * (dh - oh * jnp.mean(dh * oh, axis=-1, keepdims=True))

        @pl.when(pl.program_id(0) == SEQ // tm - 1)
        def _():
            dw_out[...] = dw_ref[...].astype(BF16)

    tile = pl.BlockSpec((tm, D_MODEL), lambda i: (i, 0))
    vec = pl.BlockSpec((1, D_MODEL), lambda i: (0, 0))
    mat = pl.BlockSpec((D_MODEL, D_MODEL), lambda i: (0, 0))
    return pl.pallas_call(
        body, name="hgrn_out_backward", grid=(SEQ // tm,),
        out_shape=[pltpu.HBM((SEQ, D_MODEL), F32)] * 2 + [pltpu.HBM((D_MODEL, D_MODEL), BF16)]
        + [jax.ShapeDtypeStruct((1, D_MODEL), F32)] * 2,
        in_specs=[tile, tile, pl.BlockSpec((None, tm, D_MODEL), lambda i: (4, i + CTX_LEN // tm, 0)), tile, vec, vec, mat],
        out_specs=[tile, tile, mat, vec, vec],
        scratch_shapes=[pltpu.VMEM((tm, D_MODEL), F32)] * 2 + [pltpu.VMEM((D_MODEL, D_MODEL), F32)],
        compiler_params=_params(40),
    )(*_pin(d_x1, o_raw, g5, res, gnorm_w, gate, w_out))


def _pool_constants():
    win = np.zeros((POOL_GROUPS, ROW_TILE, ROW_TILE), np.float32)
    inv = np.zeros((POOL_GROUPS, ROW_TILE, 1), np.float32)
    for g, w in enumerate(POOL_WINDOWS):
        for t in range(ROW_TILE):
            base, p = (t // GRID_W) * GRID_W, t % GRID_W
            lo = min(max(p - w // 2, 0), GRID_W)
            hi = min(max(p - w // 2 + w, 0), GRID_W)
            win[g, t, base + lo:base + hi] = 1.0
            inv[g, t, 0] = 1.0 / np.float32(hi - lo)
    return jnp.asarray(win, BF16), jnp.asarray(win.transpose(0, 2, 1), BF16), jnp.asarray(inv, F32)


def _pool_mix(u_ref, wg_ref, ps_ref, win_ref, inv_ref, pooled_scr, yg_scr):
    for g in range(POOL_GROUPS):
        cols = slice(g * POOL_GROUP_DIM, (g + 1) * POOL_GROUP_DIM)
        ug = u_ref[:, cols]
        pooled = _mm_exact_lhs(win_ref[g], ug) * inv_ref[g] - ug
        if pooled_scr is not None:
            pooled_scr[:, cols] = pooled
        yg_scr[:, cols] = _mm(pooled, wg_ref[g])


def _pool_forward_loss(uz, x1, target, gate, w_grp, pool_scale, w_out, final_w):
    tm = ROW_TILE
    win, _, inv = _pool_constants()

    def body(u_ref, z_ref, x_ref, t_ref, gate_ref, wg_ref, ps_ref, w_ref, fw_ref, win_ref, inv_ref,
             dx_ref, loss_ref, dfw_ref, dgate_ref, yg_scr):
        @pl.when(pl.program_id(0) == 0)
        def _():
            loss_ref[...] = jnp.zeros_like(loss_ref)
            dfw_ref[...] = jnp.zeros_like(dfw_ref)
            dgate_ref[...] = jnp.zeros_like(dgate_ref)

        _pool_mix(u_ref, wg_ref, ps_ref, win_ref, inv_ref, None, yg_scr)
        a = yg_scr[...] * ps_ref[...] * _silu(z_ref[...])
        res = _mm(a, w_ref[...])
        x2 = x_ref[...] + gate_ref[...] * res
        r = lax.rsqrt(jnp.mean(x2 * x2, axis=-1, keepdims=True) + EPS)
        xh = x2 * r
        fw = fw_ref[...]
        err = xh * fw - t_ref[...]
        loss_ref[...] += 0.5 * jnp.sum(jnp.mean(err * err, axis=-1, keepdims=True))
        d_y = err * (1.0 / D_MODEL)
        dfw_ref[...] += jnp.sum(d_y * xh, axis=0, keepdims=True)
        d_xh = d_y * fw
        d_x2 = r * (d_xh - xh * jnp.mean(d_xh * xh, axis=-1, keepdims=True))
        dx_ref[...] = d_x2
        dgate_ref[...] += jnp.sum(d_x2 * res, axis=0, keepdims=True)

    tile = pl.BlockSpec((tm, D_MODEL), lambda i: (i, 0))
    vec = pl.BlockSpec((1, D_MODEL), lambda i: (0, 0))
    grp = pl.BlockSpec((POOL_GROUPS, POOL_GROUP_DIM, POOL_GROUP_DIM), lambda i: (0, 0, 0))
    return pl.pallas_call(
        body, name="pool_forward_loss", grid=(SEQ // tm,),
        out_shape=[pltpu.HBM((SEQ, D_MODEL), F32), jax.ShapeDtypeStruct((8, 128), F32),
                   jax.ShapeDtypeStruct((1, D_MODEL), F32), jax.ShapeDtypeStruct((1, D_MODEL), F32)],
        in_specs=[pl.BlockSpec((None, tm, D_MODEL), lambda i: (0, i, 0)), pl.BlockSpec((None, tm, D_MODEL), lambda i: (1, i, 0)),
                  tile, tile, vec, grp, vec, pl.BlockSpec((D_MODEL, D_MODEL), lambda i: (0, 0)), vec, grp,
                  pl.BlockSpec((POOL_GROUPS, ROW_TILE, 1), lambda i: (0, 0, 0))],
        out_specs=[tile, pl.BlockSpec((8, 128), lambda i: (0, 0)), vec, vec],
        scratch_shapes=[pltpu.VMEM((tm, D_MODEL), F32)],
        compiler_params=_params(32),
    )(*_pin(uz, uz, x1, target, gate, w_grp, pool_scale, w_out, final_w, win, inv))


def _pool_backward(d_x2, uz, gate, w_grp, pool_scale, w_out):
    tm = ROW_TILE
    win, win_t, inv = _pool_constants()

    def body(dx_ref, u_ref, z_ref, gate_ref, wg_ref, ps_ref, w_ref, win_ref, wint_ref, inv_ref,
             duz_ref, dw_out, dwg_out, dps_ref, pooled_scr, yg_scr, dyg_scr, dw_ref, dwg_ref):
        @pl.when(pl.program_id(0) == 0)
        def _():
            dw_ref[...] = jnp.zeros_like(dw_ref)
            dwg_ref[...] = jnp.zeros_like(dwg_ref)
            dps_ref[...] = jnp.zeros_like(dps_ref)

        _pool_mix(u_ref, wg_ref, ps_ref, win_ref, inv_ref, pooled_scr, yg_scr)
        z = z_ref[...]
        sz = _silu(z)
        yg = yg_scr[...]
        y = yg * ps_ref[...]
        d_res = (dx_ref[...] * gate_ref[...]).astype(BF16)
        d_a = _mm_nt(d_res, w_ref[...])
        dw_ref[...] += _mm_tn(y * sz, d_res)
        d_y = d_a * sz
        duz_ref[1] = d_a * y * _dsilu(z)
        dps_ref[...] += jnp.sum(d_y * yg, axis=0, keepdims=True)
        dyg_scr[...] = d_y * ps_ref[...]
        for g in range(POOL_GROUPS):
            cols = slice(g * POOL_GROUP_DIM, (g + 1) * POOL_GROUP_DIM)
            d_yg = dyg_scr[:, cols].astype(BF16)
            d_pool = _mm_nt(d_yg, wg_ref[g])
            dwg_ref[g] += _mm_tn(pooled_scr[:, cols], d_yg)
            duz_ref[0, :, cols] = _mm_exact_lhs(wint_ref[g], d_pool * inv_ref[g]) - d_pool

        @pl.when(pl.program_id(0) == SEQ // tm - 1)
        def _():
            dw_out[...] = dw_ref[...].astype(BF16)
            dwg_out[...] = dwg_ref[...].astype(BF16)

    tile = pl.BlockSpec((tm, D_MODEL), lambda i: (i, 0))
    vec = pl.BlockSpec((1, D_MODEL), lambda i: (0, 0))
    mat = pl.BlockSpec((D_MODEL, D_MODEL), lambda i: (0, 0))
    grp = pl.BlockSpec((POOL_GROUPS, POOL_GROUP_DIM, POOL_GROUP_DIM), lambda i: (0, 0, 0))
    return pl.pallas_call(
        body, name="pool_backward", grid=(SEQ // tm,),
        out_shape=[pltpu.HBM((POOL_SECTIONS, SEQ, D_MODEL), F32), pltpu.HBM((D_MODEL, D_MODEL), BF16),
                   pltpu.HBM((POOL_GROUPS, POOL_GROUP_DIM, POOL_GROUP_DIM), BF16), jax.ShapeDtypeStruct((1, D_MODEL), F32)],
        in_specs=[tile, pl.BlockSpec((None, tm, D_MODEL), lambda i: (0, i, 0)), pl.BlockSpec((None, tm, D_MODEL), lambda i: (1, i, 0)),
                  vec, grp, vec, mat, grp, grp, pl.BlockSpec((POOL_GROUPS, ROW_TILE, 1), lambda i: (0, 0, 0))],
        out_specs=[pl.BlockSpec((POOL_SECTIONS, tm, D_MODEL), lambda i: (0, i, 0)), mat, grp, vec],
        scratch_shapes=[pltpu.VMEM((tm, D_MODEL), F32)] * 3 + [pltpu.VMEM((D_MODEL, D_MODEL), F32),
                                                               pltpu.VMEM((POOL_GROUPS, POOL_GROUP_DIM, POOL_GROUP_DIM), F32)],
        compiler_params=_params(40),
    )(*_pin(d_x2, uz, uz, gate, w_grp, pool_scale, w_out, win, win_t, inv))


def _ln_mod_backward(d_g, w, ctx_tile, xin, nw, scale, d_up, name):
    n_sec, rows, _ = d_g.shape
    n_mod = scale.shape[0]
    tm = ROW_TILE
    n_tiles = rows // tm
    skip = n_mod - 1

    def body(dg_ref, w_ref, *refs):
        c_ref = refs[0] if skip else None
        x_ref, nw_ref, sc_ref, up_ref, dx_ref, dnw_ref, dmod_ref = refs[skip:]
        i = pl.program_id(0)

        @pl.when(i == 0)
        def _():
            dnw_ref[...] = jnp.zeros_like(dnw_ref)

        @pl.when((i == 0) | (i == skip))
        def _():
            dmod_ref[...] = jnp.zeros_like(dmod_ref)

        d_h = _mm_nt(dg_ref[0], w_ref[:, 0:D_MODEL])
        for k in range(1, n_sec):
            d_h = d_h + _mm_nt(dg_ref[k], w_ref[:, k * D_MODEL:(k + 1) * D_MODEL])
        xv = jnp.where(i == 0, c_ref[...], x_ref[...]) if skip else x_ref[...]
        r = lax.rsqrt(jnp.mean(xv * xv, axis=-1, keepdims=True) + EPS)
        xh = xv * r
        nw_row = nw_ref[...]
        dmod_ref[0:1, :] += jnp.sum(d_h, axis=0, keepdims=True)
        dmod_ref[1:2, :] += jnp.sum(d_h * (xh * nw_row), axis=0, keepdims=True)
        d_xn = d_h * (1.0 + sc_ref[...])
        dnw_ref[...] += jnp.sum(d_xn * xh, axis=0, keepdims=True)
        d_xh = d_xn * nw_row

        @pl.when(i >= skip)
        def _():
            dx_ref[...] = up_ref[...] + r * (d_xh - xh * jnp.mean(d_xh * xh, axis=-1, keepdims=True))

    lat = lambda i: (jnp.maximum(i - skip, 0), 0)
    mod_idx = lambda i: (jnp.minimum(i, n_mod - 1), 0, 0)
    return pl.pallas_call(
        body, name=name, grid=(n_tiles,),
        out_shape=[pltpu.HBM((rows - skip * tm, D_MODEL), F32), jax.ShapeDtypeStruct((1, D_MODEL), F32),
                   jax.ShapeDtypeStruct((n_mod, 8, D_MODEL), F32)],
        in_specs=[pl.BlockSpec((n_sec, tm, D_MODEL), lambda i: (0, i, 0)),
                  pl.BlockSpec((D_MODEL, n_sec * D_MODEL), lambda i: (0, 0))]
        + [pl.BlockSpec((tm, D_MODEL), lambda i: (0, 0))] * skip
        + [pl.BlockSpec((tm, D_MODEL), lat),
           pl.BlockSpec((1, D_MODEL), lambda i: (0, 0)),
           pl.BlockSpec((None, 1, D_MODEL), mod_idx),
           pl.BlockSpec((tm, D_MODEL), lat)],
        out_specs=[pl.BlockSpec((tm, D_MODEL), lat), pl.BlockSpec((1, D_MODEL), lambda i: (0, 0)),
                   pl.BlockSpec((None, 8, D_MODEL), mod_idx)],
        compiler_params=_params(48),
    )(*_pin(d_g, w, *([ctx_tile] * skip), xin, nw, scale.reshape(n_mod, 1, D_MODEL), d_up))


def _weight_grad(h, d_g, name):
    n_sec, rows, _ = d_g.shape
    tm = 768 if rows % 768 == 0 else 512
    n_tiles = rows // tm

    def body(h_ref, dg_ref, dw_ref, acc):
        i = pl.program_id(1)
        prod = _mm_tn(h_ref[...], dg_ref[...])

        @pl.when(i == 0)
        def _():
            acc[...] = prod

        @pl.when((i > 0) & (i < n_tiles - 1))
        def _():
            acc[...] += prod

        @pl.when(i == n_tiles - 1)
        def _():
            dw_ref[...] = (acc[...] + prod).astype(BF16)

    return pl.pallas_call(
        body, name=name, grid=(n_sec, n_tiles),
        out_shape=pltpu.HBM((D_MODEL, n_sec * D_MODEL), BF16),
        in_specs=[pl.BlockSpec((tm, D_MODEL), lambda j, i: (i, 0)), pl.BlockSpec((None, tm, D_MODEL), lambda j, i: (j, i, 0))],
        out_specs=pl.BlockSpec((D_MODEL, D_MODEL), lambda j, i: (0, j)),
        scratch_shapes=[pltpu.VMEM((D_MODEL, D_MODEL), F32)],
        compiler_params=_params(32),
    )(*_pin(h, d_g))


def _sum_slots(slots, name):
    _, rows, cols = slots.shape
    tr = 128

    def body(s_ref, o_ref):
        acc = s_ref[0].astype(F32)
        for d in range(1, N_DEV):
            acc = acc + s_ref[d].astype(F32)
        o_ref[...] = acc

    return pl.pallas_call(
        body, name=name, grid=(rows // tr,),
        out_shape=jax.ShapeDtypeStruct((rows, cols), F32),
        in_specs=[pl.BlockSpec((N_DEV, tr, cols), lambda i: (0, i, 0))],
        out_specs=pl.BlockSpec((tr, cols), lambda i: (i, 0)),
    )(*_pin(slots))


def _adamw_math(w, g, m, v):
    m = ADAM_B1 * m + (1.0 - ADAM_B1) * g
    v = ADAM_B2 * v + (1.0 - ADAM_B2) * (g * g)
    m_hat = m / (1.0 - ADAM_B1 ** ADAM_STEP)
    v_hat = v / (1.0 - ADAM_B2 ** ADAM_STEP)
    return -ADAM_LR * (m_hat / (jnp.sqrt(v_hat) + ADAM_EPS) + ADAM_WD * w), m, v


def _adamw(w, g, m, v, name):
    rows, cols = w.shape
    tr = rows if rows <= 128 else 128

    def body(w_ref, g_ref, m_ref, v_ref, d_ref, mo_ref, vo_ref):
        d_ref[...], mo_ref[...], vo_ref[...] = _adamw_math(w_ref[...], g_ref[...], m_ref[...], v_ref[...])

    tile = pl.BlockSpec((tr, cols), lambda i: (i, 0))
    return pl.pallas_call(
        body, name=name, grid=(rows // tr,),
        out_shape=[pltpu.HBM((rows, cols), F32)] * 3,
        in_specs=[tile] * 4, out_specs=[tile] * 3,
    )(*_pin(w, g, m, v))


def _sum_devices(gathered):
    def body(p_ref, o_ref):
        acc = p_ref[0:SMALL_ROWS, :]
        for d in range(1, N_DEV):
            acc = acc + p_ref[d * SMALL_ROWS:(d + 1) * SMALL_ROWS, :]
        o_ref[...] = acc

    return pl.pallas_call(body, name="sum_devices", out_shape=jax.ShapeDtypeStruct((SMALL_ROWS, D_MODEL), F32),
                          in_specs=[VMEM], out_specs=VMEM)(gathered)


def _ada_update(cond_t, d_mod, ada_w, m, v):
    n_layers, _, n_cols = ada_w.shape
    tr = ROW_TILE

    def body(c_ref, dm_ref, w_ref, m_ref, v_ref, g_ref, d_ref, mo_ref, vo_ref):
        g = _mm_f32(_silu(c_ref[...]), dm_ref[...])
        g_ref[...] = g
        d_ref[...], mo_ref[...], vo_ref[...] = _adamw_math(w_ref[...], g, m_ref[...], v_ref[...])

    tile = pl.BlockSpec((None, tr, n_cols), lambda l, i: (l, i, 0))
    return pl.pallas_call(
        body, name="ada_update", grid=(n_layers, D_MODEL // tr),
        out_shape=[pltpu.HBM(ada_w.shape, F32)] * 4,
        in_specs=[pl.BlockSpec((tr, 16), lambda l, i: (i, 0)), pl.BlockSpec((None, 16, n_cols), lambda l, i: (l, 0, 0)),
                  tile, tile, tile],
        out_specs=[tile] * 4,
    )(*_pin(cond_t, d_mod, ada_w, m, v))


def _cond_ctx_partial(d_modc, ada_w0):
    n_cols = ada_w0.shape[1]
    tr = ROW_TILE

    def body(dm_ref, w_ref, o_ref):
        o_ref[...] = jnp.sum(w_ref[...] * dm_ref[...], axis=-1, keepdims=True)

    return pl.pallas_call(
        body, name="cond_ctx_partial", grid=(D_MODEL // tr,),
        out_shape=jax.ShapeDtypeStruct((D_MODEL, 1), F32),
        in_specs=[pl.BlockSpec((1, n_cols), lambda i: (0, 0)), pl.BlockSpec((tr, n_cols), lambda i: (i, 0))],
        out_specs=pl.BlockSpec((tr, 1), lambda i: (i, 0)),
    )(d_modc, ada_w0)


def _cond_ctx_update(gathered, c_ctx, m, v):
    def body(p_ref, w_ref, m_ref, v_ref, g_ref, d_ref, mo_ref, vo_ref):
        acc = p_ref[0:1, :]
        for s in range(1, N_CHIPS):
            acc = acc + p_ref[16 * s:16 * s + 1, :]
        w = w_ref[...]
        g = acc * _dsilu(w)
        g_ref[...] = g
        d_ref[...], mo_ref[...], vo_ref[...] = _adamw_math(w, g, m_ref[...], v_ref[...])

    return pl.pallas_call(body, name="cond_ctx_update", out_shape=[jax.ShapeDtypeStruct((1, D_MODEL), F32)] * 4,
                          in_specs=[VMEM] * 4, out_specs=[VMEM] * 4)(gathered, c_ctx, m, v)


def _local_step(x2, ctx2, target, mod_mine, mod_ctx, lb_logits, scale_full, w_in_full, late_shards, norm_w, gnorm, final_w):
    row = lambda a: a.reshape(1, -1)
    shift0, scale0, gate0 = (row(a) for a in jnp.split(mod_mine[0], 3))
    shift1, scale1, gate1 = (row(a) for a in jnp.split(mod_mine[1], 3))
    shift_c, scale_c, _ = (row(a) for a in jnp.split(mod_ctx, 3))
    nw0, nw1 = norm_w[0:1], norm_w[1:2]
    scales0 = jnp.concatenate([scale_c, scale0])

    g5, h0 = _ln_mod_matmul(ctx2, x2, nw0, jnp.concatenate([shift_c, shift0]), scales0, w_in_full, "hgrn_in_forward")
    o_raw, full = _gla_forward(g5, lb_logits, late_shards)
    x1, res0 = _hgrn_out_forward(o_raw, g5, x2, gnorm, gate0, full["hgrn_w_out"])
    uz, h1 = _ln_mod_matmul(None, x1, nw1, shift1, scale1, full["pool_w_in"], "pool_in_forward")
    d_x2, loss_part, d_final, d_gate1 = _pool_forward_loss(uz, x1, target, gate1, full["pool_w_grp"], scale_full,
                                                           full["pool_w_out"], final_w)

    d_uz, dw_pool_out, dw_pool_grp, d_pscale = _pool_backward(d_x2, uz, gate1, full["pool_w_grp"], scale_full, full["pool_w_out"])
    d_x1, d_nw1, d_mod1 = _ln_mod_backward(d_uz, full["pool_w_in"], None, x1, nw1, scale1, d_x2, "pool_in_backward")
    dw_pool_in = _weight_grad(h1, d_uz, "pool_in_weight_grad")
    d_o, d_z, dw_hgrn_out, d_gate0, d_gnorm = _hgrn_out_backward(d_x1, o_raw, g5, res0, gnorm, gate0, full["hgrn_w_out"])
    late_grads = {"hgrn_w_out": dw_hgrn_out, "pool_w_in": dw_pool_in, "pool_w_grp": dw_pool_grp, "pool_w_out": dw_pool_out}
    d_g5, d_lb, late_slots = _gla_backward(g5, lb_logits, d_o, d_z, [late_grads[k].astype(BF16) for k in GATHER_LATE])
    dw_hgrn_in = _weight_grad(h0, d_g5, "hgrn_in_weight_grad")
    key = GATHER_EARLY[0]
    half = _half_shape(key)
    xi, yi, ci = _my_place()
    own = lax.dynamic_slice(dw_hgrn_in, (ci * half[0], (2 * xi + yi) * half[1]), half)
    slots0 = lax.dynamic_update_slice(jnp.zeros((N_DEV,) + half, BF16), own[None], (4 * xi + 2 * yi + ci, 0, 0))
    send_sem, recv_sem, part_thru, slots_thru, token = _scatter_start(dw_hgrn_in, slots0, own, key, 0)
    d_x, d_nw0, d_mod0 = _ln_mod_backward(d_g5, w_in_full, ctx2, x2, nw0 + token[0:1, 0:1], scales0, d_x1, "hgrn_in_backward")
    slots = dict(late_slots)
    pending = (send_sem, recv_sem, part_thru, slots_thru)

    zero = jnp.zeros((1, D_MODEL), F32)
    small = jnp.concatenate([d_mod0[1, 0:2], d_gate0, d_mod1[0, 0:2], d_gate1, d_mod0[0, 0:2], zero, d_nw0, d_nw1, d_gnorm,
                             d_final, d_pscale, d_lb, jnp.broadcast_to(loss_part[0:1, 0:1], (1, D_MODEL)),
                             jnp.zeros((SMALL_ROWS - 17, D_MODEL), F32)], axis=0)
    return {"d_x": d_x, "slots": slots, "pending": pending, "small": small}


def kernel(x, c, ctx, c_ctx, ada_w, ada_b, norm_w, hgrn_w_in, hgrn_lb_logits, hgrn_gnorm_w, hgrn_w_out, pool_w_in, pool_w_grp, pool_scale, pool_w_out, final_norm_w, loss_target, m_c_ctx, m_ada_w, m_ada_b, m_norm_w, m_hgrn_w_in, m_hgrn_lb_logits, m_hgrn_gnorm_w, m_hgrn_w_out, m_pool_w_in, m_pool_w_grp, m_pool_scale, m_pool_w_out, m_final_norm_w, v_c_ctx, v_ada_w, v_ada_b, v_norm_w, v_hgrn_w_in, v_hgrn_lb_logits, v_hgrn_gnorm_w, v_hgrn_w_out, v_pool_w_in, v_pool_w_grp, v_pool_scale, v_pool_w_out, v_final_norm_w):
    xi, yi, ci = _my_place()
    chip = 2 * xi + yi
    dev = 4 * xi + 2 * yi + ci
    ada_cols = ada_w.shape[2]
    lb_cols = hgrn_lb_logits.shape[2]
    ps_cols = pool_scale.shape[1]
    row = lambda a: a.reshape(1, -1)

    def chip_cols(a, n):
        return lax.dynamic_slice_in_dim(a, chip * n, n, axis=a.ndim - 1)

    def from_chips(g, rows_per_dev, take):
        return jnp.concatenate([g[2 * s * rows_per_dev:2 * s * rows_per_dev + take] for s in range(N_CHIPS)], axis=1)

    first = jnp.concatenate([jnp.broadcast_to(c, (8, D_MODEL)), jnp.pad(hgrn_lb_logits[0], ((0, 6), (0, 0))),
                             jnp.pad(pool_scale, ((0, 7), (0, 0)))], axis=1)
    first_all = _all_gather_small(first, "gather_cond")
    cond_all = first_all[::8, :D_MODEL]
    lb_logits = from_chips(first_all[:, D_MODEL:D_MODEL + lb_cols], 8, 2)
    scale_full = from_chips(first_all[:, D_MODEL + lb_cols:], 8, 1)
    cond_rows = jnp.concatenate([cond_all, row(c_ctx), jnp.zeros((7, D_MODEL), F32)], axis=0)

    parts = _mod_parts(cond_rows, ada_w, chip_cols(ada_b, ada_cols))
    parts_all = _all_gather_small(parts.reshape(32, ada_cols), "gather_mod")
    mod_all = from_chips(parts_all, 32, 32).reshape(2, 16, 3 * D_MODEL)
    mod_mine = lax.dynamic_index_in_dim(mod_all, dev, axis=1, keepdims=False)

    shards = {"hgrn_w_in": hgrn_w_in[0], "hgrn_w_out": hgrn_w_out[0], "pool_w_in": pool_w_in[0],
              "pool_w_grp": pool_w_grp[0], "pool_w_out": pool_w_out[0]}
    w_in_full = _all_gather_weights([shards[k].astype(BF16) for k in GATHER_EARLY], GATHER_EARLY)[0]

    loc = _local_step(x[0], ctx[0], loss_target[0], mod_mine, mod_all[0, 8], lb_logits, scale_full, w_in_full,
                      [shards[k].astype(BF16) for k in GATHER_LATE], norm_w, hgrn_gnorm_w, row(final_norm_w))

    small_all = _all_gather_small(loc["small"], "gather_small")
    sums = _sum_devices(small_all)
    loss = sums[16, 0]

    part_c = _cond_ctx_partial(chip_cols(sums[6:9].reshape(1, -1), ada_cols), ada_w[0])
    part_all = _all_gather_small(jnp.pad(part_c.reshape(1, D_MODEL), ((0, 7), (0, 0))), "gather_cond_ctx")
    key = GATHER_EARLY[0]
    send0, recv0, part_thru, slots_thru = loc["pending"]
    send1, recv1, part_thru, slots_thru, _ = _scatter_start(part_thru, slots_thru, part_all, key, 1)

    def reduce_scattered(slots, names, name):
        halves = []
        for k in names:
            hs = _half_shape(k)
            halves.append(_sum_slots(slots[k].reshape(N_DEV, -1, hs[-1]), "sum_" + k).reshape(hs))
        return dict(zip(names, _exchange_halves(halves, names, name)))

    big_grads = reduce_scattered(loc["slots"], GATHER_LATE, "exchange_halves_late")

    out = {}

    def update(name, w, g, m, v):
        shape = w.shape
        w2, g2, m2, v2 = (a.reshape(-1, shape[-1]) for a in (w, g, m, v))
        d, mn, vn = _adamw(w2, g2, m2, v2, "adamw_" + name)
        out[name] = tuple(a.reshape(shape) for a in (g2, d, mn, vn))

    moments = {"hgrn_w_in": (m_hgrn_w_in, v_hgrn_w_in), "hgrn_w_out": (m_hgrn_w_out, v_hgrn_w_out),
               "pool_w_in": (m_pool_w_in, v_pool_w_in), "pool_w_grp": (m_pool_w_grp, v_pool_w_grp),
               "pool_w_out": (m_pool_w_out, v_pool_w_out)}
    weights = {"hgrn_w_in": hgrn_w_in, "hgrn_w_out": hgrn_w_out, "pool_w_in": pool_w_in, "pool_w_grp": pool_w_grp,
               "pool_w_out": pool_w_out}
    for k in GATHER_LATE:
        update(k, weights[k], big_grads[k], *moments[k])

    g_ada_b = jnp.stack([(sums[0:3] + sums[6:9]).reshape(-1), sums[3:6].reshape(-1)])
    update("ada_b", ada_b, g_ada_b, m_ada_b, v_ada_b)
    update("norm_w", norm_w, sums[9:11], m_norm_w, v_norm_w)
    update("hgrn_gnorm_w", hgrn_gnorm_w, sums[11:12], m_hgrn_gnorm_w, v_hgrn_gnorm_w)
    update("final_norm_w", row(final_norm_w), sums[12:13], row(m_final_norm_w), row(v_final_norm_w))
    update("pool_scale", pool_scale, chip_cols(sums[13:14], ps_cols), m_pool_scale, v_pool_scale)
    update("hgrn_lb_logits", hgrn_lb_logits, chip_cols(sums[14:16], lb_cols)[None], m_hgrn_lb_logits, v_hgrn_lb_logits)

    per_dev = small_all.reshape(N_DEV, SMALL_ROWS, D_MODEL)
    pad7 = jnp.zeros((7, 3 * D_MODEL), F32)
    dm0 = jnp.concatenate([per_dev[:, 0:3].reshape(N_DEV, -1), sums[6:9].reshape(1, -1), pad7], axis=0)
    dm1 = jnp.concatenate([per_dev[:, 3:6].reshape(N_DEV, -1), jnp.zeros((8, 3 * D_MODEL), F32)], axis=0)
    d_mod = chip_cols(jnp.stack([dm0, dm1]), ada_cols)
    out["ada_w"] = _ada_update(cond_rows.T, d_mod, ada_w, m_ada_w, v_ada_w)

    g_c, d_c, m_c, v_c = _cond_ctx_update(part_all, row(c_ctx), row(m_c_ctx), row(v_c_ctx))
    out["c_ctx"] = tuple(a.reshape(-1) for a in (g_c, d_c, m_c, v_c))
    out["final_norm_w"] = tuple(a.reshape(-1) for a in out["final_norm_w"])

    done = [g_c, out["ada_w"][1]] + [out[k][1] for k in GATHER_LATE]
    part_thru, slots_thru = _scatter_wait(send0, recv0, part_thru, slots_thru, done, key, 0)
    _, early = _scatter_wait(send1, recv1, part_thru, slots_thru, done, key, 1)
    big_grads = reduce_scattered({key: early}, GATHER_EARLY, "exchange_halves_early")
    for k in GATHER_EARLY:
        update(k, weights[k], big_grads[k], *moments[k])

    names = ["c_ctx", "ada_w", "ada_b", "norm_w", "hgrn_w_in", "hgrn_lb_logits", "hgrn_gnorm_w", "hgrn_w_out", "pool_w_in",
             "pool_w_grp", "pool_scale", "pool_w_out", "final_norm_w"]
    return (loss, loc["d_x"][None], *[out[k][0] for k in names], *[out[k][1] for k in names], *[out[k][2] for k in names],
            *[out[k][3] for k in names])
```

```python
import functools

import numpy as np
import jax
import jax.numpy as jnp
from jax import lax
from jax.experimental import pallas as pl
from jax.experimental.pallas import tpu as pltpu

F32 = jnp.float32
BF16 = jnp.bfloat16

D_MODEL = 1024
SEQ = 2048
CTX_LEN = 256
ROWS_ALL = CTX_LEN + SEQ
HEADS = 8
HEAD_DIM = 128
CHUNK = 64
N_CTX_CHUNKS = CTX_LEN // CHUNK
N_LAT_CHUNKS = SEQ // CHUNK
N_CHUNKS = N_CTX_CHUNKS + N_LAT_CHUNKS
GRID_W = 64
POOL_WINDOWS = (2, 4, 8, 16)
POOL_GROUPS = 4
POOL_GROUP_DIM = 256
HGRN_SECTIONS = 5
POOL_SECTIONS = 2
EPS = 1e-6
N_DEV = 8
N_CHIPS = 4
ROW_TILE = 256
SMALL_ROWS = 24

ADAM_LR = 0.001
ADAM_B1 = 0.9
ADAM_B2 = 0.999
ADAM_EPS = 1e-08
ADAM_WD = 0.01
ADAM_STEP = 10

MESH = pl.DeviceIdType.MESH
MIB = 1 << 20
ANY = pl.BlockSpec(memory_space=pl.ANY)
VMEM = pl.BlockSpec(memory_space=pltpu.VMEM)


def _params(vmem_mib=None):
    if vmem_mib is None:
        return pltpu.CompilerParams()
    return pltpu.CompilerParams(vmem_limit_bytes=vmem_mib * MIB)


def _pin(*operands):
    return [pltpu.with_memory_space_constraint(a, pltpu.HBM) if a.size * a.dtype.itemsize >= MIB else a for a in operands]


def _sig(a):
    return 1.0 / (1.0 + jnp.exp(-a))


def _silu(a):
    return a * _sig(a)


def _dsilu(a):
    s = _sig(a)
    return s * (1.0 + a * (1.0 - s))


def _mm(a, b):
    return jnp.dot(a.astype(BF16), b.astype(BF16), preferred_element_type=F32)


def _mm_nt(a, b):
    return lax.dot_general(a.astype(BF16), b.astype(BF16), (((1,), (1,)), ((), ())), preferred_element_type=F32)


def _mm_tn(a, b):
    return lax.dot_general(a.astype(BF16), b.astype(BF16), (((0,), (0,)), ((), ())), preferred_element_type=F32)


def _split2(a):
    hi = a.astype(BF16)
    lo = (a - hi.astype(F32)).astype(BF16)
    return hi, lo


def _mm_exact_lhs(m_bf, a):
    hi, lo = _split2(a)
    return jnp.dot(m_bf, hi, preferred_element_type=F32) + jnp.dot(m_bf, lo, preferred_element_type=F32)


def _mm_f32(a, b):
    ah, al = _split2(a)
    bh, bl = _split2(b)
    return (jnp.dot(ah, bh, preferred_element_type=F32) + jnp.dot(al, bh, preferred_element_type=F32)
            + jnp.dot(ah, bl, preferred_element_type=F32))


def _my_place():
    return lax.axis_index("x"), lax.axis_index("y"), lax.axis_index("c")


def _all_gather_small(blk, name):
    m_per, n = blk.shape

    def body(x_ref, out_ref, send_sems, recv_sems, local_sem):
        x, y, c = _my_place()
        me, sibling = (x, y, c), (x, y, 1 - c)
        chips = [(1 - x, y), (x, 1 - y), (1 - x, 1 - y)]

        def rows(px, py, pc):
            return out_ref.at[pl.ds((4 * px + 2 * py + pc) * m_per, m_per), :]

        def copy(k, block, to, src=None):
            return pltpu.make_async_remote_copy(
                src_ref=rows(*block) if src is None else src, dst_ref=rows(*block),
                send_sem=send_sems.at[k], recv_sem=recv_sems.at[k], device_id=to, device_id_type=MESH)

        mine = pltpu.make_async_copy(x_ref, rows(*me), local_sem)
        mine.start()
        first = [copy(0, me, sibling, src=x_ref)]
        first += [copy(1 + j, me, (*chip, c), src=x_ref) for j, chip in enumerate(chips)]
        for cp in first:
            cp.start()
        passed = [copy(4 + j, (*chip, c), sibling) for j, chip in enumerate(chips)]
        for j, chip in enumerate(chips):
            copy(1 + j, (*chip, c), me).wait_recv()
            passed[j].start()
        copy(0, sibling, me).wait_recv()
        for j, chip in enumerate(chips):
            copy(4 + j, (*chip, 1 - c), me).wait_recv()
        for cp in first + passed:
            cp.wait_send()
        mine.wait()

    return pl.pallas_call(
        body, name=name,
        out_shape=jax.ShapeDtypeStruct((N_DEV * m_per, n), blk.dtype),
        in_specs=[VMEM], out_specs=VMEM,
        scratch_shapes=[pltpu.SemaphoreType.DMA((7,)), pltpu.SemaphoreType.DMA((7,)), pltpu.SemaphoreType.DMA],
    )(blk)


W_SPECS = {
    "hgrn_w_in": ((D_MODEL, 5 * D_MODEL), (1, 1280, 0, 512)),
    "hgrn_w_out": ((D_MODEL, D_MODEL), (0, 256, 0, 128)),
    "pool_w_in": ((D_MODEL, 2 * D_MODEL), (1, 512, 0, 512)),
    "pool_w_grp": ((POOL_GROUPS, POOL_GROUP_DIM, POOL_GROUP_DIM), (1, 64, 1, 32)),
    "pool_w_out": ((D_MODEL, D_MODEL), (0, 256, 0, 128)),
}
W_NAMES = tuple(W_SPECS)


def _al(v, m):
    return pl.multiple_of(v, m)


def _region(ref, spec, chip, half):
    ca, cn, ha, hn = spec
    idx = [slice(None)] * len(ref.shape)
    if ca == ha:
        if half is None:
            idx[ca] = pl.ds(_al(chip * cn, cn), cn)
        else:
            idx[ca] = pl.ds(_al(chip * cn + half * hn, hn), hn)
    else:
        idx[ca] = pl.ds(_al(chip * cn, cn), cn)
        if half is not None:
            idx[ha] = pl.ds(_al(half * hn, hn), hn)
    return ref.at[tuple(idx)]


def _half_of(ref, spec, half):
    _, _, ha, hn = spec
    idx = [slice(None)] * len(ref.shape)
    idx[ha] = pl.ds(_al(half * hn, hn), hn)
    return ref.at[tuple(idx)]


PIECE_BYTES = 256 * 1024


def _pieces(ref):
    lead = ref.shape[0]
    want = (int(np.prod(ref.shape)) * ref.dtype.itemsize) // PIECE_BYTES
    n = max([1] + [k for k in range(1, want + 1) if lead % k == 0 and (lead // k) % 16 == 0])
    rows = lead // n
    return [ref.at[pl.ds(i * rows, rows)] for i in range(n)]


def _half_shape(name):
    full, (ca, cn, ha, hn) = W_SPECS[name]
    shp = list(full)
    shp[ca] = cn
    shp[ha] = hn
    return tuple(shp)


def _gather_direct(names, sh, full, send_sems, recv_sems, local_sems):
    specs = [W_SPECS[k][1] for k in names]
    x, y, c = _my_place()
    chip_me = 2 * x + y
    chips = [(1 - x, y), (x, 1 - y), (1 - x, 1 - y)]

    def local(a):
        return pltpu.make_async_copy(sh[a], _region(full[a], specs[a], chip_me, None), local_sems.at[a])

    def remote(a, src, dst, to):
        return pltpu.make_async_remote_copy(src_ref=src, dst_ref=dst, send_sem=send_sems.at[a], recv_sem=recv_sems.at[a],
                                            device_id=to, device_id_type=MESH)

    def start():
        for a in range(len(names)):
            local(a).start()
            for px, py in chips:
                remote(a, sh[a], _region(full[a], specs[a], chip_me, None), (px, py, c)).start()

    def wait():
        for a in range(len(names)):
            ca, cn, _, _ = specs[a]
            idx = [slice(None)] * len(full[a].shape)
            idx[ca] = pl.ds(0, 3 * cn)
            three = full[a].at[tuple(idx)]
            remote(a, three, three, (x, y, c)).wait()
            local(a).wait()

    return start, wait


def _gather_two_level(names, sh, land, full, send_sems, recv_sems, local_sems):
    n = len(names)
    specs = [W_SPECS[k][1] for k in names]
    x, y, c = _my_place()
    chip_me = 2 * x + y
    sibling = (x, y, 1 - c)
    chips = [(1 - x, y), (x, 1 - y), (1 - x, 1 - y)]

    def remote(k, src, dst, to):
        return pltpu.make_async_remote_copy(src_ref=src, dst_ref=dst, send_sem=send_sems.at[k], recv_sem=recv_sems.at[k],
                                            device_id=to, device_id_type=MESH)

    def three_halves(a):
        ca, cn, ha, hn = specs[a]
        idx = [slice(None)] * len(land[a].shape)
        if ca == ha:
            idx[ca] = pl.ds(0, 3 * hn)
        else:
            idx[ca], idx[ha] = pl.ds(0, 3 * cn), pl.ds(0, hn)
        return land[a].at[tuple(idx)]

    def own(a):
        return pltpu.make_async_copy(sh[a], _region(land[a], specs[a], chip_me, None), local_sems.at[a])

    def out(a):
        return pltpu.make_async_copy(land[a], full[a], local_sems.at[a])

    def start():
        for a in range(n):
            own(a).start()
            for px, py in chips:
                remote(a, _half_of(sh[a], specs[a], c), _region(land[a], specs[a], chip_me, c), (px, py, c)).start()

    def forward():
        for a in range(n):
            remote(a, three_halves(a), three_halves(a), sibling).wait_recv()
            for px, py in chips:
                landed = _region(land[a], specs[a], 2 * px + py, c)
                remote(n + a, landed, landed, sibling).start()

    def finish():
        for a in range(n):
            remote(n + a, three_halves(a), three_halves(a), sibling).wait_recv()
            remote(a, three_halves(a), three_halves(a), sibling).wait_send()
            remote(n + a, three_halves(a), three_halves(a), sibling).wait_send()
            own(a).wait()
        for a in range(n):
            out(a).start()
        for a in range(n):
            out(a).wait()

    return start, forward, finish


def _scatter_direct(names, part, slots, send_sems, recv_sems, local_sems):
    specs = [W_SPECS[k][1] for k in names]
    x, y, c = _my_place()
    dev_me = 4 * x + 2 * y + c

    def local(a):
        return pltpu.make_async_copy(_region(part[a], specs[a], 2 * x + y, c), slots[a].at[dev_me], local_sems.at[a])

    def start():
        for a in range(len(names)):
            local(a).start()
            for flip in range(1, N_DEV):
                tx = 1 - x if flip >> 2 else x
                ty = 1 - y if (flip >> 1) & 1 else y
                tc = 1 - c if flip & 1 else c
                pltpu.make_async_remote_copy(src_ref=_region(part[a], specs[a], 2 * tx + ty, tc), dst_ref=slots[a].at[dev_me],
                                             send_sem=send_sems.at[a], recv_sem=recv_sems.at[a], device_id=(tx, ty, tc),
                                             device_id_type=MESH).start()

    def wait():
        for a in range(len(names)):
            seven = slots[a].at[pl.ds(0, N_DEV - 1)]
            pltpu.make_async_remote_copy(src_ref=seven, dst_ref=seven, send_sem=send_sems.at[a], recv_sem=recv_sems.at[a],
                                         device_id=(x, y, c), device_id_type=MESH).wait()
            local(a).wait()

    return start, wait


HBM_SPEC = pl.BlockSpec(memory_space=pltpu.HBM)
SEM_SPEC = pl.BlockSpec(memory_space=pltpu.SEMAPHORE)
SPLIT_EFFECT = pltpu.SideEffectType.DATAFLOW_SIDE_EFFECTING


SCATTER_STAGES = ((1, 2, 4, 6), (3, 5, 7))


def _scatter_start(part, slots, after, name_key, stage):
    spec = W_SPECS[name_key][1]

    def body(part_ref, slots_ref, after_ref, send_sem, recv_sem, part_thru, slots_thru, token):
        x, y, c = _my_place()
        dev_me = 4 * x + 2 * y + c
        for flip in SCATTER_STAGES[stage]:
            tx = 1 - x if flip >> 2 else x
            ty = 1 - y if (flip >> 1) & 1 else y
            tc = 1 - c if flip & 1 else c
            pltpu.make_async_remote_copy(src_ref=_region(part_ref, spec, 2 * tx + ty, tc), dst_ref=slots_ref.at[dev_me],
                                         send_sem=send_sem, recv_sem=recv_sem, device_id=(tx, ty, tc), device_id_type=MESH).start()
        token[...] = jnp.zeros_like(token)

    return pl.pallas_call(
        body, name="scatter_start_%s_%d" % (name_key, stage),
        out_shape=(pltpu.SemaphoreType.DMA(()), pltpu.SemaphoreType.DMA(()), pltpu.HBM(part.shape, part.dtype),
                   pltpu.HBM(slots.shape, slots.dtype), jax.ShapeDtypeStruct((8, 128), F32)),
        in_specs=(HBM_SPEC, HBM_SPEC, ANY), out_specs=(SEM_SPEC, SEM_SPEC, HBM_SPEC, HBM_SPEC, VMEM),
        input_output_aliases={0: 2, 1: 3},
        compiler_params=pltpu.CompilerParams(has_side_effects=SPLIT_EFFECT),
    )(pltpu.with_memory_space_constraint(part, pltpu.HBM), pltpu.with_memory_space_constraint(slots, pltpu.HBM), after)


def _scatter_wait(send_sem, recv_sem, part_thru, slots_thru, after, name_key, stage):
    n_copies = len(SCATTER_STAGES[stage])

    def body(part_ref, slots_ref, send_sem, recv_sem, *rest):
        x, y, c = _my_place()
        landed = slots_ref.at[pl.ds(0, n_copies)]
        copy = pltpu.make_async_remote_copy(src_ref=landed, dst_ref=landed, send_sem=send_sem, recv_sem=recv_sem,
                                            device_id=(x, y, c), device_id_type=MESH)
        copy.wait_send()
        copy.wait_recv()

    return pl.pallas_call(
        body, name="scatter_wait_%s_%d" % (name_key, stage),
        out_shape=(pltpu.HBM(part_thru.shape, part_thru.dtype), pltpu.HBM(slots_thru.shape, slots_thru.dtype)),
        in_specs=(HBM_SPEC, HBM_SPEC, SEM_SPEC, SEM_SPEC) + (ANY,) * len(after), out_specs=(HBM_SPEC, HBM_SPEC),
        input_output_aliases={0: 0, 1: 1},
        compiler_params=pltpu.CompilerParams(has_side_effects=SPLIT_EFFECT),
    )(part_thru, slots_thru, send_sem, recv_sem, *after)


def _comm_sems(n):
    return [pltpu.SemaphoreType.DMA((n,)), pltpu.SemaphoreType.DMA((n,)), pltpu.SemaphoreType.DMA((n,))]


GATHER_EARLY = ("hgrn_w_in",)
GATHER_LATE = ("hgrn_w_out", "pool_w_in", "pool_w_grp", "pool_w_out")


def _all_gather_weights(shards, names):
    n = len(names)
    specs = [W_SPECS[k][1] for k in names]

    def body(*refs):
        sh, full = refs[:n], refs[n:2 * n]
        send_sems, recv_sems, local_sems = refs[2 * n:]
        x, y, c = _my_place()
        chip_me = 2 * x + y
        sibling = (x, y, 1 - c)
        chips = [(1 - x, y), (x, 1 - y), (1 - x, 1 - y)]

        def remote(a, k, src, dst, to):
            return pltpu.make_async_remote_copy(src_ref=src, dst_ref=dst, send_sem=send_sems.at[6 * a + k],
                                                recv_sem=recv_sems.at[6 * a + k], device_id=to, device_id_type=MESH)

        local = [pltpu.make_async_copy(sh[a], _region(full[a], specs[a], chip_me, None), local_sems.at[a]) for a in range(n)]
        for cp in local:
            cp.start()
        sends = []
        for a in range(n):
            for j, (px, py) in enumerate(chips):
                src, dst = _half_of(sh[a], specs[a], c), _region(full[a], specs[a], chip_me, c)
                for s_piece, d_piece in zip(_pieces(src), _pieces(dst)):
                    remote(a, j, s_piece, d_piece, (px, py, c)).start()
                sends.append(remote(a, j, src, dst, (px, py, c)))
        for a in range(n):
            for j, (px, py) in enumerate(chips):
                landed = _region(full[a], specs[a], 2 * px + py, c)
                remote(a, j, landed, landed, (px, py, c)).wait_recv()
                for piece in _pieces(landed):
                    remote(a, 3 + j, piece, piece, sibling).start()
                sends.append(remote(a, 3 + j, landed, landed, sibling))
        for a in range(n):
            for j, (px, py) in enumerate(chips):
                other = _region(full[a], specs[a], 2 * px + py, 1 - c)
                remote(a, 3 + j, other, other, sibling).wait_recv()
        for cp in sends:
            cp.wait_send()
        for cp in local:
            cp.wait()

    return pl.pallas_call(
        body, name="all_gather_weights",
        out_shape=[jax.ShapeDtypeStruct(W_SPECS[k][0], BF16) for k in names],
        in_specs=[VMEM] * n, out_specs=[VMEM] * n,
        scratch_shapes=[pltpu.SemaphoreType.DMA((6 * n,)), pltpu.SemaphoreType.DMA((6 * n,)), pltpu.SemaphoreType.DMA((n,))],
        compiler_params=_params(32),
    )(*shards)


def _slot_shapes(names):
    return [jax.ShapeDtypeStruct((N_DEV,) + _half_shape(k), BF16) for k in names]


def _scatter_grads(parts, names):
    n = len(names)

    def body(*refs):
        start, wait = _scatter_direct(names, refs[:n], refs[n:2 * n], *refs[2 * n:])
        start()
        wait()

    return pl.pallas_call(body, name="scatter_grads", out_shape=_slot_shapes(names), in_specs=[ANY] * n, out_specs=[ANY] * n,
                          scratch_shapes=_comm_sems(n))(*parts)


def _exchange_halves(halves, names, name):
    n = len(names)
    specs = [W_SPECS[k][1] for k in names]

    def shard_shape(k):
        shp = list(_half_shape(k))
        shp[W_SPECS[k][1][2]] *= 2
        return tuple(shp)

    def body(*refs):
        half, out = refs[:n], refs[n:2 * n]
        send_sems, recv_sems, local_sems = refs[2 * n:]
        x, y, c = _my_place()
        sibling = (x, y, 1 - c)

        def remote(a, src, dst):
            return pltpu.make_async_remote_copy(src_ref=src, dst_ref=dst, send_sem=send_sems.at[a], recv_sem=recv_sems.at[a],
                                                device_id=sibling, device_id_type=MESH)

        local = [pltpu.make_async_copy(half[a], _half_of(out[a], specs[a], c), local_sems.at[a]) for a in range(n)]
        for cp in local:
            cp.start()
        for a in range(n):
            mine = _half_of(out[a], specs[a], c)
            for src, dst in zip(_pieces(half[a]), _pieces(mine)):
                remote(a, src, dst).start()
        for a in range(n):
            theirs = _half_of(out[a], specs[a], 1 - c)
            remote(a, theirs, theirs).wait_recv()
        for a in range(n):
            remote(a, half[a], half[a]).wait_send()
        for cp in local:
            cp.wait()

    return pl.pallas_call(
        body, name=name,
        out_shape=[jax.ShapeDtypeStruct(shard_shape(k), F32) for k in names],
        in_specs=[VMEM] * n, out_specs=[VMEM] * n,
        scratch_shapes=[pltpu.SemaphoreType.DMA((n,)), pltpu.SemaphoreType.DMA((n,)), pltpu.SemaphoreType.DMA((n,))],
        compiler_params=_params(32),
    )(*halves)


def _mod_parts(c_rows, ada_w, ada_b_cols):
    n_layers, _, n_cols = ada_w.shape

    def body(c_ref, w_ref, b_ref, o_ref):
        o_ref[...] = _mm_f32(_silu(c_ref[...]), w_ref[...]) + b_ref[...]

    return pl.pallas_call(
        body, name="mod_parts", grid=(n_layers,),
        out_shape=jax.ShapeDtypeStruct((n_layers, 16, n_cols), F32),
        in_specs=[pl.BlockSpec((16, D_MODEL), lambda i: (0, 0)),
                  pl.BlockSpec((None, D_MODEL, n_cols), lambda i: (i, 0, 0)),
                  pl.BlockSpec((None, 1, n_cols), lambda i: (i, 0, 0))],
        out_specs=pl.BlockSpec((None, 16, n_cols), lambda i: (i, 0, 0)),
        compiler_params=_params(40),
    )(c_rows, ada_w, ada_b_cols.reshape(n_layers, 1, n_cols))


def _ln_mod_matmul(ctx_tile, xin, nw, shift, scale, w, name):
    n_mod = shift.shape[0]
    tm = ROW_TILE
    skip = n_mod - 1
    rows = xin.shape[0] + skip * tm
    n_sec = w.shape[1] // D_MODEL

    def body(*refs):
        c_ref = refs[0] if skip else None
        x_ref, nw_ref, sh_ref, sc_ref, w_ref, g_ref, h_ref = refs[skip:]
        xv = jnp.where(pl.program_id(0) == 0, c_ref[...], x_ref[...]) if skip else x_ref[...]
        r = lax.rsqrt(jnp.mean(xv * xv, axis=-1, keepdims=True) + EPS)
        h_f32 = (xv * r * nw_ref[...]) * (1.0 + sc_ref[...]) + sh_ref[...]
        h = h_f32.astype(BF16)
        h_ref[...] = h_f32.T.astype(BF16)
        for k in range(n_sec):
            g_ref[k] = jnp.dot(h, w_ref[:, k * D_MODEL:(k + 1) * D_MODEL], preferred_element_type=F32)

    mod_spec = pl.BlockSpec((None, 1, D_MODEL), lambda i: (jnp.minimum(i, n_mod - 1), 0, 0))
    return pl.pallas_call(
        body, name=name, grid=(rows // tm,),
        out_shape=[pltpu.HBM((n_sec, rows, D_MODEL), F32), pltpu.HBM((D_MODEL, rows), BF16)],
        in_specs=[pl.BlockSpec((tm, D_MODEL), lambda i: (0, 0))] * skip
        + [pl.BlockSpec((tm, D_MODEL), lambda i: (jnp.maximum(i - skip, 0), 0)),
           pl.BlockSpec((1, D_MODEL), lambda i: (0, 0)),
           mod_spec, mod_spec,
           pl.BlockSpec((D_MODEL, n_sec * D_MODEL), lambda i: (0, 0))],
        out_specs=[pl.BlockSpec((n_sec, tm, D_MODEL), lambda i: (0, i, 0)),
                   pl.BlockSpec((D_MODEL, tm), lambda i: (0, i))],
        compiler_params=_params(48),
    )(*_pin(*([ctx_tile] * skip), xin, nw, shift.reshape(n_mod, 1, D_MODEL), scale.reshape(n_mod, 1, D_MODEL), w))


def _chunk_masks(rev):
    rid = lax.broadcasted_iota(jnp.int32, (CHUNK, CHUNK), 0)
    cid = lax.broadcasted_iota(jnp.int32, (CHUNK, CHUNK), 1)
    keep = (cid >= rid) if rev else (cid <= rid)
    keep_t = (cid <= rid) if rev else (cid >= rid)
    one, zero = jnp.ones((CHUNK, CHUNK), F32), jnp.zeros((CHUNK, CHUNK), F32)
    return keep, jnp.where(keep, one, zero).astype(BF16), jnp.where(keep_t, one, zero).astype(BF16)


def _chunk_rows(t, rev, latent):
    n = N_LAT_CHUNKS if latent else N_CTX_CHUNKS
    base = CTX_LEN if latent else 0
    idx = (n - 1 - t) if rev else t
    return pl.multiple_of(base + idx * CHUNK, CHUNK)


def _gates(fpre, lb):
    sg = _sig(fpre)
    f = lb + (1.0 - lb) * sg
    return sg, f, 1.0 - f, jnp.log(f)


G_SPEC = lambda sec: pl.BlockSpec((None, ROWS_ALL, HEAD_DIM), lambda h, sec=sec: (sec, 0, h))


def _gla_forward(g5, lb_logits):
    q_scale = HEAD_DIM ** -0.5

    def body(ff_ref, fb_ref, v_ref, q_ref, lg_ref, o_ref, st_ref, decay_ref, qt_ref):
        for rev in (False, True):
            f_ref = fb_ref if rev else ff_ref
            lb = _sig(lg_ref[1:2, :] if rev else lg_ref[0:1, :])
            keep, tri, _ = _chunk_masks(rev)
            last = 0 if rev else CHUNK - 1
            mid = CHUNK // 2 if rev else CHUNK // 2 - 1

            def local_step(t, carry, latent):
                r0 = _chunk_rows(t, rev, latent)
                rows = pl.ds(r0, CHUNK)
                step = t + (N_CTX_CHUNKS if latent else 0)
                _, _, k, lf = _gates(f_ref[rows, :], lb)
                v = v_ref[rows, :]
                b = _mm_exact_lhs(tri, lf)
                bl = b[last:last + 1, :]
                if latent:
                    q = _silu(q_ref[rows, :]) * q_scale
                    bm = b[mid:mid + 1, :]
                    a = _mm_nt(q * jnp.exp(b - bm), k * jnp.exp(bm - b))
                    o = _mm(jnp.where(keep, a, 0.0), v)
                    orow = pl.ds(pl.multiple_of(r0 - CTX_LEN, CHUNK), CHUNK)
                    qt_ref[orow, :] = (q * jnp.exp(b)).astype(BF16)
                    if rev:
                        o_ref[orow, :] += o
                    else:
                        o_ref[orow, :] = o
                decay_ref[step] = jnp.exp(bl)
                st_ref[step] = _mm_tn(v, k * jnp.exp(bl - b))
                return carry

            lax.fori_loop(0, N_CTX_CHUNKS, functools.partial(local_step, latent=False), 0, unroll=2)
            lax.fori_loop(0, N_LAT_CHUNKS, functools.partial(local_step, latent=True), 0, unroll=4)

            def scan_step(t, st):
                update = st_ref[t]
                st_ref[t] = st
                return st * decay_ref[t] + update

            lax.fori_loop(0, N_CHUNKS, scan_step, jnp.zeros((HEAD_DIM, HEAD_DIM), F32), unroll=2)

            def inter_step(t, carry):
                r0 = _chunk_rows(t, rev, True)
                orow = pl.ds(pl.multiple_of(r0 - CTX_LEN, CHUNK), CHUNK)
                o_ref[orow, :] += lax.dot_general(qt_ref[orow, :], st_ref[t + N_CTX_CHUNKS].astype(BF16),
                                                  (((1,), (1,)), ((), ())), preferred_element_type=F32)
                return carry

            lax.fori_loop(0, N_LAT_CHUNKS, inter_step, 0, unroll=4)

    return pl.pallas_call(
        body, name="gla_forward", grid=(HEADS,),
        out_shape=jax.ShapeDtypeStruct((SEQ, D_MODEL), F32),
        in_specs=[G_SPEC(0), G_SPEC(1), G_SPEC(2), G_SPEC(3), pl.BlockSpec((2, HEAD_DIM), lambda h: (0, h))],
        out_specs=pl.BlockSpec((SEQ, HEAD_DIM), lambda h: (0, h)),
        scratch_shapes=[pltpu.VMEM((N_CHUNKS, HEAD_DIM, HEAD_DIM), F32), pltpu.VMEM((N_CHUNKS, 1, HEAD_DIM), F32),
                        pltpu.VMEM((SEQ, HEAD_DIM), BF16)],
        compiler_params=_params(32),
    )(g5, g5, g5, g5, lb_logits)


def _gla_backward(g5, lb_logits, d_o, d_z):
    q_scale = HEAD_DIM ** -0.5

    def body(ff_ref, fb_ref, v_ref, q_ref, lg_ref, do_ref, dz_ref, dg_ref, dlg_ref, st_ref, dst_ref, decay_ref):
        dg_ref[3, 0:CTX_LEN, :] = jnp.zeros((CTX_LEN, HEAD_DIM), F32)
        dg_ref[4, 0:CTX_LEN, :] = jnp.zeros((CTX_LEN, HEAD_DIM), F32)
        dg_ref[4, CTX_LEN:ROWS_ALL, :] = dz_ref[...]
        is_row = lax.broadcasted_iota(jnp.int32, (CHUNK, HEAD_DIM), 0)
        for rev in (False, True):
            d = 1 if rev else 0
            f_ref = fb_ref if rev else ff_ref
            lb = _sig(lg_ref[d:d + 1, :])
            keep, tri, tri_t = _chunk_masks(rev)
            last = 0 if rev else CHUNK - 1
            mid = CHUNK // 2 if rev else CHUNK // 2 - 1

            def local_step(t, carry, latent):
                r0 = _chunk_rows(t, rev, latent)
                rows = pl.ds(r0, CHUNK)
                step = t + (N_CTX_CHUNKS if latent else 0)
                _, _, k, lf = _gates(f_ref[rows, :], lb)
                b = _mm_exact_lhs(tri, lf)
                bl = b[last:last + 1, :]
                decay_ref[step] = jnp.exp(bl)
                st_ref[step] = _mm_tn(v_ref[rows, :], k * jnp.exp(bl - b))
                if latent:
                    q_t = _silu(q_ref[rows, :]) * q_scale * jnp.exp(b)
                    dst_ref[step] = _mm_tn(do_ref[pl.ds(pl.multiple_of(r0 - CTX_LEN, CHUNK), CHUNK), :], q_t)
                else:
                    dst_ref[step] = jnp.zeros((HEAD_DIM, HEAD_DIM), F32)
                return carry

            lax.fori_loop(0, N_CTX_CHUNKS, functools.partial(local_step, latent=False), 0, unroll=2)
            lax.fori_loop(0, N_LAT_CHUNKS, functools.partial(local_step, latent=True), 0, unroll=4)

            def scan_step(i, carry):
                st, d_st = carry
                j = N_CHUNKS - 1 - i
                update, d_update = st_ref[i], dst_ref[j]
                st_ref[i] = st
                dst_ref[j] = d_st
                return st * decay_ref[i] + update, d_st * decay_ref[j] + d_update

            zero_state = jnp.zeros((HEAD_DIM, HEAD_DIM), F32)
            lax.fori_loop(0, N_CHUNKS, scan_step, (zero_state, zero_state))

            def grad_step(t, dlb, latent):
                r0 = _chunk_rows(t, rev, latent)
                rows = pl.ds(r0, CHUNK)
                step = t + (N_CTX_CHUNKS if latent else 0)
                sg, f, k, lf = _gates(f_ref[rows, :], lb)
                v = v_ref[rows, :]
                b = _mm_exact_lhs(tri, lf)
                bl = b[last:last + 1, :]
                e_end = jnp.exp(bl - b)
                k_end = k * e_end
                decay = jnp.exp(bl)
                d_st = dst_ref[step]
                st_prev = st_ref[step]
                d_kend = _mm(v, d_st)
                d_decay = jnp.sum(d_st * st_prev, axis=0, keepdims=True)
                t_kend = d_kend * k_end
                d_v = _mm_nt(k_end, d_st)
                d_k = d_kend * e_end
                d_b = -t_kend
                if latent:
                    qpre = q_ref[rows, :]
                    q = _silu(qpre) * q_scale
                    bm = b[mid:mid + 1, :]
                    e_b, e_qm, e_km = jnp.exp(b), jnp.exp(b - bm), jnp.exp(bm - b)
                    q_t, q_m, k_m = q * e_b, q * e_qm, k * e_km
                    a = jnp.where(keep, _mm_nt(q_m, k_m), 0.0)
                    d_out = do_ref[pl.ds(pl.multiple_of(r0 - CTX_LEN, CHUNK), CHUNK), :]
                    d_a = jnp.where(keep, _mm_nt(d_out, v), 0.0)
                    d_qm = _mm(d_a, k_m)
                    d_km = _mm_tn(d_a, q_m)
                    d_qt = _mm(d_out, st_prev)
                    d_v = d_v + _mm_tn(a, d_out)
                    d_k = d_k + d_km * e_km
                    d_b = d_b + d_qt * q_t + d_qm * q_m - d_km * k_m
                    d_q = d_qt * e_b + d_qm * e_qm
                at_last = jnp.sum(t_kend, axis=0, keepdims=True) + d_decay * decay
                d_b = d_b + jnp.where(is_row == last, at_last, 0.0)
                d_lf = _mm_exact_lhs(tri_t, d_b)
                d_f = d_lf / f - d_k
                dg_ref[d, rows, :] = d_f * (1.0 - lb) * sg * (1.0 - sg)
                if rev:
                    dg_ref[2, rows, :] += d_v
                else:
                    dg_ref[2, rows, :] = d_v
                if latent:
                    d_qpre = d_q * q_scale * _dsilu(qpre)
                    if rev:
                        dg_ref[3, rows, :] += d_qpre
                    else:
                        dg_ref[3, rows, :] = d_qpre
                return dlb + jnp.sum(d_f * (1.0 - sg), axis=0, keepdims=True)

            dlb = lax.fori_loop(0, N_LAT_CHUNKS, functools.partial(grad_step, latent=True), jnp.zeros((1, HEAD_DIM), F32),
                                unroll=2)
            dlb = lax.fori_loop(0, N_CTX_CHUNKS, functools.partial(grad_step, latent=False), dlb, unroll=2)
            dlg_ref[d:d + 1, :] = dlb * lb * (1.0 - lb)

    col = pl.BlockSpec((SEQ, HEAD_DIM), lambda h: (0, h))
    return pl.pallas_call(
        body, name="gla_backward", grid=(HEADS,),
        out_shape=[jax.ShapeDtypeStruct((HGRN_SECTIONS, ROWS_ALL, D_MODEL), F32), jax.ShapeDtypeStruct((2, D_MODEL), F32)],
        in_specs=[G_SPEC(0), G_SPEC(1), G_SPEC(2), G_SPEC(3), pl.BlockSpec((2, HEAD_DIM), lambda h: (0, h)), col, col],
        out_specs=[pl.BlockSpec((HGRN_SECTIONS, ROWS_ALL, HEAD_DIM), lambda h: (0, 0, h)),
                   pl.BlockSpec((2, HEAD_DIM), lambda h: (0, h))],
        scratch_shapes=[pltpu.VMEM((N_CHUNKS, HEAD_DIM, HEAD_DIM), F32), pltpu.VMEM((N_CHUNKS, HEAD_DIM, HEAD_DIM), F32),
                        pltpu.VMEM((N_CHUNKS, 1, HEAD_DIM), F32)],
        compiler_params=_params(48),
    )(g5, g5, g5, g5, lb_logits, d_o, d_z)


GROUP = 2 * CHUNK


def _group_masks(rev):
    rid = lax.broadcasted_iota(jnp.int32, (GROUP, GROUP), 0)
    cid = lax.broadcasted_iota(jnp.int32, (GROUP, GROUP), 1)
    same = (rid >= CHUNK) == (cid >= CHUNK)
    causal = (cid >= rid) if rev else (cid <= rid)
    anti = (cid <= rid) if rev else (cid >= rid)
    bf = lambda m: jnp.where(m, jnp.ones((GROUP, GROUP), F32), jnp.zeros((GROUP, GROUP), F32)).astype(BF16)
    keep = same & causal
    return keep, bf(keep), bf(same & anti), bf(same & jnp.logical_not(anti))


def _group_sum(m_bf, a):
    hi, lo = _split2(a)
    r = jnp.dot(m_bf, jnp.concatenate([hi, lo], axis=1), preferred_element_type=F32)
    return r[:, :HEAD_DIM] + r[:, HEAD_DIM:]


def _chunk_row(a, pos):
    return jnp.concatenate([jnp.broadcast_to(a[c * CHUNK + pos:c * CHUNK + pos + 1, :], (CHUNK, HEAD_DIM)) for c in range(2)], axis=0)


def _by_chunk(a, second):
    return jnp.concatenate([jnp.where(second, 0.0, a), jnp.where(second, a, 0.0)], axis=1)


def _own_block(r):
    return jnp.concatenate([r[0:CHUNK, 0:HEAD_DIM], r[CHUNK:GROUP, HEAD_DIM:2 * HEAD_DIM]], axis=0)


def _scan_step_of(row_chunk, rev, latent):
    if not rev:
        return row_chunk
    return (N_CHUNKS + N_CTX_CHUNKS - 1 - row_chunk) if latent else (N_CTX_CHUNKS - 1 - row_chunk)


def _group_rows(i, j, per_step, latent):
    base = CTX_LEN if latent else 0
    return pl.multiple_of(base + (i * per_step + j) * GROUP, GROUP)


GROUPS_PER_STEP = 8
GROUPS_PER_GRAD_STEP = 4


def _gla_forward(g5, lb_logits, late_shards):
    q_scale = HEAD_DIM ** -0.5
    per_lat, per_ctx = GROUPS_PER_STEP, min(GROUPS_PER_STEP, CTX_LEN // GROUP)
    n_late = len(GATHER_LATE)

    def body(ff_ref, fb_ref, v_ref, q_ref, lg_ref, *rest):
        shard_refs, o_ref, full_refs = rest[:n_late], rest[n_late], rest[n_late + 1:2 * n_late + 1]
        st_ref, decay_ref, qt_ref = rest[2 * n_late + 1:2 * n_late + 4]
        land_refs = rest[2 * n_late + 4:3 * n_late + 4]
        start_gather, forward_gather, finish_gather = _gather_two_level(GATHER_LATE, shard_refs, land_refs, full_refs,
                                                                        *rest[3 * n_late + 4:])

        @pl.when(pl.program_id(0) == 0)
        def _():
            start_gather()

        @pl.when(pl.program_id(0) == HEADS - 1)
        def _():
            forward_gather()

        second = lax.broadcasted_iota(jnp.int32, (GROUP, HEAD_DIM), 0) >= CHUNK
        for rev in (False, True):
            f_ref = fb_ref if rev else ff_ref
            lb = _sig(lg_ref[1:2, :] if rev else lg_ref[0:1, :])
            keep, tri, _, _ = _group_masks(rev)
            last = 0 if rev else CHUNK - 1
            mid = CHUNK // 2 if rev else CHUNK // 2 - 1

            def local_step(i, carry, latent, per):
                r0s = [_group_rows(i, j, per, latent) for j in range(per)]
                rows = [pl.ds(r0, GROUP) for r0 in r0s]
                gates = [_gates(f_ref[r, :], lb) for r in rows]
                vs = [v_ref[r, :] for r in rows]
                bs = [_group_sum(tri, g[3]) for g in gates]
                bls = [_chunk_row(b, last) for b in bs]
                ups = [_mm_tn(v, _by_chunk(g[2] * jnp.exp(bl - b), second)) for v, g, b, bl in zip(vs, gates, bs, bls)]
                if latent:
                    qs = [_silu(q_ref[r, :]) * q_scale for r in rows]
                    bms = [_chunk_row(b, mid) for b in bs]
                    a_s = [_mm_nt(q * jnp.exp(b - bm), g[2] * jnp.exp(bm - b)) for q, g, b, bm in zip(qs, gates, bs, bms)]
                    outs = [_mm(jnp.where(keep, a, 0.0), v) for a, v in zip(a_s, vs)]
                for j in range(per):
                    for c in range(2):
                        step = _scan_step_of(r0s[j] // CHUNK + c, rev, latent)
                        decay_ref[step] = jnp.exp(bls[j][c * CHUNK:c * CHUNK + 1, :])
                        st_ref[step] = ups[j][:, c * HEAD_DIM:(c + 1) * HEAD_DIM]
                    if latent:
                        orow = pl.ds(pl.multiple_of(r0s[j] - CTX_LEN, GROUP), GROUP)
                        qt_ref[orow, :] = (qs[j] * jnp.exp(bs[j])).astype(BF16)
                        if rev:
                            o_ref[orow, :] += outs[j]
                        else:
                            o_ref[orow, :] = outs[j]
                return carry

            lax.fori_loop(0, CTX_LEN // (per_ctx * GROUP), functools.partial(local_step, latent=False, per=per_ctx), 0)
            lax.fori_loop(0, SEQ // (per_lat * GROUP), functools.partial(local_step, latent=True, per=per_lat), 0)

            def scan_step(t, st):
                update = st_ref[t]
                st_ref[t] = st
                return st * decay_ref[t] + update

            lax.fori_loop(0, N_CHUNKS, scan_step, jnp.zeros((HEAD_DIM, HEAD_DIM), F32), unroll=2)

            def inter_step(i, carry):
                r0s = [_group_rows(i, j, per_lat, True) for j in range(per_lat)]
                orows = [pl.ds(pl.multiple_of(r0 - CTX_LEN, GROUP), GROUP) for r0 in r0s]
                states = [jnp.concatenate([st_ref[_scan_step_of(r0 // CHUNK + c, rev, True)] for c in range(2)], axis=0)
                          for r0 in r0s]
                prods = [lax.dot_general(qt_ref[orow, :], s.astype(BF16), (((1,), (1,)), ((), ())), preferred_element_type=F32)
                         for orow, s in zip(orows, states)]
                for orow, r in zip(orows, prods):
                    o_ref[orow, :] += _own_block(r)
                return carry

            lax.fori_loop(0, SEQ // (per_lat * GROUP), inter_step, 0)

        @pl.when(pl.program_id(0) == HEADS - 1)
        def _():
            finish_gather()

    outs = pl.pallas_call(
        body, name="gla_forward", grid=(HEADS,),
        out_shape=[pltpu.HBM((SEQ, D_MODEL), F32)] + [jax.ShapeDtypeStruct(W_SPECS[k][0], BF16) for k in GATHER_LATE],
        in_specs=[G_SPEC(0), G_SPEC(1), G_SPEC(2), G_SPEC(3), pl.BlockSpec((2, HEAD_DIM), lambda h: (0, h))] + [ANY] * n_late,
        out_specs=[pl.BlockSpec((SEQ, HEAD_DIM), lambda h: (0, h))] + [ANY] * n_late,
        scratch_shapes=[pltpu.VMEM((N_CHUNKS, HEAD_DIM, HEAD_DIM), F32), pltpu.VMEM((N_CHUNKS, 1, HEAD_DIM), F32),
                        pltpu.VMEM((SEQ, HEAD_DIM), BF16)] + [pltpu.VMEM(W_SPECS[k][0], BF16) for k in GATHER_LATE]
        + [pltpu.SemaphoreType.DMA((2 * n_late,)), pltpu.SemaphoreType.DMA((2 * n_late,)), pltpu.SemaphoreType.DMA((n_late,))],
        compiler_params=_params(48),
    )(*_pin(g5, g5, g5, g5, lb_logits), *late_shards)
    return outs[0], dict(zip(GATHER_LATE, outs[1:]))


def _gla_backward(g5, lb_logits, d_o, d_z, late_parts):
    q_scale = HEAD_DIM ** -0.5
    per_lat, per_ctx = GROUPS_PER_STEP, min(GROUPS_PER_STEP, CTX_LEN // GROUP)
    n_late = len(GATHER_LATE)

    def body(ff_ref, fb_ref, v_ref, q_ref, lg_ref, do_ref, dz_ref, *rest):
        part_refs, (dg_ref, dlg_ref), slot_refs = rest[:n_late], rest[n_late:n_late + 2], rest[n_late + 2:2 * n_late + 2]
        st_ref, dst_ref, decay_ref = rest[2 * n_late + 2:2 * n_late + 5]
        start_scatter, wait_scatter = _scatter_direct(GATHER_LATE, part_refs, slot_refs, *rest[2 * n_late + 5:])

        @pl.when(pl.program_id(0) == 0)
        def _():
            start_scatter()

        dg_ref[3, 0:CTX_LEN, :] = jnp.zeros((CTX_LEN, HEAD_DIM), F32)
        dg_ref[4, 0:CTX_LEN, :] = jnp.zeros((CTX_LEN, HEAD_DIM), F32)
        dg_ref[4, CTX_LEN:ROWS_ALL, :] = dz_ref[...]
        second = lax.broadcasted_iota(jnp.int32, (GROUP, HEAD_DIM), 0) >= CHUNK
        for rev in (False, True):
            d = 1 if rev else 0
            f_ref = fb_ref if rev else ff_ref
            lb = _sig(lg_ref[d:d + 1, :])
            keep, tri, tri_t, strict = _group_masks(rev)
            last = 0 if rev else CHUNK - 1
            mid = CHUNK // 2 if rev else CHUNK // 2 - 1

            def local_step(i, carry, latent, per):
                r0s = [_group_rows(i, j, per, latent) for j in range(per)]
                rows = [pl.ds(r0, GROUP) for r0 in r0s]
                gates = [_gates(f_ref[r, :], lb) for r in rows]
                bs = [_group_sum(tri, g[3]) for g in gates]
                bls = [_chunk_row(b, last) for b in bs]
                ups = [_mm_tn(v_ref[r, :], _by_chunk(g[2] * jnp.exp(bl - b), second)) for r, g, b, bl in zip(rows, gates, bs, bls)]
                if latent:
                    orows = [pl.ds(pl.multiple_of(r0 - CTX_LEN, GROUP), GROUP) for r0 in r0s]
                    d_ups = [_mm_tn(do_ref[orow, :], _by_chunk(_silu(q_ref[r, :]) * q_scale * jnp.exp(b), second))
                             for orow, r, b in zip(orows, rows, bs)]
                for j in range(per):
                    for c in range(2):
                        step = _scan_step_of(r0s[j] // CHUNK + c, rev, latent)
                        decay_ref[step] = jnp.exp(bls[j][c * CHUNK:c * CHUNK + 1, :])
                        st_ref[step] = ups[j][:, c * HEAD_DIM:(c + 1) * HEAD_DIM]
                        if latent:
                            dst_ref[step] = d_ups[j][:, c * HEAD_DIM:(c + 1) * HEAD_DIM]
                        else:
                            dst_ref[step] = jnp.zeros((HEAD_DIM, HEAD_DIM), F32)
                return carry

            lax.fori_loop(0, CTX_LEN // (per_ctx * GROUP), functools.partial(local_step, latent=False, per=per_ctx), 0)
            lax.fori_loop(0, SEQ // (per_lat * GROUP), functools.partial(local_step, latent=True, per=per_lat), 0)

            def scan_step(i, carry):
                st, d_st = carry
                j = N_CHUNKS - 1 - i
                update, d_update = st_ref[i], dst_ref[j]
                st_ref[i] = st
                dst_ref[j] = d_st
                return st * decay_ref[i] + update, d_st * decay_ref[j] + d_update

            zero_state = jnp.zeros((HEAD_DIM, HEAD_DIM), F32)
            lax.fori_loop(0, N_CHUNKS, scan_step, (zero_state, zero_state))

            def grad_step(i, dlb, latent, per):
                r0s = [_group_rows(i, j, per, latent) for j in range(per)]
                rows = [pl.ds(r0, GROUP) for r0 in r0s]
                gates = [_gates(f_ref[r, :], lb) for r in rows]
                vs = [v_ref[r, :] for r in rows]
                bs = [_group_sum(tri, g[3]) for g in gates]
                bls = [_chunk_row(b, last) for b in bs]
                e_ends = [jnp.exp(bl - b) for b, bl in zip(bs, bls)]
                k_ends = [g[2] * e for g, e in zip(gates, e_ends)]
                sts = [[st_ref[_scan_step_of(r0 // CHUNK + c, rev, latent)] for c in range(2)] for r0 in r0s]
                d_sts = [[dst_ref[_scan_step_of(r0 // CHUNK + c, rev, latent)] for c in range(2)] for r0 in r0s]
                d_kends = [_own_block(_mm(v, jnp.concatenate(ds, axis=1))) for v, ds in zip(vs, d_sts)]
                d_vs = [_own_block(_mm_nt(ke, jnp.concatenate(ds, axis=0))) for ke, ds in zip(k_ends, d_sts)]
                at_last = [jnp.concatenate([jnp.broadcast_to(jnp.sum(ds[c] * s[c], axis=0, keepdims=True), (CHUNK, HEAD_DIM))
                                            for c in range(2)], axis=0) * jnp.exp(bl) for ds, s, bl in zip(d_sts, sts, bls)]
                t_kends = [dk * ke for dk, ke in zip(d_kends, k_ends)]
                d_ks = [dk * e for dk, e in zip(d_kends, e_ends)]
                d_lfs = [_group_sum(strict, t) + al for t, al in zip(t_kends, at_last)]
                if latent:
                    orows = [pl.ds(pl.multiple_of(r0 - CTX_LEN, GROUP), GROUP) for r0 in r0s]
                    qpres = [q_ref[r, :] for r in rows]
                    qs = [_silu(qp) * q_scale for qp in qpres]
                    bms = [_chunk_row(b, mid) for b in bs]
                    e_bs = [jnp.exp(b) for b in bs]
                    e_qms = [jnp.exp(b - bm) for b, bm in zip(bs, bms)]
                    e_kms = [jnp.exp(bm - b) for b, bm in zip(bs, bms)]
                    q_ts = [q * e for q, e in zip(qs, e_bs)]
                    q_ms = [q * e for q, e in zip(qs, e_qms)]
                    k_ms = [g[2] * e for g, e in zip(gates, e_kms)]
                    d_outs = [do_ref[orow, :] for orow in orows]
                    a_s = [jnp.where(keep, _mm_nt(qm, km), 0.0) for qm, km in zip(q_ms, k_ms)]
                    d_as = [jnp.where(keep, _mm_nt(do, v), 0.0) for do, v in zip(d_outs, vs)]
                    d_qts = [_own_block(_mm(do, jnp.concatenate(s, axis=1))) for do, s in zip(d_outs, sts)]
                    d_qms = [_mm(da, km) for da, km in zip(d_as, k_ms)]
                    d_kms = [_mm_tn(da, qm) for da, qm in zip(d_as, q_ms)]
                    d_vs = [dv + _mm_tn(a, do) for dv, a, do in zip(d_vs, a_s, d_outs)]
                    d_ks = [dk + dkm * e for dk, dkm, e in zip(d_ks, d_kms, e_kms)]
                    d_lfs = [dl + _group_sum(tri_t, dqt * qt + dqm * qm - dkm * km)
                             for dl, dqt, qt, dqm, qm, dkm, km in zip(d_lfs, d_qts, q_ts, d_qms, q_ms, d_kms, k_ms)]
                    d_qs = [dqt * eb + dqm * eq for dqt, eb, dqm, eq in zip(d_qts, e_bs, d_qms, e_qms)]
                for j in range(per):
                    sg, f = gates[j][0], gates[j][1]
                    d_f = d_lfs[j] / f - d_ks[j]
                    dg_ref[d, rows[j], :] = d_f * (1.0 - lb) * sg * (1.0 - sg)
                    dlb = dlb + jnp.sum(d_f * (1.0 - sg), axis=0, keepdims=True)
                    if rev:
                        dg_ref[2, rows[j], :] += d_vs[j]
                    else:
                        dg_ref[2, rows[j], :] = d_vs[j]
                    if latent:
                        d_qpre = d_qs[j] * q_scale * _dsilu(qpres[j])
                        if rev:
                            dg_ref[3, rows[j], :] += d_qpre
                        else:
                            dg_ref[3, rows[j], :] = d_qpre
                return dlb

            dlb = lax.fori_loop(0, SEQ // (GROUPS_PER_GRAD_STEP * GROUP),
                                functools.partial(grad_step, latent=True, per=GROUPS_PER_GRAD_STEP), jnp.zeros((1, HEAD_DIM), F32))
            dlb = lax.fori_loop(0, CTX_LEN // (per_ctx * GROUP), functools.partial(grad_step, latent=False, per=per_ctx), dlb)
            dlg_ref[d:d + 1, :] = dlb * lb * (1.0 - lb)

        @pl.when(pl.program_id(0) == HEADS - 1)
        def _():
            wait_scatter()

    col = pl.BlockSpec((SEQ, HEAD_DIM), lambda h: (0, h))
    outs = pl.pallas_call(
        body, name="gla_backward", grid=(HEADS,),
        out_shape=[pltpu.HBM((HGRN_SECTIONS, ROWS_ALL, D_MODEL), F32), jax.ShapeDtypeStruct((2, D_MODEL), F32)]
        + _slot_shapes(GATHER_LATE),
        in_specs=[G_SPEC(0), G_SPEC(1), G_SPEC(2), G_SPEC(3), pl.BlockSpec((2, HEAD_DIM), lambda h: (0, h)), col, col]
        + [ANY] * n_late,
        out_specs=[pl.BlockSpec((HGRN_SECTIONS, ROWS_ALL, HEAD_DIM), lambda h: (0, 0, h)),
                   pl.BlockSpec((2, HEAD_DIM), lambda h: (0, h))] + [ANY] * n_late,
        scratch_shapes=[pltpu.VMEM((N_CHUNKS, HEAD_DIM, HEAD_DIM), F32), pltpu.VMEM((N_CHUNKS, HEAD_DIM, HEAD_DIM), F32),
                        pltpu.VMEM((N_CHUNKS, 1, HEAD_DIM), F32)] + _comm_sems(n_late),
        compiler_params=_params(48),
    )(*_pin(g5, g5, g5, g5, lb_logits, d_o, d_z), *late_parts)
    return outs[0], outs[1], dict(zip(GATHER_LATE, outs[2:]))


def _head_norm(o, gw, scr):
    rs = []
    for h in range(HEADS):
        cols = slice(h * HEAD_DIM, (h + 1) * HEAD_DIM)
        oh = o[:, cols]
        r = lax.rsqrt(jnp.mean(oh * oh, axis=-1, keepdims=True) + EPS)
        scr[:, cols] = oh * r
        rs.append(r)
    return rs


def _hgrn_out_forward(o_raw, g5, xin, gnorm_w, gate, w_out):
    tm = ROW_TILE

    def body(o_ref, z_ref, x_ref, gw_ref, gate_ref, w_ref, x1_ref, res_ref, scr):
        _head_norm(o_ref[...], None, scr)
        a = scr[...] * gw_ref[...] * _silu(z_ref[...])
        res = _mm(a, w_ref[...])
        res_ref[...] = res
        x1_ref[...] = x_ref[...] + gate_ref[...] * res

    tile = pl.BlockSpec((tm, D_MODEL), lambda i: (i, 0))
    vec = pl.BlockSpec((1, D_MODEL), lambda i: (0, 0))
    return pl.pallas_call(
        body, name="hgrn_out_forward", grid=(SEQ // tm,),
        out_shape=[pltpu.HBM((SEQ, D_MODEL), F32)] * 2,
        in_specs=[tile, pl.BlockSpec((None, tm, D_MODEL), lambda i: (4, i + CTX_LEN // tm, 0)), tile, vec, vec,
                  pl.BlockSpec((D_MODEL, D_MODEL), lambda i: (0, 0))],
        out_specs=[tile, tile],
        scratch_shapes=[pltpu.VMEM((tm, D_MODEL), F32)],
        compiler_params=_params(32),
    )(*_pin(o_raw, g5, xin, gnorm_w, gate, w_out))


def _hgrn_out_backward(d_x1, o_raw, g5, res, gnorm_w, gate, w_out):
    tm = ROW_TILE

    def body(dx_ref, o_ref, z_ref, res_ref, gw_ref, gate_ref, w_ref, do_ref, dz_ref, dw_out, dgate_ref, dgw_ref, scr, scr2,
             dw_ref):
        @pl.when(pl.program_id(0) == 0)
        def _():
            dw_ref[...] = jnp.zeros_like(dw_ref)
            dgate_ref[...] = jnp.zeros_like(dgate_ref)
            dgw_ref[...] = jnp.zeros_like(dgw_ref)

        dx = dx_ref[...]
        dgate_ref[...] += jnp.sum(dx * res_ref[...], axis=0, keepdims=True)
        d_res = (dx * gate_ref[...]).astype(BF16)
        d_a = _mm_nt(d_res, w_ref[...])
        rs = _head_norm(o_ref[...], None, scr)
        z = z_ref[...]
        sz = _silu(z)
        o_hat = scr[...]
        o_n = o_hat * gw_ref[...]
        dw_ref[...] += _mm_tn(o_n * sz, d_res)
        d_on = d_a * sz
        dz_ref[...] = d_a * o_n * _dsilu(z)
        dgw_ref[...] += jnp.sum(d_on * o_hat, axis=0, keepdims=True)
        scr2[...] = d_on * gw_ref[...]
        for h in range(HEADS):
            cols = slice(h * HEAD_DIM, (h + 1) * HEAD_DIM)
            dh, oh = scr2[:, cols], scr[:, cols]
            do_ref[:, cols] = rs[h] * (dh - oh * jnp.mean(dh * oh, axis=-1, keepdims=True))

        @pl.when(pl.program_id(0) == SEQ // tm - 1)
        def _():
            dw_out[...] = dw_ref[...].astype(BF16)

    tile = pl.BlockSpec((tm, D_MODEL), lambda i: (i, 0))
    vec = pl.BlockSpec((1, D_MODEL), lambda i: (0, 0))
    mat = pl.BlockSpec((D_MODEL, D_MODEL), lambda i: (0, 0))
    return pl.pallas_call(
        body, name="hgrn_out_backward", grid=(SEQ // tm,),
        out_shape=[pltpu.HBM((SEQ, D_MODEL), F32)] * 2 + [pltpu.HBM((D_MODEL, D_MODEL), BF16)]
        + [jax.ShapeDtypeStruct((1, D_MODEL), F32)] * 2,
        in_specs=[tile, tile, pl.BlockSpec((None, tm, D_MODEL), lambda i: (4, i + CTX_LEN // tm, 0)), tile, vec, vec, mat],
        out_specs=[tile, tile, mat, vec, vec],
        scratch_shapes=[pltpu.VMEM((tm, D_MODEL), F32)] * 2 + [pltpu.VMEM((D_MODEL, D_MODEL), F32)],
        compiler_params=_params(40),
    )(*_pin(d_x1, o_raw, g5, res, gnorm_w, gate, w_out))


def _pool_constants():
    win = np.zeros((POOL_GROUPS, ROW_TILE, ROW_TILE), np.float32)
    inv = np.zeros((POOL_GROUPS, ROW_TILE, 1), np.float32)
    for g, w in enumerate(POOL_WINDOWS):
        for t in range(ROW_TILE):
            base, p = (t // GRID_W) * GRID_W, t % GRID_W
            lo = min(max(p - w // 2, 0), GRID_W)
            hi = min(max(p - w // 2 + w, 0), GRID_W)
            win[g, t, base + lo:base + hi] = 1.0
            inv[g, t, 0] = 1.0 / np.float32(hi - lo)
    return jnp.asarray(win, BF16), jnp.asarray(win.transpose(0, 2, 1), BF16), jnp.asarray(inv, F32)


def _pool_mix(u_ref, wg_ref, ps_ref, win_ref, inv_ref, pooled_scr, yg_scr):
    for g in range(POOL_GROUPS):
        cols = slice(g * POOL_GROUP_DIM, (g + 1) * POOL_GROUP_DIM)
        ug = u_ref[:, cols]
        pooled = _mm_exact_lhs(win_ref[g], ug) * inv_ref[g] - ug
        if pooled_scr is not None:
            pooled_scr[:, cols] = pooled
        yg_scr[:, cols] = _mm(pooled, wg_ref[g])


def _pool_forward_loss(uz, x1, target, gate, w_grp, pool_scale, w_out, final_w):
    tm = ROW_TILE
    win, _, inv = _pool_constants()

    def body(u_ref, z_ref, x_ref, t_ref, gate_ref, wg_ref, ps_ref, w_ref, fw_ref, win_ref, inv_ref,
             dx_ref, loss_ref, dfw_ref, dgate_ref, yg_scr):
        @pl.when(pl.program_id(0) == 0)
        def _():
            loss_ref[...] = jnp.zeros_like(loss_ref)
            dfw_ref[...] = jnp.zeros_like(dfw_ref)
            dgate_ref[...] = jnp.zeros_like(dgate_ref)

        _pool_mix(u_ref, wg_ref, ps_ref, win_ref, inv_ref, None, yg_scr)
        a = yg_scr[...] * ps_ref[...] * _silu(z_ref[...])
        res = _mm(a, w_ref[...])
        x2 = x_ref[...] + gate_ref[...] * res
        r = lax.rsqrt(jnp.mean(x2 * x2, axis=-1, keepdims=True) + EPS)
        xh = x2 * r
        fw = fw_ref[...]
        err = xh * fw - t_ref[...]
        loss_ref[...] += 0.5 * jnp.sum(jnp.mean(err * err, axis=-1, keepdims=True))
        d_y = err * (1.0 / D_MODEL)
        dfw_ref[...] += jnp.sum(d_y * xh, axis=0, keepdims=True)
        d_xh = d_y * fw
        d_x2 = r * (d_xh - xh * jnp.mean(d_xh * xh, axis=-1, keepdims=True))
        dx_ref[...] = d_x2
        dgate_ref[...] += jnp.sum(d_x2 * res, axis=0, keepdims=True)

    tile = pl.BlockSpec((tm, D_MODEL), lambda i: (i, 0))
    vec = pl.BlockSpec((1, D_MODEL), lambda i: (0, 0))
    grp = pl.BlockSpec((POOL_GROUPS, POOL_GROUP_DIM, POOL_GROUP_DIM), lambda i: (0, 0, 0))
    return pl.pallas_call(
        body, name="pool_forward_loss", grid=(SEQ // tm,),
        out_shape=[pltpu.HBM((SEQ, D_MODEL), F32), jax.ShapeDtypeStruct((8, 128), F32),
                   jax.ShapeDtypeStruct((1, D_MODEL), F32), jax.ShapeDtypeStruct((1, D_MODEL), F32)],
        in_specs=[pl.BlockSpec((None, tm, D_MODEL), lambda i: (0, i, 0)), pl.BlockSpec((None, tm, D_MODEL), lambda i: (1, i, 0)),
                  tile, tile, vec, grp, vec, pl.BlockSpec((D_MODEL, D_MODEL), lambda i: (0, 0)), vec, grp,
                  pl.BlockSpec((POOL_GROUPS, ROW_TILE, 1), lambda i: (0, 0, 0))],
        out_specs=[tile, pl.BlockSpec((8, 128), lambda i: (0, 0)), vec, vec],
        scratch_shapes=[pltpu.VMEM((tm, D_MODEL), F32)],
        compiler_params=_params(32),
    )(*_pin(uz, uz, x1, target, gate, w_grp, pool_scale, w_out, final_w, win, inv))


def _pool_backward(d_x2, uz, gate, w_grp, pool_scale, w_out):
    tm = ROW_TILE
    win, win_t, inv = _pool_constants()

    def body(dx_ref, u_ref, z_ref, gate_ref, wg_ref, ps_ref, w_ref, win_ref, wint_ref, inv_ref,
             duz_ref, dw_out, dwg_out, dps_ref, pooled_scr, yg_scr, dyg_scr, dw_ref, dwg_ref):
        @pl.when(pl.program_id(0) == 0)
        def _():
            dw_ref[...] = jnp.zeros_like(dw_ref)
            dwg_ref[...] = jnp.zeros_like(dwg_ref)
            dps_ref[...] = jnp.zeros_like(dps_ref)

        _pool_mix(u_ref, wg_ref, ps_ref, win_ref, inv_ref, pooled_scr, yg_scr)
        z = z_ref[...]
        sz = _silu(z)
        yg = yg_scr[...]
        y = yg * ps_ref[...]
        d_res = (dx_ref[...] * gate_ref[...]).astype(BF16)
        d_a = _mm_nt(d_res, w_ref[...])
        dw_ref[...] += _mm_tn(y * sz, d_res)
        d_y = d_a * sz
        duz_ref[1] = d_a * y * _dsilu(z)
        dps_ref[...] += jnp.sum(d_y * yg, axis=0, keepdims=True)
        dyg_scr[...] = d_y * ps_ref[...]
        for g in range(POOL_GROUPS):
            cols = slice(g * POOL_GROUP_DIM, (g + 1) * POOL_GROUP_DIM)
            d_yg = dyg_scr[:, cols].astype(BF16)
            d_pool = _mm_nt(d_yg, wg_ref[g])
            dwg_ref[g] += _mm_tn(pooled_scr[:, cols], d_yg)
            duz_ref[0, :, cols] = _mm_exact_lhs(wint_ref[g], d_pool * inv_ref[g]) - d_pool

        @pl.when(pl.program_id(0) == SEQ // tm - 1)
        def _():
            dw_out[...] = dw_ref[...].astype(BF16)
            dwg_out[...] = dwg_ref[...].astype(BF16)

    tile = pl.BlockSpec((tm, D_MODEL), lambda i: (i, 0))
    vec = pl.BlockSpec((1, D_MODEL), lambda i: (0, 0))
    mat = pl.BlockSpec((D_MODEL, D_MODEL), lambda i: (0, 0))
    grp = pl.BlockSpec((POOL_GROUPS, POOL_GROUP_DIM, POOL_GROUP_DIM), lambda i: (0, 0, 0))
    return pl.pallas_call(
        body, name="pool_backward", grid=(SEQ // tm,),
        out_shape=[pltpu.HBM((POOL_SECTIONS, SEQ, D_MODEL), F32), pltpu.HBM((D_MODEL, D_MODEL), BF16),
                   pltpu.HBM((POOL_GROUPS, POOL_GROUP_DIM, POOL_GROUP_DIM), BF16), jax.ShapeDtypeStruct((1, D_MODEL), F32)],
        in_specs=[tile, pl.BlockSpec((None, tm, D_MODEL), lambda i: (0, i, 0)), pl.BlockSpec((None, tm, D_MODEL), lambda i: (1, i, 0)),
                  vec, grp, vec, mat, grp, grp, pl.BlockSpec((POOL_GROUPS, ROW_TILE, 1), lambda i: (0, 0, 0))],
        out_specs=[pl.BlockSpec((POOL_SECTIONS, tm, D_MODEL), lambda i: (0, i, 0)), mat, grp, vec],
        scratch_shapes=[pltpu.VMEM((tm, D_MODEL), F32)] * 3 + [pltpu.VMEM((D_MODEL, D_MODEL), F32),
                                                               pltpu.VMEM((POOL_GROUPS, POOL_GROUP_DIM, POOL_GROUP_DIM), F32)],
        compiler_params=_params(40),
    )(*_pin(d_x2, uz, uz, gate, w_grp, pool_scale, w_out, win, win_t, inv))


def _ln_mod_backward(d_g, w, ctx_tile, xin, nw, scale, d_up, name):
    n_sec, rows, _ = d_g.shape
    n_mod = scale.shape[0]
    tm = ROW_TILE
    n_tiles = rows // tm
    skip = n_mod - 1

    def body(dg_ref, w_ref, *refs):
        c_ref = refs[0] if skip else None
        x_ref, nw_ref, sc_ref, up_ref, dx_ref, dnw_ref, dmod_ref = refs[skip:]
        i = pl.program_id(0)

        @pl.when(i == 0)
        def _():
            dnw_ref[...] = jnp.zeros_like(dnw_ref)

        @pl.when((i == 0) | (i == skip))
        def _():
            dmod_ref[...] = jnp.zeros_like(dmod_ref)

        d_h = _mm_nt(dg_ref[0], w_ref[:, 0:D_MODEL])
        for k in range(1, n_sec):
            d_h = d_h + _mm_nt(dg_ref[k], w_ref[:, k * D_MODEL:(k + 1) * D_MODEL])
        xv = jnp.where(i == 0, c_ref[...], x_ref[...]) if skip else x_ref[...]
        r = lax.rsqrt(jnp.mean(xv * xv, axis=-1, keepdims=True) + EPS)
        xh = xv * r
        nw_row = nw_ref[...]
        dmod_ref[0:1, :] += jnp.sum(d_h, axis=0, keepdims=True)
        dmod_ref[1:2, :] += jnp.sum(d_h * (xh * nw_row), axis=0, keepdims=True)
        d_xn = d_h * (1.0 + sc_ref[...])
        dnw_ref[...] += jnp.sum(d_xn * xh, axis=0, keepdims=True)
        d_xh = d_xn * nw_row

        @pl.when(i >= skip)
        def _():
            dx_ref[...] = up_ref[...] + r * (d_xh - xh * jnp.mean(d_xh * xh, axis=-1, keepdims=True))

    lat = lambda i: (jnp.maximum(i - skip, 0), 0)
    mod_idx = lambda i: (jnp.minimum(i, n_mod - 1), 0, 0)
    return pl.pallas_call(
        body, name=name, grid=(n_tiles,),
        out_shape=[pltpu.HBM((rows - skip * tm, D_MODEL), F32), jax.ShapeDtypeStruct((1, D_MODEL), F32),
                   jax.ShapeDtypeStruct((n_mod, 8, D_MODEL), F32)],
        in_specs=[pl.BlockSpec((n_sec, tm, D_MODEL), lambda i: (0, i, 0)),
                  pl.BlockSpec((D_MODEL, n_sec * D_MODEL), lambda i: (0, 0))]
        + [pl.BlockSpec((tm, D_MODEL), lambda i: (0, 0))] * skip
        + [pl.BlockSpec((tm, D_MODEL), lat),
           pl.BlockSpec((1, D_MODEL), lambda i: (0, 0)),
           pl.BlockSpec((None, 1, D_MODEL), mod_idx),
           pl.BlockSpec((tm, D_MODEL), lat)],
        out_specs=[pl.BlockSpec((tm, D_MODEL), lat), pl.BlockSpec((1, D_MODEL), lambda i: (0, 0)),
                   pl.BlockSpec((None, 8, D_MODEL), mod_idx)],
        compiler_params=_params(48),
    )(*_pin(d_g, w, *([ctx_tile] * skip), xin, nw, scale.reshape(n_mod, 1, D_MODEL), d_up))


def _weight_grad(h_t, d_g, name):
    n_sec, rows, _ = d_g.shape
    tm = 768 if rows % 768 == 0 else 512
    n_tiles = rows // tm

    def body(h_ref, dg_ref, dw_ref, acc):
        i = pl.program_id(1)
        prod = jnp.dot(h_ref[...], dg_ref[...].astype(BF16), preferred_element_type=F32)

        @pl.when(i == 0)
        def _():
            acc[...] = prod

        @pl.when((i > 0) & (i < n_tiles - 1))
        def _():
            acc[...] += prod

        @pl.when(i == n_tiles - 1)
        def _():
            dw_ref[...] = (acc[...] + prod).astype(BF16)

    return pl.pallas_call(
        body, name=name, grid=(n_sec, n_tiles),
        out_shape=pltpu.HBM((D_MODEL, n_sec * D_MODEL), BF16),
        in_specs=[pl.BlockSpec((D_MODEL, tm), lambda j, i: (0, i)), pl.BlockSpec((None, tm, D_MODEL), lambda j, i: (j, i, 0))],
        out_specs=pl.BlockSpec((D_MODEL, D_MODEL), lambda j, i: (0, j)),
        scratch_shapes=[pltpu.VMEM((D_MODEL, D_MODEL), F32)],
        compiler_params=_params(32),
    )(*_pin(h_t, d_g))


def _sum_slots(slots, name):
    _, rows, cols = slots.shape
    tr = 128

    def body(s_ref, o_ref):
        acc = s_ref[0].astype(F32)
        for d in range(1, N_DEV):
            acc = acc + s_ref[d].astype(F32)
        o_ref[...] = acc

    return pl.pallas_call(
        body, name=name, grid=(rows // tr,),
        out_shape=jax.ShapeDtypeStruct((rows, cols), F32),
        in_specs=[pl.BlockSpec((N_DEV, tr, cols), lambda i: (0, i, 0))],
        out_specs=pl.BlockSpec((tr, cols), lambda i: (i, 0)),
    )(*_pin(slots))


def _adamw_math(w, g, m, v):
    m = ADAM_B1 * m + (1.0 - ADAM_B1) * g
    v = ADAM_B2 * v + (1.0 - ADAM_B2) * (g * g)
    m_hat = m / (1.0 - ADAM_B1 ** ADAM_STEP)
    v_hat = v / (1.0 - ADAM_B2 ** ADAM_STEP)
    return -ADAM_LR * (m_hat / (jnp.sqrt(v_hat) + ADAM_EPS) + ADAM_WD * w), m, v


def _adamw(w, g, m, v, name):
    rows, cols = w.shape
    tr = rows if rows <= 128 else 128

    def body(w_ref, g_ref, m_ref, v_ref, d_ref, mo_ref, vo_ref):
        d_ref[...], mo_ref[...], vo_ref[...] = _adamw_math(w_ref[...], g_ref[...], m_ref[...], v_ref[...])

    tile = pl.BlockSpec((tr, cols), lambda i: (i, 0))
    return pl.pallas_call(
        body, name=name, grid=(rows // tr,),
        out_shape=[pltpu.HBM((rows, cols), F32)] * 3,
        in_specs=[tile] * 4, out_specs=[tile] * 3,
    )(*_pin(w, g, m, v))


def _sum_devices(gathered):
    def body(p_ref, o_ref):
        acc = p_ref[0:SMALL_ROWS, :]
        for d in range(1, N_DEV):
            acc = acc + p_ref[d * SMALL_ROWS:(d + 1) * SMALL_ROWS, :]
        o_ref[...] = acc

    return pl.pallas_call(body, name="sum_devices", out_shape=jax.ShapeDtypeStruct((SMALL_ROWS, D_MODEL), F32),
                          in_specs=[VMEM], out_specs=VMEM)(gathered)


def _ada_update(cond_t, d_mod, ada_w, m, v):
    n_layers, _, n_cols = ada_w.shape
    tr = ROW_TILE

    def body(c_ref, dm_ref, w_ref, m_ref, v_ref, g_ref, d_ref, mo_ref, vo_ref):
        g = _mm_f32(_silu(c_ref[...]), dm_ref[...])
        g_ref[...] = g
        d_ref[...], mo_ref[...], vo_ref[...] = _adamw_math(w_ref[...], g, m_ref[...], v_ref[...])

    tile = pl.BlockSpec((None, tr, n_cols), lambda l, i: (l, i, 0))
    return pl.pallas_call(
        body, name="ada_update", grid=(n_layers, D_MODEL // tr),
        out_shape=[pltpu.HBM(ada_w.shape, F32)] * 4,
        in_specs=[pl.BlockSpec((tr, 16), lambda l, i: (i, 0)), pl.BlockSpec((None, 16, n_cols), lambda l, i: (l, 0, 0)),
                  tile, tile, tile],
        out_specs=[tile] * 4,
    )(*_pin(cond_t, d_mod, ada_w, m, v))


def _cond_ctx_partial(d_modc, ada_w0):
    n_cols = ada_w0.shape[1]
    tr = ROW_TILE

    def body(dm_ref, w_ref, o_ref):
        o_ref[...] = jnp.sum(w_ref[...] * dm_ref[...], axis=-1, keepdims=True)

    return pl.pallas_call(
        body, name="cond_ctx_partial", grid=(D_MODEL // tr,),
        out_shape=jax.ShapeDtypeStruct((D_MODEL, 1), F32),
        in_specs=[pl.BlockSpec((1, n_cols), lambda i: (0, 0)), pl.BlockSpec((tr, n_cols), lambda i: (i, 0))],
        out_specs=pl.BlockSpec((tr, 1), lambda i: (i, 0)),
    )(d_modc, ada_w0)


def _cond_ctx_update(gathered, c_ctx, m, v):
    def body(p_ref, w_ref, m_ref, v_ref, g_ref, d_ref, mo_ref, vo_ref):
        acc = p_ref[0:1, :]
        for s in range(1, N_CHIPS):
            acc = acc + p_ref[16 * s:16 * s + 1, :]
        w = w_ref[...]
        g = acc * _dsilu(w)
        g_ref[...] = g
        d_ref[...], mo_ref[...], vo_ref[...] = _adamw_math(w, g, m_ref[...], v_ref[...])

    return pl.pallas_call(body, name="cond_ctx_update", out_shape=[jax.ShapeDtypeStruct((1, D_MODEL), F32)] * 4,
                          in_specs=[VMEM] * 4, out_specs=[VMEM] * 4)(gathered, c_ctx, m, v)


def _local_step(x2, ctx2, target, mod_mine, mod_ctx, lb_logits, scale_full, w_in_full, late_shards, norm_w, gnorm, final_w):
    row = lambda a: a.reshape(1, -1)
    shift0, scale0, gate0 = (row(a) for a in jnp.split(mod_mine[0], 3))
    shift1, scale1, gate1 = (row(a) for a in jnp.split(mod_mine[1], 3))
    shift_c, scale_c, _ = (row(a) for a in jnp.split(mod_ctx, 3))
    nw0, nw1 = norm_w[0:1], norm_w[1:2]
    scales0 = jnp.concatenate([scale_c, scale0])

    g5, h0 = _ln_mod_matmul(ctx2, x2, nw0, jnp.concatenate([shift_c, shift0]), scales0, w_in_full, "hgrn_in_forward")
    o_raw, full = _gla_forward(g5, lb_logits, late_shards)
    x1, res0 = _hgrn_out_forward(o_raw, g5, x2, gnorm, gate0, full["hgrn_w_out"])
    uz, h1 = _ln_mod_matmul(None, x1, nw1, shift1, scale1, full["pool_w_in"], "pool_in_forward")
    d_x2, loss_part, d_final, d_gate1 = _pool_forward_loss(uz, x1, target, gate1, full["pool_w_grp"], scale_full,
                                                           full["pool_w_out"], final_w)

    d_uz, dw_pool_out, dw_pool_grp, d_pscale = _pool_backward(d_x2, uz, gate1, full["pool_w_grp"], scale_full, full["pool_w_out"])
    d_x1, d_nw1, d_mod1 = _ln_mod_backward(d_uz, full["pool_w_in"], None, x1, nw1, scale1, d_x2, "pool_in_backward")
    dw_pool_in = _weight_grad(h1, d_uz, "pool_in_weight_grad")
    d_o, d_z, dw_hgrn_out, d_gate0, d_gnorm = _hgrn_out_backward(d_x1, o_raw, g5, res0, gnorm, gate0, full["hgrn_w_out"])
    late_grads = {"hgrn_w_out": dw_hgrn_out, "pool_w_in": dw_pool_in, "pool_w_grp": dw_pool_grp, "pool_w_out": dw_pool_out}
    d_g5, d_lb, late_slots = _gla_backward(g5, lb_logits, d_o, d_z, [late_grads[k].astype(BF16) for k in GATHER_LATE])
    dw_hgrn_in = _weight_grad(h0, d_g5, "hgrn_in_weight_grad")
    key = GATHER_EARLY[0]
    half = _half_shape(key)
    xi, yi, ci = _my_place()
    own = lax.dynamic_slice(dw_hgrn_in, (ci * half[0], (2 * xi + yi) * half[1]), half)
    slots0 = lax.dynamic_update_slice(jnp.zeros((N_DEV,) + half, BF16), own[None], (4 * xi + 2 * yi + ci, 0, 0))
    send_sem, recv_sem, part_thru, slots_thru, token = _scatter_start(dw_hgrn_in, slots0, own, key, 0)
    d_x, d_nw0, d_mod0 = _ln_mod_backward(d_g5, w_in_full, ctx2, x2, nw0 + token[0:1, 0:1], scales0, d_x1, "hgrn_in_backward")
    slots = dict(late_slots)
    pending = (send_sem, recv_sem, part_thru, slots_thru)

    zero = jnp.zeros((1, D_MODEL), F32)
    small = jnp.concatenate([d_mod0[1, 0:2], d_gate0, d_mod1[0, 0:2], d_gate1, d_mod0[0, 0:2], zero, d_nw0, d_nw1, d_gnorm,
                             d_final, d_pscale, d_lb, jnp.broadcast_to(loss_part[0:1, 0:1], (1, D_MODEL)),
                             jnp.zeros((SMALL_ROWS - 17, D_MODEL), F32)], axis=0)
    return {"d_x": d_x, "slots": slots, "pending": pending, "small": small}


def kernel(x, c, ctx, c_ctx, ada_w, ada_b, norm_w, hgrn_w_in, hgrn_lb_logits, hgrn_gnorm_w, hgrn_w_out, pool_w_in, pool_w_grp, pool_scale, pool_w_out, final_norm_w, loss_target, m_c_ctx, m_ada_w, m_ada_b, m_norm_w, m_hgrn_w_in, m_hgrn_lb_logits, m_hgrn_gnorm_w, m_hgrn_w_out, m_pool_w_in, m_pool_w_grp, m_pool_scale, m_pool_w_out, m_final_norm_w, v_c_ctx, v_ada_w, v_ada_b, v_norm_w, v_hgrn_w_in, v_hgrn_lb_logits, v_hgrn_gnorm_w, v_hgrn_w_out, v_pool_w_in, v_pool_w_grp, v_pool_scale, v_pool_w_out, v_final_norm_w):
    xi, yi, ci = _my_place()
    chip = 2 * xi + yi
    dev = 4 * xi + 2 * yi + ci
    ada_cols = ada_w.shape[2]
    lb_cols = hgrn_lb_logits.shape[2]
    ps_cols = pool_scale.shape[1]
    row = lambda a: a.reshape(1, -1)

    def chip_cols(a, n):
        return lax.dynamic_slice_in_dim(a, chip * n, n, axis=a.ndim - 1)

    def from_chips(g, rows_per_dev, take):
        return jnp.concatenate([g[2 * s * rows_per_dev:2 * s * rows_per_dev + take] for s in range(N_CHIPS)], axis=1)

    first = jnp.concatenate([jnp.broadcast_to(c, (8, D_MODEL)), jnp.pad(hgrn_lb_logits[0], ((0, 6), (0, 0))),
                             jnp.pad(pool_scale, ((0, 7), (0, 0)))], axis=1)
    first_all = _all_gather_small(first, "gather_cond")
    cond_all = first_all[::8, :D_MODEL]
    lb_logits = from_chips(first_all[:, D_MODEL:D_MODEL + lb_cols], 8, 2)
    scale_full = from_chips(first_all[:, D_MODEL + lb_cols:], 8, 1)
    cond_rows = jnp.concatenate([cond_all, row(c_ctx), jnp.zeros((7, D_MODEL), F32)], axis=0)

    parts = _mod_parts(cond_rows, ada_w, chip_cols(ada_b, ada_cols))
    parts_all = _all_gather_small(parts.reshape(32, ada_cols), "gather_mod")
    mod_all = from_chips(parts_all, 32, 32).reshape(2, 16, 3 * D_MODEL)
    mod_mine = lax.dynamic_index_in_dim(mod_all, dev, axis=1, keepdims=False)

    shards = {"hgrn_w_in": hgrn_w_in[0], "hgrn_w_out": hgrn_w_out[0], "pool_w_in": pool_w_in[0],
              "pool_w_grp": pool_w_grp[0], "pool_w_out": pool_w_out[0]}
    w_in_full = _all_gather_weights([shards[k].astype(BF16) for k in GATHER_EARLY], GATHER_EARLY)[0]

    loc = _local_step(x[0], ctx[0], loss_target[0], mod_mine, mod_all[0, 8], lb_logits, scale_full, w_in_full,
                      [shards[k].astype(BF16) for k in GATHER_LATE], norm_w, hgrn_gnorm_w, row(final_norm_w))

    small_all = _all_gather_small(loc["small"], "gather_small")
    sums = _sum_devices(small_all)
    loss = sums[16, 0]

    part_c = _cond_ctx_partial(chip_cols(sums[6:9].reshape(1, -1), ada_cols), ada_w[0])
    part_all = _all_gather_small(jnp.pad(part_c.reshape(1, D_MODEL), ((0, 7), (0, 0))), "gather_cond_ctx")
    key = GATHER_EARLY[0]
    send0, recv0, part_thru, slots_thru = loc["pending"]
    send1, recv1, part_thru, slots_thru, _ = _scatter_start(part_thru, slots_thru, part_all, key, 1)

    def reduce_scattered(slots, names, name):
        halves = []
        for k in names:
            hs = _half_shape(k)
            halves.append(_sum_slots(slots[k].reshape(N_DEV, -1, hs[-1]), "sum_" + k).reshape(hs))
        return dict(zip(names, _exchange_halves(halves, names, name)))

    big_grads = reduce_scattered(loc["slots"], GATHER_LATE, "exchange_halves_late")

    out = {}

    def update(name, w, g, m, v):
        shape = w.shape
        w2, g2, m2, v2 = (a.reshape(-1, shape[-1]) for a in (w, g, m, v))
        d, mn, vn = _adamw(w2, g2, m2, v2, "adamw_" + name)
        out[name] = tuple(a.reshape(shape) for a in (g2, d, mn, vn))

    moments = {"hgrn_w_in": (m_hgrn_w_in, v_hgrn_w_in), "hgrn_w_out": (m_hgrn_w_out, v_hgrn_w_out),
               "pool_w_in": (m_pool_w_in, v_pool_w_in), "pool_w_grp": (m_pool_w_grp, v_pool_w_grp),
               "pool_w_out": (m_pool_w_out, v_pool_w_out)}
    weights = {"hgrn_w_in": hgrn_w_in, "hgrn_w_out": hgrn_w_out, "pool_w_in": pool_w_in, "pool_w_grp": pool_w_grp,
               "pool_w_out": pool_w_out}
    for k in GATHER_LATE:
        update(k, weights[k], big_grads[k], *moments[k])

    g_ada_b = jnp.stack([(sums[0:3] + sums[6:9]).reshape(-1), sums[3:6].reshape(-1)])
    update("ada_b", ada_b, g_ada_b, m_ada_b, v_ada_b)
    update("norm_w", norm_w, sums[9:11], m_norm_w, v_norm_w)
    update("hgrn_gnorm_w", hgrn_gnorm_w, sums[11:12], m_hgrn_gnorm_w, v_hgrn_gnorm_w)
    update("final_norm_w", row(final_norm_w), sums[12:13], row(m_final_norm_w), row(v_final_norm_w))
    update("pool_scale", pool_scale, chip_cols(sums[13:14], ps_cols), m_pool_scale, v_pool_scale)
    update("hgrn_lb_logits", hgrn_lb_logits, chip_cols(sums[14:16], lb_cols)[None], m_hgrn_lb_logits, v_hgrn_lb_logits)

    per_dev = small_all.reshape(N_DEV, SMALL_ROWS, D_MODEL)
    pad7 = jnp.zeros((7, 3 * D_MODEL), F32)
    dm0 = jnp.concatenate([per_dev[:, 0:3].reshape(N_DEV, -1), sums[6:9].reshape(1, -1), pad7], axis=0)
    dm1 = jnp.concatenate([per_dev[:, 3:6].reshape(N_DEV, -1), jnp.zeros((8, 3 * D_MODEL), F32)], axis=0)
    d_mod = chip_cols(jnp.stack([dm0, dm1]), ada_cols)
    out["ada_w"] = _ada_update(cond_rows.T, d_mod, ada_w, m_ada_w, v_ada_w)

    g_c, d_c, m_c, v_c = _cond_ctx_update(part_all, row(c_ctx), row(m_c_ctx), row(v_c_ctx))
    out["c_ctx"] = tuple(a.reshape(-1) for a in (g_c, d_c, m_c, v_c))
    out["final_norm_w"] = tuple(a.reshape(-1) for a in out["final_norm_w"])

    done = [g_c, out["ada_w"][1]] + [out[k][1] for k in GATHER_LATE]
    part_thru, slots_thru = _scatter_wait(send0, recv0, part_thru, slots_thru, done, key, 0)
    _, early = _scatter_wait(send1, recv1, part_thru, slots_thru, done, key, 1)
    big_grads = reduce_scattered({key: early}, GATHER_EARLY, "exchange_halves_early")
    for k in GATHER_EARLY:
        update(k, weights[k], big_grads[k], *moments[k])

    names = ["c_ctx", "ada_w", "ada_b", "norm_w", "hgrn_w_in", "hgrn_lb_logits", "hgrn_gnorm_w", "hgrn_w_out", "pool_w_in",
             "pool_w_grp", "pool_scale", "pool_w_out", "final_norm_w"]
    return (loss, loc["d_x"][None], *[out[k][0] for k in names], *[out[k][1] for k in names], *[out[k][2] for k in names],
            *[out[k][3] for k in names])
```

```python
import functools

import numpy as np
import jax
import jax.numpy as jnp
from jax import lax
from jax.experimental import pallas as pl
from jax.experimental.pallas import tpu as pltpu

F32 = jnp.float32
BF16 = jnp.bfloat16

D_MODEL = 1024
SEQ = 2048
CTX_LEN = 256
ROWS_ALL = CTX_LEN + SEQ
HEADS = 8
HEAD_DIM = 128
CHUNK = 64
N_CTX_CHUNKS = CTX_LEN // CHUNK
N_LAT_CHUNKS = SEQ // CHUNK
N_CHUNKS = N_CTX_CHUNKS + N_LAT_CHUNKS
GRID_W = 64
POOL_WINDOWS = (2, 4, 8, 16)
POOL_GROUPS = 4
POOL_GROUP_DIM = 256
HGRN_SECTIONS = 5
POOL_SECTIONS = 2
EPS = 1e-6
N_DEV = 8
N_CHIPS = 4
ROW_TILE = 256
SMALL_ROWS = 24

ADAM_LR = 0.001
ADAM_B1 = 0.9
ADAM_B2 = 0.999
ADAM_EPS = 1e-08
ADAM_WD = 0.01
ADAM_STEP = 10

MESH = pl.DeviceIdType.MESH
MIB = 1 << 20
ANY = pl.BlockSpec(memory_space=pl.ANY)
VMEM = pl.BlockSpec(memory_space=pltpu.VMEM)


def _params(vmem_mib=None):
    if vmem_mib is None:
        return pltpu.CompilerParams()
    return pltpu.CompilerParams(vmem_limit_bytes=vmem_mib * MIB)


def _pin(*operands):
    return [pltpu.with_memory_space_constraint(a, pltpu.HBM) if a.size * a.dtype.itemsize >= MIB else a for a in operands]


def _sig(a):
    return 1.0 / (1.0 + jnp.exp(-a))


def _silu(a):
    return a * _sig(a)


def _dsilu(a):
    s = _sig(a)
    return s * (1.0 + a * (1.0 - s))


def _mm(a, b):
    return jnp.dot(a.astype(BF16), b.astype(BF16), preferred_element_type=F32)


def _mm_nt(a, b):
    return lax.dot_general(a.astype(BF16), b.astype(BF16), (((1,), (1,)), ((), ())), preferred_element_type=F32)


def _mm_tn(a, b):
    return lax.dot_general(a.astype(BF16), b.astype(BF16), (((0,), (0,)), ((), ())), preferred_element_type=F32)


def _split2(a):
    hi = a.astype(BF16)
    lo = (a - hi.astype(F32)).astype(BF16)
    return hi, lo


def _mm_exact_lhs(m_bf, a):
    hi, lo = _split2(a)
    return jnp.dot(m_bf, hi, preferred_element_type=F32) + jnp.dot(m_bf, lo, preferred_element_type=F32)


def _mm_f32(a, b):
    ah, al = _split2(a)
    bh, bl = _split2(b)
    return (jnp.dot(ah, bh, preferred_element_type=F32) + jnp.dot(al, bh, preferred_element_type=F32)
            + jnp.dot(ah, bl, preferred_element_type=F32))


def _my_place():
    return lax.axis_index("x"), lax.axis_index("y"), lax.axis_index("c")


def _all_gather_small(blk, name):
    m_per, n = blk.shape

    def body(x_ref, out_ref, send_sems, recv_sems, local_sem):
        x, y, c = _my_place()
        me, sibling = (x, y, c), (x, y, 1 - c)
        chips = [(1 - x, y), (x, 1 - y), (1 - x, 1 - y)]

        def rows(px, py, pc):
            return out_ref.at[pl.ds((4 * px + 2 * py + pc) * m_per, m_per), :]

        def copy(k, block, to, src=None):
            return pltpu.make_async_remote_copy(
                src_ref=rows(*block) if src is None else src, dst_ref=rows(*block),
                send_sem=send_sems.at[k], recv_sem=recv_sems.at[k], device_id=to, device_id_type=MESH)

        mine = pltpu.make_async_copy(x_ref, rows(*me), local_sem)
        mine.start()
        first = [copy(0, me, sibling, src=x_ref)]
        first += [copy(1 + j, me, (*chip, c), src=x_ref) for j, chip in enumerate(chips)]
        for cp in first:
            cp.start()
        passed = [copy(4 + j, (*chip, c), sibling) for j, chip in enumerate(chips)]
        for j, chip in enumerate(chips):
            copy(1 + j, (*chip, c), me).wait_recv()
            passed[j].start()
        copy(0, sibling, me).wait_recv()
        for j, chip in enumerate(chips):
            copy(4 + j, (*chip, 1 - c), me).wait_recv()
        for cp in first + passed:
            cp.wait_send()
        mine.wait()

    return pl.pallas_call(
        body, name=name,
        out_shape=jax.ShapeDtypeStruct((N_DEV * m_per, n), blk.dtype),
        in_specs=[VMEM], out_specs=VMEM,
        scratch_shapes=[pltpu.SemaphoreType.DMA((7,)), pltpu.SemaphoreType.DMA((7,)), pltpu.SemaphoreType.DMA],
    )(blk)


W_SPECS = {
    "hgrn_w_in": ((D_MODEL, 5 * D_MODEL), (1, 1280, 0, 512)),
    "hgrn_w_out": ((D_MODEL, D_MODEL), (0, 256, 0, 128)),
    "pool_w_in": ((D_MODEL, 2 * D_MODEL), (1, 512, 0, 512)),
    "pool_w_grp": ((POOL_GROUPS, POOL_GROUP_DIM, POOL_GROUP_DIM), (1, 64, 1, 32)),
    "pool_w_out": ((D_MODEL, D_MODEL), (0, 256, 0, 128)),
}
W_NAMES = tuple(W_SPECS)


def _al(v, m):
    return pl.multiple_of(v, m)


def _region(ref, spec, chip, half):
    ca, cn, ha, hn = spec
    idx = [slice(None)] * len(ref.shape)
    if ca == ha:
        if half is None:
            idx[ca] = pl.ds(_al(chip * cn, cn), cn)
        else:
            idx[ca] = pl.ds(_al(chip * cn + half * hn, hn), hn)
    else:
        idx[ca] = pl.ds(_al(chip * cn, cn), cn)
        if half is not None:
            idx[ha] = pl.ds(_al(half * hn, hn), hn)
    return ref.at[tuple(idx)]


def _half_of(ref, spec, half):
    _, _, ha, hn = spec
    idx = [slice(None)] * len(ref.shape)
    idx[ha] = pl.ds(_al(half * hn, hn), hn)
    return ref.at[tuple(idx)]


PIECE_BYTES = 256 * 1024


def _pieces(ref):
    lead = ref.shape[0]
    want = (int(np.prod(ref.shape)) * ref.dtype.itemsize) // PIECE_BYTES
    n = max([1] + [k for k in range(1, want + 1) if lead % k == 0 and (lead // k) % 16 == 0])
    rows = lead // n
    return [ref.at[pl.ds(i * rows, rows)] for i in range(n)]


def _half_shape(name):
    full, (ca, cn, ha, hn) = W_SPECS[name]
    shp = list(full)
    shp[ca] = cn
    shp[ha] = hn
    return tuple(shp)


def _gather_direct(names, sh, full, send_sems, recv_sems, local_sems):
    specs = [W_SPECS[k][1] for k in names]
    x, y, c = _my_place()
    chip_me = 2 * x + y
    chips = [(1 - x, y), (x, 1 - y), (1 - x, 1 - y)]

    def local(a):
        return pltpu.make_async_copy(sh[a], _region(full[a], specs[a], chip_me, None), local_sems.at[a])

    def remote(a, src, dst, to):
        return pltpu.make_async_remote_copy(src_ref=src, dst_ref=dst, send_sem=send_sems.at[a], recv_sem=recv_sems.at[a],
                                            device_id=to, device_id_type=MESH)

    def start():
        for a in range(len(names)):
            local(a).start()
            for px, py in chips:
                remote(a, sh[a], _region(full[a], specs[a], chip_me, None), (px, py, c)).start()

    def wait():
        for a in range(len(names)):
            ca, cn, _, _ = specs[a]
            idx = [slice(None)] * len(full[a].shape)
            idx[ca] = pl.ds(0, 3 * cn)
            three = full[a].at[tuple(idx)]
            remote(a, three, three, (x, y, c)).wait()
            local(a).wait()

    return start, wait


def _gather_two_level(names, sh, land, full, send_sems, recv_sems, local_sems):
    n = len(names)
    specs = [W_SPECS[k][1] for k in names]
    x, y, c = _my_place()
    chip_me = 2 * x + y
    sibling = (x, y, 1 - c)
    chips = [(1 - x, y), (x, 1 - y), (1 - x, 1 - y)]

    def remote(k, src, dst, to):
        return pltpu.make_async_remote_copy(src_ref=src, dst_ref=dst, send_sem=send_sems.at[k], recv_sem=recv_sems.at[k],
                                            device_id=to, device_id_type=MESH)

    def three_halves(a):
        ca, cn, ha, hn = specs[a]
        idx = [slice(None)] * len(land[a].shape)
        if ca == ha:
            idx[ca] = pl.ds(0, 3 * hn)
        else:
            idx[ca], idx[ha] = pl.ds(0, 3 * cn), pl.ds(0, hn)
        return land[a].at[tuple(idx)]

    def own(a):
        return pltpu.make_async_copy(sh[a], _region(land[a], specs[a], chip_me, None), local_sems.at[a])

    def out(a):
        return pltpu.make_async_copy(land[a], full[a], local_sems.at[a])

    def start():
        for a in range(n):
            own(a).start()
            for px, py in chips:
                remote(a, _half_of(sh[a], specs[a], c), _region(land[a], specs[a], chip_me, c), (px, py, c)).start()

    def forward():
        for a in range(n):
            remote(a, three_halves(a), three_halves(a), sibling).wait_recv()
            for px, py in chips:
                landed = _region(land[a], specs[a], 2 * px + py, c)
                remote(n + a, landed, landed, sibling).start()

    def finish():
        for a in range(n):
            remote(n + a, three_halves(a), three_halves(a), sibling).wait_recv()
            remote(a, three_halves(a), three_halves(a), sibling).wait_send()
            remote(n + a, three_halves(a), three_halves(a), sibling).wait_send()
            own(a).wait()
        for a in range(n):
            out(a).start()
        for a in range(n):
            out(a).wait()

    return start, forward, finish


def _scatter_direct(names, part, slots, send_sems, recv_sems, local_sems):
    specs = [W_SPECS[k][1] for k in names]
    x, y, c = _my_place()
    dev_me = 4 * x + 2 * y + c

    def local(a):
        return pltpu.make_async_copy(_region(part[a], specs[a], 2 * x + y, c), slots[a].at[dev_me], local_sems.at[a])

    def start():
        for a in range(len(names)):
            local(a).start()
            for flip in range(1, N_DEV):
                tx = 1 - x if flip >> 2 else x
                ty = 1 - y if (flip >> 1) & 1 else y
                tc = 1 - c if flip & 1 else c
                pltpu.make_async_remote_copy(src_ref=_region(part[a], specs[a], 2 * tx + ty, tc), dst_ref=slots[a].at[dev_me],
                                             send_sem=send_sems.at[a], recv_sem=recv_sems.at[a], device_id=(tx, ty, tc),
                                             device_id_type=MESH).start()

    def wait():
        for a in range(len(names)):
            seven = slots[a].at[pl.ds(0, N_DEV - 1)]
            pltpu.make_async_remote_copy(src_ref=seven, dst_ref=seven, send_sem=send_sems.at[a], recv_sem=recv_sems.at[a],
                                         device_id=(x, y, c), device_id_type=MESH).wait()
            local(a).wait()

    return start, wait


HBM_SPEC = pl.BlockSpec(memory_space=pltpu.HBM)
SEM_SPEC = pl.BlockSpec(memory_space=pltpu.SEMAPHORE)
SPLIT_EFFECT = pltpu.SideEffectType.DATAFLOW_SIDE_EFFECTING


SCATTER_STAGES = ((1, 3, 5, 7), (2, 4, 6))


def _scatter_start(part, slots, after, name_key, stage):
    spec = W_SPECS[name_key][1]

    def body(part_ref, slots_ref, after_ref, send_sem, recv_sem, part_thru, slots_thru, token):
        x, y, c = _my_place()
        dev_me = 4 * x + 2 * y + c
        for flip in SCATTER_STAGES[stage]:
            tx = 1 - x if flip >> 2 else x
            ty = 1 - y if (flip >> 1) & 1 else y
            tc = 1 - c if flip & 1 else c
            pltpu.make_async_remote_copy(src_ref=_region(part_ref, spec, 2 * tx + ty, tc), dst_ref=slots_ref.at[dev_me],
                                         send_sem=send_sem, recv_sem=recv_sem, device_id=(tx, ty, tc), device_id_type=MESH).start()
        token[...] = jnp.zeros_like(token)

    return pl.pallas_call(
        body, name="scatter_start_%s_%d" % (name_key, stage),
        out_shape=(pltpu.SemaphoreType.DMA(()), pltpu.SemaphoreType.DMA(()), pltpu.HBM(part.shape, part.dtype),
                   pltpu.HBM(slots.shape, slots.dtype), jax.ShapeDtypeStruct((8, 128), F32)),
        in_specs=(HBM_SPEC, HBM_SPEC, ANY), out_specs=(SEM_SPEC, SEM_SPEC, HBM_SPEC, HBM_SPEC, VMEM),
        input_output_aliases={0: 2, 1: 3},
        compiler_params=pltpu.CompilerParams(has_side_effects=SPLIT_EFFECT),
    )(pltpu.with_memory_space_constraint(part, pltpu.HBM), pltpu.with_memory_space_constraint(slots, pltpu.HBM), after)


def _scatter_wait(send_sem, recv_sem, part_thru, slots_thru, after, name_key, stage):
    n_copies = len(SCATTER_STAGES[stage])

    def body(part_ref, slots_ref, send_sem, recv_sem, *rest):
        x, y, c = _my_place()
        landed = slots_ref.at[pl.ds(0, n_copies)]
        copy = pltpu.make_async_remote_copy(src_ref=landed, dst_ref=landed, send_sem=send_sem, recv_sem=recv_sem,
                                            device_id=(x, y, c), device_id_type=MESH)
        copy.wait_send()
        copy.wait_recv()

    return pl.pallas_call(
        body, name="scatter_wait_%s_%d" % (name_key, stage),
        out_shape=(pltpu.HBM(part_thru.shape, part_thru.dtype), pltpu.HBM(slots_thru.shape, slots_thru.dtype)),
        in_specs=(HBM_SPEC, HBM_SPEC, SEM_SPEC, SEM_SPEC) + (ANY,) * len(after), out_specs=(HBM_SPEC, HBM_SPEC),
        input_output_aliases={0: 0, 1: 1},
        compiler_params=pltpu.CompilerParams(has_side_effects=SPLIT_EFFECT),
    )(part_thru, slots_thru, send_sem, recv_sem, *after)


def _comm_sems(n):
    return [pltpu.SemaphoreType.DMA((n,)), pltpu.SemaphoreType.DMA((n,)), pltpu.SemaphoreType.DMA((n,))]


GATHER_EARLY = ("hgrn_w_in",)
GATHER_LATE = ("hgrn_w_out", "pool_w_in", "pool_w_grp", "pool_w_out")


def _all_gather_weights(shards, names):
    n = len(names)
    specs = [W_SPECS[k][1] for k in names]

    def body(*refs):
        sh, full = refs[:n], refs[n:2 * n]
        send_sems, recv_sems, local_sems = refs[2 * n:]
        x, y, c = _my_place()
        chip_me = 2 * x + y
        sibling = (x, y, 1 - c)
        chips = [(1 - x, y), (x, 1 - y), (1 - x, 1 - y)]

        def remote(a, k, src, dst, to):
            return pltpu.make_async_remote_copy(src_ref=src, dst_ref=dst, send_sem=send_sems.at[6 * a + k],
                                                recv_sem=recv_sems.at[6 * a + k], device_id=to, device_id_type=MESH)

        local = [pltpu.make_async_copy(sh[a], _region(full[a], specs[a], chip_me, None), local_sems.at[a]) for a in range(n)]
        for cp in local:
            cp.start()
        sends = []
        for a in range(n):
            for j, (px, py) in enumerate(chips):
                src, dst = _half_of(sh[a], specs[a], c), _region(full[a], specs[a], chip_me, c)
                for s_piece, d_piece in zip(_pieces(src), _pieces(dst)):
                    remote(a, j, s_piece, d_piece, (px, py, c)).start()
                sends.append(remote(a, j, src, dst, (px, py, c)))
        for a in range(n):
            for j, (px, py) in enumerate(chips):
                landed = _region(full[a], specs[a], 2 * px + py, c)
                remote(a, j, landed, landed, (px, py, c)).wait_recv()
                for piece in _pieces(landed):
                    remote(a, 3 + j, piece, piece, sibling).start()
                sends.append(remote(a, 3 + j, landed, landed, sibling))
        for a in range(n):
            for j, (px, py) in enumerate(chips):
                other = _region(full[a], specs[a], 2 * px + py, 1 - c)
                remote(a, 3 + j, other, other, sibling).wait_recv()
        for cp in sends:
            cp.wait_send()
        for cp in local:
            cp.wait()

    return pl.pallas_call(
        body, name="all_gather_weights",
        out_shape=[jax.ShapeDtypeStruct(W_SPECS[k][0], BF16) for k in names],
        in_specs=[VMEM] * n, out_specs=[VMEM] * n,
        scratch_shapes=[pltpu.SemaphoreType.DMA((6 * n,)), pltpu.SemaphoreType.DMA((6 * n,)), pltpu.SemaphoreType.DMA((n,))],
        compiler_params=_params(32),
    )(*shards)


def _slot_shapes(names):
    return [jax.ShapeDtypeStruct((N_DEV,) + _half_shape(k), BF16) for k in names]


def _scatter_grads(parts, names):
    n = len(names)

    def body(*refs):
        start, wait = _scatter_direct(names, refs[:n], refs[n:2 * n], *refs[2 * n:])
        start()
        wait()

    return pl.pallas_call(body, name="scatter_grads", out_shape=_slot_shapes(names), in_specs=[ANY] * n, out_specs=[ANY] * n,
                          scratch_shapes=_comm_sems(n))(*parts)


def _exchange_halves(halves, names, name):
    n = len(names)
    specs = [W_SPECS[k][1] for k in names]

    def shard_shape(k):
        shp = list(_half_shape(k))
        shp[W_SPECS[k][1][2]] *= 2
        return tuple(shp)

    def body(*refs):
        half, out = refs[:n], refs[n:2 * n]
        send_sems, recv_sems, local_sems = refs[2 * n:]
        x, y, c = _my_place()
        sibling = (x, y, 1 - c)

        def remote(a, src, dst):
            return pltpu.make_async_remote_copy(src_ref=src, dst_ref=dst, send_sem=send_sems.at[a], recv_sem=recv_sems.at[a],
                                                device_id=sibling, device_id_type=MESH)

        local = [pltpu.make_async_copy(half[a], _half_of(out[a], specs[a], c), local_sems.at[a]) for a in range(n)]
        for cp in local:
            cp.start()
        for a in range(n):
            mine = _half_of(out[a], specs[a], c)
            for src, dst in zip(_pieces(half[a]), _pieces(mine)):
                remote(a, src, dst).start()
        for a in range(n):
            theirs = _half_of(out[a], specs[a], 1 - c)
            remote(a, theirs, theirs).wait_recv()
        for a in range(n):
            remote(a, half[a], half[a]).wait_send()
        for cp in local:
            cp.wait()

    return pl.pallas_call(
        body, name=name,
        out_shape=[jax.ShapeDtypeStruct(shard_shape(k), F32) for k in names],
        in_specs=[VMEM] * n, out_specs=[VMEM] * n,
        scratch_shapes=[pltpu.SemaphoreType.DMA((n,)), pltpu.SemaphoreType.DMA((n,)), pltpu.SemaphoreType.DMA((n,))],
        compiler_params=_params(32),
    )(*halves)


def _mod_parts(c_rows, ada_w, ada_b_cols):
    n_layers, _, n_cols = ada_w.shape

    def body(c_ref, w_ref, b_ref, o_ref):
        o_ref[...] = _mm_f32(_silu(c_ref[...]), w_ref[...]) + b_ref[...]

    return pl.pallas_call(
        body, name="mod_parts", grid=(n_layers,),
        out_shape=jax.ShapeDtypeStruct((n_layers, 16, n_cols), F32),
        in_specs=[pl.BlockSpec((16, D_MODEL), lambda i: (0, 0)),
                  pl.BlockSpec((None, D_MODEL, n_cols), lambda i: (i, 0, 0)),
                  pl.BlockSpec((None, 1, n_cols), lambda i: (i, 0, 0))],
        out_specs=pl.BlockSpec((None, 16, n_cols), lambda i: (i, 0, 0)),
        compiler_params=_params(40),
    )(c_rows, ada_w, ada_b_cols.reshape(n_layers, 1, n_cols))


def _ln_mod_matmul(ctx_tile, xin, nw, shift, scale, w, name):
    n_mod = shift.shape[0]
    tm = ROW_TILE
    skip = n_mod - 1
    rows = xin.shape[0] + skip * tm
    n_sec = w.shape[1] // D_MODEL

    def body(*refs):
        c_ref = refs[0] if skip else None
        x_ref, nw_ref, sh_ref, sc_ref, w_ref, g_ref, h_ref = refs[skip:]
        xv = jnp.where(pl.program_id(0) == 0, c_ref[...], x_ref[...]) if skip else x_ref[...]
        r = lax.rsqrt(jnp.mean(xv * xv, axis=-1, keepdims=True) + EPS)
        h_f32 = (xv * r * nw_ref[...]) * (1.0 + sc_ref[...]) + sh_ref[...]
        h = h_f32.astype(BF16)
        h_ref[...] = h
        for k in range(n_sec):
            g_ref[k] = jnp.dot(h, w_ref[:, k * D_MODEL:(k + 1) * D_MODEL], preferred_element_type=F32)

    mod_spec = pl.BlockSpec((None, 1, D_MODEL), lambda i: (jnp.minimum(i, n_mod - 1), 0, 0))
    return pl.pallas_call(
        body, name=name, grid=(rows // tm,),
        out_shape=[pltpu.HBM((n_sec, rows, D_MODEL), F32), pltpu.HBM((rows, D_MODEL), BF16)],
        in_specs=[pl.BlockSpec((tm, D_MODEL), lambda i: (0, 0))] * skip
        + [pl.BlockSpec((tm, D_MODEL), lambda i: (jnp.maximum(i - skip, 0), 0)),
           pl.BlockSpec((1, D_MODEL), lambda i: (0, 0)),
           mod_spec, mod_spec,
           pl.BlockSpec((D_MODEL, n_sec * D_MODEL), lambda i: (0, 0))],
        out_specs=[pl.BlockSpec((n_sec, tm, D_MODEL), lambda i: (0, i, 0)),
                   pl.BlockSpec((tm, D_MODEL), lambda i: (i, 0))],
        compiler_params=_params(48),
    )(*_pin(*([ctx_tile] * skip), xin, nw, shift.reshape(n_mod, 1, D_MODEL), scale.reshape(n_mod, 1, D_MODEL), w))


def _chunk_masks(rev):
    rid = lax.broadcasted_iota(jnp.int32, (CHUNK, CHUNK), 0)
    cid = lax.broadcasted_iota(jnp.int32, (CHUNK, CHUNK), 1)
    keep = (cid >= rid) if rev else (cid <= rid)
    keep_t = (cid <= rid) if rev else (cid >= rid)
    one, zero = jnp.ones((CHUNK, CHUNK), F32), jnp.zeros((CHUNK, CHUNK), F32)
    return keep, jnp.where(keep, one, zero).astype(BF16), jnp.where(keep_t, one, zero).astype(BF16)


def _chunk_rows(t, rev, latent):
    n = N_LAT_CHUNKS if latent else N_CTX_CHUNKS
    base = CTX_LEN if latent else 0
    idx = (n - 1 - t) if rev else t
    return pl.multiple_of(base + idx * CHUNK, CHUNK)


def _gates(fpre, lb):
    sg = _sig(fpre)
    f = lb + (1.0 - lb) * sg
    return sg, f, 1.0 - f, jnp.log(f)


G_SPEC = lambda sec: pl.BlockSpec((None, ROWS_ALL, HEAD_DIM), lambda h, sec=sec: (sec, 0, h))


def _gla_forward(g5, lb_logits):
    q_scale = HEAD_DIM ** -0.5

    def body(ff_ref, fb_ref, v_ref, q_ref, lg_ref, o_ref, st_ref, decay_ref, qt_ref):
        for rev in (False, True):
            f_ref = fb_ref if rev else ff_ref
            lb = _sig(lg_ref[1:2, :] if rev else lg_ref[0:1, :])
            keep, tri, _ = _chunk_masks(rev)
            last = 0 if rev else CHUNK - 1
            mid = CHUNK // 2 if rev else CHUNK // 2 - 1

            def local_step(t, carry, latent):
                r0 = _chunk_rows(t, rev, latent)
                rows = pl.ds(r0, CHUNK)
                step = t + (N_CTX_CHUNKS if latent else 0)
                _, _, k, lf = _gates(f_ref[rows, :], lb)
                v = v_ref[rows, :]
                b = _mm_exact_lhs(tri, lf)
                bl = b[last:last + 1, :]
                if latent:
                    q = _silu(q_ref[rows, :]) * q_scale
                    bm = b[mid:mid + 1, :]
                    a = _mm_nt(q * jnp.exp(b - bm), k * jnp.exp(bm - b))
                    o = _mm(jnp.where(keep, a, 0.0), v)
                    orow = pl.ds(pl.multiple_of(r0 - CTX_LEN, CHUNK), CHUNK)
                    qt_ref[orow, :] = (q * jnp.exp(b)).astype(BF16)
                    if rev:
                        o_ref[orow, :] += o
                    else:
                        o_ref[orow, :] = o
                decay_ref[step] = jnp.exp(bl)
                st_ref[step] = _mm_tn(v, k * jnp.exp(bl - b))
                return carry

            lax.fori_loop(0, N_CTX_CHUNKS, functools.partial(local_step, latent=False), 0, unroll=2)
            lax.fori_loop(0, N_LAT_CHUNKS, functools.partial(local_step, latent=True), 0, unroll=4)

            def scan_step(t, st):
                update = st_ref[t]
                st_ref[t] = st
                return st * decay_ref[t] + update

            lax.fori_loop(0, N_CHUNKS, scan_step, jnp.zeros((HEAD_DIM, HEAD_DIM), F32), unroll=2)

            def inter_step(t, carry):
                r0 = _chunk_rows(t, rev, True)
                orow = pl.ds(pl.multiple_of(r0 - CTX_LEN, CHUNK), CHUNK)
                o_ref[orow, :] += lax.dot_general(qt_ref[orow, :], st_ref[t + N_CTX_CHUNKS].astype(BF16),
                                                  (((1,), (1,)), ((), ())), preferred_element_type=F32)
                return carry

            lax.fori_loop(0, N_LAT_CHUNKS, inter_step, 0, unroll=4)

    return pl.pallas_call(
        body, name="gla_forward", grid=(HEADS,),
        out_shape=jax.ShapeDtypeStruct((SEQ, D_MODEL), F32),
        in_specs=[G_SPEC(0), G_SPEC(1), G_SPEC(2), G_SPEC(3), pl.BlockSpec((2, HEAD_DIM), lambda h: (0, h))],
        out_specs=pl.BlockSpec((SEQ, HEAD_DIM), lambda h: (0, h)),
        scratch_shapes=[pltpu.VMEM((N_CHUNKS, HEAD_DIM, HEAD_DIM), F32), pltpu.VMEM((N_CHUNKS, 1, HEAD_DIM), F32),
                        pltpu.VMEM((SEQ, HEAD_DIM), BF16)],
        compiler_params=_params(32),
    )(g5, g5, g5, g5, lb_logits)


def _gla_backward(g5, lb_logits, d_o, d_z):
    q_scale = HEAD_DIM ** -0.5

    def body(ff_ref, fb_ref, v_ref, q_ref, lg_ref, do_ref, dz_ref, dg_ref, dlg_ref, st_ref, dst_ref, decay_ref):
        dg_ref[3, 0:CTX_LEN, :] = jnp.zeros((CTX_LEN, HEAD_DIM), F32)
        dg_ref[4, 0:CTX_LEN, :] = jnp.zeros((CTX_LEN, HEAD_DIM), F32)
        dg_ref[4, CTX_LEN:ROWS_ALL, :] = dz_ref[...]
        is_row = lax.broadcasted_iota(jnp.int32, (CHUNK, HEAD_DIM), 0)
        for rev in (False, True):
            d = 1 if rev else 0
            f_ref = fb_ref if rev else ff_ref
            lb = _sig(lg_ref[d:d + 1, :])
            keep, tri, tri_t = _chunk_masks(rev)
            last = 0 if rev else CHUNK - 1
            mid = CHUNK // 2 if rev else CHUNK // 2 - 1

            def local_step(t, carry, latent):
                r0 = _chunk_rows(t, rev, latent)
                rows = pl.ds(r0, CHUNK)
                step = t + (N_CTX_CHUNKS if latent else 0)
                _, _, k, lf = _gates(f_ref[rows, :], lb)
                b = _mm_exact_lhs(tri, lf)
                bl = b[last:last + 1, :]
                decay_ref[step] = jnp.exp(bl)
                st_ref[step] = _mm_tn(v_ref[rows, :], k * jnp.exp(bl - b))
                if latent:
                    q_t = _silu(q_ref[rows, :]) * q_scale * jnp.exp(b)
                    dst_ref[step] = _mm_tn(do_ref[pl.ds(pl.multiple_of(r0 - CTX_LEN, CHUNK), CHUNK), :], q_t)
                else:
                    dst_ref[step] = jnp.zeros((HEAD_DIM, HEAD_DIM), F32)
                return carry

            lax.fori_loop(0, N_CTX_CHUNKS, functools.partial(local_step, latent=False), 0, unroll=2)
            lax.fori_loop(0, N_LAT_CHUNKS, functools.partial(local_step, latent=True), 0, unroll=4)

            def scan_step(i, carry):
                st, d_st = carry
                j = N_CHUNKS - 1 - i
                update, d_update = st_ref[i], dst_ref[j]
                st_ref[i] = st
                dst_ref[j] = d_st
                return st * decay_ref[i] + update, d_st * decay_ref[j] + d_update

            zero_state = jnp.zeros((HEAD_DIM, HEAD_DIM), F32)
            lax.fori_loop(0, N_CHUNKS, scan_step, (zero_state, zero_state))

            def grad_step(t, dlb, latent):
                r0 = _chunk_rows(t, rev, latent)
                rows = pl.ds(r0, CHUNK)
                step = t + (N_CTX_CHUNKS if latent else 0)
                sg, f, k, lf = _gates(f_ref[rows, :], lb)
                v = v_ref[rows, :]
                b = _mm_exact_lhs(tri, lf)
                bl = b[last:last + 1, :]
                e_end = jnp.exp(bl - b)
                k_end = k * e_end
                decay = jnp.exp(bl)
                d_st = dst_ref[step]
                st_prev = st_ref[step]
                d_kend = _mm(v, d_st)
                d_decay = jnp.sum(d_st * st_prev, axis=0, keepdims=True)
                t_kend = d_kend * k_end
                d_v = _mm_nt(k_end, d_st)
                d_k = d_kend * e_end
                d_b = -t_kend
                if latent:
                    qpre = q_ref[rows, :]
                    q = _silu(qpre) * q_scale
                    bm = b[mid:mid + 1, :]
                    e_b, e_qm, e_km = jnp.exp(b), jnp.exp(b - bm), jnp.exp(bm - b)
                    q_t, q_m, k_m = q * e_b, q * e_qm, k * e_km
                    a = jnp.where(keep, _mm_nt(q_m, k_m), 0.0)
                    d_out = do_ref[pl.ds(pl.multiple_of(r0 - CTX_LEN, CHUNK), CHUNK), :]
                    d_a = jnp.where(keep, _mm_nt(d_out, v), 0.0)
                    d_qm = _mm(d_a, k_m)
                    d_km = _mm_tn(d_a, q_m)
                    d_qt = _mm(d_out, st_prev)
                    d_v = d_v + _mm_tn(a, d_out)
                    d_k = d_k + d_km * e_km
                    d_b = d_b + d_qt * q_t + d_qm * q_m - d_km * k_m
                    d_q = d_qt * e_b + d_qm * e_qm
                at_last = jnp.sum(t_kend, axis=0, keepdims=True) + d_decay * decay
                d_b = d_b + jnp.where(is_row == last, at_last, 0.0)
                d_lf = _mm_exact_lhs(tri_t, d_b)
                d_f = d_lf / f - d_k
                dg_ref[d, rows, :] = d_f * (1.0 - lb) * sg * (1.0 - sg)
                if rev:
                    dg_ref[2, rows, :] += d_v
                else:
                    dg_ref[2, rows, :] = d_v
                if latent:
                    d_qpre = d_q * q_scale * _dsilu(qpre)
                    if rev:
                        dg_ref[3, rows, :] += d_qpre
                    else:
                        dg_ref[3, rows, :] = d_qpre
                return dlb + jnp.sum(d_f * (1.0 - sg), axis=0, keepdims=True)

            dlb = lax.fori_loop(0, N_LAT_CHUNKS, functools.partial(grad_step, latent=True), jnp.zeros((1, HEAD_DIM), F32),
                                unroll=2)
            dlb = lax.fori_loop(0, N_CTX_CHUNKS, functools.partial(grad_step, latent=False), dlb, unroll=2)
            dlg_ref[d:d + 1, :] = dlb * lb * (1.0 - lb)

    col = pl.BlockSpec((SEQ, HEAD_DIM), lambda h: (0, h))
    return pl.pallas_call(
        body, name="gla_backward", grid=(HEADS,),
        out_shape=[jax.ShapeDtypeStruct((HGRN_SECTIONS, ROWS_ALL, D_MODEL), F32), jax.ShapeDtypeStruct((2, D_MODEL), F32)],
        in_specs=[G_SPEC(0), G_SPEC(1), G_SPEC(2), G_SPEC(3), pl.BlockSpec((2, HEAD_DIM), lambda h: (0, h)), col, col],
        out_specs=[pl.BlockSpec((HGRN_SECTIONS, ROWS_ALL, HEAD_DIM), lambda h: (0, 0, h)),
                   pl.BlockSpec((2, HEAD_DIM), lambda h: (0, h))],
        scratch_shapes=[pltpu.VMEM((N_CHUNKS, HEAD_DIM, HEAD_DIM), F32), pltpu.VMEM((N_CHUNKS, HEAD_DIM, HEAD_DIM), F32),
                        pltpu.VMEM((N_CHUNKS, 1, HEAD_DIM), F32)],
        compiler_params=_params(48),
    )(g5, g5, g5, g5, lb_logits, d_o, d_z)


GROUP = 2 * CHUNK


def _group_masks(rev):
    rid = lax.broadcasted_iota(jnp.int32, (GROUP, GROUP), 0)
    cid = lax.broadcasted_iota(jnp.int32, (GROUP, GROUP), 1)
    same = (rid >= CHUNK) == (cid >= CHUNK)
    causal = (cid >= rid) if rev else (cid <= rid)
    anti = (cid <= rid) if rev else (cid >= rid)
    bf = lambda m: jnp.where(m, jnp.ones((GROUP, GROUP), F32), jnp.zeros((GROUP, GROUP), F32)).astype(BF16)
    keep = same & causal
    return keep, bf(keep), bf(same & anti), bf(same & jnp.logical_not(anti))


def _group_sum(m_bf, a):
    hi, lo = _split2(a)
    r = jnp.dot(m_bf, jnp.concatenate([hi, lo], axis=1), preferred_element_type=F32)
    return r[:, :HEAD_DIM] + r[:, HEAD_DIM:]


def _chunk_row(a, pos):
    return jnp.concatenate([jnp.broadcast_to(a[c * CHUNK + pos:c * CHUNK + pos + 1, :], (CHUNK, HEAD_DIM)) for c in range(2)], axis=0)


def _by_chunk(a, second):
    return jnp.concatenate([jnp.where(second, 0.0, a), jnp.where(second, a, 0.0)], axis=1)


def _own_block(r):
    return jnp.concatenate([r[0:CHUNK, 0:HEAD_DIM], r[CHUNK:GROUP, HEAD_DIM:2 * HEAD_DIM]], axis=0)


def _scan_step_of(row_chunk, rev, latent):
    if not rev:
        return row_chunk
    return (N_CHUNKS + N_CTX_CHUNKS - 1 - row_chunk) if latent else (N_CTX_CHUNKS - 1 - row_chunk)


def _group_rows(i, j, per_step, latent):
    base = CTX_LEN if latent else 0
    return pl.multiple_of(base + (i * per_step + j) * GROUP, GROUP)


GROUPS_PER_STEP = 8
GROUPS_PER_GRAD_STEP = 4


def _gla_forward(g5, lb_logits, late_shards):
    q_scale = HEAD_DIM ** -0.5
    per_lat, per_ctx = GROUPS_PER_STEP, min(GROUPS_PER_STEP, CTX_LEN // GROUP)
    n_late = len(GATHER_LATE)

    def body(ff_ref, fb_ref, v_ref, q_ref, lg_ref, *rest):
        shard_refs, o_ref, full_refs = rest[:n_late], rest[n_late], rest[n_late + 1:2 * n_late + 1]
        st_ref, decay_ref, qt_ref = rest[2 * n_late + 1:2 * n_late + 4]
        land_refs = rest[2 * n_late + 4:3 * n_late + 4]
        start_gather, forward_gather, finish_gather = _gather_two_level(GATHER_LATE, shard_refs, land_refs, full_refs,
                                                                        *rest[3 * n_late + 4:])

        @pl.when(pl.program_id(0) == 0)
        def _():
            start_gather()

        @pl.when(pl.program_id(0) == HEADS - 1)
        def _():
            forward_gather()

        second = lax.broadcasted_iota(jnp.int32, (GROUP, HEAD_DIM), 0) >= CHUNK
        for rev in (False, True):
            f_ref = fb_ref if rev else ff_ref
            lb = _sig(lg_ref[1:2, :] if rev else lg_ref[0:1, :])
            keep, tri, _, _ = _group_masks(rev)
            last = 0 if rev else CHUNK - 1
            mid = CHUNK // 2 if rev else CHUNK // 2 - 1

            def local_step(i, carry, latent, per):
                r0s = [_group_rows(i, j, per, latent) for j in range(per)]
                rows = [pl.ds(r0, GROUP) for r0 in r0s]
                gates = [_gates(f_ref[r, :], lb) for r in rows]
                vs = [v_ref[r, :] for r in rows]
                bs = [_group_sum(tri, g[3]) for g in gates]
                bls = [_chunk_row(b, last) for b in bs]
                ups = [_mm_tn(v, _by_chunk(g[2] * jnp.exp(bl - b), second)) for v, g, b, bl in zip(vs, gates, bs, bls)]
                if latent:
                    qs = [_silu(q_ref[r, :]) * q_scale for r in rows]
                    bms = [_chunk_row(b, mid) for b in bs]
                    a_s = [_mm_nt(q * jnp.exp(b - bm), g[2] * jnp.exp(bm - b)) for q, g, b, bm in zip(qs, gates, bs, bms)]
                    outs = [_mm(jnp.where(keep, a, 0.0), v) for a, v in zip(a_s, vs)]
                for j in range(per):
                    for c in range(2):
                        step = _scan_step_of(r0s[j] // CHUNK + c, rev, latent)
                        decay_ref[step] = jnp.exp(bls[j][c * CHUNK:c * CHUNK + 1, :])
                        st_ref[step] = ups[j][:, c * HEAD_DIM:(c + 1) * HEAD_DIM]
                    if latent:
                        orow = pl.ds(pl.multiple_of(r0s[j] - CTX_LEN, GROUP), GROUP)
                        qt_ref[orow, :] = (qs[j] * jnp.exp(bs[j])).astype(BF16)
                        if rev:
                            o_ref[orow, :] += outs[j]
                        else:
                            o_ref[orow, :] = outs[j]
                return carry

            lax.fori_loop(0, CTX_LEN // (per_ctx * GROUP), functools.partial(local_step, latent=False, per=per_ctx), 0)
            lax.fori_loop(0, SEQ // (per_lat * GROUP), functools.partial(local_step, latent=True, per=per_lat), 0)

            def scan_step(t, st):
                update = st_ref[t]
                st_ref[t] = st
                return st * decay_ref[t] + update

            lax.fori_loop(0, N_CHUNKS, scan_step, jnp.zeros((HEAD_DIM, HEAD_DIM), F32), unroll=2)

            def inter_step(i, carry):
                r0s = [_group_rows(i, j, per_lat, True) for j in range(per_lat)]
                orows = [pl.ds(pl.multiple_of(r0 - CTX_LEN, GROUP), GROUP) for r0 in r0s]
                states = [jnp.concatenate([st_ref[_scan_step_of(r0 // CHUNK + c, rev, True)] for c in range(2)], axis=0)
                          for r0 in r0s]
                prods = [lax.dot_general(qt_ref[orow, :], s.astype(BF16), (((1,), (1,)), ((), ())), preferred_element_type=F32)
                         for orow, s in zip(orows, states)]
                for orow, r in zip(orows, prods):
                    o_ref[orow, :] += _own_block(r)
                return carry

            lax.fori_loop(0, SEQ // (per_lat * GROUP), inter_step, 0)

        @pl.when(pl.program_id(0) == HEADS - 1)
        def _():
            finish_gather()

    outs = pl.pallas_call(
        body, name="gla_forward", grid=(HEADS,),
        out_shape=[pltpu.HBM((SEQ, D_MODEL), F32)] + [jax.ShapeDtypeStruct(W_SPECS[k][0], BF16) for k in GATHER_LATE],
        in_specs=[G_SPEC(0), G_SPEC(1), G_SPEC(2), G_SPEC(3), pl.BlockSpec((2, HEAD_DIM), lambda h: (0, h))] + [ANY] * n_late,
        out_specs=[pl.BlockSpec((SEQ, HEAD_DIM), lambda h: (0, h))] + [ANY] * n_late,
        scratch_shapes=[pltpu.VMEM((N_CHUNKS, HEAD_DIM, HEAD_DIM), F32), pltpu.VMEM((N_CHUNKS, 1, HEAD_DIM), F32),
                        pltpu.VMEM((SEQ, HEAD_DIM), BF16)] + [pltpu.VMEM(W_SPECS[k][0], BF16) for k in GATHER_LATE]
        + [pltpu.SemaphoreType.DMA((2 * n_late,)), pltpu.SemaphoreType.DMA((2 * n_late,)), pltpu.SemaphoreType.DMA((n_late,))],
        compiler_params=_params(48),
    )(*_pin(g5, g5, g5, g5, lb_logits), *late_shards)
    return outs[0], dict(zip(GATHER_LATE, outs[1:]))


def _gla_backward(g5, lb_logits, d_o, d_z, late_parts):
    q_scale = HEAD_DIM ** -0.5
    per_lat, per_ctx = GROUPS_PER_STEP, min(GROUPS_PER_STEP, CTX_LEN // GROUP)
    n_late = len(GATHER_LATE)

    def body(ff_ref, fb_ref, v_ref, q_ref, lg_ref, do_ref, dz_ref, *rest):
        part_refs, (dg_ref, dlg_ref), slot_refs = rest[:n_late], rest[n_late:n_late + 2], rest[n_late + 2:2 * n_late + 2]
        st_ref, dst_ref, decay_ref = rest[2 * n_late + 2:2 * n_late + 5]
        start_scatter, wait_scatter = _scatter_direct(GATHER_LATE, part_refs, slot_refs, *rest[2 * n_late + 5:])

        @pl.when(pl.program_id(0) == 0)
        def _():
            start_scatter()

        dg_ref[3, 0:CTX_LEN, :] = jnp.zeros((CTX_LEN, HEAD_DIM), F32)
        dg_ref[4, 0:CTX_LEN, :] = jnp.zeros((CTX_LEN, HEAD_DIM), F32)
        dg_ref[4, CTX_LEN:ROWS_ALL, :] = dz_ref[...]
        second = lax.broadcasted_iota(jnp.int32, (GROUP, HEAD_DIM), 0) >= CHUNK
        for rev in (False, True):
            d = 1 if rev else 0
            f_ref = fb_ref if rev else ff_ref
            lb = _sig(lg_ref[d:d + 1, :])
            keep, tri, tri_t, strict = _group_masks(rev)
            last = 0 if rev else CHUNK - 1
            mid = CHUNK // 2 if rev else CHUNK // 2 - 1

            def local_step(i, carry, latent, per):
                r0s = [_group_rows(i, j, per, latent) for j in range(per)]
                rows = [pl.ds(r0, GROUP) for r0 in r0s]
                gates = [_gates(f_ref[r, :], lb) for r in rows]
                bs = [_group_sum(tri, g[3]) for g in gates]
                bls = [_chunk_row(b, last) for b in bs]
                ups = [_mm_tn(v_ref[r, :], _by_chunk(g[2] * jnp.exp(bl - b), second)) for r, g, b, bl in zip(rows, gates, bs, bls)]
                if latent:
                    orows = [pl.ds(pl.multiple_of(r0 - CTX_LEN, GROUP), GROUP) for r0 in r0s]
                    d_ups = [_mm_tn(do_ref[orow, :], _by_chunk(_silu(q_ref[r, :]) * q_scale * jnp.exp(b), second))
                             for orow, r, b in zip(orows, rows, bs)]
                for j in range(per):
                    for c in range(2):
                        step = _scan_step_of(r0s[j] // CHUNK + c, rev, latent)
                        decay_ref[step] = jnp.exp(bls[j][c * CHUNK:c * CHUNK + 1, :])
                        st_ref[step] = ups[j][:, c * HEAD_DIM:(c + 1) * HEAD_DIM]
                        if latent:
                            dst_ref[step] = d_ups[j][:, c * HEAD_DIM:(c + 1) * HEAD_DIM]
                        else:
                            dst_ref[step] = jnp.zeros((HEAD_DIM, HEAD_DIM), F32)
                return carry

            lax.fori_loop(0, CTX_LEN // (per_ctx * GROUP), functools.partial(local_step, latent=False, per=per_ctx), 0)
            lax.fori_loop(0, SEQ // (per_lat * GROUP), functools.partial(local_step, latent=True, per=per_lat), 0)

            def scan_step(i, carry):
                st, d_st = carry
                j = N_CHUNKS - 1 - i
                update, d_update = st_ref[i], dst_ref[j]
                st_ref[i] = st
                dst_ref[j] = d_st
                return st * decay_ref[i] + update, d_st * decay_ref[j] + d_update

            zero_state = jnp.zeros((HEAD_DIM, HEAD_DIM), F32)
            lax.fori_loop(0, N_CHUNKS, scan_step, (zero_state, zero_state))

            def grad_step(i, dlb, latent, per):
                r0s = [_group_rows(i, j, per, latent) for j in range(per)]
                rows = [pl.ds(r0, GROUP) for r0 in r0s]
                gates = [_gates(f_ref[r, :], lb) for r in rows]
                vs = [v_ref[r, :] for r in rows]
                bs = [_group_sum(tri, g[3]) for g in gates]
                bls = [_chunk_row(b, last) for b in bs]
                e_ends = [jnp.exp(bl - b) for b, bl in zip(bs, bls)]
                k_ends = [g[2] * e for g, e in zip(gates, e_ends)]
                sts = [[st_ref[_scan_step_of(r0 // CHUNK + c, rev, latent)] for c in range(2)] for r0 in r0s]
                d_sts = [[dst_ref[_scan_step_of(r0 // CHUNK + c, rev, latent)] for c in range(2)] for r0 in r0s]
                d_kends = [_own_block(_mm(v, jnp.concatenate(ds, axis=1))) for v, ds in zip(vs, d_sts)]
                d_vs = [_own_block(_mm_nt(ke, jnp.concatenate(ds, axis=0))) for ke, ds in zip(k_ends, d_sts)]
                at_last = [jnp.concatenate([jnp.broadcast_to(jnp.sum(ds[c] * s[c], axis=0, keepdims=True), (CHUNK, HEAD_DIM))
                                            for c in range(2)], axis=0) * jnp.exp(bl) for ds, s, bl in zip(d_sts, sts, bls)]
                t_kends = [dk * ke for dk, ke in zip(d_kends, k_ends)]
                d_ks = [dk * e for dk, e in zip(d_kends, e_ends)]
                d_lfs = [_group_sum(strict, t) + al for t, al in zip(t_kends, at_last)]
                if latent:
                    orows = [pl.ds(pl.multiple_of(r0 - CTX_LEN, GROUP), GROUP) for r0 in r0s]
                    qpres = [q_ref[r, :] for r in rows]
                    q_sigs = [_sig(qp) for qp in qpres]
                    qs = [qp * sg * q_scale for qp, sg in zip(qpres, q_sigs)]
                    bms = [_chunk_row(b, mid) for b in bs]
                    e_bs = [jnp.exp(b) for b in bs]
                    e_qms = [jnp.exp(b - bm) for b, bm in zip(bs, bms)]
                    e_kms = [jnp.exp(bm - b) for b, bm in zip(bs, bms)]
                    q_ts = [q * e for q, e in zip(qs, e_bs)]
                    q_ms = [q * e for q, e in zip(qs, e_qms)]
                    k_ms = [g[2] * e for g, e in zip(gates, e_kms)]
                    d_outs = [do_ref[orow, :] for orow in orows]
                    a_s = [jnp.where(keep, _mm_nt(qm, km), 0.0) for qm, km in zip(q_ms, k_ms)]
                    d_as = [jnp.where(keep, _mm_nt(do, v), 0.0) for do, v in zip(d_outs, vs)]
                    d_qts = [_own_block(_mm(do, jnp.concatenate(s, axis=1))) for do, s in zip(d_outs, sts)]
                    d_qms = [_mm(da, km) for da, km in zip(d_as, k_ms)]
                    d_kms = [_mm_tn(da, qm) for da, qm in zip(d_as, q_ms)]
                    d_vs = [dv + _mm_tn(a, do) for dv, a, do in zip(d_vs, a_s, d_outs)]
                    d_ks = [dk + dkm * e for dk, dkm, e in zip(d_ks, d_kms, e_kms)]
                    d_lfs = [dl + _group_sum(tri_t, dqt * qt + dqm * qm - dkm * km)
                             for dl, dqt, qt, dqm, qm, dkm, km in zip(d_lfs, d_qts, q_ts, d_qms, q_ms, d_kms, k_ms)]
                    d_qs = [dqt * eb + dqm * eq for dqt, eb, dqm, eq in zip(d_qts, e_bs, d_qms, e_qms)]
                for j in range(per):
                    sg, f = gates[j][0], gates[j][1]
                    d_f = d_lfs[j] / f - d_ks[j]
                    dg_ref[d, rows[j], :] = d_f * (1.0 - lb) * sg * (1.0 - sg)
                    dlb = dlb + jnp.sum(d_f * (1.0 - sg), axis=0, keepdims=True)
                    if rev:
                        dg_ref[2, rows[j], :] += d_vs[j]
                    else:
                        dg_ref[2, rows[j], :] = d_vs[j]
                    if latent:
                        d_qpre = d_qs[j] * q_scale * (q_sigs[j] * (1.0 + qpres[j] * (1.0 - q_sigs[j])))
                        if rev:
                            dg_ref[3, rows[j], :] += d_qpre
                        else:
                            dg_ref[3, rows[j], :] = d_qpre
                return dlb

            dlb = lax.fori_loop(0, SEQ // (GROUPS_PER_GRAD_STEP * GROUP),
                                functools.partial(grad_step, latent=True, per=GROUPS_PER_GRAD_STEP), jnp.zeros((1, HEAD_DIM), F32))
            dlb = lax.fori_loop(0, CTX_LEN // (per_ctx * GROUP), functools.partial(grad_step, latent=False, per=per_ctx), dlb)
            dlg_ref[d:d + 1, :] = dlb * lb * (1.0 - lb)

        @pl.when(pl.program_id(0) == HEADS - 1)
        def _():
            wait_scatter()

    col = pl.BlockSpec((SEQ, HEAD_DIM), lambda h: (0, h))
    outs = pl.pallas_call(
        body, name="gla_backward", grid=(HEADS,),
        out_shape=[pltpu.HBM((HGRN_SECTIONS, ROWS_ALL, D_MODEL), F32), jax.ShapeDtypeStruct((2, D_MODEL), F32)]
        + _slot_shapes(GATHER_LATE),
        in_specs=[G_SPEC(0), G_SPEC(1), G_SPEC(2), G_SPEC(3), pl.BlockSpec((2, HEAD_DIM), lambda h: (0, h)), col, col]
        + [ANY] * n_late,
        out_specs=[pl.BlockSpec((HGRN_SECTIONS, ROWS_ALL, HEAD_DIM), lambda h: (0, 0, h)),
                   pl.BlockSpec((2, HEAD_DIM), lambda h: (0, h))] + [ANY] * n_late,
        scratch_shapes=[pltpu.VMEM((N_CHUNKS, HEAD_DIM, HEAD_DIM), F32), pltpu.VMEM((N_CHUNKS, HEAD_DIM, HEAD_DIM), F32),
                        pltpu.VMEM((N_CHUNKS, 1, HEAD_DIM), F32)] + _comm_sems(n_late),
        compiler_params=_params(48),
    )(*_pin(g5, g5, g5, g5, lb_logits, d_o, d_z), *late_parts)
    return outs[0], outs[1], dict(zip(GATHER_LATE, outs[2:]))


def _head_norm(o, gw, scr):
    rs = []
    for h in range(HEADS):
        cols = slice(h * HEAD_DIM, (h + 1) * HEAD_DIM)
        oh = o[:, cols]
        r = lax.rsqrt(jnp.mean(oh * oh, axis=-1, keepdims=True) + EPS)
        scr[:, cols] = oh * r
        rs.append(r)
    return rs


def _hgrn_out_forward(o_raw, g5, xin, gnorm_w, gate, w_out):
    tm = ROW_TILE

    def body(o_ref, z_ref, x_ref, gw_ref, gate_ref, w_ref, x1_ref, res_ref, scr):
        _head_norm(o_ref[...], None, scr)
        a = scr[...] * gw_ref[...] * _silu(z_ref[...])
        res = _mm(a, w_ref[...])
        res_ref[...] = res
        x1_ref[...] = x_ref[...] + gate_ref[...] * res

    tile = pl.BlockSpec((tm, D_MODEL), lambda i: (i, 0))
    vec = pl.BlockSpec((1, D_MODEL), lambda i: (0, 0))
    return pl.pallas_call(
        body, name="hgrn_out_forward", grid=(SEQ // tm,),
        out_shape=[pltpu.HBM((SEQ, D_MODEL), F32)] * 2,
        in_specs=[tile, pl.BlockSpec((None, tm, D_MODEL), lambda i: (4, i + CTX_LEN // tm, 0)), tile, vec, vec,
                  pl.BlockSpec((D_MODEL, D_MODEL), lambda i: (0, 0))],
        out_specs=[tile, tile],
        scratch_shapes=[pltpu.VMEM((tm, D_MODEL), F32)],
        compiler_params=_params(32),
    )(*_pin(o_raw, g5, xin, gnorm_w, gate, w_out))


def _hgrn_out_backward(d_x1, o_raw, g5, res, gnorm_w, gate, w_out):
    tm = ROW_TILE

    def body(dx_ref, o_ref, z_ref, res_ref, gw_ref, gate_ref, w_ref, do_ref, dz_ref, dw_out, dgate_ref, dgw_ref, scr, scr2,
             dw_ref):
        @pl.when(pl.program_id(0) == 0)
        def _():
            dw_ref[...] = jnp.zeros_like(dw_ref)
            dgate_ref[...] = jnp.zeros_like(dgate_ref)
            dgw_ref[...] = jnp.zeros_like(dgw_ref)

        dx = dx_ref[...]
        dgate_ref[...] += jnp.sum(dx * res_ref[...], axis=0, keepdims=True)
        d_res = (dx * gate_ref[...]).astype(BF16)
        d_a = _mm_nt(d_res, w_ref[...])
        rs = _head_norm(o_ref[...], None, scr)
        z = z_ref[...]
        sz = _silu(z)
        o_hat = scr[...]
        o_n = o_hat * gw_ref[...]
        dw_ref[...] += _mm_tn(o_n * sz, d_res)
        d_on = d_a * sz
        dz_ref[...] = d_a * o_n * _dsilu(z)
        dgw_ref[...] += jnp.sum(d_on * o_hat, axis=0, keepdims=True)
        scr2[...] = d_on * gw_ref[...]
        for h in range(HEADS):
            cols = slice(h * HEAD_DIM, (h + 1) * HEAD_DIM)
            dh, oh = scr2[:, cols], scr[:, cols]
            do_ref[:, cols] = rs[h] * (dh - oh * jnp.mean(dh * oh, axis=-1, keepdims=True))

        @pl.when(pl.program_id(0) == SEQ // tm - 1)
        def _():
            dw_out[...] = dw_ref[...].astype(BF16)

    tile = pl.BlockSpec((tm, D_MODEL), lambda i: (i, 0))
    vec = pl.BlockSpec((1, D_MODEL), lambda i: (0, 0))
    mat = pl.BlockSpec((D_MODEL, D_MODEL), lambda i: (0, 0))
    return pl.pallas_call(
        body, name="hgrn_out_backward", grid=(SEQ // tm,),
        out_shape=[pltpu.HBM((SEQ, D_MODEL), F32)] * 2 + [pltpu.HBM((D_MODEL, D_MODEL), BF16)]
        + [jax.ShapeDtypeStruct((1, D_MODEL), F32)] * 2,
        in_specs=[tile, tile, pl.BlockSpec((None, tm, D_MODEL), lambda i: (4, i + CTX_LEN // tm, 0)), tile, vec, vec, mat],
        out_specs=[tile, tile, mat, vec, vec],
        scratch_shapes=[pltpu.VMEM((tm, D_MODEL), F32)] * 2 + [pltpu.VMEM((D_MODEL, D_MODEL), F32)],
        compiler_params=_params(40),
    )(*_pin(d_x1, o_raw, g5, res, gnorm_w, gate, w_out))


def _pool_constants():
    win = np.zeros((POOL_GROUPS, ROW_TILE, ROW_TILE), np.float32)
    inv = np.zeros((POOL_GROUPS, ROW_TILE, 1), np.float32)
    for g, w in enumerate(POOL_WINDOWS):
        for t in range(ROW_TILE):
            base, p = (t // GRID_W) * GRID_W, t % GRID_W
            lo = min(max(p - w // 2, 0), GRID_W)
            hi = min(max(p - w // 2 + w, 0), GRID_W)
            win[g, t, base + lo:base + hi] = 1.0
            inv[g, t, 0] = 1.0 / np.float32(hi - lo)
    return jnp.asarray(win, BF16), jnp.asarray(win.transpose(0, 2, 1), BF16), jnp.asarray(inv, F32)


def _pool_mix(u_ref, wg_ref, ps_ref, win_ref, inv_ref, pooled_scr, yg_scr):
    for g in range(POOL_GROUPS):
        cols = slice(g * POOL_GROUP_DIM, (g + 1) * POOL_GROUP_DIM)
        ug = u_ref[:, cols]
        pooled = _mm_exact_lhs(win_ref[g], ug) * inv_ref[g] - ug
        if pooled_scr is not None:
            pooled_scr[:, cols] = pooled
        yg_scr[:, cols] = _mm(pooled, wg_ref[g])


def _pool_forward_loss(uz, x1, target, gate, w_grp, pool_scale, w_out, final_w):
    tm = ROW_TILE
    win, _, inv = _pool_constants()

    def body(u_ref, z_ref, x_ref, t_ref, gate_ref, wg_ref, ps_ref, w_ref, fw_ref, win_ref, inv_ref,
             dx_ref, loss_ref, dfw_ref, dgate_ref, yg_scr):
        @pl.when(pl.program_id(0) == 0)
        def _():
            loss_ref[...] = jnp.zeros_like(loss_ref)
            dfw_ref[...] = jnp.zeros_like(dfw_ref)
            dgate_ref[...] = jnp.zeros_like(dgate_ref)

        _pool_mix(u_ref, wg_ref, ps_ref, win_ref, inv_ref, None, yg_scr)
        a = yg_scr[...] * ps_ref[...] * _silu(z_ref[...])
        res = _mm(a, w_ref[...])
        x2 = x_ref[...] + gate_ref[...] * res
        r = lax.rsqrt(jnp.mean(x2 * x2, axis=-1, keepdims=True) + EPS)
        xh = x2 * r
        fw = fw_ref[...]
        err = xh * fw - t_ref[...]
        loss_ref[...] += 0.5 * jnp.sum(jnp.mean(err * err, axis=-1, keepdims=True))
        d_y = err * (1.0 / D_MODEL)
        dfw_ref[...] += jnp.sum(d_y * xh, axis=0, keepdims=True)
        d_xh = d_y * fw
        d_x2 = r * (d_xh - xh * jnp.mean(d_xh * xh, axis=-1, keepdims=True))
        dx_ref[...] = d_x2
        dgate_ref[...] += jnp.sum(d_x2 * res, axis=0, keepdims=True)

    tile = pl.BlockSpec((tm, D_MODEL), lambda i: (i, 0))
    vec = pl.BlockSpec((1, D_MODEL), lambda i: (0, 0))
    grp = pl.BlockSpec((POOL_GROUPS, POOL_GROUP_DIM, POOL_GROUP_DIM), lambda i: (0, 0, 0))
    return pl.pallas_call(
        body, name="pool_forward_loss", grid=(SEQ // tm,),
        out_shape=[pltpu.HBM((SEQ, D_MODEL), F32), jax.ShapeDtypeStruct((8, 128), F32),
                   jax.ShapeDtypeStruct((1, D_MODEL), F32), jax.ShapeDtypeStruct((1, D_MODEL), F32)],
        in_specs=[pl.BlockSpec((None, tm, D_MODEL), lambda i: (0, i, 0)), pl.BlockSpec((None, tm, D_MODEL), lambda i: (1, i, 0)),
                  tile, tile, vec, grp, vec, pl.BlockSpec((D_MODEL, D_MODEL), lambda i: (0, 0)), vec, grp,
                  pl.BlockSpec((POOL_GROUPS, ROW_TILE, 1), lambda i: (0, 0, 0))],
        out_specs=[tile, pl.BlockSpec((8, 128), lambda i: (0, 0)), vec, vec],
        scratch_shapes=[pltpu.VMEM((tm, D_MODEL), F32)],
        compiler_params=_params(32),
    )(*_pin(uz, uz, x1, target, gate, w_grp, pool_scale, w_out, final_w, win, inv))


def _pool_backward(d_x2, uz, gate, w_grp, pool_scale, w_out):
    tm = ROW_TILE
    win, win_t, inv = _pool_constants()

    def body(dx_ref, u_ref, z_ref, gate_ref, wg_ref, ps_ref, w_ref, win_ref, wint_ref, inv_ref,
             duz_ref, dw_out, dwg_out, dps_ref, pooled_scr, yg_scr, dyg_scr, dw_ref, dwg_ref):
        @pl.when(pl.program_id(0) == 0)
        def _():
            dw_ref[...] = jnp.zeros_like(dw_ref)
            dwg_ref[...] = jnp.zeros_like(dwg_ref)
            dps_ref[...] = jnp.zeros_like(dps_ref)

        _pool_mix(u_ref, wg_ref, ps_ref, win_ref, inv_ref, pooled_scr, yg_scr)
        z = z_ref[...]
        sz = _silu(z)
        yg = yg_scr[...]
        y = yg * ps_ref[...]
        d_res = (dx_ref[...] * gate_ref[...]).astype(BF16)
        d_a = _mm_nt(d_res, w_ref[...])
        dw_ref[...] += _mm_tn(y * sz, d_res)
        d_y = d_a * sz
        duz_ref[1] = d_a * y * _dsilu(z)
        dps_ref[...] += jnp.sum(d_y * yg, axis=0, keepdims=True)
        dyg_scr[...] = d_y * ps_ref[...]
        for g in range(POOL_GROUPS):
            cols = slice(g * POOL_GROUP_DIM, (g + 1) * POOL_GROUP_DIM)
            d_yg = dyg_scr[:, cols].astype(BF16)
            d_pool = _mm_nt(d_yg, wg_ref[g])
            dwg_ref[g] += _mm_tn(pooled_scr[:, cols], d_yg)
            duz_ref[0, :, cols] = _mm_exact_lhs(wint_ref[g], d_pool * inv_ref[g]) - d_pool

        @pl.when(pl.program_id(0) == SEQ // tm - 1)
        def _():
            dw_out[...] = dw_ref[...].astype(BF16)
            dwg_out[...] = dwg_ref[...].astype(BF16)

    tile = pl.BlockSpec((tm, D_MODEL), lambda i: (i, 0))
    vec = pl.BlockSpec((1, D_MODEL), lambda i: (0, 0))
    mat = pl.BlockSpec((D_MODEL, D_MODEL), lambda i: (0, 0))
    grp = pl.BlockSpec((POOL_GROUPS, POOL_GROUP_DIM, POOL_GROUP_DIM), lambda i: (0, 0, 0))
    return pl.pallas_call(
        body, name="pool_backward", grid=(SEQ // tm,),
        out_shape=[pltpu.HBM((POOL_SECTIONS, SEQ, D_MODEL), F32), pltpu.HBM((D_MODEL, D_MODEL), BF16),
                   pltpu.HBM((POOL_GROUPS, POOL_GROUP_DIM, POOL_GROUP_DIM), BF16), jax.ShapeDtypeStruct((1, D_MODEL), F32)],
        in_specs=[tile, pl.BlockSpec((None, tm, D_MODEL), lambda i: (0, i, 0)), pl.BlockSpec((None, tm, D_MODEL), lambda i: (1, i, 0)),
                  vec, grp, vec, mat, grp, grp, pl.BlockSpec((POOL_GROUPS, ROW_TILE, 1), lambda i: (0, 0, 0))],
        out_specs=[pl.BlockSpec((POOL_SECTIONS, tm, D_MODEL), lambda i: (0, i, 0)), mat, grp, vec],
        scratch_shapes=[pltpu.VMEM((tm, D_MODEL), F32)] * 3 + [pltpu.VMEM((D_MODEL, D_MODEL), F32),
                                                               pltpu.VMEM((POOL_GROUPS, POOL_GROUP_DIM, POOL_GROUP_DIM), F32)],
        compiler_params=_params(40),
    )(*_pin(d_x2, uz, uz, gate, w_grp, pool_scale, w_out, win, win_t, inv))


def _ln_mod_backward(d_g, w, ctx_tile, xin, nw, scale, d_up, name):
    n_sec, rows, _ = d_g.shape
    n_mod = scale.shape[0]
    tm = ROW_TILE
    n_tiles = rows // tm
    skip = n_mod - 1

    def body(dg_ref, w_ref, *refs):
        c_ref = refs[0] if skip else None
        x_ref, nw_ref, sc_ref, up_ref, dx_ref, dnw_ref, dmod_ref = refs[skip:]
        i = pl.program_id(0)

        @pl.when(i == 0)
        def _():
            dnw_ref[...] = jnp.zeros_like(dnw_ref)

        @pl.when((i == 0) | (i == skip))
        def _():
            dmod_ref[...] = jnp.zeros_like(dmod_ref)

        d_h = _mm_nt(dg_ref[0], w_ref[:, 0:D_MODEL])
        for k in range(1, n_sec):
            d_h = d_h + _mm_nt(dg_ref[k], w_ref[:, k * D_MODEL:(k + 1) * D_MODEL])
        xv = jnp.where(i == 0, c_ref[...], x_ref[...]) if skip else x_ref[...]
        r = lax.rsqrt(jnp.mean(xv * xv, axis=-1, keepdims=True) + EPS)
        xh = xv * r
        nw_row = nw_ref[...]
        dmod_ref[0:1, :] += jnp.sum(d_h, axis=0, keepdims=True)
        dmod_ref[1:2, :] += jnp.sum(d_h * (xh * nw_row), axis=0, keepdims=True)
        d_xn = d_h * (1.0 + sc_ref[...])
        dnw_ref[...] += jnp.sum(d_xn * xh, axis=0, keepdims=True)
        d_xh = d_xn * nw_row

        @pl.when(i >= skip)
        def _():
            dx_ref[...] = up_ref[...] + r * (d_xh - xh * jnp.mean(d_xh * xh, axis=-1, keepdims=True))

    lat = lambda i: (jnp.maximum(i - skip, 0), 0)
    mod_idx = lambda i: (jnp.minimum(i, n_mod - 1), 0, 0)
    return pl.pallas_call(
        body, name=name, grid=(n_tiles,),
        out_shape=[pltpu.HBM((rows - skip * tm, D_MODEL), F32), jax.ShapeDtypeStruct((1, D_MODEL), F32),
                   jax.ShapeDtypeStruct((n_mod, 8, D_MODEL), F32)],
        in_specs=[pl.BlockSpec((n_sec, tm, D_MODEL), lambda i: (0, i, 0)),
                  pl.BlockSpec((D_MODEL, n_sec * D_MODEL), lambda i: (0, 0))]
        + [pl.BlockSpec((tm, D_MODEL), lambda i: (0, 0))] * skip
        + [pl.BlockSpec((tm, D_MODEL), lat),
           pl.BlockSpec((1, D_MODEL), lambda i: (0, 0)),
           pl.BlockSpec((None, 1, D_MODEL), mod_idx),
           pl.BlockSpec((tm, D_MODEL), lat)],
        out_specs=[pl.BlockSpec((tm, D_MODEL), lat), pl.BlockSpec((1, D_MODEL), lambda i: (0, 0)),
                   pl.BlockSpec((None, 8, D_MODEL), mod_idx)],
        compiler_params=_params(48),
    )(*_pin(d_g, w, *([ctx_tile] * skip), xin, nw, scale.reshape(n_mod, 1, D_MODEL), d_up))


def _weight_grad(h, d_g, name):
    n_sec, rows, _ = d_g.shape
    tm = 768 if rows % 768 == 0 else 512
    n_tiles = rows // tm

    def body(h_ref, dg_ref, dw_ref, acc):
        i = pl.program_id(1)
        prod = _mm_tn(h_ref[...], dg_ref[...])

        @pl.when(i == 0)
        def _():
            acc[...] = prod

        @pl.when((i > 0) & (i < n_tiles - 1))
        def _():
            acc[...] += prod

        @pl.when(i == n_tiles - 1)
        def _():
            dw_ref[...] = (acc[...] + prod).astype(BF16)

    return pl.pallas_call(
        body, name=name, grid=(n_sec, n_tiles),
        out_shape=pltpu.HBM((D_MODEL, n_sec * D_MODEL), BF16),
        in_specs=[pl.BlockSpec((tm, D_MODEL), lambda j, i: (i, 0)), pl.BlockSpec((None, tm, D_MODEL), lambda j, i: (j, i, 0))],
        out_specs=pl.BlockSpec((D_MODEL, D_MODEL), lambda j, i: (0, j)),
        scratch_shapes=[pltpu.VMEM((D_MODEL, D_MODEL), F32)],
        compiler_params=_params(32),
    )(*_pin(h, d_g))


def _sum_slots(slots, name):
    _, rows, cols = slots.shape
    tr = 128

    def body(s_ref, o_ref):
        acc = s_ref[0].astype(F32)
        for d in range(1, N_DEV):
            acc = acc + s_ref[d].astype(F32)
        o_ref[...] = acc

    return pl.pallas_call(
        body, name=name, grid=(rows // tr,),
        out_shape=jax.ShapeDtypeStruct((rows, cols), F32),
        in_specs=[pl.BlockSpec((N_DEV, tr, cols), lambda i: (0, i, 0))],
        out_specs=pl.BlockSpec((tr, cols), lambda i: (i, 0)),
    )(*_pin(slots))


def _adamw_math(w, g, m, v):
    m = ADAM_B1 * m + (1.0 - ADAM_B1) * g
    v = ADAM_B2 * v + (1.0 - ADAM_B2) * (g * g)
    m_hat = m / (1.0 - ADAM_B1 ** ADAM_STEP)
    v_hat = v / (1.0 - ADAM_B2 ** ADAM_STEP)
    return -ADAM_LR * (m_hat / (jnp.sqrt(v_hat) + ADAM_EPS) + ADAM_WD * w), m, v


def _adamw(w, g, m, v, name):
    rows, cols = w.shape
    tr = rows if rows <= 128 else 128

    def body(w_ref, g_ref, m_ref, v_ref, d_ref, mo_ref, vo_ref):
        d_ref[...], mo_ref[...], vo_ref[...] = _adamw_math(w_ref[...], g_ref[...], m_ref[...], v_ref[...])

    tile = pl.BlockSpec((tr, cols), lambda i: (i, 0))
    return pl.pallas_call(
        body, name=name, grid=(rows // tr,),
        out_shape=[pltpu.HBM((rows, cols), F32)] * 3,
        in_specs=[tile] * 4, out_specs=[tile] * 3,
    )(*_pin(w, g, m, v))


def _sum_devices(gathered):
    def body(p_ref, o_ref):
        acc = p_ref[0:SMALL_ROWS, :]
        for d in range(1, N_DEV):
            acc = acc + p_ref[d * SMALL_ROWS:(d + 1) * SMALL_ROWS, :]
        o_ref[...] = acc

    return pl.pallas_call(body, name="sum_devices", out_shape=jax.ShapeDtypeStruct((SMALL_ROWS, D_MODEL), F32),
                          in_specs=[VMEM], out_specs=VMEM)(gathered)


def _ada_update(cond_t, d_mod, ada_w, m, v):
    n_layers, _, n_cols = ada_w.shape
    tr = ROW_TILE

    def body(c_ref, dm_ref, w_ref, m_ref, v_ref, g_ref, d_ref, mo_ref, vo_ref):
        g = _mm_f32(_silu(c_ref[...]), dm_ref[...])
        g_ref[...] = g
        d_ref[...], mo_ref[...], vo_ref[...] = _adamw_math(w_ref[...], g, m_ref[...], v_ref[...])

    tile = pl.BlockSpec((None, tr, n_cols), lambda l, i: (l, i, 0))
    return pl.pallas_call(
        body, name="ada_update", grid=(n_layers, D_MODEL // tr),
        out_shape=[pltpu.HBM(ada_w.shape, F32)] * 4,
        in_specs=[pl.BlockSpec((tr, 16), lambda l, i: (i, 0)), pl.BlockSpec((None, 16, n_cols), lambda l, i: (l, 0, 0)),
                  tile, tile, tile],
        out_specs=[tile] * 4,
    )(*_pin(cond_t, d_mod, ada_w, m, v))


def _cond_ctx_partial(d_modc, ada_w0):
    n_cols = ada_w0.shape[1]
    tr = ROW_TILE

    def body(dm_ref, w_ref, o_ref):
        o_ref[...] = jnp.sum(w_ref[...] * dm_ref[...], axis=-1, keepdims=True)

    return pl.pallas_call(
        body, name="cond_ctx_partial", grid=(D_MODEL // tr,),
        out_shape=jax.ShapeDtypeStruct((D_MODEL, 1), F32),
        in_specs=[pl.BlockSpec((1, n_cols), lambda i: (0, 0)), pl.BlockSpec((tr, n_cols), lambda i: (i, 0))],
        out_specs=pl.BlockSpec((tr, 1), lambda i: (i, 0)),
    )(d_modc, ada_w0)


def _cond_ctx_update(gathered, c_ctx, m, v):
    def body(p_ref, w_ref, m_ref, v_ref, g_ref, d_ref, mo_ref, vo_ref):
        acc = p_ref[0:1, :]
        for s in range(1, N_CHIPS):
            acc = acc + p_ref[16 * s:16 * s + 1, :]
        w = w_ref[...]
        g = acc * _dsilu(w)
        g_ref[...] = g
        d_ref[...], mo_ref[...], vo_ref[...] = _adamw_math(w, g, m_ref[...], v_ref[...])

    return pl.pallas_call(body, name="cond_ctx_update", out_shape=[jax.ShapeDtypeStruct((1, D_MODEL), F32)] * 4,
                          in_specs=[VMEM] * 4, out_specs=[VMEM] * 4)(gathered, c_ctx, m, v)


def _local_step(x2, ctx2, target, mod_mine, mod_ctx, lb_logits, scale_full, w_in_full, late_shards, norm_w, gnorm, final_w):
    row = lambda a: a.reshape(1, -1)
    shift0, scale0, gate0 = (row(a) for a in jnp.split(mod_mine[0], 3))
    shift1, scale1, gate1 = (row(a) for a in jnp.split(mod_mine[1], 3))
    shift_c, scale_c, _ = (row(a) for a in jnp.split(mod_ctx, 3))
    nw0, nw1 = norm_w[0:1], norm_w[1:2]
    scales0 = jnp.concatenate([scale_c, scale0])

    g5, h0 = _ln_mod_matmul(ctx2, x2, nw0, jnp.concatenate([shift_c, shift0]), scales0, w_in_full, "hgrn_in_forward")
    o_raw, full = _gla_forward(g5, lb_logits, late_shards)
    x1, res0 = _hgrn_out_forward(o_raw, g5, x2, gnorm, gate0, full["hgrn_w_out"])
    uz, h1 = _ln_mod_matmul(None, x1, nw1, shift1, scale1, full["pool_w_in"], "pool_in_forward")
    d_x2, loss_part, d_final, d_gate1 = _pool_forward_loss(uz, x1, target, gate1, full["pool_w_grp"], scale_full,
                                                           full["pool_w_out"], final_w)

    d_uz, dw_pool_out, dw_pool_grp, d_pscale = _pool_backward(d_x2, uz, gate1, full["pool_w_grp"], scale_full, full["pool_w_out"])
    d_x1, d_nw1, d_mod1 = _ln_mod_backward(d_uz, full["pool_w_in"], None, x1, nw1, scale1, d_x2, "pool_in_backward")
    dw_pool_in = _weight_grad(h1, d_uz, "pool_in_weight_grad")
    d_o, d_z, dw_hgrn_out, d_gate0, d_gnorm = _hgrn_out_backward(d_x1, o_raw, g5, res0, gnorm, gate0, full["hgrn_w_out"])
    late_grads = {"hgrn_w_out": dw_hgrn_out, "pool_w_in": dw_pool_in, "pool_w_grp": dw_pool_grp, "pool_w_out": dw_pool_out}
    d_g5, d_lb, late_slots = _gla_backward(g5, lb_logits, d_o, d_z, [late_grads[k].astype(BF16) for k in GATHER_LATE])
    dw_hgrn_in = _weight_grad(h0, d_g5, "hgrn_in_weight_grad")
    key = GATHER_EARLY[0]
    half = _half_shape(key)
    xi, yi, ci = _my_place()
    own = lax.dynamic_slice(dw_hgrn_in, (ci * half[0], (2 * xi + yi) * half[1]), half)
    slots0 = lax.dynamic_update_slice(jnp.zeros((N_DEV,) + half, BF16), own[None], (4 * xi + 2 * yi + ci, 0, 0))
    send_sem, recv_sem, part_thru, slots_thru, token = _scatter_start(dw_hgrn_in, slots0, own, key, 0)
    d_x, d_nw0, d_mod0 = _ln_mod_backward(d_g5, w_in_full, ctx2, x2, nw0 + token[0:1, 0:1], scales0, d_x1, "hgrn_in_backward")
    slots = dict(late_slots)
    pending = (send_sem, recv_sem, part_thru, slots_thru)

    zero = jnp.zeros((1, D_MODEL), F32)
    small = jnp.concatenate([d_mod0[1, 0:2], d_gate0, d_mod1[0, 0:2], d_gate1, d_mod0[0, 0:2], zero, d_nw0, d_nw1, d_gnorm,
                             d_final, d_pscale, d_lb, jnp.broadcast_to(loss_part[0:1, 0:1], (1, D_MODEL)),
                             jnp.zeros((SMALL_ROWS - 17, D_MODEL), F32)], axis=0)
    return {"d_x": d_x, "slots": slots, "pending": pending, "small": small}


def kernel(x, c, ctx, c_ctx, ada_w, ada_b, norm_w, hgrn_w_in, hgrn_lb_logits, hgrn_gnorm_w, hgrn_w_out, pool_w_in, pool_w_grp, pool_scale, pool_w_out, final_norm_w, loss_target, m_c_ctx, m_ada_w, m_ada_b, m_norm_w, m_hgrn_w_in, m_hgrn_lb_logits, m_hgrn_gnorm_w, m_hgrn_w_out, m_pool_w_in, m_pool_w_grp, m_pool_scale, m_pool_w_out, m_final_norm_w, v_c_ctx, v_ada_w, v_ada_b, v_norm_w, v_hgrn_w_in, v_hgrn_lb_logits, v_hgrn_gnorm_w, v_hgrn_w_out, v_pool_w_in, v_pool_w_grp, v_pool_scale, v_pool_w_out, v_final_norm_w):
    xi, yi, ci = _my_place()
    chip = 2 * xi + yi
    dev = 4 * xi + 2 * yi + ci
    ada_cols = ada_w.shape[2]
    lb_cols = hgrn_lb_logits.shape[2]
    ps_cols = pool_scale.shape[1]
    row = lambda a: a.reshape(1, -1)

    def chip_cols(a, n):
        return lax.dynamic_slice_in_dim(a, chip * n, n, axis=a.ndim - 1)

    def from_chips(g, rows_per_dev, take):
        return jnp.concatenate([g[2 * s * rows_per_dev:2 * s * rows_per_dev + take] for s in range(N_CHIPS)], axis=1)

    first = jnp.concatenate([jnp.broadcast_to(c, (8, D_MODEL)), jnp.pad(hgrn_lb_logits[0], ((0, 6), (0, 0))),
                             jnp.pad(pool_scale, ((0, 7), (0, 0)))], axis=1)
    first_all = _all_gather_small(first, "gather_cond")
    cond_all = first_all[::8, :D_MODEL]
    lb_logits = from_chips(first_all[:, D_MODEL:D_MODEL + lb_cols], 8, 2)
    scale_full = from_chips(first_all[:, D_MODEL + lb_cols:], 8, 1)
    cond_rows = jnp.concatenate([cond_all, row(c_ctx), jnp.zeros((7, D_MODEL), F32)], axis=0)

    parts = _mod_parts(cond_rows, ada_w, chip_cols(ada_b, ada_cols))
    parts_all = _all_gather_small(parts.reshape(32, ada_cols), "gather_mod")
    mod_all = from_chips(parts_all, 32, 32).reshape(2, 16, 3 * D_MODEL)
    mod_mine = lax.dynamic_index_in_dim(mod_all, dev, axis=1, keepdims=False)

    shards = {"hgrn_w_in": hgrn_w_in[0], "hgrn_w_out": hgrn_w_out[0], "pool_w_in": pool_w_in[0],
              "pool_w_grp": pool_w_grp[0], "pool_w_out": pool_w_out[0]}
    w_in_full = _all_gather_weights([shards[k].astype(BF16) for k in GATHER_EARLY], GATHER_EARLY)[0]

    loc = _local_step(x[0], ctx[0], loss_target[0], mod_mine, mod_all[0, 8], lb_logits, scale_full, w_in_full,
                      [shards[k].astype(BF16) for k in GATHER_LATE], norm_w, hgrn_gnorm_w, row(final_norm_w))

    small_all = _all_gather_small(loc["small"], "gather_small")
    sums = _sum_devices(small_all)
    loss = sums[16, 0]

    part_c = _cond_ctx_partial(chip_cols(sums[6:9].reshape(1, -1), ada_cols), ada_w[0])
    part_all = _all_gather_small(jnp.pad(part_c.reshape(1, D_MODEL), ((0, 7), (0, 0))), "gather_cond_ctx")
    key = GATHER_EARLY[0]
    send0, recv0, part_thru, slots_thru = loc["pending"]
    send1, recv1, part_thru, slots_thru, _ = _scatter_start(part_thru, slots_thru, part_all, key, 1)

    def reduce_scattered(slots, names, name):
        halves = []
        for k in names:
            hs = _half_shape(k)
            halves.append(_sum_slots(slots[k].reshape(N_DEV, -1, hs[-1]), "sum_" + k).reshape(hs))
        return dict(zip(names, _exchange_halves(halves, names, name)))

    big_grads = reduce_scattered(loc["slots"], GATHER_LATE, "exchange_halves_late")

    out = {}

    def update(name, w, g, m, v):
        shape = w.shape
        w2, g2, m2, v2 = (a.reshape(-1, shape[-1]) for a in (w, g, m, v))
        d, mn, vn = _adamw(w2, g2, m2, v2, "adamw_" + name)
        out[name] = tuple(a.reshape(shape) for a in (g2, d, mn, vn))

    moments = {"hgrn_w_in": (m_hgrn_w_in, v_hgrn_w_in), "hgrn_w_out": (m_hgrn_w_out, v_hgrn_w_out),
               "pool_w_in": (m_pool_w_in, v_pool_w_in), "pool_w_grp": (m_pool_w_grp, v_pool_w_grp),
               "pool_w_out": (m_pool_w_out, v_pool_w_out)}
    weights = {"hgrn_w_in": hgrn_w_in, "hgrn_w_out": hgrn_w_out, "pool_w_in": pool_w_in, "pool_w_grp": pool_w_grp,
               "pool_w_out": pool_w_out}
    for k in GATHER_LATE:
        update(k, weights[k], big_grads[k], *moments[k])

    g_ada_b = jnp.stack([(sums[0:3] + sums[6:9]).reshape(-1), sums[3:6].reshape(-1)])
    update("ada_b", ada_b, g_ada_b, m_ada_b, v_ada_b)
    update("norm_w", norm_w, sums[9:11], m_norm_w, v_norm_w)
    update("hgrn_gnorm_w", hgrn_gnorm_w, sums[11:12], m_hgrn_gnorm_w, v_hgrn_gnorm_w)
    update("final_norm_w", row(final_norm_w), sums[12:13], row(m_final_norm_w), row(v_final_norm_w))
    update("pool_scale", pool_scale, chip_cols(sums[13:14], ps_cols), m_pool_scale, v_pool_scale)
    update("hgrn_lb_logits", hgrn_lb_logits, chip_cols(sums[14:16], lb_cols)[None], m_hgrn_lb_logits, v_hgrn_lb_logits)

    per_dev = small_all.reshape(N_DEV, SMALL_ROWS, D_MODEL)
    pad7 = jnp.zeros((7, 3 * D_MODEL), F32)
    dm0 = jnp.concatenate([per_dev[:, 0:3].reshape(N_DEV, -1), sums[6:9].reshape(1, -1), pad7], axis=0)
    dm1 = jnp.concatenate([per_dev[:, 3:6].reshape(N_DEV, -1), jnp.zeros((8, 3 * D_MODEL), F32)], axis=0)
    d_mod = chip_cols(jnp.stack([dm0, dm1]), ada_cols)
    out["ada_w"] = _ada_update(cond_rows.T, d_mod, ada_w, m_ada_w, v_ada_w)

    g_c, d_c, m_c, v_c = _cond_ctx_update(part_all, row(c_ctx), row(m_c_ctx), row(v_c_ctx))
    out["c_ctx"] = tuple(a.reshape(-1) for a in (g_c, d_c, m_c, v_c))
    out["final_norm_w"] = tuple(a.reshape(-1) for a in out["final_norm_w"])

    done = [g_c, out["ada_w"][1]] + [out[k][1] for k in GATHER_LATE]
    part_thru, slots_thru = _scatter_wait(send0, recv0, part_thru, slots_thru, done, key, 0)
    _, early = _scatter_wait(send1, recv1, part_thru, slots_thru, done, key, 1)
    big_grads = reduce_scattered({key: early}, GATHER_EARLY, "exchange_halves_early")
    for k in GATHER_EARLY:
        update(k, weights[k], big_grads[k], *moments[k])

    names = ["c_ctx", "ada_w", "ada_b", "norm_w", "hgrn_w_in", "hgrn_lb_logits", "hgrn_gnorm_w", "hgrn_w_out", "pool_w_in",
             "pool_w_grp", "pool_scale", "pool_w_out", "final_norm_w"]
    return (loss, loc["d_x"][None], *[out[k][0] for k in names], *[out[k][1] for k in names], *[out[k][2] for k in names],
            *[out[k][3] for k in names])
```

```python
import functools

import numpy as np
import jax
import jax.numpy as jnp
from jax import lax
from jax.experimental import pallas as pl
from jax.experimental.pallas import tpu as pltpu

F32 = jnp.float32
BF16 = jnp.bfloat16

D_MODEL = 1024
SEQ = 2048
CTX_LEN = 256
ROWS_ALL = CTX_LEN + SEQ
HEADS = 8
HEAD_DIM = 128
CHUNK = 64
N_CTX_CHUNKS = CTX_LEN // CHUNK
N_LAT_CHUNKS = SEQ // CHUNK
N_CHUNKS = N_CTX_CHUNKS + N_LAT_CHUNKS
GRID_W = 64
POOL_WINDOWS = (2, 4, 8, 16)
POOL_GROUPS = 4
POOL_GROUP_DIM = 256
HGRN_SECTIONS = 5
POOL_SECTIONS = 2
EPS = 1e-6
N_DEV = 8
N_CHIPS = 4
ROW_TILE = 256
SMALL_ROWS = 24

ADAM_LR = 0.001
ADAM_B1 = 0.9
ADAM_B2 = 0.999
ADAM_EPS = 1e-08
ADAM_WD = 0.01
ADAM_STEP = 10

MESH = pl.DeviceIdType.MESH
MIB = 1 << 20
ANY = pl.BlockSpec(memory_space=pl.ANY)
VMEM = pl.BlockSpec(memory_space=pltpu.VMEM)


def _params(vmem_mib=None):
    if vmem_mib is None:
        return pltpu.CompilerParams()
    return pltpu.CompilerParams(vmem_limit_bytes=vmem_mib * MIB)


def _pin(*operands):
    return [pltpu.with_memory_space_constraint(a, pltpu.HBM) if a.size * a.dtype.itemsize >= MIB else a for a in operands]


def _sig(a):
    return 1.0 / (1.0 + jnp.exp(-a))


def _silu(a):
    return a * _sig(a)


def _dsilu(a):
    s = _sig(a)
    return s * (1.0 + a * (1.0 - s))


def _mm(a, b):
    return jnp.dot(a.astype(BF16), b.astype(BF16), preferred_element_type=F32)


def _mm_nt(a, b):
    return lax.dot_general(a.astype(BF16), b.astype(BF16), (((1,), (1,)), ((), ())), preferred_element_type=F32)


def _mm_tn(a, b):
    return lax.dot_general(a.astype(BF16), b.astype(BF16), (((0,), (0,)), ((), ())), preferred_element_type=F32)


def _split2(a):
    hi = a.astype(BF16)
    lo = (a - hi.astype(F32)).astype(BF16)
    return hi, lo


def _mm_exact_lhs(m_bf, a):
    hi, lo = _split2(a)
    return jnp.dot(m_bf, hi, preferred_element_type=F32) + jnp.dot(m_bf, lo, preferred_element_type=F32)


def _mm_f32(a, b):
    ah, al = _split2(a)
    bh, bl = _split2(b)
    return (jnp.dot(ah, bh, preferred_element_type=F32) + jnp.dot(al, bh, preferred_element_type=F32)
            + jnp.dot(ah, bl, preferred_element_type=F32))


def _my_place():
    return lax.axis_index("x"), lax.axis_index("y"), lax.axis_index("c")


def _all_gather_small(blk, name):
    m_per, n = blk.shape

    def body(x_ref, out_ref, send_sems, recv_sems, local_sem):
        x, y, c = _my_place()
        me, sibling = (x, y, c), (x, y, 1 - c)
        chips = [(1 - x, y), (x, 1 - y), (1 - x, 1 - y)]

        def rows(px, py, pc):
            return out_ref.at[pl.ds((4 * px + 2 * py + pc) * m_per, m_per), :]

        def copy(k, block, to, src=None):
            return pltpu.make_async_remote_copy(
                src_ref=rows(*block) if src is None else src, dst_ref=rows(*block),
                send_sem=send_sems.at[k], recv_sem=recv_sems.at[k], device_id=to, device_id_type=MESH)

        mine = pltpu.make_async_copy(x_ref, rows(*me), local_sem)
        mine.start()
        first = [copy(0, me, sibling, src=x_ref)]
        first += [copy(1 + j, me, (*chip, c), src=x_ref) for j, chip in enumerate(chips)]
        for cp in first:
            cp.start()
        passed = [copy(4 + j, (*chip, c), sibling) for j, chip in enumerate(chips)]
        for j, chip in enumerate(chips):
            copy(1 + j, (*chip, c), me).wait_recv()
            passed[j].start()
        copy(0, sibling, me).wait_recv()
        for j, chip in enumerate(chips):
            copy(4 + j, (*chip, 1 - c), me).wait_recv()
        for cp in first + passed:
            cp.wait_send()
        mine.wait()

    return pl.pallas_call(
        body, name=name,
        out_shape=jax.ShapeDtypeStruct((N_DEV * m_per, n), blk.dtype),
        in_specs=[VMEM], out_specs=VMEM,
        scratch_shapes=[pltpu.SemaphoreType.DMA((7,)), pltpu.SemaphoreType.DMA((7,)), pltpu.SemaphoreType.DMA],
    )(blk)


W_SPECS = {
    "hgrn_w_in": ((D_MODEL, 5 * D_MODEL), (1, 1280, 0, 512)),
    "hgrn_w_out": ((D_MODEL, D_MODEL), (0, 256, 0, 128)),
    "pool_w_in": ((D_MODEL, 2 * D_MODEL), (1, 512, 0, 512)),
    "pool_w_grp": ((POOL_GROUPS, POOL_GROUP_DIM, POOL_GROUP_DIM), (1, 64, 1, 32)),
    "pool_w_out": ((D_MODEL, D_MODEL), (0, 256, 0, 128)),
}
W_NAMES = tuple(W_SPECS)


def _al(v, m):
    return pl.multiple_of(v, m)


def _region(ref, spec, chip, half):
    ca, cn, ha, hn = spec
    idx = [slice(None)] * len(ref.shape)
    if ca == ha:
        if half is None:
            idx[ca] = pl.ds(_al(chip * cn, cn), cn)
        else:
            idx[ca] = pl.ds(_al(chip * cn + half * hn, hn), hn)
    else:
        idx[ca] = pl.ds(_al(chip * cn, cn), cn)
        if half is not None:
            idx[ha] = pl.ds(_al(half * hn, hn), hn)
    return ref.at[tuple(idx)]


def _half_of(ref, spec, half):
    _, _, ha, hn = spec
    idx = [slice(None)] * len(ref.shape)
    idx[ha] = pl.ds(_al(half * hn, hn), hn)
    return ref.at[tuple(idx)]


PIECE_BYTES = 256 * 1024


def _pieces(ref):
    lead = ref.shape[0]
    want = (int(np.prod(ref.shape)) * ref.dtype.itemsize) // PIECE_BYTES
    n = max([1] + [k for k in range(1, want + 1) if lead % k == 0 and (lead // k) % 16 == 0])
    rows = lead // n
    return [ref.at[pl.ds(i * rows, rows)] for i in range(n)]


def _half_shape(name):
    full, (ca, cn, ha, hn) = W_SPECS[name]
    shp = list(full)
    shp[ca] = cn
    shp[ha] = hn
    return tuple(shp)


def _gather_direct(names, sh, full, send_sems, recv_sems, local_sems):
    specs = [W_SPECS[k][1] for k in names]
    x, y, c = _my_place()
    chip_me = 2 * x + y
    chips = [(1 - x, y), (x, 1 - y), (1 - x, 1 - y)]

    def local(a):
        return pltpu.make_async_copy(sh[a], _region(full[a], specs[a], chip_me, None), local_sems.at[a])

    def remote(a, src, dst, to):
        return pltpu.make_async_remote_copy(src_ref=src, dst_ref=dst, send_sem=send_sems.at[a], recv_sem=recv_sems.at[a],
                                            device_id=to, device_id_type=MESH)

    def start():
        for a in range(len(names)):
            local(a).start()
            for px, py in chips:
                remote(a, sh[a], _region(full[a], specs[a], chip_me, None), (px, py, c)).start()

    def wait():
        for a in range(len(names)):
            ca, cn, _, _ = specs[a]
            idx = [slice(None)] * len(full[a].shape)
            idx[ca] = pl.ds(0, 3 * cn)
            three = full[a].at[tuple(idx)]
            remote(a, three, three, (x, y, c)).wait()
            local(a).wait()

    return start, wait


def _gather_two_level(names, sh, land, full, send_sems, recv_sems, local_sems):
    n = len(names)
    specs = [W_SPECS[k][1] for k in names]
    x, y, c = _my_place()
    chip_me = 2 * x + y
    sibling = (x, y, 1 - c)
    chips = [(1 - x, y), (x, 1 - y), (1 - x, 1 - y)]

    def remote(k, src, dst, to):
        return pltpu.make_async_remote_copy(src_ref=src, dst_ref=dst, send_sem=send_sems.at[k], recv_sem=recv_sems.at[k],
                                            device_id=to, device_id_type=MESH)

    def three_halves(a):
        ca, cn, ha, hn = specs[a]
        idx = [slice(None)] * len(land[a].shape)
        if ca == ha:
            idx[ca] = pl.ds(0, 3 * hn)
        else:
            idx[ca], idx[ha] = pl.ds(0, 3 * cn), pl.ds(0, hn)
        return land[a].at[tuple(idx)]

    def own(a):
        return pltpu.make_async_copy(sh[a], _region(land[a], specs[a], chip_me, None), local_sems.at[a])

    def out(a):
        return pltpu.make_async_copy(land[a], full[a], local_sems.at[a])

    def start():
        for a in range(n):
            own(a).start()
            for px, py in chips:
                remote(a, _half_of(sh[a], specs[a], c), _region(land[a], specs[a], chip_me, c), (px, py, c)).start()

    def forward():
        for a in range(n):
            remote(a, three_halves(a), three_halves(a), sibling).wait_recv()
            for px, py in chips:
                landed = _region(land[a], specs[a], 2 * px + py, c)
                remote(n + a, landed, landed, sibling).start()

    def finish():
        for a in range(n):
            remote(n + a, three_halves(a), three_halves(a), sibling).wait_recv()
            remote(a, three_halves(a), three_halves(a), sibling).wait_send()
            remote(n + a, three_halves(a), three_halves(a), sibling).wait_send()
            own(a).wait()
        for a in range(n):
            out(a).start()
        for a in range(n):
            out(a).wait()

    return start, forward, finish


def _scatter_direct(names, part, slots, send_sems, recv_sems, local_sems):
    specs = [W_SPECS[k][1] for k in names]
    x, y, c = _my_place()
    dev_me = 4 * x + 2 * y + c

    def local(a):
        return pltpu.make_async_copy(_region(part[a], specs[a], 2 * x + y, c), slots[a].at[dev_me], local_sems.at[a])

    def start():
        for a in range(len(names)):
            local(a).start()
            for flip in range(1, N_DEV):
                tx = 1 - x if flip >> 2 else x
                ty = 1 - y if (flip >> 1) & 1 else y
                tc = 1 - c if flip & 1 else c
                pltpu.make_async_remote_copy(src_ref=_region(part[a], specs[a], 2 * tx + ty, tc), dst_ref=slots[a].at[dev_me],
                                             send_sem=send_sems.at[a], recv_sem=recv_sems.at[a], device_id=(tx, ty, tc),
                                             device_id_type=MESH).start()

    def wait():
        for a in range(len(names)):
            seven = slots[a].at[pl.ds(0, N_DEV - 1)]
            pltpu.make_async_remote_copy(src_ref=seven, dst_ref=seven, send_sem=send_sems.at[a], recv_sem=recv_sems.at[a],
                                         device_id=(x, y, c), device_id_type=MESH).wait()
            local(a).wait()

    return start, wait


HBM_SPEC = pl.BlockSpec(memory_space=pltpu.HBM)
SEM_SPEC = pl.BlockSpec(memory_space=pltpu.SEMAPHORE)
SPLIT_EFFECT = pltpu.SideEffectType.DATAFLOW_SIDE_EFFECTING


def _scatter_start(part, slots, after, name_key):
    _, cols, _, _ = W_SPECS[name_key][1]

    def body(part_ref, slots_ref, after_ref, send_sem, recv_sem, part_thru, slots_thru, token):
        x, y, c = _my_place()
        for tx, ty in ((1 - x, y), (x, 1 - y), (1 - x, 1 - y)):
            src = part_ref.at[:, pl.ds(_al((2 * tx + ty) * cols, cols), cols)]
            pltpu.make_async_remote_copy(src_ref=src, dst_ref=slots_ref.at[2 * x + y], send_sem=send_sem, recv_sem=recv_sem,
                                         device_id=(tx, ty, c), device_id_type=MESH).start()
        token[...] = jnp.zeros_like(token)

    return pl.pallas_call(
        body, name="scatter_start_" + name_key,
        out_shape=(pltpu.SemaphoreType.DMA(()), pltpu.SemaphoreType.DMA(()), pltpu.HBM(part.shape, part.dtype),
                   pltpu.HBM(slots.shape, slots.dtype), jax.ShapeDtypeStruct((8, 128), F32)),
        in_specs=(HBM_SPEC, HBM_SPEC, ANY), out_specs=(SEM_SPEC, SEM_SPEC, HBM_SPEC, HBM_SPEC, VMEM),
        input_output_aliases={0: 2, 1: 3},
        compiler_params=pltpu.CompilerParams(has_side_effects=SPLIT_EFFECT),
    )(pltpu.with_memory_space_constraint(part, pltpu.HBM), pltpu.with_memory_space_constraint(slots, pltpu.HBM), after)


def _scatter_wait(send_sem, recv_sem, part_thru, slots_thru, after, name_key):
    def body(part_ref, slots_ref, send_sem, recv_sem, *rest):
        x, y, c = _my_place()
        landed = slots_ref.at[pl.ds(0, N_CHIPS - 1)]
        copy = pltpu.make_async_remote_copy(src_ref=landed, dst_ref=landed, send_sem=send_sem, recv_sem=recv_sem,
                                            device_id=(x, y, c), device_id_type=MESH)
        copy.wait_send()
        copy.wait_recv()

    return pl.pallas_call(
        body, name="scatter_wait_" + name_key,
        out_shape=(pltpu.HBM(part_thru.shape, part_thru.dtype), pltpu.HBM(slots_thru.shape, slots_thru.dtype)),
        in_specs=(HBM_SPEC, HBM_SPEC, SEM_SPEC, SEM_SPEC) + (ANY,) * len(after), out_specs=(HBM_SPEC, HBM_SPEC),
        input_output_aliases={0: 0, 1: 1},
        compiler_params=pltpu.CompilerParams(has_side_effects=SPLIT_EFFECT),
    )(part_thru, slots_thru, send_sem, recv_sem, *after)


def _comm_sems(n):
    return [pltpu.SemaphoreType.DMA((n,)), pltpu.SemaphoreType.DMA((n,)), pltpu.SemaphoreType.DMA((n,))]


GATHER_EARLY = ("hgrn_w_in",)
GATHER_LATE = ("hgrn_w_out", "pool_w_in", "pool_w_grp", "pool_w_out")


def _all_gather_weights(shards, names):
    n = len(names)
    specs = [W_SPECS[k][1] for k in names]

    def body(*refs):
        sh, full = refs[:n], refs[n:2 * n]
        send_sems, recv_sems, local_sems = refs[2 * n:]
        x, y, c = _my_place()
        chip_me = 2 * x + y
        sibling = (x, y, 1 - c)
        chips = [(1 - x, y), (x, 1 - y), (1 - x, 1 - y)]

        def remote(a, k, src, dst, to):
            return pltpu.make_async_remote_copy(src_ref=src, dst_ref=dst, send_sem=send_sems.at[6 * a + k],
                                                recv_sem=recv_sems.at[6 * a + k], device_id=to, device_id_type=MESH)

        local = [pltpu.make_async_copy(sh[a], _region(full[a], specs[a], chip_me, None), local_sems.at[a]) for a in range(n)]
        for cp in local:
            cp.start()
        sends = []
        for a in range(n):
            for j, (px, py) in enumerate(chips):
                src, dst = _half_of(sh[a], specs[a], c), _region(full[a], specs[a], chip_me, c)
                for s_piece, d_piece in zip(_pieces(src), _pieces(dst)):
                    remote(a, j, s_piece, d_piece, (px, py, c)).start()
                sends.append(remote(a, j, src, dst, (px, py, c)))
        for a in range(n):
            for j, (px, py) in enumerate(chips):
                landed = _region(full[a], specs[a], 2 * px + py, c)
                remote(a, j, landed, landed, (px, py, c)).wait_recv()
                for piece in _pieces(landed):
                    remote(a, 3 + j, piece, piece, sibling).start()
                sends.append(remote(a, 3 + j, landed, landed, sibling))
        for a in range(n):
            for j, (px, py) in enumerate(chips):
                other = _region(full[a], specs[a], 2 * px + py, 1 - c)
                remote(a, 3 + j, other, other, sibling).wait_recv()
        for cp in sends:
            cp.wait_send()
        for cp in local:
            cp.wait()

    return pl.pallas_call(
        body, name="all_gather_weights",
        out_shape=[jax.ShapeDtypeStruct(W_SPECS[k][0], BF16) for k in names],
        in_specs=[VMEM] * n, out_specs=[VMEM] * n,
        scratch_shapes=[pltpu.SemaphoreType.DMA((6 * n,)), pltpu.SemaphoreType.DMA((6 * n,)), pltpu.SemaphoreType.DMA((n,))],
        compiler_params=_params(32),
    )(*shards)


def _slot_shapes(names):
    return [jax.ShapeDtypeStruct((N_DEV,) + _half_shape(k), BF16) for k in names]


def _scatter_grads(parts, names):
    n = len(names)

    def body(*refs):
        start, wait = _scatter_direct(names, refs[:n], refs[n:2 * n], *refs[2 * n:])
        start()
        wait()

    return pl.pallas_call(body, name="scatter_grads", out_shape=_slot_shapes(names), in_specs=[ANY] * n, out_specs=[ANY] * n,
                          scratch_shapes=_comm_sems(n))(*parts)


def _exchange_halves(halves, names, name):
    n = len(names)
    specs = [W_SPECS[k][1] for k in names]

    def shard_shape(k):
        shp = list(_half_shape(k))
        shp[W_SPECS[k][1][2]] *= 2
        return tuple(shp)

    def body(*refs):
        half, out = refs[:n], refs[n:2 * n]
        send_sems, recv_sems, local_sems = refs[2 * n:]
        x, y, c = _my_place()
        sibling = (x, y, 1 - c)

        def remote(a, src, dst):
            return pltpu.make_async_remote_copy(src_ref=src, dst_ref=dst, send_sem=send_sems.at[a], recv_sem=recv_sems.at[a],
                                                device_id=sibling, device_id_type=MESH)

        local = [pltpu.make_async_copy(half[a], _half_of(out[a], specs[a], c), local_sems.at[a]) for a in range(n)]
        for cp in local:
            cp.start()
        for a in range(n):
            mine = _half_of(out[a], specs[a], c)
            for src, dst in zip(_pieces(half[a]), _pieces(mine)):
                remote(a, src, dst).start()
        for a in range(n):
            theirs = _half_of(out[a], specs[a], 1 - c)
            remote(a, theirs, theirs).wait_recv()
        for a in range(n):
            remote(a, half[a], half[a]).wait_send()
        for cp in local:
            cp.wait()

    return pl.pallas_call(
        body, name=name,
        out_shape=[jax.ShapeDtypeStruct(shard_shape(k), F32) for k in names],
        in_specs=[VMEM] * n, out_specs=[VMEM] * n,
        scratch_shapes=[pltpu.SemaphoreType.DMA((n,)), pltpu.SemaphoreType.DMA((n,)), pltpu.SemaphoreType.DMA((n,))],
        compiler_params=_params(32),
    )(*halves)


def _mod_parts(c_rows, ada_w, ada_b_cols):
    n_layers, _, n_cols = ada_w.shape

    def body(c_ref, w_ref, b_ref, o_ref):
        o_ref[...] = _mm_f32(_silu(c_ref[...]), w_ref[...]) + b_ref[...]

    return pl.pallas_call(
        body, name="mod_parts", grid=(n_layers,),
        out_shape=jax.ShapeDtypeStruct((n_layers, 16, n_cols), F32),
        in_specs=[pl.BlockSpec((16, D_MODEL), lambda i: (0, 0)),
                  pl.BlockSpec((None, D_MODEL, n_cols), lambda i: (i, 0, 0)),
                  pl.BlockSpec((None, 1, n_cols), lambda i: (i, 0, 0))],
        out_specs=pl.BlockSpec((None, 16, n_cols), lambda i: (i, 0, 0)),
        compiler_params=_params(40),
    )(c_rows, ada_w, ada_b_cols.reshape(n_layers, 1, n_cols))


def _ln_mod_matmul(ctx_tile, xin, nw, shift, scale, w, name):
    n_mod = shift.shape[0]
    tm = ROW_TILE
    skip = n_mod - 1
    rows = xin.shape[0] + skip * tm
    n_sec = w.shape[1] // D_MODEL

    def body(*refs):
        c_ref = refs[0] if skip else None
        x_ref, nw_ref, sh_ref, sc_ref, w_ref, g_ref, h_ref = refs[skip:]
        xv = jnp.where(pl.program_id(0) == 0, c_ref[...], x_ref[...]) if skip else x_ref[...]
        r = lax.rsqrt(jnp.mean(xv * xv, axis=-1, keepdims=True) + EPS)
        h_f32 = (xv * r * nw_ref[...]) * (1.0 + sc_ref[...]) + sh_ref[...]
        h = h_f32.astype(BF16)
        h_ref[...] = h
        for k in range(n_sec):
            g_ref[k] = jnp.dot(h, w_ref[:, k * D_MODEL:(k + 1) * D_MODEL], preferred_element_type=F32)

    mod_spec = pl.BlockSpec((None, 1, D_MODEL), lambda i: (jnp.minimum(i, n_mod - 1), 0, 0))
    return pl.pallas_call(
        body, name=name, grid=(rows // tm,),
        out_shape=[pltpu.HBM((n_sec, rows, D_MODEL), F32), pltpu.HBM((rows, D_MODEL), BF16)],
        in_specs=[pl.BlockSpec((tm, D_MODEL), lambda i: (0, 0))] * skip
        + [pl.BlockSpec((tm, D_MODEL), lambda i: (jnp.maximum(i - skip, 0), 0)),
           pl.BlockSpec((1, D_MODEL), lambda i: (0, 0)),
           mod_spec, mod_spec,
           pl.BlockSpec((D_MODEL, n_sec * D_MODEL), lambda i: (0, 0))],
        out_specs=[pl.BlockSpec((n_sec, tm, D_MODEL), lambda i: (0, i, 0)),
                   pl.BlockSpec((tm, D_MODEL), lambda i: (i, 0))],
        compiler_params=_params(48),
    )(*_pin(*([ctx_tile] * skip), xin, nw, shift.reshape(n_mod, 1, D_MODEL), scale.reshape(n_mod, 1, D_MODEL), w))


def _chunk_masks(rev):
    rid = lax.broadcasted_iota(jnp.int32, (CHUNK, CHUNK), 0)
    cid = lax.broadcasted_iota(jnp.int32, (CHUNK, CHUNK), 1)
    keep = (cid >= rid) if rev else (cid <= rid)
    keep_t = (cid <= rid) if rev else (cid >= rid)
    one, zero = jnp.ones((CHUNK, CHUNK), F32), jnp.zeros((CHUNK, CHUNK), F32)
    return keep, jnp.where(keep, one, zero).astype(BF16), jnp.where(keep_t, one, zero).astype(BF16)


def _chunk_rows(t, rev, latent):
    n = N_LAT_CHUNKS if latent else N_CTX_CHUNKS
    base = CTX_LEN if latent else 0
    idx = (n - 1 - t) if rev else t
    return pl.multiple_of(base + idx * CHUNK, CHUNK)


def _gates(fpre, lb):
    sg = _sig(fpre)
    f = lb + (1.0 - lb) * sg
    return sg, f, 1.0 - f, jnp.log(f)


G_SPEC = lambda sec: pl.BlockSpec((None, ROWS_ALL, HEAD_DIM), lambda h, sec=sec: (sec, 0, h))


def _gla_forward(g5, lb_logits):
    q_scale = HEAD_DIM ** -0.5

    def body(ff_ref, fb_ref, v_ref, q_ref, lg_ref, o_ref, st_ref, decay_ref, qt_ref):
        for rev in (False, True):
            f_ref = fb_ref if rev else ff_ref
            lb = _sig(lg_ref[1:2, :] if rev else lg_ref[0:1, :])
            keep, tri, _ = _chunk_masks(rev)
            last = 0 if rev else CHUNK - 1
            mid = CHUNK // 2 if rev else CHUNK // 2 - 1

            def local_step(t, carry, latent):
                r0 = _chunk_rows(t, rev, latent)
                rows = pl.ds(r0, CHUNK)
                step = t + (N_CTX_CHUNKS if latent else 0)
                _, _, k, lf = _gates(f_ref[rows, :], lb)
                v = v_ref[rows, :]
                b = _mm_exact_lhs(tri, lf)
                bl = b[last:last + 1, :]
                if latent:
                    q = _silu(q_ref[rows, :]) * q_scale
                    bm = b[mid:mid + 1, :]
                    a = _mm_nt(q * jnp.exp(b - bm), k * jnp.exp(bm - b))
                    o = _mm(jnp.where(keep, a, 0.0), v)
                    orow = pl.ds(pl.multiple_of(r0 - CTX_LEN, CHUNK), CHUNK)
                    qt_ref[orow, :] = (q * jnp.exp(b)).astype(BF16)
                    if rev:
                        o_ref[orow, :] += o
                    else:
                        o_ref[orow, :] = o
                decay_ref[step] = jnp.exp(bl)
                st_ref[step] = _mm_tn(v, k * jnp.exp(bl - b))
                return carry

            lax.fori_loop(0, N_CTX_CHUNKS, functools.partial(local_step, latent=False), 0, unroll=2)
            lax.fori_loop(0, N_LAT_CHUNKS, functools.partial(local_step, latent=True), 0, unroll=4)

            def scan_step(t, st):
                update = st_ref[t]
                st_ref[t] = st
                return st * decay_ref[t] + update

            lax.fori_loop(0, N_CHUNKS, scan_step, jnp.zeros((HEAD_DIM, HEAD_DIM), F32), unroll=2)

            def inter_step(t, carry):
                r0 = _chunk_rows(t, rev, True)
                orow = pl.ds(pl.multiple_of(r0 - CTX_LEN, CHUNK), CHUNK)
                o_ref[orow, :] += lax.dot_general(qt_ref[orow, :], st_ref[t + N_CTX_CHUNKS].astype(BF16),
                                                  (((1,), (1,)), ((), ())), preferred_element_type=F32)
                return carry

            lax.fori_loop(0, N_LAT_CHUNKS, inter_step, 0, unroll=4)

    return pl.pallas_call(
        body, name="gla_forward", grid=(HEADS,),
        out_shape=jax.ShapeDtypeStruct((SEQ, D_MODEL), F32),
        in_specs=[G_SPEC(0), G_SPEC(1), G_SPEC(2), G_SPEC(3), pl.BlockSpec((2, HEAD_DIM), lambda h: (0, h))],
        out_specs=pl.BlockSpec((SEQ, HEAD_DIM), lambda h: (0, h)),
        scratch_shapes=[pltpu.VMEM((N_CHUNKS, HEAD_DIM, HEAD_DIM), F32), pltpu.VMEM((N_CHUNKS, 1, HEAD_DIM), F32),
                        pltpu.VMEM((SEQ, HEAD_DIM), BF16)],
        compiler_params=_params(32),
    )(g5, g5, g5, g5, lb_logits)


def _gla_backward(g5, lb_logits, d_o, d_z):
    q_scale = HEAD_DIM ** -0.5

    def body(ff_ref, fb_ref, v_ref, q_ref, lg_ref, do_ref, dz_ref, dg_ref, dlg_ref, st_ref, dst_ref, decay_ref):
        dg_ref[3, 0:CTX_LEN, :] = jnp.zeros((CTX_LEN, HEAD_DIM), F32)
        dg_ref[4, 0:CTX_LEN, :] = jnp.zeros((CTX_LEN, HEAD_DIM), F32)
        dg_ref[4, CTX_LEN:ROWS_ALL, :] = dz_ref[...]
        is_row = lax.broadcasted_iota(jnp.int32, (CHUNK, HEAD_DIM), 0)
        for rev in (False, True):
            d = 1 if rev else 0
            f_ref = fb_ref if rev else ff_ref
            lb = _sig(lg_ref[d:d + 1, :])
            keep, tri, tri_t = _chunk_masks(rev)
            last = 0 if rev else CHUNK - 1
            mid = CHUNK // 2 if rev else CHUNK // 2 - 1

            def local_step(t, carry, latent):
                r0 = _chunk_rows(t, rev, latent)
                rows = pl.ds(r0, CHUNK)
                step = t + (N_CTX_CHUNKS if latent else 0)
                _, _, k, lf = _gates(f_ref[rows, :], lb)
                b = _mm_exact_lhs(tri, lf)
                bl = b[last:last + 1, :]
                decay_ref[step] = jnp.exp(bl)
                st_ref[step] = _mm_tn(v_ref[rows, :], k * jnp.exp(bl - b))
                if latent:
                    q_t = _silu(q_ref[rows, :]) * q_scale * jnp.exp(b)
                    dst_ref[step] = _mm_tn(do_ref[pl.ds(pl.multiple_of(r0 - CTX_LEN, CHUNK), CHUNK), :], q_t)
                else:
                    dst_ref[step] = jnp.zeros((HEAD_DIM, HEAD_DIM), F32)
                return carry

            lax.fori_loop(0, N_CTX_CHUNKS, functools.partial(local_step, latent=False), 0, unroll=2)
            lax.fori_loop(0, N_LAT_CHUNKS, functools.partial(local_step, latent=True), 0, unroll=4)

            def scan_step(i, carry):
                st, d_st = carry
                j = N_CHUNKS - 1 - i
                update, d_update = st_ref[i], dst_ref[j]
                st_ref[i] = st
                dst_ref[j] = d_st
                return st * decay_ref[i] + update, d_st * decay_ref[j] + d_update

            zero_state = jnp.zeros((HEAD_DIM, HEAD_DIM), F32)
            lax.fori_loop(0, N_CHUNKS, scan_step, (zero_state, zero_state))

            def grad_step(t, dlb, latent):
                r0 = _chunk_rows(t, rev, latent)
                rows = pl.ds(r0, CHUNK)
                step = t + (N_CTX_CHUNKS if latent else 0)
                sg, f, k, lf = _gates(f_ref[rows, :], lb)
                v = v_ref[rows, :]
                b = _mm_exact_lhs(tri, lf)
                bl = b[last:last + 1, :]
                e_end = jnp.exp(bl - b)
                k_end = k * e_end
                decay = jnp.exp(bl)
                d_st = dst_ref[step]
                st_prev = st_ref[step]
                d_kend = _mm(v, d_st)
                d_decay = jnp.sum(d_st * st_prev, axis=0, keepdims=True)
                t_kend = d_kend * k_end
                d_v = _mm_nt(k_end, d_st)
                d_k = d_kend * e_end
                d_b = -t_kend
                if latent:
                    qpre = q_ref[rows, :]
                    q = _silu(qpre) * q_scale
                    bm = b[mid:mid + 1, :]
                    e_b, e_qm, e_km = jnp.exp(b), jnp.exp(b - bm), jnp.exp(bm - b)
                    q_t, q_m, k_m = q * e_b, q * e_qm, k * e_km
                    a = jnp.where(keep, _mm_nt(q_m, k_m), 0.0)
                    d_out = do_ref[pl.ds(pl.multiple_of(r0 - CTX_LEN, CHUNK), CHUNK), :]
                    d_a = jnp.where(keep, _mm_nt(d_out, v), 0.0)
                    d_qm = _mm(d_a, k_m)
                    d_km = _mm_tn(d_a, q_m)
                    d_qt = _mm(d_out, st_prev)
                    d_v = d_v + _mm_tn(a, d_out)
                    d_k = d_k + d_km * e_km
                    d_b = d_b + d_qt * q_t + d_qm * q_m - d_km * k_m
                    d_q = d_qt * e_b + d_qm * e_qm
                at_last = jnp.sum(t_kend, axis=0, keepdims=True) + d_decay * decay
                d_b = d_b + jnp.where(is_row == last, at_last, 0.0)
                d_lf = _mm_exact_lhs(tri_t, d_b)
                d_f = d_lf / f - d_k
                dg_ref[d, rows, :] = d_f * (1.0 - lb) * sg * (1.0 - sg)
                if rev:
                    dg_ref[2, rows, :] += d_v
                else:
                    dg_ref[2, rows, :] = d_v
                if latent:
                    d_qpre = d_q * q_scale * _dsilu(qpre)
                    if rev:
                        dg_ref[3, rows, :] += d_qpre
                    else:
                        dg_ref[3, rows, :] = d_qpre
                return dlb + jnp.sum(d_f * (1.0 - sg), axis=0, keepdims=True)

            dlb = lax.fori_loop(0, N_LAT_CHUNKS, functools.partial(grad_step, latent=True), jnp.zeros((1, HEAD_DIM), F32),
                                unroll=2)
            dlb = lax.fori_loop(0, N_CTX_CHUNKS, functools.partial(grad_step, latent=False), dlb, unroll=2)
            dlg_ref[d:d + 1, :] = dlb * lb * (1.0 - lb)

    col = pl.BlockSpec((SEQ, HEAD_DIM), lambda h: (0, h))
    return pl.pallas_call(
        body, name="gla_backward", grid=(HEADS,),
        out_shape=[jax.ShapeDtypeStruct((HGRN_SECTIONS, ROWS_ALL, D_MODEL), F32), jax.ShapeDtypeStruct((2, D_MODEL), F32)],
        in_specs=[G_SPEC(0), G_SPEC(1), G_SPEC(2), G_SPEC(3), pl.BlockSpec((2, HEAD_DIM), lambda h: (0, h)), col, col],
        out_specs=[pl.BlockSpec((HGRN_SECTIONS, ROWS_ALL, HEAD_DIM), lambda h: (0, 0, h)),
                   pl.BlockSpec((2, HEAD_DIM), lambda h: (0, h))],
        scratch_shapes=[pltpu.VMEM((N_CHUNKS, HEAD_DIM, HEAD_DIM), F32), pltpu.VMEM((N_CHUNKS, HEAD_DIM, HEAD_DIM), F32),
                        pltpu.VMEM((N_CHUNKS, 1, HEAD_DIM), F32)],
        compiler_params=_params(48),
    )(g5, g5, g5, g5, lb_logits, d_o, d_z)


GROUP = 2 * CHUNK


def _group_masks(rev):
    rid = lax.broadcasted_iota(jnp.int32, (GROUP, GROUP), 0)
    cid = lax.broadcasted_iota(jnp.int32, (GROUP, GROUP), 1)
    same = (rid >= CHUNK) == (cid >= CHUNK)
    causal = (cid >= rid) if rev else (cid <= rid)
    anti = (cid <= rid) if rev else (cid >= rid)
    bf = lambda m: jnp.where(m, jnp.ones((GROUP, GROUP), F32), jnp.zeros((GROUP, GROUP), F32)).astype(BF16)
    keep = same & causal
    return keep, bf(keep), bf(same & anti), bf(same & jnp.logical_not(anti))


def _group_sum(m_bf, a):
    hi, lo = _split2(a)
    r = jnp.dot(m_bf, jnp.concatenate([hi, lo], axis=1), preferred_element_type=F32)
    return r[:, :HEAD_DIM] + r[:, HEAD_DIM:]


def _chunk_row(a, pos):
    return jnp.concatenate([jnp.broadcast_to(a[c * CHUNK + pos:c * CHUNK + pos + 1, :], (CHUNK, HEAD_DIM)) for c in range(2)], axis=0)


def _by_chunk(a, second):
    return jnp.concatenate([jnp.where(second, 0.0, a), jnp.where(second, a, 0.0)], axis=1)


def _own_block(r):
    return jnp.concatenate([r[0:CHUNK, 0:HEAD_DIM], r[CHUNK:GROUP, HEAD_DIM:2 * HEAD_DIM]], axis=0)


def _scan_step_of(row_chunk, rev, latent):
    if not rev:
        return row_chunk
    return (N_CHUNKS + N_CTX_CHUNKS - 1 - row_chunk) if latent else (N_CTX_CHUNKS - 1 - row_chunk)


def _group_rows(i, j, per_step, latent):
    base = CTX_LEN if latent else 0
    return pl.multiple_of(base + (i * per_step + j) * GROUP, GROUP)


GROUPS_PER_STEP = 8
GROUPS_PER_GRAD_STEP = 4


def _gla_forward(g5, lb_logits, late_shards):
    q_scale = HEAD_DIM ** -0.5
    per_lat, per_ctx = GROUPS_PER_STEP, min(GROUPS_PER_STEP, CTX_LEN // GROUP)
    n_late = len(GATHER_LATE)

    def body(ff_ref, fb_ref, v_ref, q_ref, lg_ref, *rest):
        shard_refs, o_ref, full_refs = rest[:n_late], rest[n_late], rest[n_late + 1:2 * n_late + 1]
        st_ref, decay_ref, qt_ref = rest[2 * n_late + 1:2 * n_late + 4]
        land_refs = rest[2 * n_late + 4:3 * n_late + 4]
        start_gather, forward_gather, finish_gather = _gather_two_level(GATHER_LATE, shard_refs, land_refs, full_refs,
                                                                        *rest[3 * n_late + 4:])

        @pl.when(pl.program_id(0) == 0)
        def _():
            start_gather()

        @pl.when(pl.program_id(0) == HEADS - 1)
        def _():
            forward_gather()

        second = lax.broadcasted_iota(jnp.int32, (GROUP, HEAD_DIM), 0) >= CHUNK
        for rev in (False, True):
            f_ref = fb_ref if rev else ff_ref
            lb = _sig(lg_ref[1:2, :] if rev else lg_ref[0:1, :])
            keep, tri, _, _ = _group_masks(rev)
            last = 0 if rev else CHUNK - 1
            mid = CHUNK // 2 if rev else CHUNK // 2 - 1

            def local_step(i, carry, latent, per):
                r0s = [_group_rows(i, j, per, latent) for j in range(per)]
                rows = [pl.ds(r0, GROUP) for r0 in r0s]
                gates = [_gates(f_ref[r, :], lb) for r in rows]
                vs = [v_ref[r, :] for r in rows]
                bs = [_group_sum(tri, g[3]) for g in gates]
                bls = [_chunk_row(b, last) for b in bs]
                ups = [_mm_tn(v, _by_chunk(g[2] * jnp.exp(bl - b), second)) for v, g, b, bl in zip(vs, gates, bs, bls)]
                if latent:
                    qs = [_silu(q_ref[r, :]) * q_scale for r in rows]
                    bms = [_chunk_row(b, mid) for b in bs]
                    a_s = [_mm_nt(q * jnp.exp(b - bm), g[2] * jnp.exp(bm - b)) for q, g, b, bm in zip(qs, gates, bs, bms)]
                    outs = [_mm(jnp.where(keep, a, 0.0), v) for a, v in zip(a_s, vs)]
                for j in range(per):
                    for c in range(2):
                        step = _scan_step_of(r0s[j] // CHUNK + c, rev, latent)
                        decay_ref[step] = jnp.exp(bls[j][c * CHUNK:c * CHUNK + 1, :])
                        st_ref[step] = ups[j][:, c * HEAD_DIM:(c + 1) * HEAD_DIM]
                    if latent:
                        orow = pl.ds(pl.multiple_of(r0s[j] - CTX_LEN, GROUP), GROUP)
                        qt_ref[orow, :] = (qs[j] * jnp.exp(bs[j])).astype(BF16)
                        if rev:
                            o_ref[orow, :] += outs[j]
                        else:
                            o_ref[orow, :] = outs[j]
                return carry

            lax.fori_loop(0, CTX_LEN // (per_ctx * GROUP), functools.partial(local_step, latent=False, per=per_ctx), 0)
            lax.fori_loop(0, SEQ // (per_lat * GROUP), functools.partial(local_step, latent=True, per=per_lat), 0)

            def scan_step(t, st):
                update = st_ref[t]
                st_ref[t] = st
                return st * decay_ref[t] + update

            lax.fori_loop(0, N_CHUNKS, scan_step, jnp.zeros((HEAD_DIM, HEAD_DIM), F32), unroll=2)

            def inter_step(i, carry):
                r0s = [_group_rows(i, j, per_lat, True) for j in range(per_lat)]
                orows = [pl.ds(pl.multiple_of(r0 - CTX_LEN, GROUP), GROUP) for r0 in r0s]
                states = [jnp.concatenate([st_ref[_scan_step_of(r0 // CHUNK + c, rev, True)] for c in range(2)], axis=0)
                          for r0 in r0s]
                prods = [lax.dot_general(qt_ref[orow, :], s.astype(BF16), (((1,), (1,)), ((), ())), preferred_element_type=F32)
                         for orow, s in zip(orows, states)]
                for orow, r in zip(orows, prods):
                    o_ref[orow, :] += _own_block(r)
                return carry

            lax.fori_loop(0, SEQ // (per_lat * GROUP), inter_step, 0)

        @pl.when(pl.program_id(0) == HEADS - 1)
        def _():
            finish_gather()

    outs = pl.pallas_call(
        body, name="gla_forward", grid=(HEADS,),
        out_shape=[pltpu.HBM((SEQ, D_MODEL), F32)] + [jax.ShapeDtypeStruct(W_SPECS[k][0], BF16) for k in GATHER_LATE],
        in_specs=[G_SPEC(0), G_SPEC(1), G_SPEC(2), G_SPEC(3), pl.BlockSpec((2, HEAD_DIM), lambda h: (0, h))] + [ANY] * n_late,
        out_specs=[pl.BlockSpec((SEQ, HEAD_DIM), lambda h: (0, h))] + [ANY] * n_late,
        scratch_shapes=[pltpu.VMEM((N_CHUNKS, HEAD_DIM, HEAD_DIM), F32), pltpu.VMEM((N_CHUNKS, 1, HEAD_DIM), F32),
                        pltpu.VMEM((SEQ, HEAD_DIM), BF16)] + [pltpu.VMEM(W_SPECS[k][0], BF16) for k in GATHER_LATE]
        + [pltpu.SemaphoreType.DMA((2 * n_late,)), pltpu.SemaphoreType.DMA((2 * n_late,)), pltpu.SemaphoreType.DMA((n_late,))],
        compiler_params=_params(48),
    )(*_pin(g5, g5, g5, g5, lb_logits), *late_shards)
    return outs[0], dict(zip(GATHER_LATE, outs[1:]))


def _gla_backward(g5, lb_logits, d_o, d_z, late_parts):
    q_scale = HEAD_DIM ** -0.5
    per_lat, per_ctx = GROUPS_PER_STEP, min(GROUPS_PER_STEP, CTX_LEN // GROUP)
    n_late = len(GATHER_LATE)

    def body(ff_ref, fb_ref, v_ref, q_ref, lg_ref, do_ref, dz_ref, *rest):
        part_refs, (dg_ref, dlg_ref), slot_refs = rest[:n_late], rest[n_late:n_late + 2], rest[n_late + 2:2 * n_late + 2]
        st_ref, dst_ref, decay_ref = rest[2 * n_late + 2:2 * n_late + 5]
        start_scatter, wait_scatter = _scatter_direct(GATHER_LATE, part_refs, slot_refs, *rest[2 * n_late + 5:])

        @pl.when(pl.program_id(0) == 0)
        def _():
            start_scatter()

        dg_ref[3, 0:CTX_LEN, :] = jnp.zeros((CTX_LEN, HEAD_DIM), F32)
        dg_ref[4, 0:CTX_LEN, :] = jnp.zeros((CTX_LEN, HEAD_DIM), F32)
        dg_ref[4, CTX_LEN:ROWS_ALL, :] = dz_ref[...]
        second = lax.broadcasted_iota(jnp.int32, (GROUP, HEAD_DIM), 0) >= CHUNK
        for rev in (False, True):
            d = 1 if rev else 0
            f_ref = fb_ref if rev else ff_ref
            lb = _sig(lg_ref[d:d + 1, :])
            keep, tri, tri_t, strict = _group_masks(rev)
            last = 0 if rev else CHUNK - 1
            mid = CHUNK // 2 if rev else CHUNK // 2 - 1

            def local_step(i, carry, latent, per):
                r0s = [_group_rows(i, j, per, latent) for j in range(per)]
                rows = [pl.ds(r0, GROUP) for r0 in r0s]
                gates = [_gates(f_ref[r, :], lb) for r in rows]
                bs = [_group_sum(tri, g[3]) for g in gates]
                bls = [_chunk_row(b, last) for b in bs]
                ups = [_mm_tn(v_ref[r, :], _by_chunk(g[2] * jnp.exp(bl - b), second)) for r, g, b, bl in zip(rows, gates, bs, bls)]
                if latent:
                    orows = [pl.ds(pl.multiple_of(r0 - CTX_LEN, GROUP), GROUP) for r0 in r0s]
                    d_ups = [_mm_tn(do_ref[orow, :], _by_chunk(_silu(q_ref[r, :]) * q_scale * jnp.exp(b), second))
                             for orow, r, b in zip(orows, rows, bs)]
                for j in range(per):
                    for c in range(2):
                        step = _scan_step_of(r0s[j] // CHUNK + c, rev, latent)
                        decay_ref[step] = jnp.exp(bls[j][c * CHUNK:c * CHUNK + 1, :])
                        st_ref[step] = ups[j][:, c * HEAD_DIM:(c + 1) * HEAD_DIM]
                        if latent:
                            dst_ref[step] = d_ups[j][:, c * HEAD_DIM:(c + 1) * HEAD_DIM]
                        else:
                            dst_ref[step] = jnp.zeros((HEAD_DIM, HEAD_DIM), F32)
                return carry

            lax.fori_loop(0, CTX_LEN // (per_ctx * GROUP), functools.partial(local_step, latent=False, per=per_ctx), 0)
            lax.fori_loop(0, SEQ // (per_lat * GROUP), functools.partial(local_step, latent=True, per=per_lat), 0)

            def scan_step(i, carry):
                st, d_st = carry
                j = N_CHUNKS - 1 - i
                update, d_update = st_ref[i], dst_ref[j]
                st_ref[i] = st
                dst_ref[j] = d_st
                return st * decay_ref[i] + update, d_st * decay_ref[j] + d_update

            zero_state = jnp.zeros((HEAD_DIM, HEAD_DIM), F32)
            lax.fori_loop(0, N_CHUNKS, scan_step, (zero_state, zero_state))

            def grad_step(i, dlb, latent, per):
                r0s = [_group_rows(i, j, per, latent) for j in range(per)]
                rows = [pl.ds(r0, GROUP) for r0 in r0s]
                gates = [_gates(f_ref[r, :], lb) for r in rows]
                vs = [v_ref[r, :] for r in rows]
                bs = [_group_sum(tri, g[3]) for g in gates]
                bls = [_chunk_row(b, last) for b in bs]
                e_ends = [jnp.exp(bl - b) for b, bl in zip(bs, bls)]
                k_ends = [g[2] * e for g, e in zip(gates, e_ends)]
                sts = [[st_ref[_scan_step_of(r0 // CHUNK + c, rev, latent)] for c in range(2)] for r0 in r0s]
                d_sts = [[dst_ref[_scan_step_of(r0 // CHUNK + c, rev, latent)] for c in range(2)] for r0 in r0s]
                d_kends = [_own_block(_mm(v, jnp.concatenate(ds, axis=1))) for v, ds in zip(vs, d_sts)]
                d_vs = [_own_block(_mm_nt(ke, jnp.concatenate(ds, axis=0))) for ke, ds in zip(k_ends, d_sts)]
                at_last = [jnp.concatenate([jnp.broadcast_to(jnp.sum(ds[c] * s[c], axis=0, keepdims=True), (CHUNK, HEAD_DIM))
                                            for c in range(2)], axis=0) * jnp.exp(bl) for ds, s, bl in zip(d_sts, sts, bls)]
                t_kends = [dk * ke for dk, ke in zip(d_kends, k_ends)]
                d_ks = [dk * e for dk, e in zip(d_kends, e_ends)]
                d_lfs = [_group_sum(strict, t) + al for t, al in zip(t_kends, at_last)]
                if latent:
                    orows = [pl.ds(pl.multiple_of(r0 - CTX_LEN, GROUP), GROUP) for r0 in r0s]
                    qpres = [q_ref[r, :] for r in rows]
                    q_sigs = [_sig(qp) for qp in qpres]
                    qs = [qp * sg * q_scale for qp, sg in zip(qpres, q_sigs)]
                    bms = [_chunk_row(b, mid) for b in bs]
                    e_bs = [jnp.exp(b) for b in bs]
                    e_qms = [jnp.exp(b - bm) for b, bm in zip(bs, bms)]
                    e_kms = [jnp.exp(bm - b) for b, bm in zip(bs, bms)]
                    q_ts = [q * e for q, e in zip(qs, e_bs)]
                    q_ms = [q * e for q, e in zip(qs, e_qms)]
                    k_ms = [g[2] * e for g, e in zip(gates, e_kms)]
                    d_outs = [do_ref[orow, :] for orow in orows]
                    a_s = [jnp.where(keep, _mm_nt(qm, km), 0.0) for qm, km in zip(q_ms, k_ms)]
                    d_as = [jnp.where(keep, _mm_nt(do, v), 0.0) for do, v in zip(d_outs, vs)]
                    d_qts = [_own_block(_mm(do, jnp.concatenate(s, axis=1))) for do, s in zip(d_outs, sts)]
                    d_qms = [_mm(da, km) for da, km in zip(d_as, k_ms)]
                    d_kms = [_mm_tn(da, qm) for da, qm in zip(d_as, q_ms)]
                    d_vs = [dv + _mm_tn(a, do) for dv, a, do in zip(d_vs, a_s, d_outs)]
                    d_ks = [dk + dkm * e for dk, dkm, e in zip(d_ks, d_kms, e_kms)]
                    d_lfs = [dl + _group_sum(tri_t, dqt * qt + dqm * qm - dkm * km)
                             for dl, dqt, qt, dqm, qm, dkm, km in zip(d_lfs, d_qts, q_ts, d_qms, q_ms, d_kms, k_ms)]
                    d_qs = [dqt * eb + dqm * eq for dqt, eb, dqm, eq in zip(d_qts, e_bs, d_qms, e_qms)]
                for j in range(per):
                    sg, f = gates[j][0], gates[j][1]
                    d_f = d_lfs[j] / f - d_ks[j]
                    dg_ref[d, rows[j], :] = d_f * (1.0 - lb) * sg * (1.0 - sg)
                    dlb = dlb + jnp.sum(d_f * (1.0 - sg), axis=0, keepdims=True)
                    if rev:
                        dg_ref[2, rows[j], :] += d_vs[j]
                    else:
                        dg_ref[2, rows[j], :] = d_vs[j]
                    if latent:
                        d_qpre = d_qs[j] * q_scale * (q_sigs[j] * (1.0 + qpres[j] * (1.0 - q_sigs[j])))
                        if rev:
                            dg_ref[3, rows[j], :] += d_qpre
                        else:
                            dg_ref[3, rows[j], :] = d_qpre
                return dlb

            dlb = lax.fori_loop(0, SEQ // (GROUPS_PER_GRAD_STEP * GROUP),
                                functools.partial(grad_step, latent=True, per=GROUPS_PER_GRAD_STEP), jnp.zeros((1, HEAD_DIM), F32))
            dlb = lax.fori_loop(0, CTX_LEN // (per_ctx * GROUP), functools.partial(grad_step, latent=False, per=per_ctx), dlb)
            dlg_ref[d:d + 1, :] = dlb * lb * (1.0 - lb)

        @pl.when(pl.program_id(0) == HEADS - 1)
        def _():
            wait_scatter()

    col = pl.BlockSpec((SEQ, HEAD_DIM), lambda h: (0, h))
    outs = pl.pallas_call(
        body, name="gla_backward", grid=(HEADS,),
        out_shape=[pltpu.HBM((HGRN_SECTIONS, ROWS_ALL, D_MODEL), F32), jax.ShapeDtypeStruct((2, D_MODEL), F32)]
        + _slot_shapes(GATHER_LATE),
        in_specs=[G_SPEC(0), G_SPEC(1), G_SPEC(2), G_SPEC(3), pl.BlockSpec((2, HEAD_DIM), lambda h: (0, h)), col, col]
        + [ANY] * n_late,
        out_specs=[pl.BlockSpec((HGRN_SECTIONS, ROWS_ALL, HEAD_DIM), lambda h: (0, 0, h)),
                   pl.BlockSpec((2, HEAD_DIM), lambda h: (0, h))] + [ANY] * n_late,
        scratch_shapes=[pltpu.VMEM((N_CHUNKS, HEAD_DIM, HEAD_DIM), F32), pltpu.VMEM((N_CHUNKS, HEAD_DIM, HEAD_DIM), F32),
                        pltpu.VMEM((N_CHUNKS, 1, HEAD_DIM), F32)] + _comm_sems(n_late),
        compiler_params=_params(48),
    )(*_pin(g5, g5, g5, g5, lb_logits, d_o, d_z), *late_parts)
    return outs[0], outs[1], dict(zip(GATHER_LATE, outs[2:]))


def _head_norm(o, gw, scr):
    rs = []
    for h in range(HEADS):
        cols = slice(h * HEAD_DIM, (h + 1) * HEAD_DIM)
        oh = o[:, cols]
        r = lax.rsqrt(jnp.mean(oh * oh, axis=-1, keepdims=True) + EPS)
        scr[:, cols] = oh * r
        rs.append(r)
    return rs


def _hgrn_out_forward(o_raw, g5, xin, gnorm_w, gate, w_out):
    tm = ROW_TILE

    def body(o_ref, z_ref, x_ref, gw_ref, gate_ref, w_ref, x1_ref, res_ref, scr):
        _head_norm(o_ref[...], None, scr)
        a = scr[...] * gw_ref[...] * _silu(z_ref[...])
        res = _mm(a, w_ref[...])
        res_ref[...] = res
        x1_ref[...] = x_ref[...] + gate_ref[...] * res

    tile = pl.BlockSpec((tm, D_MODEL), lambda i: (i, 0))
    vec = pl.BlockSpec((1, D_MODEL), lambda i: (0, 0))
    return pl.pallas_call(
        body, name="hgrn_out_forward", grid=(SEQ // tm,),
        out_shape=[pltpu.HBM((SEQ, D_MODEL), F32)] * 2,
        in_specs=[tile, pl.BlockSpec((None, tm, D_MODEL), lambda i: (4, i + CTX_LEN // tm, 0)), tile, vec, vec,
                  pl.BlockSpec((D_MODEL, D_MODEL), lambda i: (0, 0))],
        out_specs=[tile, tile],
        scratch_shapes=[pltpu.VMEM((tm, D_MODEL), F32)],
        compiler_params=_params(32),
    )(*_pin(o_raw, g5, xin, gnorm_w, gate, w_out))


def _hgrn_out_backward(d_x1, o_raw, g5, res, gnorm_w, gate, w_out):
    tm = ROW_TILE

    def body(dx_ref, o_ref, z_ref, res_ref, gw_ref, gate_ref, w_ref, do_ref, dz_ref, dw_out, dgate_ref, dgw_ref, scr, scr2,
             dw_ref):
        @pl.when(pl.program_id(0) == 0)
        def _():
            dw_ref[...] = jnp.zeros_like(dw_ref)
            dgate_ref[...] = jnp.zeros_like(dgate_ref)
            dgw_ref[...] = jnp.zeros_like(dgw_ref)

        dx = dx_ref[...]
        dgate_ref[...] += jnp.sum(dx * res_ref[...], axis=0, keepdims=True)
        d_res = (dx * gate_ref[...]).astype(BF16)
        d_a = _mm_nt(d_res, w_ref[...])
        rs = _head_norm(o_ref[...], None, scr)
        z = z_ref[...]
        sz = _silu(z)
        o_hat = scr[...]
        o_n = o_hat * gw_ref[...]
        dw_ref[...] += _mm_tn(o_n * sz, d_res)
        d_on = d_a * sz
        dz_ref[...] = d_a * o_n * _dsilu(z)
        dgw_ref[...] += jnp.sum(d_on * o_hat, axis=0, keepdims=True)
        scr2[...] = d_on * gw_ref[...]
        for h in range(HEADS):
            cols = slice(h * HEAD_DIM, (h + 1) * HEAD_DIM)
            dh, oh = scr2[:, cols], scr[:, cols]
            do_ref[:, cols] = rs[h] * (dh - oh * jnp.mean(dh * oh, axis=-1, keepdims=True))

        @pl.when(pl.program_id(0) == SEQ // tm - 1)
        def _():
            dw_out[...] = dw_ref[...].astype(BF16)

    tile = pl.BlockSpec((tm, D_MODEL), lambda i: (i, 0))
    vec = pl.BlockSpec((1, D_MODEL), lambda i: (0, 0))
    mat = pl.BlockSpec((D_MODEL, D_MODEL), lambda i: (0, 0))
    return pl.pallas_call(
        body, name="hgrn_out_backward", grid=(SEQ // tm,),
        out_shape=[pltpu.HBM((SEQ, D_MODEL), F32)] * 2 + [pltpu.HBM((D_MODEL, D_MODEL), BF16)]
        + [jax.ShapeDtypeStruct((1, D_MODEL), F32)] * 2,
        in_specs=[tile, tile, pl.BlockSpec((None, tm, D_MODEL), lambda i: (4, i + CTX_LEN // tm, 0)), tile, vec, vec, mat],
        out_specs=[tile, tile, mat, vec, vec],
        scratch_shapes=[pltpu.VMEM((tm, D_MODEL), F32)] * 2 + [pltpu.VMEM((D_MODEL, D_MODEL), F32)],
        compiler_params=_params(40),
    )(*_pin(d_x1, o_raw, g5, res, gnorm_w, gate, w_out))


def _pool_constants():
    win = np.zeros((POOL_GROUPS, ROW_TILE, ROW_TILE), np.float32)
    inv = np.zeros((POOL_GROUPS, ROW_TILE, 1), np.float32)
    for g, w in enumerate(POOL_WINDOWS):
        for t in range(ROW_TILE):
            base, p = (t // GRID_W) * GRID_W, t % GRID_W
            lo = min(max(p - w // 2, 0), GRID_W)
            hi = min(max(p - w // 2 + w, 0), GRID_W)
            win[g, t, base + lo:base + hi] = 1.0
            inv[g, t, 0] = 1.0 / np.float32(hi - lo)
    return jnp.asarray(win, BF16), jnp.asarray(win.transpose(0, 2, 1), BF16), jnp.asarray(inv, F32)


def _pool_mix(u_ref, wg_ref, ps_ref, win_ref, inv_ref, pooled_scr, yg_scr):
    for g in range(POOL_GROUPS):
        cols = slice(g * POOL_GROUP_DIM, (g + 1) * POOL_GROUP_DIM)
        ug = u_ref[:, cols]
        pooled = _mm_exact_lhs(win_ref[g], ug) * inv_ref[g] - ug
        if pooled_scr is not None:
            pooled_scr[:, cols] = pooled
        yg_scr[:, cols] = _mm(pooled, wg_ref[g])


def _pool_forward_loss(uz, x1, target, gate, w_grp, pool_scale, w_out, final_w):
    tm = ROW_TILE
    win, _, inv = _pool_constants()

    def body(u_ref, z_ref, x_ref, t_ref, gate_ref, wg_ref, ps_ref, w_ref, fw_ref, win_ref, inv_ref,
             dx_ref, loss_ref, dfw_ref, dgate_ref, yg_scr):
        @pl.when(pl.program_id(0) == 0)
        def _():
            loss_ref[...] = jnp.zeros_like(loss_ref)
            dfw_ref[...] = jnp.zeros_like(dfw_ref)
            dgate_ref[...] = jnp.zeros_like(dgate_ref)

        _pool_mix(u_ref, wg_ref, ps_ref, win_ref, inv_ref, None, yg_scr)
        a = yg_scr[...] * ps_ref[...] * _silu(z_ref[...])
        res = _mm(a, w_ref[...])
        x2 = x_ref[...] + gate_ref[...] * res
        r = lax.rsqrt(jnp.mean(x2 * x2, axis=-1, keepdims=True) + EPS)
        xh = x2 * r
        fw = fw_ref[...]
        err = xh * fw - t_ref[...]
        loss_ref[...] += 0.5 * jnp.sum(jnp.mean(err * err, axis=-1, keepdims=True))
        d_y = err * (1.0 / D_MODEL)
        dfw_ref[...] += jnp.sum(d_y * xh, axis=0, keepdims=True)
        d_xh = d_y * fw
        d_x2 = r * (d_xh - xh * jnp.mean(d_xh * xh, axis=-1, keepdims=True))
        dx_ref[...] = d_x2
        dgate_ref[...] += jnp.sum(d_x2 * res, axis=0, keepdims=True)

    tile = pl.BlockSpec((tm, D_MODEL), lambda i: (i, 0))
    vec = pl.BlockSpec((1, D_MODEL), lambda i: (0, 0))
    grp = pl.BlockSpec((POOL_GROUPS, POOL_GROUP_DIM, POOL_GROUP_DIM), lambda i: (0, 0, 0))
    return pl.pallas_call(
        body, name="pool_forward_loss", grid=(SEQ // tm,),
        out_shape=[pltpu.HBM((SEQ, D_MODEL), F32), jax.ShapeDtypeStruct((8, 128), F32),
                   jax.ShapeDtypeStruct((1, D_MODEL), F32), jax.ShapeDtypeStruct((1, D_MODEL), F32)],
        in_specs=[pl.BlockSpec((None, tm, D_MODEL), lambda i: (0, i, 0)), pl.BlockSpec((None, tm, D_MODEL), lambda i: (1, i, 0)),
                  tile, tile, vec, grp, vec, pl.BlockSpec((D_MODEL, D_MODEL), lambda i: (0, 0)), vec, grp,
                  pl.BlockSpec((POOL_GROUPS, ROW_TILE, 1), lambda i: (0, 0, 0))],
        out_specs=[tile, pl.BlockSpec((8, 128), lambda i: (0, 0)), vec, vec],
        scratch_shapes=[pltpu.VMEM((tm, D_MODEL), F32)],
        compiler_params=_params(32),
    )(*_pin(uz, uz, x1, target, gate, w_grp, pool_scale, w_out, final_w, win, inv))


def _pool_backward(d_x2, uz, gate, w_grp, pool_scale, w_out):
    tm = ROW_TILE
    win, win_t, inv = _pool_constants()

    def body(dx_ref, u_ref, z_ref, gate_ref, wg_ref, ps_ref, w_ref, win_ref, wint_ref, inv_ref,
             duz_ref, dw_out, dwg_out, dps_ref, pooled_scr, yg_scr, dyg_scr, dw_ref, dwg_ref):
        @pl.when(pl.program_id(0) == 0)
        def _():
            dw_ref[...] = jnp.zeros_like(dw_ref)
            dwg_ref[...] = jnp.zeros_like(dwg_ref)
            dps_ref[...] = jnp.zeros_like(dps_ref)

        _pool_mix(u_ref, wg_ref, ps_ref, win_ref, inv_ref, pooled_scr, yg_scr)
        z = z_ref[...]
        sz = _silu(z)
        yg = yg_scr[...]
        y = yg * ps_ref[...]
        d_res = (dx_ref[...] * gate_ref[...]).astype(BF16)
        d_a = _mm_nt(d_res, w_ref[...])
        dw_ref[...] += _mm_tn(y * sz, d_res)
        d_y = d_a * sz
        duz_ref[1] = d_a * y * _dsilu(z)
        dps_ref[...] += jnp.sum(d_y * yg, axis=0, keepdims=True)
        dyg_scr[...] = d_y * ps_ref[...]
        for g in range(POOL_GROUPS):
            cols = slice(g * POOL_GROUP_DIM, (g + 1) * POOL_GROUP_DIM)
            d_yg = dyg_scr[:, cols].astype(BF16)
            d_pool = _mm_nt(d_yg, wg_ref[g])
            dwg_ref[g] += _mm_tn(pooled_scr[:, cols], d_yg)
            duz_ref[0, :, cols] = _mm_exact_lhs(wint_ref[g], d_pool * inv_ref[g]) - d_pool

        @pl.when(pl.program_id(0) == SEQ // tm - 1)
        def _():
            dw_out[...] = dw_ref[...].astype(BF16)
            dwg_out[...] = dwg_ref[...].astype(BF16)

    tile = pl.BlockSpec((tm, D_MODEL), lambda i: (i, 0))
    vec = pl.BlockSpec((1, D_MODEL), lambda i: (0, 0))
    mat = pl.BlockSpec((D_MODEL, D_MODEL), lambda i: (0, 0))
    grp = pl.BlockSpec((POOL_GROUPS, POOL_GROUP_DIM, POOL_GROUP_DIM), lambda i: (0, 0, 0))
    return pl.pallas_call(
        body, name="pool_backward", grid=(SEQ // tm,),
        out_shape=[pltpu.HBM((POOL_SECTIONS, SEQ, D_MODEL), F32), pltpu.HBM((D_MODEL, D_MODEL), BF16),
                   pltpu.HBM((POOL_GROUPS, POOL_GROUP_DIM, POOL_GROUP_DIM), BF16), jax.ShapeDtypeStruct((1, D_MODEL), F32)],
        in_specs=[tile, pl.BlockSpec((None, tm, D_MODEL), lambda i: (0, i, 0)), pl.BlockSpec((None, tm, D_MODEL), lambda i: (1, i, 0)),
                  vec, grp, vec, mat, grp, grp, pl.BlockSpec((POOL_GROUPS, ROW_TILE, 1), lambda i: (0, 0, 0))],
        out_specs=[pl.BlockSpec((POOL_SECTIONS, tm, D_MODEL), lambda i: (0, i, 0)), mat, grp, vec],
        scratch_shapes=[pltpu.VMEM((tm, D_MODEL), F32)] * 3 + [pltpu.VMEM((D_MODEL, D_MODEL), F32),
                                                               pltpu.VMEM((POOL_GROUPS, POOL_GROUP_DIM, POOL_GROUP_DIM), F32)],
        compiler_params=_params(40),
    )(*_pin(d_x2, uz, uz, gate, w_grp, pool_scale, w_out, win, win_t, inv))


def _ln_mod_backward(d_g, w, ctx_tile, xin, nw, scale, d_up, name):
    n_sec, rows, _ = d_g.shape
    n_mod = scale.shape[0]
    tm = ROW_TILE
    n_tiles = rows // tm
    skip = n_mod - 1

    def body(dg_ref, w_ref, *refs):
        c_ref = refs[0] if skip else None
        x_ref, nw_ref, sc_ref, up_ref, dx_ref, dnw_ref, dmod_ref = refs[skip:]
        i = pl.program_id(0)

        @pl.when(i == 0)
        def _():
            dnw_ref[...] = jnp.zeros_like(dnw_ref)

        @pl.when((i == 0) | (i == skip))
        def _():
            dmod_ref[...] = jnp.zeros_like(dmod_ref)

        d_h = _mm_nt(dg_ref[0], w_ref[:, 0:D_MODEL])
        for k in range(1, n_sec):
            d_h = d_h + _mm_nt(dg_ref[k], w_ref[:, k * D_MODEL:(k + 1) * D_MODEL])
        xv = jnp.where(i == 0, c_ref[...], x_ref[...]) if skip else x_ref[...]
        r = lax.rsqrt(jnp.mean(xv * xv, axis=-1, keepdims=True) + EPS)
        xh = xv * r
        nw_row = nw_ref[...]
        dmod_ref[0:1, :] += jnp.sum(d_h, axis=0, keepdims=True)
        dmod_ref[1:2, :] += jnp.sum(d_h * (xh * nw_row), axis=0, keepdims=True)
        d_xn = d_h * (1.0 + sc_ref[...])
        dnw_ref[...] += jnp.sum(d_xn * xh, axis=0, keepdims=True)
        d_xh = d_xn * nw_row

        @pl.when(i >= skip)
        def _():
            dx_ref[...] = up_ref[...] + r * (d_xh - xh * jnp.mean(d_xh * xh, axis=-1, keepdims=True))

    lat = lambda i: (jnp.maximum(i - skip, 0), 0)
    mod_idx = lambda i: (jnp.minimum(i, n_mod - 1), 0, 0)
    return pl.pallas_call(
        body, name=name, grid=(n_tiles,),
        out_shape=[pltpu.HBM((rows - skip * tm, D_MODEL), F32), jax.ShapeDtypeStruct((1, D_MODEL), F32),
                   jax.ShapeDtypeStruct((n_mod, 8, D_MODEL), F32)],
        in_specs=[pl.BlockSpec((n_sec, tm, D_MODEL), lambda i: (0, i, 0)),
                  pl.BlockSpec((D_MODEL, n_sec * D_MODEL), lambda i: (0, 0))]
        + [pl.BlockSpec((tm, D_MODEL), lambda i: (0, 0))] * skip
        + [pl.BlockSpec((tm, D_MODEL), lat),
           pl.BlockSpec((1, D_MODEL), lambda i: (0, 0)),
           pl.BlockSpec((None, 1, D_MODEL), mod_idx),
           pl.BlockSpec((tm, D_MODEL), lat)],
        out_specs=[pl.BlockSpec((tm, D_MODEL), lat), pl.BlockSpec((1, D_MODEL), lambda i: (0, 0)),
                   pl.BlockSpec((None, 8, D_MODEL), mod_idx)],
        compiler_params=_params(48),
    )(*_pin(d_g, w, *([ctx_tile] * skip), xin, nw, scale.reshape(n_mod, 1, D_MODEL), d_up))


def _weight_grad(h, d_g, name):
    n_sec, rows, _ = d_g.shape
    tm = 768 if rows % 768 == 0 else 512
    n_tiles = rows // tm

    def body(h_ref, dg_ref, dw_ref, acc):
        i = pl.program_id(1)
        prod = _mm_tn(h_ref[...], dg_ref[...])

        @pl.when(i == 0)
        def _():
            acc[...] = prod

        @pl.when((i > 0) & (i < n_tiles - 1))
        def _():
            acc[...] += prod

        @pl.when(i == n_tiles - 1)
        def _():
            dw_ref[...] = (acc[...] + prod).astype(BF16)

    return pl.pallas_call(
        body, name=name, grid=(n_sec, n_tiles),
        out_shape=pltpu.HBM((D_MODEL, n_sec * D_MODEL), BF16),
        in_specs=[pl.BlockSpec((tm, D_MODEL), lambda j, i: (i, 0)), pl.BlockSpec((None, tm, D_MODEL), lambda j, i: (j, i, 0))],
        out_specs=pl.BlockSpec((D_MODEL, D_MODEL), lambda j, i: (0, j)),
        scratch_shapes=[pltpu.VMEM((D_MODEL, D_MODEL), F32)],
        compiler_params=_params(32),
    )(*_pin(h, d_g))


def _weight_grad_paired(h, d_g, name):
    n_sec, rows, _ = d_g.shape
    tm = 768
    n_tiles = rows // tm
    half = D_MODEL // 2

    def body(h_ref, dg_ref, q_ref, acc, keep, send, land, send_sems, recv_sems):
        j, i = pl.program_id(0), pl.program_id(1)
        x, y, c = _my_place()
        prod = _mm_tn(h_ref[...], dg_ref[...])

        def to_sibling(k):
            return pltpu.make_async_remote_copy(src_ref=send.at[k], dst_ref=land.at[k], send_sem=send_sems.at[k],
                                                recv_sem=recv_sems.at[k], device_id=(x, y, 1 - c), device_id_type=MESH)

        @pl.when(i == 0)
        def _():
            acc[...] = prod

        @pl.when((i > 0) & (i < n_tiles - 1))
        def _():
            acc[...] += prod

        @pl.when(i == n_tiles - 1)
        def _():
            acc[...] += prod
            keep[j] = acc[pl.ds(_al(c * half, half), half), :]
            send[j] = acc[pl.ds(_al((1 - c) * half, half), half), :].astype(BF16)
            to_sibling(j).start()

        @pl.when((j == n_sec - 1) & (i == n_tiles - 1))
        def _():
            for k in range(n_sec):
                to_sibling(k).wait()
                q_ref[:, k * D_MODEL:(k + 1) * D_MODEL] = (keep[k] + land[k].astype(F32)).astype(BF16)

    return pl.pallas_call(
        body, name=name, grid=(n_sec, n_tiles),
        out_shape=jax.ShapeDtypeStruct((half, n_sec * D_MODEL), BF16),
        in_specs=[pl.BlockSpec((tm, D_MODEL), lambda j, i: (i, 0)), pl.BlockSpec((None, tm, D_MODEL), lambda j, i: (j, i, 0))],
        out_specs=pl.BlockSpec((half, n_sec * D_MODEL), lambda j, i: (0, 0)),
        scratch_shapes=[pltpu.VMEM((D_MODEL, D_MODEL), F32), pltpu.VMEM((n_sec, half, D_MODEL), F32),
                        pltpu.VMEM((n_sec, half, D_MODEL), BF16), pltpu.VMEM((n_sec, half, D_MODEL), BF16),
                        pltpu.SemaphoreType.DMA((n_sec,)), pltpu.SemaphoreType.DMA((n_sec,))],
        compiler_params=_params(56),
    )(*_pin(h, d_g))


def _sum_slots(slots, name):
    n_slots, rows, cols = slots.shape
    tr = 128

    def body(s_ref, o_ref):
        acc = s_ref[0].astype(F32)
        for d in range(1, n_slots):
            acc = acc + s_ref[d].astype(F32)
        o_ref[...] = acc

    return pl.pallas_call(
        body, name=name, grid=(rows // tr,),
        out_shape=jax.ShapeDtypeStruct((rows, cols), F32),
        in_specs=[pl.BlockSpec((n_slots, tr, cols), lambda i: (0, i, 0))],
        out_specs=pl.BlockSpec((tr, cols), lambda i: (i, 0)),
    )(*_pin(slots))


def _adamw_math(w, g, m, v):
    m = ADAM_B1 * m + (1.0 - ADAM_B1) * g
    v = ADAM_B2 * v + (1.0 - ADAM_B2) * (g * g)
    m_hat = m / (1.0 - ADAM_B1 ** ADAM_STEP)
    v_hat = v / (1.0 - ADAM_B2 ** ADAM_STEP)
    return -ADAM_LR * (m_hat / (jnp.sqrt(v_hat) + ADAM_EPS) + ADAM_WD * w), m, v


def _adamw(w, g, m, v, name):
    rows, cols = w.shape
    tr = rows if rows <= 128 else 128

    def body(w_ref, g_ref, m_ref, v_ref, d_ref, mo_ref, vo_ref):
        d_ref[...], mo_ref[...], vo_ref[...] = _adamw_math(w_ref[...], g_ref[...], m_ref[...], v_ref[...])

    tile = pl.BlockSpec((tr, cols), lambda i: (i, 0))
    return pl.pallas_call(
        body, name=name, grid=(rows // tr,),
        out_shape=[pltpu.HBM((rows, cols), F32)] * 3,
        in_specs=[tile] * 4, out_specs=[tile] * 3,
    )(*_pin(w, g, m, v))


def _sum_devices(gathered):
    def body(p_ref, o_ref):
        acc = p_ref[0:SMALL_ROWS, :]
        for d in range(1, N_DEV):
            acc = acc + p_ref[d * SMALL_ROWS:(d + 1) * SMALL_ROWS, :]
        o_ref[...] = acc

    return pl.pallas_call(body, name="sum_devices", out_shape=jax.ShapeDtypeStruct((SMALL_ROWS, D_MODEL), F32),
                          in_specs=[VMEM], out_specs=VMEM)(gathered)


def _ada_update(cond_t, d_mod, ada_w, m, v):
    n_layers, _, n_cols = ada_w.shape
    tr = ROW_TILE

    def body(c_ref, dm_ref, w_ref, m_ref, v_ref, g_ref, d_ref, mo_ref, vo_ref):
        g = _mm_f32(_silu(c_ref[...]), dm_ref[...])
        g_ref[...] = g
        d_ref[...], mo_ref[...], vo_ref[...] = _adamw_math(w_ref[...], g, m_ref[...], v_ref[...])

    tile = pl.BlockSpec((None, tr, n_cols), lambda l, i: (l, i, 0))
    return pl.pallas_call(
        body, name="ada_update", grid=(n_layers, D_MODEL // tr),
        out_shape=[pltpu.HBM(ada_w.shape, F32)] * 4,
        in_specs=[pl.BlockSpec((tr, 16), lambda l, i: (i, 0)), pl.BlockSpec((None, 16, n_cols), lambda l, i: (l, 0, 0)),
                  tile, tile, tile],
        out_specs=[tile] * 4,
    )(*_pin(cond_t, d_mod, ada_w, m, v))


def _cond_ctx_partial(d_modc, ada_w0):
    n_cols = ada_w0.shape[1]
    tr = ROW_TILE

    def body(dm_ref, w_ref, o_ref):
        o_ref[...] = jnp.sum(w_ref[...] * dm_ref[...], axis=-1, keepdims=True)

    return pl.pallas_call(
        body, name="cond_ctx_partial", grid=(D_MODEL // tr,),
        out_shape=jax.ShapeDtypeStruct((D_MODEL, 1), F32),
        in_specs=[pl.BlockSpec((1, n_cols), lambda i: (0, 0)), pl.BlockSpec((tr, n_cols), lambda i: (i, 0))],
        out_specs=pl.BlockSpec((tr, 1), lambda i: (i, 0)),
    )(d_modc, ada_w0)


def _cond_ctx_update(gathered, c_ctx, m, v):
    def body(p_ref, w_ref, m_ref, v_ref, g_ref, d_ref, mo_ref, vo_ref):
        acc = p_ref[0:1, :]
        for s in range(1, N_CHIPS):
            acc = acc + p_ref[16 * s:16 * s + 1, :]
        w = w_ref[...]
        g = acc * _dsilu(w)
        g_ref[...] = g
        d_ref[...], mo_ref[...], vo_ref[...] = _adamw_math(w, g, m_ref[...], v_ref[...])

    return pl.pallas_call(body, name="cond_ctx_update", out_shape=[jax.ShapeDtypeStruct((1, D_MODEL), F32)] * 4,
                          in_specs=[VMEM] * 4, out_specs=[VMEM] * 4)(gathered, c_ctx, m, v)


def _local_step(x2, ctx2, target, mod_mine, mod_ctx, lb_logits, scale_full, w_in_full, late_shards, norm_w, gnorm, final_w):
    row = lambda a: a.reshape(1, -1)
    shift0, scale0, gate0 = (row(a) for a in jnp.split(mod_mine[0], 3))
    shift1, scale1, gate1 = (row(a) for a in jnp.split(mod_mine[1], 3))
    shift_c, scale_c, _ = (row(a) for a in jnp.split(mod_ctx, 3))
    nw0, nw1 = norm_w[0:1], norm_w[1:2]
    scales0 = jnp.concatenate([scale_c, scale0])

    g5, h0 = _ln_mod_matmul(ctx2, x2, nw0, jnp.concatenate([shift_c, shift0]), scales0, w_in_full, "hgrn_in_forward")
    o_raw, full = _gla_forward(g5, lb_logits, late_shards)
    x1, res0 = _hgrn_out_forward(o_raw, g5, x2, gnorm, gate0, full["hgrn_w_out"])
    uz, h1 = _ln_mod_matmul(None, x1, nw1, shift1, scale1, full["pool_w_in"], "pool_in_forward")
    d_x2, loss_part, d_final, d_gate1 = _pool_forward_loss(uz, x1, target, gate1, full["pool_w_grp"], scale_full,
                                                           full["pool_w_out"], final_w)

    d_uz, dw_pool_out, dw_pool_grp, d_pscale = _pool_backward(d_x2, uz, gate1, full["pool_w_grp"], scale_full, full["pool_w_out"])
    d_x1, d_nw1, d_mod1 = _ln_mod_backward(d_uz, full["pool_w_in"], None, x1, nw1, scale1, d_x2, "pool_in_backward")
    dw_pool_in = _weight_grad(h1, d_uz, "pool_in_weight_grad")
    d_o, d_z, dw_hgrn_out, d_gate0, d_gnorm = _hgrn_out_backward(d_x1, o_raw, g5, res0, gnorm, gate0, full["hgrn_w_out"])
    late_grads = {"hgrn_w_out": dw_hgrn_out, "pool_w_in": dw_pool_in, "pool_w_grp": dw_pool_grp, "pool_w_out": dw_pool_out}
    d_g5, d_lb, late_slots = _gla_backward(g5, lb_logits, d_o, d_z, [late_grads[k].astype(BF16) for k in GATHER_LATE])
    dw_hgrn_in = _weight_grad_paired(h0, d_g5, "hgrn_in_weight_grad")
    key = GATHER_EARLY[0]
    half = _half_shape(key)
    xi, yi, _ = _my_place()
    own = lax.dynamic_slice(dw_hgrn_in, (0, (2 * xi + yi) * half[1]), half)
    slots0 = lax.dynamic_update_slice(jnp.zeros((N_CHIPS,) + half, BF16), own[None], (2 * xi + yi, 0, 0))
    send_sem, recv_sem, part_thru, slots_thru, token = _scatter_start(dw_hgrn_in, slots0, own, key)
    d_x, d_nw0, d_mod0 = _ln_mod_backward(d_g5, w_in_full, ctx2, x2, nw0 + token[0:1, 0:1], scales0, d_x1, "hgrn_in_backward")
    slots = dict(late_slots)
    pending = (send_sem, recv_sem, part_thru, slots_thru)

    zero = jnp.zeros((1, D_MODEL), F32)
    small = jnp.concatenate([d_mod0[1, 0:2], d_gate0, d_mod1[0, 0:2], d_gate1, d_mod0[0, 0:2], zero, d_nw0, d_nw1, d_gnorm,
                             d_final, d_pscale, d_lb, jnp.broadcast_to(loss_part[0:1, 0:1], (1, D_MODEL)),
                             jnp.zeros((SMALL_ROWS - 17, D_MODEL), F32)], axis=0)
    return {"d_x": d_x, "slots": slots, "pending": pending, "small": small}


def kernel(x, c, ctx, c_ctx, ada_w, ada_b, norm_w, hgrn_w_in, hgrn_lb_logits, hgrn_gnorm_w, hgrn_w_out, pool_w_in, pool_w_grp, pool_scale, pool_w_out, final_norm_w, loss_target, m_c_ctx, m_ada_w, m_ada_b, m_norm_w, m_hgrn_w_in, m_hgrn_lb_logits, m_hgrn_gnorm_w, m_hgrn_w_out, m_pool_w_in, m_pool_w_grp, m_pool_scale, m_pool_w_out, m_final_norm_w, v_c_ctx, v_ada_w, v_ada_b, v_norm_w, v_hgrn_w_in, v_hgrn_lb_logits, v_hgrn_gnorm_w, v_hgrn_w_out, v_pool_w_in, v_pool_w_grp, v_pool_scale, v_pool_w_out, v_final_norm_w):
    xi, yi, ci = _my_place()
    chip = 2 * xi + yi
    dev = 4 * xi + 2 * yi + ci
    ada_cols = ada_w.shape[2]
    lb_cols = hgrn_lb_logits.shape[2]
    ps_cols = pool_scale.shape[1]
    row = lambda a: a.reshape(1, -1)

    def chip_cols(a, n):
        return lax.dynamic_slice_in_dim(a, chip * n, n, axis=a.ndim - 1)

    def from_chips(g, rows_per_dev, take):
        return jnp.concatenate([g[2 * s * rows_per_dev:2 * s * rows_per_dev + take] for s in range(N_CHIPS)], axis=1)

    first = jnp.concatenate([jnp.broadcast_to(c, (8, D_MODEL)), jnp.pad(hgrn_lb_logits[0], ((0, 6), (0, 0))),
                             jnp.pad(pool_scale, ((0, 7), (0, 0)))], axis=1)
    first_all = _all_gather_small(first, "gather_cond")
    cond_all = first_all[::8, :D_MODEL]
    lb_logits = from_chips(first_all[:, D_MODEL:D_MODEL + lb_cols], 8, 2)
    scale_full = from_chips(first_all[:, D_MODEL + lb_cols:], 8, 1)
    cond_rows = jnp.concatenate([cond_all, row(c_ctx), jnp.zeros((7, D_MODEL), F32)], axis=0)

    parts = _mod_parts(cond_rows, ada_w, chip_cols(ada_b, ada_cols))
    parts_all = _all_gather_small(parts.reshape(32, ada_cols), "gather_mod")
    mod_all = from_chips(parts_all, 32, 32).reshape(2, 16, 3 * D_MODEL)
    mod_mine = lax.dynamic_index_in_dim(mod_all, dev, axis=1, keepdims=False)

    shards = {"hgrn_w_in": hgrn_w_in[0], "hgrn_w_out": hgrn_w_out[0], "pool_w_in": pool_w_in[0],
              "pool_w_grp": pool_w_grp[0], "pool_w_out": pool_w_out[0]}
    w_in_full = _all_gather_weights([shards[k].astype(BF16) for k in GATHER_EARLY], GATHER_EARLY)[0]

    loc = _local_step(x[0], ctx[0], loss_target[0], mod_mine, mod_all[0, 8], lb_logits, scale_full, w_in_full,
                      [shards[k].astype(BF16) for k in GATHER_LATE], norm_w, hgrn_gnorm_w, row(final_norm_w))

    small_all = _all_gather_small(loc["small"], "gather_small")
    sums = _sum_devices(small_all)
    loss = sums[16, 0]

    part_c = _cond_ctx_partial(chip_cols(sums[6:9].reshape(1, -1), ada_cols), ada_w[0])
    part_all = _all_gather_small(jnp.pad(part_c.reshape(1, D_MODEL), ((0, 7), (0, 0))), "gather_cond_ctx")

    def reduce_scattered(slots, names, name):
        halves = []
        for k in names:
            hs = _half_shape(k)
            halves.append(_sum_slots(slots[k].reshape(slots[k].shape[0], -1, hs[-1]), "sum_" + k).reshape(hs))
        return dict(zip(names, _exchange_halves(halves, names, name)))

    big_grads = reduce_scattered(loc["slots"], GATHER_LATE, "exchange_halves_late")

    out = {}

    def update(name, w, g, m, v):
        shape = w.shape
        w2, g2, m2, v2 = (a.reshape(-1, shape[-1]) for a in (w, g, m, v))
        d, mn, vn = _adamw(w2, g2, m2, v2, "adamw_" + name)
        out[name] = tuple(a.reshape(shape) for a in (g2, d, mn, vn))

    moments = {"hgrn_w_in": (m_hgrn_w_in, v_hgrn_w_in), "hgrn_w_out": (m_hgrn_w_out, v_hgrn_w_out),
               "pool_w_in": (m_pool_w_in, v_pool_w_in), "pool_w_grp": (m_pool_w_grp, v_pool_w_grp),
               "pool_w_out": (m_pool_w_out, v_pool_w_out)}
    weights = {"hgrn_w_in": hgrn_w_in, "hgrn_w_out": hgrn_w_out, "pool_w_in": pool_w_in, "pool_w_grp": pool_w_grp,
               "pool_w_out": pool_w_out}
    for k in GATHER_LATE:
        update(k, weights[k], big_grads[k], *moments[k])

    g_ada_b = jnp.stack([(sums[0:3] + sums[6:9]).reshape(-1), sums[3:6].reshape(-1)])
    update("ada_b", ada_b, g_ada_b, m_ada_b, v_ada_b)
    update("norm_w", norm_w, sums[9:11], m_norm_w, v_norm_w)
    update("hgrn_gnorm_w", hgrn_gnorm_w, sums[11:12], m_hgrn_gnorm_w, v_hgrn_gnorm_w)
    update("final_norm_w", row(final_norm_w), sums[12:13], row(m_final_norm_w), row(v_final_norm_w))
    update("pool_scale", pool_scale, chip_cols(sums[13:14], ps_cols), m_pool_scale, v_pool_scale)
    update("hgrn_lb_logits", hgrn_lb_logits, chip_cols(sums[14:16], lb_cols)[None], m_hgrn_lb_logits, v_hgrn_lb_logits)

    per_dev = small_all.reshape(N_DEV, SMALL_ROWS, D_MODEL)
    pad7 = jnp.zeros((7, 3 * D_MODEL), F32)
    dm0 = jnp.concatenate([per_dev[:, 0:3].reshape(N_DEV, -1), sums[6:9].reshape(1, -1), pad7], axis=0)
    dm1 = jnp.concatenate([per_dev[:, 3:6].reshape(N_DEV, -1), jnp.zeros((8, 3 * D_MODEL), F32)], axis=0)
    d_mod = chip_cols(jnp.stack([dm0, dm1]), ada_cols)
    out["ada_w"] = _ada_update(cond_rows.T, d_mod, ada_w, m_ada_w, v_ada_w)

    g_c, d_c, m_c, v_c = _cond_ctx_update(part_all, row(c_ctx), row(m_c_ctx), row(v_c_ctx))
    out["c_ctx"] = tuple(a.reshape(-1) for a in (g_c, d_c, m_c, v_c))
    out["final_norm_w"] = tuple(a.reshape(-1) for a in out["final_norm_w"])

    done = [g_c, out["ada_w"][1]] + [out[k][1] for k in GATHER_LATE]
    key = GATHER_EARLY[0]
    _, early = _scatter_wait(*loc["pending"], done, key)
    big_grads = reduce_scattered({key: early}, GATHER_EARLY, "exchange_halves_early")
    for k in GATHER_EARLY:
        update(k, weights[k], big_grads[k], *moments[k])

    names = ["c_ctx", "ada_w", "ada_b", "norm_w", "hgrn_w_in", "hgrn_lb_logits", "hgrn_gnorm_w", "hgrn_w_out", "pool_w_in",
             "pool_w_grp", "pool_scale", "pool_w_out", "final_norm_w"]
    return (loss, loc["d_x"][None], *[out[k][0] for k in names], *[out[k][1] for k in names], *[out[k][2] for k in names],
            *[out[k][3] for k in names])
```

```python
import functools

import numpy as np
import jax
import jax.numpy as jnp
from jax import lax
from jax.experimental import pallas as pl
from jax.experimental.pallas import tpu as pltpu

F32 = jnp.float32
BF16 = jnp.bfloat16

D_MODEL = 1024
SEQ = 2048
CTX_LEN = 256
ROWS_ALL = CTX_LEN + SEQ
HEADS = 8
HEAD_DIM = 128
CHUNK = 64
N_CTX_CHUNKS = CTX_LEN // CHUNK
N_LAT_CHUNKS = SEQ // CHUNK
N_CHUNKS = N_CTX_CHUNKS + N_LAT_CHUNKS
GRID_W = 64
POOL_WINDOWS = (2, 4, 8, 16)
POOL_GROUPS = 4
POOL_GROUP_DIM = 256
HGRN_SECTIONS = 5
POOL_SECTIONS = 2
EPS = 1e-6
N_DEV = 8
N_CHIPS = 4
ROW_TILE = 256
SMALL_ROWS = 24

ADAM_LR = 0.001
ADAM_B1 = 0.9
ADAM_B2 = 0.999
ADAM_EPS = 1e-08
ADAM_WD = 0.01
ADAM_STEP = 10

MESH = pl.DeviceIdType.MESH
MIB = 1 << 20
ANY = pl.BlockSpec(memory_space=pl.ANY)
VMEM = pl.BlockSpec(memory_space=pltpu.VMEM)


def _params(vmem_mib=None):
    if vmem_mib is None:
        return pltpu.CompilerParams()
    return pltpu.CompilerParams(vmem_limit_bytes=vmem_mib * MIB)


def _pin(*operands):
    return [pltpu.with_memory_space_constraint(a, pltpu.HBM) if a.size * a.dtype.itemsize >= MIB else a for a in operands]


def _sig(a):
    return 1.0 / (1.0 + jnp.exp(-a))


def _silu(a):
    return a * _sig(a)


def _dsilu(a):
    s = _sig(a)
    return s * (1.0 + a * (1.0 - s))


def _mm(a, b):
    return jnp.dot(a.astype(BF16), b.astype(BF16), preferred_element_type=F32)


def _mm_nt(a, b):
    return lax.dot_general(a.astype(BF16), b.astype(BF16), (((1,), (1,)), ((), ())), preferred_element_type=F32)


def _mm_tn(a, b):
    return lax.dot_general(a.astype(BF16), b.astype(BF16), (((0,), (0,)), ((), ())), preferred_element_type=F32)


def _split2(a):
    hi = a.astype(BF16)
    lo = (a - hi.astype(F32)).astype(BF16)
    return hi, lo


def _mm_exact_lhs(m_bf, a):
    hi, lo = _split2(a)
    return jnp.dot(m_bf, hi, preferred_element_type=F32) + jnp.dot(m_bf, lo, preferred_element_type=F32)


def _mm_f32(a, b):
    ah, al = _split2(a)
    bh, bl = _split2(b)
    return (jnp.dot(ah, bh, preferred_element_type=F32) + jnp.dot(al, bh, preferred_element_type=F32)
            + jnp.dot(ah, bl, preferred_element_type=F32))


def _my_place():
    return lax.axis_index("x"), lax.axis_index("y"), lax.axis_index("c")


def _all_gather_small(blk, name):
    m_per, n = blk.shape

    def body(x_ref, out_ref, send_sems, recv_sems, local_sem):
        x, y, c = _my_place()
        me, sibling = (x, y, c), (x, y, 1 - c)
        chips = [(1 - x, y), (x, 1 - y), (1 - x, 1 - y)]

        def rows(px, py, pc):
            return out_ref.at[pl.ds((4 * px + 2 * py + pc) * m_per, m_per), :]

        def copy(k, block, to, src=None):
            return pltpu.make_async_remote_copy(
                src_ref=rows(*block) if src is None else src, dst_ref=rows(*block),
                send_sem=send_sems.at[k], recv_sem=recv_sems.at[k], device_id=to, device_id_type=MESH)

        mine = pltpu.make_async_copy(x_ref, rows(*me), local_sem)
        mine.start()
        first = [copy(0, me, sibling, src=x_ref)]
        first += [copy(1 + j, me, (*chip, c), src=x_ref) for j, chip in enumerate(chips)]
        for cp in first:
            cp.start()
        passed = [copy(4 + j, (*chip, c), sibling) for j, chip in enumerate(chips)]
        for j, chip in enumerate(chips):
            copy(1 + j, (*chip, c), me).wait_recv()
            passed[j].start()
        copy(0, sibling, me).wait_recv()
        for j, chip in enumerate(chips):
            copy(4 + j, (*chip, 1 - c), me).wait_recv()
        for cp in first + passed:
            cp.wait_send()
        mine.wait()

    return pl.pallas_call(
        body, name=name,
        out_shape=jax.ShapeDtypeStruct((N_DEV * m_per, n), blk.dtype),
        in_specs=[VMEM], out_specs=VMEM,
        scratch_shapes=[pltpu.SemaphoreType.DMA((7,)), pltpu.SemaphoreType.DMA((7,)), pltpu.SemaphoreType.DMA],
    )(blk)


W_SPECS = {
    "hgrn_w_in": ((D_MODEL, 5 * D_MODEL), (1, 1280, 0, 512)),
    "hgrn_w_out": ((D_MODEL, D_MODEL), (0, 256, 0, 128)),
    "pool_w_in": ((D_MODEL, 2 * D_MODEL), (1, 512, 0, 512)),
    "pool_w_grp": ((POOL_GROUPS, POOL_GROUP_DIM, POOL_GROUP_DIM), (1, 64, 1, 32)),
    "pool_w_out": ((D_MODEL, D_MODEL), (0, 256, 0, 128)),
}
W_NAMES = tuple(W_SPECS)


def _al(v, m):
    return pl.multiple_of(v, m)


def _region(ref, spec, chip, half):
    ca, cn, ha, hn = spec
    idx = [slice(None)] * len(ref.shape)
    if ca == ha:
        if half is None:
            idx[ca] = pl.ds(_al(chip * cn, cn), cn)
        else:
            idx[ca] = pl.ds(_al(chip * cn + half * hn, hn), hn)
    else:
        idx[ca] = pl.ds(_al(chip * cn, cn), cn)
        if half is not None:
            idx[ha] = pl.ds(_al(half * hn, hn), hn)
    return ref.at[tuple(idx)]


def _half_of(ref, spec, half):
    _, _, ha, hn = spec
    idx = [slice(None)] * len(ref.shape)
    idx[ha] = pl.ds(_al(half * hn, hn), hn)
    return ref.at[tuple(idx)]


PIECE_BYTES = 256 * 1024


def _pieces(ref):
    lead = ref.shape[0]
    want = (int(np.prod(ref.shape)) * ref.dtype.itemsize) // PIECE_BYTES
    n = max([1] + [k for k in range(1, want + 1) if lead % k == 0 and (lead // k) % 16 == 0])
    rows = lead // n
    return [ref.at[pl.ds(i * rows, rows)] for i in range(n)]


def _half_shape(name):
    full, (ca, cn, ha, hn) = W_SPECS[name]
    shp = list(full)
    shp[ca] = cn
    shp[ha] = hn
    return tuple(shp)


def _gather_direct(names, sh, full, send_sems, recv_sems, local_sems):
    specs = [W_SPECS[k][1] for k in names]
    x, y, c = _my_place()
    chip_me = 2 * x + y
    chips = [(1 - x, y), (x, 1 - y), (1 - x, 1 - y)]

    def local(a):
        return pltpu.make_async_copy(sh[a], _region(full[a], specs[a], chip_me, None), local_sems.at[a])

    def remote(a, src, dst, to):
        return pltpu.make_async_remote_copy(src_ref=src, dst_ref=dst, send_sem=send_sems.at[a], recv_sem=recv_sems.at[a],
                                            device_id=to, device_id_type=MESH)

    def start():
        for a in range(len(names)):
            local(a).start()
            for px, py in chips:
                remote(a, sh[a], _region(full[a], specs[a], chip_me, None), (px, py, c)).start()

    def wait():
        for a in range(len(names)):
            ca, cn, _, _ = specs[a]
            idx = [slice(None)] * len(full[a].shape)
            idx[ca] = pl.ds(0, 3 * cn)
            three = full[a].at[tuple(idx)]
            remote(a, three, three, (x, y, c)).wait()
            local(a).wait()

    return start, wait


def _gather_two_level(names, sh, land, full, send_sems, recv_sems, local_sems):
    n = len(names)
    specs = [W_SPECS[k][1] for k in names]
    x, y, c = _my_place()
    chip_me = 2 * x + y
    sibling = (x, y, 1 - c)
    chips = [(1 - x, y), (x, 1 - y), (1 - x, 1 - y)]

    def remote(k, src, dst, to):
        return pltpu.make_async_remote_copy(src_ref=src, dst_ref=dst, send_sem=send_sems.at[k], recv_sem=recv_sems.at[k],
                                            device_id=to, device_id_type=MESH)

    def three_halves(a):
        ca, cn, ha, hn = specs[a]
        idx = [slice(None)] * len(land[a].shape)
        if ca == ha:
            idx[ca] = pl.ds(0, 3 * hn)
        else:
            idx[ca], idx[ha] = pl.ds(0, 3 * cn), pl.ds(0, hn)
        return land[a].at[tuple(idx)]

    def own(a):
        return pltpu.make_async_copy(sh[a], _region(land[a], specs[a], chip_me, None), local_sems.at[a])

    def out(a):
        return pltpu.make_async_copy(land[a], full[a], local_sems.at[a])

    def start():
        for a in range(n):
            own(a).start()
            for px, py in chips:
                remote(a, _half_of(sh[a], specs[a], c), _region(land[a], specs[a], chip_me, c), (px, py, c)).start()

    def forward():
        for a in range(n):
            remote(a, three_halves(a), three_halves(a), sibling).wait_recv()
            for px, py in chips:
                landed = _region(land[a], specs[a], 2 * px + py, c)
                remote(n + a, landed, landed, sibling).start()

    def finish():
        for a in range(n):
            remote(n + a, three_halves(a), three_halves(a), sibling).wait_recv()
            remote(a, three_halves(a), three_halves(a), sibling).wait_send()
            remote(n + a, three_halves(a), three_halves(a), sibling).wait_send()
            own(a).wait()
        for a in range(n):
            out(a).start()
        for a in range(n):
            out(a).wait()

    return start, forward, finish


def _scatter_direct(names, part, slots, send_sems, recv_sems, local_sems):
    specs = [W_SPECS[k][1] for k in names]
    x, y, c = _my_place()
    dev_me = 4 * x + 2 * y + c

    def local(a):
        return pltpu.make_async_copy(_region(part[a], specs[a], 2 * x + y, c), slots[a].at[dev_me], local_sems.at[a])

    def start():
        for a in range(len(names)):
            local(a).start()
            for flip in range(1, N_DEV):
                tx = 1 - x if flip >> 2 else x
                ty = 1 - y if (flip >> 1) & 1 else y
                tc = 1 - c if flip & 1 else c
                pltpu.make_async_remote_copy(src_ref=_region(part[a], specs[a], 2 * tx + ty, tc), dst_ref=slots[a].at[dev_me],
                                             send_sem=send_sems.at[a], recv_sem=recv_sems.at[a], device_id=(tx, ty, tc),
                                             device_id_type=MESH).start()

    def wait():
        for a in range(len(names)):
            seven = slots[a].at[pl.ds(0, N_DEV - 1)]
            pltpu.make_async_remote_copy(src_ref=seven, dst_ref=seven, send_sem=send_sems.at[a], recv_sem=recv_sems.at[a],
                                         device_id=(x, y, c), device_id_type=MESH).wait()
            local(a).wait()

    return start, wait


HBM_SPEC = pl.BlockSpec(memory_space=pltpu.HBM)
SEM_SPEC = pl.BlockSpec(memory_space=pltpu.SEMAPHORE)
SPLIT_EFFECT = pltpu.SideEffectType.DATAFLOW_SIDE_EFFECTING


def _scatter_start(part, slots, after, name_key):
    _, cols, _, _ = W_SPECS[name_key][1]

    def body(part_ref, slots_ref, after_ref, send_sem, recv_sem, part_thru, slots_thru, token):
        x, y, c = _my_place()
        for tx, ty in ((1 - x, y), (x, 1 - y), (1 - x, 1 - y)):
            src = part_ref.at[:, pl.ds(_al((2 * tx + ty) * cols, cols), cols)]
            pltpu.make_async_remote_copy(src_ref=src, dst_ref=slots_ref.at[2 * x + y], send_sem=send_sem, recv_sem=recv_sem,
                                         device_id=(tx, ty, c), device_id_type=MESH).start()
        token[...] = jnp.zeros_like(token)

    return pl.pallas_call(
        body, name="scatter_start_" + name_key,
        out_shape=(pltpu.SemaphoreType.DMA(()), pltpu.SemaphoreType.DMA(()), pltpu.HBM(part.shape, part.dtype),
                   pltpu.HBM(slots.shape, slots.dtype), jax.ShapeDtypeStruct((8, 128), F32)),
        in_specs=(HBM_SPEC, HBM_SPEC, ANY), out_specs=(SEM_SPEC, SEM_SPEC, HBM_SPEC, HBM_SPEC, VMEM),
        input_output_aliases={0: 2, 1: 3},
        compiler_params=pltpu.CompilerParams(has_side_effects=SPLIT_EFFECT),
    )(pltpu.with_memory_space_constraint(part, pltpu.HBM), pltpu.with_memory_space_constraint(slots, pltpu.HBM), after)


def _scatter_wait(send_sem, recv_sem, part_thru, slots_thru, after, name_key):
    def body(part_ref, slots_ref, send_sem, recv_sem, *rest):
        x, y, c = _my_place()
        landed = slots_ref.at[pl.ds(0, N_CHIPS - 1)]
        copy = pltpu.make_async_remote_copy(src_ref=landed, dst_ref=landed, send_sem=send_sem, recv_sem=recv_sem,
                                            device_id=(x, y, c), device_id_type=MESH)
        copy.wait_send()
        copy.wait_recv()

    return pl.pallas_call(
        body, name="scatter_wait_" + name_key,
        out_shape=(pltpu.HBM(part_thru.shape, part_thru.dtype), pltpu.HBM(slots_thru.shape, slots_thru.dtype)),
        in_specs=(HBM_SPEC, HBM_SPEC, SEM_SPEC, SEM_SPEC) + (ANY,) * len(after), out_specs=(HBM_SPEC, HBM_SPEC),
        input_output_aliases={0: 0, 1: 1},
        compiler_params=pltpu.CompilerParams(has_side_effects=SPLIT_EFFECT),
    )(part_thru, slots_thru, send_sem, recv_sem, *after)


def _comm_sems(n):
    return [pltpu.SemaphoreType.DMA((n,)), pltpu.SemaphoreType.DMA((n,)), pltpu.SemaphoreType.DMA((n,))]


GATHER_EARLY = ("hgrn_w_in",)
GATHER_LATE = ("hgrn_w_out", "pool_w_in", "pool_w_grp", "pool_w_out")


def _all_gather_weights(shards, names):
    n = len(names)
    specs = [W_SPECS[k][1] for k in names]

    def body(*refs):
        sh, full = refs[:n], refs[n:2 * n]
        send_sems, recv_sems, local_sems = refs[2 * n:]
        x, y, c = _my_place()
        chip_me = 2 * x + y
        sibling = (x, y, 1 - c)
        chips = [(1 - x, y), (x, 1 - y), (1 - x, 1 - y)]

        def remote(a, k, src, dst, to):
            return pltpu.make_async_remote_copy(src_ref=src, dst_ref=dst, send_sem=send_sems.at[6 * a + k],
                                                recv_sem=recv_sems.at[6 * a + k], device_id=to, device_id_type=MESH)

        local = [pltpu.make_async_copy(sh[a], _region(full[a], specs[a], chip_me, None), local_sems.at[a]) for a in range(n)]
        for cp in local:
            cp.start()
        sends = []
        for a in range(n):
            for j, (px, py) in enumerate(chips):
                src, dst = _half_of(sh[a], specs[a], c), _region(full[a], specs[a], chip_me, c)
                for s_piece, d_piece in zip(_pieces(src), _pieces(dst)):
                    remote(a, j, s_piece, d_piece, (px, py, c)).start()
                sends.append(remote(a, j, src, dst, (px, py, c)))
        for a in range(n):
            for j, (px, py) in enumerate(chips):
                landed = _region(full[a], specs[a], 2 * px + py, c)
                remote(a, j, landed, landed, (px, py, c)).wait_recv()
                for piece in _pieces(landed):
                    remote(a, 3 + j, piece, piece, sibling).start()
                sends.append(remote(a, 3 + j, landed, landed, sibling))
        for a in range(n):
            for j, (px, py) in enumerate(chips):
                other = _region(full[a], specs[a], 2 * px + py, 1 - c)
                remote(a, 3 + j, other, other, sibling).wait_recv()
        for cp in sends:
            cp.wait_send()
        for cp in local:
            cp.wait()

    return pl.pallas_call(
        body, name="all_gather_weights",
        out_shape=[jax.ShapeDtypeStruct(W_SPECS[k][0], BF16) for k in names],
        in_specs=[VMEM] * n, out_specs=[VMEM] * n,
        scratch_shapes=[pltpu.SemaphoreType.DMA((6 * n,)), pltpu.SemaphoreType.DMA((6 * n,)), pltpu.SemaphoreType.DMA((n,))],
        compiler_params=_params(32),
    )(*shards)


def _slot_shapes(names):
    return [jax.ShapeDtypeStruct((N_DEV,) + _half_shape(k), BF16) for k in names]


def _scatter_grads(parts, names):
    n = len(names)

    def body(*refs):
        start, wait = _scatter_direct(names, refs[:n], refs[n:2 * n], *refs[2 * n:])
        start()
        wait()

    return pl.pallas_call(body, name="scatter_grads", out_shape=_slot_shapes(names), in_specs=[ANY] * n, out_specs=[ANY] * n,
                          scratch_shapes=_comm_sems(n))(*parts)


def _exchange_halves(halves, names, name):
    n = len(names)
    specs = [W_SPECS[k][1] for k in names]

    def shard_shape(k):
        shp = list(_half_shape(k))
        shp[W_SPECS[k][1][2]] *= 2
        return tuple(shp)

    def body(*refs):
        half, out = refs[:n], refs[n:2 * n]
        send_sems, recv_sems, local_sems = refs[2 * n:]
        x, y, c = _my_place()
        sibling = (x, y, 1 - c)

        def remote(a, src, dst):
            return pltpu.make_async_remote_copy(src_ref=src, dst_ref=dst, send_sem=send_sems.at[a], recv_sem=recv_sems.at[a],
                                                device_id=sibling, device_id_type=MESH)

        local = [pltpu.make_async_copy(half[a], _half_of(out[a], specs[a], c), local_sems.at[a]) for a in range(n)]
        for cp in local:
            cp.start()
        for a in range(n):
            mine = _half_of(out[a], specs[a], c)
            for src, dst in zip(_pieces(half[a]), _pieces(mine)):
                remote(a, src, dst).start()
        for a in range(n):
            theirs = _half_of(out[a], specs[a], 1 - c)
            remote(a, theirs, theirs).wait_recv()
        for a in range(n):
            remote(a, half[a], half[a]).wait_send()
        for cp in local:
            cp.wait()

    return pl.pallas_call(
        body, name=name,
        out_shape=[jax.ShapeDtypeStruct(shard_shape(k), F32) for k in names],
        in_specs=[VMEM] * n, out_specs=[VMEM] * n,
        scratch_shapes=[pltpu.SemaphoreType.DMA((n,)), pltpu.SemaphoreType.DMA((n,)), pltpu.SemaphoreType.DMA((n,))],
        compiler_params=_params(32),
    )(*halves)


def _mod_parts(c_rows, ada_w, ada_b_cols):
    n_layers, _, n_cols = ada_w.shape

    def body(c_ref, w_ref, b_ref, o_ref):
        o_ref[...] = _mm_f32(_silu(c_ref[...]), w_ref[...]) + b_ref[...]

    return pl.pallas_call(
        body, name="mod_parts", grid=(n_layers,),
        out_shape=jax.ShapeDtypeStruct((n_layers, 16, n_cols), F32),
        in_specs=[pl.BlockSpec((16, D_MODEL), lambda i: (0, 0)),
                  pl.BlockSpec((None, D_MODEL, n_cols), lambda i: (i, 0, 0)),
                  pl.BlockSpec((None, 1, n_cols), lambda i: (i, 0, 0))],
        out_specs=pl.BlockSpec((None, 16, n_cols), lambda i: (i, 0, 0)),
        compiler_params=_params(40),
    )(c_rows, ada_w, ada_b_cols.reshape(n_layers, 1, n_cols))


def _ln_mod_matmul(ctx_tile, xin, nw, shift, scale, w, name):
    n_mod = shift.shape[0]
    tm = ROW_TILE
    skip = n_mod - 1
    rows = xin.shape[0] + skip * tm
    n_sec = w.shape[1] // D_MODEL

    def body(*refs):
        c_ref = refs[0] if skip else None
        x_ref, nw_ref, sh_ref, sc_ref, w_ref, g_ref, h_ref = refs[skip:]
        xv = jnp.where(pl.program_id(0) == 0, c_ref[...], x_ref[...]) if skip else x_ref[...]
        r = lax.rsqrt(jnp.mean(xv * xv, axis=-1, keepdims=True) + EPS)
        h_f32 = (xv * r * nw_ref[...]) * (1.0 + sc_ref[...]) + sh_ref[...]
        h = h_f32.astype(BF16)
        h_ref[...] = h
        for k in range(n_sec):
            g_ref[k] = jnp.dot(h, w_ref[:, k * D_MODEL:(k + 1) * D_MODEL], preferred_element_type=F32)

    mod_spec = pl.BlockSpec((None, 1, D_MODEL), lambda i: (jnp.minimum(i, n_mod - 1), 0, 0))
    return pl.pallas_call(
        body, name=name, grid=(rows // tm,),
        out_shape=[pltpu.HBM((n_sec, rows, D_MODEL), F32), pltpu.HBM((rows, D_MODEL), BF16)],
        in_specs=[pl.BlockSpec((tm, D_MODEL), lambda i: (0, 0))] * skip
        + [pl.BlockSpec((tm, D_MODEL), lambda i: (jnp.maximum(i - skip, 0), 0)),
           pl.BlockSpec((1, D_MODEL), lambda i: (0, 0)),
           mod_spec, mod_spec,
           pl.BlockSpec((D_MODEL, n_sec * D_MODEL), lambda i: (0, 0))],
        out_specs=[pl.BlockSpec((n_sec, tm, D_MODEL), lambda i: (0, i, 0)),
                   pl.BlockSpec((tm, D_MODEL), lambda i: (i, 0))],
        compiler_params=_params(48),
    )(*_pin(*([ctx_tile] * skip), xin, nw, shift.reshape(n_mod, 1, D_MODEL), scale.reshape(n_mod, 1, D_MODEL), w))


def _chunk_masks(rev):
    rid = lax.broadcasted_iota(jnp.int32, (CHUNK, CHUNK), 0)
    cid = lax.broadcasted_iota(jnp.int32, (CHUNK, CHUNK), 1)
    keep = (cid >= rid) if rev else (cid <= rid)
    keep_t = (cid <= rid) if rev else (cid >= rid)
    one, zero = jnp.ones((CHUNK, CHUNK), F32), jnp.zeros((CHUNK, CHUNK), F32)
    return keep, jnp.where(keep, one, zero).astype(BF16), jnp.where(keep_t, one, zero).astype(BF16)


def _chunk_rows(t, rev, latent):
    n = N_LAT_CHUNKS if latent else N_CTX_CHUNKS
    base = CTX_LEN if latent else 0
    idx = (n - 1 - t) if rev else t
    return pl.multiple_of(base + idx * CHUNK, CHUNK)


def _gates(fpre, lb):
    sg = _sig(fpre)
    f = lb + (1.0 - lb) * sg
    return sg, f, 1.0 - f, jnp.log(f)


G_SPEC = lambda sec: pl.BlockSpec((None, ROWS_ALL, HEAD_DIM), lambda h, sec=sec: (sec, 0, h))


def _gla_forward(g5, lb_logits):
    q_scale = HEAD_DIM ** -0.5

    def body(ff_ref, fb_ref, v_ref, q_ref, lg_ref, o_ref, st_ref, decay_ref, qt_ref):
        for rev in (False, True):
            f_ref = fb_ref if rev else ff_ref
            lb = _sig(lg_ref[1:2, :] if rev else lg_ref[0:1, :])
            keep, tri, _ = _chunk_masks(rev)
            last = 0 if rev else CHUNK - 1
            mid = CHUNK // 2 if rev else CHUNK // 2 - 1

            def local_step(t, carry, latent):
                r0 = _chunk_rows(t, rev, latent)
                rows = pl.ds(r0, CHUNK)
                step = t + (N_CTX_CHUNKS if latent else 0)
                _, _, k, lf = _gates(f_ref[rows, :], lb)
                v = v_ref[rows, :]
                b = _mm_exact_lhs(tri, lf)
                bl = b[last:last + 1, :]
                if latent:
                    q = _silu(q_ref[rows, :]) * q_scale
                    bm = b[mid:mid + 1, :]
                    a = _mm_nt(q * jnp.exp(b - bm), k * jnp.exp(bm - b))
                    o = _mm(jnp.where(keep, a, 0.0), v)
                    orow = pl.ds(pl.multiple_of(r0 - CTX_LEN, CHUNK), CHUNK)
                    qt_ref[orow, :] = (q * jnp.exp(b)).astype(BF16)
                    if rev:
                        o_ref[orow, :] += o
                    else:
                        o_ref[orow, :] = o
                decay_ref[step] = jnp.exp(bl)
                st_ref[step] = _mm_tn(v, k * jnp.exp(bl - b))
                return carry

            lax.fori_loop(0, N_CTX_CHUNKS, functools.partial(local_step, latent=False), 0, unroll=2)
            lax.fori_loop(0, N_LAT_CHUNKS, functools.partial(local_step, latent=True), 0, unroll=4)

            def scan_step(t, st):
                update = st_ref[t]
                st_ref[t] = st
                return st * decay_ref[t] + update

            lax.fori_loop(0, N_CHUNKS, scan_step, jnp.zeros((HEAD_DIM, HEAD_DIM), F32), unroll=2)

            def inter_step(t, carry):
                r0 = _chunk_rows(t, rev, True)
                orow = pl.ds(pl.multiple_of(r0 - CTX_LEN, CHUNK), CHUNK)
                o_ref[orow, :] += lax.dot_general(qt_ref[orow, :], st_ref[t + N_CTX_CHUNKS].astype(BF16),
                                                  (((1,), (1,)), ((), ())), preferred_element_type=F32)
                return carry

            lax.fori_loop(0, N_LAT_CHUNKS, inter_step, 0, unroll=4)

    return pl.pallas_call(
        body, name="gla_forward", grid=(HEADS,),
        out_shape=jax.ShapeDtypeStruct((SEQ, D_MODEL), F32),
        in_specs=[G_SPEC(0), G_SPEC(1), G_SPEC(2), G_SPEC(3), pl.BlockSpec((2, HEAD_DIM), lambda h: (0, h))],
        out_specs=pl.BlockSpec((SEQ, HEAD_DIM), lambda h: (0, h)),
        scratch_shapes=[pltpu.VMEM((N_CHUNKS, HEAD_DIM, HEAD_DIM), F32), pltpu.VMEM((N_CHUNKS, 1, HEAD_DIM), F32),
                        pltpu.VMEM((SEQ, HEAD_DIM), BF16)],
        compiler_params=_params(32),
    )(g5, g5, g5, g5, lb_logits)


def _gla_backward(g5, lb_logits, d_o, d_z):
    q_scale = HEAD_DIM ** -0.5

    def body(ff_ref, fb_ref, v_ref, q_ref, lg_ref, do_ref, dz_ref, dg_ref, dlg_ref, st_ref, dst_ref, decay_ref):
        dg_ref[3, 0:CTX_LEN, :] = jnp.zeros((CTX_LEN, HEAD_DIM), F32)
        dg_ref[4, 0:CTX_LEN, :] = jnp.zeros((CTX_LEN, HEAD_DIM), F32)
        dg_ref[4, CTX_LEN:ROWS_ALL, :] = dz_ref[...]
        is_row = lax.broadcasted_iota(jnp.int32, (CHUNK, HEAD_DIM), 0)
        for rev in (False, True):
            d = 1 if rev else 0
            f_ref = fb_ref if rev else ff_ref
            lb = _sig(lg_ref[d:d + 1, :])
            keep, tri, tri_t = _chunk_masks(rev)
            last = 0 if rev else CHUNK - 1
            mid = CHUNK // 2 if rev else CHUNK // 2 - 1

            def local_step(t, carry, latent):
                r0 = _chunk_rows(t, rev, latent)
                rows = pl.ds(r0, CHUNK)
                step = t + (N_CTX_CHUNKS if latent else 0)
                _, _, k, lf = _gates(f_ref[rows, :], lb)
                b = _mm_exact_lhs(tri, lf)
                bl = b[last:last + 1, :]
                decay_ref[step] = jnp.exp(bl)
                st_ref[step] = _mm_tn(v_ref[rows, :], k * jnp.exp(bl - b))
                if latent:
                    q_t = _silu(q_ref[rows, :]) * q_scale * jnp.exp(b)
                    dst_ref[step] = _mm_tn(do_ref[pl.ds(pl.multiple_of(r0 - CTX_LEN, CHUNK), CHUNK), :], q_t)
                else:
                    dst_ref[step] = jnp.zeros((HEAD_DIM, HEAD_DIM), F32)
                return carry

            lax.fori_loop(0, N_CTX_CHUNKS, functools.partial(local_step, latent=False), 0, unroll=2)
            lax.fori_loop(0, N_LAT_CHUNKS, functools.partial(local_step, latent=True), 0, unroll=4)

            def scan_step(i, carry):
                st, d_st = carry
                j = N_CHUNKS - 1 - i
                update, d_update = st_ref[i], dst_ref[j]
                st_ref[i] = st
                dst_ref[j] = d_st
                return st * decay_ref[i] + update, d_st * decay_ref[j] + d_update

            zero_state = jnp.zeros((HEAD_DIM, HEAD_DIM), F32)
            lax.fori_loop(0, N_CHUNKS, scan_step, (zero_state, zero_state))

            def grad_step(t, dlb, latent):
                r0 = _chunk_rows(t, rev, latent)
                rows = pl.ds(r0, CHUNK)
                step = t + (N_CTX_CHUNKS if latent else 0)
                sg, f, k, lf = _gates(f_ref[rows, :], lb)
                v = v_ref[rows, :]
                b = _mm_exact_lhs(tri, lf)
                bl = b[last:last + 1, :]
                e_end = jnp.exp(bl - b)
                k_end = k * e_end
                decay = jnp.exp(bl)
                d_st = dst_ref[step]
                st_prev = st_ref[step]
                d_kend = _mm(v, d_st)
                d_decay = jnp.sum(d_st * st_prev, axis=0, keepdims=True)
                t_kend = d_kend * k_end
                d_v = _mm_nt(k_end, d_st)
                d_k = d_kend * e_end
                d_b = -t_kend
                if latent:
                    qpre = q_ref[rows, :]
                    q = _silu(qpre) * q_scale
                    bm = b[mid:mid + 1, :]
                    e_b, e_qm, e_km = jnp.exp(b), jnp.exp(b - bm), jnp.exp(bm - b)
                    q_t, q_m, k_m = q * e_b, q * e_qm, k * e_km
                    a = jnp.where(keep, _mm_nt(q_m, k_m), 0.0)
                    d_out = do_ref[pl.ds(pl.multiple_of(r0 - CTX_LEN, CHUNK), CHUNK), :]
                    d_a = jnp.where(keep, _mm_nt(d_out, v), 0.0)
                    d_qm = _mm(d_a, k_m)
                    d_km = _mm_tn(d_a, q_m)
                    d_qt = _mm(d_out, st_prev)
                    d_v = d_v + _mm_tn(a, d_out)
                    d_k = d_k + d_km * e_km
                    d_b = d_b + d_qt * q_t + d_qm * q_m - d_km * k_m
                    d_q = d_qt * e_b + d_qm * e_qm
                at_last = jnp.sum(t_kend, axis=0, keepdims=True) + d_decay * decay
                d_b = d_b + jnp.where(is_row == last, at_last, 0.0)
                d_lf = _mm_exact_lhs(tri_t, d_b)
                d_f = d_lf / f - d_k
                dg_ref[d, rows, :] = d_f * (1.0 - lb) * sg * (1.0 - sg)
                if rev:
                    dg_ref[2, rows, :] += d_v
                else:
                    dg_ref[2, rows, :] = d_v
                if latent:
                    d_qpre = d_q * q_scale * _dsilu(qpre)
                    if rev:
                        dg_ref[3, rows, :] += d_qpre
                    else:
                        dg_ref[3, rows, :] = d_qpre
                return dlb + jnp.sum(d_f * (1.0 - sg), axis=0, keepdims=True)

            dlb = lax.fori_loop(0, N_LAT_CHUNKS, functools.partial(grad_step, latent=True), jnp.zeros((1, HEAD_DIM), F32),
                                unroll=2)
            dlb = lax.fori_loop(0, N_CTX_CHUNKS, functools.partial(grad_step, latent=False), dlb, unroll=2)
            dlg_ref[d:d + 1, :] = dlb * lb * (1.0 - lb)

    col = pl.BlockSpec((SEQ, HEAD_DIM), lambda h: (0, h))
    return pl.pallas_call(
        body, name="gla_backward", grid=(HEADS,),
        out_shape=[jax.ShapeDtypeStruct((HGRN_SECTIONS, ROWS_ALL, D_MODEL), F32), jax.ShapeDtypeStruct((2, D_MODEL), F32)],
        in_specs=[G_SPEC(0), G_SPEC(1), G_SPEC(2), G_SPEC(3), pl.BlockSpec((2, HEAD_DIM), lambda h: (0, h)), col, col],
        out_specs=[pl.BlockSpec((HGRN_SECTIONS, ROWS_ALL, HEAD_DIM), lambda h: (0, 0, h)),
                   pl.BlockSpec((2, HEAD_DIM), lambda h: (0, h))],
        scratch_shapes=[pltpu.VMEM((N_CHUNKS, HEAD_DIM, HEAD_DIM), F32), pltpu.VMEM((N_CHUNKS, HEAD_DIM, HEAD_DIM), F32),
                        pltpu.VMEM((N_CHUNKS, 1, HEAD_DIM), F32)],
        compiler_params=_params(48),
    )(g5, g5, g5, g5, lb_logits, d_o, d_z)


GROUP = 2 * CHUNK


def _group_masks(rev):
    rid = lax.broadcasted_iota(jnp.int32, (GROUP, GROUP), 0)
    cid = lax.broadcasted_iota(jnp.int32, (GROUP, GROUP), 1)
    same = (rid >= CHUNK) == (cid >= CHUNK)
    causal = (cid >= rid) if rev else (cid <= rid)
    anti = (cid <= rid) if rev else (cid >= rid)
    bf = lambda m: jnp.where(m, jnp.ones((GROUP, GROUP), F32), jnp.zeros((GROUP, GROUP), F32)).astype(BF16)
    keep = same & causal
    return keep, bf(keep), bf(same & anti), bf(same & jnp.logical_not(anti))


def _group_sum(m_bf, a):
    hi, lo = _split2(a)
    r = jnp.dot(m_bf, jnp.concatenate([hi, lo], axis=1), preferred_element_type=F32)
    return r[:, :HEAD_DIM] + r[:, HEAD_DIM:]


def _chunk_row(a, pos):
    return jnp.concatenate([jnp.broadcast_to(a[c * CHUNK + pos:c * CHUNK + pos + 1, :], (CHUNK, HEAD_DIM)) for c in range(2)], axis=0)


def _by_chunk(a, second):
    return jnp.concatenate([jnp.where(second, 0.0, a), jnp.where(second, a, 0.0)], axis=1)


def _own_block(r):
    return jnp.concatenate([r[0:CHUNK, 0:HEAD_DIM], r[CHUNK:GROUP, HEAD_DIM:2 * HEAD_DIM]], axis=0)


def _scan_step_of(row_chunk, rev, latent):
    if not rev:
        return row_chunk
    return (N_CHUNKS + N_CTX_CHUNKS - 1 - row_chunk) if latent else (N_CTX_CHUNKS - 1 - row_chunk)


def _group_rows(i, j, per_step, latent):
    base = CTX_LEN if latent else 0
    return pl.multiple_of(base + (i * per_step + j) * GROUP, GROUP)


GROUPS_PER_STEP = 8
GROUPS_PER_GRAD_STEP = 4


def _gla_forward(g5, lb_logits, late_shards):
    q_scale = HEAD_DIM ** -0.5
    per_lat, per_ctx = GROUPS_PER_STEP, min(GROUPS_PER_STEP, CTX_LEN // GROUP)
    n_late = len(GATHER_LATE)

    def body(ff_ref, fb_ref, v_ref, q_ref, lg_ref, *rest):
        shard_refs, o_ref, full_refs = rest[:n_late], rest[n_late], rest[n_late + 1:2 * n_late + 1]
        st_ref, decay_ref, qt_ref = rest[2 * n_late + 1:2 * n_late + 4]
        land_refs = rest[2 * n_late + 4:3 * n_late + 4]
        start_gather, forward_gather, finish_gather = _gather_two_level(GATHER_LATE, shard_refs, land_refs, full_refs,
                                                                        *rest[3 * n_late + 4:])

        @pl.when(pl.program_id(0) == 0)
        def _():
            start_gather()

        @pl.when(pl.program_id(0) == HEADS - 1)
        def _():
            forward_gather()

        second = lax.broadcasted_iota(jnp.int32, (GROUP, HEAD_DIM), 0) >= CHUNK
        for rev in (False, True):
            f_ref = fb_ref if rev else ff_ref
            lb = _sig(lg_ref[1:2, :] if rev else lg_ref[0:1, :])
            keep, tri, _, _ = _group_masks(rev)
            last = 0 if rev else CHUNK - 1
            mid = CHUNK // 2 if rev else CHUNK // 2 - 1

            def local_step(i, carry, latent, per):
                r0s = [_group_rows(i, j, per, latent) for j in range(per)]
                rows = [pl.ds(r0, GROUP) for r0 in r0s]
                gates = [_gates(f_ref[r, :], lb) for r in rows]
                vs = [v_ref[r, :] for r in rows]
                bs = [_group_sum(tri, g[3]) for g in gates]
                bls = [_chunk_row(b, last) for b in bs]
                ups = [_mm_tn(v, _by_chunk(g[2] * jnp.exp(bl - b), second)) for v, g, b, bl in zip(vs, gates, bs, bls)]
                if latent:
                    qs = [_silu(q_ref[r, :]) * q_scale for r in rows]
                    bms = [_chunk_row(b, mid) for b in bs]
                    a_s = [_mm_nt(q * jnp.exp(b - bm), g[2] * jnp.exp(bm - b)) for q, g, b, bm in zip(qs, gates, bs, bms)]
                    outs = [_mm(jnp.where(keep, a, 0.0), v) for a, v in zip(a_s, vs)]
                for j in range(per):
                    for c in range(2):
                        step = _scan_step_of(r0s[j] // CHUNK + c, rev, latent)
                        decay_ref[step] = jnp.exp(bls[j][c * CHUNK:c * CHUNK + 1, :])
                        st_ref[step] = ups[j][:, c * HEAD_DIM:(c + 1) * HEAD_DIM]
                    if latent:
                        orow = pl.ds(pl.multiple_of(r0s[j] - CTX_LEN, GROUP), GROUP)
                        qt_ref[orow, :] = (qs[j] * jnp.exp(bs[j])).astype(BF16)
                        if rev:
                            o_ref[orow, :] += outs[j]
                        else:
                            o_ref[orow, :] = outs[j]
                return carry

            lax.fori_loop(0, CTX_LEN // (per_ctx * GROUP), functools.partial(local_step, latent=False, per=per_ctx), 0)
            lax.fori_loop(0, SEQ // (per_lat * GROUP), functools.partial(local_step, latent=True, per=per_lat), 0)

            def scan_step(t, st):
                update = st_ref[t]
                st_ref[t] = st
                return st * decay_ref[t] + update

            lax.fori_loop(0, N_CHUNKS, scan_step, jnp.zeros((HEAD_DIM, HEAD_DIM), F32), unroll=2)

            def inter_step(i, carry):
                r0s = [_group_rows(i, j, per_lat, True) for j in range(per_lat)]
                orows = [pl.ds(pl.multiple_of(r0 - CTX_LEN, GROUP), GROUP) for r0 in r0s]
                states = [jnp.concatenate([st_ref[_scan_step_of(r0 // CHUNK + c, rev, True)] for c in range(2)], axis=0)
                          for r0 in r0s]
                prods = [lax.dot_general(qt_ref[orow, :], s.astype(BF16), (((1,), (1,)), ((), ())), preferred_element_type=F32)
                         for orow, s in zip(orows, states)]
                for orow, r in zip(orows, prods):
                    o_ref[orow, :] += _own_block(r)
                return carry

            lax.fori_loop(0, SEQ // (per_lat * GROUP), inter_step, 0)

        @pl.when(pl.program_id(0) == HEADS - 1)
        def _():
            finish_gather()

    outs = pl.pallas_call(
        body, name="gla_forward", grid=(HEADS,),
        out_shape=[pltpu.HBM((SEQ, D_MODEL), F32)] + [jax.ShapeDtypeStruct(W_SPECS[k][0], BF16) for k in GATHER_LATE],
        in_specs=[G_SPEC(0), G_SPEC(1), G_SPEC(2), G_SPEC(3), pl.BlockSpec((2, HEAD_DIM), lambda h: (0, h))] + [ANY] * n_late,
        out_specs=[pl.BlockSpec((SEQ, HEAD_DIM), lambda h: (0, h))] + [ANY] * n_late,
        scratch_shapes=[pltpu.VMEM((N_CHUNKS, HEAD_DIM, HEAD_DIM), F32), pltpu.VMEM((N_CHUNKS, 1, HEAD_DIM), F32),
                        pltpu.VMEM((SEQ, HEAD_DIM), BF16)] + [pltpu.VMEM(W_SPECS[k][0], BF16) for k in GATHER_LATE]
        + [pltpu.SemaphoreType.DMA((2 * n_late,)), pltpu.SemaphoreType.DMA((2 * n_late,)), pltpu.SemaphoreType.DMA((n_late,))],
        compiler_params=_params(48),
    )(*_pin(g5, g5, g5, g5, lb_logits), *late_shards)
    return outs[0], dict(zip(GATHER_LATE, outs[1:]))


def _gla_backward(g5, lb_logits, d_o, d_z, late_parts):
    q_scale = HEAD_DIM ** -0.5
    per_lat, per_ctx = GROUPS_PER_STEP, min(GROUPS_PER_STEP, CTX_LEN // GROUP)
    n_late = len(GATHER_LATE)

    def body(ff_ref, fb_ref, v_ref, q_ref, lg_ref, do_ref, dz_ref, *rest):
        part_refs, (dg_ref, dlg_ref), slot_refs = rest[:n_late], rest[n_late:n_late + 2], rest[n_late + 2:2 * n_late + 2]
        st_ref, dst_ref, decay_ref, both_ref = rest[2 * n_late + 2:2 * n_late + 6]
        start_scatter, wait_scatter = _scatter_direct(GATHER_LATE, part_refs, slot_refs, *rest[2 * n_late + 6:])

        @pl.when(pl.program_id(0) == 0)
        def _():
            start_scatter()

        dg_ref[3, 0:CTX_LEN, :] = jnp.zeros((CTX_LEN, HEAD_DIM), BF16)
        dg_ref[4, 0:CTX_LEN, :] = jnp.zeros((CTX_LEN, HEAD_DIM), BF16)
        dg_ref[4, CTX_LEN:ROWS_ALL, :] = dz_ref[...]
        second = lax.broadcasted_iota(jnp.int32, (GROUP, HEAD_DIM), 0) >= CHUNK
        for rev in (False, True):
            d = 1 if rev else 0
            f_ref = fb_ref if rev else ff_ref
            lb = _sig(lg_ref[d:d + 1, :])
            keep, tri, tri_t, strict = _group_masks(rev)
            last = 0 if rev else CHUNK - 1
            mid = CHUNK // 2 if rev else CHUNK // 2 - 1

            def local_step(i, carry, latent, per):
                r0s = [_group_rows(i, j, per, latent) for j in range(per)]
                rows = [pl.ds(r0, GROUP) for r0 in r0s]
                gates = [_gates(f_ref[r, :], lb) for r in rows]
                bs = [_group_sum(tri, g[3]) for g in gates]
                bls = [_chunk_row(b, last) for b in bs]
                ups = [_mm_tn(v_ref[r, :], _by_chunk(g[2] * jnp.exp(bl - b), second)) for r, g, b, bl in zip(rows, gates, bs, bls)]
                if latent:
                    orows = [pl.ds(pl.multiple_of(r0 - CTX_LEN, GROUP), GROUP) for r0 in r0s]
                    d_ups = [_mm_tn(do_ref[orow, :], _by_chunk(_silu(q_ref[r, :]) * q_scale * jnp.exp(b), second))
                             for orow, r, b in zip(orows, rows, bs)]
                for j in range(per):
                    for c in range(2):
                        step = _scan_step_of(r0s[j] // CHUNK + c, rev, latent)
                        decay_ref[step] = jnp.exp(bls[j][c * CHUNK:c * CHUNK + 1, :])
                        st_ref[step] = ups[j][:, c * HEAD_DIM:(c + 1) * HEAD_DIM]
                        if latent:
                            dst_ref[step] = d_ups[j][:, c * HEAD_DIM:(c + 1) * HEAD_DIM]
                        else:
                            dst_ref[step] = jnp.zeros((HEAD_DIM, HEAD_DIM), F32)
                return carry

            lax.fori_loop(0, CTX_LEN // (per_ctx * GROUP), functools.partial(local_step, latent=False, per=per_ctx), 0)
            lax.fori_loop(0, SEQ // (per_lat * GROUP), functools.partial(local_step, latent=True, per=per_lat), 0)

            def scan_step(i, carry):
                st, d_st = carry
                j = N_CHUNKS - 1 - i
                update, d_update = st_ref[i], dst_ref[j]
                st_ref[i] = st
                dst_ref[j] = d_st
                return st * decay_ref[i] + update, d_st * decay_ref[j] + d_update

            zero_state = jnp.zeros((HEAD_DIM, HEAD_DIM), F32)
            lax.fori_loop(0, N_CHUNKS, scan_step, (zero_state, zero_state))

            def grad_step(i, dlb, latent, per):
                r0s = [_group_rows(i, j, per, latent) for j in range(per)]
                rows = [pl.ds(r0, GROUP) for r0 in r0s]
                gates = [_gates(f_ref[r, :], lb) for r in rows]
                vs = [v_ref[r, :] for r in rows]
                bs = [_group_sum(tri, g[3]) for g in gates]
                bls = [_chunk_row(b, last) for b in bs]
                e_ends = [jnp.exp(bl - b) for b, bl in zip(bs, bls)]
                k_ends = [g[2] * e for g, e in zip(gates, e_ends)]
                sts = [[st_ref[_scan_step_of(r0 // CHUNK + c, rev, latent)] for c in range(2)] for r0 in r0s]
                d_sts = [[dst_ref[_scan_step_of(r0 // CHUNK + c, rev, latent)] for c in range(2)] for r0 in r0s]
                d_kends = [_own_block(_mm(v, jnp.concatenate(ds, axis=1))) for v, ds in zip(vs, d_sts)]
                d_vs = [_own_block(_mm_nt(ke, jnp.concatenate(ds, axis=0))) for ke, ds in zip(k_ends, d_sts)]
                at_last = [jnp.concatenate([jnp.broadcast_to(jnp.sum(ds[c] * s[c], axis=0, keepdims=True), (CHUNK, HEAD_DIM))
                                            for c in range(2)], axis=0) * jnp.exp(bl) for ds, s, bl in zip(d_sts, sts, bls)]
                t_kends = [dk * ke for dk, ke in zip(d_kends, k_ends)]
                d_ks = [dk * e for dk, e in zip(d_kends, e_ends)]
                d_lfs = [_group_sum(strict, t) + al for t, al in zip(t_kends, at_last)]
                if latent:
                    orows = [pl.ds(pl.multiple_of(r0 - CTX_LEN, GROUP), GROUP) for r0 in r0s]
                    qpres = [q_ref[r, :] for r in rows]
                    q_sigs = [_sig(qp) for qp in qpres]
                    qs = [qp * sg * q_scale for qp, sg in zip(qpres, q_sigs)]
                    bms = [_chunk_row(b, mid) for b in bs]
                    e_bs = [jnp.exp(b) for b in bs]
                    e_qms = [jnp.exp(b - bm) for b, bm in zip(bs, bms)]
                    e_kms = [jnp.exp(bm - b) for b, bm in zip(bs, bms)]
                    q_ts = [q * e for q, e in zip(qs, e_bs)]
                    q_ms = [q * e for q, e in zip(qs, e_qms)]
                    k_ms = [g[2] * e for g, e in zip(gates, e_kms)]
                    d_outs = [do_ref[orow, :] for orow in orows]
                    a_s = [jnp.where(keep, _mm_nt(qm, km), 0.0) for qm, km in zip(q_ms, k_ms)]
                    d_as = [jnp.where(keep, _mm_nt(do, v), 0.0) for do, v in zip(d_outs, vs)]
                    d_qts = [_own_block(_mm(do, jnp.concatenate(s, axis=1))) for do, s in zip(d_outs, sts)]
                    d_qms = [_mm(da, km) for da, km in zip(d_as, k_ms)]
                    d_kms = [_mm_tn(da, qm) for da, qm in zip(d_as, q_ms)]
                    d_vs = [dv + _mm_tn(a, do) for dv, a, do in zip(d_vs, a_s, d_outs)]
                    d_ks = [dk + dkm * e for dk, dkm, e in zip(d_ks, d_kms, e_kms)]
                    d_lfs = [dl + _group_sum(tri_t, dqt * qt + dqm * qm - dkm * km)
                             for dl, dqt, qt, dqm, qm, dkm, km in zip(d_lfs, d_qts, q_ts, d_qms, q_ms, d_kms, k_ms)]
                    d_qs = [dqt * eb + dqm * eq for dqt, eb, dqm, eq in zip(d_qts, e_bs, d_qms, e_qms)]
                for j in range(per):
                    sg, f = gates[j][0], gates[j][1]
                    d_f = d_lfs[j] / f - d_ks[j]
                    dg_ref[d, rows[j], :] = (d_f * (1.0 - lb) * sg * (1.0 - sg)).astype(BF16)
                    dlb = dlb + jnp.sum(d_f * (1.0 - sg), axis=0, keepdims=True)
                    if rev:
                        dg_ref[2, rows[j], :] = (both_ref[0, rows[j], :] + d_vs[j]).astype(BF16)
                    else:
                        both_ref[0, rows[j], :] = d_vs[j]
                    if latent:
                        d_qpre = d_qs[j] * q_scale * (q_sigs[j] * (1.0 + qpres[j] * (1.0 - q_sigs[j])))
                        if rev:
                            dg_ref[3, rows[j], :] = (both_ref[1, rows[j], :] + d_qpre).astype(BF16)
                        else:
                            both_ref[1, rows[j], :] = d_qpre
                return dlb

            dlb = lax.fori_loop(0, SEQ // (GROUPS_PER_GRAD_STEP * GROUP),
                                functools.partial(grad_step, latent=True, per=GROUPS_PER_GRAD_STEP), jnp.zeros((1, HEAD_DIM), F32))
            dlb = lax.fori_loop(0, CTX_LEN // (per_ctx * GROUP), functools.partial(grad_step, latent=False, per=per_ctx), dlb)
            dlg_ref[d:d + 1, :] = dlb * lb * (1.0 - lb)

        @pl.when(pl.program_id(0) == HEADS - 1)
        def _():
            wait_scatter()

    col = pl.BlockSpec((SEQ, HEAD_DIM), lambda h: (0, h))
    outs = pl.pallas_call(
        body, name="gla_backward", grid=(HEADS,),
        out_shape=[pltpu.HBM((HGRN_SECTIONS, ROWS_ALL, D_MODEL), BF16), jax.ShapeDtypeStruct((2, D_MODEL), F32)]
        + _slot_shapes(GATHER_LATE),
        in_specs=[G_SPEC(0), G_SPEC(1), G_SPEC(2), G_SPEC(3), pl.BlockSpec((2, HEAD_DIM), lambda h: (0, h)), col, col]
        + [ANY] * n_late,
        out_specs=[pl.BlockSpec((HGRN_SECTIONS, ROWS_ALL, HEAD_DIM), lambda h: (0, 0, h)),
                   pl.BlockSpec((2, HEAD_DIM), lambda h: (0, h))] + [ANY] * n_late,
        scratch_shapes=[pltpu.VMEM((N_CHUNKS, HEAD_DIM, HEAD_DIM), F32), pltpu.VMEM((N_CHUNKS, HEAD_DIM, HEAD_DIM), F32),
                        pltpu.VMEM((N_CHUNKS, 1, HEAD_DIM), F32), pltpu.VMEM((2, ROWS_ALL, HEAD_DIM), F32)]
        + _comm_sems(n_late),
        compiler_params=_params(48),
    )(*_pin(g5, g5, g5, g5, lb_logits, d_o, d_z), *late_parts)
    return outs[0], outs[1], dict(zip(GATHER_LATE, outs[2:]))


def _head_norm(o, gw, scr):
    rs = []
    for h in range(HEADS):
        cols = slice(h * HEAD_DIM, (h + 1) * HEAD_DIM)
        oh = o[:, cols]
        r = lax.rsqrt(jnp.mean(oh * oh, axis=-1, keepdims=True) + EPS)
        scr[:, cols] = oh * r
        rs.append(r)
    return rs


def _hgrn_out_forward(o_raw, g5, xin, gnorm_w, gate, w_out):
    tm = ROW_TILE

    def body(o_ref, z_ref, x_ref, gw_ref, gate_ref, w_ref, x1_ref, res_ref, scr):
        _head_norm(o_ref[...], None, scr)
        a = scr[...] * gw_ref[...] * _silu(z_ref[...])
        res = _mm(a, w_ref[...])
        res_ref[...] = res
        x1_ref[...] = x_ref[...] + gate_ref[...] * res

    tile = pl.BlockSpec((tm, D_MODEL), lambda i: (i, 0))
    vec = pl.BlockSpec((1, D_MODEL), lambda i: (0, 0))
    return pl.pallas_call(
        body, name="hgrn_out_forward", grid=(SEQ // tm,),
        out_shape=[pltpu.HBM((SEQ, D_MODEL), F32)] * 2,
        in_specs=[tile, pl.BlockSpec((None, tm, D_MODEL), lambda i: (4, i + CTX_LEN // tm, 0)), tile, vec, vec,
                  pl.BlockSpec((D_MODEL, D_MODEL), lambda i: (0, 0))],
        out_specs=[tile, tile],
        scratch_shapes=[pltpu.VMEM((tm, D_MODEL), F32)],
        compiler_params=_params(32),
    )(*_pin(o_raw, g5, xin, gnorm_w, gate, w_out))


def _hgrn_out_backward(d_x1, o_raw, g5, res, gnorm_w, gate, w_out):
    tm = ROW_TILE

    def body(dx_ref, o_ref, z_ref, res_ref, gw_ref, gate_ref, w_ref, do_ref, dz_ref, dw_out, dgate_ref, dgw_ref, scr, scr2,
             dw_ref):
        @pl.when(pl.program_id(0) == 0)
        def _():
            dw_ref[...] = jnp.zeros_like(dw_ref)
            dgate_ref[...] = jnp.zeros_like(dgate_ref)
            dgw_ref[...] = jnp.zeros_like(dgw_ref)

        dx = dx_ref[...]
        dgate_ref[...] += jnp.sum(dx * res_ref[...], axis=0, keepdims=True)
        d_res = (dx * gate_ref[...]).astype(BF16)
        d_a = _mm_nt(d_res, w_ref[...])
        rs = _head_norm(o_ref[...], None, scr)
        z = z_ref[...]
        sz = _silu(z)
        o_hat = scr[...]
        o_n = o_hat * gw_ref[...]
        dw_ref[...] += _mm_tn(o_n * sz, d_res)
        d_on = d_a * sz
        dz_ref[...] = (d_a * o_n * _dsilu(z)).astype(BF16)
        dgw_ref[...] += jnp.sum(d_on * o_hat, axis=0, keepdims=True)
        scr2[...] = d_on * gw_ref[...]
        for h in range(HEADS):
            cols = slice(h * HEAD_DIM, (h + 1) * HEAD_DIM)
            dh, oh = scr2[:, cols], scr[:, cols]
            do_ref[:, cols] = (rs[h] * (dh - oh * jnp.mean(dh * oh, axis=-1, keepdims=True))).astype(BF16)

        @pl.when(pl.program_id(0) == SEQ // tm - 1)
        def _():
            dw_out[...] = dw_ref[...].astype(BF16)

    tile = pl.BlockSpec((tm, D_MODEL), lambda i: (i, 0))
    vec = pl.BlockSpec((1, D_MODEL), lambda i: (0, 0))
    mat = pl.BlockSpec((D_MODEL, D_MODEL), lambda i: (0, 0))
    return pl.pallas_call(
        body, name="hgrn_out_backward", grid=(SEQ // tm,),
        out_shape=[pltpu.HBM((SEQ, D_MODEL), BF16)] * 2 + [pltpu.HBM((D_MODEL, D_MODEL), BF16)]
        + [jax.ShapeDtypeStruct((1, D_MODEL), F32)] * 2,
        in_specs=[tile, tile, pl.BlockSpec((None, tm, D_MODEL), lambda i: (4, i + CTX_LEN // tm, 0)), tile, vec, vec, mat],
        out_specs=[tile, tile, mat, vec, vec],
        scratch_shapes=[pltpu.VMEM((tm, D_MODEL), F32)] * 2 + [pltpu.VMEM((D_MODEL, D_MODEL), F32)],
        compiler_params=_params(40),
    )(*_pin(d_x1, o_raw, g5, res, gnorm_w, gate, w_out))


def _pool_constants():
    win = np.zeros((POOL_GROUPS, ROW_TILE, ROW_TILE), np.float32)
    inv = np.zeros((POOL_GROUPS, ROW_TILE, 1), np.float32)
    for g, w in enumerate(POOL_WINDOWS):
        for t in range(ROW_TILE):
            base, p = (t // GRID_W) * GRID_W, t % GRID_W
            lo = min(max(p - w // 2, 0), GRID_W)
            hi = min(max(p - w // 2 + w, 0), GRID_W)
            win[g, t, base + lo:base + hi] = 1.0
            inv[g, t, 0] = 1.0 / np.float32(hi - lo)
    return jnp.asarray(win, BF16), jnp.asarray(win.transpose(0, 2, 1), BF16), jnp.asarray(inv, F32)


def _pool_mix(u_ref, wg_ref, ps_ref, win_ref, inv_ref, pooled_scr, yg_scr):
    for g in range(POOL_GROUPS):
        cols = slice(g * POOL_GROUP_DIM, (g + 1) * POOL_GROUP_DIM)
        ug = u_ref[:, cols]
        pooled = _mm_exact_lhs(win_ref[g], ug) * inv_ref[g] - ug
        if pooled_scr is not None:
            pooled_scr[:, cols] = pooled
        yg_scr[:, cols] = _mm(pooled, wg_ref[g])


def _pool_forward_loss(uz, x1, target, gate, w_grp, pool_scale, w_out, final_w):
    tm = ROW_TILE
    win, _, inv = _pool_constants()

    def body(u_ref, z_ref, x_ref, t_ref, gate_ref, wg_ref, ps_ref, w_ref, fw_ref, win_ref, inv_ref,
             dx_ref, loss_ref, dfw_ref, dgate_ref, yg_scr):
        @pl.when(pl.program_id(0) == 0)
        def _():
            loss_ref[...] = jnp.zeros_like(loss_ref)
            dfw_ref[...] = jnp.zeros_like(dfw_ref)
            dgate_ref[...] = jnp.zeros_like(dgate_ref)

        _pool_mix(u_ref, wg_ref, ps_ref, win_ref, inv_ref, None, yg_scr)
        a = yg_scr[...] * ps_ref[...] * _silu(z_ref[...])
        res = _mm(a, w_ref[...])
        x2 = x_ref[...] + gate_ref[...] * res
        r = lax.rsqrt(jnp.mean(x2 * x2, axis=-1, keepdims=True) + EPS)
        xh = x2 * r
        fw = fw_ref[...]
        err = xh * fw - t_ref[...]
        loss_ref[...] += 0.5 * jnp.sum(jnp.mean(err * err, axis=-1, keepdims=True))
        d_y = err * (1.0 / D_MODEL)
        dfw_ref[...] += jnp.sum(d_y * xh, axis=0, keepdims=True)
        d_xh = d_y * fw
        d_x2 = r * (d_xh - xh * jnp.mean(d_xh * xh, axis=-1, keepdims=True))
        dx_ref[...] = d_x2
        dgate_ref[...] += jnp.sum(d_x2 * res, axis=0, keepdims=True)

    tile = pl.BlockSpec((tm, D_MODEL), lambda i: (i, 0))
    vec = pl.BlockSpec((1, D_MODEL), lambda i: (0, 0))
    grp = pl.BlockSpec((POOL_GROUPS, POOL_GROUP_DIM, POOL_GROUP_DIM), lambda i: (0, 0, 0))
    return pl.pallas_call(
        body, name="pool_forward_loss", grid=(SEQ // tm,),
        out_shape=[pltpu.HBM((SEQ, D_MODEL), F32), jax.ShapeDtypeStruct((8, 128), F32),
                   jax.ShapeDtypeStruct((1, D_MODEL), F32), jax.ShapeDtypeStruct((1, D_MODEL), F32)],
        in_specs=[pl.BlockSpec((None, tm, D_MODEL), lambda i: (0, i, 0)), pl.BlockSpec((None, tm, D_MODEL), lambda i: (1, i, 0)),
                  tile, tile, vec, grp, vec, pl.BlockSpec((D_MODEL, D_MODEL), lambda i: (0, 0)), vec, grp,
                  pl.BlockSpec((POOL_GROUPS, ROW_TILE, 1), lambda i: (0, 0, 0))],
        out_specs=[tile, pl.BlockSpec((8, 128), lambda i: (0, 0)), vec, vec],
        scratch_shapes=[pltpu.VMEM((tm, D_MODEL), F32)],
        compiler_params=_params(32),
    )(*_pin(uz, uz, x1, target, gate, w_grp, pool_scale, w_out, final_w, win, inv))


def _pool_backward(d_x2, uz, gate, w_grp, pool_scale, w_out):
    tm = ROW_TILE
    win, win_t, inv = _pool_constants()

    def body(dx_ref, u_ref, z_ref, gate_ref, wg_ref, ps_ref, w_ref, win_ref, wint_ref, inv_ref,
             duz_ref, dw_out, dwg_out, dps_ref, pooled_scr, yg_scr, dyg_scr, dw_ref, dwg_ref):
        @pl.when(pl.program_id(0) == 0)
        def _():
            dw_ref[...] = jnp.zeros_like(dw_ref)
            dwg_ref[...] = jnp.zeros_like(dwg_ref)
            dps_ref[...] = jnp.zeros_like(dps_ref)

        _pool_mix(u_ref, wg_ref, ps_ref, win_ref, inv_ref, pooled_scr, yg_scr)
        z = z_ref[...]
        sz = _silu(z)
        yg = yg_scr[...]
        y = yg * ps_ref[...]
        d_res = (dx_ref[...] * gate_ref[...]).astype(BF16)
        d_a = _mm_nt(d_res, w_ref[...])
        dw_ref[...] += _mm_tn(y * sz, d_res)
        d_y = d_a * sz
        duz_ref[1] = (d_a * y * _dsilu(z)).astype(BF16)
        dps_ref[...] += jnp.sum(d_y * yg, axis=0, keepdims=True)
        dyg_scr[...] = d_y * ps_ref[...]
        for g in range(POOL_GROUPS):
            cols = slice(g * POOL_GROUP_DIM, (g + 1) * POOL_GROUP_DIM)
            d_yg = dyg_scr[:, cols].astype(BF16)
            d_pool = _mm_nt(d_yg, wg_ref[g])
            dwg_ref[g] += _mm_tn(pooled_scr[:, cols], d_yg)
            duz_ref[0, :, cols] = (_mm_exact_lhs(wint_ref[g], d_pool * inv_ref[g]) - d_pool).astype(BF16)

        @pl.when(pl.program_id(0) == SEQ // tm - 1)
        def _():
            dw_out[...] = dw_ref[...].astype(BF16)
            dwg_out[...] = dwg_ref[...].astype(BF16)

    tile = pl.BlockSpec((tm, D_MODEL), lambda i: (i, 0))
    vec = pl.BlockSpec((1, D_MODEL), lambda i: (0, 0))
    mat = pl.BlockSpec((D_MODEL, D_MODEL), lambda i: (0, 0))
    grp = pl.BlockSpec((POOL_GROUPS, POOL_GROUP_DIM, POOL_GROUP_DIM), lambda i: (0, 0, 0))
    return pl.pallas_call(
        body, name="pool_backward", grid=(SEQ // tm,),
        out_shape=[pltpu.HBM((POOL_SECTIONS, SEQ, D_MODEL), BF16), pltpu.HBM((D_MODEL, D_MODEL), BF16),
                   pltpu.HBM((POOL_GROUPS, POOL_GROUP_DIM, POOL_GROUP_DIM), BF16), jax.ShapeDtypeStruct((1, D_MODEL), F32)],
        in_specs=[tile, pl.BlockSpec((None, tm, D_MODEL), lambda i: (0, i, 0)), pl.BlockSpec((None, tm, D_MODEL), lambda i: (1, i, 0)),
                  vec, grp, vec, mat, grp, grp, pl.BlockSpec((POOL_GROUPS, ROW_TILE, 1), lambda i: (0, 0, 0))],
        out_specs=[pl.BlockSpec((POOL_SECTIONS, tm, D_MODEL), lambda i: (0, i, 0)), mat, grp, vec],
        scratch_shapes=[pltpu.VMEM((tm, D_MODEL), F32)] * 3 + [pltpu.VMEM((D_MODEL, D_MODEL), F32),
                                                               pltpu.VMEM((POOL_GROUPS, POOL_GROUP_DIM, POOL_GROUP_DIM), F32)],
        compiler_params=_params(40),
    )(*_pin(d_x2, uz, uz, gate, w_grp, pool_scale, w_out, win, win_t, inv))


def _ln_mod_backward(d_g, w, ctx_tile, xin, nw, scale, d_up, name):
    n_sec, rows, _ = d_g.shape
    n_mod = scale.shape[0]
    tm = ROW_TILE
    n_tiles = rows // tm
    skip = n_mod - 1

    def body(dg_ref, w_ref, *refs):
        c_ref = refs[0] if skip else None
        x_ref, nw_ref, sc_ref, up_ref, dx_ref, dnw_ref, dmod_ref = refs[skip:]
        i = pl.program_id(0)

        @pl.when(i == 0)
        def _():
            dnw_ref[...] = jnp.zeros_like(dnw_ref)

        @pl.when((i == 0) | (i == skip))
        def _():
            dmod_ref[...] = jnp.zeros_like(dmod_ref)

        d_h = _mm_nt(dg_ref[0], w_ref[:, 0:D_MODEL])
        for k in range(1, n_sec):
            d_h = d_h + _mm_nt(dg_ref[k], w_ref[:, k * D_MODEL:(k + 1) * D_MODEL])
        xv = jnp.where(i == 0, c_ref[...], x_ref[...]) if skip else x_ref[...]
        r = lax.rsqrt(jnp.mean(xv * xv, axis=-1, keepdims=True) + EPS)
        xh = xv * r
        nw_row = nw_ref[...]
        dmod_ref[0:1, :] += jnp.sum(d_h, axis=0, keepdims=True)
        dmod_ref[1:2, :] += jnp.sum(d_h * (xh * nw_row), axis=0, keepdims=True)
        d_xn = d_h * (1.0 + sc_ref[...])
        dnw_ref[...] += jnp.sum(d_xn * xh, axis=0, keepdims=True)
        d_xh = d_xn * nw_row

        @pl.when(i >= skip)
        def _():
            dx_ref[...] = up_ref[...] + r * (d_xh - xh * jnp.mean(d_xh * xh, axis=-1, keepdims=True))

    lat = lambda i: (jnp.maximum(i - skip, 0), 0)
    mod_idx = lambda i: (jnp.minimum(i, n_mod - 1), 0, 0)
    return pl.pallas_call(
        body, name=name, grid=(n_tiles,),
        out_shape=[pltpu.HBM((rows - skip * tm, D_MODEL), F32), jax.ShapeDtypeStruct((1, D_MODEL), F32),
                   jax.ShapeDtypeStruct((n_mod, 8, D_MODEL), F32)],
        in_specs=[pl.BlockSpec((n_sec, tm, D_MODEL), lambda i: (0, i, 0)),
                  pl.BlockSpec((D_MODEL, n_sec * D_MODEL), lambda i: (0, 0))]
        + [pl.BlockSpec((tm, D_MODEL), lambda i: (0, 0))] * skip
        + [pl.BlockSpec((tm, D_MODEL), lat),
           pl.BlockSpec((1, D_MODEL), lambda i: (0, 0)),
           pl.BlockSpec((None, 1, D_MODEL), mod_idx),
           pl.BlockSpec((tm, D_MODEL), lat)],
        out_specs=[pl.BlockSpec((tm, D_MODEL), lat), pl.BlockSpec((1, D_MODEL), lambda i: (0, 0)),
                   pl.BlockSpec((None, 8, D_MODEL), mod_idx)],
        compiler_params=_params(48),
    )(*_pin(d_g, w, *([ctx_tile] * skip), xin, nw, scale.reshape(n_mod, 1, D_MODEL), d_up))


def _weight_grad(h, d_g, name):
    n_sec, rows, _ = d_g.shape
    tm = 768 if rows % 768 == 0 else 512
    n_tiles = rows // tm

    def body(h_ref, dg_ref, dw_ref, acc):
        i = pl.program_id(1)
        prod = _mm_tn(h_ref[...], dg_ref[...])

        @pl.when(i == 0)
        def _():
            acc[...] = prod

        @pl.when((i > 0) & (i < n_tiles - 1))
        def _():
            acc[...] += prod

        @pl.when(i == n_tiles - 1)
        def _():
            dw_ref[...] = (acc[...] + prod).astype(BF16)

    return pl.pallas_call(
        body, name=name, grid=(n_sec, n_tiles),
        out_shape=pltpu.HBM((D_MODEL, n_sec * D_MODEL), BF16),
        in_specs=[pl.BlockSpec((tm, D_MODEL), lambda j, i: (i, 0)), pl.BlockSpec((None, tm, D_MODEL), lambda j, i: (j, i, 0))],
        out_specs=pl.BlockSpec((D_MODEL, D_MODEL), lambda j, i: (0, j)),
        scratch_shapes=[pltpu.VMEM((D_MODEL, D_MODEL), F32)],
        compiler_params=_params(32),
    )(*_pin(h, d_g))


def _weight_grad_paired(h, d_g, name):
    n_sec, rows, _ = d_g.shape
    tm = 768
    n_tiles = rows // tm
    half = D_MODEL // 2

    def body(h_ref, dg_ref, q_ref, acc, keep, send, land, send_sems, recv_sems):
        j, i = pl.program_id(0), pl.program_id(1)
        x, y, c = _my_place()
        prod = _mm_tn(h_ref[...], dg_ref[...])

        def to_sibling(k):
            return pltpu.make_async_remote_copy(src_ref=send.at[k], dst_ref=land.at[k], send_sem=send_sems.at[k],
                                                recv_sem=recv_sems.at[k], device_id=(x, y, 1 - c), device_id_type=MESH)

        @pl.when(i == 0)
        def _():
            acc[...] = prod

        @pl.when((i > 0) & (i < n_tiles - 1))
        def _():
            acc[...] += prod

        @pl.when(i == n_tiles - 1)
        def _():
            acc[...] += prod
            keep[j] = acc[pl.ds(_al(c * half, half), half), :]
            send[j] = acc[pl.ds(_al((1 - c) * half, half), half), :].astype(BF16)
            to_sibling(j).start()

        @pl.when((j == n_sec - 1) & (i == n_tiles - 1))
        def _():
            for k in range(n_sec):
                to_sibling(k).wait()
                q_ref[:, k * D_MODEL:(k + 1) * D_MODEL] = (keep[k] + land[k].astype(F32)).astype(BF16)

    return pl.pallas_call(
        body, name=name, grid=(n_sec, n_tiles),
        out_shape=jax.ShapeDtypeStruct((half, n_sec * D_MODEL), BF16),
        in_specs=[pl.BlockSpec((tm, D_MODEL), lambda j, i: (i, 0)), pl.BlockSpec((None, tm, D_MODEL), lambda j, i: (j, i, 0))],
        out_specs=pl.BlockSpec((half, n_sec * D_MODEL), lambda j, i: (0, 0)),
        scratch_shapes=[pltpu.VMEM((D_MODEL, D_MODEL), F32), pltpu.VMEM((n_sec, half, D_MODEL), F32),
                        pltpu.VMEM((n_sec, half, D_MODEL), BF16), pltpu.VMEM((n_sec, half, D_MODEL), BF16),
                        pltpu.SemaphoreType.DMA((n_sec,)), pltpu.SemaphoreType.DMA((n_sec,))],
        compiler_params=_params(56),
    )(*_pin(h, d_g))


def _sum_slots(slots, name):
    n_slots, rows, cols = slots.shape
    tr = 128

    def body(s_ref, o_ref):
        acc = s_ref[0].astype(F32)
        for d in range(1, n_slots):
            acc = acc + s_ref[d].astype(F32)
        o_ref[...] = acc

    return pl.pallas_call(
        body, name=name, grid=(rows // tr,),
        out_shape=jax.ShapeDtypeStruct((rows, cols), F32),
        in_specs=[pl.BlockSpec((n_slots, tr, cols), lambda i: (0, i, 0))],
        out_specs=pl.BlockSpec((tr, cols), lambda i: (i, 0)),
    )(*_pin(slots))


def _adamw_math(w, g, m, v):
    m = ADAM_B1 * m + (1.0 - ADAM_B1) * g
    v = ADAM_B2 * v + (1.0 - ADAM_B2) * (g * g)
    m_hat = m / (1.0 - ADAM_B1 ** ADAM_STEP)
    v_hat = v / (1.0 - ADAM_B2 ** ADAM_STEP)
    return -ADAM_LR * (m_hat / (jnp.sqrt(v_hat) + ADAM_EPS) + ADAM_WD * w), m, v


def _adamw(w, g, m, v, name):
    rows, cols = w.shape
    tr = rows if rows <= 128 else 128

    def body(w_ref, g_ref, m_ref, v_ref, d_ref, mo_ref, vo_ref):
        d_ref[...], mo_ref[...], vo_ref[...] = _adamw_math(w_ref[...], g_ref[...], m_ref[...], v_ref[...])

    tile = pl.BlockSpec((tr, cols), lambda i: (i, 0))
    return pl.pallas_call(
        body, name=name, grid=(rows // tr,),
        out_shape=[pltpu.HBM((rows, cols), F32)] * 3,
        in_specs=[tile] * 4, out_specs=[tile] * 3,
    )(*_pin(w, g, m, v))


def _sum_devices(gathered):
    def body(p_ref, o_ref):
        acc = p_ref[0:SMALL_ROWS, :]
        for d in range(1, N_DEV):
            acc = acc + p_ref[d * SMALL_ROWS:(d + 1) * SMALL_ROWS, :]
        o_ref[...] = acc

    return pl.pallas_call(body, name="sum_devices", out_shape=jax.ShapeDtypeStruct((SMALL_ROWS, D_MODEL), F32),
                          in_specs=[VMEM], out_specs=VMEM)(gathered)


def _ada_update(cond_t, d_mod, ada_w, m, v):
    n_layers, _, n_cols = ada_w.shape
    tr = ROW_TILE

    def body(c_ref, dm_ref, w_ref, m_ref, v_ref, g_ref, d_ref, mo_ref, vo_ref):
        g = _mm_f32(_silu(c_ref[...]), dm_ref[...])
        g_ref[...] = g
        d_ref[...], mo_ref[...], vo_ref[...] = _adamw_math(w_ref[...], g, m_ref[...], v_ref[...])

    tile = pl.BlockSpec((None, tr, n_cols), lambda l, i: (l, i, 0))
    return pl.pallas_call(
        body, name="ada_update", grid=(n_layers, D_MODEL // tr),
        out_shape=[pltpu.HBM(ada_w.shape, F32)] * 4,
        in_specs=[pl.BlockSpec((tr, 16), lambda l, i: (i, 0)), pl.BlockSpec((None, 16, n_cols), lambda l, i: (l, 0, 0)),
                  tile, tile, tile],
        out_specs=[tile] * 4,
    )(*_pin(cond_t, d_mod, ada_w, m, v))


def _cond_ctx_partial(d_modc, ada_w0):
    n_cols = ada_w0.shape[1]
    tr = ROW_TILE

    def body(dm_ref, w_ref, o_ref):
        o_ref[...] = jnp.sum(w_ref[...] * dm_ref[...], axis=-1, keepdims=True)

    return pl.pallas_call(
        body, name="cond_ctx_partial", grid=(D_MODEL // tr,),
        out_shape=jax.ShapeDtypeStruct((D_MODEL, 1), F32),
        in_specs=[pl.BlockSpec((1, n_cols), lambda i: (0, 0)), pl.BlockSpec((tr, n_cols), lambda i: (i, 0))],
        out_specs=pl.BlockSpec((tr, 1), lambda i: (i, 0)),
    )(d_modc, ada_w0)


def _cond_ctx_update(gathered, c_ctx, m, v):
    def body(p_ref, w_ref, m_ref, v_ref, g_ref, d_ref, mo_ref, vo_ref):
        acc = p_ref[0:1, :]
        for s in range(1, N_CHIPS):
            acc = acc + p_ref[16 * s:16 * s + 1, :]
        w = w_ref[...]
        g = acc * _dsilu(w)
        g_ref[...] = g
        d_ref[...], mo_ref[...], vo_ref[...] = _adamw_math(w, g, m_ref[...], v_ref[...])

    return pl.pallas_call(body, name="cond_ctx_update", out_shape=[jax.ShapeDtypeStruct((1, D_MODEL), F32)] * 4,
                          in_specs=[VMEM] * 4, out_specs=[VMEM] * 4)(gathered, c_ctx, m, v)


def _local_step(x2, ctx2, target, mod_mine, mod_ctx, lb_logits, scale_full, w_in_full, late_shards, norm_w, gnorm, final_w):
    row = lambda a: a.reshape(1, -1)
    shift0, scale0, gate0 = (row(a) for a in jnp.split(mod_mine[0], 3))
    shift1, scale1, gate1 = (row(a) for a in jnp.split(mod_mine[1], 3))
    shift_c, scale_c, _ = (row(a) for a in jnp.split(mod_ctx, 3))
    nw0, nw1 = norm_w[0:1], norm_w[1:2]
    scales0 = jnp.concatenate([scale_c, scale0])

    g5, h0 = _ln_mod_matmul(ctx2, x2, nw0, jnp.concatenate([shift_c, shift0]), scales0, w_in_full, "hgrn_in_forward")
    o_raw, full = _gla_forward(g5, lb_logits, late_shards)
    x1, res0 = _hgrn_out_forward(o_raw, g5, x2, gnorm, gate0, full["hgrn_w_out"])
    uz, h1 = _ln_mod_matmul(None, x1, nw1, shift1, scale1, full["pool_w_in"], "pool_in_forward")
    d_x2, loss_part, d_final, d_gate1 = _pool_forward_loss(uz, x1, target, gate1, full["pool_w_grp"], scale_full,
                                                           full["pool_w_out"], final_w)

    d_uz, dw_pool_out, dw_pool_grp, d_pscale = _pool_backward(d_x2, uz, gate1, full["pool_w_grp"], scale_full, full["pool_w_out"])
    d_x1, d_nw1, d_mod1 = _ln_mod_backward(d_uz, full["pool_w_in"], None, x1, nw1, scale1, d_x2, "pool_in_backward")
    dw_pool_in = _weight_grad(h1, d_uz, "pool_in_weight_grad")
    d_o, d_z, dw_hgrn_out, d_gate0, d_gnorm = _hgrn_out_backward(d_x1, o_raw, g5, res0, gnorm, gate0, full["hgrn_w_out"])
    late_grads = {"hgrn_w_out": dw_hgrn_out, "pool_w_in": dw_pool_in, "pool_w_grp": dw_pool_grp, "pool_w_out": dw_pool_out}
    d_g5, d_lb, late_slots = _gla_backward(g5, lb_logits, d_o, d_z, [late_grads[k].astype(BF16) for k in GATHER_LATE])
    dw_hgrn_in = _weight_grad_paired(h0, d_g5, "hgrn_in_weight_grad")
    key = GATHER_EARLY[0]
    half = _half_shape(key)
    xi, yi, _ = _my_place()
    own = lax.dynamic_slice(dw_hgrn_in, (0, (2 * xi + yi) * half[1]), half)
    slots0 = lax.dynamic_update_slice(jnp.zeros((N_CHIPS,) + half, BF16), own[None], (2 * xi + yi, 0, 0))
    send_sem, recv_sem, part_thru, slots_thru, token = _scatter_start(dw_hgrn_in, slots0, own, key)
    d_x, d_nw0, d_mod0 = _ln_mod_backward(d_g5, w_in_full, ctx2, x2, nw0 + token[0:1, 0:1], scales0, d_x1, "hgrn_in_backward")
    slots = dict(late_slots)
    pending = (send_sem, recv_sem, part_thru, slots_thru)

    zero = jnp.zeros((1, D_MODEL), F32)
    small = jnp.concatenate([d_mod0[1, 0:2], d_gate0, d_mod1[0, 0:2], d_gate1, d_mod0[0, 0:2], zero, d_nw0, d_nw1, d_gnorm,
                             d_final, d_pscale, d_lb, jnp.broadcast_to(loss_part[0:1, 0:1], (1, D_MODEL)),
                             jnp.zeros((SMALL_ROWS - 17, D_MODEL), F32)], axis=0)
    return {"d_x": d_x, "slots": slots, "pending": pending, "small": small}


def kernel(x, c, ctx, c_ctx, ada_w, ada_b, norm_w, hgrn_w_in, hgrn_lb_logits, hgrn_gnorm_w, hgrn_w_out, pool_w_in, pool_w_grp, pool_scale, pool_w_out, final_norm_w, loss_target, m_c_ctx, m_ada_w, m_ada_b, m_norm_w, m_hgrn_w_in, m_hgrn_lb_logits, m_hgrn_gnorm_w, m_hgrn_w_out, m_pool_w_in, m_pool_w_grp, m_pool_scale, m_pool_w_out, m_final_norm_w, v_c_ctx, v_ada_w, v_ada_b, v_norm_w, v_hgrn_w_in, v_hgrn_lb_logits, v_hgrn_gnorm_w, v_hgrn_w_out, v_pool_w_in, v_pool_w_grp, v_pool_scale, v_pool_w_out, v_final_norm_w):
    xi, yi, ci = _my_place()
    chip = 2 * xi + yi
    dev = 4 * xi + 2 * yi + ci
    ada_cols = ada_w.shape[2]
    lb_cols = hgrn_lb_logits.shape[2]
    ps_cols = pool_scale.shape[1]
    row = lambda a: a.reshape(1, -1)

    def chip_cols(a, n):
        return lax.dynamic_slice_in_dim(a, chip * n, n, axis=a.ndim - 1)

    def from_chips(g, rows_per_dev, take):
        return jnp.concatenate([g[2 * s * rows_per_dev:2 * s * rows_per_dev + take] for s in range(N_CHIPS)], axis=1)

    first = jnp.concatenate([jnp.broadcast_to(c, (8, D_MODEL)), jnp.pad(hgrn_lb_logits[0], ((0, 6), (0, 0))),
                             jnp.pad(pool_scale, ((0, 7), (0, 0)))], axis=1)
    first_all = _all_gather_small(first, "gather_cond")
    cond_all = first_all[::8, :D_MODEL]
    lb_logits = from_chips(first_all[:, D_MODEL:D_MODEL + lb_cols], 8, 2)
    scale_full = from_chips(first_all[:, D_MODEL + lb_cols:], 8, 1)
    cond_rows = jnp.concatenate([cond_all, row(c_ctx), jnp.zeros((7, D_MODEL), F32)], axis=0)

    parts = _mod_parts(cond_rows, ada_w, chip_cols(ada_b, ada_cols))
    parts_all = _all_gather_small(parts.reshape(32, ada_cols), "gather_mod")
    mod_all = from_chips(parts_all, 32, 32).reshape(2, 16, 3 * D_MODEL)
    mod_mine = lax.dynamic_index_in_dim(mod_all, dev, axis=1, keepdims=False)

    shards = {"hgrn_w_in": hgrn_w_in[0], "hgrn_w_out": hgrn_w_out[0], "pool_w_in": pool_w_in[0],
              "pool_w_grp": pool_w_grp[0], "pool_w_out": pool_w_out[0]}
    w_in_full = _all_gather_weights([shards[k].astype(BF16) for k in GATHER_EARLY], GATHER_EARLY)[0]

    loc = _local_step(x[0], ctx[0], loss_target[0], mod_mine, mod_all[0, 8], lb_logits, scale_full, w_in_full,
                      [shards[k].astype(BF16) for k in GATHER_LATE], norm_w, hgrn_gnorm_w, row(final_norm_w))

    small_all = _all_gather_small(loc["small"], "gather_small")
    sums = _sum_devices(small_all)
    loss = sums[16, 0]

    part_c = _cond_ctx_partial(chip_cols(sums[6:9].reshape(1, -1), ada_cols), ada_w[0])
    part_all = _all_gather_small(jnp.pad(part_c.reshape(1, D_MODEL), ((0, 7), (0, 0))), "gather_cond_ctx")

    def reduce_scattered(slots, names, name):
        halves = []
        for k in names:
            hs = _half_shape(k)
            halves.append(_sum_slots(slots[k].reshape(slots[k].shape[0], -1, hs[-1]), "sum_" + k).reshape(hs))
        return dict(zip(names, _exchange_halves(halves, names, name)))

    big_grads = reduce_scattered(loc["slots"], GATHER_LATE, "exchange_halves_late")

    out = {}

    def update(name, w, g, m, v):
        shape = w.shape
        w2, g2, m2, v2 = (a.reshape(-1, shape[-1]) for a in (w, g, m, v))
        d, mn, vn = _adamw(w2, g2, m2, v2, "adamw_" + name)
        out[name] = tuple(a.reshape(shape) for a in (g2, d, mn, vn))

    moments = {"hgrn_w_in": (m_hgrn_w_in, v_hgrn_w_in), "hgrn_w_out": (m_hgrn_w_out, v_hgrn_w_out),
               "pool_w_in": (m_pool_w_in, v_pool_w_in), "pool_w_grp": (m_pool_w_grp, v_pool_w_grp),
               "pool_w_out": (m_pool_w_out, v_pool_w_out)}
    weights = {"hgrn_w_in": hgrn_w_in, "hgrn_w_out": hgrn_w_out, "pool_w_in": pool_w_in, "pool_w_grp": pool_w_grp,
               "pool_w_out": pool_w_out}
    for k in GATHER_LATE:
        update(k, weights[k], big_grads[k], *moments[k])

    g_ada_b = jnp.stack([(sums[0:3] + sums[6:9]).reshape(-1), sums[3:6].reshape(-1)])
    update("ada_b", ada_b, g_ada_b, m_ada_b, v_ada_b)
    update("norm_w", norm_w, sums[9:11], m_norm_w, v_norm_w)
    update("hgrn_gnorm_w", hgrn_gnorm_w, sums[11:12], m_hgrn_gnorm_w, v_hgrn_gnorm_w)
    update("final_norm_w", row(final_norm_w), sums[12:13], row(m_final_norm_w), row(v_final_norm_w))
    update("pool_scale", pool_scale, chip_cols(sums[13:14], ps_cols), m_pool_scale, v_pool_scale)
    update("hgrn_lb_logits", hgrn_lb_logits, chip_cols(sums[14:16], lb_cols)[None], m_hgrn_lb_logits, v_hgrn_lb_logits)

    per_dev = small_all.reshape(N_DEV, SMALL_ROWS, D_MODEL)
    pad7 = jnp.zeros((7, 3 * D_MODEL), F32)
    dm0 = jnp.concatenate([per_dev[:, 0:3].reshape(N_DEV, -1), sums[6:9].reshape(1, -1), pad7], axis=0)
    dm1 = jnp.concatenate([per_dev[:, 3:6].reshape(N_DEV, -1), jnp.zeros((8, 3 * D_MODEL), F32)], axis=0)
    d_mod = chip_cols(jnp.stack([dm0, dm1]), ada_cols)
    out["ada_w"] = _ada_update(cond_rows.T, d_mod, ada_w, m_ada_w, v_ada_w)

    g_c, d_c, m_c, v_c = _cond_ctx_update(part_all, row(c_ctx), row(m_c_ctx), row(v_c_ctx))
    out["c_ctx"] = tuple(a.reshape(-1) for a in (g_c, d_c, m_c, v_c))
    out["final_norm_w"] = tuple(a.reshape(-1) for a in out["final_norm_w"])

    done = [g_c, out["ada_w"][1]] + [out[k][1] for k in GATHER_LATE]
    key = GATHER_EARLY[0]
    _, early = _scatter_wait(*loc["pending"], done, key)
    big_grads = reduce_scattered({key: early}, GATHER_EARLY, "exchange_halves_early")
    for k in GATHER_EARLY:
        update(k, weights[k], big_grads[k], *moments[k])

    names = ["c_ctx", "ada_w", "ada_b", "norm_w", "hgrn_w_in", "hgrn_lb_logits", "hgrn_gnorm_w", "hgrn_w_out", "pool_w_in",
             "pool_w_grp", "pool_scale", "pool_w_out", "final_norm_w"]
    return (loss, loc["d_x"][None], *[out[k][0] for k in names], *[out[k][1] for k in names], *[out[k][2] for k in names],
            *[out[k][3] for k in names])
```

```python
import functools

import numpy as np
import jax
import jax.numpy as jnp
from jax import lax
from jax.experimental import pallas as pl
from jax.experimental.pallas import tpu as pltpu

F32 = jnp.float32
BF16 = jnp.bfloat16

D_MODEL = 1024
SEQ = 2048
CTX_LEN = 256
ROWS_ALL = CTX_LEN + SEQ
HEADS = 8
HEAD_DIM = 128
CHUNK = 64
N_CTX_CHUNKS = CTX_LEN // CHUNK
N_LAT_CHUNKS = SEQ // CHUNK
N_CHUNKS = N_CTX_CHUNKS + N_LAT_CHUNKS
GRID_W = 64
POOL_WINDOWS = (2, 4, 8, 16)
POOL_GROUPS = 4
POOL_GROUP_DIM = 256
HGRN_SECTIONS = 5
POOL_SECTIONS = 2
EPS = 1e-6
N_DEV = 8
N_CHIPS = 4
ROW_TILE = 256
SMALL_ROWS = 24

ADAM_LR = 0.001
ADAM_B1 = 0.9
ADAM_B2 = 0.999
ADAM_EPS = 1e-08
ADAM_WD = 0.01
ADAM_STEP = 10

MESH = pl.DeviceIdType.MESH
MIB = 1 << 20
ANY = pl.BlockSpec(memory_space=pl.ANY)
VMEM = pl.BlockSpec(memory_space=pltpu.VMEM)


def _params(vmem_mib=None):
    if vmem_mib is None:
        return pltpu.CompilerParams()
    return pltpu.CompilerParams(vmem_limit_bytes=vmem_mib * MIB)


def _pin(*operands):
    return [pltpu.with_memory_space_constraint(a, pltpu.HBM) if a.size * a.dtype.itemsize >= MIB else a for a in operands]


def _sig(a):
    return 1.0 / (1.0 + jnp.exp(-a))


def _silu(a):
    return a * _sig(a)


def _dsilu(a):
    s = _sig(a)
    return s * (1.0 + a * (1.0 - s))


def _mm(a, b):
    return jnp.dot(a.astype(BF16), b.astype(BF16), preferred_element_type=F32)


def _mm_nt(a, b):
    return lax.dot_general(a.astype(BF16), b.astype(BF16), (((1,), (1,)), ((), ())), preferred_element_type=F32)


def _mm_tn(a, b):
    return lax.dot_general(a.astype(BF16), b.astype(BF16), (((0,), (0,)), ((), ())), preferred_element_type=F32)


def _split2(a):
    hi = a.astype(BF16)
    lo = (a - hi.astype(F32)).astype(BF16)
    return hi, lo


def _mm_exact_lhs(m_bf, a):
    hi, lo = _split2(a)
    return jnp.dot(m_bf, hi, preferred_element_type=F32) + jnp.dot(m_bf, lo, preferred_element_type=F32)


def _mm_f32(a, b):
    ah, al = _split2(a)
    bh, bl = _split2(b)
    return (jnp.dot(ah, bh, preferred_element_type=F32) + jnp.dot(al, bh, preferred_element_type=F32)
            + jnp.dot(ah, bl, preferred_element_type=F32))


def _my_place():
    return lax.axis_index("x"), lax.axis_index("y"), lax.axis_index("c")


def _small_gather(x_ref, out_ref, send_sems, recv_sems, local_sem):
    m_per = x_ref.shape[0]
    x, y, c = _my_place()
    me, sibling = (x, y, c), (x, y, 1 - c)
    chips = [(1 - x, y), (x, 1 - y), (1 - x, 1 - y)]

    def rows(px, py, pc):
        return out_ref.at[pl.ds((4 * px + 2 * py + pc) * m_per, m_per), :]

    def copy(k, block, to, src=None):
        return pltpu.make_async_remote_copy(
            src_ref=rows(*block) if src is None else src, dst_ref=rows(*block),
            send_sem=send_sems.at[k], recv_sem=recv_sems.at[k], device_id=to, device_id_type=MESH)

    def mine():
        return pltpu.make_async_copy(x_ref, rows(*me), local_sem)

    def first():
        return [copy(0, me, sibling, src=x_ref)] + [copy(1 + j, me, (*chip, c), src=x_ref) for j, chip in enumerate(chips)]

    def start():
        mine().start()
        for cp in first():
            cp.start()

    def finish():
        passed = [copy(4 + j, (*chip, c), sibling) for j, chip in enumerate(chips)]
        for j, chip in enumerate(chips):
            copy(1 + j, (*chip, c), me).wait_recv()
            passed[j].start()
        copy(0, sibling, me).wait_recv()
        for j, chip in enumerate(chips):
            copy(4 + j, (*chip, 1 - c), me).wait_recv()
        for cp in first() + passed:
            cp.wait_send()
        mine().wait()

    return start, finish


def _all_gather_small(blk, name):
    m_per, n = blk.shape

    def body(x_ref, out_ref, send_sems, recv_sems, local_sem):
        start, finish = _small_gather(x_ref, out_ref, send_sems, recv_sems, local_sem)
        start()
        finish()

    return pl.pallas_call(
        body, name=name,
        out_shape=jax.ShapeDtypeStruct((N_DEV * m_per, n), blk.dtype),
        in_specs=[VMEM], out_specs=VMEM,
        scratch_shapes=[pltpu.SemaphoreType.DMA((7,)), pltpu.SemaphoreType.DMA((7,)), pltpu.SemaphoreType.DMA],
    )(blk)


W_SPECS = {
    "hgrn_w_in": ((D_MODEL, 5 * D_MODEL), (1, 1280, 0, 512)),
    "hgrn_w_out": ((D_MODEL, D_MODEL), (0, 256, 0, 128)),
    "pool_w_in": ((D_MODEL, 2 * D_MODEL), (1, 512, 0, 512)),
    "pool_w_grp": ((POOL_GROUPS, POOL_GROUP_DIM, POOL_GROUP_DIM), (1, 64, 1, 32)),
    "pool_w_out": ((D_MODEL, D_MODEL), (0, 256, 0, 128)),
}
W_NAMES = tuple(W_SPECS)


def _al(v, m):
    return pl.multiple_of(v, m)


def _region(ref, spec, chip, half):
    ca, cn, ha, hn = spec
    idx = [slice(None)] * len(ref.shape)
    if ca == ha:
        if half is None:
            idx[ca] = pl.ds(_al(chip * cn, cn), cn)
        else:
            idx[ca] = pl.ds(_al(chip * cn + half * hn, hn), hn)
    else:
        idx[ca] = pl.ds(_al(chip * cn, cn), cn)
        if half is not None:
            idx[ha] = pl.ds(_al(half * hn, hn), hn)
    return ref.at[tuple(idx)]


def _half_of(ref, spec, half):
    _, _, ha, hn = spec
    idx = [slice(None)] * len(ref.shape)
    idx[ha] = pl.ds(_al(half * hn, hn), hn)
    return ref.at[tuple(idx)]


PIECE_BYTES = 256 * 1024


def _pieces(ref):
    lead = ref.shape[0]
    want = (int(np.prod(ref.shape)) * ref.dtype.itemsize) // PIECE_BYTES
    n = max([1] + [k for k in range(1, want + 1) if lead % k == 0 and (lead // k) % 16 == 0])
    rows = lead // n
    return [ref.at[pl.ds(i * rows, rows)] for i in range(n)]


def _half_shape(name):
    full, (ca, cn, ha, hn) = W_SPECS[name]
    shp = list(full)
    shp[ca] = cn
    shp[ha] = hn
    return tuple(shp)


def _gather_direct(names, sh, full, send_sems, recv_sems, local_sems):
    specs = [W_SPECS[k][1] for k in names]
    x, y, c = _my_place()
    chip_me = 2 * x + y
    chips = [(1 - x, y), (x, 1 - y), (1 - x, 1 - y)]

    def local(a):
        return pltpu.make_async_copy(sh[a], _region(full[a], specs[a], chip_me, None), local_sems.at[a])

    def remote(a, src, dst, to):
        return pltpu.make_async_remote_copy(src_ref=src, dst_ref=dst, send_sem=send_sems.at[a], recv_sem=recv_sems.at[a],
                                            device_id=to, device_id_type=MESH)

    def start():
        for a in range(len(names)):
            local(a).start()
            for px, py in chips:
                remote(a, sh[a], _region(full[a], specs[a], chip_me, None), (px, py, c)).start()

    def wait():
        for a in range(len(names)):
            ca, cn, _, _ = specs[a]
            idx = [slice(None)] * len(full[a].shape)
            idx[ca] = pl.ds(0, 3 * cn)
            three = full[a].at[tuple(idx)]
            remote(a, three, three, (x, y, c)).wait()
            local(a).wait()

    return start, wait


def _gather_two_level(names, sh, land, full, send_sems, recv_sems, local_sems):
    n = len(names)
    specs = [W_SPECS[k][1] for k in names]
    x, y, c = _my_place()
    chip_me = 2 * x + y
    sibling = (x, y, 1 - c)
    chips = [(1 - x, y), (x, 1 - y), (1 - x, 1 - y)]

    def remote(k, src, dst, to):
        return pltpu.make_async_remote_copy(src_ref=src, dst_ref=dst, send_sem=send_sems.at[k], recv_sem=recv_sems.at[k],
                                            device_id=to, device_id_type=MESH)

    def three_halves(a):
        ca, cn, ha, hn = specs[a]
        idx = [slice(None)] * len(land[a].shape)
        if ca == ha:
            idx[ca] = pl.ds(0, 3 * hn)
        else:
            idx[ca], idx[ha] = pl.ds(0, 3 * cn), pl.ds(0, hn)
        return land[a].at[tuple(idx)]

    def own(a):
        return pltpu.make_async_copy(sh[a], _region(land[a], specs[a], chip_me, None), local_sems.at[a])

    def out(a):
        return pltpu.make_async_copy(land[a], full[a], local_sems.at[a])

    def start():
        for a in range(n):
            own(a).start()
            for px, py in chips:
                remote(a, _half_of(sh[a], specs[a], c), _region(land[a], specs[a], chip_me, c), (px, py, c)).start()

    def forward():
        for a in range(n):
            remote(a, three_halves(a), three_halves(a), sibling).wait_recv()
            for px, py in chips:
                landed = _region(land[a], specs[a], 2 * px + py, c)
                remote(n + a, landed, landed, sibling).start()

    def finish():
        for a in range(n):
            remote(n + a, three_halves(a), three_halves(a), sibling).wait_recv()
            remote(a, three_halves(a), three_halves(a), sibling).wait_send()
            remote(n + a, three_halves(a), three_halves(a), sibling).wait_send()
            own(a).wait()
        for a in range(n):
            out(a).start()
        for a in range(n):
            out(a).wait()

    return start, forward, finish


def _scatter_direct(names, part, slots, send_sems, recv_sems, local_sems):
    specs = [W_SPECS[k][1] for k in names]
    x, y, c = _my_place()
    dev_me = 4 * x + 2 * y + c

    def local(a):
        return pltpu.make_async_copy(_region(part[a], specs[a], 2 * x + y, c), slots[a].at[dev_me], local_sems.at[a])

    def start():
        for a in range(len(names)):
            local(a).start()
            for flip in range(1, N_DEV):
                tx = 1 - x if flip >> 2 else x
                ty = 1 - y if (flip >> 1) & 1 else y
                tc = 1 - c if flip & 1 else c
                pltpu.make_async_remote_copy(src_ref=_region(part[a], specs[a], 2 * tx + ty, tc), dst_ref=slots[a].at[dev_me],
                                             send_sem=send_sems.at[a], recv_sem=recv_sems.at[a], device_id=(tx, ty, tc),
                                             device_id_type=MESH).start()

    def wait():
        for a in range(len(names)):
            seven = slots[a].at[pl.ds(0, N_DEV - 1)]
            pltpu.make_async_remote_copy(src_ref=seven, dst_ref=seven, send_sem=send_sems.at[a], recv_sem=recv_sems.at[a],
                                         device_id=(x, y, c), device_id_type=MESH).wait()
            local(a).wait()

    return start, wait


HBM_SPEC = pl.BlockSpec(memory_space=pltpu.HBM)
SEM_SPEC = pl.BlockSpec(memory_space=pltpu.SEMAPHORE)
SPLIT_EFFECT = pltpu.SideEffectType.DATAFLOW_SIDE_EFFECTING


def _scatter_start(part, slots, after, name_key):
    _, cols, _, _ = W_SPECS[name_key][1]

    def body(part_ref, slots_ref, after_ref, send_sem, recv_sem, part_thru, slots_thru, token):
        x, y, c = _my_place()
        for tx, ty in ((1 - x, y), (x, 1 - y), (1 - x, 1 - y)):
            src = part_ref.at[:, pl.ds(_al((2 * tx + ty) * cols, cols), cols)]
            pltpu.make_async_remote_copy(src_ref=src, dst_ref=slots_ref.at[2 * x + y], send_sem=send_sem, recv_sem=recv_sem,
                                         device_id=(tx, ty, c), device_id_type=MESH).start()
        token[...] = jnp.zeros_like(token)

    return pl.pallas_call(
        body, name="scatter_start_" + name_key,
        out_shape=(pltpu.SemaphoreType.DMA(()), pltpu.SemaphoreType.DMA(()), pltpu.HBM(part.shape, part.dtype),
                   pltpu.HBM(slots.shape, slots.dtype), jax.ShapeDtypeStruct((8, 128), F32)),
        in_specs=(HBM_SPEC, HBM_SPEC, ANY), out_specs=(SEM_SPEC, SEM_SPEC, HBM_SPEC, HBM_SPEC, VMEM),
        input_output_aliases={0: 2, 1: 3},
        compiler_params=pltpu.CompilerParams(has_side_effects=SPLIT_EFFECT),
    )(pltpu.with_memory_space_constraint(part, pltpu.HBM), pltpu.with_memory_space_constraint(slots, pltpu.HBM), after)


def _scatter_wait(send_sem, recv_sem, part_thru, slots_thru, after, name_key):
    def body(part_ref, slots_ref, send_sem, recv_sem, *rest):
        x, y, c = _my_place()
        landed = slots_ref.at[pl.ds(0, N_CHIPS - 1)]
        copy = pltpu.make_async_remote_copy(src_ref=landed, dst_ref=landed, send_sem=send_sem, recv_sem=recv_sem,
                                            device_id=(x, y, c), device_id_type=MESH)
        copy.wait_send()
        copy.wait_recv()

    return pl.pallas_call(
        body, name="scatter_wait_" + name_key,
        out_shape=(pltpu.HBM(part_thru.shape, part_thru.dtype), pltpu.HBM(slots_thru.shape, slots_thru.dtype)),
        in_specs=(HBM_SPEC, HBM_SPEC, SEM_SPEC, SEM_SPEC) + (ANY,) * len(after), out_specs=(HBM_SPEC, HBM_SPEC),
        input_output_aliases={0: 0, 1: 1},
        compiler_params=pltpu.CompilerParams(has_side_effects=SPLIT_EFFECT),
    )(part_thru, slots_thru, send_sem, recv_sem, *after)


def _comm_sems(n):
    return [pltpu.SemaphoreType.DMA((n,)), pltpu.SemaphoreType.DMA((n,)), pltpu.SemaphoreType.DMA((n,))]


GATHER_EARLY = ("hgrn_w_in",)
GATHER_LATE = ("hgrn_w_out", "pool_w_in", "pool_w_grp", "pool_w_out")


def _all_gather_weights(shards, names):
    n = len(names)
    specs = [W_SPECS[k][1] for k in names]

    def body(*refs):
        sh, full = refs[:n], refs[n:2 * n]
        send_sems, recv_sems, local_sems = refs[2 * n:]
        x, y, c = _my_place()
        chip_me = 2 * x + y
        sibling = (x, y, 1 - c)
        chips = [(1 - x, y), (x, 1 - y), (1 - x, 1 - y)]

        def remote(a, k, src, dst, to):
            return pltpu.make_async_remote_copy(src_ref=src, dst_ref=dst, send_sem=send_sems.at[6 * a + k],
                                                recv_sem=recv_sems.at[6 * a + k], device_id=to, device_id_type=MESH)

        local = [pltpu.make_async_copy(sh[a], _region(full[a], specs[a], chip_me, None), local_sems.at[a]) for a in range(n)]
        for cp in local:
            cp.start()
        sends = []
        for a in range(n):
            for j, (px, py) in enumerate(chips):
                src, dst = _half_of(sh[a], specs[a], c), _region(full[a], specs[a], chip_me, c)
                for s_piece, d_piece in zip(_pieces(src), _pieces(dst)):
                    remote(a, j, s_piece, d_piece, (px, py, c)).start()
                sends.append(remote(a, j, src, dst, (px, py, c)))
        for a in range(n):
            for j, (px, py) in enumerate(chips):
                landed = _region(full[a], specs[a], 2 * px + py, c)
                remote(a, j, landed, landed, (px, py, c)).wait_recv()
                for piece in _pieces(landed):
                    remote(a, 3 + j, piece, piece, sibling).start()
                sends.append(remote(a, 3 + j, landed, landed, sibling))
        for a in range(n):
            for j, (px, py) in enumerate(chips):
                other = _region(full[a], specs[a], 2 * px + py, 1 - c)
                remote(a, 3 + j, other, other, sibling).wait_recv()
        for cp in sends:
            cp.wait_send()
        for cp in local:
            cp.wait()

    return pl.pallas_call(
        body, name="all_gather_weights",
        out_shape=[jax.ShapeDtypeStruct(W_SPECS[k][0], BF16) for k in names],
        in_specs=[VMEM] * n, out_specs=[VMEM] * n,
        scratch_shapes=[pltpu.SemaphoreType.DMA((6 * n,)), pltpu.SemaphoreType.DMA((6 * n,)), pltpu.SemaphoreType.DMA((n,))],
        compiler_params=_params(32),
    )(*shards)


def _weight_gather(sh, full, spec, send_sems, recv_sems, local_sem):
    x, y, c = _my_place()
    chip_me = 2 * x + y
    sibling = (x, y, 1 - c)
    chips = [(1 - x, y), (x, 1 - y), (1 - x, 1 - y)]

    def remote(k, src, dst, to):
        return pltpu.make_async_remote_copy(src_ref=src, dst_ref=dst, send_sem=send_sems.at[k], recv_sem=recv_sems.at[k],
                                            device_id=to, device_id_type=MESH)

    def own():
        return pltpu.make_async_copy(sh, _region(full, spec, chip_me, None), local_sem)

    def direct():
        return [remote(j, _half_of(sh, spec, c), _region(full, spec, chip_me, c), (px, py, c)) for j, (px, py) in enumerate(chips)]

    def start():
        own().start()
        for cp in direct():
            cp.start()

    def finish():
        passed = []
        for j, (px, py) in enumerate(chips):
            landed = _region(full, spec, 2 * px + py, c)
            remote(j, landed, landed, (px, py, c)).wait_recv()
            passed.append(remote(3 + j, landed, landed, sibling))
            passed[-1].start()
        for j, (px, py) in enumerate(chips):
            other = _region(full, spec, 2 * px + py, 1 - c)
            remote(3 + j, other, other, sibling).wait_recv()
        for cp in direct() + passed:
            cp.wait_send()
        own().wait()

    return start, finish


def _prologue(first, c_ctx, ada_w, ada_b_cols, w_shard):
    n_layers, _, n_cols = ada_w.shape
    key = GATHER_EARLY[0]
    spec = W_SPECS[key][1]
    m_first = first.shape[0]

    def body(first_ref, cctx_ref, adaw_ref, adab_ref, sh_ref, first_all, parts_all, full_ref, parts_scr,
             s1, r1, l1, s2, r2, l2, ws, wr, wl):
        start_first, finish_first = _small_gather(first_ref, first_all, s1, r1, l1)
        start_parts, finish_parts = _small_gather(parts_scr, parts_all, s2, r2, l2)
        start_weight, finish_weight = _weight_gather(sh_ref, full_ref, spec, ws, wr, wl)
        start_first()
        start_weight()
        finish_first()
        cond = jnp.concatenate([first_all[m_first * d:m_first * d + 1, 0:D_MODEL] for d in range(N_DEV)]
                               + [cctx_ref[...], jnp.zeros((16 - N_DEV - 1, D_MODEL), F32)], axis=0)
        act = _silu(cond)
        for i in range(n_layers):
            parts_scr[16 * i:16 * (i + 1), :] = _mm_f32(act, adaw_ref[i]) + adab_ref[i]
        start_parts()
        finish_parts()
        finish_weight()

    seven = [pltpu.SemaphoreType.DMA((7,)), pltpu.SemaphoreType.DMA((7,)), pltpu.SemaphoreType.DMA]
    return pl.pallas_call(
        body, name="prologue",
        out_shape=[jax.ShapeDtypeStruct((N_DEV * m_first, first.shape[1]), F32),
                   jax.ShapeDtypeStruct((N_DEV * 16 * n_layers, n_cols), F32), jax.ShapeDtypeStruct(W_SPECS[key][0], BF16)],
        in_specs=[VMEM] * 5, out_specs=[VMEM] * 3,
        scratch_shapes=[pltpu.VMEM((16 * n_layers, n_cols), F32)] + seven + seven
        + [pltpu.SemaphoreType.DMA((6,)), pltpu.SemaphoreType.DMA((6,)), pltpu.SemaphoreType.DMA],
        compiler_params=_params(48),
    )(first, c_ctx, ada_w, ada_b_cols.reshape(n_layers, 1, n_cols), w_shard)


def _slot_shapes(names):
    return [jax.ShapeDtypeStruct((N_DEV,) + _half_shape(k), BF16) for k in names]


def _scatter_grads(parts, names):
    n = len(names)

    def body(*refs):
        start, wait = _scatter_direct(names, refs[:n], refs[n:2 * n], *refs[2 * n:])
        start()
        wait()

    return pl.pallas_call(body, name="scatter_grads", out_shape=_slot_shapes(names), in_specs=[ANY] * n, out_specs=[ANY] * n,
                          scratch_shapes=_comm_sems(n))(*parts)


def _exchange_halves(halves, names, name):
    n = len(names)
    specs = [W_SPECS[k][1] for k in names]

    def shard_shape(k):
        shp = list(_half_shape(k))
        shp[W_SPECS[k][1][2]] *= 2
        return tuple(shp)

    def body(*refs):
        half, out = refs[:n], refs[n:2 * n]
        send_sems, recv_sems, local_sems = refs[2 * n:]
        x, y, c = _my_place()
        sibling = (x, y, 1 - c)

        def remote(a, src, dst):
            return pltpu.make_async_remote_copy(src_ref=src, dst_ref=dst, send_sem=send_sems.at[a], recv_sem=recv_sems.at[a],
                                                device_id=sibling, device_id_type=MESH)

        local = [pltpu.make_async_copy(half[a], _half_of(out[a], specs[a], c), local_sems.at[a]) for a in range(n)]
        for cp in local:
            cp.start()
        for a in range(n):
            mine = _half_of(out[a], specs[a], c)
            for src, dst in zip(_pieces(half[a]), _pieces(mine)):
                remote(a, src, dst).start()
        for a in range(n):
            theirs = _half_of(out[a], specs[a], 1 - c)
            remote(a, theirs, theirs).wait_recv()
        for a in range(n):
            remote(a, half[a], half[a]).wait_send()
        for cp in local:
            cp.wait()

    return pl.pallas_call(
        body, name=name,
        out_shape=[jax.ShapeDtypeStruct(shard_shape(k), F32) for k in names],
        in_specs=[VMEM] * n, out_specs=[VMEM] * n,
        scratch_shapes=[pltpu.SemaphoreType.DMA((n,)), pltpu.SemaphoreType.DMA((n,)), pltpu.SemaphoreType.DMA((n,))],
        compiler_params=_params(32),
    )(*halves)


def _mod_parts(c_rows, ada_w, ada_b_cols):
    n_layers, _, n_cols = ada_w.shape

    def body(c_ref, w_ref, b_ref, o_ref):
        o_ref[...] = _mm_f32(_silu(c_ref[...]), w_ref[...]) + b_ref[...]

    return pl.pallas_call(
        body, name="mod_parts", grid=(n_layers,),
        out_shape=jax.ShapeDtypeStruct((n_layers, 16, n_cols), F32),
        in_specs=[pl.BlockSpec((16, D_MODEL), lambda i: (0, 0)),
                  pl.BlockSpec((None, D_MODEL, n_cols), lambda i: (i, 0, 0)),
                  pl.BlockSpec((None, 1, n_cols), lambda i: (i, 0, 0))],
        out_specs=pl.BlockSpec((None, 16, n_cols), lambda i: (i, 0, 0)),
        compiler_params=_params(40),
    )(c_rows, ada_w, ada_b_cols.reshape(n_layers, 1, n_cols))


def _ln_mod_matmul(ctx_tile, xin, nw, shift, scale, w, name):
    n_mod = shift.shape[0]
    tm = ROW_TILE
    skip = n_mod - 1
    rows = xin.shape[0] + skip * tm
    n_sec = w.shape[1] // D_MODEL

    def body(*refs):
        c_ref = refs[0] if skip else None
        x_ref, nw_ref, sh_ref, sc_ref, w_ref, g_ref, h_ref = refs[skip:]
        xv = jnp.where(pl.program_id(0) == 0, c_ref[...], x_ref[...]) if skip else x_ref[...]
        r = lax.rsqrt(jnp.mean(xv * xv, axis=-1, keepdims=True) + EPS)
        h_f32 = (xv * r * nw_ref[...]) * (1.0 + sc_ref[...]) + sh_ref[...]
        h = h_f32.astype(BF16)
        h_ref[...] = h
        for k in range(n_sec):
            g_ref[k] = jnp.dot(h, w_ref[:, k * D_MODEL:(k + 1) * D_MODEL], preferred_element_type=F32)

    mod_spec = pl.BlockSpec((None, 1, D_MODEL), lambda i: (jnp.minimum(i, n_mod - 1), 0, 0))
    return pl.pallas_call(
        body, name=name, grid=(rows // tm,),
        out_shape=[pltpu.HBM((n_sec, rows, D_MODEL), F32), pltpu.HBM((rows, D_MODEL), BF16)],
        in_specs=[pl.BlockSpec((tm, D_MODEL), lambda i: (0, 0))] * skip
        + [pl.BlockSpec((tm, D_MODEL), lambda i: (jnp.maximum(i - skip, 0), 0)),
           pl.BlockSpec((1, D_MODEL), lambda i: (0, 0)),
           mod_spec, mod_spec,
           pl.BlockSpec((D_MODEL, n_sec * D_MODEL), lambda i: (0, 0))],
        out_specs=[pl.BlockSpec((n_sec, tm, D_MODEL), lambda i: (0, i, 0)),
                   pl.BlockSpec((tm, D_MODEL), lambda i: (i, 0))],
        compiler_params=_params(48),
    )(*_pin(*([ctx_tile] * skip), xin, nw, shift.reshape(n_mod, 1, D_MODEL), scale.reshape(n_mod, 1, D_MODEL), w))


def _chunk_masks(rev):
    rid = lax.broadcasted_iota(jnp.int32, (CHUNK, CHUNK), 0)
    cid = lax.broadcasted_iota(jnp.int32, (CHUNK, CHUNK), 1)
    keep = (cid >= rid) if rev else (cid <= rid)
    keep_t = (cid <= rid) if rev else (cid >= rid)
    one, zero = jnp.ones((CHUNK, CHUNK), F32), jnp.zeros((CHUNK, CHUNK), F32)
    return keep, jnp.where(keep, one, zero).astype(BF16), jnp.where(keep_t, one, zero).astype(BF16)


def _chunk_rows(t, rev, latent):
    n = N_LAT_CHUNKS if latent else N_CTX_CHUNKS
    base = CTX_LEN if latent else 0
    idx = (n - 1 - t) if rev else t
    return pl.multiple_of(base + idx * CHUNK, CHUNK)


def _gates(fpre, lb):
    sg = _sig(fpre)
    f = lb + (1.0 - lb) * sg
    return sg, f, 1.0 - f, jnp.log(f)


G_SPEC = lambda sec: pl.BlockSpec((None, ROWS_ALL, HEAD_DIM), lambda h, sec=sec: (sec, 0, h))


def _gla_forward(g5, lb_logits):
    q_scale = HEAD_DIM ** -0.5

    def body(ff_ref, fb_ref, v_ref, q_ref, lg_ref, o_ref, st_ref, decay_ref, qt_ref):
        for rev in (False, True):
            f_ref = fb_ref if rev else ff_ref
            lb = _sig(lg_ref[1:2, :] if rev else lg_ref[0:1, :])
            keep, tri, _ = _chunk_masks(rev)
            last = 0 if rev else CHUNK - 1
            mid = CHUNK // 2 if rev else CHUNK // 2 - 1

            def local_step(t, carry, latent):
                r0 = _chunk_rows(t, rev, latent)
                rows = pl.ds(r0, CHUNK)
                step = t + (N_CTX_CHUNKS if latent else 0)
                _, _, k, lf = _gates(f_ref[rows, :], lb)
                v = v_ref[rows, :]
                b = _mm_exact_lhs(tri, lf)
                bl = b[last:last + 1, :]
                if latent:
                    q = _silu(q_ref[rows, :]) * q_scale
                    bm = b[mid:mid + 1, :]
                    a = _mm_nt(q * jnp.exp(b - bm), k * jnp.exp(bm - b))
                    o = _mm(jnp.where(keep, a, 0.0), v)
                    orow = pl.ds(pl.multiple_of(r0 - CTX_LEN, CHUNK), CHUNK)
                    qt_ref[orow, :] = (q * jnp.exp(b)).astype(BF16)
                    if rev:
                        o_ref[orow, :] += o
                    else:
                        o_ref[orow, :] = o
                decay_ref[step] = jnp.exp(bl)
                st_ref[step] = _mm_tn(v, k * jnp.exp(bl - b))
                return carry

            lax.fori_loop(0, N_CTX_CHUNKS, functools.partial(local_step, latent=False), 0, unroll=2)
            lax.fori_loop(0, N_LAT_CHUNKS, functools.partial(local_step, latent=True), 0, unroll=4)

            def scan_step(t, st):
                update = st_ref[t]
                st_ref[t] = st
                return st * decay_ref[t] + update

            lax.fori_loop(0, N_CHUNKS, scan_step, jnp.zeros((HEAD_DIM, HEAD_DIM), F32), unroll=2)

            def inter_step(t, carry):
                r0 = _chunk_rows(t, rev, True)
                orow = pl.ds(pl.multiple_of(r0 - CTX_LEN, CHUNK), CHUNK)
                o_ref[orow, :] += lax.dot_general(qt_ref[orow, :], st_ref[t + N_CTX_CHUNKS].astype(BF16),
                                                  (((1,), (1,)), ((), ())), preferred_element_type=F32)
                return carry

            lax.fori_loop(0, N_LAT_CHUNKS, inter_step, 0, unroll=4)

    return pl.pallas_call(
        body, name="gla_forward", grid=(HEADS,),
        out_shape=jax.ShapeDtypeStruct((SEQ, D_MODEL), F32),
        in_specs=[G_SPEC(0), G_SPEC(1), G_SPEC(2), G_SPEC(3), pl.BlockSpec((2, HEAD_DIM), lambda h: (0, h))],
        out_specs=pl.BlockSpec((SEQ, HEAD_DIM), lambda h: (0, h)),
        scratch_shapes=[pltpu.VMEM((N_CHUNKS, HEAD_DIM, HEAD_DIM), F32), pltpu.VMEM((N_CHUNKS, 1, HEAD_DIM), F32),
                        pltpu.VMEM((SEQ, HEAD_DIM), BF16)],
        compiler_params=_params(32),
    )(g5, g5, g5, g5, lb_logits)


def _gla_backward(g5, lb_logits, d_o, d_z):
    q_scale = HEAD_DIM ** -0.5

    def body(ff_ref, fb_ref, v_ref, q_ref, lg_ref, do_ref, dz_ref, dg_ref, dlg_ref, st_ref, dst_ref, decay_ref):
        dg_ref[3, 0:CTX_LEN, :] = jnp.zeros((CTX_LEN, HEAD_DIM), F32)
        dg_ref[4, 0:CTX_LEN, :] = jnp.zeros((CTX_LEN, HEAD_DIM), F32)
        dg_ref[4, CTX_LEN:ROWS_ALL, :] = dz_ref[...]
        is_row = lax.broadcasted_iota(jnp.int32, (CHUNK, HEAD_DIM), 0)
        for rev in (False, True):
            d = 1 if rev else 0
            f_ref = fb_ref if rev else ff_ref
            lb = _sig(lg_ref[d:d + 1, :])
            keep, tri, tri_t = _chunk_masks(rev)
            last = 0 if rev else CHUNK - 1
            mid = CHUNK // 2 if rev else CHUNK // 2 - 1

            def local_step(t, carry, latent):
                r0 = _chunk_rows(t, rev, latent)
                rows = pl.ds(r0, CHUNK)
                step = t + (N_CTX_CHUNKS if latent else 0)
                _, _, k, lf = _gates(f_ref[rows, :], lb)
                b = _mm_exact_lhs(tri, lf)
                bl = b[last:last + 1, :]
                decay_ref[step] = jnp.exp(bl)
                st_ref[step] = _mm_tn(v_ref[rows, :], k * jnp.exp(bl - b))
                if latent:
                    q_t = _silu(q_ref[rows, :]) * q_scale * jnp.exp(b)
                    dst_ref[step] = _mm_tn(do_ref[pl.ds(pl.multiple_of(r0 - CTX_LEN, CHUNK), CHUNK), :], q_t)
                else:
                    dst_ref[step] = jnp.zeros((HEAD_DIM, HEAD_DIM), F32)
                return carry

            lax.fori_loop(0, N_CTX_CHUNKS, functools.partial(local_step, latent=False), 0, unroll=2)
            lax.fori_loop(0, N_LAT_CHUNKS, functools.partial(local_step, latent=True), 0, unroll=4)

            def scan_step(i, carry):
                st, d_st = carry
                j = N_CHUNKS - 1 - i
                update, d_update = st_ref[i], dst_ref[j]
                st_ref[i] = st
                dst_ref[j] = d_st
                return st * decay_ref[i] + update, d_st * decay_ref[j] + d_update

            zero_state = jnp.zeros((HEAD_DIM, HEAD_DIM), F32)
            lax.fori_loop(0, N_CHUNKS, scan_step, (zero_state, zero_state))

            def grad_step(t, dlb, latent):
                r0 = _chunk_rows(t, rev, latent)
                rows = pl.ds(r0, CHUNK)
                step = t + (N_CTX_CHUNKS if latent else 0)
                sg, f, k, lf = _gates(f_ref[rows, :], lb)
                v = v_ref[rows, :]
                b = _mm_exact_lhs(tri, lf)
                bl = b[last:last + 1, :]
                e_end = jnp.exp(bl - b)
                k_end = k * e_end
                decay = jnp.exp(bl)
                d_st = dst_ref[step]
                st_prev = st_ref[step]
                d_kend = _mm(v, d_st)
                d_decay = jnp.sum(d_st * st_prev, axis=0, keepdims=True)
                t_kend = d_kend * k_end
                d_v = _mm_nt(k_end, d_st)
                d_k = d_kend * e_end
                d_b = -t_kend
                if latent:
                    qpre = q_ref[rows, :]
                    q = _silu(qpre) * q_scale
                    bm = b[mid:mid + 1, :]
                    e_b, e_qm, e_km = jnp.exp(b), jnp.exp(b - bm), jnp.exp(bm - b)
                    q_t, q_m, k_m = q * e_b, q * e_qm, k * e_km
                    a = jnp.where(keep, _mm_nt(q_m, k_m), 0.0)
                    d_out = do_ref[pl.ds(pl.multiple_of(r0 - CTX_LEN, CHUNK), CHUNK), :]
                    d_a = jnp.where(keep, _mm_nt(d_out, v), 0.0)
                    d_qm = _mm(d_a, k_m)
                    d_km = _mm_tn(d_a, q_m)
                    d_qt = _mm(d_out, st_prev)
                    d_v = d_v + _mm_tn(a, d_out)
                    d_k = d_k + d_km * e_km
                    d_b = d_b + d_qt * q_t + d_qm * q_m - d_km * k_m
                    d_q = d_qt * e_b + d_qm * e_qm
                at_last = jnp.sum(t_kend, axis=0, keepdims=True) + d_decay * decay
                d_b = d_b + jnp.where(is_row == last, at_last, 0.0)
                d_lf = _mm_exact_lhs(tri_t, d_b)
                d_f = d_lf / f - d_k
                dg_ref[d, rows, :] = d_f * (1.0 - lb) * sg * (1.0 - sg)
                if rev:
                    dg_ref[2, rows, :] += d_v
                else:
                    dg_ref[2, rows, :] = d_v
                if latent:
                    d_qpre = d_q * q_scale * _dsilu(qpre)
                    if rev:
                        dg_ref[3, rows, :] += d_qpre
                    else:
                        dg_ref[3, rows, :] = d_qpre
                return dlb + jnp.sum(d_f * (1.0 - sg), axis=0, keepdims=True)

            dlb = lax.fori_loop(0, N_LAT_CHUNKS, functools.partial(grad_step, latent=True), jnp.zeros((1, HEAD_DIM), F32),
                                unroll=2)
            dlb = lax.fori_loop(0, N_CTX_CHUNKS, functools.partial(grad_step, latent=False), dlb, unroll=2)
            dlg_ref[d:d + 1, :] = dlb * lb * (1.0 - lb)

    col = pl.BlockSpec((SEQ, HEAD_DIM), lambda h: (0, h))
    return pl.pallas_call(
        body, name="gla_backward", grid=(HEADS,),
        out_shape=[jax.ShapeDtypeStruct((HGRN_SECTIONS, ROWS_ALL, D_MODEL), F32), jax.ShapeDtypeStruct((2, D_MODEL), F32)],
        in_specs=[G_SPEC(0), G_SPEC(1), G_SPEC(2), G_SPEC(3), pl.BlockSpec((2, HEAD_DIM), lambda h: (0, h)), col, col],
        out_specs=[pl.BlockSpec((HGRN_SECTIONS, ROWS_ALL, HEAD_DIM), lambda h: (0, 0, h)),
                   pl.BlockSpec((2, HEAD_DIM), lambda h: (0, h))],
        scratch_shapes=[pltpu.VMEM((N_CHUNKS, HEAD_DIM, HEAD_DIM), F32), pltpu.VMEM((N_CHUNKS, HEAD_DIM, HEAD_DIM), F32),
                        pltpu.VMEM((N_CHUNKS, 1, HEAD_DIM), F32)],
        compiler_params=_params(48),
    )(g5, g5, g5, g5, lb_logits, d_o, d_z)


GROUP = 2 * CHUNK


def _group_masks(rev):
    rid = lax.broadcasted_iota(jnp.int32, (GROUP, GROUP), 0)
    cid = lax.broadcasted_iota(jnp.int32, (GROUP, GROUP), 1)
    same = (rid >= CHUNK) == (cid >= CHUNK)
    causal = (cid >= rid) if rev else (cid <= rid)
    anti = (cid <= rid) if rev else (cid >= rid)
    bf = lambda m: jnp.where(m, jnp.ones((GROUP, GROUP), F32), jnp.zeros((GROUP, GROUP), F32)).astype(BF16)
    keep = same & causal
    return keep, bf(keep), bf(same & anti), bf(same & jnp.logical_not(anti))


def _group_sum(m_bf, a):
    hi, lo = _split2(a)
    r = jnp.dot(m_bf, jnp.concatenate([hi, lo], axis=1), preferred_element_type=F32)
    return r[:, :HEAD_DIM] + r[:, HEAD_DIM:]


def _chunk_row(a, pos):
    return jnp.concatenate([jnp.broadcast_to(a[c * CHUNK + pos:c * CHUNK + pos + 1, :], (CHUNK, HEAD_DIM)) for c in range(2)], axis=0)


def _by_chunk(a, second):
    return jnp.concatenate([jnp.where(second, 0.0, a), jnp.where(second, a, 0.0)], axis=1)


def _own_block(r):
    return jnp.concatenate([r[0:CHUNK, 0:HEAD_DIM], r[CHUNK:GROUP, HEAD_DIM:2 * HEAD_DIM]], axis=0)


def _scan_step_of(row_chunk, rev, latent):
    if not rev:
        return row_chunk
    return (N_CHUNKS + N_CTX_CHUNKS - 1 - row_chunk) if latent else (N_CTX_CHUNKS - 1 - row_chunk)


def _group_rows(i, j, per_step, latent):
    base = CTX_LEN if latent else 0
    return pl.multiple_of(base + (i * per_step + j) * GROUP, GROUP)


GROUPS_PER_STEP = 8
GROUPS_PER_GRAD_STEP = 4


def _gla_forward(g5, lb_logits, late_shards):
    q_scale = HEAD_DIM ** -0.5
    per_lat, per_ctx = GROUPS_PER_STEP, min(GROUPS_PER_STEP, CTX_LEN // GROUP)
    n_late = len(GATHER_LATE)

    def body(ff_ref, fb_ref, v_ref, q_ref, lg_ref, *rest):
        shard_refs, o_ref, full_refs = rest[:n_late], rest[n_late], rest[n_late + 1:2 * n_late + 1]
        st_ref, decay_ref, qt_ref = rest[2 * n_late + 1:2 * n_late + 4]
        land_refs = rest[2 * n_late + 4:3 * n_late + 4]
        start_gather, forward_gather, finish_gather = _gather_two_level(GATHER_LATE, shard_refs, land_refs, full_refs,
                                                                        *rest[3 * n_late + 4:])

        @pl.when(pl.program_id(0) == 0)
        def _():
            start_gather()

        @pl.when(pl.program_id(0) == HEADS - 1)
        def _():
            forward_gather()

        second = lax.broadcasted_iota(jnp.int32, (GROUP, HEAD_DIM), 0) >= CHUNK
        for rev in (False, True):
            f_ref = fb_ref if rev else ff_ref
            lb = _sig(lg_ref[1:2, :] if rev else lg_ref[0:1, :])
            keep, tri, _, _ = _group_masks(rev)
            last = 0 if rev else CHUNK - 1
            mid = CHUNK // 2 if rev else CHUNK // 2 - 1

            def local_step(i, carry, latent, per):
                r0s = [_group_rows(i, j, per, latent) for j in range(per)]
                rows = [pl.ds(r0, GROUP) for r0 in r0s]
                gates = [_gates(f_ref[r, :], lb) for r in rows]
                vs = [v_ref[r, :] for r in rows]
                bs = [_group_sum(tri, g[3]) for g in gates]
                bls = [_chunk_row(b, last) for b in bs]
                ups = [_mm_tn(v, _by_chunk(g[2] * jnp.exp(bl - b), second)) for v, g, b, bl in zip(vs, gates, bs, bls)]
                if latent:
                    qs = [_silu(q_ref[r, :]) * q_scale for r in rows]
                    bms = [_chunk_row(b, mid) for b in bs]
                    a_s = [_mm_nt(q * jnp.exp(b - bm), g[2] * jnp.exp(bm - b)) for q, g, b, bm in zip(qs, gates, bs, bms)]
                    outs = [_mm(jnp.where(keep, a, 0.0), v) for a, v in zip(a_s, vs)]
                for j in range(per):
                    for c in range(2):
                        step = _scan_step_of(r0s[j] // CHUNK + c, rev, latent)
                        decay_ref[step] = jnp.exp(bls[j][c * CHUNK:c * CHUNK + 1, :])
                        st_ref[step] = ups[j][:, c * HEAD_DIM:(c + 1) * HEAD_DIM]
                    if latent:
                        orow = pl.ds(pl.multiple_of(r0s[j] - CTX_LEN, GROUP), GROUP)
                        qt_ref[orow, :] = (qs[j] * jnp.exp(bs[j])).astype(BF16)
                        if rev:
                            o_ref[orow, :] += outs[j]
                        else:
                            o_ref[orow, :] = outs[j]
                return carry

            lax.fori_loop(0, CTX_LEN // (per_ctx * GROUP), functools.partial(local_step, latent=False, per=per_ctx), 0)
            lax.fori_loop(0, SEQ // (per_lat * GROUP), functools.partial(local_step, latent=True, per=per_lat), 0)

            def scan_step(t, st):
                update = st_ref[t]
                st_ref[t] = st
                return st * decay_ref[t] + update

            lax.fori_loop(0, N_CHUNKS, scan_step, jnp.zeros((HEAD_DIM, HEAD_DIM), F32), unroll=2)

            def inter_step(i, carry):
                r0s = [_group_rows(i, j, per_lat, True) for j in range(per_lat)]
                orows = [pl.ds(pl.multiple_of(r0 - CTX_LEN, GROUP), GROUP) for r0 in r0s]
                states = [jnp.concatenate([st_ref[_scan_step_of(r0 // CHUNK + c, rev, True)] for c in range(2)], axis=0)
                          for r0 in r0s]
                prods = [lax.dot_general(qt_ref[orow, :], s.astype(BF16), (((1,), (1,)), ((), ())), preferred_element_type=F32)
                         for orow, s in zip(orows, states)]
                for orow, r in zip(orows, prods):
                    o_ref[orow, :] += _own_block(r)
                return carry

            lax.fori_loop(0, SEQ // (per_lat * GROUP), inter_step, 0)

        @pl.when(pl.program_id(0) == HEADS - 1)
        def _():
            finish_gather()

    outs = pl.pallas_call(
        body, name="gla_forward", grid=(HEADS,),
        out_shape=[pltpu.HBM((SEQ, D_MODEL), F32)] + [jax.ShapeDtypeStruct(W_SPECS[k][0], BF16) for k in GATHER_LATE],
        in_specs=[G_SPEC(0), G_SPEC(1), G_SPEC(2), G_SPEC(3), pl.BlockSpec((2, HEAD_DIM), lambda h: (0, h))] + [ANY] * n_late,
        out_specs=[pl.BlockSpec((SEQ, HEAD_DIM), lambda h: (0, h))] + [ANY] * n_late,
        scratch_shapes=[pltpu.VMEM((N_CHUNKS, HEAD_DIM, HEAD_DIM), F32), pltpu.VMEM((N_CHUNKS, 1, HEAD_DIM), F32),
                        pltpu.VMEM((SEQ, HEAD_DIM), BF16)] + [pltpu.VMEM(W_SPECS[k][0], BF16) for k in GATHER_LATE]
        + [pltpu.SemaphoreType.DMA((2 * n_late,)), pltpu.SemaphoreType.DMA((2 * n_late,)), pltpu.SemaphoreType.DMA((n_late,))],
        compiler_params=_params(48),
    )(*_pin(g5, g5, g5, g5, lb_logits), *late_shards)
    return outs[0], dict(zip(GATHER_LATE, outs[1:]))


def _gla_backward(g5, lb_logits, d_o, d_z, late_parts):
    q_scale = HEAD_DIM ** -0.5
    per_lat, per_ctx = GROUPS_PER_STEP, min(GROUPS_PER_STEP, CTX_LEN // GROUP)
    n_late = len(GATHER_LATE)

    def body(ff_ref, fb_ref, v_ref, q_ref, lg_ref, do_ref, dz_ref, *rest):
        part_refs, (dg_ref, dlg_ref), slot_refs = rest[:n_late], rest[n_late:n_late + 2], rest[n_late + 2:2 * n_late + 2]
        st_ref, dst_ref, decay_ref, both_ref = rest[2 * n_late + 2:2 * n_late + 6]
        start_scatter, wait_scatter = _scatter_direct(GATHER_LATE, part_refs, slot_refs, *rest[2 * n_late + 6:])

        @pl.when(pl.program_id(0) == 0)
        def _():
            start_scatter()

        dg_ref[3, 0:CTX_LEN, :] = jnp.zeros((CTX_LEN, HEAD_DIM), BF16)
        dg_ref[4, 0:CTX_LEN, :] = jnp.zeros((CTX_LEN, HEAD_DIM), BF16)
        dg_ref[4, CTX_LEN:ROWS_ALL, :] = dz_ref[...]
        second = lax.broadcasted_iota(jnp.int32, (GROUP, HEAD_DIM), 0) >= CHUNK
        for rev in (False, True):
            d = 1 if rev else 0
            f_ref = fb_ref if rev else ff_ref
            lb = _sig(lg_ref[d:d + 1, :])
            keep, tri, tri_t, strict = _group_masks(rev)
            last = 0 if rev else CHUNK - 1
            mid = CHUNK // 2 if rev else CHUNK // 2 - 1

            def local_step(i, carry, latent, per):
                r0s = [_group_rows(i, j, per, latent) for j in range(per)]
                rows = [pl.ds(r0, GROUP) for r0 in r0s]
                gates = [_gates(f_ref[r, :], lb) for r in rows]
                bs = [_group_sum(tri, g[3]) for g in gates]
                bls = [_chunk_row(b, last) for b in bs]
                ups = [_mm_tn(v_ref[r, :], _by_chunk(g[2] * jnp.exp(bl - b), second)) for r, g, b, bl in zip(rows, gates, bs, bls)]
                if latent:
                    orows = [pl.ds(pl.multiple_of(r0 - CTX_LEN, GROUP), GROUP) for r0 in r0s]
                    d_ups = [_mm_tn(do_ref[orow, :], _by_chunk(_silu(q_ref[r, :]) * q_scale * jnp.exp(b), second))
                             for orow, r, b in zip(orows, rows, bs)]
                for j in range(per):
                    for c in range(2):
                        step = _scan_step_of(r0s[j] // CHUNK + c, rev, latent)
                        decay_ref[step] = jnp.exp(bls[j][c * CHUNK:c * CHUNK + 1, :])
                        st_ref[step] = ups[j][:, c * HEAD_DIM:(c + 1) * HEAD_DIM]
                        if latent:
                            dst_ref[step] = d_ups[j][:, c * HEAD_DIM:(c + 1) * HEAD_DIM]
                        else:
                            dst_ref[step] = jnp.zeros((HEAD_DIM, HEAD_DIM), F32)
                return carry

            lax.fori_loop(0, CTX_LEN // (per_ctx * GROUP), functools.partial(local_step, latent=False, per=per_ctx), 0)
            lax.fori_loop(0, SEQ // (per_lat * GROUP), functools.partial(local_step, latent=True, per=per_lat), 0)

            def scan_step(i, carry):
                st, d_st = carry
                j = N_CHUNKS - 1 - i
                update, d_update = st_ref[i], dst_ref[j]
                st_ref[i] = st
                dst_ref[j] = d_st
                return st * decay_ref[i] + update, d_st * decay_ref[j] + d_update

            zero_state = jnp.zeros((HEAD_DIM, HEAD_DIM), F32)
            lax.fori_loop(0, N_CHUNKS, scan_step, (zero_state, zero_state))

            def grad_step(i, dlb, latent, per):
                r0s = [_group_rows(i, j, per, latent) for j in range(per)]
                rows = [pl.ds(r0, GROUP) for r0 in r0s]
                gates = [_gates(f_ref[r, :], lb) for r in rows]
                vs = [v_ref[r, :] for r in rows]
                bs = [_group_sum(tri, g[3]) for g in gates]
                bls = [_chunk_row(b, last) for b in bs]
                e_ends = [jnp.exp(bl - b) for b, bl in zip(bs, bls)]
                k_ends = [g[2] * e for g, e in zip(gates, e_ends)]
                sts = [[st_ref[_scan_step_of(r0 // CHUNK + c, rev, latent)] for c in range(2)] for r0 in r0s]
                d_sts = [[dst_ref[_scan_step_of(r0 // CHUNK + c, rev, latent)] for c in range(2)] for r0 in r0s]
                d_kends = [_own_block(_mm(v, jnp.concatenate(ds, axis=1))) for v, ds in zip(vs, d_sts)]
                d_vs = [_own_block(_mm_nt(ke, jnp.concatenate(ds, axis=0))) for ke, ds in zip(k_ends, d_sts)]
                at_last = [jnp.concatenate([jnp.broadcast_to(jnp.sum(ds[c] * s[c], axis=0, keepdims=True), (CHUNK, HEAD_DIM))
                                            for c in range(2)], axis=0) * jnp.exp(bl) for ds, s, bl in zip(d_sts, sts, bls)]
                t_kends = [dk * ke for dk, ke in zip(d_kends, k_ends)]
                d_ks = [dk * e for dk, e in zip(d_kends, e_ends)]
                d_lfs = [_group_sum(strict, t) + al for t, al in zip(t_kends, at_last)]
                if latent:
                    orows = [pl.ds(pl.multiple_of(r0 - CTX_LEN, GROUP), GROUP) for r0 in r0s]
                    qpres = [q_ref[r, :] for r in rows]
                    q_sigs = [_sig(qp) for qp in qpres]
                    qs = [qp * sg * q_scale for qp, sg in zip(qpres, q_sigs)]
                    bms = [_chunk_row(b, mid) for b in bs]
                    e_bs = [jnp.exp(b) for b in bs]
                    e_qms = [jnp.exp(b - bm) for b, bm in zip(bs, bms)]
                    e_kms = [jnp.exp(bm - b) for b, bm in zip(bs, bms)]
                    q_ts = [q * e for q, e in zip(qs, e_bs)]
                    q_ms = [q * e for q, e in zip(qs, e_qms)]
                    k_ms = [g[2] * e for g, e in zip(gates, e_kms)]
                    d_outs = [do_ref[orow, :] for orow in orows]
                    a_s = [jnp.where(keep, _mm_nt(qm, km), 0.0) for qm, km in zip(q_ms, k_ms)]
                    d_as = [jnp.where(keep, _mm_nt(do, v), 0.0) for do, v in zip(d_outs, vs)]
                    d_qts = [_own_block(_mm(do, jnp.concatenate(s, axis=1))) for do, s in zip(d_outs, sts)]
                    d_qms = [_mm(da, km) for da, km in zip(d_as, k_ms)]
                    d_kms = [_mm_tn(da, qm) for da, qm in zip(d_as, q_ms)]
                    d_vs = [dv + _mm_tn(a, do) for dv, a, do in zip(d_vs, a_s, d_outs)]
                    d_ks = [dk + dkm * e for dk, dkm, e in zip(d_ks, d_kms, e_kms)]
                    d_lfs = [dl + _group_sum(tri_t, dqt * qt + dqm * qm - dkm * km)
                             for dl, dqt, qt, dqm, qm, dkm, km in zip(d_lfs, d_qts, q_ts, d_qms, q_ms, d_kms, k_ms)]
                    d_qs = [dqt * eb + dqm * eq for dqt, eb, dqm, eq in zip(d_qts, e_bs, d_qms, e_qms)]
                for j in range(per):
                    sg, f = gates[j][0], gates[j][1]
                    d_f = d_lfs[j] / f - d_ks[j]
                    dg_ref[d, rows[j], :] = (d_f * (1.0 - lb) * sg * (1.0 - sg)).astype(BF16)
                    dlb = dlb + jnp.sum(d_f * (1.0 - sg), axis=0, keepdims=True)
                    if rev:
                        dg_ref[2, rows[j], :] = (both_ref[0, rows[j], :] + d_vs[j]).astype(BF16)
                    else:
                        both_ref[0, rows[j], :] = d_vs[j]
                    if latent:
                        d_qpre = d_qs[j] * q_scale * (q_sigs[j] * (1.0 + qpres[j] * (1.0 - q_sigs[j])))
                        if rev:
                            dg_ref[3, rows[j], :] = (both_ref[1, rows[j], :] + d_qpre).astype(BF16)
                        else:
                            both_ref[1, rows[j], :] = d_qpre
                return dlb

            dlb = lax.fori_loop(0, SEQ // (GROUPS_PER_GRAD_STEP * GROUP),
                                functools.partial(grad_step, latent=True, per=GROUPS_PER_GRAD_STEP), jnp.zeros((1, HEAD_DIM), F32))
            dlb = lax.fori_loop(0, CTX_LEN // (per_ctx * GROUP), functools.partial(grad_step, latent=False, per=per_ctx), dlb)
            dlg_ref[d:d + 1, :] = dlb * lb * (1.0 - lb)

        @pl.when(pl.program_id(0) == HEADS - 1)
        def _():
            wait_scatter()

    col = pl.BlockSpec((SEQ, HEAD_DIM), lambda h: (0, h))
    outs = pl.pallas_call(
        body, name="gla_backward", grid=(HEADS,),
        out_shape=[pltpu.HBM((HGRN_SECTIONS, ROWS_ALL, D_MODEL), BF16), jax.ShapeDtypeStruct((2, D_MODEL), F32)]
        + _slot_shapes(GATHER_LATE),
        in_specs=[G_SPEC(0), G_SPEC(1), G_SPEC(2), G_SPEC(3), pl.BlockSpec((2, HEAD_DIM), lambda h: (0, h)), col, col]
        + [ANY] * n_late,
        out_specs=[pl.BlockSpec((HGRN_SECTIONS, ROWS_ALL, HEAD_DIM), lambda h: (0, 0, h)),
                   pl.BlockSpec((2, HEAD_DIM), lambda h: (0, h))] + [ANY] * n_late,
        scratch_shapes=[pltpu.VMEM((N_CHUNKS, HEAD_DIM, HEAD_DIM), F32), pltpu.VMEM((N_CHUNKS, HEAD_DIM, HEAD_DIM), F32),
                        pltpu.VMEM((N_CHUNKS, 1, HEAD_DIM), F32), pltpu.VMEM((2, ROWS_ALL, HEAD_DIM), F32)]
        + _comm_sems(n_late),
        compiler_params=_params(48),
    )(*_pin(g5, g5, g5, g5, lb_logits, d_o, d_z), *late_parts)
    return outs[0], outs[1], dict(zip(GATHER_LATE, outs[2:]))


def _head_norm(o, gw, scr):
    rs = []
    for h in range(HEADS):
        cols = slice(h * HEAD_DIM, (h + 1) * HEAD_DIM)
        oh = o[:, cols]
        r = lax.rsqrt(jnp.mean(oh * oh, axis=-1, keepdims=True) + EPS)
        scr[:, cols] = oh * r
        rs.append(r)
    return rs


def _hgrn_out_forward(o_raw, g5, xin, gnorm_w, gate, w_out):
    tm = ROW_TILE

    def body(o_ref, z_ref, x_ref, gw_ref, gate_ref, w_ref, x1_ref, res_ref, scr):
        _head_norm(o_ref[...], None, scr)
        a = scr[...] * gw_ref[...] * _silu(z_ref[...])
        res = _mm(a, w_ref[...])
        res_ref[...] = res
        x1_ref[...] = x_ref[...] + gate_ref[...] * res

    tile = pl.BlockSpec((tm, D_MODEL), lambda i: (i, 0))
    vec = pl.BlockSpec((1, D_MODEL), lambda i: (0, 0))
    return pl.pallas_call(
        body, name="hgrn_out_forward", grid=(SEQ // tm,),
        out_shape=[pltpu.HBM((SEQ, D_MODEL), F32)] * 2,
        in_specs=[tile, pl.BlockSpec((None, tm, D_MODEL), lambda i: (4, i + CTX_LEN // tm, 0)), tile, vec, vec,
                  pl.BlockSpec((D_MODEL, D_MODEL), lambda i: (0, 0))],
        out_specs=[tile, tile],
        scratch_shapes=[pltpu.VMEM((tm, D_MODEL), F32)],
        compiler_params=_params(32),
    )(*_pin(o_raw, g5, xin, gnorm_w, gate, w_out))


def _hgrn_out_backward(d_x1, o_raw, g5, res, gnorm_w, gate, w_out):
    tm = ROW_TILE

    def body(dx_ref, o_ref, z_ref, res_ref, gw_ref, gate_ref, w_ref, do_ref, dz_ref, dw_out, dgate_ref, dgw_ref, scr, scr2,
             dw_ref):
        @pl.when(pl.program_id(0) == 0)
        def _():
            dw_ref[...] = jnp.zeros_like(dw_ref)
            dgate_ref[...] = jnp.zeros_like(dgate_ref)
            dgw_ref[...] = jnp.zeros_like(dgw_ref)

        dx = dx_ref[...]
        dgate_ref[...] += jnp.sum(dx * res_ref[...], axis=0, keepdims=True)
        d_res = (dx * gate_ref[...]).astype(BF16)
        d_a = _mm_nt(d_res, w_ref[...])
        rs = _head_norm(o_ref[...], None, scr)
        z = z_ref[...]
        sz = _silu(z)
        o_hat = scr[...]
        o_n = o_hat * gw_ref[...]
        dw_ref[...] += _mm_tn(o_n * sz, d_res)
        d_on = d_a * sz
        dz_ref[...] = (d_a * o_n * _dsilu(z)).astype(BF16)
        dgw_ref[...] += jnp.sum(d_on * o_hat, axis=0, keepdims=True)
        scr2[...] = d_on * gw_ref[...]
        for h in range(HEADS):
            cols = slice(h * HEAD_DIM, (h + 1) * HEAD_DIM)
            dh, oh = scr2[:, cols], scr[:, cols]
            do_ref[:, cols] = (rs[h] * (dh - oh * jnp.mean(dh * oh, axis=-1, keepdims=True))).astype(BF16)

        @pl.when(pl.program_id(0) == SEQ // tm - 1)
        def _():
            dw_out[...] = dw_ref[...].astype(BF16)

    tile = pl.BlockSpec((tm, D_MODEL), lambda i: (i, 0))
    vec = pl.BlockSpec((1, D_MODEL), lambda i: (0, 0))
    mat = pl.BlockSpec((D_MODEL, D_MODEL), lambda i: (0, 0))
    return pl.pallas_call(
        body, name="hgrn_out_backward", grid=(SEQ // tm,),
        out_shape=[pltpu.HBM((SEQ, D_MODEL), BF16)] * 2 + [pltpu.HBM((D_MODEL, D_MODEL), BF16)]
        + [jax.ShapeDtypeStruct((1, D_MODEL), F32)] * 2,
        in_specs=[tile, tile, pl.BlockSpec((None, tm, D_MODEL), lambda i: (4, i + CTX_LEN // tm, 0)), tile, vec, vec, mat],
        out_specs=[tile, tile, mat, vec, vec],
        scratch_shapes=[pltpu.VMEM((tm, D_MODEL), F32)] * 2 + [pltpu.VMEM((D_MODEL, D_MODEL), F32)],
        compiler_params=_params(40),
    )(*_pin(d_x1, o_raw, g5, res, gnorm_w, gate, w_out))


def _pool_constants():
    win = np.zeros((POOL_GROUPS, ROW_TILE, ROW_TILE), np.float32)
    inv = np.zeros((POOL_GROUPS, ROW_TILE, 1), np.float32)
    for g, w in enumerate(POOL_WINDOWS):
        for t in range(ROW_TILE):
            base, p = (t // GRID_W) * GRID_W, t % GRID_W
            lo = min(max(p - w // 2, 0), GRID_W)
            hi = min(max(p - w // 2 + w, 0), GRID_W)
            win[g, t, base + lo:base + hi] = 1.0
            inv[g, t, 0] = 1.0 / np.float32(hi - lo)
    return jnp.asarray(win, BF16), jnp.asarray(win.transpose(0, 2, 1), BF16), jnp.asarray(inv, F32)


def _pool_mix(u_ref, wg_ref, ps_ref, win_ref, inv_ref, pooled_scr, yg_scr):
    for g in range(POOL_GROUPS):
        cols = slice(g * POOL_GROUP_DIM, (g + 1) * POOL_GROUP_DIM)
        ug = u_ref[:, cols]
        pooled = _mm_exact_lhs(win_ref[g], ug) * inv_ref[g] - ug
        if pooled_scr is not None:
            pooled_scr[:, cols] = pooled
        yg_scr[:, cols] = _mm(pooled, wg_ref[g])


def _pool_forward_loss(uz, x1, target, gate, w_grp, pool_scale, w_out, final_w):
    tm = ROW_TILE
    win, _, inv = _pool_constants()

    def body(u_ref, z_ref, x_ref, t_ref, gate_ref, wg_ref, ps_ref, w_ref, fw_ref, win_ref, inv_ref,
             dx_ref, loss_ref, dfw_ref, dgate_ref, yg_scr):
        @pl.when(pl.program_id(0) == 0)
        def _():
            loss_ref[...] = jnp.zeros_like(loss_ref)
            dfw_ref[...] = jnp.zeros_like(dfw_ref)
            dgate_ref[...] = jnp.zeros_like(dgate_ref)

        _pool_mix(u_ref, wg_ref, ps_ref, win_ref, inv_ref, None, yg_scr)
        a = yg_scr[...] * ps_ref[...] * _silu(z_ref[...])
        res = _mm(a, w_ref[...])
        x2 = x_ref[...] + gate_ref[...] * res
        r = lax.rsqrt(jnp.mean(x2 * x2, axis=-1, keepdims=True) + EPS)
        xh = x2 * r
        fw = fw_ref[...]
        err = xh * fw - t_ref[...]
        loss_ref[...] += 0.5 * jnp.sum(jnp.mean(err * err, axis=-1, keepdims=True))
        d_y = err * (1.0 / D_MODEL)
        dfw_ref[...] += jnp.sum(d_y * xh, axis=0, keepdims=True)
        d_xh = d_y * fw
        d_x2 = r * (d_xh - xh * jnp.mean(d_xh * xh, axis=-1, keepdims=True))
        dx_ref[...] = d_x2
        dgate_ref[...] += jnp.sum(d_x2 * res, axis=0, keepdims=True)

    tile = pl.BlockSpec((tm, D_MODEL), lambda i: (i, 0))
    vec = pl.BlockSpec((1, D_MODEL), lambda i: (0, 0))
    grp = pl.BlockSpec((POOL_GROUPS, POOL_GROUP_DIM, POOL_GROUP_DIM), lambda i: (0, 0, 0))
    return pl.pallas_call(
        body, name="pool_forward_loss", grid=(SEQ // tm,),
        out_shape=[pltpu.HBM((SEQ, D_MODEL), F32), jax.ShapeDtypeStruct((8, 128), F32),
                   jax.ShapeDtypeStruct((1, D_MODEL), F32), jax.ShapeDtypeStruct((1, D_MODEL), F32)],
        in_specs=[pl.BlockSpec((None, tm, D_MODEL), lambda i: (0, i, 0)), pl.BlockSpec((None, tm, D_MODEL), lambda i: (1, i, 0)),
                  tile, tile, vec, grp, vec, pl.BlockSpec((D_MODEL, D_MODEL), lambda i: (0, 0)), vec, grp,
                  pl.BlockSpec((POOL_GROUPS, ROW_TILE, 1), lambda i: (0, 0, 0))],
        out_specs=[tile, pl.BlockSpec((8, 128), lambda i: (0, 0)), vec, vec],
        scratch_shapes=[pltpu.VMEM((tm, D_MODEL), F32)],
        compiler_params=_params(32),
    )(*_pin(uz, uz, x1, target, gate, w_grp, pool_scale, w_out, final_w, win, inv))


def _pool_backward(d_x2, uz, gate, w_grp, pool_scale, w_out):
    tm = ROW_TILE
    win, win_t, inv = _pool_constants()

    def body(dx_ref, u_ref, z_ref, gate_ref, wg_ref, ps_ref, w_ref, win_ref, wint_ref, inv_ref,
             duz_ref, dw_out, dwg_out, dps_ref, pooled_scr, yg_scr, dyg_scr, dw_ref, dwg_ref):
        @pl.when(pl.program_id(0) == 0)
        def _():
            dw_ref[...] = jnp.zeros_like(dw_ref)
            dwg_ref[...] = jnp.zeros_like(dwg_ref)
            dps_ref[...] = jnp.zeros_like(dps_ref)

        _pool_mix(u_ref, wg_ref, ps_ref, win_ref, inv_ref, pooled_scr, yg_scr)
        z = z_ref[...]
        sz = _silu(z)
        yg = yg_scr[...]
        y = yg * ps_ref[...]
        d_res = (dx_ref[...] * gate_ref[...]).astype(BF16)
        d_a = _mm_nt(d_res, w_ref[...])
        dw_ref[...] += _mm_tn(y * sz, d_res)
        d_y = d_a * sz
        duz_ref[1] = (d_a * y * _dsilu(z)).astype(BF16)
        dps_ref[...] += jnp.sum(d_y * yg, axis=0, keepdims=True)
        dyg_scr[...] = d_y * ps_ref[...]
        for g in range(POOL_GROUPS):
            cols = slice(g * POOL_GROUP_DIM, (g + 1) * POOL_GROUP_DIM)
            d_yg = dyg_scr[:, cols].astype(BF16)
            d_pool = _mm_nt(d_yg, wg_ref[g])
            dwg_ref[g] += _mm_tn(pooled_scr[:, cols], d_yg)
            duz_ref[0, :, cols] = (_mm_exact_lhs(wint_ref[g], d_pool * inv_ref[g]) - d_pool).astype(BF16)

        @pl.when(pl.program_id(0) == SEQ // tm - 1)
        def _():
            dw_out[...] = dw_ref[...].astype(BF16)
            dwg_out[...] = dwg_ref[...].astype(BF16)

    tile = pl.BlockSpec((tm, D_MODEL), lambda i: (i, 0))
    vec = pl.BlockSpec((1, D_MODEL), lambda i: (0, 0))
    mat = pl.BlockSpec((D_MODEL, D_MODEL), lambda i: (0, 0))
    grp = pl.BlockSpec((POOL_GROUPS, POOL_GROUP_DIM, POOL_GROUP_DIM), lambda i: (0, 0, 0))
    return pl.pallas_call(
        body, name="pool_backward", grid=(SEQ // tm,),
        out_shape=[pltpu.HBM((POOL_SECTIONS, SEQ, D_MODEL), BF16), pltpu.HBM((D_MODEL, D_MODEL), BF16),
                   pltpu.HBM((POOL_GROUPS, POOL_GROUP_DIM, POOL_GROUP_DIM), BF16), jax.ShapeDtypeStruct((1, D_MODEL), F32)],
        in_specs=[tile, pl.BlockSpec((None, tm, D_MODEL), lambda i: (0, i, 0)), pl.BlockSpec((None, tm, D_MODEL), lambda i: (1, i, 0)),
                  vec, grp, vec, mat, grp, grp, pl.BlockSpec((POOL_GROUPS, ROW_TILE, 1), lambda i: (0, 0, 0))],
        out_specs=[pl.BlockSpec((POOL_SECTIONS, tm, D_MODEL), lambda i: (0, i, 0)), mat, grp, vec],
        scratch_shapes=[pltpu.VMEM((tm, D_MODEL), F32)] * 3 + [pltpu.VMEM((D_MODEL, D_MODEL), F32),
                                                               pltpu.VMEM((POOL_GROUPS, POOL_GROUP_DIM, POOL_GROUP_DIM), F32)],
        compiler_params=_params(40),
    )(*_pin(d_x2, uz, uz, gate, w_grp, pool_scale, w_out, win, win_t, inv))


def _ln_mod_backward(d_g, w, ctx_tile, xin, nw, scale, d_up, name):
    n_sec, rows, _ = d_g.shape
    n_mod = scale.shape[0]
    tm = ROW_TILE
    n_tiles = rows // tm
    skip = n_mod - 1

    def body(dg_ref, w_ref, *refs):
        c_ref = refs[0] if skip else None
        x_ref, nw_ref, sc_ref, up_ref, dx_ref, dnw_ref, dmod_ref = refs[skip:]
        i = pl.program_id(0)

        @pl.when(i == 0)
        def _():
            dnw_ref[...] = jnp.zeros_like(dnw_ref)

        @pl.when((i == 0) | (i == skip))
        def _():
            dmod_ref[...] = jnp.zeros_like(dmod_ref)

        d_h = _mm_nt(dg_ref[0], w_ref[:, 0:D_MODEL])
        for k in range(1, n_sec):
            d_h = d_h + _mm_nt(dg_ref[k], w_ref[:, k * D_MODEL:(k + 1) * D_MODEL])
        xv = jnp.where(i == 0, c_ref[...], x_ref[...]) if skip else x_ref[...]
        r = lax.rsqrt(jnp.mean(xv * xv, axis=-1, keepdims=True) + EPS)
        xh = xv * r
        nw_row = nw_ref[...]
        dmod_ref[0:1, :] += jnp.sum(d_h, axis=0, keepdims=True)
        dmod_ref[1:2, :] += jnp.sum(d_h * (xh * nw_row), axis=0, keepdims=True)
        d_xn = d_h * (1.0 + sc_ref[...])
        dnw_ref[...] += jnp.sum(d_xn * xh, axis=0, keepdims=True)
        d_xh = d_xn * nw_row

        @pl.when(i >= skip)
        def _():
            dx_ref[...] = up_ref[...] + r * (d_xh - xh * jnp.mean(d_xh * xh, axis=-1, keepdims=True))

    lat = lambda i: (jnp.maximum(i - skip, 0), 0)
    mod_idx = lambda i: (jnp.minimum(i, n_mod - 1), 0, 0)
    return pl.pallas_call(
        body, name=name, grid=(n_tiles,),
        out_shape=[pltpu.HBM((rows - skip * tm, D_MODEL), F32), jax.ShapeDtypeStruct((1, D_MODEL), F32),
                   jax.ShapeDtypeStruct((n_mod, 8, D_MODEL), F32)],
        in_specs=[pl.BlockSpec((n_sec, tm, D_MODEL), lambda i: (0, i, 0)),
                  pl.BlockSpec((D_MODEL, n_sec * D_MODEL), lambda i: (0, 0))]
        + [pl.BlockSpec((tm, D_MODEL), lambda i: (0, 0))] * skip
        + [pl.BlockSpec((tm, D_MODEL), lat),
           pl.BlockSpec((1, D_MODEL), lambda i: (0, 0)),
           pl.BlockSpec((None, 1, D_MODEL), mod_idx),
           pl.BlockSpec((tm, D_MODEL), lat)],
        out_specs=[pl.BlockSpec((tm, D_MODEL), lat), pl.BlockSpec((1, D_MODEL), lambda i: (0, 0)),
                   pl.BlockSpec((None, 8, D_MODEL), mod_idx)],
        compiler_params=_params(48),
    )(*_pin(d_g, w, *([ctx_tile] * skip), xin, nw, scale.reshape(n_mod, 1, D_MODEL), d_up))


def _weight_grad(h, d_g, name):
    n_sec, rows, _ = d_g.shape
    tm = 768 if rows % 768 == 0 else 512
    n_tiles = rows // tm

    def body(h_ref, dg_ref, dw_ref, acc):
        i = pl.program_id(1)
        prod = _mm_tn(h_ref[...], dg_ref[...])

        @pl.when(i == 0)
        def _():
            acc[...] = prod

        @pl.when((i > 0) & (i < n_tiles - 1))
        def _():
            acc[...] += prod

        @pl.when(i == n_tiles - 1)
        def _():
            dw_ref[...] = (acc[...] + prod).astype(BF16)

    return pl.pallas_call(
        body, name=name, grid=(n_sec, n_tiles),
        out_shape=pltpu.HBM((D_MODEL, n_sec * D_MODEL), BF16),
        in_specs=[pl.BlockSpec((tm, D_MODEL), lambda j, i: (i, 0)), pl.BlockSpec((None, tm, D_MODEL), lambda j, i: (j, i, 0))],
        out_specs=pl.BlockSpec((D_MODEL, D_MODEL), lambda j, i: (0, j)),
        scratch_shapes=[pltpu.VMEM((D_MODEL, D_MODEL), F32)],
        compiler_params=_params(32),
    )(*_pin(h, d_g))


def _weight_grad_paired(h, d_g, name):
    n_sec, rows, _ = d_g.shape
    tm = 768
    n_tiles = rows // tm
    half = D_MODEL // 2

    def body(h_ref, dg_ref, q_ref, acc, keep, send, land, send_sems, recv_sems):
        j, i = pl.program_id(0), pl.program_id(1)
        x, y, c = _my_place()
        prod = _mm_tn(h_ref[...], dg_ref[...])

        def to_sibling(k):
            return pltpu.make_async_remote_copy(src_ref=send.at[k], dst_ref=land.at[k], send_sem=send_sems.at[k],
                                                recv_sem=recv_sems.at[k], device_id=(x, y, 1 - c), device_id_type=MESH)

        @pl.when(i == 0)
        def _():
            acc[...] = prod

        @pl.when((i > 0) & (i < n_tiles - 1))
        def _():
            acc[...] += prod

        @pl.when(i == n_tiles - 1)
        def _():
            acc[...] += prod
            keep[j] = acc[pl.ds(_al(c * half, half), half), :]
            send[j] = acc[pl.ds(_al((1 - c) * half, half), half), :].astype(BF16)
            to_sibling(j).start()

        @pl.when((j == n_sec - 1) & (i == n_tiles - 1))
        def _():
            for k in range(n_sec):
                to_sibling(k).wait()
                q_ref[:, k * D_MODEL:(k + 1) * D_MODEL] = (keep[k] + land[k].astype(F32)).astype(BF16)

    return pl.pallas_call(
        body, name=name, grid=(n_sec, n_tiles),
        out_shape=jax.ShapeDtypeStruct((half, n_sec * D_MODEL), BF16),
        in_specs=[pl.BlockSpec((tm, D_MODEL), lambda j, i: (i, 0)), pl.BlockSpec((None, tm, D_MODEL), lambda j, i: (j, i, 0))],
        out_specs=pl.BlockSpec((half, n_sec * D_MODEL), lambda j, i: (0, 0)),
        scratch_shapes=[pltpu.VMEM((D_MODEL, D_MODEL), F32), pltpu.VMEM((n_sec, half, D_MODEL), F32),
                        pltpu.VMEM((n_sec, half, D_MODEL), BF16), pltpu.VMEM((n_sec, half, D_MODEL), BF16),
                        pltpu.SemaphoreType.DMA((n_sec,)), pltpu.SemaphoreType.DMA((n_sec,))],
        compiler_params=_params(56),
    )(*_pin(h, d_g))


def _sum_slots(slots, name):
    n_slots, rows, cols = slots.shape
    tr = 128

    def body(s_ref, o_ref):
        acc = s_ref[0].astype(F32)
        for d in range(1, n_slots):
            acc = acc + s_ref[d].astype(F32)
        o_ref[...] = acc

    return pl.pallas_call(
        body, name=name, grid=(rows // tr,),
        out_shape=jax.ShapeDtypeStruct((rows, cols), F32),
        in_specs=[pl.BlockSpec((n_slots, tr, cols), lambda i: (0, i, 0))],
        out_specs=pl.BlockSpec((tr, cols), lambda i: (i, 0)),
    )(*_pin(slots))


def _adamw_math(w, g, m, v):
    m = ADAM_B1 * m + (1.0 - ADAM_B1) * g
    v = ADAM_B2 * v + (1.0 - ADAM_B2) * (g * g)
    m_hat = m / (1.0 - ADAM_B1 ** ADAM_STEP)
    v_hat = v / (1.0 - ADAM_B2 ** ADAM_STEP)
    return -ADAM_LR * (m_hat / (jnp.sqrt(v_hat) + ADAM_EPS) + ADAM_WD * w), m, v


def _adamw(w, g, m, v, name):
    rows, cols = w.shape
    tr = rows if rows <= 128 else 128

    def body(w_ref, g_ref, m_ref, v_ref, d_ref, mo_ref, vo_ref):
        d_ref[...], mo_ref[...], vo_ref[...] = _adamw_math(w_ref[...], g_ref[...], m_ref[...], v_ref[...])

    tile = pl.BlockSpec((tr, cols), lambda i: (i, 0))
    return pl.pallas_call(
        body, name=name, grid=(rows // tr,),
        out_shape=[pltpu.HBM((rows, cols), F32)] * 3,
        in_specs=[tile] * 4, out_specs=[tile] * 3,
    )(*_pin(w, g, m, v))


def _sum_devices(gathered):
    def body(p_ref, o_ref):
        acc = p_ref[0:SMALL_ROWS, :]
        for d in range(1, N_DEV):
            acc = acc + p_ref[d * SMALL_ROWS:(d + 1) * SMALL_ROWS, :]
        o_ref[...] = acc

    return pl.pallas_call(body, name="sum_devices", out_shape=jax.ShapeDtypeStruct((SMALL_ROWS, D_MODEL), F32),
                          in_specs=[VMEM], out_specs=VMEM)(gathered)


def _ada_update(cond_t, d_mod, ada_w, m, v):
    n_layers, _, n_cols = ada_w.shape
    tr = ROW_TILE

    def body(c_ref, dm_ref, w_ref, m_ref, v_ref, g_ref, d_ref, mo_ref, vo_ref):
        g = _mm_f32(_silu(c_ref[...]), dm_ref[...])
        g_ref[...] = g
        d_ref[...], mo_ref[...], vo_ref[...] = _adamw_math(w_ref[...], g, m_ref[...], v_ref[...])

    tile = pl.BlockSpec((None, tr, n_cols), lambda l, i: (l, i, 0))
    return pl.pallas_call(
        body, name="ada_update", grid=(n_layers, D_MODEL // tr),
        out_shape=[pltpu.HBM(ada_w.shape, F32)] * 4,
        in_specs=[pl.BlockSpec((tr, 16), lambda l, i: (i, 0)), pl.BlockSpec((None, 16, n_cols), lambda l, i: (l, 0, 0)),
                  tile, tile, tile],
        out_specs=[tile] * 4,
    )(*_pin(cond_t, d_mod, ada_w, m, v))


def _cond_ctx_partial(d_modc, ada_w0):
    n_cols = ada_w0.shape[1]
    tr = ROW_TILE

    def body(dm_ref, w_ref, o_ref):
        o_ref[...] = jnp.sum(w_ref[...] * dm_ref[...], axis=-1, keepdims=True)

    return pl.pallas_call(
        body, name="cond_ctx_partial", grid=(D_MODEL // tr,),
        out_shape=jax.ShapeDtypeStruct((D_MODEL, 1), F32),
        in_specs=[pl.BlockSpec((1, n_cols), lambda i: (0, 0)), pl.BlockSpec((tr, n_cols), lambda i: (i, 0))],
        out_specs=pl.BlockSpec((tr, 1), lambda i: (i, 0)),
    )(d_modc, ada_w0)


def _cond_ctx_update(gathered, c_ctx, m, v):
    def body(p_ref, w_ref, m_ref, v_ref, g_ref, d_ref, mo_ref, vo_ref):
        acc = p_ref[0:1, :]
        for s in range(1, N_CHIPS):
            acc = acc + p_ref[16 * s:16 * s + 1, :]
        w = w_ref[...]
        g = acc * _dsilu(w)
        g_ref[...] = g
        d_ref[...], mo_ref[...], vo_ref[...] = _adamw_math(w, g, m_ref[...], v_ref[...])

    return pl.pallas_call(body, name="cond_ctx_update", out_shape=[jax.ShapeDtypeStruct((1, D_MODEL), F32)] * 4,
                          in_specs=[VMEM] * 4, out_specs=[VMEM] * 4)(gathered, c_ctx, m, v)


def _local_step(x2, ctx2, target, mod_mine, mod_ctx, lb_logits, scale_full, w_in_full, late_shards, norm_w, gnorm, final_w):
    row = lambda a: a.reshape(1, -1)
    shift0, scale0, gate0 = (row(a) for a in jnp.split(mod_mine[0], 3))
    shift1, scale1, gate1 = (row(a) for a in jnp.split(mod_mine[1], 3))
    shift_c, scale_c, _ = (row(a) for a in jnp.split(mod_ctx, 3))
    nw0, nw1 = norm_w[0:1], norm_w[1:2]
    scales0 = jnp.concatenate([scale_c, scale0])

    g5, h0 = _ln_mod_matmul(ctx2, x2, nw0, jnp.concatenate([shift_c, shift0]), scales0, w_in_full, "hgrn_in_forward")
    o_raw, full = _gla_forward(g5, lb_logits, late_shards)
    x1, res0 = _hgrn_out_forward(o_raw, g5, x2, gnorm, gate0, full["hgrn_w_out"])
    uz, h1 = _ln_mod_matmul(None, x1, nw1, shift1, scale1, full["pool_w_in"], "pool_in_forward")
    d_x2, loss_part, d_final, d_gate1 = _pool_forward_loss(uz, x1, target, gate1, full["pool_w_grp"], scale_full,
                                                           full["pool_w_out"], final_w)

    d_uz, dw_pool_out, dw_pool_grp, d_pscale = _pool_backward(d_x2, uz, gate1, full["pool_w_grp"], scale_full, full["pool_w_out"])
    d_x1, d_nw1, d_mod1 = _ln_mod_backward(d_uz, full["pool_w_in"], None, x1, nw1, scale1, d_x2, "pool_in_backward")
    dw_pool_in = _weight_grad(h1, d_uz, "pool_in_weight_grad")
    d_o, d_z, dw_hgrn_out, d_gate0, d_gnorm = _hgrn_out_backward(d_x1, o_raw, g5, res0, gnorm, gate0, full["hgrn_w_out"])
    late_grads = {"hgrn_w_out": dw_hgrn_out, "pool_w_in": dw_pool_in, "pool_w_grp": dw_pool_grp, "pool_w_out": dw_pool_out}
    d_g5, d_lb, late_slots = _gla_backward(g5, lb_logits, d_o, d_z, [late_grads[k].astype(BF16) for k in GATHER_LATE])
    dw_hgrn_in = _weight_grad_paired(h0, d_g5, "hgrn_in_weight_grad")
    key = GATHER_EARLY[0]
    half = _half_shape(key)
    xi, yi, _ = _my_place()
    own = lax.dynamic_slice(dw_hgrn_in, (0, (2 * xi + yi) * half[1]), half)
    slots0 = lax.dynamic_update_slice(jnp.zeros((N_CHIPS,) + half, BF16), own[None], (2 * xi + yi, 0, 0))
    send_sem, recv_sem, part_thru, slots_thru, token = _scatter_start(dw_hgrn_in, slots0, own, key)
    d_x, d_nw0, d_mod0 = _ln_mod_backward(d_g5, w_in_full, ctx2, x2, nw0 + token[0:1, 0:1], scales0, d_x1, "hgrn_in_backward")
    slots = dict(late_slots)
    pending = (send_sem, recv_sem, part_thru, slots_thru)

    zero = jnp.zeros((1, D_MODEL), F32)
    small = jnp.concatenate([d_mod0[1, 0:2], d_gate0, d_mod1[0, 0:2], d_gate1, d_mod0[0, 0:2], zero, d_nw0, d_nw1, d_gnorm,
                             d_final, d_pscale, d_lb, jnp.broadcast_to(loss_part[0:1, 0:1], (1, D_MODEL)),
                             jnp.zeros((SMALL_ROWS - 17, D_MODEL), F32)], axis=0)
    return {"d_x": d_x, "slots": slots, "pending": pending, "small": small}


def kernel(x, c, ctx, c_ctx, ada_w, ada_b, norm_w, hgrn_w_in, hgrn_lb_logits, hgrn_gnorm_w, hgrn_w_out, pool_w_in, pool_w_grp, pool_scale, pool_w_out, final_norm_w, loss_target, m_c_ctx, m_ada_w, m_ada_b, m_norm_w, m_hgrn_w_in, m_hgrn_lb_logits, m_hgrn_gnorm_w, m_hgrn_w_out, m_pool_w_in, m_pool_w_grp, m_pool_scale, m_pool_w_out, m_final_norm_w, v_c_ctx, v_ada_w, v_ada_b, v_norm_w, v_hgrn_w_in, v_hgrn_lb_logits, v_hgrn_gnorm_w, v_hgrn_w_out, v_pool_w_in, v_pool_w_grp, v_pool_scale, v_pool_w_out, v_final_norm_w):
    xi, yi, ci = _my_place()
    chip = 2 * xi + yi
    dev = 4 * xi + 2 * yi + ci
    ada_cols = ada_w.shape[2]
    lb_cols = hgrn_lb_logits.shape[2]
    ps_cols = pool_scale.shape[1]
    row = lambda a: a.reshape(1, -1)

    def chip_cols(a, n):
        return lax.dynamic_slice_in_dim(a, chip * n, n, axis=a.ndim - 1)

    def from_chips(g, rows_per_dev, take):
        return jnp.concatenate([g[2 * s * rows_per_dev:2 * s * rows_per_dev + take] for s in range(N_CHIPS)], axis=1)

    first = jnp.concatenate([jnp.broadcast_to(c, (8, D_MODEL)), jnp.pad(hgrn_lb_logits[0], ((0, 6), (0, 0))),
                             jnp.pad(pool_scale, ((0, 7), (0, 0)))], axis=1)
    shards = {"hgrn_w_in": hgrn_w_in[0], "hgrn_w_out": hgrn_w_out[0], "pool_w_in": pool_w_in[0],
              "pool_w_grp": pool_w_grp[0], "pool_w_out": pool_w_out[0]}
    first_all, parts_all, w_in_full = _prologue(first, row(c_ctx), ada_w, chip_cols(ada_b, ada_cols),
                                                shards[GATHER_EARLY[0]].astype(BF16))
    cond_all = first_all[::8, :D_MODEL]
    lb_logits = from_chips(first_all[:, D_MODEL:D_MODEL + lb_cols], 8, 2)
    scale_full = from_chips(first_all[:, D_MODEL + lb_cols:], 8, 1)
    cond_rows = jnp.concatenate([cond_all, row(c_ctx), jnp.zeros((7, D_MODEL), F32)], axis=0)
    mod_all = from_chips(parts_all, 32, 32).reshape(2, 16, 3 * D_MODEL)
    mod_mine = lax.dynamic_index_in_dim(mod_all, dev, axis=1, keepdims=False)

    loc = _local_step(x[0], ctx[0], loss_target[0], mod_mine, mod_all[0, 8], lb_logits, scale_full, w_in_full,
                      [shards[k].astype(BF16) for k in GATHER_LATE], norm_w, hgrn_gnorm_w, row(final_norm_w))

    small_all = _all_gather_small(loc["small"], "gather_small")
    sums = _sum_devices(small_all)
    loss = sums[16, 0]

    part_c = _cond_ctx_partial(chip_cols(sums[6:9].reshape(1, -1), ada_cols), ada_w[0])
    part_all = _all_gather_small(jnp.pad(part_c.reshape(1, D_MODEL), ((0, 7), (0, 0))), "gather_cond_ctx")

    def reduce_scattered(slots, names, name):
        halves = []
        for k in names:
            hs = _half_shape(k)
            halves.append(_sum_slots(slots[k].reshape(slots[k].shape[0], -1, hs[-1]), "sum_" + k).reshape(hs))
        return dict(zip(names, _exchange_halves(halves, names, name)))

    big_grads = reduce_scattered(loc["slots"], GATHER_LATE, "exchange_halves_late")

    out = {}

    def update(name, w, g, m, v):
        shape = w.shape
        w2, g2, m2, v2 = (a.reshape(-1, shape[-1]) for a in (w, g, m, v))
        d, mn, vn = _adamw(w2, g2, m2, v2, "adamw_" + name)
        out[name] = tuple(a.reshape(shape) for a in (g2, d, mn, vn))

    moments = {"hgrn_w_in": (m_hgrn_w_in, v_hgrn_w_in), "hgrn_w_out": (m_hgrn_w_out, v_hgrn_w_out),
               "pool_w_in": (m_pool_w_in, v_pool_w_in), "pool_w_grp": (m_pool_w_grp, v_pool_w_grp),
               "pool_w_out": (m_pool_w_out, v_pool_w_out)}
    weights = {"hgrn_w_in": hgrn_w_in, "hgrn_w_out": hgrn_w_out, "pool_w_in": pool_w_in, "pool_w_grp": pool_w_grp,
               "pool_w_out": pool_w_out}
    for k in GATHER_LATE:
        update(k, weights[k], big_grads[k], *moments[k])

    g_ada_b = jnp.stack([(sums[0:3] + sums[6:9]).reshape(-1), sums[3:6].reshape(-1)])
    update("ada_b", ada_b, g_ada_b, m_ada_b, v_ada_b)
    update("norm_w", norm_w, sums[9:11], m_norm_w, v_norm_w)
    update("hgrn_gnorm_w", hgrn_gnorm_w, sums[11:12], m_hgrn_gnorm_w, v_hgrn_gnorm_w)
    update("final_norm_w", row(final_norm_w), sums[12:13], row(m_final_norm_w), row(v_final_norm_w))
    update("pool_scale", pool_scale, chip_cols(sums[13:14], ps_cols), m_pool_scale, v_pool_scale)
    update("hgrn_lb_logits", hgrn_lb_logits, chip_cols(sums[14:16], lb_cols)[None], m_hgrn_lb_logits, v_hgrn_lb_logits)

    per_dev = small_all.reshape(N_DEV, SMALL_ROWS, D_MODEL)
    pad7 = jnp.zeros((7, 3 * D_MODEL), F32)
    dm0 = jnp.concatenate([per_dev[:, 0:3].reshape(N_DEV, -1), sums[6:9].reshape(1, -1), pad7], axis=0)
    dm1 = jnp.concatenate([per_dev[:, 3:6].reshape(N_DEV, -1), jnp.zeros((8, 3 * D_MODEL), F32)], axis=0)
    d_mod = chip_cols(jnp.stack([dm0, dm1]), ada_cols)
    out["ada_w"] = _ada_update(cond_rows.T, d_mod, ada_w, m_ada_w, v_ada_w)

    g_c, d_c, m_c, v_c = _cond_ctx_update(part_all, row(c_ctx), row(m_c_ctx), row(v_c_ctx))
    out["c_ctx"] = tuple(a.reshape(-1) for a in (g_c, d_c, m_c, v_c))
    out["final_norm_w"] = tuple(a.reshape(-1) for a in out["final_norm_w"])

    done = [g_c, out["ada_w"][1]] + [out[k][1] for k in GATHER_LATE]
    key = GATHER_EARLY[0]
    _, early = _scatter_wait(*loc["pending"], done, key)
    big_grads = reduce_scattered({key: early}, GATHER_EARLY, "exchange_halves_early")
    for k in GATHER_EARLY:
        update(k, weights[k], big_grads[k], *moments[k])

    names = ["c_ctx", "ada_w", "ada_b", "norm_w", "hgrn_w_in", "hgrn_lb_logits", "hgrn_gnorm_w", "hgrn_w_out", "pool_w_in",
             "pool_w_grp", "pool_scale", "pool_w_out", "final_norm_w"]
    return (loss, loc["d_x"][None], *[out[k][0] for k in names], *[out[k][1] for k in names], *[out[k][2] for k in names],
            *[out[k][3] for k in names])
```

```python
import functools

import numpy as np
import jax
import jax.numpy as jnp
from jax import lax
from jax.experimental import pallas as pl
from jax.experimental.pallas import tpu as pltpu

F32 = jnp.float32
BF16 = jnp.bfloat16

D_MODEL = 1024
SEQ = 2048
CTX_LEN = 256
ROWS_ALL = CTX_LEN + SEQ
HEADS = 8
HEAD_DIM = 128
CHUNK = 64
N_CTX_CHUNKS = CTX_LEN // CHUNK
N_LAT_CHUNKS = SEQ // CHUNK
N_CHUNKS = N_CTX_CHUNKS + N_LAT_CHUNKS
GRID_W = 64
POOL_WINDOWS = (2, 4, 8, 16)
POOL_GROUPS = 4
POOL_GROUP_DIM = 256
HGRN_SECTIONS = 5
POOL_SECTIONS = 2
EPS = 1e-6
N_DEV = 8
N_CHIPS = 4
ROW_TILE = 256
SMALL_ROWS = 24

ADAM_LR = 0.001
ADAM_B1 = 0.9
ADAM_B2 = 0.999
ADAM_EPS = 1e-08
ADAM_WD = 0.01
ADAM_STEP = 10

MESH = pl.DeviceIdType.MESH
MIB = 1 << 20
ANY = pl.BlockSpec(memory_space=pl.ANY)
VMEM = pl.BlockSpec(memory_space=pltpu.VMEM)


def _params(vmem_mib=None):
    if vmem_mib is None:
        return pltpu.CompilerParams()
    return pltpu.CompilerParams(vmem_limit_bytes=vmem_mib * MIB)


def _pin(*operands):
    return [pltpu.with_memory_space_constraint(a, pltpu.HBM) if a.size * a.dtype.itemsize >= MIB else a for a in operands]


def _sig(a):
    return 1.0 / (1.0 + jnp.exp(-a))


def _silu(a):
    return a * _sig(a)


def _dsilu(a):
    s = _sig(a)
    return s * (1.0 + a * (1.0 - s))


def _mm(a, b):
    return jnp.dot(a.astype(BF16), b.astype(BF16), preferred_element_type=F32)


def _mm_nt(a, b):
    return lax.dot_general(a.astype(BF16), b.astype(BF16), (((1,), (1,)), ((), ())), preferred_element_type=F32)


def _mm_tn(a, b):
    return lax.dot_general(a.astype(BF16), b.astype(BF16), (((0,), (0,)), ((), ())), preferred_element_type=F32)


def _split2(a):
    hi = a.astype(BF16)
    lo = (a - hi.astype(F32)).astype(BF16)
    return hi, lo


def _mm_exact_lhs(m_bf, a):
    hi, lo = _split2(a)
    return jnp.dot(m_bf, hi, preferred_element_type=F32) + jnp.dot(m_bf, lo, preferred_element_type=F32)


def _mm_f32(a, b):
    ah, al = _split2(a)
    bh, bl = _split2(b)
    return (jnp.dot(ah, bh, preferred_element_type=F32) + jnp.dot(al, bh, preferred_element_type=F32)
            + jnp.dot(ah, bl, preferred_element_type=F32))


def _my_place():
    return lax.axis_index("x"), lax.axis_index("y"), lax.axis_index("c")


def _small_gather(x_ref, out_ref, send_sems, recv_sems, local_sem):
    m_per = x_ref.shape[0]
    x, y, c = _my_place()
    me, sibling = (x, y, c), (x, y, 1 - c)
    chips = [(1 - x, y), (x, 1 - y), (1 - x, 1 - y)]

    def rows(px, py, pc):
        return out_ref.at[pl.ds((4 * px + 2 * py + pc) * m_per, m_per), :]

    def copy(k, block, to, src=None):
        return pltpu.make_async_remote_copy(
            src_ref=rows(*block) if src is None else src, dst_ref=rows(*block),
            send_sem=send_sems.at[k], recv_sem=recv_sems.at[k], device_id=to, device_id_type=MESH)

    def mine():
        return pltpu.make_async_copy(x_ref, rows(*me), local_sem)

    def first():
        return [copy(0, me, sibling, src=x_ref)] + [copy(1 + j, me, (*chip, c), src=x_ref) for j, chip in enumerate(chips)]

    def start():
        mine().start()
        for cp in first():
            cp.start()

    def finish():
        passed = [copy(4 + j, (*chip, c), sibling) for j, chip in enumerate(chips)]
        for j, chip in enumerate(chips):
            copy(1 + j, (*chip, c), me).wait_recv()
            passed[j].start()
        copy(0, sibling, me).wait_recv()
        for j, chip in enumerate(chips):
            copy(4 + j, (*chip, 1 - c), me).wait_recv()
        for cp in first() + passed:
            cp.wait_send()
        mine().wait()

    return start, finish


def _all_gather_small(blk, name):
    m_per, n = blk.shape

    def body(x_ref, out_ref, send_sems, recv_sems, local_sem):
        start, finish = _small_gather(x_ref, out_ref, send_sems, recv_sems, local_sem)
        start()
        finish()

    return pl.pallas_call(
        body, name=name,
        out_shape=jax.ShapeDtypeStruct((N_DEV * m_per, n), blk.dtype),
        in_specs=[VMEM], out_specs=VMEM,
        scratch_shapes=[pltpu.SemaphoreType.DMA((7,)), pltpu.SemaphoreType.DMA((7,)), pltpu.SemaphoreType.DMA],
    )(blk)


W_SPECS = {
    "hgrn_w_in": ((D_MODEL, 5 * D_MODEL), (1, 1280, 0, 512)),
    "hgrn_w_out": ((D_MODEL, D_MODEL), (0, 256, 0, 128)),
    "pool_w_in": ((D_MODEL, 2 * D_MODEL), (1, 512, 0, 512)),
    "pool_w_grp": ((POOL_GROUPS, POOL_GROUP_DIM, POOL_GROUP_DIM), (1, 64, 1, 32)),
    "pool_w_out": ((D_MODEL, D_MODEL), (0, 256, 0, 128)),
}
W_NAMES = tuple(W_SPECS)


def _al(v, m):
    return pl.multiple_of(v, m)


def _region(ref, spec, chip, half):
    ca, cn, ha, hn = spec
    idx = [slice(None)] * len(ref.shape)
    if ca == ha:
        if half is None:
            idx[ca] = pl.ds(_al(chip * cn, cn), cn)
        else:
            idx[ca] = pl.ds(_al(chip * cn + half * hn, hn), hn)
    else:
        idx[ca] = pl.ds(_al(chip * cn, cn), cn)
        if half is not None:
            idx[ha] = pl.ds(_al(half * hn, hn), hn)
    return ref.at[tuple(idx)]


def _half_of(ref, spec, half):
    _, _, ha, hn = spec
    idx = [slice(None)] * len(ref.shape)
    idx[ha] = pl.ds(_al(half * hn, hn), hn)
    return ref.at[tuple(idx)]


PIECE_BYTES = 256 * 1024


def _pieces(ref):
    lead = ref.shape[0]
    want = (int(np.prod(ref.shape)) * ref.dtype.itemsize) // PIECE_BYTES
    n = max([1] + [k for k in range(1, want + 1) if lead % k == 0 and (lead // k) % 16 == 0])
    rows = lead // n
    return [ref.at[pl.ds(i * rows, rows)] for i in range(n)]


def _half_shape(name):
    full, (ca, cn, ha, hn) = W_SPECS[name]
    shp = list(full)
    shp[ca] = cn
    shp[ha] = hn
    return tuple(shp)


def _gather_direct(names, sh, full, send_sems, recv_sems, local_sems):
    specs = [W_SPECS[k][1] for k in names]
    x, y, c = _my_place()
    chip_me = 2 * x + y
    chips = [(1 - x, y), (x, 1 - y), (1 - x, 1 - y)]

    def local(a):
        return pltpu.make_async_copy(sh[a], _region(full[a], specs[a], chip_me, None), local_sems.at[a])

    def remote(a, src, dst, to):
        return pltpu.make_async_remote_copy(src_ref=src, dst_ref=dst, send_sem=send_sems.at[a], recv_sem=recv_sems.at[a],
                                            device_id=to, device_id_type=MESH)

    def start():
        for a in range(len(names)):
            local(a).start()
            for px, py in chips:
                remote(a, sh[a], _region(full[a], specs[a], chip_me, None), (px, py, c)).start()

    def wait():
        for a in range(len(names)):
            ca, cn, _, _ = specs[a]
            idx = [slice(None)] * len(full[a].shape)
            idx[ca] = pl.ds(0, 3 * cn)
            three = full[a].at[tuple(idx)]
            remote(a, three, three, (x, y, c)).wait()
            local(a).wait()

    return start, wait


def _gather_two_level(names, sh, land, full, send_sems, recv_sems, local_sems):
    n = len(names)
    specs = [W_SPECS[k][1] for k in names]
    x, y, c = _my_place()
    chip_me = 2 * x + y
    sibling = (x, y, 1 - c)
    chips = [(1 - x, y), (x, 1 - y), (1 - x, 1 - y)]

    def remote(k, src, dst, to):
        return pltpu.make_async_remote_copy(src_ref=src, dst_ref=dst, send_sem=send_sems.at[k], recv_sem=recv_sems.at[k],
                                            device_id=to, device_id_type=MESH)

    def three_halves(a):
        ca, cn, ha, hn = specs[a]
        idx = [slice(None)] * len(land[a].shape)
        if ca == ha:
            idx[ca] = pl.ds(0, 3 * hn)
        else:
            idx[ca], idx[ha] = pl.ds(0, 3 * cn), pl.ds(0, hn)
        return land[a].at[tuple(idx)]

    def own(a):
        return pltpu.make_async_copy(sh[a], _region(land[a], specs[a], chip_me, None), local_sems.at[a])

    def out(a):
        return pltpu.make_async_copy(land[a], full[a], local_sems.at[a])

    def start():
        for a in range(n):
            own(a).start()
            for px, py in chips:
                remote(a, _half_of(sh[a], specs[a], c), _region(land[a], specs[a], chip_me, c), (px, py, c)).start()

    def forward():
        for a in range(n):
            remote(a, three_halves(a), three_halves(a), sibling).wait_recv()
            for px, py in chips:
                landed = _region(land[a], specs[a], 2 * px + py, c)
                remote(n + a, landed, landed, sibling).start()

    def finish():
        for a in range(n):
            remote(n + a, three_halves(a), three_halves(a), sibling).wait_recv()
            remote(a, three_halves(a), three_halves(a), sibling).wait_send()
            remote(n + a, three_halves(a), three_halves(a), sibling).wait_send()
            own(a).wait()
        for a in range(n):
            out(a).start()
        for a in range(n):
            out(a).wait()

    return start, forward, finish


def _scatter_direct(names, part, slots, send_sems, recv_sems, local_sems):
    specs = [W_SPECS[k][1] for k in names]
    x, y, c = _my_place()
    dev_me = 4 * x + 2 * y + c

    def local(a):
        return pltpu.make_async_copy(_region(part[a], specs[a], 2 * x + y, c), slots[a].at[dev_me], local_sems.at[a])

    def start():
        for a in range(len(names)):
            local(a).start()
            for flip in range(1, N_DEV):
                tx = 1 - x if flip >> 2 else x
                ty = 1 - y if (flip >> 1) & 1 else y
                tc = 1 - c if flip & 1 else c
                pltpu.make_async_remote_copy(src_ref=_region(part[a], specs[a], 2 * tx + ty, tc), dst_ref=slots[a].at[dev_me],
                                             send_sem=send_sems.at[a], recv_sem=recv_sems.at[a], device_id=(tx, ty, tc),
                                             device_id_type=MESH).start()

    def wait():
        for a in range(len(names)):
            seven = slots[a].at[pl.ds(0, N_DEV - 1)]
            pltpu.make_async_remote_copy(src_ref=seven, dst_ref=seven, send_sem=send_sems.at[a], recv_sem=recv_sems.at[a],
                                         device_id=(x, y, c), device_id_type=MESH).wait()
            local(a).wait()

    return start, wait


HBM_SPEC = pl.BlockSpec(memory_space=pltpu.HBM)
SEM_SPEC = pl.BlockSpec(memory_space=pltpu.SEMAPHORE)
SPLIT_EFFECT = pltpu.SideEffectType.DATAFLOW_SIDE_EFFECTING


def _scatter_start(part, slots, after, name_key):
    _, cols, _, _ = W_SPECS[name_key][1]

    def body(part_ref, slots_ref, after_ref, send_sem, recv_sem, part_thru, slots_thru, token):
        x, y, c = _my_place()
        for tx, ty in ((1 - x, y), (x, 1 - y), (1 - x, 1 - y)):
            src = part_ref.at[:, pl.ds(_al((2 * tx + ty) * cols, cols), cols)]
            pltpu.make_async_remote_copy(src_ref=src, dst_ref=slots_ref.at[2 * x + y], send_sem=send_sem, recv_sem=recv_sem,
                                         device_id=(tx, ty, c), device_id_type=MESH).start()
        token[...] = jnp.zeros_like(token)

    return pl.pallas_call(
        body, name="scatter_start_" + name_key,
        out_shape=(pltpu.SemaphoreType.DMA(()), pltpu.SemaphoreType.DMA(()), pltpu.HBM(part.shape, part.dtype),
                   pltpu.HBM(slots.shape, slots.dtype), jax.ShapeDtypeStruct((8, 128), F32)),
        in_specs=(HBM_SPEC, HBM_SPEC, ANY), out_specs=(SEM_SPEC, SEM_SPEC, HBM_SPEC, HBM_SPEC, VMEM),
        input_output_aliases={0: 2, 1: 3},
        compiler_params=pltpu.CompilerParams(has_side_effects=SPLIT_EFFECT),
    )(pltpu.with_memory_space_constraint(part, pltpu.HBM), pltpu.with_memory_space_constraint(slots, pltpu.HBM), after)


def _scatter_wait(send_sem, recv_sem, part_thru, slots_thru, after, name_key):
    def body(part_ref, slots_ref, send_sem, recv_sem, *rest):
        x, y, c = _my_place()
        landed = slots_ref.at[pl.ds(0, N_CHIPS - 1)]
        copy = pltpu.make_async_remote_copy(src_ref=landed, dst_ref=landed, send_sem=send_sem, recv_sem=recv_sem,
                                            device_id=(x, y, c), device_id_type=MESH)
        copy.wait_send()
        copy.wait_recv()

    return pl.pallas_call(
        body, name="scatter_wait_" + name_key,
        out_shape=(pltpu.HBM(part_thru.shape, part_thru.dtype), pltpu.HBM(slots_thru.shape, slots_thru.dtype)),
        in_specs=(HBM_SPEC, HBM_SPEC, SEM_SPEC, SEM_SPEC) + (ANY,) * len(after), out_specs=(HBM_SPEC, HBM_SPEC),
        input_output_aliases={0: 0, 1: 1},
        compiler_params=pltpu.CompilerParams(has_side_effects=SPLIT_EFFECT),
    )(part_thru, slots_thru, send_sem, recv_sem, *after)


def _comm_sems(n):
    return [pltpu.SemaphoreType.DMA((n,)), pltpu.SemaphoreType.DMA((n,)), pltpu.SemaphoreType.DMA((n,))]


GATHER_EARLY = ("hgrn_w_in",)
GATHER_LATE = ("hgrn_w_out", "pool_w_in", "pool_w_grp", "pool_w_out")


def _all_gather_weights(shards, names):
    n = len(names)
    specs = [W_SPECS[k][1] for k in names]

    def body(*refs):
        sh, full = refs[:n], refs[n:2 * n]
        send_sems, recv_sems, local_sems = refs[2 * n:]
        x, y, c = _my_place()
        chip_me = 2 * x + y
        sibling = (x, y, 1 - c)
        chips = [(1 - x, y), (x, 1 - y), (1 - x, 1 - y)]

        def remote(a, k, src, dst, to):
            return pltpu.make_async_remote_copy(src_ref=src, dst_ref=dst, send_sem=send_sems.at[6 * a + k],
                                                recv_sem=recv_sems.at[6 * a + k], device_id=to, device_id_type=MESH)

        local = [pltpu.make_async_copy(sh[a], _region(full[a], specs[a], chip_me, None), local_sems.at[a]) for a in range(n)]
        for cp in local:
            cp.start()
        sends = []
        for a in range(n):
            for j, (px, py) in enumerate(chips):
                src, dst = _half_of(sh[a], specs[a], c), _region(full[a], specs[a], chip_me, c)
                for s_piece, d_piece in zip(_pieces(src), _pieces(dst)):
                    remote(a, j, s_piece, d_piece, (px, py, c)).start()
                sends.append(remote(a, j, src, dst, (px, py, c)))
        for a in range(n):
            for j, (px, py) in enumerate(chips):
                landed = _region(full[a], specs[a], 2 * px + py, c)
                remote(a, j, landed, landed, (px, py, c)).wait_recv()
                for piece in _pieces(landed):
                    remote(a, 3 + j, piece, piece, sibling).start()
                sends.append(remote(a, 3 + j, landed, landed, sibling))
        for a in range(n):
            for j, (px, py) in enumerate(chips):
                other = _region(full[a], specs[a], 2 * px + py, 1 - c)
                remote(a, 3 + j, other, other, sibling).wait_recv()
        for cp in sends:
            cp.wait_send()
        for cp in local:
            cp.wait()

    return pl.pallas_call(
        body, name="all_gather_weights",
        out_shape=[jax.ShapeDtypeStruct(W_SPECS[k][0], BF16) for k in names],
        in_specs=[VMEM] * n, out_specs=[VMEM] * n,
        scratch_shapes=[pltpu.SemaphoreType.DMA((6 * n,)), pltpu.SemaphoreType.DMA((6 * n,)), pltpu.SemaphoreType.DMA((n,))],
        compiler_params=_params(32),
    )(*shards)


def _weight_gather(sh, full, spec, send_sems, recv_sems, local_sem):
    x, y, c = _my_place()
    chip_me = 2 * x + y
    sibling = (x, y, 1 - c)
    chips = [(1 - x, y), (x, 1 - y), (1 - x, 1 - y)]

    def remote(k, src, dst, to):
        return pltpu.make_async_remote_copy(src_ref=src, dst_ref=dst, send_sem=send_sems.at[k], recv_sem=recv_sems.at[k],
                                            device_id=to, device_id_type=MESH)

    def own():
        return pltpu.make_async_copy(sh, _region(full, spec, chip_me, None), local_sem)

    def direct():
        return [remote(j, _half_of(sh, spec, c), _region(full, spec, chip_me, c), (px, py, c)) for j, (px, py) in enumerate(chips)]

    def start():
        own().start()
        for cp in direct():
            cp.start()

    def finish():
        passed = []
        for j, (px, py) in enumerate(chips):
            landed = _region(full, spec, 2 * px + py, c)
            remote(j, landed, landed, (px, py, c)).wait_recv()
            passed.append(remote(3 + j, landed, landed, sibling))
            passed[-1].start()
        for j, (px, py) in enumerate(chips):
            other = _region(full, spec, 2 * px + py, 1 - c)
            remote(3 + j, other, other, sibling).wait_recv()
        for cp in direct() + passed:
            cp.wait_send()
        own().wait()

    return start, finish


def _prologue(first, c_ctx, ada_w, ada_b_cols, w_shard):
    n_layers, _, n_cols = ada_w.shape
    key = GATHER_EARLY[0]
    spec = W_SPECS[key][1]
    m_first = first.shape[0]

    def body(first_ref, cctx_ref, adaw_ref, adab_ref, sh_ref, first_all, parts_all, full_ref, parts_scr,
             s1, r1, l1, s2, r2, l2, ws, wr, wl):
        start_first, finish_first = _small_gather(first_ref, first_all, s1, r1, l1)
        start_parts, finish_parts = _small_gather(parts_scr, parts_all, s2, r2, l2)
        start_weight, finish_weight = _weight_gather(sh_ref, full_ref, spec, ws, wr, wl)
        start_first()
        start_weight()
        finish_first()
        cond = jnp.concatenate([first_all[m_first * d:m_first * d + 1, 0:D_MODEL] for d in range(N_DEV)]
                               + [cctx_ref[...], jnp.zeros((16 - N_DEV - 1, D_MODEL), F32)], axis=0)
        act = _silu(cond)
        for i in range(n_layers):
            parts_scr[16 * i:16 * (i + 1), :] = _mm_f32(act, adaw_ref[i]) + adab_ref[i]
        start_parts()
        finish_parts()
        finish_weight()

    seven = [pltpu.SemaphoreType.DMA((7,)), pltpu.SemaphoreType.DMA((7,)), pltpu.SemaphoreType.DMA]
    return pl.pallas_call(
        body, name="prologue",
        out_shape=[jax.ShapeDtypeStruct((N_DEV * m_first, first.shape[1]), F32),
                   jax.ShapeDtypeStruct((N_DEV * 16 * n_layers, n_cols), F32), jax.ShapeDtypeStruct(W_SPECS[key][0], BF16)],
        in_specs=[VMEM] * 5, out_specs=[VMEM] * 3,
        scratch_shapes=[pltpu.VMEM((16 * n_layers, n_cols), F32)] + seven + seven
        + [pltpu.SemaphoreType.DMA((6,)), pltpu.SemaphoreType.DMA((6,)), pltpu.SemaphoreType.DMA],
        compiler_params=_params(48),
    )(first, c_ctx, ada_w, ada_b_cols.reshape(n_layers, 1, n_cols), w_shard)


def _slot_shapes(names):
    return [jax.ShapeDtypeStruct((N_DEV,) + _half_shape(k), BF16) for k in names]


def _scatter_grads(parts, names):
    n = len(names)

    def body(*refs):
        start, wait = _scatter_direct(names, refs[:n], refs[n:2 * n], *refs[2 * n:])
        start()
        wait()

    return pl.pallas_call(body, name="scatter_grads", out_shape=_slot_shapes(names), in_specs=[ANY] * n, out_specs=[ANY] * n,
                          scratch_shapes=_comm_sems(n))(*parts)


def _sum_and_exchange(slots, names, name):
    n = len(names)
    specs = [W_SPECS[k][1] for k in names]

    def shard_shape(k):
        shp = list(_half_shape(k))
        shp[W_SPECS[k][1][2]] *= 2
        return tuple(shp)

    def body(*refs):
        slot, out, half = refs[:n], refs[n:2 * n], refs[2 * n:3 * n]
        send_sems, recv_sems, local_sems = refs[3 * n:]
        x, y, c = _my_place()
        sibling = (x, y, 1 - c)
        for a in range(n):
            acc = slot[a][0].astype(F32)
            for d in range(1, slot[a].shape[0]):
                acc = acc + slot[a][d].astype(F32)
            half[a][...] = acc

        def remote(a, src, dst):
            return pltpu.make_async_remote_copy(src_ref=src, dst_ref=dst, send_sem=send_sems.at[a], recv_sem=recv_sems.at[a],
                                                device_id=sibling, device_id_type=MESH)

        local = [pltpu.make_async_copy(half[a], _half_of(out[a], specs[a], c), local_sems.at[a]) for a in range(n)]
        for cp in local:
            cp.start()
        for a in range(n):
            mine = _half_of(out[a], specs[a], c)
            for src, dst in zip(_pieces(half[a]), _pieces(mine)):
                remote(a, src, dst).start()
        for a in range(n):
            theirs = _half_of(out[a], specs[a], 1 - c)
            remote(a, theirs, theirs).wait_recv()
        for a in range(n):
            remote(a, half[a], half[a]).wait_send()
        for cp in local:
            cp.wait()

    return pl.pallas_call(
        body, name=name,
        out_shape=[jax.ShapeDtypeStruct(shard_shape(k), F32) for k in names],
        in_specs=[VMEM] * n, out_specs=[VMEM] * n,
        scratch_shapes=[pltpu.VMEM(_half_shape(k), F32) for k in names]
        + [pltpu.SemaphoreType.DMA((n,)), pltpu.SemaphoreType.DMA((n,)), pltpu.SemaphoreType.DMA((n,))],
        compiler_params=_params(40),
    )(*slots)


def _mod_parts(c_rows, ada_w, ada_b_cols):
    n_layers, _, n_cols = ada_w.shape

    def body(c_ref, w_ref, b_ref, o_ref):
        o_ref[...] = _mm_f32(_silu(c_ref[...]), w_ref[...]) + b_ref[...]

    return pl.pallas_call(
        body, name="mod_parts", grid=(n_layers,),
        out_shape=jax.ShapeDtypeStruct((n_layers, 16, n_cols), F32),
        in_specs=[pl.BlockSpec((16, D_MODEL), lambda i: (0, 0)),
                  pl.BlockSpec((None, D_MODEL, n_cols), lambda i: (i, 0, 0)),
                  pl.BlockSpec((None, 1, n_cols), lambda i: (i, 0, 0))],
        out_specs=pl.BlockSpec((None, 16, n_cols), lambda i: (i, 0, 0)),
        compiler_params=_params(40),
    )(c_rows, ada_w, ada_b_cols.reshape(n_layers, 1, n_cols))


def _ln_mod_matmul(ctx_tile, xin, nw, shift, scale, w, name):
    n_mod = shift.shape[0]
    skip = n_mod - 1
    tm = ROW_TILE if skip else 2 * ROW_TILE
    rows = xin.shape[0] + skip * tm
    n_sec = w.shape[1] // D_MODEL

    def body(*refs):
        c_ref = refs[0] if skip else None
        x_ref, nw_ref, sh_ref, sc_ref, w_ref, g_ref, h_ref = refs[skip:]
        xv = jnp.where(pl.program_id(0) == 0, c_ref[...], x_ref[...]) if skip else x_ref[...]
        r = lax.rsqrt(jnp.mean(xv * xv, axis=-1, keepdims=True) + EPS)
        h_f32 = (xv * r * nw_ref[...]) * (1.0 + sc_ref[...]) + sh_ref[...]
        h = h_f32.astype(BF16)
        h_ref[...] = h
        for k in range(n_sec):
            g_ref[k] = jnp.dot(h, w_ref[:, k * D_MODEL:(k + 1) * D_MODEL], preferred_element_type=F32)

    mod_spec = pl.BlockSpec((None, 1, D_MODEL), lambda i: (jnp.minimum(i, n_mod - 1), 0, 0))
    return pl.pallas_call(
        body, name=name, grid=(rows // tm,),
        out_shape=[pltpu.HBM((n_sec, rows, D_MODEL), F32), pltpu.HBM((rows, D_MODEL), BF16)],
        in_specs=[pl.BlockSpec((tm, D_MODEL), lambda i: (0, 0))] * skip
        + [pl.BlockSpec((tm, D_MODEL), lambda i: (jnp.maximum(i - skip, 0), 0)),
           pl.BlockSpec((1, D_MODEL), lambda i: (0, 0)),
           mod_spec, mod_spec,
           pl.BlockSpec((D_MODEL, n_sec * D_MODEL), lambda i: (0, 0))],
        out_specs=[pl.BlockSpec((n_sec, tm, D_MODEL), lambda i: (0, i, 0)),
                   pl.BlockSpec((tm, D_MODEL), lambda i: (i, 0))],
        compiler_params=_params(48),
    )(*_pin(*([ctx_tile] * skip), xin, nw, shift.reshape(n_mod, 1, D_MODEL), scale.reshape(n_mod, 1, D_MODEL), w))


def _chunk_masks(rev):
    rid = lax.broadcasted_iota(jnp.int32, (CHUNK, CHUNK), 0)
    cid = lax.broadcasted_iota(jnp.int32, (CHUNK, CHUNK), 1)
    keep = (cid >= rid) if rev else (cid <= rid)
    keep_t = (cid <= rid) if rev else (cid >= rid)
    one, zero = jnp.ones((CHUNK, CHUNK), F32), jnp.zeros((CHUNK, CHUNK), F32)
    return keep, jnp.where(keep, one, zero).astype(BF16), jnp.where(keep_t, one, zero).astype(BF16)


def _chunk_rows(t, rev, latent):
    n = N_LAT_CHUNKS if latent else N_CTX_CHUNKS
    base = CTX_LEN if latent else 0
    idx = (n - 1 - t) if rev else t
    return pl.multiple_of(base + idx * CHUNK, CHUNK)


def _gates(fpre, lb):
    sg = _sig(fpre)
    f = lb + (1.0 - lb) * sg
    return sg, f, 1.0 - f, jnp.log(f)


G_SPEC = lambda sec: pl.BlockSpec((None, ROWS_ALL, HEAD_DIM), lambda h, sec=sec: (sec, 0, h))


def _gla_forward(g5, lb_logits):
    q_scale = HEAD_DIM ** -0.5

    def body(ff_ref, fb_ref, v_ref, q_ref, lg_ref, o_ref, st_ref, decay_ref, qt_ref):
        for rev in (False, True):
            f_ref = fb_ref if rev else ff_ref
            lb = _sig(lg_ref[1:2, :] if rev else lg_ref[0:1, :])
            keep, tri, _ = _chunk_masks(rev)
            last = 0 if rev else CHUNK - 1
            mid = CHUNK // 2 if rev else CHUNK // 2 - 1

            def local_step(t, carry, latent):
                r0 = _chunk_rows(t, rev, latent)
                rows = pl.ds(r0, CHUNK)
                step = t + (N_CTX_CHUNKS if latent else 0)
                _, _, k, lf = _gates(f_ref[rows, :], lb)
                v = v_ref[rows, :]
                b = _mm_exact_lhs(tri, lf)
                bl = b[last:last + 1, :]
                if latent:
                    q = _silu(q_ref[rows, :]) * q_scale
                    bm = b[mid:mid + 1, :]
                    a = _mm_nt(q * jnp.exp(b - bm), k * jnp.exp(bm - b))
                    o = _mm(jnp.where(keep, a, 0.0), v)
                    orow = pl.ds(pl.multiple_of(r0 - CTX_LEN, CHUNK), CHUNK)
                    qt_ref[orow, :] = (q * jnp.exp(b)).astype(BF16)
                    if rev:
                        o_ref[orow, :] += o
                    else:
                        o_ref[orow, :] = o
                decay_ref[step] = jnp.exp(bl)
                st_ref[step] = _mm_tn(v, k * jnp.exp(bl - b))
                return carry

            lax.fori_loop(0, N_CTX_CHUNKS, functools.partial(local_step, latent=False), 0, unroll=2)
            lax.fori_loop(0, N_LAT_CHUNKS, functools.partial(local_step, latent=True), 0, unroll=4)

            def scan_step(t, st):
                update = st_ref[t]
                st_ref[t] = st
                return st * decay_ref[t] + update

            lax.fori_loop(0, N_CHUNKS, scan_step, jnp.zeros((HEAD_DIM, HEAD_DIM), F32), unroll=2)

            def inter_step(t, carry):
                r0 = _chunk_rows(t, rev, True)
                orow = pl.ds(pl.multiple_of(r0 - CTX_LEN, CHUNK), CHUNK)
                o_ref[orow, :] += lax.dot_general(qt_ref[orow, :], st_ref[t + N_CTX_CHUNKS].astype(BF16),
                                                  (((1,), (1,)), ((), ())), preferred_element_type=F32)
                return carry

            lax.fori_loop(0, N_LAT_CHUNKS, inter_step, 0, unroll=4)

    return pl.pallas_call(
        body, name="gla_forward", grid=(HEADS,),
        out_shape=jax.ShapeDtypeStruct((SEQ, D_MODEL), F32),
        in_specs=[G_SPEC(0), G_SPEC(1), G_SPEC(2), G_SPEC(3), pl.BlockSpec((2, HEAD_DIM), lambda h: (0, h))],
        out_specs=pl.BlockSpec((SEQ, HEAD_DIM), lambda h: (0, h)),
        scratch_shapes=[pltpu.VMEM((N_CHUNKS, HEAD_DIM, HEAD_DIM), F32), pltpu.VMEM((N_CHUNKS, 1, HEAD_DIM), F32),
                        pltpu.VMEM((SEQ, HEAD_DIM), BF16)],
        compiler_params=_params(32),
    )(g5, g5, g5, g5, lb_logits)


def _gla_backward(g5, lb_logits, d_o, d_z):
    q_scale = HEAD_DIM ** -0.5

    def body(ff_ref, fb_ref, v_ref, q_ref, lg_ref, do_ref, dz_ref, dg_ref, dlg_ref, st_ref, dst_ref, decay_ref):
        dg_ref[3, 0:CTX_LEN, :] = jnp.zeros((CTX_LEN, HEAD_DIM), F32)
        dg_ref[4, 0:CTX_LEN, :] = jnp.zeros((CTX_LEN, HEAD_DIM), F32)
        dg_ref[4, CTX_LEN:ROWS_ALL, :] = dz_ref[...]
        is_row = lax.broadcasted_iota(jnp.int32, (CHUNK, HEAD_DIM), 0)
        for rev in (False, True):
            d = 1 if rev else 0
            f_ref = fb_ref if rev else ff_ref
            lb = _sig(lg_ref[d:d + 1, :])
            keep, tri, tri_t = _chunk_masks(rev)
            last = 0 if rev else CHUNK - 1
            mid = CHUNK // 2 if rev else CHUNK // 2 - 1

            def local_step(t, carry, latent):
                r0 = _chunk_rows(t, rev, latent)
                rows = pl.ds(r0, CHUNK)
                step = t + (N_CTX_CHUNKS if latent else 0)
                _, _, k, lf = _gates(f_ref[rows, :], lb)
                b = _mm_exact_lhs(tri, lf)
                bl = b[last:last + 1, :]
                decay_ref[step] = jnp.exp(bl)
                st_ref[step] = _mm_tn(v_ref[rows, :], k * jnp.exp(bl - b))
                if latent:
                    q_t = _silu(q_ref[rows, :]) * q_scale * jnp.exp(b)
                    dst_ref[step] = _mm_tn(do_ref[pl.ds(pl.multiple_of(r0 - CTX_LEN, CHUNK), CHUNK), :], q_t)
                else:
                    dst_ref[step] = jnp.zeros((HEAD_DIM, HEAD_DIM), F32)
                return carry

            lax.fori_loop(0, N_CTX_CHUNKS, functools.partial(local_step, latent=False), 0, unroll=2)
            lax.fori_loop(0, N_LAT_CHUNKS, functools.partial(local_step, latent=True), 0, unroll=4)

            def scan_step(i, carry):
                st, d_st = carry
                j = N_CHUNKS - 1 - i
                update, d_update = st_ref[i], dst_ref[j]
                st_ref[i] = st
                dst_ref[j] = d_st
                return st * decay_ref[i] + update, d_st * decay_ref[j] + d_update

            zero_state = jnp.zeros((HEAD_DIM, HEAD_DIM), F32)
            lax.fori_loop(0, N_CHUNKS, scan_step, (zero_state, zero_state))

            def grad_step(t, dlb, latent):
                r0 = _chunk_rows(t, rev, latent)
                rows = pl.ds(r0, CHUNK)
                step = t + (N_CTX_CHUNKS if latent else 0)
                sg, f, k, lf = _gates(f_ref[rows, :], lb)
                v = v_ref[rows, :]
                b = _mm_exact_lhs(tri, lf)
                bl = b[last:last + 1, :]
                e_end = jnp.exp(bl - b)
                k_end = k * e_end
                decay = jnp.exp(bl)
                d_st = dst_ref[step]
                st_prev = st_ref[step]
                d_kend = _mm(v, d_st)
                d_decay = jnp.sum(d_st * st_prev, axis=0, keepdims=True)
                t_kend = d_kend * k_end
                d_v = _mm_nt(k_end, d_st)
                d_k = d_kend * e_end
                d_b = -t_kend
                if latent:
                    qpre = q_ref[rows, :]
                    q = _silu(qpre) * q_scale
                    bm = b[mid:mid + 1, :]
                    e_b, e_qm, e_km = jnp.exp(b), jnp.exp(b - bm), jnp.exp(bm - b)
                    q_t, q_m, k_m = q * e_b, q * e_qm, k * e_km
                    a = jnp.where(keep, _mm_nt(q_m, k_m), 0.0)
                    d_out = do_ref[pl.ds(pl.multiple_of(r0 - CTX_LEN, CHUNK), CHUNK), :]
                    d_a = jnp.where(keep, _mm_nt(d_out, v), 0.0)
                    d_qm = _mm(d_a, k_m)
                    d_km = _mm_tn(d_a, q_m)
                    d_qt = _mm(d_out, st_prev)
                    d_v = d_v + _mm_tn(a, d_out)
                    d_k = d_k + d_km * e_km
                    d_b = d_b + d_qt * q_t + d_qm * q_m - d_km * k_m
                    d_q = d_qt * e_b + d_qm * e_qm
                at_last = jnp.sum(t_kend, axis=0, keepdims=True) + d_decay * decay
                d_b = d_b + jnp.where(is_row == last, at_last, 0.0)
                d_lf = _mm_exact_lhs(tri_t, d_b)
                d_f = d_lf / f - d_k
                dg_ref[d, rows, :] = d_f * (1.0 - lb) * sg * (1.0 - sg)
                if rev:
                    dg_ref[2, rows, :] += d_v
                else:
                    dg_ref[2, rows, :] = d_v
                if latent:
                    d_qpre = d_q * q_scale * _dsilu(qpre)
                    if rev:
                        dg_ref[3, rows, :] += d_qpre
                    else:
                        dg_ref[3, rows, :] = d_qpre
                return dlb + jnp.sum(d_f * (1.0 - sg), axis=0, keepdims=True)

            dlb = lax.fori_loop(0, N_LAT_CHUNKS, functools.partial(grad_step, latent=True), jnp.zeros((1, HEAD_DIM), F32),
                                unroll=2)
            dlb = lax.fori_loop(0, N_CTX_CHUNKS, functools.partial(grad_step, latent=False), dlb, unroll=2)
            dlg_ref[d:d + 1, :] = dlb * lb * (1.0 - lb)

    col = pl.BlockSpec((SEQ, HEAD_DIM), lambda h: (0, h))
    return pl.pallas_call(
        body, name="gla_backward", grid=(HEADS,),
        out_shape=[jax.ShapeDtypeStruct((HGRN_SECTIONS, ROWS_ALL, D_MODEL), F32), jax.ShapeDtypeStruct((2, D_MODEL), F32)],
        in_specs=[G_SPEC(0), G_SPEC(1), G_SPEC(2), G_SPEC(3), pl.BlockSpec((2, HEAD_DIM), lambda h: (0, h)), col, col],
        out_specs=[pl.BlockSpec((HGRN_SECTIONS, ROWS_ALL, HEAD_DIM), lambda h: (0, 0, h)),
                   pl.BlockSpec((2, HEAD_DIM), lambda h: (0, h))],
        scratch_shapes=[pltpu.VMEM((N_CHUNKS, HEAD_DIM, HEAD_DIM), F32), pltpu.VMEM((N_CHUNKS, HEAD_DIM, HEAD_DIM), F32),
                        pltpu.VMEM((N_CHUNKS, 1, HEAD_DIM), F32)],
        compiler_params=_params(48),
    )(g5, g5, g5, g5, lb_logits, d_o, d_z)


GROUP = 2 * CHUNK


def _group_masks():
    rid, cid = np.arange(GROUP)[:, None], np.arange(GROUP)[None, :]
    same = (rid >= CHUNK) == (cid >= CHUNK)
    sums, back = [], []
    for rev in (False, True):
        causal = (cid >= rid) if rev else (cid <= rid)
        anti = (cid <= rid) if rev else (cid >= rid)
        sums.append(same & causal)
        back.append(np.concatenate([same & anti, same & ~anti], axis=1))
    sums, back = np.stack(sums).astype(np.float32), np.stack(back).astype(np.float32)
    return jnp.asarray(sums, BF16), jnp.asarray(back, BF16), jnp.asarray(sums, F32)


MASK_SPECS = [pl.BlockSpec((2, GROUP, GROUP), lambda h: (0, 0, 0)), pl.BlockSpec((2, GROUP, 2 * GROUP), lambda h: (0, 0, 0)),
              pl.BlockSpec((2, GROUP, GROUP), lambda h: (0, 0, 0))]


def _group_sum(m_bf, a):
    hi, lo = _split2(a)
    r = jnp.dot(m_bf, jnp.concatenate([hi, lo], axis=1), preferred_element_type=F32)
    return r[:, :HEAD_DIM] + r[:, HEAD_DIM:]


def _chunk_row(a, pos):
    return jnp.concatenate([jnp.broadcast_to(a[c * CHUNK + pos:c * CHUNK + pos + 1, :], (CHUNK, HEAD_DIM)) for c in range(2)], axis=0)


def _by_chunk(a, second):
    return jnp.concatenate([jnp.where(second, 0.0, a), jnp.where(second, a, 0.0)], axis=1)


def _own_block(r):
    return jnp.concatenate([r[0:CHUNK, 0:HEAD_DIM], r[CHUNK:GROUP, HEAD_DIM:2 * HEAD_DIM]], axis=0)


def _scan_step_of(row_chunk, rev, latent):
    if not rev:
        return row_chunk
    return (N_CHUNKS + N_CTX_CHUNKS - 1 - row_chunk) if latent else (N_CTX_CHUNKS - 1 - row_chunk)


def _group_rows(i, j, per_step, latent):
    base = CTX_LEN if latent else 0
    return pl.multiple_of(base + (i * per_step + j) * GROUP, GROUP)


GROUPS_PER_STEP = 8
GROUPS_PER_GRAD_STEP = 4


def _gla_forward(g5, lb_logits, late_shards):
    q_scale = HEAD_DIM ** -0.5
    per_lat, per_ctx = GROUPS_PER_STEP, min(GROUPS_PER_STEP, CTX_LEN // GROUP)
    n_late = len(GATHER_LATE)

    def body(ff_ref, fb_ref, v_ref, q_ref, lg_ref, sums_ref, _, keep_ref, *rest):
        shard_refs, o_ref, full_refs = rest[:n_late], rest[n_late], rest[n_late + 1:2 * n_late + 1]
        st_ref, decay_ref, qt_ref = rest[2 * n_late + 1:2 * n_late + 4]
        land_refs = rest[2 * n_late + 4:3 * n_late + 4]
        start_gather, forward_gather, finish_gather = _gather_two_level(GATHER_LATE, shard_refs, land_refs, full_refs,
                                                                        *rest[3 * n_late + 4:])

        @pl.when(pl.program_id(0) == 0)
        def _():
            start_gather()

        @pl.when(pl.program_id(0) == HEADS - 1)
        def _():
            forward_gather()

        second = lax.broadcasted_iota(jnp.int32, (GROUP, HEAD_DIM), 0) >= CHUNK
        for rev in (False, True):
            f_ref = fb_ref if rev else ff_ref
            lb = _sig(lg_ref[1:2, :] if rev else lg_ref[0:1, :])
            d = 1 if rev else 0
            last = 0 if rev else CHUNK - 1
            mid = CHUNK // 2 if rev else CHUNK // 2 - 1

            def local_step(i, carry, latent, per):
                r0s = [_group_rows(i, j, per, latent) for j in range(per)]
                rows = [pl.ds(r0, GROUP) for r0 in r0s]
                gates = [_gates(f_ref[r, :], lb) for r in rows]
                vs = [v_ref[r, :] for r in rows]
                bs = [_group_sum(sums_ref[d], g[3]) for g in gates]
                bls = [_chunk_row(b, last) for b in bs]
                ups = [_mm_tn(v, _by_chunk(g[2] * jnp.exp(bl - b), second)) for v, g, b, bl in zip(vs, gates, bs, bls)]
                if latent:
                    qs = [_silu(q_ref[r, :]) * q_scale for r in rows]
                    bms = [_chunk_row(b, mid) for b in bs]
                    a_s = [_mm_nt(q * jnp.exp(b - bm), g[2] * jnp.exp(bm - b)) for q, g, b, bm in zip(qs, gates, bs, bms)]
                    outs = [_mm(a * keep_ref[d], v) for a, v in zip(a_s, vs)]
                for j in range(per):
                    for c in range(2):
                        step = _scan_step_of(r0s[j] // CHUNK + c, rev, latent)
                        decay_ref[step] = jnp.exp(bls[j][c * CHUNK:c * CHUNK + 1, :])
                        st_ref[step] = ups[j][:, c * HEAD_DIM:(c + 1) * HEAD_DIM]
                    if latent:
                        orow = pl.ds(pl.multiple_of(r0s[j] - CTX_LEN, GROUP), GROUP)
                        qt_ref[orow, :] = (qs[j] * jnp.exp(bs[j])).astype(BF16)
                        if rev:
                            o_ref[orow, :] += outs[j]
                        else:
                            o_ref[orow, :] = outs[j]
                return carry

            lax.fori_loop(0, CTX_LEN // (per_ctx * GROUP), functools.partial(local_step, latent=False, per=per_ctx), 0)
            lax.fori_loop(0, SEQ // (per_lat * GROUP), functools.partial(local_step, latent=True, per=per_lat), 0)

            def scan_step(t, st):
                update = st_ref[t]
                st_ref[t] = st
                return st * decay_ref[t] + update

            lax.fori_loop(0, N_CHUNKS, scan_step, jnp.zeros((HEAD_DIM, HEAD_DIM), F32), unroll=2)

            def inter_step(i, carry):
                r0s = [_group_rows(i, j, per_lat, True) for j in range(per_lat)]
                orows = [pl.ds(pl.multiple_of(r0 - CTX_LEN, GROUP), GROUP) for r0 in r0s]
                states = [jnp.concatenate([st_ref[_scan_step_of(r0 // CHUNK + c, rev, True)] for c in range(2)], axis=0)
                          for r0 in r0s]
                prods = [lax.dot_general(qt_ref[orow, :], s.astype(BF16), (((1,), (1,)), ((), ())), preferred_element_type=F32)
                         for orow, s in zip(orows, states)]
                for orow, r in zip(orows, prods):
                    o_ref[orow, :] += _own_block(r)
                return carry

            lax.fori_loop(0, SEQ // (per_lat * GROUP), inter_step, 0)

        @pl.when(pl.program_id(0) == HEADS - 1)
        def _():
            finish_gather()

    outs = pl.pallas_call(
        body, name="gla_forward", grid=(HEADS,),
        out_shape=[pltpu.HBM((SEQ, D_MODEL), F32)] + [jax.ShapeDtypeStruct(W_SPECS[k][0], BF16) for k in GATHER_LATE],
        in_specs=[G_SPEC(0), G_SPEC(1), G_SPEC(2), G_SPEC(3), pl.BlockSpec((2, HEAD_DIM), lambda h: (0, h))] + MASK_SPECS
        + [ANY] * n_late,
        out_specs=[pl.BlockSpec((SEQ, HEAD_DIM), lambda h: (0, h))] + [ANY] * n_late,
        scratch_shapes=[pltpu.VMEM((N_CHUNKS, HEAD_DIM, HEAD_DIM), F32), pltpu.VMEM((N_CHUNKS, 1, HEAD_DIM), F32),
                        pltpu.VMEM((SEQ, HEAD_DIM), BF16)] + [pltpu.VMEM(W_SPECS[k][0], BF16) for k in GATHER_LATE]
        + [pltpu.SemaphoreType.DMA((2 * n_late,)), pltpu.SemaphoreType.DMA((2 * n_late,)), pltpu.SemaphoreType.DMA((n_late,))],
        compiler_params=_params(48),
    )(*_pin(g5, g5, g5, g5, lb_logits), *_group_masks(), *late_shards)
    return outs[0], dict(zip(GATHER_LATE, outs[1:]))


def _gla_backward(g5, lb_logits, d_o, d_z, late_parts):
    q_scale = HEAD_DIM ** -0.5
    per_lat, per_ctx = GROUPS_PER_STEP, min(GROUPS_PER_STEP, CTX_LEN // GROUP)
    n_late = len(GATHER_LATE)

    def body(ff_ref, fb_ref, v_ref, q_ref, lg_ref, do_ref, dz_ref, sums_ref, back_ref, keep_ref, *rest):
        part_refs, (dg_ref, dlg_ref), slot_refs = rest[:n_late], rest[n_late:n_late + 2], rest[n_late + 2:2 * n_late + 2]
        st_ref, dst_ref, decay_ref, both_ref = rest[2 * n_late + 2:2 * n_late + 6]
        start_scatter, wait_scatter = _scatter_direct(GATHER_LATE, part_refs, slot_refs, *rest[2 * n_late + 6:])

        @pl.when(pl.program_id(0) == 0)
        def _():
            start_scatter()

        dg_ref[3, 0:CTX_LEN, :] = jnp.zeros((CTX_LEN, HEAD_DIM), BF16)
        dg_ref[4, 0:CTX_LEN, :] = jnp.zeros((CTX_LEN, HEAD_DIM), BF16)
        dg_ref[4, CTX_LEN:ROWS_ALL, :] = dz_ref[...]
        second = lax.broadcasted_iota(jnp.int32, (GROUP, HEAD_DIM), 0) >= CHUNK
        for rev in (False, True):
            d = 1 if rev else 0
            f_ref = fb_ref if rev else ff_ref
            lb = _sig(lg_ref[d:d + 1, :])
            last = 0 if rev else CHUNK - 1
            mid = CHUNK // 2 if rev else CHUNK // 2 - 1

            def local_step(i, carry, latent, per):
                r0s = [_group_rows(i, j, per, latent) for j in range(per)]
                rows = [pl.ds(r0, GROUP) for r0 in r0s]
                gates = [_gates(f_ref[r, :], lb) for r in rows]
                bs = [_group_sum(sums_ref[d], g[3]) for g in gates]
                bls = [_chunk_row(b, last) for b in bs]
                ups = [_mm_tn(v_ref[r, :], _by_chunk(g[2] * jnp.exp(bl - b), second)) for r, g, b, bl in zip(rows, gates, bs, bls)]
                if latent:
                    orows = [pl.ds(pl.multiple_of(r0 - CTX_LEN, GROUP), GROUP) for r0 in r0s]
                    d_ups = [_mm_tn(do_ref[orow, :], _by_chunk(_silu(q_ref[r, :]) * q_scale * jnp.exp(b), second))
                             for orow, r, b in zip(orows, rows, bs)]
                for j in range(per):
                    for c in range(2):
                        step = _scan_step_of(r0s[j] // CHUNK + c, rev, latent)
                        decay_ref[step] = jnp.exp(bls[j][c * CHUNK:c * CHUNK + 1, :])
                        st_ref[step] = ups[j][:, c * HEAD_DIM:(c + 1) * HEAD_DIM]
                        if latent:
                            dst_ref[step] = d_ups[j][:, c * HEAD_DIM:(c + 1) * HEAD_DIM]
                        else:
                            dst_ref[step] = jnp.zeros((HEAD_DIM, HEAD_DIM), F32)
                return carry

            lax.fori_loop(0, CTX_LEN // (per_ctx * GROUP), functools.partial(local_step, latent=False, per=per_ctx), 0)
            lax.fori_loop(0, SEQ // (per_lat * GROUP), functools.partial(local_step, latent=True, per=per_lat), 0)

            def scan_step(i, carry):
                st, d_st = carry
                j = N_CHUNKS - 1 - i
                update, d_update = st_ref[i], dst_ref[j]
                st_ref[i] = st
                dst_ref[j] = d_st
                return st * decay_ref[i] + update, d_st * decay_ref[j] + d_update

            zero_state = jnp.zeros((HEAD_DIM, HEAD_DIM), F32)
            lax.fori_loop(0, N_CHUNKS, scan_step, (zero_state, zero_state))

            def grad_step(i, dlb, latent, per):
                r0s = [_group_rows(i, j, per, latent) for j in range(per)]
                rows = [pl.ds(r0, GROUP) for r0 in r0s]
                gates = [_gates(f_ref[r, :], lb) for r in rows]
                vs = [v_ref[r, :] for r in rows]
                bs = [_group_sum(sums_ref[d], g[3]) for g in gates]
                bls = [_chunk_row(b, last) for b in bs]
                e_ends = [jnp.exp(bl - b) for b, bl in zip(bs, bls)]
                k_ends = [g[2] * e for g, e in zip(gates, e_ends)]
                sts = [[st_ref[_scan_step_of(r0 // CHUNK + c, rev, latent)] for c in range(2)] for r0 in r0s]
                d_sts = [[dst_ref[_scan_step_of(r0 // CHUNK + c, rev, latent)] for c in range(2)] for r0 in r0s]
                d_kends = [_own_block(_mm(v, jnp.concatenate(ds, axis=1))) for v, ds in zip(vs, d_sts)]
                d_vs = [_own_block(_mm_nt(ke, jnp.concatenate(ds, axis=0))) for ke, ds in zip(k_ends, d_sts)]
                at_last = [jnp.concatenate([jnp.broadcast_to(jnp.sum(ds[c] * s[c], axis=0, keepdims=True), (CHUNK, HEAD_DIM))
                                            for c in range(2)], axis=0) * jnp.exp(bl) for ds, s, bl in zip(d_sts, sts, bls)]
                t_kends = [dk * ke for dk, ke in zip(d_kends, k_ends)]
                d_ks = [dk * e for dk, e in zip(d_kends, e_ends)]
                if latent:
                    orows = [pl.ds(pl.multiple_of(r0 - CTX_LEN, GROUP), GROUP) for r0 in r0s]
                    qpres = [q_ref[r, :] for r in rows]
                    q_sigs = [_sig(qp) for qp in qpres]
                    qs = [qp * sg * q_scale for qp, sg in zip(qpres, q_sigs)]
                    bms = [_chunk_row(b, mid) for b in bs]
                    e_bs = [jnp.exp(b) for b in bs]
                    e_qms = [jnp.exp(b - bm) for b, bm in zip(bs, bms)]
                    e_kms = [jnp.exp(bm - b) for b, bm in zip(bs, bms)]
                    q_ts = [q * e for q, e in zip(qs, e_bs)]
                    q_ms = [q * e for q, e in zip(qs, e_qms)]
                    k_ms = [g[2] * e for g, e in zip(gates, e_kms)]
                    d_outs = [do_ref[orow, :] for orow in orows]
                    a_s = [_mm_nt(qm, km) * keep_ref[d] for qm, km in zip(q_ms, k_ms)]
                    d_as = [_mm_nt(do, v) * keep_ref[d] for do, v in zip(d_outs, vs)]
                    d_qts = [_own_block(_mm(do, jnp.concatenate(s, axis=1))) for do, s in zip(d_outs, sts)]
                    d_qms = [_mm(da, km) for da, km in zip(d_as, k_ms)]
                    d_kms = [_mm_tn(da, qm) for da, qm in zip(d_as, q_ms)]
                    d_vs = [dv + _mm_tn(a, do) for dv, a, do in zip(d_vs, a_s, d_outs)]
                    d_ks = [dk + dkm * e for dk, dkm, e in zip(d_ks, d_kms, e_kms)]
                    d_bs = [jnp.concatenate([dqt * qt + dqm * qm - dkm * km, t], axis=0)
                            for dqt, qt, dqm, qm, dkm, km, t in zip(d_qts, q_ts, d_qms, q_ms, d_kms, k_ms, t_kends)]
                    d_qs = [dqt * eb + dqm * eq for dqt, eb, dqm, eq in zip(d_qts, e_bs, d_qms, e_qms)]
                    back = back_ref[d]
                else:
                    d_bs, back = t_kends, back_ref[d, :, GROUP:2 * GROUP]
                d_lfs = [_group_sum(back, db) + al for db, al in zip(d_bs, at_last)]
                for j in range(per):
                    sg, f = gates[j][0], gates[j][1]
                    d_f = d_lfs[j] / f - d_ks[j]
                    dg_ref[d, rows[j], :] = (d_f * (1.0 - lb) * sg * (1.0 - sg)).astype(BF16)
                    dlb = dlb + jnp.sum(d_f * (1.0 - sg), axis=0, keepdims=True)
                    if rev:
                        dg_ref[2, rows[j], :] = (both_ref[0, rows[j], :] + d_vs[j]).astype(BF16)
                    else:
                        both_ref[0, rows[j], :] = d_vs[j]
                    if latent:
                        d_qpre = d_qs[j] * q_scale * (q_sigs[j] * (1.0 + qpres[j] * (1.0 - q_sigs[j])))
                        if rev:
                            dg_ref[3, rows[j], :] = (both_ref[1, rows[j], :] + d_qpre).astype(BF16)
                        else:
                            both_ref[1, rows[j], :] = d_qpre
                return dlb

            dlb = lax.fori_loop(0, SEQ // (GROUPS_PER_GRAD_STEP * GROUP),
                                functools.partial(grad_step, latent=True, per=GROUPS_PER_GRAD_STEP), jnp.zeros((1, HEAD_DIM), F32))
            dlb = lax.fori_loop(0, CTX_LEN // (per_ctx * GROUP), functools.partial(grad_step, latent=False, per=per_ctx), dlb)
            dlg_ref[d:d + 1, :] = dlb * lb * (1.0 - lb)

        @pl.when(pl.program_id(0) == HEADS - 1)
        def _():
            wait_scatter()

    col = pl.BlockSpec((SEQ, HEAD_DIM), lambda h: (0, h))
    outs = pl.pallas_call(
        body, name="gla_backward", grid=(HEADS,),
        out_shape=[pltpu.HBM((HGRN_SECTIONS, ROWS_ALL, D_MODEL), BF16), jax.ShapeDtypeStruct((2, D_MODEL), F32)]
        + _slot_shapes(GATHER_LATE),
        in_specs=[G_SPEC(0), G_SPEC(1), G_SPEC(2), G_SPEC(3), pl.BlockSpec((2, HEAD_DIM), lambda h: (0, h)), col, col]
        + MASK_SPECS + [ANY] * n_late,
        out_specs=[pl.BlockSpec((HGRN_SECTIONS, ROWS_ALL, HEAD_DIM), lambda h: (0, 0, h)),
                   pl.BlockSpec((2, HEAD_DIM), lambda h: (0, h))] + [ANY] * n_late,
        scratch_shapes=[pltpu.VMEM((N_CHUNKS, HEAD_DIM, HEAD_DIM), F32), pltpu.VMEM((N_CHUNKS, HEAD_DIM, HEAD_DIM), F32),
                        pltpu.VMEM((N_CHUNKS, 1, HEAD_DIM), F32), pltpu.VMEM((2, ROWS_ALL, HEAD_DIM), F32)]
        + _comm_sems(n_late),
        compiler_params=_params(48),
    )(*_pin(g5, g5, g5, g5, lb_logits, d_o, d_z), *_group_masks(), *late_parts)
    return outs[0], outs[1], dict(zip(GATHER_LATE, outs[2:]))


def _head_norm(o, gw, scr):
    rs = []
    for h in range(HEADS):
        cols = slice(h * HEAD_DIM, (h + 1) * HEAD_DIM)
        oh = o[:, cols]
        r = lax.rsqrt(jnp.mean(oh * oh, axis=-1, keepdims=True) + EPS)
        scr[:, cols] = oh * r
        rs.append(r)
    return rs


def _hgrn_out_forward(o_raw, g5, xin, gnorm_w, gate, w_out):
    tm = ROW_TILE

    def body(o_ref, z_ref, x_ref, gw_ref, gate_ref, w_ref, x1_ref, res_ref, scr):
        _head_norm(o_ref[...], None, scr)
        a = scr[...] * gw_ref[...] * _silu(z_ref[...])
        res = _mm(a, w_ref[...])
        res_ref[...] = res
        x1_ref[...] = x_ref[...] + gate_ref[...] * res

    tile = pl.BlockSpec((tm, D_MODEL), lambda i: (i, 0))
    vec = pl.BlockSpec((1, D_MODEL), lambda i: (0, 0))
    return pl.pallas_call(
        body, name="hgrn_out_forward", grid=(SEQ // tm,),
        out_shape=[pltpu.HBM((SEQ, D_MODEL), F32)] * 2,
        in_specs=[tile, pl.BlockSpec((None, tm, D_MODEL), lambda i: (4, i + CTX_LEN // tm, 0)), tile, vec, vec,
                  pl.BlockSpec((D_MODEL, D_MODEL), lambda i: (0, 0))],
        out_specs=[tile, tile],
        scratch_shapes=[pltpu.VMEM((tm, D_MODEL), F32)],
        compiler_params=_params(32),
    )(*_pin(o_raw, g5, xin, gnorm_w, gate, w_out))


def _hgrn_out_backward(d_x1, o_raw, g5, res, gnorm_w, gate, w_out):
    tm = ROW_TILE

    def body(dx_ref, o_ref, z_ref, res_ref, gw_ref, gate_ref, w_ref, do_ref, dz_ref, dw_out, dgate_ref, dgw_ref, scr, scr2,
             dw_ref):
        @pl.when(pl.program_id(0) == 0)
        def _():
            dw_ref[...] = jnp.zeros_like(dw_ref)
            dgate_ref[...] = jnp.zeros_like(dgate_ref)
            dgw_ref[...] = jnp.zeros_like(dgw_ref)

        dx = dx_ref[...]
        dgate_ref[...] += jnp.sum(dx * res_ref[...], axis=0, keepdims=True)
        d_res = (dx * gate_ref[...]).astype(BF16)
        d_a = _mm_nt(d_res, w_ref[...])
        rs = _head_norm(o_ref[...], None, scr)
        z = z_ref[...]
        sz = _silu(z)
        o_hat = scr[...]
        o_n = o_hat * gw_ref[...]
        dw_ref[...] += _mm_tn(o_n * sz, d_res)
        d_on = d_a * sz
        dz_ref[...] = (d_a * o_n * _dsilu(z)).astype(BF16)
        dgw_ref[...] += jnp.sum(d_on * o_hat, axis=0, keepdims=True)
        scr2[...] = d_on * gw_ref[...]
        for h in range(HEADS):
            cols = slice(h * HEAD_DIM, (h + 1) * HEAD_DIM)
            dh, oh = scr2[:, cols], scr[:, cols]
            do_ref[:, cols] = (rs[h] * (dh - oh * jnp.mean(dh * oh, axis=-1, keepdims=True))).astype(BF16)

        @pl.when(pl.program_id(0) == SEQ // tm - 1)
        def _():
            dw_out[...] = dw_ref[...].astype(BF16)

    tile = pl.BlockSpec((tm, D_MODEL), lambda i: (i, 0))
    vec = pl.BlockSpec((1, D_MODEL), lambda i: (0, 0))
    mat = pl.BlockSpec((D_MODEL, D_MODEL), lambda i: (0, 0))
    return pl.pallas_call(
        body, name="hgrn_out_backward", grid=(SEQ // tm,),
        out_shape=[pltpu.HBM((SEQ, D_MODEL), BF16)] * 2 + [pltpu.HBM((D_MODEL, D_MODEL), BF16)]
        + [jax.ShapeDtypeStruct((1, D_MODEL), F32)] * 2,
        in_specs=[tile, tile, pl.BlockSpec((None, tm, D_MODEL), lambda i: (4, i + CTX_LEN // tm, 0)), tile, vec, vec, mat],
        out_specs=[tile, tile, mat, vec, vec],
        scratch_shapes=[pltpu.VMEM((tm, D_MODEL), F32)] * 2 + [pltpu.VMEM((D_MODEL, D_MODEL), F32)],
        compiler_params=_params(40),
    )(*_pin(d_x1, o_raw, g5, res, gnorm_w, gate, w_out))


def _pool_constants():
    win = np.zeros((POOL_GROUPS, ROW_TILE, ROW_TILE), np.float32)
    inv = np.zeros((POOL_GROUPS, ROW_TILE, 1), np.float32)
    for g, w in enumerate(POOL_WINDOWS):
        for t in range(ROW_TILE):
            base, p = (t // GRID_W) * GRID_W, t % GRID_W
            lo = min(max(p - w // 2, 0), GRID_W)
            hi = min(max(p - w // 2 + w, 0), GRID_W)
            win[g, t, base + lo:base + hi] = 1.0
            inv[g, t, 0] = 1.0 / np.float32(hi - lo)
    return jnp.asarray(win, BF16), jnp.asarray(win.transpose(0, 2, 1), BF16), jnp.asarray(inv, F32)


def _pool_mix(u_ref, wg_ref, ps_ref, win_ref, inv_ref, pooled_scr, yg_scr):
    for g in range(POOL_GROUPS):
        cols = slice(g * POOL_GROUP_DIM, (g + 1) * POOL_GROUP_DIM)
        ug = u_ref[:, cols]
        pooled = _mm_exact_lhs(win_ref[g], ug) * inv_ref[g] - ug
        if pooled_scr is not None:
            pooled_scr[:, cols] = pooled
        yg_scr[:, cols] = _mm(pooled, wg_ref[g])


def _pool_forward_loss(uz, x1, target, gate, w_grp, pool_scale, w_out, final_w):
    tm = ROW_TILE
    win, _, inv = _pool_constants()

    def body(u_ref, z_ref, x_ref, t_ref, gate_ref, wg_ref, ps_ref, w_ref, fw_ref, win_ref, inv_ref,
             dx_ref, loss_ref, dfw_ref, dgate_ref, yg_scr):
        @pl.when(pl.program_id(0) == 0)
        def _():
            loss_ref[...] = jnp.zeros_like(loss_ref)
            dfw_ref[...] = jnp.zeros_like(dfw_ref)
            dgate_ref[...] = jnp.zeros_like(dgate_ref)

        _pool_mix(u_ref, wg_ref, ps_ref, win_ref, inv_ref, None, yg_scr)
        a = yg_scr[...] * ps_ref[...] * _silu(z_ref[...])
        res = _mm(a, w_ref[...])
        x2 = x_ref[...] + gate_ref[...] * res
        r = lax.rsqrt(jnp.mean(x2 * x2, axis=-1, keepdims=True) + EPS)
        xh = x2 * r
        fw = fw_ref[...]
        err = xh * fw - t_ref[...]
        loss_ref[...] += 0.5 * jnp.sum(jnp.mean(err * err, axis=-1, keepdims=True))
        d_y = err * (1.0 / D_MODEL)
        dfw_ref[...] += jnp.sum(d_y * xh, axis=0, keepdims=True)
        d_xh = d_y * fw
        d_x2 = r * (d_xh - xh * jnp.mean(d_xh * xh, axis=-1, keepdims=True))
        dx_ref[...] = d_x2
        dgate_ref[...] += jnp.sum(d_x2 * res, axis=0, keepdims=True)

    tile = pl.BlockSpec((tm, D_MODEL), lambda i: (i, 0))
    vec = pl.BlockSpec((1, D_MODEL), lambda i: (0, 0))
    grp = pl.BlockSpec((POOL_GROUPS, POOL_GROUP_DIM, POOL_GROUP_DIM), lambda i: (0, 0, 0))
    return pl.pallas_call(
        body, name="pool_forward_loss", grid=(SEQ // tm,),
        out_shape=[pltpu.HBM((SEQ, D_MODEL), F32), jax.ShapeDtypeStruct((8, 128), F32),
                   jax.ShapeDtypeStruct((1, D_MODEL), F32), jax.ShapeDtypeStruct((1, D_MODEL), F32)],
        in_specs=[pl.BlockSpec((None, tm, D_MODEL), lambda i: (0, i, 0)), pl.BlockSpec((None, tm, D_MODEL), lambda i: (1, i, 0)),
                  tile, tile, vec, grp, vec, pl.BlockSpec((D_MODEL, D_MODEL), lambda i: (0, 0)), vec, grp,
                  pl.BlockSpec((POOL_GROUPS, ROW_TILE, 1), lambda i: (0, 0, 0))],
        out_specs=[tile, pl.BlockSpec((8, 128), lambda i: (0, 0)), vec, vec],
        scratch_shapes=[pltpu.VMEM((tm, D_MODEL), F32)],
        compiler_params=_params(32),
    )(*_pin(uz, uz, x1, target, gate, w_grp, pool_scale, w_out, final_w, win, inv))


def _pool_backward(d_x2, uz, gate, w_grp, pool_scale, w_out):
    tm = ROW_TILE
    win, win_t, inv = _pool_constants()

    def body(dx_ref, u_ref, z_ref, gate_ref, wg_ref, ps_ref, w_ref, win_ref, wint_ref, inv_ref,
             duz_ref, dw_out, dwg_out, dps_ref, pooled_scr, yg_scr, dyg_scr, dw_ref, dwg_ref):
        @pl.when(pl.program_id(0) == 0)
        def _():
            dw_ref[...] = jnp.zeros_like(dw_ref)
            dwg_ref[...] = jnp.zeros_like(dwg_ref)
            dps_ref[...] = jnp.zeros_like(dps_ref)

        _pool_mix(u_ref, wg_ref, ps_ref, win_ref, inv_ref, pooled_scr, yg_scr)
        z = z_ref[...]
        sz = _silu(z)
        yg = yg_scr[...]
        y = yg * ps_ref[...]
        d_res = (dx_ref[...] * gate_ref[...]).astype(BF16)
        d_a = _mm_nt(d_res, w_ref[...])
        dw_ref[...] += _mm_tn(y * sz, d_res)
        d_y = d_a * sz
        duz_ref[1] = (d_a * y * _dsilu(z)).astype(BF16)
        dps_ref[...] += jnp.sum(d_y * yg, axis=0, keepdims=True)
        dyg_scr[...] = d_y * ps_ref[...]
        for g in range(POOL_GROUPS):
            cols = slice(g * POOL_GROUP_DIM, (g + 1) * POOL_GROUP_DIM)
            d_yg = dyg_scr[:, cols].astype(BF16)
            d_pool = _mm_nt(d_yg, wg_ref[g])
            dwg_ref[g] += _mm_tn(pooled_scr[:, cols], d_yg)
            duz_ref[0, :, cols] = (_mm_exact_lhs(wint_ref[g], d_pool * inv_ref[g]) - d_pool).astype(BF16)

        @pl.when(pl.program_id(0) == SEQ // tm - 1)
        def _():
            dw_out[...] = dw_ref[...].astype(BF16)
            dwg_out[...] = dwg_ref[...].astype(BF16)

    tile = pl.BlockSpec((tm, D_MODEL), lambda i: (i, 0))
    vec = pl.BlockSpec((1, D_MODEL), lambda i: (0, 0))
    mat = pl.BlockSpec((D_MODEL, D_MODEL), lambda i: (0, 0))
    grp = pl.BlockSpec((POOL_GROUPS, POOL_GROUP_DIM, POOL_GROUP_DIM), lambda i: (0, 0, 0))
    return pl.pallas_call(
        body, name="pool_backward", grid=(SEQ // tm,),
        out_shape=[pltpu.HBM((POOL_SECTIONS, SEQ, D_MODEL), BF16), pltpu.HBM((D_MODEL, D_MODEL), BF16),
                   pltpu.HBM((POOL_GROUPS, POOL_GROUP_DIM, POOL_GROUP_DIM), BF16), jax.ShapeDtypeStruct((1, D_MODEL), F32)],
        in_specs=[tile, pl.BlockSpec((None, tm, D_MODEL), lambda i: (0, i, 0)), pl.BlockSpec((None, tm, D_MODEL), lambda i: (1, i, 0)),
                  vec, grp, vec, mat, grp, grp, pl.BlockSpec((POOL_GROUPS, ROW_TILE, 1), lambda i: (0, 0, 0))],
        out_specs=[pl.BlockSpec((POOL_SECTIONS, tm, D_MODEL), lambda i: (0, i, 0)), mat, grp, vec],
        scratch_shapes=[pltpu.VMEM((tm, D_MODEL), F32)] * 3 + [pltpu.VMEM((D_MODEL, D_MODEL), F32),
                                                               pltpu.VMEM((POOL_GROUPS, POOL_GROUP_DIM, POOL_GROUP_DIM), F32)],
        compiler_params=_params(40),
    )(*_pin(d_x2, uz, uz, gate, w_grp, pool_scale, w_out, win, win_t, inv))


def _ln_mod_backward(d_g, w, ctx_tile, xin, nw, scale, d_up, name):
    n_sec, rows, _ = d_g.shape
    n_mod = scale.shape[0]
    skip = n_mod - 1
    tm = ROW_TILE if skip else 2 * ROW_TILE
    n_tiles = rows // tm

    def body(dg_ref, w_ref, *refs):
        c_ref = refs[0] if skip else None
        x_ref, nw_ref, sc_ref, up_ref, dx_ref, dnw_ref, dmod_ref = refs[skip:]
        i = pl.program_id(0)

        @pl.when(i == 0)
        def _():
            dnw_ref[...] = jnp.zeros_like(dnw_ref)

        @pl.when((i == 0) | (i == skip))
        def _():
            dmod_ref[...] = jnp.zeros_like(dmod_ref)

        d_h = _mm_nt(dg_ref[0], w_ref[:, 0:D_MODEL])
        for k in range(1, n_sec):
            d_h = d_h + _mm_nt(dg_ref[k], w_ref[:, k * D_MODEL:(k + 1) * D_MODEL])
        xv = jnp.where(i == 0, c_ref[...], x_ref[...]) if skip else x_ref[...]
        r = lax.rsqrt(jnp.mean(xv * xv, axis=-1, keepdims=True) + EPS)
        xh = xv * r
        nw_row = nw_ref[...]
        dmod_ref[0:1, :] += jnp.sum(d_h, axis=0, keepdims=True)
        dmod_ref[1:2, :] += jnp.sum(d_h * (xh * nw_row), axis=0, keepdims=True)
        d_xn = d_h * (1.0 + sc_ref[...])
        dnw_ref[...] += jnp.sum(d_xn * xh, axis=0, keepdims=True)
        d_xh = d_xn * nw_row

        @pl.when(i >= skip)
        def _():
            dx_ref[...] = up_ref[...] + r * (d_xh - xh * jnp.mean(d_xh * xh, axis=-1, keepdims=True))

    lat = lambda i: (jnp.maximum(i - skip, 0), 0)
    mod_idx = lambda i: (jnp.minimum(i, n_mod - 1), 0, 0)
    return pl.pallas_call(
        body, name=name, grid=(n_tiles,),
        out_shape=[pltpu.HBM((rows - skip * tm, D_MODEL), F32), jax.ShapeDtypeStruct((1, D_MODEL), F32),
                   jax.ShapeDtypeStruct((n_mod, 8, D_MODEL), F32)],
        in_specs=[pl.BlockSpec((n_sec, tm, D_MODEL), lambda i: (0, i, 0)),
                  pl.BlockSpec((D_MODEL, n_sec * D_MODEL), lambda i: (0, 0))]
        + [pl.BlockSpec((tm, D_MODEL), lambda i: (0, 0))] * skip
        + [pl.BlockSpec((tm, D_MODEL), lat),
           pl.BlockSpec((1, D_MODEL), lambda i: (0, 0)),
           pl.BlockSpec((None, 1, D_MODEL), mod_idx),
           pl.BlockSpec((tm, D_MODEL), lat)],
        out_specs=[pl.BlockSpec((tm, D_MODEL), lat), pl.BlockSpec((1, D_MODEL), lambda i: (0, 0)),
                   pl.BlockSpec((None, 8, D_MODEL), mod_idx)],
        compiler_params=_params(48),
    )(*_pin(d_g, w, *([ctx_tile] * skip), xin, nw, scale.reshape(n_mod, 1, D_MODEL), d_up))


def _weight_grad(h, d_g, name):
    n_sec, rows, _ = d_g.shape
    tm = 768 if rows % 768 == 0 else 512
    n_tiles = rows // tm

    def body(h_ref, dg_ref, dw_ref, acc):
        i = pl.program_id(1)
        prod = _mm_tn(h_ref[...], dg_ref[...])

        @pl.when(i == 0)
        def _():
            acc[...] = prod

        @pl.when((i > 0) & (i < n_tiles - 1))
        def _():
            acc[...] += prod

        @pl.when(i == n_tiles - 1)
        def _():
            dw_ref[...] = (acc[...] + prod).astype(BF16)

    return pl.pallas_call(
        body, name=name, grid=(n_sec, n_tiles),
        out_shape=pltpu.HBM((D_MODEL, n_sec * D_MODEL), BF16),
        in_specs=[pl.BlockSpec((tm, D_MODEL), lambda j, i: (i, 0)), pl.BlockSpec((None, tm, D_MODEL), lambda j, i: (j, i, 0))],
        out_specs=pl.BlockSpec((D_MODEL, D_MODEL), lambda j, i: (0, j)),
        scratch_shapes=[pltpu.VMEM((D_MODEL, D_MODEL), F32)],
        compiler_params=_params(32),
    )(*_pin(h, d_g))


def _weight_grad_paired(h, d_g, name):
    n_sec, rows, _ = d_g.shape
    tm = 768
    n_tiles = rows // tm
    half = D_MODEL // 2

    def body(h_ref, dg_ref, q_ref, acc, keep, send, land, send_sems, recv_sems):
        j, i = pl.program_id(0), pl.program_id(1)
        x, y, c = _my_place()
        prod = _mm_tn(h_ref[...], dg_ref[...])

        def to_sibling(k):
            return pltpu.make_async_remote_copy(src_ref=send.at[k], dst_ref=land.at[k], send_sem=send_sems.at[k],
                                                recv_sem=recv_sems.at[k], device_id=(x, y, 1 - c), device_id_type=MESH)

        @pl.when(i == 0)
        def _():
            acc[...] = prod

        @pl.when((i > 0) & (i < n_tiles - 1))
        def _():
            acc[...] += prod

        @pl.when(i == n_tiles - 1)
        def _():
            acc[...] += prod
            keep[j] = acc[pl.ds(_al(c * half, half), half), :]
            send[j] = acc[pl.ds(_al((1 - c) * half, half), half), :].astype(BF16)
            to_sibling(j).start()

        @pl.when((j == n_sec - 1) & (i == n_tiles - 1))
        def _():
            for k in range(n_sec):
                to_sibling(k).wait()
                q_ref[:, k * D_MODEL:(k + 1) * D_MODEL] = (keep[k] + land[k].astype(F32)).astype(BF16)

    return pl.pallas_call(
        body, name=name, grid=(n_sec, n_tiles),
        out_shape=jax.ShapeDtypeStruct((half, n_sec * D_MODEL), BF16),
        in_specs=[pl.BlockSpec((tm, D_MODEL), lambda j, i: (i, 0)), pl.BlockSpec((None, tm, D_MODEL), lambda j, i: (j, i, 0))],
        out_specs=pl.BlockSpec((half, n_sec * D_MODEL), lambda j, i: (0, 0)),
        scratch_shapes=[pltpu.VMEM((D_MODEL, D_MODEL), F32), pltpu.VMEM((n_sec, half, D_MODEL), F32),
                        pltpu.VMEM((n_sec, half, D_MODEL), BF16), pltpu.VMEM((n_sec, half, D_MODEL), BF16),
                        pltpu.SemaphoreType.DMA((n_sec,)), pltpu.SemaphoreType.DMA((n_sec,))],
        compiler_params=_params(56),
    )(*_pin(h, d_g))


def _sum_slots(slots, name):
    n_slots, rows, cols = slots.shape
    tr = 128

    def body(s_ref, o_ref):
        acc = s_ref[0].astype(F32)
        for d in range(1, n_slots):
            acc = acc + s_ref[d].astype(F32)
        o_ref[...] = acc

    return pl.pallas_call(
        body, name=name, grid=(rows // tr,),
        out_shape=jax.ShapeDtypeStruct((rows, cols), F32),
        in_specs=[pl.BlockSpec((n_slots, tr, cols), lambda i: (0, i, 0))],
        out_specs=pl.BlockSpec((tr, cols), lambda i: (i, 0)),
    )(*_pin(slots))


def _adamw_math(w, g, m, v):
    m = ADAM_B1 * m + (1.0 - ADAM_B1) * g
    v = ADAM_B2 * v + (1.0 - ADAM_B2) * (g * g)
    m_hat = m / (1.0 - ADAM_B1 ** ADAM_STEP)
    v_hat = v / (1.0 - ADAM_B2 ** ADAM_STEP)
    return -ADAM_LR * (m_hat / (jnp.sqrt(v_hat) + ADAM_EPS) + ADAM_WD * w), m, v


def _adamw(w, g, m, v, name):
    rows, cols = w.shape
    tr = rows if rows <= 128 else 128

    def body(w_ref, g_ref, m_ref, v_ref, d_ref, mo_ref, vo_ref):
        d_ref[...], mo_ref[...], vo_ref[...] = _adamw_math(w_ref[...], g_ref[...], m_ref[...], v_ref[...])

    tile = pl.BlockSpec((tr, cols), lambda i: (i, 0))
    return pl.pallas_call(
        body, name=name, grid=(rows // tr,),
        out_shape=[pltpu.HBM((rows, cols), F32)] * 3,
        in_specs=[tile] * 4, out_specs=[tile] * 3,
    )(*_pin(w, g, m, v))


def _sum_devices(gathered):
    def body(p_ref, o_ref):
        acc = p_ref[0:SMALL_ROWS, :]
        for d in range(1, N_DEV):
            acc = acc + p_ref[d * SMALL_ROWS:(d + 1) * SMALL_ROWS, :]
        o_ref[...] = acc

    return pl.pallas_call(body, name="sum_devices", out_shape=jax.ShapeDtypeStruct((SMALL_ROWS, D_MODEL), F32),
                          in_specs=[VMEM], out_specs=VMEM)(gathered)


def _ada_update(cond_t, d_mod, ada_w, m, v):
    n_layers, _, n_cols = ada_w.shape
    tr = ROW_TILE

    def body(c_ref, dm_ref, w_ref, m_ref, v_ref, g_ref, d_ref, mo_ref, vo_ref):
        g = _mm_f32(_silu(c_ref[...]), dm_ref[...])
        g_ref[...] = g
        d_ref[...], mo_ref[...], vo_ref[...] = _adamw_math(w_ref[...], g, m_ref[...], v_ref[...])

    tile = pl.BlockSpec((None, tr, n_cols), lambda l, i: (l, i, 0))
    return pl.pallas_call(
        body, name="ada_update", grid=(n_layers, D_MODEL // tr),
        out_shape=[pltpu.HBM(ada_w.shape, F32)] * 4,
        in_specs=[pl.BlockSpec((tr, 16), lambda l, i: (i, 0)), pl.BlockSpec((None, 16, n_cols), lambda l, i: (l, 0, 0)),
                  tile, tile, tile],
        out_specs=[tile] * 4,
    )(*_pin(cond_t, d_mod, ada_w, m, v))


def _cond_ctx_partial(d_modc, ada_w0):
    n_cols = ada_w0.shape[1]
    tr = ROW_TILE

    def body(dm_ref, w_ref, o_ref):
        o_ref[...] = jnp.sum(w_ref[...] * dm_ref[...], axis=-1, keepdims=True)

    return pl.pallas_call(
        body, name="cond_ctx_partial", grid=(D_MODEL // tr,),
        out_shape=jax.ShapeDtypeStruct((D_MODEL, 1), F32),
        in_specs=[pl.BlockSpec((1, n_cols), lambda i: (0, 0)), pl.BlockSpec((tr, n_cols), lambda i: (i, 0))],
        out_specs=pl.BlockSpec((tr, 1), lambda i: (i, 0)),
    )(d_modc, ada_w0)


def _cond_ctx_update(gathered, c_ctx, m, v):
    def body(p_ref, w_ref, m_ref, v_ref, g_ref, d_ref, mo_ref, vo_ref):
        acc = p_ref[0:1, :]
        for s in range(1, N_CHIPS):
            acc = acc + p_ref[16 * s:16 * s + 1, :]
        w = w_ref[...]
        g = acc * _dsilu(w)
        g_ref[...] = g
        d_ref[...], mo_ref[...], vo_ref[...] = _adamw_math(w, g, m_ref[...], v_ref[...])

    return pl.pallas_call(body, name="cond_ctx_update", out_shape=[jax.ShapeDtypeStruct((1, D_MODEL), F32)] * 4,
                          in_specs=[VMEM] * 4, out_specs=[VMEM] * 4)(gathered, c_ctx, m, v)


def _local_step(x2, ctx2, target, mod_mine, mod_ctx, lb_logits, scale_full, w_in_full, late_shards, norm_w, gnorm, final_w):
    row = lambda a: a.reshape(1, -1)
    shift0, scale0, gate0 = (row(a) for a in jnp.split(mod_mine[0], 3))
    shift1, scale1, gate1 = (row(a) for a in jnp.split(mod_mine[1], 3))
    shift_c, scale_c, _ = (row(a) for a in jnp.split(mod_ctx, 3))
    nw0, nw1 = norm_w[0:1], norm_w[1:2]
    scales0 = jnp.concatenate([scale_c, scale0])

    g5, h0 = _ln_mod_matmul(ctx2, x2, nw0, jnp.concatenate([shift_c, shift0]), scales0, w_in_full, "hgrn_in_forward")
    o_raw, full = _gla_forward(g5, lb_logits, late_shards)
    x1, res0 = _hgrn_out_forward(o_raw, g5, x2, gnorm, gate0, full["hgrn_w_out"])
    uz, h1 = _ln_mod_matmul(None, x1, nw1, shift1, scale1, full["pool_w_in"], "pool_in_forward")
    d_x2, loss_part, d_final, d_gate1 = _pool_forward_loss(uz, x1, target, gate1, full["pool_w_grp"], scale_full,
                                                           full["pool_w_out"], final_w)

    d_uz, dw_pool_out, dw_pool_grp, d_pscale = _pool_backward(d_x2, uz, gate1, full["pool_w_grp"], scale_full, full["pool_w_out"])
    d_x1, d_nw1, d_mod1 = _ln_mod_backward(d_uz, full["pool_w_in"], None, x1, nw1, scale1, d_x2, "pool_in_backward")
    dw_pool_in = _weight_grad(h1, d_uz, "pool_in_weight_grad")
    d_o, d_z, dw_hgrn_out, d_gate0, d_gnorm = _hgrn_out_backward(d_x1, o_raw, g5, res0, gnorm, gate0, full["hgrn_w_out"])
    late_grads = {"hgrn_w_out": dw_hgrn_out, "pool_w_in": dw_pool_in, "pool_w_grp": dw_pool_grp, "pool_w_out": dw_pool_out}
    d_g5, d_lb, late_slots = _gla_backward(g5, lb_logits, d_o, d_z, [late_grads[k].astype(BF16) for k in GATHER_LATE])
    dw_hgrn_in = _weight_grad_paired(h0, d_g5, "hgrn_in_weight_grad")
    key = GATHER_EARLY[0]
    half = _half_shape(key)
    xi, yi, _ = _my_place()
    own = lax.dynamic_slice(dw_hgrn_in, (0, (2 * xi + yi) * half[1]), half)
    slots0 = lax.dynamic_update_slice(jnp.zeros((N_CHIPS,) + half, BF16), own[None], (2 * xi + yi, 0, 0))
    send_sem, recv_sem, part_thru, slots_thru, token = _scatter_start(dw_hgrn_in, slots0, own, key)
    d_x, d_nw0, d_mod0 = _ln_mod_backward(d_g5, w_in_full, ctx2, x2, nw0 + token[0:1, 0:1], scales0, d_x1, "hgrn_in_backward")
    slots = dict(late_slots)
    pending = (send_sem, recv_sem, part_thru, slots_thru)

    zero = jnp.zeros((1, D_MODEL), F32)
    small = jnp.concatenate([d_mod0[1, 0:2], d_gate0, d_mod1[0, 0:2], d_gate1, d_mod0[0, 0:2], zero, d_nw0, d_nw1, d_gnorm,
                             d_final, d_pscale, d_lb, jnp.broadcast_to(loss_part[0:1, 0:1], (1, D_MODEL)),
                             jnp.zeros((SMALL_ROWS - 17, D_MODEL), F32)], axis=0)
    return {"d_x": d_x, "slots": slots, "pending": pending, "small": small}


def kernel(x, c, ctx, c_ctx, ada_w, ada_b, norm_w, hgrn_w_in, hgrn_lb_logits, hgrn_gnorm_w, hgrn_w_out, pool_w_in, pool_w_grp, pool_scale, pool_w_out, final_norm_w, loss_target, m_c_ctx, m_ada_w, m_ada_b, m_norm_w, m_hgrn_w_in, m_hgrn_lb_logits, m_hgrn_gnorm_w, m_hgrn_w_out, m_pool_w_in, m_pool_w_grp, m_pool_scale, m_pool_w_out, m_final_norm_w, v_c_ctx, v_ada_w, v_ada_b, v_norm_w, v_hgrn_w_in, v_hgrn_lb_logits, v_hgrn_gnorm_w, v_hgrn_w_out, v_pool_w_in, v_pool_w_grp, v_pool_scale, v_pool_w_out, v_final_norm_w):
    xi, yi, ci = _my_place()
    chip = 2 * xi + yi
    dev = 4 * xi + 2 * yi + ci
    ada_cols = ada_w.shape[2]
    lb_cols = hgrn_lb_logits.shape[2]
    ps_cols = pool_scale.shape[1]
    row = lambda a: a.reshape(1, -1)

    def chip_cols(a, n):
        return lax.dynamic_slice_in_dim(a, chip * n, n, axis=a.ndim - 1)

    def from_chips(g, rows_per_dev, take):
        return jnp.concatenate([g[2 * s * rows_per_dev:2 * s * rows_per_dev + take] for s in range(N_CHIPS)], axis=1)

    first = jnp.concatenate([jnp.broadcast_to(c, (8, D_MODEL)), jnp.pad(hgrn_lb_logits[0], ((0, 6), (0, 0))),
                             jnp.pad(pool_scale, ((0, 7), (0, 0)))], axis=1)
    shards = {"hgrn_w_in": hgrn_w_in[0], "hgrn_w_out": hgrn_w_out[0], "pool_w_in": pool_w_in[0],
              "pool_w_grp": pool_w_grp[0], "pool_w_out": pool_w_out[0]}
    first_all, parts_all, w_in_full = _prologue(first, row(c_ctx), ada_w, chip_cols(ada_b, ada_cols),
                                                shards[GATHER_EARLY[0]].astype(BF16))
    cond_all = first_all[::8, :D_MODEL]
    lb_logits = from_chips(first_all[:, D_MODEL:D_MODEL + lb_cols], 8, 2)
    scale_full = from_chips(first_all[:, D_MODEL + lb_cols:], 8, 1)
    cond_rows = jnp.concatenate([cond_all, row(c_ctx), jnp.zeros((7, D_MODEL), F32)], axis=0)
    mod_all = from_chips(parts_all, 32, 32).reshape(2, 16, 3 * D_MODEL)
    mod_mine = lax.dynamic_index_in_dim(mod_all, dev, axis=1, keepdims=False)

    loc = _local_step(x[0], ctx[0], loss_target[0], mod_mine, mod_all[0, 8], lb_logits, scale_full, w_in_full,
                      [shards[k].astype(BF16) for k in GATHER_LATE], norm_w, hgrn_gnorm_w, row(final_norm_w))

    small_all = _all_gather_small(loc["small"], "gather_small")
    sums = _sum_devices(small_all)
    loss = sums[16, 0]

    part_c = _cond_ctx_partial(chip_cols(sums[6:9].reshape(1, -1), ada_cols), ada_w[0])
    part_all = _all_gather_small(jnp.pad(part_c.reshape(1, D_MODEL), ((0, 7), (0, 0))), "gather_cond_ctx")

    def reduce_scattered(slots, names, name):
        return dict(zip(names, _sum_and_exchange([slots[k] for k in names], names, name)))

    big_grads = reduce_scattered(loc["slots"], GATHER_LATE, "exchange_halves_late")

    out = {}

    def update(name, w, g, m, v):
        shape = w.shape
        w2, g2, m2, v2 = (a.reshape(-1, shape[-1]) for a in (w, g, m, v))
        d, mn, vn = _adamw(w2, g2, m2, v2, "adamw_" + name)
        out[name] = tuple(a.reshape(shape) for a in (g2, d, mn, vn))

    moments = {"hgrn_w_in": (m_hgrn_w_in, v_hgrn_w_in), "hgrn_w_out": (m_hgrn_w_out, v_hgrn_w_out),
               "pool_w_in": (m_pool_w_in, v_pool_w_in), "pool_w_grp": (m_pool_w_grp, v_pool_w_grp),
               "pool_w_out": (m_pool_w_out, v_pool_w_out)}
    weights = {"hgrn_w_in": hgrn_w_in, "hgrn_w_out": hgrn_w_out, "pool_w_in": pool_w_in, "pool_w_grp": pool_w_grp,
               "pool_w_out": pool_w_out}
    for k in GATHER_LATE:
        update(k, weights[k], big_grads[k], *moments[k])

    g_ada_b = jnp.stack([(sums[0:3] + sums[6:9]).reshape(-1), sums[3:6].reshape(-1)])
    update("ada_b", ada_b, g_ada_b, m_ada_b, v_ada_b)
    update("norm_w", norm_w, sums[9:11], m_norm_w, v_norm_w)
    update("hgrn_gnorm_w", hgrn_gnorm_w, sums[11:12], m_hgrn_gnorm_w, v_hgrn_gnorm_w)
    update("final_norm_w", row(final_norm_w), sums[12:13], row(m_final_norm_w), row(v_final_norm_w))
    update("pool_scale", pool_scale, chip_cols(sums[13:14], ps_cols), m_pool_scale, v_pool_scale)
    update("hgrn_lb_logits", hgrn_lb_logits, chip_cols(sums[14:16], lb_cols)[None], m_hgrn_lb_logits, v_hgrn_lb_logits)

    per_dev = small_all.reshape(N_DEV, SMALL_ROWS, D_MODEL)
    pad7 = jnp.zeros((7, 3 * D_MODEL), F32)
    dm0 = jnp.concatenate([per_dev[:, 0:3].reshape(N_DEV, -1), sums[6:9].reshape(1, -1), pad7], axis=0)
    dm1 = jnp.concatenate([per_dev[:, 3:6].reshape(N_DEV, -1), jnp.zeros((8, 3 * D_MODEL), F32)], axis=0)
    d_mod = chip_cols(jnp.stack([dm0, dm1]), ada_cols)
    out["ada_w"] = _ada_update(cond_rows.T, d_mod, ada_w, m_ada_w, v_ada_w)

    g_c, d_c, m_c, v_c = _cond_ctx_update(part_all, row(c_ctx), row(m_c_ctx), row(v_c_ctx))
    out["c_ctx"] = tuple(a.reshape(-1) for a in (g_c, d_c, m_c, v_c))
    out["final_norm_w"] = tuple(a.reshape(-1) for a in out["final_norm_w"])

    done = [g_c, out["ada_w"][1]] + [out[k][1] for k in GATHER_LATE]
    key = GATHER_EARLY[0]
    _, early = _scatter_wait(*loc["pending"], done, key)
    big_grads = reduce_scattered({key: early}, GATHER_EARLY, "exchange_halves_early")
    for k in GATHER_EARLY:
        update(k, weights[k], big_grads[k], *moments[k])

    names = ["c_ctx", "ada_w", "ada_b", "norm_w", "hgrn_w_in", "hgrn_lb_logits", "hgrn_gnorm_w", "hgrn_w_out", "pool_w_in",
             "pool_w_grp", "pool_scale", "pool_w_out", "final_norm_w"]
    return (loss, loc["d_x"][None], *[out[k][0] for k in names], *[out[k][1] for k in names], *[out[k][2] for k in names],
            *[out[k][3] for k in names])
```

```python
import functools

import numpy as np
import jax
import jax.numpy as jnp
from jax import lax
from jax.experimental import pallas as pl
from jax.experimental.pallas import tpu as pltpu

F32 = jnp.float32
BF16 = jnp.bfloat16

D_MODEL = 1024
SEQ = 2048
CTX_LEN = 256
ROWS_ALL = CTX_LEN + SEQ
HEADS = 8
HEAD_DIM = 128
CHUNK = 64
N_CTX_CHUNKS = CTX_LEN // CHUNK
N_LAT_CHUNKS = SEQ // CHUNK
N_CHUNKS = N_CTX_CHUNKS + N_LAT_CHUNKS
GRID_W = 64
POOL_WINDOWS = (2, 4, 8, 16)
POOL_GROUPS = 4
POOL_GROUP_DIM = 256
HGRN_SECTIONS = 5
POOL_SECTIONS = 2
EPS = 1e-6
N_DEV = 8
N_CHIPS = 4
ROW_TILE = 256
SMALL_ROWS = 24

ADAM_LR = 0.001
ADAM_B1 = 0.9
ADAM_B2 = 0.999
ADAM_EPS = 1e-08
ADAM_WD = 0.01
ADAM_STEP = 10

MESH = pl.DeviceIdType.MESH
MIB = 1 << 20
ANY = pl.BlockSpec(memory_space=pl.ANY)
VMEM = pl.BlockSpec(memory_space=pltpu.VMEM)


def _params(vmem_mib=None):
    if vmem_mib is None:
        return pltpu.CompilerParams()
    return pltpu.CompilerParams(vmem_limit_bytes=vmem_mib * MIB)


def _pin(*operands):
    return [pltpu.with_memory_space_constraint(a, pltpu.HBM) if a.size * a.dtype.itemsize >= MIB else a for a in operands]


def _sig(a):
    return 1.0 / (1.0 + jnp.exp(-a))


def _silu(a):
    return a * _sig(a)


def _dsilu(a):
    s = _sig(a)
    return s * (1.0 + a * (1.0 - s))


def _mm(a, b):
    return jnp.dot(a.astype(BF16), b.astype(BF16), preferred_element_type=F32)


def _mm_nt(a, b):
    return lax.dot_general(a.astype(BF16), b.astype(BF16), (((1,), (1,)), ((), ())), preferred_element_type=F32)


def _mm_tn(a, b):
    return lax.dot_general(a.astype(BF16), b.astype(BF16), (((0,), (0,)), ((), ())), preferred_element_type=F32)


def _split2(a):
    hi = a.astype(BF16)
    lo = (a - hi.astype(F32)).astype(BF16)
    return hi, lo


def _mm_exact_lhs(m_bf, a):
    hi, lo = _split2(a)
    return jnp.dot(m_bf, hi, preferred_element_type=F32) + jnp.dot(m_bf, lo, preferred_element_type=F32)


def _mm_f32(a, b):
    ah, al = _split2(a)
    bh, bl = _split2(b)
    return (jnp.dot(ah, bh, preferred_element_type=F32) + jnp.dot(al, bh, preferred_element_type=F32)
            + jnp.dot(ah, bl, preferred_element_type=F32))


def _my_place():
    return lax.axis_index("x"), lax.axis_index("y"), lax.axis_index("c")


def _small_gather(x_ref, out_ref, send_sems, recv_sems, local_sem):
    m_per = x_ref.shape[0]
    x, y, c = _my_place()
    me, sibling = (x, y, c), (x, y, 1 - c)
    chips = [(1 - x, y), (x, 1 - y), (1 - x, 1 - y)]

    def rows(px, py, pc):
        return out_ref.at[pl.ds((4 * px + 2 * py + pc) * m_per, m_per), :]

    def copy(k, block, to, src=None):
        return pltpu.make_async_remote_copy(
            src_ref=rows(*block) if src is None else src, dst_ref=rows(*block),
            send_sem=send_sems.at[k], recv_sem=recv_sems.at[k], device_id=to, device_id_type=MESH)

    def mine():
        return pltpu.make_async_copy(x_ref, rows(*me), local_sem)

    def first():
        return [copy(0, me, sibling, src=x_ref)] + [copy(1 + j, me, (*chip, c), src=x_ref) for j, chip in enumerate(chips)]

    def start():
        mine().start()
        for cp in first():
            cp.start()

    def finish():
        passed = [copy(4 + j, (*chip, c), sibling) for j, chip in enumerate(chips)]
        for j, chip in enumerate(chips):
            copy(1 + j, (*chip, c), me).wait_recv()
            passed[j].start()
        copy(0, sibling, me).wait_recv()
        for j, chip in enumerate(chips):
            copy(4 + j, (*chip, 1 - c), me).wait_recv()
        for cp in first() + passed:
            cp.wait_send()
        mine().wait()

    return start, finish


def _all_gather_small(blk, name):
    m_per, n = blk.shape

    def body(x_ref, out_ref, send_sems, recv_sems, local_sem):
        start, finish = _small_gather(x_ref, out_ref, send_sems, recv_sems, local_sem)
        start()
        finish()

    return pl.pallas_call(
        body, name=name,
        out_shape=jax.ShapeDtypeStruct((N_DEV * m_per, n), blk.dtype),
        in_specs=[VMEM], out_specs=VMEM,
        scratch_shapes=[pltpu.SemaphoreType.DMA((7,)), pltpu.SemaphoreType.DMA((7,)), pltpu.SemaphoreType.DMA],
    )(blk)


W_SPECS = {
    "hgrn_w_in": ((D_MODEL, 5 * D_MODEL), (1, 1280, 0, 512)),
    "hgrn_w_out": ((D_MODEL, D_MODEL), (0, 256, 0, 128)),
    "pool_w_in": ((D_MODEL, 2 * D_MODEL), (1, 512, 0, 512)),
    "pool_w_grp": ((POOL_GROUPS, POOL_GROUP_DIM, POOL_GROUP_DIM), (1, 64, 1, 32)),
    "pool_w_out": ((D_MODEL, D_MODEL), (0, 256, 0, 128)),
}
W_NAMES = tuple(W_SPECS)


def _al(v, m):
    return pl.multiple_of(v, m)


def _region(ref, spec, chip, half):
    ca, cn, ha, hn = spec
    idx = [slice(None)] * len(ref.shape)
    if ca == ha:
        if half is None:
            idx[ca] = pl.ds(_al(chip * cn, cn), cn)
        else:
            idx[ca] = pl.ds(_al(chip * cn + half * hn, hn), hn)
    else:
        idx[ca] = pl.ds(_al(chip * cn, cn), cn)
        if half is not None:
            idx[ha] = pl.ds(_al(half * hn, hn), hn)
    return ref.at[tuple(idx)]


def _half_of(ref, spec, half):
    _, _, ha, hn = spec
    idx = [slice(None)] * len(ref.shape)
    idx[ha] = pl.ds(_al(half * hn, hn), hn)
    return ref.at[tuple(idx)]


PIECE_BYTES = 256 * 1024


def _pieces(ref):
    lead = ref.shape[0]
    want = (int(np.prod(ref.shape)) * ref.dtype.itemsize) // PIECE_BYTES
    n = max([1] + [k for k in range(1, want + 1) if lead % k == 0 and (lead // k) % 16 == 0])
    rows = lead // n
    return [ref.at[pl.ds(i * rows, rows)] for i in range(n)]


def _half_shape(name):
    full, (ca, cn, ha, hn) = W_SPECS[name]
    shp = list(full)
    shp[ca] = cn
    shp[ha] = hn
    return tuple(shp)


def _gather_direct(names, sh, full, send_sems, recv_sems, local_sems):
    specs = [W_SPECS[k][1] for k in names]
    x, y, c = _my_place()
    chip_me = 2 * x + y
    chips = [(1 - x, y), (x, 1 - y), (1 - x, 1 - y)]

    def local(a):
        return pltpu.make_async_copy(sh[a], _region(full[a], specs[a], chip_me, None), local_sems.at[a])

    def remote(a, src, dst, to):
        return pltpu.make_async_remote_copy(src_ref=src, dst_ref=dst, send_sem=send_sems.at[a], recv_sem=recv_sems.at[a],
                                            device_id=to, device_id_type=MESH)

    def start():
        for a in range(len(names)):
            local(a).start()
            for px, py in chips:
                remote(a, sh[a], _region(full[a], specs[a], chip_me, None), (px, py, c)).start()

    def wait():
        for a in range(len(names)):
            ca, cn, _, _ = specs[a]
            idx = [slice(None)] * len(full[a].shape)
            idx[ca] = pl.ds(0, 3 * cn)
            three = full[a].at[tuple(idx)]
            remote(a, three, three, (x, y, c)).wait()
            local(a).wait()

    return start, wait


def _gather_two_level(names, sh, land, full, send_sems, recv_sems, local_sems):
    n = len(names)
    specs = [W_SPECS[k][1] for k in names]
    x, y, c = _my_place()
    chip_me = 2 * x + y
    sibling = (x, y, 1 - c)
    chips = [(1 - x, y), (x, 1 - y), (1 - x, 1 - y)]

    def remote(k, src, dst, to):
        return pltpu.make_async_remote_copy(src_ref=src, dst_ref=dst, send_sem=send_sems.at[k], recv_sem=recv_sems.at[k],
                                            device_id=to, device_id_type=MESH)

    def three_halves(a):
        ca, cn, ha, hn = specs[a]
        idx = [slice(None)] * len(land[a].shape)
        if ca == ha:
            idx[ca] = pl.ds(0, 3 * hn)
        else:
            idx[ca], idx[ha] = pl.ds(0, 3 * cn), pl.ds(0, hn)
        return land[a].at[tuple(idx)]

    def own(a):
        return pltpu.make_async_copy(sh[a], _region(land[a], specs[a], chip_me, None), local_sems.at[a])

    def out(a):
        return pltpu.make_async_copy(land[a], full[a], local_sems.at[a])

    def start():
        for a in range(n):
            own(a).start()
            for px, py in chips:
                remote(a, _half_of(sh[a], specs[a], c), _region(land[a], specs[a], chip_me, c), (px, py, c)).start()

    def forward():
        for a in range(n):
            remote(a, three_halves(a), three_halves(a), sibling).wait_recv()
            for px, py in chips:
                landed = _region(land[a], specs[a], 2 * px + py, c)
                remote(n + a, landed, landed, sibling).start()

    def finish():
        for a in range(n):
            remote(n + a, three_halves(a), three_halves(a), sibling).wait_recv()
            remote(a, three_halves(a), three_halves(a), sibling).wait_send()
            remote(n + a, three_halves(a), three_halves(a), sibling).wait_send()
            own(a).wait()
        for a in range(n):
            out(a).start()
        for a in range(n):
            out(a).wait()

    return start, forward, finish


def _scatter_direct(names, part, slots, send_sems, recv_sems, local_sems):
    specs = [W_SPECS[k][1] for k in names]
    x, y, c = _my_place()
    dev_me = 4 * x + 2 * y + c

    def local(a):
        return pltpu.make_async_copy(_region(part[a], specs[a], 2 * x + y, c), slots[a].at[dev_me], local_sems.at[a])

    def start():
        for a in range(len(names)):
            local(a).start()
            for flip in range(1, N_DEV):
                tx = 1 - x if flip >> 2 else x
                ty = 1 - y if (flip >> 1) & 1 else y
                tc = 1 - c if flip & 1 else c
                pltpu.make_async_remote_copy(src_ref=_region(part[a], specs[a], 2 * tx + ty, tc), dst_ref=slots[a].at[dev_me],
                                             send_sem=send_sems.at[a], recv_sem=recv_sems.at[a], device_id=(tx, ty, tc),
                                             device_id_type=MESH).start()

    def wait():
        for a in range(len(names)):
            seven = slots[a].at[pl.ds(0, N_DEV - 1)]
            pltpu.make_async_remote_copy(src_ref=seven, dst_ref=seven, send_sem=send_sems.at[a], recv_sem=recv_sems.at[a],
                                         device_id=(x, y, c), device_id_type=MESH).wait()
            local(a).wait()

    return start, wait


HBM_SPEC = pl.BlockSpec(memory_space=pltpu.HBM)
SEM_SPEC = pl.BlockSpec(memory_space=pltpu.SEMAPHORE)
SPLIT_EFFECT = pltpu.SideEffectType.DATAFLOW_SIDE_EFFECTING


def _scatter_start(part, slots, after, name_key):
    _, cols, _, _ = W_SPECS[name_key][1]

    def body(part_ref, slots_ref, after_ref, send_sem, recv_sem, part_thru, slots_thru, token):
        x, y, c = _my_place()
        for tx, ty in ((1 - x, y), (x, 1 - y), (1 - x, 1 - y)):
            src = part_ref.at[:, pl.ds(_al((2 * tx + ty) * cols, cols), cols)]
            pltpu.make_async_remote_copy(src_ref=src, dst_ref=slots_ref.at[2 * x + y], send_sem=send_sem, recv_sem=recv_sem,
                                         device_id=(tx, ty, c), device_id_type=MESH).start()
        token[...] = jnp.zeros_like(token)

    return pl.pallas_call(
        body, name="scatter_start_" + name_key,
        out_shape=(pltpu.SemaphoreType.DMA(()), pltpu.SemaphoreType.DMA(()), pltpu.HBM(part.shape, part.dtype),
                   pltpu.HBM(slots.shape, slots.dtype), jax.ShapeDtypeStruct((8, 128), F32)),
        in_specs=(HBM_SPEC, HBM_SPEC, ANY), out_specs=(SEM_SPEC, SEM_SPEC, HBM_SPEC, HBM_SPEC, VMEM),
        input_output_aliases={0: 2, 1: 3},
        compiler_params=pltpu.CompilerParams(has_side_effects=SPLIT_EFFECT),
    )(pltpu.with_memory_space_constraint(part, pltpu.HBM), pltpu.with_memory_space_constraint(slots, pltpu.HBM), after)


def _scatter_wait(send_sem, recv_sem, part_thru, slots_thru, after, name_key):
    def body(part_ref, slots_ref, send_sem, recv_sem, *rest):
        x, y, c = _my_place()
        landed = slots_ref.at[pl.ds(0, N_CHIPS - 1)]
        copy = pltpu.make_async_remote_copy(src_ref=landed, dst_ref=landed, send_sem=send_sem, recv_sem=recv_sem,
                                            device_id=(x, y, c), device_id_type=MESH)
        copy.wait_send()
        copy.wait_recv()

    return pl.pallas_call(
        body, name="scatter_wait_" + name_key,
        out_shape=(pltpu.HBM(part_thru.shape, part_thru.dtype), pltpu.HBM(slots_thru.shape, slots_thru.dtype)),
        in_specs=(HBM_SPEC, HBM_SPEC, SEM_SPEC, SEM_SPEC) + (ANY,) * len(after), out_specs=(HBM_SPEC, HBM_SPEC),
        input_output_aliases={0: 0, 1: 1},
        compiler_params=pltpu.CompilerParams(has_side_effects=SPLIT_EFFECT),
    )(part_thru, slots_thru, send_sem, recv_sem, *after)


def _comm_sems(n):
    return [pltpu.SemaphoreType.DMA((n,)), pltpu.SemaphoreType.DMA((n,)), pltpu.SemaphoreType.DMA((n,))]


GATHER_EARLY = ("hgrn_w_in",)
GATHER_LATE = ("hgrn_w_out", "pool_w_in", "pool_w_grp", "pool_w_out")


def _all_gather_weights(shards, names):
    n = len(names)
    specs = [W_SPECS[k][1] for k in names]

    def body(*refs):
        sh, full = refs[:n], refs[n:2 * n]
        send_sems, recv_sems, local_sems = refs[2 * n:]
        x, y, c = _my_place()
        chip_me = 2 * x + y
        sibling = (x, y, 1 - c)
        chips = [(1 - x, y), (x, 1 - y), (1 - x, 1 - y)]

        def remote(a, k, src, dst, to):
            return pltpu.make_async_remote_copy(src_ref=src, dst_ref=dst, send_sem=send_sems.at[6 * a + k],
                                                recv_sem=recv_sems.at[6 * a + k], device_id=to, device_id_type=MESH)

        local = [pltpu.make_async_copy(sh[a], _region(full[a], specs[a], chip_me, None), local_sems.at[a]) for a in range(n)]
        for cp in local:
            cp.start()
        sends = []
        for a in range(n):
            for j, (px, py) in enumerate(chips):
                src, dst = _half_of(sh[a], specs[a], c), _region(full[a], specs[a], chip_me, c)
                for s_piece, d_piece in zip(_pieces(src), _pieces(dst)):
                    remote(a, j, s_piece, d_piece, (px, py, c)).start()
                sends.append(remote(a, j, src, dst, (px, py, c)))
        for a in range(n):
            for j, (px, py) in enumerate(chips):
                landed = _region(full[a], specs[a], 2 * px + py, c)
                remote(a, j, landed, landed, (px, py, c)).wait_recv()
                for piece in _pieces(landed):
                    remote(a, 3 + j, piece, piece, sibling).start()
                sends.append(remote(a, 3 + j, landed, landed, sibling))
        for a in range(n):
            for j, (px, py) in enumerate(chips):
                other = _region(full[a], specs[a], 2 * px + py, 1 - c)
                remote(a, 3 + j, other, other, sibling).wait_recv()
        for cp in sends:
            cp.wait_send()
        for cp in local:
            cp.wait()

    return pl.pallas_call(
        body, name="all_gather_weights",
        out_shape=[jax.ShapeDtypeStruct(W_SPECS[k][0], BF16) for k in names],
        in_specs=[VMEM] * n, out_specs=[VMEM] * n,
        scratch_shapes=[pltpu.SemaphoreType.DMA((6 * n,)), pltpu.SemaphoreType.DMA((6 * n,)), pltpu.SemaphoreType.DMA((n,))],
        compiler_params=_params(32),
    )(*shards)


def _weight_gather(sh, full, spec, send_sems, recv_sems, local_sem):
    x, y, c = _my_place()
    chip_me = 2 * x + y
    sibling = (x, y, 1 - c)
    chips = [(1 - x, y), (x, 1 - y), (1 - x, 1 - y)]

    def remote(k, src, dst, to):
        return pltpu.make_async_remote_copy(src_ref=src, dst_ref=dst, send_sem=send_sems.at[k], recv_sem=recv_sems.at[k],
                                            device_id=to, device_id_type=MESH)

    def own():
        return pltpu.make_async_copy(sh, _region(full, spec, chip_me, None), local_sem)

    def direct():
        return [remote(j, _half_of(sh, spec, c), _region(full, spec, chip_me, c), (px, py, c)) for j, (px, py) in enumerate(chips)]

    def start():
        own().start()
        for cp in direct():
            cp.start()

    def finish():
        passed = []
        for j, (px, py) in enumerate(chips):
            landed = _region(full, spec, 2 * px + py, c)
            remote(j, landed, landed, (px, py, c)).wait_recv()
            passed.append(remote(3 + j, landed, landed, sibling))
            passed[-1].start()
        for j, (px, py) in enumerate(chips):
            other = _region(full, spec, 2 * px + py, 1 - c)
            remote(3 + j, other, other, sibling).wait_recv()
        for cp in direct() + passed:
            cp.wait_send()
        own().wait()

    return start, finish


def _prologue(first, c_ctx, ada_w, ada_b_cols, w_shard):
    n_layers, _, n_cols = ada_w.shape
    key = GATHER_EARLY[0]
    spec = W_SPECS[key][1]
    m_first = first.shape[0]

    def body(first_ref, cctx_ref, adaw_ref, adab_ref, sh_ref, first_all, parts_all, full_ref, parts_scr,
             s1, r1, l1, s2, r2, l2, ws, wr, wl):
        start_first, finish_first = _small_gather(first_ref, first_all, s1, r1, l1)
        start_parts, finish_parts = _small_gather(parts_scr, parts_all, s2, r2, l2)
        start_weight, finish_weight = _weight_gather(sh_ref, full_ref, spec, ws, wr, wl)
        start_first()
        start_weight()
        finish_first()
        cond = jnp.concatenate([first_all[m_first * d:m_first * d + 1, 0:D_MODEL] for d in range(N_DEV)]
                               + [cctx_ref[...], jnp.zeros((16 - N_DEV - 1, D_MODEL), F32)], axis=0)
        act = _silu(cond)
        for i in range(n_layers):
            parts_scr[16 * i:16 * (i + 1), :] = _mm_f32(act, adaw_ref[i]) + adab_ref[i]
        start_parts()
        finish_parts()
        finish_weight()

    seven = [pltpu.SemaphoreType.DMA((7,)), pltpu.SemaphoreType.DMA((7,)), pltpu.SemaphoreType.DMA]
    return pl.pallas_call(
        body, name="prologue",
        out_shape=[jax.ShapeDtypeStruct((N_DEV * m_first, first.shape[1]), F32),
                   jax.ShapeDtypeStruct((N_DEV * 16 * n_layers, n_cols), F32), jax.ShapeDtypeStruct(W_SPECS[key][0], BF16)],
        in_specs=[VMEM] * 5, out_specs=[VMEM] * 3,
        scratch_shapes=[pltpu.VMEM((16 * n_layers, n_cols), F32)] + seven + seven
        + [pltpu.SemaphoreType.DMA((6,)), pltpu.SemaphoreType.DMA((6,)), pltpu.SemaphoreType.DMA],
        compiler_params=_params(48),
    )(first, c_ctx, ada_w, ada_b_cols.reshape(n_layers, 1, n_cols), w_shard)


def _slot_shapes(names):
    return [jax.ShapeDtypeStruct((N_DEV,) + _half_shape(k), BF16) for k in names]


def _scatter_grads(parts, names):
    n = len(names)

    def body(*refs):
        start, wait = _scatter_direct(names, refs[:n], refs[n:2 * n], *refs[2 * n:])
        start()
        wait()

    return pl.pallas_call(body, name="scatter_grads", out_shape=_slot_shapes(names), in_specs=[ANY] * n, out_specs=[ANY] * n,
                          scratch_shapes=_comm_sems(n))(*parts)


def _sum_and_exchange(slots, names, name):
    n = len(names)
    specs = [W_SPECS[k][1] for k in names]

    def shard_shape(k):
        shp = list(_half_shape(k))
        shp[W_SPECS[k][1][2]] *= 2
        return tuple(shp)

    def body(*refs):
        slot, out, half = refs[:n], refs[n:2 * n], refs[2 * n:3 * n]
        send_sems, recv_sems, local_sems = refs[3 * n:]
        x, y, c = _my_place()
        sibling = (x, y, 1 - c)
        for a in range(n):
            acc = slot[a][0].astype(F32)
            for d in range(1, slot[a].shape[0]):
                acc = acc + slot[a][d].astype(F32)
            half[a][...] = acc

        def remote(a, src, dst):
            return pltpu.make_async_remote_copy(src_ref=src, dst_ref=dst, send_sem=send_sems.at[a], recv_sem=recv_sems.at[a],
                                                device_id=sibling, device_id_type=MESH)

        local = [pltpu.make_async_copy(half[a], _half_of(out[a], specs[a], c), local_sems.at[a]) for a in range(n)]
        for cp in local:
            cp.start()
        for a in range(n):
            mine = _half_of(out[a], specs[a], c)
            for src, dst in zip(_pieces(half[a]), _pieces(mine)):
                remote(a, src, dst).start()
        for a in range(n):
            theirs = _half_of(out[a], specs[a], 1 - c)
            remote(a, theirs, theirs).wait_recv()
        for a in range(n):
            remote(a, half[a], half[a]).wait_send()
        for cp in local:
            cp.wait()

    return pl.pallas_call(
        body, name=name,
        out_shape=[jax.ShapeDtypeStruct(shard_shape(k), F32) for k in names],
        in_specs=[VMEM] * n, out_specs=[VMEM] * n,
        scratch_shapes=[pltpu.VMEM(_half_shape(k), F32) for k in names]
        + [pltpu.SemaphoreType.DMA((n,)), pltpu.SemaphoreType.DMA((n,)), pltpu.SemaphoreType.DMA((n,))],
        compiler_params=_params(40),
    )(*slots)


def _mod_parts(c_rows, ada_w, ada_b_cols):
    n_layers, _, n_cols = ada_w.shape

    def body(c_ref, w_ref, b_ref, o_ref):
        o_ref[...] = _mm_f32(_silu(c_ref[...]), w_ref[...]) + b_ref[...]

    return pl.pallas_call(
        body, name="mod_parts", grid=(n_layers,),
        out_shape=jax.ShapeDtypeStruct((n_layers, 16, n_cols), F32),
        in_specs=[pl.BlockSpec((16, D_MODEL), lambda i: (0, 0)),
                  pl.BlockSpec((None, D_MODEL, n_cols), lambda i: (i, 0, 0)),
                  pl.BlockSpec((None, 1, n_cols), lambda i: (i, 0, 0))],
        out_specs=pl.BlockSpec((None, 16, n_cols), lambda i: (i, 0, 0)),
        compiler_params=_params(40),
    )(c_rows, ada_w, ada_b_cols.reshape(n_layers, 1, n_cols))


def _ln_mod_matmul(ctx_tile, xin, nw, shift, scale, w, name):
    n_mod = shift.shape[0]
    skip = n_mod - 1
    tm = ROW_TILE if skip else 2 * ROW_TILE
    rows = xin.shape[0] + skip * tm
    n_sec = w.shape[1] // D_MODEL

    def body(*refs):
        c_ref = refs[0] if skip else None
        x_ref, nw_ref, sh_ref, sc_ref, w_ref, g_ref, h_ref = refs[skip:]
        xv = jnp.where(pl.program_id(0) == 0, c_ref[...], x_ref[...]) if skip else x_ref[...]
        r = lax.rsqrt(jnp.mean(xv * xv, axis=-1, keepdims=True) + EPS)
        h_f32 = (xv * r * nw_ref[...]) * (1.0 + sc_ref[...]) + sh_ref[...]
        h = h_f32.astype(BF16)
        h_ref[...] = h
        for k in range(n_sec):
            g_ref[k] = jnp.dot(h, w_ref[:, k * D_MODEL:(k + 1) * D_MODEL], preferred_element_type=F32)

    mod_spec = pl.BlockSpec((None, 1, D_MODEL), lambda i: (jnp.minimum(i, n_mod - 1), 0, 0))
    return pl.pallas_call(
        body, name=name, grid=(rows // tm,),
        out_shape=[pltpu.HBM((n_sec, rows, D_MODEL), F32), pltpu.HBM((rows, D_MODEL), BF16)],
        in_specs=[pl.BlockSpec((tm, D_MODEL), lambda i: (0, 0))] * skip
        + [pl.BlockSpec((tm, D_MODEL), lambda i: (jnp.maximum(i - skip, 0), 0)),
           pl.BlockSpec((1, D_MODEL), lambda i: (0, 0)),
           mod_spec, mod_spec,
           pl.BlockSpec((D_MODEL, n_sec * D_MODEL), lambda i: (0, 0))],
        out_specs=[pl.BlockSpec((n_sec, tm, D_MODEL), lambda i: (0, i, 0)),
                   pl.BlockSpec((tm, D_MODEL), lambda i: (i, 0))],
        compiler_params=_params(48),
    )(*_pin(*([ctx_tile] * skip), xin, nw, shift.reshape(n_mod, 1, D_MODEL), scale.reshape(n_mod, 1, D_MODEL), w))


def _chunk_masks(rev):
    rid = lax.broadcasted_iota(jnp.int32, (CHUNK, CHUNK), 0)
    cid = lax.broadcasted_iota(jnp.int32, (CHUNK, CHUNK), 1)
    keep = (cid >= rid) if rev else (cid <= rid)
    keep_t = (cid <= rid) if rev else (cid >= rid)
    one, zero = jnp.ones((CHUNK, CHUNK), F32), jnp.zeros((CHUNK, CHUNK), F32)
    return keep, jnp.where(keep, one, zero).astype(BF16), jnp.where(keep_t, one, zero).astype(BF16)


def _chunk_rows(t, rev, latent):
    n = N_LAT_CHUNKS if latent else N_CTX_CHUNKS
    base = CTX_LEN if latent else 0
    idx = (n - 1 - t) if rev else t
    return pl.multiple_of(base + idx * CHUNK, CHUNK)


def _gates(fpre, lb):
    sg = _sig(fpre)
    f = lb + (1.0 - lb) * sg
    return sg, f, 1.0 - f, jnp.log(f)


G_SPEC = lambda sec: pl.BlockSpec((None, ROWS_ALL, HEAD_DIM), lambda h, sec=sec: (sec, 0, h))


def _gla_forward(g5, lb_logits):
    q_scale = HEAD_DIM ** -0.5

    def body(ff_ref, fb_ref, v_ref, q_ref, lg_ref, o_ref, st_ref, decay_ref, qt_ref):
        for rev in (False, True):
            f_ref = fb_ref if rev else ff_ref
            lb = _sig(lg_ref[1:2, :] if rev else lg_ref[0:1, :])
            keep, tri, _ = _chunk_masks(rev)
            last = 0 if rev else CHUNK - 1
            mid = CHUNK // 2 if rev else CHUNK // 2 - 1

            def local_step(t, carry, latent):
                r0 = _chunk_rows(t, rev, latent)
                rows = pl.ds(r0, CHUNK)
                step = t + (N_CTX_CHUNKS if latent else 0)
                _, _, k, lf = _gates(f_ref[rows, :], lb)
                v = v_ref[rows, :]
                b = _mm_exact_lhs(tri, lf)
                bl = b[last:last + 1, :]
                if latent:
                    q = _silu(q_ref[rows, :]) * q_scale
                    bm = b[mid:mid + 1, :]
                    a = _mm_nt(q * jnp.exp(b - bm), k * jnp.exp(bm - b))
                    o = _mm(jnp.where(keep, a, 0.0), v)
                    orow = pl.ds(pl.multiple_of(r0 - CTX_LEN, CHUNK), CHUNK)
                    qt_ref[orow, :] = (q * jnp.exp(b)).astype(BF16)
                    if rev:
                        o_ref[orow, :] += o
                    else:
                        o_ref[orow, :] = o
                decay_ref[step] = jnp.exp(bl)
                st_ref[step] = _mm_tn(v, k * jnp.exp(bl - b))
                return carry

            lax.fori_loop(0, N_CTX_CHUNKS, functools.partial(local_step, latent=False), 0, unroll=2)
            lax.fori_loop(0, N_LAT_CHUNKS, functools.partial(local_step, latent=True), 0, unroll=4)

            def scan_step(t, st):
                update = st_ref[t]
                st_ref[t] = st
                return st * decay_ref[t] + update

            lax.fori_loop(0, N_CHUNKS, scan_step, jnp.zeros((HEAD_DIM, HEAD_DIM), F32), unroll=2)

            def inter_step(t, carry):
                r0 = _chunk_rows(t, rev, True)
                orow = pl.ds(pl.multiple_of(r0 - CTX_LEN, CHUNK), CHUNK)
                o_ref[orow, :] += lax.dot_general(qt_ref[orow, :], st_ref[t + N_CTX_CHUNKS].astype(BF16),
                                                  (((1,), (1,)), ((), ())), preferred_element_type=F32)
                return carry

            lax.fori_loop(0, N_LAT_CHUNKS, inter_step, 0, unroll=4)

    return pl.pallas_call(
        body, name="gla_forward", grid=(HEADS,),
        out_shape=jax.ShapeDtypeStruct((SEQ, D_MODEL), F32),
        in_specs=[G_SPEC(0), G_SPEC(1), G_SPEC(2), G_SPEC(3), pl.BlockSpec((2, HEAD_DIM), lambda h: (0, h))],
        out_specs=pl.BlockSpec((SEQ, HEAD_DIM), lambda h: (0, h)),
        scratch_shapes=[pltpu.VMEM((N_CHUNKS, HEAD_DIM, HEAD_DIM), F32), pltpu.VMEM((N_CHUNKS, 1, HEAD_DIM), F32),
                        pltpu.VMEM((SEQ, HEAD_DIM), BF16)],
        compiler_params=_params(32),
    )(g5, g5, g5, g5, lb_logits)


def _gla_backward(g5, lb_logits, d_o, d_z):
    q_scale = HEAD_DIM ** -0.5

    def body(ff_ref, fb_ref, v_ref, q_ref, lg_ref, do_ref, dz_ref, dg_ref, dlg_ref, st_ref, dst_ref, decay_ref):
        dg_ref[3, 0:CTX_LEN, :] = jnp.zeros((CTX_LEN, HEAD_DIM), F32)
        dg_ref[4, 0:CTX_LEN, :] = jnp.zeros((CTX_LEN, HEAD_DIM), F32)
        dg_ref[4, CTX_LEN:ROWS_ALL, :] = dz_ref[...]
        is_row = lax.broadcasted_iota(jnp.int32, (CHUNK, HEAD_DIM), 0)
        for rev in (False, True):
            d = 1 if rev else 0
            f_ref = fb_ref if rev else ff_ref
            lb = _sig(lg_ref[d:d + 1, :])
            keep, tri, tri_t = _chunk_masks(rev)
            last = 0 if rev else CHUNK - 1
            mid = CHUNK // 2 if rev else CHUNK // 2 - 1

            def local_step(t, carry, latent):
                r0 = _chunk_rows(t, rev, latent)
                rows = pl.ds(r0, CHUNK)
                step = t + (N_CTX_CHUNKS if latent else 0)
                _, _, k, lf = _gates(f_ref[rows, :], lb)
                b = _mm_exact_lhs(tri, lf)
                bl = b[last:last + 1, :]
                decay_ref[step] = jnp.exp(bl)
                st_ref[step] = _mm_tn(v_ref[rows, :], k * jnp.exp(bl - b))
                if latent:
                    q_t = _silu(q_ref[rows, :]) * q_scale * jnp.exp(b)
                    dst_ref[step] = _mm_tn(do_ref[pl.ds(pl.multiple_of(r0 - CTX_LEN, CHUNK), CHUNK), :], q_t)
                else:
                    dst_ref[step] = jnp.zeros((HEAD_DIM, HEAD_DIM), F32)
                return carry

            lax.fori_loop(0, N_CTX_CHUNKS, functools.partial(local_step, latent=False), 0, unroll=2)
            lax.fori_loop(0, N_LAT_CHUNKS, functools.partial(local_step, latent=True), 0, unroll=4)

            def scan_step(i, carry):
                st, d_st = carry
                j = N_CHUNKS - 1 - i
                update, d_update = st_ref[i], dst_ref[j]
                st_ref[i] = st
                dst_ref[j] = d_st
                return st * decay_ref[i] + update, d_st * decay_ref[j] + d_update

            zero_state = jnp.zeros((HEAD_DIM, HEAD_DIM), F32)
            lax.fori_loop(0, N_CHUNKS, scan_step, (zero_state, zero_state))

            def grad_step(t, dlb, latent):
                r0 = _chunk_rows(t, rev, latent)
                rows = pl.ds(r0, CHUNK)
                step = t + (N_CTX_CHUNKS if latent else 0)
                sg, f, k, lf = _gates(f_ref[rows, :], lb)
                v = v_ref[rows, :]
                b = _mm_exact_lhs(tri, lf)
                bl = b[last:last + 1, :]
                e_end = jnp.exp(bl - b)
                k_end = k * e_end
                decay = jnp.exp(bl)
                d_st = dst_ref[step]
                st_prev = st_ref[step]
                d_kend = _mm(v, d_st)
                d_decay = jnp.sum(d_st * st_prev, axis=0, keepdims=True)
                t_kend = d_kend * k_end
                d_v = _mm_nt(k_end, d_st)
                d_k = d_kend * e_end
                d_b = -t_kend
                if latent:
                    qpre = q_ref[rows, :]
                    q = _silu(qpre) * q_scale
                    bm = b[mid:mid + 1, :]
                    e_b, e_qm, e_km = jnp.exp(b), jnp.exp(b - bm), jnp.exp(bm - b)
                    q_t, q_m, k_m = q * e_b, q * e_qm, k * e_km
                    a = jnp.where(keep, _mm_nt(q_m, k_m), 0.0)
                    d_out = do_ref[pl.ds(pl.multiple_of(r0 - CTX_LEN, CHUNK), CHUNK), :]
                    d_a = jnp.where(keep, _mm_nt(d_out, v), 0.0)
                    d_qm = _mm(d_a, k_m)
                    d_km = _mm_tn(d_a, q_m)
                    d_qt = _mm(d_out, st_prev)
                    d_v = d_v + _mm_tn(a, d_out)
                    d_k = d_k + d_km * e_km
                    d_b = d_b + d_qt * q_t + d_qm * q_m - d_km * k_m
                    d_q = d_qt * e_b + d_qm * e_qm
                at_last = jnp.sum(t_kend, axis=0, keepdims=True) + d_decay * decay
                d_b = d_b + jnp.where(is_row == last, at_last, 0.0)
                d_lf = _mm_exact_lhs(tri_t, d_b)
                d_f = d_lf / f - d_k
                dg_ref[d, rows, :] = d_f * (1.0 - lb) * sg * (1.0 - sg)
                if rev:
                    dg_ref[2, rows, :] += d_v
                else:
                    dg_ref[2, rows, :] = d_v
                if latent:
                    d_qpre = d_q * q_scale * _dsilu(qpre)
                    if rev:
                        dg_ref[3, rows, :] += d_qpre
                    else:
                        dg_ref[3, rows, :] = d_qpre
                return dlb + jnp.sum(d_f * (1.0 - sg), axis=0, keepdims=True)

            dlb = lax.fori_loop(0, N_LAT_CHUNKS, functools.partial(grad_step, latent=True), jnp.zeros((1, HEAD_DIM), F32),
                                unroll=2)
            dlb = lax.fori_loop(0, N_CTX_CHUNKS, functools.partial(grad_step, latent=False), dlb, unroll=2)
            dlg_ref[d:d + 1, :] = dlb * lb * (1.0 - lb)

    col = pl.BlockSpec((SEQ, HEAD_DIM), lambda h: (0, h))
    return pl.pallas_call(
        body, name="gla_backward", grid=(HEADS,),
        out_shape=[jax.ShapeDtypeStruct((HGRN_SECTIONS, ROWS_ALL, D_MODEL), F32), jax.ShapeDtypeStruct((2, D_MODEL), F32)],
        in_specs=[G_SPEC(0), G_SPEC(1), G_SPEC(2), G_SPEC(3), pl.BlockSpec((2, HEAD_DIM), lambda h: (0, h)), col, col],
        out_specs=[pl.BlockSpec((HGRN_SECTIONS, ROWS_ALL, HEAD_DIM), lambda h: (0, 0, h)),
                   pl.BlockSpec((2, HEAD_DIM), lambda h: (0, h))],
        scratch_shapes=[pltpu.VMEM((N_CHUNKS, HEAD_DIM, HEAD_DIM), F32), pltpu.VMEM((N_CHUNKS, HEAD_DIM, HEAD_DIM), F32),
                        pltpu.VMEM((N_CHUNKS, 1, HEAD_DIM), F32)],
        compiler_params=_params(48),
    )(g5, g5, g5, g5, lb_logits, d_o, d_z)


GROUP = 2 * CHUNK


def _group_masks():
    rid, cid = np.arange(GROUP)[:, None], np.arange(GROUP)[None, :]
    same = (rid >= CHUNK) == (cid >= CHUNK)
    sums, back = [], []
    for rev in (False, True):
        causal = (cid >= rid) if rev else (cid <= rid)
        anti = (cid <= rid) if rev else (cid >= rid)
        sums.append(same & causal)
        back.append(np.concatenate([same & anti, same & ~anti], axis=1))
    sums, back = np.stack(sums).astype(np.float32), np.stack(back).astype(np.float32)
    return jnp.asarray(sums, BF16), jnp.asarray(back, BF16), jnp.asarray(sums, F32)


MASK_SPECS = [pl.BlockSpec((2, GROUP, GROUP), lambda h: (0, 0, 0)), pl.BlockSpec((2, GROUP, 2 * GROUP), lambda h: (0, 0, 0)),
              pl.BlockSpec((2, GROUP, GROUP), lambda h: (0, 0, 0))]


def _group_sum(m_bf, a):
    hi, lo = _split2(a)
    r = jnp.dot(m_bf, jnp.concatenate([hi, lo], axis=1), preferred_element_type=F32)
    return r[:, :HEAD_DIM] + r[:, HEAD_DIM:]


def _chunk_row(a, pos):
    return jnp.concatenate([jnp.broadcast_to(a[c * CHUNK + pos:c * CHUNK + pos + 1, :], (CHUNK, HEAD_DIM)) for c in range(2)], axis=0)


def _by_chunk(a, second):
    return jnp.concatenate([jnp.where(second, 0.0, a), jnp.where(second, a, 0.0)], axis=1)


def _own_block(r):
    return jnp.concatenate([r[0:CHUNK, 0:HEAD_DIM], r[CHUNK:GROUP, HEAD_DIM:2 * HEAD_DIM]], axis=0)


def _scan_step_of(row_chunk, rev, latent):
    if not rev:
        return row_chunk
    return (N_CHUNKS + N_CTX_CHUNKS - 1 - row_chunk) if latent else (N_CTX_CHUNKS - 1 - row_chunk)


def _group_rows(i, j, per_step, latent):
    base = CTX_LEN if latent else 0
    return pl.multiple_of(base + (i * per_step + j) * GROUP, GROUP)


GROUPS_PER_STEP = 8
GROUPS_PER_GRAD_STEP = 4


def _gla_forward(g5, lb_logits, late_shards):
    q_scale = HEAD_DIM ** -0.5
    per_lat, per_ctx = GROUPS_PER_STEP, min(GROUPS_PER_STEP, CTX_LEN // GROUP)
    n_late = len(GATHER_LATE)

    def body(ff_ref, fb_ref, v_ref, q_ref, lg_ref, sums_ref, _, keep_ref, *rest):
        shard_refs, o_ref, full_refs = rest[:n_late], rest[n_late], rest[n_late + 1:2 * n_late + 1]
        st_ref, decay_ref, qt_ref = rest[2 * n_late + 1:2 * n_late + 4]
        land_refs = rest[2 * n_late + 4:3 * n_late + 4]
        start_gather, forward_gather, finish_gather = _gather_two_level(GATHER_LATE, shard_refs, land_refs, full_refs,
                                                                        *rest[3 * n_late + 4:])

        @pl.when(pl.program_id(0) == 0)
        def _():
            start_gather()

        @pl.when(pl.program_id(0) == HEADS - 1)
        def _():
            forward_gather()

        second = lax.broadcasted_iota(jnp.int32, (GROUP, HEAD_DIM), 0) >= CHUNK
        for rev in (False, True):
            f_ref = fb_ref if rev else ff_ref
            lb = _sig(lg_ref[1:2, :] if rev else lg_ref[0:1, :])
            d = 1 if rev else 0
            last = 0 if rev else CHUNK - 1
            mid = CHUNK // 2 if rev else CHUNK // 2 - 1

            def local_step(i, carry, latent, per):
                r0s = [_group_rows(i, j, per, latent) for j in range(per)]
                rows = [pl.ds(r0, GROUP) for r0 in r0s]
                gates = [_gates(f_ref[r, :], lb) for r in rows]
                vs = [v_ref[r, :] for r in rows]
                bs = [_group_sum(sums_ref[d], g[3]) for g in gates]
                bls = [_chunk_row(b, last) for b in bs]
                ups = [_mm_tn(v, _by_chunk(g[2] * jnp.exp(bl - b), second)) for v, g, b, bl in zip(vs, gates, bs, bls)]
                if latent:
                    qs = [_silu(q_ref[r, :]) * q_scale for r in rows]
                    bms = [_chunk_row(b, mid) for b in bs]
                    a_s = [_mm_nt(q * jnp.exp(b - bm), g[2] * jnp.exp(bm - b)) for q, g, b, bm in zip(qs, gates, bs, bms)]
                    outs = [_mm(a * keep_ref[d], v) for a, v in zip(a_s, vs)]
                for j in range(per):
                    for c in range(2):
                        step = _scan_step_of(r0s[j] // CHUNK + c, rev, latent)
                        decay_ref[step] = jnp.exp(bls[j][c * CHUNK:c * CHUNK + 1, :])
                        st_ref[step] = ups[j][:, c * HEAD_DIM:(c + 1) * HEAD_DIM]
                    if latent:
                        orow = pl.ds(pl.multiple_of(r0s[j] - CTX_LEN, GROUP), GROUP)
                        qt_ref[orow, :] = (qs[j] * jnp.exp(bs[j])).astype(BF16)
                        if rev:
                            o_ref[orow, :] += outs[j]
                        else:
                            o_ref[orow, :] = outs[j]
                return carry

            lax.fori_loop(0, CTX_LEN // (per_ctx * GROUP), functools.partial(local_step, latent=False, per=per_ctx), 0)
            lax.fori_loop(0, SEQ // (per_lat * GROUP), functools.partial(local_step, latent=True, per=per_lat), 0)

            def scan_step(t, st):
                update = st_ref[t]
                st_ref[t] = st
                return st * decay_ref[t] + update

            lax.fori_loop(0, N_CHUNKS, scan_step, jnp.zeros((HEAD_DIM, HEAD_DIM), F32), unroll=2)

            def inter_step(i, carry):
                r0s = [_group_rows(i, j, per_lat, True) for j in range(per_lat)]
                orows = [pl.ds(pl.multiple_of(r0 - CTX_LEN, GROUP), GROUP) for r0 in r0s]
                states = [jnp.concatenate([st_ref[_scan_step_of(r0 // CHUNK + c, rev, True)] for c in range(2)], axis=0)
                          for r0 in r0s]
                prods = [lax.dot_general(qt_ref[orow, :], s.astype(BF16), (((1,), (1,)), ((), ())), preferred_element_type=F32)
                         for orow, s in zip(orows, states)]
                for orow, r in zip(orows, prods):
                    o_ref[orow, :] += _own_block(r)
                return carry

            lax.fori_loop(0, SEQ // (per_lat * GROUP), inter_step, 0)

        @pl.when(pl.program_id(0) == HEADS - 1)
        def _():
            finish_gather()

    outs = pl.pallas_call(
        body, name="gla_forward", grid=(HEADS,),
        out_shape=[pltpu.HBM((SEQ, D_MODEL), F32)] + [jax.ShapeDtypeStruct(W_SPECS[k][0], BF16) for k in GATHER_LATE],
        in_specs=[G_SPEC(0), G_SPEC(1), G_SPEC(2), G_SPEC(3), pl.BlockSpec((2, HEAD_DIM), lambda h: (0, h))] + MASK_SPECS
        + [ANY] * n_late,
        out_specs=[pl.BlockSpec((SEQ, HEAD_DIM), lambda h: (0, h))] + [ANY] * n_late,
        scratch_shapes=[pltpu.VMEM((N_CHUNKS, HEAD_DIM, HEAD_DIM), F32), pltpu.VMEM((N_CHUNKS, 1, HEAD_DIM), F32),
                        pltpu.VMEM((SEQ, HEAD_DIM), BF16)] + [pltpu.VMEM(W_SPECS[k][0], BF16) for k in GATHER_LATE]
        + [pltpu.SemaphoreType.DMA((2 * n_late,)), pltpu.SemaphoreType.DMA((2 * n_late,)), pltpu.SemaphoreType.DMA((n_late,))],
        compiler_params=_params(48),
    )(*_pin(g5, g5, g5, g5, lb_logits), *_group_masks(), *late_shards)
    return outs[0], dict(zip(GATHER_LATE, outs[1:]))


def _gla_backward(g5, lb_logits, d_o, d_z, late_parts):
    q_scale = HEAD_DIM ** -0.5
    per_lat, per_ctx = GROUPS_PER_STEP, min(GROUPS_PER_STEP, CTX_LEN // GROUP)
    n_late = len(GATHER_LATE)

    def body(ff_ref, fb_ref, v_ref, q_ref, lg_ref, do_ref, dz_ref, sums_ref, back_ref, keep_ref, *rest):
        part_refs, (dg_ref, dlg_ref), slot_refs = rest[:n_late], rest[n_late:n_late + 2], rest[n_late + 2:2 * n_late + 2]
        st_ref, dst_ref, decay_ref, both_ref = rest[2 * n_late + 2:2 * n_late + 6]
        start_scatter, wait_scatter = _scatter_direct(GATHER_LATE, part_refs, slot_refs, *rest[2 * n_late + 6:])

        @pl.when(pl.program_id(0) == 0)
        def _():
            start_scatter()

        dg_ref[3, 0:CTX_LEN, :] = jnp.zeros((CTX_LEN, HEAD_DIM), BF16)
        dg_ref[4, 0:CTX_LEN, :] = jnp.zeros((CTX_LEN, HEAD_DIM), BF16)
        dg_ref[4, CTX_LEN:ROWS_ALL, :] = dz_ref[...]
        second = lax.broadcasted_iota(jnp.int32, (GROUP, HEAD_DIM), 0) >= CHUNK
        for rev in (False, True):
            d = 1 if rev else 0
            f_ref = fb_ref if rev else ff_ref
            lb = _sig(lg_ref[d:d + 1, :])
            last = 0 if rev else CHUNK - 1
            mid = CHUNK // 2 if rev else CHUNK // 2 - 1

            def local_step(i, carry, latent, per):
                r0s = [_group_rows(i, j, per, latent) for j in range(per)]
                rows = [pl.ds(r0, GROUP) for r0 in r0s]
                gates = [_gates(f_ref[r, :], lb) for r in rows]
                bs = [_group_sum(sums_ref[d], g[3]) for g in gates]
                bls = [_chunk_row(b, last) for b in bs]
                ups = [_mm_tn(v_ref[r, :], _by_chunk(g[2] * jnp.exp(bl - b), second)) for r, g, b, bl in zip(rows, gates, bs, bls)]
                if latent:
                    orows = [pl.ds(pl.multiple_of(r0 - CTX_LEN, GROUP), GROUP) for r0 in r0s]
                    d_ups = [_mm_tn(do_ref[orow, :], _by_chunk(_silu(q_ref[r, :]) * q_scale * jnp.exp(b), second))
                             for orow, r, b in zip(orows, rows, bs)]
                for j in range(per):
                    for c in range(2):
                        step = _scan_step_of(r0s[j] // CHUNK + c, rev, latent)
                        decay_ref[step] = jnp.exp(bls[j][c * CHUNK:c * CHUNK + 1, :])
                        st_ref[step] = ups[j][:, c * HEAD_DIM:(c + 1) * HEAD_DIM]
                        if latent:
                            dst_ref[step] = d_ups[j][:, c * HEAD_DIM:(c + 1) * HEAD_DIM]
                        else:
                            dst_ref[step] = jnp.zeros((HEAD_DIM, HEAD_DIM), F32)
                return carry

            lax.fori_loop(0, CTX_LEN // (per_ctx * GROUP), functools.partial(local_step, latent=False, per=per_ctx), 0)
            lax.fori_loop(0, SEQ // (per_lat * GROUP), functools.partial(local_step, latent=True, per=per_lat), 0)

            def scan_step(i, carry):
                st, d_st = carry
                j = N_CHUNKS - 1 - i
                update, d_update = st_ref[i], dst_ref[j]
                st_ref[i] = st
                dst_ref[j] = d_st
                return st * decay_ref[i] + update, d_st * decay_ref[j] + d_update

            zero_state = jnp.zeros((HEAD_DIM, HEAD_DIM), F32)
            lax.fori_loop(0, N_CHUNKS, scan_step, (zero_state, zero_state))

            def grad_step(i, dlb, latent, per):
                r0s = [_group_rows(i, j, per, latent) for j in range(per)]
                rows = [pl.ds(r0, GROUP) for r0 in r0s]
                gates = [_gates(f_ref[r, :], lb) for r in rows]
                vs = [v_ref[r, :] for r in rows]
                bs = [_group_sum(sums_ref[d], g[3]) for g in gates]
                bls = [_chunk_row(b, last) for b in bs]
                e_ends = [jnp.exp(bl - b) for b, bl in zip(bs, bls)]
                k_ends = [g[2] * e for g, e in zip(gates, e_ends)]
                sts = [[st_ref[_scan_step_of(r0 // CHUNK + c, rev, latent)] for c in range(2)] for r0 in r0s]
                d_sts = [[dst_ref[_scan_step_of(r0 // CHUNK + c, rev, latent)] for c in range(2)] for r0 in r0s]
                d_kends = [_own_block(_mm(v, jnp.concatenate(ds, axis=1))) for v, ds in zip(vs, d_sts)]
                d_vs = [_own_block(_mm_nt(ke, jnp.concatenate(ds, axis=0))) for ke, ds in zip(k_ends, d_sts)]
                at_last = [jnp.concatenate([jnp.broadcast_to(jnp.sum(ds[c] * s[c], axis=0, keepdims=True), (CHUNK, HEAD_DIM))
                                            for c in range(2)], axis=0) * jnp.exp(bl) for ds, s, bl in zip(d_sts, sts, bls)]
                t_kends = [dk * ke for dk, ke in zip(d_kends, k_ends)]
                d_ks = [dk * e for dk, e in zip(d_kends, e_ends)]
                if latent:
                    orows = [pl.ds(pl.multiple_of(r0 - CTX_LEN, GROUP), GROUP) for r0 in r0s]
                    qpres = [q_ref[r, :] for r in rows]
                    q_sigs = [_sig(qp) for qp in qpres]
                    qs = [qp * sg * q_scale for qp, sg in zip(qpres, q_sigs)]
                    bms = [_chunk_row(b, mid) for b in bs]
                    e_bs = [jnp.exp(b) for b in bs]
                    e_qms = [jnp.exp(b - bm) for b, bm in zip(bs, bms)]
                    e_kms = [jnp.exp(bm - b) for b, bm in zip(bs, bms)]
                    q_ts = [q * e for q, e in zip(qs, e_bs)]
                    q_ms = [q * e for q, e in zip(qs, e_qms)]
                    k_ms = [g[2] * e for g, e in zip(gates, e_kms)]
                    d_outs = [do_ref[orow, :] for orow in orows]
                    a_s = [_mm_nt(qm, km) * keep_ref[d] for qm, km in zip(q_ms, k_ms)]
                    d_as = [_mm_nt(do, v) * keep_ref[d] for do, v in zip(d_outs, vs)]
                    d_qts = [_own_block(_mm(do, jnp.concatenate(s, axis=1))) for do, s in zip(d_outs, sts)]
                    d_qms = [_mm(da, km) for da, km in zip(d_as, k_ms)]
                    d_kms = [_mm_tn(da, qm) for da, qm in zip(d_as, q_ms)]
                    d_vs = [dv + _mm_tn(a, do) for dv, a, do in zip(d_vs, a_s, d_outs)]
                    d_ks = [dk + dkm * e for dk, dkm, e in zip(d_ks, d_kms, e_kms)]
                    d_bs = [jnp.concatenate([dqt * qt + dqm * qm - dkm * km, t], axis=0)
                            for dqt, qt, dqm, qm, dkm, km, t in zip(d_qts, q_ts, d_qms, q_ms, d_kms, k_ms, t_kends)]
                    d_qs = [dqt * eb + dqm * eq for dqt, eb, dqm, eq in zip(d_qts, e_bs, d_qms, e_qms)]
                    back = back_ref[d]
                else:
                    d_bs, back = t_kends, back_ref[d, :, GROUP:2 * GROUP]
                d_lfs = [_group_sum(back, db) + al for db, al in zip(d_bs, at_last)]
                for j in range(per):
                    sg, f = gates[j][0], gates[j][1]
                    d_f = d_lfs[j] / f - d_ks[j]
                    dg_ref[d, rows[j], :] = (d_f * (1.0 - lb) * sg * (1.0 - sg)).astype(BF16)
                    dlb = dlb + jnp.sum(d_f * (1.0 - sg), axis=0, keepdims=True)
                    if rev:
                        dg_ref[2, rows[j], :] = (both_ref[0, rows[j], :] + d_vs[j]).astype(BF16)
                    else:
                        both_ref[0, rows[j], :] = d_vs[j]
                    if latent:
                        d_qpre = d_qs[j] * q_scale * (q_sigs[j] * (1.0 + qpres[j] * (1.0 - q_sigs[j])))
                        if rev:
                            dg_ref[3, rows[j], :] = (both_ref[1, rows[j], :] + d_qpre).astype(BF16)
                        else:
                            both_ref[1, rows[j], :] = d_qpre
                return dlb

            dlb = lax.fori_loop(0, SEQ // (GROUPS_PER_GRAD_STEP * GROUP),
                                functools.partial(grad_step, latent=True, per=GROUPS_PER_GRAD_STEP), jnp.zeros((1, HEAD_DIM), F32))
            dlb = lax.fori_loop(0, CTX_LEN // (per_ctx * GROUP), functools.partial(grad_step, latent=False, per=per_ctx), dlb)
            dlg_ref[d:d + 1, :] = dlb * lb * (1.0 - lb)

        @pl.when(pl.program_id(0) == HEADS - 1)
        def _():
            wait_scatter()

    col = pl.BlockSpec((SEQ, HEAD_DIM), lambda h: (0, h))
    outs = pl.pallas_call(
        body, name="gla_backward", grid=(HEADS,),
        out_shape=[pltpu.HBM((HGRN_SECTIONS, ROWS_ALL, D_MODEL), BF16), jax.ShapeDtypeStruct((2, D_MODEL), F32)]
        + _slot_shapes(GATHER_LATE),
        in_specs=[G_SPEC(0), G_SPEC(1), G_SPEC(2), G_SPEC(3), pl.BlockSpec((2, HEAD_DIM), lambda h: (0, h)), col, col]
        + MASK_SPECS + [ANY] * n_late,
        out_specs=[pl.BlockSpec((HGRN_SECTIONS, ROWS_ALL, HEAD_DIM), lambda h: (0, 0, h)),
                   pl.BlockSpec((2, HEAD_DIM), lambda h: (0, h))] + [ANY] * n_late,
        scratch_shapes=[pltpu.VMEM((N_CHUNKS, HEAD_DIM, HEAD_DIM), F32), pltpu.VMEM((N_CHUNKS, HEAD_DIM, HEAD_DIM), F32),
                        pltpu.VMEM((N_CHUNKS, 1, HEAD_DIM), F32), pltpu.VMEM((2, ROWS_ALL, HEAD_DIM), F32)]
        + _comm_sems(n_late),
        compiler_params=_params(48),
    )(*_pin(g5, g5, g5, g5, lb_logits, d_o, d_z), *_group_masks(), *late_parts)
    return outs[0], outs[1], dict(zip(GATHER_LATE, outs[2:]))


def _head_norm(o, gw, scr):
    rs = []
    for h in range(HEADS):
        cols = slice(h * HEAD_DIM, (h + 1) * HEAD_DIM)
        oh = o[:, cols]
        r = lax.rsqrt(jnp.mean(oh * oh, axis=-1, keepdims=True) + EPS)
        scr[:, cols] = oh * r
        rs.append(r)
    return rs


def _hgrn_out_forward(o_raw, g5, xin, gnorm_w, gate, w_out):
    tm = ROW_TILE

    def body(o_ref, z_ref, x_ref, gw_ref, gate_ref, w_ref, x1_ref, res_ref, scr):
        _head_norm(o_ref[...], None, scr)
        a = scr[...] * gw_ref[...] * _silu(z_ref[...])
        res = _mm(a, w_ref[...])
        res_ref[...] = res
        x1_ref[...] = x_ref[...] + gate_ref[...] * res

    tile = pl.BlockSpec((tm, D_MODEL), lambda i: (i, 0))
    vec = pl.BlockSpec((1, D_MODEL), lambda i: (0, 0))
    return pl.pallas_call(
        body, name="hgrn_out_forward", grid=(SEQ // tm,),
        out_shape=[pltpu.HBM((SEQ, D_MODEL), F32)] * 2,
        in_specs=[tile, pl.BlockSpec((None, tm, D_MODEL), lambda i: (4, i + CTX_LEN // tm, 0)), tile, vec, vec,
                  pl.BlockSpec((D_MODEL, D_MODEL), lambda i: (0, 0))],
        out_specs=[tile, tile],
        scratch_shapes=[pltpu.VMEM((tm, D_MODEL), F32)],
        compiler_params=_params(32),
    )(*_pin(o_raw, g5, xin, gnorm_w, gate, w_out))


def _hgrn_out_backward(d_x1, o_raw, g5, res, gnorm_w, gate, w_out):
    tm = ROW_TILE

    def body(dx_ref, o_ref, z_ref, res_ref, gw_ref, gate_ref, w_ref, do_ref, dz_ref, dw_out, dgate_ref, dgw_ref, scr, scr2,
             dw_ref):
        @pl.when(pl.program_id(0) == 0)
        def _():
            dw_ref[...] = jnp.zeros_like(dw_ref)
            dgate_ref[...] = jnp.zeros_like(dgate_ref)
            dgw_ref[...] = jnp.zeros_like(dgw_ref)

        dx = dx_ref[...]
        dgate_ref[...] += jnp.sum(dx * res_ref[...], axis=0, keepdims=True)
        d_res = (dx * gate_ref[...]).astype(BF16)
        d_a = _mm_nt(d_res, w_ref[...])
        rs = _head_norm(o_ref[...], None, scr)
        z = z_ref[...]
        sz = _silu(z)
        o_hat = scr[...]
        o_n = o_hat * gw_ref[...]
        dw_ref[...] += _mm_tn(o_n * sz, d_res)
        d_on = d_a * sz
        dz_ref[...] = (d_a * o_n * _dsilu(z)).astype(BF16)
        dgw_ref[...] += jnp.sum(d_on * o_hat, axis=0, keepdims=True)
        scr2[...] = d_on * gw_ref[...]
        for h in range(HEADS):
            cols = slice(h * HEAD_DIM, (h + 1) * HEAD_DIM)
            dh, oh = scr2[:, cols], scr[:, cols]
            do_ref[:, cols] = (rs[h] * (dh - oh * jnp.mean(dh * oh, axis=-1, keepdims=True))).astype(BF16)

        @pl.when(pl.program_id(0) == SEQ // tm - 1)
        def _():
            dw_out[...] = dw_ref[...].astype(BF16)

    tile = pl.BlockSpec((tm, D_MODEL), lambda i: (i, 0))
    vec = pl.BlockSpec((1, D_MODEL), lambda i: (0, 0))
    mat = pl.BlockSpec((D_MODEL, D_MODEL), lambda i: (0, 0))
    return pl.pallas_call(
        body, name="hgrn_out_backward", grid=(SEQ // tm,),
        out_shape=[pltpu.HBM((SEQ, D_MODEL), BF16)] * 2 + [pltpu.HBM((D_MODEL, D_MODEL), BF16)]
        + [jax.ShapeDtypeStruct((1, D_MODEL), F32)] * 2,
        in_specs=[tile, tile, pl.BlockSpec((None, tm, D_MODEL), lambda i: (4, i + CTX_LEN // tm, 0)), tile, vec, vec, mat],
        out_specs=[tile, tile, mat, vec, vec],
        scratch_shapes=[pltpu.VMEM((tm, D_MODEL), F32)] * 2 + [pltpu.VMEM((D_MODEL, D_MODEL), F32)],
        compiler_params=_params(40),
    )(*_pin(d_x1, o_raw, g5, res, gnorm_w, gate, w_out))


def _pool_constants():
    win = np.zeros((POOL_GROUPS, ROW_TILE, ROW_TILE), np.float32)
    inv = np.zeros((POOL_GROUPS, ROW_TILE, 1), np.float32)
    for g, w in enumerate(POOL_WINDOWS):
        for t in range(ROW_TILE):
            base, p = (t // GRID_W) * GRID_W, t % GRID_W
            lo = min(max(p - w // 2, 0), GRID_W)
            hi = min(max(p - w // 2 + w, 0), GRID_W)
            win[g, t, base + lo:base + hi] = 1.0
            inv[g, t, 0] = 1.0 / np.float32(hi - lo)
    return jnp.asarray(win, BF16), jnp.asarray(win.transpose(0, 2, 1), BF16), jnp.asarray(inv, F32)


def _pool_mix(u_ref, wg_ref, ps_ref, win_ref, inv_ref, pooled_scr, yg_scr):
    for g in range(POOL_GROUPS):
        cols = slice(g * POOL_GROUP_DIM, (g + 1) * POOL_GROUP_DIM)
        ug = u_ref[:, cols]
        pooled = _mm_exact_lhs(win_ref[g], ug) * inv_ref[g] - ug
        if pooled_scr is not None:
            pooled_scr[:, cols] = pooled
        yg_scr[:, cols] = _mm(pooled, wg_ref[g])


def _pool_forward_loss(uz, x1, target, gate, w_grp, pool_scale, w_out, final_w):
    tm = ROW_TILE
    win, _, inv = _pool_constants()

    def body(u_ref, z_ref, x_ref, t_ref, gate_ref, wg_ref, ps_ref, w_ref, fw_ref, win_ref, inv_ref,
             dx_ref, loss_ref, dfw_ref, dgate_ref, yg_scr):
        @pl.when(pl.program_id(0) == 0)
        def _():
            loss_ref[...] = jnp.zeros_like(loss_ref)
            dfw_ref[...] = jnp.zeros_like(dfw_ref)
            dgate_ref[...] = jnp.zeros_like(dgate_ref)

        _pool_mix(u_ref, wg_ref, ps_ref, win_ref, inv_ref, None, yg_scr)
        a = yg_scr[...] * ps_ref[...] * _silu(z_ref[...])
        res = _mm(a, w_ref[...])
        x2 = x_ref[...] + gate_ref[...] * res
        r = lax.rsqrt(jnp.mean(x2 * x2, axis=-1, keepdims=True) + EPS)
        xh = x2 * r
        fw = fw_ref[...]
        err = xh * fw - t_ref[...]
        loss_ref[...] += 0.5 * jnp.sum(jnp.mean(err * err, axis=-1, keepdims=True))
        d_y = err * (1.0 / D_MODEL)
        dfw_ref[...] += jnp.sum(d_y * xh, axis=0, keepdims=True)
        d_xh = d_y * fw
        d_x2 = r * (d_xh - xh * jnp.mean(d_xh * xh, axis=-1, keepdims=True))
        dx_ref[...] = d_x2
        dgate_ref[...] += jnp.sum(d_x2 * res, axis=0, keepdims=True)

    tile = pl.BlockSpec((tm, D_MODEL), lambda i: (i, 0))
    vec = pl.BlockSpec((1, D_MODEL), lambda i: (0, 0))
    grp = pl.BlockSpec((POOL_GROUPS, POOL_GROUP_DIM, POOL_GROUP_DIM), lambda i: (0, 0, 0))
    return pl.pallas_call(
        body, name="pool_forward_loss", grid=(SEQ // tm,),
        out_shape=[pltpu.HBM((SEQ, D_MODEL), F32), jax.ShapeDtypeStruct((8, 128), F32),
                   jax.ShapeDtypeStruct((1, D_MODEL), F32), jax.ShapeDtypeStruct((1, D_MODEL), F32)],
        in_specs=[pl.BlockSpec((None, tm, D_MODEL), lambda i: (0, i, 0)), pl.BlockSpec((None, tm, D_MODEL), lambda i: (1, i, 0)),
                  tile, tile, vec, grp, vec, pl.BlockSpec((D_MODEL, D_MODEL), lambda i: (0, 0)), vec, grp,
                  pl.BlockSpec((POOL_GROUPS, ROW_TILE, 1), lambda i: (0, 0, 0))],
        out_specs=[tile, pl.BlockSpec((8, 128), lambda i: (0, 0)), vec, vec],
        scratch_shapes=[pltpu.VMEM((tm, D_MODEL), F32)],
        compiler_params=_params(32),
    )(*_pin(uz, uz, x1, target, gate, w_grp, pool_scale, w_out, final_w, win, inv))


def _pool_backward(d_x2, uz, gate, w_grp, pool_scale, w_out):
    tm = ROW_TILE
    win, win_t, inv = _pool_constants()

    def body(dx_ref, u_ref, z_ref, gate_ref, wg_ref, ps_ref, w_ref, win_ref, wint_ref, inv_ref,
             duz_ref, dw_out, dwg_out, dps_ref, pooled_scr, yg_scr, dyg_scr, dw_ref, dwg_ref):
        @pl.when(pl.program_id(0) == 0)
        def _():
            dw_ref[...] = jnp.zeros_like(dw_ref)
            dwg_ref[...] = jnp.zeros_like(dwg_ref)
            dps_ref[...] = jnp.zeros_like(dps_ref)

        _pool_mix(u_ref, wg_ref, ps_ref, win_ref, inv_ref, pooled_scr, yg_scr)
        z = z_ref[...]
        sz = _silu(z)
        yg = yg_scr[...]
        y = yg * ps_ref[...]
        d_res = (dx_ref[...] * gate_ref[...]).astype(BF16)
        d_a = _mm_nt(d_res, w_ref[...])
        dw_ref[...] += _mm_tn(y * sz, d_res)
        d_y = d_a * sz
        duz_ref[1] = (d_a * y * _dsilu(z)).astype(BF16)
        dps_ref[...] += jnp.sum(d_y * yg, axis=0, keepdims=True)
        dyg_scr[...] = d_y * ps_ref[...]
        for g in range(POOL_GROUPS):
            cols = slice(g * POOL_GROUP_DIM, (g + 1) * POOL_GROUP_DIM)
            d_yg = dyg_scr[:, cols].astype(BF16)
            d_pool = _mm_nt(d_yg, wg_ref[g])
            dwg_ref[g] += _mm_tn(pooled_scr[:, cols], d_yg)
            duz_ref[0, :, cols] = (_mm_exact_lhs(wint_ref[g], d_pool * inv_ref[g]) - d_pool).astype(BF16)

        @pl.when(pl.program_id(0) == SEQ // tm - 1)
        def _():
            dw_out[...] = dw_ref[...].astype(BF16)
            dwg_out[...] = dwg_ref[...].astype(BF16)

    tile = pl.BlockSpec((tm, D_MODEL), lambda i: (i, 0))
    vec = pl.BlockSpec((1, D_MODEL), lambda i: (0, 0))
    mat = pl.BlockSpec((D_MODEL, D_MODEL), lambda i: (0, 0))
    grp = pl.BlockSpec((POOL_GROUPS, POOL_GROUP_DIM, POOL_GROUP_DIM), lambda i: (0, 0, 0))
    return pl.pallas_call(
        body, name="pool_backward", grid=(SEQ // tm,),
        out_shape=[pltpu.HBM((POOL_SECTIONS, SEQ, D_MODEL), BF16), pltpu.HBM((D_MODEL, D_MODEL), BF16),
                   pltpu.HBM((POOL_GROUPS, POOL_GROUP_DIM, POOL_GROUP_DIM), BF16), jax.ShapeDtypeStruct((1, D_MODEL), F32)],
        in_specs=[tile, pl.BlockSpec((None, tm, D_MODEL), lambda i: (0, i, 0)), pl.BlockSpec((None, tm, D_MODEL), lambda i: (1, i, 0)),
                  vec, grp, vec, mat, grp, grp, pl.BlockSpec((POOL_GROUPS, ROW_TILE, 1), lambda i: (0, 0, 0))],
        out_specs=[pl.BlockSpec((POOL_SECTIONS, tm, D_MODEL), lambda i: (0, i, 0)), mat, grp, vec],
        scratch_shapes=[pltpu.VMEM((tm, D_MODEL), F32)] * 3 + [pltpu.VMEM((D_MODEL, D_MODEL), F32),
                                                               pltpu.VMEM((POOL_GROUPS, POOL_GROUP_DIM, POOL_GROUP_DIM), F32)],
        compiler_params=_params(40),
    )(*_pin(d_x2, uz, uz, gate, w_grp, pool_scale, w_out, win, win_t, inv))


def _ln_mod_backward(d_g, w, ctx_tile, xin, nw, scale, d_up, name):
    n_sec, rows, _ = d_g.shape
    n_mod = scale.shape[0]
    skip = n_mod - 1
    tm = ROW_TILE if skip else 2 * ROW_TILE
    n_tiles = rows // tm

    def body(dg_ref, w_ref, *refs):
        c_ref = refs[0] if skip else None
        x_ref, nw_ref, sc_ref, up_ref, dx_ref, dnw_ref, dmod_ref = refs[skip:]
        i = pl.program_id(0)

        @pl.when(i == 0)
        def _():
            dnw_ref[...] = jnp.zeros_like(dnw_ref)

        @pl.when((i == 0) | (i == skip))
        def _():
            dmod_ref[...] = jnp.zeros_like(dmod_ref)

        d_h = _mm_nt(dg_ref[0], w_ref[:, 0:D_MODEL])
        for k in range(1, n_sec):
            d_h = d_h + _mm_nt(dg_ref[k], w_ref[:, k * D_MODEL:(k + 1) * D_MODEL])
        xv = jnp.where(i == 0, c_ref[...], x_ref[...]) if skip else x_ref[...]
        r = lax.rsqrt(jnp.mean(xv * xv, axis=-1, keepdims=True) + EPS)
        xh = xv * r
        nw_row = nw_ref[...]
        dmod_ref[0:1, :] += jnp.sum(d_h, axis=0, keepdims=True)
        dmod_ref[1:2, :] += jnp.sum(d_h * (xh * nw_row), axis=0, keepdims=True)
        d_xn = d_h * (1.0 + sc_ref[...])
        dnw_ref[...] += jnp.sum(d_xn * xh, axis=0, keepdims=True)
        d_xh = d_xn * nw_row

        @pl.when(i >= skip)
        def _():
            dx_ref[...] = up_ref[...] + r * (d_xh - xh * jnp.mean(d_xh * xh, axis=-1, keepdims=True))

    lat = lambda i: (jnp.maximum(i - skip, 0), 0)
    mod_idx = lambda i: (jnp.minimum(i, n_mod - 1), 0, 0)
    return pl.pallas_call(
        body, name=name, grid=(n_tiles,),
        out_shape=[pltpu.HBM((rows - skip * tm, D_MODEL), F32), jax.ShapeDtypeStruct((1, D_MODEL), F32),
                   jax.ShapeDtypeStruct((n_mod, 8, D_MODEL), F32)],
        in_specs=[pl.BlockSpec((n_sec, tm, D_MODEL), lambda i: (0, i, 0)),
                  pl.BlockSpec((D_MODEL, n_sec * D_MODEL), lambda i: (0, 0))]
        + [pl.BlockSpec((tm, D_MODEL), lambda i: (0, 0))] * skip
        + [pl.BlockSpec((tm, D_MODEL), lat),
           pl.BlockSpec((1, D_MODEL), lambda i: (0, 0)),
           pl.BlockSpec((None, 1, D_MODEL), mod_idx),
           pl.BlockSpec((tm, D_MODEL), lat)],
        out_specs=[pl.BlockSpec((tm, D_MODEL), lat), pl.BlockSpec((1, D_MODEL), lambda i: (0, 0)),
                   pl.BlockSpec((None, 8, D_MODEL), mod_idx)],
        compiler_params=_params(48),
    )(*_pin(d_g, w, *([ctx_tile] * skip), xin, nw, scale.reshape(n_mod, 1, D_MODEL), d_up))


def _weight_grad(h, d_g, name):
    n_sec, rows, _ = d_g.shape
    tm = 768 if rows % 768 == 0 else 512
    n_tiles = rows // tm

    def body(h_ref, dg_ref, dw_ref, acc):
        i = pl.program_id(1)
        prod = _mm_tn(h_ref[...], dg_ref[...])

        @pl.when(i == 0)
        def _():
            acc[...] = prod

        @pl.when((i > 0) & (i < n_tiles - 1))
        def _():
            acc[...] += prod

        @pl.when(i == n_tiles - 1)
        def _():
            dw_ref[...] = (acc[...] + prod).astype(BF16)

    return pl.pallas_call(
        body, name=name, grid=(n_sec, n_tiles),
        out_shape=pltpu.HBM((D_MODEL, n_sec * D_MODEL), BF16),
        in_specs=[pl.BlockSpec((tm, D_MODEL), lambda j, i: (i, 0)), pl.BlockSpec((None, tm, D_MODEL), lambda j, i: (j, i, 0))],
        out_specs=pl.BlockSpec((D_MODEL, D_MODEL), lambda j, i: (0, j)),
        scratch_shapes=[pltpu.VMEM((D_MODEL, D_MODEL), F32)],
        compiler_params=_params(32),
    )(*_pin(h, d_g))


def _weight_grad_paired(h, d_g, name):
    n_sec, rows, _ = d_g.shape
    tm = 768
    n_tiles = rows // tm
    half = D_MODEL // 2

    def body(h_ref, dg_ref, q_ref, acc, keep, send, land, send_sems, recv_sems):
        j, i = pl.program_id(0), pl.program_id(1)
        x, y, c = _my_place()
        prod = _mm_tn(h_ref[...], dg_ref[...])

        def to_sibling(k):
            return pltpu.make_async_remote_copy(src_ref=send.at[k], dst_ref=land.at[k], send_sem=send_sems.at[k],
                                                recv_sem=recv_sems.at[k], device_id=(x, y, 1 - c), device_id_type=MESH)

        @pl.when(i == 0)
        def _():
            acc[...] = prod

        @pl.when((i > 0) & (i < n_tiles - 1))
        def _():
            acc[...] += prod

        @pl.when(i == n_tiles - 1)
        def _():
            acc[...] += prod
            keep[j] = acc[pl.ds(_al(c * half, half), half), :]
            send[j] = acc[pl.ds(_al((1 - c) * half, half), half), :].astype(BF16)
            to_sibling(j).start()

        @pl.when((j == n_sec - 1) & (i == n_tiles - 1))
        def _():
            for k in range(n_sec):
                to_sibling(k).wait()
                q_ref[:, k * D_MODEL:(k + 1) * D_MODEL] = (keep[k] + land[k].astype(F32)).astype(BF16)

    return pl.pallas_call(
        body, name=name, grid=(n_sec, n_tiles),
        out_shape=jax.ShapeDtypeStruct((half, n_sec * D_MODEL), BF16),
        in_specs=[pl.BlockSpec((tm, D_MODEL), lambda j, i: (i, 0)), pl.BlockSpec((None, tm, D_MODEL), lambda j, i: (j, i, 0))],
        out_specs=pl.BlockSpec((half, n_sec * D_MODEL), lambda j, i: (0, 0)),
        scratch_shapes=[pltpu.VMEM((D_MODEL, D_MODEL), F32), pltpu.VMEM((n_sec, half, D_MODEL), F32),
                        pltpu.VMEM((n_sec, half, D_MODEL), BF16), pltpu.VMEM((n_sec, half, D_MODEL), BF16),
                        pltpu.SemaphoreType.DMA((n_sec,)), pltpu.SemaphoreType.DMA((n_sec,))],
        compiler_params=_params(56),
    )(*_pin(h, d_g))


def _sum_slots(slots, name):
    n_slots, rows, cols = slots.shape
    tr = 128

    def body(s_ref, o_ref):
        acc = s_ref[0].astype(F32)
        for d in range(1, n_slots):
            acc = acc + s_ref[d].astype(F32)
        o_ref[...] = acc

    return pl.pallas_call(
        body, name=name, grid=(rows // tr,),
        out_shape=jax.ShapeDtypeStruct((rows, cols), F32),
        in_specs=[pl.BlockSpec((n_slots, tr, cols), lambda i: (0, i, 0))],
        out_specs=pl.BlockSpec((tr, cols), lambda i: (i, 0)),
    )(*_pin(slots))


def _adamw_math(w, g, m, v):
    m = ADAM_B1 * m + (1.0 - ADAM_B1) * g
    v = ADAM_B2 * v + (1.0 - ADAM_B2) * (g * g)
    m_hat = m / (1.0 - ADAM_B1 ** ADAM_STEP)
    v_hat = v / (1.0 - ADAM_B2 ** ADAM_STEP)
    return -ADAM_LR * (m_hat / (jnp.sqrt(v_hat) + ADAM_EPS) + ADAM_WD * w), m, v


def _adamw(w, g, m, v, name):
    rows, cols = w.shape
    tr = rows if rows <= 128 else 128

    def body(w_ref, g_ref, m_ref, v_ref, d_ref, mo_ref, vo_ref):
        d_ref[...], mo_ref[...], vo_ref[...] = _adamw_math(w_ref[...], g_ref[...], m_ref[...], v_ref[...])

    tile = pl.BlockSpec((tr, cols), lambda i: (i, 0))
    return pl.pallas_call(
        body, name=name, grid=(rows // tr,),
        out_shape=[pltpu.HBM((rows, cols), F32)] * 3,
        in_specs=[tile] * 4, out_specs=[tile] * 3,
    )(*_pin(w, g, m, v))


def _sum_devices(gathered):
    def body(p_ref, o_ref):
        acc = p_ref[0:SMALL_ROWS, :]
        for d in range(1, N_DEV):
            acc = acc + p_ref[d * SMALL_ROWS:(d + 1) * SMALL_ROWS, :]
        o_ref[...] = acc

    return pl.pallas_call(body, name="sum_devices", out_shape=jax.ShapeDtypeStruct((SMALL_ROWS, D_MODEL), F32),
                          in_specs=[VMEM], out_specs=VMEM)(gathered)


def _small_sums_and_cond_ctx(small, ada_w0, c_ctx, m, v):
    n_cols = ada_w0.shape[1]

    def body(small_ref, w_ref, c_ref, m_ref, v_ref, all_ref, sums_ref, g_ref, d_ref, mo_ref, vo_ref, part_ref, parts_ref,
             s1, r1, l1, s2, r2, l2):
        start_slabs, finish_slabs = _small_gather(small_ref, all_ref, s1, r1, l1)
        start_parts, finish_parts = _small_gather(part_ref, parts_ref, s2, r2, l2)
        start_slabs()
        finish_slabs()
        acc = all_ref[0:SMALL_ROWS, :]
        for dev in range(1, N_DEV):
            acc = acc + all_ref[dev * SMALL_ROWS:(dev + 1) * SMALL_ROWS, :]
        sums_ref[...] = acc
        x, y, _ = _my_place()
        d_modc = jnp.concatenate([acc[6:7, :], acc[7:8, :], acc[8:9, :]], axis=1)
        mine = jnp.zeros((1, n_cols), F32)
        for s in range(N_CHIPS):
            mine = mine + jnp.where(2 * x + y == s, d_modc[:, s * n_cols:(s + 1) * n_cols], 0.0)
        ah, al = _split2(jnp.broadcast_to(mine, (8, n_cols)))
        wh, wl = _split2(w_ref[...])
        nt = lambda a, b: lax.dot_general(a, b, (((1,), (1,)), ((), ())), preferred_element_type=F32)
        part_ref[...] = nt(ah, wh) + nt(al, wh) + nt(ah, wl)
        start_parts()
        finish_parts()
        total = parts_ref[0:1, :]
        for s in range(1, N_CHIPS):
            total = total + parts_ref[16 * s:16 * s + 1, :]
        w = c_ref[...]
        g = total * _dsilu(w)
        g_ref[...] = g
        d_ref[...], mo_ref[...], vo_ref[...] = _adamw_math(w, g, m_ref[...], v_ref[...])

    seven = [pltpu.SemaphoreType.DMA((7,)), pltpu.SemaphoreType.DMA((7,)), pltpu.SemaphoreType.DMA]
    vec = jax.ShapeDtypeStruct((1, D_MODEL), F32)
    return pl.pallas_call(
        body, name="small_sums_and_cond_ctx",
        out_shape=[jax.ShapeDtypeStruct((N_DEV * SMALL_ROWS, D_MODEL), F32), jax.ShapeDtypeStruct((SMALL_ROWS, D_MODEL), F32),
                   vec, vec, vec, vec],
        in_specs=[VMEM] * 5, out_specs=[VMEM] * 6,
        scratch_shapes=[pltpu.VMEM((8, D_MODEL), F32), pltpu.VMEM((N_DEV * 8, D_MODEL), F32)] + seven + seven,
    )(small, ada_w0, c_ctx, m, v)


def _ada_update(cond_t, d_mod, ada_w, m, v):
    n_layers, _, n_cols = ada_w.shape
    tr = ROW_TILE

    def body(c_ref, dm_ref, w_ref, m_ref, v_ref, g_ref, d_ref, mo_ref, vo_ref):
        g = _mm_f32(_silu(c_ref[...]), dm_ref[...])
        g_ref[...] = g
        d_ref[...], mo_ref[...], vo_ref[...] = _adamw_math(w_ref[...], g, m_ref[...], v_ref[...])

    tile = pl.BlockSpec((None, tr, n_cols), lambda l, i: (l, i, 0))
    return pl.pallas_call(
        body, name="ada_update", grid=(n_layers, D_MODEL // tr),
        out_shape=[pltpu.HBM(ada_w.shape, F32)] * 4,
        in_specs=[pl.BlockSpec((tr, 16), lambda l, i: (i, 0)), pl.BlockSpec((None, 16, n_cols), lambda l, i: (l, 0, 0)),
                  tile, tile, tile],
        out_specs=[tile] * 4,
    )(*_pin(cond_t, d_mod, ada_w, m, v))


def _cond_ctx_partial(d_modc, ada_w0):
    n_cols = ada_w0.shape[1]
    tr = ROW_TILE

    def body(dm_ref, w_ref, o_ref):
        o_ref[...] = jnp.sum(w_ref[...] * dm_ref[...], axis=-1, keepdims=True)

    return pl.pallas_call(
        body, name="cond_ctx_partial", grid=(D_MODEL // tr,),
        out_shape=jax.ShapeDtypeStruct((D_MODEL, 1), F32),
        in_specs=[pl.BlockSpec((1, n_cols), lambda i: (0, 0)), pl.BlockSpec((tr, n_cols), lambda i: (i, 0))],
        out_specs=pl.BlockSpec((tr, 1), lambda i: (i, 0)),
    )(d_modc, ada_w0)


def _cond_ctx_update(gathered, c_ctx, m, v):
    def body(p_ref, w_ref, m_ref, v_ref, g_ref, d_ref, mo_ref, vo_ref):
        acc = p_ref[0:1, :]
        for s in range(1, N_CHIPS):
            acc = acc + p_ref[16 * s:16 * s + 1, :]
        w = w_ref[...]
        g = acc * _dsilu(w)
        g_ref[...] = g
        d_ref[...], mo_ref[...], vo_ref[...] = _adamw_math(w, g, m_ref[...], v_ref[...])

    return pl.pallas_call(body, name="cond_ctx_update", out_shape=[jax.ShapeDtypeStruct((1, D_MODEL), F32)] * 4,
                          in_specs=[VMEM] * 4, out_specs=[VMEM] * 4)(gathered, c_ctx, m, v)


def _local_step(x2, ctx2, target, mod_mine, mod_ctx, lb_logits, scale_full, w_in_full, late_shards, norm_w, gnorm, final_w):
    row = lambda a: a.reshape(1, -1)
    shift0, scale0, gate0 = (row(a) for a in jnp.split(mod_mine[0], 3))
    shift1, scale1, gate1 = (row(a) for a in jnp.split(mod_mine[1], 3))
    shift_c, scale_c, _ = (row(a) for a in jnp.split(mod_ctx, 3))
    nw0, nw1 = norm_w[0:1], norm_w[1:2]
    scales0 = jnp.concatenate([scale_c, scale0])

    g5, h0 = _ln_mod_matmul(ctx2, x2, nw0, jnp.concatenate([shift_c, shift0]), scales0, w_in_full, "hgrn_in_forward")
    o_raw, full = _gla_forward(g5, lb_logits, late_shards)
    x1, res0 = _hgrn_out_forward(o_raw, g5, x2, gnorm, gate0, full["hgrn_w_out"])
    uz, h1 = _ln_mod_matmul(None, x1, nw1, shift1, scale1, full["pool_w_in"], "pool_in_forward")
    d_x2, loss_part, d_final, d_gate1 = _pool_forward_loss(uz, x1, target, gate1, full["pool_w_grp"], scale_full,
                                                           full["pool_w_out"], final_w)

    d_uz, dw_pool_out, dw_pool_grp, d_pscale = _pool_backward(d_x2, uz, gate1, full["pool_w_grp"], scale_full, full["pool_w_out"])
    d_x1, d_nw1, d_mod1 = _ln_mod_backward(d_uz, full["pool_w_in"], None, x1, nw1, scale1, d_x2, "pool_in_backward")
    dw_pool_in = _weight_grad(h1, d_uz, "pool_in_weight_grad")
    d_o, d_z, dw_hgrn_out, d_gate0, d_gnorm = _hgrn_out_backward(d_x1, o_raw, g5, res0, gnorm, gate0, full["hgrn_w_out"])
    late_grads = {"hgrn_w_out": dw_hgrn_out, "pool_w_in": dw_pool_in, "pool_w_grp": dw_pool_grp, "pool_w_out": dw_pool_out}
    d_g5, d_lb, late_slots = _gla_backward(g5, lb_logits, d_o, d_z, [late_grads[k].astype(BF16) for k in GATHER_LATE])
    dw_hgrn_in = _weight_grad_paired(h0, d_g5, "hgrn_in_weight_grad")
    key = GATHER_EARLY[0]
    half = _half_shape(key)
    xi, yi, _ = _my_place()
    own = lax.dynamic_slice(dw_hgrn_in, (0, (2 * xi + yi) * half[1]), half)
    slots0 = lax.dynamic_update_slice(jnp.zeros((N_CHIPS,) + half, BF16), own[None], (2 * xi + yi, 0, 0))
    send_sem, recv_sem, part_thru, slots_thru, token = _scatter_start(dw_hgrn_in, slots0, own, key)
    d_x, d_nw0, d_mod0 = _ln_mod_backward(d_g5, w_in_full, ctx2, x2, nw0 + token[0:1, 0:1], scales0, d_x1, "hgrn_in_backward")
    slots = dict(late_slots)
    pending = (send_sem, recv_sem, part_thru, slots_thru)

    zero = jnp.zeros((1, D_MODEL), F32)
    small = jnp.concatenate([d_mod0[1, 0:2], d_gate0, d_mod1[0, 0:2], d_gate1, d_mod0[0, 0:2], zero, d_nw0, d_nw1, d_gnorm,
                             d_final, d_pscale, d_lb, jnp.broadcast_to(loss_part[0:1, 0:1], (1, D_MODEL)),
                             jnp.zeros((SMALL_ROWS - 17, D_MODEL), F32)], axis=0)
    return {"d_x": d_x, "slots": slots, "pending": pending, "small": small}


def kernel(x, c, ctx, c_ctx, ada_w, ada_b, norm_w, hgrn_w_in, hgrn_lb_logits, hgrn_gnorm_w, hgrn_w_out, pool_w_in, pool_w_grp, pool_scale, pool_w_out, final_norm_w, loss_target, m_c_ctx, m_ada_w, m_ada_b, m_norm_w, m_hgrn_w_in, m_hgrn_lb_logits, m_hgrn_gnorm_w, m_hgrn_w_out, m_pool_w_in, m_pool_w_grp, m_pool_scale, m_pool_w_out, m_final_norm_w, v_c_ctx, v_ada_w, v_ada_b, v_norm_w, v_hgrn_w_in, v_hgrn_lb_logits, v_hgrn_gnorm_w, v_hgrn_w_out, v_pool_w_in, v_pool_w_grp, v_pool_scale, v_pool_w_out, v_final_norm_w):
    xi, yi, ci = _my_place()
    chip = 2 * xi + yi
    dev = 4 * xi + 2 * yi + ci
    ada_cols = ada_w.shape[2]
    lb_cols = hgrn_lb_logits.shape[2]
    ps_cols = pool_scale.shape[1]
    row = lambda a: a.reshape(1, -1)

    def chip_cols(a, n):
        return lax.dynamic_slice_in_dim(a, chip * n, n, axis=a.ndim - 1)

    def from_chips(g, rows_per_dev, take):
        return jnp.concatenate([g[2 * s * rows_per_dev:2 * s * rows_per_dev + take] for s in range(N_CHIPS)], axis=1)

    first = jnp.concatenate([jnp.broadcast_to(c, (8, D_MODEL)), jnp.pad(hgrn_lb_logits[0], ((0, 6), (0, 0))),
                             jnp.pad(pool_scale, ((0, 7), (0, 0)))], axis=1)
    shards = {"hgrn_w_in": hgrn_w_in[0], "hgrn_w_out": hgrn_w_out[0], "pool_w_in": pool_w_in[0],
              "pool_w_grp": pool_w_grp[0], "pool_w_out": pool_w_out[0]}
    first_all, parts_all, w_in_full = _prologue(first, row(c_ctx), ada_w, chip_cols(ada_b, ada_cols),
                                                shards[GATHER_EARLY[0]].astype(BF16))
    cond_all = first_all[::8, :D_MODEL]
    lb_logits = from_chips(first_all[:, D_MODEL:D_MODEL + lb_cols], 8, 2)
    scale_full = from_chips(first_all[:, D_MODEL + lb_cols:], 8, 1)
    cond_rows = jnp.concatenate([cond_all, row(c_ctx), jnp.zeros((7, D_MODEL), F32)], axis=0)
    mod_all = from_chips(parts_all, 32, 32).reshape(2, 16, 3 * D_MODEL)
    mod_mine = lax.dynamic_index_in_dim(mod_all, dev, axis=1, keepdims=False)

    loc = _local_step(x[0], ctx[0], loss_target[0], mod_mine, mod_all[0, 8], lb_logits, scale_full, w_in_full,
                      [shards[k].astype(BF16) for k in GATHER_LATE], norm_w, hgrn_gnorm_w, row(final_norm_w))

    small_all, sums, g_c, d_c, m_c, v_c = _small_sums_and_cond_ctx(loc["small"], ada_w[0], row(c_ctx), row(m_c_ctx), row(v_c_ctx))
    loss = sums[16, 0]

    def reduce_scattered(slots, names, name):
        return dict(zip(names, _sum_and_exchange([slots[k] for k in names], names, name)))

    big_grads = reduce_scattered(loc["slots"], GATHER_LATE, "exchange_halves_late")

    out = {}

    def update(name, w, g, m, v):
        shape = w.shape
        w2, g2, m2, v2 = (a.reshape(-1, shape[-1]) for a in (w, g, m, v))
        d, mn, vn = _adamw(w2, g2, m2, v2, "adamw_" + name)
        out[name] = tuple(a.reshape(shape) for a in (g2, d, mn, vn))

    moments = {"hgrn_w_in": (m_hgrn_w_in, v_hgrn_w_in), "hgrn_w_out": (m_hgrn_w_out, v_hgrn_w_out),
               "pool_w_in": (m_pool_w_in, v_pool_w_in), "pool_w_grp": (m_pool_w_grp, v_pool_w_grp),
               "pool_w_out": (m_pool_w_out, v_pool_w_out)}
    weights = {"hgrn_w_in": hgrn_w_in, "hgrn_w_out": hgrn_w_out, "pool_w_in": pool_w_in, "pool_w_grp": pool_w_grp,
               "pool_w_out": pool_w_out}
    for k in GATHER_LATE:
        update(k, weights[k], big_grads[k], *moments[k])

    g_ada_b = jnp.stack([(sums[0:3] + sums[6:9]).reshape(-1), sums[3:6].reshape(-1)])
    update("ada_b", ada_b, g_ada_b, m_ada_b, v_ada_b)
    update("norm_w", norm_w, sums[9:11], m_norm_w, v_norm_w)
    update("hgrn_gnorm_w", hgrn_gnorm_w, sums[11:12], m_hgrn_gnorm_w, v_hgrn_gnorm_w)
    update("final_norm_w", row(final_norm_w), sums[12:13], row(m_final_norm_w), row(v_final_norm_w))
    update("pool_scale", pool_scale, chip_cols(sums[13:14], ps_cols), m_pool_scale, v_pool_scale)
    update("hgrn_lb_logits", hgrn_lb_logits, chip_cols(sums[14:16], lb_cols)[None], m_hgrn_lb_logits, v_hgrn_lb_logits)

    per_dev = small_all.reshape(N_DEV, SMALL_ROWS, D_MODEL)
    pad7 = jnp.zeros((7, 3 * D_MODEL), F32)
    dm0 = jnp.concatenate([per_dev[:, 0:3].reshape(N_DEV, -1), sums[6:9].reshape(1, -1), pad7], axis=0)
    dm1 = jnp.concatenate([per_dev[:, 3:6].reshape(N_DEV, -1), jnp.zeros((8, 3 * D_MODEL), F32)], axis=0)
    d_mod = chip_cols(jnp.stack([dm0, dm1]), ada_cols)
    out["ada_w"] = _ada_update(cond_rows.T, d_mod, ada_w, m_ada_w, v_ada_w)

    out["c_ctx"] = tuple(a.reshape(-1) for a in (g_c, d_c, m_c, v_c))
    out["final_norm_w"] = tuple(a.reshape(-1) for a in out["final_norm_w"])

    done = [g_c, out["ada_w"][1]] + [out[k][1] for k in GATHER_LATE]
    key = GATHER_EARLY[0]
    _, early = _scatter_wait(*loc["pending"], done, key)
    big_grads = reduce_scattered({key: early}, GATHER_EARLY, "exchange_halves_early")
    for k in GATHER_EARLY:
        update(k, weights[k], big_grads[k], *moments[k])

    names = ["c_ctx", "ada_w", "ada_b", "norm_w", "hgrn_w_in", "hgrn_lb_logits", "hgrn_gnorm_w", "hgrn_w_out", "pool_w_in",
             "pool_w_grp", "pool_scale", "pool_w_out", "final_norm_w"]
    return (loss, loc["d_x"][None], *[out[k][0] for k in names], *[out[k][1] for k in names], *[out[k][2] for k in names],
            *[out[k][3] for k in names])
```

```python
import functools

import numpy as np
import jax
import jax.numpy as jnp
from jax import lax
from jax.experimental import pallas as pl
from jax.experimental.pallas import tpu as pltpu

F32 = jnp.float32
BF16 = jnp.bfloat16

D_MODEL = 1024
SEQ = 2048
CTX_LEN = 256
ROWS_ALL = CTX_LEN + SEQ
HEADS = 8
HEAD_DIM = 128
CHUNK = 64
N_CTX_CHUNKS = CTX_LEN // CHUNK
N_LAT_CHUNKS = SEQ // CHUNK
N_CHUNKS = N_CTX_CHUNKS + N_LAT_CHUNKS
GRID_W = 64
POOL_WINDOWS = (2, 4, 8, 16)
POOL_GROUPS = 4
POOL_GROUP_DIM = 256
HGRN_SECTIONS = 5
POOL_SECTIONS = 2
EPS = 1e-6
N_DEV = 8
N_CHIPS = 4
ROW_TILE = 256
SMALL_ROWS = 24

ADAM_LR = 0.001
ADAM_B1 = 0.9
ADAM_B2 = 0.999
ADAM_EPS = 1e-08
ADAM_WD = 0.01
ADAM_STEP = 10

MESH = pl.DeviceIdType.MESH
MIB = 1 << 20
ANY = pl.BlockSpec(memory_space=pl.ANY)
VMEM = pl.BlockSpec(memory_space=pltpu.VMEM)


def _params(vmem_mib=None):
    if vmem_mib is None:
        return pltpu.CompilerParams()
    return pltpu.CompilerParams(vmem_limit_bytes=vmem_mib * MIB)


def _pin(*operands):
    return [pltpu.with_memory_space_constraint(a, pltpu.HBM) if a.size * a.dtype.itemsize >= MIB else a for a in operands]


def _sig(a):
    return 1.0 / (1.0 + jnp.exp(-a))


def _silu(a):
    return a * _sig(a)


def _dsilu(a):
    s = _sig(a)
    return s * (1.0 + a * (1.0 - s))


def _mm(a, b):
    return jnp.dot(a.astype(BF16), b.astype(BF16), preferred_element_type=F32)


def _mm_nt(a, b):
    return lax.dot_general(a.astype(BF16), b.astype(BF16), (((1,), (1,)), ((), ())), preferred_element_type=F32)


def _mm_tn(a, b):
    return lax.dot_general(a.astype(BF16), b.astype(BF16), (((0,), (0,)), ((), ())), preferred_element_type=F32)


def _split2(a):
    hi = a.astype(BF16)
    lo = (a - hi.astype(F32)).astype(BF16)
    return hi, lo


def _mm_exact_lhs(m_bf, a):
    hi, lo = _split2(a)
    return jnp.dot(m_bf, hi, preferred_element_type=F32) + jnp.dot(m_bf, lo, preferred_element_type=F32)


def _mm_f32(a, b):
    ah, al = _split2(a)
    bh, bl = _split2(b)
    return (jnp.dot(ah, bh, preferred_element_type=F32) + jnp.dot(al, bh, preferred_element_type=F32)
            + jnp.dot(ah, bl, preferred_element_type=F32))


def _my_place():
    return lax.axis_index("x"), lax.axis_index("y"), lax.axis_index("c")


def _small_gather(x_ref, out_ref, send_sems, recv_sems, local_sem):
    m_per = x_ref.shape[0]
    x, y, c = _my_place()
    me, sibling = (x, y, c), (x, y, 1 - c)
    chips = [(1 - x, y), (x, 1 - y), (1 - x, 1 - y)]

    def rows(px, py, pc):
        return out_ref.at[pl.ds((4 * px + 2 * py + pc) * m_per, m_per), :]

    def copy(k, block, to, src=None):
        return pltpu.make_async_remote_copy(
            src_ref=rows(*block) if src is None else src, dst_ref=rows(*block),
            send_sem=send_sems.at[k], recv_sem=recv_sems.at[k], device_id=to, device_id_type=MESH)

    def mine():
        return pltpu.make_async_copy(x_ref, rows(*me), local_sem)

    def first():
        return [copy(0, me, sibling, src=x_ref)] + [copy(1 + j, me, (*chip, c), src=x_ref) for j, chip in enumerate(chips)]

    def start():
        mine().start()
        for cp in first():
            cp.start()

    def finish():
        passed = [copy(4 + j, (*chip, c), sibling) for j, chip in enumerate(chips)]
        for j, chip in enumerate(chips):
            copy(1 + j, (*chip, c), me).wait_recv()
            passed[j].start()
        copy(0, sibling, me).wait_recv()
        for j, chip in enumerate(chips):
            copy(4 + j, (*chip, 1 - c), me).wait_recv()
        for cp in first() + passed:
            cp.wait_send()
        mine().wait()

    return start, finish


def _all_gather_small(blk, name):
    m_per, n = blk.shape

    def body(x_ref, out_ref, send_sems, recv_sems, local_sem):
        start, finish = _small_gather(x_ref, out_ref, send_sems, recv_sems, local_sem)
        start()
        finish()

    return pl.pallas_call(
        body, name=name,
        out_shape=jax.ShapeDtypeStruct((N_DEV * m_per, n), blk.dtype),
        in_specs=[VMEM], out_specs=VMEM,
        scratch_shapes=[pltpu.SemaphoreType.DMA((7,)), pltpu.SemaphoreType.DMA((7,)), pltpu.SemaphoreType.DMA],
    )(blk)


W_SPECS = {
    "hgrn_w_in": ((D_MODEL, 5 * D_MODEL), (1, 1280, 0, 512)),
    "hgrn_w_out": ((D_MODEL, D_MODEL), (0, 256, 0, 128)),
    "pool_w_in": ((D_MODEL, 2 * D_MODEL), (1, 512, 0, 512)),
    "pool_w_grp": ((POOL_GROUPS, POOL_GROUP_DIM, POOL_GROUP_DIM), (1, 64, 1, 32)),
    "pool_w_out": ((D_MODEL, D_MODEL), (0, 256, 0, 128)),
}
W_NAMES = tuple(W_SPECS)


def _al(v, m):
    return pl.multiple_of(v, m)


def _region(ref, spec, chip, half):
    ca, cn, ha, hn = spec
    idx = [slice(None)] * len(ref.shape)
    if ca == ha:
        if half is None:
            idx[ca] = pl.ds(_al(chip * cn, cn), cn)
        else:
            idx[ca] = pl.ds(_al(chip * cn + half * hn, hn), hn)
    else:
        idx[ca] = pl.ds(_al(chip * cn, cn), cn)
        if half is not None:
            idx[ha] = pl.ds(_al(half * hn, hn), hn)
    return ref.at[tuple(idx)]


def _half_of(ref, spec, half):
    _, _, ha, hn = spec
    idx = [slice(None)] * len(ref.shape)
    idx[ha] = pl.ds(_al(half * hn, hn), hn)
    return ref.at[tuple(idx)]


PIECE_BYTES = 256 * 1024


def _pieces(ref):
    lead = ref.shape[0]
    want = (int(np.prod(ref.shape)) * ref.dtype.itemsize) // PIECE_BYTES
    n = max([1] + [k for k in range(1, want + 1) if lead % k == 0 and (lead // k) % 16 == 0])
    rows = lead // n
    return [ref.at[pl.ds(i * rows, rows)] for i in range(n)]


def _half_shape(name):
    full, (ca, cn, ha, hn) = W_SPECS[name]
    shp = list(full)
    shp[ca] = cn
    shp[ha] = hn
    return tuple(shp)


def _gather_direct(names, sh, full, send_sems, recv_sems, local_sems):
    specs = [W_SPECS[k][1] for k in names]
    x, y, c = _my_place()
    chip_me = 2 * x + y
    chips = [(1 - x, y), (x, 1 - y), (1 - x, 1 - y)]

    def local(a):
        return pltpu.make_async_copy(sh[a], _region(full[a], specs[a], chip_me, None), local_sems.at[a])

    def remote(a, src, dst, to):
        return pltpu.make_async_remote_copy(src_ref=src, dst_ref=dst, send_sem=send_sems.at[a], recv_sem=recv_sems.at[a],
                                            device_id=to, device_id_type=MESH)

    def start():
        for a in range(len(names)):
            local(a).start()
            for px, py in chips:
                remote(a, sh[a], _region(full[a], specs[a], chip_me, None), (px, py, c)).start()

    def wait():
        for a in range(len(names)):
            ca, cn, _, _ = specs[a]
            idx = [slice(None)] * len(full[a].shape)
            idx[ca] = pl.ds(0, 3 * cn)
            three = full[a].at[tuple(idx)]
            remote(a, three, three, (x, y, c)).wait()
            local(a).wait()

    return start, wait


def _gather_two_level(names, sh, land, full, send_sems, recv_sems, local_sems):
    n = len(names)
    specs = [W_SPECS[k][1] for k in names]
    x, y, c = _my_place()
    chip_me = 2 * x + y
    sibling = (x, y, 1 - c)
    chips = [(1 - x, y), (x, 1 - y), (1 - x, 1 - y)]

    def remote(k, src, dst, to):
        return pltpu.make_async_remote_copy(src_ref=src, dst_ref=dst, send_sem=send_sems.at[k], recv_sem=recv_sems.at[k],
                                            device_id=to, device_id_type=MESH)

    def three_halves(a):
        ca, cn, ha, hn = specs[a]
        idx = [slice(None)] * len(land[a].shape)
        if ca == ha:
            idx[ca] = pl.ds(0, 3 * hn)
        else:
            idx[ca], idx[ha] = pl.ds(0, 3 * cn), pl.ds(0, hn)
        return land[a].at[tuple(idx)]

    def own(a):
        return pltpu.make_async_copy(sh[a], _region(land[a], specs[a], chip_me, None), local_sems.at[a])

    def out(a):
        return pltpu.make_async_copy(land[a], full[a], local_sems.at[a])

    def start():
        for a in range(n):
            own(a).start()
            for px, py in chips:
                remote(a, _half_of(sh[a], specs[a], c), _region(land[a], specs[a], chip_me, c), (px, py, c)).start()

    def forward():
        for a in range(n):
            remote(a, three_halves(a), three_halves(a), sibling).wait_recv()
            for px, py in chips:
                landed = _region(land[a], specs[a], 2 * px + py, c)
                remote(n + a, landed, landed, sibling).start()

    def finish():
        for a in range(n):
            remote(n + a, three_halves(a), three_halves(a), sibling).wait_recv()
            remote(a, three_halves(a), three_halves(a), sibling).wait_send()
            remote(n + a, three_halves(a), three_halves(a), sibling).wait_send()
            own(a).wait()
        for a in range(n):
            out(a).start()
        for a in range(n):
            out(a).wait()

    return start, forward, finish


def _scatter_direct(names, part, slots, send_sems, recv_sems, local_sems):
    specs = [W_SPECS[k][1] for k in names]
    x, y, c = _my_place()
    dev_me = 4 * x + 2 * y + c

    def local(a):
        return pltpu.make_async_copy(_region(part[a], specs[a], 2 * x + y, c), slots[a].at[dev_me], local_sems.at[a])

    def start():
        for a in range(len(names)):
            local(a).start()
            for flip in range(1, N_DEV):
                tx = 1 - x if flip >> 2 else x
                ty = 1 - y if (flip >> 1) & 1 else y
                tc = 1 - c if flip & 1 else c
                pltpu.make_async_remote_copy(src_ref=_region(part[a], specs[a], 2 * tx + ty, tc), dst_ref=slots[a].at[dev_me],
                                             send_sem=send_sems.at[a], recv_sem=recv_sems.at[a], device_id=(tx, ty, tc),
                                             device_id_type=MESH).start()

    def wait():
        for a in range(len(names)):
            seven = slots[a].at[pl.ds(0, N_DEV - 1)]
            pltpu.make_async_remote_copy(src_ref=seven, dst_ref=seven, send_sem=send_sems.at[a], recv_sem=recv_sems.at[a],
                                         device_id=(x, y, c), device_id_type=MESH).wait()
            local(a).wait()

    return start, wait


HBM_SPEC = pl.BlockSpec(memory_space=pltpu.HBM)
SEM_SPEC = pl.BlockSpec(memory_space=pltpu.SEMAPHORE)
SPLIT_EFFECT = pltpu.SideEffectType.DATAFLOW_SIDE_EFFECTING


def _scatter_start(part, slots, after, name_key):
    _, cols, _, _ = W_SPECS[name_key][1]

    def body(part_ref, slots_ref, after_ref, send_sem, recv_sem, part_thru, slots_thru, token):
        x, y, c = _my_place()
        for tx, ty in ((1 - x, y), (x, 1 - y), (1 - x, 1 - y)):
            src = part_ref.at[:, pl.ds(_al((2 * tx + ty) * cols, cols), cols)]
            pltpu.make_async_remote_copy(src_ref=src, dst_ref=slots_ref.at[2 * x + y], send_sem=send_sem, recv_sem=recv_sem,
                                         device_id=(tx, ty, c), device_id_type=MESH).start()
        token[...] = jnp.zeros_like(token)

    return pl.pallas_call(
        body, name="scatter_start_" + name_key,
        out_shape=(pltpu.SemaphoreType.DMA(()), pltpu.SemaphoreType.DMA(()), pltpu.HBM(part.shape, part.dtype),
                   pltpu.HBM(slots.shape, slots.dtype), jax.ShapeDtypeStruct((8, 128), F32)),
        in_specs=(HBM_SPEC, HBM_SPEC, ANY), out_specs=(SEM_SPEC, SEM_SPEC, HBM_SPEC, HBM_SPEC, VMEM),
        input_output_aliases={0: 2, 1: 3},
        compiler_params=pltpu.CompilerParams(has_side_effects=SPLIT_EFFECT),
    )(pltpu.with_memory_space_constraint(part, pltpu.HBM), pltpu.with_memory_space_constraint(slots, pltpu.HBM), after)


def _scatter_wait(send_sem, recv_sem, part_thru, slots_thru, after, name_key):
    def body(part_ref, slots_ref, send_sem, recv_sem, *rest):
        x, y, c = _my_place()
        landed = slots_ref.at[pl.ds(0, N_CHIPS - 1)]
        copy = pltpu.make_async_remote_copy(src_ref=landed, dst_ref=landed, send_sem=send_sem, recv_sem=recv_sem,
                                            device_id=(x, y, c), device_id_type=MESH)
        copy.wait_send()
        copy.wait_recv()

    return pl.pallas_call(
        body, name="scatter_wait_" + name_key,
        out_shape=(pltpu.HBM(part_thru.shape, part_thru.dtype), pltpu.HBM(slots_thru.shape, slots_thru.dtype)),
        in_specs=(HBM_SPEC, HBM_SPEC, SEM_SPEC, SEM_SPEC) + (ANY,) * len(after), out_specs=(HBM_SPEC, HBM_SPEC),
        input_output_aliases={0: 0, 1: 1},
        compiler_params=pltpu.CompilerParams(has_side_effects=SPLIT_EFFECT),
    )(part_thru, slots_thru, send_sem, recv_sem, *after)


def _comm_sems(n):
    return [pltpu.SemaphoreType.DMA((n,)), pltpu.SemaphoreType.DMA((n,)), pltpu.SemaphoreType.DMA((n,))]


GATHER_EARLY = ("hgrn_w_in",)
GATHER_LATE = ("hgrn_w_out", "pool_w_in", "pool_w_grp", "pool_w_out")


def _all_gather_weights(shards, names):
    n = len(names)
    specs = [W_SPECS[k][1] for k in names]

    def body(*refs):
        sh, full = refs[:n], refs[n:2 * n]
        send_sems, recv_sems, local_sems = refs[2 * n:]
        x, y, c = _my_place()
        chip_me = 2 * x + y
        sibling = (x, y, 1 - c)
        chips = [(1 - x, y), (x, 1 - y), (1 - x, 1 - y)]

        def remote(a, k, src, dst, to):
            return pltpu.make_async_remote_copy(src_ref=src, dst_ref=dst, send_sem=send_sems.at[6 * a + k],
                                                recv_sem=recv_sems.at[6 * a + k], device_id=to, device_id_type=MESH)

        local = [pltpu.make_async_copy(sh[a], _region(full[a], specs[a], chip_me, None), local_sems.at[a]) for a in range(n)]
        for cp in local:
            cp.start()
        sends = []
        for a in range(n):
            for j, (px, py) in enumerate(chips):
                src, dst = _half_of(sh[a], specs[a], c), _region(full[a], specs[a], chip_me, c)
                for s_piece, d_piece in zip(_pieces(src), _pieces(dst)):
                    remote(a, j, s_piece, d_piece, (px, py, c)).start()
                sends.append(remote(a, j, src, dst, (px, py, c)))
        for a in range(n):
            for j, (px, py) in enumerate(chips):
                landed = _region(full[a], specs[a], 2 * px + py, c)
                remote(a, j, landed, landed, (px, py, c)).wait_recv()
                for piece in _pieces(landed):
                    remote(a, 3 + j, piece, piece, sibling).start()
                sends.append(remote(a, 3 + j, landed, landed, sibling))
        for a in range(n):
            for j, (px, py) in enumerate(chips):
                other = _region(full[a], specs[a], 2 * px + py, 1 - c)
                remote(a, 3 + j, other, other, sibling).wait_recv()
        for cp in sends:
            cp.wait_send()
        for cp in local:
            cp.wait()

    return pl.pallas_call(
        body, name="all_gather_weights",
        out_shape=[jax.ShapeDtypeStruct(W_SPECS[k][0], BF16) for k in names],
        in_specs=[VMEM] * n, out_specs=[VMEM] * n,
        scratch_shapes=[pltpu.SemaphoreType.DMA((6 * n,)), pltpu.SemaphoreType.DMA((6 * n,)), pltpu.SemaphoreType.DMA((n,))],
        compiler_params=_params(32),
    )(*shards)


def _weight_gather(sh, full, spec, send_sems, recv_sems, local_sem):
    x, y, c = _my_place()
    chip_me = 2 * x + y
    sibling = (x, y, 1 - c)
    chips = [(1 - x, y), (x, 1 - y), (1 - x, 1 - y)]

    def remote(k, src, dst, to):
        return pltpu.make_async_remote_copy(src_ref=src, dst_ref=dst, send_sem=send_sems.at[k], recv_sem=recv_sems.at[k],
                                            device_id=to, device_id_type=MESH)

    def own():
        return pltpu.make_async_copy(sh, _region(full, spec, chip_me, None), local_sem)

    def direct():
        return [remote(j, _half_of(sh, spec, c), _region(full, spec, chip_me, c), (px, py, c)) for j, (px, py) in enumerate(chips)]

    def start():
        own().start()
        for cp in direct():
            cp.start()

    def finish():
        passed = []
        for j, (px, py) in enumerate(chips):
            landed = _region(full, spec, 2 * px + py, c)
            remote(j, landed, landed, (px, py, c)).wait_recv()
            passed.append(remote(3 + j, landed, landed, sibling))
            passed[-1].start()
        for j, (px, py) in enumerate(chips):
            other = _region(full, spec, 2 * px + py, 1 - c)
            remote(3 + j, other, other, sibling).wait_recv()
        for cp in direct() + passed:
            cp.wait_send()
        own().wait()

    return start, finish


def _prologue(first, c_ctx, ada_w, ada_b_cols, w_shard):
    n_layers, _, n_cols = ada_w.shape
    key = GATHER_EARLY[0]
    spec = W_SPECS[key][1]
    m_first = first.shape[0]

    def body(first_ref, cctx_ref, adaw_ref, adab_ref, sh_ref, first_all, parts_all, full_ref, parts_scr,
             s1, r1, l1, s2, r2, l2, ws, wr, wl):
        start_first, finish_first = _small_gather(first_ref, first_all, s1, r1, l1)
        start_parts, finish_parts = _small_gather(parts_scr, parts_all, s2, r2, l2)
        start_weight, finish_weight = _weight_gather(sh_ref, full_ref, spec, ws, wr, wl)
        start_first()
        start_weight()
        finish_first()
        cond = jnp.concatenate([first_all[m_first * d:m_first * d + 1, 0:D_MODEL] for d in range(N_DEV)]
                               + [cctx_ref[...], jnp.zeros((16 - N_DEV - 1, D_MODEL), F32)], axis=0)
        act = _silu(cond)
        for i in range(n_layers):
            parts_scr[16 * i:16 * (i + 1), :] = _mm_f32(act, adaw_ref[i]) + adab_ref[i]
        start_parts()
        finish_parts()
        finish_weight()

    seven = [pltpu.SemaphoreType.DMA((7,)), pltpu.SemaphoreType.DMA((7,)), pltpu.SemaphoreType.DMA]
    return pl.pallas_call(
        body, name="prologue",
        out_shape=[jax.ShapeDtypeStruct((N_DEV * m_first, first.shape[1]), F32),
                   jax.ShapeDtypeStruct((N_DEV * 16 * n_layers, n_cols), F32), jax.ShapeDtypeStruct(W_SPECS[key][0], BF16)],
        in_specs=[VMEM] * 5, out_specs=[VMEM] * 3,
        scratch_shapes=[pltpu.VMEM((16 * n_layers, n_cols), F32)] + seven + seven
        + [pltpu.SemaphoreType.DMA((6,)), pltpu.SemaphoreType.DMA((6,)), pltpu.SemaphoreType.DMA],
        compiler_params=_params(48),
    )(first, c_ctx, ada_w, ada_b_cols.reshape(n_layers, 1, n_cols), w_shard)


def _slot_shapes(names):
    return [jax.ShapeDtypeStruct((N_DEV,) + _half_shape(k), BF16) for k in names]


def _scatter_grads(parts, names):
    n = len(names)

    def body(*refs):
        start, wait = _scatter_direct(names, refs[:n], refs[n:2 * n], *refs[2 * n:])
        start()
        wait()

    return pl.pallas_call(body, name="scatter_grads", out_shape=_slot_shapes(names), in_specs=[ANY] * n, out_specs=[ANY] * n,
                          scratch_shapes=_comm_sems(n))(*parts)


def _sum_and_exchange(slots, names, name):
    n = len(names)
    specs = [W_SPECS[k][1] for k in names]

    def shard_shape(k):
        shp = list(_half_shape(k))
        shp[W_SPECS[k][1][2]] *= 2
        return tuple(shp)

    def body(*refs):
        slot, out, half = refs[:n], refs[n:2 * n], refs[2 * n:3 * n]
        send_sems, recv_sems, local_sems = refs[3 * n:]
        x, y, c = _my_place()
        sibling = (x, y, 1 - c)
        for a in range(n):
            acc = slot[a][0].astype(F32)
            for d in range(1, slot[a].shape[0]):
                acc = acc + slot[a][d].astype(F32)
            half[a][...] = acc

        def remote(a, src, dst):
            return pltpu.make_async_remote_copy(src_ref=src, dst_ref=dst, send_sem=send_sems.at[a], recv_sem=recv_sems.at[a],
                                                device_id=sibling, device_id_type=MESH)

        local = [pltpu.make_async_copy(half[a], _half_of(out[a], specs[a], c), local_sems.at[a]) for a in range(n)]
        for cp in local:
            cp.start()
        for a in range(n):
            mine = _half_of(out[a], specs[a], c)
            for src, dst in zip(_pieces(half[a]), _pieces(mine)):
                remote(a, src, dst).start()
        for a in range(n):
            theirs = _half_of(out[a], specs[a], 1 - c)
            remote(a, theirs, theirs).wait_recv()
        for a in range(n):
            remote(a, half[a], half[a]).wait_send()
        for cp in local:
            cp.wait()

    return pl.pallas_call(
        body, name=name,
        out_shape=[jax.ShapeDtypeStruct(shard_shape(k), F32) for k in names],
        in_specs=[VMEM] * n, out_specs=[VMEM] * n,
        scratch_shapes=[pltpu.VMEM(_half_shape(k), F32) for k in names]
        + [pltpu.SemaphoreType.DMA((n,)), pltpu.SemaphoreType.DMA((n,)), pltpu.SemaphoreType.DMA((n,))],
        compiler_params=_params(40),
    )(*slots)


def _mod_parts(c_rows, ada_w, ada_b_cols):
    n_layers, _, n_cols = ada_w.shape

    def body(c_ref, w_ref, b_ref, o_ref):
        o_ref[...] = _mm_f32(_silu(c_ref[...]), w_ref[...]) + b_ref[...]

    return pl.pallas_call(
        body, name="mod_parts", grid=(n_layers,),
        out_shape=jax.ShapeDtypeStruct((n_layers, 16, n_cols), F32),
        in_specs=[pl.BlockSpec((16, D_MODEL), lambda i: (0, 0)),
                  pl.BlockSpec((None, D_MODEL, n_cols), lambda i: (i, 0, 0)),
                  pl.BlockSpec((None, 1, n_cols), lambda i: (i, 0, 0))],
        out_specs=pl.BlockSpec((None, 16, n_cols), lambda i: (i, 0, 0)),
        compiler_params=_params(40),
    )(c_rows, ada_w, ada_b_cols.reshape(n_layers, 1, n_cols))


def _ln_mod_matmul(ctx_tile, xin, nw, shift, scale, w, name):
    n_mod = shift.shape[0]
    skip = n_mod - 1
    tm = ROW_TILE if skip else 2 * ROW_TILE
    rows = xin.shape[0] + skip * tm
    n_sec = w.shape[1] // D_MODEL

    def body(*refs):
        c_ref = refs[0] if skip else None
        x_ref, nw_ref, sh_ref, sc_ref, w_ref, g_ref, h_ref = refs[skip:]
        xv = jnp.where(pl.program_id(0) == 0, c_ref[...], x_ref[...]) if skip else x_ref[...]
        r = lax.rsqrt(jnp.mean(xv * xv, axis=-1, keepdims=True) + EPS)
        h_f32 = (xv * r * nw_ref[...]) * (1.0 + sc_ref[...]) + sh_ref[...]
        h = h_f32.astype(BF16)
        h_ref[...] = h
        for k in range(n_sec):
            g_ref[k] = jnp.dot(h, w_ref[:, k * D_MODEL:(k + 1) * D_MODEL], preferred_element_type=F32)

    mod_spec = pl.BlockSpec((None, 1, D_MODEL), lambda i: (jnp.minimum(i, n_mod - 1), 0, 0))
    return pl.pallas_call(
        body, name=name, grid=(rows // tm,),
        out_shape=[pltpu.HBM((n_sec, rows, D_MODEL), F32), pltpu.HBM((rows, D_MODEL), BF16)],
        in_specs=[pl.BlockSpec((tm, D_MODEL), lambda i: (0, 0))] * skip
        + [pl.BlockSpec((tm, D_MODEL), lambda i: (jnp.maximum(i - skip, 0), 0)),
           pl.BlockSpec((1, D_MODEL), lambda i: (0, 0)),
           mod_spec, mod_spec,
           pl.BlockSpec((D_MODEL, n_sec * D_MODEL), lambda i: (0, 0))],
        out_specs=[pl.BlockSpec((n_sec, tm, D_MODEL), lambda i: (0, i, 0)),
                   pl.BlockSpec((tm, D_MODEL), lambda i: (i, 0))],
        compiler_params=_params(48),
    )(*_pin(*([ctx_tile] * skip), xin, nw, shift.reshape(n_mod, 1, D_MODEL), scale.reshape(n_mod, 1, D_MODEL), w))


def _chunk_masks(rev):
    rid = lax.broadcasted_iota(jnp.int32, (CHUNK, CHUNK), 0)
    cid = lax.broadcasted_iota(jnp.int32, (CHUNK, CHUNK), 1)
    keep = (cid >= rid) if rev else (cid <= rid)
    keep_t = (cid <= rid) if rev else (cid >= rid)
    one, zero = jnp.ones((CHUNK, CHUNK), F32), jnp.zeros((CHUNK, CHUNK), F32)
    return keep, jnp.where(keep, one, zero).astype(BF16), jnp.where(keep_t, one, zero).astype(BF16)


def _chunk_rows(t, rev, latent):
    n = N_LAT_CHUNKS if latent else N_CTX_CHUNKS
    base = CTX_LEN if latent else 0
    idx = (n - 1 - t) if rev else t
    return pl.multiple_of(base + idx * CHUNK, CHUNK)


def _gates(fpre, lb):
    sg = _sig(fpre)
    f = lb + (1.0 - lb) * sg
    return sg, f, 1.0 - f, jnp.log(f)


G_SPEC = lambda sec: pl.BlockSpec((None, ROWS_ALL, HEAD_DIM), lambda h, sec=sec: (sec, 0, h))


def _gla_forward(g5, lb_logits):
    q_scale = HEAD_DIM ** -0.5

    def body(ff_ref, fb_ref, v_ref, q_ref, lg_ref, o_ref, st_ref, decay_ref, qt_ref):
        for rev in (False, True):
            f_ref = fb_ref if rev else ff_ref
            lb = _sig(lg_ref[1:2, :] if rev else lg_ref[0:1, :])
            keep, tri, _ = _chunk_masks(rev)
            last = 0 if rev else CHUNK - 1
            mid = CHUNK // 2 if rev else CHUNK // 2 - 1

            def local_step(t, carry, latent):
                r0 = _chunk_rows(t, rev, latent)
                rows = pl.ds(r0, CHUNK)
                step = t + (N_CTX_CHUNKS if latent else 0)
                _, _, k, lf = _gates(f_ref[rows, :], lb)
                v = v_ref[rows, :]
                b = _mm_exact_lhs(tri, lf)
                bl = b[last:last + 1, :]
                if latent:
                    q = _silu(q_ref[rows, :]) * q_scale
                    bm = b[mid:mid + 1, :]
                    a = _mm_nt(q * jnp.exp(b - bm), k * jnp.exp(bm - b))
                    o = _mm(jnp.where(keep, a, 0.0), v)
                    orow = pl.ds(pl.multiple_of(r0 - CTX_LEN, CHUNK), CHUNK)
                    qt_ref[orow, :] = (q * jnp.exp(b)).astype(BF16)
                    if rev:
                        o_ref[orow, :] += o
                    else:
                        o_ref[orow, :] = o
                decay_ref[step] = jnp.exp(bl)
                st_ref[step] = _mm_tn(v, k * jnp.exp(bl - b))
                return carry

            lax.fori_loop(0, N_CTX_CHUNKS, functools.partial(local_step, latent=False), 0, unroll=2)
            lax.fori_loop(0, N_LAT_CHUNKS, functools.partial(local_step, latent=True), 0, unroll=4)

            def scan_step(t, st):
                update = st_ref[t]
                st_ref[t] = st
                return st * decay_ref[t] + update

            lax.fori_loop(0, N_CHUNKS, scan_step, jnp.zeros((HEAD_DIM, HEAD_DIM), F32), unroll=2)

            def inter_step(t, carry):
                r0 = _chunk_rows(t, rev, True)
                orow = pl.ds(pl.multiple_of(r0 - CTX_LEN, CHUNK), CHUNK)
                o_ref[orow, :] += lax.dot_general(qt_ref[orow, :], st_ref[t + N_CTX_CHUNKS].astype(BF16),
                                                  (((1,), (1,)), ((), ())), preferred_element_type=F32)
                return carry

            lax.fori_loop(0, N_LAT_CHUNKS, inter_step, 0, unroll=4)

    return pl.pallas_call(
        body, name="gla_forward", grid=(HEADS,),
        out_shape=jax.ShapeDtypeStruct((SEQ, D_MODEL), F32),
        in_specs=[G_SPEC(0), G_SPEC(1), G_SPEC(2), G_SPEC(3), pl.BlockSpec((2, HEAD_DIM), lambda h: (0, h))],
        out_specs=pl.BlockSpec((SEQ, HEAD_DIM), lambda h: (0, h)),
        scratch_shapes=[pltpu.VMEM((N_CHUNKS, HEAD_DIM, HEAD_DIM), F32), pltpu.VMEM((N_CHUNKS, 1, HEAD_DIM), F32),
                        pltpu.VMEM((SEQ, HEAD_DIM), BF16)],
        compiler_params=_params(32),
    )(g5, g5, g5, g5, lb_logits)


def _gla_backward(g5, lb_logits, d_o, d_z):
    q_scale = HEAD_DIM ** -0.5

    def body(ff_ref, fb_ref, v_ref, q_ref, lg_ref, do_ref, dz_ref, dg_ref, dlg_ref, st_ref, dst_ref, decay_ref):
        dg_ref[3, 0:CTX_LEN, :] = jnp.zeros((CTX_LEN, HEAD_DIM), F32)
        dg_ref[4, 0:CTX_LEN, :] = jnp.zeros((CTX_LEN, HEAD_DIM), F32)
        dg_ref[4, CTX_LEN:ROWS_ALL, :] = dz_ref[...]
        is_row = lax.broadcasted_iota(jnp.int32, (CHUNK, HEAD_DIM), 0)
        for rev in (False, True):
            d = 1 if rev else 0
            f_ref = fb_ref if rev else ff_ref
            lb = _sig(lg_ref[d:d + 1, :])
            keep, tri, tri_t = _chunk_masks(rev)
            last = 0 if rev else CHUNK - 1
            mid = CHUNK // 2 if rev else CHUNK // 2 - 1

            def local_step(t, carry, latent):
                r0 = _chunk_rows(t, rev, latent)
                rows = pl.ds(r0, CHUNK)
                step = t + (N_CTX_CHUNKS if latent else 0)
                _, _, k, lf = _gates(f_ref[rows, :], lb)
                b = _mm_exact_lhs(tri, lf)
                bl = b[last:last + 1, :]
                decay_ref[step] = jnp.exp(bl)
                st_ref[step] = _mm_tn(v_ref[rows, :], k * jnp.exp(bl - b))
                if latent:
                    q_t = _silu(q_ref[rows, :]) * q_scale * jnp.exp(b)
                    dst_ref[step] = _mm_tn(do_ref[pl.ds(pl.multiple_of(r0 - CTX_LEN, CHUNK), CHUNK), :], q_t)
                else:
                    dst_ref[step] = jnp.zeros((HEAD_DIM, HEAD_DIM), F32)
                return carry

            lax.fori_loop(0, N_CTX_CHUNKS, functools.partial(local_step, latent=False), 0, unroll=2)
            lax.fori_loop(0, N_LAT_CHUNKS, functools.partial(local_step, latent=True), 0, unroll=4)

            def scan_step(i, carry):
                st, d_st = carry
                j = N_CHUNKS - 1 - i
                update, d_update = st_ref[i], dst_ref[j]
                st_ref[i] = st
                dst_ref[j] = d_st
                return st * decay_ref[i] + update, d_st * decay_ref[j] + d_update

            zero_state = jnp.zeros((HEAD_DIM, HEAD_DIM), F32)
            lax.fori_loop(0, N_CHUNKS, scan_step, (zero_state, zero_state))

            def grad_step(t, dlb, latent):
                r0 = _chunk_rows(t, rev, latent)
                rows = pl.ds(r0, CHUNK)
                step = t + (N_CTX_CHUNKS if latent else 0)
                sg, f, k, lf = _gates(f_ref[rows, :], lb)
                v = v_ref[rows, :]
                b = _mm_exact_lhs(tri, lf)
                bl = b[last:last + 1, :]
                e_end = jnp.exp(bl - b)
                k_end = k * e_end
                decay = jnp.exp(bl)
                d_st = dst_ref[step]
                st_prev = st_ref[step]
                d_kend = _mm(v, d_st)
                d_decay = jnp.sum(d_st * st_prev, axis=0, keepdims=True)
                t_kend = d_kend * k_end
                d_v = _mm_nt(k_end, d_st)
                d_k = d_kend * e_end
                d_b = -t_kend
                if latent:
                    qpre = q_ref[rows, :]
                    q = _silu(qpre) * q_scale
                    bm = b[mid:mid + 1, :]
                    e_b, e_qm, e_km = jnp.exp(b), jnp.exp(b - bm), jnp.exp(bm - b)
                    q_t, q_m, k_m = q * e_b, q * e_qm, k * e_km
                    a = jnp.where(keep, _mm_nt(q_m, k_m), 0.0)
                    d_out = do_ref[pl.ds(pl.multiple_of(r0 - CTX_LEN, CHUNK), CHUNK), :]
                    d_a = jnp.where(keep, _mm_nt(d_out, v), 0.0)
                    d_qm = _mm(d_a, k_m)
                    d_km = _mm_tn(d_a, q_m)
                    d_qt = _mm(d_out, st_prev)
                    d_v = d_v + _mm_tn(a, d_out)
                    d_k = d_k + d_km * e_km
                    d_b = d_b + d_qt * q_t + d_qm * q_m - d_km * k_m
                    d_q = d_qt * e_b + d_qm * e_qm
                at_last = jnp.sum(t_kend, axis=0, keepdims=True) + d_decay * decay
                d_b = d_b + jnp.where(is_row == last, at_last, 0.0)
                d_lf = _mm_exact_lhs(tri_t, d_b)
                d_f = d_lf / f - d_k
                dg_ref[d, rows, :] = d_f * (1.0 - lb) * sg * (1.0 - sg)
                if rev:
                    dg_ref[2, rows, :] += d_v
                else:
                    dg_ref[2, rows, :] = d_v
                if latent:
                    d_qpre = d_q * q_scale * _dsilu(qpre)
                    if rev:
                        dg_ref[3, rows, :] += d_qpre
                    else:
                        dg_ref[3, rows, :] = d_qpre
                return dlb + jnp.sum(d_f * (1.0 - sg), axis=0, keepdims=True)

            dlb = lax.fori_loop(0, N_LAT_CHUNKS, functools.partial(grad_step, latent=True), jnp.zeros((1, HEAD_DIM), F32),
                                unroll=2)
            dlb = lax.fori_loop(0, N_CTX_CHUNKS, functools.partial(grad_step, latent=False), dlb, unroll=2)
            dlg_ref[d:d + 1, :] = dlb * lb * (1.0 - lb)

    col = pl.BlockSpec((SEQ, HEAD_DIM), lambda h: (0, h))
    return pl.pallas_call(
        body, name="gla_backward", grid=(HEADS,),
        out_shape=[jax.ShapeDtypeStruct((HGRN_SECTIONS, ROWS_ALL, D_MODEL), F32), jax.ShapeDtypeStruct((2, D_MODEL), F32)],
        in_specs=[G_SPEC(0), G_SPEC(1), G_SPEC(2), G_SPEC(3), pl.BlockSpec((2, HEAD_DIM), lambda h: (0, h)), col, col],
        out_specs=[pl.BlockSpec((HGRN_SECTIONS, ROWS_ALL, HEAD_DIM), lambda h: (0, 0, h)),
                   pl.BlockSpec((2, HEAD_DIM), lambda h: (0, h))],
        scratch_shapes=[pltpu.VMEM((N_CHUNKS, HEAD_DIM, HEAD_DIM), F32), pltpu.VMEM((N_CHUNKS, HEAD_DIM, HEAD_DIM), F32),
                        pltpu.VMEM((N_CHUNKS, 1, HEAD_DIM), F32)],
        compiler_params=_params(48),
    )(g5, g5, g5, g5, lb_logits, d_o, d_z)


GROUP = 2 * CHUNK


def _group_masks():
    rid, cid = np.arange(GROUP)[:, None], np.arange(GROUP)[None, :]
    same = (rid >= CHUNK) == (cid >= CHUNK)
    sums, back = [], []
    for rev in (False, True):
        causal = (cid >= rid) if rev else (cid <= rid)
        anti = (cid <= rid) if rev else (cid >= rid)
        sums.append(same & causal)
        back.append(np.concatenate([same & anti, same & ~anti], axis=1))
    sums, back = np.stack(sums).astype(np.float32), np.stack(back).astype(np.float32)
    return jnp.asarray(sums, BF16), jnp.asarray(back, BF16), jnp.asarray(sums, F32)


MASK_SPECS = [pl.BlockSpec((2, GROUP, GROUP), lambda h: (0, 0, 0)), pl.BlockSpec((2, GROUP, 2 * GROUP), lambda h: (0, 0, 0)),
              pl.BlockSpec((2, GROUP, GROUP), lambda h: (0, 0, 0))]


def _group_sum(m_bf, a):
    hi, lo = _split2(a)
    r = jnp.dot(m_bf, jnp.concatenate([hi, lo], axis=1), preferred_element_type=F32)
    return r[:, :HEAD_DIM] + r[:, HEAD_DIM:]


def _chunk_row(a, pos):
    return jnp.concatenate([jnp.broadcast_to(a[c * CHUNK + pos:c * CHUNK + pos + 1, :], (CHUNK, HEAD_DIM)) for c in range(2)], axis=0)


def _by_chunk(a, second):
    return jnp.concatenate([jnp.where(second, 0.0, a), jnp.where(second, a, 0.0)], axis=1)


def _own_block(r):
    return jnp.concatenate([r[0:CHUNK, 0:HEAD_DIM], r[CHUNK:GROUP, HEAD_DIM:2 * HEAD_DIM]], axis=0)


def _scan_step_of(row_chunk, rev, latent):
    if not rev:
        return row_chunk
    return (N_CHUNKS + N_CTX_CHUNKS - 1 - row_chunk) if latent else (N_CTX_CHUNKS - 1 - row_chunk)


def _group_rows(i, j, per_step, latent):
    base = CTX_LEN if latent else 0
    return pl.multiple_of(base + (i * per_step + j) * GROUP, GROUP)


GROUPS_PER_STEP = 8
GROUPS_PER_GRAD_STEP = 4


def _gla_forward(g5, lb_logits, late_shards):
    q_scale = HEAD_DIM ** -0.5
    per_lat, per_ctx = GROUPS_PER_STEP, min(GROUPS_PER_STEP, CTX_LEN // GROUP)
    n_late = len(GATHER_LATE)

    def body(ff_ref, fb_ref, v_ref, q_ref, lg_ref, sums_ref, _, keep_ref, *rest):
        shard_refs, o_ref, full_refs = rest[:n_late], rest[n_late], rest[n_late + 1:2 * n_late + 1]
        st_ref, decay_ref, qt_ref = rest[2 * n_late + 1:2 * n_late + 4]
        land_refs = rest[2 * n_late + 4:3 * n_late + 4]
        start_gather, forward_gather, finish_gather = _gather_two_level(GATHER_LATE, shard_refs, land_refs, full_refs,
                                                                        *rest[3 * n_late + 4:])

        @pl.when(pl.program_id(0) == 0)
        def _():
            start_gather()

        @pl.when(pl.program_id(0) == HEADS - 1)
        def _():
            forward_gather()

        second = lax.broadcasted_iota(jnp.int32, (GROUP, HEAD_DIM), 0) >= CHUNK
        for rev in (False, True):
            f_ref = fb_ref if rev else ff_ref
            lb = _sig(lg_ref[1:2, :] if rev else lg_ref[0:1, :])
            d = 1 if rev else 0
            last = 0 if rev else CHUNK - 1
            mid = CHUNK // 2 if rev else CHUNK // 2 - 1

            def local_step(i, carry, latent, per):
                r0s = [_group_rows(i, j, per, latent) for j in range(per)]
                rows = [pl.ds(r0, GROUP) for r0 in r0s]
                gates = [_gates(f_ref[r, :], lb) for r in rows]
                vs = [v_ref[r, :] for r in rows]
                bs = [_group_sum(sums_ref[d], g[3]) for g in gates]
                bls = [_chunk_row(b, last) for b in bs]
                ups = [_mm_tn(v, _by_chunk(g[2] * jnp.exp(bl - b), second)) for v, g, b, bl in zip(vs, gates, bs, bls)]
                if latent:
                    qs = [_silu(q_ref[r, :]) * q_scale for r in rows]
                    bms = [_chunk_row(b, mid) for b in bs]
                    a_s = [_mm_nt(q * jnp.exp(b - bm), g[2] * jnp.exp(bm - b)) for q, g, b, bm in zip(qs, gates, bs, bms)]
                    outs = [_mm(a * keep_ref[d], v) for a, v in zip(a_s, vs)]
                for j in range(per):
                    for c in range(2):
                        step = _scan_step_of(r0s[j] // CHUNK + c, rev, latent)
                        decay_ref[step] = jnp.exp(bls[j][c * CHUNK:c * CHUNK + 1, :])
                        st_ref[step] = ups[j][:, c * HEAD_DIM:(c + 1) * HEAD_DIM]
                    if latent:
                        orow = pl.ds(pl.multiple_of(r0s[j] - CTX_LEN, GROUP), GROUP)
                        qt_ref[orow, :] = (qs[j] * jnp.exp(bs[j])).astype(BF16)
                        if rev:
                            o_ref[orow, :] += outs[j]
                        else:
                            o_ref[orow, :] = outs[j]
                return carry

            lax.fori_loop(0, CTX_LEN // (per_ctx * GROUP), functools.partial(local_step, latent=False, per=per_ctx), 0)
            lax.fori_loop(0, SEQ // (per_lat * GROUP), functools.partial(local_step, latent=True, per=per_lat), 0)

            def scan_step(t, st):
                update = st_ref[t]
                st_ref[t] = st
                return st * decay_ref[t] + update

            lax.fori_loop(0, N_CHUNKS, scan_step, jnp.zeros((HEAD_DIM, HEAD_DIM), F32), unroll=2)

            def inter_step(i, carry):
                r0s = [_group_rows(i, j, per_lat, True) for j in range(per_lat)]
                orows = [pl.ds(pl.multiple_of(r0 - CTX_LEN, GROUP), GROUP) for r0 in r0s]
                states = [jnp.concatenate([st_ref[_scan_step_of(r0 // CHUNK + c, rev, True)] for c in range(2)], axis=0)
                          for r0 in r0s]
                prods = [lax.dot_general(qt_ref[orow, :], s.astype(BF16), (((1,), (1,)), ((), ())), preferred_element_type=F32)
                         for orow, s in zip(orows, states)]
                for orow, r in zip(orows, prods):
                    o_ref[orow, :] += _own_block(r)
                return carry

            lax.fori_loop(0, SEQ // (per_lat * GROUP), inter_step, 0)

        @pl.when(pl.program_id(0) == HEADS - 1)
        def _():
            finish_gather()

    outs = pl.pallas_call(
        body, name="gla_forward", grid=(HEADS,),
        out_shape=[pltpu.HBM((SEQ, D_MODEL), F32)] + [jax.ShapeDtypeStruct(W_SPECS[k][0], BF16) for k in GATHER_LATE],
        in_specs=[G_SPEC(0), G_SPEC(1), G_SPEC(2), G_SPEC(3), pl.BlockSpec((2, HEAD_DIM), lambda h: (0, h))] + MASK_SPECS
        + [ANY] * n_late,
        out_specs=[pl.BlockSpec((SEQ, HEAD_DIM), lambda h: (0, h))] + [ANY] * n_late,
        scratch_shapes=[pltpu.VMEM((N_CHUNKS, HEAD_DIM, HEAD_DIM), F32), pltpu.VMEM((N_CHUNKS, 1, HEAD_DIM), F32),
                        pltpu.VMEM((SEQ, HEAD_DIM), BF16)] + [pltpu.VMEM(W_SPECS[k][0], BF16) for k in GATHER_LATE]
        + [pltpu.SemaphoreType.DMA((2 * n_late,)), pltpu.SemaphoreType.DMA((2 * n_late,)), pltpu.SemaphoreType.DMA((n_late,))],
        compiler_params=_params(48),
    )(*_pin(g5, g5, g5, g5, lb_logits), *_group_masks(), *late_shards)
    return outs[0], dict(zip(GATHER_LATE, outs[1:]))


def _gla_backward(g5, lb_logits, d_o, d_z, late_parts):
    q_scale = HEAD_DIM ** -0.5
    per_lat, per_ctx = GROUPS_PER_STEP, min(GROUPS_PER_STEP, CTX_LEN // GROUP)
    n_late = len(GATHER_LATE)

    def body(ff_ref, fb_ref, v_ref, q_ref, lg_ref, do_ref, dz_ref, sums_ref, back_ref, keep_ref, *rest):
        part_refs, (dg_ref, dlg_ref), slot_refs = rest[:n_late], rest[n_late:n_late + 2], rest[n_late + 2:2 * n_late + 2]
        st_ref, dst_ref, decay_ref, both_ref = rest[2 * n_late + 2:2 * n_late + 6]
        start_scatter, wait_scatter = _scatter_direct(GATHER_LATE, part_refs, slot_refs, *rest[2 * n_late + 6:])

        @pl.when(pl.program_id(0) == 0)
        def _():
            start_scatter()

        dg_ref[3, 0:CTX_LEN, :] = jnp.zeros((CTX_LEN, HEAD_DIM), BF16)
        dg_ref[4, 0:CTX_LEN, :] = jnp.zeros((CTX_LEN, HEAD_DIM), BF16)
        dg_ref[4, CTX_LEN:ROWS_ALL, :] = dz_ref[...]
        second = lax.broadcasted_iota(jnp.int32, (GROUP, HEAD_DIM), 0) >= CHUNK
        for rev in (False, True):
            d = 1 if rev else 0
            f_ref = fb_ref if rev else ff_ref
            lb = _sig(lg_ref[d:d + 1, :])
            last = 0 if rev else CHUNK - 1
            mid = CHUNK // 2 if rev else CHUNK // 2 - 1

            def local_step(i, carry, latent, per):
                r0s = [_group_rows(i, j, per, latent) for j in range(per)]
                rows = [pl.ds(r0, GROUP) for r0 in r0s]
                gates = [_gates(f_ref[r, :], lb) for r in rows]
                bs = [_group_sum(sums_ref[d], g[3]) for g in gates]
                bls = [_chunk_row(b, last) for b in bs]
                ups = [_mm_tn(v_ref[r, :], _by_chunk(g[2] * jnp.exp(bl - b), second)) for r, g, b, bl in zip(rows, gates, bs, bls)]
                if latent:
                    orows = [pl.ds(pl.multiple_of(r0 - CTX_LEN, GROUP), GROUP) for r0 in r0s]
                    d_ups = [_mm_tn(do_ref[orow, :], _by_chunk(_silu(q_ref[r, :]) * q_scale * jnp.exp(b), second))
                             for orow, r, b in zip(orows, rows, bs)]
                for j in range(per):
                    for c in range(2):
                        step = _scan_step_of(r0s[j] // CHUNK + c, rev, latent)
                        decay_ref[step] = jnp.exp(bls[j][c * CHUNK:c * CHUNK + 1, :])
                        st_ref[step] = ups[j][:, c * HEAD_DIM:(c + 1) * HEAD_DIM]
                        if latent:
                            dst_ref[step] = d_ups[j][:, c * HEAD_DIM:(c + 1) * HEAD_DIM]
                        else:
                            dst_ref[step] = jnp.zeros((HEAD_DIM, HEAD_DIM), F32)
                return carry

            lax.fori_loop(0, CTX_LEN // (per_ctx * GROUP), functools.partial(local_step, latent=False, per=per_ctx), 0)
            lax.fori_loop(0, SEQ // (per_lat * GROUP), functools.partial(local_step, latent=True, per=per_lat), 0)

            def scan_step(i, carry):
                st, d_st = carry
                j = N_CHUNKS - 1 - i
                update, d_update = st_ref[i], dst_ref[j]
                st_ref[i] = st
                dst_ref[j] = d_st
                return st * decay_ref[i] + update, d_st * decay_ref[j] + d_update

            zero_state = jnp.zeros((HEAD_DIM, HEAD_DIM), F32)
            lax.fori_loop(0, N_CHUNKS, scan_step, (zero_state, zero_state))

            def grad_step(i, dlb, latent, per):
                r0s = [_group_rows(i, j, per, latent) for j in range(per)]
                rows = [pl.ds(r0, GROUP) for r0 in r0s]
                gates = [_gates(f_ref[r, :], lb) for r in rows]
                vs = [v_ref[r, :] for r in rows]
                bs = [_group_sum(sums_ref[d], g[3]) for g in gates]
                bls = [_chunk_row(b, last) for b in bs]
                e_ends = [jnp.exp(bl - b) for b, bl in zip(bs, bls)]
                k_ends = [g[2] * e for g, e in zip(gates, e_ends)]
                sts = [[st_ref[_scan_step_of(r0 // CHUNK + c, rev, latent)] for c in range(2)] for r0 in r0s]
                d_sts = [[dst_ref[_scan_step_of(r0 // CHUNK + c, rev, latent)] for c in range(2)] for r0 in r0s]
                d_kends = [_own_block(_mm(v, jnp.concatenate(ds, axis=1))) for v, ds in zip(vs, d_sts)]
                d_vs = [_own_block(_mm_nt(ke, jnp.concatenate(ds, axis=0))) for ke, ds in zip(k_ends, d_sts)]
                at_last = [jnp.concatenate([jnp.broadcast_to(jnp.sum(ds[c] * s[c], axis=0, keepdims=True), (CHUNK, HEAD_DIM))
                                            for c in range(2)], axis=0) * jnp.exp(bl) for ds, s, bl in zip(d_sts, sts, bls)]
                t_kends = [dk * ke for dk, ke in zip(d_kends, k_ends)]
                d_ks = [dk * e for dk, e in zip(d_kends, e_ends)]
                if latent:
                    orows = [pl.ds(pl.multiple_of(r0 - CTX_LEN, GROUP), GROUP) for r0 in r0s]
                    qpres = [q_ref[r, :] for r in rows]
                    q_sigs = [_sig(qp) for qp in qpres]
                    qs = [qp * sg * q_scale for qp, sg in zip(qpres, q_sigs)]
                    bms = [_chunk_row(b, mid) for b in bs]
                    e_bs = [jnp.exp(b) for b in bs]
                    e_qms = [jnp.exp(b - bm) for b, bm in zip(bs, bms)]
                    e_kms = [jnp.exp(bm - b) for b, bm in zip(bs, bms)]
                    q_ts = [q * e for q, e in zip(qs, e_bs)]
                    q_ms = [q * e for q, e in zip(qs, e_qms)]
                    k_ms = [g[2] * e for g, e in zip(gates, e_kms)]
                    d_outs = [do_ref[orow, :] for orow in orows]
                    a_s = [_mm_nt(qm, km) * keep_ref[d] for qm, km in zip(q_ms, k_ms)]
                    d_as = [_mm_nt(do, v) * keep_ref[d] for do, v in zip(d_outs, vs)]
                    d_qts = [_own_block(_mm(do, jnp.concatenate(s, axis=1))) for do, s in zip(d_outs, sts)]
                    d_qms = [_mm(da, km) for da, km in zip(d_as, k_ms)]
                    d_kms = [_mm_tn(da, qm) for da, qm in zip(d_as, q_ms)]
                    d_vs = [dv + _mm_tn(a, do) for dv, a, do in zip(d_vs, a_s, d_outs)]
                    d_ks = [dk + dkm * e for dk, dkm, e in zip(d_ks, d_kms, e_kms)]
                    d_bs = [jnp.concatenate([dqt * qt + dqm * qm - dkm * km, t], axis=0)
                            for dqt, qt, dqm, qm, dkm, km, t in zip(d_qts, q_ts, d_qms, q_ms, d_kms, k_ms, t_kends)]
                    d_qs = [dqt * eb + dqm * eq for dqt, eb, dqm, eq in zip(d_qts, e_bs, d_qms, e_qms)]
                    back = back_ref[d]
                else:
                    d_bs, back = t_kends, back_ref[d, :, GROUP:2 * GROUP]
                d_lfs = [_group_sum(back, db) + al for db, al in zip(d_bs, at_last)]
                for j in range(per):
                    sg, f = gates[j][0], gates[j][1]
                    d_f = d_lfs[j] / f - d_ks[j]
                    dg_ref[d, rows[j], :] = (d_f * (1.0 - lb) * sg * (1.0 - sg)).astype(BF16)
                    dlb = dlb + jnp.sum(d_f * (1.0 - sg), axis=0, keepdims=True)
                    if rev:
                        dg_ref[2, rows[j], :] = (both_ref[0, rows[j], :] + d_vs[j]).astype(BF16)
                    else:
                        both_ref[0, rows[j], :] = d_vs[j]
                    if latent:
                        d_qpre = d_qs[j] * q_scale * (q_sigs[j] * (1.0 + qpres[j] * (1.0 - q_sigs[j])))
                        if rev:
                            dg_ref[3, rows[j], :] = (both_ref[1, rows[j], :] + d_qpre).astype(BF16)
                        else:
                            both_ref[1, rows[j], :] = d_qpre
                return dlb

            dlb = lax.fori_loop(0, SEQ // (GROUPS_PER_GRAD_STEP * GROUP),
                                functools.partial(grad_step, latent=True, per=GROUPS_PER_GRAD_STEP), jnp.zeros((1, HEAD_DIM), F32))
            dlb = lax.fori_loop(0, CTX_LEN // (per_ctx * GROUP), functools.partial(grad_step, latent=False, per=per_ctx), dlb)
            dlg_ref[d:d + 1, :] = dlb * lb * (1.0 - lb)

        @pl.when(pl.program_id(0) == HEADS - 1)
        def _():
            wait_scatter()

    col = pl.BlockSpec((SEQ, HEAD_DIM), lambda h: (0, h))
    outs = pl.pallas_call(
        body, name="gla_backward", grid=(HEADS,),
        out_shape=[pltpu.HBM((HGRN_SECTIONS, ROWS_ALL, D_MODEL), BF16), jax.ShapeDtypeStruct((2, D_MODEL), F32)]
        + _slot_shapes(GATHER_LATE),
        in_specs=[G_SPEC(0), G_SPEC(1), G_SPEC(2), G_SPEC(3), pl.BlockSpec((2, HEAD_DIM), lambda h: (0, h)), col, col]
        + MASK_SPECS + [ANY] * n_late,
        out_specs=[pl.BlockSpec((HGRN_SECTIONS, ROWS_ALL, HEAD_DIM), lambda h: (0, 0, h)),
                   pl.BlockSpec((2, HEAD_DIM), lambda h: (0, h))] + [ANY] * n_late,
        scratch_shapes=[pltpu.VMEM((N_CHUNKS, HEAD_DIM, HEAD_DIM), F32), pltpu.VMEM((N_CHUNKS, HEAD_DIM, HEAD_DIM), F32),
                        pltpu.VMEM((N_CHUNKS, 1, HEAD_DIM), F32), pltpu.VMEM((2, ROWS_ALL, HEAD_DIM), F32)]
        + _comm_sems(n_late),
        compiler_params=_params(48),
    )(*_pin(g5, g5, g5, g5, lb_logits, d_o, d_z), *_group_masks(), *late_parts)
    return outs[0], outs[1], dict(zip(GATHER_LATE, outs[2:]))


def _head_norm(o, gw, scr):
    rs = []
    for h in range(HEADS):
        cols = slice(h * HEAD_DIM, (h + 1) * HEAD_DIM)
        oh = o[:, cols]
        r = lax.rsqrt(jnp.mean(oh * oh, axis=-1, keepdims=True) + EPS)
        scr[:, cols] = oh * r
        rs.append(r)
    return rs


def _hgrn_out_forward(o_raw, g5, xin, gnorm_w, gate, w_out):
    tm = ROW_TILE

    def body(o_ref, z_ref, x_ref, gw_ref, gate_ref, w_ref, x1_ref, res_ref, scr):
        _head_norm(o_ref[...], None, scr)
        a = scr[...] * gw_ref[...] * _silu(z_ref[...])
        res = _mm(a, w_ref[...])
        res_ref[...] = res
        x1_ref[...] = x_ref[...] + gate_ref[...] * res

    tile = pl.BlockSpec((tm, D_MODEL), lambda i: (i, 0))
    vec = pl.BlockSpec((1, D_MODEL), lambda i: (0, 0))
    return pl.pallas_call(
        body, name="hgrn_out_forward", grid=(SEQ // tm,),
        out_shape=[pltpu.HBM((SEQ, D_MODEL), F32)] * 2,
        in_specs=[tile, pl.BlockSpec((None, tm, D_MODEL), lambda i: (4, i + CTX_LEN // tm, 0)), tile, vec, vec,
                  pl.BlockSpec((D_MODEL, D_MODEL), lambda i: (0, 0))],
        out_specs=[tile, tile],
        scratch_shapes=[pltpu.VMEM((tm, D_MODEL), F32)],
        compiler_params=_params(32),
    )(*_pin(o_raw, g5, xin, gnorm_w, gate, w_out))


def _hgrn_out_backward(d_x1, o_raw, g5, res, gnorm_w, gate, w_out):
    tm = ROW_TILE

    def body(dx_ref, o_ref, z_ref, res_ref, gw_ref, gate_ref, w_ref, do_ref, dz_ref, dw_out, dgate_ref, dgw_ref, scr, scr2,
             dw_ref):
        @pl.when(pl.program_id(0) == 0)
        def _():
            dw_ref[...] = jnp.zeros_like(dw_ref)
            dgate_ref[...] = jnp.zeros_like(dgate_ref)
            dgw_ref[...] = jnp.zeros_like(dgw_ref)

        dx = dx_ref[...]
        dgate_ref[...] += jnp.sum(dx * res_ref[...], axis=0, keepdims=True)
        d_res = (dx * gate_ref[...]).astype(BF16)
        d_a = _mm_nt(d_res, w_ref[...])
        rs = _head_norm(o_ref[...], None, scr)
        z = z_ref[...]
        sz = _silu(z)
        o_hat = scr[...]
        o_n = o_hat * gw_ref[...]
        dw_ref[...] += _mm_tn(o_n * sz, d_res)
        d_on = d_a * sz
        dz_ref[...] = (d_a * o_n * _dsilu(z)).astype(BF16)
        dgw_ref[...] += jnp.sum(d_on * o_hat, axis=0, keepdims=True)
        scr2[...] = d_on * gw_ref[...]
        for h in range(HEADS):
            cols = slice(h * HEAD_DIM, (h + 1) * HEAD_DIM)
            dh, oh = scr2[:, cols], scr[:, cols]
            do_ref[:, cols] = (rs[h] * (dh - oh * jnp.mean(dh * oh, axis=-1, keepdims=True))).astype(BF16)

        @pl.when(pl.program_id(0) == SEQ // tm - 1)
        def _():
            dw_out[...] = dw_ref[...].astype(BF16)

    tile = pl.BlockSpec((tm, D_MODEL), lambda i: (i, 0))
    vec = pl.BlockSpec((1, D_MODEL), lambda i: (0, 0))
    mat = pl.BlockSpec((D_MODEL, D_MODEL), lambda i: (0, 0))
    return pl.pallas_call(
        body, name="hgrn_out_backward", grid=(SEQ // tm,),
        out_shape=[pltpu.HBM((SEQ, D_MODEL), BF16)] * 2 + [pltpu.HBM((D_MODEL, D_MODEL), BF16)]
        + [jax.ShapeDtypeStruct((1, D_MODEL), F32)] * 2,
        in_specs=[tile, tile, pl.BlockSpec((None, tm, D_MODEL), lambda i: (4, i + CTX_LEN // tm, 0)), tile, vec, vec, mat],
        out_specs=[tile, tile, mat, vec, vec],
        scratch_shapes=[pltpu.VMEM((tm, D_MODEL), F32)] * 2 + [pltpu.VMEM((D_MODEL, D_MODEL), F32)],
        compiler_params=_params(40),
    )(*_pin(d_x1, o_raw, g5, res, gnorm_w, gate, w_out))


def _pool_constants():
    win = np.zeros((POOL_GROUPS, ROW_TILE, ROW_TILE), np.float32)
    inv = np.zeros((POOL_GROUPS, ROW_TILE, 1), np.float32)
    for g, w in enumerate(POOL_WINDOWS):
        for t in range(ROW_TILE):
            base, p = (t // GRID_W) * GRID_W, t % GRID_W
            lo = min(max(p - w // 2, 0), GRID_W)
            hi = min(max(p - w // 2 + w, 0), GRID_W)
            win[g, t, base + lo:base + hi] = 1.0
            inv[g, t, 0] = 1.0 / np.float32(hi - lo)
    return jnp.asarray(win, BF16), jnp.asarray(win.transpose(0, 2, 1), BF16), jnp.asarray(inv, F32)


def _pool_mix(u_ref, wg_ref, ps_ref, win_ref, inv_ref, pooled_scr, yg_scr):
    for g in range(POOL_GROUPS):
        cols = slice(g * POOL_GROUP_DIM, (g + 1) * POOL_GROUP_DIM)
        ug = u_ref[:, cols]
        pooled = _mm_exact_lhs(win_ref[g], ug) * inv_ref[g] - ug
        if pooled_scr is not None:
            pooled_scr[:, cols] = pooled
        yg_scr[:, cols] = _mm(pooled, wg_ref[g])


def _pool_forward_loss(uz, x1, target, gate, w_grp, pool_scale, w_out, final_w):
    tm = ROW_TILE
    win, _, inv = _pool_constants()

    def body(u_ref, z_ref, x_ref, t_ref, gate_ref, wg_ref, ps_ref, w_ref, fw_ref, win_ref, inv_ref,
             dx_ref, loss_ref, dfw_ref, dgate_ref, yg_scr):
        @pl.when(pl.program_id(0) == 0)
        def _():
            loss_ref[...] = jnp.zeros_like(loss_ref)
            dfw_ref[...] = jnp.zeros_like(dfw_ref)
            dgate_ref[...] = jnp.zeros_like(dgate_ref)

        _pool_mix(u_ref, wg_ref, ps_ref, win_ref, inv_ref, None, yg_scr)
        a = yg_scr[...] * ps_ref[...] * _silu(z_ref[...])
        res = _mm(a, w_ref[...])
        x2 = x_ref[...] + gate_ref[...] * res
        r = lax.rsqrt(jnp.mean(x2 * x2, axis=-1, keepdims=True) + EPS)
        xh = x2 * r
        fw = fw_ref[...]
        err = xh * fw - t_ref[...]
        loss_ref[...] += 0.5 * jnp.sum(jnp.mean(err * err, axis=-1, keepdims=True))
        d_y = err * (1.0 / D_MODEL)
        dfw_ref[...] += jnp.sum(d_y * xh, axis=0, keepdims=True)
        d_xh = d_y * fw
        d_x2 = r * (d_xh - xh * jnp.mean(d_xh * xh, axis=-1, keepdims=True))
        dx_ref[...] = d_x2
        dgate_ref[...] += jnp.sum(d_x2 * res, axis=0, keepdims=True)

    tile = pl.BlockSpec((tm, D_MODEL), lambda i: (i, 0))
    vec = pl.BlockSpec((1, D_MODEL), lambda i: (0, 0))
    grp = pl.BlockSpec((POOL_GROUPS, POOL_GROUP_DIM, POOL_GROUP_DIM), lambda i: (0, 0, 0))
    return pl.pallas_call(
        body, name="pool_forward_loss", grid=(SEQ // tm,),
        out_shape=[pltpu.HBM((SEQ, D_MODEL), F32), jax.ShapeDtypeStruct((8, 128), F32),
                   jax.ShapeDtypeStruct((1, D_MODEL), F32), jax.ShapeDtypeStruct((1, D_MODEL), F32)],
        in_specs=[pl.BlockSpec((None, tm, D_MODEL), lambda i: (0, i, 0)), pl.BlockSpec((None, tm, D_MODEL), lambda i: (1, i, 0)),
                  tile, tile, vec, grp, vec, pl.BlockSpec((D_MODEL, D_MODEL), lambda i: (0, 0)), vec, grp,
                  pl.BlockSpec((POOL_GROUPS, ROW_TILE, 1), lambda i: (0, 0, 0))],
        out_specs=[tile, pl.BlockSpec((8, 128), lambda i: (0, 0)), vec, vec],
        scratch_shapes=[pltpu.VMEM((tm, D_MODEL), F32)],
        compiler_params=_params(32),
    )(*_pin(uz, uz, x1, target, gate, w_grp, pool_scale, w_out, final_w, win, inv))


def _pool_backward(d_x2, uz, gate, w_grp, pool_scale, w_out):
    tm = ROW_TILE
    win, win_t, inv = _pool_constants()

    def body(dx_ref, u_ref, z_ref, gate_ref, wg_ref, ps_ref, w_ref, win_ref, wint_ref, inv_ref,
             duz_ref, dw_out, dwg_out, dps_ref, pooled_scr, yg_scr, dyg_scr, dw_ref, dwg_ref):
        @pl.when(pl.program_id(0) == 0)
        def _():
            dw_ref[...] = jnp.zeros_like(dw_ref)
            dwg_ref[...] = jnp.zeros_like(dwg_ref)
            dps_ref[...] = jnp.zeros_like(dps_ref)

        _pool_mix(u_ref, wg_ref, ps_ref, win_ref, inv_ref, pooled_scr, yg_scr)
        z = z_ref[...]
        sz = _silu(z)
        yg = yg_scr[...]
        y = yg * ps_ref[...]
        d_res = (dx_ref[...] * gate_ref[...]).astype(BF16)
        d_a = _mm_nt(d_res, w_ref[...])
        dw_ref[...] += _mm_tn(y * sz, d_res)
        d_y = d_a * sz
        duz_ref[1] = (d_a * y * _dsilu(z)).astype(BF16)
        dps_ref[...] += jnp.sum(d_y * yg, axis=0, keepdims=True)
        dyg_scr[...] = d_y * ps_ref[...]
        for g in range(POOL_GROUPS):
            cols = slice(g * POOL_GROUP_DIM, (g + 1) * POOL_GROUP_DIM)
            d_yg = dyg_scr[:, cols].astype(BF16)
            d_pool = _mm_nt(d_yg, wg_ref[g])
            dwg_ref[g] += _mm_tn(pooled_scr[:, cols], d_yg)
            duz_ref[0, :, cols] = (_mm_exact_lhs(wint_ref[g], d_pool * inv_ref[g]) - d_pool).astype(BF16)

        @pl.when(pl.program_id(0) == SEQ // tm - 1)
        def _():
            dw_out[...] = dw_ref[...].astype(BF16)
            dwg_out[...] = dwg_ref[...].astype(BF16)

    tile = pl.BlockSpec((tm, D_MODEL), lambda i: (i, 0))
    vec = pl.BlockSpec((1, D_MODEL), lambda i: (0, 0))
    mat = pl.BlockSpec((D_MODEL, D_MODEL), lambda i: (0, 0))
    grp = pl.BlockSpec((POOL_GROUPS, POOL_GROUP_DIM, POOL_GROUP_DIM), lambda i: (0, 0, 0))
    return pl.pallas_call(
        body, name="pool_backward", grid=(SEQ // tm,),
        out_shape=[pltpu.HBM((POOL_SECTIONS, SEQ, D_MODEL), BF16), pltpu.HBM((D_MODEL, D_MODEL), BF16),
                   pltpu.HBM((POOL_GROUPS, POOL_GROUP_DIM, POOL_GROUP_DIM), BF16), jax.ShapeDtypeStruct((1, D_MODEL), F32)],
        in_specs=[tile, pl.BlockSpec((None, tm, D_MODEL), lambda i: (0, i, 0)), pl.BlockSpec((None, tm, D_MODEL), lambda i: (1, i, 0)),
                  vec, grp, vec, mat, grp, grp, pl.BlockSpec((POOL_GROUPS, ROW_TILE, 1), lambda i: (0, 0, 0))],
        out_specs=[pl.BlockSpec((POOL_SECTIONS, tm, D_MODEL), lambda i: (0, i, 0)), mat, grp, vec],
        scratch_shapes=[pltpu.VMEM((tm, D_MODEL), F32)] * 3 + [pltpu.VMEM((D_MODEL, D_MODEL), F32),
                                                               pltpu.VMEM((POOL_GROUPS, POOL_GROUP_DIM, POOL_GROUP_DIM), F32)],
        compiler_params=_params(40),
    )(*_pin(d_x2, uz, uz, gate, w_grp, pool_scale, w_out, win, win_t, inv))


def _ln_mod_backward(d_g, w, ctx_tile, xin, nw, scale, d_up, name):
    n_sec, rows, _ = d_g.shape
    n_mod = scale.shape[0]
    skip = n_mod - 1
    tm = ROW_TILE if skip else 2 * ROW_TILE
    n_tiles = rows // tm

    def body(dg_ref, w_ref, *refs):
        c_ref = refs[0] if skip else None
        x_ref, nw_ref, sc_ref, up_ref, dx_ref, dnw_ref, dmod_ref = refs[skip:]
        i = pl.program_id(0)

        @pl.when(i == 0)
        def _():
            dnw_ref[...] = jnp.zeros_like(dnw_ref)

        @pl.when((i == 0) | (i == skip))
        def _():
            dmod_ref[...] = jnp.zeros_like(dmod_ref)

        d_h = _mm_nt(dg_ref[0], w_ref[:, 0:D_MODEL])
        for k in range(1, n_sec):
            d_h = d_h + _mm_nt(dg_ref[k], w_ref[:, k * D_MODEL:(k + 1) * D_MODEL])
        xv = jnp.where(i == 0, c_ref[...], x_ref[...]) if skip else x_ref[...]
        r = lax.rsqrt(jnp.mean(xv * xv, axis=-1, keepdims=True) + EPS)
        xh = xv * r
        nw_row = nw_ref[...]
        dmod_ref[0:1, :] += jnp.sum(d_h, axis=0, keepdims=True)
        dmod_ref[1:2, :] += jnp.sum(d_h * (xh * nw_row), axis=0, keepdims=True)
        d_xn = d_h * (1.0 + sc_ref[...])
        dnw_ref[...] += jnp.sum(d_xn * xh, axis=0, keepdims=True)
        d_xh = d_xn * nw_row

        @pl.when(i >= skip)
        def _():
            dx_ref[...] = up_ref[...] + r * (d_xh - xh * jnp.mean(d_xh * xh, axis=-1, keepdims=True))

    lat = lambda i: (jnp.maximum(i - skip, 0), 0)
    mod_idx = lambda i: (jnp.minimum(i, n_mod - 1), 0, 0)
    return pl.pallas_call(
        body, name=name, grid=(n_tiles,),
        out_shape=[pltpu.HBM((rows - skip * tm, D_MODEL), F32), jax.ShapeDtypeStruct((1, D_MODEL), F32),
                   jax.ShapeDtypeStruct((n_mod, 8, D_MODEL), F32)],
        in_specs=[pl.BlockSpec((n_sec, tm, D_MODEL), lambda i: (0, i, 0)),
                  pl.BlockSpec((D_MODEL, n_sec * D_MODEL), lambda i: (0, 0))]
        + [pl.BlockSpec((tm, D_MODEL), lambda i: (0, 0))] * skip
        + [pl.BlockSpec((tm, D_MODEL), lat),
           pl.BlockSpec((1, D_MODEL), lambda i: (0, 0)),
           pl.BlockSpec((None, 1, D_MODEL), mod_idx),
           pl.BlockSpec((tm, D_MODEL), lat)],
        out_specs=[pl.BlockSpec((tm, D_MODEL), lat), pl.BlockSpec((1, D_MODEL), lambda i: (0, 0)),
                   pl.BlockSpec((None, 8, D_MODEL), mod_idx)],
        compiler_params=_params(48),
    )(*_pin(d_g, w, *([ctx_tile] * skip), xin, nw, scale.reshape(n_mod, 1, D_MODEL), d_up))


def _weight_grad(h, d_g, name):
    n_sec, rows, _ = d_g.shape
    tm = 768 if rows % 768 == 0 else 512
    n_tiles = rows // tm

    def body(h_ref, dg_ref, dw_ref, acc):
        i = pl.program_id(1)
        prod = _mm_tn(h_ref[...], dg_ref[...])

        @pl.when(i == 0)
        def _():
            acc[...] = prod

        @pl.when((i > 0) & (i < n_tiles - 1))
        def _():
            acc[...] += prod

        @pl.when(i == n_tiles - 1)
        def _():
            dw_ref[...] = (acc[...] + prod).astype(BF16)

    return pl.pallas_call(
        body, name=name, grid=(n_sec, n_tiles),
        out_shape=pltpu.HBM((D_MODEL, n_sec * D_MODEL), BF16),
        in_specs=[pl.BlockSpec((tm, D_MODEL), lambda j, i: (i, 0)), pl.BlockSpec((None, tm, D_MODEL), lambda j, i: (j, i, 0))],
        out_specs=pl.BlockSpec((D_MODEL, D_MODEL), lambda j, i: (0, j)),
        scratch_shapes=[pltpu.VMEM((D_MODEL, D_MODEL), F32)],
        compiler_params=_params(32),
    )(*_pin(h, d_g))


def _weight_grad_paired(h, d_g, name):
    n_sec, rows, _ = d_g.shape
    tm = 768
    n_tiles = rows // tm
    half = D_MODEL // 2
    chip_cols = n_sec * D_MODEL // N_CHIPS

    def body(h_ref, dg_ref, q_ref, slots_ref, acc, keep, send, land, send_sems, recv_sems, own_sem):
        j, i = pl.program_id(0), pl.program_id(1)
        x, y, c = _my_place()
        prod = _mm_tn(h_ref[...], dg_ref[...])

        def to_sibling(k):
            return pltpu.make_async_remote_copy(src_ref=send.at[k], dst_ref=land.at[k], send_sem=send_sems.at[k],
                                                recv_sem=recv_sems.at[k], device_id=(x, y, 1 - c), device_id_type=MESH)

        @pl.when(i == 0)
        def _():
            acc[...] = prod

        @pl.when((i > 0) & (i < n_tiles - 1))
        def _():
            acc[...] += prod

        @pl.when(i == n_tiles - 1)
        def _():
            acc[...] += prod
            keep[j] = acc[pl.ds(_al(c * half, half), half), :]
            send[j] = acc[pl.ds(_al((1 - c) * half, half), half), :].astype(BF16)
            to_sibling(j).start()

        @pl.when((j == n_sec - 1) & (i == n_tiles - 1))
        def _():
            for k in range(n_sec):
                to_sibling(k).wait()
                q_ref[:, k * D_MODEL:(k + 1) * D_MODEL] = (keep[k] + land[k].astype(F32)).astype(BF16)
            chip = 2 * x + y
            own = pltpu.make_async_copy(q_ref.at[:, pl.ds(_al(chip * chip_cols, chip_cols), chip_cols)], slots_ref.at[chip], own_sem)
            own.start()
            own.wait()

    return pl.pallas_call(
        body, name=name, grid=(n_sec, n_tiles),
        out_shape=[jax.ShapeDtypeStruct((half, n_sec * D_MODEL), BF16),
                   jax.ShapeDtypeStruct((N_CHIPS, half, chip_cols), BF16)],
        in_specs=[pl.BlockSpec((tm, D_MODEL), lambda j, i: (i, 0)), pl.BlockSpec((None, tm, D_MODEL), lambda j, i: (j, i, 0))],
        out_specs=[pl.BlockSpec((half, n_sec * D_MODEL), lambda j, i: (0, 0)), ANY],
        scratch_shapes=[pltpu.VMEM((D_MODEL, D_MODEL), F32), pltpu.VMEM((n_sec, half, D_MODEL), F32),
                        pltpu.VMEM((n_sec, half, D_MODEL), BF16), pltpu.VMEM((n_sec, half, D_MODEL), BF16),
                        pltpu.SemaphoreType.DMA((n_sec,)), pltpu.SemaphoreType.DMA((n_sec,)), pltpu.SemaphoreType.DMA],
        compiler_params=_params(56),
    )(*_pin(h, d_g))


def _sum_slots(slots, name):
    n_slots, rows, cols = slots.shape
    tr = 128

    def body(s_ref, o_ref):
        acc = s_ref[0].astype(F32)
        for d in range(1, n_slots):
            acc = acc + s_ref[d].astype(F32)
        o_ref[...] = acc

    return pl.pallas_call(
        body, name=name, grid=(rows // tr,),
        out_shape=jax.ShapeDtypeStruct((rows, cols), F32),
        in_specs=[pl.BlockSpec((n_slots, tr, cols), lambda i: (0, i, 0))],
        out_specs=pl.BlockSpec((tr, cols), lambda i: (i, 0)),
    )(*_pin(slots))


def _adamw_math(w, g, m, v):
    m = ADAM_B1 * m + (1.0 - ADAM_B1) * g
    v = ADAM_B2 * v + (1.0 - ADAM_B2) * (g * g)
    m_hat = m / (1.0 - ADAM_B1 ** ADAM_STEP)
    v_hat = v / (1.0 - ADAM_B2 ** ADAM_STEP)
    return -ADAM_LR * (m_hat / (jnp.sqrt(v_hat) + ADAM_EPS) + ADAM_WD * w), m, v


def _adamw(w, g, m, v, name):
    rows, cols = w.shape
    tr = rows if rows <= 128 else 128

    def body(w_ref, g_ref, m_ref, v_ref, d_ref, mo_ref, vo_ref):
        d_ref[...], mo_ref[...], vo_ref[...] = _adamw_math(w_ref[...], g_ref[...], m_ref[...], v_ref[...])

    tile = pl.BlockSpec((tr, cols), lambda i: (i, 0))
    return pl.pallas_call(
        body, name=name, grid=(rows // tr,),
        out_shape=[pltpu.HBM((rows, cols), F32)] * 3,
        in_specs=[tile] * 4, out_specs=[tile] * 3,
    )(*_pin(w, g, m, v))


def _sum_devices(gathered):
    def body(p_ref, o_ref):
        acc = p_ref[0:SMALL_ROWS, :]
        for d in range(1, N_DEV):
            acc = acc + p_ref[d * SMALL_ROWS:(d + 1) * SMALL_ROWS, :]
        o_ref[...] = acc

    return pl.pallas_call(body, name="sum_devices", out_shape=jax.ShapeDtypeStruct((SMALL_ROWS, D_MODEL), F32),
                          in_specs=[VMEM], out_specs=VMEM)(gathered)


def _small_sums_and_cond_ctx(small, ada_w0, c_ctx, m, v):
    n_cols = ada_w0.shape[1]

    def body(small_ref, w_ref, c_ref, m_ref, v_ref, all_ref, sums_ref, g_ref, d_ref, mo_ref, vo_ref, part_ref, parts_ref,
             s1, r1, l1, s2, r2, l2):
        start_slabs, finish_slabs = _small_gather(small_ref, all_ref, s1, r1, l1)
        start_parts, finish_parts = _small_gather(part_ref, parts_ref, s2, r2, l2)
        start_slabs()
        finish_slabs()
        acc = all_ref[0:SMALL_ROWS, :]
        for dev in range(1, N_DEV):
            acc = acc + all_ref[dev * SMALL_ROWS:(dev + 1) * SMALL_ROWS, :]
        sums_ref[...] = acc
        x, y, _ = _my_place()
        d_modc = jnp.concatenate([acc[6:7, :], acc[7:8, :], acc[8:9, :]], axis=1)
        mine = jnp.zeros((1, n_cols), F32)
        for s in range(N_CHIPS):
            mine = mine + jnp.where(2 * x + y == s, d_modc[:, s * n_cols:(s + 1) * n_cols], 0.0)
        ah, al = _split2(jnp.broadcast_to(mine, (8, n_cols)))
        wh, wl = _split2(w_ref[...])
        nt = lambda a, b: lax.dot_general(a, b, (((1,), (1,)), ((), ())), preferred_element_type=F32)
        part_ref[...] = nt(ah, wh) + nt(al, wh) + nt(ah, wl)
        start_parts()
        finish_parts()
        total = parts_ref[0:1, :]
        for s in range(1, N_CHIPS):
            total = total + parts_ref[16 * s:16 * s + 1, :]
        w = c_ref[...]
        g = total * _dsilu(w)
        g_ref[...] = g
        d_ref[...], mo_ref[...], vo_ref[...] = _adamw_math(w, g, m_ref[...], v_ref[...])

    seven = [pltpu.SemaphoreType.DMA((7,)), pltpu.SemaphoreType.DMA((7,)), pltpu.SemaphoreType.DMA]
    vec = jax.ShapeDtypeStruct((1, D_MODEL), F32)
    return pl.pallas_call(
        body, name="small_sums_and_cond_ctx",
        out_shape=[jax.ShapeDtypeStruct((N_DEV * SMALL_ROWS, D_MODEL), F32), jax.ShapeDtypeStruct((SMALL_ROWS, D_MODEL), F32),
                   vec, vec, vec, vec],
        in_specs=[VMEM] * 5, out_specs=[VMEM] * 6,
        scratch_shapes=[pltpu.VMEM((8, D_MODEL), F32), pltpu.VMEM((N_DEV * 8, D_MODEL), F32)] + seven + seven,
    )(small, ada_w0, c_ctx, m, v)


def _ada_update(cond_t, d_mod, ada_w, m, v):
    n_layers, _, n_cols = ada_w.shape
    tr = ROW_TILE

    def body(c_ref, dm_ref, w_ref, m_ref, v_ref, g_ref, d_ref, mo_ref, vo_ref):
        g = _mm_f32(_silu(c_ref[...]), dm_ref[...])
        g_ref[...] = g
        d_ref[...], mo_ref[...], vo_ref[...] = _adamw_math(w_ref[...], g, m_ref[...], v_ref[...])

    tile = pl.BlockSpec((None, tr, n_cols), lambda l, i: (l, i, 0))
    return pl.pallas_call(
        body, name="ada_update", grid=(n_layers, D_MODEL // tr),
        out_shape=[pltpu.HBM(ada_w.shape, F32)] * 4,
        in_specs=[pl.BlockSpec((tr, 16), lambda l, i: (i, 0)), pl.BlockSpec((None, 16, n_cols), lambda l, i: (l, 0, 0)),
                  tile, tile, tile],
        out_specs=[tile] * 4,
    )(*_pin(cond_t, d_mod, ada_w, m, v))


def _cond_ctx_partial(d_modc, ada_w0):
    n_cols = ada_w0.shape[1]
    tr = ROW_TILE

    def body(dm_ref, w_ref, o_ref):
        o_ref[...] = jnp.sum(w_ref[...] * dm_ref[...], axis=-1, keepdims=True)

    return pl.pallas_call(
        body, name="cond_ctx_partial", grid=(D_MODEL // tr,),
        out_shape=jax.ShapeDtypeStruct((D_MODEL, 1), F32),
        in_specs=[pl.BlockSpec((1, n_cols), lambda i: (0, 0)), pl.BlockSpec((tr, n_cols), lambda i: (i, 0))],
        out_specs=pl.BlockSpec((tr, 1), lambda i: (i, 0)),
    )(d_modc, ada_w0)


def _cond_ctx_update(gathered, c_ctx, m, v):
    def body(p_ref, w_ref, m_ref, v_ref, g_ref, d_ref, mo_ref, vo_ref):
        acc = p_ref[0:1, :]
        for s in range(1, N_CHIPS):
            acc = acc + p_ref[16 * s:16 * s + 1, :]
        w = w_ref[...]
        g = acc * _dsilu(w)
        g_ref[...] = g
        d_ref[...], mo_ref[...], vo_ref[...] = _adamw_math(w, g, m_ref[...], v_ref[...])

    return pl.pallas_call(body, name="cond_ctx_update", out_shape=[jax.ShapeDtypeStruct((1, D_MODEL), F32)] * 4,
                          in_specs=[VMEM] * 4, out_specs=[VMEM] * 4)(gathered, c_ctx, m, v)


def _local_step(x2, ctx2, target, mod_mine, mod_ctx, lb_logits, scale_full, w_in_full, late_shards, norm_w, gnorm, final_w):
    row = lambda a: a.reshape(1, -1)
    shift0, scale0, gate0 = (row(a) for a in jnp.split(mod_mine[0], 3))
    shift1, scale1, gate1 = (row(a) for a in jnp.split(mod_mine[1], 3))
    shift_c, scale_c, _ = (row(a) for a in jnp.split(mod_ctx, 3))
    nw0, nw1 = norm_w[0:1], norm_w[1:2]
    scales0 = jnp.concatenate([scale_c, scale0])

    g5, h0 = _ln_mod_matmul(ctx2, x2, nw0, jnp.concatenate([shift_c, shift0]), scales0, w_in_full, "hgrn_in_forward")
    o_raw, full = _gla_forward(g5, lb_logits, late_shards)
    x1, res0 = _hgrn_out_forward(o_raw, g5, x2, gnorm, gate0, full["hgrn_w_out"])
    uz, h1 = _ln_mod_matmul(None, x1, nw1, shift1, scale1, full["pool_w_in"], "pool_in_forward")
    d_x2, loss_part, d_final, d_gate1 = _pool_forward_loss(uz, x1, target, gate1, full["pool_w_grp"], scale_full,
                                                           full["pool_w_out"], final_w)

    d_uz, dw_pool_out, dw_pool_grp, d_pscale = _pool_backward(d_x2, uz, gate1, full["pool_w_grp"], scale_full, full["pool_w_out"])
    d_x1, d_nw1, d_mod1 = _ln_mod_backward(d_uz, full["pool_w_in"], None, x1, nw1, scale1, d_x2, "pool_in_backward")
    dw_pool_in = _weight_grad(h1, d_uz, "pool_in_weight_grad")
    d_o, d_z, dw_hgrn_out, d_gate0, d_gnorm = _hgrn_out_backward(d_x1, o_raw, g5, res0, gnorm, gate0, full["hgrn_w_out"])
    late_grads = {"hgrn_w_out": dw_hgrn_out, "pool_w_in": dw_pool_in, "pool_w_grp": dw_pool_grp, "pool_w_out": dw_pool_out}
    d_g5, d_lb, late_slots = _gla_backward(g5, lb_logits, d_o, d_z, [late_grads[k].astype(BF16) for k in GATHER_LATE])
    dw_hgrn_in, slots0 = _weight_grad_paired(h0, d_g5, "hgrn_in_weight_grad")
    key = GATHER_EARLY[0]
    send_sem, recv_sem, part_thru, slots_thru, token = _scatter_start(dw_hgrn_in, slots0, d_lb, key)
    d_x, d_nw0, d_mod0 = _ln_mod_backward(d_g5, w_in_full, ctx2, x2, nw0 + token[0:1, 0:1], scales0, d_x1, "hgrn_in_backward")
    slots = dict(late_slots)
    pending = (send_sem, recv_sem, part_thru, slots_thru)

    zero = jnp.zeros((1, D_MODEL), F32)
    small = jnp.concatenate([d_mod0[1, 0:2], d_gate0, d_mod1[0, 0:2], d_gate1, d_mod0[0, 0:2], zero, d_nw0, d_nw1, d_gnorm,
                             d_final, d_pscale, d_lb, jnp.broadcast_to(loss_part[0:1, 0:1], (1, D_MODEL)),
                             jnp.zeros((SMALL_ROWS - 17, D_MODEL), F32)], axis=0)
    return {"d_x": d_x, "slots": slots, "pending": pending, "small": small}


def kernel(x, c, ctx, c_ctx, ada_w, ada_b, norm_w, hgrn_w_in, hgrn_lb_logits, hgrn_gnorm_w, hgrn_w_out, pool_w_in, pool_w_grp, pool_scale, pool_w_out, final_norm_w, loss_target, m_c_ctx, m_ada_w, m_ada_b, m_norm_w, m_hgrn_w_in, m_hgrn_lb_logits, m_hgrn_gnorm_w, m_hgrn_w_out, m_pool_w_in, m_pool_w_grp, m_pool_scale, m_pool_w_out, m_final_norm_w, v_c_ctx, v_ada_w, v_ada_b, v_norm_w, v_hgrn_w_in, v_hgrn_lb_logits, v_hgrn_gnorm_w, v_hgrn_w_out, v_pool_w_in, v_pool_w_grp, v_pool_scale, v_pool_w_out, v_final_norm_w):
    xi, yi, ci = _my_place()
    chip = 2 * xi + yi
    dev = 4 * xi + 2 * yi + ci
    ada_cols = ada_w.shape[2]
    lb_cols = hgrn_lb_logits.shape[2]
    ps_cols = pool_scale.shape[1]
    row = lambda a: a.reshape(1, -1)

    def chip_cols(a, n):
        return lax.dynamic_slice_in_dim(a, chip * n, n, axis=a.ndim - 1)

    def from_chips(g, rows_per_dev, take):
        return jnp.concatenate([g[2 * s * rows_per_dev:2 * s * rows_per_dev + take] for s in range(N_CHIPS)], axis=1)

    first = jnp.concatenate([jnp.broadcast_to(c, (8, D_MODEL)), jnp.pad(hgrn_lb_logits[0], ((0, 6), (0, 0))),
                             jnp.pad(pool_scale, ((0, 7), (0, 0)))], axis=1)
    shards = {"hgrn_w_in": hgrn_w_in[0], "hgrn_w_out": hgrn_w_out[0], "pool_w_in": pool_w_in[0],
              "pool_w_grp": pool_w_grp[0], "pool_w_out": pool_w_out[0]}
    first_all, parts_all, w_in_full = _prologue(first, row(c_ctx), ada_w, chip_cols(ada_b, ada_cols),
                                                shards[GATHER_EARLY[0]].astype(BF16))
    cond_all = first_all[::8, :D_MODEL]
    lb_logits = from_chips(first_all[:, D_MODEL:D_MODEL + lb_cols], 8, 2)
    scale_full = from_chips(first_all[:, D_MODEL + lb_cols:], 8, 1)
    cond_rows = jnp.concatenate([cond_all, row(c_ctx), jnp.zeros((7, D_MODEL), F32)], axis=0)
    mod_all = from_chips(parts_all, 32, 32).reshape(2, 16, 3 * D_MODEL)
    mod_mine = lax.dynamic_index_in_dim(mod_all, dev, axis=1, keepdims=False)

    loc = _local_step(x[0], ctx[0], loss_target[0], mod_mine, mod_all[0, 8], lb_logits, scale_full, w_in_full,
                      [shards[k].astype(BF16) for k in GATHER_LATE], norm_w, hgrn_gnorm_w, row(final_norm_w))

    small_all, sums, g_c, d_c, m_c, v_c = _small_sums_and_cond_ctx(loc["small"], ada_w[0], row(c_ctx), row(m_c_ctx), row(v_c_ctx))
    loss = sums[16, 0]

    def reduce_scattered(slots, names, name):
        return dict(zip(names, _sum_and_exchange([slots[k] for k in names], names, name)))

    big_grads = reduce_scattered(loc["slots"], GATHER_LATE, "exchange_halves_late")

    out = {}

    def update(name, w, g, m, v):
        shape = w.shape
        w2, g2, m2, v2 = (a.reshape(-1, shape[-1]) for a in (w, g, m, v))
        d, mn, vn = _adamw(w2, g2, m2, v2, "adamw_" + name)
        out[name] = tuple(a.reshape(shape) for a in (g2, d, mn, vn))

    moments = {"hgrn_w_in": (m_hgrn_w_in, v_hgrn_w_in), "hgrn_w_out": (m_hgrn_w_out, v_hgrn_w_out),
               "pool_w_in": (m_pool_w_in, v_pool_w_in), "pool_w_grp": (m_pool_w_grp, v_pool_w_grp),
               "pool_w_out": (m_pool_w_out, v_pool_w_out)}
    weights = {"hgrn_w_in": hgrn_w_in, "hgrn_w_out": hgrn_w_out, "pool_w_in": pool_w_in, "pool_w_grp": pool_w_grp,
               "pool_w_out": pool_w_out}
    for k in GATHER_LATE:
        update(k, weights[k], big_grads[k], *moments[k])

    g_ada_b = jnp.stack([(sums[0:3] + sums[6:9]).reshape(-1), sums[3:6].reshape(-1)])
    update("ada_b", ada_b, g_ada_b, m_ada_b, v_ada_b)
    update("norm_w", norm_w, sums[9:11], m_norm_w, v_norm_w)
    update("hgrn_gnorm_w", hgrn_gnorm_w, sums[11:12], m_hgrn_gnorm_w, v_hgrn_gnorm_w)
    update("final_norm_w", row(final_norm_w), sums[12:13], row(m_final_norm_w), row(v_final_norm_w))
    update("pool_scale", pool_scale, chip_cols(sums[13:14], ps_cols), m_pool_scale, v_pool_scale)
    update("hgrn_lb_logits", hgrn_lb_logits, chip_cols(sums[14:16], lb_cols)[None], m_hgrn_lb_logits, v_hgrn_lb_logits)

    per_dev = small_all.reshape(N_DEV, SMALL_ROWS, D_MODEL)
    pad7 = jnp.zeros((7, 3 * D_MODEL), F32)
    dm0 = jnp.concatenate([per_dev[:, 0:3].reshape(N_DEV, -1), sums[6:9].reshape(1, -1), pad7], axis=0)
    dm1 = jnp.concatenate([per_dev[:, 3:6].reshape(N_DEV, -1), jnp.zeros((8, 3 * D_MODEL), F32)], axis=0)
    d_mod = chip_cols(jnp.stack([dm0, dm1]), ada_cols)
    out["ada_w"] = _ada_update(cond_rows.T, d_mod, ada_w, m_ada_w, v_ada_w)

    out["c_ctx"] = tuple(a.reshape(-1) for a in (g_c, d_c, m_c, v_c))
    out["final_norm_w"] = tuple(a.reshape(-1) for a in out["final_norm_w"])

    done = [g_c, out["ada_w"][1]] + [out[k][1] for k in GATHER_LATE]
    key = GATHER_EARLY[0]
    _, early = _scatter_wait(*loc["pending"], done, key)
    big_grads = reduce_scattered({key: early}, GATHER_EARLY, "exchange_halves_early")
    for k in GATHER_EARLY:
        update(k, weights[k], big_grads[k], *moments[k])

    names = ["c_ctx", "ada_w", "ada_b", "norm_w", "hgrn_w_in", "hgrn_lb_logits", "hgrn_gnorm_w", "hgrn_w_out", "pool_w_in",
             "pool_w_grp", "pool_scale", "pool_w_out", "final_norm_w"]
    return (loss, loc["d_x"][None], *[out[k][0] for k in names], *[out[k][1] for k in names], *[out[k][2] for k in names],
            *[out[k][3] for k in names])
```

```python
import functools

import numpy as np
import jax
import jax.numpy as jnp
from jax import lax
from jax.experimental import pallas as pl
from jax.experimental.pallas import tpu as pltpu

F32 = jnp.float32
BF16 = jnp.bfloat16

D_MODEL = 1024
SEQ = 2048
CTX_LEN = 256
ROWS_ALL = CTX_LEN + SEQ
HEADS = 8
HEAD_DIM = 128
CHUNK = 64
N_CTX_CHUNKS = CTX_LEN // CHUNK
N_LAT_CHUNKS = SEQ // CHUNK
N_CHUNKS = N_CTX_CHUNKS + N_LAT_CHUNKS
GRID_W = 64
POOL_WINDOWS = (2, 4, 8, 16)
POOL_GROUPS = 4
POOL_GROUP_DIM = 256
HGRN_SECTIONS = 5
POOL_SECTIONS = 2
EPS = 1e-6
N_DEV = 8
N_CHIPS = 4
ROW_TILE = 256
SMALL_ROWS = 24

ADAM_LR = 0.001
ADAM_B1 = 0.9
ADAM_B2 = 0.999
ADAM_EPS = 1e-08
ADAM_WD = 0.01
ADAM_STEP = 10

MESH = pl.DeviceIdType.MESH
MIB = 1 << 20
ANY = pl.BlockSpec(memory_space=pl.ANY)
VMEM = pl.BlockSpec(memory_space=pltpu.VMEM)


def _params(vmem_mib=None):
    if vmem_mib is None:
        return pltpu.CompilerParams()
    return pltpu.CompilerParams(vmem_limit_bytes=vmem_mib * MIB)


def _pin(*operands):
    return [pltpu.with_memory_space_constraint(a, pltpu.HBM) if a.size * a.dtype.itemsize >= MIB else a for a in operands]


def _sig(a):
    return 0.5 * jnp.tanh(0.5 * a) + 0.5


def _silu(a):
    return a * _sig(a)


def _dsilu(a):
    s = _sig(a)
    return s * (1.0 + a * (1.0 - s))


def _mm(a, b):
    return jnp.dot(a.astype(BF16), b.astype(BF16), preferred_element_type=F32)


def _mm_nt(a, b):
    return lax.dot_general(a.astype(BF16), b.astype(BF16), (((1,), (1,)), ((), ())), preferred_element_type=F32)


def _mm_tn(a, b):
    return lax.dot_general(a.astype(BF16), b.astype(BF16), (((0,), (0,)), ((), ())), preferred_element_type=F32)


def _split2(a):
    hi = a.astype(BF16)
    lo = (a - hi.astype(F32)).astype(BF16)
    return hi, lo


def _mm_exact_lhs(m_bf, a):
    hi, lo = _split2(a)
    return jnp.dot(m_bf, hi, preferred_element_type=F32) + jnp.dot(m_bf, lo, preferred_element_type=F32)


def _mm_f32(a, b):
    ah, al = _split2(a)
    bh, bl = _split2(b)
    return (jnp.dot(ah, bh, preferred_element_type=F32) + jnp.dot(al, bh, preferred_element_type=F32)
            + jnp.dot(ah, bl, preferred_element_type=F32))


def _my_place():
    return lax.axis_index("x"), lax.axis_index("y"), lax.axis_index("c")


def _small_gather(x_ref, out_ref, send_sems, recv_sems, local_sem):
    m_per = x_ref.shape[0]
    x, y, c = _my_place()
    me, sibling = (x, y, c), (x, y, 1 - c)
    chips = [(1 - x, y), (x, 1 - y), (1 - x, 1 - y)]

    def rows(px, py, pc):
        return out_ref.at[pl.ds((4 * px + 2 * py + pc) * m_per, m_per), :]

    def copy(k, block, to, src=None):
        return pltpu.make_async_remote_copy(
            src_ref=rows(*block) if src is None else src, dst_ref=rows(*block),
            send_sem=send_sems.at[k], recv_sem=recv_sems.at[k], device_id=to, device_id_type=MESH)

    def mine():
        return pltpu.make_async_copy(x_ref, rows(*me), local_sem)

    def first():
        return [copy(0, me, sibling, src=x_ref)] + [copy(1 + j, me, (*chip, c), src=x_ref) for j, chip in enumerate(chips)]

    def start():
        mine().start()
        for cp in first():
            cp.start()

    def finish():
        passed = [copy(4 + j, (*chip, c), sibling) for j, chip in enumerate(chips)]
        for j, chip in enumerate(chips):
            copy(1 + j, (*chip, c), me).wait_recv()
            passed[j].start()
        copy(0, sibling, me).wait_recv()
        for j, chip in enumerate(chips):
            copy(4 + j, (*chip, 1 - c), me).wait_recv()
        for cp in first() + passed:
            cp.wait_send()
        mine().wait()

    return start, finish


W_SPECS = {
    "hgrn_w_in": ((D_MODEL, 5 * D_MODEL), (1, 1280, 0, 512)),
    "hgrn_w_out": ((D_MODEL, D_MODEL), (0, 256, 0, 128)),
    "pool_w_in": ((D_MODEL, 2 * D_MODEL), (1, 512, 0, 512)),
    "pool_w_grp": ((POOL_GROUPS, POOL_GROUP_DIM, POOL_GROUP_DIM), (1, 64, 1, 32)),
    "pool_w_out": ((D_MODEL, D_MODEL), (0, 256, 0, 128)),
}

def _al(v, m):
    return pl.multiple_of(v, m)


def _region(ref, spec, chip, half):
    ca, cn, ha, hn = spec
    idx = [slice(None)] * len(ref.shape)
    if ca == ha:
        if half is None:
            idx[ca] = pl.ds(_al(chip * cn, cn), cn)
        else:
            idx[ca] = pl.ds(_al(chip * cn + half * hn, hn), hn)
    else:
        idx[ca] = pl.ds(_al(chip * cn, cn), cn)
        if half is not None:
            idx[ha] = pl.ds(_al(half * hn, hn), hn)
    return ref.at[tuple(idx)]


def _half_of(ref, spec, half):
    _, _, ha, hn = spec
    idx = [slice(None)] * len(ref.shape)
    idx[ha] = pl.ds(_al(half * hn, hn), hn)
    return ref.at[tuple(idx)]


PIECE_BYTES = 256 * 1024


def _pieces(ref):
    lead = ref.shape[0]
    want = (int(np.prod(ref.shape)) * ref.dtype.itemsize) // PIECE_BYTES
    n = max([1] + [k for k in range(1, want + 1) if lead % k == 0 and (lead // k) % 16 == 0])
    rows = lead // n
    return [ref.at[pl.ds(i * rows, rows)] for i in range(n)]


def _half_shape(name):
    full, (ca, cn, ha, hn) = W_SPECS[name]
    shp = list(full)
    shp[ca] = cn
    shp[ha] = hn
    return tuple(shp)


def _gather_two_level(names, sh, land, full, send_sems, recv_sems, local_sems):
    n = len(names)
    specs = [W_SPECS[k][1] for k in names]
    x, y, c = _my_place()
    chip_me = 2 * x + y
    sibling = (x, y, 1 - c)
    chips = [(1 - x, y), (x, 1 - y), (1 - x, 1 - y)]

    def remote(k, src, dst, to):
        return pltpu.make_async_remote_copy(src_ref=src, dst_ref=dst, send_sem=send_sems.at[k], recv_sem=recv_sems.at[k],
                                            device_id=to, device_id_type=MESH)

    def three_halves(a):
        ca, cn, ha, hn = specs[a]
        idx = [slice(None)] * len(land[a].shape)
        if ca == ha:
            idx[ca] = pl.ds(0, 3 * hn)
        else:
            idx[ca], idx[ha] = pl.ds(0, 3 * cn), pl.ds(0, hn)
        return land[a].at[tuple(idx)]

    def own(a):
        return pltpu.make_async_copy(sh[a], _region(land[a], specs[a], chip_me, None), local_sems.at[a])

    def out(a):
        return pltpu.make_async_copy(land[a], full[a], local_sems.at[a])

    def start():
        for a in range(n):
            own(a).start()
            for px, py in chips:
                remote(a, _half_of(sh[a], specs[a], c), _region(land[a], specs[a], chip_me, c), (px, py, c)).start()

    def forward():
        for a in range(n):
            remote(a, three_halves(a), three_halves(a), sibling).wait_recv()
            for px, py in chips:
                landed = _region(land[a], specs[a], 2 * px + py, c)
                remote(n + a, landed, landed, sibling).start()

    def finish():
        for a in range(n):
            remote(n + a, three_halves(a), three_halves(a), sibling).wait_recv()
            remote(a, three_halves(a), three_halves(a), sibling).wait_send()
            remote(n + a, three_halves(a), three_halves(a), sibling).wait_send()
            own(a).wait()
        for a in range(n):
            out(a).start()
        for a in range(n):
            out(a).wait()

    return start, forward, finish


def _scatter_direct(names, part, slots, send_sems, recv_sems, local_sems):
    specs = [W_SPECS[k][1] for k in names]
    x, y, c = _my_place()
    dev_me = 4 * x + 2 * y + c

    def local(a):
        return pltpu.make_async_copy(_region(part[a], specs[a], 2 * x + y, c), slots[a].at[dev_me], local_sems.at[a])

    def start():
        for a in range(len(names)):
            local(a).start()
            for flip in range(1, N_DEV):
                tx = 1 - x if flip >> 2 else x
                ty = 1 - y if (flip >> 1) & 1 else y
                tc = 1 - c if flip & 1 else c
                pltpu.make_async_remote_copy(src_ref=_region(part[a], specs[a], 2 * tx + ty, tc), dst_ref=slots[a].at[dev_me],
                                             send_sem=send_sems.at[a], recv_sem=recv_sems.at[a], device_id=(tx, ty, tc),
                                             device_id_type=MESH).start()

    def wait():
        for a in range(len(names)):
            seven = slots[a].at[pl.ds(0, N_DEV - 1)]
            pltpu.make_async_remote_copy(src_ref=seven, dst_ref=seven, send_sem=send_sems.at[a], recv_sem=recv_sems.at[a],
                                         device_id=(x, y, c), device_id_type=MESH).wait()
            local(a).wait()

    return start, wait


HBM_SPEC = pl.BlockSpec(memory_space=pltpu.HBM)
SEM_SPEC = pl.BlockSpec(memory_space=pltpu.SEMAPHORE)
SPLIT_EFFECT = pltpu.SideEffectType.DATAFLOW_SIDE_EFFECTING


def _scatter_start(part, slots, after, name_key):
    _, cols, _, _ = W_SPECS[name_key][1]

    def body(part_ref, slots_ref, after_ref, send_sem, recv_sem, part_thru, slots_thru, token):
        x, y, c = _my_place()
        for tx, ty in ((1 - x, y), (x, 1 - y), (1 - x, 1 - y)):
            src = part_ref.at[:, pl.ds(_al((2 * tx + ty) * cols, cols), cols)]
            pltpu.make_async_remote_copy(src_ref=src, dst_ref=slots_ref.at[2 * x + y], send_sem=send_sem, recv_sem=recv_sem,
                                         device_id=(tx, ty, c), device_id_type=MESH).start()
        token[...] = jnp.zeros_like(token)

    return pl.pallas_call(
        body, name="scatter_start_" + name_key,
        out_shape=(pltpu.SemaphoreType.DMA(()), pltpu.SemaphoreType.DMA(()), pltpu.HBM(part.shape, part.dtype),
                   pltpu.HBM(slots.shape, slots.dtype), jax.ShapeDtypeStruct((8, 128), F32)),
        in_specs=(HBM_SPEC, HBM_SPEC, ANY), out_specs=(SEM_SPEC, SEM_SPEC, HBM_SPEC, HBM_SPEC, VMEM),
        input_output_aliases={0: 2, 1: 3},
        compiler_params=pltpu.CompilerParams(has_side_effects=SPLIT_EFFECT),
    )(pltpu.with_memory_space_constraint(part, pltpu.HBM), pltpu.with_memory_space_constraint(slots, pltpu.HBM), after)


def _scatter_wait(send_sem, recv_sem, part_thru, slots_thru, after, name_key):
    def body(part_ref, slots_ref, send_sem, recv_sem, *rest):
        x, y, c = _my_place()
        landed = slots_ref.at[pl.ds(0, N_CHIPS - 1)]
        copy = pltpu.make_async_remote_copy(src_ref=landed, dst_ref=landed, send_sem=send_sem, recv_sem=recv_sem,
                                            device_id=(x, y, c), device_id_type=MESH)
        copy.wait_send()
        copy.wait_recv()

    return pl.pallas_call(
        body, name="scatter_wait_" + name_key,
        out_shape=(pltpu.HBM(part_thru.shape, part_thru.dtype), pltpu.HBM(slots_thru.shape, slots_thru.dtype)),
        in_specs=(HBM_SPEC, HBM_SPEC, SEM_SPEC, SEM_SPEC) + (ANY,) * len(after), out_specs=(HBM_SPEC, HBM_SPEC),
        input_output_aliases={0: 0, 1: 1},
        compiler_params=pltpu.CompilerParams(has_side_effects=SPLIT_EFFECT),
    )(part_thru, slots_thru, send_sem, recv_sem, *after)


def _comm_sems(n):
    return [pltpu.SemaphoreType.DMA((n,)), pltpu.SemaphoreType.DMA((n,)), pltpu.SemaphoreType.DMA((n,))]


GATHER_EARLY = ("hgrn_w_in",)
GATHER_LATE = ("hgrn_w_out", "pool_w_in", "pool_w_grp", "pool_w_out")


def _weight_gather(sh, full, spec, send_sems, recv_sems, local_sem):
    x, y, c = _my_place()
    chip_me = 2 * x + y
    sibling = (x, y, 1 - c)
    chips = [(1 - x, y), (x, 1 - y), (1 - x, 1 - y)]

    def remote(k, src, dst, to):
        return pltpu.make_async_remote_copy(src_ref=src, dst_ref=dst, send_sem=send_sems.at[k], recv_sem=recv_sems.at[k],
                                            device_id=to, device_id_type=MESH)

    def own():
        return pltpu.make_async_copy(sh, _region(full, spec, chip_me, None), local_sem)

    def direct():
        return [remote(j, _half_of(sh, spec, c), _region(full, spec, chip_me, c), (px, py, c)) for j, (px, py) in enumerate(chips)]

    def start():
        own().start()
        for cp in direct():
            cp.start()

    def finish():
        passed = []
        for j, (px, py) in enumerate(chips):
            landed = _region(full, spec, 2 * px + py, c)
            remote(j, landed, landed, (px, py, c)).wait_recv()
            passed.append(remote(3 + j, landed, landed, sibling))
            passed[-1].start()
        for j, (px, py) in enumerate(chips):
            other = _region(full, spec, 2 * px + py, 1 - c)
            remote(3 + j, other, other, sibling).wait_recv()
        for cp in direct() + passed:
            cp.wait_send()
        own().wait()

    return start, finish


def _prologue(first, c_ctx, ada_w, ada_b_cols, w_shard):
    n_layers, _, n_cols = ada_w.shape
    key = GATHER_EARLY[0]
    spec = W_SPECS[key][1]
    m_first = first.shape[0]

    def body(first_ref, cctx_ref, adaw_ref, adab_ref, sh_ref, first_all, parts_all, full_ref, parts_scr,
             s1, r1, l1, s2, r2, l2, ws, wr, wl):
        start_first, finish_first = _small_gather(first_ref, first_all, s1, r1, l1)
        start_parts, finish_parts = _small_gather(parts_scr, parts_all, s2, r2, l2)
        start_weight, finish_weight = _weight_gather(sh_ref, full_ref, spec, ws, wr, wl)
        start_first()
        start_weight()
        finish_first()
        cond = jnp.concatenate([first_all[m_first * d:m_first * d + 1, 0:D_MODEL] for d in range(N_DEV)]
                               + [cctx_ref[...], jnp.zeros((16 - N_DEV - 1, D_MODEL), F32)], axis=0)
        act = _silu(cond)
        for i in range(n_layers):
            parts_scr[16 * i:16 * (i + 1), :] = _mm_f32(act, adaw_ref[i]) + adab_ref[i]
        start_parts()
        finish_parts()
        finish_weight()

    seven = [pltpu.SemaphoreType.DMA((7,)), pltpu.SemaphoreType.DMA((7,)), pltpu.SemaphoreType.DMA]
    return pl.pallas_call(
        body, name="prologue",
        out_shape=[jax.ShapeDtypeStruct((N_DEV * m_first, first.shape[1]), F32),
                   jax.ShapeDtypeStruct((N_DEV * 16 * n_layers, n_cols), F32), jax.ShapeDtypeStruct(W_SPECS[key][0], BF16)],
        in_specs=[VMEM] * 5, out_specs=[VMEM] * 3,
        scratch_shapes=[pltpu.VMEM((16 * n_layers, n_cols), F32)] + seven + seven
        + [pltpu.SemaphoreType.DMA((6,)), pltpu.SemaphoreType.DMA((6,)), pltpu.SemaphoreType.DMA],
        compiler_params=_params(48),
    )(first, c_ctx, ada_w, ada_b_cols.reshape(n_layers, 1, n_cols), w_shard)


def _slot_shapes(names):
    return [jax.ShapeDtypeStruct((N_DEV,) + _half_shape(k), BF16) for k in names]


def _sum_and_exchange(slots, names, name):
    n = len(names)
    specs = [W_SPECS[k][1] for k in names]

    def shard_shape(k):
        shp = list(_half_shape(k))
        shp[W_SPECS[k][1][2]] *= 2
        return tuple(shp)

    def body(*refs):
        slot, out, half = refs[:n], refs[n:2 * n], refs[2 * n:3 * n]
        send_sems, recv_sems, local_sems = refs[3 * n:]
        x, y, c = _my_place()
        sibling = (x, y, 1 - c)
        for a in range(n):
            acc = slot[a][0].astype(F32)
            for d in range(1, slot[a].shape[0]):
                acc = acc + slot[a][d].astype(F32)
            half[a][...] = acc

        def remote(a, src, dst):
            return pltpu.make_async_remote_copy(src_ref=src, dst_ref=dst, send_sem=send_sems.at[a], recv_sem=recv_sems.at[a],
                                                device_id=sibling, device_id_type=MESH)

        local = [pltpu.make_async_copy(half[a], _half_of(out[a], specs[a], c), local_sems.at[a]) for a in range(n)]
        for cp in local:
            cp.start()
        for a in range(n):
            mine = _half_of(out[a], specs[a], c)
            for src, dst in zip(_pieces(half[a]), _pieces(mine)):
                remote(a, src, dst).start()
        for a in range(n):
            theirs = _half_of(out[a], specs[a], 1 - c)
            remote(a, theirs, theirs).wait_recv()
        for a in range(n):
            remote(a, half[a], half[a]).wait_send()
        for cp in local:
            cp.wait()

    return pl.pallas_call(
        body, name=name,
        out_shape=[jax.ShapeDtypeStruct(shard_shape(k), F32) for k in names],
        in_specs=[VMEM] * n, out_specs=[VMEM] * n,
        scratch_shapes=[pltpu.VMEM(_half_shape(k), F32) for k in names]
        + [pltpu.SemaphoreType.DMA((n,)), pltpu.SemaphoreType.DMA((n,)), pltpu.SemaphoreType.DMA((n,))],
        compiler_params=_params(40),
    )(*slots)


def _ln_mod_matmul(ctx_tile, xin, nw, shift, scale, w, name):
    n_mod = shift.shape[0]
    skip = n_mod - 1
    tm = ROW_TILE if skip else 2 * ROW_TILE
    rows = xin.shape[0] + skip * tm
    n_sec = w.shape[1] // D_MODEL

    def body(*refs):
        c_ref = refs[0] if skip else None
        x_ref, nw_ref, sh_ref, sc_ref, w_ref, g_ref, h_ref = refs[skip:]
        xv = jnp.where(pl.program_id(0) == 0, c_ref[...], x_ref[...]) if skip else x_ref[...]
        r = lax.rsqrt(jnp.mean(xv * xv, axis=-1, keepdims=True) + EPS)
        h_f32 = (xv * r * nw_ref[...]) * (1.0 + sc_ref[...]) + sh_ref[...]
        h = h_f32.astype(BF16)
        h_ref[...] = h
        for k in range(n_sec):
            g_ref[k] = jnp.dot(h, w_ref[:, k * D_MODEL:(k + 1) * D_MODEL], preferred_element_type=F32)

    mod_spec = pl.BlockSpec((None, 1, D_MODEL), lambda i: (jnp.minimum(i, n_mod - 1), 0, 0))
    return pl.pallas_call(
        body, name=name, grid=(rows // tm,),
        out_shape=[pltpu.HBM((n_sec, rows, D_MODEL), F32), pltpu.HBM((rows, D_MODEL), BF16)],
        in_specs=[pl.BlockSpec((tm, D_MODEL), lambda i: (0, 0))] * skip
        + [pl.BlockSpec((tm, D_MODEL), lambda i: (jnp.maximum(i - skip, 0), 0)),
           pl.BlockSpec((1, D_MODEL), lambda i: (0, 0)),
           mod_spec, mod_spec,
           pl.BlockSpec((D_MODEL, n_sec * D_MODEL), lambda i: (0, 0))],
        out_specs=[pl.BlockSpec((n_sec, tm, D_MODEL), lambda i: (0, i, 0)),
                   pl.BlockSpec((tm, D_MODEL), lambda i: (i, 0))],
        compiler_params=_params(48),
    )(*_pin(*([ctx_tile] * skip), xin, nw, shift.reshape(n_mod, 1, D_MODEL), scale.reshape(n_mod, 1, D_MODEL), w))


def _gates(fpre, lb):
    sg = _sig(fpre)
    f = lb + (1.0 - lb) * sg
    return sg, f, 1.0 - f, jnp.log(f)


G_SPEC = lambda sec: pl.BlockSpec((None, ROWS_ALL, HEAD_DIM), lambda h, sec=sec: (sec, 0, h))


GROUP = 2 * CHUNK


def _group_masks():
    rid, cid = np.arange(GROUP)[:, None], np.arange(GROUP)[None, :]
    same = (rid >= CHUNK) == (cid >= CHUNK)
    sums, back = [], []
    for rev in (False, True):
        causal = (cid >= rid) if rev else (cid <= rid)
        anti = (cid <= rid) if rev else (cid >= rid)
        sums.append(same & causal)
        back.append(np.concatenate([same & anti, same & ~anti], axis=1))
    sums, back = np.stack(sums).astype(np.float32), np.stack(back).astype(np.float32)
    return jnp.asarray(sums, BF16), jnp.asarray(back, BF16), jnp.asarray(sums, F32)


MASK_SPECS = [pl.BlockSpec((2, GROUP, GROUP), lambda h: (0, 0, 0)), pl.BlockSpec((2, GROUP, 2 * GROUP), lambda h: (0, 0, 0)),
              pl.BlockSpec((2, GROUP, GROUP), lambda h: (0, 0, 0))]


def _group_sum(m_bf, a):
    hi, lo = _split2(a)
    r = jnp.dot(m_bf, jnp.concatenate([hi, lo], axis=1), preferred_element_type=F32)
    return r[:, :HEAD_DIM] + r[:, HEAD_DIM:]


def _chunk_row(a, pos):
    return jnp.concatenate([jnp.broadcast_to(a[c * CHUNK + pos:c * CHUNK + pos + 1, :], (CHUNK, HEAD_DIM)) for c in range(2)], axis=0)


def _by_chunk(a, second):
    return jnp.concatenate([jnp.where(second, 0.0, a), jnp.where(second, a, 0.0)], axis=1)


def _own_block(r):
    return jnp.concatenate([r[0:CHUNK, 0:HEAD_DIM], r[CHUNK:GROUP, HEAD_DIM:2 * HEAD_DIM]], axis=0)


def _scan_step_of(row_chunk, rev, latent):
    if not rev:
        return row_chunk
    return (N_CHUNKS + N_CTX_CHUNKS - 1 - row_chunk) if latent else (N_CTX_CHUNKS - 1 - row_chunk)


def _group_rows(i, j, per_step, latent):
    base = CTX_LEN if latent else 0
    return pl.multiple_of(base + (i * per_step + j) * GROUP, GROUP)


GROUPS_PER_STEP = 8
GROUPS_PER_GRAD_STEP = 4


def _gla_forward(g5, lb_logits, late_shards):
    q_scale = HEAD_DIM ** -0.5
    per_lat, per_ctx = GROUPS_PER_STEP, min(GROUPS_PER_STEP, CTX_LEN // GROUP)
    n_late = len(GATHER_LATE)

    def body(ff_ref, fb_ref, v_ref, q_ref, lg_ref, sums_ref, _, keep_ref, *rest):
        shard_refs, o_ref, full_refs = rest[:n_late], rest[n_late], rest[n_late + 1:2 * n_late + 1]
        st_ref, decay_ref, qt_ref = rest[2 * n_late + 1:2 * n_late + 4]
        land_refs = rest[2 * n_late + 4:3 * n_late + 4]
        start_gather, forward_gather, finish_gather = _gather_two_level(GATHER_LATE, shard_refs, land_refs, full_refs,
                                                                        *rest[3 * n_late + 4:])

        @pl.when(pl.program_id(0) == 0)
        def _():
            start_gather()

        @pl.when(pl.program_id(0) == HEADS - 1)
        def _():
            forward_gather()

        second = lax.broadcasted_iota(jnp.int32, (GROUP, HEAD_DIM), 0) >= CHUNK
        for rev in (False, True):
            f_ref = fb_ref if rev else ff_ref
            lb = _sig(lg_ref[1:2, :] if rev else lg_ref[0:1, :])
            d = 1 if rev else 0
            last = 0 if rev else CHUNK - 1
            mid = CHUNK // 2 if rev else CHUNK // 2 - 1

            def local_step(i, carry, latent, per):
                r0s = [_group_rows(i, j, per, latent) for j in range(per)]
                rows = [pl.ds(r0, GROUP) for r0 in r0s]
                gates = [_gates(f_ref[r, :], lb) for r in rows]
                vs = [v_ref[r, :] for r in rows]
                bs = [_group_sum(sums_ref[d], g[3]) for g in gates]
                bls = [_chunk_row(b, last) for b in bs]
                ups = [_mm_tn(v, _by_chunk(g[2] * jnp.exp(bl - b), second)) for v, g, b, bl in zip(vs, gates, bs, bls)]
                if latent:
                    qs = [_silu(q_ref[r, :]) * q_scale for r in rows]
                    bms = [_chunk_row(b, mid) for b in bs]
                    a_s = [_mm_nt(q * jnp.exp(b - bm), g[2] * jnp.exp(bm - b)) for q, g, b, bm in zip(qs, gates, bs, bms)]
                    outs = [_mm(a * keep_ref[d], v) for a, v in zip(a_s, vs)]
                for j in range(per):
                    for c in range(2):
                        step = _scan_step_of(r0s[j] // CHUNK + c, rev, latent)
                        decay_ref[step] = jnp.exp(bls[j][c * CHUNK:c * CHUNK + 1, :])
                        st_ref[step] = ups[j][:, c * HEAD_DIM:(c + 1) * HEAD_DIM]
                    if latent:
                        orow = pl.ds(pl.multiple_of(r0s[j] - CTX_LEN, GROUP), GROUP)
                        qt_ref[orow, :] = (qs[j] * jnp.exp(bs[j])).astype(BF16)
                        if rev:
                            o_ref[orow, :] += outs[j]
                        else:
                            o_ref[orow, :] = outs[j]
                return carry

            lax.fori_loop(0, CTX_LEN // (per_ctx * GROUP), functools.partial(local_step, latent=False, per=per_ctx), 0)
            lax.fori_loop(0, SEQ // (per_lat * GROUP), functools.partial(local_step, latent=True, per=per_lat), 0)

            def scan_step(t, st):
                update = st_ref[t]
                st_ref[t] = st
                return st * decay_ref[t] + update

            lax.fori_loop(0, N_CHUNKS, scan_step, jnp.zeros((HEAD_DIM, HEAD_DIM), F32), unroll=2)

            def inter_step(i, carry):
                r0s = [_group_rows(i, j, per_lat, True) for j in range(per_lat)]
                orows = [pl.ds(pl.multiple_of(r0 - CTX_LEN, GROUP), GROUP) for r0 in r0s]
                states = [jnp.concatenate([st_ref[_scan_step_of(r0 // CHUNK + c, rev, True)] for c in range(2)], axis=0)
                          for r0 in r0s]
                prods = [lax.dot_general(qt_ref[orow, :], s.astype(BF16), (((1,), (1,)), ((), ())), preferred_element_type=F32)
                         for orow, s in zip(orows, states)]
                for orow, r in zip(orows, prods):
                    o_ref[orow, :] += _own_block(r)
                return carry

            lax.fori_loop(0, SEQ // (per_lat * GROUP), inter_step, 0)

        @pl.when(pl.program_id(0) == HEADS - 1)
        def _():
            finish_gather()

    outs = pl.pallas_call(
        body, name="gla_forward", grid=(HEADS,),
        out_shape=[pltpu.HBM((SEQ, D_MODEL), F32)] + [jax.ShapeDtypeStruct(W_SPECS[k][0], BF16) for k in GATHER_LATE],
        in_specs=[G_SPEC(0), G_SPEC(1), G_SPEC(2), G_SPEC(3), pl.BlockSpec((2, HEAD_DIM), lambda h: (0, h))] + MASK_SPECS
        + [ANY] * n_late,
        out_specs=[pl.BlockSpec((SEQ, HEAD_DIM), lambda h: (0, h))] + [ANY] * n_late,
        scratch_shapes=[pltpu.VMEM((N_CHUNKS, HEAD_DIM, HEAD_DIM), F32), pltpu.VMEM((N_CHUNKS, 1, HEAD_DIM), F32),
                        pltpu.VMEM((SEQ, HEAD_DIM), BF16)] + [pltpu.VMEM(W_SPECS[k][0], BF16) for k in GATHER_LATE]
        + [pltpu.SemaphoreType.DMA((2 * n_late,)), pltpu.SemaphoreType.DMA((2 * n_late,)), pltpu.SemaphoreType.DMA((n_late,))],
        compiler_params=_params(48),
    )(*_pin(g5, g5, g5, g5, lb_logits), *_group_masks(), *late_shards)
    return outs[0], dict(zip(GATHER_LATE, outs[1:]))


def _gla_backward(g5, lb_logits, d_o, d_z, late_parts):
    q_scale = HEAD_DIM ** -0.5
    per_lat, per_ctx = GROUPS_PER_STEP, min(GROUPS_PER_STEP, CTX_LEN // GROUP)
    n_late = len(GATHER_LATE)

    def body(ff_ref, fb_ref, v_ref, q_ref, lg_ref, do_ref, dz_ref, sums_ref, back_ref, keep_ref, *rest):
        part_refs, (dg_ref, dlg_ref), slot_refs = rest[:n_late], rest[n_late:n_late + 2], rest[n_late + 2:2 * n_late + 2]
        st_ref, dst_ref, decay_ref, both_ref = rest[2 * n_late + 2:2 * n_late + 6]
        start_scatter, wait_scatter = _scatter_direct(GATHER_LATE, part_refs, slot_refs, *rest[2 * n_late + 6:])

        @pl.when(pl.program_id(0) == 0)
        def _():
            start_scatter()

        dg_ref[3, 0:CTX_LEN, :] = jnp.zeros((CTX_LEN, HEAD_DIM), BF16)
        dg_ref[4, 0:CTX_LEN, :] = jnp.zeros((CTX_LEN, HEAD_DIM), BF16)
        dg_ref[4, CTX_LEN:ROWS_ALL, :] = dz_ref[...]
        second = lax.broadcasted_iota(jnp.int32, (GROUP, HEAD_DIM), 0) >= CHUNK
        for rev in (False, True):
            d = 1 if rev else 0
            f_ref = fb_ref if rev else ff_ref
            lb = _sig(lg_ref[d:d + 1, :])
            last = 0 if rev else CHUNK - 1
            mid = CHUNK // 2 if rev else CHUNK // 2 - 1

            def local_step(i, carry, latent, per):
                r0s = [_group_rows(i, j, per, latent) for j in range(per)]
                rows = [pl.ds(r0, GROUP) for r0 in r0s]
                gates = [_gates(f_ref[r, :], lb) for r in rows]
                bs = [_group_sum(sums_ref[d], g[3]) for g in gates]
                bls = [_chunk_row(b, last) for b in bs]
                ups = [_mm_tn(v_ref[r, :], _by_chunk(g[2] * jnp.exp(bl - b), second)) for r, g, b, bl in zip(rows, gates, bs, bls)]
                if latent:
                    orows = [pl.ds(pl.multiple_of(r0 - CTX_LEN, GROUP), GROUP) for r0 in r0s]
                    d_ups = [_mm_tn(do_ref[orow, :], _by_chunk(_silu(q_ref[r, :]) * q_scale * jnp.exp(b), second))
                             for orow, r, b in zip(orows, rows, bs)]
                for j in range(per):
                    for c in range(2):
                        step = _scan_step_of(r0s[j] // CHUNK + c, rev, latent)
                        decay_ref[step] = jnp.exp(bls[j][c * CHUNK:c * CHUNK + 1, :])
                        st_ref[step] = ups[j][:, c * HEAD_DIM:(c + 1) * HEAD_DIM]
                        if latent:
                            dst_ref[step] = d_ups[j][:, c * HEAD_DIM:(c + 1) * HEAD_DIM]
                        else:
                            dst_ref[step] = jnp.zeros((HEAD_DIM, HEAD_DIM), F32)
                return carry

            lax.fori_loop(0, CTX_LEN // (per_ctx * GROUP), functools.partial(local_step, latent=False, per=per_ctx), 0)
            lax.fori_loop(0, SEQ // (per_lat * GROUP), functools.partial(local_step, latent=True, per=per_lat), 0)

            def scan_step(i, carry):
                st, d_st = carry
                j = N_CHUNKS - 1 - i
                update, d_update = st_ref[i], dst_ref[j]
                st_ref[i] = st
                dst_ref[j] = d_st
                return st * decay_ref[i] + update, d_st * decay_ref[j] + d_update

            zero_state = jnp.zeros((HEAD_DIM, HEAD_DIM), F32)
            lax.fori_loop(0, N_CHUNKS, scan_step, (zero_state, zero_state))

            def grad_step(i, dlb, latent, per):
                r0s = [_group_rows(i, j, per, latent) for j in range(per)]
                rows = [pl.ds(r0, GROUP) for r0 in r0s]
                gates = [_gates(f_ref[r, :], lb) for r in rows]
                vs = [v_ref[r, :] for r in rows]
                bs = [_group_sum(sums_ref[d], g[3]) for g in gates]
                bls = [_chunk_row(b, last) for b in bs]
                e_ends = [jnp.exp(bl - b) for b, bl in zip(bs, bls)]
                k_ends = [g[2] * e for g, e in zip(gates, e_ends)]
                sts = [[st_ref[_scan_step_of(r0 // CHUNK + c, rev, latent)] for c in range(2)] for r0 in r0s]
                d_sts = [[dst_ref[_scan_step_of(r0 // CHUNK + c, rev, latent)] for c in range(2)] for r0 in r0s]
                d_kends = [_own_block(_mm(v, jnp.concatenate(ds, axis=1))) for v, ds in zip(vs, d_sts)]
                d_vs = [_own_block(_mm_nt(ke, jnp.concatenate(ds, axis=0))) for ke, ds in zip(k_ends, d_sts)]
                at_last = [jnp.concatenate([jnp.broadcast_to(jnp.sum(ds[c] * s[c], axis=0, keepdims=True), (CHUNK, HEAD_DIM))
                                            for c in range(2)], axis=0) * jnp.exp(bl) for ds, s, bl in zip(d_sts, sts, bls)]
                t_kends = [dk * ke for dk, ke in zip(d_kends, k_ends)]
                d_ks = [dk * e for dk, e in zip(d_kends, e_ends)]
                if latent:
                    orows = [pl.ds(pl.multiple_of(r0 - CTX_LEN, GROUP), GROUP) for r0 in r0s]
                    qpres = [q_ref[r, :] for r in rows]
                    q_sigs = [_sig(qp) for qp in qpres]
                    qs = [qp * sg * q_scale for qp, sg in zip(qpres, q_sigs)]
                    bms = [_chunk_row(b, mid) for b in bs]
                    e_bs = [jnp.exp(b) for b in bs]
                    e_qms = [jnp.exp(b - bm) for b, bm in zip(bs, bms)]
                    e_kms = [jnp.exp(bm - b) for b, bm in zip(bs, bms)]
                    q_ts = [q * e for q, e in zip(qs, e_bs)]
                    q_ms = [q * e for q, e in zip(qs, e_qms)]
                    k_ms = [g[2] * e for g, e in zip(gates, e_kms)]
                    d_outs = [do_ref[orow, :] for orow in orows]
                    a_s = [_mm_nt(qm, km) * keep_ref[d] for qm, km in zip(q_ms, k_ms)]
                    d_as = [_mm_nt(do, v) * keep_ref[d] for do, v in zip(d_outs, vs)]
                    d_qts = [_own_block(_mm(do, jnp.concatenate(s, axis=1))) for do, s in zip(d_outs, sts)]
                    d_qms = [_mm(da, km) for da, km in zip(d_as, k_ms)]
                    d_kms = [_mm_tn(da, qm) for da, qm in zip(d_as, q_ms)]
                    d_vs = [dv + _mm_tn(a, do) for dv, a, do in zip(d_vs, a_s, d_outs)]
                    d_ks = [dk + dkm * e for dk, dkm, e in zip(d_ks, d_kms, e_kms)]
                    d_bs = [jnp.concatenate([dqt * qt + dqm * qm - dkm * km, t], axis=0)
                            for dqt, qt, dqm, qm, dkm, km, t in zip(d_qts, q_ts, d_qms, q_ms, d_kms, k_ms, t_kends)]
                    d_qs = [dqt * eb + dqm * eq for dqt, eb, dqm, eq in zip(d_qts, e_bs, d_qms, e_qms)]
                    back = back_ref[d]
                else:
                    d_bs, back = t_kends, back_ref[d, :, GROUP:2 * GROUP]
                d_lfs = [_group_sum(back, db) + al for db, al in zip(d_bs, at_last)]
                for j in range(per):
                    sg, f = gates[j][0], gates[j][1]
                    d_f = d_lfs[j] / f - d_ks[j]
                    dg_ref[d, rows[j], :] = (d_f * (1.0 - lb) * sg * (1.0 - sg)).astype(BF16)
                    dlb = dlb + jnp.sum(d_f * (1.0 - sg), axis=0, keepdims=True)
                    if rev:
                        dg_ref[2, rows[j], :] = (both_ref[0, rows[j], :] + d_vs[j]).astype(BF16)
                    else:
                        both_ref[0, rows[j], :] = d_vs[j]
                    if latent:
                        d_qpre = d_qs[j] * q_scale * (q_sigs[j] * (1.0 + qpres[j] * (1.0 - q_sigs[j])))
                        if rev:
                            dg_ref[3, rows[j], :] = (both_ref[1, rows[j], :] + d_qpre).astype(BF16)
                        else:
                            both_ref[1, rows[j], :] = d_qpre
                return dlb

            dlb = lax.fori_loop(0, SEQ // (GROUPS_PER_GRAD_STEP * GROUP),
                                functools.partial(grad_step, latent=True, per=GROUPS_PER_GRAD_STEP), jnp.zeros((1, HEAD_DIM), F32))
            dlb = lax.fori_loop(0, CTX_LEN // (per_ctx * GROUP), functools.partial(grad_step, latent=False, per=per_ctx), dlb)
            dlg_ref[d:d + 1, :] = dlb * lb * (1.0 - lb)

        @pl.when(pl.program_id(0) == HEADS - 1)
        def _():
            wait_scatter()

    col = pl.BlockSpec((SEQ, HEAD_DIM), lambda h: (0, h))
    outs = pl.pallas_call(
        body, name="gla_backward", grid=(HEADS,),
        out_shape=[pltpu.HBM((HGRN_SECTIONS, ROWS_ALL, D_MODEL), BF16), jax.ShapeDtypeStruct((2, D_MODEL), F32)]
        + _slot_shapes(GATHER_LATE),
        in_specs=[G_SPEC(0), G_SPEC(1), G_SPEC(2), G_SPEC(3), pl.BlockSpec((2, HEAD_DIM), lambda h: (0, h)), col, col]
        + MASK_SPECS + [ANY] * n_late,
        out_specs=[pl.BlockSpec((HGRN_SECTIONS, ROWS_ALL, HEAD_DIM), lambda h: (0, 0, h)),
                   pl.BlockSpec((2, HEAD_DIM), lambda h: (0, h))] + [ANY] * n_late,
        scratch_shapes=[pltpu.VMEM((N_CHUNKS, HEAD_DIM, HEAD_DIM), F32), pltpu.VMEM((N_CHUNKS, HEAD_DIM, HEAD_DIM), F32),
                        pltpu.VMEM((N_CHUNKS, 1, HEAD_DIM), F32), pltpu.VMEM((2, ROWS_ALL, HEAD_DIM), F32)]
        + _comm_sems(n_late),
        compiler_params=_params(48),
    )(*_pin(g5, g5, g5, g5, lb_logits, d_o, d_z), *_group_masks(), *late_parts)
    return outs[0], outs[1], dict(zip(GATHER_LATE, outs[2:]))


def _head_norm(o, scr):
    rs = []
    for h in range(HEADS):
        cols = slice(h * HEAD_DIM, (h + 1) * HEAD_DIM)
        oh = o[:, cols]
        r = lax.rsqrt(jnp.mean(oh * oh, axis=-1, keepdims=True) + EPS)
        scr[:, cols] = oh * r
        rs.append(r)
    return rs


def _hgrn_out_forward(o_raw, g5, xin, gnorm_w, gate, w_out):
    tm = ROW_TILE

    def body(o_ref, z_ref, x_ref, gw_ref, gate_ref, w_ref, x1_ref, res_ref, scr):
        _head_norm(o_ref[...], scr)
        a = scr[...] * gw_ref[...] * _silu(z_ref[...])
        res = _mm(a, w_ref[...])
        res_ref[...] = res
        x1_ref[...] = x_ref[...] + gate_ref[...] * res

    tile = pl.BlockSpec((tm, D_MODEL), lambda i: (i, 0))
    vec = pl.BlockSpec((1, D_MODEL), lambda i: (0, 0))
    return pl.pallas_call(
        body, name="hgrn_out_forward", grid=(SEQ // tm,),
        out_shape=[pltpu.HBM((SEQ, D_MODEL), F32)] * 2,
        in_specs=[tile, pl.BlockSpec((None, tm, D_MODEL), lambda i: (4, i + CTX_LEN // tm, 0)), tile, vec, vec,
                  pl.BlockSpec((D_MODEL, D_MODEL), lambda i: (0, 0))],
        out_specs=[tile, tile],
        scratch_shapes=[pltpu.VMEM((tm, D_MODEL), F32)],
        compiler_params=_params(32),
    )(*_pin(o_raw, g5, xin, gnorm_w, gate, w_out))


def _hgrn_out_backward(d_x1, o_raw, g5, res, gnorm_w, gate, w_out):
    tm = ROW_TILE

    def body(dx_ref, o_ref, z_ref, res_ref, gw_ref, gate_ref, w_ref, do_ref, dz_ref, dw_out, dgate_ref, dgw_ref, scr, scr2,
             dw_ref):
        @pl.when(pl.program_id(0) == 0)
        def _():
            dw_ref[...] = jnp.zeros_like(dw_ref)
            dgate_ref[...] = jnp.zeros_like(dgate_ref)
            dgw_ref[...] = jnp.zeros_like(dgw_ref)

        dx = dx_ref[...]
        dgate_ref[...] += jnp.sum(dx * res_ref[...], axis=0, keepdims=True)
        d_res = (dx * gate_ref[...]).astype(BF16)
        d_a = _mm_nt(d_res, w_ref[...])
        rs = _head_norm(o_ref[...], scr)
        z = z_ref[...]
        sz = _silu(z)
        o_hat = scr[...]
        o_n = o_hat * gw_ref[...]
        dw_ref[...] += _mm_tn(o_n * sz, d_res)
        d_on = d_a * sz
        dz_ref[...] = (d_a * o_n * _dsilu(z)).astype(BF16)
        dgw_ref[...] += jnp.sum(d_on * o_hat, axis=0, keepdims=True)
        scr2[...] = d_on * gw_ref[...]
        for h in range(HEADS):
            cols = slice(h * HEAD_DIM, (h + 1) * HEAD_DIM)
            dh, oh = scr2[:, cols], scr[:, cols]
            do_ref[:, cols] = (rs[h] * (dh - oh * jnp.mean(dh * oh, axis=-1, keepdims=True))).astype(BF16)

        @pl.when(pl.program_id(0) == SEQ // tm - 1)
        def _():
            dw_out[...] = dw_ref[...].astype(BF16)

    tile = pl.BlockSpec((tm, D_MODEL), lambda i: (i, 0))
    vec = pl.BlockSpec((1, D_MODEL), lambda i: (0, 0))
    mat = pl.BlockSpec((D_MODEL, D_MODEL), lambda i: (0, 0))
    return pl.pallas_call(
        body, name="hgrn_out_backward", grid=(SEQ // tm,),
        out_shape=[pltpu.HBM((SEQ, D_MODEL), BF16)] * 2 + [pltpu.HBM((D_MODEL, D_MODEL), BF16)]
        + [jax.ShapeDtypeStruct((1, D_MODEL), F32)] * 2,
        in_specs=[tile, tile, pl.BlockSpec((None, tm, D_MODEL), lambda i: (4, i + CTX_LEN // tm, 0)), tile, vec, vec, mat],
        out_specs=[tile, tile, mat, vec, vec],
        scratch_shapes=[pltpu.VMEM((tm, D_MODEL), F32)] * 2 + [pltpu.VMEM((D_MODEL, D_MODEL), F32)],
        compiler_params=_params(40),
    )(*_pin(d_x1, o_raw, g5, res, gnorm_w, gate, w_out))


def _pool_constants():
    win = np.zeros((POOL_GROUPS, ROW_TILE, ROW_TILE), np.float32)
    inv = np.zeros((POOL_GROUPS, ROW_TILE, 1), np.float32)
    for g, w in enumerate(POOL_WINDOWS):
        for t in range(ROW_TILE):
            base, p = (t // GRID_W) * GRID_W, t % GRID_W
            lo = min(max(p - w // 2, 0), GRID_W)
            hi = min(max(p - w // 2 + w, 0), GRID_W)
            win[g, t, base + lo:base + hi] = 1.0
            inv[g, t, 0] = 1.0 / np.float32(hi - lo)
    return jnp.asarray(win, BF16), jnp.asarray(win.transpose(0, 2, 1), BF16), jnp.asarray(inv, F32)


def _pool_mix(u_ref, wg_ref, win_ref, inv_ref, pooled_scr, yg_scr):
    for g in range(POOL_GROUPS):
        cols = slice(g * POOL_GROUP_DIM, (g + 1) * POOL_GROUP_DIM)
        ug = u_ref[:, cols]
        pooled = _mm_exact_lhs(win_ref[g], ug) * inv_ref[g] - ug
        if pooled_scr is not None:
            pooled_scr[:, cols] = pooled
        yg_scr[:, cols] = _mm(pooled, wg_ref[g])


def _pool_forward_loss(uz, x1, target, gate, w_grp, pool_scale, w_out, final_w):
    tm = ROW_TILE
    win, _, inv = _pool_constants()

    def body(u_ref, z_ref, x_ref, t_ref, gate_ref, wg_ref, ps_ref, w_ref, fw_ref, win_ref, inv_ref,
             dx_ref, loss_ref, dfw_ref, dgate_ref, yg_scr):
        @pl.when(pl.program_id(0) == 0)
        def _():
            loss_ref[...] = jnp.zeros_like(loss_ref)
            dfw_ref[...] = jnp.zeros_like(dfw_ref)
            dgate_ref[...] = jnp.zeros_like(dgate_ref)

        _pool_mix(u_ref, wg_ref, win_ref, inv_ref, None, yg_scr)
        a = yg_scr[...] * ps_ref[...] * _silu(z_ref[...])
        res = _mm(a, w_ref[...])
        x2 = x_ref[...] + gate_ref[...] * res
        r = lax.rsqrt(jnp.mean(x2 * x2, axis=-1, keepdims=True) + EPS)
        xh = x2 * r
        fw = fw_ref[...]
        err = xh * fw - t_ref[...]
        loss_ref[...] += 0.5 * jnp.sum(jnp.mean(err * err, axis=-1, keepdims=True))
        d_y = err * (1.0 / D_MODEL)
        dfw_ref[...] += jnp.sum(d_y * xh, axis=0, keepdims=True)
        d_xh = d_y * fw
        d_x2 = r * (d_xh - xh * jnp.mean(d_xh * xh, axis=-1, keepdims=True))
        dx_ref[...] = d_x2
        dgate_ref[...] += jnp.sum(d_x2 * res, axis=0, keepdims=True)

    tile = pl.BlockSpec((tm, D_MODEL), lambda i: (i, 0))
    vec = pl.BlockSpec((1, D_MODEL), lambda i: (0, 0))
    grp = pl.BlockSpec((POOL_GROUPS, POOL_GROUP_DIM, POOL_GROUP_DIM), lambda i: (0, 0, 0))
    return pl.pallas_call(
        body, name="pool_forward_loss", grid=(SEQ // tm,),
        out_shape=[pltpu.HBM((SEQ, D_MODEL), F32), jax.ShapeDtypeStruct((8, 128), F32),
                   jax.ShapeDtypeStruct((1, D_MODEL), F32), jax.ShapeDtypeStruct((1, D_MODEL), F32)],
        in_specs=[pl.BlockSpec((None, tm, D_MODEL), lambda i: (0, i, 0)), pl.BlockSpec((None, tm, D_MODEL), lambda i: (1, i, 0)),
                  tile, tile, vec, grp, vec, pl.BlockSpec((D_MODEL, D_MODEL), lambda i: (0, 0)), vec, grp,
                  pl.BlockSpec((POOL_GROUPS, ROW_TILE, 1), lambda i: (0, 0, 0))],
        out_specs=[tile, pl.BlockSpec((8, 128), lambda i: (0, 0)), vec, vec],
        scratch_shapes=[pltpu.VMEM((tm, D_MODEL), F32)],
        compiler_params=_params(32),
    )(*_pin(uz, uz, x1, target, gate, w_grp, pool_scale, w_out, final_w, win, inv))


def _pool_backward(d_x2, uz, gate, w_grp, pool_scale, w_out):
    tm = ROW_TILE
    win, win_t, inv = _pool_constants()

    def body(dx_ref, u_ref, z_ref, gate_ref, wg_ref, ps_ref, w_ref, win_ref, wint_ref, inv_ref,
             duz_ref, dw_out, dwg_out, dps_ref, pooled_scr, yg_scr, dyg_scr, dw_ref, dwg_ref):
        @pl.when(pl.program_id(0) == 0)
        def _():
            dw_ref[...] = jnp.zeros_like(dw_ref)
            dwg_ref[...] = jnp.zeros_like(dwg_ref)
            dps_ref[...] = jnp.zeros_like(dps_ref)

        _pool_mix(u_ref, wg_ref, win_ref, inv_ref, pooled_scr, yg_scr)
        z = z_ref[...]
        sz = _silu(z)
        yg = yg_scr[...]
        y = yg * ps_ref[...]
        d_res = (dx_ref[...] * gate_ref[...]).astype(BF16)
        d_a = _mm_nt(d_res, w_ref[...])
        dw_ref[...] += _mm_tn(y * sz, d_res)
        d_y = d_a * sz
        duz_ref[1] = (d_a * y * _dsilu(z)).astype(BF16)
        dps_ref[...] += jnp.sum(d_y * yg, axis=0, keepdims=True)
        dyg_scr[...] = d_y * ps_ref[...]
        for g in range(POOL_GROUPS):
            cols = slice(g * POOL_GROUP_DIM, (g + 1) * POOL_GROUP_DIM)
            d_yg = dyg_scr[:, cols].astype(BF16)
            d_pool = _mm_nt(d_yg, wg_ref[g])
            dwg_ref[g] += _mm_tn(pooled_scr[:, cols], d_yg)
            duz_ref[0, :, cols] = (_mm_exact_lhs(wint_ref[g], d_pool * inv_ref[g]) - d_pool).astype(BF16)

        @pl.when(pl.program_id(0) == SEQ // tm - 1)
        def _():
            dw_out[...] = dw_ref[...].astype(BF16)
            dwg_out[...] = dwg_ref[...].astype(BF16)

    tile = pl.BlockSpec((tm, D_MODEL), lambda i: (i, 0))
    vec = pl.BlockSpec((1, D_MODEL), lambda i: (0, 0))
    mat = pl.BlockSpec((D_MODEL, D_MODEL), lambda i: (0, 0))
    grp = pl.BlockSpec((POOL_GROUPS, POOL_GROUP_DIM, POOL_GROUP_DIM), lambda i: (0, 0, 0))
    return pl.pallas_call(
        body, name="pool_backward", grid=(SEQ // tm,),
        out_shape=[pltpu.HBM((POOL_SECTIONS, SEQ, D_MODEL), BF16), pltpu.HBM((D_MODEL, D_MODEL), BF16),
                   pltpu.HBM((POOL_GROUPS, POOL_GROUP_DIM, POOL_GROUP_DIM), BF16), jax.ShapeDtypeStruct((1, D_MODEL), F32)],
        in_specs=[tile, pl.BlockSpec((None, tm, D_MODEL), lambda i: (0, i, 0)), pl.BlockSpec((None, tm, D_MODEL), lambda i: (1, i, 0)),
                  vec, grp, vec, mat, grp, grp, pl.BlockSpec((POOL_GROUPS, ROW_TILE, 1), lambda i: (0, 0, 0))],
        out_specs=[pl.BlockSpec((POOL_SECTIONS, tm, D_MODEL), lambda i: (0, i, 0)), mat, grp, vec],
        scratch_shapes=[pltpu.VMEM((tm, D_MODEL), F32)] * 3 + [pltpu.VMEM((D_MODEL, D_MODEL), F32),
                                                               pltpu.VMEM((POOL_GROUPS, POOL_GROUP_DIM, POOL_GROUP_DIM), F32)],
        compiler_params=_params(40),
    )(*_pin(d_x2, uz, uz, gate, w_grp, pool_scale, w_out, win, win_t, inv))


def _ln_mod_backward(d_g, w, ctx_tile, xin, nw, scale, d_up, name):
    n_sec, rows, _ = d_g.shape
    n_mod = scale.shape[0]
    skip = n_mod - 1
    tm = ROW_TILE if skip else 2 * ROW_TILE
    n_tiles = rows // tm

    def body(dg_ref, w_ref, *refs):
        c_ref = refs[0] if skip else None
        x_ref, nw_ref, sc_ref, up_ref, dx_ref, dnw_ref, dmod_ref = refs[skip:]
        i = pl.program_id(0)

        @pl.when(i == 0)
        def _():
            dnw_ref[...] = jnp.zeros_like(dnw_ref)

        @pl.when((i == 0) | (i == skip))
        def _():
            dmod_ref[...] = jnp.zeros_like(dmod_ref)

        d_h = _mm_nt(dg_ref[0], w_ref[:, 0:D_MODEL])
        for k in range(1, n_sec):
            d_h = d_h + _mm_nt(dg_ref[k], w_ref[:, k * D_MODEL:(k + 1) * D_MODEL])
        xv = jnp.where(i == 0, c_ref[...], x_ref[...]) if skip else x_ref[...]
        r = lax.rsqrt(jnp.mean(xv * xv, axis=-1, keepdims=True) + EPS)
        xh = xv * r
        nw_row = nw_ref[...]
        dmod_ref[0:1, :] += jnp.sum(d_h, axis=0, keepdims=True)
        dmod_ref[1:2, :] += jnp.sum(d_h * (xh * nw_row), axis=0, keepdims=True)
        d_xn = d_h * (1.0 + sc_ref[...])
        dnw_ref[...] += jnp.sum(d_xn * xh, axis=0, keepdims=True)
        d_xh = d_xn * nw_row

        @pl.when(i >= skip)
        def _():
            dx_ref[...] = up_ref[...] + r * (d_xh - xh * jnp.mean(d_xh * xh, axis=-1, keepdims=True))

    lat = lambda i: (jnp.maximum(i - skip, 0), 0)
    mod_idx = lambda i: (jnp.minimum(i, n_mod - 1), 0, 0)
    return pl.pallas_call(
        body, name=name, grid=(n_tiles,),
        out_shape=[pltpu.HBM((rows - skip * tm, D_MODEL), F32), jax.ShapeDtypeStruct((1, D_MODEL), F32),
                   jax.ShapeDtypeStruct((n_mod, 8, D_MODEL), F32)],
        in_specs=[pl.BlockSpec((n_sec, tm, D_MODEL), lambda i: (0, i, 0)),
                  pl.BlockSpec((D_MODEL, n_sec * D_MODEL), lambda i: (0, 0))]
        + [pl.BlockSpec((tm, D_MODEL), lambda i: (0, 0))] * skip
        + [pl.BlockSpec((tm, D_MODEL), lat),
           pl.BlockSpec((1, D_MODEL), lambda i: (0, 0)),
           pl.BlockSpec((None, 1, D_MODEL), mod_idx),
           pl.BlockSpec((tm, D_MODEL), lat)],
        out_specs=[pl.BlockSpec((tm, D_MODEL), lat), pl.BlockSpec((1, D_MODEL), lambda i: (0, 0)),
                   pl.BlockSpec((None, 8, D_MODEL), mod_idx)],
        compiler_params=_params(48),
    )(*_pin(d_g, w, *([ctx_tile] * skip), xin, nw, scale.reshape(n_mod, 1, D_MODEL), d_up))


def _weight_grad(h, d_g, name):
    n_sec, rows, _ = d_g.shape
    tm = 768 if rows % 768 == 0 else 512
    n_tiles = rows // tm

    def body(h_ref, dg_ref, dw_ref, acc):
        i = pl.program_id(1)
        prod = _mm_tn(h_ref[...], dg_ref[...])

        @pl.when(i == 0)
        def _():
            acc[...] = prod

        @pl.when((i > 0) & (i < n_tiles - 1))
        def _():
            acc[...] += prod

        @pl.when(i == n_tiles - 1)
        def _():
            dw_ref[...] = (acc[...] + prod).astype(BF16)

    return pl.pallas_call(
        body, name=name, grid=(n_sec, n_tiles),
        out_shape=pltpu.HBM((D_MODEL, n_sec * D_MODEL), BF16),
        in_specs=[pl.BlockSpec((tm, D_MODEL), lambda j, i: (i, 0)), pl.BlockSpec((None, tm, D_MODEL), lambda j, i: (j, i, 0))],
        out_specs=pl.BlockSpec((D_MODEL, D_MODEL), lambda j, i: (0, j)),
        scratch_shapes=[pltpu.VMEM((D_MODEL, D_MODEL), F32)],
        compiler_params=_params(32),
    )(*_pin(h, d_g))


def _weight_grad_paired(h, d_g, name):
    n_sec, rows, _ = d_g.shape
    tm = 768
    n_tiles = rows // tm
    half = D_MODEL // 2
    chip_cols = n_sec * D_MODEL // N_CHIPS

    def body(h_ref, dg_ref, q_ref, slots_ref, acc, keep, send, land, send_sems, recv_sems, own_sem):
        j, i = pl.program_id(0), pl.program_id(1)
        x, y, c = _my_place()
        prod = _mm_tn(h_ref[...], dg_ref[...])

        def to_sibling(k):
            return pltpu.make_async_remote_copy(src_ref=send.at[k], dst_ref=land.at[k], send_sem=send_sems.at[k],
                                                recv_sem=recv_sems.at[k], device_id=(x, y, 1 - c), device_id_type=MESH)

        @pl.when(i == 0)
        def _():
            acc[...] = prod

        @pl.when((i > 0) & (i < n_tiles - 1))
        def _():
            acc[...] += prod

        @pl.when(i == n_tiles - 1)
        def _():
            acc[...] += prod
            keep[j] = acc[pl.ds(_al(c * half, half), half), :]
            send[j] = acc[pl.ds(_al((1 - c) * half, half), half), :].astype(BF16)
            to_sibling(j).start()

        @pl.when((j == n_sec - 1) & (i == n_tiles - 1))
        def _():
            for k in range(n_sec):
                to_sibling(k).wait()
                q_ref[:, k * D_MODEL:(k + 1) * D_MODEL] = (keep[k] + land[k].astype(F32)).astype(BF16)
            chip = 2 * x + y
            own = pltpu.make_async_copy(q_ref.at[:, pl.ds(_al(chip * chip_cols, chip_cols), chip_cols)], slots_ref.at[chip], own_sem)
            own.start()
            own.wait()

    return pl.pallas_call(
        body, name=name, grid=(n_sec, n_tiles),
        out_shape=[jax.ShapeDtypeStruct((half, n_sec * D_MODEL), BF16),
                   jax.ShapeDtypeStruct((N_CHIPS, half, chip_cols), BF16)],
        in_specs=[pl.BlockSpec((tm, D_MODEL), lambda j, i: (i, 0)), pl.BlockSpec((None, tm, D_MODEL), lambda j, i: (j, i, 0))],
        out_specs=[pl.BlockSpec((half, n_sec * D_MODEL), lambda j, i: (0, 0)), ANY],
        scratch_shapes=[pltpu.VMEM((D_MODEL, D_MODEL), F32), pltpu.VMEM((n_sec, half, D_MODEL), F32),
                        pltpu.VMEM((n_sec, half, D_MODEL), BF16), pltpu.VMEM((n_sec, half, D_MODEL), BF16),
                        pltpu.SemaphoreType.DMA((n_sec,)), pltpu.SemaphoreType.DMA((n_sec,)), pltpu.SemaphoreType.DMA],
        compiler_params=_params(56),
    )(*_pin(h, d_g))


def _adamw_math(w, g, m, v):
    m = ADAM_B1 * m + (1.0 - ADAM_B1) * g
    v = ADAM_B2 * v + (1.0 - ADAM_B2) * (g * g)
    m_hat = m / (1.0 - ADAM_B1 ** ADAM_STEP)
    v_hat = v / (1.0 - ADAM_B2 ** ADAM_STEP)
    return -ADAM_LR * (m_hat / (jnp.sqrt(v_hat) + ADAM_EPS) + ADAM_WD * w), m, v


def _adamw(w, g, m, v, name):
    rows, cols = w.shape
    tr = rows if rows <= 128 else 128

    def body(w_ref, g_ref, m_ref, v_ref, d_ref, mo_ref, vo_ref):
        d_ref[...], mo_ref[...], vo_ref[...] = _adamw_math(w_ref[...], g_ref[...], m_ref[...], v_ref[...])

    tile = pl.BlockSpec((tr, cols), lambda i: (i, 0))
    return pl.pallas_call(
        body, name=name, grid=(rows // tr,),
        out_shape=[pltpu.HBM((rows, cols), F32)] * 3,
        in_specs=[tile] * 4, out_specs=[tile] * 3,
    )(*_pin(w, g, m, v))


def _small_sums_and_cond_ctx(small, ada_w0, c_ctx, m, v):
    n_cols = ada_w0.shape[1]

    def body(small_ref, w_ref, c_ref, m_ref, v_ref, all_ref, sums_ref, g_ref, d_ref, mo_ref, vo_ref, part_ref, parts_ref,
             s1, r1, l1, s2, r2, l2):
        start_slabs, finish_slabs = _small_gather(small_ref, all_ref, s1, r1, l1)
        start_parts, finish_parts = _small_gather(part_ref, parts_ref, s2, r2, l2)
        start_slabs()
        finish_slabs()
        acc = all_ref[0:SMALL_ROWS, :]
        for dev in range(1, N_DEV):
            acc = acc + all_ref[dev * SMALL_ROWS:(dev + 1) * SMALL_ROWS, :]
        sums_ref[...] = acc
        x, y, _ = _my_place()
        d_modc = jnp.concatenate([acc[6:7, :], acc[7:8, :], acc[8:9, :]], axis=1)
        mine = jnp.zeros((1, n_cols), F32)
        for s in range(N_CHIPS):
            mine = mine + jnp.where(2 * x + y == s, d_modc[:, s * n_cols:(s + 1) * n_cols], 0.0)
        ah, al = _split2(jnp.broadcast_to(mine, (8, n_cols)))
        wh, wl = _split2(w_ref[...])
        nt = lambda a, b: lax.dot_general(a, b, (((1,), (1,)), ((), ())), preferred_element_type=F32)
        part_ref[...] = nt(ah, wh) + nt(al, wh) + nt(ah, wl)
        start_parts()
        finish_parts()
        total = parts_ref[0:1, :]
        for s in range(1, N_CHIPS):
            total = total + parts_ref[16 * s:16 * s + 1, :]
        w = c_ref[...]
        g = total * _dsilu(w)
        g_ref[...] = g
        d_ref[...], mo_ref[...], vo_ref[...] = _adamw_math(w, g, m_ref[...], v_ref[...])

    seven = [pltpu.SemaphoreType.DMA((7,)), pltpu.SemaphoreType.DMA((7,)), pltpu.SemaphoreType.DMA]
    vec = jax.ShapeDtypeStruct((1, D_MODEL), F32)
    return pl.pallas_call(
        body, name="small_sums_and_cond_ctx",
        out_shape=[jax.ShapeDtypeStruct((N_DEV * SMALL_ROWS, D_MODEL), F32), jax.ShapeDtypeStruct((SMALL_ROWS, D_MODEL), F32),
                   vec, vec, vec, vec],
        in_specs=[VMEM] * 5, out_specs=[VMEM] * 6,
        scratch_shapes=[pltpu.VMEM((8, D_MODEL), F32), pltpu.VMEM((N_DEV * 8, D_MODEL), F32)] + seven + seven,
    )(small, ada_w0, c_ctx, m, v)


def _ada_update(cond_t, d_mod, ada_w, m, v):
    n_layers, _, n_cols = ada_w.shape
    tr = ROW_TILE

    def body(c_ref, dm_ref, w_ref, m_ref, v_ref, g_ref, d_ref, mo_ref, vo_ref):
        g = _mm_f32(_silu(c_ref[...]), dm_ref[...])
        g_ref[...] = g
        d_ref[...], mo_ref[...], vo_ref[...] = _adamw_math(w_ref[...], g, m_ref[...], v_ref[...])

    tile = pl.BlockSpec((None, tr, n_cols), lambda l, i: (l, i, 0))
    return pl.pallas_call(
        body, name="ada_update", grid=(n_layers, D_MODEL // tr),
        out_shape=[pltpu.HBM(ada_w.shape, F32)] * 4,
        in_specs=[pl.BlockSpec((tr, 16), lambda l, i: (i, 0)), pl.BlockSpec((None, 16, n_cols), lambda l, i: (l, 0, 0)),
                  tile, tile, tile],
        out_specs=[tile] * 4,
    )(*_pin(cond_t, d_mod, ada_w, m, v))


def _local_step(x2, ctx2, target, mod_mine, mod_ctx, lb_logits, scale_full, w_in_full, late_shards, norm_w, gnorm, final_w):
    row = lambda a: a.reshape(1, -1)
    shift0, scale0, gate0 = (row(a) for a in jnp.split(mod_mine[0], 3))
    shift1, scale1, gate1 = (row(a) for a in jnp.split(mod_mine[1], 3))
    shift_c, scale_c, _ = (row(a) for a in jnp.split(mod_ctx, 3))
    nw0, nw1 = norm_w[0:1], norm_w[1:2]
    scales0 = jnp.concatenate([scale_c, scale0])

    g5, h0 = _ln_mod_matmul(ctx2, x2, nw0, jnp.concatenate([shift_c, shift0]), scales0, w_in_full, "hgrn_in_forward")
    o_raw, full = _gla_forward(g5, lb_logits, late_shards)
    x1, res0 = _hgrn_out_forward(o_raw, g5, x2, gnorm, gate0, full["hgrn_w_out"])
    uz, h1 = _ln_mod_matmul(None, x1, nw1, shift1, scale1, full["pool_w_in"], "pool_in_forward")
    d_x2, loss_part, d_final, d_gate1 = _pool_forward_loss(uz, x1, target, gate1, full["pool_w_grp"], scale_full,
                                                           full["pool_w_out"], final_w)

    d_uz, dw_pool_out, dw_pool_grp, d_pscale = _pool_backward(d_x2, uz, gate1, full["pool_w_grp"], scale_full, full["pool_w_out"])
    d_x1, d_nw1, d_mod1 = _ln_mod_backward(d_uz, full["pool_w_in"], None, x1, nw1, scale1, d_x2, "pool_in_backward")
    dw_pool_in = _weight_grad(h1, d_uz, "pool_in_weight_grad")
    d_o, d_z, dw_hgrn_out, d_gate0, d_gnorm = _hgrn_out_backward(d_x1, o_raw, g5, res0, gnorm, gate0, full["hgrn_w_out"])
    late_grads = {"hgrn_w_out": dw_hgrn_out, "pool_w_in": dw_pool_in, "pool_w_grp": dw_pool_grp, "pool_w_out": dw_pool_out}
    d_g5, d_lb, late_slots = _gla_backward(g5, lb_logits, d_o, d_z, [late_grads[k].astype(BF16) for k in GATHER_LATE])
    dw_hgrn_in, slots0 = _weight_grad_paired(h0, d_g5, "hgrn_in_weight_grad")
    key = GATHER_EARLY[0]
    send_sem, recv_sem, part_thru, slots_thru, token = _scatter_start(dw_hgrn_in, slots0, d_lb, key)
    d_x, d_nw0, d_mod0 = _ln_mod_backward(d_g5, w_in_full, ctx2, x2, nw0 + token[0:1, 0:1], scales0, d_x1, "hgrn_in_backward")
    slots = dict(late_slots)
    pending = (send_sem, recv_sem, part_thru, slots_thru)

    zero = jnp.zeros((1, D_MODEL), F32)
    small = jnp.concatenate([d_mod0[1, 0:2], d_gate0, d_mod1[0, 0:2], d_gate1, d_mod0[0, 0:2], zero, d_nw0, d_nw1, d_gnorm,
                             d_final, d_pscale, d_lb, jnp.broadcast_to(loss_part[0:1, 0:1], (1, D_MODEL)),
                             jnp.zeros((SMALL_ROWS - 17, D_MODEL), F32)], axis=0)
    return {"d_x": d_x, "slots": slots, "pending": pending, "small": small}


def kernel(x, c, ctx, c_ctx, ada_w, ada_b, norm_w, hgrn_w_in, hgrn_lb_logits, hgrn_gnorm_w, hgrn_w_out, pool_w_in, pool_w_grp, pool_scale, pool_w_out, final_norm_w, loss_target, m_c_ctx, m_ada_w, m_ada_b, m_norm_w, m_hgrn_w_in, m_hgrn_lb_logits, m_hgrn_gnorm_w, m_hgrn_w_out, m_pool_w_in, m_pool_w_grp, m_pool_scale, m_pool_w_out, m_final_norm_w, v_c_ctx, v_ada_w, v_ada_b, v_norm_w, v_hgrn_w_in, v_hgrn_lb_logits, v_hgrn_gnorm_w, v_hgrn_w_out, v_pool_w_in, v_pool_w_grp, v_pool_scale, v_pool_w_out, v_final_norm_w):
    xi, yi, ci = _my_place()
    chip = 2 * xi + yi
    dev = 4 * xi + 2 * yi + ci
    ada_cols = ada_w.shape[2]
    lb_cols = hgrn_lb_logits.shape[2]
    ps_cols = pool_scale.shape[1]
    row = lambda a: a.reshape(1, -1)

    def chip_cols(a, n):
        return lax.dynamic_slice_in_dim(a, chip * n, n, axis=a.ndim - 1)

    def from_chips(g, rows_per_dev, take):
        return jnp.concatenate([g[2 * s * rows_per_dev:2 * s * rows_per_dev + take] for s in range(N_CHIPS)], axis=1)

    first = jnp.concatenate([jnp.broadcast_to(c, (8, D_MODEL)), jnp.pad(hgrn_lb_logits[0], ((0, 6), (0, 0))),
                             jnp.pad(pool_scale, ((0, 7), (0, 0)))], axis=1)
    shards = {"hgrn_w_in": hgrn_w_in[0], "hgrn_w_out": hgrn_w_out[0], "pool_w_in": pool_w_in[0],
              "pool_w_grp": pool_w_grp[0], "pool_w_out": pool_w_out[0]}
    first_all, parts_all, w_in_full = _prologue(first, row(c_ctx), ada_w, chip_cols(ada_b, ada_cols),
                                                shards[GATHER_EARLY[0]].astype(BF16))
    cond_all = first_all[::8, :D_MODEL]
    lb_logits = from_chips(first_all[:, D_MODEL:D_MODEL + lb_cols], 8, 2)
    scale_full = from_chips(first_all[:, D_MODEL + lb_cols:], 8, 1)
    cond_rows = jnp.concatenate([cond_all, row(c_ctx), jnp.zeros((7, D_MODEL), F32)], axis=0)
    mod_all = from_chips(parts_all, 32, 32).reshape(2, 16, 3 * D_MODEL)
    mod_mine = lax.dynamic_index_in_dim(mod_all, dev, axis=1, keepdims=False)

    loc = _local_step(x[0], ctx[0], loss_target[0], mod_mine, mod_all[0, 8], lb_logits, scale_full, w_in_full,
                      [shards[k].astype(BF16) for k in GATHER_LATE], norm_w, hgrn_gnorm_w, row(final_norm_w))

    small_all, sums, g_c, d_c, m_c, v_c = _small_sums_and_cond_ctx(loc["small"], ada_w[0], row(c_ctx), row(m_c_ctx), row(v_c_ctx))
    loss = sums[16, 0]

    def reduce_scattered(slots, names, name):
        return dict(zip(names, _sum_and_exchange([slots[k] for k in names], names, name)))

    big_grads = reduce_scattered(loc["slots"], GATHER_LATE, "exchange_halves_late")

    out = {}

    def update(name, w, g, m, v):
        shape = w.shape
        w2, g2, m2, v2 = (a.reshape(-1, shape[-1]) for a in (w, g, m, v))
        d, mn, vn = _adamw(w2, g2, m2, v2, "adamw_" + name)
        out[name] = tuple(a.reshape(shape) for a in (g2, d, mn, vn))

    moments = {"hgrn_w_in": (m_hgrn_w_in, v_hgrn_w_in), "hgrn_w_out": (m_hgrn_w_out, v_hgrn_w_out),
               "pool_w_in": (m_pool_w_in, v_pool_w_in), "pool_w_grp": (m_pool_w_grp, v_pool_w_grp),
               "pool_w_out": (m_pool_w_out, v_pool_w_out)}
    weights = {"hgrn_w_in": hgrn_w_in, "hgrn_w_out": hgrn_w_out, "pool_w_in": pool_w_in, "pool_w_grp": pool_w_grp,
               "pool_w_out": pool_w_out}
    for k in GATHER_LATE:
        update(k, weights[k], big_grads[k], *moments[k])

    g_ada_b = jnp.stack([(sums[0:3] + sums[6:9]).reshape(-1), sums[3:6].reshape(-1)])
    update("ada_b", ada_b, g_ada_b, m_ada_b, v_ada_b)
    update("norm_w", norm_w, sums[9:11], m_norm_w, v_norm_w)
    update("hgrn_gnorm_w", hgrn_gnorm_w, sums[11:12], m_hgrn_gnorm_w, v_hgrn_gnorm_w)
    update("final_norm_w", row(final_norm_w), sums[12:13], row(m_final_norm_w), row(v_final_norm_w))
    update("pool_scale", pool_scale, chip_cols(sums[13:14], ps_cols), m_pool_scale, v_pool_scale)
    update("hgrn_lb_logits", hgrn_lb_logits, chip_cols(sums[14:16], lb_cols)[None], m_hgrn_lb_logits, v_hgrn_lb_logits)

    per_dev = small_all.reshape(N_DEV, SMALL_ROWS, D_MODEL)
    pad7 = jnp.zeros((7, 3 * D_MODEL), F32)
    dm0 = jnp.concatenate([per_dev[:, 0:3].reshape(N_DEV, -1), sums[6:9].reshape(1, -1), pad7], axis=0)
    dm1 = jnp.concatenate([per_dev[:, 3:6].reshape(N_DEV, -1), jnp.zeros((8, 3 * D_MODEL), F32)], axis=0)
    d_mod = chip_cols(jnp.stack([dm0, dm1]), ada_cols)
    out["ada_w"] = _ada_update(cond_rows.T, d_mod, ada_w, m_ada_w, v_ada_w)

    out["c_ctx"] = tuple(a.reshape(-1) for a in (g_c, d_c, m_c, v_c))
    out["final_norm_w"] = tuple(a.reshape(-1) for a in out["final_norm_w"])

    done = [g_c, out["ada_w"][1]] + [out[k][1] for k in GATHER_LATE]
    key = GATHER_EARLY[0]
    _, early = _scatter_wait(*loc["pending"], done, key)
    big_grads = reduce_scattered({key: early}, GATHER_EARLY, "exchange_halves_early")
    for k in GATHER_EARLY:
        update(k, weights[k], big_grads[k], *moments[k])

    names = ["c_ctx", "ada_w", "ada_b", "norm_w", "hgrn_w_in", "hgrn_lb_logits", "hgrn_gnorm_w", "hgrn_w_out", "pool_w_in",
             "pool_w_grp", "pool_scale", "pool_w_out", "final_norm_w"]
    return (loss, loc["d_x"][None], *[out[k][0] for k in names], *[out[k][1] for k in names], *[out[k][2] for k in names],
            *[out[k][3] for k in names])
```

```python
import functools

import numpy as np
import jax
import jax.numpy as jnp
from jax import lax
from jax.experimental import pallas as pl
from jax.experimental.pallas import tpu as pltpu

F32 = jnp.float32
BF16 = jnp.bfloat16

D_MODEL = 1024
SEQ = 2048
CTX_LEN = 256
ROWS_ALL = CTX_LEN + SEQ
HEADS = 8
HEAD_DIM = 128
CHUNK = 64
N_CTX_CHUNKS = CTX_LEN // CHUNK
N_LAT_CHUNKS = SEQ // CHUNK
N_CHUNKS = N_CTX_CHUNKS + N_LAT_CHUNKS
GRID_W = 64
POOL_WINDOWS = (2, 4, 8, 16)
POOL_GROUPS = 4
POOL_GROUP_DIM = 256
HGRN_SECTIONS = 5
POOL_SECTIONS = 2
EPS = 1e-6
N_DEV = 8
N_CHIPS = 4
ROW_TILE = 256
POOL_TILE = 512
SMALL_ROWS = 24

ADAM_LR = 0.001
ADAM_B1 = 0.9
ADAM_B2 = 0.999
ADAM_EPS = 1e-08
ADAM_WD = 0.01
ADAM_STEP = 10

MESH = pl.DeviceIdType.MESH
MIB = 1 << 20
ANY = pl.BlockSpec(memory_space=pl.ANY)
VMEM = pl.BlockSpec(memory_space=pltpu.VMEM)


def _params(vmem_mib=None):
    if vmem_mib is None:
        return pltpu.CompilerParams()
    return pltpu.CompilerParams(vmem_limit_bytes=vmem_mib * MIB)


def _pin(*operands):
    return [pltpu.with_memory_space_constraint(a, pltpu.HBM) if a.size * a.dtype.itemsize >= MIB else a for a in operands]


def _sig(a):
    return 0.5 * jnp.tanh(0.5 * a) + 0.5


def _silu(a):
    return a * _sig(a)


def _dsilu(a):
    s = _sig(a)
    return s * (1.0 + a * (1.0 - s))


def _mm(a, b):
    return jnp.dot(a.astype(BF16), b.astype(BF16), preferred_element_type=F32)


def _mm_nt(a, b):
    return lax.dot_general(a.astype(BF16), b.astype(BF16), (((1,), (1,)), ((), ())), preferred_element_type=F32)


def _mm_tn(a, b):
    return lax.dot_general(a.astype(BF16), b.astype(BF16), (((0,), (0,)), ((), ())), preferred_element_type=F32)


def _split2(a):
    hi = a.astype(BF16)
    lo = (a - hi.astype(F32)).astype(BF16)
    return hi, lo


def _mm_exact_lhs(m_bf, a):
    hi, lo = _split2(a)
    return jnp.dot(m_bf, hi, preferred_element_type=F32) + jnp.dot(m_bf, lo, preferred_element_type=F32)


def _mm_f32(a, b):
    ah, al = _split2(a)
    bh, bl = _split2(b)
    return (jnp.dot(ah, bh, preferred_element_type=F32) + jnp.dot(al, bh, preferred_element_type=F32)
            + jnp.dot(ah, bl, preferred_element_type=F32))


def _my_place():
    return lax.axis_index("x"), lax.axis_index("y"), lax.axis_index("c")


def _small_gather(x_ref, out_ref, send_sems, recv_sems, local_sem):
    m_per = x_ref.shape[0]
    x, y, c = _my_place()
    me, sibling = (x, y, c), (x, y, 1 - c)
    chips = [(1 - x, y), (x, 1 - y), (1 - x, 1 - y)]

    def rows(px, py, pc):
        return out_ref.at[pl.ds((4 * px + 2 * py + pc) * m_per, m_per), :]

    def copy(k, block, to, src=None):
        return pltpu.make_async_remote_copy(
            src_ref=rows(*block) if src is None else src, dst_ref=rows(*block),
            send_sem=send_sems.at[k], recv_sem=recv_sems.at[k], device_id=to, device_id_type=MESH)

    def mine():
        return pltpu.make_async_copy(x_ref, rows(*me), local_sem)

    def first():
        return [copy(0, me, sibling, src=x_ref)] + [copy(1 + j, me, (*chip, c), src=x_ref) for j, chip in enumerate(chips)]

    def start():
        mine().start()
        for cp in first():
            cp.start()

    def finish():
        passed = [copy(4 + j, (*chip, c), sibling) for j, chip in enumerate(chips)]
        for j, chip in enumerate(chips):
            copy(1 + j, (*chip, c), me).wait_recv()
            passed[j].start()
        copy(0, sibling, me).wait_recv()
        for j, chip in enumerate(chips):
            copy(4 + j, (*chip, 1 - c), me).wait_recv()
        for cp in first() + passed:
            cp.wait_send()
        mine().wait()

    return start, finish


W_SPECS = {
    "hgrn_w_in": ((D_MODEL, 5 * D_MODEL), (1, 1280, 0, 512)),
    "hgrn_w_out": ((D_MODEL, D_MODEL), (0, 256, 0, 128)),
    "pool_w_in": ((D_MODEL, 2 * D_MODEL), (1, 512, 0, 512)),
    "pool_w_grp": ((POOL_GROUPS, POOL_GROUP_DIM, POOL_GROUP_DIM), (1, 64, 1, 32)),
    "pool_w_out": ((D_MODEL, D_MODEL), (0, 256, 0, 128)),
}

def _al(v, m):
    return pl.multiple_of(v, m)


def _region(ref, spec, chip, half):
    ca, cn, ha, hn = spec
    idx = [slice(None)] * len(ref.shape)
    if ca == ha:
        if half is None:
            idx[ca] = pl.ds(_al(chip * cn, cn), cn)
        else:
            idx[ca] = pl.ds(_al(chip * cn + half * hn, hn), hn)
    else:
        idx[ca] = pl.ds(_al(chip * cn, cn), cn)
        if half is not None:
            idx[ha] = pl.ds(_al(half * hn, hn), hn)
    return ref.at[tuple(idx)]


def _half_of(ref, spec, half):
    _, _, ha, hn = spec
    idx = [slice(None)] * len(ref.shape)
    idx[ha] = pl.ds(_al(half * hn, hn), hn)
    return ref.at[tuple(idx)]


PIECE_BYTES = 256 * 1024


def _pieces(ref):
    lead = ref.shape[0]
    want = (int(np.prod(ref.shape)) * ref.dtype.itemsize) // PIECE_BYTES
    n = max([1] + [k for k in range(1, want + 1) if lead % k == 0 and (lead // k) % 16 == 0])
    rows = lead // n
    return [ref.at[pl.ds(i * rows, rows)] for i in range(n)]


def _half_shape(name):
    full, (ca, cn, ha, hn) = W_SPECS[name]
    shp = list(full)
    shp[ca] = cn
    shp[ha] = hn
    return tuple(shp)


def _gather_two_level(names, sh, land, full, send_sems, recv_sems, local_sems):
    n = len(names)
    specs = [W_SPECS[k][1] for k in names]
    x, y, c = _my_place()
    chip_me = 2 * x + y
    sibling = (x, y, 1 - c)
    chips = [(1 - x, y), (x, 1 - y), (1 - x, 1 - y)]

    def remote(k, src, dst, to):
        return pltpu.make_async_remote_copy(src_ref=src, dst_ref=dst, send_sem=send_sems.at[k], recv_sem=recv_sems.at[k],
                                            device_id=to, device_id_type=MESH)

    def three_halves(a):
        ca, cn, ha, hn = specs[a]
        idx = [slice(None)] * len(land[a].shape)
        if ca == ha:
            idx[ca] = pl.ds(0, 3 * hn)
        else:
            idx[ca], idx[ha] = pl.ds(0, 3 * cn), pl.ds(0, hn)
        return land[a].at[tuple(idx)]

    def own(a):
        return pltpu.make_async_copy(sh[a], _region(land[a], specs[a], chip_me, None), local_sems.at[a])

    def out(a):
        return pltpu.make_async_copy(land[a], full[a], local_sems.at[a])

    def start():
        for a in range(n):
            own(a).start()
            for px, py in chips:
                remote(a, _half_of(sh[a], specs[a], c), _region(land[a], specs[a], chip_me, c), (px, py, c)).start()

    def forward():
        for a in range(n):
            remote(a, three_halves(a), three_halves(a), sibling).wait_recv()
            for px, py in chips:
                landed = _region(land[a], specs[a], 2 * px + py, c)
                remote(n + a, landed, landed, sibling).start()

    def finish():
        for a in range(n):
            remote(n + a, three_halves(a), three_halves(a), sibling).wait_recv()
            remote(a, three_halves(a), three_halves(a), sibling).wait_send()
            remote(n + a, three_halves(a), three_halves(a), sibling).wait_send()
            own(a).wait()
        for a in range(n):
            out(a).start()
        for a in range(n):
            out(a).wait()

    return start, forward, finish


def _scatter_direct(names, part, slots, send_sems, recv_sems, local_sems):
    specs = [W_SPECS[k][1] for k in names]
    x, y, c = _my_place()
    dev_me = 4 * x + 2 * y + c

    def local(a):
        return pltpu.make_async_copy(_region(part[a], specs[a], 2 * x + y, c), slots[a].at[dev_me], local_sems.at[a])

    def start():
        for a in range(len(names)):
            local(a).start()
            for flip in range(1, N_DEV):
                tx = 1 - x if flip >> 2 else x
                ty = 1 - y if (flip >> 1) & 1 else y
                tc = 1 - c if flip & 1 else c
                pltpu.make_async_remote_copy(src_ref=_region(part[a], specs[a], 2 * tx + ty, tc), dst_ref=slots[a].at[dev_me],
                                             send_sem=send_sems.at[a], recv_sem=recv_sems.at[a], device_id=(tx, ty, tc),
                                             device_id_type=MESH).start()

    def wait():
        for a in range(len(names)):
            seven = slots[a].at[pl.ds(0, N_DEV - 1)]
            pltpu.make_async_remote_copy(src_ref=seven, dst_ref=seven, send_sem=send_sems.at[a], recv_sem=recv_sems.at[a],
                                         device_id=(x, y, c), device_id_type=MESH).wait()
            local(a).wait()

    return start, wait


HBM_SPEC = pl.BlockSpec(memory_space=pltpu.HBM)
SEM_SPEC = pl.BlockSpec(memory_space=pltpu.SEMAPHORE)
SPLIT_EFFECT = pltpu.SideEffectType.DATAFLOW_SIDE_EFFECTING


def _scatter_start(part, slots, after, name_key):
    _, cols, _, _ = W_SPECS[name_key][1]

    def body(part_ref, slots_ref, after_ref, send_sem, recv_sem, part_thru, slots_thru, token):
        x, y, c = _my_place()
        for tx, ty in ((1 - x, y), (x, 1 - y), (1 - x, 1 - y)):
            src = part_ref.at[:, pl.ds(_al((2 * tx + ty) * cols, cols), cols)]
            pltpu.make_async_remote_copy(src_ref=src, dst_ref=slots_ref.at[2 * x + y], send_sem=send_sem, recv_sem=recv_sem,
                                         device_id=(tx, ty, c), device_id_type=MESH).start()
        token[...] = jnp.zeros_like(token)

    return pl.pallas_call(
        body, name="scatter_start_" + name_key,
        out_shape=(pltpu.SemaphoreType.DMA(()), pltpu.SemaphoreType.DMA(()), pltpu.HBM(part.shape, part.dtype),
                   pltpu.HBM(slots.shape, slots.dtype), jax.ShapeDtypeStruct((8, 128), F32)),
        in_specs=(HBM_SPEC, HBM_SPEC, ANY), out_specs=(SEM_SPEC, SEM_SPEC, HBM_SPEC, HBM_SPEC, VMEM),
        input_output_aliases={0: 2, 1: 3},
        compiler_params=pltpu.CompilerParams(has_side_effects=SPLIT_EFFECT),
    )(pltpu.with_memory_space_constraint(part, pltpu.HBM), pltpu.with_memory_space_constraint(slots, pltpu.HBM), after)


def _scatter_wait(send_sem, recv_sem, part_thru, slots_thru, after, name_key):
    def body(part_ref, slots_ref, send_sem, recv_sem, *rest):
        x, y, c = _my_place()
        landed = slots_ref.at[pl.ds(0, N_CHIPS - 1)]
        copy = pltpu.make_async_remote_copy(src_ref=landed, dst_ref=landed, send_sem=send_sem, recv_sem=recv_sem,
                                            device_id=(x, y, c), device_id_type=MESH)
        copy.wait_send()
        copy.wait_recv()

    return pl.pallas_call(
        body, name="scatter_wait_" + name_key,
        out_shape=(pltpu.HBM(part_thru.shape, part_thru.dtype), pltpu.HBM(slots_thru.shape, slots_thru.dtype)),
        in_specs=(HBM_SPEC, HBM_SPEC, SEM_SPEC, SEM_SPEC) + (ANY,) * len(after), out_specs=(HBM_SPEC, HBM_SPEC),
        input_output_aliases={0: 0, 1: 1},
        compiler_params=pltpu.CompilerParams(has_side_effects=SPLIT_EFFECT),
    )(part_thru, slots_thru, send_sem, recv_sem, *after)


def _comm_sems(n):
    return [pltpu.SemaphoreType.DMA((n,)), pltpu.SemaphoreType.DMA((n,)), pltpu.SemaphoreType.DMA((n,))]


GATHER_EARLY = ("hgrn_w_in",)
GATHER_LATE = ("hgrn_w_out", "pool_w_in", "pool_w_grp", "pool_w_out")


def _weight_gather(sh, full, spec, send_sems, recv_sems, local_sem):
    x, y, c = _my_place()
    chip_me = 2 * x + y
    sibling = (x, y, 1 - c)
    chips = [(1 - x, y), (x, 1 - y), (1 - x, 1 - y)]

    def remote(k, src, dst, to):
        return pltpu.make_async_remote_copy(src_ref=src, dst_ref=dst, send_sem=send_sems.at[k], recv_sem=recv_sems.at[k],
                                            device_id=to, device_id_type=MESH)

    def own():
        return pltpu.make_async_copy(sh, _region(full, spec, chip_me, None), local_sem)

    def direct():
        return [remote(j, _half_of(sh, spec, c), _region(full, spec, chip_me, c), (px, py, c)) for j, (px, py) in enumerate(chips)]

    def start():
        own().start()
        for cp in direct():
            cp.start()

    def finish():
        passed = []
        for j, (px, py) in enumerate(chips):
            landed = _region(full, spec, 2 * px + py, c)
            remote(j, landed, landed, (px, py, c)).wait_recv()
            passed.append(remote(3 + j, landed, landed, sibling))
            passed[-1].start()
        for j, (px, py) in enumerate(chips):
            other = _region(full, spec, 2 * px + py, 1 - c)
            remote(3 + j, other, other, sibling).wait_recv()
        for cp in direct() + passed:
            cp.wait_send()
        own().wait()

    return start, finish


def _prologue(first, c_ctx, ada_w, ada_b_cols, w_shard):
    n_layers, _, n_cols = ada_w.shape
    key = GATHER_EARLY[0]
    spec = W_SPECS[key][1]
    m_first = first.shape[0]

    def body(first_ref, cctx_ref, adaw_ref, adab_ref, sh_ref, first_all, parts_all, full_ref, parts_scr,
             s1, r1, l1, s2, r2, l2, ws, wr, wl):
        start_first, finish_first = _small_gather(first_ref, first_all, s1, r1, l1)
        start_parts, finish_parts = _small_gather(parts_scr, parts_all, s2, r2, l2)
        start_weight, finish_weight = _weight_gather(sh_ref, full_ref, spec, ws, wr, wl)
        start_first()
        start_weight()
        finish_first()
        cond = jnp.concatenate([first_all[m_first * d:m_first * d + 1, 0:D_MODEL] for d in range(N_DEV)]
                               + [cctx_ref[...], jnp.zeros((16 - N_DEV - 1, D_MODEL), F32)], axis=0)
        act = _silu(cond)
        for i in range(n_layers):
            parts_scr[16 * i:16 * (i + 1), :] = _mm_f32(act, adaw_ref[i]) + adab_ref[i]
        start_parts()
        finish_parts()
        finish_weight()

    seven = [pltpu.SemaphoreType.DMA((7,)), pltpu.SemaphoreType.DMA((7,)), pltpu.SemaphoreType.DMA]
    return pl.pallas_call(
        body, name="prologue",
        out_shape=[jax.ShapeDtypeStruct((N_DEV * m_first, first.shape[1]), F32),
                   jax.ShapeDtypeStruct((N_DEV * 16 * n_layers, n_cols), F32), jax.ShapeDtypeStruct(W_SPECS[key][0], BF16)],
        in_specs=[VMEM] * 5, out_specs=[VMEM] * 3,
        scratch_shapes=[pltpu.VMEM((16 * n_layers, n_cols), F32)] + seven + seven
        + [pltpu.SemaphoreType.DMA((6,)), pltpu.SemaphoreType.DMA((6,)), pltpu.SemaphoreType.DMA],
        compiler_params=_params(48),
    )(first, c_ctx, ada_w, ada_b_cols.reshape(n_layers, 1, n_cols), w_shard)


def _slot_shapes(names):
    return [jax.ShapeDtypeStruct((N_DEV,) + _half_shape(k), BF16) for k in names]


def _sum_and_exchange(slots, names, name):
    n = len(names)
    specs = [W_SPECS[k][1] for k in names]

    def shard_shape(k):
        shp = list(_half_shape(k))
        shp[W_SPECS[k][1][2]] *= 2
        return tuple(shp)

    def body(*refs):
        slot, out, half = refs[:n], refs[n:2 * n], refs[2 * n:3 * n]
        send_sems, recv_sems, local_sems = refs[3 * n:]
        x, y, c = _my_place()
        sibling = (x, y, 1 - c)
        for a in range(n):
            acc = slot[a][0].astype(F32)
            for d in range(1, slot[a].shape[0]):
                acc = acc + slot[a][d].astype(F32)
            half[a][...] = acc

        def remote(a, src, dst):
            return pltpu.make_async_remote_copy(src_ref=src, dst_ref=dst, send_sem=send_sems.at[a], recv_sem=recv_sems.at[a],
                                                device_id=sibling, device_id_type=MESH)

        local = [pltpu.make_async_copy(half[a], _half_of(out[a], specs[a], c), local_sems.at[a]) for a in range(n)]
        for cp in local:
            cp.start()
        for a in range(n):
            mine = _half_of(out[a], specs[a], c)
            for src, dst in zip(_pieces(half[a]), _pieces(mine)):
                remote(a, src, dst).start()
        for a in range(n):
            theirs = _half_of(out[a], specs[a], 1 - c)
            remote(a, theirs, theirs).wait_recv()
        for a in range(n):
            remote(a, half[a], half[a]).wait_send()
        for cp in local:
            cp.wait()

    return pl.pallas_call(
        body, name=name,
        out_shape=[jax.ShapeDtypeStruct(shard_shape(k), F32) for k in names],
        in_specs=[VMEM] * n, out_specs=[VMEM] * n,
        scratch_shapes=[pltpu.VMEM(_half_shape(k), F32) for k in names]
        + [pltpu.SemaphoreType.DMA((n,)), pltpu.SemaphoreType.DMA((n,)), pltpu.SemaphoreType.DMA((n,))],
        compiler_params=_params(40),
    )(*slots)


def _ln_mod_matmul(ctx_tile, xin, nw, shift, scale, w, name):
    n_mod = shift.shape[0]
    skip = n_mod - 1
    tm = ROW_TILE if skip else 2 * ROW_TILE
    rows = xin.shape[0] + skip * tm
    n_sec = w.shape[1] // D_MODEL

    def body(*refs):
        c_ref = refs[0] if skip else None
        x_ref, nw_ref, sh_ref, sc_ref, w_ref, g_ref, h_ref = refs[skip:]
        xv = jnp.where(pl.program_id(0) == 0, c_ref[...], x_ref[...]) if skip else x_ref[...]
        r = lax.rsqrt(jnp.mean(xv * xv, axis=-1, keepdims=True) + EPS)
        h_f32 = (xv * r * nw_ref[...]) * (1.0 + sc_ref[...]) + sh_ref[...]
        h = h_f32.astype(BF16)
        h_ref[...] = h
        for k in range(n_sec):
            g_ref[k] = jnp.dot(h, w_ref[:, k * D_MODEL:(k + 1) * D_MODEL], preferred_element_type=F32)

    mod_spec = pl.BlockSpec((None, 1, D_MODEL), lambda i: (jnp.minimum(i, n_mod - 1), 0, 0))
    return pl.pallas_call(
        body, name=name, grid=(rows // tm,),
        out_shape=[pltpu.HBM((n_sec, rows, D_MODEL), F32), pltpu.HBM((rows, D_MODEL), BF16)],
        in_specs=[pl.BlockSpec((tm, D_MODEL), lambda i: (0, 0))] * skip
        + [pl.BlockSpec((tm, D_MODEL), lambda i: (jnp.maximum(i - skip, 0), 0)),
           pl.BlockSpec((1, D_MODEL), lambda i: (0, 0)),
           mod_spec, mod_spec,
           pl.BlockSpec((D_MODEL, n_sec * D_MODEL), lambda i: (0, 0))],
        out_specs=[pl.BlockSpec((n_sec, tm, D_MODEL), lambda i: (0, i, 0)),
                   pl.BlockSpec((tm, D_MODEL), lambda i: (i, 0))],
        compiler_params=_params(48),
    )(*_pin(*([ctx_tile] * skip), xin, nw, shift.reshape(n_mod, 1, D_MODEL), scale.reshape(n_mod, 1, D_MODEL), w))


def _gates(fpre, lb):
    sg = _sig(fpre)
    f = lb + (1.0 - lb) * sg
    return sg, f, 1.0 - f, jnp.log(f)


G_SPEC = lambda sec: pl.BlockSpec((None, ROWS_ALL, HEAD_DIM), lambda h, sec=sec: (sec, 0, h))


GROUP = 2 * CHUNK


def _group_masks():
    rid, cid = np.arange(GROUP)[:, None], np.arange(GROUP)[None, :]
    same = (rid >= CHUNK) == (cid >= CHUNK)
    sums, back = [], []
    for rev in (False, True):
        causal = (cid >= rid) if rev else (cid <= rid)
        anti = (cid <= rid) if rev else (cid >= rid)
        sums.append(same & causal)
        back.append(np.concatenate([same & anti, same & ~anti], axis=1))
    sums, back = np.stack(sums).astype(np.float32), np.stack(back).astype(np.float32)
    return jnp.asarray(sums, BF16), jnp.asarray(back, BF16), jnp.asarray(sums, F32)


MASK_SPECS = [pl.BlockSpec((2, GROUP, GROUP), lambda h: (0, 0, 0)), pl.BlockSpec((2, GROUP, 2 * GROUP), lambda h: (0, 0, 0)),
              pl.BlockSpec((2, GROUP, GROUP), lambda h: (0, 0, 0))]


def _group_sum(m_bf, a):
    hi, lo = _split2(a)
    r = jnp.dot(m_bf, jnp.concatenate([hi, lo], axis=1), preferred_element_type=F32)
    return r[:, :HEAD_DIM] + r[:, HEAD_DIM:]


def _chunk_row(a, pos):
    return jnp.concatenate([jnp.broadcast_to(a[c * CHUNK + pos:c * CHUNK + pos + 1, :], (CHUNK, HEAD_DIM)) for c in range(2)], axis=0)


def _by_chunk(a, second):
    return jnp.concatenate([jnp.where(second, 0.0, a), jnp.where(second, a, 0.0)], axis=1)


def _own_block(r):
    return jnp.concatenate([r[0:CHUNK, 0:HEAD_DIM], r[CHUNK:GROUP, HEAD_DIM:2 * HEAD_DIM]], axis=0)


def _scan_step_of(row_chunk, rev, latent):
    if not rev:
        return row_chunk
    return (N_CHUNKS + N_CTX_CHUNKS - 1 - row_chunk) if latent else (N_CTX_CHUNKS - 1 - row_chunk)


def _group_rows(i, j, per_step, latent):
    base = CTX_LEN if latent else 0
    return pl.multiple_of(base + (i * per_step + j) * GROUP, GROUP)


GROUPS_PER_STEP = 8
GROUPS_PER_GRAD_STEP = 4


def _gla_forward(g5, lb_logits, late_shards):
    q_scale = HEAD_DIM ** -0.5
    per_lat, per_ctx = GROUPS_PER_STEP, min(GROUPS_PER_STEP, CTX_LEN // GROUP)
    n_late = len(GATHER_LATE)

    def body(ff_ref, fb_ref, v_ref, q_ref, lg_ref, sums_ref, _, keep_ref, *rest):
        shard_refs, o_ref, full_refs = rest[:n_late], rest[n_late], rest[n_late + 1:2 * n_late + 1]
        st_ref, decay_ref, qt_ref = rest[2 * n_late + 1:2 * n_late + 4]
        land_refs = rest[2 * n_late + 4:3 * n_late + 4]
        start_gather, forward_gather, finish_gather = _gather_two_level(GATHER_LATE, shard_refs, land_refs, full_refs,
                                                                        *rest[3 * n_late + 4:])

        @pl.when(pl.program_id(0) == 0)
        def _():
            start_gather()

        @pl.when(pl.program_id(0) == HEADS - 1)
        def _():
            forward_gather()

        second = lax.broadcasted_iota(jnp.int32, (GROUP, HEAD_DIM), 0) >= CHUNK
        for rev in (False, True):
            f_ref = fb_ref if rev else ff_ref
            lb = _sig(lg_ref[1:2, :] if rev else lg_ref[0:1, :])
            d = 1 if rev else 0
            last = 0 if rev else CHUNK - 1
            mid = CHUNK // 2 if rev else CHUNK // 2 - 1

            def local_step(i, carry, latent, per):
                r0s = [_group_rows(i, j, per, latent) for j in range(per)]
                rows = [pl.ds(r0, GROUP) for r0 in r0s]
                gates = [_gates(f_ref[r, :], lb) for r in rows]
                vs = [v_ref[r, :] for r in rows]
                bs = [_group_sum(sums_ref[d], g[3]) for g in gates]
                bls = [_chunk_row(b, last) for b in bs]
                ups = [_mm_tn(v, _by_chunk(g[2] * jnp.exp(bl - b), second)) for v, g, b, bl in zip(vs, gates, bs, bls)]
                if latent:
                    qs = [_silu(q_ref[r, :]) * q_scale for r in rows]
                    bms = [_chunk_row(b, mid) for b in bs]
                    a_s = [_mm_nt(q * jnp.exp(b - bm), g[2] * jnp.exp(bm - b)) for q, g, b, bm in zip(qs, gates, bs, bms)]
                    outs = [_mm(a * keep_ref[d], v) for a, v in zip(a_s, vs)]
                for j in range(per):
                    for c in range(2):
                        step = _scan_step_of(r0s[j] // CHUNK + c, rev, latent)
                        decay_ref[step] = jnp.exp(bls[j][c * CHUNK:c * CHUNK + 1, :])
                        st_ref[step] = ups[j][:, c * HEAD_DIM:(c + 1) * HEAD_DIM]
                    if latent:
                        orow = pl.ds(pl.multiple_of(r0s[j] - CTX_LEN, GROUP), GROUP)
                        qt_ref[orow, :] = (qs[j] * jnp.exp(bs[j])).astype(BF16)
                        if rev:
                            o_ref[orow, :] += outs[j]
                        else:
                            o_ref[orow, :] = outs[j]
                return carry

            lax.fori_loop(0, CTX_LEN // (per_ctx * GROUP), functools.partial(local_step, latent=False, per=per_ctx), 0)
            lax.fori_loop(0, SEQ // (per_lat * GROUP), functools.partial(local_step, latent=True, per=per_lat), 0)

            def scan_step(t, st):
                update = st_ref[t]
                st_ref[t] = st
                return st * decay_ref[t] + update

            lax.fori_loop(0, N_CHUNKS, scan_step, jnp.zeros((HEAD_DIM, HEAD_DIM), F32), unroll=2)

            def inter_step(i, carry):
                r0s = [_group_rows(i, j, per_lat, True) for j in range(per_lat)]
                orows = [pl.ds(pl.multiple_of(r0 - CTX_LEN, GROUP), GROUP) for r0 in r0s]
                states = [jnp.concatenate([st_ref[_scan_step_of(r0 // CHUNK + c, rev, True)] for c in range(2)], axis=0)
                          for r0 in r0s]
                prods = [lax.dot_general(qt_ref[orow, :], s.astype(BF16), (((1,), (1,)), ((), ())), preferred_element_type=F32)
                         for orow, s in zip(orows, states)]
                for orow, r in zip(orows, prods):
                    o_ref[orow, :] += _own_block(r)
                return carry

            lax.fori_loop(0, SEQ // (per_lat * GROUP), inter_step, 0)

        @pl.when(pl.program_id(0) == HEADS - 1)
        def _():
            finish_gather()

    outs = pl.pallas_call(
        body, name="gla_forward", grid=(HEADS,),
        out_shape=[pltpu.HBM((SEQ, D_MODEL), F32)] + [jax.ShapeDtypeStruct(W_SPECS[k][0], BF16) for k in GATHER_LATE],
        in_specs=[G_SPEC(0), G_SPEC(1), G_SPEC(2), G_SPEC(3), pl.BlockSpec((2, HEAD_DIM), lambda h: (0, h))] + MASK_SPECS
        + [ANY] * n_late,
        out_specs=[pl.BlockSpec((SEQ, HEAD_DIM), lambda h: (0, h))] + [ANY] * n_late,
        scratch_shapes=[pltpu.VMEM((N_CHUNKS, HEAD_DIM, HEAD_DIM), F32), pltpu.VMEM((N_CHUNKS, 1, HEAD_DIM), F32),
                        pltpu.VMEM((SEQ, HEAD_DIM), BF16)] + [pltpu.VMEM(W_SPECS[k][0], BF16) for k in GATHER_LATE]
        + [pltpu.SemaphoreType.DMA((2 * n_late,)), pltpu.SemaphoreType.DMA((2 * n_late,)), pltpu.SemaphoreType.DMA((n_late,))],
        compiler_params=_params(48),
    )(*_pin(g5, g5, g5, g5, lb_logits), *_group_masks(), *late_shards)
    return outs[0], dict(zip(GATHER_LATE, outs[1:]))


def _gla_backward(g5, lb_logits, d_o, d_z, late_parts):
    q_scale = HEAD_DIM ** -0.5
    per_lat, per_ctx = GROUPS_PER_STEP, min(GROUPS_PER_STEP, CTX_LEN // GROUP)
    n_late = len(GATHER_LATE)

    def body(ff_ref, fb_ref, v_ref, q_ref, lg_ref, do_ref, dz_ref, sums_ref, back_ref, keep_ref, *rest):
        part_refs, (dg_ref, dlg_ref), slot_refs = rest[:n_late], rest[n_late:n_late + 2], rest[n_late + 2:2 * n_late + 2]
        st_ref, dst_ref, decay_ref, both_ref = rest[2 * n_late + 2:2 * n_late + 6]
        start_scatter, wait_scatter = _scatter_direct(GATHER_LATE, part_refs, slot_refs, *rest[2 * n_late + 6:])

        @pl.when(pl.program_id(0) == 0)
        def _():
            start_scatter()

        dg_ref[3, 0:CTX_LEN, :] = jnp.zeros((CTX_LEN, HEAD_DIM), BF16)
        dg_ref[4, 0:CTX_LEN, :] = jnp.zeros((CTX_LEN, HEAD_DIM), BF16)
        dg_ref[4, CTX_LEN:ROWS_ALL, :] = dz_ref[...]
        second = lax.broadcasted_iota(jnp.int32, (GROUP, HEAD_DIM), 0) >= CHUNK
        for rev in (False, True):
            d = 1 if rev else 0
            f_ref = fb_ref if rev else ff_ref
            lb = _sig(lg_ref[d:d + 1, :])
            last = 0 if rev else CHUNK - 1
            mid = CHUNK // 2 if rev else CHUNK // 2 - 1

            def local_step(i, carry, latent, per):
                r0s = [_group_rows(i, j, per, latent) for j in range(per)]
                rows = [pl.ds(r0, GROUP) for r0 in r0s]
                gates = [_gates(f_ref[r, :], lb) for r in rows]
                bs = [_group_sum(sums_ref[d], g[3]) for g in gates]
                bls = [_chunk_row(b, last) for b in bs]
                ups = [_mm_tn(v_ref[r, :], _by_chunk(g[2] * jnp.exp(bl - b), second)) for r, g, b, bl in zip(rows, gates, bs, bls)]
                if latent:
                    orows = [pl.ds(pl.multiple_of(r0 - CTX_LEN, GROUP), GROUP) for r0 in r0s]
                    d_ups = [_mm_tn(do_ref[orow, :], _by_chunk(_silu(q_ref[r, :]) * q_scale * jnp.exp(b), second))
                             for orow, r, b in zip(orows, rows, bs)]
                for j in range(per):
                    for c in range(2):
                        step = _scan_step_of(r0s[j] // CHUNK + c, rev, latent)
                        decay_ref[step] = jnp.exp(bls[j][c * CHUNK:c * CHUNK + 1, :])
                        st_ref[step] = ups[j][:, c * HEAD_DIM:(c + 1) * HEAD_DIM]
                        if latent:
                            dst_ref[step] = d_ups[j][:, c * HEAD_DIM:(c + 1) * HEAD_DIM]
                        else:
                            dst_ref[step] = jnp.zeros((HEAD_DIM, HEAD_DIM), F32)
                return carry

            lax.fori_loop(0, CTX_LEN // (per_ctx * GROUP), functools.partial(local_step, latent=False, per=per_ctx), 0)
            lax.fori_loop(0, SEQ // (per_lat * GROUP), functools.partial(local_step, latent=True, per=per_lat), 0)

            def scan_step(i, carry):
                st, d_st = carry
                j = N_CHUNKS - 1 - i
                update, d_update = st_ref[i], dst_ref[j]
                st_ref[i] = st
                dst_ref[j] = d_st
                return st * decay_ref[i] + update, d_st * decay_ref[j] + d_update

            zero_state = jnp.zeros((HEAD_DIM, HEAD_DIM), F32)
            lax.fori_loop(0, N_CHUNKS, scan_step, (zero_state, zero_state))

            def grad_step(i, dlb, latent, per):
                r0s = [_group_rows(i, j, per, latent) for j in range(per)]
                rows = [pl.ds(r0, GROUP) for r0 in r0s]
                gates = [_gates(f_ref[r, :], lb) for r in rows]
                vs = [v_ref[r, :] for r in rows]
                bs = [_group_sum(sums_ref[d], g[3]) for g in gates]
                bls = [_chunk_row(b, last) for b in bs]
                e_ends = [jnp.exp(bl - b) for b, bl in zip(bs, bls)]
                k_ends = [g[2] * e for g, e in zip(gates, e_ends)]
                sts = [[st_ref[_scan_step_of(r0 // CHUNK + c, rev, latent)] for c in range(2)] for r0 in r0s]
                d_sts = [[dst_ref[_scan_step_of(r0 // CHUNK + c, rev, latent)] for c in range(2)] for r0 in r0s]
                d_kends = [_own_block(_mm(v, jnp.concatenate(ds, axis=1))) for v, ds in zip(vs, d_sts)]
                d_vs = [_own_block(_mm_nt(ke, jnp.concatenate(ds, axis=0))) for ke, ds in zip(k_ends, d_sts)]
                at_last = [jnp.concatenate([jnp.broadcast_to(jnp.sum(ds[c] * s[c], axis=0, keepdims=True), (CHUNK, HEAD_DIM))
                                            for c in range(2)], axis=0) * jnp.exp(bl) for ds, s, bl in zip(d_sts, sts, bls)]
                t_kends = [dk * ke for dk, ke in zip(d_kends, k_ends)]
                d_ks = [dk * e for dk, e in zip(d_kends, e_ends)]
                if latent:
                    orows = [pl.ds(pl.multiple_of(r0 - CTX_LEN, GROUP), GROUP) for r0 in r0s]
                    qpres = [q_ref[r, :] for r in rows]
                    q_sigs = [_sig(qp) for qp in qpres]
                    qs = [qp * sg * q_scale for qp, sg in zip(qpres, q_sigs)]
                    bms = [_chunk_row(b, mid) for b in bs]
                    e_bs = [jnp.exp(b) for b in bs]
                    e_qms = [jnp.exp(b - bm) for b, bm in zip(bs, bms)]
                    e_kms = [jnp.exp(bm - b) for b, bm in zip(bs, bms)]
                    q_ts = [q * e for q, e in zip(qs, e_bs)]
                    q_ms = [q * e for q, e in zip(qs, e_qms)]
                    k_ms = [g[2] * e for g, e in zip(gates, e_kms)]
                    d_outs = [do_ref[orow, :] for orow in orows]
                    a_s = [_mm_nt(qm, km) * keep_ref[d] for qm, km in zip(q_ms, k_ms)]
                    d_as = [_mm_nt(do, v) * keep_ref[d] for do, v in zip(d_outs, vs)]
                    d_qts = [_own_block(_mm(do, jnp.concatenate(s, axis=1))) for do, s in zip(d_outs, sts)]
                    d_qms = [_mm(da, km) for da, km in zip(d_as, k_ms)]
                    d_kms = [_mm_tn(da, qm) for da, qm in zip(d_as, q_ms)]
                    d_vs = [dv + _mm_tn(a, do) for dv, a, do in zip(d_vs, a_s, d_outs)]
                    d_ks = [dk + dkm * e for dk, dkm, e in zip(d_ks, d_kms, e_kms)]
                    d_bs = [jnp.concatenate([dqt * qt + dqm * qm - dkm * km, t], axis=0)
                            for dqt, qt, dqm, qm, dkm, km, t in zip(d_qts, q_ts, d_qms, q_ms, d_kms, k_ms, t_kends)]
                    d_qs = [dqt * eb + dqm * eq for dqt, eb, dqm, eq in zip(d_qts, e_bs, d_qms, e_qms)]
                    back = back_ref[d]
                else:
                    d_bs, back = t_kends, back_ref[d, :, GROUP:2 * GROUP]
                d_lfs = [_group_sum(back, db) + al for db, al in zip(d_bs, at_last)]
                for j in range(per):
                    sg, f = gates[j][0], gates[j][1]
                    d_f = d_lfs[j] / f - d_ks[j]
                    dg_ref[d, rows[j], :] = (d_f * (1.0 - lb) * sg * (1.0 - sg)).astype(BF16)
                    dlb = dlb + jnp.sum(d_f * (1.0 - sg), axis=0, keepdims=True)
                    if rev:
                        dg_ref[2, rows[j], :] = (both_ref[0, rows[j], :] + d_vs[j]).astype(BF16)
                    else:
                        both_ref[0, rows[j], :] = d_vs[j]
                    if latent:
                        d_qpre = d_qs[j] * q_scale * (q_sigs[j] * (1.0 + qpres[j] * (1.0 - q_sigs[j])))
                        if rev:
                            dg_ref[3, rows[j], :] = (both_ref[1, rows[j], :] + d_qpre).astype(BF16)
                        else:
                            both_ref[1, rows[j], :] = d_qpre
                return dlb

            dlb = lax.fori_loop(0, SEQ // (GROUPS_PER_GRAD_STEP * GROUP),
                                functools.partial(grad_step, latent=True, per=GROUPS_PER_GRAD_STEP), jnp.zeros((1, HEAD_DIM), F32))
            dlb = lax.fori_loop(0, CTX_LEN // (per_ctx * GROUP), functools.partial(grad_step, latent=False, per=per_ctx), dlb)
            dlg_ref[d:d + 1, :] = dlb * lb * (1.0 - lb)

        @pl.when(pl.program_id(0) == HEADS - 1)
        def _():
            wait_scatter()

    col = pl.BlockSpec((SEQ, HEAD_DIM), lambda h: (0, h))
    outs = pl.pallas_call(
        body, name="gla_backward", grid=(HEADS,),
        out_shape=[pltpu.HBM((HGRN_SECTIONS, ROWS_ALL, D_MODEL), BF16), jax.ShapeDtypeStruct((2, D_MODEL), F32)]
        + _slot_shapes(GATHER_LATE),
        in_specs=[G_SPEC(0), G_SPEC(1), G_SPEC(2), G_SPEC(3), pl.BlockSpec((2, HEAD_DIM), lambda h: (0, h)), col, col]
        + MASK_SPECS + [ANY] * n_late,
        out_specs=[pl.BlockSpec((HGRN_SECTIONS, ROWS_ALL, HEAD_DIM), lambda h: (0, 0, h)),
                   pl.BlockSpec((2, HEAD_DIM), lambda h: (0, h))] + [ANY] * n_late,
        scratch_shapes=[pltpu.VMEM((N_CHUNKS, HEAD_DIM, HEAD_DIM), F32), pltpu.VMEM((N_CHUNKS, HEAD_DIM, HEAD_DIM), F32),
                        pltpu.VMEM((N_CHUNKS, 1, HEAD_DIM), F32), pltpu.VMEM((2, ROWS_ALL, HEAD_DIM), F32)]
        + _comm_sems(n_late),
        compiler_params=_params(48),
    )(*_pin(g5, g5, g5, g5, lb_logits, d_o, d_z), *_group_masks(), *late_parts)
    return outs[0], outs[1], dict(zip(GATHER_LATE, outs[2:]))


def _head_norm(o, scr):
    rs = []
    for h in range(HEADS):
        cols = slice(h * HEAD_DIM, (h + 1) * HEAD_DIM)
        oh = o[:, cols]
        r = lax.rsqrt(jnp.mean(oh * oh, axis=-1, keepdims=True) + EPS)
        scr[:, cols] = oh * r
        rs.append(r)
    return rs


def _hgrn_out_forward(o_raw, g5, xin, gnorm_w, gate, w_out):
    tm = ROW_TILE

    def body(o_ref, z_ref, x_ref, gw_ref, gate_ref, w_ref, x1_ref, res_ref, scr):
        _head_norm(o_ref[...], scr)
        a = scr[...] * gw_ref[...] * _silu(z_ref[...])
        res = _mm(a, w_ref[...])
        res_ref[...] = res
        x1_ref[...] = x_ref[...] + gate_ref[...] * res

    tile = pl.BlockSpec((tm, D_MODEL), lambda i: (i, 0))
    vec = pl.BlockSpec((1, D_MODEL), lambda i: (0, 0))
    return pl.pallas_call(
        body, name="hgrn_out_forward", grid=(SEQ // tm,),
        out_shape=[pltpu.HBM((SEQ, D_MODEL), F32)] * 2,
        in_specs=[tile, pl.BlockSpec((None, tm, D_MODEL), lambda i: (4, i + CTX_LEN // tm, 0)), tile, vec, vec,
                  pl.BlockSpec((D_MODEL, D_MODEL), lambda i: (0, 0))],
        out_specs=[tile, tile],
        scratch_shapes=[pltpu.VMEM((tm, D_MODEL), F32)],
        compiler_params=_params(32),
    )(*_pin(o_raw, g5, xin, gnorm_w, gate, w_out))


def _hgrn_out_backward(d_x1, o_raw, g5, res, gnorm_w, gate, w_out):
    tm = ROW_TILE

    def body(dx_ref, o_ref, z_ref, res_ref, gw_ref, gate_ref, w_ref, do_ref, dz_ref, dw_out, dgate_ref, dgw_ref, scr, scr2,
             dw_ref):
        @pl.when(pl.program_id(0) == 0)
        def _():
            dw_ref[...] = jnp.zeros_like(dw_ref)
            dgate_ref[...] = jnp.zeros_like(dgate_ref)
            dgw_ref[...] = jnp.zeros_like(dgw_ref)

        dx = dx_ref[...]
        dgate_ref[...] += jnp.sum(dx * res_ref[...], axis=0, keepdims=True)
        d_res = (dx * gate_ref[...]).astype(BF16)
        d_a = _mm_nt(d_res, w_ref[...])
        rs = _head_norm(o_ref[...], scr)
        z = z_ref[...]
        sz = _silu(z)
        o_hat = scr[...]
        o_n = o_hat * gw_ref[...]
        dw_ref[...] += _mm_tn(o_n * sz, d_res)
        d_on = d_a * sz
        dz_ref[...] = (d_a * o_n * _dsilu(z)).astype(BF16)
        dgw_ref[...] += jnp.sum(d_on * o_hat, axis=0, keepdims=True)
        scr2[...] = d_on * gw_ref[...]
        for h in range(HEADS):
            cols = slice(h * HEAD_DIM, (h + 1) * HEAD_DIM)
            dh, oh = scr2[:, cols], scr[:, cols]
            do_ref[:, cols] = (rs[h] * (dh - oh * jnp.mean(dh * oh, axis=-1, keepdims=True))).astype(BF16)

        @pl.when(pl.program_id(0) == SEQ // tm - 1)
        def _():
            dw_out[...] = dw_ref[...].astype(BF16)

    tile = pl.BlockSpec((tm, D_MODEL), lambda i: (i, 0))
    vec = pl.BlockSpec((1, D_MODEL), lambda i: (0, 0))
    mat = pl.BlockSpec((D_MODEL, D_MODEL), lambda i: (0, 0))
    return pl.pallas_call(
        body, name="hgrn_out_backward", grid=(SEQ // tm,),
        out_shape=[pltpu.HBM((SEQ, D_MODEL), BF16)] * 2 + [pltpu.HBM((D_MODEL, D_MODEL), BF16)]
        + [jax.ShapeDtypeStruct((1, D_MODEL), F32)] * 2,
        in_specs=[tile, tile, pl.BlockSpec((None, tm, D_MODEL), lambda i: (4, i + CTX_LEN // tm, 0)), tile, vec, vec, mat],
        out_specs=[tile, tile, mat, vec, vec],
        scratch_shapes=[pltpu.VMEM((tm, D_MODEL), F32)] * 2 + [pltpu.VMEM((D_MODEL, D_MODEL), F32)],
        compiler_params=_params(40),
    )(*_pin(d_x1, o_raw, g5, res, gnorm_w, gate, w_out))


def _pool_constants():
    win = np.zeros((POOL_GROUPS, ROW_TILE, ROW_TILE), np.float32)
    inv = np.zeros((POOL_GROUPS, ROW_TILE, 1), np.float32)
    for g, w in enumerate(POOL_WINDOWS):
        for t in range(ROW_TILE):
            base, p = (t // GRID_W) * GRID_W, t % GRID_W
            lo = min(max(p - w // 2, 0), GRID_W)
            hi = min(max(p - w // 2 + w, 0), GRID_W)
            win[g, t, base + lo:base + hi] = 1.0
            inv[g, t, 0] = 1.0 / np.float32(hi - lo)
    return jnp.asarray(win, BF16), jnp.asarray(win.transpose(0, 2, 1), BF16), jnp.asarray(inv, F32)


def _pool_mix(u_ref, wg_ref, win_ref, inv_ref, pooled_scr, yg_scr):
    for g in range(POOL_GROUPS):
        cols = slice(g * POOL_GROUP_DIM, (g + 1) * POOL_GROUP_DIM)
        for r in range(u_ref.shape[0] // ROW_TILE):
            rows = slice(r * ROW_TILE, (r + 1) * ROW_TILE)
            ug = u_ref[rows, cols]
            pooled_scr[rows, cols] = _mm_exact_lhs(win_ref[g], ug) * inv_ref[g] - ug
        yg_scr[:, cols] = _mm(pooled_scr[:, cols], wg_ref[g])


def _pool_forward_loss(uz, x1, target, gate, w_grp, pool_scale, w_out, final_w):
    tm = POOL_TILE
    win, _, inv = _pool_constants()

    def body(u_ref, z_ref, x_ref, t_ref, gate_ref, wg_ref, ps_ref, w_ref, fw_ref, win_ref, inv_ref,
             dx_ref, loss_ref, dfw_ref, dgate_ref, yg_scr, pooled_scr):
        @pl.when(pl.program_id(0) == 0)
        def _():
            loss_ref[...] = jnp.zeros_like(loss_ref)
            dfw_ref[...] = jnp.zeros_like(dfw_ref)
            dgate_ref[...] = jnp.zeros_like(dgate_ref)

        _pool_mix(u_ref, wg_ref, win_ref, inv_ref, pooled_scr, yg_scr)
        a = yg_scr[...] * ps_ref[...] * _silu(z_ref[...])
        res = _mm(a, w_ref[...])
        x2 = x_ref[...] + gate_ref[...] * res
        r = lax.rsqrt(jnp.mean(x2 * x2, axis=-1, keepdims=True) + EPS)
        xh = x2 * r
        fw = fw_ref[...]
        err = xh * fw - t_ref[...]
        loss_ref[...] += 0.5 * jnp.sum(jnp.mean(err * err, axis=-1, keepdims=True))
        d_y = err * (1.0 / D_MODEL)
        dfw_ref[...] += jnp.sum(d_y * xh, axis=0, keepdims=True)
        d_xh = d_y * fw
        d_x2 = r * (d_xh - xh * jnp.mean(d_xh * xh, axis=-1, keepdims=True))
        dx_ref[...] = d_x2
        dgate_ref[...] += jnp.sum(d_x2 * res, axis=0, keepdims=True)

    tile = pl.BlockSpec((tm, D_MODEL), lambda i: (i, 0))
    vec = pl.BlockSpec((1, D_MODEL), lambda i: (0, 0))
    grp = pl.BlockSpec((POOL_GROUPS, POOL_GROUP_DIM, POOL_GROUP_DIM), lambda i: (0, 0, 0))
    return pl.pallas_call(
        body, name="pool_forward_loss", grid=(SEQ // tm,),
        out_shape=[pltpu.HBM((SEQ, D_MODEL), F32), jax.ShapeDtypeStruct((8, 128), F32),
                   jax.ShapeDtypeStruct((1, D_MODEL), F32), jax.ShapeDtypeStruct((1, D_MODEL), F32)],
        in_specs=[pl.BlockSpec((None, tm, D_MODEL), lambda i: (0, i, 0)), pl.BlockSpec((None, tm, D_MODEL), lambda i: (1, i, 0)),
                  tile, tile, vec, grp, vec, pl.BlockSpec((D_MODEL, D_MODEL), lambda i: (0, 0)), vec, grp,
                  pl.BlockSpec((POOL_GROUPS, ROW_TILE, 1), lambda i: (0, 0, 0))],
        out_specs=[tile, pl.BlockSpec((8, 128), lambda i: (0, 0)), vec, vec],
        scratch_shapes=[pltpu.VMEM((tm, D_MODEL), F32)] * 2,
        compiler_params=_params(40),
    )(*_pin(uz, uz, x1, target, gate, w_grp, pool_scale, w_out, final_w, win, inv))


def _pool_backward(d_x2, uz, gate, w_grp, pool_scale, w_out):
    tm = POOL_TILE
    win, win_t, inv = _pool_constants()

    def body(dx_ref, u_ref, z_ref, gate_ref, wg_ref, ps_ref, w_ref, win_ref, wint_ref, inv_ref,
             duz_ref, dw_out, dwg_out, dps_ref, pooled_scr, yg_scr, dyg_scr, dw_ref, dwg_ref):
        @pl.when(pl.program_id(0) == 0)
        def _():
            dw_ref[...] = jnp.zeros_like(dw_ref)
            dwg_ref[...] = jnp.zeros_like(dwg_ref)
            dps_ref[...] = jnp.zeros_like(dps_ref)

        _pool_mix(u_ref, wg_ref, win_ref, inv_ref, pooled_scr, yg_scr)
        z = z_ref[...]
        sz = _silu(z)
        yg = yg_scr[...]
        y = yg * ps_ref[...]
        d_res = (dx_ref[...] * gate_ref[...]).astype(BF16)
        d_a = _mm_nt(d_res, w_ref[...])
        dw_ref[...] += _mm_tn(y * sz, d_res)
        d_y = d_a * sz
        duz_ref[1] = (d_a * y * _dsilu(z)).astype(BF16)
        dps_ref[...] += jnp.sum(d_y * yg, axis=0, keepdims=True)
        dyg_scr[...] = d_y * ps_ref[...]
        for g in range(POOL_GROUPS):
            cols = slice(g * POOL_GROUP_DIM, (g + 1) * POOL_GROUP_DIM)
            d_yg = dyg_scr[:, cols].astype(BF16)
            d_pool = _mm_nt(d_yg, wg_ref[g])
            dwg_ref[g] += _mm_tn(pooled_scr[:, cols], d_yg)
            for r in range(tm // ROW_TILE):
                rows = slice(r * ROW_TILE, (r + 1) * ROW_TILE)
                dp = d_pool[rows, :]
                duz_ref[0, rows, cols] = (_mm_exact_lhs(wint_ref[g], dp * inv_ref[g]) - dp).astype(BF16)

        @pl.when(pl.program_id(0) == SEQ // tm - 1)
        def _():
            dw_out[...] = dw_ref[...].astype(BF16)
            dwg_out[...] = dwg_ref[...].astype(BF16)

    tile = pl.BlockSpec((tm, D_MODEL), lambda i: (i, 0))
    vec = pl.BlockSpec((1, D_MODEL), lambda i: (0, 0))
    mat = pl.BlockSpec((D_MODEL, D_MODEL), lambda i: (0, 0))
    grp = pl.BlockSpec((POOL_GROUPS, POOL_GROUP_DIM, POOL_GROUP_DIM), lambda i: (0, 0, 0))
    return pl.pallas_call(
        body, name="pool_backward", grid=(SEQ // tm,),
        out_shape=[pltpu.HBM((POOL_SECTIONS, SEQ, D_MODEL), BF16), pltpu.HBM((D_MODEL, D_MODEL), BF16),
                   pltpu.HBM((POOL_GROUPS, POOL_GROUP_DIM, POOL_GROUP_DIM), BF16), jax.ShapeDtypeStruct((1, D_MODEL), F32)],
        in_specs=[tile, pl.BlockSpec((None, tm, D_MODEL), lambda i: (0, i, 0)), pl.BlockSpec((None, tm, D_MODEL), lambda i: (1, i, 0)),
                  vec, grp, vec, mat, grp, grp, pl.BlockSpec((POOL_GROUPS, ROW_TILE, 1), lambda i: (0, 0, 0))],
        out_specs=[pl.BlockSpec((POOL_SECTIONS, tm, D_MODEL), lambda i: (0, i, 0)), mat, grp, vec],
        scratch_shapes=[pltpu.VMEM((tm, D_MODEL), F32)] * 3 + [pltpu.VMEM((D_MODEL, D_MODEL), F32),
                                                               pltpu.VMEM((POOL_GROUPS, POOL_GROUP_DIM, POOL_GROUP_DIM), F32)],
        compiler_params=_params(48),
    )(*_pin(d_x2, uz, uz, gate, w_grp, pool_scale, w_out, win, win_t, inv))


def _ln_mod_backward(d_g, w, ctx_tile, xin, nw, scale, d_up, name):
    n_sec, rows, _ = d_g.shape
    n_mod = scale.shape[0]
    skip = n_mod - 1
    tm = ROW_TILE if skip else 2 * ROW_TILE
    n_tiles = rows // tm

    def body(dg_ref, w_ref, *refs):
        c_ref = refs[0] if skip else None
        x_ref, nw_ref, sc_ref, up_ref, dx_ref, dnw_ref, dmod_ref = refs[skip:]
        i = pl.program_id(0)

        @pl.when(i == 0)
        def _():
            dnw_ref[...] = jnp.zeros_like(dnw_ref)

        @pl.when((i == 0) | (i == skip))
        def _():
            dmod_ref[...] = jnp.zeros_like(dmod_ref)

        d_h = _mm_nt(dg_ref[0], w_ref[:, 0:D_MODEL])
        for k in range(1, n_sec):
            d_h = d_h + _mm_nt(dg_ref[k], w_ref[:, k * D_MODEL:(k + 1) * D_MODEL])
        xv = jnp.where(i == 0, c_ref[...], x_ref[...]) if skip else x_ref[...]
        r = lax.rsqrt(jnp.mean(xv * xv, axis=-1, keepdims=True) + EPS)
        xh = xv * r
        nw_row = nw_ref[...]
        dmod_ref[0:1, :] += jnp.sum(d_h, axis=0, keepdims=True)
        dmod_ref[1:2, :] += jnp.sum(d_h * (xh * nw_row), axis=0, keepdims=True)
        d_xn = d_h * (1.0 + sc_ref[...])
        dnw_ref[...] += jnp.sum(d_xn * xh, axis=0, keepdims=True)
        d_xh = d_xn * nw_row

        @pl.when(i >= skip)
        def _():
            dx_ref[...] = up_ref[...] + r * (d_xh - xh * jnp.mean(d_xh * xh, axis=-1, keepdims=True))

    lat = lambda i: (jnp.maximum(i - skip, 0), 0)
    mod_idx = lambda i: (jnp.minimum(i, n_mod - 1), 0, 0)
    return pl.pallas_call(
        body, name=name, grid=(n_tiles,),
        out_shape=[pltpu.HBM((rows - skip * tm, D_MODEL), F32), jax.ShapeDtypeStruct((1, D_MODEL), F32),
                   jax.ShapeDtypeStruct((n_mod, 8, D_MODEL), F32)],
        in_specs=[pl.BlockSpec((n_sec, tm, D_MODEL), lambda i: (0, i, 0)),
                  pl.BlockSpec((D_MODEL, n_sec * D_MODEL), lambda i: (0, 0))]
        + [pl.BlockSpec((tm, D_MODEL), lambda i: (0, 0))] * skip
        + [pl.BlockSpec((tm, D_MODEL), lat),
           pl.BlockSpec((1, D_MODEL), lambda i: (0, 0)),
           pl.BlockSpec((None, 1, D_MODEL), mod_idx),
           pl.BlockSpec((tm, D_MODEL), lat)],
        out_specs=[pl.BlockSpec((tm, D_MODEL), lat), pl.BlockSpec((1, D_MODEL), lambda i: (0, 0)),
                   pl.BlockSpec((None, 8, D_MODEL), mod_idx)],
        compiler_params=_params(48),
    )(*_pin(d_g, w, *([ctx_tile] * skip), xin, nw, scale.reshape(n_mod, 1, D_MODEL), d_up))


def _weight_grad(h, d_g, name):
    n_sec, rows, _ = d_g.shape
    tm = 768 if rows % 768 == 0 else 512
    n_tiles = rows // tm

    def body(h_ref, dg_ref, dw_ref, acc):
        i = pl.program_id(1)
        prod = _mm_tn(h_ref[...], dg_ref[...])

        @pl.when(i == 0)
        def _():
            acc[...] = prod

        @pl.when((i > 0) & (i < n_tiles - 1))
        def _():
            acc[...] += prod

        @pl.when(i == n_tiles - 1)
        def _():
            dw_ref[...] = (acc[...] + prod).astype(BF16)

    return pl.pallas_call(
        body, name=name, grid=(n_sec, n_tiles),
        out_shape=pltpu.HBM((D_MODEL, n_sec * D_MODEL), BF16),
        in_specs=[pl.BlockSpec((tm, D_MODEL), lambda j, i: (i, 0)), pl.BlockSpec((None, tm, D_MODEL), lambda j, i: (j, i, 0))],
        out_specs=pl.BlockSpec((D_MODEL, D_MODEL), lambda j, i: (0, j)),
        scratch_shapes=[pltpu.VMEM((D_MODEL, D_MODEL), F32)],
        compiler_params=_params(32),
    )(*_pin(h, d_g))


def _weight_grad_paired(h, d_g, name):
    n_sec, rows, _ = d_g.shape
    tm = 768
    n_tiles = rows // tm
    half = D_MODEL // 2
    chip_cols = n_sec * D_MODEL // N_CHIPS

    def body(h_ref, dg_ref, q_ref, slots_ref, acc, keep, send, land, send_sems, recv_sems, own_sem):
        j, i = pl.program_id(0), pl.program_id(1)
        x, y, c = _my_place()
        prod = _mm_tn(h_ref[...], dg_ref[...])

        def to_sibling(k):
            return pltpu.make_async_remote_copy(src_ref=send.at[k], dst_ref=land.at[k], send_sem=send_sems.at[k],
                                                recv_sem=recv_sems.at[k], device_id=(x, y, 1 - c), device_id_type=MESH)

        @pl.when(i == 0)
        def _():
            acc[...] = prod

        @pl.when((i > 0) & (i < n_tiles - 1))
        def _():
            acc[...] += prod

        @pl.when(i == n_tiles - 1)
        def _():
            acc[...] += prod
            keep[j] = acc[pl.ds(_al(c * half, half), half), :]
            send[j] = acc[pl.ds(_al((1 - c) * half, half), half), :].astype(BF16)
            to_sibling(j).start()

        @pl.when((j == n_sec - 1) & (i == n_tiles - 1))
        def _():
            for k in range(n_sec):
                to_sibling(k).wait()
                q_ref[:, k * D_MODEL:(k + 1) * D_MODEL] = (keep[k] + land[k].astype(F32)).astype(BF16)
            chip = 2 * x + y
            own = pltpu.make_async_copy(q_ref.at[:, pl.ds(_al(chip * chip_cols, chip_cols), chip_cols)], slots_ref.at[chip], own_sem)
            own.start()
            own.wait()

    return pl.pallas_call(
        body, name=name, grid=(n_sec, n_tiles),
        out_shape=[jax.ShapeDtypeStruct((half, n_sec * D_MODEL), BF16),
                   jax.ShapeDtypeStruct((N_CHIPS, half, chip_cols), BF16)],
        in_specs=[pl.BlockSpec((tm, D_MODEL), lambda j, i: (i, 0)), pl.BlockSpec((None, tm, D_MODEL), lambda j, i: (j, i, 0))],
        out_specs=[pl.BlockSpec((half, n_sec * D_MODEL), lambda j, i: (0, 0)), ANY],
        scratch_shapes=[pltpu.VMEM((D_MODEL, D_MODEL), F32), pltpu.VMEM((n_sec, half, D_MODEL), F32),
                        pltpu.VMEM((n_sec, half, D_MODEL), BF16), pltpu.VMEM((n_sec, half, D_MODEL), BF16),
                        pltpu.SemaphoreType.DMA((n_sec,)), pltpu.SemaphoreType.DMA((n_sec,)), pltpu.SemaphoreType.DMA],
        compiler_params=_params(56),
    )(*_pin(h, d_g))


def _adamw_math(w, g, m, v):
    m = ADAM_B1 * m + (1.0 - ADAM_B1) * g
    v = ADAM_B2 * v + (1.0 - ADAM_B2) * (g * g)
    m_hat = m / (1.0 - ADAM_B1 ** ADAM_STEP)
    v_hat = v / (1.0 - ADAM_B2 ** ADAM_STEP)
    return -ADAM_LR * (m_hat / (jnp.sqrt(v_hat) + ADAM_EPS) + ADAM_WD * w), m, v


def _adamw(w, g, m, v, name):
    rows, cols = w.shape
    tr = rows if rows <= 128 else 128

    def body(w_ref, g_ref, m_ref, v_ref, d_ref, mo_ref, vo_ref):
        d_ref[...], mo_ref[...], vo_ref[...] = _adamw_math(w_ref[...], g_ref[...], m_ref[...], v_ref[...])

    tile = pl.BlockSpec((tr, cols), lambda i: (i, 0))
    return pl.pallas_call(
        body, name=name, grid=(rows // tr,),
        out_shape=[pltpu.HBM((rows, cols), F32)] * 3,
        in_specs=[tile] * 4, out_specs=[tile] * 3,
    )(*_pin(w, g, m, v))


def _small_sums_and_cond_ctx(small, ada_w0, c_ctx, m, v):
    n_cols = ada_w0.shape[1]

    def body(small_ref, w_ref, c_ref, m_ref, v_ref, all_ref, sums_ref, g_ref, d_ref, mo_ref, vo_ref, part_ref, parts_ref,
             s1, r1, l1, s2, r2, l2):
        start_slabs, finish_slabs = _small_gather(small_ref, all_ref, s1, r1, l1)
        start_parts, finish_parts = _small_gather(part_ref, parts_ref, s2, r2, l2)
        start_slabs()
        finish_slabs()
        acc = all_ref[0:SMALL_ROWS, :]
        for dev in range(1, N_DEV):
            acc = acc + all_ref[dev * SMALL_ROWS:(dev + 1) * SMALL_ROWS, :]
        sums_ref[...] = acc
        x, y, _ = _my_place()
        d_modc = jnp.concatenate([acc[6:7, :], acc[7:8, :], acc[8:9, :]], axis=1)
        mine = jnp.zeros((1, n_cols), F32)
        for s in range(N_CHIPS):
            mine = mine + jnp.where(2 * x + y == s, d_modc[:, s * n_cols:(s + 1) * n_cols], 0.0)
        ah, al = _split2(jnp.broadcast_to(mine, (8, n_cols)))
        wh, wl = _split2(w_ref[...])
        nt = lambda a, b: lax.dot_general(a, b, (((1,), (1,)), ((), ())), preferred_element_type=F32)
        part_ref[...] = nt(ah, wh) + nt(al, wh) + nt(ah, wl)
        start_parts()
        finish_parts()
        total = parts_ref[0:1, :]
        for s in range(1, N_CHIPS):
            total = total + parts_ref[16 * s:16 * s + 1, :]
        w = c_ref[...]
        g = total * _dsilu(w)
        g_ref[...] = g
        d_ref[...], mo_ref[...], vo_ref[...] = _adamw_math(w, g, m_ref[...], v_ref[...])

    seven = [pltpu.SemaphoreType.DMA((7,)), pltpu.SemaphoreType.DMA((7,)), pltpu.SemaphoreType.DMA]
    vec = jax.ShapeDtypeStruct((1, D_MODEL), F32)
    return pl.pallas_call(
        body, name="small_sums_and_cond_ctx",
        out_shape=[jax.ShapeDtypeStruct((N_DEV * SMALL_ROWS, D_MODEL), F32), jax.ShapeDtypeStruct((SMALL_ROWS, D_MODEL), F32),
                   vec, vec, vec, vec],
        in_specs=[VMEM] * 5, out_specs=[VMEM] * 6,
        scratch_shapes=[pltpu.VMEM((8, D_MODEL), F32), pltpu.VMEM((N_DEV * 8, D_MODEL), F32)] + seven + seven,
    )(small, ada_w0, c_ctx, m, v)


def _ada_update(cond_t, d_mod, ada_w, m, v):
    n_layers, _, n_cols = ada_w.shape
    tr = ROW_TILE

    def body(c_ref, dm_ref, w_ref, m_ref, v_ref, g_ref, d_ref, mo_ref, vo_ref):
        g = _mm_f32(_silu(c_ref[...]), dm_ref[...])
        g_ref[...] = g
        d_ref[...], mo_ref[...], vo_ref[...] = _adamw_math(w_ref[...], g, m_ref[...], v_ref[...])

    tile = pl.BlockSpec((None, tr, n_cols), lambda l, i: (l, i, 0))
    return pl.pallas_call(
        body, name="ada_update", grid=(n_layers, D_MODEL // tr),
        out_shape=[pltpu.HBM(ada_w.shape, F32)] * 4,
        in_specs=[pl.BlockSpec((tr, 16), lambda l, i: (i, 0)), pl.BlockSpec((None, 16, n_cols), lambda l, i: (l, 0, 0)),
                  tile, tile, tile],
        out_specs=[tile] * 4,
    )(*_pin(cond_t, d_mod, ada_w, m, v))


def _local_step(x2, ctx2, target, mod_mine, mod_ctx, lb_logits, scale_full, w_in_full, late_shards, norm_w, gnorm, final_w):
    row = lambda a: a.reshape(1, -1)
    shift0, scale0, gate0 = (row(a) for a in jnp.split(mod_mine[0], 3))
    shift1, scale1, gate1 = (row(a) for a in jnp.split(mod_mine[1], 3))
    shift_c, scale_c, _ = (row(a) for a in jnp.split(mod_ctx, 3))
    nw0, nw1 = norm_w[0:1], norm_w[1:2]
    scales0 = jnp.concatenate([scale_c, scale0])

    g5, h0 = _ln_mod_matmul(ctx2, x2, nw0, jnp.concatenate([shift_c, shift0]), scales0, w_in_full, "hgrn_in_forward")
    o_raw, full = _gla_forward(g5, lb_logits, late_shards)
    x1, res0 = _hgrn_out_forward(o_raw, g5, x2, gnorm, gate0, full["hgrn_w_out"])
    uz, h1 = _ln_mod_matmul(None, x1, nw1, shift1, scale1, full["pool_w_in"], "pool_in_forward")
    d_x2, loss_part, d_final, d_gate1 = _pool_forward_loss(uz, x1, target, gate1, full["pool_w_grp"], scale_full,
                                                           full["pool_w_out"], final_w)

    d_uz, dw_pool_out, dw_pool_grp, d_pscale = _pool_backward(d_x2, uz, gate1, full["pool_w_grp"], scale_full, full["pool_w_out"])
    d_x1, d_nw1, d_mod1 = _ln_mod_backward(d_uz, full["pool_w_in"], None, x1, nw1, scale1, d_x2, "pool_in_backward")
    dw_pool_in = _weight_grad(h1, d_uz, "pool_in_weight_grad")
    d_o, d_z, dw_hgrn_out, d_gate0, d_gnorm = _hgrn_out_backward(d_x1, o_raw, g5, res0, gnorm, gate0, full["hgrn_w_out"])
    late_grads = {"hgrn_w_out": dw_hgrn_out, "pool_w_in": dw_pool_in, "pool_w_grp": dw_pool_grp, "pool_w_out": dw_pool_out}
    d_g5, d_lb, late_slots = _gla_backward(g5, lb_logits, d_o, d_z, [late_grads[k].astype(BF16) for k in GATHER_LATE])
    dw_hgrn_in, slots0 = _weight_grad_paired(h0, d_g5, "hgrn_in_weight_grad")
    key = GATHER_EARLY[0]
    send_sem, recv_sem, part_thru, slots_thru, token = _scatter_start(dw_hgrn_in, slots0, d_lb, key)
    d_x, d_nw0, d_mod0 = _ln_mod_backward(d_g5, w_in_full, ctx2, x2, nw0 + token[0:1, 0:1], scales0, d_x1, "hgrn_in_backward")
    slots = dict(late_slots)
    pending = (send_sem, recv_sem, part_thru, slots_thru)

    zero = jnp.zeros((1, D_MODEL), F32)
    small = jnp.concatenate([d_mod0[1, 0:2], d_gate0, d_mod1[0, 0:2], d_gate1, d_mod0[0, 0:2], zero, d_nw0, d_nw1, d_gnorm,
                             d_final, d_pscale, d_lb, jnp.broadcast_to(loss_part[0:1, 0:1], (1, D_MODEL)),
                             jnp.zeros((SMALL_ROWS - 17, D_MODEL), F32)], axis=0)
    return {"d_x": d_x, "slots": slots, "pending": pending, "small": small}


def kernel(x, c, ctx, c_ctx, ada_w, ada_b, norm_w, hgrn_w_in, hgrn_lb_logits, hgrn_gnorm_w, hgrn_w_out, pool_w_in, pool_w_grp, pool_scale, pool_w_out, final_norm_w, loss_target, m_c_ctx, m_ada_w, m_ada_b, m_norm_w, m_hgrn_w_in, m_hgrn_lb_logits, m_hgrn_gnorm_w, m_hgrn_w_out, m_pool_w_in, m_pool_w_grp, m_pool_scale, m_pool_w_out, m_final_norm_w, v_c_ctx, v_ada_w, v_ada_b, v_norm_w, v_hgrn_w_in, v_hgrn_lb_logits, v_hgrn_gnorm_w, v_hgrn_w_out, v_pool_w_in, v_pool_w_grp, v_pool_scale, v_pool_w_out, v_final_norm_w):
    xi, yi, ci = _my_place()
    chip = 2 * xi + yi
    dev = 4 * xi + 2 * yi + ci
    ada_cols = ada_w.shape[2]
    lb_cols = hgrn_lb_logits.shape[2]
    ps_cols = pool_scale.shape[1]
    row = lambda a: a.reshape(1, -1)

    def chip_cols(a, n):
        return lax.dynamic_slice_in_dim(a, chip * n, n, axis=a.ndim - 1)

    def from_chips(g, rows_per_dev, take):
        return jnp.concatenate([g[2 * s * rows_per_dev:2 * s * rows_per_dev + take] for s in range(N_CHIPS)], axis=1)

    first = jnp.concatenate([jnp.broadcast_to(c, (8, D_MODEL)), jnp.pad(hgrn_lb_logits[0], ((0, 6), (0, 0))),
                             jnp.pad(pool_scale, ((0, 7), (0, 0)))], axis=1)
    shards = {"hgrn_w_in": hgrn_w_in[0], "hgrn_w_out": hgrn_w_out[0], "pool_w_in": pool_w_in[0],
              "pool_w_grp": pool_w_grp[0], "pool_w_out": pool_w_out[0]}
    first_all, parts_all, w_in_full = _prologue(first, row(c_ctx), ada_w, chip_cols(ada_b, ada_cols),
                                                shards[GATHER_EARLY[0]].astype(BF16))
    cond_all = first_all[::8, :D_MODEL]
    lb_logits = from_chips(first_all[:, D_MODEL:D_MODEL + lb_cols], 8, 2)
    scale_full = from_chips(first_all[:, D_MODEL + lb_cols:], 8, 1)
    cond_rows = jnp.concatenate([cond_all, row(c_ctx), jnp.zeros((7, D_MODEL), F32)], axis=0)
    mod_all = from_chips(parts_all, 32, 32).reshape(2, 16, 3 * D_MODEL)
    mod_mine = lax.dynamic_index_in_dim(mod_all, dev, axis=1, keepdims=False)

    loc = _local_step(x[0], ctx[0], loss_target[0], mod_mine, mod_all[0, 8], lb_logits, scale_full, w_in_full,
                      [shards[k].astype(BF16) for k in GATHER_LATE], norm_w, hgrn_gnorm_w, row(final_norm_w))

    small_all, sums, g_c, d_c, m_c, v_c = _small_sums_and_cond_ctx(loc["small"], ada_w[0], row(c_ctx), row(m_c_ctx), row(v_c_ctx))
    loss = sums[16, 0]

    def reduce_scattered(slots, names, name):
        return dict(zip(names, _sum_and_exchange([slots[k] for k in names], names, name)))

    big_grads = reduce_scattered(loc["slots"], GATHER_LATE, "exchange_halves_late")

    out = {}

    def update(name, w, g, m, v):
        shape = w.shape
        w2, g2, m2, v2 = (a.reshape(-1, shape[-1]) for a in (w, g, m, v))
        d, mn, vn = _adamw(w2, g2, m2, v2, "adamw_" + name)
        out[name] = tuple(a.reshape(shape) for a in (g2, d, mn, vn))

    moments = {"hgrn_w_in": (m_hgrn_w_in, v_hgrn_w_in), "hgrn_w_out": (m_hgrn_w_out, v_hgrn_w_out),
               "pool_w_in": (m_pool_w_in, v_pool_w_in), "pool_w_grp": (m_pool_w_grp, v_pool_w_grp),
               "pool_w_out": (m_pool_w_out, v_pool_w_out)}
    weights = {"hgrn_w_in": hgrn_w_in, "hgrn_w_out": hgrn_w_out, "pool_w_in": pool_w_in, "pool_w_grp": pool_w_grp,
               "pool_w_out": pool_w_out}
    for k in GATHER_LATE:
        update(k, weights[k], big_grads[k], *moments[k])

    g_ada_b = jnp.stack([(sums[0:3] + sums[6:9]).reshape(-1), sums[3:6].reshape(-1)])
    update("ada_b", ada_b, g_ada_b, m_ada_b, v_ada_b)
    update("norm_w", norm_w, sums[9:11], m_norm_w, v_norm_w)
    update("hgrn_gnorm_w", hgrn_gnorm_w, sums[11:12], m_hgrn_gnorm_w, v_hgrn_gnorm_w)
    update("final_norm_w", row(final_norm_w), sums[12:13], row(m_final_norm_w), row(v_final_norm_w))
    update("pool_scale", pool_scale, chip_cols(sums[13:14], ps_cols), m_pool_scale, v_pool_scale)
    update("hgrn_lb_logits", hgrn_lb_logits, chip_cols(sums[14:16], lb_cols)[None], m_hgrn_lb_logits, v_hgrn_lb_logits)

    per_dev = small_all.reshape(N_DEV, SMALL_ROWS, D_MODEL)
    pad7 = jnp.zeros((7, 3 * D_MODEL), F32)
    dm0 = jnp.concatenate([per_dev[:, 0:3].reshape(N_DEV, -1), sums[6:9].reshape(1, -1), pad7], axis=0)
    dm1 = jnp.concatenate([per_dev[:, 3:6].reshape(N_DEV, -1), jnp.zeros((8, 3 * D_MODEL), F32)], axis=0)
    d_mod = chip_cols(jnp.stack([dm0, dm1]), ada_cols)
    out["ada_w"] = _ada_update(cond_rows.T, d_mod, ada_w, m_ada_w, v_ada_w)

    out["c_ctx"] = tuple(a.reshape(-1) for a in (g_c, d_c, m_c, v_c))
    out["final_norm_w"] = tuple(a.reshape(-1) for a in out["final_norm_w"])

    done = [g_c, out["ada_w"][1]] + [out[k][1] for k in GATHER_LATE]
    key = GATHER_EARLY[0]
    _, early = _scatter_wait(*loc["pending"], done, key)
    big_grads = reduce_scattered({key: early}, GATHER_EARLY, "exchange_halves_early")
    for k in GATHER_EARLY:
        update(k, weights[k], big_grads[k], *moments[k])

    names = ["c_ctx", "ada_w", "ada_b", "norm_w", "hgrn_w_in", "hgrn_lb_logits", "hgrn_gnorm_w", "hgrn_w_out", "pool_w_in",
             "pool_w_grp", "pool_scale", "pool_w_out", "final_norm_w"]
    return (loss, loc["d_x"][None], *[out[k][0] for k in names], *[out[k][1] for k in names], *[out[k][2] for k in names],
            *[out[k][3] for k in names])
```

```python
import functools

import numpy as np
import jax
import jax.numpy as jnp
from jax import lax
from jax.experimental import pallas as pl
from jax.experimental.pallas import tpu as pltpu

F32 = jnp.float32
BF16 = jnp.bfloat16

D_MODEL = 1024
SEQ = 2048
CTX_LEN = 256
ROWS_ALL = CTX_LEN + SEQ
HEADS = 8
HEAD_DIM = 128
CHUNK = 64
N_CTX_CHUNKS = CTX_LEN // CHUNK
N_LAT_CHUNKS = SEQ // CHUNK
N_CHUNKS = N_CTX_CHUNKS + N_LAT_CHUNKS
GRID_W = 64
POOL_WINDOWS = (2, 4, 8, 16)
POOL_GROUPS = 4
POOL_GROUP_DIM = 256
HGRN_SECTIONS = 5
POOL_SECTIONS = 2
EPS = 1e-6
N_DEV = 8
N_CHIPS = 4
ROW_TILE = 256
POOL_TILE = 512
SMALL_ROWS = 24

ADAM_LR = 0.001
ADAM_B1 = 0.9
ADAM_B2 = 0.999
ADAM_EPS = 1e-08
ADAM_WD = 0.01
ADAM_STEP = 10

MESH = pl.DeviceIdType.MESH
MIB = 1 << 20
ANY = pl.BlockSpec(memory_space=pl.ANY)
VMEM = pl.BlockSpec(memory_space=pltpu.VMEM)


def _params(vmem_mib=None):
    if vmem_mib is None:
        return pltpu.CompilerParams()
    return pltpu.CompilerParams(vmem_limit_bytes=vmem_mib * MIB)


def _pin(*operands):
    return [pltpu.with_memory_space_constraint(a, pltpu.HBM) if a.size * a.dtype.itemsize >= MIB else a for a in operands]


def _sig(a):
    return 0.5 * jnp.tanh(0.5 * a) + 0.5


def _silu(a):
    return a * _sig(a)


def _dsilu(a):
    s = _sig(a)
    return s * (1.0 + a * (1.0 - s))


def _mm(a, b):
    return jnp.dot(a.astype(BF16), b.astype(BF16), preferred_element_type=F32)


def _mm_nt(a, b):
    return lax.dot_general(a.astype(BF16), b.astype(BF16), (((1,), (1,)), ((), ())), preferred_element_type=F32)


def _mm_tn(a, b):
    return lax.dot_general(a.astype(BF16), b.astype(BF16), (((0,), (0,)), ((), ())), preferred_element_type=F32)


def _split2(a):
    hi = a.astype(BF16)
    lo = (a - hi.astype(F32)).astype(BF16)
    return hi, lo


def _mm_exact_lhs(m_bf, a):
    hi, lo = _split2(a)
    return jnp.dot(m_bf, hi, preferred_element_type=F32) + jnp.dot(m_bf, lo, preferred_element_type=F32)


def _mm_f32(a, b):
    ah, al = _split2(a)
    bh, bl = _split2(b)
    return (jnp.dot(ah, bh, preferred_element_type=F32) + jnp.dot(al, bh, preferred_element_type=F32)
            + jnp.dot(ah, bl, preferred_element_type=F32))


def _my_place():
    return lax.axis_index("x"), lax.axis_index("y"), lax.axis_index("c")


def _small_gather(x_ref, out_ref, send_sems, recv_sems, local_sem):
    m_per = x_ref.shape[0]
    x, y, c = _my_place()
    me, sibling = (x, y, c), (x, y, 1 - c)
    chips = [(1 - x, y), (x, 1 - y), (1 - x, 1 - y)]

    def rows(px, py, pc):
        return out_ref.at[pl.ds((4 * px + 2 * py + pc) * m_per, m_per), :]

    def copy(k, block, to, src=None):
        return pltpu.make_async_remote_copy(
            src_ref=rows(*block) if src is None else src, dst_ref=rows(*block),
            send_sem=send_sems.at[k], recv_sem=recv_sems.at[k], device_id=to, device_id_type=MESH)

    def mine():
        return pltpu.make_async_copy(x_ref, rows(*me), local_sem)

    def first():
        return [copy(0, me, sibling, src=x_ref)] + [copy(1 + j, me, (*chip, c), src=x_ref) for j, chip in enumerate(chips)]

    def start():
        mine().start()
        for cp in first():
            cp.start()

    def finish():
        passed = [copy(4 + j, (*chip, c), sibling) for j, chip in enumerate(chips)]
        for j, chip in enumerate(chips):
            copy(1 + j, (*chip, c), me).wait_recv()
            passed[j].start()
        copy(0, sibling, me).wait_recv()
        for j, chip in enumerate(chips):
            copy(4 + j, (*chip, 1 - c), me).wait_recv()
        for cp in first() + passed:
            cp.wait_send()
        mine().wait()

    return start, finish


W_SPECS = {
    "hgrn_w_in": ((D_MODEL, 5 * D_MODEL), (1, 1280, 0, 512)),
    "hgrn_w_out": ((D_MODEL, D_MODEL), (0, 256, 0, 128)),
    "pool_w_in": ((D_MODEL, 2 * D_MODEL), (1, 512, 0, 512)),
    "pool_w_grp": ((POOL_GROUPS, POOL_GROUP_DIM, POOL_GROUP_DIM), (1, 64, 1, 32)),
    "pool_w_out": ((D_MODEL, D_MODEL), (0, 256, 0, 128)),
}

def _al(v, m):
    return pl.multiple_of(v, m)


def _region(ref, spec, chip, half):
    ca, cn, ha, hn = spec
    idx = [slice(None)] * len(ref.shape)
    if ca == ha:
        if half is None:
            idx[ca] = pl.ds(_al(chip * cn, cn), cn)
        else:
            idx[ca] = pl.ds(_al(chip * cn + half * hn, hn), hn)
    else:
        idx[ca] = pl.ds(_al(chip * cn, cn), cn)
        if half is not None:
            idx[ha] = pl.ds(_al(half * hn, hn), hn)
    return ref.at[tuple(idx)]


def _half_of(ref, spec, half):
    _, _, ha, hn = spec
    idx = [slice(None)] * len(ref.shape)
    idx[ha] = pl.ds(_al(half * hn, hn), hn)
    return ref.at[tuple(idx)]


def _half_shape(name):
    full, (ca, cn, ha, hn) = W_SPECS[name]
    shp = list(full)
    shp[ca] = cn
    shp[ha] = hn
    return tuple(shp)


def _gather_two_level(names, sh, land, full, send_sems, recv_sems, local_sems):
    n = len(names)
    specs = [W_SPECS[k][1] for k in names]
    x, y, c = _my_place()
    chip_me = 2 * x + y
    sibling = (x, y, 1 - c)
    chips = [(1 - x, y), (x, 1 - y), (1 - x, 1 - y)]

    def remote(k, src, dst, to):
        return pltpu.make_async_remote_copy(src_ref=src, dst_ref=dst, send_sem=send_sems.at[k], recv_sem=recv_sems.at[k],
                                            device_id=to, device_id_type=MESH)

    def three_halves(a):
        ca, cn, ha, hn = specs[a]
        idx = [slice(None)] * len(land[a].shape)
        if ca == ha:
            idx[ca] = pl.ds(0, 3 * hn)
        else:
            idx[ca], idx[ha] = pl.ds(0, 3 * cn), pl.ds(0, hn)
        return land[a].at[tuple(idx)]

    def own(a):
        return pltpu.make_async_copy(sh[a], _region(land[a], specs[a], chip_me, None), local_sems.at[a])

    def out(a):
        return pltpu.make_async_copy(land[a], full[a], local_sems.at[a])

    def start():
        for a in range(n):
            own(a).start()
            for px, py in chips:
                remote(a, _half_of(sh[a], specs[a], c), _region(land[a], specs[a], chip_me, c), (px, py, c)).start()

    def forward():
        for a in range(n):
            remote(a, three_halves(a), three_halves(a), sibling).wait_recv()
            for px, py in chips:
                landed = _region(land[a], specs[a], 2 * px + py, c)
                remote(n + a, landed, landed, sibling).start()

    def finish():
        for a in range(n):
            remote(n + a, three_halves(a), three_halves(a), sibling).wait_recv()
            remote(a, three_halves(a), three_halves(a), sibling).wait_send()
            remote(n + a, three_halves(a), three_halves(a), sibling).wait_send()
            own(a).wait()
        for a in range(n):
            out(a).start()
        for a in range(n):
            out(a).wait()

    return start, forward, finish


def _scatter_direct(names, part, slots, send_sems, recv_sems, local_sems):
    specs = [W_SPECS[k][1] for k in names]
    x, y, c = _my_place()
    dev_me = 4 * x + 2 * y + c

    def local(a):
        return pltpu.make_async_copy(_region(part[a], specs[a], 2 * x + y, c), slots[a].at[dev_me], local_sems.at[a])

    def start():
        for a in range(len(names)):
            local(a).start()
            for flip in range(1, N_DEV):
                tx = 1 - x if flip >> 2 else x
                ty = 1 - y if (flip >> 1) & 1 else y
                tc = 1 - c if flip & 1 else c
                pltpu.make_async_remote_copy(src_ref=_region(part[a], specs[a], 2 * tx + ty, tc), dst_ref=slots[a].at[dev_me],
                                             send_sem=send_sems.at[a], recv_sem=recv_sems.at[a], device_id=(tx, ty, tc),
                                             device_id_type=MESH).start()

    def wait():
        for a in range(len(names)):
            seven = slots[a].at[pl.ds(0, N_DEV - 1)]
            pltpu.make_async_remote_copy(src_ref=seven, dst_ref=seven, send_sem=send_sems.at[a], recv_sem=recv_sems.at[a],
                                         device_id=(x, y, c), device_id_type=MESH).wait()
            local(a).wait()

    return start, wait


HBM_SPEC = pl.BlockSpec(memory_space=pltpu.HBM)
SEM_SPEC = pl.BlockSpec(memory_space=pltpu.SEMAPHORE)
SPLIT_EFFECT = pltpu.SideEffectType.DATAFLOW_SIDE_EFFECTING


def _scatter_start(part, slots, after, name_key):
    _, cols, _, _ = W_SPECS[name_key][1]

    def body(part_ref, slots_ref, after_ref, send_sem, recv_sem, part_thru, slots_thru, token):
        x, y, c = _my_place()
        for tx, ty in ((1 - x, y), (x, 1 - y), (1 - x, 1 - y)):
            src = part_ref.at[:, pl.ds(_al((2 * tx + ty) * cols, cols), cols)]
            pltpu.make_async_remote_copy(src_ref=src, dst_ref=slots_ref.at[2 * x + y], send_sem=send_sem, recv_sem=recv_sem,
                                         device_id=(tx, ty, c), device_id_type=MESH).start()
        token[...] = jnp.zeros_like(token)

    return pl.pallas_call(
        body, name="scatter_start_" + name_key,
        out_shape=(pltpu.SemaphoreType.DMA(()), pltpu.SemaphoreType.DMA(()), pltpu.HBM(part.shape, part.dtype),
                   pltpu.HBM(slots.shape, slots.dtype), jax.ShapeDtypeStruct((8, 128), F32)),
        in_specs=(HBM_SPEC, HBM_SPEC, ANY), out_specs=(SEM_SPEC, SEM_SPEC, HBM_SPEC, HBM_SPEC, VMEM),
        input_output_aliases={0: 2, 1: 3},
        compiler_params=pltpu.CompilerParams(has_side_effects=SPLIT_EFFECT),
    )(pltpu.with_memory_space_constraint(part, pltpu.HBM), pltpu.with_memory_space_constraint(slots, pltpu.HBM), after)


def _scatter_wait(send_sem, recv_sem, part_thru, slots_thru, after, name_key):
    def body(part_ref, slots_ref, send_sem, recv_sem, *rest):
        x, y, c = _my_place()
        landed = slots_ref.at[pl.ds(0, N_CHIPS - 1)]
        copy = pltpu.make_async_remote_copy(src_ref=landed, dst_ref=landed, send_sem=send_sem, recv_sem=recv_sem,
                                            device_id=(x, y, c), device_id_type=MESH)
        copy.wait_send()
        copy.wait_recv()

    return pl.pallas_call(
        body, name="scatter_wait_" + name_key,
        out_shape=(pltpu.HBM(part_thru.shape, part_thru.dtype), pltpu.HBM(slots_thru.shape, slots_thru.dtype)),
        in_specs=(HBM_SPEC, HBM_SPEC, SEM_SPEC, SEM_SPEC) + (ANY,) * len(after), out_specs=(HBM_SPEC, HBM_SPEC),
        input_output_aliases={0: 0, 1: 1},
        compiler_params=pltpu.CompilerParams(has_side_effects=SPLIT_EFFECT),
    )(part_thru, slots_thru, send_sem, recv_sem, *after)


def _comm_sems(n):
    return [pltpu.SemaphoreType.DMA((n,)), pltpu.SemaphoreType.DMA((n,)), pltpu.SemaphoreType.DMA((n,))]


GATHER_EARLY = ("hgrn_w_in",)
GATHER_LATE = ("hgrn_w_out", "pool_w_in", "pool_w_grp", "pool_w_out")


def _weight_gather(sh, full, spec, send_sems, recv_sems, local_sem):
    x, y, c = _my_place()
    chip_me = 2 * x + y
    sibling = (x, y, 1 - c)
    chips = [(1 - x, y), (x, 1 - y), (1 - x, 1 - y)]

    def remote(k, src, dst, to):
        return pltpu.make_async_remote_copy(src_ref=src, dst_ref=dst, send_sem=send_sems.at[k], recv_sem=recv_sems.at[k],
                                            device_id=to, device_id_type=MESH)

    def own():
        return pltpu.make_async_copy(sh, _region(full, spec, chip_me, None), local_sem)

    def direct():
        return [remote(j, _half_of(sh, spec, c), _region(full, spec, chip_me, c), (px, py, c)) for j, (px, py) in enumerate(chips)]

    def start():
        own().start()
        for cp in direct():
            cp.start()

    def finish():
        passed = []
        for j, (px, py) in enumerate(chips):
            landed = _region(full, spec, 2 * px + py, c)
            remote(j, landed, landed, (px, py, c)).wait_recv()
            passed.append(remote(3 + j, landed, landed, sibling))
            passed[-1].start()
        for j, (px, py) in enumerate(chips):
            other = _region(full, spec, 2 * px + py, 1 - c)
            remote(3 + j, other, other, sibling).wait_recv()
        for cp in direct() + passed:
            cp.wait_send()
        own().wait()

    return start, finish


def _prologue(first, c_ctx, ada_w, ada_b_cols, w_shard):
    n_layers, _, n_cols = ada_w.shape
    key = GATHER_EARLY[0]
    spec = W_SPECS[key][1]
    m_first = first.shape[0]

    def body(first_ref, cctx_ref, adaw_ref, adab_ref, sh_ref, first_all, parts_all, full_ref, parts_scr,
             s1, r1, l1, s2, r2, l2, ws, wr, wl):
        start_first, finish_first = _small_gather(first_ref, first_all, s1, r1, l1)
        start_parts, finish_parts = _small_gather(parts_scr, parts_all, s2, r2, l2)
        start_weight, finish_weight = _weight_gather(sh_ref, full_ref, spec, ws, wr, wl)
        start_first()
        start_weight()
        finish_first()
        cond = jnp.concatenate([first_all[m_first * d:m_first * d + 1, 0:D_MODEL] for d in range(N_DEV)]
                               + [cctx_ref[...], jnp.zeros((16 - N_DEV - 1, D_MODEL), F32)], axis=0)
        act = _silu(cond)
        for i in range(n_layers):
            parts_scr[16 * i:16 * (i + 1), :] = _mm_f32(act, adaw_ref[i]) + adab_ref[i]
        start_parts()
        finish_parts()
        finish_weight()

    seven = [pltpu.SemaphoreType.DMA((7,)), pltpu.SemaphoreType.DMA((7,)), pltpu.SemaphoreType.DMA]
    return pl.pallas_call(
        body, name="prologue",
        out_shape=[jax.ShapeDtypeStruct((N_DEV * m_first, first.shape[1]), F32),
                   jax.ShapeDtypeStruct((N_DEV * 16 * n_layers, n_cols), F32), jax.ShapeDtypeStruct(W_SPECS[key][0], BF16)],
        in_specs=[VMEM] * 5, out_specs=[VMEM] * 3,
        scratch_shapes=[pltpu.VMEM((16 * n_layers, n_cols), F32)] + seven + seven
        + [pltpu.SemaphoreType.DMA((6,)), pltpu.SemaphoreType.DMA((6,)), pltpu.SemaphoreType.DMA],
        compiler_params=_params(48),
    )(first, c_ctx, ada_w, ada_b_cols.reshape(n_layers, 1, n_cols), w_shard)


def _slot_shapes(names):
    return [jax.ShapeDtypeStruct((N_DEV,) + _half_shape(k), BF16) for k in names]


def _sum_and_exchange(slots, names, name):
    n = len(names)
    specs = [W_SPECS[k][1] for k in names]

    def shard_shape(k):
        shp = list(_half_shape(k))
        shp[W_SPECS[k][1][2]] *= 2
        return tuple(shp)

    def body(*refs):
        slot, out, half = refs[:n], refs[n:2 * n], refs[2 * n:3 * n]
        send_sems, recv_sems, local_sems = refs[3 * n:]
        x, y, c = _my_place()
        sibling = (x, y, 1 - c)

        def remote(a, src, dst):
            return pltpu.make_async_remote_copy(src_ref=src, dst_ref=dst, send_sem=send_sems.at[a], recv_sem=recv_sems.at[a],
                                                device_id=sibling, device_id_type=MESH)

        local = [pltpu.make_async_copy(half[a], _half_of(out[a], specs[a], c), local_sems.at[a]) for a in range(n)]
        for a in range(n):
            acc = slot[a][0].astype(F32)
            for d in range(1, slot[a].shape[0]):
                acc = acc + slot[a][d].astype(F32)
            half[a][...] = acc
            local[a].start()
            remote(a, half[a], _half_of(out[a], specs[a], c)).start()
        for a in range(n):
            theirs = _half_of(out[a], specs[a], 1 - c)
            remote(a, theirs, theirs).wait_recv()
            remote(a, half[a], half[a]).wait_send()
            local[a].wait()

    return pl.pallas_call(
        body, name=name,
        out_shape=[jax.ShapeDtypeStruct(shard_shape(k), F32) for k in names],
        in_specs=[VMEM] * n, out_specs=[VMEM] * n,
        scratch_shapes=[pltpu.VMEM(_half_shape(k), F32) for k in names]
        + [pltpu.SemaphoreType.DMA((n,)), pltpu.SemaphoreType.DMA((n,)), pltpu.SemaphoreType.DMA((n,))],
        compiler_params=_params(40),
    )(*slots)


def _ln_mod_matmul(ctx_tile, xin, nw, shift, scale, w, name):
    n_mod = shift.shape[0]
    skip = n_mod - 1
    tm = ROW_TILE if skip else 2 * ROW_TILE
    rows = xin.shape[0] + skip * tm
    n_sec = w.shape[1] // D_MODEL

    def body(*refs):
        c_ref = refs[0] if skip else None
        x_ref, nw_ref, sh_ref, sc_ref, w_ref, g_ref, h_ref = refs[skip:]
        xv = jnp.where(pl.program_id(0) == 0, c_ref[...], x_ref[...]) if skip else x_ref[...]
        r = lax.rsqrt(jnp.mean(xv * xv, axis=-1, keepdims=True) + EPS)
        h_f32 = (xv * r * nw_ref[...]) * (1.0 + sc_ref[...]) + sh_ref[...]
        h = h_f32.astype(BF16)
        h_ref[...] = h
        for k in range(n_sec):
            g_ref[k] = jnp.dot(h, w_ref[:, k * D_MODEL:(k + 1) * D_MODEL], preferred_element_type=F32)

    mod_spec = pl.BlockSpec((None, 1, D_MODEL), lambda i: (jnp.minimum(i, n_mod - 1), 0, 0))
    return pl.pallas_call(
        body, name=name, grid=(rows // tm,),
        out_shape=[pltpu.HBM((n_sec, rows, D_MODEL), F32), pltpu.HBM((rows, D_MODEL), BF16)],
        in_specs=[pl.BlockSpec((tm, D_MODEL), lambda i: (0, 0))] * skip
        + [pl.BlockSpec((tm, D_MODEL), lambda i: (jnp.maximum(i - skip, 0), 0)),
           pl.BlockSpec((1, D_MODEL), lambda i: (0, 0)),
           mod_spec, mod_spec,
           pl.BlockSpec((D_MODEL, n_sec * D_MODEL), lambda i: (0, 0))],
        out_specs=[pl.BlockSpec((n_sec, tm, D_MODEL), lambda i: (0, i, 0)),
                   pl.BlockSpec((tm, D_MODEL), lambda i: (i, 0))],
        compiler_params=_params(48),
    )(*_pin(*([ctx_tile] * skip), xin, nw, shift.reshape(n_mod, 1, D_MODEL), scale.reshape(n_mod, 1, D_MODEL), w))


def _gates(fpre, lb):
    sg = _sig(fpre)
    f = lb + (1.0 - lb) * sg
    return sg, f, 1.0 - f, jnp.log(f)


G_SPEC = lambda sec: pl.BlockSpec((None, ROWS_ALL, HEAD_DIM), lambda h, sec=sec: (sec, 0, h))


GROUP = 2 * CHUNK


def _group_masks():
    rid, cid = np.arange(GROUP)[:, None], np.arange(GROUP)[None, :]
    same = (rid >= CHUNK) == (cid >= CHUNK)
    sums, back = [], []
    for rev in (False, True):
        causal = (cid >= rid) if rev else (cid <= rid)
        anti = (cid <= rid) if rev else (cid >= rid)
        sums.append(same & causal)
        back.append(np.concatenate([same & anti, same & ~anti], axis=1))
    sums, back = np.stack(sums).astype(np.float32), np.stack(back).astype(np.float32)
    return jnp.asarray(sums, BF16), jnp.asarray(back, BF16), jnp.asarray(sums, F32)


MASK_SPECS = [pl.BlockSpec((2, GROUP, GROUP), lambda h: (0, 0, 0)), pl.BlockSpec((2, GROUP, 2 * GROUP), lambda h: (0, 0, 0)),
              pl.BlockSpec((2, GROUP, GROUP), lambda h: (0, 0, 0))]


def _group_sum(m_bf, a):
    hi, lo = _split2(a)
    r = jnp.dot(m_bf, jnp.concatenate([hi, lo], axis=1), preferred_element_type=F32)
    return r[:, :HEAD_DIM] + r[:, HEAD_DIM:]


def _chunk_row(a, pos):
    return jnp.concatenate([jnp.broadcast_to(a[c * CHUNK + pos:c * CHUNK + pos + 1, :], (CHUNK, HEAD_DIM)) for c in range(2)], axis=0)


def _by_chunk(a, second):
    return jnp.concatenate([jnp.where(second, 0.0, a), jnp.where(second, a, 0.0)], axis=1)


def _own_block(r):
    return jnp.concatenate([r[0:CHUNK, 0:HEAD_DIM], r[CHUNK:GROUP, HEAD_DIM:2 * HEAD_DIM]], axis=0)


def _scan_step_of(row_chunk, rev, latent):
    if not rev:
        return row_chunk
    return (N_CHUNKS + N_CTX_CHUNKS - 1 - row_chunk) if latent else (N_CTX_CHUNKS - 1 - row_chunk)


def _group_rows(i, j, per_step, latent):
    base = CTX_LEN if latent else 0
    return pl.multiple_of(base + (i * per_step + j) * GROUP, GROUP)


GROUPS_PER_STEP = 8
GROUPS_PER_GRAD_STEP = 4


def _gla_forward(g5, lb_logits, late_shards):
    q_scale = HEAD_DIM ** -0.5
    per_lat, per_ctx = GROUPS_PER_STEP, min(GROUPS_PER_STEP, CTX_LEN // GROUP)
    n_late = len(GATHER_LATE)

    def body(ff_ref, fb_ref, v_ref, q_ref, lg_ref, sums_ref, _, keep_ref, *rest):
        shard_refs, o_ref, full_refs = rest[:n_late], rest[n_late], rest[n_late + 1:2 * n_late + 1]
        st_ref, decay_ref, qt_ref = rest[2 * n_late + 1:2 * n_late + 4]
        land_refs = rest[2 * n_late + 4:3 * n_late + 4]
        start_gather, forward_gather, finish_gather = _gather_two_level(GATHER_LATE, shard_refs, land_refs, full_refs,
                                                                        *rest[3 * n_late + 4:])

        @pl.when(pl.program_id(0) == 0)
        def _():
            start_gather()

        @pl.when(pl.program_id(0) == HEADS - 1)
        def _():
            forward_gather()

        second = lax.broadcasted_iota(jnp.int32, (GROUP, HEAD_DIM), 0) >= CHUNK
        for rev in (False, True):
            f_ref = fb_ref if rev else ff_ref
            lb = _sig(lg_ref[1:2, :] if rev else lg_ref[0:1, :])
            d = 1 if rev else 0
            last = 0 if rev else CHUNK - 1
            mid = CHUNK // 2 if rev else CHUNK // 2 - 1

            def local_step(i, carry, latent, per):
                r0s = [_group_rows(i, j, per, latent) for j in range(per)]
                rows = [pl.ds(r0, GROUP) for r0 in r0s]
                gates = [_gates(f_ref[r, :], lb) for r in rows]
                vs = [v_ref[r, :] for r in rows]
                bs = [_group_sum(sums_ref[d], g[3]) for g in gates]
                bls = [_chunk_row(b, last) for b in bs]
                ups = [_mm_tn(v, _by_chunk(g[2] * jnp.exp(bl - b), second)) for v, g, b, bl in zip(vs, gates, bs, bls)]
                if latent:
                    qs = [_silu(q_ref[r, :]) * q_scale for r in rows]
                    bms = [_chunk_row(b, mid) for b in bs]
                    a_s = [_mm_nt(q * jnp.exp(b - bm), g[2] * jnp.exp(bm - b)) for q, g, b, bm in zip(qs, gates, bs, bms)]
                    outs = [_mm(a * keep_ref[d], v) for a, v in zip(a_s, vs)]
                for j in range(per):
                    for c in range(2):
                        step = _scan_step_of(r0s[j] // CHUNK + c, rev, latent)
                        decay_ref[step] = jnp.exp(bls[j][c * CHUNK:c * CHUNK + 1, :])
                        st_ref[step] = ups[j][:, c * HEAD_DIM:(c + 1) * HEAD_DIM]
                    if latent:
                        orow = pl.ds(pl.multiple_of(r0s[j] - CTX_LEN, GROUP), GROUP)
                        qt_ref[orow, :] = (qs[j] * jnp.exp(bs[j])).astype(BF16)
                        if rev:
                            o_ref[orow, :] += outs[j]
                        else:
                            o_ref[orow, :] = outs[j]
                return carry

            lax.fori_loop(0, CTX_LEN // (per_ctx * GROUP), functools.partial(local_step, latent=False, per=per_ctx), 0)
            lax.fori_loop(0, SEQ // (per_lat * GROUP), functools.partial(local_step, latent=True, per=per_lat), 0)

            def scan_step(t, st):
                update = st_ref[t]
                st_ref[t] = st
                return st * decay_ref[t] + update

            lax.fori_loop(0, N_CHUNKS, scan_step, jnp.zeros((HEAD_DIM, HEAD_DIM), F32), unroll=2)

            def inter_step(i, carry):
                r0s = [_group_rows(i, j, per_lat, True) for j in range(per_lat)]
                orows = [pl.ds(pl.multiple_of(r0 - CTX_LEN, GROUP), GROUP) for r0 in r0s]
                states = [jnp.concatenate([st_ref[_scan_step_of(r0 // CHUNK + c, rev, True)] for c in range(2)], axis=0)
                          for r0 in r0s]
                prods = [lax.dot_general(qt_ref[orow, :], s.astype(BF16), (((1,), (1,)), ((), ())), preferred_element_type=F32)
                         for orow, s in zip(orows, states)]
                for orow, r in zip(orows, prods):
                    o_ref[orow, :] += _own_block(r)
                return carry

            lax.fori_loop(0, SEQ // (per_lat * GROUP), inter_step, 0)

        @pl.when(pl.program_id(0) == HEADS - 1)
        def _():
            finish_gather()

    outs = pl.pallas_call(
        body, name="gla_forward", grid=(HEADS,),
        out_shape=[pltpu.HBM((SEQ, D_MODEL), F32)] + [jax.ShapeDtypeStruct(W_SPECS[k][0], BF16) for k in GATHER_LATE],
        in_specs=[G_SPEC(0), G_SPEC(1), G_SPEC(2), G_SPEC(3), pl.BlockSpec((2, HEAD_DIM), lambda h: (0, h))] + MASK_SPECS
        + [ANY] * n_late,
        out_specs=[pl.BlockSpec((SEQ, HEAD_DIM), lambda h: (0, h))] + [ANY] * n_late,
        scratch_shapes=[pltpu.VMEM((N_CHUNKS, HEAD_DIM, HEAD_DIM), F32), pltpu.VMEM((N_CHUNKS, 1, HEAD_DIM), F32),
                        pltpu.VMEM((SEQ, HEAD_DIM), BF16)] + [pltpu.VMEM(W_SPECS[k][0], BF16) for k in GATHER_LATE]
        + [pltpu.SemaphoreType.DMA((2 * n_late,)), pltpu.SemaphoreType.DMA((2 * n_late,)), pltpu.SemaphoreType.DMA((n_late,))],
        compiler_params=_params(48),
    )(*_pin(g5, g5, g5, g5, lb_logits), *_group_masks(), *late_shards)
    return outs[0], dict(zip(GATHER_LATE, outs[1:]))


def _gla_backward(g5, lb_logits, d_o, d_z, late_parts):
    q_scale = HEAD_DIM ** -0.5
    per_lat, per_ctx = GROUPS_PER_STEP, min(GROUPS_PER_STEP, CTX_LEN // GROUP)
    n_late = len(GATHER_LATE)

    def body(ff_ref, fb_ref, v_ref, q_ref, lg_ref, do_ref, dz_ref, sums_ref, back_ref, keep_ref, *rest):
        part_refs, (dg_ref, dlg_ref), slot_refs = rest[:n_late], rest[n_late:n_late + 2], rest[n_late + 2:2 * n_late + 2]
        st_ref, dst_ref, decay_ref, both_ref = rest[2 * n_late + 2:2 * n_late + 6]
        start_scatter, wait_scatter = _scatter_direct(GATHER_LATE, part_refs, slot_refs, *rest[2 * n_late + 6:])

        @pl.when(pl.program_id(0) == 0)
        def _():
            start_scatter()

        dg_ref[3, 0:CTX_LEN, :] = jnp.zeros((CTX_LEN, HEAD_DIM), BF16)
        dg_ref[4, 0:CTX_LEN, :] = jnp.zeros((CTX_LEN, HEAD_DIM), BF16)
        dg_ref[4, CTX_LEN:ROWS_ALL, :] = dz_ref[...]
        second = lax.broadcasted_iota(jnp.int32, (GROUP, HEAD_DIM), 0) >= CHUNK
        for rev in (False, True):
            d = 1 if rev else 0
            f_ref = fb_ref if rev else ff_ref
            lb = _sig(lg_ref[d:d + 1, :])
            last = 0 if rev else CHUNK - 1
            mid = CHUNK // 2 if rev else CHUNK // 2 - 1

            def local_step(i, carry, latent, per):
                r0s = [_group_rows(i, j, per, latent) for j in range(per)]
                rows = [pl.ds(r0, GROUP) for r0 in r0s]
                gates = [_gates(f_ref[r, :], lb) for r in rows]
                bs = [_group_sum(sums_ref[d], g[3]) for g in gates]
                bls = [_chunk_row(b, last) for b in bs]
                ups = [_mm_tn(v_ref[r, :], _by_chunk(g[2] * jnp.exp(bl - b), second)) for r, g, b, bl in zip(rows, gates, bs, bls)]
                if latent:
                    orows = [pl.ds(pl.multiple_of(r0 - CTX_LEN, GROUP), GROUP) for r0 in r0s]
                    d_ups = [_mm_tn(do_ref[orow, :], _by_chunk(_silu(q_ref[r, :]) * q_scale * jnp.exp(b), second))
                             for orow, r, b in zip(orows, rows, bs)]
                for j in range(per):
                    for c in range(2):
                        step = _scan_step_of(r0s[j] // CHUNK + c, rev, latent)
                        decay_ref[step] = jnp.exp(bls[j][c * CHUNK:c * CHUNK + 1, :])
                        st_ref[step] = ups[j][:, c * HEAD_DIM:(c + 1) * HEAD_DIM]
                        if latent:
                            dst_ref[step] = d_ups[j][:, c * HEAD_DIM:(c + 1) * HEAD_DIM]
                        else:
                            dst_ref[step] = jnp.zeros((HEAD_DIM, HEAD_DIM), F32)
                return carry

            lax.fori_loop(0, CTX_LEN // (per_ctx * GROUP), functools.partial(local_step, latent=False, per=per_ctx), 0)
            lax.fori_loop(0, SEQ // (per_lat * GROUP), functools.partial(local_step, latent=True, per=per_lat), 0)

            def scan_step(i, carry):
                st, d_st = carry
                j = N_CHUNKS - 1 - i
                update, d_update = st_ref[i], dst_ref[j]
                st_ref[i] = st
                dst_ref[j] = d_st
                return st * decay_ref[i] + update, d_st * decay_ref[j] + d_update

            zero_state = jnp.zeros((HEAD_DIM, HEAD_DIM), F32)
            lax.fori_loop(0, N_CHUNKS, scan_step, (zero_state, zero_state))

            def grad_step(i, dlb, latent, per):
                r0s = [_group_rows(i, j, per, latent) for j in range(per)]
                rows = [pl.ds(r0, GROUP) for r0 in r0s]
                gates = [_gates(f_ref[r, :], lb) for r in rows]
                vs = [v_ref[r, :] for r in rows]
                bs = [_group_sum(sums_ref[d], g[3]) for g in gates]
                bls = [_chunk_row(b, last) for b in bs]
                e_ends = [jnp.exp(bl - b) for b, bl in zip(bs, bls)]
                k_ends = [g[2] * e for g, e in zip(gates, e_ends)]
                sts = [[st_ref[_scan_step_of(r0 // CHUNK + c, rev, latent)] for c in range(2)] for r0 in r0s]
                d_sts = [[dst_ref[_scan_step_of(r0 // CHUNK + c, rev, latent)] for c in range(2)] for r0 in r0s]
                d_kends = [_own_block(_mm(v, jnp.concatenate(ds, axis=1))) for v, ds in zip(vs, d_sts)]
                d_vs = [_own_block(_mm_nt(ke, jnp.concatenate(ds, axis=0))) for ke, ds in zip(k_ends, d_sts)]
                at_last = [jnp.concatenate([jnp.broadcast_to(jnp.sum(ds[c] * s[c], axis=0, keepdims=True), (CHUNK, HEAD_DIM))
                                            for c in range(2)], axis=0) * jnp.exp(bl) for ds, s, bl in zip(d_sts, sts, bls)]
                t_kends = [dk * ke for dk, ke in zip(d_kends, k_ends)]
                d_ks = [dk * e for dk, e in zip(d_kends, e_ends)]
                if latent:
                    orows = [pl.ds(pl.multiple_of(r0 - CTX_LEN, GROUP), GROUP) for r0 in r0s]
                    qpres = [q_ref[r, :] for r in rows]
                    q_sigs = [_sig(qp) for qp in qpres]
                    qs = [qp * sg * q_scale for qp, sg in zip(qpres, q_sigs)]
                    bms = [_chunk_row(b, mid) for b in bs]
                    e_bs = [jnp.exp(b) for b in bs]
                    e_qms = [jnp.exp(b - bm) for b, bm in zip(bs, bms)]
                    e_kms = [jnp.exp(bm - b) for b, bm in zip(bs, bms)]
                    q_ts = [q * e for q, e in zip(qs, e_bs)]
                    q_ms = [q * e for q, e in zip(qs, e_qms)]
                    k_ms = [g[2] * e for g, e in zip(gates, e_kms)]
                    d_outs = [do_ref[orow, :] for orow in orows]
                    a_s = [_mm_nt(qm, km) * keep_ref[d] for qm, km in zip(q_ms, k_ms)]
                    d_as = [_mm_nt(do, v) * keep_ref[d] for do, v in zip(d_outs, vs)]
                    d_qts = [_own_block(_mm(do, jnp.concatenate(s, axis=1))) for do, s in zip(d_outs, sts)]
                    d_qms = [_mm(da, km) for da, km in zip(d_as, k_ms)]
                    d_kms = [_mm_tn(da, qm) for da, qm in zip(d_as, q_ms)]
                    d_vs = [dv + _mm_tn(a, do) for dv, a, do in zip(d_vs, a_s, d_outs)]
                    d_ks = [dk + dkm * e for dk, dkm, e in zip(d_ks, d_kms, e_kms)]
                    d_bs = [jnp.concatenate([dqt * qt + dqm * qm - dkm * km, t], axis=0)
                            for dqt, qt, dqm, qm, dkm, km, t in zip(d_qts, q_ts, d_qms, q_ms, d_kms, k_ms, t_kends)]
                    d_qs = [dqt * eb + dqm * eq for dqt, eb, dqm, eq in zip(d_qts, e_bs, d_qms, e_qms)]
                    back = back_ref[d]
                else:
                    d_bs, back = t_kends, back_ref[d, :, GROUP:2 * GROUP]
                d_lfs = [_group_sum(back, db) + al for db, al in zip(d_bs, at_last)]
                for j in range(per):
                    sg, f = gates[j][0], gates[j][1]
                    d_f = d_lfs[j] / f - d_ks[j]
                    dg_ref[d, rows[j], :] = (d_f * (1.0 - lb) * sg * (1.0 - sg)).astype(BF16)
                    dlb = dlb + jnp.sum(d_f * (1.0 - sg), axis=0, keepdims=True)
                    if rev:
                        dg_ref[2, rows[j], :] = (both_ref[0, rows[j], :] + d_vs[j]).astype(BF16)
                    else:
                        both_ref[0, rows[j], :] = d_vs[j]
                    if latent:
                        d_qpre = d_qs[j] * q_scale * (q_sigs[j] * (1.0 + qpres[j] * (1.0 - q_sigs[j])))
                        if rev:
                            dg_ref[3, rows[j], :] = (both_ref[1, rows[j], :] + d_qpre).astype(BF16)
                        else:
                            both_ref[1, rows[j], :] = d_qpre
                return dlb

            dlb = lax.fori_loop(0, SEQ // (GROUPS_PER_GRAD_STEP * GROUP),
                                functools.partial(grad_step, latent=True, per=GROUPS_PER_GRAD_STEP), jnp.zeros((1, HEAD_DIM), F32))
            dlb = lax.fori_loop(0, CTX_LEN // (per_ctx * GROUP), functools.partial(grad_step, latent=False, per=per_ctx), dlb)
            dlg_ref[d:d + 1, :] = dlb * lb * (1.0 - lb)

        @pl.when(pl.program_id(0) == HEADS - 1)
        def _():
            wait_scatter()

    col = pl.BlockSpec((SEQ, HEAD_DIM), lambda h: (0, h))
    outs = pl.pallas_call(
        body, name="gla_backward", grid=(HEADS,),
        out_shape=[pltpu.HBM((HGRN_SECTIONS, ROWS_ALL, D_MODEL), BF16), jax.ShapeDtypeStruct((2, D_MODEL), F32)]
        + _slot_shapes(GATHER_LATE),
        in_specs=[G_SPEC(0), G_SPEC(1), G_SPEC(2), G_SPEC(3), pl.BlockSpec((2, HEAD_DIM), lambda h: (0, h)), col, col]
        + MASK_SPECS + [ANY] * n_late,
        out_specs=[pl.BlockSpec((HGRN_SECTIONS, ROWS_ALL, HEAD_DIM), lambda h: (0, 0, h)),
                   pl.BlockSpec((2, HEAD_DIM), lambda h: (0, h))] + [ANY] * n_late,
        scratch_shapes=[pltpu.VMEM((N_CHUNKS, HEAD_DIM, HEAD_DIM), F32), pltpu.VMEM((N_CHUNKS, HEAD_DIM, HEAD_DIM), F32),
                        pltpu.VMEM((N_CHUNKS, 1, HEAD_DIM), F32), pltpu.VMEM((2, ROWS_ALL, HEAD_DIM), F32)]
        + _comm_sems(n_late),
        compiler_params=_params(48),
    )(*_pin(g5, g5, g5, g5, lb_logits, d_o, d_z), *_group_masks(), *late_parts)
    return outs[0], outs[1], dict(zip(GATHER_LATE, outs[2:]))


def _head_norm(o, scr):
    rs = []
    for h in range(HEADS):
        cols = slice(h * HEAD_DIM, (h + 1) * HEAD_DIM)
        oh = o[:, cols]
        r = lax.rsqrt(jnp.mean(oh * oh, axis=-1, keepdims=True) + EPS)
        scr[:, cols] = oh * r
        rs.append(r)
    return rs


def _hgrn_out_forward(o_raw, g5, xin, gnorm_w, gate, w_out):
    tm = ROW_TILE

    def body(o_ref, z_ref, x_ref, gw_ref, gate_ref, w_ref, x1_ref, res_ref, scr):
        _head_norm(o_ref[...], scr)
        a = scr[...] * gw_ref[...] * _silu(z_ref[...])
        res = _mm(a, w_ref[...])
        res_ref[...] = res
        x1_ref[...] = x_ref[...] + gate_ref[...] * res

    tile = pl.BlockSpec((tm, D_MODEL), lambda i: (i, 0))
    vec = pl.BlockSpec((1, D_MODEL), lambda i: (0, 0))
    return pl.pallas_call(
        body, name="hgrn_out_forward", grid=(SEQ // tm,),
        out_shape=[pltpu.HBM((SEQ, D_MODEL), F32)] * 2,
        in_specs=[tile, pl.BlockSpec((None, tm, D_MODEL), lambda i: (4, i + CTX_LEN // tm, 0)), tile, vec, vec,
                  pl.BlockSpec((D_MODEL, D_MODEL), lambda i: (0, 0))],
        out_specs=[tile, tile],
        scratch_shapes=[pltpu.VMEM((tm, D_MODEL), F32)],
        compiler_params=_params(32),
    )(*_pin(o_raw, g5, xin, gnorm_w, gate, w_out))


def _hgrn_out_backward(d_x1, o_raw, g5, res, gnorm_w, gate, w_out):
    tm = ROW_TILE

    def body(dx_ref, o_ref, z_ref, res_ref, gw_ref, gate_ref, w_ref, do_ref, dz_ref, dw_out, dgate_ref, dgw_ref, scr, scr2,
             dw_ref):
        @pl.when(pl.program_id(0) == 0)
        def _():
            dw_ref[...] = jnp.zeros_like(dw_ref)
            dgate_ref[...] = jnp.zeros_like(dgate_ref)
            dgw_ref[...] = jnp.zeros_like(dgw_ref)

        dx = dx_ref[...]
        dgate_ref[...] += jnp.sum(dx * res_ref[...], axis=0, keepdims=True)
        d_res = (dx * gate_ref[...]).astype(BF16)
        d_a = _mm_nt(d_res, w_ref[...])
        rs = _head_norm(o_ref[...], scr)
        z = z_ref[...]
        sz = _silu(z)
        o_hat = scr[...]
        o_n = o_hat * gw_ref[...]
        dw_ref[...] += _mm_tn(o_n * sz, d_res)
        d_on = d_a * sz
        dz_ref[...] = (d_a * o_n * _dsilu(z)).astype(BF16)
        dgw_ref[...] += jnp.sum(d_on * o_hat, axis=0, keepdims=True)
        scr2[...] = d_on * gw_ref[...]
        for h in range(HEADS):
            cols = slice(h * HEAD_DIM, (h + 1) * HEAD_DIM)
            dh, oh = scr2[:, cols], scr[:, cols]
            do_ref[:, cols] = (rs[h] * (dh - oh * jnp.mean(dh * oh, axis=-1, keepdims=True))).astype(BF16)

        @pl.when(pl.program_id(0) == SEQ // tm - 1)
        def _():
            dw_out[...] = dw_ref[...].astype(BF16)

    tile = pl.BlockSpec((tm, D_MODEL), lambda i: (i, 0))
    vec = pl.BlockSpec((1, D_MODEL), lambda i: (0, 0))
    mat = pl.BlockSpec((D_MODEL, D_MODEL), lambda i: (0, 0))
    return pl.pallas_call(
        body, name="hgrn_out_backward", grid=(SEQ // tm,),
        out_shape=[pltpu.HBM((SEQ, D_MODEL), BF16)] * 2 + [pltpu.HBM((D_MODEL, D_MODEL), BF16)]
        + [jax.ShapeDtypeStruct((1, D_MODEL), F32)] * 2,
        in_specs=[tile, tile, pl.BlockSpec((None, tm, D_MODEL), lambda i: (4, i + CTX_LEN // tm, 0)), tile, vec, vec, mat],
        out_specs=[tile, tile, mat, vec, vec],
        scratch_shapes=[pltpu.VMEM((tm, D_MODEL), F32)] * 2 + [pltpu.VMEM((D_MODEL, D_MODEL), F32)],
        compiler_params=_params(40),
    )(*_pin(d_x1, o_raw, g5, res, gnorm_w, gate, w_out))


def _pool_constants():
    win = np.zeros((POOL_GROUPS, ROW_TILE, ROW_TILE), np.float32)
    inv = np.zeros((POOL_GROUPS, ROW_TILE, 1), np.float32)
    for g, w in enumerate(POOL_WINDOWS):
        for t in range(ROW_TILE):
            base, p = (t // GRID_W) * GRID_W, t % GRID_W
            lo = min(max(p - w // 2, 0), GRID_W)
            hi = min(max(p - w // 2 + w, 0), GRID_W)
            win[g, t, base + lo:base + hi] = 1.0
            inv[g, t, 0] = 1.0 / np.float32(hi - lo)
    return jnp.asarray(win, BF16), jnp.asarray(win.transpose(0, 2, 1), BF16), jnp.asarray(inv, F32)


def _pool_mix(u_ref, wg_ref, win_ref, inv_ref, pooled_scr, yg_scr):
    for g in range(POOL_GROUPS):
        cols = slice(g * POOL_GROUP_DIM, (g + 1) * POOL_GROUP_DIM)
        for r in range(u_ref.shape[0] // ROW_TILE):
            rows = slice(r * ROW_TILE, (r + 1) * ROW_TILE)
            ug = u_ref[rows, cols]
            pooled_scr[rows, cols] = _mm_exact_lhs(win_ref[g], ug) * inv_ref[g] - ug
        yg_scr[:, cols] = _mm(pooled_scr[:, cols], wg_ref[g])


def _pool_forward_loss(uz, x1, target, gate, w_grp, pool_scale, w_out, final_w):
    tm = POOL_TILE
    win, _, inv = _pool_constants()

    def body(u_ref, z_ref, x_ref, t_ref, gate_ref, wg_ref, ps_ref, w_ref, fw_ref, win_ref, inv_ref,
             dx_ref, loss_ref, dfw_ref, dgate_ref, yg_scr, pooled_scr):
        @pl.when(pl.program_id(0) == 0)
        def _():
            loss_ref[...] = jnp.zeros_like(loss_ref)
            dfw_ref[...] = jnp.zeros_like(dfw_ref)
            dgate_ref[...] = jnp.zeros_like(dgate_ref)

        _pool_mix(u_ref, wg_ref, win_ref, inv_ref, pooled_scr, yg_scr)
        a = yg_scr[...] * ps_ref[...] * _silu(z_ref[...])
        res = _mm(a, w_ref[...])
        x2 = x_ref[...] + gate_ref[...] * res
        r = lax.rsqrt(jnp.mean(x2 * x2, axis=-1, keepdims=True) + EPS)
        xh = x2 * r
        fw = fw_ref[...]
        err = xh * fw - t_ref[...]
        loss_ref[...] += 0.5 * jnp.sum(jnp.mean(err * err, axis=-1, keepdims=True))
        d_y = err * (1.0 / D_MODEL)
        dfw_ref[...] += jnp.sum(d_y * xh, axis=0, keepdims=True)
        d_xh = d_y * fw
        d_x2 = r * (d_xh - xh * jnp.mean(d_xh * xh, axis=-1, keepdims=True))
        dx_ref[...] = d_x2
        dgate_ref[...] += jnp.sum(d_x2 * res, axis=0, keepdims=True)

    tile = pl.BlockSpec((tm, D_MODEL), lambda i: (i, 0))
    vec = pl.BlockSpec((1, D_MODEL), lambda i: (0, 0))
    grp = pl.BlockSpec((POOL_GROUPS, POOL_GROUP_DIM, POOL_GROUP_DIM), lambda i: (0, 0, 0))
    return pl.pallas_call(
        body, name="pool_forward_loss", grid=(SEQ // tm,),
        out_shape=[pltpu.HBM((SEQ, D_MODEL), F32), jax.ShapeDtypeStruct((8, 128), F32),
                   jax.ShapeDtypeStruct((1, D_MODEL), F32), jax.ShapeDtypeStruct((1, D_MODEL), F32)],
        in_specs=[pl.BlockSpec((None, tm, D_MODEL), lambda i: (0, i, 0)), pl.BlockSpec((None, tm, D_MODEL), lambda i: (1, i, 0)),
                  tile, tile, vec, grp, vec, pl.BlockSpec((D_MODEL, D_MODEL), lambda i: (0, 0)), vec, grp,
                  pl.BlockSpec((POOL_GROUPS, ROW_TILE, 1), lambda i: (0, 0, 0))],
        out_specs=[tile, pl.BlockSpec((8, 128), lambda i: (0, 0)), vec, vec],
        scratch_shapes=[pltpu.VMEM((tm, D_MODEL), F32)] * 2,
        compiler_params=_params(40),
    )(*_pin(uz, uz, x1, target, gate, w_grp, pool_scale, w_out, final_w, win, inv))


def _pool_backward(d_x2, uz, gate, w_grp, pool_scale, w_out):
    tm = POOL_TILE
    win, win_t, inv = _pool_constants()

    def body(dx_ref, u_ref, z_ref, gate_ref, wg_ref, ps_ref, w_ref, win_ref, wint_ref, inv_ref,
             duz_ref, dw_out, dwg_out, dps_ref, pooled_scr, yg_scr, dyg_scr, dw_ref, dwg_ref):
        @pl.when(pl.program_id(0) == 0)
        def _():
            dw_ref[...] = jnp.zeros_like(dw_ref)
            dwg_ref[...] = jnp.zeros_like(dwg_ref)
            dps_ref[...] = jnp.zeros_like(dps_ref)

        _pool_mix(u_ref, wg_ref, win_ref, inv_ref, pooled_scr, yg_scr)
        z = z_ref[...]
        sz = _silu(z)
        yg = yg_scr[...]
        y = yg * ps_ref[...]
        d_res = (dx_ref[...] * gate_ref[...]).astype(BF16)
        d_a = _mm_nt(d_res, w_ref[...])
        dw_ref[...] += _mm_tn(y * sz, d_res)
        d_y = d_a * sz
        duz_ref[1] = (d_a * y * _dsilu(z)).astype(BF16)
        dps_ref[...] += jnp.sum(d_y * yg, axis=0, keepdims=True)
        dyg_scr[...] = d_y * ps_ref[...]
        for g in range(POOL_GROUPS):
            cols = slice(g * POOL_GROUP_DIM, (g + 1) * POOL_GROUP_DIM)
            d_yg = dyg_scr[:, cols].astype(BF16)
            d_pool = _mm_nt(d_yg, wg_ref[g])
            dwg_ref[g] += _mm_tn(pooled_scr[:, cols], d_yg)
            for r in range(tm // ROW_TILE):
                rows = slice(r * ROW_TILE, (r + 1) * ROW_TILE)
                dp = d_pool[rows, :]
                duz_ref[0, rows, cols] = (_mm_exact_lhs(wint_ref[g], dp * inv_ref[g]) - dp).astype(BF16)

        @pl.when(pl.program_id(0) == SEQ // tm - 1)
        def _():
            dw_out[...] = dw_ref[...].astype(BF16)
            dwg_out[...] = dwg_ref[...].astype(BF16)

    tile = pl.BlockSpec((tm, D_MODEL), lambda i: (i, 0))
    vec = pl.BlockSpec((1, D_MODEL), lambda i: (0, 0))
    mat = pl.BlockSpec((D_MODEL, D_MODEL), lambda i: (0, 0))
    grp = pl.BlockSpec((POOL_GROUPS, POOL_GROUP_DIM, POOL_GROUP_DIM), lambda i: (0, 0, 0))
    return pl.pallas_call(
        body, name="pool_backward", grid=(SEQ // tm,),
        out_shape=[pltpu.HBM((POOL_SECTIONS, SEQ, D_MODEL), BF16), pltpu.HBM((D_MODEL, D_MODEL), BF16),
                   pltpu.HBM((POOL_GROUPS, POOL_GROUP_DIM, POOL_GROUP_DIM), BF16), jax.ShapeDtypeStruct((1, D_MODEL), F32)],
        in_specs=[tile, pl.BlockSpec((None, tm, D_MODEL), lambda i: (0, i, 0)), pl.BlockSpec((None, tm, D_MODEL), lambda i: (1, i, 0)),
                  vec, grp, vec, mat, grp, grp, pl.BlockSpec((POOL_GROUPS, ROW_TILE, 1), lambda i: (0, 0, 0))],
        out_specs=[pl.BlockSpec((POOL_SECTIONS, tm, D_MODEL), lambda i: (0, i, 0)), mat, grp, vec],
        scratch_shapes=[pltpu.VMEM((tm, D_MODEL), F32)] * 3 + [pltpu.VMEM((D_MODEL, D_MODEL), F32),
                                                               pltpu.VMEM((POOL_GROUPS, POOL_GROUP_DIM, POOL_GROUP_DIM), F32)],
        compiler_params=_params(48),
    )(*_pin(d_x2, uz, uz, gate, w_grp, pool_scale, w_out, win, win_t, inv))


def _ln_mod_backward(d_g, w, ctx_tile, xin, nw, scale, d_up, name):
    n_sec, rows, _ = d_g.shape
    n_mod = scale.shape[0]
    skip = n_mod - 1
    tm = ROW_TILE if skip else 2 * ROW_TILE
    n_tiles = rows // tm

    def body(dg_ref, w_ref, *refs):
        c_ref = refs[0] if skip else None
        x_ref, nw_ref, sc_ref, up_ref, dx_ref, dnw_ref, dmod_ref = refs[skip:]
        i = pl.program_id(0)

        @pl.when(i == 0)
        def _():
            dnw_ref[...] = jnp.zeros_like(dnw_ref)

        @pl.when((i == 0) | (i == skip))
        def _():
            dmod_ref[...] = jnp.zeros_like(dmod_ref)

        d_h = _mm_nt(dg_ref[0], w_ref[:, 0:D_MODEL])
        for k in range(1, n_sec):
            d_h = d_h + _mm_nt(dg_ref[k], w_ref[:, k * D_MODEL:(k + 1) * D_MODEL])
        xv = jnp.where(i == 0, c_ref[...], x_ref[...]) if skip else x_ref[...]
        r = lax.rsqrt(jnp.mean(xv * xv, axis=-1, keepdims=True) + EPS)
        xh = xv * r
        nw_row = nw_ref[...]
        dmod_ref[0:1, :] += jnp.sum(d_h, axis=0, keepdims=True)
        dmod_ref[1:2, :] += jnp.sum(d_h * (xh * nw_row), axis=0, keepdims=True)
        d_xn = d_h * (1.0 + sc_ref[...])
        dnw_ref[...] += jnp.sum(d_xn * xh, axis=0, keepdims=True)
        d_xh = d_xn * nw_row

        @pl.when(i >= skip)
        def _():
            dx_ref[...] = up_ref[...] + r * (d_xh - xh * jnp.mean(d_xh * xh, axis=-1, keepdims=True))

    lat = lambda i: (jnp.maximum(i - skip, 0), 0)
    mod_idx = lambda i: (jnp.minimum(i, n_mod - 1), 0, 0)
    return pl.pallas_call(
        body, name=name, grid=(n_tiles,),
        out_shape=[pltpu.HBM((rows - skip * tm, D_MODEL), F32), jax.ShapeDtypeStruct((1, D_MODEL), F32),
                   jax.ShapeDtypeStruct((n_mod, 8, D_MODEL), F32)],
        in_specs=[pl.BlockSpec((n_sec, tm, D_MODEL), lambda i: (0, i, 0)),
                  pl.BlockSpec((D_MODEL, n_sec * D_MODEL), lambda i: (0, 0))]
        + [pl.BlockSpec((tm, D_MODEL), lambda i: (0, 0))] * skip
        + [pl.BlockSpec((tm, D_MODEL), lat),
           pl.BlockSpec((1, D_MODEL), lambda i: (0, 0)),
           pl.BlockSpec((None, 1, D_MODEL), mod_idx),
           pl.BlockSpec((tm, D_MODEL), lat)],
        out_specs=[pl.BlockSpec((tm, D_MODEL), lat), pl.BlockSpec((1, D_MODEL), lambda i: (0, 0)),
                   pl.BlockSpec((None, 8, D_MODEL), mod_idx)],
        compiler_params=_params(48),
    )(*_pin(d_g, w, *([ctx_tile] * skip), xin, nw, scale.reshape(n_mod, 1, D_MODEL), d_up))


def _weight_grad(h, d_g, name):
    n_sec, rows, _ = d_g.shape
    tm = 768 if rows % 768 == 0 else 512
    n_tiles = rows // tm

    def body(h_ref, dg_ref, dw_ref, acc):
        i = pl.program_id(1)
        prod = _mm_tn(h_ref[...], dg_ref[...])

        @pl.when(i == 0)
        def _():
            acc[...] = prod

        @pl.when((i > 0) & (i < n_tiles - 1))
        def _():
            acc[...] += prod

        @pl.when(i == n_tiles - 1)
        def _():
            dw_ref[...] = (acc[...] + prod).astype(BF16)

    return pl.pallas_call(
        body, name=name, grid=(n_sec, n_tiles),
        out_shape=pltpu.HBM((D_MODEL, n_sec * D_MODEL), BF16),
        in_specs=[pl.BlockSpec((tm, D_MODEL), lambda j, i: (i, 0)), pl.BlockSpec((None, tm, D_MODEL), lambda j, i: (j, i, 0))],
        out_specs=pl.BlockSpec((D_MODEL, D_MODEL), lambda j, i: (0, j)),
        scratch_shapes=[pltpu.VMEM((D_MODEL, D_MODEL), F32)],
        compiler_params=_params(32),
    )(*_pin(h, d_g))


def _weight_grad_paired(h, d_g, name):
    n_sec, rows, _ = d_g.shape
    tm = 768
    n_tiles = rows // tm
    half = D_MODEL // 2
    chip_cols = n_sec * D_MODEL // N_CHIPS

    def body(h_ref, dg_ref, q_ref, slots_ref, acc, keep, send, land, send_sems, recv_sems, own_sem):
        j, i = pl.program_id(0), pl.program_id(1)
        x, y, c = _my_place()
        prod = _mm_tn(h_ref[...], dg_ref[...])

        def to_sibling(k):
            return pltpu.make_async_remote_copy(src_ref=send.at[k], dst_ref=land.at[k], send_sem=send_sems.at[k],
                                                recv_sem=recv_sems.at[k], device_id=(x, y, 1 - c), device_id_type=MESH)

        @pl.when(i == 0)
        def _():
            acc[...] = prod

        @pl.when((i > 0) & (i < n_tiles - 1))
        def _():
            acc[...] += prod

        @pl.when(i == n_tiles - 1)
        def _():
            acc[...] += prod
            keep[j] = acc[pl.ds(_al(c * half, half), half), :]
            send[j] = acc[pl.ds(_al((1 - c) * half, half), half), :].astype(BF16)
            to_sibling(j).start()

        @pl.when((j == n_sec - 1) & (i == n_tiles - 1))
        def _():
            for k in range(n_sec):
                to_sibling(k).wait()
                q_ref[:, k * D_MODEL:(k + 1) * D_MODEL] = (keep[k] + land[k].astype(F32)).astype(BF16)
            chip = 2 * x + y
            own = pltpu.make_async_copy(q_ref.at[:, pl.ds(_al(chip * chip_cols, chip_cols), chip_cols)], slots_ref.at[chip], own_sem)
            own.start()
            own.wait()

    return pl.pallas_call(
        body, name=name, grid=(n_sec, n_tiles),
        out_shape=[jax.ShapeDtypeStruct((half, n_sec * D_MODEL), BF16),
                   jax.ShapeDtypeStruct((N_CHIPS, half, chip_cols), BF16)],
        in_specs=[pl.BlockSpec((tm, D_MODEL), lambda j, i: (i, 0)), pl.BlockSpec((None, tm, D_MODEL), lambda j, i: (j, i, 0))],
        out_specs=[pl.BlockSpec((half, n_sec * D_MODEL), lambda j, i: (0, 0)), ANY],
        scratch_shapes=[pltpu.VMEM((D_MODEL, D_MODEL), F32), pltpu.VMEM((n_sec, half, D_MODEL), F32),
                        pltpu.VMEM((n_sec, half, D_MODEL), BF16), pltpu.VMEM((n_sec, half, D_MODEL), BF16),
                        pltpu.SemaphoreType.DMA((n_sec,)), pltpu.SemaphoreType.DMA((n_sec,)), pltpu.SemaphoreType.DMA],
        compiler_params=_params(56),
    )(*_pin(h, d_g))


def _adamw_math(w, g, m, v):
    m = ADAM_B1 * m + (1.0 - ADAM_B1) * g
    v = ADAM_B2 * v + (1.0 - ADAM_B2) * (g * g)
    m_hat = m / (1.0 - ADAM_B1 ** ADAM_STEP)
    v_hat = v / (1.0 - ADAM_B2 ** ADAM_STEP)
    return -ADAM_LR * (m_hat / (jnp.sqrt(v_hat) + ADAM_EPS) + ADAM_WD * w), m, v


def _adamw(w, g, m, v, name):
    rows, cols = w.shape
    tr = rows if rows <= 128 else 128

    def body(w_ref, g_ref, m_ref, v_ref, d_ref, mo_ref, vo_ref):
        d_ref[...], mo_ref[...], vo_ref[...] = _adamw_math(w_ref[...], g_ref[...], m_ref[...], v_ref[...])

    tile = pl.BlockSpec((tr, cols), lambda i: (i, 0))
    return pl.pallas_call(
        body, name=name, grid=(rows // tr,),
        out_shape=[pltpu.HBM((rows, cols), F32)] * 3,
        in_specs=[tile] * 4, out_specs=[tile] * 3,
    )(*_pin(w, g, m, v))


def _small_sums_and_cond_ctx(small, ada_w0, c_ctx, m, v):
    n_cols = ada_w0.shape[1]

    def body(small_ref, w_ref, c_ref, m_ref, v_ref, all_ref, sums_ref, g_ref, d_ref, mo_ref, vo_ref, part_ref, parts_ref,
             s1, r1, l1, s2, r2, l2):
        start_slabs, finish_slabs = _small_gather(small_ref, all_ref, s1, r1, l1)
        start_parts, finish_parts = _small_gather(part_ref, parts_ref, s2, r2, l2)
        start_slabs()
        finish_slabs()
        acc = all_ref[0:SMALL_ROWS, :]
        for dev in range(1, N_DEV):
            acc = acc + all_ref[dev * SMALL_ROWS:(dev + 1) * SMALL_ROWS, :]
        sums_ref[...] = acc
        x, y, _ = _my_place()
        d_modc = jnp.concatenate([acc[6:7, :], acc[7:8, :], acc[8:9, :]], axis=1)
        mine = jnp.zeros((1, n_cols), F32)
        for s in range(N_CHIPS):
            mine = mine + jnp.where(2 * x + y == s, d_modc[:, s * n_cols:(s + 1) * n_cols], 0.0)
        ah, al = _split2(jnp.broadcast_to(mine, (8, n_cols)))
        wh, wl = _split2(w_ref[...])
        nt = lambda a, b: lax.dot_general(a, b, (((1,), (1,)), ((), ())), preferred_element_type=F32)
        part_ref[...] = nt(ah, wh) + nt(al, wh) + nt(ah, wl)
        start_parts()
        finish_parts()
        total = parts_ref[0:1, :]
        for s in range(1, N_CHIPS):
            total = total + parts_ref[16 * s:16 * s + 1, :]
        w = c_ref[...]
        g = total * _dsilu(w)
        g_ref[...] = g
        d_ref[...], mo_ref[...], vo_ref[...] = _adamw_math(w, g, m_ref[...], v_ref[...])

    seven = [pltpu.SemaphoreType.DMA((7,)), pltpu.SemaphoreType.DMA((7,)), pltpu.SemaphoreType.DMA]
    vec = jax.ShapeDtypeStruct((1, D_MODEL), F32)
    return pl.pallas_call(
        body, name="small_sums_and_cond_ctx",
        out_shape=[jax.ShapeDtypeStruct((N_DEV * SMALL_ROWS, D_MODEL), F32), jax.ShapeDtypeStruct((SMALL_ROWS, D_MODEL), F32),
                   vec, vec, vec, vec],
        in_specs=[VMEM] * 5, out_specs=[VMEM] * 6,
        scratch_shapes=[pltpu.VMEM((8, D_MODEL), F32), pltpu.VMEM((N_DEV * 8, D_MODEL), F32)] + seven + seven,
    )(small, ada_w0, c_ctx, m, v)


def _ada_update(cond_t, d_mod, ada_w, m, v):
    n_layers, _, n_cols = ada_w.shape
    tr = ROW_TILE

    def body(c_ref, dm_ref, w_ref, m_ref, v_ref, g_ref, d_ref, mo_ref, vo_ref):
        g = _mm_f32(_silu(c_ref[...]), dm_ref[...])
        g_ref[...] = g
        d_ref[...], mo_ref[...], vo_ref[...] = _adamw_math(w_ref[...], g, m_ref[...], v_ref[...])

    tile = pl.BlockSpec((None, tr, n_cols), lambda l, i: (l, i, 0))
    return pl.pallas_call(
        body, name="ada_update", grid=(n_layers, D_MODEL // tr),
        out_shape=[pltpu.HBM(ada_w.shape, F32)] * 4,
        in_specs=[pl.BlockSpec((tr, 16), lambda l, i: (i, 0)), pl.BlockSpec((None, 16, n_cols), lambda l, i: (l, 0, 0)),
                  tile, tile, tile],
        out_specs=[tile] * 4,
    )(*_pin(cond_t, d_mod, ada_w, m, v))


def _local_step(x2, ctx2, target, mod_mine, mod_ctx, lb_logits, scale_full, w_in_full, late_shards, norm_w, gnorm, final_w):
    row = lambda a: a.reshape(1, -1)
    shift0, scale0, gate0 = (row(a) for a in jnp.split(mod_mine[0], 3))
    shift1, scale1, gate1 = (row(a) for a in jnp.split(mod_mine[1], 3))
    shift_c, scale_c, _ = (row(a) for a in jnp.split(mod_ctx, 3))
    nw0, nw1 = norm_w[0:1], norm_w[1:2]
    scales0 = jnp.concatenate([scale_c, scale0])

    g5, h0 = _ln_mod_matmul(ctx2, x2, nw0, jnp.concatenate([shift_c, shift0]), scales0, w_in_full, "hgrn_in_forward")
    o_raw, full = _gla_forward(g5, lb_logits, late_shards)
    x1, res0 = _hgrn_out_forward(o_raw, g5, x2, gnorm, gate0, full["hgrn_w_out"])
    uz, h1 = _ln_mod_matmul(None, x1, nw1, shift1, scale1, full["pool_w_in"], "pool_in_forward")
    d_x2, loss_part, d_final, d_gate1 = _pool_forward_loss(uz, x1, target, gate1, full["pool_w_grp"], scale_full,
                                                           full["pool_w_out"], final_w)

    d_uz, dw_pool_out, dw_pool_grp, d_pscale = _pool_backward(d_x2, uz, gate1, full["pool_w_grp"], scale_full, full["pool_w_out"])
    d_x1, d_nw1, d_mod1 = _ln_mod_backward(d_uz, full["pool_w_in"], None, x1, nw1, scale1, d_x2, "pool_in_backward")
    dw_pool_in = _weight_grad(h1, d_uz, "pool_in_weight_grad")
    d_o, d_z, dw_hgrn_out, d_gate0, d_gnorm = _hgrn_out_backward(d_x1, o_raw, g5, res0, gnorm, gate0, full["hgrn_w_out"])
    late_grads = {"hgrn_w_out": dw_hgrn_out, "pool_w_in": dw_pool_in, "pool_w_grp": dw_pool_grp, "pool_w_out": dw_pool_out}
    d_g5, d_lb, late_slots = _gla_backward(g5, lb_logits, d_o, d_z, [late_grads[k].astype(BF16) for k in GATHER_LATE])
    dw_hgrn_in, slots0 = _weight_grad_paired(h0, d_g5, "hgrn_in_weight_grad")
    key = GATHER_EARLY[0]
    send_sem, recv_sem, part_thru, slots_thru, token = _scatter_start(dw_hgrn_in, slots0, d_lb, key)
    d_x, d_nw0, d_mod0 = _ln_mod_backward(d_g5, w_in_full, ctx2, x2, nw0 + token[0:1, 0:1], scales0, d_x1, "hgrn_in_backward")
    slots = dict(late_slots)
    pending = (send_sem, recv_sem, part_thru, slots_thru)

    zero = jnp.zeros((1, D_MODEL), F32)
    small = jnp.concatenate([d_mod0[1, 0:2], d_gate0, d_mod1[0, 0:2], d_gate1, d_mod0[0, 0:2], zero, d_nw0, d_nw1, d_gnorm,
                             d_final, d_pscale, d_lb, jnp.broadcast_to(loss_part[0:1, 0:1], (1, D_MODEL)),
                             jnp.zeros((SMALL_ROWS - 17, D_MODEL), F32)], axis=0)
    return {"d_x": d_x, "slots": slots, "pending": pending, "small": small}


def kernel(x, c, ctx, c_ctx, ada_w, ada_b, norm_w, hgrn_w_in, hgrn_lb_logits, hgrn_gnorm_w, hgrn_w_out, pool_w_in, pool_w_grp, pool_scale, pool_w_out, final_norm_w, loss_target, m_c_ctx, m_ada_w, m_ada_b, m_norm_w, m_hgrn_w_in, m_hgrn_lb_logits, m_hgrn_gnorm_w, m_hgrn_w_out, m_pool_w_in, m_pool_w_grp, m_pool_scale, m_pool_w_out, m_final_norm_w, v_c_ctx, v_ada_w, v_ada_b, v_norm_w, v_hgrn_w_in, v_hgrn_lb_logits, v_hgrn_gnorm_w, v_hgrn_w_out, v_pool_w_in, v_pool_w_grp, v_pool_scale, v_pool_w_out, v_final_norm_w):
    xi, yi, ci = _my_place()
    chip = 2 * xi + yi
    dev = 4 * xi + 2 * yi + ci
    ada_cols = ada_w.shape[2]
    lb_cols = hgrn_lb_logits.shape[2]
    ps_cols = pool_scale.shape[1]
    row = lambda a: a.reshape(1, -1)

    def chip_cols(a, n):
        return lax.dynamic_slice_in_dim(a, chip * n, n, axis=a.ndim - 1)

    def from_chips(g, rows_per_dev, take):
        return jnp.concatenate([g[2 * s * rows_per_dev:2 * s * rows_per_dev + take] for s in range(N_CHIPS)], axis=1)

    first = jnp.concatenate([jnp.broadcast_to(c, (8, D_MODEL)), jnp.pad(hgrn_lb_logits[0], ((0, 6), (0, 0))),
                             jnp.pad(pool_scale, ((0, 7), (0, 0)))], axis=1)
    shards = {"hgrn_w_in": hgrn_w_in[0], "hgrn_w_out": hgrn_w_out[0], "pool_w_in": pool_w_in[0],
              "pool_w_grp": pool_w_grp[0], "pool_w_out": pool_w_out[0]}
    first_all, parts_all, w_in_full = _prologue(first, row(c_ctx), ada_w, chip_cols(ada_b, ada_cols),
                                                shards[GATHER_EARLY[0]].astype(BF16))
    cond_all = first_all[::8, :D_MODEL]
    lb_logits = from_chips(first_all[:, D_MODEL:D_MODEL + lb_cols], 8, 2)
    scale_full = from_chips(first_all[:, D_MODEL + lb_cols:], 8, 1)
    cond_rows = jnp.concatenate([cond_all, row(c_ctx), jnp.zeros((7, D_MODEL), F32)], axis=0)
    mod_all = from_chips(parts_all, 32, 32).reshape(2, 16, 3 * D_MODEL)
    mod_mine = lax.dynamic_index_in_dim(mod_all, dev, axis=1, keepdims=False)

    loc = _local_step(x[0], ctx[0], loss_target[0], mod_mine, mod_all[0, 8], lb_logits, scale_full, w_in_full,
                      [shards[k].astype(BF16) for k in GATHER_LATE], norm_w, hgrn_gnorm_w, row(final_norm_w))

    small_all, sums, g_c, d_c, m_c, v_c = _small_sums_and_cond_ctx(loc["small"], ada_w[0], row(c_ctx), row(m_c_ctx), row(v_c_ctx))
    loss = sums[16, 0]

    def reduce_scattered(slots, names, name):
        return dict(zip(names, _sum_and_exchange([slots[k] for k in names], names, name)))

    big_grads = reduce_scattered(loc["slots"], GATHER_LATE, "exchange_halves_late")

    out = {}

    def update(name, w, g, m, v):
        shape = w.shape
        w2, g2, m2, v2 = (a.reshape(-1, shape[-1]) for a in (w, g, m, v))
        d, mn, vn = _adamw(w2, g2, m2, v2, "adamw_" + name)
        out[name] = tuple(a.reshape(shape) for a in (g2, d, mn, vn))

    moments = {"hgrn_w_in": (m_hgrn_w_in, v_hgrn_w_in), "hgrn_w_out": (m_hgrn_w_out, v_hgrn_w_out),
               "pool_w_in": (m_pool_w_in, v_pool_w_in), "pool_w_grp": (m_pool_w_grp, v_pool_w_grp),
               "pool_w_out": (m_pool_w_out, v_pool_w_out)}
    weights = {"hgrn_w_in": hgrn_w_in, "hgrn_w_out": hgrn_w_out, "pool_w_in": pool_w_in, "pool_w_grp": pool_w_grp,
               "pool_w_out": pool_w_out}
    for k in GATHER_LATE:
        update(k, weights[k], big_grads[k], *moments[k])

    g_ada_b = jnp.stack([(sums[0:3] + sums[6:9]).reshape(-1), sums[3:6].reshape(-1)])
    update("ada_b", ada_b, g_ada_b, m_ada_b, v_ada_b)
    update("norm_w", norm_w, sums[9:11], m_norm_w, v_norm_w)
    update("hgrn_gnorm_w", hgrn_gnorm_w, sums[11:12], m_hgrn_gnorm_w, v_hgrn_gnorm_w)
    update("final_norm_w", row(final_norm_w), sums[12:13], row(m_final_norm_w), row(v_final_norm_w))
    update("pool_scale", pool_scale, chip_cols(sums[13:14], ps_cols), m_pool_scale, v_pool_scale)
    update("hgrn_lb_logits", hgrn_lb_logits, chip_cols(sums[14:16], lb_cols)[None], m_hgrn_lb_logits, v_hgrn_lb_logits)

    per_dev = small_all.reshape(N_DEV, SMALL_ROWS, D_MODEL)
    pad7 = jnp.zeros((7, 3 * D_MODEL), F32)
    dm0 = jnp.concatenate([per_dev[:, 0:3].reshape(N_DEV, -1), sums[6:9].reshape(1, -1), pad7], axis=0)
    dm1 = jnp.concatenate([per_dev[:, 3:6].reshape(N_DEV, -1), jnp.zeros((8, 3 * D_MODEL), F32)], axis=0)
    d_mod = chip_cols(jnp.stack([dm0, dm1]), ada_cols)
    out["ada_w"] = _ada_update(cond_rows.T, d_mod, ada_w, m_ada_w, v_ada_w)

    out["c_ctx"] = tuple(a.reshape(-1) for a in (g_c, d_c, m_c, v_c))
    out["final_norm_w"] = tuple(a.reshape(-1) for a in out["final_norm_w"])

    done = [g_c, out["ada_w"][1]] + [out[k][1] for k in GATHER_LATE]
    key = GATHER_EARLY[0]
    _, early = _scatter_wait(*loc["pending"], done, key)
    big_grads = reduce_scattered({key: early}, GATHER_EARLY, "exchange_halves_early")
    for k in GATHER_EARLY:
        update(k, weights[k], big_grads[k], *moments[k])

    names = ["c_ctx", "ada_w", "ada_b", "norm_w", "hgrn_w_in", "hgrn_lb_logits", "hgrn_gnorm_w", "hgrn_w_out", "pool_w_in",
             "pool_w_grp", "pool_scale", "pool_w_out", "final_norm_w"]
    return (loss, loc["d_x"][None], *[out[k][0] for k in names], *[out[k][1] for k in names], *[out[k][2] for k in names],
            *[out[k][3] for k in names])
```

```python
import functools

import numpy as np
import jax
import jax.numpy as jnp
from jax import lax
from jax.experimental import pallas as pl
from jax.experimental.pallas import tpu as pltpu
from jax.experimental.pallas import tpu_sc as plsc

F32 = jnp.float32
BF16 = jnp.bfloat16

D_MODEL = 1024
SEQ = 2048
CTX_LEN = 256
ROWS_ALL = CTX_LEN + SEQ
HEADS = 8
HEAD_DIM = 128
CHUNK = 64
N_CTX_CHUNKS = CTX_LEN // CHUNK
N_LAT_CHUNKS = SEQ // CHUNK
N_CHUNKS = N_CTX_CHUNKS + N_LAT_CHUNKS
GRID_W = 64
POOL_WINDOWS = (2, 4, 8, 16)
POOL_GROUPS = 4
POOL_GROUP_DIM = 256
HGRN_SECTIONS = 5
POOL_SECTIONS = 2
EPS = 1e-6
N_DEV = 8
N_CHIPS = 4
ROW_TILE = 256
POOL_TILE = 512
SMALL_ROWS = 24

ADAM_LR = 0.001
ADAM_B1 = 0.9
ADAM_B2 = 0.999
ADAM_EPS = 1e-08
ADAM_WD = 0.01
ADAM_STEP = 10

MESH = pl.DeviceIdType.MESH
MIB = 1 << 20
ANY = pl.BlockSpec(memory_space=pl.ANY)
VMEM = pl.BlockSpec(memory_space=pltpu.VMEM)


def _params(vmem_mib=None):
    if vmem_mib is None:
        return pltpu.CompilerParams()
    return pltpu.CompilerParams(vmem_limit_bytes=vmem_mib * MIB)


def _pin(*operands):
    return [pltpu.with_memory_space_constraint(a, pltpu.HBM) if a.size * a.dtype.itemsize >= MIB else a for a in operands]


def _sig(a):
    return 0.5 * jnp.tanh(0.5 * a) + 0.5


def _silu(a):
    return a * _sig(a)


def _dsilu(a):
    s = _sig(a)
    return s * (1.0 + a * (1.0 - s))


def _mm(a, b):
    return jnp.dot(a.astype(BF16), b.astype(BF16), preferred_element_type=F32)


def _mm_nt(a, b):
    return lax.dot_general(a.astype(BF16), b.astype(BF16), (((1,), (1,)), ((), ())), preferred_element_type=F32)


def _mm_tn(a, b):
    return lax.dot_general(a.astype(BF16), b.astype(BF16), (((0,), (0,)), ((), ())), preferred_element_type=F32)


def _split2(a):
    hi = a.astype(BF16)
    lo = (a - hi.astype(F32)).astype(BF16)
    return hi, lo


def _mm_exact_lhs(m_bf, a):
    hi, lo = _split2(a)
    return jnp.dot(m_bf, hi, preferred_element_type=F32) + jnp.dot(m_bf, lo, preferred_element_type=F32)


def _mm_f32(a, b):
    ah, al = _split2(a)
    bh, bl = _split2(b)
    return (jnp.dot(ah, bh, preferred_element_type=F32) + jnp.dot(al, bh, preferred_element_type=F32)
            + jnp.dot(ah, bl, preferred_element_type=F32))


def _my_place():
    return lax.axis_index("x"), lax.axis_index("y"), lax.axis_index("c")


def _small_gather(x_ref, out_ref, send_sems, recv_sems, local_sem):
    m_per = x_ref.shape[0]
    x, y, c = _my_place()
    me, sibling = (x, y, c), (x, y, 1 - c)
    chips = [(1 - x, y), (x, 1 - y), (1 - x, 1 - y)]

    def rows(px, py, pc):
        return out_ref.at[pl.ds((4 * px + 2 * py + pc) * m_per, m_per), :]

    def copy(k, block, to, src=None):
        return pltpu.make_async_remote_copy(
            src_ref=rows(*block) if src is None else src, dst_ref=rows(*block),
            send_sem=send_sems.at[k], recv_sem=recv_sems.at[k], device_id=to, device_id_type=MESH)

    def mine():
        return pltpu.make_async_copy(x_ref, rows(*me), local_sem)

    def first():
        return [copy(0, me, sibling, src=x_ref)] + [copy(1 + j, me, (*chip, c), src=x_ref) for j, chip in enumerate(chips)]

    def start():
        mine().start()
        for cp in first():
            cp.start()

    def finish():
        passed = [copy(4 + j, (*chip, c), sibling) for j, chip in enumerate(chips)]
        for j, chip in enumerate(chips):
            copy(1 + j, (*chip, c), me).wait_recv()
            passed[j].start()
        copy(0, sibling, me).wait_recv()
        for j, chip in enumerate(chips):
            copy(4 + j, (*chip, 1 - c), me).wait_recv()
        for cp in first() + passed:
            cp.wait_send()
        mine().wait()

    return start, finish


W_SPECS = {
    "hgrn_w_in": ((D_MODEL, 5 * D_MODEL), (1, 1280, 0, 512)),
    "hgrn_w_out": ((D_MODEL, D_MODEL), (0, 256, 0, 128)),
    "pool_w_in": ((D_MODEL, 2 * D_MODEL), (1, 512, 0, 512)),
    "pool_w_grp": ((POOL_GROUPS, POOL_GROUP_DIM, POOL_GROUP_DIM), (1, 64, 1, 32)),
    "pool_w_out": ((D_MODEL, D_MODEL), (0, 256, 0, 128)),
}

def _al(v, m):
    return pl.multiple_of(v, m)


def _region(ref, spec, chip, half):
    ca, cn, ha, hn = spec
    idx = [slice(None)] * len(ref.shape)
    if ca == ha:
        if half is None:
            idx[ca] = pl.ds(_al(chip * cn, cn), cn)
        else:
            idx[ca] = pl.ds(_al(chip * cn + half * hn, hn), hn)
    else:
        idx[ca] = pl.ds(_al(chip * cn, cn), cn)
        if half is not None:
            idx[ha] = pl.ds(_al(half * hn, hn), hn)
    return ref.at[tuple(idx)]


def _half_of(ref, spec, half):
    _, _, ha, hn = spec
    idx = [slice(None)] * len(ref.shape)
    idx[ha] = pl.ds(_al(half * hn, hn), hn)
    return ref.at[tuple(idx)]


def _half_shape(name):
    full, (ca, cn, ha, hn) = W_SPECS[name]
    shp = list(full)
    shp[ca] = cn
    shp[ha] = hn
    return tuple(shp)


def _gather_two_level(names, sh, land, full, send_sems, recv_sems, local_sems):
    n = len(names)
    specs = [W_SPECS[k][1] for k in names]
    x, y, c = _my_place()
    chip_me = 2 * x + y
    sibling = (x, y, 1 - c)
    chips = [(1 - x, y), (x, 1 - y), (1 - x, 1 - y)]

    def remote(k, src, dst, to):
        return pltpu.make_async_remote_copy(src_ref=src, dst_ref=dst, send_sem=send_sems.at[k], recv_sem=recv_sems.at[k],
                                            device_id=to, device_id_type=MESH)

    def three_halves(a):
        ca, cn, ha, hn = specs[a]
        idx = [slice(None)] * len(land[a].shape)
        if ca == ha:
            idx[ca] = pl.ds(0, 3 * hn)
        else:
            idx[ca], idx[ha] = pl.ds(0, 3 * cn), pl.ds(0, hn)
        return land[a].at[tuple(idx)]

    def own(a):
        return pltpu.make_async_copy(sh[a], _region(land[a], specs[a], chip_me, None), local_sems.at[a])

    def out(a):
        return pltpu.make_async_copy(land[a], full[a], local_sems.at[a])

    def start():
        for a in range(n):
            own(a).start()
            for px, py in chips:
                remote(a, _half_of(sh[a], specs[a], c), _region(land[a], specs[a], chip_me, c), (px, py, c)).start()

    def forward():
        for a in range(n):
            remote(a, three_halves(a), three_halves(a), sibling).wait_recv()
            for px, py in chips:
                landed = _region(land[a], specs[a], 2 * px + py, c)
                remote(n + a, landed, landed, sibling).start()

    def finish():
        for a in range(n):
            remote(n + a, three_halves(a), three_halves(a), sibling).wait_recv()
            remote(a, three_halves(a), three_halves(a), sibling).wait_send()
            remote(n + a, three_halves(a), three_halves(a), sibling).wait_send()
            own(a).wait()
        for a in range(n):
            out(a).start()
        for a in range(n):
            out(a).wait()

    return start, forward, finish


def _scatter_direct(names, part, slots, send_sems, recv_sems, local_sems):
    specs = [W_SPECS[k][1] for k in names]
    x, y, c = _my_place()
    dev_me = 4 * x + 2 * y + c

    def local(a):
        return pltpu.make_async_copy(_region(part[a], specs[a], 2 * x + y, c), slots[a].at[dev_me], local_sems.at[a])

    def start():
        for a in range(len(names)):
            local(a).start()
            for flip in range(1, N_DEV):
                tx = 1 - x if flip >> 2 else x
                ty = 1 - y if (flip >> 1) & 1 else y
                tc = 1 - c if flip & 1 else c
                pltpu.make_async_remote_copy(src_ref=_region(part[a], specs[a], 2 * tx + ty, tc), dst_ref=slots[a].at[dev_me],
                                             send_sem=send_sems.at[a], recv_sem=recv_sems.at[a], device_id=(tx, ty, tc),
                                             device_id_type=MESH).start()

    def wait():
        for a in range(len(names)):
            seven = slots[a].at[pl.ds(0, N_DEV - 1)]
            pltpu.make_async_remote_copy(src_ref=seven, dst_ref=seven, send_sem=send_sems.at[a], recv_sem=recv_sems.at[a],
                                         device_id=(x, y, c), device_id_type=MESH).wait()
            local(a).wait()

    return start, wait


HBM_SPEC = pl.BlockSpec(memory_space=pltpu.HBM)
SEM_SPEC = pl.BlockSpec(memory_space=pltpu.SEMAPHORE)
SPLIT_EFFECT = pltpu.SideEffectType.DATAFLOW_SIDE_EFFECTING


def _scatter_start(part, slots, after, name_key):
    _, cols, _, _ = W_SPECS[name_key][1]

    def body(part_ref, slots_ref, after_ref, send_sem, recv_sem, part_thru, slots_thru, token):
        x, y, c = _my_place()
        for tx, ty in ((1 - x, y), (x, 1 - y), (1 - x, 1 - y)):
            src = part_ref.at[:, pl.ds(_al((2 * tx + ty) * cols, cols), cols)]
            pltpu.make_async_remote_copy(src_ref=src, dst_ref=slots_ref.at[2 * x + y], send_sem=send_sem, recv_sem=recv_sem,
                                         device_id=(tx, ty, c), device_id_type=MESH).start()
        token[...] = jnp.zeros_like(token)

    return pl.pallas_call(
        body, name="scatter_start_" + name_key,
        out_shape=(pltpu.SemaphoreType.DMA(()), pltpu.SemaphoreType.DMA(()), pltpu.HBM(part.shape, part.dtype),
                   pltpu.HBM(slots.shape, slots.dtype), jax.ShapeDtypeStruct((8, 128), F32)),
        in_specs=(HBM_SPEC, HBM_SPEC, ANY), out_specs=(SEM_SPEC, SEM_SPEC, HBM_SPEC, HBM_SPEC, VMEM),
        input_output_aliases={0: 2, 1: 3},
        compiler_params=pltpu.CompilerParams(has_side_effects=SPLIT_EFFECT),
    )(pltpu.with_memory_space_constraint(part, pltpu.HBM), pltpu.with_memory_space_constraint(slots, pltpu.HBM), after)


def _scatter_wait(send_sem, recv_sem, part_thru, slots_thru, after, name_key):
    def body(part_ref, slots_ref, send_sem, recv_sem, *rest):
        x, y, c = _my_place()
        landed = slots_ref.at[pl.ds(0, N_CHIPS - 1)]
        copy = pltpu.make_async_remote_copy(src_ref=landed, dst_ref=landed, send_sem=send_sem, recv_sem=recv_sem,
                                            device_id=(x, y, c), device_id_type=MESH)
        copy.wait_send()
        copy.wait_recv()

    return pl.pallas_call(
        body, name="scatter_wait_" + name_key,
        out_shape=(pltpu.HBM(part_thru.shape, part_thru.dtype), pltpu.HBM(slots_thru.shape, slots_thru.dtype)),
        in_specs=(HBM_SPEC, HBM_SPEC, SEM_SPEC, SEM_SPEC) + (ANY,) * len(after), out_specs=(HBM_SPEC, HBM_SPEC),
        input_output_aliases={0: 0, 1: 1},
        compiler_params=pltpu.CompilerParams(has_side_effects=SPLIT_EFFECT),
    )(part_thru, slots_thru, send_sem, recv_sem, *after)


def _comm_sems(n):
    return [pltpu.SemaphoreType.DMA((n,)), pltpu.SemaphoreType.DMA((n,)), pltpu.SemaphoreType.DMA((n,))]


GATHER_EARLY = ("hgrn_w_in",)
GATHER_LATE = ("hgrn_w_out", "pool_w_in", "pool_w_grp", "pool_w_out")


def _weight_gather(sh, full, spec, send_sems, recv_sems, local_sem):
    x, y, c = _my_place()
    chip_me = 2 * x + y
    sibling = (x, y, 1 - c)
    chips = [(1 - x, y), (x, 1 - y), (1 - x, 1 - y)]

    def remote(k, src, dst, to):
        return pltpu.make_async_remote_copy(src_ref=src, dst_ref=dst, send_sem=send_sems.at[k], recv_sem=recv_sems.at[k],
                                            device_id=to, device_id_type=MESH)

    def own():
        return pltpu.make_async_copy(sh, _region(full, spec, chip_me, None), local_sem)

    def direct():
        return [remote(j, _half_of(sh, spec, c), _region(full, spec, chip_me, c), (px, py, c)) for j, (px, py) in enumerate(chips)]

    def start():
        own().start()
        for cp in direct():
            cp.start()

    def finish():
        passed = []
        for j, (px, py) in enumerate(chips):
            landed = _region(full, spec, 2 * px + py, c)
            remote(j, landed, landed, (px, py, c)).wait_recv()
            passed.append(remote(3 + j, landed, landed, sibling))
            passed[-1].start()
        for j, (px, py) in enumerate(chips):
            other = _region(full, spec, 2 * px + py, 1 - c)
            remote(3 + j, other, other, sibling).wait_recv()
        for cp in direct() + passed:
            cp.wait_send()
        own().wait()

    return start, finish


def _prologue(first, c_ctx, ada_w, ada_b_cols, w_shard):
    n_layers, _, n_cols = ada_w.shape
    key = GATHER_EARLY[0]
    spec = W_SPECS[key][1]
    m_first = first.shape[0]

    def body(first_ref, cctx_ref, adaw_ref, adab_ref, sh_ref, first_all, parts_all, full_ref, parts_scr,
             s1, r1, l1, s2, r2, l2, ws, wr, wl):
        start_first, finish_first = _small_gather(first_ref, first_all, s1, r1, l1)
        start_parts, finish_parts = _small_gather(parts_scr, parts_all, s2, r2, l2)
        start_weight, finish_weight = _weight_gather(sh_ref, full_ref, spec, ws, wr, wl)
        start_first()
        start_weight()
        finish_first()
        cond = jnp.concatenate([first_all[m_first * d:m_first * d + 1, 0:D_MODEL] for d in range(N_DEV)]
                               + [cctx_ref[...], jnp.zeros((16 - N_DEV - 1, D_MODEL), F32)], axis=0)
        act = _silu(cond)
        for i in range(n_layers):
            parts_scr[16 * i:16 * (i + 1), :] = _mm_f32(act, adaw_ref[i]) + adab_ref[i]
        start_parts()
        finish_parts()
        finish_weight()

    seven = [pltpu.SemaphoreType.DMA((7,)), pltpu.SemaphoreType.DMA((7,)), pltpu.SemaphoreType.DMA]
    return pl.pallas_call(
        body, name="prologue",
        out_shape=[jax.ShapeDtypeStruct((N_DEV * m_first, first.shape[1]), F32),
                   jax.ShapeDtypeStruct((N_DEV * 16 * n_layers, n_cols), F32), jax.ShapeDtypeStruct(W_SPECS[key][0], BF16)],
        in_specs=[VMEM] * 5, out_specs=[VMEM] * 3,
        scratch_shapes=[pltpu.VMEM((16 * n_layers, n_cols), F32)] + seven + seven
        + [pltpu.SemaphoreType.DMA((6,)), pltpu.SemaphoreType.DMA((6,)), pltpu.SemaphoreType.DMA],
        compiler_params=_params(48),
    )(first, c_ctx, ada_w, ada_b_cols.reshape(n_layers, 1, n_cols), w_shard)


def _slot_shapes(names):
    return [jax.ShapeDtypeStruct((N_DEV,) + _half_shape(k), BF16) for k in names]


def _sum_and_exchange(slots, names, name):
    n = len(names)
    specs = [W_SPECS[k][1] for k in names]

    def shard_shape(k):
        shp = list(_half_shape(k))
        shp[W_SPECS[k][1][2]] *= 2
        return tuple(shp)

    def body(*refs):
        slot, out, half = refs[:n], refs[n:2 * n], refs[2 * n:3 * n]
        send_sems, recv_sems, local_sems = refs[3 * n:]
        x, y, c = _my_place()
        sibling = (x, y, 1 - c)

        def remote(a, src, dst):
            return pltpu.make_async_remote_copy(src_ref=src, dst_ref=dst, send_sem=send_sems.at[a], recv_sem=recv_sems.at[a],
                                                device_id=sibling, device_id_type=MESH)

        local = [pltpu.make_async_copy(half[a], _half_of(out[a], specs[a], c), local_sems.at[a]) for a in range(n)]
        for a in range(n):
            acc = slot[a][0].astype(F32)
            for d in range(1, slot[a].shape[0]):
                acc = acc + slot[a][d].astype(F32)
            half[a][...] = acc
            local[a].start()
            remote(a, half[a], _half_of(out[a], specs[a], c)).start()
        for a in range(n):
            theirs = _half_of(out[a], specs[a], 1 - c)
            remote(a, theirs, theirs).wait_recv()
            remote(a, half[a], half[a]).wait_send()
            local[a].wait()

    return pl.pallas_call(
        body, name=name,
        out_shape=[jax.ShapeDtypeStruct(shard_shape(k), F32) for k in names],
        in_specs=[VMEM] * n, out_specs=[VMEM] * n,
        scratch_shapes=[pltpu.VMEM(_half_shape(k), F32) for k in names]
        + [pltpu.SemaphoreType.DMA((n,)), pltpu.SemaphoreType.DMA((n,)), pltpu.SemaphoreType.DMA((n,))],
        compiler_params=_params(40),
    )(*slots)


def _ln_mod_matmul(ctx_tile, xin, nw, shift, scale, w, name):
    n_mod = shift.shape[0]
    skip = n_mod - 1
    tm = ROW_TILE if skip else 2 * ROW_TILE
    rows = xin.shape[0] + skip * tm
    n_sec = w.shape[1] // D_MODEL

    def body(*refs):
        c_ref = refs[0] if skip else None
        x_ref, nw_ref, sh_ref, sc_ref, w_ref, g_ref, h_ref = refs[skip:]
        xv = jnp.where(pl.program_id(0) == 0, c_ref[...], x_ref[...]) if skip else x_ref[...]
        r = lax.rsqrt(jnp.mean(xv * xv, axis=-1, keepdims=True) + EPS)
        h_f32 = (xv * r * nw_ref[...]) * (1.0 + sc_ref[...]) + sh_ref[...]
        h = h_f32.astype(BF16)
        h_ref[...] = h
        for k in range(n_sec):
            g_ref[k] = jnp.dot(h, w_ref[:, k * D_MODEL:(k + 1) * D_MODEL], preferred_element_type=F32)

    mod_spec = pl.BlockSpec((None, 1, D_MODEL), lambda i: (jnp.minimum(i, n_mod - 1), 0, 0))
    return pl.pallas_call(
        body, name=name, grid=(rows // tm,),
        out_shape=[pltpu.HBM((n_sec, rows, D_MODEL), F32), pltpu.HBM((rows, D_MODEL), BF16)],
        in_specs=[pl.BlockSpec((tm, D_MODEL), lambda i: (0, 0))] * skip
        + [pl.BlockSpec((tm, D_MODEL), lambda i: (jnp.maximum(i - skip, 0), 0)),
           pl.BlockSpec((1, D_MODEL), lambda i: (0, 0)),
           mod_spec, mod_spec,
           pl.BlockSpec((D_MODEL, n_sec * D_MODEL), lambda i: (0, 0))],
        out_specs=[pl.BlockSpec((n_sec, tm, D_MODEL), lambda i: (0, i, 0)),
                   pl.BlockSpec((tm, D_MODEL), lambda i: (i, 0))],
        compiler_params=_params(48),
    )(*_pin(*([ctx_tile] * skip), xin, nw, shift.reshape(n_mod, 1, D_MODEL), scale.reshape(n_mod, 1, D_MODEL), w))


def _gates(fpre, lb):
    sg = _sig(fpre)
    f = lb + (1.0 - lb) * sg
    return sg, f, 1.0 - f, jnp.log(f)


G_SPEC = lambda sec: pl.BlockSpec((None, ROWS_ALL, HEAD_DIM), lambda h, sec=sec: (sec, 0, h))


GROUP = 2 * CHUNK


def _group_masks():
    rid, cid = np.arange(GROUP)[:, None], np.arange(GROUP)[None, :]
    same = (rid >= CHUNK) == (cid >= CHUNK)
    sums, back = [], []
    for rev in (False, True):
        causal = (cid >= rid) if rev else (cid <= rid)
        anti = (cid <= rid) if rev else (cid >= rid)
        sums.append(same & causal)
        back.append(np.concatenate([same & anti, same & ~anti], axis=1))
    sums, back = np.stack(sums).astype(np.float32), np.stack(back).astype(np.float32)
    return jnp.asarray(sums, BF16), jnp.asarray(back, BF16), jnp.asarray(sums, F32)


MASK_SPECS = [pl.BlockSpec((2, GROUP, GROUP), lambda h: (0, 0, 0)), pl.BlockSpec((2, GROUP, 2 * GROUP), lambda h: (0, 0, 0)),
              pl.BlockSpec((2, GROUP, GROUP), lambda h: (0, 0, 0))]


def _group_sum(m_bf, a):
    hi, lo = _split2(a)
    r = jnp.dot(m_bf, jnp.concatenate([hi, lo], axis=1), preferred_element_type=F32)
    return r[:, :HEAD_DIM] + r[:, HEAD_DIM:]


def _chunk_row(a, pos):
    return jnp.concatenate([jnp.broadcast_to(a[c * CHUNK + pos:c * CHUNK + pos + 1, :], (CHUNK, HEAD_DIM)) for c in range(2)], axis=0)


def _by_chunk(a, second):
    return jnp.concatenate([jnp.where(second, 0.0, a), jnp.where(second, a, 0.0)], axis=1)


def _own_block(r):
    return jnp.concatenate([r[0:CHUNK, 0:HEAD_DIM], r[CHUNK:GROUP, HEAD_DIM:2 * HEAD_DIM]], axis=0)


def _scan_step_of(row_chunk, rev, latent):
    if not rev:
        return row_chunk
    return (N_CHUNKS + N_CTX_CHUNKS - 1 - row_chunk) if latent else (N_CTX_CHUNKS - 1 - row_chunk)


def _group_rows(i, j, per_step, latent):
    base = CTX_LEN if latent else 0
    return pl.multiple_of(base + (i * per_step + j) * GROUP, GROUP)


GROUPS_PER_STEP = 8
GROUPS_PER_GRAD_STEP = 4


def _gla_forward(g5, lb_logits, late_shards):
    q_scale = HEAD_DIM ** -0.5
    per_lat, per_ctx = GROUPS_PER_STEP, min(GROUPS_PER_STEP, CTX_LEN // GROUP)
    n_late = len(GATHER_LATE)

    def body(ff_ref, fb_ref, v_ref, q_ref, lg_ref, sums_ref, _, keep_ref, *rest):
        shard_refs, o_ref, full_refs = rest[:n_late], rest[n_late], rest[n_late + 1:2 * n_late + 1]
        st_ref, decay_ref, qt_ref = rest[2 * n_late + 1:2 * n_late + 4]
        land_refs = rest[2 * n_late + 4:3 * n_late + 4]
        start_gather, forward_gather, finish_gather = _gather_two_level(GATHER_LATE, shard_refs, land_refs, full_refs,
                                                                        *rest[3 * n_late + 4:])

        @pl.when(pl.program_id(0) == 0)
        def _():
            start_gather()

        @pl.when(pl.program_id(0) == HEADS - 1)
        def _():
            forward_gather()

        second = lax.broadcasted_iota(jnp.int32, (GROUP, HEAD_DIM), 0) >= CHUNK
        for rev in (False, True):
            f_ref = fb_ref if rev else ff_ref
            lb = _sig(lg_ref[1:2, :] if rev else lg_ref[0:1, :])
            d = 1 if rev else 0
            last = 0 if rev else CHUNK - 1
            mid = CHUNK // 2 if rev else CHUNK // 2 - 1

            def local_step(i, carry, latent, per):
                r0s = [_group_rows(i, j, per, latent) for j in range(per)]
                rows = [pl.ds(r0, GROUP) for r0 in r0s]
                gates = [_gates(f_ref[r, :], lb) for r in rows]
                vs = [v_ref[r, :] for r in rows]
                bs = [_group_sum(sums_ref[d], g[3]) for g in gates]
                bls = [_chunk_row(b, last) for b in bs]
                ups = [_mm_tn(v, _by_chunk(g[2] * jnp.exp(bl - b), second)) for v, g, b, bl in zip(vs, gates, bs, bls)]
                if latent:
                    qs = [_silu(q_ref[r, :]) * q_scale for r in rows]
                    bms = [_chunk_row(b, mid) for b in bs]
                    a_s = [_mm_nt(q * jnp.exp(b - bm), g[2] * jnp.exp(bm - b)) for q, g, b, bm in zip(qs, gates, bs, bms)]
                    outs = [_mm(a * keep_ref[d], v) for a, v in zip(a_s, vs)]
                for j in range(per):
                    for c in range(2):
                        step = _scan_step_of(r0s[j] // CHUNK + c, rev, latent)
                        decay_ref[step] = jnp.exp(bls[j][c * CHUNK:c * CHUNK + 1, :])
                        st_ref[step] = ups[j][:, c * HEAD_DIM:(c + 1) * HEAD_DIM]
                    if latent:
                        orow = pl.ds(pl.multiple_of(r0s[j] - CTX_LEN, GROUP), GROUP)
                        qt_ref[orow, :] = (qs[j] * jnp.exp(bs[j])).astype(BF16)
                        if rev:
                            o_ref[orow, :] += outs[j]
                        else:
                            o_ref[orow, :] = outs[j]
                return carry

            lax.fori_loop(0, CTX_LEN // (per_ctx * GROUP), functools.partial(local_step, latent=False, per=per_ctx), 0)
            lax.fori_loop(0, SEQ // (per_lat * GROUP), functools.partial(local_step, latent=True, per=per_lat), 0)

            def scan_step(t, st):
                update = st_ref[t]
                st_ref[t] = st
                return st * decay_ref[t] + update

            lax.fori_loop(0, N_CHUNKS, scan_step, jnp.zeros((HEAD_DIM, HEAD_DIM), F32), unroll=2)

            def inter_step(i, carry):
                r0s = [_group_rows(i, j, per_lat, True) for j in range(per_lat)]
                orows = [pl.ds(pl.multiple_of(r0 - CTX_LEN, GROUP), GROUP) for r0 in r0s]
                states = [jnp.concatenate([st_ref[_scan_step_of(r0 // CHUNK + c, rev, True)] for c in range(2)], axis=0)
                          for r0 in r0s]
                prods = [lax.dot_general(qt_ref[orow, :], s.astype(BF16), (((1,), (1,)), ((), ())), preferred_element_type=F32)
                         for orow, s in zip(orows, states)]
                for orow, r in zip(orows, prods):
                    o_ref[orow, :] += _own_block(r)
                return carry

            lax.fori_loop(0, SEQ // (per_lat * GROUP), inter_step, 0)

        @pl.when(pl.program_id(0) == HEADS - 1)
        def _():
            finish_gather()

    outs = pl.pallas_call(
        body, name="gla_forward", grid=(HEADS,),
        out_shape=[pltpu.HBM((SEQ, D_MODEL), F32)] + [jax.ShapeDtypeStruct(W_SPECS[k][0], BF16) for k in GATHER_LATE],
        in_specs=[G_SPEC(0), G_SPEC(1), G_SPEC(2), G_SPEC(3), pl.BlockSpec((2, HEAD_DIM), lambda h: (0, h))] + MASK_SPECS
        + [ANY] * n_late,
        out_specs=[pl.BlockSpec((SEQ, HEAD_DIM), lambda h: (0, h))] + [ANY] * n_late,
        scratch_shapes=[pltpu.VMEM((N_CHUNKS, HEAD_DIM, HEAD_DIM), F32), pltpu.VMEM((N_CHUNKS, 1, HEAD_DIM), F32),
                        pltpu.VMEM((SEQ, HEAD_DIM), BF16)] + [pltpu.VMEM(W_SPECS[k][0], BF16) for k in GATHER_LATE]
        + [pltpu.SemaphoreType.DMA((2 * n_late,)), pltpu.SemaphoreType.DMA((2 * n_late,)), pltpu.SemaphoreType.DMA((n_late,))],
        compiler_params=_params(48),
    )(*_pin(g5, g5, g5, g5, lb_logits), *_group_masks(), *late_shards)
    return outs[0], dict(zip(GATHER_LATE, outs[1:]))


def _gla_backward(g5, lb_logits, d_o, d_z, late_parts):
    q_scale = HEAD_DIM ** -0.5
    per_lat, per_ctx = GROUPS_PER_STEP, min(GROUPS_PER_STEP, CTX_LEN // GROUP)
    n_late = len(GATHER_LATE)

    def body(ff_ref, fb_ref, v_ref, q_ref, lg_ref, do_ref, dz_ref, sums_ref, back_ref, keep_ref, *rest):
        part_refs, (dg_ref, dlg_ref), slot_refs = rest[:n_late], rest[n_late:n_late + 2], rest[n_late + 2:2 * n_late + 2]
        st_ref, dst_ref, decay_ref, both_ref = rest[2 * n_late + 2:2 * n_late + 6]
        start_scatter, wait_scatter = _scatter_direct(GATHER_LATE, part_refs, slot_refs, *rest[2 * n_late + 6:])

        @pl.when(pl.program_id(0) == 0)
        def _():
            start_scatter()

        dg_ref[3, 0:CTX_LEN, :] = jnp.zeros((CTX_LEN, HEAD_DIM), BF16)
        dg_ref[4, 0:CTX_LEN, :] = jnp.zeros((CTX_LEN, HEAD_DIM), BF16)
        dg_ref[4, CTX_LEN:ROWS_ALL, :] = dz_ref[...]
        second = lax.broadcasted_iota(jnp.int32, (GROUP, HEAD_DIM), 0) >= CHUNK
        for rev in (False, True):
            d = 1 if rev else 0
            f_ref = fb_ref if rev else ff_ref
            lb = _sig(lg_ref[d:d + 1, :])
            last = 0 if rev else CHUNK - 1
            mid = CHUNK // 2 if rev else CHUNK // 2 - 1

            def local_step(i, carry, latent, per):
                r0s = [_group_rows(i, j, per, latent) for j in range(per)]
                rows = [pl.ds(r0, GROUP) for r0 in r0s]
                gates = [_gates(f_ref[r, :], lb) for r in rows]
                bs = [_group_sum(sums_ref[d], g[3]) for g in gates]
                bls = [_chunk_row(b, last) for b in bs]
                ups = [_mm_tn(v_ref[r, :], _by_chunk(g[2] * jnp.exp(bl - b), second)) for r, g, b, bl in zip(rows, gates, bs, bls)]
                if latent:
                    orows = [pl.ds(pl.multiple_of(r0 - CTX_LEN, GROUP), GROUP) for r0 in r0s]
                    d_ups = [_mm_tn(do_ref[orow, :], _by_chunk(_silu(q_ref[r, :]) * q_scale * jnp.exp(b), second))
                             for orow, r, b in zip(orows, rows, bs)]
                for j in range(per):
                    for c in range(2):
                        step = _scan_step_of(r0s[j] // CHUNK + c, rev, latent)
                        decay_ref[step] = jnp.exp(bls[j][c * CHUNK:c * CHUNK + 1, :])
                        st_ref[step] = ups[j][:, c * HEAD_DIM:(c + 1) * HEAD_DIM]
                        if latent:
                            dst_ref[step] = d_ups[j][:, c * HEAD_DIM:(c + 1) * HEAD_DIM]
                        else:
                            dst_ref[step] = jnp.zeros((HEAD_DIM, HEAD_DIM), F32)
                return carry

            lax.fori_loop(0, CTX_LEN // (per_ctx * GROUP), functools.partial(local_step, latent=False, per=per_ctx), 0)
            lax.fori_loop(0, SEQ // (per_lat * GROUP), functools.partial(local_step, latent=True, per=per_lat), 0)

            def scan_step(i, carry):
                st, d_st = carry
                j = N_CHUNKS - 1 - i
                update, d_update = st_ref[i], dst_ref[j]
                st_ref[i] = st
                dst_ref[j] = d_st
                return st * decay_ref[i] + update, d_st * decay_ref[j] + d_update

            zero_state = jnp.zeros((HEAD_DIM, HEAD_DIM), F32)
            lax.fori_loop(0, N_CHUNKS, scan_step, (zero_state, zero_state))

            def grad_step(i, dlb, latent, per):
                r0s = [_group_rows(i, j, per, latent) for j in range(per)]
                rows = [pl.ds(r0, GROUP) for r0 in r0s]
                gates = [_gates(f_ref[r, :], lb) for r in rows]
                vs = [v_ref[r, :] for r in rows]
                bs = [_group_sum(sums_ref[d], g[3]) for g in gates]
                bls = [_chunk_row(b, last) for b in bs]
                e_ends = [jnp.exp(bl - b) for b, bl in zip(bs, bls)]
                k_ends = [g[2] * e for g, e in zip(gates, e_ends)]
                sts = [[st_ref[_scan_step_of(r0 // CHUNK + c, rev, latent)] for c in range(2)] for r0 in r0s]
                d_sts = [[dst_ref[_scan_step_of(r0 // CHUNK + c, rev, latent)] for c in range(2)] for r0 in r0s]
                d_kends = [_own_block(_mm(v, jnp.concatenate(ds, axis=1))) for v, ds in zip(vs, d_sts)]
                d_vs = [_own_block(_mm_nt(ke, jnp.concatenate(ds, axis=0))) for ke, ds in zip(k_ends, d_sts)]
                at_last = [jnp.concatenate([jnp.broadcast_to(jnp.sum(ds[c] * s[c], axis=0, keepdims=True), (CHUNK, HEAD_DIM))
                                            for c in range(2)], axis=0) * jnp.exp(bl) for ds, s, bl in zip(d_sts, sts, bls)]
                t_kends = [dk * ke for dk, ke in zip(d_kends, k_ends)]
                d_ks = [dk * e for dk, e in zip(d_kends, e_ends)]
                if latent:
                    orows = [pl.ds(pl.multiple_of(r0 - CTX_LEN, GROUP), GROUP) for r0 in r0s]
                    qpres = [q_ref[r, :] for r in rows]
                    q_sigs = [_sig(qp) for qp in qpres]
                    qs = [qp * sg * q_scale for qp, sg in zip(qpres, q_sigs)]
                    bms = [_chunk_row(b, mid) for b in bs]
                    e_bs = [jnp.exp(b) for b in bs]
                    e_qms = [jnp.exp(b - bm) for b, bm in zip(bs, bms)]
                    e_kms = [jnp.exp(bm - b) for b, bm in zip(bs, bms)]
                    q_ts = [q * e for q, e in zip(qs, e_bs)]
                    q_ms = [q * e for q, e in zip(qs, e_qms)]
                    k_ms = [g[2] * e for g, e in zip(gates, e_kms)]
                    d_outs = [do_ref[orow, :] for orow in orows]
                    a_s = [_mm_nt(qm, km) * keep_ref[d] for qm, km in zip(q_ms, k_ms)]
                    d_as = [_mm_nt(do, v) * keep_ref[d] for do, v in zip(d_outs, vs)]
                    d_qts = [_own_block(_mm(do, jnp.concatenate(s, axis=1))) for do, s in zip(d_outs, sts)]
                    d_qms = [_mm(da, km) for da, km in zip(d_as, k_ms)]
                    d_kms = [_mm_tn(da, qm) for da, qm in zip(d_as, q_ms)]
                    d_vs = [dv + _mm_tn(a, do) for dv, a, do in zip(d_vs, a_s, d_outs)]
                    d_ks = [dk + dkm * e for dk, dkm, e in zip(d_ks, d_kms, e_kms)]
                    d_bs = [jnp.concatenate([dqt * qt + dqm * qm - dkm * km, t], axis=0)
                            for dqt, qt, dqm, qm, dkm, km, t in zip(d_qts, q_ts, d_qms, q_ms, d_kms, k_ms, t_kends)]
                    d_qs = [dqt * eb + dqm * eq for dqt, eb, dqm, eq in zip(d_qts, e_bs, d_qms, e_qms)]
                    back = back_ref[d]
                else:
                    d_bs, back = t_kends, back_ref[d, :, GROUP:2 * GROUP]
                d_lfs = [_group_sum(back, db) + al for db, al in zip(d_bs, at_last)]
                for j in range(per):
                    sg, f = gates[j][0], gates[j][1]
                    d_f = d_lfs[j] / f - d_ks[j]
                    dg_ref[d, rows[j], :] = (d_f * (1.0 - lb) * sg * (1.0 - sg)).astype(BF16)
                    dlb = dlb + jnp.sum(d_f * (1.0 - sg), axis=0, keepdims=True)
                    if rev:
                        dg_ref[2, rows[j], :] = (both_ref[0, rows[j], :] + d_vs[j]).astype(BF16)
                    else:
                        both_ref[0, rows[j], :] = d_vs[j]
                    if latent:
                        d_qpre = d_qs[j] * q_scale * (q_sigs[j] * (1.0 + qpres[j] * (1.0 - q_sigs[j])))
                        if rev:
                            dg_ref[3, rows[j], :] = (both_ref[1, rows[j], :] + d_qpre).astype(BF16)
                        else:
                            both_ref[1, rows[j], :] = d_qpre
                return dlb

            dlb = lax.fori_loop(0, SEQ // (GROUPS_PER_GRAD_STEP * GROUP),
                                functools.partial(grad_step, latent=True, per=GROUPS_PER_GRAD_STEP), jnp.zeros((1, HEAD_DIM), F32))
            dlb = lax.fori_loop(0, CTX_LEN // (per_ctx * GROUP), functools.partial(grad_step, latent=False, per=per_ctx), dlb)
            dlg_ref[d:d + 1, :] = dlb * lb * (1.0 - lb)

        @pl.when(pl.program_id(0) == HEADS - 1)
        def _():
            wait_scatter()

    col = pl.BlockSpec((SEQ, HEAD_DIM), lambda h: (0, h))
    outs = pl.pallas_call(
        body, name="gla_backward", grid=(HEADS,),
        out_shape=[pltpu.HBM((HGRN_SECTIONS, ROWS_ALL, D_MODEL), BF16), jax.ShapeDtypeStruct((2, D_MODEL), F32)]
        + _slot_shapes(GATHER_LATE),
        in_specs=[G_SPEC(0), G_SPEC(1), G_SPEC(2), G_SPEC(3), pl.BlockSpec((2, HEAD_DIM), lambda h: (0, h)), col, col]
        + MASK_SPECS + [ANY] * n_late,
        out_specs=[pl.BlockSpec((HGRN_SECTIONS, ROWS_ALL, HEAD_DIM), lambda h: (0, 0, h)),
                   pl.BlockSpec((2, HEAD_DIM), lambda h: (0, h))] + [ANY] * n_late,
        scratch_shapes=[pltpu.VMEM((N_CHUNKS, HEAD_DIM, HEAD_DIM), F32), pltpu.VMEM((N_CHUNKS, HEAD_DIM, HEAD_DIM), F32),
                        pltpu.VMEM((N_CHUNKS, 1, HEAD_DIM), F32), pltpu.VMEM((2, ROWS_ALL, HEAD_DIM), F32)]
        + _comm_sems(n_late),
        compiler_params=_params(48),
    )(*_pin(g5, g5, g5, g5, lb_logits, d_o, d_z), *_group_masks(), *late_parts)
    return outs[0], outs[1], dict(zip(GATHER_LATE, outs[2:]))


def _head_norm(o, scr):
    rs = []
    for h in range(HEADS):
        cols = slice(h * HEAD_DIM, (h + 1) * HEAD_DIM)
        oh = o[:, cols]
        r = lax.rsqrt(jnp.mean(oh * oh, axis=-1, keepdims=True) + EPS)
        scr[:, cols] = oh * r
        rs.append(r)
    return rs


def _hgrn_out_forward(o_raw, g5, xin, gnorm_w, gate, w_out):
    tm = ROW_TILE

    def body(o_ref, z_ref, x_ref, gw_ref, gate_ref, w_ref, x1_ref, res_ref, scr):
        _head_norm(o_ref[...], scr)
        a = scr[...] * gw_ref[...] * _silu(z_ref[...])
        res = _mm(a, w_ref[...])
        res_ref[...] = res
        x1_ref[...] = x_ref[...] + gate_ref[...] * res

    tile = pl.BlockSpec((tm, D_MODEL), lambda i: (i, 0))
    vec = pl.BlockSpec((1, D_MODEL), lambda i: (0, 0))
    return pl.pallas_call(
        body, name="hgrn_out_forward", grid=(SEQ // tm,),
        out_shape=[pltpu.HBM((SEQ, D_MODEL), F32)] * 2,
        in_specs=[tile, pl.BlockSpec((None, tm, D_MODEL), lambda i: (4, i + CTX_LEN // tm, 0)), tile, vec, vec,
                  pl.BlockSpec((D_MODEL, D_MODEL), lambda i: (0, 0))],
        out_specs=[tile, tile],
        scratch_shapes=[pltpu.VMEM((tm, D_MODEL), F32)],
        compiler_params=_params(32),
    )(*_pin(o_raw, g5, xin, gnorm_w, gate, w_out))


def _hgrn_out_backward(d_x1, o_raw, g5, res, gnorm_w, gate, w_out):
    tm = ROW_TILE

    def body(dx_ref, o_ref, z_ref, res_ref, gw_ref, gate_ref, w_ref, do_ref, dz_ref, dw_out, dgate_ref, dgw_ref, scr, scr2,
             dw_ref):
        @pl.when(pl.program_id(0) == 0)
        def _():
            dw_ref[...] = jnp.zeros_like(dw_ref)
            dgate_ref[...] = jnp.zeros_like(dgate_ref)
            dgw_ref[...] = jnp.zeros_like(dgw_ref)

        dx = dx_ref[...]
        dgate_ref[...] += jnp.sum(dx * res_ref[...], axis=0, keepdims=True)
        d_res = (dx * gate_ref[...]).astype(BF16)
        d_a = _mm_nt(d_res, w_ref[...])
        rs = _head_norm(o_ref[...], scr)
        z = z_ref[...]
        sz = _silu(z)
        o_hat = scr[...]
        o_n = o_hat * gw_ref[...]
        dw_ref[...] += _mm_tn(o_n * sz, d_res)
        d_on = d_a * sz
        dz_ref[...] = (d_a * o_n * _dsilu(z)).astype(BF16)
        dgw_ref[...] += jnp.sum(d_on * o_hat, axis=0, keepdims=True)
        scr2[...] = d_on * gw_ref[...]
        for h in range(HEADS):
            cols = slice(h * HEAD_DIM, (h + 1) * HEAD_DIM)
            dh, oh = scr2[:, cols], scr[:, cols]
            do_ref[:, cols] = (rs[h] * (dh - oh * jnp.mean(dh * oh, axis=-1, keepdims=True))).astype(BF16)

        @pl.when(pl.program_id(0) == SEQ // tm - 1)
        def _():
            dw_out[...] = dw_ref[...].astype(BF16)

    tile = pl.BlockSpec((tm, D_MODEL), lambda i: (i, 0))
    vec = pl.BlockSpec((1, D_MODEL), lambda i: (0, 0))
    mat = pl.BlockSpec((D_MODEL, D_MODEL), lambda i: (0, 0))
    return pl.pallas_call(
        body, name="hgrn_out_backward", grid=(SEQ // tm,),
        out_shape=[pltpu.HBM((SEQ, D_MODEL), BF16)] * 2 + [pltpu.HBM((D_MODEL, D_MODEL), BF16)]
        + [jax.ShapeDtypeStruct((1, D_MODEL), F32)] * 2,
        in_specs=[tile, tile, pl.BlockSpec((None, tm, D_MODEL), lambda i: (4, i + CTX_LEN // tm, 0)), tile, vec, vec, mat],
        out_specs=[tile, tile, mat, vec, vec],
        scratch_shapes=[pltpu.VMEM((tm, D_MODEL), F32)] * 2 + [pltpu.VMEM((D_MODEL, D_MODEL), F32)],
        compiler_params=_params(40),
    )(*_pin(d_x1, o_raw, g5, res, gnorm_w, gate, w_out))


def _pool_constants():
    win = np.zeros((POOL_GROUPS, ROW_TILE, ROW_TILE), np.float32)
    inv = np.zeros((POOL_GROUPS, ROW_TILE, 1), np.float32)
    for g, w in enumerate(POOL_WINDOWS):
        for t in range(ROW_TILE):
            base, p = (t // GRID_W) * GRID_W, t % GRID_W
            lo = min(max(p - w // 2, 0), GRID_W)
            hi = min(max(p - w // 2 + w, 0), GRID_W)
            win[g, t, base + lo:base + hi] = 1.0
            inv[g, t, 0] = 1.0 / np.float32(hi - lo)
    return jnp.asarray(win, BF16), jnp.asarray(win.transpose(0, 2, 1), BF16), jnp.asarray(inv, F32)


def _pool_mix(u_ref, wg_ref, win_ref, inv_ref, pooled_scr, yg_scr):
    for g in range(POOL_GROUPS):
        cols = slice(g * POOL_GROUP_DIM, (g + 1) * POOL_GROUP_DIM)
        for r in range(u_ref.shape[0] // ROW_TILE):
            rows = slice(r * ROW_TILE, (r + 1) * ROW_TILE)
            ug = u_ref[rows, cols]
            pooled_scr[rows, cols] = _mm_exact_lhs(win_ref[g], ug) * inv_ref[g] - ug
        yg_scr[:, cols] = _mm(pooled_scr[:, cols], wg_ref[g])


def _pool_forward_loss(uz, x1, target, gate, w_grp, pool_scale, w_out, final_w):
    tm = POOL_TILE
    win, _, inv = _pool_constants()

    def body(u_ref, z_ref, x_ref, t_ref, gate_ref, wg_ref, ps_ref, w_ref, fw_ref, win_ref, inv_ref,
             dx_ref, loss_ref, dfw_ref, dgate_ref, yg_scr, pooled_scr):
        @pl.when(pl.program_id(0) == 0)
        def _():
            loss_ref[...] = jnp.zeros_like(loss_ref)
            dfw_ref[...] = jnp.zeros_like(dfw_ref)
            dgate_ref[...] = jnp.zeros_like(dgate_ref)

        _pool_mix(u_ref, wg_ref, win_ref, inv_ref, pooled_scr, yg_scr)
        a = yg_scr[...] * ps_ref[...] * _silu(z_ref[...])
        res = _mm(a, w_ref[...])
        x2 = x_ref[...] + gate_ref[...] * res
        r = lax.rsqrt(jnp.mean(x2 * x2, axis=-1, keepdims=True) + EPS)
        xh = x2 * r
        fw = fw_ref[...]
        err = xh * fw - t_ref[...]
        loss_ref[...] += 0.5 * jnp.sum(jnp.mean(err * err, axis=-1, keepdims=True))
        d_y = err * (1.0 / D_MODEL)
        dfw_ref[...] += jnp.sum(d_y * xh, axis=0, keepdims=True)
        d_xh = d_y * fw
        d_x2 = r * (d_xh - xh * jnp.mean(d_xh * xh, axis=-1, keepdims=True))
        dx_ref[...] = d_x2
        dgate_ref[...] += jnp.sum(d_x2 * res, axis=0, keepdims=True)

    tile = pl.BlockSpec((tm, D_MODEL), lambda i: (i, 0))
    vec = pl.BlockSpec((1, D_MODEL), lambda i: (0, 0))
    grp = pl.BlockSpec((POOL_GROUPS, POOL_GROUP_DIM, POOL_GROUP_DIM), lambda i: (0, 0, 0))
    return pl.pallas_call(
        body, name="pool_forward_loss", grid=(SEQ // tm,),
        out_shape=[pltpu.HBM((SEQ, D_MODEL), F32), jax.ShapeDtypeStruct((8, 128), F32),
                   jax.ShapeDtypeStruct((1, D_MODEL), F32), jax.ShapeDtypeStruct((1, D_MODEL), F32)],
        in_specs=[pl.BlockSpec((None, tm, D_MODEL), lambda i: (0, i, 0)), pl.BlockSpec((None, tm, D_MODEL), lambda i: (1, i, 0)),
                  tile, tile, vec, grp, vec, pl.BlockSpec((D_MODEL, D_MODEL), lambda i: (0, 0)), vec, grp,
                  pl.BlockSpec((POOL_GROUPS, ROW_TILE, 1), lambda i: (0, 0, 0))],
        out_specs=[tile, pl.BlockSpec((8, 128), lambda i: (0, 0)), vec, vec],
        scratch_shapes=[pltpu.VMEM((tm, D_MODEL), F32)] * 2,
        compiler_params=_params(40),
    )(*_pin(uz, uz, x1, target, gate, w_grp, pool_scale, w_out, final_w, win, inv))


def _pool_backward(d_x2, uz, gate, w_grp, pool_scale, w_out):
    tm = POOL_TILE
    win, win_t, inv = _pool_constants()

    def body(dx_ref, u_ref, z_ref, gate_ref, wg_ref, ps_ref, w_ref, win_ref, wint_ref, inv_ref,
             duz_ref, dw_out, dwg_out, dps_ref, pooled_scr, yg_scr, dyg_scr, dw_ref, dwg_ref):
        @pl.when(pl.program_id(0) == 0)
        def _():
            dw_ref[...] = jnp.zeros_like(dw_ref)
            dwg_ref[...] = jnp.zeros_like(dwg_ref)
            dps_ref[...] = jnp.zeros_like(dps_ref)

        _pool_mix(u_ref, wg_ref, win_ref, inv_ref, pooled_scr, yg_scr)
        z = z_ref[...]
        sz = _silu(z)
        yg = yg_scr[...]
        y = yg * ps_ref[...]
        d_res = (dx_ref[...] * gate_ref[...]).astype(BF16)
        d_a = _mm_nt(d_res, w_ref[...])
        dw_ref[...] += _mm_tn(y * sz, d_res)
        d_y = d_a * sz
        duz_ref[1] = (d_a * y * _dsilu(z)).astype(BF16)
        dps_ref[...] += jnp.sum(d_y * yg, axis=0, keepdims=True)
        dyg_scr[...] = d_y * ps_ref[...]
        for g in range(POOL_GROUPS):
            cols = slice(g * POOL_GROUP_DIM, (g + 1) * POOL_GROUP_DIM)
            d_yg = dyg_scr[:, cols].astype(BF16)
            d_pool = _mm_nt(d_yg, wg_ref[g])
            dwg_ref[g] += _mm_tn(pooled_scr[:, cols], d_yg)
            for r in range(tm // ROW_TILE):
                rows = slice(r * ROW_TILE, (r + 1) * ROW_TILE)
                dp = d_pool[rows, :]
                duz_ref[0, rows, cols] = (_mm_exact_lhs(wint_ref[g], dp * inv_ref[g]) - dp).astype(BF16)

        @pl.when(pl.program_id(0) == SEQ // tm - 1)
        def _():
            dw_out[...] = dw_ref[...].astype(BF16)
            dwg_out[...] = dwg_ref[...].astype(BF16)

    tile = pl.BlockSpec((tm, D_MODEL), lambda i: (i, 0))
    vec = pl.BlockSpec((1, D_MODEL), lambda i: (0, 0))
    mat = pl.BlockSpec((D_MODEL, D_MODEL), lambda i: (0, 0))
    grp = pl.BlockSpec((POOL_GROUPS, POOL_GROUP_DIM, POOL_GROUP_DIM), lambda i: (0, 0, 0))
    return pl.pallas_call(
        body, name="pool_backward", grid=(SEQ // tm,),
        out_shape=[pltpu.HBM((POOL_SECTIONS, SEQ, D_MODEL), BF16), pltpu.HBM((D_MODEL, D_MODEL), BF16),
                   pltpu.HBM((POOL_GROUPS, POOL_GROUP_DIM, POOL_GROUP_DIM), BF16), jax.ShapeDtypeStruct((1, D_MODEL), F32)],
        in_specs=[tile, pl.BlockSpec((None, tm, D_MODEL), lambda i: (0, i, 0)), pl.BlockSpec((None, tm, D_MODEL), lambda i: (1, i, 0)),
                  vec, grp, vec, mat, grp, grp, pl.BlockSpec((POOL_GROUPS, ROW_TILE, 1), lambda i: (0, 0, 0))],
        out_specs=[pl.BlockSpec((POOL_SECTIONS, tm, D_MODEL), lambda i: (0, i, 0)), mat, grp, vec],
        scratch_shapes=[pltpu.VMEM((tm, D_MODEL), F32)] * 3 + [pltpu.VMEM((D_MODEL, D_MODEL), F32),
                                                               pltpu.VMEM((POOL_GROUPS, POOL_GROUP_DIM, POOL_GROUP_DIM), F32)],
        compiler_params=_params(48),
    )(*_pin(d_x2, uz, uz, gate, w_grp, pool_scale, w_out, win, win_t, inv))


def _ln_mod_backward(d_g, w, ctx_tile, xin, nw, scale, d_up, name):
    n_sec, rows, _ = d_g.shape
    n_mod = scale.shape[0]
    skip = n_mod - 1
    tm = ROW_TILE if skip else 2 * ROW_TILE
    n_tiles = rows // tm

    def body(dg_ref, w_ref, *refs):
        c_ref = refs[0] if skip else None
        x_ref, nw_ref, sc_ref, up_ref, dx_ref, dnw_ref, dmod_ref = refs[skip:]
        i = pl.program_id(0)

        @pl.when(i == 0)
        def _():
            dnw_ref[...] = jnp.zeros_like(dnw_ref)

        @pl.when((i == 0) | (i == skip))
        def _():
            dmod_ref[...] = jnp.zeros_like(dmod_ref)

        d_h = _mm_nt(dg_ref[0], w_ref[:, 0:D_MODEL])
        for k in range(1, n_sec):
            d_h = d_h + _mm_nt(dg_ref[k], w_ref[:, k * D_MODEL:(k + 1) * D_MODEL])
        xv = jnp.where(i == 0, c_ref[...], x_ref[...]) if skip else x_ref[...]
        r = lax.rsqrt(jnp.mean(xv * xv, axis=-1, keepdims=True) + EPS)
        xh = xv * r
        nw_row = nw_ref[...]
        dmod_ref[0:1, :] += jnp.sum(d_h, axis=0, keepdims=True)
        dmod_ref[1:2, :] += jnp.sum(d_h * (xh * nw_row), axis=0, keepdims=True)
        d_xn = d_h * (1.0 + sc_ref[...])
        dnw_ref[...] += jnp.sum(d_xn * xh, axis=0, keepdims=True)
        d_xh = d_xn * nw_row

        @pl.when(i >= skip)
        def _():
            dx_ref[...] = up_ref[...] + r * (d_xh - xh * jnp.mean(d_xh * xh, axis=-1, keepdims=True))

    lat = lambda i: (jnp.maximum(i - skip, 0), 0)
    mod_idx = lambda i: (jnp.minimum(i, n_mod - 1), 0, 0)
    return pl.pallas_call(
        body, name=name, grid=(n_tiles,),
        out_shape=[pltpu.HBM((rows - skip * tm, D_MODEL), F32), jax.ShapeDtypeStruct((1, D_MODEL), F32),
                   jax.ShapeDtypeStruct((n_mod, 8, D_MODEL), F32)],
        in_specs=[pl.BlockSpec((n_sec, tm, D_MODEL), lambda i: (0, i, 0)),
                  pl.BlockSpec((D_MODEL, n_sec * D_MODEL), lambda i: (0, 0))]
        + [pl.BlockSpec((tm, D_MODEL), lambda i: (0, 0))] * skip
        + [pl.BlockSpec((tm, D_MODEL), lat),
           pl.BlockSpec((1, D_MODEL), lambda i: (0, 0)),
           pl.BlockSpec((None, 1, D_MODEL), mod_idx),
           pl.BlockSpec((tm, D_MODEL), lat)],
        out_specs=[pl.BlockSpec((tm, D_MODEL), lat), pl.BlockSpec((1, D_MODEL), lambda i: (0, 0)),
                   pl.BlockSpec((None, 8, D_MODEL), mod_idx)],
        compiler_params=_params(48),
    )(*_pin(d_g, w, *([ctx_tile] * skip), xin, nw, scale.reshape(n_mod, 1, D_MODEL), d_up))


def _weight_grad(h, d_g, name):
    n_sec, rows, _ = d_g.shape
    tm = 768 if rows % 768 == 0 else 512
    n_tiles = rows // tm

    def body(h_ref, dg_ref, dw_ref, acc):
        i = pl.program_id(1)
        prod = _mm_tn(h_ref[...], dg_ref[...])

        @pl.when(i == 0)
        def _():
            acc[...] = prod

        @pl.when((i > 0) & (i < n_tiles - 1))
        def _():
            acc[...] += prod

        @pl.when(i == n_tiles - 1)
        def _():
            dw_ref[...] = (acc[...] + prod).astype(BF16)

    return pl.pallas_call(
        body, name=name, grid=(n_sec, n_tiles),
        out_shape=pltpu.HBM((D_MODEL, n_sec * D_MODEL), BF16),
        in_specs=[pl.BlockSpec((tm, D_MODEL), lambda j, i: (i, 0)), pl.BlockSpec((None, tm, D_MODEL), lambda j, i: (j, i, 0))],
        out_specs=pl.BlockSpec((D_MODEL, D_MODEL), lambda j, i: (0, j)),
        scratch_shapes=[pltpu.VMEM((D_MODEL, D_MODEL), F32)],
        compiler_params=_params(32),
    )(*_pin(h, d_g))


def _weight_grad_paired(h, d_g, name):
    n_sec, rows, _ = d_g.shape
    tm = 768
    n_tiles = rows // tm
    half = D_MODEL // 2
    chip_cols = n_sec * D_MODEL // N_CHIPS

    def body(h_ref, dg_ref, q_ref, slots_ref, acc, keep, send, land, send_sems, recv_sems, own_sem):
        j, i = pl.program_id(0), pl.program_id(1)
        x, y, c = _my_place()
        prod = _mm_tn(h_ref[...], dg_ref[...])

        def to_sibling(k):
            return pltpu.make_async_remote_copy(src_ref=send.at[k], dst_ref=land.at[k], send_sem=send_sems.at[k],
                                                recv_sem=recv_sems.at[k], device_id=(x, y, 1 - c), device_id_type=MESH)

        @pl.when(i == 0)
        def _():
            acc[...] = prod

        @pl.when((i > 0) & (i < n_tiles - 1))
        def _():
            acc[...] += prod

        @pl.when(i == n_tiles - 1)
        def _():
            acc[...] += prod
            keep[j] = acc[pl.ds(_al(c * half, half), half), :]
            send[j] = acc[pl.ds(_al((1 - c) * half, half), half), :].astype(BF16)
            to_sibling(j).start()

        @pl.when((j == n_sec - 1) & (i == n_tiles - 1))
        def _():
            for k in range(n_sec):
                to_sibling(k).wait()
                q_ref[:, k * D_MODEL:(k + 1) * D_MODEL] = (keep[k] + land[k].astype(F32)).astype(BF16)
            chip = 2 * x + y
            own = pltpu.make_async_copy(q_ref.at[:, pl.ds(_al(chip * chip_cols, chip_cols), chip_cols)], slots_ref.at[chip], own_sem)
            own.start()
            own.wait()

    return pl.pallas_call(
        body, name=name, grid=(n_sec, n_tiles),
        out_shape=[jax.ShapeDtypeStruct((half, n_sec * D_MODEL), BF16),
                   jax.ShapeDtypeStruct((N_CHIPS, half, chip_cols), BF16)],
        in_specs=[pl.BlockSpec((tm, D_MODEL), lambda j, i: (i, 0)), pl.BlockSpec((None, tm, D_MODEL), lambda j, i: (j, i, 0))],
        out_specs=[pl.BlockSpec((half, n_sec * D_MODEL), lambda j, i: (0, 0)), ANY],
        scratch_shapes=[pltpu.VMEM((D_MODEL, D_MODEL), F32), pltpu.VMEM((n_sec, half, D_MODEL), F32),
                        pltpu.VMEM((n_sec, half, D_MODEL), BF16), pltpu.VMEM((n_sec, half, D_MODEL), BF16),
                        pltpu.SemaphoreType.DMA((n_sec,)), pltpu.SemaphoreType.DMA((n_sec,)), pltpu.SemaphoreType.DMA],
        compiler_params=_params(56),
    )(*_pin(h, d_g))


def _adamw_math(w, g, m, v):
    m = ADAM_B1 * m + (1.0 - ADAM_B1) * g
    v = ADAM_B2 * v + (1.0 - ADAM_B2) * (g * g)
    m_hat = m / (1.0 - ADAM_B1 ** ADAM_STEP)
    v_hat = v / (1.0 - ADAM_B2 ** ADAM_STEP)
    return -ADAM_LR * (m_hat / (jnp.sqrt(v_hat) + ADAM_EPS) + ADAM_WD * w), m, v


def _adamw(w, g, m, v, name):
    rows, cols = w.shape
    tr = rows if rows <= 128 else 128

    def body(w_ref, g_ref, m_ref, v_ref, d_ref, mo_ref, vo_ref):
        d_ref[...], mo_ref[...], vo_ref[...] = _adamw_math(w_ref[...], g_ref[...], m_ref[...], v_ref[...])

    tile = pl.BlockSpec((tr, cols), lambda i: (i, 0))
    return pl.pallas_call(
        body, name=name, grid=(rows // tr,),
        out_shape=[pltpu.HBM((rows, cols), F32)] * 3,
        in_specs=[tile] * 4, out_specs=[tile] * 3,
    )(*_pin(w, g, m, v))


SC_TILES = 32
SC_LANES = 16


def _adamw_sparsecore(w, g, m, v, name):
    rows, cols = w.shape
    per = rows // SC_TILES

    def body(w_hbm, g_hbm, m_hbm, v_hbm, d_hbm, mo_hbm, vo_hbm, wb, gb, mb, vb):
        tile = lax.axis_index("sc_subcore") * 2 + lax.axis_index("sc_core")
        mine = pl.ds(tile * per, per)
        pltpu.sync_copy(w_hbm.at[mine, :], wb)
        pltpu.sync_copy(g_hbm.at[mine, :], gb)
        pltpu.sync_copy(m_hbm.at[mine, :], mb)
        pltpu.sync_copy(v_hbm.at[mine, :], vb)

        @pl.loop(0, per)
        def _(r):
            @pl.loop(0, cols, step=SC_LANES)
            def _(i):
                at = (r, pl.ds(i, SC_LANES))
                d, m_new, v_new = _adamw_math(wb[at], gb[at], mb[at], vb[at])
                wb[at] = d
                mb[at] = m_new
                vb[at] = v_new

        pltpu.sync_copy(wb, d_hbm.at[mine, :])
        pltpu.sync_copy(mb, mo_hbm.at[mine, :])
        pltpu.sync_copy(vb, vo_hbm.at[mine, :])

    return pl.kernel(
        body, name=name, out_type=[jax.ShapeDtypeStruct((rows, cols), F32)] * 3,
        mesh=plsc.VectorSubcoreMesh(core_axis_name="sc_core", subcore_axis_name="sc_subcore"),
        scratch_types=[pltpu.VMEM((per, cols), F32)] * 4,
    )(w, g, m, v)


def _small_sums_and_cond_ctx(small, ada_w0, c_ctx, m, v):
    n_cols = ada_w0.shape[1]

    def body(small_ref, w_ref, c_ref, m_ref, v_ref, all_ref, sums_ref, g_ref, d_ref, mo_ref, vo_ref, part_ref, parts_ref,
             s1, r1, l1, s2, r2, l2):
        start_slabs, finish_slabs = _small_gather(small_ref, all_ref, s1, r1, l1)
        start_parts, finish_parts = _small_gather(part_ref, parts_ref, s2, r2, l2)
        start_slabs()
        finish_slabs()
        acc = all_ref[0:SMALL_ROWS, :]
        for dev in range(1, N_DEV):
            acc = acc + all_ref[dev * SMALL_ROWS:(dev + 1) * SMALL_ROWS, :]
        sums_ref[...] = acc
        x, y, _ = _my_place()
        d_modc = jnp.concatenate([acc[6:7, :], acc[7:8, :], acc[8:9, :]], axis=1)
        mine = jnp.zeros((1, n_cols), F32)
        for s in range(N_CHIPS):
            mine = mine + jnp.where(2 * x + y == s, d_modc[:, s * n_cols:(s + 1) * n_cols], 0.0)
        ah, al = _split2(jnp.broadcast_to(mine, (8, n_cols)))
        wh, wl = _split2(w_ref[...])
        nt = lambda a, b: lax.dot_general(a, b, (((1,), (1,)), ((), ())), preferred_element_type=F32)
        part_ref[...] = nt(ah, wh) + nt(al, wh) + nt(ah, wl)
        start_parts()
        finish_parts()
        total = parts_ref[0:1, :]
        for s in range(1, N_CHIPS):
            total = total + parts_ref[16 * s:16 * s + 1, :]
        w = c_ref[...]
        g = total * _dsilu(w)
        g_ref[...] = g
        d_ref[...], mo_ref[...], vo_ref[...] = _adamw_math(w, g, m_ref[...], v_ref[...])

    seven = [pltpu.SemaphoreType.DMA((7,)), pltpu.SemaphoreType.DMA((7,)), pltpu.SemaphoreType.DMA]
    vec = jax.ShapeDtypeStruct((1, D_MODEL), F32)
    return pl.pallas_call(
        body, name="small_sums_and_cond_ctx",
        out_shape=[jax.ShapeDtypeStruct((N_DEV * SMALL_ROWS, D_MODEL), F32), jax.ShapeDtypeStruct((SMALL_ROWS, D_MODEL), F32),
                   vec, vec, vec, vec],
        in_specs=[VMEM] * 5, out_specs=[VMEM] * 6,
        scratch_shapes=[pltpu.VMEM((8, D_MODEL), F32), pltpu.VMEM((N_DEV * 8, D_MODEL), F32)] + seven + seven,
    )(small, ada_w0, c_ctx, m, v)


def _ada_update(cond_t, d_mod, ada_w, m, v):
    n_layers, _, n_cols = ada_w.shape
    tr = ROW_TILE

    def body(c_ref, dm_ref, w_ref, m_ref, v_ref, g_ref, d_ref, mo_ref, vo_ref):
        g = _mm_f32(_silu(c_ref[...]), dm_ref[...])
        g_ref[...] = g
        d_ref[...], mo_ref[...], vo_ref[...] = _adamw_math(w_ref[...], g, m_ref[...], v_ref[...])

    tile = pl.BlockSpec((None, tr, n_cols), lambda l, i: (l, i, 0))
    return pl.pallas_call(
        body, name="ada_update", grid=(n_layers, D_MODEL // tr),
        out_shape=[pltpu.HBM(ada_w.shape, F32)] * 4,
        in_specs=[pl.BlockSpec((tr, 16), lambda l, i: (i, 0)), pl.BlockSpec((None, 16, n_cols), lambda l, i: (l, 0, 0)),
                  tile, tile, tile],
        out_specs=[tile] * 4,
    )(*_pin(cond_t, d_mod, ada_w, m, v))


def _local_step(x2, ctx2, target, mod_mine, mod_ctx, lb_logits, scale_full, w_in_full, late_shards, norm_w, gnorm, final_w,
                update_late):
    row = lambda a: a.reshape(1, -1)
    shift0, scale0, gate0 = (row(a) for a in jnp.split(mod_mine[0], 3))
    shift1, scale1, gate1 = (row(a) for a in jnp.split(mod_mine[1], 3))
    shift_c, scale_c, _ = (row(a) for a in jnp.split(mod_ctx, 3))
    nw0, nw1 = norm_w[0:1], norm_w[1:2]
    scales0 = jnp.concatenate([scale_c, scale0])

    g5, h0 = _ln_mod_matmul(ctx2, x2, nw0, jnp.concatenate([shift_c, shift0]), scales0, w_in_full, "hgrn_in_forward")
    o_raw, full = _gla_forward(g5, lb_logits, late_shards)
    x1, res0 = _hgrn_out_forward(o_raw, g5, x2, gnorm, gate0, full["hgrn_w_out"])
    uz, h1 = _ln_mod_matmul(None, x1, nw1, shift1, scale1, full["pool_w_in"], "pool_in_forward")
    d_x2, loss_part, d_final, d_gate1 = _pool_forward_loss(uz, x1, target, gate1, full["pool_w_grp"], scale_full,
                                                           full["pool_w_out"], final_w)

    d_uz, dw_pool_out, dw_pool_grp, d_pscale = _pool_backward(d_x2, uz, gate1, full["pool_w_grp"], scale_full, full["pool_w_out"])
    d_x1, d_nw1, d_mod1 = _ln_mod_backward(d_uz, full["pool_w_in"], None, x1, nw1, scale1, d_x2, "pool_in_backward")
    dw_pool_in = _weight_grad(h1, d_uz, "pool_in_weight_grad")
    d_o, d_z, dw_hgrn_out, d_gate0, d_gnorm = _hgrn_out_backward(d_x1, o_raw, g5, res0, gnorm, gate0, full["hgrn_w_out"])
    late_grads = {"hgrn_w_out": dw_hgrn_out, "pool_w_in": dw_pool_in, "pool_w_grp": dw_pool_grp, "pool_w_out": dw_pool_out}
    d_g5, d_lb, late_slots = _gla_backward(g5, lb_logits, d_o, d_z, [late_grads[k].astype(BF16) for k in GATHER_LATE])
    update_late(late_slots)
    dw_hgrn_in, slots0 = _weight_grad_paired(h0, d_g5, "hgrn_in_weight_grad")
    key = GATHER_EARLY[0]
    send_sem, recv_sem, part_thru, slots_thru, token = _scatter_start(dw_hgrn_in, slots0, d_lb, key)
    d_x, d_nw0, d_mod0 = _ln_mod_backward(d_g5, w_in_full, ctx2, x2, nw0 + token[0:1, 0:1], scales0, d_x1, "hgrn_in_backward")
    pending = (send_sem, recv_sem, part_thru, slots_thru)

    zero = jnp.zeros((1, D_MODEL), F32)
    small = jnp.concatenate([d_mod0[1, 0:2], d_gate0, d_mod1[0, 0:2], d_gate1, d_mod0[0, 0:2], zero, d_nw0, d_nw1, d_gnorm,
                             d_final, d_pscale, d_lb, jnp.broadcast_to(loss_part[0:1, 0:1], (1, D_MODEL)),
                             jnp.zeros((SMALL_ROWS - 17, D_MODEL), F32)], axis=0)
    return {"d_x": d_x, "pending": pending, "small": small}


def kernel(x, c, ctx, c_ctx, ada_w, ada_b, norm_w, hgrn_w_in, hgrn_lb_logits, hgrn_gnorm_w, hgrn_w_out, pool_w_in, pool_w_grp, pool_scale, pool_w_out, final_norm_w, loss_target, m_c_ctx, m_ada_w, m_ada_b, m_norm_w, m_hgrn_w_in, m_hgrn_lb_logits, m_hgrn_gnorm_w, m_hgrn_w_out, m_pool_w_in, m_pool_w_grp, m_pool_scale, m_pool_w_out, m_final_norm_w, v_c_ctx, v_ada_w, v_ada_b, v_norm_w, v_hgrn_w_in, v_hgrn_lb_logits, v_hgrn_gnorm_w, v_hgrn_w_out, v_pool_w_in, v_pool_w_grp, v_pool_scale, v_pool_w_out, v_final_norm_w):
    xi, yi, ci = _my_place()
    chip = 2 * xi + yi
    dev = 4 * xi + 2 * yi + ci
    ada_cols = ada_w.shape[2]
    lb_cols = hgrn_lb_logits.shape[2]
    ps_cols = pool_scale.shape[1]
    row = lambda a: a.reshape(1, -1)

    def chip_cols(a, n):
        return lax.dynamic_slice_in_dim(a, chip * n, n, axis=a.ndim - 1)

    def from_chips(g, rows_per_dev, take):
        return jnp.concatenate([g[2 * s * rows_per_dev:2 * s * rows_per_dev + take] for s in range(N_CHIPS)], axis=1)

    first = jnp.concatenate([jnp.broadcast_to(c, (8, D_MODEL)), jnp.pad(hgrn_lb_logits[0], ((0, 6), (0, 0))),
                             jnp.pad(pool_scale, ((0, 7), (0, 0)))], axis=1)
    shards = {"hgrn_w_in": hgrn_w_in[0], "hgrn_w_out": hgrn_w_out[0], "pool_w_in": pool_w_in[0],
              "pool_w_grp": pool_w_grp[0], "pool_w_out": pool_w_out[0]}
    first_all, parts_all, w_in_full = _prologue(first, row(c_ctx), ada_w, chip_cols(ada_b, ada_cols),
                                                shards[GATHER_EARLY[0]].astype(BF16))
    cond_all = first_all[::8, :D_MODEL]
    lb_logits = from_chips(first_all[:, D_MODEL:D_MODEL + lb_cols], 8, 2)
    scale_full = from_chips(first_all[:, D_MODEL + lb_cols:], 8, 1)
    cond_rows = jnp.concatenate([cond_all, row(c_ctx), jnp.zeros((7, D_MODEL), F32)], axis=0)
    mod_all = from_chips(parts_all, 32, 32).reshape(2, 16, 3 * D_MODEL)
    mod_mine = lax.dynamic_index_in_dim(mod_all, dev, axis=1, keepdims=False)

    out = {}

    def update(name, w, g, m, v):
        shape = w.shape
        w2, g2, m2, v2 = (a.reshape(-1, shape[-1]) for a in (w, g, m, v))
        adamw = _adamw_sparsecore if name in GATHER_LATE else _adamw
        d, mn, vn = adamw(w2, g2, m2, v2, "adamw_" + name)
        out[name] = tuple(a.reshape(shape) for a in (g2, d, mn, vn))

    moments = {"hgrn_w_in": (m_hgrn_w_in, v_hgrn_w_in), "hgrn_w_out": (m_hgrn_w_out, v_hgrn_w_out),
               "pool_w_in": (m_pool_w_in, v_pool_w_in), "pool_w_grp": (m_pool_w_grp, v_pool_w_grp),
               "pool_w_out": (m_pool_w_out, v_pool_w_out)}
    weights = {"hgrn_w_in": hgrn_w_in, "hgrn_w_out": hgrn_w_out, "pool_w_in": pool_w_in, "pool_w_grp": pool_w_grp,
               "pool_w_out": pool_w_out}

    def reduce_scattered(slots, names, name):
        return dict(zip(names, _sum_and_exchange([slots[k] for k in names], names, name)))

    def update_late(late_slots):
        big = reduce_scattered(late_slots, GATHER_LATE, "exchange_halves_late")
        for k in GATHER_LATE:
            update(k, weights[k], big[k], *moments[k])

    loc = _local_step(x[0], ctx[0], loss_target[0], mod_mine, mod_all[0, 8], lb_logits, scale_full, w_in_full,
                      [shards[k].astype(BF16) for k in GATHER_LATE], norm_w, hgrn_gnorm_w, row(final_norm_w), update_late)

    small_all, sums, g_c, d_c, m_c, v_c = _small_sums_and_cond_ctx(loc["small"], ada_w[0], row(c_ctx), row(m_c_ctx), row(v_c_ctx))
    loss = sums[16, 0]

    g_ada_b = jnp.stack([(sums[0:3] + sums[6:9]).reshape(-1), sums[3:6].reshape(-1)])
    update("ada_b", ada_b, g_ada_b, m_ada_b, v_ada_b)
    update("norm_w", norm_w, sums[9:11], m_norm_w, v_norm_w)
    update("hgrn_gnorm_w", hgrn_gnorm_w, sums[11:12], m_hgrn_gnorm_w, v_hgrn_gnorm_w)
    update("final_norm_w", row(final_norm_w), sums[12:13], row(m_final_norm_w), row(v_final_norm_w))
    update("pool_scale", pool_scale, chip_cols(sums[13:14], ps_cols), m_pool_scale, v_pool_scale)
    update("hgrn_lb_logits", hgrn_lb_logits, chip_cols(sums[14:16], lb_cols)[None], m_hgrn_lb_logits, v_hgrn_lb_logits)

    per_dev = small_all.reshape(N_DEV, SMALL_ROWS, D_MODEL)
    pad7 = jnp.zeros((7, 3 * D_MODEL), F32)
    dm0 = jnp.concatenate([per_dev[:, 0:3].reshape(N_DEV, -1), sums[6:9].reshape(1, -1), pad7], axis=0)
    dm1 = jnp.concatenate([per_dev[:, 3:6].reshape(N_DEV, -1), jnp.zeros((8, 3 * D_MODEL), F32)], axis=0)
    d_mod = chip_cols(jnp.stack([dm0, dm1]), ada_cols)
    out["ada_w"] = _ada_update(cond_rows.T, d_mod, ada_w, m_ada_w, v_ada_w)

    out["c_ctx"] = tuple(a.reshape(-1) for a in (g_c, d_c, m_c, v_c))
    out["final_norm_w"] = tuple(a.reshape(-1) for a in out["final_norm_w"])

    done = [g_c, out["ada_w"][1]]
    key = GATHER_EARLY[0]
    _, early = _scatter_wait(*loc["pending"], done, key)
    big_grads = reduce_scattered({key: early}, GATHER_EARLY, "exchange_halves_early")
    for k in GATHER_EARLY:
        update(k, weights[k], big_grads[k], *moments[k])

    names = ["c_ctx", "ada_w", "ada_b", "norm_w", "hgrn_w_in", "hgrn_lb_logits", "hgrn_gnorm_w", "hgrn_w_out", "pool_w_in",
             "pool_w_grp", "pool_scale", "pool_w_out", "final_norm_w"]
    return (loss, loc["d_x"][None], *[out[k][0] for k in names], *[out[k][1] for k in names], *[out[k][2] for k in names],
            *[out[k][3] for k in names])
```

```python
import functools

import numpy as np
import jax
import jax.numpy as jnp
from jax import lax
from jax.experimental import pallas as pl
from jax.experimental.pallas import tpu as pltpu
from jax.experimental.pallas import tpu_sc as plsc

F32 = jnp.float32
BF16 = jnp.bfloat16

D_MODEL = 1024
SEQ = 2048
CTX_LEN = 256
ROWS_ALL = CTX_LEN + SEQ
HEADS = 8
HEAD_DIM = 128
CHUNK = 64
N_CTX_CHUNKS = CTX_LEN // CHUNK
N_LAT_CHUNKS = SEQ // CHUNK
N_CHUNKS = N_CTX_CHUNKS + N_LAT_CHUNKS
GRID_W = 64
POOL_WINDOWS = (2, 4, 8, 16)
POOL_GROUPS = 4
POOL_GROUP_DIM = 256
HGRN_SECTIONS = 5
POOL_SECTIONS = 2
EPS = 1e-6
N_DEV = 8
N_CHIPS = 4
ROW_TILE = 256
POOL_TILE = 512
SMALL_ROWS = 24

ADAM_LR = 0.001
ADAM_B1 = 0.9
ADAM_B2 = 0.999
ADAM_EPS = 1e-08
ADAM_WD = 0.01
ADAM_STEP = 10

MESH = pl.DeviceIdType.MESH
MIB = 1 << 20
ANY = pl.BlockSpec(memory_space=pl.ANY)
VMEM = pl.BlockSpec(memory_space=pltpu.VMEM)


def _params(vmem_mib=None):
    if vmem_mib is None:
        return pltpu.CompilerParams()
    return pltpu.CompilerParams(vmem_limit_bytes=vmem_mib * MIB)


def _pin(*operands):
    return [pltpu.with_memory_space_constraint(a, pltpu.HBM) if a.size * a.dtype.itemsize >= MIB else a for a in operands]


def _sig(a):
    return 0.5 * jnp.tanh(0.5 * a) + 0.5


def _silu(a):
    return a * _sig(a)


def _dsilu(a):
    s = _sig(a)
    return s * (1.0 + a * (1.0 - s))


def _mm(a, b):
    return jnp.dot(a.astype(BF16), b.astype(BF16), preferred_element_type=F32)


def _mm_nt(a, b):
    return lax.dot_general(a.astype(BF16), b.astype(BF16), (((1,), (1,)), ((), ())), preferred_element_type=F32)


def _mm_tn(a, b):
    return lax.dot_general(a.astype(BF16), b.astype(BF16), (((0,), (0,)), ((), ())), preferred_element_type=F32)


def _split2(a):
    hi = a.astype(BF16)
    lo = (a - hi.astype(F32)).astype(BF16)
    return hi, lo


def _mm_exact_lhs(m_bf, a):
    hi, lo = _split2(a)
    return jnp.dot(m_bf, hi, preferred_element_type=F32) + jnp.dot(m_bf, lo, preferred_element_type=F32)


def _mm_f32(a, b):
    ah, al = _split2(a)
    bh, bl = _split2(b)
    return (jnp.dot(ah, bh, preferred_element_type=F32) + jnp.dot(al, bh, preferred_element_type=F32)
            + jnp.dot(ah, bl, preferred_element_type=F32))


def _my_place():
    return lax.axis_index("x"), lax.axis_index("y"), lax.axis_index("c")


def _small_gather(x_ref, out_ref, send_sems, recv_sems, local_sem):
    m_per = x_ref.shape[0]
    x, y, c = _my_place()
    me, sibling = (x, y, c), (x, y, 1 - c)
    chips = [(1 - x, y), (x, 1 - y), (1 - x, 1 - y)]

    def rows(px, py, pc):
        return out_ref.at[pl.ds((4 * px + 2 * py + pc) * m_per, m_per), :]

    def copy(k, block, to, src=None):
        return pltpu.make_async_remote_copy(
            src_ref=rows(*block) if src is None else src, dst_ref=rows(*block),
            send_sem=send_sems.at[k], recv_sem=recv_sems.at[k], device_id=to, device_id_type=MESH)

    def mine():
        return pltpu.make_async_copy(x_ref, rows(*me), local_sem)

    def first():
        return [copy(0, me, sibling, src=x_ref)] + [copy(1 + j, me, (*chip, c), src=x_ref) for j, chip in enumerate(chips)]

    def start():
        mine().start()
        for cp in first():
            cp.start()

    def finish():
        passed = [copy(4 + j, (*chip, c), sibling) for j, chip in enumerate(chips)]
        for j, chip in enumerate(chips):
            copy(1 + j, (*chip, c), me).wait_recv()
            passed[j].start()
        copy(0, sibling, me).wait_recv()
        for j, chip in enumerate(chips):
            copy(4 + j, (*chip, 1 - c), me).wait_recv()
        for cp in first() + passed:
            cp.wait_send()
        mine().wait()

    return start, finish


W_SPECS = {
    "hgrn_w_in": ((D_MODEL, 5 * D_MODEL), (1, 1280, 0, 512)),
    "hgrn_w_out": ((D_MODEL, D_MODEL), (0, 256, 0, 128)),
    "pool_w_in": ((D_MODEL, 2 * D_MODEL), (1, 512, 0, 512)),
    "pool_w_grp": ((POOL_GROUPS, POOL_GROUP_DIM, POOL_GROUP_DIM), (1, 64, 1, 32)),
    "pool_w_out": ((D_MODEL, D_MODEL), (0, 256, 0, 128)),
}

def _al(v, m):
    return pl.multiple_of(v, m)


def _region(ref, spec, chip, half):
    ca, cn, ha, hn = spec
    idx = [slice(None)] * len(ref.shape)
    if ca == ha:
        if half is None:
            idx[ca] = pl.ds(_al(chip * cn, cn), cn)
        else:
            idx[ca] = pl.ds(_al(chip * cn + half * hn, hn), hn)
    else:
        idx[ca] = pl.ds(_al(chip * cn, cn), cn)
        if half is not None:
            idx[ha] = pl.ds(_al(half * hn, hn), hn)
    return ref.at[tuple(idx)]


def _half_of(ref, spec, half):
    _, _, ha, hn = spec
    idx = [slice(None)] * len(ref.shape)
    idx[ha] = pl.ds(_al(half * hn, hn), hn)
    return ref.at[tuple(idx)]


def _half_shape(name):
    full, (ca, cn, ha, hn) = W_SPECS[name]
    shp = list(full)
    shp[ca] = cn
    shp[ha] = hn
    return tuple(shp)


def _gather_two_level(names, sh, land, full, send_sems, recv_sems, local_sems):
    n = len(names)
    specs = [W_SPECS[k][1] for k in names]
    x, y, c = _my_place()
    chip_me = 2 * x + y
    sibling = (x, y, 1 - c)
    chips = [(1 - x, y), (x, 1 - y), (1 - x, 1 - y)]

    def remote(k, src, dst, to):
        return pltpu.make_async_remote_copy(src_ref=src, dst_ref=dst, send_sem=send_sems.at[k], recv_sem=recv_sems.at[k],
                                            device_id=to, device_id_type=MESH)

    def three_halves(a):
        ca, cn, ha, hn = specs[a]
        idx = [slice(None)] * len(land[a].shape)
        if ca == ha:
            idx[ca] = pl.ds(0, 3 * hn)
        else:
            idx[ca], idx[ha] = pl.ds(0, 3 * cn), pl.ds(0, hn)
        return land[a].at[tuple(idx)]

    def own(a):
        return pltpu.make_async_copy(sh[a], _region(land[a], specs[a], chip_me, None), local_sems.at[a])

    def out(a):
        return pltpu.make_async_copy(land[a], full[a], local_sems.at[a])

    def start():
        for a in range(n):
            own(a).start()
            for px, py in chips:
                remote(a, _half_of(sh[a], specs[a], c), _region(land[a], specs[a], chip_me, c), (px, py, c)).start()

    def forward():
        for a in range(n):
            remote(a, three_halves(a), three_halves(a), sibling).wait_recv()
            for px, py in chips:
                landed = _region(land[a], specs[a], 2 * px + py, c)
                remote(n + a, landed, landed, sibling).start()

    def finish():
        for a in range(n):
            remote(n + a, three_halves(a), three_halves(a), sibling).wait_recv()
            remote(a, three_halves(a), three_halves(a), sibling).wait_send()
            remote(n + a, three_halves(a), three_halves(a), sibling).wait_send()
            own(a).wait()
        for a in range(n):
            out(a).start()
        for a in range(n):
            out(a).wait()

    return start, forward, finish


def _scatter_direct(names, part, slots, send_sems, recv_sems, local_sems):
    specs = [W_SPECS[k][1] for k in names]
    x, y, c = _my_place()
    dev_me = 4 * x + 2 * y + c

    def local(a):
        return pltpu.make_async_copy(_region(part[a], specs[a], 2 * x + y, c), slots[a].at[dev_me], local_sems.at[a])

    def start():
        for a in range(len(names)):
            local(a).start()
            for flip in range(1, N_DEV):
                tx = 1 - x if flip >> 2 else x
                ty = 1 - y if (flip >> 1) & 1 else y
                tc = 1 - c if flip & 1 else c
                pltpu.make_async_remote_copy(src_ref=_region(part[a], specs[a], 2 * tx + ty, tc), dst_ref=slots[a].at[dev_me],
                                             send_sem=send_sems.at[a], recv_sem=recv_sems.at[a], device_id=(tx, ty, tc),
                                             device_id_type=MESH).start()

    def wait():
        for a in range(len(names)):
            seven = slots[a].at[pl.ds(0, N_DEV - 1)]
            pltpu.make_async_remote_copy(src_ref=seven, dst_ref=seven, send_sem=send_sems.at[a], recv_sem=recv_sems.at[a],
                                         device_id=(x, y, c), device_id_type=MESH).wait()
            local(a).wait()

    return start, wait


HBM_SPEC = pl.BlockSpec(memory_space=pltpu.HBM)
SEM_SPEC = pl.BlockSpec(memory_space=pltpu.SEMAPHORE)
SPLIT_EFFECT = pltpu.SideEffectType.DATAFLOW_SIDE_EFFECTING


def _scatter_start(part, slots, after, name_key):
    _, cols, _, _ = W_SPECS[name_key][1]

    def body(part_ref, slots_ref, after_ref, send_sem, recv_sem, part_thru, slots_thru, token):
        x, y, c = _my_place()
        for tx, ty in ((1 - x, y), (x, 1 - y), (1 - x, 1 - y)):
            src = part_ref.at[:, pl.ds(_al((2 * tx + ty) * cols, cols), cols)]
            pltpu.make_async_remote_copy(src_ref=src, dst_ref=slots_ref.at[2 * x + y], send_sem=send_sem, recv_sem=recv_sem,
                                         device_id=(tx, ty, c), device_id_type=MESH).start()
        token[...] = jnp.zeros_like(token)

    return pl.pallas_call(
        body, name="scatter_start_" + name_key,
        out_shape=(pltpu.SemaphoreType.DMA(()), pltpu.SemaphoreType.DMA(()), pltpu.HBM(part.shape, part.dtype),
                   pltpu.HBM(slots.shape, slots.dtype), jax.ShapeDtypeStruct((8, 128), F32)),
        in_specs=(HBM_SPEC, HBM_SPEC, ANY), out_specs=(SEM_SPEC, SEM_SPEC, HBM_SPEC, HBM_SPEC, VMEM),
        input_output_aliases={0: 2, 1: 3},
        compiler_params=pltpu.CompilerParams(has_side_effects=SPLIT_EFFECT),
    )(pltpu.with_memory_space_constraint(part, pltpu.HBM), pltpu.with_memory_space_constraint(slots, pltpu.HBM), after)


def _scatter_wait(send_sem, recv_sem, part_thru, slots_thru, after, name_key):
    def body(part_ref, slots_ref, send_sem, recv_sem, *rest):
        x, y, c = _my_place()
        landed = slots_ref.at[pl.ds(0, N_CHIPS - 1)]
        copy = pltpu.make_async_remote_copy(src_ref=landed, dst_ref=landed, send_sem=send_sem, recv_sem=recv_sem,
                                            device_id=(x, y, c), device_id_type=MESH)
        copy.wait_send()
        copy.wait_recv()

    return pl.pallas_call(
        body, name="scatter_wait_" + name_key,
        out_shape=(pltpu.HBM(part_thru.shape, part_thru.dtype), pltpu.HBM(slots_thru.shape, slots_thru.dtype)),
        in_specs=(HBM_SPEC, HBM_SPEC, SEM_SPEC, SEM_SPEC) + (ANY,) * len(after), out_specs=(HBM_SPEC, HBM_SPEC),
        input_output_aliases={0: 0, 1: 1},
        compiler_params=pltpu.CompilerParams(has_side_effects=SPLIT_EFFECT),
    )(part_thru, slots_thru, send_sem, recv_sem, *after)


def _comm_sems(n):
    return [pltpu.SemaphoreType.DMA((n,)), pltpu.SemaphoreType.DMA((n,)), pltpu.SemaphoreType.DMA((n,))]


GATHER_EARLY = ("hgrn_w_in",)
GATHER_LATE = ("hgrn_w_out", "pool_w_in", "pool_w_grp", "pool_w_out")


def _weight_gather(sh, full, spec, send_sems, recv_sems, local_sem):
    x, y, c = _my_place()
    chip_me = 2 * x + y
    sibling = (x, y, 1 - c)
    chips = [(1 - x, y), (x, 1 - y), (1 - x, 1 - y)]

    def remote(k, src, dst, to):
        return pltpu.make_async_remote_copy(src_ref=src, dst_ref=dst, send_sem=send_sems.at[k], recv_sem=recv_sems.at[k],
                                            device_id=to, device_id_type=MESH)

    def own():
        return pltpu.make_async_copy(sh, _region(full, spec, chip_me, None), local_sem)

    def direct():
        return [remote(j, _half_of(sh, spec, c), _region(full, spec, chip_me, c), (px, py, c)) for j, (px, py) in enumerate(chips)]

    def start():
        own().start()
        for cp in direct():
            cp.start()

    def finish():
        passed = []
        for j, (px, py) in enumerate(chips):
            landed = _region(full, spec, 2 * px + py, c)
            remote(j, landed, landed, (px, py, c)).wait_recv()
            passed.append(remote(3 + j, landed, landed, sibling))
            passed[-1].start()
        for j, (px, py) in enumerate(chips):
            other = _region(full, spec, 2 * px + py, 1 - c)
            remote(3 + j, other, other, sibling).wait_recv()
        for cp in direct() + passed:
            cp.wait_send()
        own().wait()

    return start, finish


def _prologue(first, c_ctx, ada_w, ada_b_cols, w_shard):
    n_layers, _, n_cols = ada_w.shape
    key = GATHER_EARLY[0]
    spec = W_SPECS[key][1]
    m_first = first.shape[0]

    def body(first_ref, cctx_ref, adaw_ref, adab_ref, sh_ref, first_all, parts_all, full_ref, parts_scr,
             s1, r1, l1, s2, r2, l2, ws, wr, wl):
        start_first, finish_first = _small_gather(first_ref, first_all, s1, r1, l1)
        start_parts, finish_parts = _small_gather(parts_scr, parts_all, s2, r2, l2)
        start_weight, finish_weight = _weight_gather(sh_ref, full_ref, spec, ws, wr, wl)
        start_first()
        start_weight()
        finish_first()
        cond = jnp.concatenate([first_all[m_first * d:m_first * d + 1, 0:D_MODEL] for d in range(N_DEV)]
                               + [cctx_ref[...], jnp.zeros((16 - N_DEV - 1, D_MODEL), F32)], axis=0)
        act = _silu(cond)
        for i in range(n_layers):
            parts_scr[16 * i:16 * (i + 1), :] = _mm_f32(act, adaw_ref[i]) + adab_ref[i]
        start_parts()
        finish_parts()
        finish_weight()

    seven = [pltpu.SemaphoreType.DMA((7,)), pltpu.SemaphoreType.DMA((7,)), pltpu.SemaphoreType.DMA]
    return pl.pallas_call(
        body, name="prologue",
        out_shape=[jax.ShapeDtypeStruct((N_DEV * m_first, first.shape[1]), F32),
                   jax.ShapeDtypeStruct((N_DEV * 16 * n_layers, n_cols), F32), jax.ShapeDtypeStruct(W_SPECS[key][0], BF16)],
        in_specs=[VMEM] * 5, out_specs=[VMEM] * 3,
        scratch_shapes=[pltpu.VMEM((16 * n_layers, n_cols), F32)] + seven + seven
        + [pltpu.SemaphoreType.DMA((6,)), pltpu.SemaphoreType.DMA((6,)), pltpu.SemaphoreType.DMA],
        compiler_params=_params(48),
    )(first, c_ctx, ada_w, ada_b_cols.reshape(n_layers, 1, n_cols), w_shard)


def _slot_shapes(names):
    return [jax.ShapeDtypeStruct((N_DEV,) + _half_shape(k), BF16) for k in names]


def _sum_and_exchange(slots, names, name):
    n = len(names)
    specs = [W_SPECS[k][1] for k in names]

    def shard_shape(k):
        shp = list(_half_shape(k))
        shp[W_SPECS[k][1][2]] *= 2
        return tuple(shp)

    def body(*refs):
        slot, out, half = refs[:n], refs[n:2 * n], refs[2 * n:3 * n]
        send_sems, recv_sems, local_sems = refs[3 * n:]
        x, y, c = _my_place()
        sibling = (x, y, 1 - c)

        def remote(a, src, dst):
            return pltpu.make_async_remote_copy(src_ref=src, dst_ref=dst, send_sem=send_sems.at[a], recv_sem=recv_sems.at[a],
                                                device_id=sibling, device_id_type=MESH)

        local = [pltpu.make_async_copy(half[a], _half_of(out[a], specs[a], c), local_sems.at[a]) for a in range(n)]
        for a in range(n):
            acc = slot[a][0].astype(F32)
            for d in range(1, slot[a].shape[0]):
                acc = acc + slot[a][d].astype(F32)
            half[a][...] = acc
            local[a].start()
            remote(a, half[a], _half_of(out[a], specs[a], c)).start()
        for a in range(n):
            theirs = _half_of(out[a], specs[a], 1 - c)
            remote(a, theirs, theirs).wait_recv()
            remote(a, half[a], half[a]).wait_send()
            local[a].wait()

    return pl.pallas_call(
        body, name=name,
        out_shape=[jax.ShapeDtypeStruct(shard_shape(k), F32) for k in names],
        in_specs=[VMEM] * n, out_specs=[VMEM] * n,
        scratch_shapes=[pltpu.VMEM(_half_shape(k), F32) for k in names]
        + [pltpu.SemaphoreType.DMA((n,)), pltpu.SemaphoreType.DMA((n,)), pltpu.SemaphoreType.DMA((n,))],
        compiler_params=_params(40),
    )(*slots)


def _ln_mod_matmul(ctx_tile, xin, nw, shift, scale, w, name):
    n_mod = shift.shape[0]
    skip = n_mod - 1
    tm = ROW_TILE if skip else 2 * ROW_TILE
    rows = xin.shape[0] + skip * tm
    n_sec = w.shape[1] // D_MODEL

    def body(*refs):
        c_ref = refs[0] if skip else None
        x_ref, nw_ref, sh_ref, sc_ref, w_ref, g_ref, h_ref = refs[skip:]
        xv = jnp.where(pl.program_id(0) == 0, c_ref[...], x_ref[...]) if skip else x_ref[...]
        r = lax.rsqrt(jnp.mean(xv * xv, axis=-1, keepdims=True) + EPS)
        h_f32 = (xv * r * nw_ref[...]) * (1.0 + sc_ref[...]) + sh_ref[...]
        h = h_f32.astype(BF16)
        h_ref[...] = h
        for k in range(n_sec):
            g_ref[k] = jnp.dot(h, w_ref[:, k * D_MODEL:(k + 1) * D_MODEL], preferred_element_type=F32)

    mod_spec = pl.BlockSpec((None, 1, D_MODEL), lambda i: (jnp.minimum(i, n_mod - 1), 0, 0))
    return pl.pallas_call(
        body, name=name, grid=(rows // tm,),
        out_shape=[pltpu.HBM((n_sec, rows, D_MODEL), F32), pltpu.HBM((rows, D_MODEL), BF16)],
        in_specs=[pl.BlockSpec((tm, D_MODEL), lambda i: (0, 0))] * skip
        + [pl.BlockSpec((tm, D_MODEL), lambda i: (jnp.maximum(i - skip, 0), 0)),
           pl.BlockSpec((1, D_MODEL), lambda i: (0, 0)),
           mod_spec, mod_spec,
           pl.BlockSpec((D_MODEL, n_sec * D_MODEL), lambda i: (0, 0))],
        out_specs=[pl.BlockSpec((n_sec, tm, D_MODEL), lambda i: (0, i, 0)),
                   pl.BlockSpec((tm, D_MODEL), lambda i: (i, 0))],
        compiler_params=_params(48),
    )(*_pin(*([ctx_tile] * skip), xin, nw, shift.reshape(n_mod, 1, D_MODEL), scale.reshape(n_mod, 1, D_MODEL), w))


def _gates(fpre, lb):
    sg = _sig(fpre)
    f = lb + (1.0 - lb) * sg
    return sg, f, 1.0 - f, jnp.log(f)


G_SPEC = lambda sec: pl.BlockSpec((None, ROWS_ALL, HEAD_DIM), lambda h, sec=sec: (sec, 0, h))


GROUP = 2 * CHUNK


def _group_masks():
    rid, cid = np.arange(GROUP)[:, None], np.arange(GROUP)[None, :]
    same = (rid >= CHUNK) == (cid >= CHUNK)
    sums, back = [], []
    for rev in (False, True):
        causal = (cid >= rid) if rev else (cid <= rid)
        anti = (cid <= rid) if rev else (cid >= rid)
        sums.append(same & causal)
        back.append(np.concatenate([same & anti, same & ~anti], axis=1))
    sums, back = np.stack(sums).astype(np.float32), np.stack(back).astype(np.float32)
    return jnp.asarray(sums, BF16), jnp.asarray(back, BF16), jnp.asarray(sums, F32)


MASK_SPECS = [pl.BlockSpec((2, GROUP, GROUP), lambda h: (0, 0, 0)), pl.BlockSpec((2, GROUP, 2 * GROUP), lambda h: (0, 0, 0)),
              pl.BlockSpec((2, GROUP, GROUP), lambda h: (0, 0, 0))]


def _group_sum(m_bf, a):
    hi, lo = _split2(a)
    r = jnp.dot(m_bf, jnp.concatenate([hi, lo], axis=1), preferred_element_type=F32)
    return r[:, :HEAD_DIM] + r[:, HEAD_DIM:]


def _chunk_row(a, pos):
    return jnp.concatenate([jnp.broadcast_to(a[c * CHUNK + pos:c * CHUNK + pos + 1, :], (CHUNK, HEAD_DIM)) for c in range(2)], axis=0)


def _by_chunk(a, second):
    return jnp.concatenate([jnp.where(second, 0.0, a), jnp.where(second, a, 0.0)], axis=1)


def _own_block(r):
    return jnp.concatenate([r[0:CHUNK, 0:HEAD_DIM], r[CHUNK:GROUP, HEAD_DIM:2 * HEAD_DIM]], axis=0)


def _scan_step_of(row_chunk, rev, latent):
    if not rev:
        return row_chunk
    return (N_CHUNKS + N_CTX_CHUNKS - 1 - row_chunk) if latent else (N_CTX_CHUNKS - 1 - row_chunk)


def _group_rows(i, j, per_step, latent):
    base = CTX_LEN if latent else 0
    return pl.multiple_of(base + (i * per_step + j) * GROUP, GROUP)


GROUPS_PER_STEP = 8
GROUPS_PER_GRAD_STEP = 4


def _gla_forward(g5, lb_logits, late_shards):
    q_scale = HEAD_DIM ** -0.5
    per_lat, per_ctx = GROUPS_PER_STEP, min(GROUPS_PER_STEP, CTX_LEN // GROUP)
    n_late = len(GATHER_LATE)

    def body(ff_ref, fb_ref, v_ref, q_ref, lg_ref, sums_ref, _, keep_ref, *rest):
        shard_refs, o_ref, full_refs = rest[:n_late], rest[n_late], rest[n_late + 1:2 * n_late + 1]
        st_ref, decay_ref, qt_ref = rest[2 * n_late + 1:2 * n_late + 4]
        land_refs = rest[2 * n_late + 4:3 * n_late + 4]
        start_gather, forward_gather, finish_gather = _gather_two_level(GATHER_LATE, shard_refs, land_refs, full_refs,
                                                                        *rest[3 * n_late + 4:])

        @pl.when(pl.program_id(0) == 0)
        def _():
            start_gather()

        @pl.when(pl.program_id(0) == HEADS - 1)
        def _():
            forward_gather()

        second = lax.broadcasted_iota(jnp.int32, (GROUP, HEAD_DIM), 0) >= CHUNK
        for rev in (False, True):
            f_ref = fb_ref if rev else ff_ref
            lb = _sig(lg_ref[1:2, :] if rev else lg_ref[0:1, :])
            d = 1 if rev else 0
            last = 0 if rev else CHUNK - 1
            mid = CHUNK // 2 if rev else CHUNK // 2 - 1

            def local_step(i, carry, latent, per):
                r0s = [_group_rows(i, j, per, latent) for j in range(per)]
                rows = [pl.ds(r0, GROUP) for r0 in r0s]
                gates = [_gates(f_ref[r, :], lb) for r in rows]
                vs = [v_ref[r, :] for r in rows]
                bs = [_group_sum(sums_ref[d], g[3]) for g in gates]
                bls = [_chunk_row(b, last) for b in bs]
                ups = [_mm_tn(v, _by_chunk(g[2] * jnp.exp(bl - b), second)) for v, g, b, bl in zip(vs, gates, bs, bls)]
                if latent:
                    qs = [_silu(q_ref[r, :]) * q_scale for r in rows]
                    bms = [_chunk_row(b, mid) for b in bs]
                    a_s = [_mm_nt(q * jnp.exp(b - bm), g[2] * jnp.exp(bm - b)) for q, g, b, bm in zip(qs, gates, bs, bms)]
                    outs = [_mm(a * keep_ref[d], v) for a, v in zip(a_s, vs)]
                for j in range(per):
                    for c in range(2):
                        step = _scan_step_of(r0s[j] // CHUNK + c, rev, latent)
                        decay_ref[step] = jnp.exp(bls[j][c * CHUNK:c * CHUNK + 1, :])
                        st_ref[step] = ups[j][:, c * HEAD_DIM:(c + 1) * HEAD_DIM]
                    if latent:
                        orow = pl.ds(pl.multiple_of(r0s[j] - CTX_LEN, GROUP), GROUP)
                        qt_ref[orow, :] = (qs[j] * jnp.exp(bs[j])).astype(BF16)
                        if rev:
                            o_ref[orow, :] += outs[j]
                        else:
                            o_ref[orow, :] = outs[j]
                return carry

            lax.fori_loop(0, CTX_LEN // (per_ctx * GROUP), functools.partial(local_step, latent=False, per=per_ctx), 0)
            lax.fori_loop(0, SEQ // (per_lat * GROUP), functools.partial(local_step, latent=True, per=per_lat), 0)

            def scan_step(t, st):
                update = st_ref[t]
                st_ref[t] = st
                return st * decay_ref[t] + update

            lax.fori_loop(0, N_CHUNKS, scan_step, jnp.zeros((HEAD_DIM, HEAD_DIM), F32), unroll=2)

            def inter_step(i, carry):
                r0s = [_group_rows(i, j, per_lat, True) for j in range(per_lat)]
                orows = [pl.ds(pl.multiple_of(r0 - CTX_LEN, GROUP), GROUP) for r0 in r0s]
                states = [jnp.concatenate([st_ref[_scan_step_of(r0 // CHUNK + c, rev, True)] for c in range(2)], axis=0)
                          for r0 in r0s]
                prods = [lax.dot_general(qt_ref[orow, :], s.astype(BF16), (((1,), (1,)), ((), ())), preferred_element_type=F32)
                         for orow, s in zip(orows, states)]
                for orow, r in zip(orows, prods):
                    o_ref[orow, :] += _own_block(r)
                return carry

            lax.fori_loop(0, SEQ // (per_lat * GROUP), inter_step, 0)

        @pl.when(pl.program_id(0) == HEADS - 1)
        def _():
            finish_gather()

    outs = pl.pallas_call(
        body, name="gla_forward", grid=(HEADS,),
        out_shape=[pltpu.HBM((SEQ, D_MODEL), F32)] + [jax.ShapeDtypeStruct(W_SPECS[k][0], BF16) for k in GATHER_LATE],
        in_specs=[G_SPEC(0), G_SPEC(1), G_SPEC(2), G_SPEC(3), pl.BlockSpec((2, HEAD_DIM), lambda h: (0, h))] + MASK_SPECS
        + [ANY] * n_late,
        out_specs=[pl.BlockSpec((SEQ, HEAD_DIM), lambda h: (0, h))] + [ANY] * n_late,
        scratch_shapes=[pltpu.VMEM((N_CHUNKS, HEAD_DIM, HEAD_DIM), F32), pltpu.VMEM((N_CHUNKS, 1, HEAD_DIM), F32),
                        pltpu.VMEM((SEQ, HEAD_DIM), BF16)] + [pltpu.VMEM(W_SPECS[k][0], BF16) for k in GATHER_LATE]
        + [pltpu.SemaphoreType.DMA((2 * n_late,)), pltpu.SemaphoreType.DMA((2 * n_late,)), pltpu.SemaphoreType.DMA((n_late,))],
        compiler_params=_params(48),
    )(*_pin(g5, g5, g5, g5, lb_logits), *_group_masks(), *late_shards)
    return outs[0], dict(zip(GATHER_LATE, outs[1:]))


def _gla_backward(g5, lb_logits, d_o, d_z, late_parts):
    q_scale = HEAD_DIM ** -0.5
    per_lat, per_ctx = GROUPS_PER_STEP, min(GROUPS_PER_STEP, CTX_LEN // GROUP)
    n_late = len(GATHER_LATE)

    def body(ff_ref, fb_ref, v_ref, q_ref, lg_ref, do_ref, dz_ref, sums_ref, back_ref, keep_ref, *rest):
        part_refs, (dg_ref, dlg_ref), slot_refs = rest[:n_late], rest[n_late:n_late + 2], rest[n_late + 2:2 * n_late + 2]
        st_ref, dst_ref, decay_ref, both_ref = rest[2 * n_late + 2:2 * n_late + 6]
        start_scatter, wait_scatter = _scatter_direct(GATHER_LATE, part_refs, slot_refs, *rest[2 * n_late + 6:])

        @pl.when(pl.program_id(0) == 0)
        def _():
            start_scatter()

        dg_ref[3, 0:CTX_LEN, :] = jnp.zeros((CTX_LEN, HEAD_DIM), BF16)
        dg_ref[4, 0:CTX_LEN, :] = jnp.zeros((CTX_LEN, HEAD_DIM), BF16)
        dg_ref[4, CTX_LEN:ROWS_ALL, :] = dz_ref[...]
        second = lax.broadcasted_iota(jnp.int32, (GROUP, HEAD_DIM), 0) >= CHUNK
        for rev in (False, True):
            d = 1 if rev else 0
            f_ref = fb_ref if rev else ff_ref
            lb = _sig(lg_ref[d:d + 1, :])
            last = 0 if rev else CHUNK - 1
            mid = CHUNK // 2 if rev else CHUNK // 2 - 1

            def local_step(i, carry, latent, per):
                r0s = [_group_rows(i, j, per, latent) for j in range(per)]
                rows = [pl.ds(r0, GROUP) for r0 in r0s]
                gates = [_gates(f_ref[r, :], lb) for r in rows]
                bs = [_group_sum(sums_ref[d], g[3]) for g in gates]
                bls = [_chunk_row(b, last) for b in bs]
                ups = [_mm_tn(v_ref[r, :], _by_chunk(g[2] * jnp.exp(bl - b), second)) for r, g, b, bl in zip(rows, gates, bs, bls)]
                if latent:
                    orows = [pl.ds(pl.multiple_of(r0 - CTX_LEN, GROUP), GROUP) for r0 in r0s]
                    d_ups = [_mm_tn(do_ref[orow, :], _by_chunk(_silu(q_ref[r, :]) * q_scale * jnp.exp(b), second))
                             for orow, r, b in zip(orows, rows, bs)]
                for j in range(per):
                    for c in range(2):
                        step = _scan_step_of(r0s[j] // CHUNK + c, rev, latent)
                        decay_ref[step] = jnp.exp(bls[j][c * CHUNK:c * CHUNK + 1, :])
                        st_ref[step] = ups[j][:, c * HEAD_DIM:(c + 1) * HEAD_DIM]
                        if latent:
                            dst_ref[step] = d_ups[j][:, c * HEAD_DIM:(c + 1) * HEAD_DIM]
                        else:
                            dst_ref[step] = jnp.zeros((HEAD_DIM, HEAD_DIM), F32)
                return carry

            lax.fori_loop(0, CTX_LEN // (per_ctx * GROUP), functools.partial(local_step, latent=False, per=per_ctx), 0)
            lax.fori_loop(0, SEQ // (per_lat * GROUP), functools.partial(local_step, latent=True, per=per_lat), 0)

            def scan_step(i, carry):
                st, d_st = carry
                j = N_CHUNKS - 1 - i
                update, d_update = st_ref[i], dst_ref[j]
                st_ref[i] = st
                dst_ref[j] = d_st
                return st * decay_ref[i] + update, d_st * decay_ref[j] + d_update

            zero_state = jnp.zeros((HEAD_DIM, HEAD_DIM), F32)
            lax.fori_loop(0, N_CHUNKS, scan_step, (zero_state, zero_state))

            def grad_step(i, dlb, latent, per):
                r0s = [_group_rows(i, j, per, latent) for j in range(per)]
                rows = [pl.ds(r0, GROUP) for r0 in r0s]
                gates = [_gates(f_ref[r, :], lb) for r in rows]
                vs = [v_ref[r, :] for r in rows]
                bs = [_group_sum(sums_ref[d], g[3]) for g in gates]
                bls = [_chunk_row(b, last) for b in bs]
                e_ends = [jnp.exp(bl - b) for b, bl in zip(bs, bls)]
                k_ends = [g[2] * e for g, e in zip(gates, e_ends)]
                sts = [[st_ref[_scan_step_of(r0 // CHUNK + c, rev, latent)] for c in range(2)] for r0 in r0s]
                d_sts = [[dst_ref[_scan_step_of(r0 // CHUNK + c, rev, latent)] for c in range(2)] for r0 in r0s]
                d_kends = [_own_block(_mm(v, jnp.concatenate(ds, axis=1))) for v, ds in zip(vs, d_sts)]
                d_vs = [_own_block(_mm_nt(ke, jnp.concatenate(ds, axis=0))) for ke, ds in zip(k_ends, d_sts)]
                at_last = [jnp.concatenate([jnp.broadcast_to(jnp.sum(ds[c] * s[c], axis=0, keepdims=True), (CHUNK, HEAD_DIM))
                                            for c in range(2)], axis=0) * jnp.exp(bl) for ds, s, bl in zip(d_sts, sts, bls)]
                t_kends = [dk * ke for dk, ke in zip(d_kends, k_ends)]
                d_ks = [dk * e for dk, e in zip(d_kends, e_ends)]
                if latent:
                    orows = [pl.ds(pl.multiple_of(r0 - CTX_LEN, GROUP), GROUP) for r0 in r0s]
                    qpres = [q_ref[r, :] for r in rows]
                    q_sigs = [_sig(qp) for qp in qpres]
                    qs = [qp * sg * q_scale for qp, sg in zip(qpres, q_sigs)]
                    bms = [_chunk_row(b, mid) for b in bs]
                    e_bs = [jnp.exp(b) for b in bs]
                    e_qms = [jnp.exp(b - bm) for b, bm in zip(bs, bms)]
                    e_kms = [jnp.exp(bm - b) for b, bm in zip(bs, bms)]
                    q_ts = [q * e for q, e in zip(qs, e_bs)]
                    q_ms = [q * e for q, e in zip(qs, e_qms)]
                    k_ms = [g[2] * e for g, e in zip(gates, e_kms)]
                    d_outs = [do_ref[orow, :] for orow in orows]
                    a_s = [_mm_nt(qm, km) * keep_ref[d] for qm, km in zip(q_ms, k_ms)]
                    d_as = [_mm_nt(do, v) * keep_ref[d] for do, v in zip(d_outs, vs)]
                    d_qts = [_own_block(_mm(do, jnp.concatenate(s, axis=1))) for do, s in zip(d_outs, sts)]
                    d_qms = [_mm(da, km) for da, km in zip(d_as, k_ms)]
                    d_kms = [_mm_tn(da, qm) for da, qm in zip(d_as, q_ms)]
                    d_vs = [dv + _mm_tn(a, do) for dv, a, do in zip(d_vs, a_s, d_outs)]
                    d_ks = [dk + dkm * e for dk, dkm, e in zip(d_ks, d_kms, e_kms)]
                    d_bs = [jnp.concatenate([dqt * qt + dqm * qm - dkm * km, t], axis=0)
                            for dqt, qt, dqm, qm, dkm, km, t in zip(d_qts, q_ts, d_qms, q_ms, d_kms, k_ms, t_kends)]
                    d_qs = [dqt * eb + dqm * eq for dqt, eb, dqm, eq in zip(d_qts, e_bs, d_qms, e_qms)]
                    back = back_ref[d]
                else:
                    d_bs, back = t_kends, back_ref[d, :, GROUP:2 * GROUP]
                d_lfs = [_group_sum(back, db) + al for db, al in zip(d_bs, at_last)]
                for j in range(per):
                    sg, f = gates[j][0], gates[j][1]
                    d_f = d_lfs[j] / f - d_ks[j]
                    dg_ref[d, rows[j], :] = (d_f * (1.0 - lb) * sg * (1.0 - sg)).astype(BF16)
                    dlb = dlb + jnp.sum(d_f * (1.0 - sg), axis=0, keepdims=True)
                    if rev:
                        dg_ref[2, rows[j], :] = (both_ref[0, rows[j], :] + d_vs[j]).astype(BF16)
                    else:
                        both_ref[0, rows[j], :] = d_vs[j]
                    if latent:
                        d_qpre = d_qs[j] * q_scale * (q_sigs[j] * (1.0 + qpres[j] * (1.0 - q_sigs[j])))
                        if rev:
                            dg_ref[3, rows[j], :] = (both_ref[1, rows[j], :] + d_qpre).astype(BF16)
                        else:
                            both_ref[1, rows[j], :] = d_qpre
                return dlb

            dlb = lax.fori_loop(0, SEQ // (GROUPS_PER_GRAD_STEP * GROUP),
                                functools.partial(grad_step, latent=True, per=GROUPS_PER_GRAD_STEP), jnp.zeros((1, HEAD_DIM), F32))
            dlb = lax.fori_loop(0, CTX_LEN // (per_ctx * GROUP), functools.partial(grad_step, latent=False, per=per_ctx), dlb)
            dlg_ref[d:d + 1, :] = dlb * lb * (1.0 - lb)

        @pl.when(pl.program_id(0) == HEADS - 1)
        def _():
            wait_scatter()

    col = pl.BlockSpec((SEQ, HEAD_DIM), lambda h: (0, h))
    outs = pl.pallas_call(
        body, name="gla_backward", grid=(HEADS,),
        out_shape=[pltpu.HBM((HGRN_SECTIONS, ROWS_ALL, D_MODEL), BF16), jax.ShapeDtypeStruct((2, D_MODEL), F32)]
        + _slot_shapes(GATHER_LATE),
        in_specs=[G_SPEC(0), G_SPEC(1), G_SPEC(2), G_SPEC(3), pl.BlockSpec((2, HEAD_DIM), lambda h: (0, h)), col, col]
        + MASK_SPECS + [ANY] * n_late,
        out_specs=[pl.BlockSpec((HGRN_SECTIONS, ROWS_ALL, HEAD_DIM), lambda h: (0, 0, h)),
                   pl.BlockSpec((2, HEAD_DIM), lambda h: (0, h))] + [ANY] * n_late,
        scratch_shapes=[pltpu.VMEM((N_CHUNKS, HEAD_DIM, HEAD_DIM), F32), pltpu.VMEM((N_CHUNKS, HEAD_DIM, HEAD_DIM), F32),
                        pltpu.VMEM((N_CHUNKS, 1, HEAD_DIM), F32), pltpu.VMEM((2, ROWS_ALL, HEAD_DIM), F32)]
        + _comm_sems(n_late),
        compiler_params=_params(48),
    )(*_pin(g5, g5, g5, g5, lb_logits, d_o, d_z), *_group_masks(), *late_parts)
    return outs[0], outs[1], dict(zip(GATHER_LATE, outs[2:]))


def _head_norm(o, scr):
    rs = []
    for h in range(HEADS):
        cols = slice(h * HEAD_DIM, (h + 1) * HEAD_DIM)
        oh = o[:, cols]
        r = lax.rsqrt(jnp.mean(oh * oh, axis=-1, keepdims=True) + EPS)
        scr[:, cols] = oh * r
        rs.append(r)
    return rs


def _hgrn_out_forward(o_raw, g5, xin, gnorm_w, gate, w_out):
    tm = ROW_TILE

    def body(o_ref, z_ref, x_ref, gw_ref, gate_ref, w_ref, x1_ref, res_ref, scr):
        _head_norm(o_ref[...], scr)
        a = scr[...] * gw_ref[...] * _silu(z_ref[...])
        res = _mm(a, w_ref[...])
        res_ref[...] = res
        x1_ref[...] = x_ref[...] + gate_ref[...] * res

    tile = pl.BlockSpec((tm, D_MODEL), lambda i: (i, 0))
    vec = pl.BlockSpec((1, D_MODEL), lambda i: (0, 0))
    return pl.pallas_call(
        body, name="hgrn_out_forward", grid=(SEQ // tm,),
        out_shape=[pltpu.HBM((SEQ, D_MODEL), F32)] * 2,
        in_specs=[tile, pl.BlockSpec((None, tm, D_MODEL), lambda i: (4, i + CTX_LEN // tm, 0)), tile, vec, vec,
                  pl.BlockSpec((D_MODEL, D_MODEL), lambda i: (0, 0))],
        out_specs=[tile, tile],
        scratch_shapes=[pltpu.VMEM((tm, D_MODEL), F32)],
        compiler_params=_params(32),
    )(*_pin(o_raw, g5, xin, gnorm_w, gate, w_out))


def _hgrn_out_backward(d_x1, o_raw, g5, res, gnorm_w, gate, w_out):
    tm = ROW_TILE

    def body(dx_ref, o_ref, z_ref, res_ref, gw_ref, gate_ref, w_ref, do_ref, dz_ref, dw_out, dgate_ref, dgw_ref, scr, scr2,
             dw_ref):
        @pl.when(pl.program_id(0) == 0)
        def _():
            dw_ref[...] = jnp.zeros_like(dw_ref)
            dgate_ref[...] = jnp.zeros_like(dgate_ref)
            dgw_ref[...] = jnp.zeros_like(dgw_ref)

        dx = dx_ref[...]
        dgate_ref[...] += jnp.sum(dx * res_ref[...], axis=0, keepdims=True)
        d_res = (dx * gate_ref[...]).astype(BF16)
        d_a = _mm_nt(d_res, w_ref[...])
        rs = _head_norm(o_ref[...], scr)
        z = z_ref[...]
        sz = _silu(z)
        o_hat = scr[...]
        o_n = o_hat * gw_ref[...]
        dw_ref[...] += _mm_tn(o_n * sz, d_res)
        d_on = d_a * sz
        dz_ref[...] = (d_a * o_n * _dsilu(z)).astype(BF16)
        dgw_ref[...] += jnp.sum(d_on * o_hat, axis=0, keepdims=True)
        scr2[...] = d_on * gw_ref[...]
        for h in range(HEADS):
            cols = slice(h * HEAD_DIM, (h + 1) * HEAD_DIM)
            dh, oh = scr2[:, cols], scr[:, cols]
            do_ref[:, cols] = (rs[h] * (dh - oh * jnp.mean(dh * oh, axis=-1, keepdims=True))).astype(BF16)

        @pl.when(pl.program_id(0) == SEQ // tm - 1)
        def _():
            dw_out[...] = dw_ref[...].astype(BF16)

    tile = pl.BlockSpec((tm, D_MODEL), lambda i: (i, 0))
    vec = pl.BlockSpec((1, D_MODEL), lambda i: (0, 0))
    mat = pl.BlockSpec((D_MODEL, D_MODEL), lambda i: (0, 0))
    return pl.pallas_call(
        body, name="hgrn_out_backward", grid=(SEQ // tm,),
        out_shape=[pltpu.HBM((SEQ, D_MODEL), BF16)] * 2 + [pltpu.HBM((D_MODEL, D_MODEL), BF16)]
        + [jax.ShapeDtypeStruct((1, D_MODEL), F32)] * 2,
        in_specs=[tile, tile, pl.BlockSpec((None, tm, D_MODEL), lambda i: (4, i + CTX_LEN // tm, 0)), tile, vec, vec, mat],
        out_specs=[tile, tile, mat, vec, vec],
        scratch_shapes=[pltpu.VMEM((tm, D_MODEL), F32)] * 2 + [pltpu.VMEM((D_MODEL, D_MODEL), F32)],
        compiler_params=_params(40),
    )(*_pin(d_x1, o_raw, g5, res, gnorm_w, gate, w_out))


def _pool_constants():
    win = np.zeros((POOL_GROUPS, ROW_TILE, ROW_TILE), np.float32)
    inv = np.zeros((POOL_GROUPS, ROW_TILE, 1), np.float32)
    for g, w in enumerate(POOL_WINDOWS):
        for t in range(ROW_TILE):
            base, p = (t // GRID_W) * GRID_W, t % GRID_W
            lo = min(max(p - w // 2, 0), GRID_W)
            hi = min(max(p - w // 2 + w, 0), GRID_W)
            win[g, t, base + lo:base + hi] = 1.0
            inv[g, t, 0] = 1.0 / np.float32(hi - lo)
    return jnp.asarray(win, BF16), jnp.asarray(win.transpose(0, 2, 1), BF16), jnp.asarray(inv, F32)


def _pool_mix(u_ref, wg_ref, win_ref, inv_ref, pooled_scr, yg_scr):
    for g in range(POOL_GROUPS):
        cols = slice(g * POOL_GROUP_DIM, (g + 1) * POOL_GROUP_DIM)
        for r in range(u_ref.shape[0] // ROW_TILE):
            rows = slice(r * ROW_TILE, (r + 1) * ROW_TILE)
            ug = u_ref[rows, cols]
            pooled_scr[rows, cols] = _mm_exact_lhs(win_ref[g], ug) * inv_ref[g] - ug
        yg_scr[:, cols] = _mm(pooled_scr[:, cols], wg_ref[g])


def _pool_forward_loss(uz, x1, target, gate, w_grp, pool_scale, w_out, final_w):
    tm = POOL_TILE
    win, _, inv = _pool_constants()

    def body(u_ref, z_ref, x_ref, t_ref, gate_ref, wg_ref, ps_ref, w_ref, fw_ref, win_ref, inv_ref,
             dx_ref, loss_ref, dfw_ref, dgate_ref, yg_scr, pooled_scr):
        @pl.when(pl.program_id(0) == 0)
        def _():
            loss_ref[...] = jnp.zeros_like(loss_ref)
            dfw_ref[...] = jnp.zeros_like(dfw_ref)
            dgate_ref[...] = jnp.zeros_like(dgate_ref)

        _pool_mix(u_ref, wg_ref, win_ref, inv_ref, pooled_scr, yg_scr)
        a = yg_scr[...] * ps_ref[...] * _silu(z_ref[...])
        res = _mm(a, w_ref[...])
        x2 = x_ref[...] + gate_ref[...] * res
        r = lax.rsqrt(jnp.mean(x2 * x2, axis=-1, keepdims=True) + EPS)
        xh = x2 * r
        fw = fw_ref[...]
        err = xh * fw - t_ref[...]
        loss_ref[...] += 0.5 * jnp.sum(jnp.mean(err * err, axis=-1, keepdims=True))
        d_y = err * (1.0 / D_MODEL)
        dfw_ref[...] += jnp.sum(d_y * xh, axis=0, keepdims=True)
        d_xh = d_y * fw
        d_x2 = r * (d_xh - xh * jnp.mean(d_xh * xh, axis=-1, keepdims=True))
        dx_ref[...] = d_x2
        dgate_ref[...] += jnp.sum(d_x2 * res, axis=0, keepdims=True)

    tile = pl.BlockSpec((tm, D_MODEL), lambda i: (i, 0))
    vec = pl.BlockSpec((1, D_MODEL), lambda i: (0, 0))
    grp = pl.BlockSpec((POOL_GROUPS, POOL_GROUP_DIM, POOL_GROUP_DIM), lambda i: (0, 0, 0))
    return pl.pallas_call(
        body, name="pool_forward_loss", grid=(SEQ // tm,),
        out_shape=[pltpu.HBM((SEQ, D_MODEL), F32), jax.ShapeDtypeStruct((8, 128), F32),
                   jax.ShapeDtypeStruct((1, D_MODEL), F32), jax.ShapeDtypeStruct((1, D_MODEL), F32)],
        in_specs=[pl.BlockSpec((None, tm, D_MODEL), lambda i: (0, i, 0)), pl.BlockSpec((None, tm, D_MODEL), lambda i: (1, i, 0)),
                  tile, tile, vec, grp, vec, pl.BlockSpec((D_MODEL, D_MODEL), lambda i: (0, 0)), vec, grp,
                  pl.BlockSpec((POOL_GROUPS, ROW_TILE, 1), lambda i: (0, 0, 0))],
        out_specs=[tile, pl.BlockSpec((8, 128), lambda i: (0, 0)), vec, vec],
        scratch_shapes=[pltpu.VMEM((tm, D_MODEL), F32)] * 2,
        compiler_params=_params(40),
    )(*_pin(uz, uz, x1, target, gate, w_grp, pool_scale, w_out, final_w, win, inv))


def _pool_backward(d_x2, uz, gate, w_grp, pool_scale, w_out):
    tm = POOL_TILE
    win, win_t, inv = _pool_constants()

    def body(dx_ref, u_ref, z_ref, gate_ref, wg_ref, ps_ref, w_ref, win_ref, wint_ref, inv_ref,
             duz_ref, dw_out, dwg_out, dps_ref, pooled_scr, yg_scr, dyg_scr, dw_ref, dwg_ref):
        @pl.when(pl.program_id(0) == 0)
        def _():
            dw_ref[...] = jnp.zeros_like(dw_ref)
            dwg_ref[...] = jnp.zeros_like(dwg_ref)
            dps_ref[...] = jnp.zeros_like(dps_ref)

        _pool_mix(u_ref, wg_ref, win_ref, inv_ref, pooled_scr, yg_scr)
        z = z_ref[...]
        sz = _silu(z)
        yg = yg_scr[...]
        y = yg * ps_ref[...]
        d_res = (dx_ref[...] * gate_ref[...]).astype(BF16)
        d_a = _mm_nt(d_res, w_ref[...])
        dw_ref[...] += _mm_tn(y * sz, d_res)
        d_y = d_a * sz
        duz_ref[1] = (d_a * y * _dsilu(z)).astype(BF16)
        dps_ref[...] += jnp.sum(d_y * yg, axis=0, keepdims=True)
        dyg_scr[...] = d_y * ps_ref[...]
        for g in range(POOL_GROUPS):
            cols = slice(g * POOL_GROUP_DIM, (g + 1) * POOL_GROUP_DIM)
            d_yg = dyg_scr[:, cols].astype(BF16)
            d_pool = _mm_nt(d_yg, wg_ref[g])
            dwg_ref[g] += _mm_tn(pooled_scr[:, cols], d_yg)
            for r in range(tm // ROW_TILE):
                rows = slice(r * ROW_TILE, (r + 1) * ROW_TILE)
                dp = d_pool[rows, :]
                duz_ref[0, rows, cols] = (_mm_exact_lhs(wint_ref[g], dp * inv_ref[g]) - dp).astype(BF16)

        @pl.when(pl.program_id(0) == SEQ // tm - 1)
        def _():
            dw_out[...] = dw_ref[...].astype(BF16)
            dwg_out[...] = dwg_ref[...].astype(BF16)

    tile = pl.BlockSpec((tm, D_MODEL), lambda i: (i, 0))
    vec = pl.BlockSpec((1, D_MODEL), lambda i: (0, 0))
    mat = pl.BlockSpec((D_MODEL, D_MODEL), lambda i: (0, 0))
    grp = pl.BlockSpec((POOL_GROUPS, POOL_GROUP_DIM, POOL_GROUP_DIM), lambda i: (0, 0, 0))
    return pl.pallas_call(
        body, name="pool_backward", grid=(SEQ // tm,),
        out_shape=[pltpu.HBM((POOL_SECTIONS, SEQ, D_MODEL), BF16), pltpu.HBM((D_MODEL, D_MODEL), BF16),
                   pltpu.HBM((POOL_GROUPS, POOL_GROUP_DIM, POOL_GROUP_DIM), BF16), jax.ShapeDtypeStruct((1, D_MODEL), F32)],
        in_specs=[tile, pl.BlockSpec((None, tm, D_MODEL), lambda i: (0, i, 0)), pl.BlockSpec((None, tm, D_MODEL), lambda i: (1, i, 0)),
                  vec, grp, vec, mat, grp, grp, pl.BlockSpec((POOL_GROUPS, ROW_TILE, 1), lambda i: (0, 0, 0))],
        out_specs=[pl.BlockSpec((POOL_SECTIONS, tm, D_MODEL), lambda i: (0, i, 0)), mat, grp, vec],
        scratch_shapes=[pltpu.VMEM((tm, D_MODEL), F32)] * 3 + [pltpu.VMEM((D_MODEL, D_MODEL), F32),
                                                               pltpu.VMEM((POOL_GROUPS, POOL_GROUP_DIM, POOL_GROUP_DIM), F32)],
        compiler_params=_params(48),
    )(*_pin(d_x2, uz, uz, gate, w_grp, pool_scale, w_out, win, win_t, inv))


def _ln_mod_backward(d_g, w, ctx_tile, xin, nw, scale, d_up, name):
    n_sec, rows, _ = d_g.shape
    n_mod = scale.shape[0]
    skip = n_mod - 1
    tm = ROW_TILE if skip else 2 * ROW_TILE
    n_tiles = rows // tm

    def body(dg_ref, w_ref, *refs):
        c_ref = refs[0] if skip else None
        x_ref, nw_ref, sc_ref, up_ref, dx_ref, dnw_ref, dmod_ref = refs[skip:]
        i = pl.program_id(0)

        @pl.when(i == 0)
        def _():
            dnw_ref[...] = jnp.zeros_like(dnw_ref)

        @pl.when((i == 0) | (i == skip))
        def _():
            dmod_ref[...] = jnp.zeros_like(dmod_ref)

        d_h = _mm_nt(dg_ref[0], w_ref[:, 0:D_MODEL])
        for k in range(1, n_sec):
            d_h = d_h + _mm_nt(dg_ref[k], w_ref[:, k * D_MODEL:(k + 1) * D_MODEL])
        xv = jnp.where(i == 0, c_ref[...], x_ref[...]) if skip else x_ref[...]
        r = lax.rsqrt(jnp.mean(xv * xv, axis=-1, keepdims=True) + EPS)
        xh = xv * r
        nw_row = nw_ref[...]
        dmod_ref[0:1, :] += jnp.sum(d_h, axis=0, keepdims=True)
        dmod_ref[1:2, :] += jnp.sum(d_h * (xh * nw_row), axis=0, keepdims=True)
        d_xn = d_h * (1.0 + sc_ref[...])
        dnw_ref[...] += jnp.sum(d_xn * xh, axis=0, keepdims=True)
        d_xh = d_xn * nw_row

        @pl.when(i >= skip)
        def _():
            dx_ref[...] = up_ref[...] + r * (d_xh - xh * jnp.mean(d_xh * xh, axis=-1, keepdims=True))

    lat = lambda i: (jnp.maximum(i - skip, 0), 0)
    mod_idx = lambda i: (jnp.minimum(i, n_mod - 1), 0, 0)
    return pl.pallas_call(
        body, name=name, grid=(n_tiles,),
        out_shape=[pltpu.HBM((rows - skip * tm, D_MODEL), F32), jax.ShapeDtypeStruct((1, D_MODEL), F32),
                   jax.ShapeDtypeStruct((n_mod, 8, D_MODEL), F32)],
        in_specs=[pl.BlockSpec((n_sec, tm, D_MODEL), lambda i: (0, i, 0)),
                  pl.BlockSpec((D_MODEL, n_sec * D_MODEL), lambda i: (0, 0))]
        + [pl.BlockSpec((tm, D_MODEL), lambda i: (0, 0))] * skip
        + [pl.BlockSpec((tm, D_MODEL), lat),
           pl.BlockSpec((1, D_MODEL), lambda i: (0, 0)),
           pl.BlockSpec((None, 1, D_MODEL), mod_idx),
           pl.BlockSpec((tm, D_MODEL), lat)],
        out_specs=[pl.BlockSpec((tm, D_MODEL), lat), pl.BlockSpec((1, D_MODEL), lambda i: (0, 0)),
                   pl.BlockSpec((None, 8, D_MODEL), mod_idx)],
        compiler_params=_params(48),
    )(*_pin(d_g, w, *([ctx_tile] * skip), xin, nw, scale.reshape(n_mod, 1, D_MODEL), d_up))


def _weight_grad(h, d_g, name):
    n_sec, rows, _ = d_g.shape
    tm = 768 if rows % 768 == 0 else 512
    n_tiles = rows // tm

    def body(h_ref, dg_ref, dw_ref, acc):
        i = pl.program_id(1)
        prod = _mm_tn(h_ref[...], dg_ref[...])

        @pl.when(i == 0)
        def _():
            acc[...] = prod

        @pl.when((i > 0) & (i < n_tiles - 1))
        def _():
            acc[...] += prod

        @pl.when(i == n_tiles - 1)
        def _():
            dw_ref[...] = (acc[...] + prod).astype(BF16)

    return pl.pallas_call(
        body, name=name, grid=(n_sec, n_tiles),
        out_shape=pltpu.HBM((D_MODEL, n_sec * D_MODEL), BF16),
        in_specs=[pl.BlockSpec((tm, D_MODEL), lambda j, i: (i, 0)), pl.BlockSpec((None, tm, D_MODEL), lambda j, i: (j, i, 0))],
        out_specs=pl.BlockSpec((D_MODEL, D_MODEL), lambda j, i: (0, j)),
        scratch_shapes=[pltpu.VMEM((D_MODEL, D_MODEL), F32)],
        compiler_params=_params(32),
    )(*_pin(h, d_g))


def _weight_grad_paired(h, d_g, name):
    n_sec, rows, _ = d_g.shape
    tm = 768
    n_tiles = rows // tm
    half = D_MODEL // 2
    chip_cols = n_sec * D_MODEL // N_CHIPS

    def body(h_ref, dg_ref, q_ref, slots_ref, acc, keep, send, land, send_sems, recv_sems, own_sem):
        j, i = pl.program_id(0), pl.program_id(1)
        x, y, c = _my_place()
        prod = _mm_tn(h_ref[...], dg_ref[...])

        def to_sibling(k):
            return pltpu.make_async_remote_copy(src_ref=send.at[k], dst_ref=land.at[k], send_sem=send_sems.at[k],
                                                recv_sem=recv_sems.at[k], device_id=(x, y, 1 - c), device_id_type=MESH)

        @pl.when(i == 0)
        def _():
            acc[...] = prod

        @pl.when((i > 0) & (i < n_tiles - 1))
        def _():
            acc[...] += prod

        @pl.when(i == n_tiles - 1)
        def _():
            acc[...] += prod
            keep[j] = acc[pl.ds(_al(c * half, half), half), :]
            send[j] = acc[pl.ds(_al((1 - c) * half, half), half), :].astype(BF16)
            to_sibling(j).start()

        @pl.when((j == n_sec - 1) & (i == n_tiles - 1))
        def _():
            for k in range(n_sec):
                to_sibling(k).wait()
                q_ref[:, k * D_MODEL:(k + 1) * D_MODEL] = (keep[k] + land[k].astype(F32)).astype(BF16)
            chip = 2 * x + y
            own = pltpu.make_async_copy(q_ref.at[:, pl.ds(_al(chip * chip_cols, chip_cols), chip_cols)], slots_ref.at[chip], own_sem)
            own.start()
            own.wait()

    return pl.pallas_call(
        body, name=name, grid=(n_sec, n_tiles),
        out_shape=[jax.ShapeDtypeStruct((half, n_sec * D_MODEL), BF16),
                   jax.ShapeDtypeStruct((N_CHIPS, half, chip_cols), BF16)],
        in_specs=[pl.BlockSpec((tm, D_MODEL), lambda j, i: (i, 0)), pl.BlockSpec((None, tm, D_MODEL), lambda j, i: (j, i, 0))],
        out_specs=[pl.BlockSpec((half, n_sec * D_MODEL), lambda j, i: (0, 0)), ANY],
        scratch_shapes=[pltpu.VMEM((D_MODEL, D_MODEL), F32), pltpu.VMEM((n_sec, half, D_MODEL), F32),
                        pltpu.VMEM((n_sec, half, D_MODEL), BF16), pltpu.VMEM((n_sec, half, D_MODEL), BF16),
                        pltpu.SemaphoreType.DMA((n_sec,)), pltpu.SemaphoreType.DMA((n_sec,)), pltpu.SemaphoreType.DMA],
        compiler_params=_params(56),
    )(*_pin(h, d_g))


def _adamw_math(w, g, m, v):
    m = ADAM_B1 * m + (1.0 - ADAM_B1) * g
    v = ADAM_B2 * v + (1.0 - ADAM_B2) * (g * g)
    m_hat = m / (1.0 - ADAM_B1 ** ADAM_STEP)
    v_hat = v / (1.0 - ADAM_B2 ** ADAM_STEP)
    return -ADAM_LR * (m_hat / (jnp.sqrt(v_hat) + ADAM_EPS) + ADAM_WD * w), m, v


def _adamw(w, g, m, v, name):
    rows, cols = w.shape
    tr = rows if rows <= 128 else 128

    def body(w_ref, g_ref, m_ref, v_ref, d_ref, mo_ref, vo_ref):
        d_ref[...], mo_ref[...], vo_ref[...] = _adamw_math(w_ref[...], g_ref[...], m_ref[...], v_ref[...])

    tile = pl.BlockSpec((tr, cols), lambda i: (i, 0))
    return pl.pallas_call(
        body, name=name, grid=(rows // tr,),
        out_shape=[pltpu.HBM((rows, cols), F32)] * 3,
        in_specs=[tile] * 4, out_specs=[tile] * 3,
    )(*_pin(w, g, m, v))


SC_TILES = 32
SC_LANES = 16


def _adamw_sparsecore(quads, name):
    n = len(quads)
    shapes = [q[0].shape for q in quads]
    slabs = sorted({(rows // SC_TILES, cols) for rows, cols in shapes})

    def body(*refs):
        ins, outs, bufs = refs[:4 * n], refs[4 * n:7 * n], refs[7 * n:]
        tile = lax.axis_index("sc_subcore") * 2 + lax.axis_index("sc_core")
        for a, (rows, cols) in enumerate(shapes):
            per = rows // SC_TILES
            k = 4 * slabs.index((per, cols))
            wb, gb, mb, vb = bufs[k:k + 4]
            mine = pl.ds(tile * per, per)
            for src, buf in zip(ins[4 * a:4 * a + 4], (wb, gb, mb, vb)):
                pltpu.sync_copy(src.at[mine, :], buf)

            @pl.loop(0, per)
            def _(r):
                @pl.loop(0, cols, step=SC_LANES)
                def _(i):
                    at = (r, pl.ds(i, SC_LANES))
                    d, m_new, v_new = _adamw_math(wb[at], gb[at], mb[at], vb[at])
                    wb[at] = d
                    mb[at] = m_new
                    vb[at] = v_new

            for buf, dst in zip((wb, mb, vb), outs[3 * a:3 * a + 3]):
                pltpu.sync_copy(buf, dst.at[mine, :])

    outs = pl.kernel(
        body, name=name, out_type=[jax.ShapeDtypeStruct(s, F32) for s in shapes for _ in range(3)],
        mesh=plsc.VectorSubcoreMesh(core_axis_name="sc_core", subcore_axis_name="sc_subcore"),
        scratch_types=[pltpu.VMEM(s, F32) for s in slabs for _ in range(4)],
    )(*[a for q in quads for a in q])
    return [tuple(outs[3 * a:3 * a + 3]) for a in range(n)]


def _small_sums_and_cond_ctx(small, ada_w0, c_ctx, m, v):
    n_cols = ada_w0.shape[1]

    def body(small_ref, w_ref, c_ref, m_ref, v_ref, all_ref, sums_ref, g_ref, d_ref, mo_ref, vo_ref, part_ref, parts_ref,
             s1, r1, l1, s2, r2, l2):
        start_slabs, finish_slabs = _small_gather(small_ref, all_ref, s1, r1, l1)
        start_parts, finish_parts = _small_gather(part_ref, parts_ref, s2, r2, l2)
        start_slabs()
        finish_slabs()
        acc = all_ref[0:SMALL_ROWS, :]
        for dev in range(1, N_DEV):
            acc = acc + all_ref[dev * SMALL_ROWS:(dev + 1) * SMALL_ROWS, :]
        sums_ref[...] = acc
        x, y, _ = _my_place()
        d_modc = jnp.concatenate([acc[6:7, :], acc[7:8, :], acc[8:9, :]], axis=1)
        mine = jnp.zeros((1, n_cols), F32)
        for s in range(N_CHIPS):
            mine = mine + jnp.where(2 * x + y == s, d_modc[:, s * n_cols:(s + 1) * n_cols], 0.0)
        ah, al = _split2(jnp.broadcast_to(mine, (8, n_cols)))
        wh, wl = _split2(w_ref[...])
        nt = lambda a, b: lax.dot_general(a, b, (((1,), (1,)), ((), ())), preferred_element_type=F32)
        part_ref[...] = nt(ah, wh) + nt(al, wh) + nt(ah, wl)
        start_parts()
        finish_parts()
        total = parts_ref[0:1, :]
        for s in range(1, N_CHIPS):
            total = total + parts_ref[16 * s:16 * s + 1, :]
        w = c_ref[...]
        g = total * _dsilu(w)
        g_ref[...] = g
        d_ref[...], mo_ref[...], vo_ref[...] = _adamw_math(w, g, m_ref[...], v_ref[...])

    seven = [pltpu.SemaphoreType.DMA((7,)), pltpu.SemaphoreType.DMA((7,)), pltpu.SemaphoreType.DMA]
    vec = jax.ShapeDtypeStruct((1, D_MODEL), F32)
    return pl.pallas_call(
        body, name="small_sums_and_cond_ctx",
        out_shape=[jax.ShapeDtypeStruct((N_DEV * SMALL_ROWS, D_MODEL), F32), jax.ShapeDtypeStruct((SMALL_ROWS, D_MODEL), F32),
                   vec, vec, vec, vec],
        in_specs=[VMEM] * 5, out_specs=[VMEM] * 6,
        scratch_shapes=[pltpu.VMEM((8, D_MODEL), F32), pltpu.VMEM((N_DEV * 8, D_MODEL), F32)] + seven + seven,
    )(small, ada_w0, c_ctx, m, v)


def _ada_update(cond_t, d_mod, ada_w, m, v):
    n_layers, _, n_cols = ada_w.shape
    tr = ROW_TILE

    def body(c_ref, dm_ref, w_ref, m_ref, v_ref, g_ref, d_ref, mo_ref, vo_ref):
        g = _mm_f32(_silu(c_ref[...]), dm_ref[...])
        g_ref[...] = g
        d_ref[...], mo_ref[...], vo_ref[...] = _adamw_math(w_ref[...], g, m_ref[...], v_ref[...])

    tile = pl.BlockSpec((None, tr, n_cols), lambda l, i: (l, i, 0))
    return pl.pallas_call(
        body, name="ada_update", grid=(n_layers, D_MODEL // tr),
        out_shape=[pltpu.HBM(ada_w.shape, F32)] * 4,
        in_specs=[pl.BlockSpec((tr, 16), lambda l, i: (i, 0)), pl.BlockSpec((None, 16, n_cols), lambda l, i: (l, 0, 0)),
                  tile, tile, tile],
        out_specs=[tile] * 4,
    )(*_pin(cond_t, d_mod, ada_w, m, v))


def _local_step(x2, ctx2, target, mod_mine, mod_ctx, lb_logits, scale_full, w_in_full, late_shards, norm_w, gnorm, final_w,
                update_late):
    row = lambda a: a.reshape(1, -1)
    shift0, scale0, gate0 = (row(a) for a in jnp.split(mod_mine[0], 3))
    shift1, scale1, gate1 = (row(a) for a in jnp.split(mod_mine[1], 3))
    shift_c, scale_c, _ = (row(a) for a in jnp.split(mod_ctx, 3))
    nw0, nw1 = norm_w[0:1], norm_w[1:2]
    scales0 = jnp.concatenate([scale_c, scale0])

    g5, h0 = _ln_mod_matmul(ctx2, x2, nw0, jnp.concatenate([shift_c, shift0]), scales0, w_in_full, "hgrn_in_forward")
    o_raw, full = _gla_forward(g5, lb_logits, late_shards)
    x1, res0 = _hgrn_out_forward(o_raw, g5, x2, gnorm, gate0, full["hgrn_w_out"])
    uz, h1 = _ln_mod_matmul(None, x1, nw1, shift1, scale1, full["pool_w_in"], "pool_in_forward")
    d_x2, loss_part, d_final, d_gate1 = _pool_forward_loss(uz, x1, target, gate1, full["pool_w_grp"], scale_full,
                                                           full["pool_w_out"], final_w)

    d_uz, dw_pool_out, dw_pool_grp, d_pscale = _pool_backward(d_x2, uz, gate1, full["pool_w_grp"], scale_full, full["pool_w_out"])
    d_x1, d_nw1, d_mod1 = _ln_mod_backward(d_uz, full["pool_w_in"], None, x1, nw1, scale1, d_x2, "pool_in_backward")
    dw_pool_in = _weight_grad(h1, d_uz, "pool_in_weight_grad")
    d_o, d_z, dw_hgrn_out, d_gate0, d_gnorm = _hgrn_out_backward(d_x1, o_raw, g5, res0, gnorm, gate0, full["hgrn_w_out"])
    late_grads = {"hgrn_w_out": dw_hgrn_out, "pool_w_in": dw_pool_in, "pool_w_grp": dw_pool_grp, "pool_w_out": dw_pool_out}
    d_g5, d_lb, late_slots = _gla_backward(g5, lb_logits, d_o, d_z, [late_grads[k].astype(BF16) for k in GATHER_LATE])
    update_late(late_slots)
    dw_hgrn_in, slots0 = _weight_grad_paired(h0, d_g5, "hgrn_in_weight_grad")
    key = GATHER_EARLY[0]
    send_sem, recv_sem, part_thru, slots_thru, token = _scatter_start(dw_hgrn_in, slots0, d_lb, key)
    d_x, d_nw0, d_mod0 = _ln_mod_backward(d_g5, w_in_full, ctx2, x2, nw0 + token[0:1, 0:1], scales0, d_x1, "hgrn_in_backward")
    pending = (send_sem, recv_sem, part_thru, slots_thru)

    zero = jnp.zeros((1, D_MODEL), F32)
    small = jnp.concatenate([d_mod0[1, 0:2], d_gate0, d_mod1[0, 0:2], d_gate1, d_mod0[0, 0:2], zero, d_nw0, d_nw1, d_gnorm,
                             d_final, d_pscale, d_lb, jnp.broadcast_to(loss_part[0:1, 0:1], (1, D_MODEL)),
                             jnp.zeros((SMALL_ROWS - 17, D_MODEL), F32)], axis=0)
    return {"d_x": d_x, "pending": pending, "small": small}


def kernel(x, c, ctx, c_ctx, ada_w, ada_b, norm_w, hgrn_w_in, hgrn_lb_logits, hgrn_gnorm_w, hgrn_w_out, pool_w_in, pool_w_grp, pool_scale, pool_w_out, final_norm_w, loss_target, m_c_ctx, m_ada_w, m_ada_b, m_norm_w, m_hgrn_w_in, m_hgrn_lb_logits, m_hgrn_gnorm_w, m_hgrn_w_out, m_pool_w_in, m_pool_w_grp, m_pool_scale, m_pool_w_out, m_final_norm_w, v_c_ctx, v_ada_w, v_ada_b, v_norm_w, v_hgrn_w_in, v_hgrn_lb_logits, v_hgrn_gnorm_w, v_hgrn_w_out, v_pool_w_in, v_pool_w_grp, v_pool_scale, v_pool_w_out, v_final_norm_w):
    xi, yi, ci = _my_place()
    chip = 2 * xi + yi
    dev = 4 * xi + 2 * yi + ci
    ada_cols = ada_w.shape[2]
    lb_cols = hgrn_lb_logits.shape[2]
    ps_cols = pool_scale.shape[1]
    row = lambda a: a.reshape(1, -1)

    def chip_cols(a, n):
        return lax.dynamic_slice_in_dim(a, chip * n, n, axis=a.ndim - 1)

    def from_chips(g, rows_per_dev, take):
        return jnp.concatenate([g[2 * s * rows_per_dev:2 * s * rows_per_dev + take] for s in range(N_CHIPS)], axis=1)

    first = jnp.concatenate([jnp.broadcast_to(c, (8, D_MODEL)), jnp.pad(hgrn_lb_logits[0], ((0, 6), (0, 0))),
                             jnp.pad(pool_scale, ((0, 7), (0, 0)))], axis=1)
    shards = {"hgrn_w_in": hgrn_w_in[0], "hgrn_w_out": hgrn_w_out[0], "pool_w_in": pool_w_in[0],
              "pool_w_grp": pool_w_grp[0], "pool_w_out": pool_w_out[0]}
    first_all, parts_all, w_in_full = _prologue(first, row(c_ctx), ada_w, chip_cols(ada_b, ada_cols),
                                                shards[GATHER_EARLY[0]].astype(BF16))
    cond_all = first_all[::8, :D_MODEL]
    lb_logits = from_chips(first_all[:, D_MODEL:D_MODEL + lb_cols], 8, 2)
    scale_full = from_chips(first_all[:, D_MODEL + lb_cols:], 8, 1)
    cond_rows = jnp.concatenate([cond_all, row(c_ctx), jnp.zeros((7, D_MODEL), F32)], axis=0)
    mod_all = from_chips(parts_all, 32, 32).reshape(2, 16, 3 * D_MODEL)
    mod_mine = lax.dynamic_index_in_dim(mod_all, dev, axis=1, keepdims=False)

    out = {}

    def update(name, w, g, m, v):
        shape = w.shape
        w2, g2, m2, v2 = (a.reshape(-1, shape[-1]) for a in (w, g, m, v))
        d, mn, vn = _adamw(w2, g2, m2, v2, "adamw_" + name)
        out[name] = tuple(a.reshape(shape) for a in (g2, d, mn, vn))

    moments = {"hgrn_w_in": (m_hgrn_w_in, v_hgrn_w_in), "hgrn_w_out": (m_hgrn_w_out, v_hgrn_w_out),
               "pool_w_in": (m_pool_w_in, v_pool_w_in), "pool_w_grp": (m_pool_w_grp, v_pool_w_grp),
               "pool_w_out": (m_pool_w_out, v_pool_w_out)}
    weights = {"hgrn_w_in": hgrn_w_in, "hgrn_w_out": hgrn_w_out, "pool_w_in": pool_w_in, "pool_w_grp": pool_w_grp,
               "pool_w_out": pool_w_out}

    def reduce_scattered(slots, names, name):
        return dict(zip(names, _sum_and_exchange([slots[k] for k in names], names, name)))

    def update_late(late_slots):
        big = reduce_scattered(late_slots, GATHER_LATE, "exchange_halves_late")
        flat = lambda a, k: a.reshape(-1, weights[k].shape[-1])
        quads = [tuple(flat(a, k) for a in (weights[k], big[k], *moments[k])) for k in GATHER_LATE]
        for k, quad, (d, mn, vn) in zip(GATHER_LATE, quads, _adamw_sparsecore(quads, "adamw_late")):
            out[k] = tuple(a.reshape(weights[k].shape) for a in (quad[1], d, mn, vn))

    loc = _local_step(x[0], ctx[0], loss_target[0], mod_mine, mod_all[0, 8], lb_logits, scale_full, w_in_full,
                      [shards[k].astype(BF16) for k in GATHER_LATE], norm_w, hgrn_gnorm_w, row(final_norm_w), update_late)

    small_all, sums, g_c, d_c, m_c, v_c = _small_sums_and_cond_ctx(loc["small"], ada_w[0], row(c_ctx), row(m_c_ctx), row(v_c_ctx))
    loss = sums[16, 0]

    g_ada_b = jnp.stack([(sums[0:3] + sums[6:9]).reshape(-1), sums[3:6].reshape(-1)])
    update("ada_b", ada_b, g_ada_b, m_ada_b, v_ada_b)
    update("norm_w", norm_w, sums[9:11], m_norm_w, v_norm_w)
    update("hgrn_gnorm_w", hgrn_gnorm_w, sums[11:12], m_hgrn_gnorm_w, v_hgrn_gnorm_w)
    update("final_norm_w", row(final_norm_w), sums[12:13], row(m_final_norm_w), row(v_final_norm_w))
    update("pool_scale", pool_scale, chip_cols(sums[13:14], ps_cols), m_pool_scale, v_pool_scale)
    update("hgrn_lb_logits", hgrn_lb_logits, chip_cols(sums[14:16], lb_cols)[None], m_hgrn_lb_logits, v_hgrn_lb_logits)

    per_dev = small_all.reshape(N_DEV, SMALL_ROWS, D_MODEL)
    pad7 = jnp.zeros((7, 3 * D_MODEL), F32)
    dm0 = jnp.concatenate([per_dev[:, 0:3].reshape(N_DEV, -1), sums[6:9].reshape(1, -1), pad7], axis=0)
    dm1 = jnp.concatenate([per_dev[:, 3:6].reshape(N_DEV, -1), jnp.zeros((8, 3 * D_MODEL), F32)], axis=0)
    d_mod = chip_cols(jnp.stack([dm0, dm1]), ada_cols)
    out["ada_w"] = _ada_update(cond_rows.T, d_mod, ada_w, m_ada_w, v_ada_w)

    out["c_ctx"] = tuple(a.reshape(-1) for a in (g_c, d_c, m_c, v_c))
    out["final_norm_w"] = tuple(a.reshape(-1) for a in out["final_norm_w"])

    done = [g_c, out["ada_w"][1]]
    key = GATHER_EARLY[0]
    _, early = _scatter_wait(*loc["pending"], done, key)
    big_grads = reduce_scattered({key: early}, GATHER_EARLY, "exchange_halves_early")
    for k in GATHER_EARLY:
        update(k, weights[k], big_grads[k], *moments[k])

    names = ["c_ctx", "ada_w", "ada_b", "norm_w", "hgrn_w_in", "hgrn_lb_logits", "hgrn_gnorm_w", "hgrn_w_out", "pool_w_in",
             "pool_w_grp", "pool_scale", "pool_w_out", "final_norm_w"]
    return (loss, loc["d_x"][None], *[out[k][0] for k in names], *[out[k][1] for k in names], *[out[k][2] for k in names],
            *[out[k][3] for k in names])
```

```python
import functools

import numpy as np
import jax
import jax.numpy as jnp
from jax import lax
from jax.experimental import pallas as pl
from jax.experimental.pallas import tpu as pltpu

F32 = jnp.float32
BF16 = jnp.bfloat16

D_MODEL = 1024
SEQ = 2048
CTX_LEN = 256
ROWS_ALL = CTX_LEN + SEQ
HEADS = 8
HEAD_DIM = 128
CHUNK = 64
N_CTX_CHUNKS = CTX_LEN // CHUNK
N_LAT_CHUNKS = SEQ // CHUNK
N_CHUNKS = N_CTX_CHUNKS + N_LAT_CHUNKS
GRID_W = 64
POOL_WINDOWS = (2, 4, 8, 16)
POOL_GROUPS = 4
POOL_GROUP_DIM = 256
HGRN_SECTIONS = 5
POOL_SECTIONS = 2
EPS = 1e-6
N_DEV = 8
N_CHIPS = 4
ROW_TILE = 256
POOL_TILE = 512
SMALL_ROWS = 24

ADAM_LR = 0.001
ADAM_B1 = 0.9
ADAM_B2 = 0.999
ADAM_EPS = 1e-08
ADAM_WD = 0.01
ADAM_STEP = 10

MESH = pl.DeviceIdType.MESH
MIB = 1 << 20
ANY = pl.BlockSpec(memory_space=pl.ANY)
VMEM = pl.BlockSpec(memory_space=pltpu.VMEM)


def _params(vmem_mib=None):
    if vmem_mib is None:
        return pltpu.CompilerParams()
    return pltpu.CompilerParams(vmem_limit_bytes=vmem_mib * MIB)


def _pin(*operands):
    return [pltpu.with_memory_space_constraint(a, pltpu.HBM) if a.size * a.dtype.itemsize >= MIB else a for a in operands]


def _sig(a):
    return 0.5 * jnp.tanh(0.5 * a) + 0.5


def _silu(a):
    return a * _sig(a)


def _dsilu(a):
    s = _sig(a)
    return s * (1.0 + a * (1.0 - s))


def _mm(a, b):
    return jnp.dot(a.astype(BF16), b.astype(BF16), preferred_element_type=F32)


def _mm_nt(a, b):
    return lax.dot_general(a.astype(BF16), b.astype(BF16), (((1,), (1,)), ((), ())), preferred_element_type=F32)


def _mm_tn(a, b):
    return lax.dot_general(a.astype(BF16), b.astype(BF16), (((0,), (0,)), ((), ())), preferred_element_type=F32)


def _split2(a):
    hi = a.astype(BF16)
    lo = (a - hi.astype(F32)).astype(BF16)
    return hi, lo


def _mm_exact_lhs(m_bf, a):
    hi, lo = _split2(a)
    return jnp.dot(m_bf, hi, preferred_element_type=F32) + jnp.dot(m_bf, lo, preferred_element_type=F32)


def _mm_f32(a, b):
    ah, al = _split2(a)
    bh, bl = _split2(b)
    return (jnp.dot(ah, bh, preferred_element_type=F32) + jnp.dot(al, bh, preferred_element_type=F32)
            + jnp.dot(ah, bl, preferred_element_type=F32))


def _my_place():
    return lax.axis_index("x"), lax.axis_index("y"), lax.axis_index("c")


def _small_gather(x_ref, out_ref, send_sems, recv_sems, local_sem):
    m_per = x_ref.shape[0]
    x, y, c = _my_place()
    me, sibling = (x, y, c), (x, y, 1 - c)
    chips = [(1 - x, y), (x, 1 - y), (1 - x, 1 - y)]

    def rows(px, py, pc):
        return out_ref.at[pl.ds((4 * px + 2 * py + pc) * m_per, m_per), :]

    def copy(k, block, to, src=None):
        return pltpu.make_async_remote_copy(
            src_ref=rows(*block) if src is None else src, dst_ref=rows(*block),
            send_sem=send_sems.at[k], recv_sem=recv_sems.at[k], device_id=to, device_id_type=MESH)

    def mine():
        return pltpu.make_async_copy(x_ref, rows(*me), local_sem)

    def first():
        return [copy(0, me, sibling, src=x_ref)] + [copy(1 + j, me, (*chip, c), src=x_ref) for j, chip in enumerate(chips)]

    def start():
        mine().start()
        for cp in first():
            cp.start()

    def finish():
        passed = [copy(4 + j, (*chip, c), sibling) for j, chip in enumerate(chips)]
        for j, chip in enumerate(chips):
            copy(1 + j, (*chip, c), me).wait_recv()
            passed[j].start()
        copy(0, sibling, me).wait_recv()
        for j, chip in enumerate(chips):
            copy(4 + j, (*chip, 1 - c), me).wait_recv()
        for cp in first() + passed:
            cp.wait_send()
        mine().wait()

    return start, finish


W_SPECS = {
    "hgrn_w_in": ((D_MODEL, 5 * D_MODEL), (1, 1280, 0, 512)),
    "hgrn_w_out": ((D_MODEL, D_MODEL), (0, 256, 0, 128)),
    "pool_w_in": ((D_MODEL, 2 * D_MODEL), (1, 512, 0, 512)),
    "pool_w_grp": ((POOL_GROUPS, POOL_GROUP_DIM, POOL_GROUP_DIM), (1, 64, 1, 32)),
    "pool_w_out": ((D_MODEL, D_MODEL), (0, 256, 0, 128)),
}

def _al(v, m):
    return pl.multiple_of(v, m)


def _region(ref, spec, chip, half):
    ca, cn, ha, hn = spec
    idx = [slice(None)] * len(ref.shape)
    if ca == ha:
        if half is None:
            idx[ca] = pl.ds(_al(chip * cn, cn), cn)
        else:
            idx[ca] = pl.ds(_al(chip * cn + half * hn, hn), hn)
    else:
        idx[ca] = pl.ds(_al(chip * cn, cn), cn)
        if half is not None:
            idx[ha] = pl.ds(_al(half * hn, hn), hn)
    return ref.at[tuple(idx)]


def _half_of(ref, spec, half):
    _, _, ha, hn = spec
    idx = [slice(None)] * len(ref.shape)
    idx[ha] = pl.ds(_al(half * hn, hn), hn)
    return ref.at[tuple(idx)]


def _half_shape(name):
    full, (ca, cn, ha, hn) = W_SPECS[name]
    shp = list(full)
    shp[ca] = cn
    shp[ha] = hn
    return tuple(shp)


def _gather_two_level(names, sh, land, full, send_sems, recv_sems, local_sems):
    n = len(names)
    specs = [W_SPECS[k][1] for k in names]
    x, y, c = _my_place()
    chip_me = 2 * x + y
    sibling = (x, y, 1 - c)
    chips = [(1 - x, y), (x, 1 - y), (1 - x, 1 - y)]

    def remote(k, src, dst, to):
        return pltpu.make_async_remote_copy(src_ref=src, dst_ref=dst, send_sem=send_sems.at[k], recv_sem=recv_sems.at[k],
                                            device_id=to, device_id_type=MESH)

    def three_halves(a):
        ca, cn, ha, hn = specs[a]
        idx = [slice(None)] * len(land[a].shape)
        if ca == ha:
            idx[ca] = pl.ds(0, 3 * hn)
        else:
            idx[ca], idx[ha] = pl.ds(0, 3 * cn), pl.ds(0, hn)
        return land[a].at[tuple(idx)]

    def own(a):
        return pltpu.make_async_copy(sh[a], _region(land[a], specs[a], chip_me, None), local_sems.at[a])

    def out(a):
        return pltpu.make_async_copy(land[a], full[a], local_sems.at[a])

    def start():
        for a in range(n):
            own(a).start()
            for px, py in chips:
                remote(a, _half_of(sh[a], specs[a], c), _region(land[a], specs[a], chip_me, c), (px, py, c)).start()

    def forward():
        for a in range(n):
            remote(a, three_halves(a), three_halves(a), sibling).wait_recv()
            for px, py in chips:
                landed = _region(land[a], specs[a], 2 * px + py, c)
                remote(n + a, landed, landed, sibling).start()

    def finish():
        for a in range(n):
            remote(n + a, three_halves(a), three_halves(a), sibling).wait_recv()
            remote(a, three_halves(a), three_halves(a), sibling).wait_send()
            remote(n + a, three_halves(a), three_halves(a), sibling).wait_send()
            own(a).wait()
        for a in range(n):
            out(a).start()
        for a in range(n):
            out(a).wait()

    return start, forward, finish


def _scatter_direct(names, part, slots, send_sems, recv_sems, local_sems):
    specs = [W_SPECS[k][1] for k in names]
    x, y, c = _my_place()
    dev_me = 4 * x + 2 * y + c

    def local(a):
        return pltpu.make_async_copy(_region(part[a], specs[a], 2 * x + y, c), slots[a].at[dev_me], local_sems.at[a])

    def start():
        for a in range(len(names)):
            local(a).start()
            for flip in range(1, N_DEV):
                tx = 1 - x if flip >> 2 else x
                ty = 1 - y if (flip >> 1) & 1 else y
                tc = 1 - c if flip & 1 else c
                pltpu.make_async_remote_copy(src_ref=_region(part[a], specs[a], 2 * tx + ty, tc), dst_ref=slots[a].at[dev_me],
                                             send_sem=send_sems.at[a], recv_sem=recv_sems.at[a], device_id=(tx, ty, tc),
                                             device_id_type=MESH).start()

    def wait():
        for a in range(len(names)):
            seven = slots[a].at[pl.ds(0, N_DEV - 1)]
            pltpu.make_async_remote_copy(src_ref=seven, dst_ref=seven, send_sem=send_sems.at[a], recv_sem=recv_sems.at[a],
                                         device_id=(x, y, c), device_id_type=MESH).wait()
            local(a).wait()

    return start, wait


HBM_SPEC = pl.BlockSpec(memory_space=pltpu.HBM)
SEM_SPEC = pl.BlockSpec(memory_space=pltpu.SEMAPHORE)
SPLIT_EFFECT = pltpu.SideEffectType.DATAFLOW_SIDE_EFFECTING


def _scatter_start(part, slots, after, name_key):
    _, cols, _, _ = W_SPECS[name_key][1]

    def body(part_ref, slots_ref, after_ref, send_sem, recv_sem, part_thru, slots_thru, token):
        x, y, c = _my_place()
        for tx, ty in ((1 - x, y), (x, 1 - y), (1 - x, 1 - y)):
            src = part_ref.at[:, pl.ds(_al((2 * tx + ty) * cols, cols), cols)]
            pltpu.make_async_remote_copy(src_ref=src, dst_ref=slots_ref.at[2 * x + y], send_sem=send_sem, recv_sem=recv_sem,
                                         device_id=(tx, ty, c), device_id_type=MESH).start()
        token[...] = jnp.zeros_like(token)

    return pl.pallas_call(
        body, name="scatter_start_" + name_key,
        out_shape=(pltpu.SemaphoreType.DMA(()), pltpu.SemaphoreType.DMA(()), pltpu.HBM(part.shape, part.dtype),
                   pltpu.HBM(slots.shape, slots.dtype), jax.ShapeDtypeStruct((8, 128), F32)),
        in_specs=(HBM_SPEC, HBM_SPEC, ANY), out_specs=(SEM_SPEC, SEM_SPEC, HBM_SPEC, HBM_SPEC, VMEM),
        input_output_aliases={0: 2, 1: 3},
        compiler_params=pltpu.CompilerParams(has_side_effects=SPLIT_EFFECT),
    )(pltpu.with_memory_space_constraint(part, pltpu.HBM), pltpu.with_memory_space_constraint(slots, pltpu.HBM), after)


def _scatter_wait(send_sem, recv_sem, part_thru, slots_thru, after, name_key):
    def body(part_ref, slots_ref, send_sem, recv_sem, *rest):
        x, y, c = _my_place()
        landed = slots_ref.at[pl.ds(0, N_CHIPS - 1)]
        copy = pltpu.make_async_remote_copy(src_ref=landed, dst_ref=landed, send_sem=send_sem, recv_sem=recv_sem,
                                            device_id=(x, y, c), device_id_type=MESH)
        copy.wait_send()
        copy.wait_recv()

    return pl.pallas_call(
        body, name="scatter_wait_" + name_key,
        out_shape=(pltpu.HBM(part_thru.shape, part_thru.dtype), pltpu.HBM(slots_thru.shape, slots_thru.dtype)),
        in_specs=(HBM_SPEC, HBM_SPEC, SEM_SPEC, SEM_SPEC) + (ANY,) * len(after), out_specs=(HBM_SPEC, HBM_SPEC),
        input_output_aliases={0: 0, 1: 1},
        compiler_params=pltpu.CompilerParams(has_side_effects=SPLIT_EFFECT),
    )(part_thru, slots_thru, send_sem, recv_sem, *after)


def _comm_sems(n):
    return [pltpu.SemaphoreType.DMA((n,)), pltpu.SemaphoreType.DMA((n,)), pltpu.SemaphoreType.DMA((n,))]


GATHER_EARLY = ("hgrn_w_in",)
GATHER_LATE = ("hgrn_w_out", "pool_w_in", "pool_w_grp", "pool_w_out")


def _weight_gather(sh, full, spec, send_sems, recv_sems, local_sem):
    x, y, c = _my_place()
    chip_me = 2 * x + y
    sibling = (x, y, 1 - c)
    chips = [(1 - x, y), (x, 1 - y), (1 - x, 1 - y)]

    def remote(k, src, dst, to):
        return pltpu.make_async_remote_copy(src_ref=src, dst_ref=dst, send_sem=send_sems.at[k], recv_sem=recv_sems.at[k],
                                            device_id=to, device_id_type=MESH)

    def own():
        return pltpu.make_async_copy(sh, _region(full, spec, chip_me, None), local_sem)

    def direct():
        return [remote(j, _half_of(sh, spec, c), _region(full, spec, chip_me, c), (px, py, c)) for j, (px, py) in enumerate(chips)]

    def start():
        own().start()
        for cp in direct():
            cp.start()

    def finish():
        passed = []
        for j, (px, py) in enumerate(chips):
            landed = _region(full, spec, 2 * px + py, c)
            remote(j, landed, landed, (px, py, c)).wait_recv()
            passed.append(remote(3 + j, landed, landed, sibling))
            passed[-1].start()
        for j, (px, py) in enumerate(chips):
            other = _region(full, spec, 2 * px + py, 1 - c)
            remote(3 + j, other, other, sibling).wait_recv()
        for cp in direct() + passed:
            cp.wait_send()
        own().wait()

    return start, finish


def _prologue(first, c_ctx, ada_w, ada_b_cols, w_shard):
    n_layers, _, n_cols = ada_w.shape
    key = GATHER_EARLY[0]
    spec = W_SPECS[key][1]
    m_first = first.shape[0]

    def body(first_ref, cctx_ref, adaw_ref, adab_ref, sh_ref, first_all, parts_all, full_ref, parts_scr,
             s1, r1, l1, s2, r2, l2, ws, wr, wl):
        start_first, finish_first = _small_gather(first_ref, first_all, s1, r1, l1)
        start_parts, finish_parts = _small_gather(parts_scr, parts_all, s2, r2, l2)
        start_weight, finish_weight = _weight_gather(sh_ref, full_ref, spec, ws, wr, wl)
        start_first()
        start_weight()
        finish_first()
        cond = jnp.concatenate([first_all[m_first * d:m_first * d + 1, 0:D_MODEL] for d in range(N_DEV)]
                               + [cctx_ref[...], jnp.zeros((16 - N_DEV - 1, D_MODEL), F32)], axis=0)
        act = _silu(cond)
        for i in range(n_layers):
            parts_scr[16 * i:16 * (i + 1), :] = _mm_f32(act, adaw_ref[i]) + adab_ref[i]
        start_parts()
        finish_parts()
        finish_weight()

    seven = [pltpu.SemaphoreType.DMA((7,)), pltpu.SemaphoreType.DMA((7,)), pltpu.SemaphoreType.DMA]
    return pl.pallas_call(
        body, name="prologue",
        out_shape=[jax.ShapeDtypeStruct((N_DEV * m_first, first.shape[1]), F32),
                   jax.ShapeDtypeStruct((N_DEV * 16 * n_layers, n_cols), F32), jax.ShapeDtypeStruct(W_SPECS[key][0], BF16)],
        in_specs=[VMEM] * 5, out_specs=[VMEM] * 3,
        scratch_shapes=[pltpu.VMEM((16 * n_layers, n_cols), F32)] + seven + seven
        + [pltpu.SemaphoreType.DMA((6,)), pltpu.SemaphoreType.DMA((6,)), pltpu.SemaphoreType.DMA],
        compiler_params=_params(48),
    )(first, c_ctx, ada_w, ada_b_cols.reshape(n_layers, 1, n_cols), w_shard)


def _slot_shapes(names):
    return [jax.ShapeDtypeStruct((N_DEV,) + _half_shape(k), BF16) for k in names]


def _sum_and_exchange(slots, names, name):
    n = len(names)
    specs = [W_SPECS[k][1] for k in names]

    def shard_shape(k):
        shp = list(_half_shape(k))
        shp[W_SPECS[k][1][2]] *= 2
        return tuple(shp)

    def body(*refs):
        slot, out, half = refs[:n], refs[n:2 * n], refs[2 * n:3 * n]
        send_sems, recv_sems, local_sems = refs[3 * n:]
        x, y, c = _my_place()
        sibling = (x, y, 1 - c)

        def remote(a, src, dst):
            return pltpu.make_async_remote_copy(src_ref=src, dst_ref=dst, send_sem=send_sems.at[a], recv_sem=recv_sems.at[a],
                                                device_id=sibling, device_id_type=MESH)

        local = [pltpu.make_async_copy(half[a], _half_of(out[a], specs[a], c), local_sems.at[a]) for a in range(n)]
        for a in range(n):
            acc = slot[a][0].astype(F32)
            for d in range(1, slot[a].shape[0]):
                acc = acc + slot[a][d].astype(F32)
            half[a][...] = acc
            local[a].start()
            remote(a, half[a], _half_of(out[a], specs[a], c)).start()
        for a in range(n):
            theirs = _half_of(out[a], specs[a], 1 - c)
            remote(a, theirs, theirs).wait_recv()
            remote(a, half[a], half[a]).wait_send()
            local[a].wait()

    return pl.pallas_call(
        body, name=name,
        out_shape=[jax.ShapeDtypeStruct(shard_shape(k), F32) for k in names],
        in_specs=[VMEM] * n, out_specs=[VMEM] * n,
        scratch_shapes=[pltpu.VMEM(_half_shape(k), F32) for k in names]
        + [pltpu.SemaphoreType.DMA((n,)), pltpu.SemaphoreType.DMA((n,)), pltpu.SemaphoreType.DMA((n,))],
        compiler_params=_params(40),
    )(*slots)


def _ln_mod_matmul(ctx_tile, xin, nw, shift, scale, w, name):
    n_mod = shift.shape[0]
    skip = n_mod - 1
    tm = ROW_TILE if skip else 2 * ROW_TILE
    rows = xin.shape[0] + skip * tm
    n_sec = w.shape[1] // D_MODEL

    def body(*refs):
        c_ref = refs[0] if skip else None
        x_ref, nw_ref, sh_ref, sc_ref, w_ref, g_ref, h_ref = refs[skip:]
        xv = jnp.where(pl.program_id(0) == 0, c_ref[...], x_ref[...]) if skip else x_ref[...]
        r = lax.rsqrt(jnp.mean(xv * xv, axis=-1, keepdims=True) + EPS)
        h_f32 = (xv * r * nw_ref[...]) * (1.0 + sc_ref[...]) + sh_ref[...]
        h = h_f32.astype(BF16)
        h_ref[...] = h
        for k in range(n_sec):
            g_ref[k] = jnp.dot(h, w_ref[:, k * D_MODEL:(k + 1) * D_MODEL], preferred_element_type=F32)

    mod_spec = pl.BlockSpec((None, 1, D_MODEL), lambda i: (jnp.minimum(i, n_mod - 1), 0, 0))
    return pl.pallas_call(
        body, name=name, grid=(rows // tm,),
        out_shape=[pltpu.HBM((n_sec, rows, D_MODEL), F32), pltpu.HBM((rows, D_MODEL), BF16)],
        in_specs=[pl.BlockSpec((tm, D_MODEL), lambda i: (0, 0))] * skip
        + [pl.BlockSpec((tm, D_MODEL), lambda i: (jnp.maximum(i - skip, 0), 0)),
           pl.BlockSpec((1, D_MODEL), lambda i: (0, 0)),
           mod_spec, mod_spec,
           pl.BlockSpec((D_MODEL, n_sec * D_MODEL), lambda i: (0, 0))],
        out_specs=[pl.BlockSpec((n_sec, tm, D_MODEL), lambda i: (0, i, 0)),
                   pl.BlockSpec((tm, D_MODEL), lambda i: (i, 0))],
        compiler_params=_params(48),
    )(*_pin(*([ctx_tile] * skip), xin, nw, shift.reshape(n_mod, 1, D_MODEL), scale.reshape(n_mod, 1, D_MODEL), w))


def _gates(fpre, lb):
    sg = _sig(fpre)
    f = lb + (1.0 - lb) * sg
    return sg, f, 1.0 - f, jnp.log(f)


G_SPEC = lambda sec: pl.BlockSpec((None, ROWS_ALL, HEAD_DIM), lambda h, sec=sec: (sec, 0, h))


GROUP = 2 * CHUNK


def _group_masks():
    rid, cid = np.arange(GROUP)[:, None], np.arange(GROUP)[None, :]
    same = (rid >= CHUNK) == (cid >= CHUNK)
    sums, back = [], []
    for rev in (False, True):
        causal = (cid >= rid) if rev else (cid <= rid)
        anti = (cid <= rid) if rev else (cid >= rid)
        sums.append(same & causal)
        back.append(np.concatenate([same & anti, same & ~anti], axis=1))
    sums, back = np.stack(sums).astype(np.float32), np.stack(back).astype(np.float32)
    return jnp.asarray(sums, BF16), jnp.asarray(back, BF16), jnp.asarray(sums, F32)


MASK_SPECS = [pl.BlockSpec((2, GROUP, GROUP), lambda h: (0, 0, 0)), pl.BlockSpec((2, GROUP, 2 * GROUP), lambda h: (0, 0, 0)),
              pl.BlockSpec((2, GROUP, GROUP), lambda h: (0, 0, 0))]


def _group_sum(m_bf, a):
    hi, lo = _split2(a)
    r = jnp.dot(m_bf, jnp.concatenate([hi, lo], axis=1), preferred_element_type=F32)
    return r[:, :HEAD_DIM] + r[:, HEAD_DIM:]


def _chunk_row(a, pos):
    return jnp.concatenate([jnp.broadcast_to(a[c * CHUNK + pos:c * CHUNK + pos + 1, :], (CHUNK, HEAD_DIM)) for c in range(2)], axis=0)


def _by_chunk(a, second):
    return jnp.concatenate([jnp.where(second, 0.0, a), jnp.where(second, a, 0.0)], axis=1)


def _own_block(r):
    return jnp.concatenate([r[0:CHUNK, 0:HEAD_DIM], r[CHUNK:GROUP, HEAD_DIM:2 * HEAD_DIM]], axis=0)


def _scan_step_of(row_chunk, rev, latent):
    if not rev:
        return row_chunk
    return (N_CHUNKS + N_CTX_CHUNKS - 1 - row_chunk) if latent else (N_CTX_CHUNKS - 1 - row_chunk)


def _group_rows(i, j, per_step, latent):
    base = CTX_LEN if latent else 0
    return pl.multiple_of(base + (i * per_step + j) * GROUP, GROUP)


GROUPS_PER_STEP = 8
GROUPS_PER_GRAD_STEP = 4


def _gla_forward(g5, lb_logits, late_shards):
    q_scale = HEAD_DIM ** -0.5
    per_lat, per_ctx = GROUPS_PER_STEP, min(GROUPS_PER_STEP, CTX_LEN // GROUP)
    n_late = len(GATHER_LATE)

    def body(ff_ref, fb_ref, v_ref, q_ref, lg_ref, sums_ref, _, keep_ref, *rest):
        shard_refs, o_ref, full_refs = rest[:n_late], rest[n_late], rest[n_late + 1:2 * n_late + 1]
        st_ref, decay_ref, qt_ref = rest[2 * n_late + 1:2 * n_late + 4]
        land_refs = rest[2 * n_late + 4:3 * n_late + 4]
        start_gather, forward_gather, finish_gather = _gather_two_level(GATHER_LATE, shard_refs, land_refs, full_refs,
                                                                        *rest[3 * n_late + 4:])

        @pl.when(pl.program_id(0) == 0)
        def _():
            start_gather()

        @pl.when(pl.program_id(0) == HEADS - 1)
        def _():
            forward_gather()

        second = lax.broadcasted_iota(jnp.int32, (GROUP, HEAD_DIM), 0) >= CHUNK
        for rev in (False, True):
            f_ref = fb_ref if rev else ff_ref
            lb = _sig(lg_ref[1:2, :] if rev else lg_ref[0:1, :])
            d = 1 if rev else 0
            last = 0 if rev else CHUNK - 1
            mid = CHUNK // 2 if rev else CHUNK // 2 - 1

            def local_step(i, carry, latent, per):
                r0s = [_group_rows(i, j, per, latent) for j in range(per)]
                rows = [pl.ds(r0, GROUP) for r0 in r0s]
                gates = [_gates(f_ref[r, :], lb) for r in rows]
                vs = [v_ref[r, :] for r in rows]
                bs = [_group_sum(sums_ref[d], g[3]) for g in gates]
                bls = [_chunk_row(b, last) for b in bs]
                ups = [_mm_tn(v, _by_chunk(g[2] * jnp.exp(bl - b), second)) for v, g, b, bl in zip(vs, gates, bs, bls)]
                if latent:
                    qs = [_silu(q_ref[r, :]) * q_scale for r in rows]
                    bms = [_chunk_row(b, mid) for b in bs]
                    a_s = [_mm_nt(q * jnp.exp(b - bm), g[2] * jnp.exp(bm - b)) for q, g, b, bm in zip(qs, gates, bs, bms)]
                    outs = [_mm(a * keep_ref[d], v) for a, v in zip(a_s, vs)]
                for j in range(per):
                    for c in range(2):
                        step = _scan_step_of(r0s[j] // CHUNK + c, rev, latent)
                        decay_ref[step] = jnp.exp(bls[j][c * CHUNK:c * CHUNK + 1, :])
                        st_ref[step] = ups[j][:, c * HEAD_DIM:(c + 1) * HEAD_DIM]
                    if latent:
                        orow = pl.ds(pl.multiple_of(r0s[j] - CTX_LEN, GROUP), GROUP)
                        qt_ref[orow, :] = (qs[j] * jnp.exp(bs[j])).astype(BF16)
                        if rev:
                            o_ref[orow, :] += outs[j]
                        else:
                            o_ref[orow, :] = outs[j]
                return carry

            lax.fori_loop(0, CTX_LEN // (per_ctx * GROUP), functools.partial(local_step, latent=False, per=per_ctx), 0)
            lax.fori_loop(0, SEQ // (per_lat * GROUP), functools.partial(local_step, latent=True, per=per_lat), 0)

            def scan_step(t, st):
                update = st_ref[t]
                st_ref[t] = st
                return st * decay_ref[t] + update

            lax.fori_loop(0, N_CHUNKS, scan_step, jnp.zeros((HEAD_DIM, HEAD_DIM), F32), unroll=2)

            def inter_step(i, carry):
                r0s = [_group_rows(i, j, per_lat, True) for j in range(per_lat)]
                orows = [pl.ds(pl.multiple_of(r0 - CTX_LEN, GROUP), GROUP) for r0 in r0s]
                states = [jnp.concatenate([st_ref[_scan_step_of(r0 // CHUNK + c, rev, True)] for c in range(2)], axis=0)
                          for r0 in r0s]
                prods = [lax.dot_general(qt_ref[orow, :], s.astype(BF16), (((1,), (1,)), ((), ())), preferred_element_type=F32)
                         for orow, s in zip(orows, states)]
                for orow, r in zip(orows, prods):
                    o_ref[orow, :] += _own_block(r)
                return carry

            lax.fori_loop(0, SEQ // (per_lat * GROUP), inter_step, 0)

        @pl.when(pl.program_id(0) == HEADS - 1)
        def _():
            finish_gather()

    outs = pl.pallas_call(
        body, name="gla_forward", grid=(HEADS,),
        out_shape=[pltpu.HBM((SEQ, D_MODEL), F32)] + [jax.ShapeDtypeStruct(W_SPECS[k][0], BF16) for k in GATHER_LATE],
        in_specs=[G_SPEC(0), G_SPEC(1), G_SPEC(2), G_SPEC(3), pl.BlockSpec((2, HEAD_DIM), lambda h: (0, h))] + MASK_SPECS
        + [ANY] * n_late,
        out_specs=[pl.BlockSpec((SEQ, HEAD_DIM), lambda h: (0, h))] + [ANY] * n_late,
        scratch_shapes=[pltpu.VMEM((N_CHUNKS, HEAD_DIM, HEAD_DIM), F32), pltpu.VMEM((N_CHUNKS, 1, HEAD_DIM), F32),
                        pltpu.VMEM((SEQ, HEAD_DIM), BF16)] + [pltpu.VMEM(W_SPECS[k][0], BF16) for k in GATHER_LATE]
        + [pltpu.SemaphoreType.DMA((2 * n_late,)), pltpu.SemaphoreType.DMA((2 * n_late,)), pltpu.SemaphoreType.DMA((n_late,))],
        compiler_params=_params(48),
    )(*_pin(g5, g5, g5, g5, lb_logits), *_group_masks(), *late_shards)
    return outs[0], dict(zip(GATHER_LATE, outs[1:]))


def _gla_backward(g5, lb_logits, d_o, d_z, late_parts):
    q_scale = HEAD_DIM ** -0.5
    per_lat, per_ctx = GROUPS_PER_STEP, min(GROUPS_PER_STEP, CTX_LEN // GROUP)
    n_late = len(GATHER_LATE)

    def body(ff_ref, fb_ref, v_ref, q_ref, lg_ref, do_ref, dz_ref, sums_ref, back_ref, keep_ref, *rest):
        part_refs, (dg_ref, dlg_ref), slot_refs = rest[:n_late], rest[n_late:n_late + 2], rest[n_late + 2:2 * n_late + 2]
        st_ref, dst_ref, decay_ref, both_ref = rest[2 * n_late + 2:2 * n_late + 6]
        start_scatter, wait_scatter = _scatter_direct(GATHER_LATE, part_refs, slot_refs, *rest[2 * n_late + 6:])

        @pl.when(pl.program_id(0) == 0)
        def _():
            start_scatter()

        dg_ref[3, 0:CTX_LEN, :] = jnp.zeros((CTX_LEN, HEAD_DIM), BF16)
        dg_ref[4, 0:CTX_LEN, :] = jnp.zeros((CTX_LEN, HEAD_DIM), BF16)
        dg_ref[4, CTX_LEN:ROWS_ALL, :] = dz_ref[...]
        second = lax.broadcasted_iota(jnp.int32, (GROUP, HEAD_DIM), 0) >= CHUNK
        for rev in (False, True):
            d = 1 if rev else 0
            f_ref = fb_ref if rev else ff_ref
            lb = _sig(lg_ref[d:d + 1, :])
            last = 0 if rev else CHUNK - 1
            mid = CHUNK // 2 if rev else CHUNK // 2 - 1

            def local_step(i, carry, latent, per):
                r0s = [_group_rows(i, j, per, latent) for j in range(per)]
                rows = [pl.ds(r0, GROUP) for r0 in r0s]
                gates = [_gates(f_ref[r, :], lb) for r in rows]
                bs = [_group_sum(sums_ref[d], g[3]) for g in gates]
                bls = [_chunk_row(b, last) for b in bs]
                ups = [_mm_tn(v_ref[r, :], _by_chunk(g[2] * jnp.exp(bl - b), second)) for r, g, b, bl in zip(rows, gates, bs, bls)]
                if latent:
                    orows = [pl.ds(pl.multiple_of(r0 - CTX_LEN, GROUP), GROUP) for r0 in r0s]
                    d_ups = [_mm_tn(do_ref[orow, :], _by_chunk(_silu(q_ref[r, :]) * q_scale * jnp.exp(b), second))
                             for orow, r, b in zip(orows, rows, bs)]
                for j in range(per):
                    for c in range(2):
                        step = _scan_step_of(r0s[j] // CHUNK + c, rev, latent)
                        decay_ref[step] = jnp.exp(bls[j][c * CHUNK:c * CHUNK + 1, :])
                        st_ref[step] = ups[j][:, c * HEAD_DIM:(c + 1) * HEAD_DIM]
                        if latent:
                            dst_ref[step] = d_ups[j][:, c * HEAD_DIM:(c + 1) * HEAD_DIM]
                        else:
                            dst_ref[step] = jnp.zeros((HEAD_DIM, HEAD_DIM), F32)
                return carry

            lax.fori_loop(0, CTX_LEN // (per_ctx * GROUP), functools.partial(local_step, latent=False, per=per_ctx), 0)
            lax.fori_loop(0, SEQ // (per_lat * GROUP), functools.partial(local_step, latent=True, per=per_lat), 0)

            def scan_step(i, carry):
                st, d_st = carry
                j = N_CHUNKS - 1 - i
                update, d_update = st_ref[i], dst_ref[j]
                st_ref[i] = st
                dst_ref[j] = d_st
                return st * decay_ref[i] + update, d_st * decay_ref[j] + d_update

            zero_state = jnp.zeros((HEAD_DIM, HEAD_DIM), F32)
            lax.fori_loop(0, N_CHUNKS, scan_step, (zero_state, zero_state))

            def grad_step(i, dlb, latent, per):
                r0s = [_group_rows(i, j, per, latent) for j in range(per)]
                rows = [pl.ds(r0, GROUP) for r0 in r0s]
                gates = [_gates(f_ref[r, :], lb) for r in rows]
                vs = [v_ref[r, :] for r in rows]
                bs = [_group_sum(sums_ref[d], g[3]) for g in gates]
                bls = [_chunk_row(b, last) for b in bs]
                e_ends = [jnp.exp(bl - b) for b, bl in zip(bs, bls)]
                k_ends = [g[2] * e for g, e in zip(gates, e_ends)]
                sts = [[st_ref[_scan_step_of(r0 // CHUNK + c, rev, latent)] for c in range(2)] for r0 in r0s]
                d_sts = [[dst_ref[_scan_step_of(r0 // CHUNK + c, rev, latent)] for c in range(2)] for r0 in r0s]
                d_kends = [_own_block(_mm(v, jnp.concatenate(ds, axis=1))) for v, ds in zip(vs, d_sts)]
                d_vs = [_own_block(_mm_nt(ke, jnp.concatenate(ds, axis=0))) for ke, ds in zip(k_ends, d_sts)]
                at_last = [jnp.concatenate([jnp.broadcast_to(jnp.sum(ds[c] * s[c], axis=0, keepdims=True), (CHUNK, HEAD_DIM))
                                            for c in range(2)], axis=0) * jnp.exp(bl) for ds, s, bl in zip(d_sts, sts, bls)]
                t_kends = [dk * ke for dk, ke in zip(d_kends, k_ends)]
                d_ks = [dk * e for dk, e in zip(d_kends, e_ends)]
                if latent:
                    orows = [pl.ds(pl.multiple_of(r0 - CTX_LEN, GROUP), GROUP) for r0 in r0s]
                    qpres = [q_ref[r, :] for r in rows]
                    q_sigs = [_sig(qp) for qp in qpres]
                    qs = [qp * sg * q_scale for qp, sg in zip(qpres, q_sigs)]
                    bms = [_chunk_row(b, mid) for b in bs]
                    e_bs = [jnp.exp(b) for b in bs]
                    e_qms = [jnp.exp(b - bm) for b, bm in zip(bs, bms)]
                    e_kms = [jnp.exp(bm - b) for b, bm in zip(bs, bms)]
                    q_ts = [q * e for q, e in zip(qs, e_bs)]
                    q_ms = [q * e for q, e in zip(qs, e_qms)]
                    k_ms = [g[2] * e for g, e in zip(gates, e_kms)]
                    d_outs = [do_ref[orow, :] for orow in orows]
                    a_s = [_mm_nt(qm, km) * keep_ref[d] for qm, km in zip(q_ms, k_ms)]
                    d_as = [_mm_nt(do, v) * keep_ref[d] for do, v in zip(d_outs, vs)]
                    d_qts = [_own_block(_mm(do, jnp.concatenate(s, axis=1))) for do, s in zip(d_outs, sts)]
                    d_qms = [_mm(da, km) for da, km in zip(d_as, k_ms)]
                    d_kms = [_mm_tn(da, qm) for da, qm in zip(d_as, q_ms)]
                    d_vs = [dv + _mm_tn(a, do) for dv, a, do in zip(d_vs, a_s, d_outs)]
                    d_ks = [dk + dkm * e for dk, dkm, e in zip(d_ks, d_kms, e_kms)]
                    d_bs = [jnp.concatenate([dqt * qt + dqm * qm - dkm * km, t], axis=0)
                            for dqt, qt, dqm, qm, dkm, km, t in zip(d_qts, q_ts, d_qms, q_ms, d_kms, k_ms, t_kends)]
                    d_qs = [dqt * eb + dqm * eq for dqt, eb, dqm, eq in zip(d_qts, e_bs, d_qms, e_qms)]
                    back = back_ref[d]
                else:
                    d_bs, back = t_kends, back_ref[d, :, GROUP:2 * GROUP]
                d_lfs = [_group_sum(back, db) + al for db, al in zip(d_bs, at_last)]
                for j in range(per):
                    sg, f = gates[j][0], gates[j][1]
                    d_f = d_lfs[j] / f - d_ks[j]
                    dg_ref[d, rows[j], :] = (d_f * (1.0 - lb) * sg * (1.0 - sg)).astype(BF16)
                    dlb = dlb + jnp.sum(d_f * (1.0 - sg), axis=0, keepdims=True)
                    if rev:
                        dg_ref[2, rows[j], :] = (both_ref[0, rows[j], :] + d_vs[j]).astype(BF16)
                    else:
                        both_ref[0, rows[j], :] = d_vs[j]
                    if latent:
                        d_qpre = d_qs[j] * q_scale * (q_sigs[j] * (1.0 + qpres[j] * (1.0 - q_sigs[j])))
                        if rev:
                            dg_ref[3, rows[j], :] = (both_ref[1, rows[j], :] + d_qpre).astype(BF16)
                        else:
                            both_ref[1, rows[j], :] = d_qpre
                return dlb

            dlb = lax.fori_loop(0, SEQ // (GROUPS_PER_GRAD_STEP * GROUP),
                                functools.partial(grad_step, latent=True, per=GROUPS_PER_GRAD_STEP), jnp.zeros((1, HEAD_DIM), F32))
            dlb = lax.fori_loop(0, CTX_LEN // (per_ctx * GROUP), functools.partial(grad_step, latent=False, per=per_ctx), dlb)
            dlg_ref[d:d + 1, :] = dlb * lb * (1.0 - lb)

        @pl.when(pl.program_id(0) == HEADS - 1)
        def _():
            wait_scatter()

    col = pl.BlockSpec((SEQ, HEAD_DIM), lambda h: (0, h))
    outs = pl.pallas_call(
        body, name="gla_backward", grid=(HEADS,),
        out_shape=[pltpu.HBM((HGRN_SECTIONS, ROWS_ALL, D_MODEL), BF16), jax.ShapeDtypeStruct((2, D_MODEL), F32)]
        + _slot_shapes(GATHER_LATE),
        in_specs=[G_SPEC(0), G_SPEC(1), G_SPEC(2), G_SPEC(3), pl.BlockSpec((2, HEAD_DIM), lambda h: (0, h)), col, col]
        + MASK_SPECS + [ANY] * n_late,
        out_specs=[pl.BlockSpec((HGRN_SECTIONS, ROWS_ALL, HEAD_DIM), lambda h: (0, 0, h)),
                   pl.BlockSpec((2, HEAD_DIM), lambda h: (0, h))] + [ANY] * n_late,
        scratch_shapes=[pltpu.VMEM((N_CHUNKS, HEAD_DIM, HEAD_DIM), F32), pltpu.VMEM((N_CHUNKS, HEAD_DIM, HEAD_DIM), F32),
                        pltpu.VMEM((N_CHUNKS, 1, HEAD_DIM), F32), pltpu.VMEM((2, ROWS_ALL, HEAD_DIM), F32)]
        + _comm_sems(n_late),
        compiler_params=_params(48),
    )(*_pin(g5, g5, g5, g5, lb_logits, d_o, d_z), *_group_masks(), *late_parts)
    return outs[0], outs[1], dict(zip(GATHER_LATE, outs[2:]))


def _head_norm(o, scr):
    rs = []
    for h in range(HEADS):
        cols = slice(h * HEAD_DIM, (h + 1) * HEAD_DIM)
        oh = o[:, cols]
        r = lax.rsqrt(jnp.mean(oh * oh, axis=-1, keepdims=True) + EPS)
        scr[:, cols] = oh * r
        rs.append(r)
    return rs


def _hgrn_out_forward(o_raw, g5, xin, gnorm_w, gate, w_out):
    tm = ROW_TILE

    def body(o_ref, z_ref, x_ref, gw_ref, gate_ref, w_ref, x1_ref, res_ref, scr):
        _head_norm(o_ref[...], scr)
        a = scr[...] * gw_ref[...] * _silu(z_ref[...])
        res = _mm(a, w_ref[...])
        res_ref[...] = res
        x1_ref[...] = x_ref[...] + gate_ref[...] * res

    tile = pl.BlockSpec((tm, D_MODEL), lambda i: (i, 0))
    vec = pl.BlockSpec((1, D_MODEL), lambda i: (0, 0))
    return pl.pallas_call(
        body, name="hgrn_out_forward", grid=(SEQ // tm,),
        out_shape=[pltpu.HBM((SEQ, D_MODEL), F32)] * 2,
        in_specs=[tile, pl.BlockSpec((None, tm, D_MODEL), lambda i: (4, i + CTX_LEN // tm, 0)), tile, vec, vec,
                  pl.BlockSpec((D_MODEL, D_MODEL), lambda i: (0, 0))],
        out_specs=[tile, tile],
        scratch_shapes=[pltpu.VMEM((tm, D_MODEL), F32)],
        compiler_params=_params(32),
    )(*_pin(o_raw, g5, xin, gnorm_w, gate, w_out))


def _hgrn_out_backward(d_x1, o_raw, g5, res, gnorm_w, gate, w_out):
    tm = ROW_TILE

    def body(dx_ref, o_ref, z_ref, res_ref, gw_ref, gate_ref, w_ref, do_ref, dz_ref, dw_out, dgate_ref, dgw_ref, scr, scr2,
             dw_ref):
        @pl.when(pl.program_id(0) == 0)
        def _():
            dw_ref[...] = jnp.zeros_like(dw_ref)
            dgate_ref[...] = jnp.zeros_like(dgate_ref)
            dgw_ref[...] = jnp.zeros_like(dgw_ref)

        dx = dx_ref[...]
        dgate_ref[...] += jnp.sum(dx * res_ref[...], axis=0, keepdims=True)
        d_res = (dx * gate_ref[...]).astype(BF16)
        d_a = _mm_nt(d_res, w_ref[...])
        rs = _head_norm(o_ref[...], scr)
        z = z_ref[...]
        sz = _silu(z)
        o_hat = scr[...]
        o_n = o_hat * gw_ref[...]
        dw_ref[...] += _mm_tn(o_n * sz, d_res)
        d_on = d_a * sz
        dz_ref[...] = (d_a * o_n * _dsilu(z)).astype(BF16)
        dgw_ref[...] += jnp.sum(d_on * o_hat, axis=0, keepdims=True)
        scr2[...] = d_on * gw_ref[...]
        for h in range(HEADS):
            cols = slice(h * HEAD_DIM, (h + 1) * HEAD_DIM)
            dh, oh = scr2[:, cols], scr[:, cols]
            do_ref[:, cols] = (rs[h] * (dh - oh * jnp.mean(dh * oh, axis=-1, keepdims=True))).astype(BF16)

        @pl.when(pl.program_id(0) == SEQ // tm - 1)
        def _():
            dw_out[...] = dw_ref[...].astype(BF16)

    tile = pl.BlockSpec((tm, D_MODEL), lambda i: (i, 0))
    vec = pl.BlockSpec((1, D_MODEL), lambda i: (0, 0))
    mat = pl.BlockSpec((D_MODEL, D_MODEL), lambda i: (0, 0))
    return pl.pallas_call(
        body, name="hgrn_out_backward", grid=(SEQ // tm,),
        out_shape=[pltpu.HBM((SEQ, D_MODEL), BF16)] * 2 + [pltpu.HBM((D_MODEL, D_MODEL), BF16)]
        + [jax.ShapeDtypeStruct((1, D_MODEL), F32)] * 2,
        in_specs=[tile, tile, pl.BlockSpec((None, tm, D_MODEL), lambda i: (4, i + CTX_LEN // tm, 0)), tile, vec, vec, mat],
        out_specs=[tile, tile, mat, vec, vec],
        scratch_shapes=[pltpu.VMEM((tm, D_MODEL), F32)] * 2 + [pltpu.VMEM((D_MODEL, D_MODEL), F32)],
        compiler_params=_params(40),
    )(*_pin(d_x1, o_raw, g5, res, gnorm_w, gate, w_out))


def _pool_constants():
    win = np.zeros((POOL_GROUPS, ROW_TILE, ROW_TILE), np.float32)
    inv = np.zeros((POOL_GROUPS, ROW_TILE, 1), np.float32)
    for g, w in enumerate(POOL_WINDOWS):
        for t in range(ROW_TILE):
            base, p = (t // GRID_W) * GRID_W, t % GRID_W
            lo = min(max(p - w // 2, 0), GRID_W)
            hi = min(max(p - w // 2 + w, 0), GRID_W)
            win[g, t, base + lo:base + hi] = 1.0
            inv[g, t, 0] = 1.0 / np.float32(hi - lo)
    return jnp.asarray(win, BF16), jnp.asarray(win.transpose(0, 2, 1), BF16), jnp.asarray(inv, F32)


def _pool_mix(u_ref, wg_ref, win_ref, inv_ref, pooled_scr, yg_scr):
    for g in range(POOL_GROUPS):
        cols = slice(g * POOL_GROUP_DIM, (g + 1) * POOL_GROUP_DIM)
        for r in range(u_ref.shape[0] // ROW_TILE):
            rows = slice(r * ROW_TILE, (r + 1) * ROW_TILE)
            ug = u_ref[rows, cols]
            pooled_scr[rows, cols] = _mm_exact_lhs(win_ref[g], ug) * inv_ref[g] - ug
        yg_scr[:, cols] = _mm(pooled_scr[:, cols], wg_ref[g])


def _pool_forward_loss(uz, x1, target, gate, w_grp, pool_scale, w_out, final_w):
    tm = POOL_TILE
    win, _, inv = _pool_constants()

    def body(u_ref, z_ref, x_ref, t_ref, gate_ref, wg_ref, ps_ref, w_ref, fw_ref, win_ref, inv_ref,
             dx_ref, loss_ref, dfw_ref, dgate_ref, yg_scr, pooled_scr):
        @pl.when(pl.program_id(0) == 0)
        def _():
            loss_ref[...] = jnp.zeros_like(loss_ref)
            dfw_ref[...] = jnp.zeros_like(dfw_ref)
            dgate_ref[...] = jnp.zeros_like(dgate_ref)

        _pool_mix(u_ref, wg_ref, win_ref, inv_ref, pooled_scr, yg_scr)
        a = yg_scr[...] * ps_ref[...] * _silu(z_ref[...])
        res = _mm(a, w_ref[...])
        x2 = x_ref[...] + gate_ref[...] * res
        r = lax.rsqrt(jnp.mean(x2 * x2, axis=-1, keepdims=True) + EPS)
        xh = x2 * r
        fw = fw_ref[...]
        err = xh * fw - t_ref[...]
        loss_ref[...] += 0.5 * jnp.sum(jnp.mean(err * err, axis=-1, keepdims=True))
        d_y = err * (1.0 / D_MODEL)
        dfw_ref[...] += jnp.sum(d_y * xh, axis=0, keepdims=True)
        d_xh = d_y * fw
        d_x2 = r * (d_xh - xh * jnp.mean(d_xh * xh, axis=-1, keepdims=True))
        dx_ref[...] = d_x2
        dgate_ref[...] += jnp.sum(d_x2 * res, axis=0, keepdims=True)

    tile = pl.BlockSpec((tm, D_MODEL), lambda i: (i, 0))
    vec = pl.BlockSpec((1, D_MODEL), lambda i: (0, 0))
    grp = pl.BlockSpec((POOL_GROUPS, POOL_GROUP_DIM, POOL_GROUP_DIM), lambda i: (0, 0, 0))
    return pl.pallas_call(
        body, name="pool_forward_loss", grid=(SEQ // tm,),
        out_shape=[pltpu.HBM((SEQ, D_MODEL), F32), jax.ShapeDtypeStruct((8, 128), F32),
                   jax.ShapeDtypeStruct((1, D_MODEL), F32), jax.ShapeDtypeStruct((1, D_MODEL), F32)],
        in_specs=[pl.BlockSpec((None, tm, D_MODEL), lambda i: (0, i, 0)), pl.BlockSpec((None, tm, D_MODEL), lambda i: (1, i, 0)),
                  tile, tile, vec, grp, vec, pl.BlockSpec((D_MODEL, D_MODEL), lambda i: (0, 0)), vec, grp,
                  pl.BlockSpec((POOL_GROUPS, ROW_TILE, 1), lambda i: (0, 0, 0))],
        out_specs=[tile, pl.BlockSpec((8, 128), lambda i: (0, 0)), vec, vec],
        scratch_shapes=[pltpu.VMEM((tm, D_MODEL), F32)] * 2,
        compiler_params=_params(40),
    )(*_pin(uz, uz, x1, target, gate, w_grp, pool_scale, w_out, final_w, win, inv))


def _pool_backward(d_x2, uz, gate, w_grp, pool_scale, w_out):
    tm = POOL_TILE
    win, win_t, inv = _pool_constants()

    def body(dx_ref, u_ref, z_ref, gate_ref, wg_ref, ps_ref, w_ref, win_ref, wint_ref, inv_ref,
             duz_ref, dw_out, dwg_out, dps_ref, pooled_scr, yg_scr, dyg_scr, dw_ref, dwg_ref):
        @pl.when(pl.program_id(0) == 0)
        def _():
            dw_ref[...] = jnp.zeros_like(dw_ref)
            dwg_ref[...] = jnp.zeros_like(dwg_ref)
            dps_ref[...] = jnp.zeros_like(dps_ref)

        _pool_mix(u_ref, wg_ref, win_ref, inv_ref, pooled_scr, yg_scr)
        z = z_ref[...]
        sz = _silu(z)
        yg = yg_scr[...]
        y = yg * ps_ref[...]
        d_res = (dx_ref[...] * gate_ref[...]).astype(BF16)
        d_a = _mm_nt(d_res, w_ref[...])
        dw_ref[...] += _mm_tn(y * sz, d_res)
        d_y = d_a * sz
        duz_ref[1] = (d_a * y * _dsilu(z)).astype(BF16)
        dps_ref[...] += jnp.sum(d_y * yg, axis=0, keepdims=True)
        dyg_scr[...] = d_y * ps_ref[...]
        for g in range(POOL_GROUPS):
            cols = slice(g * POOL_GROUP_DIM, (g + 1) * POOL_GROUP_DIM)
            d_yg = dyg_scr[:, cols].astype(BF16)
            d_pool = _mm_nt(d_yg, wg_ref[g])
            dwg_ref[g] += _mm_tn(pooled_scr[:, cols], d_yg)
            for r in range(tm // ROW_TILE):
                rows = slice(r * ROW_TILE, (r + 1) * ROW_TILE)
                dp = d_pool[rows, :]
                duz_ref[0, rows, cols] = (_mm_exact_lhs(wint_ref[g], dp * inv_ref[g]) - dp).astype(BF16)

        @pl.when(pl.program_id(0) == SEQ // tm - 1)
        def _():
            dw_out[...] = dw_ref[...].astype(BF16)
            dwg_out[...] = dwg_ref[...].astype(BF16)

    tile = pl.BlockSpec((tm, D_MODEL), lambda i: (i, 0))
    vec = pl.BlockSpec((1, D_MODEL), lambda i: (0, 0))
    mat = pl.BlockSpec((D_MODEL, D_MODEL), lambda i: (0, 0))
    grp = pl.BlockSpec((POOL_GROUPS, POOL_GROUP_DIM, POOL_GROUP_DIM), lambda i: (0, 0, 0))
    return pl.pallas_call(
        body, name="pool_backward", grid=(SEQ // tm,),
        out_shape=[pltpu.HBM((POOL_SECTIONS, SEQ, D_MODEL), BF16), pltpu.HBM((D_MODEL, D_MODEL), BF16),
                   pltpu.HBM((POOL_GROUPS, POOL_GROUP_DIM, POOL_GROUP_DIM), BF16), jax.ShapeDtypeStruct((1, D_MODEL), F32)],
        in_specs=[tile, pl.BlockSpec((None, tm, D_MODEL), lambda i: (0, i, 0)), pl.BlockSpec((None, tm, D_MODEL), lambda i: (1, i, 0)),
                  vec, grp, vec, mat, grp, grp, pl.BlockSpec((POOL_GROUPS, ROW_TILE, 1), lambda i: (0, 0, 0))],
        out_specs=[pl.BlockSpec((POOL_SECTIONS, tm, D_MODEL), lambda i: (0, i, 0)), mat, grp, vec],
        scratch_shapes=[pltpu.VMEM((tm, D_MODEL), F32)] * 3 + [pltpu.VMEM((D_MODEL, D_MODEL), F32),
                                                               pltpu.VMEM((POOL_GROUPS, POOL_GROUP_DIM, POOL_GROUP_DIM), F32)],
        compiler_params=_params(48),
    )(*_pin(d_x2, uz, uz, gate, w_grp, pool_scale, w_out, win, win_t, inv))


def _ln_mod_backward(d_g, w, ctx_tile, xin, nw, scale, d_up, name):
    n_sec, rows, _ = d_g.shape
    n_mod = scale.shape[0]
    skip = n_mod - 1
    tm = ROW_TILE if skip else 2 * ROW_TILE
    n_tiles = rows // tm

    def body(dg_ref, w_ref, *refs):
        c_ref = refs[0] if skip else None
        x_ref, nw_ref, sc_ref, up_ref, dx_ref, dnw_ref, dmod_ref = refs[skip:]
        i = pl.program_id(0)

        @pl.when(i == 0)
        def _():
            dnw_ref[...] = jnp.zeros_like(dnw_ref)

        @pl.when((i == 0) | (i == skip))
        def _():
            dmod_ref[...] = jnp.zeros_like(dmod_ref)

        d_h = _mm_nt(dg_ref[0], w_ref[:, 0:D_MODEL])
        for k in range(1, n_sec):
            d_h = d_h + _mm_nt(dg_ref[k], w_ref[:, k * D_MODEL:(k + 1) * D_MODEL])
        xv = jnp.where(i == 0, c_ref[...], x_ref[...]) if skip else x_ref[...]
        r = lax.rsqrt(jnp.mean(xv * xv, axis=-1, keepdims=True) + EPS)
        xh = xv * r
        nw_row = nw_ref[...]
        dmod_ref[0:1, :] += jnp.sum(d_h, axis=0, keepdims=True)
        dmod_ref[1:2, :] += jnp.sum(d_h * (xh * nw_row), axis=0, keepdims=True)
        d_xn = d_h * (1.0 + sc_ref[...])
        dnw_ref[...] += jnp.sum(d_xn * xh, axis=0, keepdims=True)
        d_xh = d_xn * nw_row

        @pl.when(i >= skip)
        def _():
            dx_ref[...] = up_ref[...] + r * (d_xh - xh * jnp.mean(d_xh * xh, axis=-1, keepdims=True))

    lat = lambda i: (jnp.maximum(i - skip, 0), 0)
    mod_idx = lambda i: (jnp.minimum(i, n_mod - 1), 0, 0)
    return pl.pallas_call(
        body, name=name, grid=(n_tiles,),
        out_shape=[pltpu.HBM((rows - skip * tm, D_MODEL), F32), jax.ShapeDtypeStruct((1, D_MODEL), F32),
                   jax.ShapeDtypeStruct((n_mod, 8, D_MODEL), F32)],
        in_specs=[pl.BlockSpec((n_sec, tm, D_MODEL), lambda i: (0, i, 0)),
                  pl.BlockSpec((D_MODEL, n_sec * D_MODEL), lambda i: (0, 0))]
        + [pl.BlockSpec((tm, D_MODEL), lambda i: (0, 0))] * skip
        + [pl.BlockSpec((tm, D_MODEL), lat),
           pl.BlockSpec((1, D_MODEL), lambda i: (0, 0)),
           pl.BlockSpec((None, 1, D_MODEL), mod_idx),
           pl.BlockSpec((tm, D_MODEL), lat)],
        out_specs=[pl.BlockSpec((tm, D_MODEL), lat), pl.BlockSpec((1, D_MODEL), lambda i: (0, 0)),
                   pl.BlockSpec((None, 8, D_MODEL), mod_idx)],
        compiler_params=_params(48),
    )(*_pin(d_g, w, *([ctx_tile] * skip), xin, nw, scale.reshape(n_mod, 1, D_MODEL), d_up))


def _weight_grad(h, d_g, name):
    n_sec, rows, _ = d_g.shape
    tm = 768 if rows % 768 == 0 else 512
    n_tiles = rows // tm

    def body(h_ref, dg_ref, dw_ref, acc):
        i = pl.program_id(1)
        prod = _mm_tn(h_ref[...], dg_ref[...])

        @pl.when(i == 0)
        def _():
            acc[...] = prod

        @pl.when((i > 0) & (i < n_tiles - 1))
        def _():
            acc[...] += prod

        @pl.when(i == n_tiles - 1)
        def _():
            dw_ref[...] = (acc[...] + prod).astype(BF16)

    return pl.pallas_call(
        body, name=name, grid=(n_sec, n_tiles),
        out_shape=pltpu.HBM((D_MODEL, n_sec * D_MODEL), BF16),
        in_specs=[pl.BlockSpec((tm, D_MODEL), lambda j, i: (i, 0)), pl.BlockSpec((None, tm, D_MODEL), lambda j, i: (j, i, 0))],
        out_specs=pl.BlockSpec((D_MODEL, D_MODEL), lambda j, i: (0, j)),
        scratch_shapes=[pltpu.VMEM((D_MODEL, D_MODEL), F32)],
        compiler_params=_params(32),
    )(*_pin(h, d_g))


def _weight_grad_paired(h, d_g, name):
    n_sec, rows, _ = d_g.shape
    tm = 768
    n_tiles = rows // tm
    half = D_MODEL // 2
    chip_cols = n_sec * D_MODEL // N_CHIPS

    def body(h_ref, dg_ref, q_ref, slots_ref, acc, keep, send, land, send_sems, recv_sems, own_sem):
        j, i = pl.program_id(0), pl.program_id(1)
        x, y, c = _my_place()
        prod = _mm_tn(h_ref[...], dg_ref[...])

        def to_sibling(k):
            return pltpu.make_async_remote_copy(src_ref=send.at[k], dst_ref=land.at[k], send_sem=send_sems.at[k],
                                                recv_sem=recv_sems.at[k], device_id=(x, y, 1 - c), device_id_type=MESH)

        @pl.when(i == 0)
        def _():
            acc[...] = prod

        @pl.when((i > 0) & (i < n_tiles - 1))
        def _():
            acc[...] += prod

        @pl.when(i == n_tiles - 1)
        def _():
            acc[...] += prod
            keep[j] = acc[pl.ds(_al(c * half, half), half), :]
            send[j] = acc[pl.ds(_al((1 - c) * half, half), half), :].astype(BF16)
            to_sibling(j).start()

        @pl.when((j == n_sec - 1) & (i == n_tiles - 1))
        def _():
            for k in range(n_sec):
                to_sibling(k).wait()
                q_ref[:, k * D_MODEL:(k + 1) * D_MODEL] = (keep[k] + land[k].astype(F32)).astype(BF16)
            chip = 2 * x + y
            own = pltpu.make_async_copy(q_ref.at[:, pl.ds(_al(chip * chip_cols, chip_cols), chip_cols)], slots_ref.at[chip], own_sem)
            own.start()
            own.wait()

    return pl.pallas_call(
        body, name=name, grid=(n_sec, n_tiles),
        out_shape=[jax.ShapeDtypeStruct((half, n_sec * D_MODEL), BF16),
                   jax.ShapeDtypeStruct((N_CHIPS, half, chip_cols), BF16)],
        in_specs=[pl.BlockSpec((tm, D_MODEL), lambda j, i: (i, 0)), pl.BlockSpec((None, tm, D_MODEL), lambda j, i: (j, i, 0))],
        out_specs=[pl.BlockSpec((half, n_sec * D_MODEL), lambda j, i: (0, 0)), ANY],
        scratch_shapes=[pltpu.VMEM((D_MODEL, D_MODEL), F32), pltpu.VMEM((n_sec, half, D_MODEL), F32),
                        pltpu.VMEM((n_sec, half, D_MODEL), BF16), pltpu.VMEM((n_sec, half, D_MODEL), BF16),
                        pltpu.SemaphoreType.DMA((n_sec,)), pltpu.SemaphoreType.DMA((n_sec,)), pltpu.SemaphoreType.DMA],
        compiler_params=_params(56),
    )(*_pin(h, d_g))


def _adamw_math(w, g, m, v):
    m = ADAM_B1 * m + (1.0 - ADAM_B1) * g
    v = ADAM_B2 * v + (1.0 - ADAM_B2) * (g * g)
    m_hat = m / (1.0 - ADAM_B1 ** ADAM_STEP)
    v_hat = v / (1.0 - ADAM_B2 ** ADAM_STEP)
    return -ADAM_LR * (m_hat / (jnp.sqrt(v_hat) + ADAM_EPS) + ADAM_WD * w), m, v


def _adamw(w, g, m, v, name):
    rows, cols = w.shape
    tr = rows if rows <= 128 else 128

    def body(w_ref, g_ref, m_ref, v_ref, d_ref, mo_ref, vo_ref):
        d_ref[...], mo_ref[...], vo_ref[...] = _adamw_math(w_ref[...], g_ref[...], m_ref[...], v_ref[...])

    tile = pl.BlockSpec((tr, cols), lambda i: (i, 0))
    return pl.pallas_call(
        body, name=name, grid=(rows // tr,),
        out_shape=[pltpu.HBM((rows, cols), F32)] * 3,
        in_specs=[tile] * 4, out_specs=[tile] * 3,
    )(*_pin(w, g, m, v))


def _small_sums_and_cond_ctx(small, ada_w0, c_ctx, m, v):
    n_cols = ada_w0.shape[1]

    def body(small_ref, w_ref, c_ref, m_ref, v_ref, all_ref, sums_ref, g_ref, d_ref, mo_ref, vo_ref, part_ref, parts_ref,
             s1, r1, l1, s2, r2, l2):
        start_slabs, finish_slabs = _small_gather(small_ref, all_ref, s1, r1, l1)
        start_parts, finish_parts = _small_gather(part_ref, parts_ref, s2, r2, l2)
        start_slabs()
        finish_slabs()
        acc = all_ref[0:SMALL_ROWS, :]
        for dev in range(1, N_DEV):
            acc = acc + all_ref[dev * SMALL_ROWS:(dev + 1) * SMALL_ROWS, :]
        sums_ref[...] = acc
        x, y, _ = _my_place()
        d_modc = jnp.concatenate([acc[6:7, :], acc[7:8, :], acc[8:9, :]], axis=1)
        mine = jnp.zeros((1, n_cols), F32)
        for s in range(N_CHIPS):
            mine = mine + jnp.where(2 * x + y == s, d_modc[:, s * n_cols:(s + 1) * n_cols], 0.0)
        ah, al = _split2(jnp.broadcast_to(mine, (8, n_cols)))
        wh, wl = _split2(w_ref[...])
        nt = lambda a, b: lax.dot_general(a, b, (((1,), (1,)), ((), ())), preferred_element_type=F32)
        part_ref[...] = nt(ah, wh) + nt(al, wh) + nt(ah, wl)
        start_parts()
        finish_parts()
        total = parts_ref[0:1, :]
        for s in range(1, N_CHIPS):
            total = total + parts_ref[16 * s:16 * s + 1, :]
        w = c_ref[...]
        g = total * _dsilu(w)
        g_ref[...] = g
        d_ref[...], mo_ref[...], vo_ref[...] = _adamw_math(w, g, m_ref[...], v_ref[...])

    seven = [pltpu.SemaphoreType.DMA((7,)), pltpu.SemaphoreType.DMA((7,)), pltpu.SemaphoreType.DMA]
    vec = jax.ShapeDtypeStruct((1, D_MODEL), F32)
    return pl.pallas_call(
        body, name="small_sums_and_cond_ctx",
        out_shape=[jax.ShapeDtypeStruct((N_DEV * SMALL_ROWS, D_MODEL), F32), jax.ShapeDtypeStruct((SMALL_ROWS, D_MODEL), F32),
                   vec, vec, vec, vec],
        in_specs=[VMEM] * 5, out_specs=[VMEM] * 6,
        scratch_shapes=[pltpu.VMEM((8, D_MODEL), F32), pltpu.VMEM((N_DEV * 8, D_MODEL), F32)] + seven + seven,
    )(small, ada_w0, c_ctx, m, v)


def _ada_update(cond_t, d_mod, ada_w, m, v):
    n_layers, _, n_cols = ada_w.shape
    tr = ROW_TILE

    def body(c_ref, dm_ref, w_ref, m_ref, v_ref, g_ref, d_ref, mo_ref, vo_ref):
        g = _mm_f32(_silu(c_ref[...]), dm_ref[...])
        g_ref[...] = g
        d_ref[...], mo_ref[...], vo_ref[...] = _adamw_math(w_ref[...], g, m_ref[...], v_ref[...])

    tile = pl.BlockSpec((None, tr, n_cols), lambda l, i: (l, i, 0))
    return pl.pallas_call(
        body, name="ada_update", grid=(n_layers, D_MODEL // tr),
        out_shape=[pltpu.HBM(ada_w.shape, F32)] * 4,
        in_specs=[pl.BlockSpec((tr, 16), lambda l, i: (i, 0)), pl.BlockSpec((None, 16, n_cols), lambda l, i: (l, 0, 0)),
                  tile, tile, tile],
        out_specs=[tile] * 4,
    )(*_pin(cond_t, d_mod, ada_w, m, v))


def _local_step(x2, ctx2, target, mod_mine, mod_ctx, lb_logits, scale_full, w_in_full, late_shards, norm_w, gnorm, final_w,
                update_late):
    row = lambda a: a.reshape(1, -1)
    shift0, scale0, gate0 = (row(a) for a in jnp.split(mod_mine[0], 3))
    shift1, scale1, gate1 = (row(a) for a in jnp.split(mod_mine[1], 3))
    shift_c, scale_c, _ = (row(a) for a in jnp.split(mod_ctx, 3))
    nw0, nw1 = norm_w[0:1], norm_w[1:2]
    scales0 = jnp.concatenate([scale_c, scale0])

    g5, h0 = _ln_mod_matmul(ctx2, x2, nw0, jnp.concatenate([shift_c, shift0]), scales0, w_in_full, "hgrn_in_forward")
    o_raw, full = _gla_forward(g5, lb_logits, late_shards)
    x1, res0 = _hgrn_out_forward(o_raw, g5, x2, gnorm, gate0, full["hgrn_w_out"])
    uz, h1 = _ln_mod_matmul(None, x1, nw1, shift1, scale1, full["pool_w_in"], "pool_in_forward")
    d_x2, loss_part, d_final, d_gate1 = _pool_forward_loss(uz, x1, target, gate1, full["pool_w_grp"], scale_full,
                                                           full["pool_w_out"], final_w)

    d_uz, dw_pool_out, dw_pool_grp, d_pscale = _pool_backward(d_x2, uz, gate1, full["pool_w_grp"], scale_full, full["pool_w_out"])
    d_x1, d_nw1, d_mod1 = _ln_mod_backward(d_uz, full["pool_w_in"], None, x1, nw1, scale1, d_x2, "pool_in_backward")
    dw_pool_in = _weight_grad(h1, d_uz, "pool_in_weight_grad")
    d_o, d_z, dw_hgrn_out, d_gate0, d_gnorm = _hgrn_out_backward(d_x1, o_raw, g5, res0, gnorm, gate0, full["hgrn_w_out"])
    late_grads = {"hgrn_w_out": dw_hgrn_out, "pool_w_in": dw_pool_in, "pool_w_grp": dw_pool_grp, "pool_w_out": dw_pool_out}
    d_g5, d_lb, late_slots = _gla_backward(g5, lb_logits, d_o, d_z, [late_grads[k].astype(BF16) for k in GATHER_LATE])
    update_late(late_slots)
    dw_hgrn_in, slots0 = _weight_grad_paired(h0, d_g5, "hgrn_in_weight_grad")
    key = GATHER_EARLY[0]
    send_sem, recv_sem, part_thru, slots_thru, token = _scatter_start(dw_hgrn_in, slots0, d_lb, key)
    d_x, d_nw0, d_mod0 = _ln_mod_backward(d_g5, w_in_full, ctx2, x2, nw0 + token[0:1, 0:1], scales0, d_x1, "hgrn_in_backward")
    pending = (send_sem, recv_sem, part_thru, slots_thru)

    zero = jnp.zeros((1, D_MODEL), F32)
    small = jnp.concatenate([d_mod0[1, 0:2], d_gate0, d_mod1[0, 0:2], d_gate1, d_mod0[0, 0:2], zero, d_nw0, d_nw1, d_gnorm,
                             d_final, d_pscale, d_lb, jnp.broadcast_to(loss_part[0:1, 0:1], (1, D_MODEL)),
                             jnp.zeros((SMALL_ROWS - 17, D_MODEL), F32)], axis=0)
    return {"d_x": d_x, "pending": pending, "small": small}


def kernel(x, c, ctx, c_ctx, ada_w, ada_b, norm_w, hgrn_w_in, hgrn_lb_logits, hgrn_gnorm_w, hgrn_w_out, pool_w_in, pool_w_grp, pool_scale, pool_w_out, final_norm_w, loss_target, m_c_ctx, m_ada_w, m_ada_b, m_norm_w, m_hgrn_w_in, m_hgrn_lb_logits, m_hgrn_gnorm_w, m_hgrn_w_out, m_pool_w_in, m_pool_w_grp, m_pool_scale, m_pool_w_out, m_final_norm_w, v_c_ctx, v_ada_w, v_ada_b, v_norm_w, v_hgrn_w_in, v_hgrn_lb_logits, v_hgrn_gnorm_w, v_hgrn_w_out, v_pool_w_in, v_pool_w_grp, v_pool_scale, v_pool_w_out, v_final_norm_w):
    xi, yi, ci = _my_place()
    chip = 2 * xi + yi
    dev = 4 * xi + 2 * yi + ci
    ada_cols = ada_w.shape[2]
    lb_cols = hgrn_lb_logits.shape[2]
    ps_cols = pool_scale.shape[1]
    row = lambda a: a.reshape(1, -1)

    def chip_cols(a, n):
        return lax.dynamic_slice_in_dim(a, chip * n, n, axis=a.ndim - 1)

    def from_chips(g, rows_per_dev, take):
        return jnp.concatenate([g[2 * s * rows_per_dev:2 * s * rows_per_dev + take] for s in range(N_CHIPS)], axis=1)

    first = jnp.concatenate([jnp.broadcast_to(c, (8, D_MODEL)), jnp.pad(hgrn_lb_logits[0], ((0, 6), (0, 0))),
                             jnp.pad(pool_scale, ((0, 7), (0, 0)))], axis=1)
    shards = {"hgrn_w_in": hgrn_w_in[0], "hgrn_w_out": hgrn_w_out[0], "pool_w_in": pool_w_in[0],
              "pool_w_grp": pool_w_grp[0], "pool_w_out": pool_w_out[0]}
    first_all, parts_all, w_in_full = _prologue(first, row(c_ctx), ada_w, chip_cols(ada_b, ada_cols),
                                                shards[GATHER_EARLY[0]].astype(BF16))
    cond_all = first_all[::8, :D_MODEL]
    lb_logits = from_chips(first_all[:, D_MODEL:D_MODEL + lb_cols], 8, 2)
    scale_full = from_chips(first_all[:, D_MODEL + lb_cols:], 8, 1)
    cond_rows = jnp.concatenate([cond_all, row(c_ctx), jnp.zeros((7, D_MODEL), F32)], axis=0)
    mod_all = from_chips(parts_all, 32, 32).reshape(2, 16, 3 * D_MODEL)
    mod_mine = lax.dynamic_index_in_dim(mod_all, dev, axis=1, keepdims=False)

    out = {}

    def update(name, w, g, m, v):
        shape = w.shape
        w2, g2, m2, v2 = (a.reshape(-1, shape[-1]) for a in (w, g, m, v))
        d, mn, vn = _adamw(w2, g2, m2, v2, "adamw_" + name)
        out[name] = tuple(a.reshape(shape) for a in (g2, d, mn, vn))

    moments = {"hgrn_w_in": (m_hgrn_w_in, v_hgrn_w_in), "hgrn_w_out": (m_hgrn_w_out, v_hgrn_w_out),
               "pool_w_in": (m_pool_w_in, v_pool_w_in), "pool_w_grp": (m_pool_w_grp, v_pool_w_grp),
               "pool_w_out": (m_pool_w_out, v_pool_w_out)}
    weights = {"hgrn_w_in": hgrn_w_in, "hgrn_w_out": hgrn_w_out, "pool_w_in": pool_w_in, "pool_w_grp": pool_w_grp,
               "pool_w_out": pool_w_out}

    def reduce_scattered(slots, names, name):
        return dict(zip(names, _sum_and_exchange([slots[k] for k in names], names, name)))

    def update_late(late_slots):
        big = reduce_scattered(late_slots, GATHER_LATE, "exchange_halves_late")
        for k in GATHER_LATE:
            update(k, weights[k], big[k], *moments[k])

    loc = _local_step(x[0], ctx[0], loss_target[0], mod_mine, mod_all[0, 8], lb_logits, scale_full, w_in_full,
                      [shards[k].astype(BF16) for k in GATHER_LATE], norm_w, hgrn_gnorm_w, row(final_norm_w), update_late)

    small_all, sums, g_c, d_c, m_c, v_c = _small_sums_and_cond_ctx(loc["small"], ada_w[0], row(c_ctx), row(m_c_ctx), row(v_c_ctx))
    loss = sums[16, 0]

    g_ada_b = jnp.stack([(sums[0:3] + sums[6:9]).reshape(-1), sums[3:6].reshape(-1)])
    update("ada_b", ada_b, g_ada_b, m_ada_b, v_ada_b)
    update("norm_w", norm_w, sums[9:11], m_norm_w, v_norm_w)
    update("hgrn_gnorm_w", hgrn_gnorm_w, sums[11:12], m_hgrn_gnorm_w, v_hgrn_gnorm_w)
    update("final_norm_w", row(final_norm_w), sums[12:13], row(m_final_norm_w), row(v_final_norm_w))
    update("pool_scale", pool_scale, chip_cols(sums[13:14], ps_cols), m_pool_scale, v_pool_scale)
    update("hgrn_lb_logits", hgrn_lb_logits, chip_cols(sums[14:16], lb_cols)[None], m_hgrn_lb_logits, v_hgrn_lb_logits)

    per_dev = small_all.reshape(N_DEV, SMALL_ROWS, D_MODEL)
    pad7 = jnp.zeros((7, 3 * D_MODEL), F32)
    dm0 = jnp.concatenate([per_dev[:, 0:3].reshape(N_DEV, -1), sums[6:9].reshape(1, -1), pad7], axis=0)
    dm1 = jnp.concatenate([per_dev[:, 3:6].reshape(N_DEV, -1), jnp.zeros((8, 3 * D_MODEL), F32)], axis=0)
    d_mod = chip_cols(jnp.stack([dm0, dm1]), ada_cols)
    out["ada_w"] = _ada_update(cond_rows.T, d_mod, ada_w, m_ada_w, v_ada_w)

    out["c_ctx"] = tuple(a.reshape(-1) for a in (g_c, d_c, m_c, v_c))
    out["final_norm_w"] = tuple(a.reshape(-1) for a in out["final_norm_w"])

    done = [g_c, out["ada_w"][1]]
    key = GATHER_EARLY[0]
    _, early = _scatter_wait(*loc["pending"], done, key)
    big_grads = reduce_scattered({key: early}, GATHER_EARLY, "exchange_halves_early")
    for k in GATHER_EARLY:
        update(k, weights[k], big_grads[k], *moments[k])

    names = ["c_ctx", "ada_w", "ada_b", "norm_w", "hgrn_w_in", "hgrn_lb_logits", "hgrn_gnorm_w", "hgrn_w_out", "pool_w_in",
             "pool_w_grp", "pool_scale", "pool_w_out", "final_norm_w"]
    return (loss, loc["d_x"][None], *[out[k][0] for k in names], *[out[k][1] for k in names], *[out[k][2] for k in names],
            *[out[k][3] for k in names])
```

```python
import functools

import numpy as np
import jax
import jax.numpy as jnp
from jax import lax
from jax.experimental import pallas as pl
from jax.experimental.pallas import tpu as pltpu

F32 = jnp.float32
BF16 = jnp.bfloat16

D_MODEL = 1024
SEQ = 2048
CTX_LEN = 256
ROWS_ALL = CTX_LEN + SEQ
HEADS = 8
HEAD_DIM = 128
CHUNK = 64
N_CTX_CHUNKS = CTX_LEN // CHUNK
N_LAT_CHUNKS = SEQ // CHUNK
N_CHUNKS = N_CTX_CHUNKS + N_LAT_CHUNKS
GRID_W = 64
POOL_WINDOWS = (2, 4, 8, 16)
POOL_GROUPS = 4
POOL_GROUP_DIM = 256
HGRN_SECTIONS = 5
POOL_SECTIONS = 2
EPS = 1e-6
N_DEV = 8
N_CHIPS = 4
ROW_TILE = 256
POOL_TILE = 512
SMALL_ROWS = 24

ADAM_LR = 0.001
ADAM_B1 = 0.9
ADAM_B2 = 0.999
ADAM_EPS = 1e-08
ADAM_WD = 0.01
ADAM_STEP = 10

MESH = pl.DeviceIdType.MESH
MIB = 1 << 20
ANY = pl.BlockSpec(memory_space=pl.ANY)
VMEM = pl.BlockSpec(memory_space=pltpu.VMEM)


def _params(vmem_mib=None):
    if vmem_mib is None:
        return pltpu.CompilerParams()
    return pltpu.CompilerParams(vmem_limit_bytes=vmem_mib * MIB)


def _pin(*operands):
    return [pltpu.with_memory_space_constraint(a, pltpu.HBM) if a.size * a.dtype.itemsize >= MIB else a for a in operands]


def _sig(a):
    return 0.5 * jnp.tanh(0.5 * a) + 0.5


def _silu(a):
    return a * _sig(a)


def _dsilu(a):
    s = _sig(a)
    return s * (1.0 + a * (1.0 - s))


def _mm(a, b):
    return jnp.dot(a.astype(BF16), b.astype(BF16), preferred_element_type=F32)


def _mm_nt(a, b):
    return lax.dot_general(a.astype(BF16), b.astype(BF16), (((1,), (1,)), ((), ())), preferred_element_type=F32)


def _mm_tn(a, b):
    return lax.dot_general(a.astype(BF16), b.astype(BF16), (((0,), (0,)), ((), ())), preferred_element_type=F32)


def _split2(a):
    hi = a.astype(BF16)
    lo = (a - hi.astype(F32)).astype(BF16)
    return hi, lo


def _mm_exact_lhs(m_bf, a):
    hi, lo = _split2(a)
    return jnp.dot(m_bf, hi, preferred_element_type=F32) + jnp.dot(m_bf, lo, preferred_element_type=F32)


def _mm_f32(a, b):
    ah, al = _split2(a)
    bh, bl = _split2(b)
    return (jnp.dot(ah, bh, preferred_element_type=F32) + jnp.dot(al, bh, preferred_element_type=F32)
            + jnp.dot(ah, bl, preferred_element_type=F32))


def _my_place():
    return lax.axis_index("x"), lax.axis_index("y"), lax.axis_index("c")


def _small_gather(x_ref, out_ref, send_sems, recv_sems, local_sem):
    m_per = x_ref.shape[0]
    x, y, c = _my_place()
    me, sibling = (x, y, c), (x, y, 1 - c)
    chips = [(1 - x, y), (x, 1 - y), (1 - x, 1 - y)]

    def rows(px, py, pc):
        return out_ref.at[pl.ds((4 * px + 2 * py + pc) * m_per, m_per), :]

    def copy(k, block, to, src=None):
        return pltpu.make_async_remote_copy(
            src_ref=rows(*block) if src is None else src, dst_ref=rows(*block),
            send_sem=send_sems.at[k], recv_sem=recv_sems.at[k], device_id=to, device_id_type=MESH)

    def mine():
        return pltpu.make_async_copy(x_ref, rows(*me), local_sem)

    def first():
        return [copy(0, me, sibling, src=x_ref)] + [copy(1 + j, me, (*chip, c), src=x_ref) for j, chip in enumerate(chips)]

    def start():
        mine().start()
        for cp in first():
            cp.start()

    def finish():
        passed = [copy(4 + j, (*chip, c), sibling) for j, chip in enumerate(chips)]
        for j, chip in enumerate(chips):
            copy(1 + j, (*chip, c), me).wait_recv()
            passed[j].start()
        copy(0, sibling, me).wait_recv()
        for j, chip in enumerate(chips):
            copy(4 + j, (*chip, 1 - c), me).wait_recv()
        for cp in first() + passed:
            cp.wait_send()
        mine().wait()

    return start, finish


W_SPECS = {
    "hgrn_w_in": ((D_MODEL, 5 * D_MODEL), (1, 1280, 0, 512)),
    "hgrn_w_out": ((D_MODEL, D_MODEL), (0, 256, 0, 128)),
    "pool_w_in": ((D_MODEL, 2 * D_MODEL), (1, 512, 0, 512)),
    "pool_w_grp": ((POOL_GROUPS, POOL_GROUP_DIM, POOL_GROUP_DIM), (1, 64, 1, 32)),
    "pool_w_out": ((D_MODEL, D_MODEL), (0, 256, 0, 128)),
}

def _al(v, m):
    return pl.multiple_of(v, m)


def _region(ref, spec, chip, half):
    ca, cn, ha, hn = spec
    idx = [slice(None)] * len(ref.shape)
    if ca == ha:
        if half is None:
            idx[ca] = pl.ds(_al(chip * cn, cn), cn)
        else:
            idx[ca] = pl.ds(_al(chip * cn + half * hn, hn), hn)
    else:
        idx[ca] = pl.ds(_al(chip * cn, cn), cn)
        if half is not None:
            idx[ha] = pl.ds(_al(half * hn, hn), hn)
    return ref.at[tuple(idx)]


def _half_of(ref, spec, half):
    _, _, ha, hn = spec
    idx = [slice(None)] * len(ref.shape)
    idx[ha] = pl.ds(_al(half * hn, hn), hn)
    return ref.at[tuple(idx)]


def _half_shape(name):
    full, (ca, cn, ha, hn) = W_SPECS[name]
    shp = list(full)
    shp[ca] = cn
    shp[ha] = hn
    return tuple(shp)


def _gather_two_level(names, sh, land, full, send_sems, recv_sems, local_sems):
    n = len(names)
    specs = [W_SPECS[k][1] for k in names]
    x, y, c = _my_place()
    chip_me = 2 * x + y
    sibling = (x, y, 1 - c)
    chips = [(1 - x, y), (x, 1 - y), (1 - x, 1 - y)]

    def remote(k, src, dst, to):
        return pltpu.make_async_remote_copy(src_ref=src, dst_ref=dst, send_sem=send_sems.at[k], recv_sem=recv_sems.at[k],
                                            device_id=to, device_id_type=MESH)

    def three_halves(a):
        ca, cn, ha, hn = specs[a]
        idx = [slice(None)] * len(land[a].shape)
        if ca == ha:
            idx[ca] = pl.ds(0, 3 * hn)
        else:
            idx[ca], idx[ha] = pl.ds(0, 3 * cn), pl.ds(0, hn)
        return land[a].at[tuple(idx)]

    def own(a):
        return pltpu.make_async_copy(sh[a], _region(land[a], specs[a], chip_me, None), local_sems.at[a])

    def out(a):
        return pltpu.make_async_copy(land[a], full[a], local_sems.at[a])

    def start():
        for a in range(n):
            own(a).start()
            for px, py in chips:
                remote(a, _half_of(sh[a], specs[a], c), _region(land[a], specs[a], chip_me, c), (px, py, c)).start()

    def forward():
        for a in range(n):
            remote(a, three_halves(a), three_halves(a), sibling).wait_recv()
            for px, py in chips:
                landed = _region(land[a], specs[a], 2 * px + py, c)
                remote(n + a, landed, landed, sibling).start()

    def finish():
        for a in range(n):
            remote(n + a, three_halves(a), three_halves(a), sibling).wait_recv()
            remote(a, three_halves(a), three_halves(a), sibling).wait_send()
            remote(n + a, three_halves(a), three_halves(a), sibling).wait_send()
            own(a).wait()
        for a in range(n):
            out(a).start()
        for a in range(n):
            out(a).wait()

    return start, forward, finish


def _scatter_direct(names, part, slots, send_sems, recv_sems, local_sems):
    specs = [W_SPECS[k][1] for k in names]
    x, y, c = _my_place()
    dev_me = 4 * x + 2 * y + c

    def local(a):
        return pltpu.make_async_copy(_region(part[a], specs[a], 2 * x + y, c), slots[a].at[dev_me], local_sems.at[a])

    def start():
        for a in range(len(names)):
            local(a).start()
            for flip in range(1, N_DEV):
                tx = 1 - x if flip >> 2 else x
                ty = 1 - y if (flip >> 1) & 1 else y
                tc = 1 - c if flip & 1 else c
                pltpu.make_async_remote_copy(src_ref=_region(part[a], specs[a], 2 * tx + ty, tc), dst_ref=slots[a].at[dev_me],
                                             send_sem=send_sems.at[a], recv_sem=recv_sems.at[a], device_id=(tx, ty, tc),
                                             device_id_type=MESH).start()

    def wait():
        for a in range(len(names)):
            seven = slots[a].at[pl.ds(0, N_DEV - 1)]
            pltpu.make_async_remote_copy(src_ref=seven, dst_ref=seven, send_sem=send_sems.at[a], recv_sem=recv_sems.at[a],
                                         device_id=(x, y, c), device_id_type=MESH).wait()
            local(a).wait()

    return start, wait


HBM_SPEC = pl.BlockSpec(memory_space=pltpu.HBM)
SEM_SPEC = pl.BlockSpec(memory_space=pltpu.SEMAPHORE)
SPLIT_EFFECT = pltpu.SideEffectType.DATAFLOW_SIDE_EFFECTING


def _scatter_start(part, slots, after, name_key):
    _, cols, _, _ = W_SPECS[name_key][1]

    def body(part_ref, slots_ref, after_ref, send_sem, recv_sem, part_thru, slots_thru, token):
        x, y, c = _my_place()
        for tx, ty in ((1 - x, y), (x, 1 - y), (1 - x, 1 - y)):
            src = part_ref.at[:, pl.ds(_al((2 * tx + ty) * cols, cols), cols)]
            pltpu.make_async_remote_copy(src_ref=src, dst_ref=slots_ref.at[2 * x + y], send_sem=send_sem, recv_sem=recv_sem,
                                         device_id=(tx, ty, c), device_id_type=MESH).start()
        token[...] = jnp.zeros_like(token)

    return pl.pallas_call(
        body, name="scatter_start_" + name_key,
        out_shape=(pltpu.SemaphoreType.DMA(()), pltpu.SemaphoreType.DMA(()), pltpu.HBM(part.shape, part.dtype),
                   pltpu.HBM(slots.shape, slots.dtype), jax.ShapeDtypeStruct((8, 128), F32)),
        in_specs=(HBM_SPEC, HBM_SPEC, ANY), out_specs=(SEM_SPEC, SEM_SPEC, HBM_SPEC, HBM_SPEC, VMEM),
        input_output_aliases={0: 2, 1: 3},
        compiler_params=pltpu.CompilerParams(has_side_effects=SPLIT_EFFECT),
    )(pltpu.with_memory_space_constraint(part, pltpu.HBM), pltpu.with_memory_space_constraint(slots, pltpu.HBM), after)


def _scatter_wait(send_sem, recv_sem, part_thru, slots_thru, after, name_key):
    def body(part_ref, slots_ref, send_sem, recv_sem, *rest):
        x, y, c = _my_place()
        landed = slots_ref.at[pl.ds(0, N_CHIPS - 1)]
        copy = pltpu.make_async_remote_copy(src_ref=landed, dst_ref=landed, send_sem=send_sem, recv_sem=recv_sem,
                                            device_id=(x, y, c), device_id_type=MESH)
        copy.wait_send()
        copy.wait_recv()

    return pl.pallas_call(
        body, name="scatter_wait_" + name_key,
        out_shape=(pltpu.HBM(part_thru.shape, part_thru.dtype), pltpu.HBM(slots_thru.shape, slots_thru.dtype)),
        in_specs=(HBM_SPEC, HBM_SPEC, SEM_SPEC, SEM_SPEC) + (ANY,) * len(after), out_specs=(HBM_SPEC, HBM_SPEC),
        input_output_aliases={0: 0, 1: 1},
        compiler_params=pltpu.CompilerParams(has_side_effects=SPLIT_EFFECT),
    )(part_thru, slots_thru, send_sem, recv_sem, *after)


def _comm_sems(n):
    return [pltpu.SemaphoreType.DMA((n,)), pltpu.SemaphoreType.DMA((n,)), pltpu.SemaphoreType.DMA((n,))]


GATHER_EARLY = ("hgrn_w_in",)
GATHER_LATE = ("hgrn_w_out", "pool_w_in", "pool_w_grp", "pool_w_out")


def _weight_gather(sh, full, spec, send_sems, recv_sems, local_sem):
    x, y, c = _my_place()
    chip_me = 2 * x + y
    sibling = (x, y, 1 - c)
    chips = [(1 - x, y), (x, 1 - y), (1 - x, 1 - y)]

    def remote(k, src, dst, to):
        return pltpu.make_async_remote_copy(src_ref=src, dst_ref=dst, send_sem=send_sems.at[k], recv_sem=recv_sems.at[k],
                                            device_id=to, device_id_type=MESH)

    def own():
        return pltpu.make_async_copy(sh, _region(full, spec, chip_me, None), local_sem)

    def direct():
        return [remote(j, _half_of(sh, spec, c), _region(full, spec, chip_me, c), (px, py, c)) for j, (px, py) in enumerate(chips)]

    def start():
        own().start()
        for cp in direct():
            cp.start()

    def finish():
        passed = []
        for j, (px, py) in enumerate(chips):
            landed = _region(full, spec, 2 * px + py, c)
            remote(j, landed, landed, (px, py, c)).wait_recv()
            passed.append(remote(3 + j, landed, landed, sibling))
            passed[-1].start()
        for j, (px, py) in enumerate(chips):
            other = _region(full, spec, 2 * px + py, 1 - c)
            remote(3 + j, other, other, sibling).wait_recv()
        for cp in direct() + passed:
            cp.wait_send()
        own().wait()

    return start, finish


def _prologue(first, c_ctx, ada_w, ada_b_cols, w_shard):
    n_layers, _, n_cols = ada_w.shape
    key = GATHER_EARLY[0]
    spec = W_SPECS[key][1]
    m_first = first.shape[0]

    def body(first_ref, cctx_ref, adaw_ref, adab_ref, sh_ref, first_all, parts_all, full_ref, parts_scr,
             s1, r1, l1, s2, r2, l2, ws, wr, wl):
        start_first, finish_first = _small_gather(first_ref, first_all, s1, r1, l1)
        start_parts, finish_parts = _small_gather(parts_scr, parts_all, s2, r2, l2)
        start_weight, finish_weight = _weight_gather(sh_ref, full_ref, spec, ws, wr, wl)
        start_first()
        start_weight()
        finish_first()
        cond = jnp.concatenate([first_all[m_first * d:m_first * d + 1, 0:D_MODEL] for d in range(N_DEV)]
                               + [cctx_ref[...], jnp.zeros((16 - N_DEV - 1, D_MODEL), F32)], axis=0)
        act = _silu(cond)
        for i in range(n_layers):
            parts_scr[16 * i:16 * (i + 1), :] = _mm_f32(act, adaw_ref[i]) + adab_ref[i]
        start_parts()
        finish_parts()
        finish_weight()

    seven = [pltpu.SemaphoreType.DMA((7,)), pltpu.SemaphoreType.DMA((7,)), pltpu.SemaphoreType.DMA]
    return pl.pallas_call(
        body, name="prologue",
        out_shape=[jax.ShapeDtypeStruct((N_DEV * m_first, first.shape[1]), F32),
                   jax.ShapeDtypeStruct((N_DEV * 16 * n_layers, n_cols), F32), jax.ShapeDtypeStruct(W_SPECS[key][0], BF16)],
        in_specs=[VMEM] * 5, out_specs=[VMEM] * 3,
        scratch_shapes=[pltpu.VMEM((16 * n_layers, n_cols), F32)] + seven + seven
        + [pltpu.SemaphoreType.DMA((6,)), pltpu.SemaphoreType.DMA((6,)), pltpu.SemaphoreType.DMA],
        compiler_params=_params(48),
    )(first, c_ctx, ada_w, ada_b_cols.reshape(n_layers, 1, n_cols), w_shard)


def _slot_shapes(names):
    return [jax.ShapeDtypeStruct((N_DEV,) + _half_shape(k), BF16) for k in names]


def _sum_and_exchange(slots, names, name):
    n = len(names)
    specs = [W_SPECS[k][1] for k in names]

    def shard_shape(k):
        shp = list(_half_shape(k))
        shp[W_SPECS[k][1][2]] *= 2
        return tuple(shp)

    def body(*refs):
        slot, out, half = refs[:n], refs[n:2 * n], refs[2 * n:3 * n]
        send_sems, recv_sems, local_sems = refs[3 * n:]
        x, y, c = _my_place()
        sibling = (x, y, 1 - c)

        def remote(a, src, dst):
            return pltpu.make_async_remote_copy(src_ref=src, dst_ref=dst, send_sem=send_sems.at[a], recv_sem=recv_sems.at[a],
                                                device_id=sibling, device_id_type=MESH)

        local = [pltpu.make_async_copy(half[a], _half_of(out[a], specs[a], c), local_sems.at[a]) for a in range(n)]
        for a in range(n):
            acc = slot[a][0].astype(F32)
            for d in range(1, slot[a].shape[0]):
                acc = acc + slot[a][d].astype(F32)
            half[a][...] = acc
            local[a].start()
            remote(a, half[a], _half_of(out[a], specs[a], c)).start()
        for a in range(n):
            theirs = _half_of(out[a], specs[a], 1 - c)
            remote(a, theirs, theirs).wait_recv()
            remote(a, half[a], half[a]).wait_send()
            local[a].wait()

    return pl.pallas_call(
        body, name=name,
        out_shape=[jax.ShapeDtypeStruct(shard_shape(k), F32) for k in names],
        in_specs=[VMEM] * n, out_specs=[VMEM] * n,
        scratch_shapes=[pltpu.VMEM(_half_shape(k), F32) for k in names]
        + [pltpu.SemaphoreType.DMA((n,)), pltpu.SemaphoreType.DMA((n,)), pltpu.SemaphoreType.DMA((n,))],
        compiler_params=_params(40),
    )(*slots)


def _ln_mod_matmul(ctx_tile, xin, nw, shift, scale, w, name):
    n_mod = shift.shape[0]
    skip = n_mod - 1
    tm = ROW_TILE if skip else 2 * ROW_TILE
    rows = xin.shape[0] + skip * tm
    n_sec = w.shape[1] // D_MODEL

    def body(*refs):
        c_ref = refs[0] if skip else None
        x_ref, nw_ref, sh_ref, sc_ref, w_ref, g_ref, h_ref = refs[skip:]
        xv = jnp.where(pl.program_id(0) == 0, c_ref[...], x_ref[...]) if skip else x_ref[...]
        r = lax.rsqrt(jnp.mean(xv * xv, axis=-1, keepdims=True) + EPS)
        h_f32 = (xv * r * nw_ref[...]) * (1.0 + sc_ref[...]) + sh_ref[...]
        h = h_f32.astype(BF16)
        h_ref[...] = h
        for k in range(n_sec):
            g_ref[k] = jnp.dot(h, w_ref[:, k * D_MODEL:(k + 1) * D_MODEL], preferred_element_type=F32)

    mod_spec = pl.BlockSpec((None, 1, D_MODEL), lambda i: (jnp.minimum(i, n_mod - 1), 0, 0))
    return pl.pallas_call(
        body, name=name, grid=(rows // tm,),
        out_shape=[pltpu.HBM((n_sec, rows, D_MODEL), F32), pltpu.HBM((rows, D_MODEL), BF16)],
        in_specs=[pl.BlockSpec((tm, D_MODEL), lambda i: (0, 0))] * skip
        + [pl.BlockSpec((tm, D_MODEL), lambda i: (jnp.maximum(i - skip, 0), 0)),
           pl.BlockSpec((1, D_MODEL), lambda i: (0, 0)),
           mod_spec, mod_spec,
           pl.BlockSpec((D_MODEL, n_sec * D_MODEL), lambda i: (0, 0))],
        out_specs=[pl.BlockSpec((n_sec, tm, D_MODEL), lambda i: (0, i, 0)),
                   pl.BlockSpec((tm, D_MODEL), lambda i: (i, 0))],
        compiler_params=_params(48),
    )(*_pin(*([ctx_tile] * skip), xin, nw, shift.reshape(n_mod, 1, D_MODEL), scale.reshape(n_mod, 1, D_MODEL), w))


def _gates(fpre, lb):
    sg = _sig(fpre)
    f = lb + (1.0 - lb) * sg
    return sg, f, 1.0 - f, jnp.log(f)


G_SPEC = lambda sec: pl.BlockSpec((None, ROWS_ALL, HEAD_DIM), lambda h, sec=sec: (sec, 0, h))


GROUP = 2 * CHUNK


def _group_masks():
    rid, cid = np.arange(GROUP)[:, None], np.arange(GROUP)[None, :]
    same = (rid >= CHUNK) == (cid >= CHUNK)
    sums, back = [], []
    for rev in (False, True):
        causal = (cid >= rid) if rev else (cid <= rid)
        anti = (cid <= rid) if rev else (cid >= rid)
        sums.append(same & causal)
        back.append(np.concatenate([same & anti, same & ~anti], axis=1))
    sums, back = np.stack(sums).astype(np.float32), np.stack(back).astype(np.float32)
    return jnp.asarray(sums, BF16), jnp.asarray(back, BF16), jnp.asarray(sums, F32)


MASK_SPECS = [pl.BlockSpec((2, GROUP, GROUP), lambda h: (0, 0, 0)), pl.BlockSpec((2, GROUP, 2 * GROUP), lambda h: (0, 0, 0)),
              pl.BlockSpec((2, GROUP, GROUP), lambda h: (0, 0, 0))]


def _group_sum(m_bf, a):
    hi, lo = _split2(a)
    r = jnp.dot(m_bf, jnp.concatenate([hi, lo], axis=1), preferred_element_type=F32)
    return r[:, :HEAD_DIM] + r[:, HEAD_DIM:]


def _chunk_row(a, pos):
    return jnp.concatenate([jnp.broadcast_to(a[c * CHUNK + pos:c * CHUNK + pos + 1, :], (CHUNK, HEAD_DIM)) for c in range(2)], axis=0)


def _by_chunk(a, second):
    return jnp.concatenate([jnp.where(second, 0.0, a), jnp.where(second, a, 0.0)], axis=1)


def _own_block(r):
    return jnp.concatenate([r[0:CHUNK, 0:HEAD_DIM], r[CHUNK:GROUP, HEAD_DIM:2 * HEAD_DIM]], axis=0)


def _scan_step_of(row_chunk, rev, latent):
    if not rev:
        return row_chunk
    return (N_CHUNKS + N_CTX_CHUNKS - 1 - row_chunk) if latent else (N_CTX_CHUNKS - 1 - row_chunk)


def _group_rows(i, j, per_step, latent):
    base = CTX_LEN if latent else 0
    return pl.multiple_of(base + (i * per_step + j) * GROUP, GROUP)


GROUPS_PER_STEP = 16
GROUPS_PER_GRAD_STEP = 4


def _gla_forward(g5, lb_logits, late_shards):
    q_scale = HEAD_DIM ** -0.5
    per_lat, per_ctx = GROUPS_PER_STEP, min(GROUPS_PER_STEP, CTX_LEN // GROUP)
    n_late = len(GATHER_LATE)

    def body(ff_ref, fb_ref, v_ref, q_ref, lg_ref, sums_ref, _, keep_ref, *rest):
        shard_refs, o_ref, full_refs = rest[:n_late], rest[n_late], rest[n_late + 1:2 * n_late + 1]
        st_ref, decay_ref, qt_ref = rest[2 * n_late + 1:2 * n_late + 4]
        land_refs = rest[2 * n_late + 4:3 * n_late + 4]
        start_gather, forward_gather, finish_gather = _gather_two_level(GATHER_LATE, shard_refs, land_refs, full_refs,
                                                                        *rest[3 * n_late + 4:])

        @pl.when(pl.program_id(0) == 0)
        def _():
            start_gather()

        @pl.when(pl.program_id(0) == HEADS - 1)
        def _():
            forward_gather()

        second = lax.broadcasted_iota(jnp.int32, (GROUP, HEAD_DIM), 0) >= CHUNK
        for rev in (False, True):
            f_ref = fb_ref if rev else ff_ref
            lb = _sig(lg_ref[1:2, :] if rev else lg_ref[0:1, :])
            d = 1 if rev else 0
            last = 0 if rev else CHUNK - 1
            mid = CHUNK // 2 if rev else CHUNK // 2 - 1

            def local_step(i, carry, latent, per):
                r0s = [_group_rows(i, j, per, latent) for j in range(per)]
                rows = [pl.ds(r0, GROUP) for r0 in r0s]
                gates = [_gates(f_ref[r, :], lb) for r in rows]
                vs = [v_ref[r, :] for r in rows]
                bs = [_group_sum(sums_ref[d], g[3]) for g in gates]
                bls = [_chunk_row(b, last) for b in bs]
                ups = [_mm_tn(v, _by_chunk(g[2] * jnp.exp(bl - b), second)) for v, g, b, bl in zip(vs, gates, bs, bls)]
                if latent:
                    qs = [_silu(q_ref[r, :]) * q_scale for r in rows]
                    bms = [_chunk_row(b, mid) for b in bs]
                    a_s = [_mm_nt(q * jnp.exp(b - bm), g[2] * jnp.exp(bm - b)) for q, g, b, bm in zip(qs, gates, bs, bms)]
                    outs = [_mm(a * keep_ref[d], v) for a, v in zip(a_s, vs)]
                for j in range(per):
                    for c in range(2):
                        step = _scan_step_of(r0s[j] // CHUNK + c, rev, latent)
                        decay_ref[step] = jnp.exp(bls[j][c * CHUNK:c * CHUNK + 1, :])
                        st_ref[step] = ups[j][:, c * HEAD_DIM:(c + 1) * HEAD_DIM]
                    if latent:
                        orow = pl.ds(pl.multiple_of(r0s[j] - CTX_LEN, GROUP), GROUP)
                        qt_ref[orow, :] = (qs[j] * jnp.exp(bs[j])).astype(BF16)
                        if rev:
                            o_ref[orow, :] += outs[j]
                        else:
                            o_ref[orow, :] = outs[j]
                return carry

            lax.fori_loop(0, CTX_LEN // (per_ctx * GROUP), functools.partial(local_step, latent=False, per=per_ctx), 0)
            lax.fori_loop(0, SEQ // (per_lat * GROUP), functools.partial(local_step, latent=True, per=per_lat), 0)

            def scan_step(t, st):
                update = st_ref[t]
                st_ref[t] = st
                return st * decay_ref[t] + update

            lax.fori_loop(0, N_CHUNKS, scan_step, jnp.zeros((HEAD_DIM, HEAD_DIM), F32), unroll=2)

            def inter_step(i, carry):
                r0s = [_group_rows(i, j, per_lat, True) for j in range(per_lat)]
                orows = [pl.ds(pl.multiple_of(r0 - CTX_LEN, GROUP), GROUP) for r0 in r0s]
                states = [jnp.concatenate([st_ref[_scan_step_of(r0 // CHUNK + c, rev, True)] for c in range(2)], axis=0)
                          for r0 in r0s]
                prods = [lax.dot_general(qt_ref[orow, :], s.astype(BF16), (((1,), (1,)), ((), ())), preferred_element_type=F32)
                         for orow, s in zip(orows, states)]
                for orow, r in zip(orows, prods):
                    o_ref[orow, :] += _own_block(r)
                return carry

            lax.fori_loop(0, SEQ // (per_lat * GROUP), inter_step, 0)

        @pl.when(pl.program_id(0) == HEADS - 1)
        def _():
            finish_gather()

    outs = pl.pallas_call(
        body, name="gla_forward", grid=(HEADS,),
        out_shape=[pltpu.HBM((SEQ, D_MODEL), F32)] + [jax.ShapeDtypeStruct(W_SPECS[k][0], BF16) for k in GATHER_LATE],
        in_specs=[G_SPEC(0), G_SPEC(1), G_SPEC(2), G_SPEC(3), pl.BlockSpec((2, HEAD_DIM), lambda h: (0, h))] + MASK_SPECS
        + [ANY] * n_late,
        out_specs=[pl.BlockSpec((SEQ, HEAD_DIM), lambda h: (0, h))] + [ANY] * n_late,
        scratch_shapes=[pltpu.VMEM((N_CHUNKS, HEAD_DIM, HEAD_DIM), F32), pltpu.VMEM((N_CHUNKS, 1, HEAD_DIM), F32),
                        pltpu.VMEM((SEQ, HEAD_DIM), BF16)] + [pltpu.VMEM(W_SPECS[k][0], BF16) for k in GATHER_LATE]
        + [pltpu.SemaphoreType.DMA((2 * n_late,)), pltpu.SemaphoreType.DMA((2 * n_late,)), pltpu.SemaphoreType.DMA((n_late,))],
        compiler_params=_params(48),
    )(*_pin(g5, g5, g5, g5, lb_logits), *_group_masks(), *late_shards)
    return outs[0], dict(zip(GATHER_LATE, outs[1:]))


def _gla_backward(g5, lb_logits, d_o, d_z, late_parts):
    q_scale = HEAD_DIM ** -0.5
    per_lat, per_ctx = GROUPS_PER_STEP, min(GROUPS_PER_STEP, CTX_LEN // GROUP)
    n_late = len(GATHER_LATE)

    def body(ff_ref, fb_ref, v_ref, q_ref, lg_ref, do_ref, dz_ref, sums_ref, back_ref, keep_ref, *rest):
        part_refs, (dg_ref, dlg_ref), slot_refs = rest[:n_late], rest[n_late:n_late + 2], rest[n_late + 2:2 * n_late + 2]
        st_ref, dst_ref, decay_ref, both_ref = rest[2 * n_late + 2:2 * n_late + 6]
        start_scatter, wait_scatter = _scatter_direct(GATHER_LATE, part_refs, slot_refs, *rest[2 * n_late + 6:])

        @pl.when(pl.program_id(0) == 0)
        def _():
            start_scatter()

        dg_ref[3, 0:CTX_LEN, :] = jnp.zeros((CTX_LEN, HEAD_DIM), BF16)
        dg_ref[4, 0:CTX_LEN, :] = jnp.zeros((CTX_LEN, HEAD_DIM), BF16)
        dg_ref[4, CTX_LEN:ROWS_ALL, :] = dz_ref[...]
        second = lax.broadcasted_iota(jnp.int32, (GROUP, HEAD_DIM), 0) >= CHUNK
        for rev in (False, True):
            d = 1 if rev else 0
            f_ref = fb_ref if rev else ff_ref
            lb = _sig(lg_ref[d:d + 1, :])
            last = 0 if rev else CHUNK - 1
            mid = CHUNK // 2 if rev else CHUNK // 2 - 1

            def local_step(i, carry, latent, per):
                r0s = [_group_rows(i, j, per, latent) for j in range(per)]
                rows = [pl.ds(r0, GROUP) for r0 in r0s]
                gates = [_gates(f_ref[r, :], lb) for r in rows]
                bs = [_group_sum(sums_ref[d], g[3]) for g in gates]
                bls = [_chunk_row(b, last) for b in bs]
                ups = [_mm_tn(v_ref[r, :], _by_chunk(g[2] * jnp.exp(bl - b), second)) for r, g, b, bl in zip(rows, gates, bs, bls)]
                if latent:
                    orows = [pl.ds(pl.multiple_of(r0 - CTX_LEN, GROUP), GROUP) for r0 in r0s]
                    d_ups = [_mm_tn(do_ref[orow, :], _by_chunk(_silu(q_ref[r, :]) * q_scale * jnp.exp(b), second))
                             for orow, r, b in zip(orows, rows, bs)]
                for j in range(per):
                    for c in range(2):
                        step = _scan_step_of(r0s[j] // CHUNK + c, rev, latent)
                        decay_ref[step] = jnp.exp(bls[j][c * CHUNK:c * CHUNK + 1, :])
                        st_ref[step] = ups[j][:, c * HEAD_DIM:(c + 1) * HEAD_DIM]
                        if latent:
                            dst_ref[step] = d_ups[j][:, c * HEAD_DIM:(c + 1) * HEAD_DIM]
                        else:
                            dst_ref[step] = jnp.zeros((HEAD_DIM, HEAD_DIM), F32)
                return carry

            lax.fori_loop(0, CTX_LEN // (per_ctx * GROUP), functools.partial(local_step, latent=False, per=per_ctx), 0)
            lax.fori_loop(0, SEQ // (per_lat * GROUP), functools.partial(local_step, latent=True, per=per_lat), 0)

            def scan_step(i, carry):
                st, d_st = carry
                j = N_CHUNKS - 1 - i
                update, d_update = st_ref[i], dst_ref[j]
                st_ref[i] = st
                dst_ref[j] = d_st
                return st * decay_ref[i] + update, d_st * decay_ref[j] + d_update

            zero_state = jnp.zeros((HEAD_DIM, HEAD_DIM), F32)
            lax.fori_loop(0, N_CHUNKS, scan_step, (zero_state, zero_state))

            def grad_step(i, dlb, latent, per):
                r0s = [_group_rows(i, j, per, latent) for j in range(per)]
                rows = [pl.ds(r0, GROUP) for r0 in r0s]
                gates = [_gates(f_ref[r, :], lb) for r in rows]
                vs = [v_ref[r, :] for r in rows]
                bs = [_group_sum(sums_ref[d], g[3]) for g in gates]
                bls = [_chunk_row(b, last) for b in bs]
                e_ends = [jnp.exp(bl - b) for b, bl in zip(bs, bls)]
                k_ends = [g[2] * e for g, e in zip(gates, e_ends)]
                sts = [[st_ref[_scan_step_of(r0 // CHUNK + c, rev, latent)] for c in range(2)] for r0 in r0s]
                d_sts = [[dst_ref[_scan_step_of(r0 // CHUNK + c, rev, latent)] for c in range(2)] for r0 in r0s]
                d_kends = [_own_block(_mm(v, jnp.concatenate(ds, axis=1))) for v, ds in zip(vs, d_sts)]
                d_vs = [_own_block(_mm_nt(ke, jnp.concatenate(ds, axis=0))) for ke, ds in zip(k_ends, d_sts)]
                at_last = [jnp.concatenate([jnp.broadcast_to(jnp.sum(ds[c] * s[c], axis=0, keepdims=True), (CHUNK, HEAD_DIM))
                                            for c in range(2)], axis=0) * jnp.exp(bl) for ds, s, bl in zip(d_sts, sts, bls)]
                t_kends = [dk * ke for dk, ke in zip(d_kends, k_ends)]
                d_ks = [dk * e for dk, e in zip(d_kends, e_ends)]
                if latent:
                    orows = [pl.ds(pl.multiple_of(r0 - CTX_LEN, GROUP), GROUP) for r0 in r0s]
                    qpres = [q_ref[r, :] for r in rows]
                    q_sigs = [_sig(qp) for qp in qpres]
                    qs = [qp * sg * q_scale for qp, sg in zip(qpres, q_sigs)]
                    bms = [_chunk_row(b, mid) for b in bs]
                    e_bs = [jnp.exp(b) for b in bs]
                    e_qms = [jnp.exp(b - bm) for b, bm in zip(bs, bms)]
                    e_kms = [jnp.exp(bm - b) for b, bm in zip(bs, bms)]
                    q_ts = [q * e for q, e in zip(qs, e_bs)]
                    q_ms = [q * e for q, e in zip(qs, e_qms)]
                    k_ms = [g[2] * e for g, e in zip(gates, e_kms)]
                    d_outs = [do_ref[orow, :] for orow in orows]
                    a_s = [_mm_nt(qm, km) * keep_ref[d] for qm, km in zip(q_ms, k_ms)]
                    d_as = [_mm_nt(do, v) * keep_ref[d] for do, v in zip(d_outs, vs)]
                    d_qts = [_own_block(_mm(do, jnp.concatenate(s, axis=1))) for do, s in zip(d_outs, sts)]
                    d_qms = [_mm(da, km) for da, km in zip(d_as, k_ms)]
                    d_kms = [_mm_tn(da, qm) for da, qm in zip(d_as, q_ms)]
                    d_vs = [dv + _mm_tn(a, do) for dv, a, do in zip(d_vs, a_s, d_outs)]
                    d_ks = [dk + dkm * e for dk, dkm, e in zip(d_ks, d_kms, e_kms)]
                    d_bs = [jnp.concatenate([dqt * qt + dqm * qm - dkm * km, t], axis=0)
                            for dqt, qt, dqm, qm, dkm, km, t in zip(d_qts, q_ts, d_qms, q_ms, d_kms, k_ms, t_kends)]
                    d_qs = [dqt * eb + dqm * eq for dqt, eb, dqm, eq in zip(d_qts, e_bs, d_qms, e_qms)]
                    back = back_ref[d]
                else:
                    d_bs, back = t_kends, back_ref[d, :, GROUP:2 * GROUP]
                d_lfs = [_group_sum(back, db) + al for db, al in zip(d_bs, at_last)]
                for j in range(per):
                    sg, f = gates[j][0], gates[j][1]
                    d_f = d_lfs[j] / f - d_ks[j]
                    dg_ref[d, rows[j], :] = (d_f * (1.0 - lb) * sg * (1.0 - sg)).astype(BF16)
                    dlb = dlb + jnp.sum(d_f * (1.0 - sg), axis=0, keepdims=True)
                    if rev:
                        dg_ref[2, rows[j], :] = (both_ref[0, rows[j], :] + d_vs[j]).astype(BF16)
                    else:
                        both_ref[0, rows[j], :] = d_vs[j]
                    if latent:
                        d_qpre = d_qs[j] * q_scale * (q_sigs[j] * (1.0 + qpres[j] * (1.0 - q_sigs[j])))
                        if rev:
                            dg_ref[3, rows[j], :] = (both_ref[1, rows[j], :] + d_qpre).astype(BF16)
                        else:
                            both_ref[1, rows[j], :] = d_qpre
                return dlb

            dlb = lax.fori_loop(0, SEQ // (GROUPS_PER_GRAD_STEP * GROUP),
                                functools.partial(grad_step, latent=True, per=GROUPS_PER_GRAD_STEP), jnp.zeros((1, HEAD_DIM), F32))
            dlb = lax.fori_loop(0, CTX_LEN // (per_ctx * GROUP), functools.partial(grad_step, latent=False, per=per_ctx), dlb)
            dlg_ref[d:d + 1, :] = dlb * lb * (1.0 - lb)

        @pl.when(pl.program_id(0) == HEADS - 1)
        def _():
            wait_scatter()

    col = pl.BlockSpec((SEQ, HEAD_DIM), lambda h: (0, h))
    outs = pl.pallas_call(
        body, name="gla_backward", grid=(HEADS,),
        out_shape=[pltpu.HBM((HGRN_SECTIONS, ROWS_ALL, D_MODEL), BF16), jax.ShapeDtypeStruct((2, D_MODEL), F32)]
        + _slot_shapes(GATHER_LATE),
        in_specs=[G_SPEC(0), G_SPEC(1), G_SPEC(2), G_SPEC(3), pl.BlockSpec((2, HEAD_DIM), lambda h: (0, h)), col, col]
        + MASK_SPECS + [ANY] * n_late,
        out_specs=[pl.BlockSpec((HGRN_SECTIONS, ROWS_ALL, HEAD_DIM), lambda h: (0, 0, h)),
                   pl.BlockSpec((2, HEAD_DIM), lambda h: (0, h))] + [ANY] * n_late,
        scratch_shapes=[pltpu.VMEM((N_CHUNKS, HEAD_DIM, HEAD_DIM), F32), pltpu.VMEM((N_CHUNKS, HEAD_DIM, HEAD_DIM), F32),
                        pltpu.VMEM((N_CHUNKS, 1, HEAD_DIM), F32), pltpu.VMEM((2, ROWS_ALL, HEAD_DIM), F32)]
        + _comm_sems(n_late),
        compiler_params=_params(48),
    )(*_pin(g5, g5, g5, g5, lb_logits, d_o, d_z), *_group_masks(), *late_parts)
    return outs[0], outs[1], dict(zip(GATHER_LATE, outs[2:]))


def _head_norm(o, scr):
    rs = []
    for h in range(HEADS):
        cols = slice(h * HEAD_DIM, (h + 1) * HEAD_DIM)
        oh = o[:, cols]
        r = lax.rsqrt(jnp.mean(oh * oh, axis=-1, keepdims=True) + EPS)
        scr[:, cols] = oh * r
        rs.append(r)
    return rs


def _hgrn_out_forward(o_raw, g5, xin, gnorm_w, gate, w_out):
    tm = ROW_TILE

    def body(o_ref, z_ref, x_ref, gw_ref, gate_ref, w_ref, x1_ref, res_ref, scr):
        _head_norm(o_ref[...], scr)
        a = scr[...] * gw_ref[...] * _silu(z_ref[...])
        res = _mm(a, w_ref[...])
        res_ref[...] = res
        x1_ref[...] = x_ref[...] + gate_ref[...] * res

    tile = pl.BlockSpec((tm, D_MODEL), lambda i: (i, 0))
    vec = pl.BlockSpec((1, D_MODEL), lambda i: (0, 0))
    return pl.pallas_call(
        body, name="hgrn_out_forward", grid=(SEQ // tm,),
        out_shape=[pltpu.HBM((SEQ, D_MODEL), F32)] * 2,
        in_specs=[tile, pl.BlockSpec((None, tm, D_MODEL), lambda i: (4, i + CTX_LEN // tm, 0)), tile, vec, vec,
                  pl.BlockSpec((D_MODEL, D_MODEL), lambda i: (0, 0))],
        out_specs=[tile, tile],
        scratch_shapes=[pltpu.VMEM((tm, D_MODEL), F32)],
        compiler_params=_params(32),
    )(*_pin(o_raw, g5, xin, gnorm_w, gate, w_out))


def _hgrn_out_backward(d_x1, o_raw, g5, res, gnorm_w, gate, w_out):
    tm = ROW_TILE

    def body(dx_ref, o_ref, z_ref, res_ref, gw_ref, gate_ref, w_ref, do_ref, dz_ref, dw_out, dgate_ref, dgw_ref, scr, scr2,
             dw_ref):
        @pl.when(pl.program_id(0) == 0)
        def _():
            dw_ref[...] = jnp.zeros_like(dw_ref)
            dgate_ref[...] = jnp.zeros_like(dgate_ref)
            dgw_ref[...] = jnp.zeros_like(dgw_ref)

        dx = dx_ref[...]
        dgate_ref[...] += jnp.sum(dx * res_ref[...], axis=0, keepdims=True)
        d_res = (dx * gate_ref[...]).astype(BF16)
        d_a = _mm_nt(d_res, w_ref[...])
        rs = _head_norm(o_ref[...], scr)
        z = z_ref[...]
        sz = _silu(z)
        o_hat = scr[...]
        o_n = o_hat * gw_ref[...]
        dw_ref[...] += _mm_tn(o_n * sz, d_res)
        d_on = d_a * sz
        dz_ref[...] = (d_a * o_n * _dsilu(z)).astype(BF16)
        dgw_ref[...] += jnp.sum(d_on * o_hat, axis=0, keepdims=True)
        scr2[...] = d_on * gw_ref[...]
        for h in range(HEADS):
            cols = slice(h * HEAD_DIM, (h + 1) * HEAD_DIM)
            dh, oh = scr2[:, cols], scr[:, cols]
            do_ref[:, cols] = (rs[h] * (dh - oh * jnp.mean(dh * oh, axis=-1, keepdims=True))).astype(BF16)

        @pl.when(pl.program_id(0) == SEQ // tm - 1)
        def _():
            dw_out[...] = dw_ref[...].astype(BF16)

    tile = pl.BlockSpec((tm, D_MODEL), lambda i: (i, 0))
    vec = pl.BlockSpec((1, D_MODEL), lambda i: (0, 0))
    mat = pl.BlockSpec((D_MODEL, D_MODEL), lambda i: (0, 0))
    return pl.pallas_call(
        body, name="hgrn_out_backward", grid=(SEQ // tm,),
        out_shape=[pltpu.HBM((SEQ, D_MODEL), BF16)] * 2 + [pltpu.HBM((D_MODEL, D_MODEL), BF16)]
        + [jax.ShapeDtypeStruct((1, D_MODEL), F32)] * 2,
        in_specs=[tile, tile, pl.BlockSpec((None, tm, D_MODEL), lambda i: (4, i + CTX_LEN // tm, 0)), tile, vec, vec, mat],
        out_specs=[tile, tile, mat, vec, vec],
        scratch_shapes=[pltpu.VMEM((tm, D_MODEL), F32)] * 2 + [pltpu.VMEM((D_MODEL, D_MODEL), F32)],
        compiler_params=_params(40),
    )(*_pin(d_x1, o_raw, g5, res, gnorm_w, gate, w_out))


def _pool_constants():
    win = np.zeros((POOL_GROUPS, ROW_TILE, ROW_TILE), np.float32)
    inv = np.zeros((POOL_GROUPS, ROW_TILE, 1), np.float32)
    for g, w in enumerate(POOL_WINDOWS):
        for t in range(ROW_TILE):
            base, p = (t // GRID_W) * GRID_W, t % GRID_W
            lo = min(max(p - w // 2, 0), GRID_W)
            hi = min(max(p - w // 2 + w, 0), GRID_W)
            win[g, t, base + lo:base + hi] = 1.0
            inv[g, t, 0] = 1.0 / np.float32(hi - lo)
    return jnp.asarray(win, BF16), jnp.asarray(win.transpose(0, 2, 1), BF16), jnp.asarray(inv, F32)


def _pool_mix(u_ref, wg_ref, win_ref, inv_ref, pooled_scr, yg_scr):
    for g in range(POOL_GROUPS):
        cols = slice(g * POOL_GROUP_DIM, (g + 1) * POOL_GROUP_DIM)
        for r in range(u_ref.shape[0] // ROW_TILE):
            rows = slice(r * ROW_TILE, (r + 1) * ROW_TILE)
            ug = u_ref[rows, cols]
            pooled_scr[rows, cols] = _mm_exact_lhs(win_ref[g], ug) * inv_ref[g] - ug
        yg_scr[:, cols] = _mm(pooled_scr[:, cols], wg_ref[g])


def _pool_forward_loss(uz, x1, target, gate, w_grp, pool_scale, w_out, final_w):
    tm = POOL_TILE
    win, _, inv = _pool_constants()

    def body(u_ref, z_ref, x_ref, t_ref, gate_ref, wg_ref, ps_ref, w_ref, fw_ref, win_ref, inv_ref,
             dx_ref, loss_ref, dfw_ref, dgate_ref, yg_scr, pooled_scr):
        @pl.when(pl.program_id(0) == 0)
        def _():
            loss_ref[...] = jnp.zeros_like(loss_ref)
            dfw_ref[...] = jnp.zeros_like(dfw_ref)
            dgate_ref[...] = jnp.zeros_like(dgate_ref)

        _pool_mix(u_ref, wg_ref, win_ref, inv_ref, pooled_scr, yg_scr)
        a = yg_scr[...] * ps_ref[...] * _silu(z_ref[...])
        res = _mm(a, w_ref[...])
        x2 = x_ref[...] + gate_ref[...] * res
        r = lax.rsqrt(jnp.mean(x2 * x2, axis=-1, keepdims=True) + EPS)
        xh = x2 * r
        fw = fw_ref[...]
        err = xh * fw - t_ref[...]
        loss_ref[...] += 0.5 * jnp.sum(jnp.mean(err * err, axis=-1, keepdims=True))
        d_y = err * (1.0 / D_MODEL)
        dfw_ref[...] += jnp.sum(d_y * xh, axis=0, keepdims=True)
        d_xh = d_y * fw
        d_x2 = r * (d_xh - xh * jnp.mean(d_xh * xh, axis=-1, keepdims=True))
        dx_ref[...] = d_x2
        dgate_ref[...] += jnp.sum(d_x2 * res, axis=0, keepdims=True)

    tile = pl.BlockSpec((tm, D_MODEL), lambda i: (i, 0))
    vec = pl.BlockSpec((1, D_MODEL), lambda i: (0, 0))
    grp = pl.BlockSpec((POOL_GROUPS, POOL_GROUP_DIM, POOL_GROUP_DIM), lambda i: (0, 0, 0))
    return pl.pallas_call(
        body, name="pool_forward_loss", grid=(SEQ // tm,),
        out_shape=[pltpu.HBM((SEQ, D_MODEL), F32), jax.ShapeDtypeStruct((8, 128), F32),
                   jax.ShapeDtypeStruct((1, D_MODEL), F32), jax.ShapeDtypeStruct((1, D_MODEL), F32)],
        in_specs=[pl.BlockSpec((None, tm, D_MODEL), lambda i: (0, i, 0)), pl.BlockSpec((None, tm, D_MODEL), lambda i: (1, i, 0)),
                  tile, tile, vec, grp, vec, pl.BlockSpec((D_MODEL, D_MODEL), lambda i: (0, 0)), vec, grp,
                  pl.BlockSpec((POOL_GROUPS, ROW_TILE, 1), lambda i: (0, 0, 0))],
        out_specs=[tile, pl.BlockSpec((8, 128), lambda i: (0, 0)), vec, vec],
        scratch_shapes=[pltpu.VMEM((tm, D_MODEL), F32)] * 2,
        compiler_params=_params(40),
    )(*_pin(uz, uz, x1, target, gate, w_grp, pool_scale, w_out, final_w, win, inv))


def _pool_backward(d_x2, uz, gate, w_grp, pool_scale, w_out):
    tm = POOL_TILE
    win, win_t, inv = _pool_constants()

    def body(dx_ref, u_ref, z_ref, gate_ref, wg_ref, ps_ref, w_ref, win_ref, wint_ref, inv_ref,
             duz_ref, dw_out, dwg_out, dps_ref, pooled_scr, yg_scr, dyg_scr, dw_ref, dwg_ref):
        @pl.when(pl.program_id(0) == 0)
        def _():
            dw_ref[...] = jnp.zeros_like(dw_ref)
            dwg_ref[...] = jnp.zeros_like(dwg_ref)
            dps_ref[...] = jnp.zeros_like(dps_ref)

        _pool_mix(u_ref, wg_ref, win_ref, inv_ref, pooled_scr, yg_scr)
        z = z_ref[...]
        sz = _silu(z)
        yg = yg_scr[...]
        y = yg * ps_ref[...]
        d_res = (dx_ref[...] * gate_ref[...]).astype(BF16)
        d_a = _mm_nt(d_res, w_ref[...])
        dw_ref[...] += _mm_tn(y * sz, d_res)
        d_y = d_a * sz
        duz_ref[1] = (d_a * y * _dsilu(z)).astype(BF16)
        dps_ref[...] += jnp.sum(d_y * yg, axis=0, keepdims=True)
        dyg_scr[...] = d_y * ps_ref[...]
        for g in range(POOL_GROUPS):
            cols = slice(g * POOL_GROUP_DIM, (g + 1) * POOL_GROUP_DIM)
            d_yg = dyg_scr[:, cols].astype(BF16)
            d_pool = _mm_nt(d_yg, wg_ref[g])
            dwg_ref[g] += _mm_tn(pooled_scr[:, cols], d_yg)
            for r in range(tm // ROW_TILE):
                rows = slice(r * ROW_TILE, (r + 1) * ROW_TILE)
                dp = d_pool[rows, :]
                duz_ref[0, rows, cols] = (_mm_exact_lhs(wint_ref[g], dp * inv_ref[g]) - dp).astype(BF16)

        @pl.when(pl.program_id(0) == SEQ // tm - 1)
        def _():
            dw_out[...] = dw_ref[...].astype(BF16)
            dwg_out[...] = dwg_ref[...].astype(BF16)

    tile = pl.BlockSpec((tm, D_MODEL), lambda i: (i, 0))
    vec = pl.BlockSpec((1, D_MODEL), lambda i: (0, 0))
    mat = pl.BlockSpec((D_MODEL, D_MODEL), lambda i: (0, 0))
    grp = pl.BlockSpec((POOL_GROUPS, POOL_GROUP_DIM, POOL_GROUP_DIM), lambda i: (0, 0, 0))
    return pl.pallas_call(
        body, name="pool_backward", grid=(SEQ // tm,),
        out_shape=[pltpu.HBM((POOL_SECTIONS, SEQ, D_MODEL), BF16), pltpu.HBM((D_MODEL, D_MODEL), BF16),
                   pltpu.HBM((POOL_GROUPS, POOL_GROUP_DIM, POOL_GROUP_DIM), BF16), jax.ShapeDtypeStruct((1, D_MODEL), F32)],
        in_specs=[tile, pl.BlockSpec((None, tm, D_MODEL), lambda i: (0, i, 0)), pl.BlockSpec((None, tm, D_MODEL), lambda i: (1, i, 0)),
                  vec, grp, vec, mat, grp, grp, pl.BlockSpec((POOL_GROUPS, ROW_TILE, 1), lambda i: (0, 0, 0))],
        out_specs=[pl.BlockSpec((POOL_SECTIONS, tm, D_MODEL), lambda i: (0, i, 0)), mat, grp, vec],
        scratch_shapes=[pltpu.VMEM((tm, D_MODEL), F32)] * 3 + [pltpu.VMEM((D_MODEL, D_MODEL), F32),
                                                               pltpu.VMEM((POOL_GROUPS, POOL_GROUP_DIM, POOL_GROUP_DIM), F32)],
        compiler_params=_params(48),
    )(*_pin(d_x2, uz, uz, gate, w_grp, pool_scale, w_out, win, win_t, inv))


def _ln_mod_backward(d_g, w, ctx_tile, xin, nw, scale, d_up, name):
    n_sec, rows, _ = d_g.shape
    n_mod = scale.shape[0]
    skip = n_mod - 1
    tm = ROW_TILE if skip else 2 * ROW_TILE
    n_tiles = rows // tm

    def body(dg_ref, w_ref, *refs):
        c_ref = refs[0] if skip else None
        x_ref, nw_ref, sc_ref, up_ref, dx_ref, dnw_ref, dmod_ref = refs[skip:]
        i = pl.program_id(0)

        @pl.when(i == 0)
        def _():
            dnw_ref[...] = jnp.zeros_like(dnw_ref)

        @pl.when((i == 0) | (i == skip))
        def _():
            dmod_ref[...] = jnp.zeros_like(dmod_ref)

        d_h = _mm_nt(dg_ref[0], w_ref[:, 0:D_MODEL])
        for k in range(1, n_sec):
            d_h = d_h + _mm_nt(dg_ref[k], w_ref[:, k * D_MODEL:(k + 1) * D_MODEL])
        xv = jnp.where(i == 0, c_ref[...], x_ref[...]) if skip else x_ref[...]
        r = lax.rsqrt(jnp.mean(xv * xv, axis=-1, keepdims=True) + EPS)
        xh = xv * r
        nw_row = nw_ref[...]
        dmod_ref[0:1, :] += jnp.sum(d_h, axis=0, keepdims=True)
        dmod_ref[1:2, :] += jnp.sum(d_h * (xh * nw_row), axis=0, keepdims=True)
        d_xn = d_h * (1.0 + sc_ref[...])
        dnw_ref[...] += jnp.sum(d_xn * xh, axis=0, keepdims=True)
        d_xh = d_xn * nw_row

        @pl.when(i >= skip)
        def _():
            dx_ref[...] = up_ref[...] + r * (d_xh - xh * jnp.mean(d_xh * xh, axis=-1, keepdims=True))

    lat = lambda i: (jnp.maximum(i - skip, 0), 0)
    mod_idx = lambda i: (jnp.minimum(i, n_mod - 1), 0, 0)
    return pl.pallas_call(
        body, name=name, grid=(n_tiles,),
        out_shape=[pltpu.HBM((rows - skip * tm, D_MODEL), F32), jax.ShapeDtypeStruct((1, D_MODEL), F32),
                   jax.ShapeDtypeStruct((n_mod, 8, D_MODEL), F32)],
        in_specs=[pl.BlockSpec((n_sec, tm, D_MODEL), lambda i: (0, i, 0)),
                  pl.BlockSpec((D_MODEL, n_sec * D_MODEL), lambda i: (0, 0))]
        + [pl.BlockSpec((tm, D_MODEL), lambda i: (0, 0))] * skip
        + [pl.BlockSpec((tm, D_MODEL), lat),
           pl.BlockSpec((1, D_MODEL), lambda i: (0, 0)),
           pl.BlockSpec((None, 1, D_MODEL), mod_idx),
           pl.BlockSpec((tm, D_MODEL), lat)],
        out_specs=[pl.BlockSpec((tm, D_MODEL), lat), pl.BlockSpec((1, D_MODEL), lambda i: (0, 0)),
                   pl.BlockSpec((None, 8, D_MODEL), mod_idx)],
        compiler_params=_params(48),
    )(*_pin(d_g, w, *([ctx_tile] * skip), xin, nw, scale.reshape(n_mod, 1, D_MODEL), d_up))


def _weight_grad(h, d_g, name):
    n_sec, rows, _ = d_g.shape
    tm = 768 if rows % 768 == 0 else 512
    n_tiles = rows // tm

    def body(h_ref, dg_ref, dw_ref, acc):
        i = pl.program_id(1)
        prod = _mm_tn(h_ref[...], dg_ref[...])

        @pl.when(i == 0)
        def _():
            acc[...] = prod

        @pl.when((i > 0) & (i < n_tiles - 1))
        def _():
            acc[...] += prod

        @pl.when(i == n_tiles - 1)
        def _():
            dw_ref[...] = (acc[...] + prod).astype(BF16)

    return pl.pallas_call(
        body, name=name, grid=(n_sec, n_tiles),
        out_shape=pltpu.HBM((D_MODEL, n_sec * D_MODEL), BF16),
        in_specs=[pl.BlockSpec((tm, D_MODEL), lambda j, i: (i, 0)), pl.BlockSpec((None, tm, D_MODEL), lambda j, i: (j, i, 0))],
        out_specs=pl.BlockSpec((D_MODEL, D_MODEL), lambda j, i: (0, j)),
        scratch_shapes=[pltpu.VMEM((D_MODEL, D_MODEL), F32)],
        compiler_params=_params(32),
    )(*_pin(h, d_g))


def _weight_grad_paired(h, d_g, name):
    n_sec, rows, _ = d_g.shape
    tm = 768
    n_tiles = rows // tm
    half = D_MODEL // 2
    chip_cols = n_sec * D_MODEL // N_CHIPS

    def body(h_ref, dg_ref, q_ref, slots_ref, acc, keep, send, land, send_sems, recv_sems, own_sem):
        j, i = pl.program_id(0), pl.program_id(1)
        x, y, c = _my_place()
        prod = _mm_tn(h_ref[...], dg_ref[...])

        def to_sibling(k):
            return pltpu.make_async_remote_copy(src_ref=send.at[k], dst_ref=land.at[k], send_sem=send_sems.at[k],
                                                recv_sem=recv_sems.at[k], device_id=(x, y, 1 - c), device_id_type=MESH)

        @pl.when(i == 0)
        def _():
            acc[...] = prod

        @pl.when((i > 0) & (i < n_tiles - 1))
        def _():
            acc[...] += prod

        @pl.when(i == n_tiles - 1)
        def _():
            acc[...] += prod
            keep[j] = acc[pl.ds(_al(c * half, half), half), :]
            send[j] = acc[pl.ds(_al((1 - c) * half, half), half), :].astype(BF16)
            to_sibling(j).start()

        @pl.when((j == n_sec - 1) & (i == n_tiles - 1))
        def _():
            for k in range(n_sec):
                to_sibling(k).wait()
                q_ref[:, k * D_MODEL:(k + 1) * D_MODEL] = (keep[k] + land[k].astype(F32)).astype(BF16)
            chip = 2 * x + y
            own = pltpu.make_async_copy(q_ref.at[:, pl.ds(_al(chip * chip_cols, chip_cols), chip_cols)], slots_ref.at[chip], own_sem)
            own.start()
            own.wait()

    return pl.pallas_call(
        body, name=name, grid=(n_sec, n_tiles),
        out_shape=[jax.ShapeDtypeStruct((half, n_sec * D_MODEL), BF16),
                   jax.ShapeDtypeStruct((N_CHIPS, half, chip_cols), BF16)],
        in_specs=[pl.BlockSpec((tm, D_MODEL), lambda j, i: (i, 0)), pl.BlockSpec((None, tm, D_MODEL), lambda j, i: (j, i, 0))],
        out_specs=[pl.BlockSpec((half, n_sec * D_MODEL), lambda j, i: (0, 0)), ANY],
        scratch_shapes=[pltpu.VMEM((D_MODEL, D_MODEL), F32), pltpu.VMEM((n_sec, half, D_MODEL), F32),
                        pltpu.VMEM((n_sec, half, D_MODEL), BF16), pltpu.VMEM((n_sec, half, D_MODEL), BF16),
                        pltpu.SemaphoreType.DMA((n_sec,)), pltpu.SemaphoreType.DMA((n_sec,)), pltpu.SemaphoreType.DMA],
        compiler_params=_params(56),
    )(*_pin(h, d_g))


def _adamw_math(w, g, m, v):
    m = ADAM_B1 * m + (1.0 - ADAM_B1) * g
    v = ADAM_B2 * v + (1.0 - ADAM_B2) * (g * g)
    m_hat = m / (1.0 - ADAM_B1 ** ADAM_STEP)
    v_hat = v / (1.0 - ADAM_B2 ** ADAM_STEP)
    return -ADAM_LR * (m_hat / (jnp.sqrt(v_hat) + ADAM_EPS) + ADAM_WD * w), m, v


def _adamw(w, g, m, v, name):
    rows, cols = w.shape
    tr = rows if rows <= 128 else 128

    def body(w_ref, g_ref, m_ref, v_ref, d_ref, mo_ref, vo_ref):
        d_ref[...], mo_ref[...], vo_ref[...] = _adamw_math(w_ref[...], g_ref[...], m_ref[...], v_ref[...])

    tile = pl.BlockSpec((tr, cols), lambda i: (i, 0))
    return pl.pallas_call(
        body, name=name, grid=(rows // tr,),
        out_shape=[pltpu.HBM((rows, cols), F32)] * 3,
        in_specs=[tile] * 4, out_specs=[tile] * 3,
    )(*_pin(w, g, m, v))


def _small_sums_and_cond_ctx(small, ada_w0, c_ctx, m, v):
    n_cols = ada_w0.shape[1]

    def body(small_ref, w_ref, c_ref, m_ref, v_ref, all_ref, sums_ref, g_ref, d_ref, mo_ref, vo_ref, part_ref, parts_ref,
             s1, r1, l1, s2, r2, l2):
        start_slabs, finish_slabs = _small_gather(small_ref, all_ref, s1, r1, l1)
        start_parts, finish_parts = _small_gather(part_ref, parts_ref, s2, r2, l2)
        start_slabs()
        finish_slabs()
        acc = all_ref[0:SMALL_ROWS, :]
        for dev in range(1, N_DEV):
            acc = acc + all_ref[dev * SMALL_ROWS:(dev + 1) * SMALL_ROWS, :]
        sums_ref[...] = acc
        x, y, _ = _my_place()
        d_modc = jnp.concatenate([acc[6:7, :], acc[7:8, :], acc[8:9, :]], axis=1)
        mine = jnp.zeros((1, n_cols), F32)
        for s in range(N_CHIPS):
            mine = mine + jnp.where(2 * x + y == s, d_modc[:, s * n_cols:(s + 1) * n_cols], 0.0)
        ah, al = _split2(jnp.broadcast_to(mine, (8, n_cols)))
        wh, wl = _split2(w_ref[...])
        nt = lambda a, b: lax.dot_general(a, b, (((1,), (1,)), ((), ())), preferred_element_type=F32)
        part_ref[...] = nt(ah, wh) + nt(al, wh) + nt(ah, wl)
        start_parts()
        finish_parts()
        total = parts_ref[0:1, :]
        for s in range(1, N_CHIPS):
            total = total + parts_ref[16 * s:16 * s + 1, :]
        w = c_ref[...]
        g = total * _dsilu(w)
        g_ref[...] = g
        d_ref[...], mo_ref[...], vo_ref[...] = _adamw_math(w, g, m_ref[...], v_ref[...])

    seven = [pltpu.SemaphoreType.DMA((7,)), pltpu.SemaphoreType.DMA((7,)), pltpu.SemaphoreType.DMA]
    vec = jax.ShapeDtypeStruct((1, D_MODEL), F32)
    return pl.pallas_call(
        body, name="small_sums_and_cond_ctx",
        out_shape=[jax.ShapeDtypeStruct((N_DEV * SMALL_ROWS, D_MODEL), F32), jax.ShapeDtypeStruct((SMALL_ROWS, D_MODEL), F32),
                   vec, vec, vec, vec],
        in_specs=[VMEM] * 5, out_specs=[VMEM] * 6,
        scratch_shapes=[pltpu.VMEM((8, D_MODEL), F32), pltpu.VMEM((N_DEV * 8, D_MODEL), F32)] + seven + seven,
    )(small, ada_w0, c_ctx, m, v)


def _ada_update(cond_t, d_mod, ada_w, m, v):
    n_layers, _, n_cols = ada_w.shape
    tr = ROW_TILE

    def body(c_ref, dm_ref, w_ref, m_ref, v_ref, g_ref, d_ref, mo_ref, vo_ref):
        g = _mm_f32(_silu(c_ref[...]), dm_ref[...])
        g_ref[...] = g
        d_ref[...], mo_ref[...], vo_ref[...] = _adamw_math(w_ref[...], g, m_ref[...], v_ref[...])

    tile = pl.BlockSpec((None, tr, n_cols), lambda l, i: (l, i, 0))
    return pl.pallas_call(
        body, name="ada_update", grid=(n_layers, D_MODEL // tr),
        out_shape=[pltpu.HBM(ada_w.shape, F32)] * 4,
        in_specs=[pl.BlockSpec((tr, 16), lambda l, i: (i, 0)), pl.BlockSpec((None, 16, n_cols), lambda l, i: (l, 0, 0)),
                  tile, tile, tile],
        out_specs=[tile] * 4,
    )(*_pin(cond_t, d_mod, ada_w, m, v))


def _local_step(x2, ctx2, target, mod_mine, mod_ctx, lb_logits, scale_full, w_in_full, late_shards, norm_w, gnorm, final_w):
    row = lambda a: a.reshape(1, -1)
    shift0, scale0, gate0 = (row(a) for a in jnp.split(mod_mine[0], 3))
    shift1, scale1, gate1 = (row(a) for a in jnp.split(mod_mine[1], 3))
    shift_c, scale_c, _ = (row(a) for a in jnp.split(mod_ctx, 3))
    nw0, nw1 = norm_w[0:1], norm_w[1:2]
    scales0 = jnp.concatenate([scale_c, scale0])

    g5, h0 = _ln_mod_matmul(ctx2, x2, nw0, jnp.concatenate([shift_c, shift0]), scales0, w_in_full, "hgrn_in_forward")
    o_raw, full = _gla_forward(g5, lb_logits, late_shards)
    x1, res0 = _hgrn_out_forward(o_raw, g5, x2, gnorm, gate0, full["hgrn_w_out"])
    uz, h1 = _ln_mod_matmul(None, x1, nw1, shift1, scale1, full["pool_w_in"], "pool_in_forward")
    d_x2, loss_part, d_final, d_gate1 = _pool_forward_loss(uz, x1, target, gate1, full["pool_w_grp"], scale_full,
                                                           full["pool_w_out"], final_w)

    d_uz, dw_pool_out, dw_pool_grp, d_pscale = _pool_backward(d_x2, uz, gate1, full["pool_w_grp"], scale_full, full["pool_w_out"])
    d_x1, d_nw1, d_mod1 = _ln_mod_backward(d_uz, full["pool_w_in"], None, x1, nw1, scale1, d_x2, "pool_in_backward")
    dw_pool_in = _weight_grad(h1, d_uz, "pool_in_weight_grad")
    d_o, d_z, dw_hgrn_out, d_gate0, d_gnorm = _hgrn_out_backward(d_x1, o_raw, g5, res0, gnorm, gate0, full["hgrn_w_out"])
    late_grads = {"hgrn_w_out": dw_hgrn_out, "pool_w_in": dw_pool_in, "pool_w_grp": dw_pool_grp, "pool_w_out": dw_pool_out}
    d_g5, d_lb, late_slots = _gla_backward(g5, lb_logits, d_o, d_z, [late_grads[k].astype(BF16) for k in GATHER_LATE])
    dw_hgrn_in, slots0 = _weight_grad_paired(h0, d_g5, "hgrn_in_weight_grad")
    key = GATHER_EARLY[0]
    send_sem, recv_sem, part_thru, slots_thru, token = _scatter_start(dw_hgrn_in, slots0, d_lb, key)
    d_x, d_nw0, d_mod0 = _ln_mod_backward(d_g5, w_in_full, ctx2, x2, nw0 + token[0:1, 0:1], scales0, d_x1, "hgrn_in_backward")
    slots = dict(late_slots)
    pending = (send_sem, recv_sem, part_thru, slots_thru)

    zero = jnp.zeros((1, D_MODEL), F32)
    small = jnp.concatenate([d_mod0[1, 0:2], d_gate0, d_mod1[0, 0:2], d_gate1, d_mod0[0, 0:2], zero, d_nw0, d_nw1, d_gnorm,
                             d_final, d_pscale, d_lb, jnp.broadcast_to(loss_part[0:1, 0:1], (1, D_MODEL)),
                             jnp.zeros((SMALL_ROWS - 17, D_MODEL), F32)], axis=0)
    return {"d_x": d_x, "slots": slots, "pending": pending, "small": small}


def kernel(x, c, ctx, c_ctx, ada_w, ada_b, norm_w, hgrn_w_in, hgrn_lb_logits, hgrn_gnorm_w, hgrn_w_out, pool_w_in, pool_w_grp, pool_scale, pool_w_out, final_norm_w, loss_target, m_c_ctx, m_ada_w, m_ada_b, m_norm_w, m_hgrn_w_in, m_hgrn_lb_logits, m_hgrn_gnorm_w, m_hgrn_w_out, m_pool_w_in, m_pool_w_grp, m_pool_scale, m_pool_w_out, m_final_norm_w, v_c_ctx, v_ada_w, v_ada_b, v_norm_w, v_hgrn_w_in, v_hgrn_lb_logits, v_hgrn_gnorm_w, v_hgrn_w_out, v_pool_w_in, v_pool_w_grp, v_pool_scale, v_pool_w_out, v_final_norm_w):
    xi, yi, ci = _my_place()
    chip = 2 * xi + yi
    dev = 4 * xi + 2 * yi + ci
    ada_cols = ada_w.shape[2]
    lb_cols = hgrn_lb_logits.shape[2]
    ps_cols = pool_scale.shape[1]
    row = lambda a: a.reshape(1, -1)

    def chip_cols(a, n):
        return lax.dynamic_slice_in_dim(a, chip * n, n, axis=a.ndim - 1)

    def from_chips(g, rows_per_dev, take):
        return jnp.concatenate([g[2 * s * rows_per_dev:2 * s * rows_per_dev + take] for s in range(N_CHIPS)], axis=1)

    first = jnp.concatenate([jnp.broadcast_to(c, (8, D_MODEL)), jnp.pad(hgrn_lb_logits[0], ((0, 6), (0, 0))),
                             jnp.pad(pool_scale, ((0, 7), (0, 0)))], axis=1)
    shards = {"hgrn_w_in": hgrn_w_in[0], "hgrn_w_out": hgrn_w_out[0], "pool_w_in": pool_w_in[0],
              "pool_w_grp": pool_w_grp[0], "pool_w_out": pool_w_out[0]}
    first_all, parts_all, w_in_full = _prologue(first, row(c_ctx), ada_w, chip_cols(ada_b, ada_cols),
                                                shards[GATHER_EARLY[0]].astype(BF16))
    cond_all = first_all[::8, :D_MODEL]
    lb_logits = from_chips(first_all[:, D_MODEL:D_MODEL + lb_cols], 8, 2)
    scale_full = from_chips(first_all[:, D_MODEL + lb_cols:], 8, 1)
    cond_rows = jnp.concatenate([cond_all, row(c_ctx), jnp.zeros((7, D_MODEL), F32)], axis=0)
    mod_all = from_chips(parts_all, 32, 32).reshape(2, 16, 3 * D_MODEL)
    mod_mine = lax.dynamic_index_in_dim(mod_all, dev, axis=1, keepdims=False)

    loc = _local_step(x[0], ctx[0], loss_target[0], mod_mine, mod_all[0, 8], lb_logits, scale_full, w_in_full,
                      [shards[k].astype(BF16) for k in GATHER_LATE], norm_w, hgrn_gnorm_w, row(final_norm_w))

    small_all, sums, g_c, d_c, m_c, v_c = _small_sums_and_cond_ctx(loc["small"], ada_w[0], row(c_ctx), row(m_c_ctx), row(v_c_ctx))
    loss = sums[16, 0]

    def reduce_scattered(slots, names, name):
        return dict(zip(names, _sum_and_exchange([slots[k] for k in names], names, name)))

    big_grads = reduce_scattered(loc["slots"], GATHER_LATE, "exchange_halves_late")

    out = {}

    def update(name, w, g, m, v):
        shape = w.shape
        w2, g2, m2, v2 = (a.reshape(-1, shape[-1]) for a in (w, g, m, v))
        d, mn, vn = _adamw(w2, g2, m2, v2, "adamw_" + name)
        out[name] = tuple(a.reshape(shape) for a in (g2, d, mn, vn))

    moments = {"hgrn_w_in": (m_hgrn_w_in, v_hgrn_w_in), "hgrn_w_out": (m_hgrn_w_out, v_hgrn_w_out),
               "pool_w_in": (m_pool_w_in, v_pool_w_in), "pool_w_grp": (m_pool_w_grp, v_pool_w_grp),
               "pool_w_out": (m_pool_w_out, v_pool_w_out)}
    weights = {"hgrn_w_in": hgrn_w_in, "hgrn_w_out": hgrn_w_out, "pool_w_in": pool_w_in, "pool_w_grp": pool_w_grp,
               "pool_w_out": pool_w_out}
    for k in GATHER_LATE:
        update(k, weights[k], big_grads[k], *moments[k])

    g_ada_b = jnp.stack([(sums[0:3] + sums[6:9]).reshape(-1), sums[3:6].reshape(-1)])
    update("ada_b", ada_b, g_ada_b, m_ada_b, v_ada_b)
    update("norm_w", norm_w, sums[9:11], m_norm_w, v_norm_w)
    update("hgrn_gnorm_w", hgrn_gnorm_w, sums[11:12], m_hgrn_gnorm_w, v_hgrn_gnorm_w)
    update("final_norm_w", row(final_norm_w), sums[12:13], row(m_final_norm_w), row(v_final_norm_w))
    update("pool_scale", pool_scale, chip_cols(sums[13:14], ps_cols), m_pool_scale, v_pool_scale)
    update("hgrn_lb_logits", hgrn_lb_logits, chip_cols(sums[14:16], lb_cols)[None], m_hgrn_lb_logits, v_hgrn_lb_logits)

    per_dev = small_all.reshape(N_DEV, SMALL_ROWS, D_MODEL)
    pad7 = jnp.zeros((7, 3 * D_MODEL), F32)
    dm0 = jnp.concatenate([per_dev[:, 0:3].reshape(N_DEV, -1), sums[6:9].reshape(1, -1), pad7], axis=0)
    dm1 = jnp.concatenate([per_dev[:, 3:6].reshape(N_DEV, -1), jnp.zeros((8, 3 * D_MODEL), F32)], axis=0)
    d_mod = chip_cols(jnp.stack([dm0, dm1]), ada_cols)
    out["ada_w"] = _ada_update(cond_rows.T, d_mod, ada_w, m_ada_w, v_ada_w)

    out["c_ctx"] = tuple(a.reshape(-1) for a in (g_c, d_c, m_c, v_c))
    out["final_norm_w"] = tuple(a.reshape(-1) for a in out["final_norm_w"])

    done = [g_c, out["ada_w"][1]] + [out[k][1] for k in GATHER_LATE]
    key = GATHER_EARLY[0]
    _, early = _scatter_wait(*loc["pending"], done, key)
    big_grads = reduce_scattered({key: early}, GATHER_EARLY, "exchange_halves_early")
    for k in GATHER_EARLY:
        update(k, weights[k], big_grads[k], *moments[k])

    names = ["c_ctx", "ada_w", "ada_b", "norm_w", "hgrn_w_in", "hgrn_lb_logits", "hgrn_gnorm_w", "hgrn_w_out", "pool_w_in",
             "pool_w_grp", "pool_scale", "pool_w_out", "final_norm_w"]
    return (loss, loc["d_x"][None], *[out[k][0] for k in names], *[out[k][1] for k in names], *[out[k][2] for k in names],
            *[out[k][3] for k in names])
```

```python
import functools

import numpy as np
import jax
import jax.numpy as jnp
from jax import lax
from jax.experimental import pallas as pl
from jax.experimental.pallas import tpu as pltpu

F32 = jnp.float32
BF16 = jnp.bfloat16

D_MODEL = 1024
SEQ = 2048
CTX_LEN = 256
ROWS_ALL = CTX_LEN + SEQ
HEADS = 8
HEAD_DIM = 128
CHUNK = 64
N_CTX_CHUNKS = CTX_LEN // CHUNK
N_LAT_CHUNKS = SEQ // CHUNK
N_CHUNKS = N_CTX_CHUNKS + N_LAT_CHUNKS
GRID_W = 64
POOL_WINDOWS = (2, 4, 8, 16)
POOL_GROUPS = 4
POOL_GROUP_DIM = 256
HGRN_SECTIONS = 5
POOL_SECTIONS = 2
EPS = 1e-6
N_DEV = 8
N_CHIPS = 4
ROW_TILE = 256
POOL_TILE = 512
SMALL_ROWS = 24

ADAM_LR = 0.001
ADAM_B1 = 0.9
ADAM_B2 = 0.999
ADAM_EPS = 1e-08
ADAM_WD = 0.01
ADAM_STEP = 10

MESH = pl.DeviceIdType.MESH
MIB = 1 << 20
ANY = pl.BlockSpec(memory_space=pl.ANY)
VMEM = pl.BlockSpec(memory_space=pltpu.VMEM)


def _params(vmem_mib=None):
    if vmem_mib is None:
        return pltpu.CompilerParams()
    return pltpu.CompilerParams(vmem_limit_bytes=vmem_mib * MIB)


def _pin(*operands):
    return [pltpu.with_memory_space_constraint(a, pltpu.HBM) if a.size * a.dtype.itemsize >= MIB else a for a in operands]


def _sig(a):
    return 0.5 * jnp.tanh(0.5 * a) + 0.5


def _silu(a):
    return a * _sig(a)


def _dsilu(a):
    s = _sig(a)
    return s * (1.0 + a * (1.0 - s))


def _mm(a, b):
    return jnp.dot(a.astype(BF16), b.astype(BF16), preferred_element_type=F32)


def _mm_nt(a, b):
    return lax.dot_general(a.astype(BF16), b.astype(BF16), (((1,), (1,)), ((), ())), preferred_element_type=F32)


def _mm_tn(a, b):
    return lax.dot_general(a.astype(BF16), b.astype(BF16), (((0,), (0,)), ((), ())), preferred_element_type=F32)


def _split2(a):
    hi = a.astype(BF16)
    lo = (a - hi.astype(F32)).astype(BF16)
    return hi, lo


def _mm_exact_lhs(m_bf, a):
    hi, lo = _split2(a)
    return jnp.dot(m_bf, hi, preferred_element_type=F32) + jnp.dot(m_bf, lo, preferred_element_type=F32)


def _mm_f32(a, b):
    ah, al = _split2(a)
    bh, bl = _split2(b)
    return (jnp.dot(ah, bh, preferred_element_type=F32) + jnp.dot(al, bh, preferred_element_type=F32)
            + jnp.dot(ah, bl, preferred_element_type=F32))


def _my_place():
    return lax.axis_index("x"), lax.axis_index("y"), lax.axis_index("c")


def _small_gather(x_ref, out_ref, send_sems, recv_sems, local_sem):
    m_per = x_ref.shape[0]
    x, y, c = _my_place()
    me, sibling = (x, y, c), (x, y, 1 - c)
    chips = [(1 - x, y), (x, 1 - y), (1 - x, 1 - y)]

    def rows(px, py, pc):
        return out_ref.at[pl.ds((4 * px + 2 * py + pc) * m_per, m_per), :]

    def copy(k, block, to, src=None):
        return pltpu.make_async_remote_copy(
            src_ref=rows(*block) if src is None else src, dst_ref=rows(*block),
            send_sem=send_sems.at[k], recv_sem=recv_sems.at[k], device_id=to, device_id_type=MESH)

    def mine():
        return pltpu.make_async_copy(x_ref, rows(*me), local_sem)

    def first():
        return [copy(0, me, sibling, src=x_ref)] + [copy(1 + j, me, (*chip, c), src=x_ref) for j, chip in enumerate(chips)]

    def start():
        mine().start()
        for cp in first():
            cp.start()

    def finish():
        passed = [copy(4 + j, (*chip, c), sibling) for j, chip in enumerate(chips)]
        for j, chip in enumerate(chips):
            copy(1 + j, (*chip, c), me).wait_recv()
            passed[j].start()
        copy(0, sibling, me).wait_recv()
        for j, chip in enumerate(chips):
            copy(4 + j, (*chip, 1 - c), me).wait_recv()
        for cp in first() + passed:
            cp.wait_send()
        mine().wait()

    return start, finish


W_SPECS = {
    "hgrn_w_in": ((D_MODEL, 5 * D_MODEL), (1, 1280, 0, 512)),
    "hgrn_w_out": ((D_MODEL, D_MODEL), (0, 256, 0, 128)),
    "pool_w_in": ((D_MODEL, 2 * D_MODEL), (1, 512, 0, 512)),
    "pool_w_grp": ((POOL_GROUPS, POOL_GROUP_DIM, POOL_GROUP_DIM), (1, 64, 1, 32)),
    "pool_w_out": ((D_MODEL, D_MODEL), (0, 256, 0, 128)),
}

def _al(v, m):
    return pl.multiple_of(v, m)


def _region(ref, spec, chip, half):
    ca, cn, ha, hn = spec
    idx = [slice(None)] * len(ref.shape)
    if ca == ha:
        if half is None:
            idx[ca] = pl.ds(_al(chip * cn, cn), cn)
        else:
            idx[ca] = pl.ds(_al(chip * cn + half * hn, hn), hn)
    else:
        idx[ca] = pl.ds(_al(chip * cn, cn), cn)
        if half is not None:
            idx[ha] = pl.ds(_al(half * hn, hn), hn)
    return ref.at[tuple(idx)]


def _half_of(ref, spec, half):
    _, _, ha, hn = spec
    idx = [slice(None)] * len(ref.shape)
    idx[ha] = pl.ds(_al(half * hn, hn), hn)
    return ref.at[tuple(idx)]


def _half_shape(name):
    full, (ca, cn, ha, hn) = W_SPECS[name]
    shp = list(full)
    shp[ca] = cn
    shp[ha] = hn
    return tuple(shp)


def _gather_two_level(names, sh, land, full, send_sems, recv_sems, local_sems):
    n = len(names)
    specs = [W_SPECS[k][1] for k in names]
    x, y, c = _my_place()
    chip_me = 2 * x + y
    sibling = (x, y, 1 - c)
    chips = [(1 - x, y), (x, 1 - y), (1 - x, 1 - y)]

    def remote(k, src, dst, to):
        return pltpu.make_async_remote_copy(src_ref=src, dst_ref=dst, send_sem=send_sems.at[k], recv_sem=recv_sems.at[k],
                                            device_id=to, device_id_type=MESH)

    def three_halves(a):
        ca, cn, ha, hn = specs[a]
        idx = [slice(None)] * len(land[a].shape)
        if ca == ha:
            idx[ca] = pl.ds(0, 3 * hn)
        else:
            idx[ca], idx[ha] = pl.ds(0, 3 * cn), pl.ds(0, hn)
        return land[a].at[tuple(idx)]

    def own(a):
        return pltpu.make_async_copy(sh[a], _region(land[a], specs[a], chip_me, None), local_sems.at[a])

    def out(a):
        return pltpu.make_async_copy(land[a], full[a], local_sems.at[a])

    def start():
        for a in range(n):
            own(a).start()
            for px, py in chips:
                remote(a, _half_of(sh[a], specs[a], c), _region(land[a], specs[a], chip_me, c), (px, py, c)).start()

    def forward():
        for a in range(n):
            remote(a, three_halves(a), three_halves(a), sibling).wait_recv()
            for px, py in chips:
                landed = _region(land[a], specs[a], 2 * px + py, c)
                remote(n + a, landed, landed, sibling).start()

    def finish():
        for a in range(n):
            remote(n + a, three_halves(a), three_halves(a), sibling).wait_recv()
            remote(a, three_halves(a), three_halves(a), sibling).wait_send()
            remote(n + a, three_halves(a), three_halves(a), sibling).wait_send()
            own(a).wait()
        for a in range(n):
            out(a).start()
        for a in range(n):
            out(a).wait()

    return start, forward, finish


def _scatter_direct(names, part, slots, send_sems, recv_sems, local_sems):
    specs = [W_SPECS[k][1] for k in names]
    x, y, c = _my_place()
    dev_me = 4 * x + 2 * y + c

    def local(a):
        return pltpu.make_async_copy(_region(part[a], specs[a], 2 * x + y, c), slots[a].at[dev_me], local_sems.at[a])

    def start():
        for a in range(len(names)):
            local(a).start()
            for flip in range(1, N_DEV):
                tx = 1 - x if flip >> 2 else x
                ty = 1 - y if (flip >> 1) & 1 else y
                tc = 1 - c if flip & 1 else c
                pltpu.make_async_remote_copy(src_ref=_region(part[a], specs[a], 2 * tx + ty, tc), dst_ref=slots[a].at[dev_me],
                                             send_sem=send_sems.at[a], recv_sem=recv_sems.at[a], device_id=(tx, ty, tc),
                                             device_id_type=MESH).start()

    def wait():
        for a in range(len(names)):
            seven = slots[a].at[pl.ds(0, N_DEV - 1)]
            pltpu.make_async_remote_copy(src_ref=seven, dst_ref=seven, send_sem=send_sems.at[a], recv_sem=recv_sems.at[a],
                                         device_id=(x, y, c), device_id_type=MESH).wait()
            local(a).wait()

    return start, wait


HBM_SPEC = pl.BlockSpec(memory_space=pltpu.HBM)
SEM_SPEC = pl.BlockSpec(memory_space=pltpu.SEMAPHORE)
SPLIT_EFFECT = pltpu.SideEffectType.DATAFLOW_SIDE_EFFECTING


def _scatter_start(part, slots, after, name_key):
    _, cols, _, _ = W_SPECS[name_key][1]

    def body(part_ref, slots_ref, after_ref, send_sem, recv_sem, part_thru, slots_thru, token):
        x, y, c = _my_place()
        for tx, ty in ((1 - x, y), (x, 1 - y), (1 - x, 1 - y)):
            src = part_ref.at[:, pl.ds(_al((2 * tx + ty) * cols, cols), cols)]
            pltpu.make_async_remote_copy(src_ref=src, dst_ref=slots_ref.at[2 * x + y], send_sem=send_sem, recv_sem=recv_sem,
                                         device_id=(tx, ty, c), device_id_type=MESH).start()
        token[...] = jnp.zeros_like(token)

    return pl.pallas_call(
        body, name="scatter_start_" + name_key,
        out_shape=(pltpu.SemaphoreType.DMA(()), pltpu.SemaphoreType.DMA(()), pltpu.HBM(part.shape, part.dtype),
                   pltpu.HBM(slots.shape, slots.dtype), jax.ShapeDtypeStruct((8, 128), F32)),
        in_specs=(HBM_SPEC, HBM_SPEC, ANY), out_specs=(SEM_SPEC, SEM_SPEC, HBM_SPEC, HBM_SPEC, VMEM),
        input_output_aliases={0: 2, 1: 3},
        compiler_params=pltpu.CompilerParams(has_side_effects=SPLIT_EFFECT),
    )(pltpu.with_memory_space_constraint(part, pltpu.HBM), pltpu.with_memory_space_constraint(slots, pltpu.HBM), after)


def _scatter_wait(send_sem, recv_sem, part_thru, slots_thru, after, name_key):
    def body(part_ref, slots_ref, send_sem, recv_sem, *rest):
        x, y, c = _my_place()
        landed = slots_ref.at[pl.ds(0, N_CHIPS - 1)]
        copy = pltpu.make_async_remote_copy(src_ref=landed, dst_ref=landed, send_sem=send_sem, recv_sem=recv_sem,
                                            device_id=(x, y, c), device_id_type=MESH)
        copy.wait_send()
        copy.wait_recv()

    return pl.pallas_call(
        body, name="scatter_wait_" + name_key,
        out_shape=(pltpu.HBM(part_thru.shape, part_thru.dtype), pltpu.HBM(slots_thru.shape, slots_thru.dtype)),
        in_specs=(HBM_SPEC, HBM_SPEC, SEM_SPEC, SEM_SPEC) + (ANY,) * len(after), out_specs=(HBM_SPEC, HBM_SPEC),
        input_output_aliases={0: 0, 1: 1},
        compiler_params=pltpu.CompilerParams(has_side_effects=SPLIT_EFFECT),
    )(part_thru, slots_thru, send_sem, recv_sem, *after)


def _comm_sems(n):
    return [pltpu.SemaphoreType.DMA((n,)), pltpu.SemaphoreType.DMA((n,)), pltpu.SemaphoreType.DMA((n,))]


GATHER_EARLY = ("hgrn_w_in",)
GATHER_LATE = ("hgrn_w_out", "pool_w_in", "pool_w_grp", "pool_w_out")


def _weight_gather(sh, full, spec, send_sems, recv_sems, local_sem):
    x, y, c = _my_place()
    chip_me = 2 * x + y
    sibling = (x, y, 1 - c)
    chips = [(1 - x, y), (x, 1 - y), (1 - x, 1 - y)]

    def remote(k, src, dst, to):
        return pltpu.make_async_remote_copy(src_ref=src, dst_ref=dst, send_sem=send_sems.at[k], recv_sem=recv_sems.at[k],
                                            device_id=to, device_id_type=MESH)

    def own():
        return pltpu.make_async_copy(sh, _region(full, spec, chip_me, None), local_sem)

    def direct():
        return [remote(j, _half_of(sh, spec, c), _region(full, spec, chip_me, c), (px, py, c)) for j, (px, py) in enumerate(chips)]

    def start():
        own().start()
        for cp in direct():
            cp.start()

    def finish():
        passed = []
        for j, (px, py) in enumerate(chips):
            landed = _region(full, spec, 2 * px + py, c)
            remote(j, landed, landed, (px, py, c)).wait_recv()
            passed.append(remote(3 + j, landed, landed, sibling))
            passed[-1].start()
        for j, (px, py) in enumerate(chips):
            other = _region(full, spec, 2 * px + py, 1 - c)
            remote(3 + j, other, other, sibling).wait_recv()
        for cp in direct() + passed:
            cp.wait_send()
        own().wait()

    return start, finish


def _prologue(first, c_ctx, ada_w, ada_b_cols, w_shard):
    n_layers, _, n_cols = ada_w.shape
    key = GATHER_EARLY[0]
    spec = W_SPECS[key][1]
    m_first = first.shape[0]

    def body(first_ref, cctx_ref, adaw_ref, adab_ref, sh_ref, first_all, parts_all, full_ref, parts_scr,
             s1, r1, l1, s2, r2, l2, ws, wr, wl):
        start_first, finish_first = _small_gather(first_ref, first_all, s1, r1, l1)
        start_parts, finish_parts = _small_gather(parts_scr, parts_all, s2, r2, l2)
        start_weight, finish_weight = _weight_gather(sh_ref, full_ref, spec, ws, wr, wl)
        start_first()
        start_weight()
        finish_first()
        cond = jnp.concatenate([first_all[m_first * d:m_first * d + 1, 0:D_MODEL] for d in range(N_DEV)]
                               + [cctx_ref[...], jnp.zeros((16 - N_DEV - 1, D_MODEL), F32)], axis=0)
        act = _silu(cond)
        for i in range(n_layers):
            parts_scr[16 * i:16 * (i + 1), :] = _mm_f32(act, adaw_ref[i]) + adab_ref[i]
        start_parts()
        finish_parts()
        finish_weight()

    seven = [pltpu.SemaphoreType.DMA((7,)), pltpu.SemaphoreType.DMA((7,)), pltpu.SemaphoreType.DMA]
    return pl.pallas_call(
        body, name="prologue",
        out_shape=[jax.ShapeDtypeStruct((N_DEV * m_first, first.shape[1]), F32),
                   jax.ShapeDtypeStruct((N_DEV * 16 * n_layers, n_cols), F32), jax.ShapeDtypeStruct(W_SPECS[key][0], BF16)],
        in_specs=[VMEM] * 5, out_specs=[VMEM] * 3,
        scratch_shapes=[pltpu.VMEM((16 * n_layers, n_cols), F32)] + seven + seven
        + [pltpu.SemaphoreType.DMA((6,)), pltpu.SemaphoreType.DMA((6,)), pltpu.SemaphoreType.DMA],
        compiler_params=_params(48),
    )(first, c_ctx, ada_w, ada_b_cols.reshape(n_layers, 1, n_cols), w_shard)


def _slot_shapes(names):
    return [jax.ShapeDtypeStruct((N_DEV,) + _half_shape(k), BF16) for k in names]


def _sum_and_exchange(slots, names, name):
    n = len(names)
    specs = [W_SPECS[k][1] for k in names]

    def shard_shape(k):
        shp = list(_half_shape(k))
        shp[W_SPECS[k][1][2]] *= 2
        return tuple(shp)

    def body(*refs):
        slot, out, half = refs[:n], refs[n:2 * n], refs[2 * n:3 * n]
        send_sems, recv_sems, local_sems = refs[3 * n:]
        x, y, c = _my_place()
        sibling = (x, y, 1 - c)

        def remote(a, src, dst):
            return pltpu.make_async_remote_copy(src_ref=src, dst_ref=dst, send_sem=send_sems.at[a], recv_sem=recv_sems.at[a],
                                                device_id=sibling, device_id_type=MESH)

        local = [pltpu.make_async_copy(half[a], _half_of(out[a], specs[a], c), local_sems.at[a]) for a in range(n)]
        for a in range(n):
            acc = slot[a][0].astype(F32)
            for d in range(1, slot[a].shape[0]):
                acc = acc + slot[a][d].astype(F32)
            half[a][...] = acc
            local[a].start()
            remote(a, half[a], _half_of(out[a], specs[a], c)).start()
        for a in range(n):
            theirs = _half_of(out[a], specs[a], 1 - c)
            remote(a, theirs, theirs).wait_recv()
            remote(a, half[a], half[a]).wait_send()
            local[a].wait()

    return pl.pallas_call(
        body, name=name,
        out_shape=[jax.ShapeDtypeStruct(shard_shape(k), F32) for k in names],
        in_specs=[VMEM] * n, out_specs=[VMEM] * n,
        scratch_shapes=[pltpu.VMEM(_half_shape(k), F32) for k in names]
        + [pltpu.SemaphoreType.DMA((n,)), pltpu.SemaphoreType.DMA((n,)), pltpu.SemaphoreType.DMA((n,))],
        compiler_params=_params(40),
    )(*slots)


def _ln_mod_matmul(ctx_tile, xin, nw, shift, scale, w, name):
    n_mod = shift.shape[0]
    skip = n_mod - 1
    tm = ROW_TILE if skip else 2 * ROW_TILE
    rows = xin.shape[0] + skip * tm
    n_sec = w.shape[1] // D_MODEL

    def body(*refs):
        c_ref = refs[0] if skip else None
        x_ref, nw_ref, sh_ref, sc_ref, w_ref, g_ref, h_ref = refs[skip:]
        xv = jnp.where(pl.program_id(0) == 0, c_ref[...], x_ref[...]) if skip else x_ref[...]
        r = lax.rsqrt(jnp.mean(xv * xv, axis=-1, keepdims=True) + EPS)
        h_f32 = (xv * r * nw_ref[...]) * (1.0 + sc_ref[...]) + sh_ref[...]
        h = h_f32.astype(BF16)
        h_ref[...] = h
        for k in range(n_sec):
            g_ref[k] = jnp.dot(h, w_ref[:, k * D_MODEL:(k + 1) * D_MODEL], preferred_element_type=F32)

    mod_spec = pl.BlockSpec((None, 1, D_MODEL), lambda i: (jnp.minimum(i, n_mod - 1), 0, 0))
    return pl.pallas_call(
        body, name=name, grid=(rows // tm,),
        out_shape=[pltpu.HBM((n_sec, rows, D_MODEL), F32), pltpu.HBM((rows, D_MODEL), BF16)],
        in_specs=[pl.BlockSpec((tm, D_MODEL), lambda i: (0, 0))] * skip
        + [pl.BlockSpec((tm, D_MODEL), lambda i: (jnp.maximum(i - skip, 0), 0)),
           pl.BlockSpec((1, D_MODEL), lambda i: (0, 0)),
           mod_spec, mod_spec,
           pl.BlockSpec((D_MODEL, n_sec * D_MODEL), lambda i: (0, 0))],
        out_specs=[pl.BlockSpec((n_sec, tm, D_MODEL), lambda i: (0, i, 0)),
                   pl.BlockSpec((tm, D_MODEL), lambda i: (i, 0))],
        compiler_params=_params(48),
    )(*_pin(*([ctx_tile] * skip), xin, nw, shift.reshape(n_mod, 1, D_MODEL), scale.reshape(n_mod, 1, D_MODEL), w))


def _gates(fpre, lb):
    sg = _sig(fpre)
    f = lb + (1.0 - lb) * sg
    return sg, f, 1.0 - f, jnp.log(f)


G_SPEC = lambda sec: pl.BlockSpec((None, ROWS_ALL, HEAD_DIM), lambda h, sec=sec: (sec, 0, h))


GROUP = 2 * CHUNK


def _group_masks():
    rid, cid = np.arange(GROUP)[:, None], np.arange(GROUP)[None, :]
    same = (rid >= CHUNK) == (cid >= CHUNK)
    sums, back = [], []
    for rev in (False, True):
        causal = (cid >= rid) if rev else (cid <= rid)
        anti = (cid <= rid) if rev else (cid >= rid)
        sums.append(same & causal)
        back.append(np.concatenate([same & anti, same & ~anti], axis=1))
    sums, back = np.stack(sums).astype(np.float32), np.stack(back).astype(np.float32)
    return jnp.asarray(sums, BF16), jnp.asarray(back, BF16), jnp.asarray(sums, F32)


MASK_SPECS = [pl.BlockSpec((2, GROUP, GROUP), lambda h: (0, 0, 0)), pl.BlockSpec((2, GROUP, 2 * GROUP), lambda h: (0, 0, 0)),
              pl.BlockSpec((2, GROUP, GROUP), lambda h: (0, 0, 0))]


def _group_sum(m_bf, a):
    hi, lo = _split2(a)
    r = jnp.dot(m_bf, jnp.concatenate([hi, lo], axis=1), preferred_element_type=F32)
    return r[:, :HEAD_DIM] + r[:, HEAD_DIM:]


def _chunk_row(a, pos):
    return jnp.concatenate([jnp.broadcast_to(a[c * CHUNK + pos:c * CHUNK + pos + 1, :], (CHUNK, HEAD_DIM)) for c in range(2)], axis=0)


def _by_chunk(a, second):
    return jnp.concatenate([jnp.where(second, 0.0, a), jnp.where(second, a, 0.0)], axis=1)


def _own_block(r):
    return jnp.concatenate([r[0:CHUNK, 0:HEAD_DIM], r[CHUNK:GROUP, HEAD_DIM:2 * HEAD_DIM]], axis=0)


def _scan_step_of(row_chunk, rev, latent):
    if not rev:
        return row_chunk
    return (N_CHUNKS + N_CTX_CHUNKS - 1 - row_chunk) if latent else (N_CTX_CHUNKS - 1 - row_chunk)


def _group_rows(i, j, per_step, latent):
    base = CTX_LEN if latent else 0
    return pl.multiple_of(base + (i * per_step + j) * GROUP, GROUP)


GROUPS_PER_STEP = 16
GROUPS_PER_GRAD_STEP = 4


def _gla_forward(g5, lb_logits, late_shards):
    q_scale = HEAD_DIM ** -0.5
    per_lat, per_ctx = GROUPS_PER_STEP, min(GROUPS_PER_STEP, CTX_LEN // GROUP)
    n_late = len(GATHER_LATE)

    def body(ff_ref, fb_ref, v_ref, q_ref, lg_ref, sums_ref, _, keep_ref, *rest):
        shard_refs, o_ref, full_refs = rest[:n_late], rest[n_late], rest[n_late + 1:2 * n_late + 1]
        st_ref, decay_ref, qt_ref = rest[2 * n_late + 1:2 * n_late + 4]
        land_refs = rest[2 * n_late + 4:3 * n_late + 4]
        start_gather, forward_gather, finish_gather = _gather_two_level(GATHER_LATE, shard_refs, land_refs, full_refs,
                                                                        *rest[3 * n_late + 4:])

        @pl.when(pl.program_id(0) == 0)
        def _():
            start_gather()

        @pl.when(pl.program_id(0) == HEADS - 1)
        def _():
            forward_gather()

        second = lax.broadcasted_iota(jnp.int32, (GROUP, HEAD_DIM), 0) >= CHUNK
        for rev in (False, True):
            f_ref = fb_ref if rev else ff_ref
            lb = _sig(lg_ref[1:2, :] if rev else lg_ref[0:1, :])
            d = 1 if rev else 0
            last = 0 if rev else CHUNK - 1
            mid = CHUNK // 2 if rev else CHUNK // 2 - 1

            def local_step(i, carry, latent, per):
                r0s = [_group_rows(i, j, per, latent) for j in range(per)]
                rows = [pl.ds(r0, GROUP) for r0 in r0s]
                gates = [_gates(f_ref[r, :], lb) for r in rows]
                vs = [v_ref[r, :] for r in rows]
                bs = [_group_sum(sums_ref[d], g[3]) for g in gates]
                bls = [_chunk_row(b, last) for b in bs]
                ups = [_mm_tn(v, _by_chunk(g[2] * jnp.exp(bl - b), second)) for v, g, b, bl in zip(vs, gates, bs, bls)]
                if latent:
                    qs = [_silu(q_ref[r, :]) * q_scale for r in rows]
                    bms = [_chunk_row(b, mid) for b in bs]
                    a_s = [_mm_nt(q * jnp.exp(b - bm), g[2] * jnp.exp(bm - b)) for q, g, b, bm in zip(qs, gates, bs, bms)]
                    outs = [_mm(a * keep_ref[d], v) for a, v in zip(a_s, vs)]
                for j in range(per):
                    for c in range(2):
                        step = _scan_step_of(r0s[j] // CHUNK + c, rev, latent)
                        decay_ref[step] = jnp.exp(bls[j][c * CHUNK:c * CHUNK + 1, :])
                        st_ref[step] = ups[j][:, c * HEAD_DIM:(c + 1) * HEAD_DIM]
                    if latent:
                        orow = pl.ds(pl.multiple_of(r0s[j] - CTX_LEN, GROUP), GROUP)
                        qt_ref[orow, :] = (qs[j] * jnp.exp(bs[j])).astype(BF16)
                        if rev:
                            o_ref[orow, :] += outs[j]
                        else:
                            o_ref[orow, :] = outs[j]
                return carry

            lax.fori_loop(0, CTX_LEN // (per_ctx * GROUP), functools.partial(local_step, latent=False, per=per_ctx), 0)
            lax.fori_loop(0, SEQ // (per_lat * GROUP), functools.partial(local_step, latent=True, per=per_lat), 0)

            def scan_step(t, st):
                update = st_ref[t]
                st_ref[t] = st
                return st * decay_ref[t] + update

            lax.fori_loop(0, N_CHUNKS, scan_step, jnp.zeros((HEAD_DIM, HEAD_DIM), F32), unroll=2)

            def inter_step(i, carry):
                r0s = [_group_rows(i, j, per_lat, True) for j in range(per_lat)]
                orows = [pl.ds(pl.multiple_of(r0 - CTX_LEN, GROUP), GROUP) for r0 in r0s]
                states = [jnp.concatenate([st_ref[_scan_step_of(r0 // CHUNK + c, rev, True)] for c in range(2)], axis=0)
                          for r0 in r0s]
                prods = [lax.dot_general(qt_ref[orow, :], s.astype(BF16), (((1,), (1,)), ((), ())), preferred_element_type=F32)
                         for orow, s in zip(orows, states)]
                for orow, r in zip(orows, prods):
                    o_ref[orow, :] += _own_block(r)
                return carry

            lax.fori_loop(0, SEQ // (per_lat * GROUP), inter_step, 0)

        @pl.when(pl.program_id(0) == HEADS - 1)
        def _():
            finish_gather()

    outs = pl.pallas_call(
        body, name="gla_forward", grid=(HEADS,),
        out_shape=[pltpu.HBM((SEQ, D_MODEL), F32)] + [jax.ShapeDtypeStruct(W_SPECS[k][0], BF16) for k in GATHER_LATE],
        in_specs=[G_SPEC(0), G_SPEC(1), G_SPEC(2), G_SPEC(3), pl.BlockSpec((2, HEAD_DIM), lambda h: (0, h))] + MASK_SPECS
        + [ANY] * n_late,
        out_specs=[pl.BlockSpec((SEQ, HEAD_DIM), lambda h: (0, h))] + [ANY] * n_late,
        scratch_shapes=[pltpu.VMEM((N_CHUNKS, HEAD_DIM, HEAD_DIM), F32), pltpu.VMEM((N_CHUNKS, 1, HEAD_DIM), F32),
                        pltpu.VMEM((SEQ, HEAD_DIM), BF16)] + [pltpu.VMEM(W_SPECS[k][0], BF16) for k in GATHER_LATE]
        + [pltpu.SemaphoreType.DMA((2 * n_late,)), pltpu.SemaphoreType.DMA((2 * n_late,)), pltpu.SemaphoreType.DMA((n_late,))],
        compiler_params=_params(48),
    )(*_pin(g5, g5, g5, g5, lb_logits), *_group_masks(), *late_shards)
    return outs[0], dict(zip(GATHER_LATE, outs[1:]))


def _gla_backward(g5, lb_logits, d_o, d_z, late_parts):
    q_scale = HEAD_DIM ** -0.5
    per_lat, per_ctx = GROUPS_PER_STEP, min(GROUPS_PER_STEP, CTX_LEN // GROUP)
    n_late = len(GATHER_LATE)

    def body(ff_ref, fb_ref, v_ref, q_ref, lg_ref, do_ref, dz_ref, sums_ref, back_ref, keep_ref, *rest):
        part_refs, (dg_ref, dlg_ref), slot_refs = rest[:n_late], rest[n_late:n_late + 2], rest[n_late + 2:2 * n_late + 2]
        st_ref, dst_ref, decay_ref, both_ref = rest[2 * n_late + 2:2 * n_late + 6]
        start_scatter, wait_scatter = _scatter_direct(GATHER_LATE, part_refs, slot_refs, *rest[2 * n_late + 6:])

        @pl.when(pl.program_id(0) == 0)
        def _():
            start_scatter()

        dg_ref[3, 0:CTX_LEN, :] = jnp.zeros((CTX_LEN, HEAD_DIM), BF16)
        dg_ref[4, 0:CTX_LEN, :] = jnp.zeros((CTX_LEN, HEAD_DIM), BF16)
        dg_ref[4, CTX_LEN:ROWS_ALL, :] = dz_ref[...]
        second = lax.broadcasted_iota(jnp.int32, (GROUP, HEAD_DIM), 0) >= CHUNK
        for rev in (False, True):
            d = 1 if rev else 0
            f_ref = fb_ref if rev else ff_ref
            lb = _sig(lg_ref[d:d + 1, :])
            last = 0 if rev else CHUNK - 1
            mid = CHUNK // 2 if rev else CHUNK // 2 - 1

            def local_step(i, carry, latent, per):
                r0s = [_group_rows(i, j, per, latent) for j in range(per)]
                rows = [pl.ds(r0, GROUP) for r0 in r0s]
                gates = [_gates(f_ref[r, :], lb) for r in rows]
                bs = [_group_sum(sums_ref[d], g[3]) for g in gates]
                bls = [_chunk_row(b, last) for b in bs]
                ups = [_mm_tn(v_ref[r, :], _by_chunk(g[2] * jnp.exp(bl - b), second)) for r, g, b, bl in zip(rows, gates, bs, bls)]
                if latent:
                    orows = [pl.ds(pl.multiple_of(r0 - CTX_LEN, GROUP), GROUP) for r0 in r0s]
                    d_ups = [_mm_tn(do_ref[orow, :], _by_chunk(_silu(q_ref[r, :]) * q_scale * jnp.exp(b), second))
                             for orow, r, b in zip(orows, rows, bs)]
                for j in range(per):
                    for c in range(2):
                        step = _scan_step_of(r0s[j] // CHUNK + c, rev, latent)
                        decay_ref[step] = jnp.exp(bls[j][c * CHUNK:c * CHUNK + 1, :])
                        st_ref[step] = ups[j][:, c * HEAD_DIM:(c + 1) * HEAD_DIM]
                        if latent:
                            dst_ref[step] = d_ups[j][:, c * HEAD_DIM:(c + 1) * HEAD_DIM]
                        else:
                            dst_ref[step] = jnp.zeros((HEAD_DIM, HEAD_DIM), F32)
                return carry

            lax.fori_loop(0, CTX_LEN // (per_ctx * GROUP), functools.partial(local_step, latent=False, per=per_ctx), 0)
            lax.fori_loop(0, SEQ // (per_lat * GROUP), functools.partial(local_step, latent=True, per=per_lat), 0)

            def scan_step(i, carry):
                st, d_st = carry
                j = N_CHUNKS - 1 - i
                update, d_update = st_ref[i], dst_ref[j]
                st_ref[i] = st
                dst_ref[j] = d_st
                return st * decay_ref[i] + update, d_st * decay_ref[j] + d_update

            zero_state = jnp.zeros((HEAD_DIM, HEAD_DIM), F32)
            lax.fori_loop(0, N_CHUNKS, scan_step, (zero_state, zero_state))

            def grad_step(i, dlb, latent, per):
                r0s = [_group_rows(i, j, per, latent) for j in range(per)]
                rows = [pl.ds(r0, GROUP) for r0 in r0s]
                gates = [_gates(f_ref[r, :], lb) for r in rows]
                vs = [v_ref[r, :] for r in rows]
                bs = [_group_sum(sums_ref[d], g[3]) for g in gates]
                bls = [_chunk_row(b, last) for b in bs]
                e_ends = [jnp.exp(bl - b) for b, bl in zip(bs, bls)]
                k_ends = [g[2] * e for g, e in zip(gates, e_ends)]
                sts = [[st_ref[_scan_step_of(r0 // CHUNK + c, rev, latent)] for c in range(2)] for r0 in r0s]
                d_sts = [[dst_ref[_scan_step_of(r0 // CHUNK + c, rev, latent)] for c in range(2)] for r0 in r0s]
                d_kends = [_own_block(_mm(v, jnp.concatenate(ds, axis=1))) for v, ds in zip(vs, d_sts)]
                d_vs = [_own_block(_mm_nt(ke, jnp.concatenate(ds, axis=0))) for ke, ds in zip(k_ends, d_sts)]
                at_last = [jnp.concatenate([jnp.broadcast_to(jnp.sum(ds[c] * s[c], axis=0, keepdims=True), (CHUNK, HEAD_DIM))
                                            for c in range(2)], axis=0) * jnp.exp(bl) for ds, s, bl in zip(d_sts, sts, bls)]
                t_kends = [dk * ke for dk, ke in zip(d_kends, k_ends)]
                d_ks = [dk * e for dk, e in zip(d_kends, e_ends)]
                if latent:
                    orows = [pl.ds(pl.multiple_of(r0 - CTX_LEN, GROUP), GROUP) for r0 in r0s]
                    qpres = [q_ref[r, :] for r in rows]
                    q_sigs = [_sig(qp) for qp in qpres]
                    qs = [qp * sg * q_scale for qp, sg in zip(qpres, q_sigs)]
                    bms = [_chunk_row(b, mid) for b in bs]
                    e_bs = [jnp.exp(b) for b in bs]
                    e_qms = [jnp.exp(b - bm) for b, bm in zip(bs, bms)]
                    e_kms = [jnp.exp(bm - b) for b, bm in zip(bs, bms)]
                    q_ts = [q * e for q, e in zip(qs, e_bs)]
                    q_ms = [q * e for q, e in zip(qs, e_qms)]
                    k_ms = [g[2] * e for g, e in zip(gates, e_kms)]
                    d_outs = [do_ref[orow, :] for orow in orows]
                    a_s = [_mm_nt(qm, km) * keep_ref[d] for qm, km in zip(q_ms, k_ms)]
                    d_as = [_mm_nt(do, v) * keep_ref[d] for do, v in zip(d_outs, vs)]
                    d_qts = [_own_block(_mm(do, jnp.concatenate(s, axis=1))) for do, s in zip(d_outs, sts)]
                    d_qms = [_mm(da, km) for da, km in zip(d_as, k_ms)]
                    d_kms = [_mm_tn(da, qm) for da, qm in zip(d_as, q_ms)]
                    d_vs = [dv + _mm_tn(a, do) for dv, a, do in zip(d_vs, a_s, d_outs)]
                    d_ks = [dk + dkm * e for dk, dkm, e in zip(d_ks, d_kms, e_kms)]
                    d_bs = [jnp.concatenate([dqt * qt + dqm * qm - dkm * km, t], axis=0)
                            for dqt, qt, dqm, qm, dkm, km, t in zip(d_qts, q_ts, d_qms, q_ms, d_kms, k_ms, t_kends)]
                    d_qs = [dqt * eb + dqm * eq for dqt, eb, dqm, eq in zip(d_qts, e_bs, d_qms, e_qms)]
                    back = back_ref[d]
                else:
                    d_bs, back = t_kends, back_ref[d, :, GROUP:2 * GROUP]
                d_lfs = [_group_sum(back, db) + al for db, al in zip(d_bs, at_last)]
                for j in range(per):
                    sg, f = gates[j][0], gates[j][1]
                    d_f = d_lfs[j] / f - d_ks[j]
                    dg_ref[d, rows[j], :] = (d_f * (1.0 - lb) * sg * (1.0 - sg)).astype(BF16)
                    dlb = dlb + jnp.sum(d_f * (1.0 - sg), axis=0, keepdims=True)
                    if rev:
                        dg_ref[2, rows[j], :] = (both_ref[0, rows[j], :] + d_vs[j]).astype(BF16)
                    else:
                        both_ref[0, rows[j], :] = d_vs[j]
                    if latent:
                        d_qpre = d_qs[j] * q_scale * (q_sigs[j] * (1.0 + qpres[j] * (1.0 - q_sigs[j])))
                        if rev:
                            dg_ref[3, rows[j], :] = (both_ref[1, rows[j], :] + d_qpre).astype(BF16)
                        else:
                            both_ref[1, rows[j], :] = d_qpre
                return dlb

            dlb = lax.fori_loop(0, SEQ // (GROUPS_PER_GRAD_STEP * GROUP),
                                functools.partial(grad_step, latent=True, per=GROUPS_PER_GRAD_STEP), jnp.zeros((1, HEAD_DIM), F32))
            dlb = lax.fori_loop(0, CTX_LEN // (per_ctx * GROUP), functools.partial(grad_step, latent=False, per=per_ctx), dlb)
            dlg_ref[d:d + 1, :] = dlb * lb * (1.0 - lb)

        @pl.when(pl.program_id(0) == HEADS - 1)
        def _():
            wait_scatter()

    col = pl.BlockSpec((SEQ, HEAD_DIM), lambda h: (0, h))
    outs = pl.pallas_call(
        body, name="gla_backward", grid=(HEADS,),
        out_shape=[pltpu.HBM((HGRN_SECTIONS, ROWS_ALL, D_MODEL), BF16), jax.ShapeDtypeStruct((2, D_MODEL), F32)]
        + _slot_shapes(GATHER_LATE),
        in_specs=[G_SPEC(0), G_SPEC(1), G_SPEC(2), G_SPEC(3), pl.BlockSpec((2, HEAD_DIM), lambda h: (0, h)), col, col]
        + MASK_SPECS + [ANY] * n_late,
        out_specs=[pl.BlockSpec((HGRN_SECTIONS, ROWS_ALL, HEAD_DIM), lambda h: (0, 0, h)),
                   pl.BlockSpec((2, HEAD_DIM), lambda h: (0, h))] + [ANY] * n_late,
        scratch_shapes=[pltpu.VMEM((N_CHUNKS, HEAD_DIM, HEAD_DIM), F32), pltpu.VMEM((N_CHUNKS, HEAD_DIM, HEAD_DIM), F32),
                        pltpu.VMEM((N_CHUNKS, 1, HEAD_DIM), F32), pltpu.VMEM((2, ROWS_ALL, HEAD_DIM), F32)]
        + _comm_sems(n_late),
        compiler_params=_params(48),
    )(*_pin(g5, g5, g5, g5, lb_logits, d_o, d_z), *_group_masks(), *late_parts)
    return outs[0], outs[1], dict(zip(GATHER_LATE, outs[2:]))


def _head_norm(o, scr):
    rs = []
    for h in range(HEADS):
        cols = slice(h * HEAD_DIM, (h + 1) * HEAD_DIM)
        oh = o[:, cols]
        r = lax.rsqrt(jnp.mean(oh * oh, axis=-1, keepdims=True) + EPS)
        scr[:, cols] = oh * r
        rs.append(r)
    return rs


def _hgrn_out_forward(o_raw, g5, xin, gnorm_w, gate, w_out):
    tm = ROW_TILE

    def body(o_ref, z_ref, x_ref, gw_ref, gate_ref, w_ref, x1_ref, res_ref, scr):
        _head_norm(o_ref[...], scr)
        a = scr[...] * gw_ref[...] * _silu(z_ref[...])
        res = _mm(a, w_ref[...])
        res_ref[...] = res
        x1_ref[...] = x_ref[...] + gate_ref[...] * res

    tile = pl.BlockSpec((tm, D_MODEL), lambda i: (i, 0))
    vec = pl.BlockSpec((1, D_MODEL), lambda i: (0, 0))
    return pl.pallas_call(
        body, name="hgrn_out_forward", grid=(SEQ // tm,),
        out_shape=[pltpu.HBM((SEQ, D_MODEL), F32)] * 2,
        in_specs=[tile, pl.BlockSpec((None, tm, D_MODEL), lambda i: (4, i + CTX_LEN // tm, 0)), tile, vec, vec,
                  pl.BlockSpec((D_MODEL, D_MODEL), lambda i: (0, 0))],
        out_specs=[tile, tile],
        scratch_shapes=[pltpu.VMEM((tm, D_MODEL), F32)],
        compiler_params=_params(32),
    )(*_pin(o_raw, g5, xin, gnorm_w, gate, w_out))


def _hgrn_out_backward(d_x1, o_raw, g5, res, gnorm_w, gate, w_out):
    tm = ROW_TILE

    def body(dx_ref, o_ref, z_ref, res_ref, gw_ref, gate_ref, w_ref, do_ref, dz_ref, dw_out, dgate_ref, dgw_ref, scr, scr2,
             dw_ref):
        @pl.when(pl.program_id(0) == 0)
        def _():
            dw_ref[...] = jnp.zeros_like(dw_ref)
            dgate_ref[...] = jnp.zeros_like(dgate_ref)
            dgw_ref[...] = jnp.zeros_like(dgw_ref)

        dx = dx_ref[...]
        dgate_ref[...] += jnp.sum(dx * res_ref[...], axis=0, keepdims=True)
        d_res = (dx * gate_ref[...]).astype(BF16)
        d_a = _mm_nt(d_res, w_ref[...])
        rs = _head_norm(o_ref[...], scr)
        z = z_ref[...]
        sz = _silu(z)
        o_hat = scr[...]
        o_n = o_hat * gw_ref[...]
        dw_ref[...] += _mm_tn(o_n * sz, d_res)
        d_on = d_a * sz
        dz_ref[...] = (d_a * o_n * _dsilu(z)).astype(BF16)
        dgw_ref[...] += jnp.sum(d_on * o_hat, axis=0, keepdims=True)
        scr2[...] = d_on * gw_ref[...]
        for h in range(HEADS):
            cols = slice(h * HEAD_DIM, (h + 1) * HEAD_DIM)
            dh, oh = scr2[:, cols], scr[:, cols]
            do_ref[:, cols] = (rs[h] * (dh - oh * jnp.mean(dh * oh, axis=-1, keepdims=True))).astype(BF16)

        @pl.when(pl.program_id(0) == SEQ // tm - 1)
        def _():
            dw_out[...] = dw_ref[...].astype(BF16)

    tile = pl.BlockSpec((tm, D_MODEL), lambda i: (i, 0))
    vec = pl.BlockSpec((1, D_MODEL), lambda i: (0, 0))
    mat = pl.BlockSpec((D_MODEL, D_MODEL), lambda i: (0, 0))
    return pl.pallas_call(
        body, name="hgrn_out_backward", grid=(SEQ // tm,),
        out_shape=[pltpu.HBM((SEQ, D_MODEL), BF16)] * 2 + [pltpu.HBM((D_MODEL, D_MODEL), BF16)]
        + [jax.ShapeDtypeStruct((1, D_MODEL), F32)] * 2,
        in_specs=[tile, tile, pl.BlockSpec((None, tm, D_MODEL), lambda i: (4, i + CTX_LEN // tm, 0)), tile, vec, vec, mat],
        out_specs=[tile, tile, mat, vec, vec],
        scratch_shapes=[pltpu.VMEM((tm, D_MODEL), F32)] * 2 + [pltpu.VMEM((D_MODEL, D_MODEL), F32)],
        compiler_params=_params(40),
    )(*_pin(d_x1, o_raw, g5, res, gnorm_w, gate, w_out))


def _pool_constants():
    win = np.zeros((POOL_GROUPS, ROW_TILE, ROW_TILE), np.float32)
    inv = np.zeros((POOL_GROUPS, ROW_TILE, 1), np.float32)
    for g, w in enumerate(POOL_WINDOWS):
        for t in range(ROW_TILE):
            base, p = (t // GRID_W) * GRID_W, t % GRID_W
            lo = min(max(p - w // 2, 0), GRID_W)
            hi = min(max(p - w // 2 + w, 0), GRID_W)
            win[g, t, base + lo:base + hi] = 1.0
            inv[g, t, 0] = 1.0 / np.float32(hi - lo)
    return jnp.asarray(win, BF16), jnp.asarray(win.transpose(0, 2, 1), BF16), jnp.asarray(inv, F32)


def _pool_mix(u_ref, wg_ref, win_ref, inv_ref, pooled_scr, yg_scr):
    for g in range(POOL_GROUPS):
        cols = slice(g * POOL_GROUP_DIM, (g + 1) * POOL_GROUP_DIM)
        for r in range(u_ref.shape[0] // ROW_TILE):
            rows = slice(r * ROW_TILE, (r + 1) * ROW_TILE)
            ug = u_ref[rows, cols]
            pooled_scr[rows, cols] = _mm_exact_lhs(win_ref[g], ug) * inv_ref[g] - ug
        yg_scr[:, cols] = _mm(pooled_scr[:, cols], wg_ref[g])


def _pool_forward_loss(uz, x1, target, gate, w_grp, pool_scale, w_out, final_w):
    tm = POOL_TILE
    win, _, inv = _pool_constants()

    def body(u_ref, z_ref, x_ref, t_ref, gate_ref, wg_ref, ps_ref, w_ref, fw_ref, win_ref, inv_ref,
             dx_ref, loss_ref, dfw_ref, dgate_ref, yg_scr, pooled_scr):
        @pl.when(pl.program_id(0) == 0)
        def _():
            loss_ref[...] = jnp.zeros_like(loss_ref)
            dfw_ref[...] = jnp.zeros_like(dfw_ref)
            dgate_ref[...] = jnp.zeros_like(dgate_ref)

        _pool_mix(u_ref, wg_ref, win_ref, inv_ref, pooled_scr, yg_scr)
        a = yg_scr[...] * ps_ref[...] * _silu(z_ref[...])
        res = _mm(a, w_ref[...])
        x2 = x_ref[...] + gate_ref[...] * res
        r = lax.rsqrt(jnp.mean(x2 * x2, axis=-1, keepdims=True) + EPS)
        xh = x2 * r
        fw = fw_ref[...]
        err = xh * fw - t_ref[...]
        loss_ref[...] += 0.5 * jnp.sum(jnp.mean(err * err, axis=-1, keepdims=True))
        d_y = err * (1.0 / D_MODEL)
        dfw_ref[...] += jnp.sum(d_y * xh, axis=0, keepdims=True)
        d_xh = d_y * fw
        d_x2 = r * (d_xh - xh * jnp.mean(d_xh * xh, axis=-1, keepdims=True))
        dx_ref[...] = d_x2
        dgate_ref[...] += jnp.sum(d_x2 * res, axis=0, keepdims=True)

    tile = pl.BlockSpec((tm, D_MODEL), lambda i: (i, 0))
    vec = pl.BlockSpec((1, D_MODEL), lambda i: (0, 0))
    grp = pl.BlockSpec((POOL_GROUPS, POOL_GROUP_DIM, POOL_GROUP_DIM), lambda i: (0, 0, 0))
    return pl.pallas_call(
        body, name="pool_forward_loss", grid=(SEQ // tm,),
        out_shape=[pltpu.HBM((SEQ, D_MODEL), F32), jax.ShapeDtypeStruct((8, 128), F32),
                   jax.ShapeDtypeStruct((1, D_MODEL), F32), jax.ShapeDtypeStruct((1, D_MODEL), F32)],
        in_specs=[pl.BlockSpec((None, tm, D_MODEL), lambda i: (0, i, 0)), pl.BlockSpec((None, tm, D_MODEL), lambda i: (1, i, 0)),
                  tile, tile, vec, grp, vec, pl.BlockSpec((D_MODEL, D_MODEL), lambda i: (0, 0)), vec, grp,
                  pl.BlockSpec((POOL_GROUPS, ROW_TILE, 1), lambda i: (0, 0, 0))],
        out_specs=[tile, pl.BlockSpec((8, 128), lambda i: (0, 0)), vec, vec],
        scratch_shapes=[pltpu.VMEM((tm, D_MODEL), F32)] * 2,
        compiler_params=_params(40),
    )(*_pin(uz, uz, x1, target, gate, w_grp, pool_scale, w_out, final_w, win, inv))


def _pool_backward(d_x2, uz, gate, w_grp, pool_scale, w_out):
    tm = POOL_TILE
    win, win_t, inv = _pool_constants()

    def body(dx_ref, u_ref, z_ref, gate_ref, wg_ref, ps_ref, w_ref, win_ref, wint_ref, inv_ref,
             duz_ref, dw_out, dwg_out, dps_ref, pooled_scr, yg_scr, dyg_scr, dw_ref, dwg_ref):
        @pl.when(pl.program_id(0) == 0)
        def _():
            dw_ref[...] = jnp.zeros_like(dw_ref)
            dwg_ref[...] = jnp.zeros_like(dwg_ref)
            dps_ref[...] = jnp.zeros_like(dps_ref)

        _pool_mix(u_ref, wg_ref, win_ref, inv_ref, pooled_scr, yg_scr)
        z = z_ref[...]
        sz = _silu(z)
        yg = yg_scr[...]
        y = yg * ps_ref[...]
        d_res = (dx_ref[...] * gate_ref[...]).astype(BF16)
        d_a = _mm_nt(d_res, w_ref[...])
        dw_ref[...] += _mm_tn(y * sz, d_res)
        d_y = d_a * sz
        duz_ref[1] = (d_a * y * _dsilu(z)).astype(BF16)
        dps_ref[...] += jnp.sum(d_y * yg, axis=0, keepdims=True)
        dyg_scr[...] = d_y * ps_ref[...]
        for g in range(POOL_GROUPS):
            cols = slice(g * POOL_GROUP_DIM, (g + 1) * POOL_GROUP_DIM)
            d_yg = dyg_scr[:, cols].astype(BF16)
            d_pool = _mm_nt(d_yg, wg_ref[g])
            dwg_ref[g] += _mm_tn(pooled_scr[:, cols], d_yg)
            for r in range(tm // ROW_TILE):
                rows = slice(r * ROW_TILE, (r + 1) * ROW_TILE)
                dp = d_pool[rows, :]
                duz_ref[0, rows, cols] = (_mm_exact_lhs(wint_ref[g], dp * inv_ref[g]) - dp).astype(BF16)

        @pl.when(pl.program_id(0) == SEQ // tm - 1)
        def _():
            dw_out[...] = dw_ref[...].astype(BF16)
            dwg_out[...] = dwg_ref[...].astype(BF16)

    tile = pl.BlockSpec((tm, D_MODEL), lambda i: (i, 0))
    vec = pl.BlockSpec((1, D_MODEL), lambda i: (0, 0))
    mat = pl.BlockSpec((D_MODEL, D_MODEL), lambda i: (0, 0))
    grp = pl.BlockSpec((POOL_GROUPS, POOL_GROUP_DIM, POOL_GROUP_DIM), lambda i: (0, 0, 0))
    return pl.pallas_call(
        body, name="pool_backward", grid=(SEQ // tm,),
        out_shape=[pltpu.HBM((POOL_SECTIONS, SEQ, D_MODEL), BF16), pltpu.HBM((D_MODEL, D_MODEL), BF16),
                   pltpu.HBM((POOL_GROUPS, POOL_GROUP_DIM, POOL_GROUP_DIM), BF16), jax.ShapeDtypeStruct((1, D_MODEL), F32)],
        in_specs=[tile, pl.BlockSpec((None, tm, D_MODEL), lambda i: (0, i, 0)), pl.BlockSpec((None, tm, D_MODEL), lambda i: (1, i, 0)),
                  vec, grp, vec, mat, grp, grp, pl.BlockSpec((POOL_GROUPS, ROW_TILE, 1), lambda i: (0, 0, 0))],
        out_specs=[pl.BlockSpec((POOL_SECTIONS, tm, D_MODEL), lambda i: (0, i, 0)), mat, grp, vec],
        scratch_shapes=[pltpu.VMEM((tm, D_MODEL), F32)] * 3 + [pltpu.VMEM((D_MODEL, D_MODEL), F32),
                                                               pltpu.VMEM((POOL_GROUPS, POOL_GROUP_DIM, POOL_GROUP_DIM), F32)],
        compiler_params=_params(48),
    )(*_pin(d_x2, uz, uz, gate, w_grp, pool_scale, w_out, win, win_t, inv))


def _ln_mod_backward(d_g, w, ctx_tile, xin, nw, scale, d_up, name):
    n_sec, rows, _ = d_g.shape
    n_mod = scale.shape[0]
    skip = n_mod - 1
    tm = ROW_TILE if skip else 2 * ROW_TILE
    n_tiles = rows // tm

    def body(dg_ref, w_ref, *refs):
        c_ref = refs[0] if skip else None
        x_ref, nw_ref, sc_ref, up_ref, dx_ref, dnw_ref, dmod_ref = refs[skip:]
        i = pl.program_id(0)

        @pl.when(i == 0)
        def _():
            dnw_ref[...] = jnp.zeros_like(dnw_ref)

        @pl.when((i == 0) | (i == skip))
        def _():
            dmod_ref[...] = jnp.zeros_like(dmod_ref)

        d_h = _mm_nt(dg_ref[0], w_ref[:, 0:D_MODEL])
        for k in range(1, n_sec):
            d_h = d_h + _mm_nt(dg_ref[k], w_ref[:, k * D_MODEL:(k + 1) * D_MODEL])
        xv = jnp.where(i == 0, c_ref[...], x_ref[...]) if skip else x_ref[...]
        r = lax.rsqrt(jnp.mean(xv * xv, axis=-1, keepdims=True) + EPS)
        xh = xv * r
        nw_row = nw_ref[...]
        dmod_ref[0:1, :] += jnp.sum(d_h, axis=0, keepdims=True)
        dmod_ref[1:2, :] += jnp.sum(d_h * (xh * nw_row), axis=0, keepdims=True)
        d_xn = d_h * (1.0 + sc_ref[...])
        dnw_ref[...] += jnp.sum(d_xn * xh, axis=0, keepdims=True)
        d_xh = d_xn * nw_row

        @pl.when(i >= skip)
        def _():
            dx_ref[...] = up_ref[...] + r * (d_xh - xh * jnp.mean(d_xh * xh, axis=-1, keepdims=True))

    lat = lambda i: (jnp.maximum(i - skip, 0), 0)
    mod_idx = lambda i: (jnp.minimum(i, n_mod - 1), 0, 0)
    return pl.pallas_call(
        body, name=name, grid=(n_tiles,),
        out_shape=[pltpu.HBM((rows - skip * tm, D_MODEL), F32), jax.ShapeDtypeStruct((1, D_MODEL), F32),
                   jax.ShapeDtypeStruct((n_mod, 8, D_MODEL), F32)],
        in_specs=[pl.BlockSpec((n_sec, tm, D_MODEL), lambda i: (0, i, 0)),
                  pl.BlockSpec((D_MODEL, n_sec * D_MODEL), lambda i: (0, 0))]
        + [pl.BlockSpec((tm, D_MODEL), lambda i: (0, 0))] * skip
        + [pl.BlockSpec((tm, D_MODEL), lat),
           pl.BlockSpec((1, D_MODEL), lambda i: (0, 0)),
           pl.BlockSpec((None, 1, D_MODEL), mod_idx),
           pl.BlockSpec((tm, D_MODEL), lat)],
        out_specs=[pl.BlockSpec((tm, D_MODEL), lat), pl.BlockSpec((1, D_MODEL), lambda i: (0, 0)),
                   pl.BlockSpec((None, 8, D_MODEL), mod_idx)],
        compiler_params=_params(48),
    )(*_pin(d_g, w, *([ctx_tile] * skip), xin, nw, scale.reshape(n_mod, 1, D_MODEL), d_up))


def _weight_grad(h, d_g, name):
    n_sec, rows, _ = d_g.shape
    tm = 768 if rows % 768 == 0 else 512
    n_tiles = rows // tm

    def body(h_ref, dg_ref, dw_ref, acc):
        i = pl.program_id(1)
        prod = _mm_tn(h_ref[...], dg_ref[...])

        @pl.when(i == 0)
        def _():
            acc[...] = prod

        @pl.when((i > 0) & (i < n_tiles - 1))
        def _():
            acc[...] += prod

        @pl.when(i == n_tiles - 1)
        def _():
            dw_ref[...] = (acc[...] + prod).astype(BF16)

    return pl.pallas_call(
        body, name=name, grid=(n_sec, n_tiles),
        out_shape=pltpu.HBM((D_MODEL, n_sec * D_MODEL), BF16),
        in_specs=[pl.BlockSpec((tm, D_MODEL), lambda j, i: (i, 0)), pl.BlockSpec((None, tm, D_MODEL), lambda j, i: (j, i, 0))],
        out_specs=pl.BlockSpec((D_MODEL, D_MODEL), lambda j, i: (0, j)),
        scratch_shapes=[pltpu.VMEM((D_MODEL, D_MODEL), F32)],
        compiler_params=_params(32),
    )(*_pin(h, d_g))


def _weight_grad_paired(h, d_g, name):
    n_sec, rows, _ = d_g.shape
    tm = 768
    n_tiles = rows // tm
    half = D_MODEL // 2
    chip_cols = n_sec * D_MODEL // N_CHIPS

    def body(h_ref, dg_ref, q_ref, slots_ref, acc, keep, send, land, send_sems, recv_sems, own_sem):
        j, i = pl.program_id(0), pl.program_id(1)
        x, y, c = _my_place()
        prod = _mm_tn(h_ref[...], dg_ref[...])

        def to_sibling(k):
            return pltpu.make_async_remote_copy(src_ref=send.at[k], dst_ref=land.at[k], send_sem=send_sems.at[k],
                                                recv_sem=recv_sems.at[k], device_id=(x, y, 1 - c), device_id_type=MESH)

        @pl.when(i == 0)
        def _():
            acc[...] = prod

        @pl.when((i > 0) & (i < n_tiles - 1))
        def _():
            acc[...] += prod

        @pl.when(i == n_tiles - 1)
        def _():
            acc[...] += prod
            keep[j] = acc[pl.ds(_al(c * half, half), half), :]
            send[j] = acc[pl.ds(_al((1 - c) * half, half), half), :].astype(BF16)
            to_sibling(j).start()

        @pl.when((j == n_sec - 1) & (i == n_tiles - 1))
        def _():
            for k in range(n_sec):
                to_sibling(k).wait()
                q_ref[:, k * D_MODEL:(k + 1) * D_MODEL] = (keep[k] + land[k].astype(F32)).astype(BF16)
            chip = 2 * x + y
            own = pltpu.make_async_copy(q_ref.at[:, pl.ds(_al(chip * chip_cols, chip_cols), chip_cols)], slots_ref.at[chip], own_sem)
            own.start()
            own.wait()

    return pl.pallas_call(
        body, name=name, grid=(n_sec, n_tiles),
        out_shape=[jax.ShapeDtypeStruct((half, n_sec * D_MODEL), BF16),
                   jax.ShapeDtypeStruct((N_CHIPS, half, chip_cols), BF16)],
        in_specs=[pl.BlockSpec((tm, D_MODEL), lambda j, i: (i, 0)), pl.BlockSpec((None, tm, D_MODEL), lambda j, i: (j, i, 0))],
        out_specs=[pl.BlockSpec((half, n_sec * D_MODEL), lambda j, i: (0, 0)), ANY],
        scratch_shapes=[pltpu.VMEM((D_MODEL, D_MODEL), F32), pltpu.VMEM((n_sec, half, D_MODEL), F32),
                        pltpu.VMEM((n_sec, half, D_MODEL), BF16), pltpu.VMEM((n_sec, half, D_MODEL), BF16),
                        pltpu.SemaphoreType.DMA((n_sec,)), pltpu.SemaphoreType.DMA((n_sec,)), pltpu.SemaphoreType.DMA],
        compiler_params=_params(56),
    )(*_pin(h, d_g))


def _adamw_math(w, g, m, v):
    m = ADAM_B1 * m + (1.0 - ADAM_B1) * g
    v = ADAM_B2 * v + (1.0 - ADAM_B2) * (g * g)
    m_hat = m / (1.0 - ADAM_B1 ** ADAM_STEP)
    v_hat = v / (1.0 - ADAM_B2 ** ADAM_STEP)
    return -ADAM_LR * (m_hat / (jnp.sqrt(v_hat) + ADAM_EPS) + ADAM_WD * w), m, v


def _adamw(w, g, m, v, name):
    rows, cols = w.shape
    tr = rows if rows <= 128 else 128

    def body(w_ref, g_ref, m_ref, v_ref, d_ref, mo_ref, vo_ref):
        d_ref[...], mo_ref[...], vo_ref[...] = _adamw_math(w_ref[...], g_ref[...], m_ref[...], v_ref[...])

    tile = pl.BlockSpec((tr, cols), lambda i: (i, 0))
    return pl.pallas_call(
        body, name=name, grid=(rows // tr,),
        out_shape=[pltpu.HBM((rows, cols), F32)] * 3,
        in_specs=[tile] * 4, out_specs=[tile] * 3,
    )(*_pin(w, g, m, v))


def _adamw_small(quads, name):
    n = len(quads)

    def body(*refs):
        ins, outs = refs[:4 * n], refs[4 * n:]
        for a in range(n):
            w_ref, g_ref, m_ref, v_ref = ins[4 * a:4 * a + 4]
            d_ref, mo_ref, vo_ref = outs[3 * a:3 * a + 3]
            d_ref[...], mo_ref[...], vo_ref[...] = _adamw_math(w_ref[...], g_ref[...], m_ref[...], v_ref[...])

    outs = pl.pallas_call(
        body, name=name, out_shape=[jax.ShapeDtypeStruct(q[0].shape, F32) for q in quads for _ in range(3)],
        in_specs=[VMEM] * (4 * n), out_specs=[VMEM] * (3 * n),
    )(*[a for q in quads for a in q])
    return [tuple(outs[3 * a:3 * a + 3]) for a in range(n)]


def _small_sums_and_cond_ctx(small, ada_w, c_ctx, m, v):
    n_cols = ada_w.shape[2]

    def body(small_ref, w_ref, c_ref, m_ref, v_ref, all_ref, sums_ref, g_ref, d_ref, mo_ref, vo_ref, part_ref, parts_ref,
             s1, r1, l1, s2, r2, l2):
        start_slabs, finish_slabs = _small_gather(small_ref, all_ref, s1, r1, l1)
        start_parts, finish_parts = _small_gather(part_ref, parts_ref, s2, r2, l2)
        start_slabs()
        finish_slabs()
        acc = all_ref[0:SMALL_ROWS, :]
        for dev in range(1, N_DEV):
            acc = acc + all_ref[dev * SMALL_ROWS:(dev + 1) * SMALL_ROWS, :]
        sums_ref[...] = acc
        x, y, _ = _my_place()
        d_modc = jnp.concatenate([acc[6:7, :], acc[7:8, :], acc[8:9, :]], axis=1)
        mine = jnp.zeros((1, n_cols), F32)
        for s in range(N_CHIPS):
            mine = mine + jnp.where(2 * x + y == s, d_modc[:, s * n_cols:(s + 1) * n_cols], 0.0)
        ah, al = _split2(jnp.broadcast_to(mine, (8, n_cols)))
        wh, wl = _split2(w_ref[0])
        nt = lambda a, b: lax.dot_general(a, b, (((1,), (1,)), ((), ())), preferred_element_type=F32)
        part_ref[...] = nt(ah, wh) + nt(al, wh) + nt(ah, wl)
        start_parts()
        finish_parts()
        total = parts_ref[0:1, :]
        for s in range(1, N_CHIPS):
            total = total + parts_ref[16 * s:16 * s + 1, :]
        w = c_ref[...]
        g = total * _dsilu(w)
        g_ref[...] = g
        d_ref[...], mo_ref[...], vo_ref[...] = _adamw_math(w, g, m_ref[...], v_ref[...])

    seven = [pltpu.SemaphoreType.DMA((7,)), pltpu.SemaphoreType.DMA((7,)), pltpu.SemaphoreType.DMA]
    vec = jax.ShapeDtypeStruct((1, D_MODEL), F32)
    return pl.pallas_call(
        body, name="small_sums_and_cond_ctx",
        out_shape=[jax.ShapeDtypeStruct((N_DEV * SMALL_ROWS, D_MODEL), F32), jax.ShapeDtypeStruct((SMALL_ROWS, D_MODEL), F32),
                   vec, vec, vec, vec],
        in_specs=[VMEM] * 5, out_specs=[VMEM] * 6,
        scratch_shapes=[pltpu.VMEM((8, D_MODEL), F32), pltpu.VMEM((N_DEV * 8, D_MODEL), F32)] + seven + seven,
    )(small, ada_w, c_ctx, m, v)


def _ada_update(cond_t, d_mod, ada_w, m, v):
    n_layers, _, n_cols = ada_w.shape
    tr = ROW_TILE

    def body(c_ref, dm_ref, w_ref, m_ref, v_ref, g_ref, d_ref, mo_ref, vo_ref):
        g = _mm_f32(_silu(c_ref[...]), dm_ref[...])
        g_ref[...] = g
        d_ref[...], mo_ref[...], vo_ref[...] = _adamw_math(w_ref[...], g, m_ref[...], v_ref[...])

    tile = pl.BlockSpec((None, tr, n_cols), lambda l, i: (l, i, 0))
    return pl.pallas_call(
        body, name="ada_update", grid=(n_layers, D_MODEL // tr),
        out_shape=[pltpu.HBM(ada_w.shape, F32)] * 4,
        in_specs=[pl.BlockSpec((tr, 16), lambda l, i: (i, 0)), pl.BlockSpec((None, 16, n_cols), lambda l, i: (l, 0, 0)),
                  tile, tile, tile],
        out_specs=[tile] * 4,
    )(*_pin(cond_t, d_mod, ada_w, m, v))


def _local_step(x2, ctx2, target, mod_mine, mod_ctx, lb_logits, scale_full, w_in_full, late_shards, norm_w, gnorm, final_w):
    row = lambda a: a.reshape(1, -1)
    shift0, scale0, gate0 = (row(a) for a in jnp.split(mod_mine[0], 3))
    shift1, scale1, gate1 = (row(a) for a in jnp.split(mod_mine[1], 3))
    shift_c, scale_c, _ = (row(a) for a in jnp.split(mod_ctx, 3))
    nw0, nw1 = norm_w[0:1], norm_w[1:2]
    scales0 = jnp.concatenate([scale_c, scale0])

    g5, h0 = _ln_mod_matmul(ctx2, x2, nw0, jnp.concatenate([shift_c, shift0]), scales0, w_in_full, "hgrn_in_forward")
    o_raw, full = _gla_forward(g5, lb_logits, late_shards)
    x1, res0 = _hgrn_out_forward(o_raw, g5, x2, gnorm, gate0, full["hgrn_w_out"])
    uz, h1 = _ln_mod_matmul(None, x1, nw1, shift1, scale1, full["pool_w_in"], "pool_in_forward")
    d_x2, loss_part, d_final, d_gate1 = _pool_forward_loss(uz, x1, target, gate1, full["pool_w_grp"], scale_full,
                                                           full["pool_w_out"], final_w)

    d_uz, dw_pool_out, dw_pool_grp, d_pscale = _pool_backward(d_x2, uz, gate1, full["pool_w_grp"], scale_full, full["pool_w_out"])
    d_x1, d_nw1, d_mod1 = _ln_mod_backward(d_uz, full["pool_w_in"], None, x1, nw1, scale1, d_x2, "pool_in_backward")
    dw_pool_in = _weight_grad(h1, d_uz, "pool_in_weight_grad")
    d_o, d_z, dw_hgrn_out, d_gate0, d_gnorm = _hgrn_out_backward(d_x1, o_raw, g5, res0, gnorm, gate0, full["hgrn_w_out"])
    late_grads = {"hgrn_w_out": dw_hgrn_out, "pool_w_in": dw_pool_in, "pool_w_grp": dw_pool_grp, "pool_w_out": dw_pool_out}
    d_g5, d_lb, late_slots = _gla_backward(g5, lb_logits, d_o, d_z, [late_grads[k].astype(BF16) for k in GATHER_LATE])
    dw_hgrn_in, slots0 = _weight_grad_paired(h0, d_g5, "hgrn_in_weight_grad")
    key = GATHER_EARLY[0]
    send_sem, recv_sem, part_thru, slots_thru, token = _scatter_start(dw_hgrn_in, slots0, d_lb, key)
    d_x, d_nw0, d_mod0 = _ln_mod_backward(d_g5, w_in_full, ctx2, x2, nw0 + token[0:1, 0:1], scales0, d_x1, "hgrn_in_backward")
    slots = dict(late_slots)
    pending = (send_sem, recv_sem, part_thru, slots_thru)

    zero = jnp.zeros((1, D_MODEL), F32)
    small = jnp.concatenate([d_mod0[1, 0:2], d_gate0, d_mod1[0, 0:2], d_gate1, d_mod0[0, 0:2], zero, d_nw0, d_nw1, d_gnorm,
                             d_final, d_pscale, d_lb, jnp.broadcast_to(loss_part[0:1, 0:1], (1, D_MODEL)),
                             jnp.zeros((SMALL_ROWS - 17, D_MODEL), F32)], axis=0)
    return {"d_x": d_x, "slots": slots, "pending": pending, "small": small}


def kernel(x, c, ctx, c_ctx, ada_w, ada_b, norm_w, hgrn_w_in, hgrn_lb_logits, hgrn_gnorm_w, hgrn_w_out, pool_w_in, pool_w_grp, pool_scale, pool_w_out, final_norm_w, loss_target, m_c_ctx, m_ada_w, m_ada_b, m_norm_w, m_hgrn_w_in, m_hgrn_lb_logits, m_hgrn_gnorm_w, m_hgrn_w_out, m_pool_w_in, m_pool_w_grp, m_pool_scale, m_pool_w_out, m_final_norm_w, v_c_ctx, v_ada_w, v_ada_b, v_norm_w, v_hgrn_w_in, v_hgrn_lb_logits, v_hgrn_gnorm_w, v_hgrn_w_out, v_pool_w_in, v_pool_w_grp, v_pool_scale, v_pool_w_out, v_final_norm_w):
    xi, yi, ci = _my_place()
    chip = 2 * xi + yi
    dev = 4 * xi + 2 * yi + ci
    ada_cols = ada_w.shape[2]
    lb_cols = hgrn_lb_logits.shape[2]
    ps_cols = pool_scale.shape[1]
    row = lambda a: a.reshape(1, -1)

    def chip_cols(a, n):
        return lax.dynamic_slice_in_dim(a, chip * n, n, axis=a.ndim - 1)

    def from_chips(g, rows_per_dev, take):
        return jnp.concatenate([g[2 * s * rows_per_dev:2 * s * rows_per_dev + take] for s in range(N_CHIPS)], axis=1)

    first = jnp.concatenate([jnp.broadcast_to(c, (8, D_MODEL)), jnp.pad(hgrn_lb_logits[0], ((0, 6), (0, 0))),
                             jnp.pad(pool_scale, ((0, 7), (0, 0)))], axis=1)
    shards = {"hgrn_w_in": hgrn_w_in[0], "hgrn_w_out": hgrn_w_out[0], "pool_w_in": pool_w_in[0],
              "pool_w_grp": pool_w_grp[0], "pool_w_out": pool_w_out[0]}
    first_all, parts_all, w_in_full = _prologue(first, row(c_ctx), ada_w, chip_cols(ada_b, ada_cols),
                                                shards[GATHER_EARLY[0]].astype(BF16))
    cond_all = first_all[::8, :D_MODEL]
    lb_logits = from_chips(first_all[:, D_MODEL:D_MODEL + lb_cols], 8, 2)
    scale_full = from_chips(first_all[:, D_MODEL + lb_cols:], 8, 1)
    cond_rows = jnp.concatenate([cond_all, row(c_ctx), jnp.zeros((7, D_MODEL), F32)], axis=0)
    mod_all = from_chips(parts_all, 32, 32).reshape(2, 16, 3 * D_MODEL)
    mod_mine = lax.dynamic_index_in_dim(mod_all, dev, axis=1, keepdims=False)

    loc = _local_step(x[0], ctx[0], loss_target[0], mod_mine, mod_all[0, 8], lb_logits, scale_full, w_in_full,
                      [shards[k].astype(BF16) for k in GATHER_LATE], norm_w, hgrn_gnorm_w, row(final_norm_w))

    small_all, sums, g_c, d_c, m_c, v_c = _small_sums_and_cond_ctx(loc["small"], ada_w, row(c_ctx), row(m_c_ctx), row(v_c_ctx))
    loss = sums[16, 0]

    def reduce_scattered(slots, names, name):
        return dict(zip(names, _sum_and_exchange([slots[k] for k in names], names, name)))

    big_grads = reduce_scattered(loc["slots"], GATHER_LATE, "exchange_halves_late")

    out = {}

    def update(name, w, g, m, v):
        shape = w.shape
        w2, g2, m2, v2 = (a.reshape(-1, shape[-1]) for a in (w, g, m, v))
        d, mn, vn = _adamw(w2, g2, m2, v2, "adamw_" + name)
        out[name] = tuple(a.reshape(shape) for a in (g2, d, mn, vn))

    moments = {"hgrn_w_in": (m_hgrn_w_in, v_hgrn_w_in), "hgrn_w_out": (m_hgrn_w_out, v_hgrn_w_out),
               "pool_w_in": (m_pool_w_in, v_pool_w_in), "pool_w_grp": (m_pool_w_grp, v_pool_w_grp),
               "pool_w_out": (m_pool_w_out, v_pool_w_out)}
    weights = {"hgrn_w_in": hgrn_w_in, "hgrn_w_out": hgrn_w_out, "pool_w_in": pool_w_in, "pool_w_grp": pool_w_grp,
               "pool_w_out": pool_w_out}
    for k in GATHER_LATE:
        update(k, weights[k], big_grads[k], *moments[k])

    g_ada_b = jnp.stack([(sums[0:3] + sums[6:9]).reshape(-1), sums[3:6].reshape(-1)])
    small_w = {"ada_b": (ada_b, g_ada_b, m_ada_b, v_ada_b),
               "norm_w": (norm_w, sums[9:11], m_norm_w, v_norm_w),
               "hgrn_gnorm_w": (hgrn_gnorm_w, sums[11:12], m_hgrn_gnorm_w, v_hgrn_gnorm_w),
               "final_norm_w": (row(final_norm_w), sums[12:13], row(m_final_norm_w), row(v_final_norm_w)),
               "pool_scale": (pool_scale, chip_cols(sums[13:14], ps_cols), m_pool_scale, v_pool_scale),
               "hgrn_lb_logits": (hgrn_lb_logits, chip_cols(sums[14:16], lb_cols), m_hgrn_lb_logits, v_hgrn_lb_logits)}
    quads = [tuple(a.reshape(-1, q[0].shape[-1]) for a in q) for q in small_w.values()]
    for (k, q), quad, (d, mn, vn) in zip(small_w.items(), quads, _adamw_small(quads, "adamw_small")):
        out[k] = tuple(a.reshape(q[0].shape) for a in (quad[1], d, mn, vn))

    per_dev = small_all.reshape(N_DEV, SMALL_ROWS, D_MODEL)
    pad7 = jnp.zeros((7, 3 * D_MODEL), F32)
    dm0 = jnp.concatenate([per_dev[:, 0:3].reshape(N_DEV, -1), sums[6:9].reshape(1, -1), pad7], axis=0)
    dm1 = jnp.concatenate([per_dev[:, 3:6].reshape(N_DEV, -1), jnp.zeros((8, 3 * D_MODEL), F32)], axis=0)
    d_mod = chip_cols(jnp.stack([dm0, dm1]), ada_cols)
    out["ada_w"] = _ada_update(cond_rows.T, d_mod, ada_w, m_ada_w, v_ada_w)

    out["c_ctx"] = tuple(a.reshape(-1) for a in (g_c, d_c, m_c, v_c))
    out["final_norm_w"] = tuple(a.reshape(-1) for a in out["final_norm_w"])

    done = [g_c, out["ada_w"][1]] + [out[k][1] for k in GATHER_LATE]
    key = GATHER_EARLY[0]
    _, early = _scatter_wait(*loc["pending"], done, key)
    big_grads = reduce_scattered({key: early}, GATHER_EARLY, "exchange_halves_early")
    for k in GATHER_EARLY:
        update(k, weights[k], big_grads[k], *moments[k])

    names = ["c_ctx", "ada_w", "ada_b", "norm_w", "hgrn_w_in", "hgrn_lb_logits", "hgrn_gnorm_w", "hgrn_w_out", "pool_w_in",
             "pool_w_grp", "pool_scale", "pool_w_out", "final_norm_w"]
    return (loss, loc["d_x"][None], *[out[k][0] for k in names], *[out[k][1] for k in names], *[out[k][2] for k in names],
            *[out[k][3] for k in names])
```

```python
import functools

import numpy as np
import jax
import jax.numpy as jnp
from jax import lax
from jax.experimental import pallas as pl
from jax.experimental.pallas import tpu as pltpu

F32 = jnp.float32
BF16 = jnp.bfloat16

D_MODEL = 1024
SEQ = 2048
CTX_LEN = 256
ROWS_ALL = CTX_LEN + SEQ
HEADS = 8
HEAD_DIM = 128
CHUNK = 64
N_CTX_CHUNKS = CTX_LEN // CHUNK
N_LAT_CHUNKS = SEQ // CHUNK
N_CHUNKS = N_CTX_CHUNKS + N_LAT_CHUNKS
GRID_W = 64
POOL_WINDOWS = (2, 4, 8, 16)
POOL_GROUPS = 4
POOL_GROUP_DIM = 256
HGRN_SECTIONS = 5
POOL_SECTIONS = 2
EPS = 1e-6
N_DEV = 8
N_CHIPS = 4
ROW_TILE = 256
POOL_TILE = 512
SMALL_ROWS = 24

ADAM_LR = 0.001
ADAM_B1 = 0.9
ADAM_B2 = 0.999
ADAM_EPS = 1e-08
ADAM_WD = 0.01
ADAM_STEP = 10

MESH = pl.DeviceIdType.MESH
MIB = 1 << 20
ANY = pl.BlockSpec(memory_space=pl.ANY)
VMEM = pl.BlockSpec(memory_space=pltpu.VMEM)


def _params(vmem_mib=None):
    if vmem_mib is None:
        return pltpu.CompilerParams()
    return pltpu.CompilerParams(vmem_limit_bytes=vmem_mib * MIB)


def _pin(*operands):
    return [pltpu.with_memory_space_constraint(a, pltpu.HBM) if a.size * a.dtype.itemsize >= MIB else a for a in operands]


def _sig(a):
    return 0.5 * jnp.tanh(0.5 * a) + 0.5


def _silu(a):
    return a * _sig(a)


def _dsilu(a):
    s = _sig(a)
    return s * (1.0 + a * (1.0 - s))


def _mm(a, b):
    return jnp.dot(a.astype(BF16), b.astype(BF16), preferred_element_type=F32)


def _mm_nt(a, b):
    return lax.dot_general(a.astype(BF16), b.astype(BF16), (((1,), (1,)), ((), ())), preferred_element_type=F32)


def _mm_tn(a, b):
    return lax.dot_general(a.astype(BF16), b.astype(BF16), (((0,), (0,)), ((), ())), preferred_element_type=F32)


def _split2(a):
    hi = a.astype(BF16)
    lo = (a - hi.astype(F32)).astype(BF16)
    return hi, lo


def _mm_exact_lhs(m_bf, a):
    hi, lo = _split2(a)
    return jnp.dot(m_bf, hi, preferred_element_type=F32) + jnp.dot(m_bf, lo, preferred_element_type=F32)


def _mm_f32(a, b):
    ah, al = _split2(a)
    bh, bl = _split2(b)
    return (jnp.dot(ah, bh, preferred_element_type=F32) + jnp.dot(al, bh, preferred_element_type=F32)
            + jnp.dot(ah, bl, preferred_element_type=F32))


def _my_place():
    return lax.axis_index("x"), lax.axis_index("y"), lax.axis_index("c")


def _small_gather(x_ref, out_ref, send_sems, recv_sems, local_sem):
    m_per = x_ref.shape[0]
    x, y, c = _my_place()
    me, sibling = (x, y, c), (x, y, 1 - c)
    chips = [(1 - x, y), (x, 1 - y), (1 - x, 1 - y)]

    def rows(px, py, pc):
        return out_ref.at[pl.ds((4 * px + 2 * py + pc) * m_per, m_per), :]

    def copy(k, block, to, src=None):
        return pltpu.make_async_remote_copy(
            src_ref=rows(*block) if src is None else src, dst_ref=rows(*block),
            send_sem=send_sems.at[k], recv_sem=recv_sems.at[k], device_id=to, device_id_type=MESH)

    def mine():
        return pltpu.make_async_copy(x_ref, rows(*me), local_sem)

    def first():
        return [copy(0, me, sibling, src=x_ref)] + [copy(1 + j, me, (*chip, c), src=x_ref) for j, chip in enumerate(chips)]

    def start():
        mine().start()
        for cp in first():
            cp.start()

    def finish():
        passed = [copy(4 + j, (*chip, c), sibling) for j, chip in enumerate(chips)]
        for j, chip in enumerate(chips):
            copy(1 + j, (*chip, c), me).wait_recv()
            passed[j].start()
        copy(0, sibling, me).wait_recv()
        for j, chip in enumerate(chips):
            copy(4 + j, (*chip, 1 - c), me).wait_recv()
        for cp in first() + passed:
            cp.wait_send()
        mine().wait()

    return start, finish


W_SPECS = {
    "hgrn_w_in": ((D_MODEL, 5 * D_MODEL), (1, 1280, 0, 512)),
    "hgrn_w_out": ((D_MODEL, D_MODEL), (0, 256, 0, 128)),
    "pool_w_in": ((D_MODEL, 2 * D_MODEL), (1, 512, 0, 512)),
    "pool_w_grp": ((POOL_GROUPS, POOL_GROUP_DIM, POOL_GROUP_DIM), (1, 64, 1, 32)),
    "pool_w_out": ((D_MODEL, D_MODEL), (0, 256, 0, 128)),
}

def _al(v, m):
    return pl.multiple_of(v, m)


def _region(ref, spec, chip, half):
    ca, cn, ha, hn = spec
    idx = [slice(None)] * len(ref.shape)
    if ca == ha:
        if half is None:
            idx[ca] = pl.ds(_al(chip * cn, cn), cn)
        else:
            idx[ca] = pl.ds(_al(chip * cn + half * hn, hn), hn)
    else:
        idx[ca] = pl.ds(_al(chip * cn, cn), cn)
        if half is not None:
            idx[ha] = pl.ds(_al(half * hn, hn), hn)
    return ref.at[tuple(idx)]


def _half_of(ref, spec, half):
    _, _, ha, hn = spec
    idx = [slice(None)] * len(ref.shape)
    idx[ha] = pl.ds(_al(half * hn, hn), hn)
    return ref.at[tuple(idx)]


def _half_shape(name):
    full, (ca, cn, ha, hn) = W_SPECS[name]
    shp = list(full)
    shp[ca] = cn
    shp[ha] = hn
    return tuple(shp)


def _gather_two_level(names, sh, land, full, send_sems, recv_sems, local_sems):
    n = len(names)
    specs = [W_SPECS[k][1] for k in names]
    x, y, c = _my_place()
    chip_me = 2 * x + y
    sibling = (x, y, 1 - c)
    chips = [(1 - x, y), (x, 1 - y), (1 - x, 1 - y)]

    def remote(k, src, dst, to):
        return pltpu.make_async_remote_copy(src_ref=src, dst_ref=dst, send_sem=send_sems.at[k], recv_sem=recv_sems.at[k],
                                            device_id=to, device_id_type=MESH)

    def three_halves(a):
        ca, cn, ha, hn = specs[a]
        idx = [slice(None)] * len(land[a].shape)
        if ca == ha:
            idx[ca] = pl.ds(0, 3 * hn)
        else:
            idx[ca], idx[ha] = pl.ds(0, 3 * cn), pl.ds(0, hn)
        return land[a].at[tuple(idx)]

    def own(a):
        return pltpu.make_async_copy(sh[a], _region(land[a], specs[a], chip_me, None), local_sems.at[a])

    def out(a):
        return pltpu.make_async_copy(land[a], full[a], local_sems.at[a])

    def start():
        for a in range(n):
            own(a).start()
            for px, py in chips:
                remote(a, _half_of(sh[a], specs[a], c), _region(land[a], specs[a], chip_me, c), (px, py, c)).start()

    def forward():
        for a in range(n):
            remote(a, three_halves(a), three_halves(a), sibling).wait_recv()
            for px, py in chips:
                landed = _region(land[a], specs[a], 2 * px + py, c)
                remote(n + a, landed, landed, sibling).start()

    def finish():
        for a in range(n):
            remote(n + a, three_halves(a), three_halves(a), sibling).wait_recv()
            remote(a, three_halves(a), three_halves(a), sibling).wait_send()
            remote(n + a, three_halves(a), three_halves(a), sibling).wait_send()
            own(a).wait()
        for a in range(n):
            out(a).start()
        for a in range(n):
            out(a).wait()

    return start, forward, finish


def _scatter_direct(names, part, slots, send_sems, recv_sems, local_sems):
    specs = [W_SPECS[k][1] for k in names]
    x, y, c = _my_place()
    dev_me = 4 * x + 2 * y + c

    def local(a):
        return pltpu.make_async_copy(_region(part[a], specs[a], 2 * x + y, c), slots[a].at[dev_me], local_sems.at[a])

    def start():
        for a in range(len(names)):
            local(a).start()
            for flip in range(1, N_DEV):
                tx = 1 - x if flip >> 2 else x
                ty = 1 - y if (flip >> 1) & 1 else y
                tc = 1 - c if flip & 1 else c
                pltpu.make_async_remote_copy(src_ref=_region(part[a], specs[a], 2 * tx + ty, tc), dst_ref=slots[a].at[dev_me],
                                             send_sem=send_sems.at[a], recv_sem=recv_sems.at[a], device_id=(tx, ty, tc),
                                             device_id_type=MESH).start()

    def wait():
        for a in range(len(names)):
            seven = slots[a].at[pl.ds(0, N_DEV - 1)]
            pltpu.make_async_remote_copy(src_ref=seven, dst_ref=seven, send_sem=send_sems.at[a], recv_sem=recv_sems.at[a],
                                         device_id=(x, y, c), device_id_type=MESH).wait()
            local(a).wait()

    return start, wait


HBM_SPEC = pl.BlockSpec(memory_space=pltpu.HBM)
SEM_SPEC = pl.BlockSpec(memory_space=pltpu.SEMAPHORE)
SPLIT_EFFECT = pltpu.SideEffectType.DATAFLOW_SIDE_EFFECTING


def _scatter_start(part, slots, after, name_key):
    _, cols, _, _ = W_SPECS[name_key][1]

    def body(part_ref, slots_ref, after_ref, send_sem, recv_sem, part_thru, slots_thru, token):
        x, y, c = _my_place()
        for tx, ty in ((1 - x, y), (x, 1 - y), (1 - x, 1 - y)):
            src = part_ref.at[:, pl.ds(_al((2 * tx + ty) * cols, cols), cols)]
            pltpu.make_async_remote_copy(src_ref=src, dst_ref=slots_ref.at[2 * x + y], send_sem=send_sem, recv_sem=recv_sem,
                                         device_id=(tx, ty, c), device_id_type=MESH).start()
        token[...] = jnp.zeros_like(token)

    return pl.pallas_call(
        body, name="scatter_start_" + name_key,
        out_shape=(pltpu.SemaphoreType.DMA(()), pltpu.SemaphoreType.DMA(()), pltpu.HBM(part.shape, part.dtype),
                   pltpu.HBM(slots.shape, slots.dtype), jax.ShapeDtypeStruct((8, 128), F32)),
        in_specs=(HBM_SPEC, HBM_SPEC, ANY), out_specs=(SEM_SPEC, SEM_SPEC, HBM_SPEC, HBM_SPEC, VMEM),
        input_output_aliases={0: 2, 1: 3},
        compiler_params=pltpu.CompilerParams(has_side_effects=SPLIT_EFFECT),
    )(pltpu.with_memory_space_constraint(part, pltpu.HBM), pltpu.with_memory_space_constraint(slots, pltpu.HBM), after)


def _scatter_wait(send_sem, recv_sem, part_thru, slots_thru, after, name_key):
    def body(part_ref, slots_ref, send_sem, recv_sem, *rest):
        x, y, c = _my_place()
        landed = slots_ref.at[pl.ds(0, N_CHIPS - 1)]
        copy = pltpu.make_async_remote_copy(src_ref=landed, dst_ref=landed, send_sem=send_sem, recv_sem=recv_sem,
                                            device_id=(x, y, c), device_id_type=MESH)
        copy.wait_send()
        copy.wait_recv()

    return pl.pallas_call(
        body, name="scatter_wait_" + name_key,
        out_shape=(pltpu.HBM(part_thru.shape, part_thru.dtype), pltpu.HBM(slots_thru.shape, slots_thru.dtype)),
        in_specs=(HBM_SPEC, HBM_SPEC, SEM_SPEC, SEM_SPEC) + (ANY,) * len(after), out_specs=(HBM_SPEC, HBM_SPEC),
        input_output_aliases={0: 0, 1: 1},
        compiler_params=pltpu.CompilerParams(has_side_effects=SPLIT_EFFECT),
    )(part_thru, slots_thru, send_sem, recv_sem, *after)


def _comm_sems(n):
    return [pltpu.SemaphoreType.DMA((n,)), pltpu.SemaphoreType.DMA((n,)), pltpu.SemaphoreType.DMA((n,))]


GATHER_EARLY = ("hgrn_w_in",)
GATHER_LATE = ("hgrn_w_out", "pool_w_in", "pool_w_grp", "pool_w_out")


def _weight_gather(sh, full, spec, send_sems, recv_sems, local_sem):
    x, y, c = _my_place()
    chip_me = 2 * x + y
    sibling = (x, y, 1 - c)
    chips = [(1 - x, y), (x, 1 - y), (1 - x, 1 - y)]

    def remote(k, src, dst, to):
        return pltpu.make_async_remote_copy(src_ref=src, dst_ref=dst, send_sem=send_sems.at[k], recv_sem=recv_sems.at[k],
                                            device_id=to, device_id_type=MESH)

    def own():
        return pltpu.make_async_copy(sh, _region(full, spec, chip_me, None), local_sem)

    def direct():
        return [remote(j, _half_of(sh, spec, c), _region(full, spec, chip_me, c), (px, py, c)) for j, (px, py) in enumerate(chips)]

    def start():
        own().start()
        for cp in direct():
            cp.start()

    def finish():
        passed = []
        for j, (px, py) in enumerate(chips):
            landed = _region(full, spec, 2 * px + py, c)
            remote(j, landed, landed, (px, py, c)).wait_recv()
            passed.append(remote(3 + j, landed, landed, sibling))
            passed[-1].start()
        for j, (px, py) in enumerate(chips):
            other = _region(full, spec, 2 * px + py, 1 - c)
            remote(3 + j, other, other, sibling).wait_recv()
        for cp in direct() + passed:
            cp.wait_send()
        own().wait()

    return start, finish


def _prologue(first, c_ctx, ada_w, ada_b_cols, w_shard):
    n_layers, _, n_cols = ada_w.shape
    key = GATHER_EARLY[0]
    spec = W_SPECS[key][1]
    m_first = first.shape[0]

    def body(first_ref, cctx_ref, adaw_ref, adab_ref, sh_ref, first_all, parts_all, full_ref, parts_scr,
             s1, r1, l1, s2, r2, l2, ws, wr, wl):
        start_first, finish_first = _small_gather(first_ref, first_all, s1, r1, l1)
        start_parts, finish_parts = _small_gather(parts_scr, parts_all, s2, r2, l2)
        start_weight, finish_weight = _weight_gather(sh_ref, full_ref, spec, ws, wr, wl)
        start_first()
        start_weight()
        finish_first()
        cond = jnp.concatenate([first_all[m_first * d:m_first * d + 1, 0:D_MODEL] for d in range(N_DEV)]
                               + [cctx_ref[...], jnp.zeros((16 - N_DEV - 1, D_MODEL), F32)], axis=0)
        act = _silu(cond)
        for i in range(n_layers):
            parts_scr[16 * i:16 * (i + 1), :] = _mm_f32(act, adaw_ref[i]) + adab_ref[i]
        start_parts()
        finish_parts()
        finish_weight()

    seven = [pltpu.SemaphoreType.DMA((7,)), pltpu.SemaphoreType.DMA((7,)), pltpu.SemaphoreType.DMA]
    return pl.pallas_call(
        body, name="prologue",
        out_shape=[jax.ShapeDtypeStruct((N_DEV * m_first, first.shape[1]), F32),
                   jax.ShapeDtypeStruct((N_DEV * 16 * n_layers, n_cols), F32), jax.ShapeDtypeStruct(W_SPECS[key][0], BF16)],
        in_specs=[VMEM] * 5, out_specs=[VMEM] * 3,
        scratch_shapes=[pltpu.VMEM((16 * n_layers, n_cols), F32)] + seven + seven
        + [pltpu.SemaphoreType.DMA((6,)), pltpu.SemaphoreType.DMA((6,)), pltpu.SemaphoreType.DMA],
        compiler_params=_params(48),
    )(first, c_ctx, ada_w, ada_b_cols.reshape(n_layers, 1, n_cols), w_shard)


def _slot_shapes(names):
    return [jax.ShapeDtypeStruct((N_DEV,) + _half_shape(k), BF16) for k in names]


def _sum_and_exchange(slots, names, name):
    n = len(names)
    specs = [W_SPECS[k][1] for k in names]

    def shard_shape(k):
        shp = list(_half_shape(k))
        shp[W_SPECS[k][1][2]] *= 2
        return tuple(shp)

    def body(*refs):
        slot, out, half = refs[:n], refs[n:2 * n], refs[2 * n:3 * n]
        send_sems, recv_sems, local_sems = refs[3 * n:]
        x, y, c = _my_place()
        sibling = (x, y, 1 - c)

        def remote(a, src, dst):
            return pltpu.make_async_remote_copy(src_ref=src, dst_ref=dst, send_sem=send_sems.at[a], recv_sem=recv_sems.at[a],
                                                device_id=sibling, device_id_type=MESH)

        local = [pltpu.make_async_copy(half[a], _half_of(out[a], specs[a], c), local_sems.at[a]) for a in range(n)]
        for a in range(n):
            acc = slot[a][0].astype(F32)
            for d in range(1, slot[a].shape[0]):
                acc = acc + slot[a][d].astype(F32)
            half[a][...] = acc
            local[a].start()
            remote(a, half[a], _half_of(out[a], specs[a], c)).start()
        for a in range(n):
            theirs = _half_of(out[a], specs[a], 1 - c)
            remote(a, theirs, theirs).wait_recv()
            remote(a, half[a], half[a]).wait_send()
            local[a].wait()

    return pl.pallas_call(
        body, name=name,
        out_shape=[jax.ShapeDtypeStruct(shard_shape(k), F32) for k in names],
        in_specs=[VMEM] * n, out_specs=[VMEM] * n,
        scratch_shapes=[pltpu.VMEM(_half_shape(k), F32) for k in names]
        + [pltpu.SemaphoreType.DMA((n,)), pltpu.SemaphoreType.DMA((n,)), pltpu.SemaphoreType.DMA((n,))],
        compiler_params=_params(40),
    )(*slots)


def _ln_mod_matmul(ctx_tile, xin, nw, shift, scale, w, name):
    n_mod = shift.shape[0]
    skip = n_mod - 1
    tm = ROW_TILE if skip else 2 * ROW_TILE
    rows = xin.shape[0] + skip * tm
    n_sec = w.shape[1] // D_MODEL

    def body(*refs):
        c_ref = refs[0] if skip else None
        x_ref, nw_ref, sh_ref, sc_ref, w_ref, g_ref, h_ref = refs[skip:]
        xv = jnp.where(pl.program_id(0) == 0, c_ref[...], x_ref[...]) if skip else x_ref[...]
        r = lax.rsqrt(jnp.mean(xv * xv, axis=-1, keepdims=True) + EPS)
        h_f32 = (xv * r * nw_ref[...]) * (1.0 + sc_ref[...]) + sh_ref[...]
        h = h_f32.astype(BF16)
        h_ref[...] = h
        for k in range(n_sec):
            g_ref[k] = jnp.dot(h, w_ref[:, k * D_MODEL:(k + 1) * D_MODEL], preferred_element_type=F32)

    mod_spec = pl.BlockSpec((None, 1, D_MODEL), lambda i: (jnp.minimum(i, n_mod - 1), 0, 0))
    return pl.pallas_call(
        body, name=name, grid=(rows // tm,),
        out_shape=[pltpu.HBM((n_sec, rows, D_MODEL), F32), pltpu.HBM((rows, D_MODEL), BF16)],
        in_specs=[pl.BlockSpec((tm, D_MODEL), lambda i: (0, 0))] * skip
        + [pl.BlockSpec((tm, D_MODEL), lambda i: (jnp.maximum(i - skip, 0), 0)),
           pl.BlockSpec((1, D_MODEL), lambda i: (0, 0)),
           mod_spec, mod_spec,
           pl.BlockSpec((D_MODEL, n_sec * D_MODEL), lambda i: (0, 0))],
        out_specs=[pl.BlockSpec((n_sec, tm, D_MODEL), lambda i: (0, i, 0)),
                   pl.BlockSpec((tm, D_MODEL), lambda i: (i, 0))],
        compiler_params=_params(48),
    )(*_pin(*([ctx_tile] * skip), xin, nw, shift.reshape(n_mod, 1, D_MODEL), scale.reshape(n_mod, 1, D_MODEL), w))


def _gates(fpre, lb):
    sg = _sig(fpre)
    f = lb + (1.0 - lb) * sg
    return sg, f, 1.0 - f, jnp.log(f)


G_SPEC = lambda sec: pl.BlockSpec((None, ROWS_ALL, HEAD_DIM), lambda h, sec=sec: (sec, 0, h))


GROUP = 2 * CHUNK


def _group_masks():
    rid, cid = np.arange(GROUP)[:, None], np.arange(GROUP)[None, :]
    same = (rid >= CHUNK) == (cid >= CHUNK)
    sums, back = [], []
    for rev in (False, True):
        causal = (cid >= rid) if rev else (cid <= rid)
        anti = (cid <= rid) if rev else (cid >= rid)
        sums.append(same & causal)
        back.append(np.concatenate([same & anti, same & ~anti], axis=1))
    sums, back = np.stack(sums).astype(np.float32), np.stack(back).astype(np.float32)
    return jnp.asarray(sums, BF16), jnp.asarray(back, BF16), jnp.asarray(sums, F32)


MASK_SPECS = [pl.BlockSpec((2, GROUP, GROUP), lambda h: (0, 0, 0)), pl.BlockSpec((2, GROUP, 2 * GROUP), lambda h: (0, 0, 0)),
              pl.BlockSpec((2, GROUP, GROUP), lambda h: (0, 0, 0))]


def _group_sum(m_bf, a):
    hi, lo = _split2(a)
    r = jnp.dot(m_bf, jnp.concatenate([hi, lo], axis=1), preferred_element_type=F32)
    return r[:, :HEAD_DIM] + r[:, HEAD_DIM:]


def _chunk_row(a, pos):
    return jnp.concatenate([jnp.broadcast_to(a[c * CHUNK + pos:c * CHUNK + pos + 1, :], (CHUNK, HEAD_DIM)) for c in range(2)], axis=0)


def _by_chunk(a, second):
    return jnp.concatenate([jnp.where(second, 0.0, a), jnp.where(second, a, 0.0)], axis=1)


def _own_block(r):
    return jnp.concatenate([r[0:CHUNK, 0:HEAD_DIM], r[CHUNK:GROUP, HEAD_DIM:2 * HEAD_DIM]], axis=0)


def _scan_step_of(row_chunk, rev, latent):
    if not rev:
        return row_chunk
    return (N_CHUNKS + N_CTX_CHUNKS - 1 - row_chunk) if latent else (N_CTX_CHUNKS - 1 - row_chunk)


def _group_rows(i, j, per_step, latent):
    base = CTX_LEN if latent else 0
    return pl.multiple_of(base + (i * per_step + j) * GROUP, GROUP)


GROUPS_PER_STEP = 16
GROUPS_PER_GRAD_STEP = 4


def _gla_forward(g5, lb_logits, late_shards):
    q_scale = HEAD_DIM ** -0.5
    per_lat, per_ctx = GROUPS_PER_STEP, min(GROUPS_PER_STEP, CTX_LEN // GROUP)
    n_late = len(GATHER_LATE)

    def body(ff_ref, fb_ref, v_ref, q_ref, lg_ref, sums_ref, _, keep_ref, *rest):
        shard_refs, o_ref, full_refs = rest[:n_late], rest[n_late], rest[n_late + 1:2 * n_late + 1]
        st_ref, decay_ref, qt_ref = rest[2 * n_late + 1:2 * n_late + 4]
        land_refs = rest[2 * n_late + 4:3 * n_late + 4]
        start_gather, forward_gather, finish_gather = _gather_two_level(GATHER_LATE, shard_refs, land_refs, full_refs,
                                                                        *rest[3 * n_late + 4:])

        @pl.when(pl.program_id(0) == 0)
        def _():
            start_gather()

        @pl.when(pl.program_id(0) == HEADS - 1)
        def _():
            forward_gather()

        second = lax.broadcasted_iota(jnp.int32, (GROUP, HEAD_DIM), 0) >= CHUNK
        for rev in (False, True):
            f_ref = fb_ref if rev else ff_ref
            lb = _sig(lg_ref[1:2, :] if rev else lg_ref[0:1, :])
            d = 1 if rev else 0
            last = 0 if rev else CHUNK - 1
            mid = CHUNK // 2 if rev else CHUNK // 2 - 1

            def local_step(i, carry, latent, per):
                r0s = [_group_rows(i, j, per, latent) for j in range(per)]
                rows = [pl.ds(r0, GROUP) for r0 in r0s]
                gates = [_gates(f_ref[r, :], lb) for r in rows]
                vs = [v_ref[r, :] for r in rows]
                bs = [_group_sum(sums_ref[d], g[3]) for g in gates]
                bls = [_chunk_row(b, last) for b in bs]
                ups = [_mm_tn(v, _by_chunk(g[2] * jnp.exp(bl - b), second)) for v, g, b, bl in zip(vs, gates, bs, bls)]
                if latent:
                    qs = [_silu(q_ref[r, :]) * q_scale for r in rows]
                    bms = [_chunk_row(b, mid) for b in bs]
                    a_s = [_mm_nt(q * jnp.exp(b - bm), g[2] * jnp.exp(bm - b)) for q, g, b, bm in zip(qs, gates, bs, bms)]
                    outs = [_mm(a * keep_ref[d], v) for a, v in zip(a_s, vs)]
                for j in range(per):
                    for c in range(2):
                        step = _scan_step_of(r0s[j] // CHUNK + c, rev, latent)
                        decay_ref[step] = jnp.exp(bls[j][c * CHUNK:c * CHUNK + 1, :])
                        st_ref[step] = ups[j][:, c * HEAD_DIM:(c + 1) * HEAD_DIM]
                    if latent:
                        orow = pl.ds(pl.multiple_of(r0s[j] - CTX_LEN, GROUP), GROUP)
                        qt_ref[orow, :] = (qs[j] * jnp.exp(bs[j])).astype(BF16)
                        if rev:
                            o_ref[orow, :] += outs[j]
                        else:
                            o_ref[orow, :] = outs[j]
                return carry

            lax.fori_loop(0, CTX_LEN // (per_ctx * GROUP), functools.partial(local_step, latent=False, per=per_ctx), 0)
            lax.fori_loop(0, SEQ // (per_lat * GROUP), functools.partial(local_step, latent=True, per=per_lat), 0)

            def scan_step(t, st):
                update = st_ref[t]
                st_ref[t] = st
                return st * decay_ref[t] + update

            lax.fori_loop(0, N_CHUNKS, scan_step, jnp.zeros((HEAD_DIM, HEAD_DIM), F32), unroll=2)

            def inter_step(i, carry):
                r0s = [_group_rows(i, j, per_lat, True) for j in range(per_lat)]
                orows = [pl.ds(pl.multiple_of(r0 - CTX_LEN, GROUP), GROUP) for r0 in r0s]
                states = [jnp.concatenate([st_ref[_scan_step_of(r0 // CHUNK + c, rev, True)] for c in range(2)], axis=0)
                          for r0 in r0s]
                prods = [lax.dot_general(qt_ref[orow, :], s.astype(BF16), (((1,), (1,)), ((), ())), preferred_element_type=F32)
                         for orow, s in zip(orows, states)]
                for orow, r in zip(orows, prods):
                    o_ref[orow, :] += _own_block(r)
                return carry

            lax.fori_loop(0, SEQ // (per_lat * GROUP), inter_step, 0)

        @pl.when(pl.program_id(0) == HEADS - 1)
        def _():
            finish_gather()

    outs = pl.pallas_call(
        body, name="gla_forward", grid=(HEADS,),
        out_shape=[pltpu.HBM((SEQ, D_MODEL), F32)] + [jax.ShapeDtypeStruct(W_SPECS[k][0], BF16) for k in GATHER_LATE],
        in_specs=[G_SPEC(0), G_SPEC(1), G_SPEC(2), G_SPEC(3), pl.BlockSpec((2, HEAD_DIM), lambda h: (0, h))] + MASK_SPECS
        + [ANY] * n_late,
        out_specs=[pl.BlockSpec((SEQ, HEAD_DIM), lambda h: (0, h))] + [ANY] * n_late,
        scratch_shapes=[pltpu.VMEM((N_CHUNKS, HEAD_DIM, HEAD_DIM), F32), pltpu.VMEM((N_CHUNKS, 1, HEAD_DIM), F32),
                        pltpu.VMEM((SEQ, HEAD_DIM), BF16)] + [pltpu.VMEM(W_SPECS[k][0], BF16) for k in GATHER_LATE]
        + [pltpu.SemaphoreType.DMA((2 * n_late,)), pltpu.SemaphoreType.DMA((2 * n_late,)), pltpu.SemaphoreType.DMA((n_late,))],
        compiler_params=_params(48),
    )(*_pin(g5, g5, g5, g5, lb_logits), *_group_masks(), *late_shards)
    return outs[0], dict(zip(GATHER_LATE, outs[1:]))


def _gla_backward(g5, lb_logits, d_o, d_z, late_parts):
    q_scale = HEAD_DIM ** -0.5
    per_lat, per_ctx = GROUPS_PER_STEP, min(GROUPS_PER_STEP, CTX_LEN // GROUP)
    n_late = len(GATHER_LATE)

    def body(ff_ref, fb_ref, v_ref, q_ref, lg_ref, do_ref, dz_ref, sums_ref, back_ref, keep_ref, *rest):
        part_refs, (dg_ref, dlg_ref), slot_refs = rest[:n_late], rest[n_late:n_late + 2], rest[n_late + 2:2 * n_late + 2]
        st_ref, dst_ref, decay_ref, both_ref = rest[2 * n_late + 2:2 * n_late + 6]
        start_scatter, wait_scatter = _scatter_direct(GATHER_LATE, part_refs, slot_refs, *rest[2 * n_late + 6:])

        @pl.when(pl.program_id(0) == 0)
        def _():
            start_scatter()

        dg_ref[3, 0:CTX_LEN, :] = jnp.zeros((CTX_LEN, HEAD_DIM), BF16)
        dg_ref[4, 0:CTX_LEN, :] = jnp.zeros((CTX_LEN, HEAD_DIM), BF16)
        dg_ref[4, CTX_LEN:ROWS_ALL, :] = dz_ref[...]
        second = lax.broadcasted_iota(jnp.int32, (GROUP, HEAD_DIM), 0) >= CHUNK
        for rev in (False, True):
            d = 1 if rev else 0
            f_ref = fb_ref if rev else ff_ref
            lb = _sig(lg_ref[d:d + 1, :])
            last = 0 if rev else CHUNK - 1
            mid = CHUNK // 2 if rev else CHUNK // 2 - 1

            def local_step(i, carry, latent, per):
                r0s = [_group_rows(i, j, per, latent) for j in range(per)]
                rows = [pl.ds(r0, GROUP) for r0 in r0s]
                gates = [_gates(f_ref[r, :], lb) for r in rows]
                bs = [_group_sum(sums_ref[d], g[3]) for g in gates]
                bls = [_chunk_row(b, last) for b in bs]
                ups = [_mm_tn(v_ref[r, :], _by_chunk(g[2] * jnp.exp(bl - b), second)) for r, g, b, bl in zip(rows, gates, bs, bls)]
                if latent:
                    orows = [pl.ds(pl.multiple_of(r0 - CTX_LEN, GROUP), GROUP) for r0 in r0s]
                    d_ups = [_mm_tn(do_ref[orow, :], _by_chunk(_silu(q_ref[r, :]) * q_scale * jnp.exp(b), second))
                             for orow, r, b in zip(orows, rows, bs)]
                for j in range(per):
                    for c in range(2):
                        step = _scan_step_of(r0s[j] // CHUNK + c, rev, latent)
                        decay_ref[step] = jnp.exp(bls[j][c * CHUNK:c * CHUNK + 1, :])
                        st_ref[step] = ups[j][:, c * HEAD_DIM:(c + 1) * HEAD_DIM]
                        if latent:
                            dst_ref[step] = d_ups[j][:, c * HEAD_DIM:(c + 1) * HEAD_DIM]
                        else:
                            dst_ref[step] = jnp.zeros((HEAD_DIM, HEAD_DIM), F32)
                return carry

            lax.fori_loop(0, CTX_LEN // (per_ctx * GROUP), functools.partial(local_step, latent=False, per=per_ctx), 0)
            lax.fori_loop(0, SEQ // (per_lat * GROUP), functools.partial(local_step, latent=True, per=per_lat), 0)

            def scan_step(i, carry):
                st, d_st = carry
                j = N_CHUNKS - 1 - i
                update, d_update = st_ref[i], dst_ref[j]
                st_ref[i] = st
                dst_ref[j] = d_st
                return st * decay_ref[i] + update, d_st * decay_ref[j] + d_update

            zero_state = jnp.zeros((HEAD_DIM, HEAD_DIM), F32)
            lax.fori_loop(0, N_CHUNKS, scan_step, (zero_state, zero_state))

            def grad_step(i, dlb, latent, per):
                r0s = [_group_rows(i, j, per, latent) for j in range(per)]
                rows = [pl.ds(r0, GROUP) for r0 in r0s]
                gates = [_gates(f_ref[r, :], lb) for r in rows]
                vs = [v_ref[r, :] for r in rows]
                bs = [_group_sum(sums_ref[d], g[3]) for g in gates]
                bls = [_chunk_row(b, last) for b in bs]
                e_ends = [jnp.exp(bl - b) for b, bl in zip(bs, bls)]
                k_ends = [g[2] * e for g, e in zip(gates, e_ends)]
                sts = [[st_ref[_scan_step_of(r0 // CHUNK + c, rev, latent)] for c in range(2)] for r0 in r0s]
                d_sts = [[dst_ref[_scan_step_of(r0 // CHUNK + c, rev, latent)] for c in range(2)] for r0 in r0s]
                d_kends = [_own_block(_mm(v, jnp.concatenate(ds, axis=1))) for v, ds in zip(vs, d_sts)]
                d_vs = [_own_block(_mm_nt(ke, jnp.concatenate(ds, axis=0))) for ke, ds in zip(k_ends, d_sts)]
                at_last = [jnp.concatenate([jnp.broadcast_to(jnp.sum(ds[c] * s[c], axis=0, keepdims=True), (CHUNK, HEAD_DIM))
                                            for c in range(2)], axis=0) * jnp.exp(bl) for ds, s, bl in zip(d_sts, sts, bls)]
                t_kends = [dk * ke for dk, ke in zip(d_kends, k_ends)]
                d_ks = [dk * e for dk, e in zip(d_kends, e_ends)]
                if latent:
                    orows = [pl.ds(pl.multiple_of(r0 - CTX_LEN, GROUP), GROUP) for r0 in r0s]
                    qpres = [q_ref[r, :] for r in rows]
                    q_sigs = [_sig(qp) for qp in qpres]
                    qs = [qp * sg * q_scale for qp, sg in zip(qpres, q_sigs)]
                    bms = [_chunk_row(b, mid) for b in bs]
                    e_bs = [jnp.exp(b) for b in bs]
                    e_qms = [jnp.exp(b - bm) for b, bm in zip(bs, bms)]
                    e_kms = [jnp.exp(bm - b) for b, bm in zip(bs, bms)]
                    q_ts = [q * e for q, e in zip(qs, e_bs)]
                    q_ms = [q * e for q, e in zip(qs, e_qms)]
                    k_ms = [g[2] * e for g, e in zip(gates, e_kms)]
                    d_outs = [do_ref[orow, :] for orow in orows]
                    a_s = [_mm_nt(qm, km) * keep_ref[d] for qm, km in zip(q_ms, k_ms)]
                    d_as = [_mm_nt(do, v) * keep_ref[d] for do, v in zip(d_outs, vs)]
                    d_qts = [_own_block(_mm(do, jnp.concatenate(s, axis=1))) for do, s in zip(d_outs, sts)]
                    d_qms = [_mm(da, km) for da, km in zip(d_as, k_ms)]
                    d_kms = [_mm_tn(da, qm) for da, qm in zip(d_as, q_ms)]
                    d_vs = [dv + _mm_tn(a, do) for dv, a, do in zip(d_vs, a_s, d_outs)]
                    d_ks = [dk + dkm * e for dk, dkm, e in zip(d_ks, d_kms, e_kms)]
                    d_bs = [jnp.concatenate([dqt * qt + dqm * qm - dkm * km, t], axis=0)
                            for dqt, qt, dqm, qm, dkm, km, t in zip(d_qts, q_ts, d_qms, q_ms, d_kms, k_ms, t_kends)]
                    d_qs = [dqt * eb + dqm * eq for dqt, eb, dqm, eq in zip(d_qts, e_bs, d_qms, e_qms)]
                    back = back_ref[d]
                else:
                    d_bs, back = t_kends, back_ref[d, :, GROUP:2 * GROUP]
                d_lfs = [_group_sum(back, db) + al for db, al in zip(d_bs, at_last)]
                for j in range(per):
                    sg, f = gates[j][0], gates[j][1]
                    d_f = d_lfs[j] / f - d_ks[j]
                    dg_ref[d, rows[j], :] = (d_f * (1.0 - lb) * sg * (1.0 - sg)).astype(BF16)
                    dlb = dlb + jnp.sum(d_f * (1.0 - sg), axis=0, keepdims=True)
                    if rev:
                        dg_ref[2, rows[j], :] = (both_ref[0, rows[j], :] + d_vs[j]).astype(BF16)
                    else:
                        both_ref[0, rows[j], :] = d_vs[j]
                    if latent:
                        d_qpre = d_qs[j] * q_scale * (q_sigs[j] * (1.0 + qpres[j] * (1.0 - q_sigs[j])))
                        if rev:
                            dg_ref[3, rows[j], :] = (both_ref[1, rows[j], :] + d_qpre).astype(BF16)
                        else:
                            both_ref[1, rows[j], :] = d_qpre
                return dlb

            dlb = lax.fori_loop(0, SEQ // (GROUPS_PER_GRAD_STEP * GROUP),
                                functools.partial(grad_step, latent=True, per=GROUPS_PER_GRAD_STEP), jnp.zeros((1, HEAD_DIM), F32))
            dlb = lax.fori_loop(0, CTX_LEN // (per_ctx * GROUP), functools.partial(grad_step, latent=False, per=per_ctx), dlb)
            dlg_ref[d:d + 1, :] = dlb * lb * (1.0 - lb)

        @pl.when(pl.program_id(0) == HEADS - 1)
        def _():
            wait_scatter()

    col = pl.BlockSpec((SEQ, HEAD_DIM), lambda h: (0, h))
    outs = pl.pallas_call(
        body, name="gla_backward", grid=(HEADS,),
        out_shape=[pltpu.HBM((HGRN_SECTIONS, ROWS_ALL, D_MODEL), BF16), jax.ShapeDtypeStruct((2, D_MODEL), F32)]
        + _slot_shapes(GATHER_LATE),
        in_specs=[G_SPEC(0), G_SPEC(1), G_SPEC(2), G_SPEC(3), pl.BlockSpec((2, HEAD_DIM), lambda h: (0, h)), col, col]
        + MASK_SPECS + [ANY] * n_late,
        out_specs=[pl.BlockSpec((HGRN_SECTIONS, ROWS_ALL, HEAD_DIM), lambda h: (0, 0, h)),
                   pl.BlockSpec((2, HEAD_DIM), lambda h: (0, h))] + [ANY] * n_late,
        scratch_shapes=[pltpu.VMEM((N_CHUNKS, HEAD_DIM, HEAD_DIM), F32), pltpu.VMEM((N_CHUNKS, HEAD_DIM, HEAD_DIM), F32),
                        pltpu.VMEM((N_CHUNKS, 1, HEAD_DIM), F32), pltpu.VMEM((2, ROWS_ALL, HEAD_DIM), F32)]
        + _comm_sems(n_late),
        compiler_params=_params(48),
    )(*_pin(g5, g5, g5, g5, lb_logits, d_o, d_z), *_group_masks(), *late_parts)
    return outs[0], outs[1], dict(zip(GATHER_LATE, outs[2:]))


def _head_norm(o, scr):
    rs = []
    for h in range(HEADS):
        cols = slice(h * HEAD_DIM, (h + 1) * HEAD_DIM)
        oh = o[:, cols]
        r = lax.rsqrt(jnp.mean(oh * oh, axis=-1, keepdims=True) + EPS)
        scr[:, cols] = oh * r
        rs.append(r)
    return rs


def _hgrn_out_forward(o_raw, g5, xin, gnorm_w, gate, w_out):
    tm = ROW_TILE

    def body(o_ref, z_ref, x_ref, gw_ref, gate_ref, w_ref, x1_ref, res_ref, scr):
        _head_norm(o_ref[...], scr)
        a = scr[...] * gw_ref[...] * _silu(z_ref[...])
        res = _mm(a, w_ref[...])
        res_ref[...] = res
        x1_ref[...] = x_ref[...] + gate_ref[...] * res

    tile = pl.BlockSpec((tm, D_MODEL), lambda i: (i, 0))
    vec = pl.BlockSpec((1, D_MODEL), lambda i: (0, 0))
    return pl.pallas_call(
        body, name="hgrn_out_forward", grid=(SEQ // tm,),
        out_shape=[pltpu.HBM((SEQ, D_MODEL), F32)] * 2,
        in_specs=[tile, pl.BlockSpec((None, tm, D_MODEL), lambda i: (4, i + CTX_LEN // tm, 0)), tile, vec, vec,
                  pl.BlockSpec((D_MODEL, D_MODEL), lambda i: (0, 0))],
        out_specs=[tile, tile],
        scratch_shapes=[pltpu.VMEM((tm, D_MODEL), F32)],
        compiler_params=_params(32),
    )(*_pin(o_raw, g5, xin, gnorm_w, gate, w_out))


def _hgrn_out_backward(d_x1, o_raw, g5, res, gnorm_w, gate, w_out):
    tm = ROW_TILE

    def body(dx_ref, o_ref, z_ref, res_ref, gw_ref, gate_ref, w_ref, do_ref, dz_ref, dw_out, dgate_ref, dgw_ref, scr, scr2,
             dw_ref):
        @pl.when(pl.program_id(0) == 0)
        def _():
            dw_ref[...] = jnp.zeros_like(dw_ref)
            dgate_ref[...] = jnp.zeros_like(dgate_ref)
            dgw_ref[...] = jnp.zeros_like(dgw_ref)

        dx = dx_ref[...]
        dgate_ref[...] += jnp.sum(dx * res_ref[...], axis=0, keepdims=True)
        d_res = (dx * gate_ref[...]).astype(BF16)
        d_a = _mm_nt(d_res, w_ref[...])
        rs = _head_norm(o_ref[...], scr)
        z = z_ref[...]
        sz = _silu(z)
        o_hat = scr[...]
        o_n = o_hat * gw_ref[...]
        dw_ref[...] += _mm_tn(o_n * sz, d_res)
        d_on = d_a * sz
        dz_ref[...] = (d_a * o_n * _dsilu(z)).astype(BF16)
        dgw_ref[...] += jnp.sum(d_on * o_hat, axis=0, keepdims=True)
        scr2[...] = d_on * gw_ref[...]
        for h in range(HEADS):
            cols = slice(h * HEAD_DIM, (h + 1) * HEAD_DIM)
            dh, oh = scr2[:, cols], scr[:, cols]
            do_ref[:, cols] = (rs[h] * (dh - oh * jnp.mean(dh * oh, axis=-1, keepdims=True))).astype(BF16)

        @pl.when(pl.program_id(0) == SEQ // tm - 1)
        def _():
            dw_out[...] = dw_ref[...].astype(BF16)

    tile = pl.BlockSpec((tm, D_MODEL), lambda i: (i, 0))
    vec = pl.BlockSpec((1, D_MODEL), lambda i: (0, 0))
    mat = pl.BlockSpec((D_MODEL, D_MODEL), lambda i: (0, 0))
    return pl.pallas_call(
        body, name="hgrn_out_backward", grid=(SEQ // tm,),
        out_shape=[pltpu.HBM((SEQ, D_MODEL), BF16)] * 2 + [pltpu.HBM((D_MODEL, D_MODEL), BF16)]
        + [jax.ShapeDtypeStruct((1, D_MODEL), F32)] * 2,
        in_specs=[tile, tile, pl.BlockSpec((None, tm, D_MODEL), lambda i: (4, i + CTX_LEN // tm, 0)), tile, vec, vec, mat],
        out_specs=[tile, tile, mat, vec, vec],
        scratch_shapes=[pltpu.VMEM((tm, D_MODEL), F32)] * 2 + [pltpu.VMEM((D_MODEL, D_MODEL), F32)],
        compiler_params=_params(40),
    )(*_pin(d_x1, o_raw, g5, res, gnorm_w, gate, w_out))


def _pool_constants():
    win = np.zeros((POOL_GROUPS, ROW_TILE, ROW_TILE), np.float32)
    inv = np.zeros((POOL_GROUPS, ROW_TILE, 1), np.float32)
    for g, w in enumerate(POOL_WINDOWS):
        for t in range(ROW_TILE):
            base, p = (t // GRID_W) * GRID_W, t % GRID_W
            lo = min(max(p - w // 2, 0), GRID_W)
            hi = min(max(p - w // 2 + w, 0), GRID_W)
            win[g, t, base + lo:base + hi] = 1.0
            inv[g, t, 0] = 1.0 / np.float32(hi - lo)
    return jnp.asarray(win, BF16), jnp.asarray(win.transpose(0, 2, 1), BF16), jnp.asarray(inv, F32)


def _pool_mix(u_ref, wg_ref, win_ref, inv_ref, pooled_scr, yg_scr):
    for g in range(POOL_GROUPS):
        cols = slice(g * POOL_GROUP_DIM, (g + 1) * POOL_GROUP_DIM)
        for r in range(u_ref.shape[0] // ROW_TILE):
            rows = slice(r * ROW_TILE, (r + 1) * ROW_TILE)
            ug = u_ref[rows, cols]
            pooled_scr[rows, cols] = _mm_exact_lhs(win_ref[g], ug) * inv_ref[g] - ug
        yg_scr[:, cols] = _mm(pooled_scr[:, cols], wg_ref[g])


def _pool_forward_loss(uz, x1, target, gate, w_grp, pool_scale, w_out, final_w):
    tm = POOL_TILE
    win, _, inv = _pool_constants()

    def body(u_ref, z_ref, x_ref, t_ref, gate_ref, wg_ref, ps_ref, w_ref, fw_ref, win_ref, inv_ref,
             dx_ref, loss_ref, dfw_ref, dgate_ref, yg_scr, pooled_scr):
        @pl.when(pl.program_id(0) == 0)
        def _():
            loss_ref[...] = jnp.zeros_like(loss_ref)
            dfw_ref[...] = jnp.zeros_like(dfw_ref)
            dgate_ref[...] = jnp.zeros_like(dgate_ref)

        _pool_mix(u_ref, wg_ref, win_ref, inv_ref, pooled_scr, yg_scr)
        a = yg_scr[...] * ps_ref[...] * _silu(z_ref[...])
        res = _mm(a, w_ref[...])
        x2 = x_ref[...] + gate_ref[...] * res
        r = lax.rsqrt(jnp.mean(x2 * x2, axis=-1, keepdims=True) + EPS)
        xh = x2 * r
        fw = fw_ref[...]
        err = xh * fw - t_ref[...]
        loss_ref[...] += 0.5 * jnp.sum(jnp.mean(err * err, axis=-1, keepdims=True))
        d_y = err * (1.0 / D_MODEL)
        dfw_ref[...] += jnp.sum(d_y * xh, axis=0, keepdims=True)
        d_xh = d_y * fw
        d_x2 = r * (d_xh - xh * jnp.mean(d_xh * xh, axis=-1, keepdims=True))
        dx_ref[...] = d_x2
        dgate_ref[...] += jnp.sum(d_x2 * res, axis=0, keepdims=True)

    tile = pl.BlockSpec((tm, D_MODEL), lambda i: (i, 0))
    vec = pl.BlockSpec((1, D_MODEL), lambda i: (0, 0))
    grp = pl.BlockSpec((POOL_GROUPS, POOL_GROUP_DIM, POOL_GROUP_DIM), lambda i: (0, 0, 0))
    return pl.pallas_call(
        body, name="pool_forward_loss", grid=(SEQ // tm,),
        out_shape=[pltpu.HBM((SEQ, D_MODEL), F32), jax.ShapeDtypeStruct((8, 128), F32),
                   jax.ShapeDtypeStruct((1, D_MODEL), F32), jax.ShapeDtypeStruct((1, D_MODEL), F32)],
        in_specs=[pl.BlockSpec((None, tm, D_MODEL), lambda i: (0, i, 0)), pl.BlockSpec((None, tm, D_MODEL), lambda i: (1, i, 0)),
                  tile, tile, vec, grp, vec, pl.BlockSpec((D_MODEL, D_MODEL), lambda i: (0, 0)), vec, grp,
                  pl.BlockSpec((POOL_GROUPS, ROW_TILE, 1), lambda i: (0, 0, 0))],
        out_specs=[tile, pl.BlockSpec((8, 128), lambda i: (0, 0)), vec, vec],
        scratch_shapes=[pltpu.VMEM((tm, D_MODEL), F32)] * 2,
        compiler_params=_params(40),
    )(*_pin(uz, uz, x1, target, gate, w_grp, pool_scale, w_out, final_w, win, inv))


def _pool_backward(d_x2, uz, gate, w_grp, pool_scale, w_out):
    tm = POOL_TILE
    win, win_t, inv = _pool_constants()

    def body(dx_ref, u_ref, z_ref, gate_ref, wg_ref, ps_ref, w_ref, win_ref, wint_ref, inv_ref,
             duz_ref, dw_out, dwg_out, dps_ref, pooled_scr, yg_scr, dyg_scr, dw_ref, dwg_ref):
        @pl.when(pl.program_id(0) == 0)
        def _():
            dw_ref[...] = jnp.zeros_like(dw_ref)
            dwg_ref[...] = jnp.zeros_like(dwg_ref)
            dps_ref[...] = jnp.zeros_like(dps_ref)

        _pool_mix(u_ref, wg_ref, win_ref, inv_ref, pooled_scr, yg_scr)
        z = z_ref[...]
        sz = _silu(z)
        yg = yg_scr[...]
        y = yg * ps_ref[...]
        d_res = (dx_ref[...] * gate_ref[...]).astype(BF16)
        d_a = _mm_nt(d_res, w_ref[...])
        dw_ref[...] += _mm_tn(y * sz, d_res)
        d_y = d_a * sz
        duz_ref[1] = (d_a * y * _dsilu(z)).astype(BF16)
        dps_ref[...] += jnp.sum(d_y * yg, axis=0, keepdims=True)
        dyg_scr[...] = d_y * ps_ref[...]
        for g in range(POOL_GROUPS):
            cols = slice(g * POOL_GROUP_DIM, (g + 1) * POOL_GROUP_DIM)
            d_yg = dyg_scr[:, cols].astype(BF16)
            d_pool = _mm_nt(d_yg, wg_ref[g])
            dwg_ref[g] += _mm_tn(pooled_scr[:, cols], d_yg)
            for r in range(tm // ROW_TILE):
                rows = slice(r * ROW_TILE, (r + 1) * ROW_TILE)
                dp = d_pool[rows, :]
                duz_ref[0, rows, cols] = (_mm_exact_lhs(wint_ref[g], dp * inv_ref[g]) - dp).astype(BF16)

        @pl.when(pl.program_id(0) == SEQ // tm - 1)
        def _():
            dw_out[...] = dw_ref[...].astype(BF16)
            dwg_out[...] = dwg_ref[...].astype(BF16)

    tile = pl.BlockSpec((tm, D_MODEL), lambda i: (i, 0))
    vec = pl.BlockSpec((1, D_MODEL), lambda i: (0, 0))
    mat = pl.BlockSpec((D_MODEL, D_MODEL), lambda i: (0, 0))
    grp = pl.BlockSpec((POOL_GROUPS, POOL_GROUP_DIM, POOL_GROUP_DIM), lambda i: (0, 0, 0))
    return pl.pallas_call(
        body, name="pool_backward", grid=(SEQ // tm,),
        out_shape=[pltpu.HBM((POOL_SECTIONS, SEQ, D_MODEL), BF16), pltpu.HBM((D_MODEL, D_MODEL), BF16),
                   pltpu.HBM((POOL_GROUPS, POOL_GROUP_DIM, POOL_GROUP_DIM), BF16), jax.ShapeDtypeStruct((1, D_MODEL), F32)],
        in_specs=[tile, pl.BlockSpec((None, tm, D_MODEL), lambda i: (0, i, 0)), pl.BlockSpec((None, tm, D_MODEL), lambda i: (1, i, 0)),
                  vec, grp, vec, mat, grp, grp, pl.BlockSpec((POOL_GROUPS, ROW_TILE, 1), lambda i: (0, 0, 0))],
        out_specs=[pl.BlockSpec((POOL_SECTIONS, tm, D_MODEL), lambda i: (0, i, 0)), mat, grp, vec],
        scratch_shapes=[pltpu.VMEM((tm, D_MODEL), F32)] * 3 + [pltpu.VMEM((D_MODEL, D_MODEL), F32),
                                                               pltpu.VMEM((POOL_GROUPS, POOL_GROUP_DIM, POOL_GROUP_DIM), F32)],
        compiler_params=_params(48),
    )(*_pin(d_x2, uz, uz, gate, w_grp, pool_scale, w_out, win, win_t, inv))


def _ln_mod_backward(d_g, w, ctx_tile, xin, nw, scale, d_up, name):
    n_sec, rows, _ = d_g.shape
    n_mod = scale.shape[0]
    skip = n_mod - 1
    tm = ROW_TILE if skip else 2 * ROW_TILE
    n_tiles = rows // tm

    def body(dg_ref, w_ref, *refs):
        c_ref = refs[0] if skip else None
        x_ref, nw_ref, sc_ref, up_ref, dx_ref, dnw_ref, dmod_ref = refs[skip:]
        i = pl.program_id(0)

        @pl.when(i == 0)
        def _():
            dnw_ref[...] = jnp.zeros_like(dnw_ref)

        @pl.when((i == 0) | (i == skip))
        def _():
            dmod_ref[...] = jnp.zeros_like(dmod_ref)

        d_h = _mm_nt(dg_ref[0], w_ref[:, 0:D_MODEL])
        for k in range(1, n_sec):
            d_h = d_h + _mm_nt(dg_ref[k], w_ref[:, k * D_MODEL:(k + 1) * D_MODEL])
        xv = jnp.where(i == 0, c_ref[...], x_ref[...]) if skip else x_ref[...]
        r = lax.rsqrt(jnp.mean(xv * xv, axis=-1, keepdims=True) + EPS)
        xh = xv * r
        nw_row = nw_ref[...]
        dmod_ref[0:1, :] += jnp.sum(d_h, axis=0, keepdims=True)
        dmod_ref[1:2, :] += jnp.sum(d_h * (xh * nw_row), axis=0, keepdims=True)
        d_xn = d_h * (1.0 + sc_ref[...])
        dnw_ref[...] += jnp.sum(d_xn * xh, axis=0, keepdims=True)
        d_xh = d_xn * nw_row

        @pl.when(i >= skip)
        def _():
            dx_ref[...] = up_ref[...] + r * (d_xh - xh * jnp.mean(d_xh * xh, axis=-1, keepdims=True))

    lat = lambda i: (jnp.maximum(i - skip, 0), 0)
    mod_idx = lambda i: (jnp.minimum(i, n_mod - 1), 0, 0)
    return pl.pallas_call(
        body, name=name, grid=(n_tiles,),
        out_shape=[pltpu.HBM((rows - skip * tm, D_MODEL), F32), jax.ShapeDtypeStruct((1, D_MODEL), F32),
                   jax.ShapeDtypeStruct((n_mod, 8, D_MODEL), F32)],
        in_specs=[pl.BlockSpec((n_sec, tm, D_MODEL), lambda i: (0, i, 0)),
                  pl.BlockSpec((D_MODEL, n_sec * D_MODEL), lambda i: (0, 0))]
        + [pl.BlockSpec((tm, D_MODEL), lambda i: (0, 0))] * skip
        + [pl.BlockSpec((tm, D_MODEL), lat),
           pl.BlockSpec((1, D_MODEL), lambda i: (0, 0)),
           pl.BlockSpec((None, 1, D_MODEL), mod_idx),
           pl.BlockSpec((tm, D_MODEL), lat)],
        out_specs=[pl.BlockSpec((tm, D_MODEL), lat), pl.BlockSpec((1, D_MODEL), lambda i: (0, 0)),
                   pl.BlockSpec((None, 8, D_MODEL), mod_idx)],
        compiler_params=_params(48),
    )(*_pin(d_g, w, *([ctx_tile] * skip), xin, nw, scale.reshape(n_mod, 1, D_MODEL), d_up))


def _weight_grad(h, d_g, name):
    n_sec, rows, _ = d_g.shape
    tm = 768 if rows % 768 == 0 else 512
    n_tiles = rows // tm

    def body(h_ref, dg_ref, dw_ref, acc):
        i = pl.program_id(1)
        prod = _mm_tn(h_ref[...], dg_ref[...])

        @pl.when(i == 0)
        def _():
            acc[...] = prod

        @pl.when((i > 0) & (i < n_tiles - 1))
        def _():
            acc[...] += prod

        @pl.when(i == n_tiles - 1)
        def _():
            dw_ref[...] = (acc[...] + prod).astype(BF16)

    return pl.pallas_call(
        body, name=name, grid=(n_sec, n_tiles),
        out_shape=pltpu.HBM((D_MODEL, n_sec * D_MODEL), BF16),
        in_specs=[pl.BlockSpec((tm, D_MODEL), lambda j, i: (i, 0)), pl.BlockSpec((None, tm, D_MODEL), lambda j, i: (j, i, 0))],
        out_specs=pl.BlockSpec((D_MODEL, D_MODEL), lambda j, i: (0, j)),
        scratch_shapes=[pltpu.VMEM((D_MODEL, D_MODEL), F32)],
        compiler_params=_params(32),
    )(*_pin(h, d_g))


def _weight_grad_paired(h, d_g, name):
    n_sec, rows, _ = d_g.shape
    tm = 768
    n_tiles = rows // tm
    half = D_MODEL // 2
    chip_cols = n_sec * D_MODEL // N_CHIPS

    def body(h_ref, dg_ref, q_ref, slots_ref, acc, keep, send, land, send_sems, recv_sems, own_sem):
        j, i = pl.program_id(0), pl.program_id(1)
        x, y, c = _my_place()
        prod = _mm_tn(h_ref[...], dg_ref[...])

        def to_sibling(k):
            return pltpu.make_async_remote_copy(src_ref=send.at[k], dst_ref=land.at[k], send_sem=send_sems.at[k],
                                                recv_sem=recv_sems.at[k], device_id=(x, y, 1 - c), device_id_type=MESH)

        @pl.when(i == 0)
        def _():
            acc[...] = prod

        @pl.when((i > 0) & (i < n_tiles - 1))
        def _():
            acc[...] += prod

        @pl.when(i == n_tiles - 1)
        def _():
            acc[...] += prod
            keep[j] = acc[pl.ds(_al(c * half, half), half), :]
            send[j] = acc[pl.ds(_al((1 - c) * half, half), half), :].astype(BF16)
            to_sibling(j).start()

        @pl.when((j == n_sec - 1) & (i == n_tiles - 1))
        def _():
            for k in range(n_sec):
                to_sibling(k).wait()
                q_ref[:, k * D_MODEL:(k + 1) * D_MODEL] = (keep[k] + land[k].astype(F32)).astype(BF16)
            chip = 2 * x + y
            own = pltpu.make_async_copy(q_ref.at[:, pl.ds(_al(chip * chip_cols, chip_cols), chip_cols)], slots_ref.at[chip], own_sem)
            own.start()
            own.wait()

    return pl.pallas_call(
        body, name=name, grid=(n_sec, n_tiles),
        out_shape=[jax.ShapeDtypeStruct((half, n_sec * D_MODEL), BF16),
                   jax.ShapeDtypeStruct((N_CHIPS, half, chip_cols), BF16)],
        in_specs=[pl.BlockSpec((tm, D_MODEL), lambda j, i: (i, 0)), pl.BlockSpec((None, tm, D_MODEL), lambda j, i: (j, i, 0))],
        out_specs=[pl.BlockSpec((half, n_sec * D_MODEL), lambda j, i: (0, 0)), ANY],
        scratch_shapes=[pltpu.VMEM((D_MODEL, D_MODEL), F32), pltpu.VMEM((n_sec, half, D_MODEL), F32),
                        pltpu.VMEM((n_sec, half, D_MODEL), BF16), pltpu.VMEM((n_sec, half, D_MODEL), BF16),
                        pltpu.SemaphoreType.DMA((n_sec,)), pltpu.SemaphoreType.DMA((n_sec,)), pltpu.SemaphoreType.DMA],
        compiler_params=_params(56),
    )(*_pin(h, d_g))


def _adamw_math(w, g, m, v):
    m = ADAM_B1 * m + (1.0 - ADAM_B1) * g
    v = ADAM_B2 * v + (1.0 - ADAM_B2) * (g * g)
    m_hat = m / (1.0 - ADAM_B1 ** ADAM_STEP)
    v_hat = v / (1.0 - ADAM_B2 ** ADAM_STEP)
    return -ADAM_LR * (m_hat / (jnp.sqrt(v_hat) + ADAM_EPS) + ADAM_WD * w), m, v


def _adamw(w, g, m, v, name):
    rows, cols = w.shape
    tr = rows if rows <= 128 else 128

    def body(w_ref, g_ref, m_ref, v_ref, d_ref, mo_ref, vo_ref):
        d_ref[...], mo_ref[...], vo_ref[...] = _adamw_math(w_ref[...], g_ref[...], m_ref[...], v_ref[...])

    tile = pl.BlockSpec((tr, cols), lambda i: (i, 0))
    return pl.pallas_call(
        body, name=name, grid=(rows // tr,),
        out_shape=[pltpu.HBM((rows, cols), F32)] * 3,
        in_specs=[tile] * 4, out_specs=[tile] * 3,
    )(*_pin(w, g, m, v))


def _adamw_small(quads, name, vmem_mib=None):
    n = len(quads)

    def body(*refs):
        ins, outs = refs[:4 * n], refs[4 * n:]
        for a in range(n):
            w_ref, g_ref, m_ref, v_ref = ins[4 * a:4 * a + 4]
            d_ref, mo_ref, vo_ref = outs[3 * a:3 * a + 3]
            d_ref[...], mo_ref[...], vo_ref[...] = _adamw_math(w_ref[...], g_ref[...], m_ref[...], v_ref[...])

    outs = pl.pallas_call(
        body, name=name, out_shape=[jax.ShapeDtypeStruct(q[0].shape, F32) for q in quads for _ in range(3)],
        in_specs=[VMEM] * (4 * n), out_specs=[VMEM] * (3 * n), compiler_params=_params(vmem_mib),
    )(*[a for q in quads for a in q])
    return [tuple(outs[3 * a:3 * a + 3]) for a in range(n)]


def _small_sums_and_cond_ctx(small, ada_w, c_ctx, m, v):
    n_cols = ada_w.shape[2]

    def body(small_ref, w_ref, c_ref, m_ref, v_ref, all_ref, sums_ref, g_ref, d_ref, mo_ref, vo_ref, part_ref, parts_ref,
             s1, r1, l1, s2, r2, l2):
        start_slabs, finish_slabs = _small_gather(small_ref, all_ref, s1, r1, l1)
        start_parts, finish_parts = _small_gather(part_ref, parts_ref, s2, r2, l2)
        start_slabs()
        finish_slabs()
        acc = all_ref[0:SMALL_ROWS, :]
        for dev in range(1, N_DEV):
            acc = acc + all_ref[dev * SMALL_ROWS:(dev + 1) * SMALL_ROWS, :]
        sums_ref[...] = acc
        x, y, _ = _my_place()
        d_modc = jnp.concatenate([acc[6:7, :], acc[7:8, :], acc[8:9, :]], axis=1)
        mine = jnp.zeros((1, n_cols), F32)
        for s in range(N_CHIPS):
            mine = mine + jnp.where(2 * x + y == s, d_modc[:, s * n_cols:(s + 1) * n_cols], 0.0)
        ah, al = _split2(jnp.broadcast_to(mine, (8, n_cols)))
        wh, wl = _split2(w_ref[0])
        nt = lambda a, b: lax.dot_general(a, b, (((1,), (1,)), ((), ())), preferred_element_type=F32)
        part_ref[...] = nt(ah, wh) + nt(al, wh) + nt(ah, wl)
        start_parts()
        finish_parts()
        total = parts_ref[0:1, :]
        for s in range(1, N_CHIPS):
            total = total + parts_ref[16 * s:16 * s + 1, :]
        w = c_ref[...]
        g = total * _dsilu(w)
        g_ref[...] = g
        d_ref[...], mo_ref[...], vo_ref[...] = _adamw_math(w, g, m_ref[...], v_ref[...])

    seven = [pltpu.SemaphoreType.DMA((7,)), pltpu.SemaphoreType.DMA((7,)), pltpu.SemaphoreType.DMA]
    vec = jax.ShapeDtypeStruct((1, D_MODEL), F32)
    return pl.pallas_call(
        body, name="small_sums_and_cond_ctx",
        out_shape=[jax.ShapeDtypeStruct((N_DEV * SMALL_ROWS, D_MODEL), F32), jax.ShapeDtypeStruct((SMALL_ROWS, D_MODEL), F32),
                   vec, vec, vec, vec],
        in_specs=[VMEM] * 5, out_specs=[VMEM] * 6,
        scratch_shapes=[pltpu.VMEM((8, D_MODEL), F32), pltpu.VMEM((N_DEV * 8, D_MODEL), F32)] + seven + seven,
    )(small, ada_w, c_ctx, m, v)


def _ada_update(cond_t, d_mod, ada_w, m, v):
    n_layers, _, n_cols = ada_w.shape
    tr = ROW_TILE

    def body(c_ref, dm_ref, w_ref, m_ref, v_ref, g_ref, d_ref, mo_ref, vo_ref):
        g = _mm_f32(_silu(c_ref[...]), dm_ref[...])
        g_ref[...] = g
        d_ref[...], mo_ref[...], vo_ref[...] = _adamw_math(w_ref[...], g, m_ref[...], v_ref[...])

    tile = pl.BlockSpec((None, tr, n_cols), lambda l, i: (l, i, 0))
    return pl.pallas_call(
        body, name="ada_update", grid=(n_layers, D_MODEL // tr),
        out_shape=[pltpu.HBM(ada_w.shape, F32)] * 4,
        in_specs=[pl.BlockSpec((tr, 16), lambda l, i: (i, 0)), pl.BlockSpec((None, 16, n_cols), lambda l, i: (l, 0, 0)),
                  tile, tile, tile],
        out_specs=[tile] * 4,
    )(*_pin(cond_t, d_mod, ada_w, m, v))


def _local_step(x2, ctx2, target, mod_mine, mod_ctx, lb_logits, scale_full, w_in_full, late_shards, norm_w, gnorm, final_w):
    row = lambda a: a.reshape(1, -1)
    shift0, scale0, gate0 = (row(a) for a in jnp.split(mod_mine[0], 3))
    shift1, scale1, gate1 = (row(a) for a in jnp.split(mod_mine[1], 3))
    shift_c, scale_c, _ = (row(a) for a in jnp.split(mod_ctx, 3))
    nw0, nw1 = norm_w[0:1], norm_w[1:2]
    scales0 = jnp.concatenate([scale_c, scale0])

    g5, h0 = _ln_mod_matmul(ctx2, x2, nw0, jnp.concatenate([shift_c, shift0]), scales0, w_in_full, "hgrn_in_forward")
    o_raw, full = _gla_forward(g5, lb_logits, late_shards)
    x1, res0 = _hgrn_out_forward(o_raw, g5, x2, gnorm, gate0, full["hgrn_w_out"])
    uz, h1 = _ln_mod_matmul(None, x1, nw1, shift1, scale1, full["pool_w_in"], "pool_in_forward")
    d_x2, loss_part, d_final, d_gate1 = _pool_forward_loss(uz, x1, target, gate1, full["pool_w_grp"], scale_full,
                                                           full["pool_w_out"], final_w)

    d_uz, dw_pool_out, dw_pool_grp, d_pscale = _pool_backward(d_x2, uz, gate1, full["pool_w_grp"], scale_full, full["pool_w_out"])
    d_x1, d_nw1, d_mod1 = _ln_mod_backward(d_uz, full["pool_w_in"], None, x1, nw1, scale1, d_x2, "pool_in_backward")
    dw_pool_in = _weight_grad(h1, d_uz, "pool_in_weight_grad")
    d_o, d_z, dw_hgrn_out, d_gate0, d_gnorm = _hgrn_out_backward(d_x1, o_raw, g5, res0, gnorm, gate0, full["hgrn_w_out"])
    late_grads = {"hgrn_w_out": dw_hgrn_out, "pool_w_in": dw_pool_in, "pool_w_grp": dw_pool_grp, "pool_w_out": dw_pool_out}
    d_g5, d_lb, late_slots = _gla_backward(g5, lb_logits, d_o, d_z, [late_grads[k].astype(BF16) for k in GATHER_LATE])
    dw_hgrn_in, slots0 = _weight_grad_paired(h0, d_g5, "hgrn_in_weight_grad")
    key = GATHER_EARLY[0]
    send_sem, recv_sem, part_thru, slots_thru, token = _scatter_start(dw_hgrn_in, slots0, d_lb, key)
    d_x, d_nw0, d_mod0 = _ln_mod_backward(d_g5, w_in_full, ctx2, x2, nw0 + token[0:1, 0:1], scales0, d_x1, "hgrn_in_backward")
    slots = dict(late_slots)
    pending = (send_sem, recv_sem, part_thru, slots_thru)

    zero = jnp.zeros((1, D_MODEL), F32)
    small = jnp.concatenate([d_mod0[1, 0:2], d_gate0, d_mod1[0, 0:2], d_gate1, d_mod0[0, 0:2], zero, d_nw0, d_nw1, d_gnorm,
                             d_final, d_pscale, d_lb, jnp.broadcast_to(loss_part[0:1, 0:1], (1, D_MODEL)),
                             jnp.zeros((SMALL_ROWS - 17, D_MODEL), F32)], axis=0)
    return {"d_x": d_x, "slots": slots, "pending": pending, "small": small}


def kernel(x, c, ctx, c_ctx, ada_w, ada_b, norm_w, hgrn_w_in, hgrn_lb_logits, hgrn_gnorm_w, hgrn_w_out, pool_w_in, pool_w_grp, pool_scale, pool_w_out, final_norm_w, loss_target, m_c_ctx, m_ada_w, m_ada_b, m_norm_w, m_hgrn_w_in, m_hgrn_lb_logits, m_hgrn_gnorm_w, m_hgrn_w_out, m_pool_w_in, m_pool_w_grp, m_pool_scale, m_pool_w_out, m_final_norm_w, v_c_ctx, v_ada_w, v_ada_b, v_norm_w, v_hgrn_w_in, v_hgrn_lb_logits, v_hgrn_gnorm_w, v_hgrn_w_out, v_pool_w_in, v_pool_w_grp, v_pool_scale, v_pool_w_out, v_final_norm_w):
    xi, yi, ci = _my_place()
    chip = 2 * xi + yi
    dev = 4 * xi + 2 * yi + ci
    ada_cols = ada_w.shape[2]
    lb_cols = hgrn_lb_logits.shape[2]
    ps_cols = pool_scale.shape[1]
    row = lambda a: a.reshape(1, -1)

    def chip_cols(a, n):
        return lax.dynamic_slice_in_dim(a, chip * n, n, axis=a.ndim - 1)

    def from_chips(g, rows_per_dev, take):
        return jnp.concatenate([g[2 * s * rows_per_dev:2 * s * rows_per_dev + take] for s in range(N_CHIPS)], axis=1)

    first = jnp.concatenate([jnp.broadcast_to(c, (8, D_MODEL)), jnp.pad(hgrn_lb_logits[0], ((0, 6), (0, 0))),
                             jnp.pad(pool_scale, ((0, 7), (0, 0)))], axis=1)
    shards = {"hgrn_w_in": hgrn_w_in[0], "hgrn_w_out": hgrn_w_out[0], "pool_w_in": pool_w_in[0],
              "pool_w_grp": pool_w_grp[0], "pool_w_out": pool_w_out[0]}
    first_all, parts_all, w_in_full = _prologue(first, row(c_ctx), ada_w, chip_cols(ada_b, ada_cols),
                                                shards[GATHER_EARLY[0]].astype(BF16))
    cond_all = first_all[::8, :D_MODEL]
    lb_logits = from_chips(first_all[:, D_MODEL:D_MODEL + lb_cols], 8, 2)
    scale_full = from_chips(first_all[:, D_MODEL + lb_cols:], 8, 1)
    cond_rows = jnp.concatenate([cond_all, row(c_ctx), jnp.zeros((7, D_MODEL), F32)], axis=0)
    mod_all = from_chips(parts_all, 32, 32).reshape(2, 16, 3 * D_MODEL)
    mod_mine = lax.dynamic_index_in_dim(mod_all, dev, axis=1, keepdims=False)

    loc = _local_step(x[0], ctx[0], loss_target[0], mod_mine, mod_all[0, 8], lb_logits, scale_full, w_in_full,
                      [shards[k].astype(BF16) for k in GATHER_LATE], norm_w, hgrn_gnorm_w, row(final_norm_w))

    small_all, sums, g_c, d_c, m_c, v_c = _small_sums_and_cond_ctx(loc["small"], ada_w, row(c_ctx), row(m_c_ctx), row(v_c_ctx))
    loss = sums[16, 0]

    def reduce_scattered(slots, names, name):
        return dict(zip(names, _sum_and_exchange([slots[k] for k in names], names, name)))

    big_grads = reduce_scattered(loc["slots"], GATHER_LATE, "exchange_halves_late")

    out = {}

    def update(name, w, g, m, v):
        shape = w.shape
        w2, g2, m2, v2 = (a.reshape(-1, shape[-1]) for a in (w, g, m, v))
        d, mn, vn = _adamw(w2, g2, m2, v2, "adamw_" + name)
        out[name] = tuple(a.reshape(shape) for a in (g2, d, mn, vn))

    moments = {"hgrn_w_in": (m_hgrn_w_in, v_hgrn_w_in), "hgrn_w_out": (m_hgrn_w_out, v_hgrn_w_out),
               "pool_w_in": (m_pool_w_in, v_pool_w_in), "pool_w_grp": (m_pool_w_grp, v_pool_w_grp),
               "pool_w_out": (m_pool_w_out, v_pool_w_out)}
    weights = {"hgrn_w_in": hgrn_w_in, "hgrn_w_out": hgrn_w_out, "pool_w_in": pool_w_in, "pool_w_grp": pool_w_grp,
               "pool_w_out": pool_w_out}
    late = [tuple(a.reshape(-1, weights[k].shape[-1]) for a in (weights[k], big_grads[k], *moments[k])) for k in GATHER_LATE]
    for k, quad, (d, mn, vn) in zip(GATHER_LATE, late, _adamw_small(late, "adamw_late", 48)):
        out[k] = tuple(a.reshape(weights[k].shape) for a in (quad[1], d, mn, vn))

    g_ada_b = jnp.stack([(sums[0:3] + sums[6:9]).reshape(-1), sums[3:6].reshape(-1)])
    small_w = {"ada_b": (ada_b, g_ada_b, m_ada_b, v_ada_b),
               "norm_w": (norm_w, sums[9:11], m_norm_w, v_norm_w),
               "hgrn_gnorm_w": (hgrn_gnorm_w, sums[11:12], m_hgrn_gnorm_w, v_hgrn_gnorm_w),
               "final_norm_w": (row(final_norm_w), sums[12:13], row(m_final_norm_w), row(v_final_norm_w)),
               "pool_scale": (pool_scale, chip_cols(sums[13:14], ps_cols), m_pool_scale, v_pool_scale),
               "hgrn_lb_logits": (hgrn_lb_logits, chip_cols(sums[14:16], lb_cols), m_hgrn_lb_logits, v_hgrn_lb_logits)}
    quads = [tuple(a.reshape(-1, q[0].shape[-1]) for a in q) for q in small_w.values()]
    for (k, q), quad, (d, mn, vn) in zip(small_w.items(), quads, _adamw_small(quads, "adamw_small")):
        out[k] = tuple(a.reshape(q[0].shape) for a in (quad[1], d, mn, vn))

    per_dev = small_all.reshape(N_DEV, SMALL_ROWS, D_MODEL)
    pad7 = jnp.zeros((7, 3 * D_MODEL), F32)
    dm0 = jnp.concatenate([per_dev[:, 0:3].reshape(N_DEV, -1), sums[6:9].reshape(1, -1), pad7], axis=0)
    dm1 = jnp.concatenate([per_dev[:, 3:6].reshape(N_DEV, -1), jnp.zeros((8, 3 * D_MODEL), F32)], axis=0)
    d_mod = chip_cols(jnp.stack([dm0, dm1]), ada_cols)
    out["ada_w"] = _ada_update(cond_rows.T, d_mod, ada_w, m_ada_w, v_ada_w)

    out["c_ctx"] = tuple(a.reshape(-1) for a in (g_c, d_c, m_c, v_c))
    out["final_norm_w"] = tuple(a.reshape(-1) for a in out["final_norm_w"])

    done = [g_c, out["ada_w"][1]] + [out[k][1] for k in GATHER_LATE]
    key = GATHER_EARLY[0]
    _, early = _scatter_wait(*loc["pending"], done, key)
    big_grads = reduce_scattered({key: early}, GATHER_EARLY, "exchange_halves_early")
    for k in GATHER_EARLY:
        update(k, weights[k], big_grads[k], *moments[k])

    names = ["c_ctx", "ada_w", "ada_b", "norm_w", "hgrn_w_in", "hgrn_lb_logits", "hgrn_gnorm_w", "hgrn_w_out", "pool_w_in",
             "pool_w_grp", "pool_scale", "pool_w_out", "final_norm_w"]
    return (loss, loc["d_x"][None], *[out[k][0] for k in names], *[out[k][1] for k in names], *[out[k][2] for k in names],
            *[out[k][3] for k in names])
```

```python
import functools

import numpy as np
import jax
import jax.numpy as jnp
from jax import lax
from jax.experimental import pallas as pl
from jax.experimental.pallas import tpu as pltpu

F32 = jnp.float32
BF16 = jnp.bfloat16

D_MODEL = 1024
SEQ = 2048
CTX_LEN = 256
ROWS_ALL = CTX_LEN + SEQ
HEADS = 8
HEAD_DIM = 128
CHUNK = 64
N_CTX_CHUNKS = CTX_LEN // CHUNK
N_LAT_CHUNKS = SEQ // CHUNK
N_CHUNKS = N_CTX_CHUNKS + N_LAT_CHUNKS
GRID_W = 64
POOL_WINDOWS = (2, 4, 8, 16)
POOL_GROUPS = 4
POOL_GROUP_DIM = 256
HGRN_SECTIONS = 5
POOL_SECTIONS = 2
EPS = 1e-6
N_DEV = 8
N_CHIPS = 4
ROW_TILE = 256
POOL_TILE = 512
SMALL_ROWS = 24

ADAM_LR = 0.001
ADAM_B1 = 0.9
ADAM_B2 = 0.999
ADAM_EPS = 1e-08
ADAM_WD = 0.01
ADAM_STEP = 10

MESH = pl.DeviceIdType.MESH
MIB = 1 << 20
ANY = pl.BlockSpec(memory_space=pl.ANY)
VMEM = pl.BlockSpec(memory_space=pltpu.VMEM)


def _params(vmem_mib=None):
    if vmem_mib is None:
        return pltpu.CompilerParams()
    return pltpu.CompilerParams(vmem_limit_bytes=vmem_mib * MIB)


def _pin(*operands):
    return [pltpu.with_memory_space_constraint(a, pltpu.HBM) if a.size * a.dtype.itemsize >= MIB else a for a in operands]


def _sig(a):
    return 0.5 * jnp.tanh(0.5 * a) + 0.5


def _silu(a):
    return a * _sig(a)


def _dsilu(a):
    s = _sig(a)
    return s * (1.0 + a * (1.0 - s))


def _mm(a, b):
    return jnp.dot(a.astype(BF16), b.astype(BF16), preferred_element_type=F32)


def _mm_nt(a, b):
    return lax.dot_general(a.astype(BF16), b.astype(BF16), (((1,), (1,)), ((), ())), preferred_element_type=F32)


def _mm_tn(a, b):
    return lax.dot_general(a.astype(BF16), b.astype(BF16), (((0,), (0,)), ((), ())), preferred_element_type=F32)


def _split2(a):
    hi = a.astype(BF16)
    lo = (a - hi.astype(F32)).astype(BF16)
    return hi, lo


def _mm_exact_lhs(m_bf, a):
    hi, lo = _split2(a)
    return jnp.dot(m_bf, hi, preferred_element_type=F32) + jnp.dot(m_bf, lo, preferred_element_type=F32)


def _mm_f32(a, b):
    ah, al = _split2(a)
    bh, bl = _split2(b)
    return (jnp.dot(ah, bh, preferred_element_type=F32) + jnp.dot(al, bh, preferred_element_type=F32)
            + jnp.dot(ah, bl, preferred_element_type=F32))


def _my_place():
    return lax.axis_index("x"), lax.axis_index("y"), lax.axis_index("c")


def _small_gather(x_ref, out_ref, send_sems, recv_sems, local_sem):
    m_per = x_ref.shape[0]
    x, y, c = _my_place()
    me, sibling = (x, y, c), (x, y, 1 - c)
    chips = [(1 - x, y), (x, 1 - y), (1 - x, 1 - y)]

    def rows(px, py, pc):
        return out_ref.at[pl.ds((4 * px + 2 * py + pc) * m_per, m_per), :]

    def copy(k, block, to, src=None):
        return pltpu.make_async_remote_copy(
            src_ref=rows(*block) if src is None else src, dst_ref=rows(*block),
            send_sem=send_sems.at[k], recv_sem=recv_sems.at[k], device_id=to, device_id_type=MESH)

    def mine():
        return pltpu.make_async_copy(x_ref, rows(*me), local_sem)

    def first():
        return [copy(0, me, sibling, src=x_ref)] + [copy(1 + j, me, (*chip, c), src=x_ref) for j, chip in enumerate(chips)]

    def start():
        mine().start()
        for cp in first():
            cp.start()

    def finish():
        passed = [copy(4 + j, (*chip, c), sibling) for j, chip in enumerate(chips)]
        for j, chip in enumerate(chips):
            copy(1 + j, (*chip, c), me).wait_recv()
            passed[j].start()
        copy(0, sibling, me).wait_recv()
        for j, chip in enumerate(chips):
            copy(4 + j, (*chip, 1 - c), me).wait_recv()
        for cp in first() + passed:
            cp.wait_send()
        mine().wait()

    return start, finish


W_SPECS = {
    "hgrn_w_in": ((D_MODEL, 5 * D_MODEL), (1, 1280, 0, 512)),
    "hgrn_w_out": ((D_MODEL, D_MODEL), (0, 256, 0, 128)),
    "pool_w_in": ((D_MODEL, 2 * D_MODEL), (1, 512, 0, 512)),
    "pool_w_grp": ((POOL_GROUPS, POOL_GROUP_DIM, POOL_GROUP_DIM), (1, 64, 1, 32)),
    "pool_w_out": ((D_MODEL, D_MODEL), (0, 256, 0, 128)),
}

def _al(v, m):
    return pl.multiple_of(v, m)


def _region(ref, spec, chip, half):
    ca, cn, ha, hn = spec
    idx = [slice(None)] * len(ref.shape)
    if ca == ha:
        if half is None:
            idx[ca] = pl.ds(_al(chip * cn, cn), cn)
        else:
            idx[ca] = pl.ds(_al(chip * cn + half * hn, hn), hn)
    else:
        idx[ca] = pl.ds(_al(chip * cn, cn), cn)
        if half is not None:
            idx[ha] = pl.ds(_al(half * hn, hn), hn)
    return ref.at[tuple(idx)]


def _half_of(ref, spec, half):
    _, _, ha, hn = spec
    idx = [slice(None)] * len(ref.shape)
    idx[ha] = pl.ds(_al(half * hn, hn), hn)
    return ref.at[tuple(idx)]


def _half_shape(name):
    full, (ca, cn, ha, hn) = W_SPECS[name]
    shp = list(full)
    shp[ca] = cn
    shp[ha] = hn
    return tuple(shp)


def _gather_two_level(names, sh, land, full, send_sems, recv_sems, local_sems):
    n = len(names)
    specs = [W_SPECS[k][1] for k in names]
    x, y, c = _my_place()
    chip_me = 2 * x + y
    sibling = (x, y, 1 - c)
    chips = [(1 - x, y), (x, 1 - y), (1 - x, 1 - y)]

    def remote(k, src, dst, to):
        return pltpu.make_async_remote_copy(src_ref=src, dst_ref=dst, send_sem=send_sems.at[k], recv_sem=recv_sems.at[k],
                                            device_id=to, device_id_type=MESH)

    def three_halves(a):
        ca, cn, ha, hn = specs[a]
        idx = [slice(None)] * len(land[a].shape)
        if ca == ha:
            idx[ca] = pl.ds(0, 3 * hn)
        else:
            idx[ca], idx[ha] = pl.ds(0, 3 * cn), pl.ds(0, hn)
        return land[a].at[tuple(idx)]

    def own(a):
        return pltpu.make_async_copy(sh[a], _region(land[a], specs[a], chip_me, None), local_sems.at[a])

    def out(a):
        return pltpu.make_async_copy(land[a], full[a], local_sems.at[a])

    def start():
        for a in range(n):
            own(a).start()
            for px, py in chips:
                remote(a, _half_of(sh[a], specs[a], c), _region(land[a], specs[a], chip_me, c), (px, py, c)).start()

    def forward():
        for a in range(n):
            remote(a, three_halves(a), three_halves(a), sibling).wait_recv()
            for px, py in chips:
                landed = _region(land[a], specs[a], 2 * px + py, c)
                remote(n + a, landed, landed, sibling).start()

    def finish():
        for a in range(n):
            remote(n + a, three_halves(a), three_halves(a), sibling).wait_recv()
            remote(a, three_halves(a), three_halves(a), sibling).wait_send()
            remote(n + a, three_halves(a), three_halves(a), sibling).wait_send()
            own(a).wait()
        for a in range(n):
            out(a).start()
        for a in range(n):
            out(a).wait()

    return start, forward, finish


def _scatter_direct(names, part, slots, send_sems, recv_sems, local_sems):
    specs = [W_SPECS[k][1] for k in names]
    x, y, c = _my_place()
    dev_me = 4 * x + 2 * y + c

    def local(a):
        return pltpu.make_async_copy(_region(part[a], specs[a], 2 * x + y, c), slots[a].at[dev_me], local_sems.at[a])

    def start():
        for a in range(len(names)):
            local(a).start()
            for flip in range(1, N_DEV):
                tx = 1 - x if flip >> 2 else x
                ty = 1 - y if (flip >> 1) & 1 else y
                tc = 1 - c if flip & 1 else c
                pltpu.make_async_remote_copy(src_ref=_region(part[a], specs[a], 2 * tx + ty, tc), dst_ref=slots[a].at[dev_me],
                                             send_sem=send_sems.at[a], recv_sem=recv_sems.at[a], device_id=(tx, ty, tc),
                                             device_id_type=MESH).start()

    def wait():
        for a in range(len(names)):
            seven = slots[a].at[pl.ds(0, N_DEV - 1)]
            pltpu.make_async_remote_copy(src_ref=seven, dst_ref=seven, send_sem=send_sems.at[a], recv_sem=recv_sems.at[a],
                                         device_id=(x, y, c), device_id_type=MESH).wait()
            local(a).wait()

    return start, wait


HBM_SPEC = pl.BlockSpec(memory_space=pltpu.HBM)
SEM_SPEC = pl.BlockSpec(memory_space=pltpu.SEMAPHORE)
SPLIT_EFFECT = pltpu.SideEffectType.DATAFLOW_SIDE_EFFECTING


def _scatter_start(part, slots, after, name_key):
    _, cols, _, _ = W_SPECS[name_key][1]

    def body(part_ref, slots_ref, after_ref, send_sem, recv_sem, part_thru, slots_thru, token):
        x, y, c = _my_place()
        for tx, ty in ((1 - x, y), (x, 1 - y), (1 - x, 1 - y)):
            src = part_ref.at[:, pl.ds(_al((2 * tx + ty) * cols, cols), cols)]
            pltpu.make_async_remote_copy(src_ref=src, dst_ref=slots_ref.at[2 * x + y], send_sem=send_sem, recv_sem=recv_sem,
                                         device_id=(tx, ty, c), device_id_type=MESH).start()
        token[...] = jnp.zeros_like(token)

    return pl.pallas_call(
        body, name="scatter_start_" + name_key,
        out_shape=(pltpu.SemaphoreType.DMA(()), pltpu.SemaphoreType.DMA(()), pltpu.HBM(part.shape, part.dtype),
                   pltpu.HBM(slots.shape, slots.dtype), jax.ShapeDtypeStruct((8, 128), F32)),
        in_specs=(HBM_SPEC, HBM_SPEC, ANY), out_specs=(SEM_SPEC, SEM_SPEC, HBM_SPEC, HBM_SPEC, VMEM),
        input_output_aliases={0: 2, 1: 3},
        compiler_params=pltpu.CompilerParams(has_side_effects=SPLIT_EFFECT),
    )(pltpu.with_memory_space_constraint(part, pltpu.HBM), pltpu.with_memory_space_constraint(slots, pltpu.HBM), after)


def _scatter_wait(send_sem, recv_sem, part_thru, slots_thru, after, name_key):
    def body(part_ref, slots_ref, send_sem, recv_sem, *rest):
        x, y, c = _my_place()
        landed = slots_ref.at[pl.ds(0, N_CHIPS - 1)]
        copy = pltpu.make_async_remote_copy(src_ref=landed, dst_ref=landed, send_sem=send_sem, recv_sem=recv_sem,
                                            device_id=(x, y, c), device_id_type=MESH)
        copy.wait_send()
        copy.wait_recv()

    return pl.pallas_call(
        body, name="scatter_wait_" + name_key,
        out_shape=(pltpu.HBM(part_thru.shape, part_thru.dtype), pltpu.HBM(slots_thru.shape, slots_thru.dtype)),
        in_specs=(HBM_SPEC, HBM_SPEC, SEM_SPEC, SEM_SPEC) + (ANY,) * len(after), out_specs=(HBM_SPEC, HBM_SPEC),
        input_output_aliases={0: 0, 1: 1},
        compiler_params=pltpu.CompilerParams(has_side_effects=SPLIT_EFFECT),
    )(part_thru, slots_thru, send_sem, recv_sem, *after)


def _comm_sems(n):
    return [pltpu.SemaphoreType.DMA((n,)), pltpu.SemaphoreType.DMA((n,)), pltpu.SemaphoreType.DMA((n,))]


GATHER_EARLY = ("hgrn_w_in",)
GATHER_LATE = ("hgrn_w_out", "pool_w_in", "pool_w_grp", "pool_w_out")


def _weight_gather(sh, full, spec, send_sems, recv_sems, local_sem):
    x, y, c = _my_place()
    chip_me = 2 * x + y
    sibling = (x, y, 1 - c)
    chips = [(1 - x, y), (x, 1 - y), (1 - x, 1 - y)]

    def remote(k, src, dst, to):
        return pltpu.make_async_remote_copy(src_ref=src, dst_ref=dst, send_sem=send_sems.at[k], recv_sem=recv_sems.at[k],
                                            device_id=to, device_id_type=MESH)

    def own():
        return pltpu.make_async_copy(sh, _region(full, spec, chip_me, None), local_sem)

    def direct():
        return [remote(j, _half_of(sh, spec, c), _region(full, spec, chip_me, c), (px, py, c)) for j, (px, py) in enumerate(chips)]

    def start():
        own().start()
        for cp in direct():
            cp.start()

    def finish():
        passed = []
        for j, (px, py) in enumerate(chips):
            landed = _region(full, spec, 2 * px + py, c)
            remote(j, landed, landed, (px, py, c)).wait_recv()
            passed.append(remote(3 + j, landed, landed, sibling))
            passed[-1].start()
        for j, (px, py) in enumerate(chips):
            other = _region(full, spec, 2 * px + py, 1 - c)
            remote(3 + j, other, other, sibling).wait_recv()
        for cp in direct() + passed:
            cp.wait_send()
        own().wait()

    return start, finish


def _prologue(first, c_ctx, ada_w, ada_b_cols, w_shard):
    n_layers, _, n_cols = ada_w.shape
    key = GATHER_EARLY[0]
    spec = W_SPECS[key][1]
    m_first = first.shape[0]

    def body(first_ref, cctx_ref, adaw_ref, adab_ref, sh_ref, first_all, parts_all, full_ref, parts_scr,
             s1, r1, l1, s2, r2, l2, ws, wr, wl):
        start_first, finish_first = _small_gather(first_ref, first_all, s1, r1, l1)
        start_parts, finish_parts = _small_gather(parts_scr, parts_all, s2, r2, l2)
        start_weight, finish_weight = _weight_gather(sh_ref, full_ref, spec, ws, wr, wl)
        start_first()
        start_weight()
        finish_first()
        cond = jnp.concatenate([first_all[m_first * d:m_first * d + 1, 0:D_MODEL] for d in range(N_DEV)]
                               + [cctx_ref[...], jnp.zeros((16 - N_DEV - 1, D_MODEL), F32)], axis=0)
        act = _silu(cond)
        for i in range(n_layers):
            parts_scr[16 * i:16 * (i + 1), :] = _mm_f32(act, adaw_ref[i]) + adab_ref[i]
        start_parts()
        finish_parts()
        finish_weight()

    seven = [pltpu.SemaphoreType.DMA((7,)), pltpu.SemaphoreType.DMA((7,)), pltpu.SemaphoreType.DMA]
    return pl.pallas_call(
        body, name="prologue",
        out_shape=[jax.ShapeDtypeStruct((N_DEV * m_first, first.shape[1]), F32),
                   jax.ShapeDtypeStruct((N_DEV * 16 * n_layers, n_cols), F32), jax.ShapeDtypeStruct(W_SPECS[key][0], BF16)],
        in_specs=[VMEM] * 5, out_specs=[VMEM] * 3,
        scratch_shapes=[pltpu.VMEM((16 * n_layers, n_cols), F32)] + seven + seven
        + [pltpu.SemaphoreType.DMA((6,)), pltpu.SemaphoreType.DMA((6,)), pltpu.SemaphoreType.DMA],
        compiler_params=_params(48),
    )(first, c_ctx, ada_w, ada_b_cols.reshape(n_layers, 1, n_cols), w_shard)


def _slot_shapes(names):
    return [jax.ShapeDtypeStruct((N_DEV,) + _half_shape(k), BF16) for k in names]


def _sum_and_exchange(slots, names, name):
    n = len(names)
    specs = [W_SPECS[k][1] for k in names]

    def shard_shape(k):
        shp = list(_half_shape(k))
        shp[W_SPECS[k][1][2]] *= 2
        return tuple(shp)

    def body(*refs):
        slot, out, half = refs[:n], refs[n:2 * n], refs[2 * n:3 * n]
        send_sems, recv_sems, local_sems = refs[3 * n:]
        x, y, c = _my_place()
        sibling = (x, y, 1 - c)

        def remote(a, src, dst):
            return pltpu.make_async_remote_copy(src_ref=src, dst_ref=dst, send_sem=send_sems.at[a], recv_sem=recv_sems.at[a],
                                                device_id=sibling, device_id_type=MESH)

        local = [pltpu.make_async_copy(half[a], _half_of(out[a], specs[a], c), local_sems.at[a]) for a in range(n)]
        for a in range(n):
            acc = slot[a][0].astype(F32)
            for d in range(1, slot[a].shape[0]):
                acc = acc + slot[a][d].astype(F32)
            half[a][...] = acc
            local[a].start()
            remote(a, half[a], _half_of(out[a], specs[a], c)).start()
        for a in range(n):
            theirs = _half_of(out[a], specs[a], 1 - c)
            remote(a, theirs, theirs).wait_recv()
            remote(a, half[a], half[a]).wait_send()
            local[a].wait()

    return pl.pallas_call(
        body, name=name,
        out_shape=[jax.ShapeDtypeStruct(shard_shape(k), F32) for k in names],
        in_specs=[VMEM] * n, out_specs=[VMEM] * n,
        scratch_shapes=[pltpu.VMEM(_half_shape(k), F32) for k in names]
        + [pltpu.SemaphoreType.DMA((n,)), pltpu.SemaphoreType.DMA((n,)), pltpu.SemaphoreType.DMA((n,))],
        compiler_params=_params(40),
    )(*slots)


def _ln_mod_matmul(ctx_tile, xin, nw, shift, scale, w, name):
    n_mod = shift.shape[0]
    skip = n_mod - 1
    tm = ROW_TILE if skip else 2 * ROW_TILE
    rows = xin.shape[0] + skip * tm
    n_sec = w.shape[1] // D_MODEL

    def body(*refs):
        c_ref = refs[0] if skip else None
        x_ref, nw_ref, sh_ref, sc_ref, w_ref, g_ref, h_ref = refs[skip:]
        xv = jnp.where(pl.program_id(0) == 0, c_ref[...], x_ref[...]) if skip else x_ref[...]
        r = lax.rsqrt(jnp.mean(xv * xv, axis=-1, keepdims=True) + EPS)
        h_f32 = (xv * r * nw_ref[...]) * (1.0 + sc_ref[...]) + sh_ref[...]
        h = h_f32.astype(BF16)
        h_ref[...] = h
        for k in range(n_sec):
            g_ref[k] = jnp.dot(h, w_ref[:, k * D_MODEL:(k + 1) * D_MODEL], preferred_element_type=F32)

    mod_spec = pl.BlockSpec((None, 1, D_MODEL), lambda i: (jnp.minimum(i, n_mod - 1), 0, 0))
    return pl.pallas_call(
        body, name=name, grid=(rows // tm,),
        out_shape=[pltpu.HBM((n_sec, rows, D_MODEL), F32), pltpu.HBM((rows, D_MODEL), BF16)],
        in_specs=[pl.BlockSpec((tm, D_MODEL), lambda i: (0, 0))] * skip
        + [pl.BlockSpec((tm, D_MODEL), lambda i: (jnp.maximum(i - skip, 0), 0)),
           pl.BlockSpec((1, D_MODEL), lambda i: (0, 0)),
           mod_spec, mod_spec,
           pl.BlockSpec((D_MODEL, n_sec * D_MODEL), lambda i: (0, 0))],
        out_specs=[pl.BlockSpec((n_sec, tm, D_MODEL), lambda i: (0, i, 0)),
                   pl.BlockSpec((tm, D_MODEL), lambda i: (i, 0))],
        compiler_params=_params(48),
    )(*_pin(*([ctx_tile] * skip), xin, nw, shift.reshape(n_mod, 1, D_MODEL), scale.reshape(n_mod, 1, D_MODEL), w))


def _gates(fpre, lb):
    sg = _sig(fpre)
    f = lb + (1.0 - lb) * sg
    return sg, f, 1.0 - f, jnp.log(f)


G_SPEC = lambda sec: pl.BlockSpec((None, ROWS_ALL, HEAD_DIM), lambda h, sec=sec: (sec, 0, h))


GROUP = 2 * CHUNK


def _group_masks():
    rid, cid = np.arange(GROUP)[:, None], np.arange(GROUP)[None, :]
    same = (rid >= CHUNK) == (cid >= CHUNK)
    sums, back = [], []
    for rev in (False, True):
        causal = (cid >= rid) if rev else (cid <= rid)
        anti = (cid <= rid) if rev else (cid >= rid)
        sums.append(same & causal)
        back.append(np.concatenate([same & anti, same & ~anti], axis=1))
    sums, back = np.stack(sums).astype(np.float32), np.stack(back).astype(np.float32)
    return jnp.asarray(sums, BF16), jnp.asarray(back, BF16), jnp.asarray(sums, F32)


MASK_SPECS = [pl.BlockSpec((2, GROUP, GROUP), lambda h: (0, 0, 0)), pl.BlockSpec((2, GROUP, 2 * GROUP), lambda h: (0, 0, 0)),
              pl.BlockSpec((2, GROUP, GROUP), lambda h: (0, 0, 0))]


def _group_sum(m_bf, a):
    hi, lo = _split2(a)
    r = jnp.dot(m_bf, jnp.concatenate([hi, lo], axis=1), preferred_element_type=F32)
    return r[:, :HEAD_DIM] + r[:, HEAD_DIM:]


def _chunk_row(a, pos):
    return jnp.concatenate([jnp.broadcast_to(a[c * CHUNK + pos:c * CHUNK + pos + 1, :], (CHUNK, HEAD_DIM)) for c in range(2)], axis=0)


def _by_chunk(a, second):
    return jnp.concatenate([jnp.where(second, 0.0, a), jnp.where(second, a, 0.0)], axis=1)


def _own_block(r):
    return jnp.concatenate([r[0:CHUNK, 0:HEAD_DIM], r[CHUNK:GROUP, HEAD_DIM:2 * HEAD_DIM]], axis=0)


def _scan_step_of(row_chunk, rev, latent):
    if not rev:
        return row_chunk
    return (N_CHUNKS + N_CTX_CHUNKS - 1 - row_chunk) if latent else (N_CTX_CHUNKS - 1 - row_chunk)


def _group_rows(i, j, per_step, latent):
    base = CTX_LEN if latent else 0
    return pl.multiple_of(base + (i * per_step + j) * GROUP, GROUP)


GROUPS_PER_STEP = 16
GROUPS_PER_GRAD_STEP = 4


def _gla_forward(g5, lb_logits, late_shards):
    q_scale = HEAD_DIM ** -0.5
    per_lat, per_ctx = GROUPS_PER_STEP, min(GROUPS_PER_STEP, CTX_LEN // GROUP)
    n_late = len(GATHER_LATE)

    def body(ff_ref, fb_ref, v_ref, q_ref, lg_ref, sums_ref, _, keep_ref, *rest):
        shard_refs, o_ref, full_refs = rest[:n_late], rest[n_late], rest[n_late + 1:2 * n_late + 1]
        st_ref, decay_ref, qt_ref = rest[2 * n_late + 1:2 * n_late + 4]
        land_refs = rest[2 * n_late + 4:3 * n_late + 4]
        start_gather, forward_gather, finish_gather = _gather_two_level(GATHER_LATE, shard_refs, land_refs, full_refs,
                                                                        *rest[3 * n_late + 4:])

        @pl.when(pl.program_id(0) == 0)
        def _():
            start_gather()

        @pl.when(pl.program_id(0) == HEADS - 1)
        def _():
            forward_gather()

        second = lax.broadcasted_iota(jnp.int32, (GROUP, HEAD_DIM), 0) >= CHUNK
        for rev in (False, True):
            f_ref = fb_ref if rev else ff_ref
            lb = _sig(lg_ref[1:2, :] if rev else lg_ref[0:1, :])
            d = 1 if rev else 0
            last = 0 if rev else CHUNK - 1
            mid = CHUNK // 2 if rev else CHUNK // 2 - 1

            def local_step(i, carry, latent, per):
                r0s = [_group_rows(i, j, per, latent) for j in range(per)]
                rows = [pl.ds(r0, GROUP) for r0 in r0s]
                gates = [_gates(f_ref[r, :], lb) for r in rows]
                vs = [v_ref[r, :] for r in rows]
                bs = [_group_sum(sums_ref[d], g[3]) for g in gates]
                bls = [_chunk_row(b, last) for b in bs]
                ups = [_mm_tn(v, _by_chunk(g[2] * jnp.exp(bl - b), second)) for v, g, b, bl in zip(vs, gates, bs, bls)]
                if latent:
                    qs = [_silu(q_ref[r, :]) * q_scale for r in rows]
                    bms = [_chunk_row(b, mid) for b in bs]
                    a_s = [_mm_nt(q * jnp.exp(b - bm), g[2] * jnp.exp(bm - b)) for q, g, b, bm in zip(qs, gates, bs, bms)]
                    outs = [_mm(a * keep_ref[d], v) for a, v in zip(a_s, vs)]
                for j in range(per):
                    for c in range(2):
                        step = _scan_step_of(r0s[j] // CHUNK + c, rev, latent)
                        decay_ref[step] = jnp.exp(bls[j][c * CHUNK:c * CHUNK + 1, :])
                        st_ref[step] = ups[j][:, c * HEAD_DIM:(c + 1) * HEAD_DIM]
                    if latent:
                        orow = pl.ds(pl.multiple_of(r0s[j] - CTX_LEN, GROUP), GROUP)
                        qt_ref[orow, :] = (qs[j] * jnp.exp(bs[j])).astype(BF16)
                        if rev:
                            o_ref[orow, :] += outs[j]
                        else:
                            o_ref[orow, :] = outs[j]
                return carry

            lax.fori_loop(0, CTX_LEN // (per_ctx * GROUP), functools.partial(local_step, latent=False, per=per_ctx), 0)
            lax.fori_loop(0, SEQ // (per_lat * GROUP), functools.partial(local_step, latent=True, per=per_lat), 0)

            def scan_step(t, st):
                update = st_ref[t]
                st_ref[t] = st
                return st * decay_ref[t] + update

            lax.fori_loop(0, N_CHUNKS, scan_step, jnp.zeros((HEAD_DIM, HEAD_DIM), F32), unroll=2)

            def inter_step(i, carry):
                r0s = [_group_rows(i, j, per_lat, True) for j in range(per_lat)]
                orows = [pl.ds(pl.multiple_of(r0 - CTX_LEN, GROUP), GROUP) for r0 in r0s]
                states = [jnp.concatenate([st_ref[_scan_step_of(r0 // CHUNK + c, rev, True)] for c in range(2)], axis=0)
                          for r0 in r0s]
                prods = [lax.dot_general(qt_ref[orow, :], s.astype(BF16), (((1,), (1,)), ((), ())), preferred_element_type=F32)
                         for orow, s in zip(orows, states)]
                for orow, r in zip(orows, prods):
                    o_ref[orow, :] += _own_block(r)
                return carry

            lax.fori_loop(0, SEQ // (per_lat * GROUP), inter_step, 0)

        @pl.when(pl.program_id(0) == HEADS - 1)
        def _():
            finish_gather()

    outs = pl.pallas_call(
        body, name="gla_forward", grid=(HEADS,),
        out_shape=[pltpu.HBM((SEQ, D_MODEL), F32)] + [jax.ShapeDtypeStruct(W_SPECS[k][0], BF16) for k in GATHER_LATE],
        in_specs=[G_SPEC(0), G_SPEC(1), G_SPEC(2), G_SPEC(3), pl.BlockSpec((2, HEAD_DIM), lambda h: (0, h))] + MASK_SPECS
        + [ANY] * n_late,
        out_specs=[pl.BlockSpec((SEQ, HEAD_DIM), lambda h: (0, h))] + [ANY] * n_late,
        scratch_shapes=[pltpu.VMEM((N_CHUNKS, HEAD_DIM, HEAD_DIM), F32), pltpu.VMEM((N_CHUNKS, 1, HEAD_DIM), F32),
                        pltpu.VMEM((SEQ, HEAD_DIM), BF16)] + [pltpu.VMEM(W_SPECS[k][0], BF16) for k in GATHER_LATE]
        + [pltpu.SemaphoreType.DMA((2 * n_late,)), pltpu.SemaphoreType.DMA((2 * n_late,)), pltpu.SemaphoreType.DMA((n_late,))],
        compiler_params=_params(48),
    )(*_pin(g5, g5, g5, g5, lb_logits), *_group_masks(), *late_shards)
    return outs[0], dict(zip(GATHER_LATE, outs[1:]))


def _gla_backward(g5, lb_logits, d_o, d_z, late_parts):
    q_scale = HEAD_DIM ** -0.5
    per_lat, per_ctx = GROUPS_PER_STEP, min(GROUPS_PER_STEP, CTX_LEN // GROUP)
    n_late = len(GATHER_LATE)

    def body(ff_ref, fb_ref, v_ref, q_ref, lg_ref, do_ref, dz_ref, sums_ref, back_ref, keep_ref, *rest):
        part_refs, (dg_ref, dlg_ref), slot_refs = rest[:n_late], rest[n_late:n_late + 2], rest[n_late + 2:2 * n_late + 2]
        st_ref, dst_ref, decay_ref, both_ref = rest[2 * n_late + 2:2 * n_late + 6]
        start_scatter, wait_scatter = _scatter_direct(GATHER_LATE, part_refs, slot_refs, *rest[2 * n_late + 6:])

        @pl.when(pl.program_id(0) == 0)
        def _():
            start_scatter()

        dg_ref[3, 0:CTX_LEN, :] = jnp.zeros((CTX_LEN, HEAD_DIM), BF16)
        dg_ref[4, 0:CTX_LEN, :] = jnp.zeros((CTX_LEN, HEAD_DIM), BF16)
        dg_ref[4, CTX_LEN:ROWS_ALL, :] = dz_ref[...]
        second = lax.broadcasted_iota(jnp.int32, (GROUP, HEAD_DIM), 0) >= CHUNK
        for rev in (False, True):
            d = 1 if rev else 0
            f_ref = fb_ref if rev else ff_ref
            lb = _sig(lg_ref[d:d + 1, :])
            last = 0 if rev else CHUNK - 1
            mid = CHUNK // 2 if rev else CHUNK // 2 - 1

            def local_step(i, carry, latent, per):
                r0s = [_group_rows(i, j, per, latent) for j in range(per)]
                rows = [pl.ds(r0, GROUP) for r0 in r0s]
                gates = [_gates(f_ref[r, :], lb) for r in rows]
                bs = [_group_sum(sums_ref[d], g[3]) for g in gates]
                bls = [_chunk_row(b, last) for b in bs]
                ups = [_mm_tn(v_ref[r, :], _by_chunk(g[2] * jnp.exp(bl - b), second)) for r, g, b, bl in zip(rows, gates, bs, bls)]
                if latent:
                    orows = [pl.ds(pl.multiple_of(r0 - CTX_LEN, GROUP), GROUP) for r0 in r0s]
                    d_ups = [_mm_tn(do_ref[orow, :], _by_chunk(_silu(q_ref[r, :]) * q_scale * jnp.exp(b), second))
                             for orow, r, b in zip(orows, rows, bs)]
                for j in range(per):
                    for c in range(2):
                        step = _scan_step_of(r0s[j] // CHUNK + c, rev, latent)
                        decay_ref[step] = jnp.exp(bls[j][c * CHUNK:c * CHUNK + 1, :])
                        st_ref[step] = ups[j][:, c * HEAD_DIM:(c + 1) * HEAD_DIM]
                        if latent:
                            dst_ref[step] = d_ups[j][:, c * HEAD_DIM:(c + 1) * HEAD_DIM]
                        else:
                            dst_ref[step] = jnp.zeros((HEAD_DIM, HEAD_DIM), F32)
                return carry

            lax.fori_loop(0, CTX_LEN // (per_ctx * GROUP), functools.partial(local_step, latent=False, per=per_ctx), 0)
            lax.fori_loop(0, SEQ // (per_lat * GROUP), functools.partial(local_step, latent=True, per=per_lat), 0)

            def scan_step(i, carry):
                st, d_st = carry
                j = N_CHUNKS - 1 - i
                update, d_update = st_ref[i], dst_ref[j]
                st_ref[i] = st
                dst_ref[j] = d_st
                return st * decay_ref[i] + update, d_st * decay_ref[j] + d_update

            zero_state = jnp.zeros((HEAD_DIM, HEAD_DIM), F32)
            lax.fori_loop(0, N_CHUNKS, scan_step, (zero_state, zero_state))

            def grad_step(i, dlb, latent, per):
                r0s = [_group_rows(i, j, per, latent) for j in range(per)]
                rows = [pl.ds(r0, GROUP) for r0 in r0s]
                gates = [_gates(f_ref[r, :], lb) for r in rows]
                vs = [v_ref[r, :] for r in rows]
                bs = [_group_sum(sums_ref[d], g[3]) for g in gates]
                bls = [_chunk_row(b, last) for b in bs]
                e_ends = [jnp.exp(bl - b) for b, bl in zip(bs, bls)]
                k_ends = [g[2] * e for g, e in zip(gates, e_ends)]
                sts = [[st_ref[_scan_step_of(r0 // CHUNK + c, rev, latent)] for c in range(2)] for r0 in r0s]
                d_sts = [[dst_ref[_scan_step_of(r0 // CHUNK + c, rev, latent)] for c in range(2)] for r0 in r0s]
                d_kends = [_own_block(_mm(v, jnp.concatenate(ds, axis=1))) for v, ds in zip(vs, d_sts)]
                d_vs = [_own_block(_mm_nt(ke, jnp.concatenate(ds, axis=0))) for ke, ds in zip(k_ends, d_sts)]
                at_last = [jnp.concatenate([jnp.broadcast_to(jnp.sum(ds[c] * s[c], axis=0, keepdims=True), (CHUNK, HEAD_DIM))
                                            for c in range(2)], axis=0) * jnp.exp(bl) for ds, s, bl in zip(d_sts, sts, bls)]
                t_kends = [dk * ke for dk, ke in zip(d_kends, k_ends)]
                d_ks = [dk * e for dk, e in zip(d_kends, e_ends)]
                if latent:
                    orows = [pl.ds(pl.multiple_of(r0 - CTX_LEN, GROUP), GROUP) for r0 in r0s]
                    qpres = [q_ref[r, :] for r in rows]
                    q_sigs = [_sig(qp) for qp in qpres]
                    qs = [qp * sg * q_scale for qp, sg in zip(qpres, q_sigs)]
                    bms = [_chunk_row(b, mid) for b in bs]
                    e_bs = [jnp.exp(b) for b in bs]
                    e_qms = [jnp.exp(b - bm) for b, bm in zip(bs, bms)]
                    e_kms = [jnp.exp(bm - b) for b, bm in zip(bs, bms)]
                    q_ts = [q * e for q, e in zip(qs, e_bs)]
                    q_ms = [q * e for q, e in zip(qs, e_qms)]
                    k_ms = [g[2] * e for g, e in zip(gates, e_kms)]
                    d_outs = [do_ref[orow, :] for orow in orows]
                    a_s = [_mm_nt(qm, km) * keep_ref[d] for qm, km in zip(q_ms, k_ms)]
                    d_as = [_mm_nt(do, v) * keep_ref[d] for do, v in zip(d_outs, vs)]
                    d_qts = [_own_block(_mm(do, jnp.concatenate(s, axis=1))) for do, s in zip(d_outs, sts)]
                    d_qms = [_mm(da, km) for da, km in zip(d_as, k_ms)]
                    d_kms = [_mm_tn(da, qm) for da, qm in zip(d_as, q_ms)]
                    d_vs = [dv + _mm_tn(a, do) for dv, a, do in zip(d_vs, a_s, d_outs)]
                    d_ks = [dk + dkm * e for dk, dkm, e in zip(d_ks, d_kms, e_kms)]
                    d_bs = [jnp.concatenate([dqt * qt + dqm * qm - dkm * km, t], axis=0)
                            for dqt, qt, dqm, qm, dkm, km, t in zip(d_qts, q_ts, d_qms, q_ms, d_kms, k_ms, t_kends)]
                    d_qs = [dqt * eb + dqm * eq for dqt, eb, dqm, eq in zip(d_qts, e_bs, d_qms, e_qms)]
                    back = back_ref[d]
                else:
                    d_bs, back = t_kends, back_ref[d, :, GROUP:2 * GROUP]
                d_lfs = [_group_sum(back, db) + al for db, al in zip(d_bs, at_last)]
                for j in range(per):
                    sg, f = gates[j][0], gates[j][1]
                    d_f = d_lfs[j] / f - d_ks[j]
                    dg_ref[d, rows[j], :] = (d_f * (1.0 - lb) * sg * (1.0 - sg)).astype(BF16)
                    dlb = dlb + jnp.sum(d_f * (1.0 - sg), axis=0, keepdims=True)
                    if rev:
                        dg_ref[2, rows[j], :] = (both_ref[0, rows[j], :] + d_vs[j]).astype(BF16)
                    else:
                        both_ref[0, rows[j], :] = d_vs[j]
                    if latent:
                        d_qpre = d_qs[j] * q_scale * (q_sigs[j] * (1.0 + qpres[j] * (1.0 - q_sigs[j])))
                        if rev:
                            dg_ref[3, rows[j], :] = (both_ref[1, rows[j], :] + d_qpre).astype(BF16)
                        else:
                            both_ref[1, rows[j], :] = d_qpre
                return dlb

            dlb = lax.fori_loop(0, SEQ // (GROUPS_PER_GRAD_STEP * GROUP),
                                functools.partial(grad_step, latent=True, per=GROUPS_PER_GRAD_STEP), jnp.zeros((1, HEAD_DIM), F32))
            dlb = lax.fori_loop(0, CTX_LEN // (per_ctx * GROUP), functools.partial(grad_step, latent=False, per=per_ctx), dlb)
            dlg_ref[d:d + 1, :] = dlb * lb * (1.0 - lb)

        @pl.when(pl.program_id(0) == HEADS - 1)
        def _():
            wait_scatter()

    col = pl.BlockSpec((SEQ, HEAD_DIM), lambda h: (0, h))
    outs = pl.pallas_call(
        body, name="gla_backward", grid=(HEADS,),
        out_shape=[pltpu.HBM((HGRN_SECTIONS, ROWS_ALL, D_MODEL), BF16), jax.ShapeDtypeStruct((2, D_MODEL), F32)]
        + _slot_shapes(GATHER_LATE),
        in_specs=[G_SPEC(0), G_SPEC(1), G_SPEC(2), G_SPEC(3), pl.BlockSpec((2, HEAD_DIM), lambda h: (0, h)), col, col]
        + MASK_SPECS + [ANY] * n_late,
        out_specs=[pl.BlockSpec((HGRN_SECTIONS, ROWS_ALL, HEAD_DIM), lambda h: (0, 0, h)),
                   pl.BlockSpec((2, HEAD_DIM), lambda h: (0, h))] + [ANY] * n_late,
        scratch_shapes=[pltpu.VMEM((N_CHUNKS, HEAD_DIM, HEAD_DIM), F32), pltpu.VMEM((N_CHUNKS, HEAD_DIM, HEAD_DIM), F32),
                        pltpu.VMEM((N_CHUNKS, 1, HEAD_DIM), F32), pltpu.VMEM((2, ROWS_ALL, HEAD_DIM), F32)]
        + _comm_sems(n_late),
        compiler_params=_params(48),
    )(*_pin(g5, g5, g5, g5, lb_logits, d_o, d_z), *_group_masks(), *late_parts)
    return outs[0], outs[1], dict(zip(GATHER_LATE, outs[2:]))


def _head_norm(o, scr):
    rs = []
    for h in range(HEADS):
        cols = slice(h * HEAD_DIM, (h + 1) * HEAD_DIM)
        oh = o[:, cols]
        r = lax.rsqrt(jnp.mean(oh * oh, axis=-1, keepdims=True) + EPS)
        scr[:, cols] = oh * r
        rs.append(r)
    return rs


def _hgrn_out_forward(o_raw, g5, xin, gnorm_w, gate, w_out):
    tm = ROW_TILE

    def body(o_ref, z_ref, x_ref, gw_ref, gate_ref, w_ref, x1_ref, res_ref, scr):
        _head_norm(o_ref[...], scr)
        a = scr[...] * gw_ref[...] * _silu(z_ref[...])
        res = _mm(a, w_ref[...])
        res_ref[...] = res
        x1_ref[...] = x_ref[...] + gate_ref[...] * res

    tile = pl.BlockSpec((tm, D_MODEL), lambda i: (i, 0))
    vec = pl.BlockSpec((1, D_MODEL), lambda i: (0, 0))
    return pl.pallas_call(
        body, name="hgrn_out_forward", grid=(SEQ // tm,),
        out_shape=[pltpu.HBM((SEQ, D_MODEL), F32)] * 2,
        in_specs=[tile, pl.BlockSpec((None, tm, D_MODEL), lambda i: (4, i + CTX_LEN // tm, 0)), tile, vec, vec,
                  pl.BlockSpec((D_MODEL, D_MODEL), lambda i: (0, 0))],
        out_specs=[tile, tile],
        scratch_shapes=[pltpu.VMEM((tm, D_MODEL), F32)],
        compiler_params=_params(32),
    )(*_pin(o_raw, g5, xin, gnorm_w, gate, w_out))


def _hgrn_out_backward(d_x1, o_raw, g5, res, gnorm_w, gate, w_out):
    tm = ROW_TILE

    def body(dx_ref, o_ref, z_ref, res_ref, gw_ref, gate_ref, w_ref, do_ref, dz_ref, dw_out, dgate_ref, dgw_ref, scr, scr2,
             dw_ref):
        @pl.when(pl.program_id(0) == 0)
        def _():
            dw_ref[...] = jnp.zeros_like(dw_ref)
            dgate_ref[...] = jnp.zeros_like(dgate_ref)
            dgw_ref[...] = jnp.zeros_like(dgw_ref)

        dx = dx_ref[...]
        dgate_ref[...] += jnp.sum(dx * res_ref[...], axis=0, keepdims=True)
        d_res = (dx * gate_ref[...]).astype(BF16)
        d_a = _mm_nt(d_res, w_ref[...])
        rs = _head_norm(o_ref[...], scr)
        z = z_ref[...]
        sz = _silu(z)
        o_hat = scr[...]
        o_n = o_hat * gw_ref[...]
        dw_ref[...] += _mm_tn(o_n * sz, d_res)
        d_on = d_a * sz
        dz_ref[...] = (d_a * o_n * _dsilu(z)).astype(BF16)
        dgw_ref[...] += jnp.sum(d_on * o_hat, axis=0, keepdims=True)
        scr2[...] = d_on * gw_ref[...]
        for h in range(HEADS):
            cols = slice(h * HEAD_DIM, (h + 1) * HEAD_DIM)
            dh, oh = scr2[:, cols], scr[:, cols]
            do_ref[:, cols] = (rs[h] * (dh - oh * jnp.mean(dh * oh, axis=-1, keepdims=True))).astype(BF16)

        @pl.when(pl.program_id(0) == SEQ // tm - 1)
        def _():
            dw_out[...] = dw_ref[...].astype(BF16)

    tile = pl.BlockSpec((tm, D_MODEL), lambda i: (i, 0))
    vec = pl.BlockSpec((1, D_MODEL), lambda i: (0, 0))
    mat = pl.BlockSpec((D_MODEL, D_MODEL), lambda i: (0, 0))
    return pl.pallas_call(
        body, name="hgrn_out_backward", grid=(SEQ // tm,),
        out_shape=[pltpu.HBM((SEQ, D_MODEL), BF16)] * 2 + [pltpu.HBM((D_MODEL, D_MODEL), BF16)]
        + [jax.ShapeDtypeStruct((1, D_MODEL), F32)] * 2,
        in_specs=[tile, tile, pl.BlockSpec((None, tm, D_MODEL), lambda i: (4, i + CTX_LEN // tm, 0)), tile, vec, vec, mat],
        out_specs=[tile, tile, mat, vec, vec],
        scratch_shapes=[pltpu.VMEM((tm, D_MODEL), F32)] * 2 + [pltpu.VMEM((D_MODEL, D_MODEL), F32)],
        compiler_params=_params(40),
    )(*_pin(d_x1, o_raw, g5, res, gnorm_w, gate, w_out))


def _pool_constants():
    win = np.zeros((POOL_GROUPS, ROW_TILE, ROW_TILE), np.float32)
    inv = np.zeros((POOL_GROUPS, ROW_TILE, 1), np.float32)
    for g, w in enumerate(POOL_WINDOWS):
        for t in range(ROW_TILE):
            base, p = (t // GRID_W) * GRID_W, t % GRID_W
            lo = min(max(p - w // 2, 0), GRID_W)
            hi = min(max(p - w // 2 + w, 0), GRID_W)
            win[g, t, base + lo:base + hi] = 1.0
            inv[g, t, 0] = 1.0 / np.float32(hi - lo)
    return jnp.asarray(win, BF16), jnp.asarray(win.transpose(0, 2, 1), BF16), jnp.asarray(inv, F32)


def _pool_mix(u_ref, wg_ref, win_ref, inv_ref, pooled_scr, yg_scr):
    for g in range(POOL_GROUPS):
        cols = slice(g * POOL_GROUP_DIM, (g + 1) * POOL_GROUP_DIM)
        for r in range(u_ref.shape[0] // ROW_TILE):
            rows = slice(r * ROW_TILE, (r + 1) * ROW_TILE)
            ug = u_ref[rows, cols]
            pooled_scr[rows, cols] = _mm_exact_lhs(win_ref[g], ug) * inv_ref[g] - ug
        yg_scr[:, cols] = _mm(pooled_scr[:, cols], wg_ref[g])


def _pool_forward_loss(uz, x1, target, gate, w_grp, pool_scale, w_out, final_w):
    tm = POOL_TILE
    win, _, inv = _pool_constants()

    def body(u_ref, z_ref, x_ref, t_ref, gate_ref, wg_ref, ps_ref, w_ref, fw_ref, win_ref, inv_ref,
             dx_ref, loss_ref, dfw_ref, dgate_ref, yg_scr, pooled_scr):
        @pl.when(pl.program_id(0) == 0)
        def _():
            loss_ref[...] = jnp.zeros_like(loss_ref)
            dfw_ref[...] = jnp.zeros_like(dfw_ref)
            dgate_ref[...] = jnp.zeros_like(dgate_ref)

        _pool_mix(u_ref, wg_ref, win_ref, inv_ref, pooled_scr, yg_scr)
        a = yg_scr[...] * ps_ref[...] * _silu(z_ref[...])
        res = _mm(a, w_ref[...])
        x2 = x_ref[...] + gate_ref[...] * res
        r = lax.rsqrt(jnp.mean(x2 * x2, axis=-1, keepdims=True) + EPS)
        xh = x2 * r
        fw = fw_ref[...]
        err = xh * fw - t_ref[...]
        loss_ref[...] += 0.5 * jnp.sum(jnp.mean(err * err, axis=-1, keepdims=True))
        d_y = err * (1.0 / D_MODEL)
        dfw_ref[...] += jnp.sum(d_y * xh, axis=0, keepdims=True)
        d_xh = d_y * fw
        d_x2 = r * (d_xh - xh * jnp.mean(d_xh * xh, axis=-1, keepdims=True))
        dx_ref[...] = d_x2
        dgate_ref[...] += jnp.sum(d_x2 * res, axis=0, keepdims=True)

    tile = pl.BlockSpec((tm, D_MODEL), lambda i: (i, 0))
    vec = pl.BlockSpec((1, D_MODEL), lambda i: (0, 0))
    grp = pl.BlockSpec((POOL_GROUPS, POOL_GROUP_DIM, POOL_GROUP_DIM), lambda i: (0, 0, 0))
    return pl.pallas_call(
        body, name="pool_forward_loss", grid=(SEQ // tm,),
        out_shape=[pltpu.HBM((SEQ, D_MODEL), F32), jax.ShapeDtypeStruct((8, 128), F32),
                   jax.ShapeDtypeStruct((1, D_MODEL), F32), jax.ShapeDtypeStruct((1, D_MODEL), F32)],
        in_specs=[pl.BlockSpec((None, tm, D_MODEL), lambda i: (0, i, 0)), pl.BlockSpec((None, tm, D_MODEL), lambda i: (1, i, 0)),
                  tile, tile, vec, grp, vec, pl.BlockSpec((D_MODEL, D_MODEL), lambda i: (0, 0)), vec, grp,
                  pl.BlockSpec((POOL_GROUPS, ROW_TILE, 1), lambda i: (0, 0, 0))],
        out_specs=[tile, pl.BlockSpec((8, 128), lambda i: (0, 0)), vec, vec],
        scratch_shapes=[pltpu.VMEM((tm, D_MODEL), F32)] * 2,
        compiler_params=_params(40),
    )(*_pin(uz, uz, x1, target, gate, w_grp, pool_scale, w_out, final_w, win, inv))


def _pool_backward(d_x2, uz, gate, w_grp, pool_scale, w_out):
    tm = POOL_TILE
    win, win_t, inv = _pool_constants()

    def body(dx_ref, u_ref, z_ref, gate_ref, wg_ref, ps_ref, w_ref, win_ref, wint_ref, inv_ref,
             duz_ref, dw_out, dwg_out, dps_ref, pooled_scr, yg_scr, dyg_scr, dw_ref, dwg_ref):
        @pl.when(pl.program_id(0) == 0)
        def _():
            dw_ref[...] = jnp.zeros_like(dw_ref)
            dwg_ref[...] = jnp.zeros_like(dwg_ref)
            dps_ref[...] = jnp.zeros_like(dps_ref)

        _pool_mix(u_ref, wg_ref, win_ref, inv_ref, pooled_scr, yg_scr)
        z = z_ref[...]
        sz = _silu(z)
        yg = yg_scr[...]
        y = yg * ps_ref[...]
        d_res = (dx_ref[...] * gate_ref[...]).astype(BF16)
        d_a = _mm_nt(d_res, w_ref[...])
        dw_ref[...] += _mm_tn(y * sz, d_res)
        d_y = d_a * sz
        duz_ref[1] = (d_a * y * _dsilu(z)).astype(BF16)
        dps_ref[...] += jnp.sum(d_y * yg, axis=0, keepdims=True)
        dyg_scr[...] = d_y * ps_ref[...]
        for g in range(POOL_GROUPS):
            cols = slice(g * POOL_GROUP_DIM, (g + 1) * POOL_GROUP_DIM)
            d_yg = dyg_scr[:, cols].astype(BF16)
            d_pool = _mm_nt(d_yg, wg_ref[g])
            dwg_ref[g] += _mm_tn(pooled_scr[:, cols], d_yg)
            for r in range(tm // ROW_TILE):
                rows = slice(r * ROW_TILE, (r + 1) * ROW_TILE)
                dp = d_pool[rows, :]
                duz_ref[0, rows, cols] = (_mm_exact_lhs(wint_ref[g], dp * inv_ref[g]) - dp).astype(BF16)

        @pl.when(pl.program_id(0) == SEQ // tm - 1)
        def _():
            dw_out[...] = dw_ref[...].astype(BF16)
            dwg_out[...] = dwg_ref[...].astype(BF16)

    tile = pl.BlockSpec((tm, D_MODEL), lambda i: (i, 0))
    vec = pl.BlockSpec((1, D_MODEL), lambda i: (0, 0))
    mat = pl.BlockSpec((D_MODEL, D_MODEL), lambda i: (0, 0))
    grp = pl.BlockSpec((POOL_GROUPS, POOL_GROUP_DIM, POOL_GROUP_DIM), lambda i: (0, 0, 0))
    return pl.pallas_call(
        body, name="pool_backward", grid=(SEQ // tm,),
        out_shape=[pltpu.HBM((POOL_SECTIONS, SEQ, D_MODEL), BF16), pltpu.HBM((D_MODEL, D_MODEL), BF16),
                   pltpu.HBM((POOL_GROUPS, POOL_GROUP_DIM, POOL_GROUP_DIM), BF16), jax.ShapeDtypeStruct((1, D_MODEL), F32)],
        in_specs=[tile, pl.BlockSpec((None, tm, D_MODEL), lambda i: (0, i, 0)), pl.BlockSpec((None, tm, D_MODEL), lambda i: (1, i, 0)),
                  vec, grp, vec, mat, grp, grp, pl.BlockSpec((POOL_GROUPS, ROW_TILE, 1), lambda i: (0, 0, 0))],
        out_specs=[pl.BlockSpec((POOL_SECTIONS, tm, D_MODEL), lambda i: (0, i, 0)), mat, grp, vec],
        scratch_shapes=[pltpu.VMEM((tm, D_MODEL), F32)] * 3 + [pltpu.VMEM((D_MODEL, D_MODEL), F32),
                                                               pltpu.VMEM((POOL_GROUPS, POOL_GROUP_DIM, POOL_GROUP_DIM), F32)],
        compiler_params=_params(48),
    )(*_pin(d_x2, uz, uz, gate, w_grp, pool_scale, w_out, win, win_t, inv))


def _ln_mod_backward(d_g, w, ctx_tile, xin, nw, scale, d_up, name):
    n_sec, rows, _ = d_g.shape
    n_mod = scale.shape[0]
    skip = n_mod - 1
    tm = ROW_TILE if skip else 2 * ROW_TILE
    n_tiles = rows // tm

    def body(dg_ref, w_ref, *refs):
        c_ref = refs[0] if skip else None
        x_ref, nw_ref, sc_ref, up_ref, dx_ref, dnw_ref, dmod_ref = refs[skip:]
        i = pl.program_id(0)

        @pl.when(i == 0)
        def _():
            dnw_ref[...] = jnp.zeros_like(dnw_ref)

        @pl.when((i == 0) | (i == skip))
        def _():
            dmod_ref[...] = jnp.zeros_like(dmod_ref)

        d_h = _mm_nt(dg_ref[0], w_ref[:, 0:D_MODEL])
        for k in range(1, n_sec):
            d_h = d_h + _mm_nt(dg_ref[k], w_ref[:, k * D_MODEL:(k + 1) * D_MODEL])
        xv = jnp.where(i == 0, c_ref[...], x_ref[...]) if skip else x_ref[...]
        r = lax.rsqrt(jnp.mean(xv * xv, axis=-1, keepdims=True) + EPS)
        xh = xv * r
        nw_row = nw_ref[...]
        dmod_ref[0:1, :] += jnp.sum(d_h, axis=0, keepdims=True)
        dmod_ref[1:2, :] += jnp.sum(d_h * (xh * nw_row), axis=0, keepdims=True)
        d_xn = d_h * (1.0 + sc_ref[...])
        dnw_ref[...] += jnp.sum(d_xn * xh, axis=0, keepdims=True)
        d_xh = d_xn * nw_row

        @pl.when(i >= skip)
        def _():
            dx_ref[...] = up_ref[...] + r * (d_xh - xh * jnp.mean(d_xh * xh, axis=-1, keepdims=True))

    lat = lambda i: (jnp.maximum(i - skip, 0), 0)
    mod_idx = lambda i: (jnp.minimum(i, n_mod - 1), 0, 0)
    return pl.pallas_call(
        body, name=name, grid=(n_tiles,),
        out_shape=[pltpu.HBM((rows - skip * tm, D_MODEL), F32), jax.ShapeDtypeStruct((1, D_MODEL), F32),
                   jax.ShapeDtypeStruct((n_mod, 8, D_MODEL), F32)],
        in_specs=[pl.BlockSpec((n_sec, tm, D_MODEL), lambda i: (0, i, 0)),
                  pl.BlockSpec((D_MODEL, n_sec * D_MODEL), lambda i: (0, 0))]
        + [pl.BlockSpec((tm, D_MODEL), lambda i: (0, 0))] * skip
        + [pl.BlockSpec((tm, D_MODEL), lat),
           pl.BlockSpec((1, D_MODEL), lambda i: (0, 0)),
           pl.BlockSpec((None, 1, D_MODEL), mod_idx),
           pl.BlockSpec((tm, D_MODEL), lat)],
        out_specs=[pl.BlockSpec((tm, D_MODEL), lat), pl.BlockSpec((1, D_MODEL), lambda i: (0, 0)),
                   pl.BlockSpec((None, 8, D_MODEL), mod_idx)],
        compiler_params=_params(48),
    )(*_pin(d_g, w, *([ctx_tile] * skip), xin, nw, scale.reshape(n_mod, 1, D_MODEL), d_up))


def _weight_grad(h, d_g, name):
    n_sec, rows, _ = d_g.shape
    tm = 768 if rows % 768 == 0 else 512
    n_tiles = rows // tm

    def body(h_ref, dg_ref, dw_ref, acc):
        i = pl.program_id(1)
        prod = _mm_tn(h_ref[...], dg_ref[...])

        @pl.when(i == 0)
        def _():
            acc[...] = prod

        @pl.when((i > 0) & (i < n_tiles - 1))
        def _():
            acc[...] += prod

        @pl.when(i == n_tiles - 1)
        def _():
            dw_ref[...] = (acc[...] + prod).astype(BF16)

    return pl.pallas_call(
        body, name=name, grid=(n_sec, n_tiles),
        out_shape=pltpu.HBM((D_MODEL, n_sec * D_MODEL), BF16),
        in_specs=[pl.BlockSpec((tm, D_MODEL), lambda j, i: (i, 0)), pl.BlockSpec((None, tm, D_MODEL), lambda j, i: (j, i, 0))],
        out_specs=pl.BlockSpec((D_MODEL, D_MODEL), lambda j, i: (0, j)),
        scratch_shapes=[pltpu.VMEM((D_MODEL, D_MODEL), F32)],
        compiler_params=_params(32),
    )(*_pin(h, d_g))


def _weight_grad_paired(h, d_g, name):
    n_sec, rows, _ = d_g.shape
    tm = 768
    n_tiles = rows // tm
    half = D_MODEL // 2
    chip_cols = n_sec * D_MODEL // N_CHIPS

    def body(h_ref, dg_ref, q_ref, slots_ref, acc, keep, send, land, send_sems, recv_sems, own_sem):
        j, i = pl.program_id(0), pl.program_id(1)
        x, y, c = _my_place()
        prod = _mm_tn(h_ref[...], dg_ref[...])

        def to_sibling(k):
            return pltpu.make_async_remote_copy(src_ref=send.at[k], dst_ref=land.at[k], send_sem=send_sems.at[k],
                                                recv_sem=recv_sems.at[k], device_id=(x, y, 1 - c), device_id_type=MESH)

        @pl.when(i == 0)
        def _():
            acc[...] = prod

        @pl.when((i > 0) & (i < n_tiles - 1))
        def _():
            acc[...] += prod

        @pl.when(i == n_tiles - 1)
        def _():
            acc[...] += prod
            keep[j] = acc[pl.ds(_al(c * half, half), half), :]
            send[j] = acc[pl.ds(_al((1 - c) * half, half), half), :].astype(BF16)
            to_sibling(j).start()

        @pl.when((j == n_sec - 1) & (i == n_tiles - 1))
        def _():
            for k in range(n_sec):
                to_sibling(k).wait()
                q_ref[:, k * D_MODEL:(k + 1) * D_MODEL] = (keep[k] + land[k].astype(F32)).astype(BF16)
            chip = 2 * x + y
            own = pltpu.make_async_copy(q_ref.at[:, pl.ds(_al(chip * chip_cols, chip_cols), chip_cols)], slots_ref.at[chip], own_sem)
            own.start()
            own.wait()

    return pl.pallas_call(
        body, name=name, grid=(n_sec, n_tiles),
        out_shape=[jax.ShapeDtypeStruct((half, n_sec * D_MODEL), BF16),
                   jax.ShapeDtypeStruct((N_CHIPS, half, chip_cols), BF16)],
        in_specs=[pl.BlockSpec((tm, D_MODEL), lambda j, i: (i, 0)), pl.BlockSpec((None, tm, D_MODEL), lambda j, i: (j, i, 0))],
        out_specs=[pl.BlockSpec((half, n_sec * D_MODEL), lambda j, i: (0, 0)), ANY],
        scratch_shapes=[pltpu.VMEM((D_MODEL, D_MODEL), F32), pltpu.VMEM((n_sec, half, D_MODEL), F32),
                        pltpu.VMEM((n_sec, half, D_MODEL), BF16), pltpu.VMEM((n_sec, half, D_MODEL), BF16),
                        pltpu.SemaphoreType.DMA((n_sec,)), pltpu.SemaphoreType.DMA((n_sec,)), pltpu.SemaphoreType.DMA],
        compiler_params=_params(56),
    )(*_pin(h, d_g))


def _adamw_math(w, g, m, v):
    m = ADAM_B1 * m + (1.0 - ADAM_B1) * g
    v = ADAM_B2 * v + (1.0 - ADAM_B2) * (g * g)
    m_hat = m / (1.0 - ADAM_B1 ** ADAM_STEP)
    v_hat = v / (1.0 - ADAM_B2 ** ADAM_STEP)
    return -ADAM_LR * (m_hat / (jnp.sqrt(v_hat) + ADAM_EPS) + ADAM_WD * w), m, v


def _adamw(w, g, m, v, name):
    rows, cols = w.shape
    tr = rows if rows <= 128 else 128

    def body(w_ref, g_ref, m_ref, v_ref, d_ref, mo_ref, vo_ref):
        d_ref[...], mo_ref[...], vo_ref[...] = _adamw_math(w_ref[...], g_ref[...], m_ref[...], v_ref[...])

    tile = pl.BlockSpec((tr, cols), lambda i: (i, 0))
    return pl.pallas_call(
        body, name=name, grid=(rows // tr,),
        out_shape=[pltpu.HBM((rows, cols), F32)] * 3,
        in_specs=[tile] * 4, out_specs=[tile] * 3,
    )(*_pin(w, g, m, v))


def _adamw_small(quads, name, vmem_mib=None):
    n = len(quads)

    def body(*refs):
        ins, outs = refs[:4 * n], refs[4 * n:]
        for a in range(n):
            w_ref, g_ref, m_ref, v_ref = ins[4 * a:4 * a + 4]
            d_ref, mo_ref, vo_ref = outs[3 * a:3 * a + 3]
            d_ref[...], mo_ref[...], vo_ref[...] = _adamw_math(w_ref[...], g_ref[...], m_ref[...], v_ref[...])

    outs = pl.pallas_call(
        body, name=name, out_shape=[jax.ShapeDtypeStruct(q[0].shape, F32) for q in quads for _ in range(3)],
        in_specs=[VMEM] * (4 * n), out_specs=[VMEM] * (3 * n), compiler_params=_params(vmem_mib),
    )(*[a for q in quads for a in q])
    return [tuple(outs[3 * a:3 * a + 3]) for a in range(n)]


def _small_sums_and_cond_ctx(small, ada_w, c_ctx, m, v):
    n_cols = ada_w.shape[2]

    def body(small_ref, w_ref, c_ref, m_ref, v_ref, all_ref, sums_ref, g_ref, d_ref, mo_ref, vo_ref, part_ref, parts_ref,
             s1, r1, l1, s2, r2, l2):
        start_slabs, finish_slabs = _small_gather(small_ref, all_ref, s1, r1, l1)
        start_parts, finish_parts = _small_gather(part_ref, parts_ref, s2, r2, l2)
        start_slabs()
        finish_slabs()
        acc = all_ref[0:SMALL_ROWS, :]
        for dev in range(1, N_DEV):
            acc = acc + all_ref[dev * SMALL_ROWS:(dev + 1) * SMALL_ROWS, :]
        sums_ref[...] = acc
        x, y, _ = _my_place()
        d_modc = jnp.concatenate([acc[6:7, :], acc[7:8, :], acc[8:9, :]], axis=1)
        mine = jnp.zeros((1, n_cols), F32)
        for s in range(N_CHIPS):
            mine = mine + jnp.where(2 * x + y == s, d_modc[:, s * n_cols:(s + 1) * n_cols], 0.0)
        ah, al = _split2(jnp.broadcast_to(mine, (8, n_cols)))
        wh, wl = _split2(w_ref[0])
        nt = lambda a, b: lax.dot_general(a, b, (((1,), (1,)), ((), ())), preferred_element_type=F32)
        part_ref[...] = nt(ah, wh) + nt(al, wh) + nt(ah, wl)
        start_parts()
        finish_parts()
        total = parts_ref[0:1, :]
        for s in range(1, N_CHIPS):
            total = total + parts_ref[16 * s:16 * s + 1, :]
        w = c_ref[...]
        g = total * _dsilu(w)
        g_ref[...] = g
        d_ref[...], mo_ref[...], vo_ref[...] = _adamw_math(w, g, m_ref[...], v_ref[...])

    seven = [pltpu.SemaphoreType.DMA((7,)), pltpu.SemaphoreType.DMA((7,)), pltpu.SemaphoreType.DMA]
    vec = jax.ShapeDtypeStruct((1, D_MODEL), F32)
    return pl.pallas_call(
        body, name="small_sums_and_cond_ctx",
        out_shape=[jax.ShapeDtypeStruct((N_DEV * SMALL_ROWS, D_MODEL), F32), jax.ShapeDtypeStruct((SMALL_ROWS, D_MODEL), F32),
                   vec, vec, vec, vec],
        in_specs=[VMEM] * 5, out_specs=[VMEM] * 6,
        scratch_shapes=[pltpu.VMEM((8, D_MODEL), F32), pltpu.VMEM((N_DEV * 8, D_MODEL), F32)] + seven + seven,
    )(small, ada_w, c_ctx, m, v)


def _ada_update(cond_t, d_mod, ada_w, m, v):
    n_layers, _, n_cols = ada_w.shape
    tr = ROW_TILE

    def body(c_ref, dm_ref, w_ref, m_ref, v_ref, g_ref, d_ref, mo_ref, vo_ref):
        g = _mm_f32(_silu(c_ref[...]), dm_ref[...])
        g_ref[...] = g
        d_ref[...], mo_ref[...], vo_ref[...] = _adamw_math(w_ref[...], g, m_ref[...], v_ref[...])

    tile = pl.BlockSpec((None, tr, n_cols), lambda l, i: (l, i, 0))
    return pl.pallas_call(
        body, name="ada_update", grid=(n_layers, D_MODEL // tr),
        out_shape=[pltpu.HBM(ada_w.shape, F32)] * 4,
        in_specs=[pl.BlockSpec((tr, 16), lambda l, i: (i, 0)), pl.BlockSpec((None, 16, n_cols), lambda l, i: (l, 0, 0)),
                  tile, tile, tile],
        out_specs=[tile] * 4,
    )(*_pin(cond_t, d_mod, ada_w, m, v))


def _local_step(x2, ctx2, target, mod_mine, mod_ctx, lb_logits, scale_full, w_in_full, late_shards, norm_w, gnorm, final_w):
    row = lambda a: a.reshape(1, -1)
    shift0, scale0, gate0 = (row(a) for a in jnp.split(mod_mine[0], 3))
    shift1, scale1, gate1 = (row(a) for a in jnp.split(mod_mine[1], 3))
    shift_c, scale_c, _ = (row(a) for a in jnp.split(mod_ctx, 3))
    nw0, nw1 = norm_w[0:1], norm_w[1:2]
    scales0 = jnp.concatenate([scale_c, scale0])

    g5, h0 = _ln_mod_matmul(ctx2, x2, nw0, jnp.concatenate([shift_c, shift0]), scales0, w_in_full, "hgrn_in_forward")
    o_raw, full = _gla_forward(g5, lb_logits, late_shards)
    x1, res0 = _hgrn_out_forward(o_raw, g5, x2, gnorm, gate0, full["hgrn_w_out"])
    uz, h1 = _ln_mod_matmul(None, x1, nw1, shift1, scale1, full["pool_w_in"], "pool_in_forward")
    d_x2, loss_part, d_final, d_gate1 = _pool_forward_loss(uz, x1, target, gate1, full["pool_w_grp"], scale_full,
                                                           full["pool_w_out"], final_w)

    d_uz, dw_pool_out, dw_pool_grp, d_pscale = _pool_backward(d_x2, uz, gate1, full["pool_w_grp"], scale_full, full["pool_w_out"])
    d_x1, d_nw1, d_mod1 = _ln_mod_backward(d_uz, full["pool_w_in"], None, x1, nw1, scale1, d_x2, "pool_in_backward")
    dw_pool_in = _weight_grad(h1, d_uz, "pool_in_weight_grad")
    d_o, d_z, dw_hgrn_out, d_gate0, d_gnorm = _hgrn_out_backward(d_x1, o_raw, g5, res0, gnorm, gate0, full["hgrn_w_out"])
    late_grads = {"hgrn_w_out": dw_hgrn_out, "pool_w_in": dw_pool_in, "pool_w_grp": dw_pool_grp, "pool_w_out": dw_pool_out}
    d_g5, d_lb, late_slots = _gla_backward(g5, lb_logits, d_o, d_z, [late_grads[k].astype(BF16) for k in GATHER_LATE])
    dw_hgrn_in, slots0 = _weight_grad_paired(h0, d_g5, "hgrn_in_weight_grad")
    key = GATHER_EARLY[0]
    send_sem, recv_sem, part_thru, slots_thru, token = _scatter_start(dw_hgrn_in, slots0, d_lb, key)
    d_x, d_nw0, d_mod0 = _ln_mod_backward(d_g5, w_in_full, ctx2, x2, nw0 + token[0:1, 0:1], scales0, d_x1, "hgrn_in_backward")
    slots = dict(late_slots)
    pending = (send_sem, recv_sem, part_thru, slots_thru)

    zero = jnp.zeros((1, D_MODEL), F32)
    small = jnp.concatenate([d_mod0[1, 0:2], d_gate0, d_mod1[0, 0:2], d_gate1, d_mod0[0, 0:2], zero, d_nw0, d_nw1, d_gnorm,
                             d_final, d_pscale, d_lb, jnp.broadcast_to(loss_part[0:1, 0:1], (1, D_MODEL)),
                             jnp.zeros((SMALL_ROWS - 17, D_MODEL), F32)], axis=0)
    return {"d_x": d_x, "slots": slots, "pending": pending, "small": small}


def kernel(x, c, ctx, c_ctx, ada_w, ada_b, norm_w, hgrn_w_in, hgrn_lb_logits, hgrn_gnorm_w, hgrn_w_out, pool_w_in, pool_w_grp, pool_scale, pool_w_out, final_norm_w, loss_target, m_c_ctx, m_ada_w, m_ada_b, m_norm_w, m_hgrn_w_in, m_hgrn_lb_logits, m_hgrn_gnorm_w, m_hgrn_w_out, m_pool_w_in, m_pool_w_grp, m_pool_scale, m_pool_w_out, m_final_norm_w, v_c_ctx, v_ada_w, v_ada_b, v_norm_w, v_hgrn_w_in, v_hgrn_lb_logits, v_hgrn_gnorm_w, v_hgrn_w_out, v_pool_w_in, v_pool_w_grp, v_pool_scale, v_pool_w_out, v_final_norm_w):
    xi, yi, ci = _my_place()
    chip = 2 * xi + yi
    dev = 4 * xi + 2 * yi + ci
    ada_cols = ada_w.shape[2]
    lb_cols = hgrn_lb_logits.shape[2]
    ps_cols = pool_scale.shape[1]
    row = lambda a: a.reshape(1, -1)

    def chip_cols(a, n):
        return lax.dynamic_slice_in_dim(a, chip * n, n, axis=a.ndim - 1)

    def from_chips(g, rows_per_dev, take):
        return jnp.concatenate([g[2 * s * rows_per_dev:2 * s * rows_per_dev + take] for s in range(N_CHIPS)], axis=1)

    first = jnp.concatenate([jnp.broadcast_to(c, (8, D_MODEL)), jnp.pad(hgrn_lb_logits[0], ((0, 6), (0, 0))),
                             jnp.pad(pool_scale, ((0, 7), (0, 0)))], axis=1)
    shards = {"hgrn_w_in": hgrn_w_in[0], "hgrn_w_out": hgrn_w_out[0], "pool_w_in": pool_w_in[0],
              "pool_w_grp": pool_w_grp[0], "pool_w_out": pool_w_out[0]}
    first_all, parts_all, w_in_full = _prologue(first, row(c_ctx), ada_w, chip_cols(ada_b, ada_cols),
                                                shards[GATHER_EARLY[0]].astype(BF16))
    cond_all = first_all[::8, :D_MODEL]
    lb_logits = from_chips(first_all[:, D_MODEL:D_MODEL + lb_cols], 8, 2)
    scale_full = from_chips(first_all[:, D_MODEL + lb_cols:], 8, 1)
    cond_rows = jnp.concatenate([cond_all, row(c_ctx), jnp.zeros((7, D_MODEL), F32)], axis=0)
    mod_all = from_chips(parts_all, 32, 32).reshape(2, 16, 3 * D_MODEL)
    mod_mine = lax.dynamic_index_in_dim(mod_all, dev, axis=1, keepdims=False)

    loc = _local_step(x[0], ctx[0], loss_target[0], mod_mine, mod_all[0, 8], lb_logits, scale_full, w_in_full,
                      [shards[k].astype(BF16) for k in GATHER_LATE], norm_w, hgrn_gnorm_w, row(final_norm_w))

    small_all, sums, g_c, d_c, m_c, v_c = _small_sums_and_cond_ctx(loc["small"], ada_w, row(c_ctx), row(m_c_ctx), row(v_c_ctx))
    loss = sums[16, 0]

    def reduce_scattered(slots, names, name):
        return dict(zip(names, _sum_and_exchange([slots[k] for k in names], names, name)))

    big_grads = reduce_scattered(loc["slots"], GATHER_LATE, "exchange_halves_late")

    out = {}

    def update(name, w, g, m, v):
        shape = w.shape
        w2, g2, m2, v2 = (a.reshape(-1, shape[-1]) for a in (w, g, m, v))
        d, mn, vn = _adamw(w2, g2, m2, v2, "adamw_" + name)
        out[name] = tuple(a.reshape(shape) for a in (g2, d, mn, vn))

    moments = {"hgrn_w_in": (m_hgrn_w_in, v_hgrn_w_in), "hgrn_w_out": (m_hgrn_w_out, v_hgrn_w_out),
               "pool_w_in": (m_pool_w_in, v_pool_w_in), "pool_w_grp": (m_pool_w_grp, v_pool_w_grp),
               "pool_w_out": (m_pool_w_out, v_pool_w_out)}
    weights = {"hgrn_w_in": hgrn_w_in, "hgrn_w_out": hgrn_w_out, "pool_w_in": pool_w_in, "pool_w_grp": pool_w_grp,
               "pool_w_out": pool_w_out}
    late = [tuple(a.reshape(-1, weights[k].shape[-1]) for a in (weights[k], big_grads[k], *moments[k])) for k in GATHER_LATE]
    for k, quad, (d, mn, vn) in zip(GATHER_LATE, late, _adamw_small(late, "adamw_late", 48)):
        out[k] = tuple(a.reshape(weights[k].shape) for a in (quad[1], d, mn, vn))

    g_ada_b = jnp.stack([(sums[0:3] + sums[6:9]).reshape(-1), sums[3:6].reshape(-1)])
    small_w = {"ada_b": (ada_b, g_ada_b, m_ada_b, v_ada_b),
               "norm_w": (norm_w, sums[9:11], m_norm_w, v_norm_w),
               "hgrn_gnorm_w": (hgrn_gnorm_w, sums[11:12], m_hgrn_gnorm_w, v_hgrn_gnorm_w),
               "final_norm_w": (row(final_norm_w), sums[12:13], row(m_final_norm_w), row(v_final_norm_w)),
               "pool_scale": (pool_scale, chip_cols(sums[13:14], ps_cols), m_pool_scale, v_pool_scale),
               "hgrn_lb_logits": (hgrn_lb_logits, chip_cols(sums[14:16], lb_cols), m_hgrn_lb_logits, v_hgrn_lb_logits)}
    quads = [tuple(a.reshape(-1, q[0].shape[-1]) for a in q) for q in small_w.values()]
    for (k, q), quad, (d, mn, vn) in zip(small_w.items(), quads, _adamw_small(quads, "adamw_small")):
        out[k] = tuple(a.reshape(q[0].shape) for a in (quad[1], d, mn, vn))

    per_dev = small_all.reshape(N_DEV, SMALL_ROWS, D_MODEL)
    pad7 = jnp.zeros((7, 3 * D_MODEL), F32)
    dm0 = jnp.concatenate([per_dev[:, 0:3].reshape(N_DEV, -1), sums[6:9].reshape(1, -1), pad7], axis=0)
    dm1 = jnp.concatenate([per_dev[:, 3:6].reshape(N_DEV, -1), jnp.zeros((8, 3 * D_MODEL), F32)], axis=0)
    d_mod = chip_cols(jnp.stack([dm0, dm1]), ada_cols)
    out["ada_w"] = _ada_update(cond_rows.T, d_mod, ada_w, m_ada_w, v_ada_w)

    out["c_ctx"] = tuple(a.reshape(-1) for a in (g_c, d_c, m_c, v_c))
    out["final_norm_w"] = tuple(a.reshape(-1) for a in out["final_norm_w"])

    done = [g_c, out["ada_w"][1]] + [out[k][1] for k in GATHER_LATE]
    key = GATHER_EARLY[0]
    _, early = _scatter_wait(*loc["pending"], done, key)
    big_grads = reduce_scattered({key: early}, GATHER_EARLY, "exchange_halves_early")
    early = [tuple(a.reshape(-1, weights[k].shape[-1]) for a in (weights[k], big_grads[k], *moments[k])) for k in GATHER_EARLY]
    for k, quad, (d, mn, vn) in zip(GATHER_EARLY, early, _adamw_small(early, "adamw_early", 56)):
        out[k] = tuple(a.reshape(weights[k].shape) for a in (quad[1], d, mn, vn))

    names = ["c_ctx", "ada_w", "ada_b", "norm_w", "hgrn_w_in", "hgrn_lb_logits", "hgrn_gnorm_w", "hgrn_w_out", "pool_w_in",
             "pool_w_grp", "pool_scale", "pool_w_out", "final_norm_w"]
    return (loss, loc["d_x"][None], *[out[k][0] for k in names], *[out[k][1] for k in names], *[out[k][2] for k in names],
            *[out[k][3] for k in names])
```
